```python
import jax, jax.numpy as jnp
from jax import lax
import numpy as np

D_MODEL = 1024
BATCH = 32
SEQ = 2048
DEPTH = 4

N_MIXERS = 2
SSD_EXPAND = 2
D_INNER = SSD_EXPAND * D_MODEL
HEAD_DIM = 64
N_SSD_HEADS = D_INNER // HEAD_DIM
N_SSD_GROUPS = 4
HEADS_PER_GROUP = N_SSD_HEADS // N_SSD_GROUPS
D_STATE = 128
SSD_CONV = 4
CHUNK = 128
D_XBC = D_INNER + 2 * N_SSD_GROUPS * D_STATE
D_IN_PROJ = D_INNER + D_XBC + N_SSD_HEADS
DT_MIN = 1e-3
DT_MAX = 1e-1
A_MIN = 1.0
A_MAX = 16.0
POOL_WINDOWS = (2, 4, 8, 16)
N_POOL_GROUPS = len(POOL_WINDOWS)
POOL_GROUP_DIM = D_MODEL // N_POOL_GROUPS
D_FF = 2816
FFN_CONV = 3
EPS = 1e-6

kernel_name = 'hybrid_ssd_pool_convffn_trunk'


def rms_norm(x, w):
    xf = x.astype(jnp.float32)
    xf = xf * lax.rsqrt(jnp.mean(xf * xf, axis=-1, keepdims=True) + EPS)
    return (xf * w.astype(jnp.float32)).astype(x.dtype)


def causal_depthwise_conv(x, w, b):
    k = w.shape[0]
    y = lax.conv_general_dilated(x, w[:, None, :].astype(x.dtype), window_strides=(1,),
                                 padding=[(k - 1, 0)],
                                 dimension_numbers=('NWC', 'WIO', 'NWC'),
                                 feature_group_count=x.shape[-1])
    return y + b.astype(x.dtype)


def segsum_decay(a_cum):
    q = a_cum.shape[-1]
    mask = jnp.tril(jnp.ones((q, q), dtype=bool))
    diff = a_cum[..., :, None] - a_cum[..., None, :]
    return jnp.exp(jnp.where(mask, diff, -jnp.inf))


def ssd_chunked(xh, dt, a, bm, cm):
    b, l, h, p = xh.shape
    g, n = bm.shape[2], bm.shape[3]
    r = h // g
    nc = l // CHUNK
    xf = xh.astype(jnp.float32).reshape(b, nc, CHUNK, g, r, p)
    dtc = dt.reshape(b, nc, CHUNK, g, r)
    bc = bm.astype(jnp.float32).reshape(b, nc, CHUNK, g, n)
    cc = cm.astype(jnp.float32).reshape(b, nc, CHUNK, g, n)
    a_dt = jnp.moveaxis(dtc * a.reshape(g, r), 2, -1)
    a_cum = jnp.cumsum(a_dt, axis=-1)
    xdt = xf * dtc[..., None]
    decay = segsum_decay(a_cum)
    cb = jnp.einsum('bclgn,bcsgn->bcgls', cc, bc)
    y_diag = jnp.einsum('bcgls,bcgrls,bcsgrp->bclgrp', cb, decay, xdt)
    decay_to_end = jnp.exp(a_cum[..., -1:] - a_cum)
    states = jnp.einsum('bcsgn,bcgrs,bcsgrp->bcgrpn', bc, decay_to_end, xdt)
    chunk_decay = jnp.exp(a_cum[..., -1])

    def step(carry, inp):
        st, dec = inp
        return carry * dec[..., None, None] + st, carry

    h0 = jnp.zeros((b, g, r, p, n), jnp.float32)
    _, prev = lax.scan(step, h0, (jnp.moveaxis(states, 1, 0), jnp.moveaxis(chunk_decay, 1, 0)))
    prev = jnp.moveaxis(prev, 0, 1)
    y_off = jnp.einsum('bclgn,bcgrpn,bcgrl->bclgrp', cc, prev, jnp.exp(a_cum))
    return (y_diag + y_off).reshape(b, l, h, p)


def ssd_mixer(u, w_in, conv_w, conv_b, dt_bias, a_log, d_skip, norm_w, w_out):
    b, l, _ = u.shape
    zxbcdt = u @ w_in.astype(u.dtype)
    z, xbc, dt = jnp.split(zxbcdt, [D_INNER, D_INNER + D_XBC], axis=-1)
    xbc = jax.nn.silu(causal_depthwise_conv(xbc, conv_w, conv_b))
    xs, bm, cm = jnp.split(xbc, [D_INNER, D_INNER + N_SSD_GROUPS * D_STATE], axis=-1)
    xh = xs.reshape(b, l, N_SSD_HEADS, HEAD_DIM)
    bm = bm.reshape(b, l, N_SSD_GROUPS, D_STATE)
    cm = cm.reshape(b, l, N_SSD_GROUPS, D_STATE)
    dt = jax.nn.softplus(dt.astype(jnp.float32) + dt_bias.astype(jnp.float32))
    a = -jnp.exp(a_log.astype(jnp.float32))
    y = ssd_chunked(xh, dt, a, bm, cm)
    y = y + d_skip.astype(jnp.float32)[:, None] * xh.astype(jnp.float32)
    y = y.reshape(b, l, D_INNER) * jax.nn.silu(z.astype(jnp.float32))
    yg = y.reshape(b, l, N_SSD_GROUPS, D_INNER // N_SSD_GROUPS)
    yg = yg * lax.rsqrt(jnp.mean(yg * yg, axis=-1, keepdims=True) + EPS)
    y = (yg.reshape(b, l, D_INNER) * norm_w.astype(jnp.float32)).astype(u.dtype)
    return y @ w_out.astype(u.dtype)


def pool_mixer(u, w_pool, scale):
    b, l, d = u.shape
    uf = u.astype(jnp.float32)
    cs = jnp.pad(jnp.cumsum(uf, axis=1), ((0, 0), (1, 0), (0, 0)))
    pos = jnp.arange(1, l + 1, dtype=jnp.float32)[None, :, None]
    groups = []
    for k, w in enumerate(POOL_WINDOWS):
        cg = cs[..., k * POOL_GROUP_DIM:(k + 1) * POOL_GROUP_DIM]
        lagged = jnp.pad(cg, ((0, 0), (w - 1, 0), (0, 0)))[:, :l]
        mean = (cg[:, 1:] - lagged) / jnp.minimum(pos, float(w))
        groups.append(mean - uf[..., k * POOL_GROUP_DIM:(k + 1) * POOL_GROUP_DIM])
    mixed = jnp.stack(groups, axis=2)
    out = jnp.einsum('blgc,gcd->blgd', mixed, w_pool.astype(jnp.float32)).reshape(b, l, d)
    return (out * scale.astype(jnp.float32)).astype(u.dtype)


def conv_ffn(u, w_up, conv_w, conv_b, w_down):
    h = u @ w_up.astype(u.dtype)
    h = causal_depthwise_conv(h, conv_w, conv_b)
    gate, val = jnp.split(h, 2, axis=-1)
    return (jax.nn.silu(gate) * val) @ w_down.astype(u.dtype)


def _fwd_setup_inputs(seed: int = 0) -> dict:
    key = jax.random.key(seed)
    ks = jax.random.split(key, 24)
    n_ssd = (DEPTH + N_MIXERS - 1) // N_MIXERS
    n_pool = DEPTH // N_MIXERS
    nrm = jax.random.normal
    x = nrm(ks[0], (BATCH, SEQ, D_MODEL), jnp.float32)
    ssd_w_in = nrm(ks[1], (n_ssd, D_MODEL, D_IN_PROJ), jnp.float32) * D_MODEL ** -0.5
    ssd_conv_w = nrm(ks[2], (n_ssd, SSD_CONV, D_XBC), jnp.float32) * SSD_CONV ** -0.5
    ssd_conv_b = nrm(ks[3], (n_ssd, D_XBC), jnp.float32) * 0.01
    u = jax.random.uniform(ks[4], (n_ssd, N_SSD_HEADS), jnp.float32)
    dt0 = jnp.exp(u * (np.log(DT_MAX) - np.log(DT_MIN)) + np.log(DT_MIN))
    ssd_dt_bias = dt0 + jnp.log(-jnp.expm1(-dt0))
    ssd_a_log = jnp.log(jax.random.uniform(ks[5], (n_ssd, N_SSD_HEADS), jnp.float32, A_MIN, A_MAX))
    ssd_d = 1.0 + 0.1 * nrm(ks[6], (n_ssd, N_SSD_HEADS), jnp.float32)
    ssd_norm_w = 1.0 + 0.1 * nrm(ks[7], (n_ssd, D_INNER), jnp.float32)
    ssd_w_out = nrm(ks[8], (n_ssd, D_INNER, D_MODEL), jnp.float32) * D_INNER ** -0.5
    pool_w = nrm(ks[9], (n_pool, N_POOL_GROUPS, POOL_GROUP_DIM, POOL_GROUP_DIM), jnp.float32) * POOL_GROUP_DIM ** -0.5
    pool_scale = 1.0 + 0.1 * nrm(ks[10], (n_pool, D_MODEL), jnp.float32)
    ffn_w_up = nrm(ks[11], (DEPTH, D_MODEL, 2 * D_FF), jnp.float32) * D_MODEL ** -0.5
    ffn_conv_w = nrm(ks[12], (DEPTH, FFN_CONV, 2 * D_FF), jnp.float32) * FFN_CONV ** -0.5
    ffn_conv_b = nrm(ks[13], (DEPTH, 2 * D_FF), jnp.float32) * 0.01
    ffn_w_down = nrm(ks[14], (DEPTH, D_FF, D_MODEL), jnp.float32) * D_FF ** -0.5
    norm_mix_pre = 1.0 + 0.1 * nrm(ks[15], (DEPTH, D_MODEL), jnp.float32)
    norm_mix_post = 1.0 + 0.1 * nrm(ks[16], (DEPTH, D_MODEL), jnp.float32)
    norm_ffn_pre = 1.0 + 0.1 * nrm(ks[17], (DEPTH, D_MODEL), jnp.float32)
    norm_ffn_post = 1.0 + 0.1 * nrm(ks[18], (DEPTH, D_MODEL), jnp.float32)
    return {'x': x, 'ssd_w_in': ssd_w_in, 'ssd_conv_w': ssd_conv_w, 'ssd_conv_b': ssd_conv_b,
            'ssd_dt_bias': ssd_dt_bias, 'ssd_a_log': ssd_a_log, 'ssd_d': ssd_d,
            'ssd_norm_w': ssd_norm_w, 'ssd_w_out': ssd_w_out, 'pool_w': pool_w,
            'pool_scale': pool_scale, 'ffn_w_up': ffn_w_up, 'ffn_conv_w': ffn_conv_w,
            'ffn_conv_b': ffn_conv_b, 'ffn_w_down': ffn_w_down, 'norm_mix_pre': norm_mix_pre,
            'norm_mix_post': norm_mix_post, 'norm_ffn_pre': norm_ffn_pre,
            'norm_ffn_post': norm_ffn_post}


def _fwd_reference(x, ssd_w_in, ssd_conv_w, ssd_conv_b, ssd_dt_bias, ssd_a_log, ssd_d,
              ssd_norm_w, ssd_w_out, pool_w, pool_scale, ffn_w_up, ffn_conv_w, ffn_conv_b,
              ffn_w_down, norm_mix_pre, norm_mix_post, norm_ffn_pre, norm_ffn_post):
    for i in range(DEPTH):
        j = i // N_MIXERS
        h = rms_norm(x, norm_mix_pre[i])
        if i % N_MIXERS == 0:
            mix = ssd_mixer(h, ssd_w_in[j], ssd_conv_w[j], ssd_conv_b[j], ssd_dt_bias[j],
                            ssd_a_log[j], ssd_d[j], ssd_norm_w[j], ssd_w_out[j])
        else:
            mix = pool_mixer(h, pool_w[j], pool_scale[j])
        x = x + rms_norm(mix, norm_mix_post[i])
        f = conv_ffn(rms_norm(x, norm_ffn_pre[i]), ffn_w_up[i], ffn_conv_w[i], ffn_conv_b[i],
                     ffn_w_down[i])
        x = x + rms_norm(f, norm_ffn_post[i])
    return x


import jax as _jax
import jax.numpy as _jnp

TWIN_FORMAT = 'train_step'
FWD_PARAMS = ['x', 'ssd_w_in', 'ssd_conv_w', 'ssd_conv_b', 'ssd_dt_bias', 'ssd_a_log', 'ssd_d', 'ssd_norm_w', 'ssd_w_out', 'pool_w', 'pool_scale', 'ffn_w_up', 'ffn_conv_w', 'ffn_conv_b', 'ffn_w_down', 'norm_mix_pre', 'norm_mix_post', 'norm_ffn_pre', 'norm_ffn_post']
TWIN_WEIGHTS = ['ssd_w_in', 'ssd_conv_w', 'ssd_conv_b', 'ssd_dt_bias', 'ssd_a_log', 'ssd_d', 'ssd_norm_w', 'ssd_w_out', 'pool_w', 'pool_scale', 'ffn_w_up', 'ffn_conv_w', 'ffn_conv_b', 'ffn_w_down', 'norm_mix_pre', 'norm_mix_post', 'norm_ffn_pre', 'norm_ffn_post']
TWIN_DIFF_INPUT = 'x'
TWIN_INPUTS = ['x', 'ssd_w_in', 'ssd_conv_w', 'ssd_conv_b', 'ssd_dt_bias', 'ssd_a_log', 'ssd_d', 'ssd_norm_w', 'ssd_w_out', 'pool_w', 'pool_scale', 'ffn_w_up', 'ffn_conv_w', 'ffn_conv_b', 'ffn_w_down', 'norm_mix_pre', 'norm_mix_post', 'norm_ffn_pre', 'norm_ffn_post', 'loss_target', 'm_ssd_w_in', 'm_ssd_conv_w', 'm_ssd_conv_b', 'm_ssd_dt_bias', 'm_ssd_a_log', 'm_ssd_d', 'm_ssd_norm_w', 'm_ssd_w_out', 'm_pool_w', 'm_pool_scale', 'm_ffn_w_up', 'm_ffn_conv_w', 'm_ffn_conv_b', 'm_ffn_w_down', 'm_norm_mix_pre', 'm_norm_mix_post', 'm_norm_ffn_pre', 'm_norm_ffn_post', 'v_ssd_w_in', 'v_ssd_conv_w', 'v_ssd_conv_b', 'v_ssd_dt_bias', 'v_ssd_a_log', 'v_ssd_d', 'v_ssd_norm_w', 'v_ssd_w_out', 'v_pool_w', 'v_pool_scale', 'v_ffn_w_up', 'v_ffn_conv_w', 'v_ffn_conv_b', 'v_ffn_w_down', 'v_norm_mix_pre', 'v_norm_mix_post', 'v_norm_ffn_pre', 'v_norm_ffn_post']
TWIN_OUTPUTS = ['loss', 'grad_x', 'grad_ssd_w_in', 'grad_ssd_conv_w', 'grad_ssd_conv_b', 'grad_ssd_dt_bias', 'grad_ssd_a_log', 'grad_ssd_d', 'grad_ssd_norm_w', 'grad_ssd_w_out', 'grad_pool_w', 'grad_pool_scale', 'grad_ffn_w_up', 'grad_ffn_conv_w', 'grad_ffn_conv_b', 'grad_ffn_w_down', 'grad_norm_mix_pre', 'grad_norm_mix_post', 'grad_norm_ffn_pre', 'grad_norm_ffn_post', 'delta_ssd_w_in', 'delta_ssd_conv_w', 'delta_ssd_conv_b', 'delta_ssd_dt_bias', 'delta_ssd_a_log', 'delta_ssd_d', 'delta_ssd_norm_w', 'delta_ssd_w_out', 'delta_pool_w', 'delta_pool_scale', 'delta_ffn_w_up', 'delta_ffn_conv_w', 'delta_ffn_conv_b', 'delta_ffn_w_down', 'delta_norm_mix_pre', 'delta_norm_mix_post', 'delta_norm_ffn_pre', 'delta_norm_ffn_post', 'new_m_ssd_w_in', 'new_m_ssd_conv_w', 'new_m_ssd_conv_b', 'new_m_ssd_dt_bias', 'new_m_ssd_a_log', 'new_m_ssd_d', 'new_m_ssd_norm_w', 'new_m_ssd_w_out', 'new_m_pool_w', 'new_m_pool_scale', 'new_m_ffn_w_up', 'new_m_ffn_conv_w', 'new_m_ffn_conv_b', 'new_m_ffn_w_down', 'new_m_norm_mix_pre', 'new_m_norm_mix_post', 'new_m_norm_ffn_pre', 'new_m_norm_ffn_post', 'new_v_ssd_w_in', 'new_v_ssd_conv_w', 'new_v_ssd_conv_b', 'new_v_ssd_dt_bias', 'new_v_ssd_a_log', 'new_v_ssd_d', 'new_v_ssd_norm_w', 'new_v_ssd_w_out', 'new_v_pool_w', 'new_v_pool_scale', 'new_v_ffn_w_up', 'new_v_ffn_conv_w', 'new_v_ffn_conv_b', 'new_v_ffn_w_down', 'new_v_norm_mix_pre', 'new_v_norm_mix_post', 'new_v_norm_ffn_pre', 'new_v_norm_ffn_post']
TWIN_LEAF_KINDS = {'loss': 'loss', 'grad_x': 'grad_x', 'grad_ssd_w_in': 'grad_w', 'grad_ssd_conv_w': 'grad_w', 'grad_ssd_conv_b': 'grad_w', 'grad_ssd_dt_bias': 'grad_w', 'grad_ssd_a_log': 'grad_w', 'grad_ssd_d': 'grad_w', 'grad_ssd_norm_w': 'grad_w', 'grad_ssd_w_out': 'grad_w', 'grad_pool_w': 'grad_w', 'grad_pool_scale': 'grad_w', 'grad_ffn_w_up': 'grad_w', 'grad_ffn_conv_w': 'grad_w', 'grad_ffn_conv_b': 'grad_w', 'grad_ffn_w_down': 'grad_w', 'grad_norm_mix_pre': 'grad_w', 'grad_norm_mix_post': 'grad_w', 'grad_norm_ffn_pre': 'grad_w', 'grad_norm_ffn_post': 'grad_w', 'delta_ssd_w_in': 'delta_w', 'delta_ssd_conv_w': 'delta_w', 'delta_ssd_conv_b': 'delta_w', 'delta_ssd_dt_bias': 'delta_w', 'delta_ssd_a_log': 'delta_w', 'delta_ssd_d': 'delta_w', 'delta_ssd_norm_w': 'delta_w', 'delta_ssd_w_out': 'delta_w', 'delta_pool_w': 'delta_w', 'delta_pool_scale': 'delta_w', 'delta_ffn_w_up': 'delta_w', 'delta_ffn_conv_w': 'delta_w', 'delta_ffn_conv_b': 'delta_w', 'delta_ffn_w_down': 'delta_w', 'delta_norm_mix_pre': 'delta_w', 'delta_norm_mix_post': 'delta_w', 'delta_norm_ffn_pre': 'delta_w', 'delta_norm_ffn_post': 'delta_w', 'new_m_ssd_w_in': 'new_m', 'new_m_ssd_conv_w': 'new_m', 'new_m_ssd_conv_b': 'new_m', 'new_m_ssd_dt_bias': 'new_m', 'new_m_ssd_a_log': 'new_m', 'new_m_ssd_d': 'new_m', 'new_m_ssd_norm_w': 'new_m', 'new_m_ssd_w_out': 'new_m', 'new_m_pool_w': 'new_m', 'new_m_pool_scale': 'new_m', 'new_m_ffn_w_up': 'new_m', 'new_m_ffn_conv_w': 'new_m', 'new_m_ffn_conv_b': 'new_m', 'new_m_ffn_w_down': 'new_m', 'new_m_norm_mix_pre': 'new_m', 'new_m_norm_mix_post': 'new_m', 'new_m_norm_ffn_pre': 'new_m', 'new_m_norm_ffn_post': 'new_m', 'new_v_ssd_w_in': 'new_v', 'new_v_ssd_conv_w': 'new_v', 'new_v_ssd_conv_b': 'new_v', 'new_v_ssd_dt_bias': 'new_v', 'new_v_ssd_a_log': 'new_v', 'new_v_ssd_d': 'new_v', 'new_v_ssd_norm_w': 'new_v', 'new_v_ssd_w_out': 'new_v', 'new_v_pool_w': 'new_v', 'new_v_pool_scale': 'new_v', 'new_v_ffn_w_up': 'new_v', 'new_v_ffn_conv_w': 'new_v', 'new_v_ffn_conv_b': 'new_v', 'new_v_ffn_w_down': 'new_v', 'new_v_norm_mix_pre': 'new_v', 'new_v_norm_mix_post': 'new_v', 'new_v_norm_ffn_pre': 'new_v', 'new_v_norm_ffn_post': 'new_v'}


def _forward(args):
    return _fwd_reference(*[args[k] for k in FWD_PARAMS])


def _output_shape():
    out = _jax.eval_shape(lambda: _forward(_fwd_setup_inputs(0)))
    return out.shape, out.dtype

N_MICROBATCH = 1
ADAM_LR = 0.001
ADAM_B1 = 0.9
ADAM_B2 = 0.999
ADAM_EPS = 1e-08
ADAM_WD = 0.01
ADAM_STEP = 10
PER_EXAMPLE_BATCH_AXIS = {'x': 0, 'loss_target': 0}
SHARED_INPUTS = []
_WEIGHT_DTYPES = {'ssd_w_in': _jnp.float32, 'ssd_conv_w': _jnp.float32, 'ssd_conv_b': _jnp.float32, 'ssd_dt_bias': _jnp.float32, 'ssd_a_log': _jnp.float32, 'ssd_d': _jnp.float32, 'ssd_norm_w': _jnp.float32, 'ssd_w_out': _jnp.float32, 'pool_w': _jnp.float32, 'pool_scale': _jnp.float32, 'ffn_w_up': _jnp.float32, 'ffn_conv_w': _jnp.float32, 'ffn_conv_b': _jnp.float32, 'ffn_w_down': _jnp.float32, 'norm_mix_pre': _jnp.float32, 'norm_mix_post': _jnp.float32, 'norm_ffn_pre': _jnp.float32, 'norm_ffn_post': _jnp.float32}
MOMENT_SCALE = {'ssd_w_in': 1.369491e+00, 'ssd_conv_w': 1.438774e+00, 'ssd_conv_b': 3.469513e+00, 'ssd_dt_bias': 4.781426e+00, 'ssd_a_log': 8.162431e+00, 'ssd_d': 1.653320e+01, 'ssd_norm_w': 2.191728e+00, 'ssd_w_out': 2.867687e+00, 'pool_w': 6.557904e+00, 'pool_scale': 1.363101e+01, 'ffn_w_up': 8.447096e-01, 'ffn_conv_w': 9.140119e-01, 'ffn_conv_b': 2.348103e+00, 'ffn_w_down': 1.627178e+00, 'norm_mix_pre': 3.995890e+00, 'norm_mix_post': 6.515438e+01, 'norm_ffn_pre': 1.957309e+00, 'norm_ffn_post': 6.413520e+01}


def _to_microbatches(a, axis):
    t = _jnp.moveaxis(a, axis, 0)
    t = t.reshape((N_MICROBATCH, t.shape[0] // N_MICROBATCH) + t.shape[1:])
    return _jnp.moveaxis(t, 1, axis + 1)


def setup_inputs(seed: int = 0) -> dict:
    inp = _fwd_setup_inputs(seed)
    key = _jax.random.fold_in(_jax.random.key(seed), 7919)
    shape, _ = _output_shape()
    out = dict(inp)
    out["loss_target"] = _jax.random.normal(_jax.random.fold_in(key, 0), shape, _jnp.float32)
    for i, name in enumerate(TWIN_WEIGHTS):
        w = inp[name].astype(_jnp.float32)
        if MOMENT_SCALE is None:
            s = _jnp.sqrt(_jnp.mean(_jnp.square(w)) + 1e-30)
        else:
            s = MOMENT_SCALE[name]
        km, kv = _jax.random.split(_jax.random.fold_in(key, i + 1))
        out[name] = w
        out["m_" + name] = s * _jax.random.normal(km, w.shape, _jnp.float32)
        out["v_" + name] = (s * s) * _jax.random.uniform(kv, w.shape, _jnp.float32, 0.5, 1.5)
    if N_MICROBATCH > 1:
        for name, axis in PER_EXAMPLE_BATCH_AXIS.items():
            out[name] = _to_microbatches(out[name], axis)
    return {'x': out['x'], 'ssd_w_in': out['ssd_w_in'], 'ssd_conv_w': out['ssd_conv_w'], 'ssd_conv_b': out['ssd_conv_b'], 'ssd_dt_bias': out['ssd_dt_bias'], 'ssd_a_log': out['ssd_a_log'], 'ssd_d': out['ssd_d'], 'ssd_norm_w': out['ssd_norm_w'], 'ssd_w_out': out['ssd_w_out'], 'pool_w': out['pool_w'], 'pool_scale': out['pool_scale'], 'ffn_w_up': out['ffn_w_up'], 'ffn_conv_w': out['ffn_conv_w'], 'ffn_conv_b': out['ffn_conv_b'], 'ffn_w_down': out['ffn_w_down'], 'norm_mix_pre': out['norm_mix_pre'], 'norm_mix_post': out['norm_mix_post'], 'norm_ffn_pre': out['norm_ffn_pre'], 'norm_ffn_post': out['norm_ffn_post'], 'loss_target': out['loss_target'], 'm_ssd_w_in': out['m_ssd_w_in'], 'm_ssd_conv_w': out['m_ssd_conv_w'], 'm_ssd_conv_b': out['m_ssd_conv_b'], 'm_ssd_dt_bias': out['m_ssd_dt_bias'], 'm_ssd_a_log': out['m_ssd_a_log'], 'm_ssd_d': out['m_ssd_d'], 'm_ssd_norm_w': out['m_ssd_norm_w'], 'm_ssd_w_out': out['m_ssd_w_out'], 'm_pool_w': out['m_pool_w'], 'm_pool_scale': out['m_pool_scale'], 'm_ffn_w_up': out['m_ffn_w_up'], 'm_ffn_conv_w': out['m_ffn_conv_w'], 'm_ffn_conv_b': out['m_ffn_conv_b'], 'm_ffn_w_down': out['m_ffn_w_down'], 'm_norm_mix_pre': out['m_norm_mix_pre'], 'm_norm_mix_post': out['m_norm_mix_post'], 'm_norm_ffn_pre': out['m_norm_ffn_pre'], 'm_norm_ffn_post': out['m_norm_ffn_post'], 'v_ssd_w_in': out['v_ssd_w_in'], 'v_ssd_conv_w': out['v_ssd_conv_w'], 'v_ssd_conv_b': out['v_ssd_conv_b'], 'v_ssd_dt_bias': out['v_ssd_dt_bias'], 'v_ssd_a_log': out['v_ssd_a_log'], 'v_ssd_d': out['v_ssd_d'], 'v_ssd_norm_w': out['v_ssd_norm_w'], 'v_ssd_w_out': out['v_ssd_w_out'], 'v_pool_w': out['v_pool_w'], 'v_pool_scale': out['v_pool_scale'], 'v_ffn_w_up': out['v_ffn_w_up'], 'v_ffn_conv_w': out['v_ffn_conv_w'], 'v_ffn_conv_b': out['v_ffn_conv_b'], 'v_ffn_w_down': out['v_ffn_w_down'], 'v_norm_mix_pre': out['v_norm_mix_pre'], 'v_norm_mix_post': out['v_norm_mix_post'], 'v_norm_ffn_pre': out['v_norm_ffn_pre'], 'v_norm_ffn_post': out['v_norm_ffn_post']}


def _loss(weights, diff, rest, loss_target):
    with _jax.named_scope("forward"):
        args = {**rest, TWIN_DIFF_INPUT: diff, **{k: w.astype(_WEIGHT_DTYPES[k]) for k, w in weights.items()}}
        y = _forward(args)
    with _jax.named_scope("loss_head"):
        err = _jnp.square(y.astype(_jnp.float32) - loss_target)
        return 0.5 * _jnp.sum(_jnp.mean(err, axis=-1)) if err.ndim else 0.5 * err


def _adamw(w, g, m, v):
    m = ADAM_B1 * m + (1.0 - ADAM_B1) * g
    v = ADAM_B2 * v + (1.0 - ADAM_B2) * _jnp.square(g)
    m_hat = m / (1.0 - ADAM_B1 ** ADAM_STEP)
    v_hat = v / (1.0 - ADAM_B2 ** ADAM_STEP)
    delta = -ADAM_LR * (m_hat / (_jnp.sqrt(v_hat) + ADAM_EPS) + ADAM_WD * w)
    return delta, m, v


def reference(x, ssd_w_in, ssd_conv_w, ssd_conv_b, ssd_dt_bias, ssd_a_log, ssd_d, ssd_norm_w, ssd_w_out, pool_w, pool_scale, ffn_w_up, ffn_conv_w, ffn_conv_b, ffn_w_down, norm_mix_pre, norm_mix_post, norm_ffn_pre, norm_ffn_post, loss_target, m_ssd_w_in, m_ssd_conv_w, m_ssd_conv_b, m_ssd_dt_bias, m_ssd_a_log, m_ssd_d, m_ssd_norm_w, m_ssd_w_out, m_pool_w, m_pool_scale, m_ffn_w_up, m_ffn_conv_w, m_ffn_conv_b, m_ffn_w_down, m_norm_mix_pre, m_norm_mix_post, m_norm_ffn_pre, m_norm_ffn_post, v_ssd_w_in, v_ssd_conv_w, v_ssd_conv_b, v_ssd_dt_bias, v_ssd_a_log, v_ssd_d, v_ssd_norm_w, v_ssd_w_out, v_pool_w, v_pool_scale, v_ffn_w_up, v_ffn_conv_w, v_ffn_conv_b, v_ffn_w_down, v_norm_mix_pre, v_norm_mix_post, v_norm_ffn_pre, v_norm_ffn_post):
    given = dict(x=x, ssd_w_in=ssd_w_in, ssd_conv_w=ssd_conv_w, ssd_conv_b=ssd_conv_b, ssd_dt_bias=ssd_dt_bias, ssd_a_log=ssd_a_log, ssd_d=ssd_d, ssd_norm_w=ssd_norm_w, ssd_w_out=ssd_w_out, pool_w=pool_w, pool_scale=pool_scale, ffn_w_up=ffn_w_up, ffn_conv_w=ffn_conv_w, ffn_conv_b=ffn_conv_b, ffn_w_down=ffn_w_down, norm_mix_pre=norm_mix_pre, norm_mix_post=norm_mix_post, norm_ffn_pre=norm_ffn_pre, norm_ffn_post=norm_ffn_post, loss_target=loss_target, m_ssd_w_in=m_ssd_w_in, m_ssd_conv_w=m_ssd_conv_w, m_ssd_conv_b=m_ssd_conv_b, m_ssd_dt_bias=m_ssd_dt_bias, m_ssd_a_log=m_ssd_a_log, m_ssd_d=m_ssd_d, m_ssd_norm_w=m_ssd_norm_w, m_ssd_w_out=m_ssd_w_out, m_pool_w=m_pool_w, m_pool_scale=m_pool_scale, m_ffn_w_up=m_ffn_w_up, m_ffn_conv_w=m_ffn_conv_w, m_ffn_conv_b=m_ffn_conv_b, m_ffn_w_down=m_ffn_w_down, m_norm_mix_pre=m_norm_mix_pre, m_norm_mix_post=m_norm_mix_post, m_norm_ffn_pre=m_norm_ffn_pre, m_norm_ffn_post=m_norm_ffn_post, v_ssd_w_in=v_ssd_w_in, v_ssd_conv_w=v_ssd_conv_w, v_ssd_conv_b=v_ssd_conv_b, v_ssd_dt_bias=v_ssd_dt_bias, v_ssd_a_log=v_ssd_a_log, v_ssd_d=v_ssd_d, v_ssd_norm_w=v_ssd_norm_w, v_ssd_w_out=v_ssd_w_out, v_pool_w=v_pool_w, v_pool_scale=v_pool_scale, v_ffn_w_up=v_ffn_w_up, v_ffn_conv_w=v_ffn_conv_w, v_ffn_conv_b=v_ffn_conv_b, v_ffn_w_down=v_ffn_w_down, v_norm_mix_pre=v_norm_mix_pre, v_norm_mix_post=v_norm_mix_post, v_norm_ffn_pre=v_norm_ffn_pre, v_norm_ffn_post=v_norm_ffn_post)
    weights = {n: given[n] for n in TWIN_WEIGHTS}
    shared = {n: given[n] for n in SHARED_INPUTS}
    per_example = {n: given[n] for n in ['x']}
    grad_fn = _jax.value_and_grad(_loss, argnums=(0, 1))

    def one_microbatch(ex, loss_target):
        ex = dict(ex)
        diff = ex.pop(TWIN_DIFF_INPUT)
        return grad_fn(weights, diff, {**shared, **ex}, loss_target)

    if N_MICROBATCH == 1:
        loss, (grad_w, grad_x) = one_microbatch(per_example, given["loss_target"])
    else:
        def body(carry, xs):
            loss_sum, grad_sum = carry
            l_k, (gw_k, gx_k) = one_microbatch(xs[0], xs[1])
            with _jax.named_scope("update"):
                return (loss_sum + l_k, _jax.tree.map(_jnp.add, grad_sum, gw_k)), gx_k

        init = (_jnp.zeros((), _jnp.float32), _jax.tree.map(_jnp.zeros_like, weights))
        (loss, grad_w), grad_x = _jax.lax.scan(body, init, (per_example, given["loss_target"]))
    with _jax.named_scope("update"):
        delta_w, new_m, new_v = {}, {}, {}
        for n in TWIN_WEIGHTS:
            delta_w[n], new_m[n], new_v[n] = _adamw(weights[n], grad_w[n], given["m_" + n], given["v_" + n])
    return (loss, grad_x, *[grad_w[n] for n in TWIN_WEIGHTS], *[delta_w[n] for n in TWIN_WEIGHTS],
            *[new_m[n] for n in TWIN_WEIGHTS], *[new_v[n] for n in TWIN_WEIGHTS])
```

```python
import functools
import math

import jax
import jax.numpy as jnp
from jax import lax
from jax.experimental import pallas as pl
from jax.experimental.pallas import tpu as pltpu

F32 = jnp.float32
BF16 = jnp.bfloat16
MESH = pl.DeviceIdType.MESH
ANY = pl.BlockSpec(memory_space=pl.ANY)

HEAD_DIM = 64
D_STATE = 128
CHUNK = 128
N_GROUPS = 4
SSD_CONV = 4
FFN_CONV = 3
EPS = 1e-6
N_CHIPS = 4
LANES = 128
FLAT_COLS = 1024

ADAM_LR = 0.001
ADAM_B1 = 0.9
ADAM_B2 = 0.999
ADAM_EPS = 1e-08
ADAM_WD = 0.01
ADAM_STEP = 10

VMEM_LIMIT_BYTES = 56 * 1024 * 1024


def _params(sem=None):
    kw = dict(vmem_limit_bytes=VMEM_LIMIT_BYTES)
    if sem is not None:
        kw["dimension_semantics"] = sem
    return pltpu.CompilerParams(**kw)


def _sigmoid(x):
    return 1.0 / (1.0 + jnp.exp(-x))


def _softplus(x):
    return jnp.maximum(x, 0.0) + jnp.log(1.0 + jnp.exp(-jnp.abs(x)))


def _dot(a, b, dn):
    return lax.dot_general(a, b, (dn, ((), ())), preferred_element_type=F32)


def _nn(a, b):
    return _dot(a, b, ((1,), (0,)))


def _nt(a, b):
    return _dot(a, b, ((1,), (1,)))


def _tn(a, b):
    return _dot(a, b, ((0,), (0,)))


def _split(x, parts):
    out = []
    r = x
    for _ in range(parts):
        p = r.astype(BF16)
        out.append(p)
        r = r - p.astype(F32)
    return out


def _sel_left(sel, x, parts=3):
    n = x.shape[1]
    r = _nn(sel, jnp.concatenate(_split(x, parts), axis=1))
    out = r[:, 0:n]
    for i in range(1, parts):
        out = out + r[:, i * n:(i + 1) * n]
    return out


def _sel_right(x, sel_stacked, parts=3):
    return _nn(jnp.concatenate(_split(x, parts), axis=1), sel_stacked)


def _mm(a, b, dims, out_dtype, name, tm, tn, tk):
    if dims == "nn":
        (m, k), (k2, n) = a.shape, b.shape
    elif dims == "nt":
        (m, k), (n, k2) = a.shape, b.shape
    else:
        (k, m), (k2, n) = a.shape, b.shape
    assert k == k2, (a.shape, b.shape, dims)
    tm, tn, tk = min(tm, m), min(tn, n), min(tk, k)
    assert m % tm == 0 and n % tn == 0 and k % tk == 0, (name, m, n, k, tm, tn, tk)
    nk = k // tk
    dn = {"nn": ((1,), (0,)), "nt": ((1,), (1,)), "tn": ((0,), (0,))}[dims]

    def body(a_ref, b_ref, o_ref, *scratch):
        p = _dot(a_ref[...].astype(BF16), b_ref[...].astype(BF16), dn)
        if nk == 1:
            o_ref[...] = p.astype(out_dtype)
            return
        acc = scratch[0]
        kk = pl.program_id(2)

        @pl.when(kk == 0)
        def _():
            acc[...] = p

        @pl.when(kk > 0)
        def _():
            acc[...] += p

        @pl.when(kk == nk - 1)
        def _():
            o_ref[...] = acc[...].astype(out_dtype)

    if dims == "nn":
        a_spec = pl.BlockSpec((tm, tk), lambda i, j, kk: (i, kk))
        b_spec = pl.BlockSpec((tk, tn), lambda i, j, kk: (kk, j))
    elif dims == "nt":
        a_spec = pl.BlockSpec((tm, tk), lambda i, j, kk: (i, kk))
        b_spec = pl.BlockSpec((tn, tk), lambda i, j, kk: (j, kk))
    else:
        a_spec = pl.BlockSpec((tk, tm), lambda i, j, kk: (kk, i))
        b_spec = pl.BlockSpec((tk, tn), lambda i, j, kk: (kk, j))
    return pl.pallas_call(
        body,
        out_shape=jax.ShapeDtypeStruct((m, n), out_dtype),
        grid=(m // tm, n // tn, nk),
        in_specs=[a_spec, b_spec],
        out_specs=pl.BlockSpec((tm, tn), lambda i, j, kk: (i, j)),
        scratch_shapes=[] if nk == 1 else [pltpu.VMEM((tm, tn), F32)],
        compiler_params=_params(("parallel", "parallel", "arbitrary")),
        name=name,
    )(a, b)


def _row_tile(t, want):
    tm = min(want, t)
    assert t % tm == 0
    return tm


def _norm_fwd(x, w, out_dtype, name, resid=None):
    t, d = x.shape
    tm = _row_tile(t, 512)

    def body(*refs):
        if resid is None:
            x_ref, w_ref, o_ref = refs
        else:
            x_ref, w_ref, r_ref, o_ref = refs
        xv = x_ref[...]
        r = lax.rsqrt(jnp.mean(xv * xv, axis=-1, keepdims=True) + EPS)
        y = (xv * r) * w_ref[...]
        if resid is not None:
            y = r_ref[...] + y
        o_ref[...] = y.astype(out_dtype)

    row = pl.BlockSpec((tm, d), lambda i: (i, 0))
    vec = pl.BlockSpec((1, d), lambda i: (0, 0))
    args = [x, w] + ([] if resid is None else [resid])
    return pl.pallas_call(
        body, out_shape=jax.ShapeDtypeStruct((t, d), out_dtype), grid=(t // tm,),
        in_specs=[row, vec] + ([] if resid is None else [row]), out_specs=row,
        compiler_params=_params(("parallel",)), name=name)(*args)


def _norm_bwd(src, w, dy, out_dtype, name, resid=None):
    t, d = src.shape
    tm = _row_tile(t, 512)

    def body(*refs):
        if resid is None:
            x_ref, w_ref, g_ref, o_ref, dw_ref = refs
        else:
            x_ref, w_ref, g_ref, r_ref, o_ref, dw_ref = refs
        xv = x_ref[...]
        g = g_ref[...].astype(F32)
        r = lax.rsqrt(jnp.mean(xv * xv, axis=-1, keepdims=True) + EPS)
        xh = xv * r
        gh = g * w_ref[...]
        mean = jnp.mean(gh * xh, axis=-1, keepdims=True)
        dx = r * (gh - xh * mean)
        if resid is not None:
            dx = r_ref[...] + dx
        o_ref[...] = dx.astype(out_dtype)
        part = jnp.sum(g * xh, axis=0, keepdims=True)

        @pl.when(pl.program_id(0) == 0)
        def _():
            dw_ref[...] = part

        @pl.when(pl.program_id(0) > 0)
        def _():
            dw_ref[...] += part

    row = pl.BlockSpec((tm, d), lambda i: (i, 0))
    vec = pl.BlockSpec((1, d), lambda i: (0, 0))
    args = [src, w, dy] + ([] if resid is None else [resid])
    return pl.pallas_call(
        body,
        out_shape=(jax.ShapeDtypeStruct((t, d), out_dtype), jax.ShapeDtypeStruct((1, d), F32)),
        grid=(t // tm,),
        in_specs=[row, vec, row] + ([] if resid is None else [row]),
        out_specs=(row, vec),
        compiler_params=_params(("arbitrary",)), name=name)(*args)


def _loss_head(y, target, name):
    t, d = y.shape
    tm = _row_tile(t, 512)

    def body(y_ref, t_ref, dy_ref, l_ref):
        e = y_ref[...] - t_ref[...]
        dy_ref[...] = e * (1.0 / d)
        col = jnp.sum(e * e, axis=0, keepdims=True)
        s = jnp.sum(col, axis=1, keepdims=True) * (0.5 / d)
        part = jnp.broadcast_to(s, (1, LANES))

        @pl.when(pl.program_id(0) == 0)
        def _():
            l_ref[...] = part

        @pl.when(pl.program_id(0) > 0)
        def _():
            l_ref[...] += part

    row = pl.BlockSpec((tm, d), lambda i: (i, 0))
    return pl.pallas_call(
        body,
        out_shape=(jax.ShapeDtypeStruct((t, d), F32), jax.ShapeDtypeStruct((1, LANES), F32)),
        grid=(t // tm,), in_specs=[row, row],
        out_specs=(row, pl.BlockSpec((1, LANES), lambda i: (0, 0))),
        compiler_params=_params(("arbitrary",)), name=name)(y, target)


def _window(ref, c, rows, seq, before, after):
    r0 = pl.multiple_of(c * rows, rows)
    parts = []
    if before:
        h0 = pl.multiple_of(jnp.maximum(r0 - before, 0), 8)
        halo = ref[pl.ds(h0, before), :].astype(F32)
        parts.append(jnp.where(c > 0, halo, 0.0))
    parts.append(ref[pl.ds(r0, rows), :].astype(F32))
    if after:
        h1 = pl.multiple_of(jnp.minimum(r0 + rows, seq - after), 8)
        halo = ref[pl.ds(h1, after), :].astype(F32)
        parts.append(jnp.where(c < seq // rows - 1, halo, 0.0))
    return parts[0] if len(parts) == 1 else jnp.concatenate(parts, axis=0)


def _lag(x, k):
    return pltpu.roll(x, k, 0) if k else x


def _lead(x, k):
    return pltpu.roll(x, x.shape[0] - k, 0) if k else x


SHIFT_ROWS = 128
SHIFT_COLS = 256


def _ffn_act_fwd(hpre, cw, cb, name):
    b, seq, f2 = hpre.shape
    cbk = SHIFT_COLS
    nj = f2 // (2 * cbk)
    rows = min(SHIFT_ROWS, seq)

    def body(h_ref, w_ref, b_ref, o_ref):
        w = w_ref[...]
        bias = b_ref[...]

        def chunk(c, carry):
            ext = _window(h_ref, c, rows, seq, 8, 0)
            acc = bias + w[2:3, :] * ext[8:, :]
            acc = acc + w[1:2, :] * _lag(ext, 1)[8:, :]
            acc = acc + w[0:1, :] * _lag(ext, 2)[8:, :]
            gate, val = acc[:, :cbk], acc[:, cbk:]
            a = gate * _sigmoid(gate) * val
            o_ref[pl.ds(pl.multiple_of(c * rows, rows), rows), :] = a.astype(BF16)
            return carry

        lax.fori_loop(0, seq // rows, chunk, 0)

    return pl.pallas_call(
        body, out_shape=jax.ShapeDtypeStruct((b, seq, f2 // 2), BF16), grid=(b, nj),
        in_specs=[pl.BlockSpec((None, seq, 2 * cbk), lambda i, j: (i, 0, j)),
                  pl.BlockSpec((FFN_CONV, 2 * cbk), lambda i, j: (0, j)),
                  pl.BlockSpec((1, 2 * cbk), lambda i, j: (0, j))],
        out_specs=pl.BlockSpec((None, seq, cbk), lambda i, j: (i, 0, j)),
        compiler_params=_params(("parallel", "parallel")), name=name)(hpre, cw, cb)


def _ffn_act_bwd(hpre, da, cw, cb, name):
    b, seq, f2 = hpre.shape
    cbk = SHIFT_COLS
    nj = f2 // (2 * cbk)
    rows = min(SHIFT_ROWS, seq)

    def body(h_ref, da_ref, w_ref, b_ref, o_ref, dw_ref, db_ref):
        w = w_ref[...]
        bias = b_ref[...]

        def chunk(c, carry):
            dw0, dw1, dw2, dbias = carry
            ext = _window(h_ref, c, rows, seq, 8, 8)
            x0 = ext[8:, :]
            x1 = _lag(ext, 1)[8:, :]
            x2 = _lag(ext, 2)[8:, :]
            pre = bias + w[2:3, :] * x0 + w[1:2, :] * x1 + w[0:1, :] * x2
            gate, val = pre[:, :cbk], pre[:, cbk:]
            dav = _window(da_ref, c, rows, seq, 0, 8)
            sg = _sigmoid(gate)
            dgate = dav * val * (sg * (1.0 + gate * (1.0 - sg)))
            dval = dav * (gate * sg)
            dpre = jnp.concatenate([dgate, dval], axis=1)
            dx = w[2:3, :] * dpre + w[1:2, :] * _lead(dpre, 1) + w[0:1, :] * _lead(dpre, 2)
            o_ref[pl.ds(pl.multiple_of(c * rows, rows), rows), :] = dx[:rows, :].astype(BF16)
            dp = dpre[:rows, :]
            dw2 = dw2 + jnp.sum(dp * x0[:rows, :], axis=0, keepdims=True)
            dw1 = dw1 + jnp.sum(dp * x1[:rows, :], axis=0, keepdims=True)
            dw0 = dw0 + jnp.sum(dp * x2[:rows, :], axis=0, keepdims=True)
            dbias = dbias + jnp.sum(dp, axis=0, keepdims=True)
            return dw0, dw1, dw2, dbias

        z = jnp.zeros((1, 2 * cbk), F32)
        dw0, dw1, dw2, dbias = lax.fori_loop(0, seq // rows, chunk, (z, z, z, z))
        dwv = jnp.concatenate([dw0, dw1, dw2], axis=0)

        @pl.when(pl.program_id(1) == 0)
        def _():
            dw_ref[...] = dwv
            db_ref[...] = dbias

        @pl.when(pl.program_id(1) > 0)
        def _():
            dw_ref[...] += dwv
            db_ref[...] += dbias

    return pl.pallas_call(
        body,
        out_shape=(jax.ShapeDtypeStruct((b, seq, f2), BF16), jax.ShapeDtypeStruct((FFN_CONV, f2), F32),
                   jax.ShapeDtypeStruct((1, f2), F32)),
        grid=(nj, b),
        in_specs=[pl.BlockSpec((None, seq, 2 * cbk), lambda j, i: (i, 0, j)),
                  pl.BlockSpec((None, seq, cbk), lambda j, i: (i, 0, j)),
                  pl.BlockSpec((FFN_CONV, 2 * cbk), lambda j, i: (0, j)),
                  pl.BlockSpec((1, 2 * cbk), lambda j, i: (0, j))],
        out_specs=(pl.BlockSpec((None, seq, 2 * cbk), lambda j, i: (i, 0, j)),
                   pl.BlockSpec((FFN_CONV, 2 * cbk), lambda j, i: (0, j)),
                   pl.BlockSpec((1, 2 * cbk), lambda j, i: (0, j))),
        compiler_params=_params(("parallel", "arbitrary")), name=name)(hpre, da, cw, cb)


def _ssd_conv_fwd(zx, cw, cb, d_inner, name):
    b, seq, _ = zx.shape
    xbc = cw.shape[1]
    cbk = SHIFT_COLS
    off = d_inner // cbk
    rows = min(SHIFT_ROWS, seq)

    def body(h_ref, w_ref, b_ref, o_ref):
        w = w_ref[...]
        bias = b_ref[...]

        def chunk(c, carry):
            ext = _window(h_ref, c, rows, seq, 8, 0)
            acc = bias + w[3:4, :] * ext[8:, :]
            for k in range(1, SSD_CONV):
                acc = acc + w[3 - k:4 - k, :] * _lag(ext, k)[8:, :]
            o_ref[pl.ds(pl.multiple_of(c * rows, rows), rows), :] = acc * _sigmoid(acc)
            return carry

        lax.fori_loop(0, seq // rows, chunk, 0)

    return pl.pallas_call(
        body, out_shape=jax.ShapeDtypeStruct((b, seq, xbc), F32), grid=(b, xbc // cbk),
        in_specs=[pl.BlockSpec((None, seq, cbk), lambda i, j: (i, 0, j + off)),
                  pl.BlockSpec((SSD_CONV, cbk), lambda i, j: (0, j)),
                  pl.BlockSpec((1, cbk), lambda i, j: (0, j))],
        out_specs=pl.BlockSpec((None, seq, cbk), lambda i, j: (i, 0, j)),
        compiler_params=_params(("parallel", "parallel")), name=name)(zx, cw, cb)


def _ssd_conv_bwd(zx, dxc, cw, cb, d_inner, name):
    b, seq, _ = zx.shape
    xbc = cw.shape[1]
    cbk = SHIFT_COLS
    off = d_inner // cbk
    rows = min(SHIFT_ROWS, seq)

    def body(h_ref, g_ref, w_ref, b_ref, o_ref, dw_ref, db_ref):
        w = w_ref[...]
        bias = b_ref[...]

        def chunk(c, carry):
            dws, dbias = carry
            ext = _window(h_ref, c, rows, seq, 8, 8)
            xs = [_lag(ext, k)[8:, :] for k in range(SSD_CONV)]
            pre = bias + w[3:4, :] * xs[0]
            for k in range(1, SSD_CONV):
                pre = pre + w[3 - k:4 - k, :] * xs[k]
            s = _sigmoid(pre)
            dpre = _window(g_ref, c, rows, seq, 0, 8) * (s * (1.0 + pre * (1.0 - s)))
            dx = w[3:4, :] * dpre
            for k in range(1, SSD_CONV):
                dx = dx + w[3 - k:4 - k, :] * _lead(dpre, k)
            o_ref[pl.ds(pl.multiple_of(c * rows, rows), rows), :] = dx[:rows, :].astype(BF16)
            dp = dpre[:rows, :]
            dws = tuple(dws[k] + jnp.sum(dp * xs[k][:rows, :], axis=0, keepdims=True) for k in range(SSD_CONV))
            dbias = dbias + jnp.sum(dp, axis=0, keepdims=True)
            return dws, dbias

        z = jnp.zeros((1, cbk), F32)
        dws, dbias = lax.fori_loop(0, seq // rows, chunk, ((z,) * SSD_CONV, z))
        dwv = jnp.concatenate([dws[3 - i] for i in range(SSD_CONV)], axis=0)

        @pl.when(pl.program_id(1) == 0)
        def _():
            dw_ref[...] = dwv
            db_ref[...] = dbias

        @pl.when(pl.program_id(1) > 0)
        def _():
            dw_ref[...] += dwv
            db_ref[...] += dbias

    return pl.pallas_call(
        body,
        out_shape=(jax.ShapeDtypeStruct((b, seq, xbc), BF16), jax.ShapeDtypeStruct((SSD_CONV, xbc), F32),
                   jax.ShapeDtypeStruct((1, xbc), F32)),
        grid=(xbc // cbk, b),
        in_specs=[pl.BlockSpec((None, seq, cbk), lambda j, i: (i, 0, j + off)),
                  pl.BlockSpec((None, seq, cbk), lambda j, i: (i, 0, j)),
                  pl.BlockSpec((SSD_CONV, cbk), lambda j, i: (0, j)),
                  pl.BlockSpec((1, cbk), lambda j, i: (0, j))],
        out_specs=(pl.BlockSpec((None, seq, cbk), lambda j, i: (i, 0, j)),
                   pl.BlockSpec((SSD_CONV, cbk), lambda j, i: (0, j)),
                   pl.BlockSpec((1, cbk), lambda j, i: (0, j))),
        compiler_params=_params(("parallel", "arbitrary")), name=name)(zx, dxc, cw, cb)


def _pool_sums(q, g, lead):
    sh = _lead if lead else _lag
    s2 = q + sh(q, 1)
    s4 = s2 + sh(s2, 2)
    s8 = s4 + sh(s4, 4)
    s16 = s8 + sh(s8, 8)
    return jnp.where(g == 0, s2, jnp.where(g == 1, s4, jnp.where(g == 2, s8, s16)))


def _pool_count(r0, n, g, shape):
    t = (r0 + lax.broadcasted_iota(jnp.int32, shape, 0) + 1).astype(F32)
    return jnp.minimum(t, (2 << g).astype(F32))


def _pool_fwd(h, pw, scale, name):
    b, seq, d = h.shape
    dg = d // 4
    rows = min(SHIFT_ROWS, seq)

    def body(h_ref, w_ref, s_ref, o_ref):
        g = pl.program_id(1)
        wmat = w_ref[...]
        sc = s_ref[...]

        def chunk(c, carry):
            r0 = c * rows
            ext = _window(h_ref, c, rows, seq, 16, 0)
            sums = _pool_sums(ext, g, False)[16:, :]
            mixed = sums / _pool_count(r0, rows, g, (rows, dg)) - ext[16:, :]
            o_ref[pl.ds(pl.multiple_of(r0, rows), rows), :] = _nn(mixed.astype(BF16), wmat) * sc
            return carry

        lax.fori_loop(0, seq // rows, chunk, 0)

    return pl.pallas_call(
        body, out_shape=jax.ShapeDtypeStruct((b, seq, d), F32), grid=(b, 4),
        in_specs=[pl.BlockSpec((None, seq, dg), lambda i, g: (i, 0, g)),
                  pl.BlockSpec((None, dg, dg), lambda i, g: (g, 0, 0)),
                  pl.BlockSpec((1, dg), lambda i, g: (0, g))],
        out_specs=pl.BlockSpec((None, seq, dg), lambda i, g: (i, 0, g)),
        compiler_params=_params(("parallel", "parallel")), name=name)(h, pw, scale)


def _pool_bwd(h, dout, pw, scale, name):
    b, seq, d = h.shape
    dg = d // 4
    rows = min(SHIFT_ROWS, seq)

    def body(h_ref, g_ref, w_ref, s_ref, o_ref, dw_ref, ds_ref, dw_acc):
        g = pl.program_id(0)
        wmat = w_ref[...]
        sc = s_ref[...]
        dw_acc[...] = jnp.zeros_like(dw_acc)

        def chunk(c, dsc):
            r0 = c * rows
            ext = _window(h_ref, c, rows, seq, 16, 0)
            sums = _pool_sums(ext, g, False)[16:, :]
            mixed = (sums / _pool_count(r0, rows, g, (rows, dg)) - ext[16:, :]).astype(BF16)
            gext = _window(g_ref, c, rows, seq, 0, 16)
            dsc = dsc + jnp.sum(gext[:rows, :] * _nn(mixed, wmat), axis=0, keepdims=True)
            dpre = (gext * sc).astype(BF16)
            dw_acc[...] += _tn(mixed, dpre[:rows, :])
            dmix = _nt(dpre, wmat)
            q = dmix / _pool_count(r0, rows + 16, g, (rows + 16, dg))
            back = _pool_sums(q, g, True)
            o_ref[pl.ds(pl.multiple_of(r0, rows), rows), :] = back[:rows, :] - dmix[:rows, :]
            return dsc

        dsc = lax.fori_loop(0, seq // rows, chunk, jnp.zeros((1, dg), F32))

        @pl.when(pl.program_id(1) == 0)
        def _():
            dw_ref[...] = dw_acc[...]
            ds_ref[...] = dsc

        @pl.when(pl.program_id(1) > 0)
        def _():
            dw_ref[...] += dw_acc[...]
            ds_ref[...] += dsc

    return pl.pallas_call(
        body,
        out_shape=(jax.ShapeDtypeStruct((b, seq, d), F32), jax.ShapeDtypeStruct((4, dg, dg), F32),
                   jax.ShapeDtypeStruct((1, d), F32)),
        grid=(4, b),
        in_specs=[pl.BlockSpec((None, seq, dg), lambda g, i: (i, 0, g)),
                  pl.BlockSpec((None, seq, dg), lambda g, i: (i, 0, g)),
                  pl.BlockSpec((None, dg, dg), lambda g, i: (g, 0, 0)),
                  pl.BlockSpec((1, dg), lambda g, i: (0, g))],
        out_specs=(pl.BlockSpec((None, seq, dg), lambda g, i: (i, 0, g)),
                   pl.BlockSpec((None, dg, dg), lambda g, i: (g, 0, 0)),
                   pl.BlockSpec((1, dg), lambda g, i: (0, g))),
        scratch_shapes=[pltpu.VMEM((dg, dg), F32)],
        compiler_params=_params(("parallel", "arbitrary")), name=name)(h, dout, pw, scale)


def _head_of(channel):
    return jnp.right_shift(channel, HEAD_DIM.bit_length() - 1)


def _ssd_consts(gw):
    q = CHUNK
    row = lax.broadcasted_iota(jnp.int32, (q, q), 0)
    col = lax.broadcasted_iota(jnp.int32, (q, q), 1)
    tril = (row >= col).astype(BF16)
    triu = (row <= col).astype(BF16)
    e = (_head_of(lax.broadcasted_iota(jnp.int32, (LANES, gw), 1))
         == lax.broadcasted_iota(jnp.int32, (LANES, gw), 0)).astype(BF16)
    et = (_head_of(lax.broadcasted_iota(jnp.int32, (gw, LANES), 0))
          == lax.broadcasted_iota(jnp.int32, (gw, LANES), 1)).astype(BF16)
    return row, col, tril, triu, e, et


def _ssd_common(dtr, dtb, alog, gw):
    q = CHUNK
    row, col, tril, triu, e, et = _ssd_consts(gw)
    dt = _softplus(dtr + dtb)
    a_row = -jnp.exp(alog)
    acum = _sel_left(tril, dt * a_row)
    ac_last = jnp.sum(jnp.where(row == q - 1, acum, 0.0), axis=0, keepdims=True)
    eac = jnp.exp(acum)
    de = jnp.exp(ac_last - acum)
    e3 = jnp.concatenate([e, e, e], axis=0)
    expand = _sel_right(jnp.concatenate([dt, eac, de], axis=0), e3)
    dt_x, eac_x, de_x = expand[0:q], expand[q:2 * q], expand[2 * q:3 * q]
    acum_t = acum.T
    cd_col = jnp.exp(acum_t[:, q - 1:q])
    et3 = jnp.concatenate([et, et, et], axis=1)
    cdmat = _nn(et3, jnp.concatenate(_split(jnp.broadcast_to(cd_col, (LANES, D_STATE)), 3), axis=0))
    consts = dict(row=row, col=col, tril=tril, triu=triu, e=e, et=et)
    return dt, a_row, acum, acum_t, ac_last, eac, de, dt_x, eac_x, de_x, cdmat, consts


def _decay(acum, acum_t, j, row, col):
    diff = acum[:, j:j + 1] - acum_t[j:j + 1, :]
    return jnp.exp(jnp.where(row >= col, diff, -1e30))


def _ssd_fwd(xc, zx, dtb, alog, dskip, nw, d_inner, name):
    b, seq, xbc = xc.shape
    q = CHUNK
    nc = seq // q
    gw = d_inner // N_GROUPS
    nh = gw // HEAD_DIM
    xb0 = d_inner // D_STATE
    xc0 = xb0 + N_GROUPS
    dt0 = (d_inner + xbc) // LANES

    def body(x_ref, b_ref, c_ref, z_ref, dtr_ref, dtb_ref, al_ref, dsk_ref, nw_ref, y_ref, yn_ref, st_ref, s_ref):
        @pl.when(pl.program_id(2) == 0)
        def _():
            s_ref[...] = jnp.zeros_like(s_ref)

        prev = s_ref[...]
        st_ref[...] = prev
        x = x_ref[...]
        bm = b_ref[...].astype(BF16)
        cm = c_ref[...].astype(BF16)
        (dt, a_row, acum, acum_t, ac_last, eac, de, dt_x, eac_x, de_x, cdmat, k) = _ssd_common(
            dtr_ref[...], dtb_ref[0:1, :], al_ref[0:1, :], gw)
        xdt = x * dt_x
        xdt_b = xdt.astype(BF16)
        cb = _nt(cm, bm)
        head = _head_of(lax.broadcasted_iota(jnp.int32, (q, gw), 1))
        y = dsk_ref[0:1, :] * x
        for j in range(nh):
            m = (cb * _decay(acum, acum_t, j, k["row"], k["col"])).astype(BF16)
            y = y + jnp.where(head == j, _nn(m, xdt_b), 0.0)
        prev_b = prev.astype(BF16)
        y = y + eac_x * _nt(cm, prev_b)
        s_ref[...] = cdmat * prev + _tn((xdt * de_x).astype(BF16), bm)
        y_ref[...] = y
        z = z_ref[...]
        yg = y * (z * _sigmoid(z))
        r = lax.rsqrt(jnp.mean(yg * yg, axis=-1, keepdims=True) + EPS)
        yn_ref[...] = ((yg * r) * nw_ref[0:1, :]).astype(BF16)

    par = lambda w: pl.BlockSpec((None, 8, w), lambda i, g, c: (g, 0, 0))
    return pl.pallas_call(
        body,
        out_shape=(jax.ShapeDtypeStruct((b, seq, d_inner), F32), jax.ShapeDtypeStruct((b, seq, d_inner), BF16),
                   jax.ShapeDtypeStruct((b, nc, N_GROUPS, gw, D_STATE), F32)),
        grid=(b, N_GROUPS, nc),
        in_specs=[pl.BlockSpec((None, q, gw), lambda i, g, c: (i, c, g)),
                  pl.BlockSpec((None, q, D_STATE), lambda i, g, c: (i, c, xb0 + g)),
                  pl.BlockSpec((None, q, D_STATE), lambda i, g, c: (i, c, xc0 + g)),
                  pl.BlockSpec((None, q, gw), lambda i, g, c: (i, c, g)),
                  pl.BlockSpec((None, q, LANES), lambda i, g, c: (i, c, dt0 + g)),
                  par(LANES), par(LANES), par(gw), par(gw)],
        out_specs=(pl.BlockSpec((None, q, gw), lambda i, g, c: (i, c, g)),
                   pl.BlockSpec((None, q, gw), lambda i, g, c: (i, c, g)),
                   pl.BlockSpec((None, None, None, gw, D_STATE), lambda i, g, c: (i, c, g, 0, 0))),
        scratch_shapes=[pltpu.VMEM((gw, D_STATE), F32)],
        compiler_params=_params(("parallel", "parallel", "arbitrary")), name=name,
    )(xc, xc, xc, zx, zx, dtb, alog, dskip, nw)


def _ssd_bwd(xc, zx, y, dyn, st, dtb, alog, dskip, nw, d_inner, name):
    b, seq, xbc = xc.shape
    q = CHUNK
    nc = seq // q
    gw = d_inner // N_GROUPS
    nh = gw // HEAD_DIM
    xb0 = d_inner // D_STATE
    xc0 = xb0 + N_GROUPS
    dt0 = (d_inner + xbc) // LANES

    def body(x_ref, b_ref, c_ref, z_ref, dtr_ref, y_ref, g_ref, st_ref, dtb_ref, al_ref, dsk_ref, nw_ref,
             dz_ref, dx_ref, db_ref, dc_ref, ddt_ref, dnw_ref, dd_ref, dal_ref, dbias_ref,
             ds_ref, colbuf, rowbuf):
        first = jnp.logical_and(pl.program_id(1) == 0, pl.program_id(2) == 0)

        @pl.when(pl.program_id(2) == 0)
        def _():
            ds_ref[...] = jnp.zeros_like(ds_ref)

        x = x_ref[...]
        bm = b_ref[...].astype(BF16)
        cm = c_ref[...].astype(BF16)
        z = z_ref[...]
        y = y_ref[...]
        prev = st_ref[...]
        dtr = dtr_ref[...] + dtb_ref[0:1, :]
        (dt, a_row, acum, acum_t, ac_last, eac, de, dt_x, eac_x, de_x, cdmat, k) = _ssd_common(
            dtr_ref[...], dtb_ref[0:1, :], al_ref[0:1, :], gw)
        row, col = k["row"], k["col"]
        et2 = jnp.concatenate([k["et"], k["et"]], axis=0)
        et3 = jnp.concatenate([k["et"], k["et"], k["et"]], axis=0)
        head = _head_of(lax.broadcasted_iota(jnp.int32, (q, gw), 1))

        sz = _sigmoid(z)
        silu_z = z * sz
        yg = y * silu_z
        r = lax.rsqrt(jnp.mean(yg * yg, axis=-1, keepdims=True) + EPS)
        xh = yg * r
        dyn = g_ref[...]
        gh = dyn * nw_ref[0:1, :]
        dyg = r * (gh - xh * jnp.mean(gh * xh, axis=-1, keepdims=True))
        dnw = jnp.sum(dyn * xh, axis=0, keepdims=True)
        g = dyg * silu_z
        dz_ref[...] = (dyg * y * (sz * (1.0 + z * (1.0 - sz)))).astype(BF16)
        dd = _sel_right(jnp.broadcast_to(jnp.sum(g * x, axis=0, keepdims=True), (8, gw)), et3)

        xdt = x * dt_x
        xdt_b = xdt.astype(BF16)
        g_b = g.astype(BF16)
        prev_b = prev.astype(BF16)
        cb = _nt(cm, bm)

        cp = _nt(cm, prev_b)
        ge = g * eac_x
        dac = _sel_right(ge * cp, et3)
        ge_b = ge.astype(BF16)
        dcm = _nn(ge_b, prev_b)
        dprev = _tn(ge_b, cm)

        colbuf[...] = jnp.zeros_like(colbuf)
        rowbuf[...] = jnp.zeros_like(rowbuf)
        dcb = jnp.zeros((q, q), F32)
        dxdt = jnp.zeros((q, gw), F32)
        for j in range(nh):
            dec = _decay(acum, acum_t, j, row, col)
            m = cb * dec
            dm = _nt(jnp.where(head == j, g, 0.0).astype(BF16), xdt_b)
            w = dm * m
            colbuf[:, j:j + 1] = jnp.sum(w, axis=1, keepdims=True)
            rowbuf[j:j + 1, :] = jnp.sum(w, axis=0, keepdims=True)
            dcb = dcb + dm * dec
            dxdt = dxdt + jnp.where(head == j, _tn(m.astype(BF16), g_b), 0.0)
        dcb_b = dcb.astype(BF16)
        dcm = dcm + _nn(dcb_b, bm)
        dbm = _tn(dcb_b, cm)

        ds = ds_ref[...]
        ds_b = ds.astype(BF16)
        u = _nt(bm, ds_b)
        dxdt = dxdt + u * de_x
        dde = _sel_right(u * xdt, et3)
        dbm = dbm + _nn((xdt * de_x).astype(BF16), ds_b)
        pm = jnp.concatenate(_split(ds * prev, 2), axis=1)
        t2 = _tn(pm, k["et"])
        dcd_row = jnp.sum(t2[0:D_STATE] + t2[D_STATE:2 * D_STATE], axis=0, keepdims=True)
        last = dcd_row * jnp.exp(ac_last) + jnp.sum(dde * de, axis=0, keepdims=True)
        dac = dac + colbuf[...] - rowbuf[...].T - dde * de + jnp.where(row == q - 1, last, 0.0)
        ds_ref[...] = cdmat * ds + dprev

        dadt = _sel_left(k["triu"], dac)
        ddt = _sel_right(dxdt * x, et3) + dadt * a_row
        dal = jnp.sum(dadt * dt, axis=0, keepdims=True) * a_row
        lane = lax.broadcasted_iota(jnp.int32, (q, LANES), 1)
        ddtr = jnp.where(lane < nh, ddt * _sigmoid(dtr), 0.0)
        ddt_ref[...] = ddtr.astype(BF16)
        dbias = jnp.sum(ddtr, axis=0, keepdims=True)
        dx_ref[...] = dxdt * dt_x + dsk_ref[0:1, :] * g
        db_ref[...] = dbm
        dc_ref[...] = dcm

        @pl.when(first)
        def _():
            dnw_ref[...] = jnp.broadcast_to(dnw, (8, gw))
            dd_ref[...] = dd
            dal_ref[...] = jnp.broadcast_to(dal, (8, LANES))
            dbias_ref[...] = jnp.broadcast_to(dbias, (8, LANES))

        @pl.when(jnp.logical_not(first))
        def _():
            dnw_ref[...] += jnp.broadcast_to(dnw, (8, gw))
            dd_ref[...] += dd
            dal_ref[...] += jnp.broadcast_to(dal, (8, LANES))
            dbias_ref[...] += jnp.broadcast_to(dbias, (8, LANES))

    rc = lambda c: nc - 1 - c
    par = lambda w: pl.BlockSpec((None, 8, w), lambda g, i, c: (g, 0, 0))
    blk = lambda w: pl.BlockSpec((None, q, w), lambda g, i, c: (i, rc(c), g))
    return pl.pallas_call(
        body,
        out_shape=(jax.ShapeDtypeStruct((b, seq, d_inner), BF16),
                   jax.ShapeDtypeStruct((b, seq, d_inner), F32),
                   jax.ShapeDtypeStruct((b, seq, N_GROUPS * D_STATE), F32),
                   jax.ShapeDtypeStruct((b, seq, N_GROUPS * D_STATE), F32),
                   jax.ShapeDtypeStruct((b, seq, N_GROUPS * LANES), BF16),
                   jax.ShapeDtypeStruct((N_GROUPS, 8, gw), F32),
                   jax.ShapeDtypeStruct((N_GROUPS, 8, LANES), F32),
                   jax.ShapeDtypeStruct((N_GROUPS, 8, LANES), F32),
                   jax.ShapeDtypeStruct((N_GROUPS, 8, LANES), F32)),
        grid=(N_GROUPS, b, nc),
        in_specs=[blk(gw),
                  pl.BlockSpec((None, q, D_STATE), lambda g, i, c: (i, rc(c), xb0 + g)),
                  pl.BlockSpec((None, q, D_STATE), lambda g, i, c: (i, rc(c), xc0 + g)),
                  blk(gw),
                  pl.BlockSpec((None, q, LANES), lambda g, i, c: (i, rc(c), dt0 + g)),
                  blk(gw), blk(gw),
                  pl.BlockSpec((None, None, None, gw, D_STATE), lambda g, i, c: (i, rc(c), g, 0, 0)),
                  par(LANES), par(LANES), par(gw), par(gw)],
        out_specs=(blk(gw), blk(gw), blk(D_STATE), blk(D_STATE), blk(LANES),
                   par(gw), par(LANES), par(LANES), par(LANES)),
        scratch_shapes=[pltpu.VMEM((gw, D_STATE), F32), pltpu.VMEM((q, LANES), F32), pltpu.VMEM((LANES, q), F32)],
        compiler_params=_params(("parallel", "arbitrary", "arbitrary")), name=name,
    )(xc, xc, xc, zx, zx, y, dyn, st, dtb, alog, dskip, nw)


def _adamw(w, g, m, v, name):
    rows, cols = w.shape
    tr = rows
    for cand in (512, 256, 128, 64, 32, 16, 8):
        if rows % cand == 0 and cand * cols * 4 <= 2 * 1024 * 1024:
            tr = cand
            break
    c1 = 1.0 - ADAM_B1 ** ADAM_STEP
    c2 = 1.0 - ADAM_B2 ** ADAM_STEP

    def body(w_ref, g_ref, m_ref, v_ref, d_ref, mo_ref, vo_ref):
        gv = g_ref[...]
        mn = ADAM_B1 * m_ref[...] + (1.0 - ADAM_B1) * gv
        vn = ADAM_B2 * v_ref[...] + (1.0 - ADAM_B2) * (gv * gv)
        mo_ref[...] = mn
        vo_ref[...] = vn
        d_ref[...] = -ADAM_LR * ((mn / c1) / (jnp.sqrt(vn / c2) + ADAM_EPS) + ADAM_WD * w_ref[...])

    spec = pl.BlockSpec((tr, cols), lambda i: (i, 0))
    shp = jax.ShapeDtypeStruct((rows, cols), F32)
    return pl.pallas_call(body, out_shape=(shp, shp, shp), grid=(rows // tr,), in_specs=[spec] * 4,
                          out_specs=(spec,) * 3, compiler_params=_params(("parallel",)), name=name)(w, g, m, v)


def _sum_slabs(parts, name):
    rows, cols = parts[0].shape
    tr = rows
    for cand in (512, 256, 128, 64, 32, 16, 8):
        if rows % cand == 0:
            tr = cand
            break

    def body(*refs):
        acc = refs[0][...]
        for rf in refs[1:-1]:
            acc = acc + rf[...]
        refs[-1][...] = acc

    spec = pl.BlockSpec((tr, cols), lambda i: (i, 0))
    return pl.pallas_call(body, out_shape=jax.ShapeDtypeStruct((rows, cols), F32), grid=(rows // tr,),
                          in_specs=[spec] * len(parts), out_specs=spec,
                          compiler_params=_params(("parallel",)), name=name)(*parts)


def _coords():
    return lax.axis_index("x"), lax.axis_index("y"), lax.axis_index("c")


def _other_chips(x, y):
    return [(1 - x, y), (x, 1 - y), (1 - x, 1 - y)]


def _allgather_halves(src, name):
    rows, cols = src.shape

    def body(x_ref, o_ref, send, recv, local):
        x, y, c = _coords()
        sib = (x, y, 1 - c)
        chips = _other_chips(x, y)

        def slot(h, cx, cy):
            return o_ref.at[h, 2 * cx + cy]

        def copy(kk, dst, to, src_ref):
            return pltpu.make_async_remote_copy(src_ref=src_ref, dst_ref=dst, send_sem=send.at[kk],
                                                recv_sem=recv.at[kk], device_id=to, device_id_type=MESH)

        mine = pltpu.make_async_copy(x_ref, slot(c, x, y), local)
        mine.start()
        first = [copy(0, slot(c, x, y), sib, x_ref)]
        first += [copy(1 + j, slot(c, x, y), (*chip, c), x_ref) for j, chip in enumerate(chips)]
        for cp in first:
            cp.start()
        passed = [copy(4 + j, slot(c, *chip), sib, slot(c, *chip)) for j, chip in enumerate(chips)]
        for j, chip in enumerate(chips):
            copy(1 + j, slot(c, *chip), (x, y, c), x_ref).wait_recv()
            passed[j].start()
        copy(0, slot(1 - c, x, y), (x, y, c), x_ref).wait_recv()
        for j, chip in enumerate(chips):
            copy(4 + j, slot(1 - c, *chip), (x, y, c), x_ref).wait_recv()
        for cp in first + passed:
            cp.wait_send()
        mine.wait()

    return pl.pallas_call(
        body, out_shape=jax.ShapeDtypeStruct((2, N_CHIPS, rows, cols), src.dtype),
        in_specs=[ANY], out_specs=ANY,
        scratch_shapes=[pltpu.SemaphoreType.DMA((7,)), pltpu.SemaphoreType.DMA((7,)), pltpu.SemaphoreType.DMA],
        name=name)(src)


def _swap_sibling(src, name):
    def body(x_ref, o_ref, send, recv):
        x, y, c = _coords()
        cp = pltpu.make_async_remote_copy(src_ref=x_ref, dst_ref=o_ref, send_sem=send, recv_sem=recv,
                                          device_id=(x, y, 1 - c), device_id_type=MESH)
        cp.start()
        cp.wait()

    return pl.pallas_call(body, out_shape=jax.ShapeDtypeStruct(src.shape, src.dtype), in_specs=[ANY], out_specs=ANY,
                          scratch_shapes=[pltpu.SemaphoreType.DMA, pltpu.SemaphoreType.DMA], name=name)(src)


def _scatter_chips(src, name):
    _, rows, cols = src.shape

    def body(x_ref, o_ref, send, recv, local):
        x, y, c = _coords()
        me = 2 * x + y
        mine = pltpu.make_async_copy(x_ref.at[me], o_ref.at[me], local)
        mine.start()
        cps = []
        for j, (cx, cy) in enumerate(_other_chips(x, y)):
            cp = pltpu.make_async_remote_copy(src_ref=x_ref.at[2 * cx + cy], dst_ref=o_ref.at[me], send_sem=send.at[j],
                                              recv_sem=recv.at[j], device_id=(cx, cy, c), device_id_type=MESH)
            cp.start()
            cps.append(cp)
        for j, (cx, cy) in enumerate(_other_chips(x, y)):
            pltpu.make_async_remote_copy(src_ref=x_ref.at[me], dst_ref=o_ref.at[2 * cx + cy], send_sem=send.at[j],
                                         recv_sem=recv.at[j], device_id=(x, y, c), device_id_type=MESH).wait_recv()
        for cp in cps:
            cp.wait_send()
        mine.wait()

    return pl.pallas_call(
        body, out_shape=jax.ShapeDtypeStruct(src.shape, src.dtype), in_specs=[ANY], out_specs=ANY,
        scratch_shapes=[pltpu.SemaphoreType.DMA((3,)), pltpu.SemaphoreType.DMA((3,)), pltpu.SemaphoreType.DMA],
        name=name)(src)


def _share_halves(src, name):
    rows, cols = src.shape

    def body(x_ref, o_ref, send, recv, local):
        x, y, c = _coords()
        mine = pltpu.make_async_copy(x_ref, o_ref.at[c], local)
        mine.start()
        cp = pltpu.make_async_remote_copy(src_ref=x_ref, dst_ref=o_ref.at[c], send_sem=send, recv_sem=recv,
                                          device_id=(x, y, 1 - c), device_id_type=MESH)
        cp.start()
        pltpu.make_async_remote_copy(src_ref=x_ref, dst_ref=o_ref.at[1 - c], send_sem=send, recv_sem=recv,
                                     device_id=(x, y, c), device_id_type=MESH).wait_recv()
        cp.wait_send()
        mine.wait()

    return pl.pallas_call(
        body, out_shape=jax.ShapeDtypeStruct((2, rows, cols), src.dtype), in_specs=[ANY], out_specs=ANY,
        scratch_shapes=[pltpu.SemaphoreType.DMA, pltpu.SemaphoreType.DMA, pltpu.SemaphoreType.DMA], name=name)(src)


def _allreduce_small(vec, name):
    rows, cols = vec.shape

    def body(x_ref, o_ref, buf, send, recv):
        x, y, c = _coords()
        me = 4 * x + 2 * y + c
        buf[me] = x_ref[...]
        cps = []
        for kk in range(1, 8):
            dx, dy, dc = (kk >> 2) & 1, (kk >> 1) & 1, kk & 1
            to = (1 - x if dx else x, 1 - y if dy else y, 1 - c if dc else c)
            cp = pltpu.make_async_remote_copy(src_ref=x_ref, dst_ref=buf.at[me], send_sem=send.at[kk - 1],
                                              recv_sem=recv.at[kk - 1], device_id=to, device_id_type=MESH)
            cp.start()
            cps.append((cp, 4 * to[0] + 2 * to[1] + to[2]))
        for kk, (cp, frm) in enumerate(cps):
            pltpu.make_async_remote_copy(src_ref=x_ref, dst_ref=buf.at[frm], send_sem=send.at[kk],
                                         recv_sem=recv.at[kk], device_id=(x, y, c), device_id_type=MESH).wait_recv()
        for cp, _ in cps:
            cp.wait_send()
        acc = buf[0]
        for kk in range(1, 8):
            acc = acc + buf[kk]
        o_ref[...] = acc

    vm = pl.BlockSpec(memory_space=pltpu.VMEM)
    return pl.pallas_call(
        body, out_shape=jax.ShapeDtypeStruct((rows, cols), F32), in_specs=[vm], out_specs=vm,
        scratch_shapes=[pltpu.VMEM((8, rows, cols), F32), pltpu.SemaphoreType.DMA((7,)), pltpu.SemaphoreType.DMA((7,))],
        compiler_params=_params(), name=name)(vec)


BIG = (("ssd_w_in", 2), ("ssd_w_out", 1), ("pool_w", 2), ("ffn_w_up", 2), ("ffn_w_down", 1))
SMALL = (("ssd_conv_w", 2), ("pool_scale", 1), ("ffn_conv_w", 2))
REPL = ("ssd_conv_b", "ssd_dt_bias", "ssd_a_log", "ssd_d", "ssd_norm_w", "ffn_conv_b",
        "norm_mix_pre", "norm_mix_post", "norm_ffn_pre", "norm_ffn_post")
WEIGHTS = ("ssd_w_in", "ssd_conv_w", "ssd_conv_b", "ssd_dt_bias", "ssd_a_log", "ssd_d", "ssd_norm_w", "ssd_w_out",
           "pool_w", "pool_scale", "ffn_w_up", "ffn_conv_w", "ffn_conv_b", "ffn_w_down", "norm_mix_pre",
           "norm_mix_post", "norm_ffn_pre", "norm_ffn_post")


def _flat_rows(n):
    unit = 2 * 16 * FLAT_COLS
    return 2 * 16 * ((n + unit - 1) // unit)


def _flatten_shards(arrs, dtype):
    flat = jnp.concatenate([a.astype(dtype).reshape(-1) for a in arrs])
    rows = _flat_rows(flat.shape[0])
    flat = jnp.pad(flat, (0, rows * FLAT_COLS - flat.shape[0]))
    return flat.reshape(2, rows // 2, FLAT_COLS)


def _unflatten_full(gathered, shard_shapes, axes):
    per_chip = jnp.swapaxes(gathered, 0, 1).reshape(N_CHIPS, -1)
    out, off = [], 0
    for shp, ax in zip(shard_shapes, axes):
        n = math.prod(shp)
        pieces = [per_chip[k, off:off + n].reshape(shp) for k in range(N_CHIPS)]
        out.append(jnp.concatenate(pieces, axis=ax))
        off += n
    return out


def _flatten_full_grads(grads, axes):
    slabs = []
    for k in range(N_CHIPS):
        pieces = []
        for g, ax in zip(grads, axes):
            n = g.shape[ax] // N_CHIPS
            pieces.append(lax.slice_in_dim(g, k * n, (k + 1) * n, axis=ax).reshape(-1))
        slabs.append(_flatten_shards(pieces, F32))
    return jnp.stack(slabs, axis=1)


def _unflatten_shard(flat2, shard_shapes):
    flat = flat2.reshape(-1)
    out, off = [], 0
    for shp in shard_shapes:
        n = math.prod(shp)
        out.append(flat[off:off + n].reshape(shp))
        off += n
    return out


def kernel(x, ssd_w_in, ssd_conv_w, ssd_conv_b, ssd_dt_bias, ssd_a_log, ssd_d, ssd_norm_w, ssd_w_out, pool_w, pool_scale, ffn_w_up, ffn_conv_w, ffn_conv_b, ffn_w_down, norm_mix_pre, norm_mix_post, norm_ffn_pre, norm_ffn_post, loss_target, m_ssd_w_in, m_ssd_conv_w, m_ssd_conv_b, m_ssd_dt_bias, m_ssd_a_log, m_ssd_d, m_ssd_norm_w, m_ssd_w_out, m_pool_w, m_pool_scale, m_ffn_w_up, m_ffn_conv_w, m_ffn_conv_b, m_ffn_w_down, m_norm_mix_pre, m_norm_mix_post, m_norm_ffn_pre, m_norm_ffn_post, v_ssd_w_in, v_ssd_conv_w, v_ssd_conv_b, v_ssd_dt_bias, v_ssd_a_log, v_ssd_d, v_ssd_norm_w, v_ssd_w_out, v_pool_w, v_pool_scale, v_ffn_w_up, v_ffn_conv_w, v_ffn_conv_b, v_ffn_w_down, v_norm_mix_pre, v_norm_mix_post, v_norm_ffn_pre, v_norm_ffn_post):
    wts = dict(ssd_w_in=ssd_w_in, ssd_conv_w=ssd_conv_w, ssd_conv_b=ssd_conv_b, ssd_dt_bias=ssd_dt_bias,
               ssd_a_log=ssd_a_log, ssd_d=ssd_d, ssd_norm_w=ssd_norm_w, ssd_w_out=ssd_w_out, pool_w=pool_w,
               pool_scale=pool_scale, ffn_w_up=ffn_w_up, ffn_conv_w=ffn_conv_w, ffn_conv_b=ffn_conv_b,
               ffn_w_down=ffn_w_down, norm_mix_pre=norm_mix_pre, norm_mix_post=norm_mix_post,
               norm_ffn_pre=norm_ffn_pre, norm_ffn_post=norm_ffn_post)
    mom = dict(ssd_w_in=m_ssd_w_in, ssd_conv_w=m_ssd_conv_w, ssd_conv_b=m_ssd_conv_b, ssd_dt_bias=m_ssd_dt_bias,
               ssd_a_log=m_ssd_a_log, ssd_d=m_ssd_d, ssd_norm_w=m_ssd_norm_w, ssd_w_out=m_ssd_w_out, pool_w=m_pool_w,
               pool_scale=m_pool_scale, ffn_w_up=m_ffn_w_up, ffn_conv_w=m_ffn_conv_w, ffn_conv_b=m_ffn_conv_b,
               ffn_w_down=m_ffn_w_down, norm_mix_pre=m_norm_mix_pre, norm_mix_post=m_norm_mix_post,
               norm_ffn_pre=m_norm_ffn_pre, norm_ffn_post=m_norm_ffn_post)
    var = dict(ssd_w_in=v_ssd_w_in, ssd_conv_w=v_ssd_conv_w, ssd_conv_b=v_ssd_conv_b, ssd_dt_bias=v_ssd_dt_bias,
               ssd_a_log=v_ssd_a_log, ssd_d=v_ssd_d, ssd_norm_w=v_ssd_norm_w, ssd_w_out=v_ssd_w_out, pool_w=v_pool_w,
               pool_scale=v_pool_scale, ffn_w_up=v_ffn_w_up, ffn_conv_w=v_ffn_conv_w, ffn_conv_b=v_ffn_conv_b,
               ffn_w_down=v_ffn_w_down, norm_mix_pre=v_norm_mix_pre, norm_mix_post=v_norm_mix_post,
               norm_ffn_pre=v_norm_ffn_pre, norm_ffn_post=v_norm_ffn_post)

    bl, seq, d = x.shape
    t = bl * seq
    depth = norm_mix_pre.shape[0]
    n_ssd = ssd_w_out.shape[0]
    d_inner = ssd_w_out.shape[1] * N_CHIPS
    nheads = d_inner // HEAD_DIM
    hpg = nheads // N_GROUPS
    gw = d_inner // N_GROUPS
    xbc = ssd_conv_w.shape[2] * N_CHIPS
    f2 = ffn_w_up.shape[2] * N_CHIPS
    ff = f2 // 2
    dg = d // 4
    cy = lax.axis_index("c")
    chip = 2 * lax.axis_index("x") + lax.axis_index("y")

    big_shapes = [wts[n].shape for n, _ in BIG]
    big_axes = [a for _, a in BIG]
    small_shapes = [wts[n].shape for n, _ in SMALL]
    small_axes = [a for _, a in SMALL]
    big_flat = _flatten_shards([wts[n] for n, _ in BIG], BF16)
    small_flat = _flatten_shards([wts[n] for n, _ in SMALL], F32)
    big_half = lax.dynamic_index_in_dim(big_flat, cy, 0, keepdims=False)
    small_half = lax.dynamic_index_in_dim(small_flat, cy, 0, keepdims=False)
    big_all = _allgather_halves(big_half, "gather_big")
    small_all = _allgather_halves(small_half, "gather_small")
    w_in, w_out, w_pool, w_up, w_down = _unflatten_full(big_all, big_shapes, big_axes)
    conv_w, p_scale, f_conv_w = _unflatten_full(small_all, small_shapes, small_axes)

    nj = ff // SHIFT_COLS

    def interleave(a):
        lead = a.shape[:-1]
        return jnp.swapaxes(a.reshape(lead + (2, nj, SHIFT_COLS)), -3, -2).reshape(lead + (f2,))

    def deinterleave(a):
        lead = a.shape[:-1]
        return jnp.swapaxes(a.reshape(lead + (nj, 2, SHIFT_COLS)), -3, -2).reshape(lead + (f2,))

    def pad_heads(a):
        lead = a.shape[:-1]
        a = a.reshape(lead + (N_GROUPS, hpg))
        a = jnp.pad(a, [(0, 0)] * len(lead) + [(0, 0), (0, LANES - hpg)])
        return a.reshape(lead + (N_GROUPS * LANES,))

    def unpad_heads(a):
        lead = a.shape[:-1]
        return a.reshape(lead + (N_GROUPS, LANES))[..., :hpg].reshape(lead + (nheads,))

    def group_rows(a, width):
        return jnp.broadcast_to(a.reshape(N_GROUPS, 1, width), (N_GROUPS, 8, width))

    w_up_i = interleave(w_up)
    f_conv_w_i = interleave(f_conv_w)
    f_conv_b_i = interleave(ffn_conv_b)
    w_in_p = jnp.concatenate([w_in[..., :d_inner + xbc], pad_heads(w_in[..., d_inner + xbc:])], axis=-1)
    zw = w_in_p.shape[-1]

    x2 = x.reshape(t, d)
    tgt2 = loss_target.reshape(t, d)

    saved = []
    cur = x2
    for i in range(depth):
        j = i // 2
        sv = dict(x_in=cur)
        if i % 2 == 0:
            h = _norm_fwd(cur, norm_mix_pre[i:i + 1], BF16, f"norm_pre_b")
            zx = _mm(h, w_in_p[j], "nn", F32, "mm_up", 1024, 512, d).reshape(bl, seq, zw)
            xc = _ssd_conv_fwd(zx, conv_w[j], ssd_conv_b[j:j + 1], d_inner, "ssd_conv_fwd")
            dtb = group_rows(pad_heads(ssd_dt_bias[j]), LANES)
            alog = group_rows(pad_heads(ssd_a_log[j]), LANES)
            dskip = group_rows(jnp.repeat(ssd_d[j], HEAD_DIM), gw)
            nw = group_rows(ssd_norm_w[j], gw)
            y, yn, st = _ssd_fwd(xc, zx, dtb, alog, dskip, nw, d_inner, "ssd_fwd")
            mix = _mm(yn.reshape(t, d_inner), w_out[j], "nn", F32, "mm_ssd_out", 512, 512, d_inner)
            sv.update(h=h, zx=zx, xc=xc, y=y, yn=yn, st=st, dtb=dtb, alog=alog, dskip=dskip, nw=nw)
        else:
            h = _norm_fwd(cur, norm_mix_pre[i:i + 1], F32, "norm_pre_f")
            mix = _pool_fwd(h.reshape(bl, seq, d), w_pool[j], p_scale[j:j + 1], "pool_fwd").reshape(t, d)
            sv.update(h=h)
        sv.update(mix=mix)
        mid = _norm_fwd(mix, norm_mix_post[i:i + 1], F32, "norm_post", resid=cur)
        u = _norm_fwd(mid, norm_ffn_pre[i:i + 1], BF16, "norm_pre_b")
        hpre = _mm(u, w_up_i[i], "nn", F32, "mm_up", 1024, 512, d).reshape(bl, seq, f2)
        act = _ffn_act_fwd(hpre, f_conv_w_i[i], f_conv_b_i[i:i + 1], "ffn_act_fwd").reshape(t, ff)
        fo = _mm(act, w_down[i], "nn", F32, "mm_down", 512, 512, ff // 2)
        cur = _norm_fwd(fo, norm_ffn_post[i:i + 1], F32, "norm_post", resid=mid)
        sv.update(mid=mid, u=u, hpre=hpre, act=act, fo=fo)
        saved.append(sv)

    dcur, loss_part = _loss_head(cur, tgt2, "loss_head")

    g = {n: [None] * wts[n].shape[0] for n in WEIGHTS}
    for i in reversed(range(depth)):
        j = i // 2
        sv = saved[i]
        dfo, g["norm_ffn_post"][i] = _norm_bwd(sv["fo"], norm_ffn_post[i:i + 1], dcur, BF16, "norm_bwd_b")
        dact = _mm(dfo, w_down[i], "nt", F32, "mm_down_dx", 1024, 256, d)
        g["ffn_w_down"][i] = _mm(sv["act"], dfo, "tn", F32, "mm_down_dw", ff // 2, 512, 512)
        dhpre, dcw, dcb = _ffn_act_bwd(sv["hpre"], dact.reshape(bl, seq, ff), f_conv_w_i[i], f_conv_b_i[i:i + 1],
                                       "ffn_act_bwd")
        g["ffn_conv_w"][i] = deinterleave(dcw)
        g["ffn_conv_b"][i] = deinterleave(dcb)[0]
        dhpre2 = dhpre.reshape(t, f2)
        du = _mm(dhpre2, w_up_i[i], "nt", F32, "mm_up_dx", 512, d, 512)
        g["ffn_w_up"][i] = deinterleave(_mm(sv["u"], dhpre2, "tn", F32, "mm_up_dw", 512, 512, 512))
        dmid, g["norm_ffn_pre"][i] = _norm_bwd(sv["mid"], norm_ffn_pre[i:i + 1], du, F32, "norm_bwd_r", resid=dcur)
        if i % 2 == 0:
            dmix, g["norm_mix_post"][i] = _norm_bwd(sv["mix"], norm_mix_post[i:i + 1], dmid, BF16, "norm_bwd_b")
            dyn = _mm(dmix, w_out[j], "nt", F32, "mm_ssd_out_dx", 1024, 512, d)
            g["ssd_w_out"][j] = _mm(sv["yn"].reshape(t, d_inner), dmix, "tn", F32, "mm_ssd_out_dw", 512, 512, 512)
            dz, dxs, dbm, dcm, ddt, dnw, dd, dal, dbias = _ssd_bwd(
                sv["xc"], sv["zx"], sv["y"], dyn.reshape(bl, seq, d_inner), sv["st"], sv["dtb"], sv["alog"],
                sv["dskip"], sv["nw"], d_inner, "ssd_bwd")
            g["ssd_norm_w"][j] = dnw[:, 0, :].reshape(d_inner)
            g["ssd_d"][j] = dd[:, 0, :hpg].reshape(nheads)
            g["ssd_a_log"][j] = dal[:, 0, :hpg].reshape(nheads)
            g["ssd_dt_bias"][j] = dbias[:, 0, :hpg].reshape(nheads)
            dxc = jnp.concatenate([dxs, dbm, dcm], axis=-1)
            dxbc, dcw, dcb = _ssd_conv_bwd(sv["zx"], dxc, conv_w[j], ssd_conv_b[j:j + 1], d_inner, "ssd_conv_bwd")
            g["ssd_conv_w"][j] = dcw
            g["ssd_conv_b"][j] = dcb[0]
            dzx = jnp.concatenate([dz, dxbc, ddt], axis=-1).reshape(t, zw)
            dh = _mm(dzx, w_in_p[j], "nt", F32, "mm_up_dx", 512, d, 512)
            dwin = _mm(sv["h"], dzx, "tn", F32, "mm_up_dw", 512, 512, 512)
            g["ssd_w_in"][j] = jnp.concatenate([dwin[:, :d_inner + xbc], unpad_heads(dwin[:, d_inner + xbc:])], axis=-1)
        else:
            dmix, g["norm_mix_post"][i] = _norm_bwd(sv["mix"], norm_mix_post[i:i + 1], dmid, F32, "norm_bwd_f")
            dh3, g["pool_w"][j], dps = _pool_bwd(sv["h"].reshape(bl, seq, d), dmix.reshape(bl, seq, d), w_pool[j],
                                                 p_scale[j:j + 1], "pool_bwd")
            g["pool_scale"][j] = dps[0]
            dh = dh3.reshape(t, d)
        dcur, g["norm_mix_pre"][i] = _norm_bwd(sv["x_in"], norm_mix_pre[i:i + 1], dh, F32, "norm_bwd_r", resid=dmid)

    grad_x = dcur.reshape(bl, seq, d)
    for n in ("norm_mix_pre", "norm_mix_post", "norm_ffn_pre", "norm_ffn_post"):
        g[n] = [a[0] for a in g[n]]
    full = {n: jnp.stack(g[n], axis=0) for n in WEIGHTS}

    gflat = _flatten_full_grads([full[n] for n, _ in BIG], big_axes)
    keep = lax.dynamic_index_in_dim(gflat, cy, 0, keepdims=False)
    give = lax.dynamic_index_in_dim(gflat, 1 - cy, 0, keepdims=False)
    rows2 = keep.shape[1]
    got = _swap_sibling(give, "swap_grads")
    chip_sum = _sum_slabs([keep.reshape(N_CHIPS * rows2, FLAT_COLS), got.reshape(N_CHIPS * rows2, FLAT_COLS)],
                          "sum_pair").reshape(N_CHIPS, rows2, FLAT_COLS)
    parts = _scatter_chips(chip_sum, "scatter_grads")
    half_sum = _sum_slabs([parts[k] for k in range(N_CHIPS)], "sum_chips")
    shard_flat = _share_halves(half_sum, "share_grads")
    big_grads = dict(zip([n for n, _ in BIG], _unflatten_shard(shard_flat, big_shapes)))

    small_names = [n for n, _ in SMALL] + list(REPL)
    vec = jnp.concatenate([full[n].reshape(-1) for n in small_names] + [loss_part[0, :1]])
    nvec = vec.shape[0]
    vrows = 8 * ((nvec + 8 * FLAT_COLS - 1) // (8 * FLAT_COLS))
    vec = jnp.pad(vec, (0, vrows * FLAT_COLS - nvec)).reshape(vrows, FLAT_COLS)
    tot = _allreduce_small(vec, "allreduce_small").reshape(-1)
    small_grads, off = {}, 0
    for n in small_names:
        cnt = math.prod(full[n].shape)
        small_grads[n] = tot[off:off + cnt].reshape(full[n].shape)
        off += cnt
    loss = tot[off]
    for n, ax in SMALL:
        w = wts[n].shape[ax]
        small_grads[n] = lax.dynamic_slice_in_dim(small_grads[n], chip * w, w, axis=ax)

    grads, deltas, new_m, new_v = {}, {}, {}, {}
    for n in WEIGHTS:
        gr = big_grads[n] if n in big_grads else small_grads[n]
        shp = wts[n].shape
        two = (math.prod(shp[:-1]), shp[-1])
        dl, mn, vn = _adamw(wts[n].reshape(two), gr.reshape(two), mom[n].reshape(two), var[n].reshape(two),
                            "adamw_" + n)
        grads[n], deltas[n], new_m[n], new_v[n] = gr, dl.reshape(shp), mn.reshape(shp), vn.reshape(shp)

    return (loss, grad_x, *[grads[n] for n in WEIGHTS], *[deltas[n] for n in WEIGHTS],
            *[new_m[n] for n in WEIGHTS], *[new_v[n] for n in WEIGHTS])
```

```python
import functools
import math

import jax
import jax.numpy as jnp
from jax import lax
from jax.experimental import pallas as pl
from jax.experimental.pallas import tpu as pltpu

F32 = jnp.float32
BF16 = jnp.bfloat16
MESH = pl.DeviceIdType.MESH
ANY = pl.BlockSpec(memory_space=pl.ANY)

HEAD_DIM = 64
D_STATE = 128
CHUNK = 128
N_GROUPS = 4
SSD_CONV = 4
FFN_CONV = 3
EPS = 1e-6
N_CHIPS = 4
LANES = 128
FLAT_COLS = 1024

ADAM_LR = 0.001
ADAM_B1 = 0.9
ADAM_B2 = 0.999
ADAM_EPS = 1e-08
ADAM_WD = 0.01
ADAM_STEP = 10

VMEM_LIMIT_BYTES = 56 * 1024 * 1024


def _params(sem=None):
    kw = dict(vmem_limit_bytes=VMEM_LIMIT_BYTES)
    if sem is not None:
        kw["dimension_semantics"] = sem
    return pltpu.CompilerParams(**kw)


def _sigmoid(x):
    return 1.0 / (1.0 + jnp.exp(-x))


def _softplus(x):
    return jnp.maximum(x, 0.0) + jnp.log(1.0 + jnp.exp(-jnp.abs(x)))


def _dot(a, b, dn):
    return lax.dot_general(a, b, (dn, ((), ())), preferred_element_type=F32)


def _nn(a, b):
    return _dot(a, b, ((1,), (0,)))


def _nt(a, b):
    return _dot(a, b, ((1,), (1,)))


def _tn(a, b):
    return _dot(a, b, ((0,), (0,)))


def _split(x, parts):
    out = []
    r = x
    for _ in range(parts):
        p = r.astype(BF16)
        out.append(p)
        r = r - p.astype(F32)
    return out


def _sel_left(sel, x, parts=3):
    n = x.shape[1]
    r = _nn(sel, jnp.concatenate(_split(x, parts), axis=1))
    out = r[:, 0:n]
    for i in range(1, parts):
        out = out + r[:, i * n:(i + 1) * n]
    return out


def _sel_right(x, sel_stacked, parts=3):
    return _nn(jnp.concatenate(_split(x, parts), axis=1), sel_stacked)


def _mm(a, b, dims, out_dtype, name, tm, tn, tk, b_layer=None, out_buf=None):
    a_list = list(a) if isinstance(a, (list, tuple)) else [a]
    b_list = list(b) if isinstance(b, (list, tuple)) else [b]
    if dims in ("nn", "nt"):
        assert len(b_list) == 1
        m = a_list[0].shape[0]
        segs = [x.shape[1] for x in a_list]
        k = sum(segs)
        bshape = b_list[0].shape[-2:]
        n = bshape[1] if dims == "nn" else bshape[0]
        assert (bshape[0] if dims == "nn" else bshape[1]) == k
    else:
        assert len(a_list) == 1 and b_layer is None
        k, m = a_list[0].shape
        segs = [x.shape[1] for x in b_list]
        n = sum(segs)
    tm, tn, tk = min(tm, m), min(tn, n), min(tk, k)
    if dims == "tn":
        tn = min(tn, min(segs))
    else:
        tk = min(tk, min(segs))
    unit = tk if dims != "tn" else tn
    assert m % tm == 0 and n % tn == 0 and k % tk == 0 and all(s % unit == 0 for s in segs), (name, m, n, k, segs)
    nk = k // tk
    starts = [sum(segs[:s]) // unit for s in range(len(segs))]
    counts = [s // unit for s in segs]
    nseg = len(segs)
    dn = {"nn": ((1,), (0,)), "nt": ((1,), (1,)), "tn": ((0,), (0,))}[dims]

    def body(*refs):
        a_refs = refs[:len(a_list)]
        b_refs = refs[len(a_list):len(a_list) + len(b_list)]
        rest = refs[len(a_list) + len(b_list) + (0 if out_buf is None else 1):]
        o_ref = rest[0]
        acc = rest[1] if nk > 1 else None
        kk = pl.program_id(2)
        sel = kk if dims != "tn" else pl.program_id(1)

        def step(a_ref, b_ref):
            p = _dot(a_ref[...].astype(BF16), b_ref[...].astype(BF16), dn)
            if nk == 1:
                o_ref[...] = p.astype(out_dtype)
                return

            @pl.when(kk == 0)
            def _():
                acc[...] = p

            @pl.when(kk > 0)
            def _():
                acc[...] += p

        if nseg == 1:
            step(a_refs[0], b_refs[0])
        else:
            for s in range(nseg):
                @pl.when(jnp.logical_and(sel >= starts[s], sel < starts[s] + counts[s]))
                def _(s=s):
                    step(a_refs[s] if dims != "tn" else a_refs[0], b_refs[0] if dims != "tn" else b_refs[s])

        if nk > 1:
            @pl.when(kk == nk - 1)
            def _():
                o_ref[...] = acc[...].astype(out_dtype)

    def seg_index(v, s):
        return v if nseg == 1 else jnp.clip(v - starts[s], 0, counts[s] - 1)

    lead = () if b_layer is None else (b_layer,)
    none = () if b_layer is None else (None,)
    if dims == "nn":
        a_specs = [pl.BlockSpec((tm, tk), lambda i, j, kk, s=s: (i, seg_index(kk, s))) for s in range(nseg)]
        b_specs = [pl.BlockSpec(none + (tk, tn), lambda i, j, kk: lead + (kk, j))]
    elif dims == "nt":
        a_specs = [pl.BlockSpec((tm, tk), lambda i, j, kk, s=s: (i, seg_index(kk, s))) for s in range(nseg)]
        b_specs = [pl.BlockSpec(none + (tn, tk), lambda i, j, kk: lead + (j, kk))]
    else:
        a_specs = [pl.BlockSpec((tk, tm), lambda i, j, kk: (kk, i))]
        b_specs = [pl.BlockSpec((tk, tn), lambda i, j, kk, s=s: (kk, seg_index(j, s))) for s in range(nseg)]
    args = a_list + b_list
    in_specs = a_specs + b_specs
    aliases = {}
    if out_buf is None:
        out_shape = jax.ShapeDtypeStruct((m, n), out_dtype)
        out_spec = pl.BlockSpec((tm, tn), lambda i, j, kk: (i, j))
    else:
        buf, slab = out_buf
        assert buf.shape[1:] == (m, n) and buf.dtype == out_dtype
        out_shape = jax.ShapeDtypeStruct(buf.shape, out_dtype)
        out_spec = pl.BlockSpec((None, tm, tn), lambda i, j, kk: (slab, i, j))
        aliases = {len(args): 0}
        args = args + [buf]
        in_specs = in_specs + [ANY]
    return pl.pallas_call(
        body,
        out_shape=out_shape,
        grid=(m // tm, n // tn, nk),
        in_specs=in_specs,
        out_specs=out_spec,
        scratch_shapes=[] if nk == 1 else [pltpu.VMEM((tm, tn), F32)],
        input_output_aliases=aliases,
        compiler_params=_params(("parallel", "parallel", "arbitrary")),
        name=name,
    )(*args)


def _row_tile(t, want):
    tm = min(want, t)
    assert t % tm == 0
    return tm


def _norm_fwd(x, w, out_dtype, name, resid=None):
    t, d = x.shape
    tm = _row_tile(t, 512)

    def body(*refs):
        if resid is None:
            x_ref, w_ref, o_ref = refs
        else:
            x_ref, w_ref, r_ref, o_ref = refs
        xv = x_ref[...]
        r = lax.rsqrt(jnp.mean(xv * xv, axis=-1, keepdims=True) + EPS)
        y = (xv * r) * w_ref[...]
        if resid is not None:
            y = r_ref[...] + y
        o_ref[...] = y.astype(out_dtype)

    row = pl.BlockSpec((tm, d), lambda i: (i, 0))
    vec = pl.BlockSpec((1, d), lambda i: (0, 0))
    args = [x, w] + ([] if resid is None else [resid])
    return pl.pallas_call(
        body, out_shape=jax.ShapeDtypeStruct((t, d), out_dtype), grid=(t // tm,),
        in_specs=[row, vec] + ([] if resid is None else [row]), out_specs=row,
        compiler_params=_params(("parallel",)), name=name)(*args)


def _norm_bwd(src, w, dy, out_dtype, name, resid=None):
    t, d = src.shape
    tm = _row_tile(t, 512)

    def body(*refs):
        if resid is None:
            x_ref, w_ref, g_ref, o_ref, dw_ref = refs
        else:
            x_ref, w_ref, g_ref, r_ref, o_ref, dw_ref = refs
        xv = x_ref[...]
        g = g_ref[...].astype(F32)
        r = lax.rsqrt(jnp.mean(xv * xv, axis=-1, keepdims=True) + EPS)
        xh = xv * r
        gh = g * w_ref[...]
        mean = jnp.mean(gh * xh, axis=-1, keepdims=True)
        dx = r * (gh - xh * mean)
        if resid is not None:
            dx = r_ref[...] + dx
        o_ref[...] = dx.astype(out_dtype)
        part = jnp.sum(g * xh, axis=0, keepdims=True)

        @pl.when(pl.program_id(0) == 0)
        def _():
            dw_ref[...] = part

        @pl.when(pl.program_id(0) > 0)
        def _():
            dw_ref[...] += part

    row = pl.BlockSpec((tm, d), lambda i: (i, 0))
    vec = pl.BlockSpec((1, d), lambda i: (0, 0))
    args = [src, w, dy] + ([] if resid is None else [resid])
    return pl.pallas_call(
        body,
        out_shape=(jax.ShapeDtypeStruct((t, d), out_dtype), jax.ShapeDtypeStruct((1, d), F32)),
        grid=(t // tm,),
        in_specs=[row, vec, row] + ([] if resid is None else [row]),
        out_specs=(row, vec),
        compiler_params=_params(("arbitrary",)), name=name)(*args)


def _loss_head(y, target, name):
    t, d = y.shape
    tm = _row_tile(t, 512)

    def body(y_ref, t_ref, dy_ref, l_ref):
        e = y_ref[...] - t_ref[...]
        dy_ref[...] = e * (1.0 / d)
        col = jnp.sum(e * e, axis=0, keepdims=True)
        s = jnp.sum(col, axis=1, keepdims=True) * (0.5 / d)
        part = jnp.broadcast_to(s, (1, LANES))

        @pl.when(pl.program_id(0) == 0)
        def _():
            l_ref[...] = part

        @pl.when(pl.program_id(0) > 0)
        def _():
            l_ref[...] += part

    row = pl.BlockSpec((tm, d), lambda i: (i, 0))
    return pl.pallas_call(
        body,
        out_shape=(jax.ShapeDtypeStruct((t, d), F32), jax.ShapeDtypeStruct((1, LANES), F32)),
        grid=(t // tm,), in_specs=[row, row],
        out_specs=(row, pl.BlockSpec((1, LANES), lambda i: (0, 0))),
        compiler_params=_params(("arbitrary",)), name=name)(y, target)


def _window(ref, c, rows, seq, before, after):
    r0 = pl.multiple_of(c * rows, rows)
    parts = []
    if before:
        h0 = pl.multiple_of(jnp.maximum(r0 - before, 0), before)
        halo = ref[pl.ds(h0, before), :].astype(F32)
        parts.append(jnp.where(c > 0, halo, 0.0))
    parts.append(ref[pl.ds(r0, rows), :].astype(F32))
    if after:
        h1 = pl.multiple_of(jnp.minimum(r0 + rows, seq - after), after)
        halo = ref[pl.ds(h1, after), :].astype(F32)
        parts.append(jnp.where(c < seq // rows - 1, halo, 0.0))
    return parts[0] if len(parts) == 1 else jnp.concatenate(parts, axis=0)


def _lag(x, k):
    return pltpu.roll(x, k, 0) if k else x


def _lead(x, k):
    return pltpu.roll(x, x.shape[0] - k, 0) if k else x


SHIFT_ROWS = 128
SHIFT_COLS = 256


HALO = 16


def _conv3(ext, w, bias):
    acc = bias + w[2:3, :] * ext[HALO:, :]
    acc = acc + w[1:2, :] * _lag(ext, 1)[HALO:, :]
    return acc + w[0:1, :] * _lag(ext, 2)[HALO:, :]


def _ffn_act_fwd(hpre, cw, cb, name):
    b, seq, f2 = hpre.shape
    cbk = SHIFT_COLS
    nj = f2 // (2 * cbk)
    rows = min(SHIFT_ROWS, seq)

    def body(g_ref, v_ref, wg_ref, wv_ref, bg_ref, bv_ref, o_ref):
        def chunk(c, carry):
            gate = _conv3(_window(g_ref, c, rows, seq, HALO, 0), wg_ref[...], bg_ref[...])
            val = _conv3(_window(v_ref, c, rows, seq, HALO, 0), wv_ref[...], bv_ref[...])
            a = gate * _sigmoid(gate) * val
            o_ref[pl.ds(pl.multiple_of(c * rows, rows), rows), :] = a.astype(BF16)
            return carry

        lax.fori_loop(0, seq // rows, chunk, 0)

    blk = lambda off: pl.BlockSpec((None, seq, cbk), lambda i, j: (i, 0, j + off))
    wsp = lambda r, off: pl.BlockSpec((r, cbk), lambda i, j: (0, j + off))
    return pl.pallas_call(
        body, out_shape=jax.ShapeDtypeStruct((b, seq, f2 // 2), BF16), grid=(b, nj),
        in_specs=[blk(0), blk(nj), wsp(FFN_CONV, 0), wsp(FFN_CONV, nj), wsp(1, 0), wsp(1, nj)],
        out_specs=blk(0),
        compiler_params=_params(("parallel", "parallel")), name=name)(hpre, hpre, cw, cw, cb, cb)


def _ffn_act_bwd(hpre, da, cw, cb, name):
    b, seq, f2 = hpre.shape
    cbk = SHIFT_COLS
    nj = f2 // (2 * cbk)
    rows = min(SHIFT_ROWS, seq)

    def body(g_ref, v_ref, da_ref, wg_ref, wv_ref, bg_ref, bv_ref, og_ref, ov_ref, dwg_ref, dwv_ref, dbg_ref, dbv_ref):
        wg, wv = wg_ref[...], wv_ref[...]

        def shifted(ref, c):
            ext = _window(ref, c, rows, seq, HALO, HALO)
            return [_lag(ext, k)[HALO:, :] for k in range(FFN_CONV)]

        def back(dpre, w, o_ref, c, xs, carry):
            dx = w[2:3, :] * dpre + w[1:2, :] * _lead(dpre, 1) + w[0:1, :] * _lead(dpre, 2)
            o_ref[pl.ds(pl.multiple_of(c * rows, rows), rows), :] = dx[:rows, :].astype(BF16)
            dp = dpre[:rows, :]
            return tuple(carry[k] + jnp.sum(dp * xs[k][:rows, :], axis=0, keepdims=True) for k in range(FFN_CONV)) + (
                carry[FFN_CONV] + jnp.sum(dp, axis=0, keepdims=True),)

        def chunk(c, carry):
            cg, cv = carry
            gs, vs = shifted(g_ref, c), shifted(v_ref, c)
            gate = bg_ref[...] + wg[2:3, :] * gs[0] + wg[1:2, :] * gs[1] + wg[0:1, :] * gs[2]
            val = bv_ref[...] + wv[2:3, :] * vs[0] + wv[1:2, :] * vs[1] + wv[0:1, :] * vs[2]
            dav = _window(da_ref, c, rows, seq, 0, HALO)
            sg = _sigmoid(gate)
            cg = back(dav * val * (sg * (1.0 + gate * (1.0 - sg))), wg, og_ref, c, gs, cg)
            cv = back(dav * (gate * sg), wv, ov_ref, c, vs, cv)
            return cg, cv

        z = jnp.zeros((1, cbk), F32)
        cg, cv = lax.fori_loop(0, seq // rows, chunk, ((z,) * (FFN_CONV + 1), (z,) * (FFN_CONV + 1)))
        dwg = jnp.concatenate([cg[2], cg[1], cg[0]], axis=0)
        dwv = jnp.concatenate([cv[2], cv[1], cv[0]], axis=0)

        @pl.when(pl.program_id(1) == 0)
        def _():
            dwg_ref[...] = dwg
            dwv_ref[...] = dwv
            dbg_ref[...] = cg[FFN_CONV]
            dbv_ref[...] = cv[FFN_CONV]

        @pl.when(pl.program_id(1) > 0)
        def _():
            dwg_ref[...] += dwg
            dwv_ref[...] += dwv
            dbg_ref[...] += cg[FFN_CONV]
            dbv_ref[...] += cv[FFN_CONV]

    blk = lambda off: pl.BlockSpec((None, seq, cbk), lambda j, i: (i, 0, j + off))
    wsp = lambda r, off: pl.BlockSpec((r, cbk), lambda j, i: (0, j + off))
    half = jax.ShapeDtypeStruct((b, seq, f2 // 2), BF16)
    dwshape = jax.ShapeDtypeStruct((FFN_CONV, f2 // 2), F32)
    dbshape = jax.ShapeDtypeStruct((1, f2 // 2), F32)
    dg, dv, dwg, dwv, dbg, dbv = pl.pallas_call(
        body,
        out_shape=(half, half, dwshape, dwshape, dbshape, dbshape),
        grid=(nj, b),
        in_specs=[blk(0), blk(nj), blk(0), wsp(FFN_CONV, 0), wsp(FFN_CONV, nj), wsp(1, 0), wsp(1, nj)],
        out_specs=(blk(0), blk(0), wsp(FFN_CONV, 0), wsp(FFN_CONV, 0), wsp(1, 0), wsp(1, 0)),
        compiler_params=_params(("parallel", "arbitrary")), name=name)(hpre, hpre, da, cw, cw, cb, cb)
    return dg, dv, jnp.concatenate([dwg, dwv], axis=1), jnp.concatenate([dbg, dbv], axis=1)


def _ssd_conv_fwd(zx, cw, cb, d_inner, name):
    b, seq, _ = zx.shape
    xbc = cw.shape[1]
    cbk = SHIFT_COLS
    off = d_inner // cbk
    rows = min(SHIFT_ROWS, seq)

    def body(h_ref, w_ref, b_ref, o_ref):
        w = w_ref[...]
        bias = b_ref[...]

        def chunk(c, carry):
            ext = _window(h_ref, c, rows, seq, 8, 0)
            acc = bias + w[3:4, :] * ext[8:, :]
            for k in range(1, SSD_CONV):
                acc = acc + w[3 - k:4 - k, :] * _lag(ext, k)[8:, :]
            o_ref[pl.ds(pl.multiple_of(c * rows, rows), rows), :] = acc * _sigmoid(acc)
            return carry

        lax.fori_loop(0, seq // rows, chunk, 0)

    return pl.pallas_call(
        body, out_shape=jax.ShapeDtypeStruct((b, seq, xbc), F32), grid=(b, xbc // cbk),
        in_specs=[pl.BlockSpec((None, seq, cbk), lambda i, j: (i, 0, j + off)),
                  pl.BlockSpec((SSD_CONV, cbk), lambda i, j: (0, j)),
                  pl.BlockSpec((1, cbk), lambda i, j: (0, j))],
        out_specs=pl.BlockSpec((None, seq, cbk), lambda i, j: (i, 0, j)),
        compiler_params=_params(("parallel", "parallel")), name=name)(zx, cw, cb)


def _ssd_conv_bwd(zx, dparts, cw, cb, d_inner, name):
    b, seq, _ = zx.shape
    xbc = cw.shape[1]
    cbk = SHIFT_COLS
    off = d_inner // cbk
    rows = min(SHIFT_ROWS, seq)
    nblk = [p.shape[2] // cbk for p in dparts]
    first = [sum(nblk[:s]) for s in range(len(dparts))]
    assert sum(nblk) == xbc // cbk

    def body(h_ref, gx_ref, gb_ref, gc_ref, w_ref, b_ref, o_ref, dw_ref, db_ref):
        w = w_ref[...]
        bias = b_ref[...]
        j = pl.program_id(0)

        def chunk(c, carry):
            dws, dbias = carry
            ext = _window(h_ref, c, rows, seq, 8, 8)
            xs = [_lag(ext, k)[8:, :] for k in range(SSD_CONV)]
            pre = bias + w[3:4, :] * xs[0]
            for k in range(1, SSD_CONV):
                pre = pre + w[3 - k:4 - k, :] * xs[k]
            s = _sigmoid(pre)
            gsel = jnp.where(j < first[1], _window(gx_ref, c, rows, seq, 0, 8),
                             jnp.where(j < first[2], _window(gb_ref, c, rows, seq, 0, 8),
                                       _window(gc_ref, c, rows, seq, 0, 8)))
            dpre = gsel * (s * (1.0 + pre * (1.0 - s)))
            dx = w[3:4, :] * dpre
            for k in range(1, SSD_CONV):
                dx = dx + w[3 - k:4 - k, :] * _lead(dpre, k)
            o_ref[pl.ds(pl.multiple_of(c * rows, rows), rows), :] = dx[:rows, :].astype(BF16)
            dp = dpre[:rows, :]
            dws = tuple(dws[k] + jnp.sum(dp * xs[k][:rows, :], axis=0, keepdims=True) for k in range(SSD_CONV))
            dbias = dbias + jnp.sum(dp, axis=0, keepdims=True)
            return dws, dbias

        z = jnp.zeros((1, cbk), F32)
        dws, dbias = lax.fori_loop(0, seq // rows, chunk, ((z,) * SSD_CONV, z))
        dwv = jnp.concatenate([dws[3 - i] for i in range(SSD_CONV)], axis=0)

        @pl.when(pl.program_id(1) == 0)
        def _():
            dw_ref[...] = dwv
            db_ref[...] = dbias

        @pl.when(pl.program_id(1) > 0)
        def _():
            dw_ref[...] += dwv
            db_ref[...] += dbias

    return pl.pallas_call(
        body,
        out_shape=(jax.ShapeDtypeStruct((b, seq, xbc), BF16), jax.ShapeDtypeStruct((SSD_CONV, xbc), F32),
                   jax.ShapeDtypeStruct((1, xbc), F32)),
        grid=(xbc // cbk, b),
        in_specs=[pl.BlockSpec((None, seq, cbk), lambda j, i: (i, 0, j + off))] + [
                  pl.BlockSpec((None, seq, cbk), lambda j, i, s=s: (i, 0, jnp.clip(j - first[s], 0, nblk[s] - 1)))
                  for s in range(3)] + [
                  pl.BlockSpec((SSD_CONV, cbk), lambda j, i: (0, j)),
                  pl.BlockSpec((1, cbk), lambda j, i: (0, j))],
        out_specs=(pl.BlockSpec((None, seq, cbk), lambda j, i: (i, 0, j)),
                   pl.BlockSpec((SSD_CONV, cbk), lambda j, i: (0, j)),
                   pl.BlockSpec((1, cbk), lambda j, i: (0, j))),
        compiler_params=_params(("parallel", "arbitrary")), name=name)(zx, *dparts, cw, cb)


def _pool_sums(q, g, lead):
    sh = _lead if lead else _lag
    s2 = q + sh(q, 1)
    s4 = s2 + sh(s2, 2)
    s8 = s4 + sh(s4, 4)
    s16 = s8 + sh(s8, 8)
    return jnp.where(g == 0, s2, jnp.where(g == 1, s4, jnp.where(g == 2, s8, s16)))


def _pool_count(r0, n, g, shape):
    t = (r0 + lax.broadcasted_iota(jnp.int32, shape, 0) + 1).astype(F32)
    return jnp.minimum(t, (2 << g).astype(F32))


def _pool_fwd(h, pw, scale, name):
    b, seq, d = h.shape
    dg = d // 4
    rows = min(SHIFT_ROWS, seq)

    def body(h_ref, w_ref, s_ref, o_ref):
        g = pl.program_id(1)
        wmat = w_ref[...]
        sc = s_ref[...]

        def chunk(c, carry):
            r0 = c * rows
            ext = _window(h_ref, c, rows, seq, 16, 0)
            sums = _pool_sums(ext, g, False)[16:, :]
            mixed = sums / _pool_count(r0, rows, g, (rows, dg)) - ext[16:, :]
            o_ref[pl.ds(pl.multiple_of(r0, rows), rows), :] = _nn(mixed.astype(BF16), wmat) * sc
            return carry

        lax.fori_loop(0, seq // rows, chunk, 0)

    return pl.pallas_call(
        body, out_shape=jax.ShapeDtypeStruct((b, seq, d), F32), grid=(b, 4),
        in_specs=[pl.BlockSpec((None, seq, dg), lambda i, g: (i, 0, g)),
                  pl.BlockSpec((None, dg, dg), lambda i, g: (g, 0, 0)),
                  pl.BlockSpec((1, dg), lambda i, g: (0, g))],
        out_specs=pl.BlockSpec((None, seq, dg), lambda i, g: (i, 0, g)),
        compiler_params=_params(("parallel", "parallel")), name=name)(h, pw, scale)


def _pool_bwd(h, dout, pw, scale, name):
    b, seq, d = h.shape
    dg = d // 4
    rows = min(SHIFT_ROWS, seq)

    def body(h_ref, g_ref, w_ref, s_ref, o_ref, dw_ref, ds_ref, dw_acc):
        g = pl.program_id(0)
        wmat = w_ref[...]
        sc = s_ref[...]
        dw_acc[...] = jnp.zeros_like(dw_acc)

        def chunk(c, dsc):
            r0 = c * rows
            ext = _window(h_ref, c, rows, seq, 16, 0)
            sums = _pool_sums(ext, g, False)[16:, :]
            mixed = (sums / _pool_count(r0, rows, g, (rows, dg)) - ext[16:, :]).astype(BF16)
            gext = _window(g_ref, c, rows, seq, 0, 16)
            dsc = dsc + jnp.sum(gext[:rows, :] * _nn(mixed, wmat), axis=0, keepdims=True)
            dpre = (gext * sc).astype(BF16)
            dw_acc[...] += _tn(mixed, dpre[:rows, :])
            dmix = _nt(dpre, wmat)
            q = dmix / _pool_count(r0, rows + 16, g, (rows + 16, dg))
            back = _pool_sums(q, g, True)
            o_ref[pl.ds(pl.multiple_of(r0, rows), rows), :] = back[:rows, :] - dmix[:rows, :]
            return dsc

        dsc = lax.fori_loop(0, seq // rows, chunk, jnp.zeros((1, dg), F32))

        @pl.when(pl.program_id(1) == 0)
        def _():
            dw_ref[...] = dw_acc[...]
            ds_ref[...] = dsc

        @pl.when(pl.program_id(1) > 0)
        def _():
            dw_ref[...] += dw_acc[...]
            ds_ref[...] += dsc

    return pl.pallas_call(
        body,
        out_shape=(jax.ShapeDtypeStruct((b, seq, d), F32), jax.ShapeDtypeStruct((4, dg, dg), F32),
                   jax.ShapeDtypeStruct((1, d), F32)),
        grid=(4, b),
        in_specs=[pl.BlockSpec((None, seq, dg), lambda g, i: (i, 0, g)),
                  pl.BlockSpec((None, seq, dg), lambda g, i: (i, 0, g)),
                  pl.BlockSpec((None, dg, dg), lambda g, i: (g, 0, 0)),
                  pl.BlockSpec((1, dg), lambda g, i: (0, g))],
        out_specs=(pl.BlockSpec((None, seq, dg), lambda g, i: (i, 0, g)),
                   pl.BlockSpec((None, dg, dg), lambda g, i: (g, 0, 0)),
                   pl.BlockSpec((1, dg), lambda g, i: (0, g))),
        scratch_shapes=[pltpu.VMEM((dg, dg), F32)],
        compiler_params=_params(("parallel", "arbitrary")), name=name)(h, dout, pw, scale)


def _head_of(channel):
    return jnp.right_shift(channel, HEAD_DIM.bit_length() - 1)


def _ssd_consts(gw):
    q = CHUNK
    row = lax.broadcasted_iota(jnp.int32, (q, q), 0)
    col = lax.broadcasted_iota(jnp.int32, (q, q), 1)
    tril = (row >= col).astype(BF16)
    triu = (row <= col).astype(BF16)
    e = (_head_of(lax.broadcasted_iota(jnp.int32, (LANES, gw), 1))
         == lax.broadcasted_iota(jnp.int32, (LANES, gw), 0)).astype(BF16)
    et = (_head_of(lax.broadcasted_iota(jnp.int32, (gw, LANES), 0))
          == lax.broadcasted_iota(jnp.int32, (gw, LANES), 1)).astype(BF16)
    return row, col, tril, triu, e, et


def _ssd_common(dtr, dtb, alog, gw):
    q = CHUNK
    row, col, tril, triu, e, et = _ssd_consts(gw)
    dt = _softplus(dtr + dtb)
    a_row = -jnp.exp(alog)
    acum = _sel_left(tril, dt * a_row)
    ac_last = jnp.sum(jnp.where(row == q - 1, acum, 0.0), axis=0, keepdims=True)
    eac = jnp.exp(acum)
    de = jnp.exp(ac_last - acum)
    e3 = jnp.concatenate([e, e, e], axis=0)
    expand = _sel_right(jnp.concatenate([dt, eac, de], axis=0), e3)
    dt_x, eac_x, de_x = expand[0:q], expand[q:2 * q], expand[2 * q:3 * q]
    acum_t = acum.T
    cd_col = jnp.exp(acum_t[:, q - 1:q])
    et3 = jnp.concatenate([et, et, et], axis=1)
    cdmat = _nn(et3, jnp.concatenate(_split(jnp.broadcast_to(cd_col, (LANES, D_STATE)), 3), axis=0))
    consts = dict(row=row, col=col, tril=tril, triu=triu, e=e, et=et)
    return dt, a_row, acum, acum_t, ac_last, eac, de, dt_x, eac_x, de_x, cdmat, consts


def _decay(acum, acum_t, j, row, col):
    diff = acum[:, j:j + 1] - acum_t[j:j + 1, :]
    return jnp.exp(jnp.where(row >= col, diff, -1e30))


def _ssd_fwd(xc, zx, dtb, alog, dskip, nw, d_inner, name):
    b, seq, xbc = xc.shape
    q = CHUNK
    nc = seq // q
    gw = d_inner // N_GROUPS
    nh = gw // HEAD_DIM
    xb0 = d_inner // D_STATE
    xc0 = xb0 + N_GROUPS
    dt0 = (d_inner + xbc) // LANES

    def body(x_ref, b_ref, c_ref, z_ref, dtr_ref, dtb_ref, al_ref, dsk_ref, nw_ref, y_ref, yn_ref, st_ref, s_ref):
        @pl.when(pl.program_id(2) == 0)
        def _():
            s_ref[...] = jnp.zeros_like(s_ref)

        prev = s_ref[...]
        st_ref[...] = prev
        x = x_ref[...]
        bm = b_ref[...].astype(BF16)
        cm = c_ref[...].astype(BF16)
        (dt, a_row, acum, acum_t, ac_last, eac, de, dt_x, eac_x, de_x, cdmat, k) = _ssd_common(
            dtr_ref[...], dtb_ref[0:1, :], al_ref[0:1, :], gw)
        xdt = x * dt_x
        xdt_b = xdt.astype(BF16)
        cb = _nt(cm, bm)
        head = _head_of(lax.broadcasted_iota(jnp.int32, (q, gw), 1))
        y = dsk_ref[0:1, :] * x
        for j in range(nh):
            m = (cb * _decay(acum, acum_t, j, k["row"], k["col"])).astype(BF16)
            y = y + jnp.where(head == j, _nn(m, xdt_b), 0.0)
        prev_b = prev.astype(BF16)
        y = y + eac_x * _nt(cm, prev_b)
        s_ref[...] = cdmat * prev + _tn((xdt * de_x).astype(BF16), bm)
        y_ref[...] = y
        z = z_ref[...]
        yg = y * (z * _sigmoid(z))
        r = lax.rsqrt(jnp.mean(yg * yg, axis=-1, keepdims=True) + EPS)
        yn_ref[...] = ((yg * r) * nw_ref[0:1, :]).astype(BF16)

    par = lambda w: pl.BlockSpec((None, 8, w), lambda i, g, c: (g, 0, 0))
    return pl.pallas_call(
        body,
        out_shape=(jax.ShapeDtypeStruct((b, seq, d_inner), F32), jax.ShapeDtypeStruct((b, seq, d_inner), BF16),
                   jax.ShapeDtypeStruct((b, nc, N_GROUPS, gw, D_STATE), F32)),
        grid=(b, N_GROUPS, nc),
        in_specs=[pl.BlockSpec((None, q, gw), lambda i, g, c: (i, c, g)),
                  pl.BlockSpec((None, q, D_STATE), lambda i, g, c: (i, c, xb0 + g)),
                  pl.BlockSpec((None, q, D_STATE), lambda i, g, c: (i, c, xc0 + g)),
                  pl.BlockSpec((None, q, gw), lambda i, g, c: (i, c, g)),
                  pl.BlockSpec((None, q, LANES), lambda i, g, c: (i, c, dt0 + g)),
                  par(LANES), par(LANES), par(gw), par(gw)],
        out_specs=(pl.BlockSpec((None, q, gw), lambda i, g, c: (i, c, g)),
                   pl.BlockSpec((None, q, gw), lambda i, g, c: (i, c, g)),
                   pl.BlockSpec((None, None, None, gw, D_STATE), lambda i, g, c: (i, c, g, 0, 0))),
        scratch_shapes=[pltpu.VMEM((gw, D_STATE), F32)],
        compiler_params=_params(("parallel", "parallel", "arbitrary")), name=name,
    )(xc, xc, xc, zx, zx, dtb, alog, dskip, nw)


def _ssd_bwd(xc, zx, y, dyn, st, dtb, alog, dskip, nw, d_inner, name):
    b, seq, xbc = xc.shape
    q = CHUNK
    nc = seq // q
    gw = d_inner // N_GROUPS
    nh = gw // HEAD_DIM
    xb0 = d_inner // D_STATE
    xc0 = xb0 + N_GROUPS
    dt0 = (d_inner + xbc) // LANES

    def body(x_ref, b_ref, c_ref, z_ref, dtr_ref, y_ref, g_ref, st_ref, dtb_ref, al_ref, dsk_ref, nw_ref,
             dz_ref, dx_ref, db_ref, dc_ref, ddt_ref, dnw_ref, dd_ref, dal_ref, dbias_ref,
             ds_ref, colbuf, rowbuf):
        first = jnp.logical_and(pl.program_id(1) == 0, pl.program_id(2) == 0)

        @pl.when(pl.program_id(2) == 0)
        def _():
            ds_ref[...] = jnp.zeros_like(ds_ref)

        x = x_ref[...]
        bm = b_ref[...].astype(BF16)
        cm = c_ref[...].astype(BF16)
        z = z_ref[...]
        y = y_ref[...]
        prev = st_ref[...]
        dtr = dtr_ref[...] + dtb_ref[0:1, :]
        (dt, a_row, acum, acum_t, ac_last, eac, de, dt_x, eac_x, de_x, cdmat, k) = _ssd_common(
            dtr_ref[...], dtb_ref[0:1, :], al_ref[0:1, :], gw)
        row, col = k["row"], k["col"]
        et2 = jnp.concatenate([k["et"], k["et"]], axis=0)
        et3 = jnp.concatenate([k["et"], k["et"], k["et"]], axis=0)
        head = _head_of(lax.broadcasted_iota(jnp.int32, (q, gw), 1))

        sz = _sigmoid(z)
        silu_z = z * sz
        yg = y * silu_z
        r = lax.rsqrt(jnp.mean(yg * yg, axis=-1, keepdims=True) + EPS)
        xh = yg * r
        dyn = g_ref[...]
        gh = dyn * nw_ref[0:1, :]
        dyg = r * (gh - xh * jnp.mean(gh * xh, axis=-1, keepdims=True))
        dnw = jnp.sum(dyn * xh, axis=0, keepdims=True)
        g = dyg * silu_z
        dz_ref[...] = (dyg * y * (sz * (1.0 + z * (1.0 - sz)))).astype(BF16)
        dd = _sel_right(jnp.broadcast_to(jnp.sum(g * x, axis=0, keepdims=True), (8, gw)), et3)

        xdt = x * dt_x
        xdt_b = xdt.astype(BF16)
        g_b = g.astype(BF16)
        prev_b = prev.astype(BF16)
        cb = _nt(cm, bm)

        cp = _nt(cm, prev_b)
        ge = g * eac_x
        dac = _sel_right(ge * cp, et3)
        ge_b = ge.astype(BF16)
        dcm = _nn(ge_b, prev_b)
        dprev = _tn(ge_b, cm)

        colbuf[...] = jnp.zeros_like(colbuf)
        rowbuf[...] = jnp.zeros_like(rowbuf)
        dcb = jnp.zeros((q, q), F32)
        dxdt = jnp.zeros((q, gw), F32)
        for j in range(nh):
            dec = _decay(acum, acum_t, j, row, col)
            m = cb * dec
            dm = _nt(jnp.where(head == j, g, 0.0).astype(BF16), xdt_b)
            w = dm * m
            colbuf[:, j:j + 1] = jnp.sum(w, axis=1, keepdims=True)
            rowbuf[j:j + 1, :] = jnp.sum(w, axis=0, keepdims=True)
            dcb = dcb + dm * dec
            dxdt = dxdt + jnp.where(head == j, _tn(m.astype(BF16), g_b), 0.0)
        dcb_b = dcb.astype(BF16)
        dcm = dcm + _nn(dcb_b, bm)
        dbm = _tn(dcb_b, cm)

        ds = ds_ref[...]
        ds_b = ds.astype(BF16)
        u = _nt(bm, ds_b)
        dxdt = dxdt + u * de_x
        dde = _sel_right(u * xdt, et3)
        dbm = dbm + _nn((xdt * de_x).astype(BF16), ds_b)
        pm = jnp.concatenate(_split(ds * prev, 2), axis=1)
        t2 = _tn(pm, k["et"])
        dcd_row = jnp.sum(t2[0:D_STATE] + t2[D_STATE:2 * D_STATE], axis=0, keepdims=True)
        last = dcd_row * jnp.exp(ac_last) + jnp.sum(dde * de, axis=0, keepdims=True)
        dac = dac + colbuf[...] - rowbuf[...].T - dde * de + jnp.where(row == q - 1, last, 0.0)
        ds_ref[...] = cdmat * ds + dprev

        dadt = _sel_left(k["triu"], dac)
        ddt = _sel_right(dxdt * x, et3) + dadt * a_row
        dal = jnp.sum(dadt * dt, axis=0, keepdims=True) * a_row
        lane = lax.broadcasted_iota(jnp.int32, (q, LANES), 1)
        ddtr = jnp.where(lane < nh, ddt * _sigmoid(dtr), 0.0)
        ddt_ref[...] = ddtr.astype(BF16)
        dbias = jnp.sum(ddtr, axis=0, keepdims=True)
        dx_ref[...] = dxdt * dt_x + dsk_ref[0:1, :] * g
        db_ref[...] = dbm
        dc_ref[...] = dcm

        @pl.when(first)
        def _():
            dnw_ref[...] = jnp.broadcast_to(dnw, (8, gw))
            dd_ref[...] = dd
            dal_ref[...] = jnp.broadcast_to(dal, (8, LANES))
            dbias_ref[...] = jnp.broadcast_to(dbias, (8, LANES))

        @pl.when(jnp.logical_not(first))
        def _():
            dnw_ref[...] += jnp.broadcast_to(dnw, (8, gw))
            dd_ref[...] += dd
            dal_ref[...] += jnp.broadcast_to(dal, (8, LANES))
            dbias_ref[...] += jnp.broadcast_to(dbias, (8, LANES))

    rc = lambda c: nc - 1 - c
    par = lambda w: pl.BlockSpec((None, 8, w), lambda g, i, c: (g, 0, 0))
    blk = lambda w: pl.BlockSpec((None, q, w), lambda g, i, c: (i, rc(c), g))
    return pl.pallas_call(
        body,
        out_shape=(jax.ShapeDtypeStruct((b, seq, d_inner), BF16),
                   jax.ShapeDtypeStruct((b, seq, d_inner), F32),
                   jax.ShapeDtypeStruct((b, seq, N_GROUPS * D_STATE), F32),
                   jax.ShapeDtypeStruct((b, seq, N_GROUPS * D_STATE), F32),
                   jax.ShapeDtypeStruct((b, seq, N_GROUPS * LANES), BF16),
                   jax.ShapeDtypeStruct((N_GROUPS, 8, gw), F32),
                   jax.ShapeDtypeStruct((N_GROUPS, 8, LANES), F32),
                   jax.ShapeDtypeStruct((N_GROUPS, 8, LANES), F32),
                   jax.ShapeDtypeStruct((N_GROUPS, 8, LANES), F32)),
        grid=(N_GROUPS, b, nc),
        in_specs=[blk(gw),
                  pl.BlockSpec((None, q, D_STATE), lambda g, i, c: (i, rc(c), xb0 + g)),
                  pl.BlockSpec((None, q, D_STATE), lambda g, i, c: (i, rc(c), xc0 + g)),
                  blk(gw),
                  pl.BlockSpec((None, q, LANES), lambda g, i, c: (i, rc(c), dt0 + g)),
                  blk(gw), blk(gw),
                  pl.BlockSpec((None, None, None, gw, D_STATE), lambda g, i, c: (i, rc(c), g, 0, 0)),
                  par(LANES), par(LANES), par(gw), par(gw)],
        out_specs=(blk(gw), blk(gw), blk(D_STATE), blk(D_STATE), blk(LANES),
                   par(gw), par(LANES), par(LANES), par(LANES)),
        scratch_shapes=[pltpu.VMEM((gw, D_STATE), F32), pltpu.VMEM((q, LANES), F32), pltpu.VMEM((LANES, q), F32)],
        compiler_params=_params(("parallel", "arbitrary", "arbitrary")), name=name,
    )(xc, xc, xc, zx, zx, y, dyn, st, dtb, alog, dskip, nw)


def _adamw(w, g, m, v, name):
    rows, cols = w.shape
    tr = rows
    for cand in (512, 256, 128, 64, 32, 16, 8):
        if rows % cand == 0 and cand * cols * 4 <= 2 * 1024 * 1024:
            tr = cand
            break
    c1 = 1.0 - ADAM_B1 ** ADAM_STEP
    c2 = 1.0 - ADAM_B2 ** ADAM_STEP

    def body(w_ref, g_ref, m_ref, v_ref, d_ref, mo_ref, vo_ref):
        gv = g_ref[...]
        mn = ADAM_B1 * m_ref[...] + (1.0 - ADAM_B1) * gv
        vn = ADAM_B2 * v_ref[...] + (1.0 - ADAM_B2) * (gv * gv)
        mo_ref[...] = mn
        vo_ref[...] = vn
        d_ref[...] = -ADAM_LR * ((mn / c1) / (jnp.sqrt(vn / c2) + ADAM_EPS) + ADAM_WD * w_ref[...])

    spec = pl.BlockSpec((tr, cols), lambda i: (i, 0))
    shp = jax.ShapeDtypeStruct((rows, cols), F32)
    return pl.pallas_call(body, out_shape=(shp, shp, shp), grid=(rows // tr,), in_specs=[spec] * 4,
                          out_specs=(spec,) * 3, compiler_params=_params(("parallel",)), name=name)(w, g, m, v)


def _pick_rows(rows, row_bytes, limit=1 << 20):
    for cand in (2048, 1024, 512, 256, 128, 64, 32, 16):
        if rows % cand == 0 and cand * row_bytes <= limit:
            return cand
    return rows


def _as3d(a, lead):
    return a.reshape(a.shape[:lead] + (-1, a.shape[-1]))


def _pair_sum(g, got, core, name):
    h = got.shape[0]
    g3, got3 = _as3d(g, 1), _as3d(got, 1)
    _, rows, cols = got3.shape
    tr = _pick_rows(rows, cols * 4)

    def body(c_ref, g_ref, r_ref, o_ref):
        o_ref[...] = (g_ref[...] + r_ref[...]).astype(BF16)

    out = pl.pallas_call(
        body, out_shape=jax.ShapeDtypeStruct(got3.shape, BF16),
        grid_spec=pltpu.PrefetchScalarGridSpec(
            num_scalar_prefetch=1, grid=(h, rows // tr),
            in_specs=[pl.BlockSpec((None, tr, cols), lambda l, i, c_ref: (c_ref[0] * h + l, i, 0)),
                      pl.BlockSpec((None, tr, cols), lambda l, i, c_ref: (l, i, 0))],
            out_specs=pl.BlockSpec((None, tr, cols), lambda l, i, c_ref: (l, i, 0))),
        compiler_params=_params(("parallel", "parallel")), name=name)(core, g3, got3)
    return out.reshape(got.shape)


def _sum4(q, name):
    q4 = _as3d(q, 2)
    _, h, rows, cols = q4.shape
    tr = _pick_rows(rows, cols * 4)

    def body(q0, q1, q2, q3, o_ref):
        o_ref[...] = ((q0[...].astype(F32) + q1[...].astype(F32)) + q2[...].astype(F32)) + q3[...].astype(F32)

    out = pl.pallas_call(
        body, out_shape=jax.ShapeDtypeStruct((h, rows, cols), F32), grid=(h, rows // tr),
        in_specs=[pl.BlockSpec((None, None, tr, cols), lambda l, i, k=k: (k, l, i, 0)) for k in range(N_CHIPS)],
        out_specs=pl.BlockSpec((None, tr, cols), lambda l, i: (l, i, 0)),
        compiler_params=_params(("parallel", "parallel")), name=name)(q4, q4, q4, q4)
    return out.reshape(q.shape[1:])


def _coords():
    return lax.axis_index("x"), lax.axis_index("y"), lax.axis_index("c")


def _other_chips(x, y):
    return [(1 - x, y), (x, 1 - y), (1 - x, 1 - y)]


def _allgather_halves(src, name):
    rows, cols = src.shape

    def body(x_ref, o_ref, send, recv, local):
        x, y, c = _coords()
        sib = (x, y, 1 - c)
        chips = _other_chips(x, y)

        def slot(h, cx, cy):
            return o_ref.at[h, 2 * cx + cy]

        def copy(kk, dst, to, src_ref):
            return pltpu.make_async_remote_copy(src_ref=src_ref, dst_ref=dst, send_sem=send.at[kk],
                                                recv_sem=recv.at[kk], device_id=to, device_id_type=MESH)

        mine = pltpu.make_async_copy(x_ref, slot(c, x, y), local)
        mine.start()
        first = [copy(0, slot(c, x, y), sib, x_ref)]
        first += [copy(1 + j, slot(c, x, y), (*chip, c), x_ref) for j, chip in enumerate(chips)]
        for cp in first:
            cp.start()
        passed = [copy(4 + j, slot(c, *chip), sib, slot(c, *chip)) for j, chip in enumerate(chips)]
        for j, chip in enumerate(chips):
            copy(1 + j, slot(c, *chip), (x, y, c), x_ref).wait_recv()
            passed[j].start()
        copy(0, slot(1 - c, x, y), (x, y, c), x_ref).wait_recv()
        for j, chip in enumerate(chips):
            copy(4 + j, slot(1 - c, *chip), (x, y, c), x_ref).wait_recv()
        for cp in first + passed:
            cp.wait_send()
        mine.wait()

    return pl.pallas_call(
        body, out_shape=jax.ShapeDtypeStruct((2, N_CHIPS, rows, cols), src.dtype),
        in_specs=[ANY], out_specs=ANY,
        scratch_shapes=[pltpu.SemaphoreType.DMA((7,)), pltpu.SemaphoreType.DMA((7,)), pltpu.SemaphoreType.DMA],
        name=name)(src)


BIGW = (("ssd_w_in", None), ("ssd_w_out", 0), ("pool_w", 1), ("ffn_w_up", 1), ("ffn_w_down", 0))


def _chip_window(axis, ref, layers, k):
    if axis is None:
        return ref.at[layers, k]
    n = ref.shape[1 + axis] // N_CHIPS
    sl = pl.ds(pl.multiple_of(k * n, LANES if 1 + axis == len(ref.shape) - 1 else 8), n)
    idx = [layers] + [slice(None)] * (len(ref.shape) - 1)
    idx[1 + axis] = sl
    return ref.at[tuple(idx)]


def _full_shape(axis, shard_shape):
    if axis is None:
        return (shard_shape[0], N_CHIPS) + tuple(shard_shape[1:])
    full = list(shard_shape)
    full[1 + axis] *= N_CHIPS
    return tuple(full)


def _gather_big(shards, name):
    n = len(BIGW)

    def body(*refs):
        ins, outs = refs[:n], refs[n:2 * n]
        send, recv, local = refs[2 * n:]
        x, y, c = _coords()
        me = 2 * x + y
        sib = (x, y, 1 - c)
        chips = _other_chips(x, y)
        plan = []
        for w, (_, axis) in enumerate(BIGW):
            h = ins[w].shape[0] // 2
            mine, theirs = pl.ds(c * h, h), pl.ds((1 - c) * h, h)
            src = ins[w].at[mine]

            def dst(layers, k, w=w, axis=axis):
                return _chip_window(axis, outs[w], layers, k)

            def copy(kk, d, to, s, w=w):
                return pltpu.make_async_remote_copy(src_ref=s, dst_ref=d, send_sem=send.at[7 * w + kk],
                                                    recv_sem=recv.at[7 * w + kk], device_id=to, device_id_type=MESH)

            lc = pltpu.make_async_copy(src, dst(mine, me), local.at[w])
            lc.start()
            first = [copy(0, dst(mine, me), sib, src)]
            first += [copy(1 + j, dst(mine, me), (cx, cy, c), src) for j, (cx, cy) in enumerate(chips)]
            for cp in first:
                cp.start()
            plan.append((lc, first, dst, copy, mine, theirs, src))
        sent = []
        for lc, first, dst, copy, mine, theirs, src in plan:
            for j, (cx, cy) in enumerate(chips):
                blk = dst(mine, 2 * cx + cy)
                copy(1 + j, blk, (x, y, c), src).wait_recv()
                fwd = copy(4 + j, blk, sib, blk)
                fwd.start()
                sent.append(fwd)
        for lc, first, dst, copy, mine, theirs, src in plan:
            copy(0, dst(theirs, me), (x, y, c), src).wait_recv()
            for j, (cx, cy) in enumerate(chips):
                copy(4 + j, dst(theirs, 2 * cx + cy), (x, y, c), src).wait_recv()
            for cp in first:
                cp.wait_send()
            lc.wait()
        for cp in sent:
            cp.wait_send()

    outs = pl.pallas_call(
        body,
        out_shape=tuple(jax.ShapeDtypeStruct(_full_shape(axis, s.shape), s.dtype) for s, (_, axis) in zip(shards, BIGW)),
        in_specs=[ANY] * n, out_specs=(ANY,) * n,
        scratch_shapes=[pltpu.SemaphoreType.DMA((7 * n,)), pltpu.SemaphoreType.DMA((7 * n,)),
                        pltpu.SemaphoreType.DMA((n,))],
        name=name)(*shards)
    return list(outs)


def _swap_grads(gs, name):
    n = len(gs)

    def body(*refs):
        ins, outs = refs[:n], refs[n:2 * n]
        send, recv = refs[2 * n:]
        x, y, c = _coords()
        cps = []
        for w in range(n):
            h = ins[w].shape[0] // 2
            cp = pltpu.make_async_remote_copy(src_ref=ins[w].at[pl.ds((1 - c) * h, h)], dst_ref=outs[w],
                                              send_sem=send.at[w], recv_sem=recv.at[w],
                                              device_id=(x, y, 1 - c), device_id_type=MESH)
            cp.start()
            cps.append(cp)
        for cp in cps:
            cp.wait()

    outs = pl.pallas_call(
        body, out_shape=tuple(jax.ShapeDtypeStruct((g.shape[0] // 2,) + g.shape[1:], g.dtype) for g in gs),
        in_specs=[ANY] * n, out_specs=(ANY,) * n,
        scratch_shapes=[pltpu.SemaphoreType.DMA((n,)), pltpu.SemaphoreType.DMA((n,))], name=name)(*gs)
    return list(outs)


def _scatter_grads(ps, name):
    n = len(ps)

    def shard_shape(p, axis):
        if axis is None:
            return (p.shape[0],) + p.shape[2:]
        s = list(p.shape)
        s[1 + axis] //= N_CHIPS
        return tuple(s)

    def body(*refs):
        ins, outs = refs[:n], refs[n:2 * n]
        send, recv, local = refs[2 * n:]
        x, y, c = _coords()
        me = 2 * x + y
        chips = _other_chips(x, y)
        cps, lcs = [], []
        for w, (_, axis) in enumerate(BIGW):
            layers = pl.ds(0, ins[w].shape[0])
            lc = pltpu.make_async_copy(_chip_window(axis, ins[w], layers, me), outs[w].at[me], local.at[w])
            lc.start()
            lcs.append(lc)
            for j, (cx, cy) in enumerate(chips):
                cp = pltpu.make_async_remote_copy(src_ref=_chip_window(axis, ins[w], layers, 2 * cx + cy),
                                                  dst_ref=outs[w].at[me], send_sem=send.at[3 * w + j],
                                                  recv_sem=recv.at[3 * w + j], device_id=(cx, cy, c), device_id_type=MESH)
                cp.start()
                cps.append(cp)
        for w, (_, axis) in enumerate(BIGW):
            layers = pl.ds(0, ins[w].shape[0])
            for j, (cx, cy) in enumerate(chips):
                pltpu.make_async_remote_copy(src_ref=_chip_window(axis, ins[w], layers, me), dst_ref=outs[w].at[2 * cx + cy],
                                             send_sem=send.at[3 * w + j], recv_sem=recv.at[3 * w + j],
                                             device_id=(x, y, c), device_id_type=MESH).wait_recv()
        for cp in cps:
            cp.wait_send()
        for lc in lcs:
            lc.wait()

    outs = pl.pallas_call(
        body,
        out_shape=tuple(jax.ShapeDtypeStruct((N_CHIPS,) + shard_shape(p, axis), p.dtype) for p, (_, axis) in zip(ps, BIGW)),
        in_specs=[ANY] * n, out_specs=(ANY,) * n,
        scratch_shapes=[pltpu.SemaphoreType.DMA((3 * n,)), pltpu.SemaphoreType.DMA((3 * n,)),
                        pltpu.SemaphoreType.DMA((n,))],
        name=name)(*ps)
    return list(outs)


def _share_grads(ss, name):
    n = len(ss)

    def body(*refs):
        ins, outs = refs[:n], refs[n:2 * n]
        send, recv, local = refs[2 * n:]
        x, y, c = _coords()
        cps, lcs = [], []
        for w in range(n):
            h = ins[w].shape[0]
            mine = pl.ds(c * h, h)
            lc = pltpu.make_async_copy(ins[w], outs[w].at[mine], local.at[w])
            lc.start()
            lcs.append(lc)
            cp = pltpu.make_async_remote_copy(src_ref=ins[w], dst_ref=outs[w].at[mine], send_sem=send.at[w],
                                              recv_sem=recv.at[w], device_id=(x, y, 1 - c), device_id_type=MESH)
            cp.start()
            cps.append(cp)
        for w in range(n):
            h = ins[w].shape[0]
            pltpu.make_async_remote_copy(src_ref=ins[w], dst_ref=outs[w].at[pl.ds((1 - c) * h, h)], send_sem=send.at[w],
                                         recv_sem=recv.at[w], device_id=(x, y, c), device_id_type=MESH).wait_recv()
        for cp in cps:
            cp.wait_send()
        for lc in lcs:
            lc.wait()

    outs = pl.pallas_call(
        body, out_shape=tuple(jax.ShapeDtypeStruct((2 * s.shape[0],) + s.shape[1:], s.dtype) for s in ss),
        in_specs=[ANY] * n, out_specs=(ANY,) * n,
        scratch_shapes=[pltpu.SemaphoreType.DMA((n,)), pltpu.SemaphoreType.DMA((n,)), pltpu.SemaphoreType.DMA((n,))],
        name=name)(*ss)
    return list(outs)


def _allreduce_small(vec, name):
    rows, cols = vec.shape

    def body(x_ref, o_ref, buf, send, recv):
        x, y, c = _coords()
        me = 4 * x + 2 * y + c
        buf[me] = x_ref[...]
        cps = []
        for kk in range(1, 8):
            dx, dy, dc = (kk >> 2) & 1, (kk >> 1) & 1, kk & 1
            to = (1 - x if dx else x, 1 - y if dy else y, 1 - c if dc else c)
            cp = pltpu.make_async_remote_copy(src_ref=x_ref, dst_ref=buf.at[me], send_sem=send.at[kk - 1],
                                              recv_sem=recv.at[kk - 1], device_id=to, device_id_type=MESH)
            cp.start()
            cps.append((cp, 4 * to[0] + 2 * to[1] + to[2]))
        for kk, (cp, frm) in enumerate(cps):
            pltpu.make_async_remote_copy(src_ref=x_ref, dst_ref=buf.at[frm], send_sem=send.at[kk],
                                         recv_sem=recv.at[kk], device_id=(x, y, c), device_id_type=MESH).wait_recv()
        for cp, _ in cps:
            cp.wait_send()
        acc = buf[0]
        for kk in range(1, 8):
            acc = acc + buf[kk]
        o_ref[...] = acc

    vm = pl.BlockSpec(memory_space=pltpu.VMEM)
    return pl.pallas_call(
        body, out_shape=jax.ShapeDtypeStruct((rows, cols), F32), in_specs=[vm], out_specs=vm,
        scratch_shapes=[pltpu.VMEM((8, rows, cols), F32), pltpu.SemaphoreType.DMA((7,)), pltpu.SemaphoreType.DMA((7,))],
        compiler_params=_params(), name=name)(vec)


SMALL = (("ssd_conv_w", 2), ("pool_scale", 1), ("ffn_conv_w", 2))
REPL = ("ssd_conv_b", "ssd_dt_bias", "ssd_a_log", "ssd_d", "ssd_norm_w", "ffn_conv_b",
        "norm_mix_pre", "norm_mix_post", "norm_ffn_pre", "norm_ffn_post")
WEIGHTS = ("ssd_w_in", "ssd_conv_w", "ssd_conv_b", "ssd_dt_bias", "ssd_a_log", "ssd_d", "ssd_norm_w", "ssd_w_out",
           "pool_w", "pool_scale", "ffn_w_up", "ffn_conv_w", "ffn_conv_b", "ffn_w_down", "norm_mix_pre",
           "norm_mix_post", "norm_ffn_pre", "norm_ffn_post")


def _flat_rows(n):
    unit = 2 * 16 * FLAT_COLS
    return 2 * 16 * ((n + unit - 1) // unit)


def _flatten_shards(arrs, dtype):
    flat = jnp.concatenate([a.astype(dtype).reshape(-1) for a in arrs])
    rows = _flat_rows(flat.shape[0])
    flat = jnp.pad(flat, (0, rows * FLAT_COLS - flat.shape[0]))
    return flat.reshape(2, rows // 2, FLAT_COLS)


def _unflatten_full(gathered, shard_shapes, axes):
    per_chip = jnp.swapaxes(gathered, 0, 1).reshape(N_CHIPS, -1)
    out, off = [], 0
    for shp, ax in zip(shard_shapes, axes):
        n = math.prod(shp)
        pieces = [per_chip[k, off:off + n].reshape(shp) for k in range(N_CHIPS)]
        out.append(jnp.concatenate(pieces, axis=ax))
        off += n
    return out


def kernel(x, ssd_w_in, ssd_conv_w, ssd_conv_b, ssd_dt_bias, ssd_a_log, ssd_d, ssd_norm_w, ssd_w_out, pool_w, pool_scale, ffn_w_up, ffn_conv_w, ffn_conv_b, ffn_w_down, norm_mix_pre, norm_mix_post, norm_ffn_pre, norm_ffn_post, loss_target, m_ssd_w_in, m_ssd_conv_w, m_ssd_conv_b, m_ssd_dt_bias, m_ssd_a_log, m_ssd_d, m_ssd_norm_w, m_ssd_w_out, m_pool_w, m_pool_scale, m_ffn_w_up, m_ffn_conv_w, m_ffn_conv_b, m_ffn_w_down, m_norm_mix_pre, m_norm_mix_post, m_norm_ffn_pre, m_norm_ffn_post, v_ssd_w_in, v_ssd_conv_w, v_ssd_conv_b, v_ssd_dt_bias, v_ssd_a_log, v_ssd_d, v_ssd_norm_w, v_ssd_w_out, v_pool_w, v_pool_scale, v_ffn_w_up, v_ffn_conv_w, v_ffn_conv_b, v_ffn_w_down, v_norm_mix_pre, v_norm_mix_post, v_norm_ffn_pre, v_norm_ffn_post):
    wts = dict(ssd_w_in=ssd_w_in, ssd_conv_w=ssd_conv_w, ssd_conv_b=ssd_conv_b, ssd_dt_bias=ssd_dt_bias,
               ssd_a_log=ssd_a_log, ssd_d=ssd_d, ssd_norm_w=ssd_norm_w, ssd_w_out=ssd_w_out, pool_w=pool_w,
               pool_scale=pool_scale, ffn_w_up=ffn_w_up, ffn_conv_w=ffn_conv_w, ffn_conv_b=ffn_conv_b,
               ffn_w_down=ffn_w_down, norm_mix_pre=norm_mix_pre, norm_mix_post=norm_mix_post,
               norm_ffn_pre=norm_ffn_pre, norm_ffn_post=norm_ffn_post)
    mom = dict(ssd_w_in=m_ssd_w_in, ssd_conv_w=m_ssd_conv_w, ssd_conv_b=m_ssd_conv_b, ssd_dt_bias=m_ssd_dt_bias,
               ssd_a_log=m_ssd_a_log, ssd_d=m_ssd_d, ssd_norm_w=m_ssd_norm_w, ssd_w_out=m_ssd_w_out, pool_w=m_pool_w,
               pool_scale=m_pool_scale, ffn_w_up=m_ffn_w_up, ffn_conv_w=m_ffn_conv_w, ffn_conv_b=m_ffn_conv_b,
               ffn_w_down=m_ffn_w_down, norm_mix_pre=m_norm_mix_pre, norm_mix_post=m_norm_mix_post,
               norm_ffn_pre=m_norm_ffn_pre, norm_ffn_post=m_norm_ffn_post)
    var = dict(ssd_w_in=v_ssd_w_in, ssd_conv_w=v_ssd_conv_w, ssd_conv_b=v_ssd_conv_b, ssd_dt_bias=v_ssd_dt_bias,
               ssd_a_log=v_ssd_a_log, ssd_d=v_ssd_d, ssd_norm_w=v_ssd_norm_w, ssd_w_out=v_ssd_w_out, pool_w=v_pool_w,
               pool_scale=v_pool_scale, ffn_w_up=v_ffn_w_up, ffn_conv_w=v_ffn_conv_w, ffn_conv_b=v_ffn_conv_b,
               ffn_w_down=v_ffn_w_down, norm_mix_pre=v_norm_mix_pre, norm_mix_post=v_norm_mix_post,
               norm_ffn_pre=v_norm_ffn_pre, norm_ffn_post=v_norm_ffn_post)

    bl, seq, d = x.shape
    t = bl * seq
    depth = norm_mix_pre.shape[0]
    n_ssd = ssd_w_out.shape[0]
    d_inner = ssd_w_out.shape[1] * N_CHIPS
    nheads = d_inner // HEAD_DIM
    hpg = nheads // N_GROUPS
    gw = d_inner // N_GROUPS
    xbc = ssd_conv_w.shape[2] * N_CHIPS
    f2 = ffn_w_up.shape[2] * N_CHIPS
    ff = f2 // 2
    dg = d // 4
    cy = lax.axis_index("c")
    chip = 2 * lax.axis_index("x") + lax.axis_index("y")

    small_shapes = [wts[n].shape for n, _ in SMALL]
    small_axes = [a for _, a in SMALL]
    small_flat = _flatten_shards([wts[n] for n, _ in SMALL], F32)
    small_half = lax.dynamic_index_in_dim(small_flat, cy, 0, keepdims=False)
    small_all = _allgather_halves(small_half, "gather_small")
    conv_w, p_scale, f_conv_w = _unflatten_full(small_all, small_shapes, small_axes)
    w_in_cm, w_out, w_pool, w_up, w_down = _gather_big([wts[n].astype(BF16) for n, _ in BIGW], "gather_big")
    w_in = jnp.swapaxes(w_in_cm, 1, 2).reshape(n_ssd, d, -1)

    def pad_heads(a):
        lead = a.shape[:-1]
        a = a.reshape(lead + (N_GROUPS, hpg))
        a = jnp.pad(a, [(0, 0)] * len(lead) + [(0, 0), (0, LANES - hpg)])
        return a.reshape(lead + (N_GROUPS * LANES,))

    def unpad_heads(a):
        lead = a.shape[:-1]
        return a.reshape(lead + (N_GROUPS, LANES))[..., :hpg].reshape(lead + (nheads,))

    def group_rows(a, width):
        return jnp.broadcast_to(a.reshape(N_GROUPS, 1, width), (N_GROUPS, 8, width))

    w_in_p = jnp.concatenate([w_in[..., :d_inner + xbc], pad_heads(w_in[..., d_inner + xbc:])], axis=-1)
    zw = w_in_p.shape[-1]

    x2 = x.reshape(t, d)
    tgt2 = loss_target.reshape(t, d)

    saved = []
    cur = x2
    for i in range(depth):
        j = i // 2
        sv = dict(x_in=cur)
        if i % 2 == 0:
            h = _norm_fwd(cur, norm_mix_pre[i:i + 1], BF16, f"norm_pre_b")
            zx = _mm(h, w_in_p, "nn", F32, "mm_ssd_in", 2048, 512, d, b_layer=j).reshape(bl, seq, zw)
            xc = _ssd_conv_fwd(zx, conv_w[j], ssd_conv_b[j:j + 1], d_inner, "ssd_conv_fwd")
            dtb = group_rows(pad_heads(ssd_dt_bias[j]), LANES)
            alog = group_rows(pad_heads(ssd_a_log[j]), LANES)
            dskip = group_rows(jnp.repeat(ssd_d[j], HEAD_DIM), gw)
            nw = group_rows(ssd_norm_w[j], gw)
            y, yn, st = _ssd_fwd(xc, zx, dtb, alog, dskip, nw, d_inner, "ssd_fwd")
            mix = _mm(yn.reshape(t, d_inner), w_out, "nn", F32, "mm_ssd_out", 512, 512, d_inner, b_layer=j)
            sv.update(h=h, zx=zx, xc=xc, y=y, yn=yn, st=st, dtb=dtb, alog=alog, dskip=dskip, nw=nw)
        else:
            h = _norm_fwd(cur, norm_mix_pre[i:i + 1], F32, "norm_pre_f")
            mix = _pool_fwd(h.reshape(bl, seq, d), w_pool[j], p_scale[j:j + 1], "pool_fwd").reshape(t, d)
            sv.update(h=h)
        sv.update(mix=mix)
        mid = _norm_fwd(mix, norm_mix_post[i:i + 1], F32, "norm_post", resid=cur)
        u = _norm_fwd(mid, norm_ffn_pre[i:i + 1], BF16, "norm_pre_b")
        hpre = _mm(u, w_up, "nn", BF16, "mm_up", 2048, 512, d, b_layer=i).reshape(bl, seq, f2)
        act = _ffn_act_fwd(hpre, f_conv_w[i], ffn_conv_b[i:i + 1], "ffn_act_fwd").reshape(t, ff)
        fo = _mm(act, w_down, "nn", F32, "mm_down", 1024, 512, ff, b_layer=i)
        cur = _norm_fwd(fo, norm_ffn_post[i:i + 1], F32, "norm_post", resid=mid)
        sv.update(mid=mid, u=u, hpre=hpre, act=act, fo=fo)
        saved.append(sv)

    dcur, loss_part = _loss_head(cur, tgt2, "loss_head")

    g = {n: [None] * wts[n].shape[0] for n in WEIGHTS}
    g_w_up = lax.empty((depth, d, f2), F32)
    g_w_down = lax.empty((depth, ff, d), F32)
    g_w_out = lax.empty((n_ssd, d_inner, d), F32)
    g_w_in = lax.empty((n_ssd, d, zw), F32)
    for i in reversed(range(depth)):
        j = i // 2
        sv = saved[i]
        dfo, g["norm_ffn_post"][i] = _norm_bwd(sv["fo"], norm_ffn_post[i:i + 1], dcur, BF16, "norm_bwd_b")
        dact = _mm(dfo, w_down, "nt", BF16, "mm_down_dx", 1024, ff // 2, d, b_layer=i)
        g_w_down = _mm(sv["act"], dfo, "tn", F32, "mm_down_dw", ff // 2, 512, 2048, out_buf=(g_w_down, i))
        dhg, dhv, dcw, dcb = _ffn_act_bwd(sv["hpre"], dact.reshape(bl, seq, ff), f_conv_w[i], ffn_conv_b[i:i + 1],
                                          "ffn_act_bwd")
        g["ffn_conv_w"][i] = dcw
        g["ffn_conv_b"][i] = dcb[0]
        dhs = [dhg.reshape(t, ff), dhv.reshape(t, ff)]
        du = _mm(dhs, w_up, "nt", F32, "mm_up_dx", 1024, d, ff // 2, b_layer=i)
        g_w_up = _mm(sv["u"], dhs, "tn", F32, "mm_up_dw", 512, ff // 2, 2048, out_buf=(g_w_up, i))
        dmid, g["norm_ffn_pre"][i] = _norm_bwd(sv["mid"], norm_ffn_pre[i:i + 1], du, F32, "norm_bwd_r", resid=dcur)
        if i % 2 == 0:
            dmix, g["norm_mix_post"][i] = _norm_bwd(sv["mix"], norm_mix_post[i:i + 1], dmid, BF16, "norm_bwd_b")
            dyn = _mm(dmix, w_out, "nt", F32, "mm_ssd_out_dx", 1024, 1024, d, b_layer=j)
            g_w_out = _mm(sv["yn"].reshape(t, d_inner), dmix, "tn", F32, "mm_ssd_out_dw", 1024, 512, 2048,
                          out_buf=(g_w_out, j))
            dz, dxs, dbm, dcm, ddt, dnw, dd, dal, dbias = _ssd_bwd(
                sv["xc"], sv["zx"], sv["y"], dyn.reshape(bl, seq, d_inner), sv["st"], sv["dtb"], sv["alog"],
                sv["dskip"], sv["nw"], d_inner, "ssd_bwd")
            g["ssd_norm_w"][j] = dnw[:, 0, :].reshape(d_inner)
            g["ssd_d"][j] = dd[:, 0, :hpg].reshape(nheads)
            g["ssd_a_log"][j] = dal[:, 0, :hpg].reshape(nheads)
            g["ssd_dt_bias"][j] = dbias[:, 0, :hpg].reshape(nheads)
            dxbc, dcw, dcb = _ssd_conv_bwd(sv["zx"], (dxs, dbm, dcm), conv_w[j], ssd_conv_b[j:j + 1], d_inner,
                                           "ssd_conv_bwd")
            g["ssd_conv_w"][j] = dcw
            g["ssd_conv_b"][j] = dcb[0]
            dzs = [dz.reshape(t, d_inner), dxbc.reshape(t, xbc), ddt.reshape(t, N_GROUPS * LANES)]
            dh = _mm(dzs, w_in_p, "nt", F32, "mm_ssd_in_dx", 1024, d, 512, b_layer=j)
            g_w_in = _mm(sv["h"], dzs, "tn", F32, "mm_ssd_in_dw", 1024, 512, 2048, out_buf=(g_w_in, j))
        else:
            dmix, g["norm_mix_post"][i] = _norm_bwd(sv["mix"], norm_mix_post[i:i + 1], dmid, F32, "norm_bwd_f")
            dh3, g["pool_w"][j], dps = _pool_bwd(sv["h"].reshape(bl, seq, d), dmix.reshape(bl, seq, d), w_pool[j],
                                                 p_scale[j:j + 1], "pool_bwd")
            g["pool_scale"][j] = dps[0]
            dh = dh3.reshape(t, d)
        dcur, g["norm_mix_pre"][i] = _norm_bwd(sv["x_in"], norm_mix_pre[i:i + 1], dh, F32, "norm_bwd_r", resid=dmid)

    grad_x = dcur.reshape(bl, seq, d)
    for n in ("norm_mix_pre", "norm_mix_post", "norm_ffn_pre", "norm_ffn_post"):
        g[n] = [a[0] for a in g[n]]
    small_names = [n for n, _ in SMALL] + list(REPL)
    full = {n: jnp.stack(g[n], axis=0) for n in small_names}

    g_in = jnp.concatenate([g_w_in[..., :d_inner + xbc], unpad_heads(g_w_in[..., d_inner + xbc:])], axis=-1)
    g_in_cm = jnp.swapaxes(g_in.reshape(n_ssd, d, N_CHIPS, -1), 1, 2)
    gs = [g_in_cm, g_w_out, jnp.stack(g["pool_w"], axis=0), g_w_up, g_w_down]
    core = cy.reshape(1).astype(jnp.int32)
    got = _swap_grads(gs, "swap_grads")
    pair = [_pair_sum(a, r, core, "pair_sum_" + n) for a, r, (n, _) in zip(gs, got, BIGW)]
    parts = _scatter_grads(pair, "scatter_grads")
    halves = [_sum4(q, "sum4_" + n) for q, (n, _) in zip(parts, BIGW)]
    shards = _share_grads(halves, "share_grads")
    big_grads = {n: s for s, (n, _) in zip(shards, BIGW)}

    vec = jnp.concatenate([full[n].reshape(-1) for n in small_names] + [loss_part[0, :1]])
    nvec = vec.shape[0]
    vrows = 8 * ((nvec + 8 * FLAT_COLS - 1) // (8 * FLAT_COLS))
    vec = jnp.pad(vec, (0, vrows * FLAT_COLS - nvec)).reshape(vrows, FLAT_COLS)
    tot = _allreduce_small(vec, "allreduce_small").reshape(-1)
    small_grads, off = {}, 0
    for n in small_names:
        cnt = math.prod(full[n].shape)
        small_grads[n] = tot[off:off + cnt].reshape(full[n].shape)
        off += cnt
    loss = tot[off]
    for n, ax in SMALL:
        w = wts[n].shape[ax]
        small_grads[n] = lax.dynamic_slice_in_dim(small_grads[n], chip * w, w, axis=ax)

    grads, deltas, new_m, new_v = {}, {}, {}, {}
    for n in WEIGHTS:
        gr = big_grads[n] if n in big_grads else small_grads[n]
        shp = wts[n].shape
        two = (math.prod(shp[:-1]), shp[-1])
        dl, mn, vn = _adamw(wts[n].reshape(two), gr.reshape(two), mom[n].reshape(two), var[n].reshape(two),
                            "adamw_" + n)
        grads[n], deltas[n], new_m[n], new_v[n] = gr, dl.reshape(shp), mn.reshape(shp), vn.reshape(shp)

    return (loss, grad_x, *[grads[n] for n in WEIGHTS], *[deltas[n] for n in WEIGHTS],
            *[new_m[n] for n in WEIGHTS], *[new_v[n] for n in WEIGHTS])
```

```python
import functools
import math

import jax
import jax.numpy as jnp
from jax import lax
from jax.experimental import pallas as pl
from jax.experimental.pallas import tpu as pltpu

F32 = jnp.float32
BF16 = jnp.bfloat16
MESH = pl.DeviceIdType.MESH
ANY = pl.BlockSpec(memory_space=pl.ANY)

HEAD_DIM = 64
D_STATE = 128
CHUNK = 128
N_GROUPS = 4
SSD_CONV = 4
FFN_CONV = 3
EPS = 1e-6
N_CHIPS = 4
LANES = 128
FLAT_COLS = 1024

ADAM_LR = 0.001
ADAM_B1 = 0.9
ADAM_B2 = 0.999
ADAM_EPS = 1e-08
ADAM_WD = 0.01
ADAM_STEP = 10

VMEM_LIMIT_BYTES = 56 * 1024 * 1024


def _params(sem=None):
    kw = dict(vmem_limit_bytes=VMEM_LIMIT_BYTES)
    if sem is not None:
        kw["dimension_semantics"] = sem
    return pltpu.CompilerParams(**kw)


def _sigmoid(x):
    return 1.0 / (1.0 + jnp.exp(-x))


def _softplus(x):
    return jnp.maximum(x, 0.0) + jnp.log(1.0 + jnp.exp(-jnp.abs(x)))


def _dot(a, b, dn):
    return lax.dot_general(a, b, (dn, ((), ())), preferred_element_type=F32)


def _nn(a, b):
    return _dot(a, b, ((1,), (0,)))


def _nt(a, b):
    return _dot(a, b, ((1,), (1,)))


def _tn(a, b):
    return _dot(a, b, ((0,), (0,)))


def _split(x, parts):
    out = []
    r = x
    for _ in range(parts):
        p = r.astype(BF16)
        out.append(p)
        r = r - p.astype(F32)
    return out


def _sel_left(sel, x, parts=3):
    n = x.shape[1]
    r = _nn(sel, jnp.concatenate(_split(x, parts), axis=1))
    out = r[:, 0:n]
    for i in range(1, parts):
        out = out + r[:, i * n:(i + 1) * n]
    return out


def _sel_right(x, sel_stacked, parts=3):
    return _nn(jnp.concatenate(_split(x, parts), axis=1), sel_stacked)


def _mm(a, b, dims, out_dtype, name, tm, tn, tk, b_layer=None, out_buf=None):
    a_list = list(a) if isinstance(a, (list, tuple)) else [a]
    b_list = list(b) if isinstance(b, (list, tuple)) else [b]
    if dims in ("nn", "nt"):
        assert len(b_list) == 1
        m = a_list[0].shape[0]
        segs = [x.shape[1] for x in a_list]
        k = sum(segs)
        bshape = b_list[0].shape[-2:]
        n = bshape[1] if dims == "nn" else bshape[0]
        assert (bshape[0] if dims == "nn" else bshape[1]) == k
    else:
        assert len(a_list) == 1 and b_layer is None
        k, m = a_list[0].shape
        segs = [x.shape[1] for x in b_list]
        n = sum(segs)
    tm, tn, tk = min(tm, m), min(tn, n), min(tk, k)
    if dims == "tn":
        tn = min(tn, min(segs))
    else:
        tk = min(tk, min(segs))
    unit = tk if dims != "tn" else tn
    assert m % tm == 0 and n % tn == 0 and k % tk == 0 and all(s % unit == 0 for s in segs), (name, m, n, k, segs)
    nk = k // tk
    starts = [sum(segs[:s]) // unit for s in range(len(segs))]
    counts = [s // unit for s in segs]
    nseg = len(segs)
    dn = {"nn": ((1,), (0,)), "nt": ((1,), (1,)), "tn": ((0,), (0,))}[dims]

    def body(*refs):
        a_refs = refs[:len(a_list)]
        b_refs = refs[len(a_list):len(a_list) + len(b_list)]
        rest = refs[len(a_list) + len(b_list) + (0 if out_buf is None else 1):]
        o_ref = rest[0]
        acc = rest[1] if nk > 1 else None
        kk = pl.program_id(2)
        sel = kk if dims != "tn" else pl.program_id(1)

        def step(a_ref, b_ref):
            p = _dot(a_ref[...].astype(BF16), b_ref[...].astype(BF16), dn)
            if nk == 1:
                o_ref[...] = p.astype(out_dtype)
                return

            @pl.when(kk == 0)
            def _():
                acc[...] = p

            @pl.when(kk > 0)
            def _():
                acc[...] += p

        if nseg == 1:
            step(a_refs[0], b_refs[0])
        else:
            for s in range(nseg):
                @pl.when(jnp.logical_and(sel >= starts[s], sel < starts[s] + counts[s]))
                def _(s=s):
                    step(a_refs[s] if dims != "tn" else a_refs[0], b_refs[0] if dims != "tn" else b_refs[s])

        if nk > 1:
            @pl.when(kk == nk - 1)
            def _():
                o_ref[...] = acc[...].astype(out_dtype)

    def seg_index(v, s):
        return v if nseg == 1 else jnp.clip(v - starts[s], 0, counts[s] - 1)

    lead = () if b_layer is None else (b_layer,)
    none = () if b_layer is None else (None,)
    if dims == "nn":
        a_specs = [pl.BlockSpec((tm, tk), lambda i, j, kk, s=s: (i, seg_index(kk, s))) for s in range(nseg)]
        b_specs = [pl.BlockSpec(none + (tk, tn), lambda i, j, kk: lead + (kk, j))]
    elif dims == "nt":
        a_specs = [pl.BlockSpec((tm, tk), lambda i, j, kk, s=s: (i, seg_index(kk, s))) for s in range(nseg)]
        b_specs = [pl.BlockSpec(none + (tn, tk), lambda i, j, kk: lead + (j, kk))]
    else:
        a_specs = [pl.BlockSpec((tk, tm), lambda i, j, kk: (kk, i))]
        b_specs = [pl.BlockSpec((tk, tn), lambda i, j, kk, s=s: (kk, seg_index(j, s))) for s in range(nseg)]
    args = a_list + b_list
    in_specs = a_specs + b_specs
    aliases = {}
    if out_buf is None:
        out_shape = jax.ShapeDtypeStruct((m, n), out_dtype)
        out_spec = pl.BlockSpec((tm, tn), lambda i, j, kk: (i, j))
    else:
        buf, slab = out_buf
        assert buf.shape[1:] == (m, n) and buf.dtype == out_dtype
        out_shape = jax.ShapeDtypeStruct(buf.shape, out_dtype)
        out_spec = pl.BlockSpec((None, tm, tn), lambda i, j, kk: (slab, i, j))
        aliases = {len(args): 0}
        args = args + [buf]
        in_specs = in_specs + [ANY]
    return pl.pallas_call(
        body,
        out_shape=out_shape,
        grid=(m // tm, n // tn, nk),
        in_specs=in_specs,
        out_specs=out_spec,
        scratch_shapes=[] if nk == 1 else [pltpu.VMEM((tm, tn), F32)],
        input_output_aliases=aliases,
        compiler_params=_params(("parallel", "parallel", "arbitrary")),
        name=name,
    )(*args)


def _row_tile(t, want):
    tm = min(want, t)
    assert t % tm == 0
    return tm


def _norm_fwd(x, w, out_dtype, name, resid=None):
    t, d = x.shape
    tm = _row_tile(t, 512)

    def body(*refs):
        if resid is None:
            x_ref, w_ref, o_ref = refs
        else:
            x_ref, w_ref, r_ref, o_ref = refs
        xv = x_ref[...]
        r = lax.rsqrt(jnp.mean(xv * xv, axis=-1, keepdims=True) + EPS)
        y = (xv * r) * w_ref[...]
        if resid is not None:
            y = r_ref[...] + y
        o_ref[...] = y.astype(out_dtype)

    row = pl.BlockSpec((tm, d), lambda i: (i, 0))
    vec = pl.BlockSpec((1, d), lambda i: (0, 0))
    args = [x, w] + ([] if resid is None else [resid])
    return pl.pallas_call(
        body, out_shape=jax.ShapeDtypeStruct((t, d), out_dtype), grid=(t // tm,),
        in_specs=[row, vec] + ([] if resid is None else [row]), out_specs=row,
        compiler_params=_params(("parallel",)), name=name)(*args)


def _norm_bwd(src, w, dy, out_dtype, name, resid=None):
    t, d = src.shape
    tm = _row_tile(t, 512)

    def body(*refs):
        if resid is None:
            x_ref, w_ref, g_ref, o_ref, dw_ref = refs
        else:
            x_ref, w_ref, g_ref, r_ref, o_ref, dw_ref = refs
        xv = x_ref[...]
        g = g_ref[...].astype(F32)
        r = lax.rsqrt(jnp.mean(xv * xv, axis=-1, keepdims=True) + EPS)
        xh = xv * r
        gh = g * w_ref[...]
        mean = jnp.mean(gh * xh, axis=-1, keepdims=True)
        dx = r * (gh - xh * mean)
        if resid is not None:
            dx = r_ref[...] + dx
        o_ref[...] = dx.astype(out_dtype)
        part = jnp.sum(g * xh, axis=0, keepdims=True)

        @pl.when(pl.program_id(0) == 0)
        def _():
            dw_ref[...] = part

        @pl.when(pl.program_id(0) > 0)
        def _():
            dw_ref[...] += part

    row = pl.BlockSpec((tm, d), lambda i: (i, 0))
    vec = pl.BlockSpec((1, d), lambda i: (0, 0))
    args = [src, w, dy] + ([] if resid is None else [resid])
    return pl.pallas_call(
        body,
        out_shape=(jax.ShapeDtypeStruct((t, d), out_dtype), jax.ShapeDtypeStruct((1, d), F32)),
        grid=(t // tm,),
        in_specs=[row, vec, row] + ([] if resid is None else [row]),
        out_specs=(row, vec),
        compiler_params=_params(("arbitrary",)), name=name)(*args)


def _loss_head(y, target, name):
    t, d = y.shape
    tm = _row_tile(t, 512)

    def body(y_ref, t_ref, dy_ref, l_ref):
        e = y_ref[...] - t_ref[...]
        dy_ref[...] = e * (1.0 / d)
        col = jnp.sum(e * e, axis=0, keepdims=True)
        s = jnp.sum(col, axis=1, keepdims=True) * (0.5 / d)
        part = jnp.broadcast_to(s, (1, LANES))

        @pl.when(pl.program_id(0) == 0)
        def _():
            l_ref[...] = part

        @pl.when(pl.program_id(0) > 0)
        def _():
            l_ref[...] += part

    row = pl.BlockSpec((tm, d), lambda i: (i, 0))
    return pl.pallas_call(
        body,
        out_shape=(jax.ShapeDtypeStruct((t, d), F32), jax.ShapeDtypeStruct((1, LANES), F32)),
        grid=(t // tm,), in_specs=[row, row],
        out_specs=(row, pl.BlockSpec((1, LANES), lambda i: (0, 0))),
        compiler_params=_params(("arbitrary",)), name=name)(y, target)


def _window(ref, c, rows, seq, before, after):
    r0 = pl.multiple_of(c * rows, rows)
    parts = []
    if before:
        h0 = pl.multiple_of(jnp.maximum(r0 - before, 0), before)
        halo = ref[pl.ds(h0, before), :].astype(F32)
        parts.append(jnp.where(c > 0, halo, 0.0))
    parts.append(ref[pl.ds(r0, rows), :].astype(F32))
    if after:
        h1 = pl.multiple_of(jnp.minimum(r0 + rows, seq - after), after)
        halo = ref[pl.ds(h1, after), :].astype(F32)
        parts.append(jnp.where(c < seq // rows - 1, halo, 0.0))
    return parts[0] if len(parts) == 1 else jnp.concatenate(parts, axis=0)


def _lag(x, k):
    return pltpu.roll(x, k, 0) if k else x


def _lead(x, k):
    return pltpu.roll(x, x.shape[0] - k, 0) if k else x


SHIFT_ROWS = 128
SHIFT_COLS = 256


HALO = 16


def _conv3(ext, w, bias):
    acc = bias + w[2:3, :] * ext[HALO:, :]
    acc = acc + w[1:2, :] * _lag(ext, 1)[HALO:, :]
    return acc + w[0:1, :] * _lag(ext, 2)[HALO:, :]


def _ffn_act_fwd(hpre, cw, cb, name):
    b, seq, f2 = hpre.shape
    cbk = SHIFT_COLS
    nj = f2 // (2 * cbk)
    rows = min(SHIFT_ROWS, seq)

    def body(g_ref, v_ref, wg_ref, wv_ref, bg_ref, bv_ref, o_ref):
        def chunk(c, carry):
            gate = _conv3(_window(g_ref, c, rows, seq, HALO, 0), wg_ref[...], bg_ref[...])
            val = _conv3(_window(v_ref, c, rows, seq, HALO, 0), wv_ref[...], bv_ref[...])
            a = gate * _sigmoid(gate) * val
            o_ref[pl.ds(pl.multiple_of(c * rows, rows), rows), :] = a.astype(BF16)
            return carry

        lax.fori_loop(0, seq // rows, chunk, 0)

    blk = lambda off: pl.BlockSpec((None, seq, cbk), lambda i, j: (i, 0, j + off))
    wsp = lambda r, off: pl.BlockSpec((r, cbk), lambda i, j: (0, j + off))
    return pl.pallas_call(
        body, out_shape=jax.ShapeDtypeStruct((b, seq, f2 // 2), BF16), grid=(b, nj),
        in_specs=[blk(0), blk(nj), wsp(FFN_CONV, 0), wsp(FFN_CONV, nj), wsp(1, 0), wsp(1, nj)],
        out_specs=blk(0),
        compiler_params=_params(("parallel", "parallel")), name=name)(hpre, hpre, cw, cw, cb, cb)


def _ffn_act_bwd(hpre, da, cw, cb, name):
    b, seq, f2 = hpre.shape
    cbk = SHIFT_COLS
    nj = f2 // (2 * cbk)
    rows = min(SHIFT_ROWS, seq)

    def body(g_ref, v_ref, da_ref, wg_ref, wv_ref, bg_ref, bv_ref, og_ref, ov_ref, dwg_ref, dwv_ref, dbg_ref, dbv_ref):
        wg, wv = wg_ref[...], wv_ref[...]

        def shifted(ref, c):
            ext = _window(ref, c, rows, seq, HALO, HALO)
            return [_lag(ext, k)[HALO:, :] for k in range(FFN_CONV)]

        def back(dpre, w, o_ref, c, xs, carry):
            dx = w[2:3, :] * dpre + w[1:2, :] * _lead(dpre, 1) + w[0:1, :] * _lead(dpre, 2)
            o_ref[pl.ds(pl.multiple_of(c * rows, rows), rows), :] = dx[:rows, :].astype(BF16)
            dp = dpre[:rows, :]
            return tuple(carry[k] + jnp.sum(dp * xs[k][:rows, :], axis=0, keepdims=True) for k in range(FFN_CONV)) + (
                carry[FFN_CONV] + jnp.sum(dp, axis=0, keepdims=True),)

        def chunk(c, carry):
            cg, cv = carry
            gs, vs = shifted(g_ref, c), shifted(v_ref, c)
            gate = bg_ref[...] + wg[2:3, :] * gs[0] + wg[1:2, :] * gs[1] + wg[0:1, :] * gs[2]
            val = bv_ref[...] + wv[2:3, :] * vs[0] + wv[1:2, :] * vs[1] + wv[0:1, :] * vs[2]
            dav = _window(da_ref, c, rows, seq, 0, HALO)
            sg = _sigmoid(gate)
            cg = back(dav * val * (sg * (1.0 + gate * (1.0 - sg))), wg, og_ref, c, gs, cg)
            cv = back(dav * (gate * sg), wv, ov_ref, c, vs, cv)
            return cg, cv

        z = jnp.zeros((1, cbk), F32)
        cg, cv = lax.fori_loop(0, seq // rows, chunk, ((z,) * (FFN_CONV + 1), (z,) * (FFN_CONV + 1)))
        dwg = jnp.concatenate([cg[2], cg[1], cg[0]], axis=0)
        dwv = jnp.concatenate([cv[2], cv[1], cv[0]], axis=0)

        @pl.when(pl.program_id(1) == 0)
        def _():
            dwg_ref[...] = dwg
            dwv_ref[...] = dwv
            dbg_ref[...] = cg[FFN_CONV]
            dbv_ref[...] = cv[FFN_CONV]

        @pl.when(pl.program_id(1) > 0)
        def _():
            dwg_ref[...] += dwg
            dwv_ref[...] += dwv
            dbg_ref[...] += cg[FFN_CONV]
            dbv_ref[...] += cv[FFN_CONV]

    blk = lambda off: pl.BlockSpec((None, seq, cbk), lambda j, i: (i, 0, j + off))
    wsp = lambda r, off: pl.BlockSpec((r, cbk), lambda j, i: (0, j + off))
    half = jax.ShapeDtypeStruct((b, seq, f2 // 2), BF16)
    dwshape = jax.ShapeDtypeStruct((FFN_CONV, f2 // 2), F32)
    dbshape = jax.ShapeDtypeStruct((1, f2 // 2), F32)
    dg, dv, dwg, dwv, dbg, dbv = pl.pallas_call(
        body,
        out_shape=(half, half, dwshape, dwshape, dbshape, dbshape),
        grid=(nj, b),
        in_specs=[blk(0), blk(nj), blk(0), wsp(FFN_CONV, 0), wsp(FFN_CONV, nj), wsp(1, 0), wsp(1, nj)],
        out_specs=(blk(0), blk(0), wsp(FFN_CONV, 0), wsp(FFN_CONV, 0), wsp(1, 0), wsp(1, 0)),
        compiler_params=_params(("parallel", "arbitrary")), name=name)(hpre, hpre, da, cw, cw, cb, cb)
    return dg, dv, jnp.concatenate([dwg, dwv], axis=1), jnp.concatenate([dbg, dbv], axis=1)


def _ssd_conv_fwd(zx, cw, cb, d_inner, name):
    b, seq, _ = zx.shape
    xbc = cw.shape[1]
    cbk = SHIFT_COLS
    off = d_inner // cbk
    rows = min(SHIFT_ROWS, seq)

    def body(h_ref, w_ref, b_ref, o_ref):
        w = w_ref[...]
        bias = b_ref[...]

        def chunk(c, carry):
            ext = _window(h_ref, c, rows, seq, 8, 0)
            acc = bias + w[3:4, :] * ext[8:, :]
            for k in range(1, SSD_CONV):
                acc = acc + w[3 - k:4 - k, :] * _lag(ext, k)[8:, :]
            o_ref[pl.ds(pl.multiple_of(c * rows, rows), rows), :] = acc * _sigmoid(acc)
            return carry

        lax.fori_loop(0, seq // rows, chunk, 0)

    return pl.pallas_call(
        body, out_shape=jax.ShapeDtypeStruct((b, seq, xbc), F32), grid=(b, xbc // cbk),
        in_specs=[pl.BlockSpec((None, seq, cbk), lambda i, j: (i, 0, j + off)),
                  pl.BlockSpec((SSD_CONV, cbk), lambda i, j: (0, j)),
                  pl.BlockSpec((1, cbk), lambda i, j: (0, j))],
        out_specs=pl.BlockSpec((None, seq, cbk), lambda i, j: (i, 0, j)),
        compiler_params=_params(("parallel", "parallel")), name=name)(zx, cw, cb)


def _ssd_conv_bwd(zx, dparts, cw, cb, d_inner, name):
    b, seq, _ = zx.shape
    xbc = cw.shape[1]
    cbk = SHIFT_COLS
    off = d_inner // cbk
    rows = min(SHIFT_ROWS, seq)
    nblk = [p.shape[2] // cbk for p in dparts]
    first = [sum(nblk[:s]) for s in range(len(dparts))]
    assert sum(nblk) == xbc // cbk

    def body(h_ref, gx_ref, gb_ref, gc_ref, w_ref, b_ref, o_ref, dw_ref, db_ref):
        w = w_ref[...]
        bias = b_ref[...]
        j = pl.program_id(0)

        def chunk(c, carry):
            dws, dbias = carry
            ext = _window(h_ref, c, rows, seq, 8, 8)
            xs = [_lag(ext, k)[8:, :] for k in range(SSD_CONV)]
            pre = bias + w[3:4, :] * xs[0]
            for k in range(1, SSD_CONV):
                pre = pre + w[3 - k:4 - k, :] * xs[k]
            s = _sigmoid(pre)
            gsel = jnp.where(j < first[1], _window(gx_ref, c, rows, seq, 0, 8),
                             jnp.where(j < first[2], _window(gb_ref, c, rows, seq, 0, 8),
                                       _window(gc_ref, c, rows, seq, 0, 8)))
            dpre = gsel * (s * (1.0 + pre * (1.0 - s)))
            dx = w[3:4, :] * dpre
            for k in range(1, SSD_CONV):
                dx = dx + w[3 - k:4 - k, :] * _lead(dpre, k)
            o_ref[pl.ds(pl.multiple_of(c * rows, rows), rows), :] = dx[:rows, :].astype(BF16)
            dp = dpre[:rows, :]
            dws = tuple(dws[k] + jnp.sum(dp * xs[k][:rows, :], axis=0, keepdims=True) for k in range(SSD_CONV))
            dbias = dbias + jnp.sum(dp, axis=0, keepdims=True)
            return dws, dbias

        z = jnp.zeros((1, cbk), F32)
        dws, dbias = lax.fori_loop(0, seq // rows, chunk, ((z,) * SSD_CONV, z))
        dwv = jnp.concatenate([dws[3 - i] for i in range(SSD_CONV)], axis=0)

        @pl.when(pl.program_id(1) == 0)
        def _():
            dw_ref[...] = dwv
            db_ref[...] = dbias

        @pl.when(pl.program_id(1) > 0)
        def _():
            dw_ref[...] += dwv
            db_ref[...] += dbias

    return pl.pallas_call(
        body,
        out_shape=(jax.ShapeDtypeStruct((b, seq, xbc), BF16), jax.ShapeDtypeStruct((SSD_CONV, xbc), F32),
                   jax.ShapeDtypeStruct((1, xbc), F32)),
        grid=(xbc // cbk, b),
        in_specs=[pl.BlockSpec((None, seq, cbk), lambda j, i: (i, 0, j + off))] + [
                  pl.BlockSpec((None, seq, cbk), lambda j, i, s=s: (i, 0, jnp.clip(j - first[s], 0, nblk[s] - 1)))
                  for s in range(3)] + [
                  pl.BlockSpec((SSD_CONV, cbk), lambda j, i: (0, j)),
                  pl.BlockSpec((1, cbk), lambda j, i: (0, j))],
        out_specs=(pl.BlockSpec((None, seq, cbk), lambda j, i: (i, 0, j)),
                   pl.BlockSpec((SSD_CONV, cbk), lambda j, i: (0, j)),
                   pl.BlockSpec((1, cbk), lambda j, i: (0, j))),
        compiler_params=_params(("parallel", "arbitrary")), name=name)(zx, *dparts, cw, cb)


def _pool_sums(q, g, lead):
    sh = _lead if lead else _lag
    s2 = q + sh(q, 1)
    s4 = s2 + sh(s2, 2)
    s8 = s4 + sh(s4, 4)
    s16 = s8 + sh(s8, 8)
    return jnp.where(g == 0, s2, jnp.where(g == 1, s4, jnp.where(g == 2, s8, s16)))


def _pool_count(r0, n, g, shape):
    t = (r0 + lax.broadcasted_iota(jnp.int32, shape, 0) + 1).astype(F32)
    return jnp.minimum(t, (2 << g).astype(F32))


def _pool_fwd(h, pw, scale, name):
    b, seq, d = h.shape
    dg = d // 4
    rows = min(SHIFT_ROWS, seq)

    def body(h_ref, w_ref, s_ref, o_ref):
        g = pl.program_id(1)
        wmat = w_ref[...]
        sc = s_ref[...]

        def chunk(c, carry):
            r0 = c * rows
            ext = _window(h_ref, c, rows, seq, 16, 0)
            sums = _pool_sums(ext, g, False)[16:, :]
            mixed = sums / _pool_count(r0, rows, g, (rows, dg)) - ext[16:, :]
            o_ref[pl.ds(pl.multiple_of(r0, rows), rows), :] = _nn(mixed.astype(BF16), wmat) * sc
            return carry

        lax.fori_loop(0, seq // rows, chunk, 0)

    return pl.pallas_call(
        body, out_shape=jax.ShapeDtypeStruct((b, seq, d), F32), grid=(b, 4),
        in_specs=[pl.BlockSpec((None, seq, dg), lambda i, g: (i, 0, g)),
                  pl.BlockSpec((None, dg, dg), lambda i, g: (g, 0, 0)),
                  pl.BlockSpec((1, dg), lambda i, g: (0, g))],
        out_specs=pl.BlockSpec((None, seq, dg), lambda i, g: (i, 0, g)),
        compiler_params=_params(("parallel", "parallel")), name=name)(h, pw, scale)


def _pool_bwd(h, dout, pw, scale, name):
    b, seq, d = h.shape
    dg = d // 4
    rows = min(SHIFT_ROWS, seq)

    def body(h_ref, g_ref, w_ref, s_ref, o_ref, dw_ref, ds_ref, dw_acc):
        g = pl.program_id(0)
        wmat = w_ref[...]
        sc = s_ref[...]
        dw_acc[...] = jnp.zeros_like(dw_acc)

        def chunk(c, dsc):
            r0 = c * rows
            ext = _window(h_ref, c, rows, seq, 16, 0)
            sums = _pool_sums(ext, g, False)[16:, :]
            mixed = (sums / _pool_count(r0, rows, g, (rows, dg)) - ext[16:, :]).astype(BF16)
            gext = _window(g_ref, c, rows, seq, 0, 16)
            dsc = dsc + jnp.sum(gext[:rows, :] * _nn(mixed, wmat), axis=0, keepdims=True)
            dpre = (gext * sc).astype(BF16)
            dw_acc[...] += _tn(mixed, dpre[:rows, :])
            dmix = _nt(dpre, wmat)
            q = dmix / _pool_count(r0, rows + 16, g, (rows + 16, dg))
            back = _pool_sums(q, g, True)
            o_ref[pl.ds(pl.multiple_of(r0, rows), rows), :] = back[:rows, :] - dmix[:rows, :]
            return dsc

        dsc = lax.fori_loop(0, seq // rows, chunk, jnp.zeros((1, dg), F32))

        @pl.when(pl.program_id(1) == 0)
        def _():
            dw_ref[...] = dw_acc[...]
            ds_ref[...] = dsc

        @pl.when(pl.program_id(1) > 0)
        def _():
            dw_ref[...] += dw_acc[...]
            ds_ref[...] += dsc

    return pl.pallas_call(
        body,
        out_shape=(jax.ShapeDtypeStruct((b, seq, d), F32), jax.ShapeDtypeStruct((4, dg, dg), F32),
                   jax.ShapeDtypeStruct((1, d), F32)),
        grid=(4, b),
        in_specs=[pl.BlockSpec((None, seq, dg), lambda g, i: (i, 0, g)),
                  pl.BlockSpec((None, seq, dg), lambda g, i: (i, 0, g)),
                  pl.BlockSpec((None, dg, dg), lambda g, i: (g, 0, 0)),
                  pl.BlockSpec((1, dg), lambda g, i: (0, g))],
        out_specs=(pl.BlockSpec((None, seq, dg), lambda g, i: (i, 0, g)),
                   pl.BlockSpec((None, dg, dg), lambda g, i: (g, 0, 0)),
                   pl.BlockSpec((1, dg), lambda g, i: (0, g))),
        scratch_shapes=[pltpu.VMEM((dg, dg), F32)],
        compiler_params=_params(("parallel", "arbitrary")), name=name)(h, dout, pw, scale)


def _head_of(channel):
    return jnp.right_shift(channel, HEAD_DIM.bit_length() - 1)


def _ssd_consts(gw):
    q = CHUNK
    row = lax.broadcasted_iota(jnp.int32, (q, q), 0)
    col = lax.broadcasted_iota(jnp.int32, (q, q), 1)
    tril = (row >= col).astype(BF16)
    triu = (row <= col).astype(BF16)
    e = (_head_of(lax.broadcasted_iota(jnp.int32, (LANES, gw), 1))
         == lax.broadcasted_iota(jnp.int32, (LANES, gw), 0)).astype(BF16)
    et = (_head_of(lax.broadcasted_iota(jnp.int32, (gw, LANES), 0))
          == lax.broadcasted_iota(jnp.int32, (gw, LANES), 1)).astype(BF16)
    return row, col, tril, triu, e, et


def _ssd_common(dtr, dtb, alog, gw):
    q = CHUNK
    row, col, tril, triu, e, et = _ssd_consts(gw)
    dt = _softplus(dtr + dtb)
    a_row = -jnp.exp(alog)
    acum = _sel_left(tril, dt * a_row)
    ac_last = jnp.sum(jnp.where(row == q - 1, acum, 0.0), axis=0, keepdims=True)
    eac = jnp.exp(acum)
    de = jnp.exp(ac_last - acum)
    e2 = jnp.concatenate([e, e], axis=0)
    expand = _sel_right(jnp.concatenate([dt, eac, de], axis=0), e2, 2)
    dt_x, eac_x, de_x = expand[0:q], expand[q:2 * q], expand[2 * q:3 * q]
    acum_t = acum.T
    cd_col = jnp.exp(acum_t[:, q - 1:q])
    et3 = jnp.concatenate([et, et, et], axis=1)
    cdmat = _nn(et3, jnp.concatenate(_split(jnp.broadcast_to(cd_col, (LANES, D_STATE)), 3), axis=0))
    consts = dict(row=row, col=col, tril=tril, triu=triu, e=e, et=et)
    return dt, a_row, acum, acum_t, ac_last, eac, de, dt_x, eac_x, de_x, cdmat, consts


def _decay(acum, acum_t, j, row, col):
    diff = acum[:, j:j + 1] - acum_t[j:j + 1, :]
    return jnp.exp(jnp.where(row >= col, diff, -1e30))


def _ssd_fwd(xc, zx, dtb, alog, dskip, nw, d_inner, name):
    b, seq, xbc = xc.shape
    q = CHUNK
    nc = seq // q
    gw = d_inner // N_GROUPS
    nh = gw // HEAD_DIM
    xb0 = d_inner // D_STATE
    xc0 = xb0 + N_GROUPS
    dt0 = (d_inner + xbc) // LANES

    def body(x_ref, b_ref, c_ref, z_ref, dtr_ref, dtb_ref, al_ref, dsk_ref, nw_ref, y_ref, yn_ref, st_ref, s_ref):
        @pl.when(pl.program_id(2) == 0)
        def _():
            s_ref[...] = jnp.zeros_like(s_ref)

        prev = s_ref[...]
        st_ref[...] = prev
        x = x_ref[...]
        bm = b_ref[...].astype(BF16)
        cm = c_ref[...].astype(BF16)
        (dt, a_row, acum, acum_t, ac_last, eac, de, dt_x, eac_x, de_x, cdmat, k) = _ssd_common(
            dtr_ref[...], dtb_ref[0:1, :], al_ref[0:1, :], gw)
        xdt = x * dt_x
        xdt_b = xdt.astype(BF16)
        cb = _nt(cm, bm)
        half = _head_of(lax.broadcasted_iota(jnp.int32, (q, LANES), 1))
        pairs = []
        for j in range(nh):
            pc = (j // 2) * LANES
            m = (cb * _decay(acum, acum_t, j, k["row"], k["col"])).astype(BF16)
            yj = jnp.where(half == j % 2, _nn(m, xdt_b[:, pc:pc + LANES]), 0.0)
            if j % 2 == 0:
                pairs.append(yj)
            else:
                pairs[-1] = pairs[-1] + yj
        prev_b = prev.astype(BF16)
        y = dsk_ref[0:1, :] * x + jnp.concatenate(pairs, axis=1) + eac_x * _nt(cm, prev_b)
        s_ref[...] = cdmat * prev + _tn((xdt * de_x).astype(BF16), bm)
        y_ref[...] = y
        z = z_ref[...]
        yg = y * (z * _sigmoid(z))
        r = lax.rsqrt(jnp.mean(yg * yg, axis=-1, keepdims=True) + EPS)
        yn_ref[...] = ((yg * r) * nw_ref[0:1, :]).astype(BF16)

    par = lambda w: pl.BlockSpec((None, 8, w), lambda i, g, c: (g, 0, 0))
    return pl.pallas_call(
        body,
        out_shape=(jax.ShapeDtypeStruct((b, seq, d_inner), F32), jax.ShapeDtypeStruct((b, seq, d_inner), BF16),
                   jax.ShapeDtypeStruct((b, nc, N_GROUPS, gw, D_STATE), F32)),
        grid=(b, N_GROUPS, nc),
        in_specs=[pl.BlockSpec((None, q, gw), lambda i, g, c: (i, c, g)),
                  pl.BlockSpec((None, q, D_STATE), lambda i, g, c: (i, c, xb0 + g)),
                  pl.BlockSpec((None, q, D_STATE), lambda i, g, c: (i, c, xc0 + g)),
                  pl.BlockSpec((None, q, gw), lambda i, g, c: (i, c, g)),
                  pl.BlockSpec((None, q, LANES), lambda i, g, c: (i, c, dt0 + g)),
                  par(LANES), par(LANES), par(gw), par(gw)],
        out_specs=(pl.BlockSpec((None, q, gw), lambda i, g, c: (i, c, g)),
                   pl.BlockSpec((None, q, gw), lambda i, g, c: (i, c, g)),
                   pl.BlockSpec((None, None, None, gw, D_STATE), lambda i, g, c: (i, c, g, 0, 0))),
        scratch_shapes=[pltpu.VMEM((gw, D_STATE), F32)],
        compiler_params=_params(("parallel", "parallel", "arbitrary")), name=name,
    )(xc, xc, xc, zx, zx, dtb, alog, dskip, nw)


def _ssd_bwd(xc, zx, y, dyn, st, dtb, alog, dskip, nw, d_inner, name):
    b, seq, xbc = xc.shape
    q = CHUNK
    nc = seq // q
    gw = d_inner // N_GROUPS
    nh = gw // HEAD_DIM
    xb0 = d_inner // D_STATE
    xc0 = xb0 + N_GROUPS
    dt0 = (d_inner + xbc) // LANES

    def body(x_ref, b_ref, c_ref, z_ref, dtr_ref, y_ref, g_ref, st_ref, dtb_ref, al_ref, dsk_ref, nw_ref,
             dz_ref, dx_ref, db_ref, dc_ref, ddt_ref, dnw_ref, dd_ref, dal_ref, dbias_ref,
             ds_ref, colbuf, rowbuf):
        first = jnp.logical_and(pl.program_id(1) == 0, pl.program_id(2) == 0)

        @pl.when(pl.program_id(2) == 0)
        def _():
            ds_ref[...] = jnp.zeros_like(ds_ref)

        x = x_ref[...]
        bm = b_ref[...].astype(BF16)
        cm = c_ref[...].astype(BF16)
        z = z_ref[...]
        y = y_ref[...]
        prev = st_ref[...]
        dtr = dtr_ref[...] + dtb_ref[0:1, :]
        (dt, a_row, acum, acum_t, ac_last, eac, de, dt_x, eac_x, de_x, cdmat, k) = _ssd_common(
            dtr_ref[...], dtb_ref[0:1, :], al_ref[0:1, :], gw)
        row, col = k["row"], k["col"]
        et2 = jnp.concatenate([k["et"], k["et"]], axis=0)

        sz = _sigmoid(z)
        silu_z = z * sz
        yg = y * silu_z
        r = lax.rsqrt(jnp.mean(yg * yg, axis=-1, keepdims=True) + EPS)
        xh = yg * r
        dyn = g_ref[...]
        gh = dyn * nw_ref[0:1, :]
        dyg = r * (gh - xh * jnp.mean(gh * xh, axis=-1, keepdims=True))
        dnw = jnp.sum(dyn * xh, axis=0, keepdims=True)
        g = dyg * silu_z
        dz_ref[...] = (dyg * y * (sz * (1.0 + z * (1.0 - sz)))).astype(BF16)
        dd = _sel_right(jnp.broadcast_to(jnp.sum(g * x, axis=0, keepdims=True), (8, gw)), et2, 2)

        xdt = x * dt_x
        xdt_b = xdt.astype(BF16)
        g_b = g.astype(BF16)
        prev_b = prev.astype(BF16)
        cb = _nt(cm, bm)

        cp = _nt(cm, prev_b)
        ge = g * eac_x
        dac = _sel_right(ge * cp, et2, 2)
        ge_b = ge.astype(BF16)
        dcm = _nn(ge_b, prev_b)
        dprev = _tn(ge_b, cm)

        colbuf[...] = jnp.zeros_like(colbuf)
        rowbuf[...] = jnp.zeros_like(rowbuf)
        dcb = jnp.zeros((q, q), F32)
        half = _head_of(lax.broadcasted_iota(jnp.int32, (q, LANES), 1))
        pairs = []
        for j in range(nh):
            pc = (j // 2) * LANES
            dec = _decay(acum, acum_t, j, row, col)
            m = cb * dec
            gj = jnp.where(half == j % 2, g[:, pc:pc + LANES], 0.0).astype(BF16)
            dm = _nt(gj, xdt_b[:, pc:pc + LANES])
            w = dm * m
            colbuf[:, j:j + 1] = jnp.sum(w, axis=1, keepdims=True)
            rowbuf[j:j + 1, :] = jnp.sum(w, axis=0, keepdims=True)
            dcb = dcb + dm * dec
            dj = jnp.where(half == j % 2, _tn(m.astype(BF16), g_b[:, pc:pc + LANES]), 0.0)
            if j % 2 == 0:
                pairs.append(dj)
            else:
                pairs[-1] = pairs[-1] + dj
        dxdt = jnp.concatenate(pairs, axis=1)
        dcb_b = dcb.astype(BF16)
        dcm = dcm + _nn(dcb_b, bm)
        dbm = _tn(dcb_b, cm)

        ds = ds_ref[...]
        ds_b = ds.astype(BF16)
        u = _nt(bm, ds_b)
        dxdt = dxdt + u * de_x
        dde = _sel_right(u * xdt, et2, 2)
        dbm = dbm + _nn((xdt * de_x).astype(BF16), ds_b)
        pm = jnp.concatenate(_split(ds * prev, 2), axis=1)
        t2 = _tn(pm, k["et"])
        dcd_row = jnp.sum(t2[0:D_STATE] + t2[D_STATE:2 * D_STATE], axis=0, keepdims=True)
        last = dcd_row * jnp.exp(ac_last) + jnp.sum(dde * de, axis=0, keepdims=True)
        dac = dac + colbuf[...] - rowbuf[...].T - dde * de + jnp.where(row == q - 1, last, 0.0)
        ds_ref[...] = cdmat * ds + dprev

        dadt = _sel_left(k["triu"], dac)
        ddt = _sel_right(dxdt * x, et2, 2) + dadt * a_row
        dal = jnp.sum(dadt * dt, axis=0, keepdims=True) * a_row
        lane = lax.broadcasted_iota(jnp.int32, (q, LANES), 1)
        ddtr = jnp.where(lane < nh, ddt * _sigmoid(dtr), 0.0)
        ddt_ref[...] = ddtr.astype(BF16)
        dbias = jnp.sum(ddtr, axis=0, keepdims=True)
        dx_ref[...] = dxdt * dt_x + dsk_ref[0:1, :] * g
        db_ref[...] = dbm
        dc_ref[...] = dcm

        @pl.when(first)
        def _():
            dnw_ref[...] = jnp.broadcast_to(dnw, (8, gw))
            dd_ref[...] = dd
            dal_ref[...] = jnp.broadcast_to(dal, (8, LANES))
            dbias_ref[...] = jnp.broadcast_to(dbias, (8, LANES))

        @pl.when(jnp.logical_not(first))
        def _():
            dnw_ref[...] += jnp.broadcast_to(dnw, (8, gw))
            dd_ref[...] += dd
            dal_ref[...] += jnp.broadcast_to(dal, (8, LANES))
            dbias_ref[...] += jnp.broadcast_to(dbias, (8, LANES))

    rc = lambda c: nc - 1 - c
    par = lambda w: pl.BlockSpec((None, 8, w), lambda g, i, c: (g, 0, 0))
    blk = lambda w: pl.BlockSpec((None, q, w), lambda g, i, c: (i, rc(c), g))
    return pl.pallas_call(
        body,
        out_shape=(jax.ShapeDtypeStruct((b, seq, d_inner), BF16),
                   jax.ShapeDtypeStruct((b, seq, d_inner), F32),
                   jax.ShapeDtypeStruct((b, seq, N_GROUPS * D_STATE), F32),
                   jax.ShapeDtypeStruct((b, seq, N_GROUPS * D_STATE), F32),
                   jax.ShapeDtypeStruct((b, seq, N_GROUPS * LANES), BF16),
                   jax.ShapeDtypeStruct((N_GROUPS, 8, gw), F32),
                   jax.ShapeDtypeStruct((N_GROUPS, 8, LANES), F32),
                   jax.ShapeDtypeStruct((N_GROUPS, 8, LANES), F32),
                   jax.ShapeDtypeStruct((N_GROUPS, 8, LANES), F32)),
        grid=(N_GROUPS, b, nc),
        in_specs=[blk(gw),
                  pl.BlockSpec((None, q, D_STATE), lambda g, i, c: (i, rc(c), xb0 + g)),
                  pl.BlockSpec((None, q, D_STATE), lambda g, i, c: (i, rc(c), xc0 + g)),
                  blk(gw),
                  pl.BlockSpec((None, q, LANES), lambda g, i, c: (i, rc(c), dt0 + g)),
                  blk(gw), blk(gw),
                  pl.BlockSpec((None, None, None, gw, D_STATE), lambda g, i, c: (i, rc(c), g, 0, 0)),
                  par(LANES), par(LANES), par(gw), par(gw)],
        out_specs=(blk(gw), blk(gw), blk(D_STATE), blk(D_STATE), blk(LANES),
                   par(gw), par(LANES), par(LANES), par(LANES)),
        scratch_shapes=[pltpu.VMEM((gw, D_STATE), F32), pltpu.VMEM((q, LANES), F32), pltpu.VMEM((LANES, q), F32)],
        compiler_params=_params(("parallel", "arbitrary", "arbitrary")), name=name,
    )(xc, xc, xc, zx, zx, y, dyn, st, dtb, alog, dskip, nw)


def _adamw(w, g, m, v, name):
    rows, cols = w.shape
    tr = rows
    for cand in (512, 256, 128, 64, 32, 16, 8):
        if rows % cand == 0 and cand * cols * 4 <= 2 * 1024 * 1024:
            tr = cand
            break
    c1 = 1.0 - ADAM_B1 ** ADAM_STEP
    c2 = 1.0 - ADAM_B2 ** ADAM_STEP

    def body(w_ref, g_ref, m_ref, v_ref, d_ref, mo_ref, vo_ref):
        gv = g_ref[...]
        mn = ADAM_B1 * m_ref[...] + (1.0 - ADAM_B1) * gv
        vn = ADAM_B2 * v_ref[...] + (1.0 - ADAM_B2) * (gv * gv)
        mo_ref[...] = mn
        vo_ref[...] = vn
        d_ref[...] = -ADAM_LR * ((mn / c1) / (jnp.sqrt(vn / c2) + ADAM_EPS) + ADAM_WD * w_ref[...])

    spec = pl.BlockSpec((tr, cols), lambda i: (i, 0))
    shp = jax.ShapeDtypeStruct((rows, cols), F32)
    return pl.pallas_call(body, out_shape=(shp, shp, shp), grid=(rows // tr,), in_specs=[spec] * 4,
                          out_specs=(spec,) * 3, compiler_params=_params(("parallel",)), name=name)(w, g, m, v)


def _pick_rows(rows, row_bytes, limit=1 << 20):
    for cand in (2048, 1024, 512, 256, 128, 64, 32, 16):
        if rows % cand == 0 and cand * row_bytes <= limit:
            return cand
    return rows


def _as3d(a, lead):
    return a.reshape(a.shape[:lead] + (-1, a.shape[-1]))


def _pair_sum(g, got, core, name):
    h = got.shape[0]
    g3, got3 = _as3d(g, 1), _as3d(got, 1)
    _, rows, cols = got3.shape
    tr = _pick_rows(rows, cols * 4)

    def body(c_ref, g_ref, r_ref, o_ref):
        o_ref[...] = (g_ref[...] + r_ref[...]).astype(BF16)

    out = pl.pallas_call(
        body, out_shape=jax.ShapeDtypeStruct(got3.shape, BF16),
        grid_spec=pltpu.PrefetchScalarGridSpec(
            num_scalar_prefetch=1, grid=(h, rows // tr),
            in_specs=[pl.BlockSpec((None, tr, cols), lambda l, i, c_ref: (c_ref[0] * h + l, i, 0)),
                      pl.BlockSpec((None, tr, cols), lambda l, i, c_ref: (l, i, 0))],
            out_specs=pl.BlockSpec((None, tr, cols), lambda l, i, c_ref: (l, i, 0))),
        compiler_params=_params(("parallel", "parallel")), name=name)(core, g3, got3)
    return out.reshape(got.shape)


def _sum4(q, core, name):
    q4 = _as3d(q, 2)
    _, h, rows, cols = q4.shape
    tr = _pick_rows(rows, cols * 4)

    def body(c_ref, q0, q1, q2, q3, o_ref):
        o_ref[...] = ((q0[...].astype(F32) + q1[...].astype(F32)) + q2[...].astype(F32)) + q3[...].astype(F32)

    out = pl.pallas_call(
        body, out_shape=jax.ShapeDtypeStruct((2 * h, rows, cols), F32),
        grid_spec=pltpu.PrefetchScalarGridSpec(
            num_scalar_prefetch=1, grid=(h, rows // tr),
            in_specs=[pl.BlockSpec((None, None, tr, cols), lambda l, i, c_ref, k=k: (k, l, i, 0))
                      for k in range(N_CHIPS)],
            out_specs=pl.BlockSpec((None, tr, cols), lambda l, i, c_ref: (c_ref[0] * h + l, i, 0))),
        compiler_params=_params(("parallel", "parallel")), name=name)(core, q4, q4, q4, q4)
    return out.reshape((2 * h,) + q.shape[2:])


def _coords():
    return lax.axis_index("x"), lax.axis_index("y"), lax.axis_index("c")


def _other_chips(x, y):
    return [(1 - x, y), (x, 1 - y), (1 - x, 1 - y)]


def _allgather_halves(src, name):
    rows, cols = src.shape

    def body(x_ref, o_ref, send, recv, local):
        x, y, c = _coords()
        sib = (x, y, 1 - c)
        chips = _other_chips(x, y)

        def slot(h, cx, cy):
            return o_ref.at[h, 2 * cx + cy]

        def copy(kk, dst, to, src_ref):
            return pltpu.make_async_remote_copy(src_ref=src_ref, dst_ref=dst, send_sem=send.at[kk],
                                                recv_sem=recv.at[kk], device_id=to, device_id_type=MESH)

        mine = pltpu.make_async_copy(x_ref, slot(c, x, y), local)
        mine.start()
        first = [copy(0, slot(c, x, y), sib, x_ref)]
        first += [copy(1 + j, slot(c, x, y), (*chip, c), x_ref) for j, chip in enumerate(chips)]
        for cp in first:
            cp.start()
        passed = [copy(4 + j, slot(c, *chip), sib, slot(c, *chip)) for j, chip in enumerate(chips)]
        for j, chip in enumerate(chips):
            copy(1 + j, slot(c, *chip), (x, y, c), x_ref).wait_recv()
            passed[j].start()
        copy(0, slot(1 - c, x, y), (x, y, c), x_ref).wait_recv()
        for j, chip in enumerate(chips):
            copy(4 + j, slot(1 - c, *chip), (x, y, c), x_ref).wait_recv()
        for cp in first + passed:
            cp.wait_send()
        mine.wait()

    return pl.pallas_call(
        body, out_shape=jax.ShapeDtypeStruct((2, N_CHIPS, rows, cols), src.dtype),
        in_specs=[ANY], out_specs=ANY,
        scratch_shapes=[pltpu.SemaphoreType.DMA((7,)), pltpu.SemaphoreType.DMA((7,)), pltpu.SemaphoreType.DMA],
        name=name)(src)


BIGW = (("ssd_w_in", None), ("ssd_w_out", 0), ("pool_w", 1), ("ffn_w_up", 1), ("ffn_w_down", 0))


def _chip_window(axis, ref, layers, k):
    if axis is None:
        return ref.at[layers, k]
    n = ref.shape[1 + axis] // N_CHIPS
    sl = pl.ds(pl.multiple_of(k * n, LANES if 1 + axis == len(ref.shape) - 1 else 8), n)
    idx = [layers] + [slice(None)] * (len(ref.shape) - 1)
    idx[1 + axis] = sl
    return ref.at[tuple(idx)]


def _full_shape(axis, shard_shape):
    if axis is None:
        return (shard_shape[0], N_CHIPS) + tuple(shard_shape[1:])
    full = list(shard_shape)
    full[1 + axis] *= N_CHIPS
    return tuple(full)


def _gather_big(shards, name):
    n = len(BIGW)

    def body(*refs):
        ins, outs = refs[:n], refs[n:2 * n]
        send, recv, local = refs[2 * n:]
        x, y, c = _coords()
        me = 2 * x + y
        sib = (x, y, 1 - c)
        chips = _other_chips(x, y)
        plan = []
        for w, (_, axis) in enumerate(BIGW):
            h = ins[w].shape[0] // 2
            mine, theirs = pl.ds(c * h, h), pl.ds((1 - c) * h, h)
            src = ins[w].at[mine]

            def dst(layers, k, w=w, axis=axis):
                return _chip_window(axis, outs[w], layers, k)

            def copy(kk, d, to, s, w=w):
                return pltpu.make_async_remote_copy(src_ref=s, dst_ref=d, send_sem=send.at[7 * w + kk],
                                                    recv_sem=recv.at[7 * w + kk], device_id=to, device_id_type=MESH)

            lc = pltpu.make_async_copy(src, dst(mine, me), local.at[w])
            lc.start()
            first = [copy(0, dst(mine, me), sib, src)]
            first += [copy(1 + j, dst(mine, me), (cx, cy, c), src) for j, (cx, cy) in enumerate(chips)]
            for cp in first:
                cp.start()
            plan.append((lc, first, dst, copy, mine, theirs, src))
        sent = []
        for lc, first, dst, copy, mine, theirs, src in plan:
            for j, (cx, cy) in enumerate(chips):
                blk = dst(mine, 2 * cx + cy)
                copy(1 + j, blk, (x, y, c), src).wait_recv()
                fwd = copy(4 + j, blk, sib, blk)
                fwd.start()
                sent.append(fwd)
        for lc, first, dst, copy, mine, theirs, src in plan:
            copy(0, dst(theirs, me), (x, y, c), src).wait_recv()
            for j, (cx, cy) in enumerate(chips):
                copy(4 + j, dst(theirs, 2 * cx + cy), (x, y, c), src).wait_recv()
            for cp in first:
                cp.wait_send()
            lc.wait()
        for cp in sent:
            cp.wait_send()

    outs = pl.pallas_call(
        body,
        out_shape=tuple(jax.ShapeDtypeStruct(_full_shape(axis, s.shape), s.dtype) for s, (_, axis) in zip(shards, BIGW)),
        in_specs=[ANY] * n, out_specs=(ANY,) * n,
        scratch_shapes=[pltpu.SemaphoreType.DMA((7 * n,)), pltpu.SemaphoreType.DMA((7 * n,)),
                        pltpu.SemaphoreType.DMA((n,))],
        name=name)(*shards)
    return list(outs)


def _swap_grads(gs, name):
    n = len(gs)

    def body(*refs):
        ins, outs = refs[:n], refs[n:2 * n]
        send, recv = refs[2 * n:]
        x, y, c = _coords()
        cps = []
        for w in range(n):
            h = ins[w].shape[0] // 2
            cp = pltpu.make_async_remote_copy(src_ref=ins[w].at[pl.ds((1 - c) * h, h)], dst_ref=outs[w],
                                              send_sem=send.at[w], recv_sem=recv.at[w],
                                              device_id=(x, y, 1 - c), device_id_type=MESH)
            cp.start()
            cps.append(cp)
        for cp in cps:
            cp.wait()

    outs = pl.pallas_call(
        body, out_shape=tuple(jax.ShapeDtypeStruct((g.shape[0] // 2,) + g.shape[1:], g.dtype) for g in gs),
        in_specs=[ANY] * n, out_specs=(ANY,) * n,
        scratch_shapes=[pltpu.SemaphoreType.DMA((n,)), pltpu.SemaphoreType.DMA((n,))], name=name)(*gs)
    return list(outs)


def _scatter_grads(ps, name):
    n = len(ps)

    def shard_shape(p, axis):
        if axis is None:
            return (p.shape[0],) + p.shape[2:]
        s = list(p.shape)
        s[1 + axis] //= N_CHIPS
        return tuple(s)

    def body(*refs):
        ins, outs = refs[:n], refs[n:2 * n]
        send, recv, local = refs[2 * n:]
        x, y, c = _coords()
        me = 2 * x + y
        chips = _other_chips(x, y)
        cps, lcs = [], []
        for w, (_, axis) in enumerate(BIGW):
            layers = pl.ds(0, ins[w].shape[0])
            lc = pltpu.make_async_copy(_chip_window(axis, ins[w], layers, me), outs[w].at[me], local.at[w])
            lc.start()
            lcs.append(lc)
            for j, (cx, cy) in enumerate(chips):
                cp = pltpu.make_async_remote_copy(src_ref=_chip_window(axis, ins[w], layers, 2 * cx + cy),
                                                  dst_ref=outs[w].at[me], send_sem=send.at[3 * w + j],
                                                  recv_sem=recv.at[3 * w + j], device_id=(cx, cy, c), device_id_type=MESH)
                cp.start()
                cps.append(cp)
        for w, (_, axis) in enumerate(BIGW):
            layers = pl.ds(0, ins[w].shape[0])
            for j, (cx, cy) in enumerate(chips):
                pltpu.make_async_remote_copy(src_ref=_chip_window(axis, ins[w], layers, me), dst_ref=outs[w].at[2 * cx + cy],
                                             send_sem=send.at[3 * w + j], recv_sem=recv.at[3 * w + j],
                                             device_id=(x, y, c), device_id_type=MESH).wait_recv()
        for cp in cps:
            cp.wait_send()
        for lc in lcs:
            lc.wait()

    outs = pl.pallas_call(
        body,
        out_shape=tuple(jax.ShapeDtypeStruct((N_CHIPS,) + shard_shape(p, axis), p.dtype) for p, (_, axis) in zip(ps, BIGW)),
        in_specs=[ANY] * n, out_specs=(ANY,) * n,
        scratch_shapes=[pltpu.SemaphoreType.DMA((3 * n,)), pltpu.SemaphoreType.DMA((3 * n,)),
                        pltpu.SemaphoreType.DMA((n,))],
        name=name)(*ps)
    return list(outs)


def _share_grads(ss, name):
    n = len(ss)

    def body(*refs):
        outs = refs[n:2 * n]
        send, recv = refs[2 * n:]
        x, y, c = _coords()
        cps = []
        for w in range(n):
            h = outs[w].shape[0] // 2
            mine = outs[w].at[pl.ds(c * h, h)]
            cp = pltpu.make_async_remote_copy(src_ref=mine, dst_ref=mine, send_sem=send.at[w], recv_sem=recv.at[w],
                                              device_id=(x, y, 1 - c), device_id_type=MESH)
            cp.start()
            cps.append(cp)
        for w in range(n):
            h = outs[w].shape[0] // 2
            theirs = outs[w].at[pl.ds((1 - c) * h, h)]
            pltpu.make_async_remote_copy(src_ref=theirs, dst_ref=theirs, send_sem=send.at[w], recv_sem=recv.at[w],
                                         device_id=(x, y, c), device_id_type=MESH).wait_recv()
        for cp in cps:
            cp.wait_send()

    outs = pl.pallas_call(
        body, out_shape=tuple(jax.ShapeDtypeStruct(s.shape, s.dtype) for s in ss),
        in_specs=[ANY] * n, out_specs=(ANY,) * n, input_output_aliases={w: w for w in range(n)},
        scratch_shapes=[pltpu.SemaphoreType.DMA((n,)), pltpu.SemaphoreType.DMA((n,))],
        name=name)(*ss)
    return list(outs)


def _allreduce_small(vec, name):
    rows, cols = vec.shape

    def body(x_ref, o_ref, buf, send, recv):
        x, y, c = _coords()
        me = 4 * x + 2 * y + c
        buf[me] = x_ref[...]
        cps = []
        for kk in range(1, 8):
            dx, dy, dc = (kk >> 2) & 1, (kk >> 1) & 1, kk & 1
            to = (1 - x if dx else x, 1 - y if dy else y, 1 - c if dc else c)
            cp = pltpu.make_async_remote_copy(src_ref=x_ref, dst_ref=buf.at[me], send_sem=send.at[kk - 1],
                                              recv_sem=recv.at[kk - 1], device_id=to, device_id_type=MESH)
            cp.start()
            cps.append((cp, 4 * to[0] + 2 * to[1] + to[2]))
        for kk, (cp, frm) in enumerate(cps):
            pltpu.make_async_remote_copy(src_ref=x_ref, dst_ref=buf.at[frm], send_sem=send.at[kk],
                                         recv_sem=recv.at[kk], device_id=(x, y, c), device_id_type=MESH).wait_recv()
        for cp, _ in cps:
            cp.wait_send()
        acc = buf[0]
        for kk in range(1, 8):
            acc = acc + buf[kk]
        o_ref[...] = acc

    vm = pl.BlockSpec(memory_space=pltpu.VMEM)
    return pl.pallas_call(
        body, out_shape=jax.ShapeDtypeStruct((rows, cols), F32), in_specs=[vm], out_specs=vm,
        scratch_shapes=[pltpu.VMEM((8, rows, cols), F32), pltpu.SemaphoreType.DMA((7,)), pltpu.SemaphoreType.DMA((7,))],
        compiler_params=_params(), name=name)(vec)


SMALL = (("ssd_conv_w", 2), ("pool_scale", 1), ("ffn_conv_w", 2))
REPL = ("ssd_conv_b", "ssd_dt_bias", "ssd_a_log", "ssd_d", "ssd_norm_w", "ffn_conv_b",
        "norm_mix_pre", "norm_mix_post", "norm_ffn_pre", "norm_ffn_post")
WEIGHTS = ("ssd_w_in", "ssd_conv_w", "ssd_conv_b", "ssd_dt_bias", "ssd_a_log", "ssd_d", "ssd_norm_w", "ssd_w_out",
           "pool_w", "pool_scale", "ffn_w_up", "ffn_conv_w", "ffn_conv_b", "ffn_w_down", "norm_mix_pre",
           "norm_mix_post", "norm_ffn_pre", "norm_ffn_post")


def _flat_rows(n):
    unit = 2 * 16 * FLAT_COLS
    return 2 * 16 * ((n + unit - 1) // unit)


def _flatten_shards(arrs, dtype):
    flat = jnp.concatenate([a.astype(dtype).reshape(-1) for a in arrs])
    rows = _flat_rows(flat.shape[0])
    flat = jnp.pad(flat, (0, rows * FLAT_COLS - flat.shape[0]))
    return flat.reshape(2, rows // 2, FLAT_COLS)


def _unflatten_full(gathered, shard_shapes, axes):
    per_chip = jnp.swapaxes(gathered, 0, 1).reshape(N_CHIPS, -1)
    out, off = [], 0
    for shp, ax in zip(shard_shapes, axes):
        n = math.prod(shp)
        pieces = [per_chip[k, off:off + n].reshape(shp) for k in range(N_CHIPS)]
        out.append(jnp.concatenate(pieces, axis=ax))
        off += n
    return out


def kernel(x, ssd_w_in, ssd_conv_w, ssd_conv_b, ssd_dt_bias, ssd_a_log, ssd_d, ssd_norm_w, ssd_w_out, pool_w, pool_scale, ffn_w_up, ffn_conv_w, ffn_conv_b, ffn_w_down, norm_mix_pre, norm_mix_post, norm_ffn_pre, norm_ffn_post, loss_target, m_ssd_w_in, m_ssd_conv_w, m_ssd_conv_b, m_ssd_dt_bias, m_ssd_a_log, m_ssd_d, m_ssd_norm_w, m_ssd_w_out, m_pool_w, m_pool_scale, m_ffn_w_up, m_ffn_conv_w, m_ffn_conv_b, m_ffn_w_down, m_norm_mix_pre, m_norm_mix_post, m_norm_ffn_pre, m_norm_ffn_post, v_ssd_w_in, v_ssd_conv_w, v_ssd_conv_b, v_ssd_dt_bias, v_ssd_a_log, v_ssd_d, v_ssd_norm_w, v_ssd_w_out, v_pool_w, v_pool_scale, v_ffn_w_up, v_ffn_conv_w, v_ffn_conv_b, v_ffn_w_down, v_norm_mix_pre, v_norm_mix_post, v_norm_ffn_pre, v_norm_ffn_post):
    wts = dict(ssd_w_in=ssd_w_in, ssd_conv_w=ssd_conv_w, ssd_conv_b=ssd_conv_b, ssd_dt_bias=ssd_dt_bias,
               ssd_a_log=ssd_a_log, ssd_d=ssd_d, ssd_norm_w=ssd_norm_w, ssd_w_out=ssd_w_out, pool_w=pool_w,
               pool_scale=pool_scale, ffn_w_up=ffn_w_up, ffn_conv_w=ffn_conv_w, ffn_conv_b=ffn_conv_b,
               ffn_w_down=ffn_w_down, norm_mix_pre=norm_mix_pre, norm_mix_post=norm_mix_post,
               norm_ffn_pre=norm_ffn_pre, norm_ffn_post=norm_ffn_post)
    mom = dict(ssd_w_in=m_ssd_w_in, ssd_conv_w=m_ssd_conv_w, ssd_conv_b=m_ssd_conv_b, ssd_dt_bias=m_ssd_dt_bias,
               ssd_a_log=m_ssd_a_log, ssd_d=m_ssd_d, ssd_norm_w=m_ssd_norm_w, ssd_w_out=m_ssd_w_out, pool_w=m_pool_w,
               pool_scale=m_pool_scale, ffn_w_up=m_ffn_w_up, ffn_conv_w=m_ffn_conv_w, ffn_conv_b=m_ffn_conv_b,
               ffn_w_down=m_ffn_w_down, norm_mix_pre=m_norm_mix_pre, norm_mix_post=m_norm_mix_post,
               norm_ffn_pre=m_norm_ffn_pre, norm_ffn_post=m_norm_ffn_post)
    var = dict(ssd_w_in=v_ssd_w_in, ssd_conv_w=v_ssd_conv_w, ssd_conv_b=v_ssd_conv_b, ssd_dt_bias=v_ssd_dt_bias,
               ssd_a_log=v_ssd_a_log, ssd_d=v_ssd_d, ssd_norm_w=v_ssd_norm_w, ssd_w_out=v_ssd_w_out, pool_w=v_pool_w,
               pool_scale=v_pool_scale, ffn_w_up=v_ffn_w_up, ffn_conv_w=v_ffn_conv_w, ffn_conv_b=v_ffn_conv_b,
               ffn_w_down=v_ffn_w_down, norm_mix_pre=v_norm_mix_pre, norm_mix_post=v_norm_mix_post,
               norm_ffn_pre=v_norm_ffn_pre, norm_ffn_post=v_norm_ffn_post)

    bl, seq, d = x.shape
    t = bl * seq
    depth = norm_mix_pre.shape[0]
    n_ssd = ssd_w_out.shape[0]
    d_inner = ssd_w_out.shape[1] * N_CHIPS
    nheads = d_inner // HEAD_DIM
    hpg = nheads // N_GROUPS
    gw = d_inner // N_GROUPS
    xbc = ssd_conv_w.shape[2] * N_CHIPS
    f2 = ffn_w_up.shape[2] * N_CHIPS
    ff = f2 // 2
    dg = d // 4
    cy = lax.axis_index("c")
    chip = 2 * lax.axis_index("x") + lax.axis_index("y")

    small_shapes = [wts[n].shape for n, _ in SMALL]
    small_axes = [a for _, a in SMALL]
    small_flat = _flatten_shards([wts[n] for n, _ in SMALL], F32)
    small_half = lax.dynamic_index_in_dim(small_flat, cy, 0, keepdims=False)
    small_all = _allgather_halves(small_half, "gather_small")
    conv_w, p_scale, f_conv_w = _unflatten_full(small_all, small_shapes, small_axes)
    w_in_cm, w_out, w_pool, w_up, w_down = _gather_big([wts[n].astype(BF16) for n, _ in BIGW], "gather_big")
    w_in = jnp.swapaxes(w_in_cm, 1, 2).reshape(n_ssd, d, -1)

    def pad_heads(a):
        lead = a.shape[:-1]
        a = a.reshape(lead + (N_GROUPS, hpg))
        a = jnp.pad(a, [(0, 0)] * len(lead) + [(0, 0), (0, LANES - hpg)])
        return a.reshape(lead + (N_GROUPS * LANES,))

    def unpad_heads(a):
        lead = a.shape[:-1]
        return a.reshape(lead + (N_GROUPS, LANES))[..., :hpg].reshape(lead + (nheads,))

    def group_rows(a, width):
        return jnp.broadcast_to(a.reshape(N_GROUPS, 1, width), (N_GROUPS, 8, width))

    w_in_p = jnp.concatenate([w_in[..., :d_inner + xbc], pad_heads(w_in[..., d_inner + xbc:])], axis=-1)
    zw = w_in_p.shape[-1]

    x2 = x.reshape(t, d)
    tgt2 = loss_target.reshape(t, d)

    saved = []
    cur = x2
    for i in range(depth):
        j = i // 2
        sv = dict(x_in=cur)
        if i % 2 == 0:
            h = _norm_fwd(cur, norm_mix_pre[i:i + 1], BF16, f"norm_pre_b")
            zx = _mm(h, w_in_p, "nn", F32, "mm_ssd_in", 2048, 512, d, b_layer=j).reshape(bl, seq, zw)
            xc = _ssd_conv_fwd(zx, conv_w[j], ssd_conv_b[j:j + 1], d_inner, "ssd_conv_fwd")
            dtb = group_rows(pad_heads(ssd_dt_bias[j]), LANES)
            alog = group_rows(pad_heads(ssd_a_log[j]), LANES)
            dskip = group_rows(jnp.repeat(ssd_d[j], HEAD_DIM), gw)
            nw = group_rows(ssd_norm_w[j], gw)
            y, yn, st = _ssd_fwd(xc, zx, dtb, alog, dskip, nw, d_inner, "ssd_fwd")
            mix = _mm(yn.reshape(t, d_inner), w_out, "nn", F32, "mm_ssd_out", 512, 512, d_inner, b_layer=j)
            sv.update(h=h, zx=zx, xc=xc, y=y, yn=yn, st=st, dtb=dtb, alog=alog, dskip=dskip, nw=nw)
        else:
            h = _norm_fwd(cur, norm_mix_pre[i:i + 1], F32, "norm_pre_f")
            mix = _pool_fwd(h.reshape(bl, seq, d), w_pool[j], p_scale[j:j + 1], "pool_fwd").reshape(t, d)
            sv.update(h=h)
        sv.update(mix=mix)
        mid = _norm_fwd(mix, norm_mix_post[i:i + 1], F32, "norm_post", resid=cur)
        u = _norm_fwd(mid, norm_ffn_pre[i:i + 1], BF16, "norm_pre_b")
        hpre = _mm(u, w_up, "nn", BF16, "mm_up", 2048, 512, d, b_layer=i).reshape(bl, seq, f2)
        act = _ffn_act_fwd(hpre, f_conv_w[i], ffn_conv_b[i:i + 1], "ffn_act_fwd").reshape(t, ff)
        fo = _mm(act, w_down, "nn", F32, "mm_down", 1024, 512, ff, b_layer=i)
        cur = _norm_fwd(fo, norm_ffn_post[i:i + 1], F32, "norm_post", resid=mid)
        sv.update(mid=mid, u=u, hpre=hpre, act=act, fo=fo)
        saved.append(sv)

    dcur, loss_part = _loss_head(cur, tgt2, "loss_head")

    g = {n: [None] * wts[n].shape[0] for n in WEIGHTS}
    g_w_up = lax.empty((depth, d, f2), F32)
    g_w_down = lax.empty((depth, ff, d), F32)
    g_w_out = lax.empty((n_ssd, d_inner, d), F32)
    g_w_in = lax.empty((n_ssd, d, zw), F32)
    for i in reversed(range(depth)):
        j = i // 2
        sv = saved[i]
        dfo, g["norm_ffn_post"][i] = _norm_bwd(sv["fo"], norm_ffn_post[i:i + 1], dcur, BF16, "norm_bwd_b")
        dact = _mm(dfo, w_down, "nt", BF16, "mm_down_dx", 1024, ff // 2, d, b_layer=i)
        g_w_down = _mm(sv["act"], dfo, "tn", F32, "mm_down_dw", ff // 2, 512, 2048, out_buf=(g_w_down, i))
        dhg, dhv, dcw, dcb = _ffn_act_bwd(sv["hpre"], dact.reshape(bl, seq, ff), f_conv_w[i], ffn_conv_b[i:i + 1],
                                          "ffn_act_bwd")
        g["ffn_conv_w"][i] = dcw
        g["ffn_conv_b"][i] = dcb[0]
        dhs = [dhg.reshape(t, ff), dhv.reshape(t, ff)]
        du = _mm(dhs, w_up, "nt", F32, "mm_up_dx", 1024, d, ff // 2, b_layer=i)
        g_w_up = _mm(sv["u"], dhs, "tn", F32, "mm_up_dw", 512, ff // 2, 2048, out_buf=(g_w_up, i))
        dmid, g["norm_ffn_pre"][i] = _norm_bwd(sv["mid"], norm_ffn_pre[i:i + 1], du, F32, "norm_bwd_r", resid=dcur)
        if i % 2 == 0:
            dmix, g["norm_mix_post"][i] = _norm_bwd(sv["mix"], norm_mix_post[i:i + 1], dmid, BF16, "norm_bwd_b")
            dyn = _mm(dmix, w_out, "nt", F32, "mm_ssd_out_dx", 1024, 1024, d, b_layer=j)
            g_w_out = _mm(sv["yn"].reshape(t, d_inner), dmix, "tn", F32, "mm_ssd_out_dw", 1024, 512, 2048,
                          out_buf=(g_w_out, j))
            dz, dxs, dbm, dcm, ddt, dnw, dd, dal, dbias = _ssd_bwd(
                sv["xc"], sv["zx"], sv["y"], dyn.reshape(bl, seq, d_inner), sv["st"], sv["dtb"], sv["alog"],
                sv["dskip"], sv["nw"], d_inner, "ssd_bwd")
            g["ssd_norm_w"][j] = dnw[:, 0, :].reshape(d_inner)
            g["ssd_d"][j] = dd[:, 0, :hpg].reshape(nheads)
            g["ssd_a_log"][j] = dal[:, 0, :hpg].reshape(nheads)
            g["ssd_dt_bias"][j] = dbias[:, 0, :hpg].reshape(nheads)
            dxbc, dcw, dcb = _ssd_conv_bwd(sv["zx"], (dxs, dbm, dcm), conv_w[j], ssd_conv_b[j:j + 1], d_inner,
                                           "ssd_conv_bwd")
            g["ssd_conv_w"][j] = dcw
            g["ssd_conv_b"][j] = dcb[0]
            dzs = [dz.reshape(t, d_inner), dxbc.reshape(t, xbc), ddt.reshape(t, N_GROUPS * LANES)]
            dh = _mm(dzs, w_in_p, "nt", F32, "mm_ssd_in_dx", 1024, d, 512, b_layer=j)
            g_w_in = _mm(sv["h"], dzs, "tn", F32, "mm_ssd_in_dw", 1024, 512, 2048, out_buf=(g_w_in, j))
        else:
            dmix, g["norm_mix_post"][i] = _norm_bwd(sv["mix"], norm_mix_post[i:i + 1], dmid, F32, "norm_bwd_f")
            dh3, g["pool_w"][j], dps = _pool_bwd(sv["h"].reshape(bl, seq, d), dmix.reshape(bl, seq, d), w_pool[j],
                                                 p_scale[j:j + 1], "pool_bwd")
            g["pool_scale"][j] = dps[0]
            dh = dh3.reshape(t, d)
        dcur, g["norm_mix_pre"][i] = _norm_bwd(sv["x_in"], norm_mix_pre[i:i + 1], dh, F32, "norm_bwd_r", resid=dmid)

    grad_x = dcur.reshape(bl, seq, d)
    for n in ("norm_mix_pre", "norm_mix_post", "norm_ffn_pre", "norm_ffn_post"):
        g[n] = [a[0] for a in g[n]]
    small_names = [n for n, _ in SMALL] + list(REPL)
    full = {n: jnp.stack(g[n], axis=0) for n in small_names}

    g_in = jnp.concatenate([g_w_in[..., :d_inner + xbc], unpad_heads(g_w_in[..., d_inner + xbc:])], axis=-1)
    g_in_cm = jnp.swapaxes(g_in.reshape(n_ssd, d, N_CHIPS, -1), 1, 2)
    gs = [g_in_cm, g_w_out, jnp.stack(g["pool_w"], axis=0), g_w_up, g_w_down]
    core = cy.reshape(1).astype(jnp.int32)
    got = _swap_grads(gs, "swap_grads")
    pair = [_pair_sum(a, r, core, "pair_sum_" + n) for a, r, (n, _) in zip(gs, got, BIGW)]
    parts = _scatter_grads(pair, "scatter_grads")
    halves = [_sum4(q, core, "sum4_" + n) for q, (n, _) in zip(parts, BIGW)]
    shards = _share_grads(halves, "share_grads")
    big_grads = {n: s for s, (n, _) in zip(shards, BIGW)}

    vec = jnp.concatenate([full[n].reshape(-1) for n in small_names] + [loss_part[0, :1]])
    nvec = vec.shape[0]
    vrows = 8 * ((nvec + 8 * FLAT_COLS - 1) // (8 * FLAT_COLS))
    vec = jnp.pad(vec, (0, vrows * FLAT_COLS - nvec)).reshape(vrows, FLAT_COLS)
    tot = _allreduce_small(vec, "allreduce_small").reshape(-1)
    small_grads, off = {}, 0
    for n in small_names:
        cnt = math.prod(full[n].shape)
        small_grads[n] = tot[off:off + cnt].reshape(full[n].shape)
        off += cnt
    loss = tot[off]
    for n, ax in SMALL:
        w = wts[n].shape[ax]
        small_grads[n] = lax.dynamic_slice_in_dim(small_grads[n], chip * w, w, axis=ax)

    grads, deltas, new_m, new_v = {}, {}, {}, {}
    for n in WEIGHTS:
        gr = big_grads[n] if n in big_grads else small_grads[n]
        shp = wts[n].shape
        two = (math.prod(shp[:-1]), shp[-1])
        dl, mn, vn = _adamw(wts[n].reshape(two), gr.reshape(two), mom[n].reshape(two), var[n].reshape(two),
                            "adamw_" + n)
        grads[n], deltas[n], new_m[n], new_v[n] = gr, dl.reshape(shp), mn.reshape(shp), vn.reshape(shp)

    return (loss, grad_x, *[grads[n] for n in WEIGHTS], *[deltas[n] for n in WEIGHTS],
            *[new_m[n] for n in WEIGHTS], *[new_v[n] for n in WEIGHTS])
```

```python
import functools
import math

import jax
import jax.numpy as jnp
from jax import lax
from jax.experimental import pallas as pl
from jax.experimental.pallas import tpu as pltpu

F32 = jnp.float32
BF16 = jnp.bfloat16
MESH = pl.DeviceIdType.MESH
ANY = pl.BlockSpec(memory_space=pl.ANY)

HEAD_DIM = 64
D_STATE = 128
CHUNK = 128
N_GROUPS = 4
SSD_CONV = 4
FFN_CONV = 3
EPS = 1e-6
N_CHIPS = 4
LANES = 128
FLAT_COLS = 1024

ADAM_LR = 0.001
ADAM_B1 = 0.9
ADAM_B2 = 0.999
ADAM_EPS = 1e-08
ADAM_WD = 0.01
ADAM_STEP = 10

VMEM_LIMIT_BYTES = 56 * 1024 * 1024


def _params(sem=None):
    kw = dict(vmem_limit_bytes=VMEM_LIMIT_BYTES)
    if sem is not None:
        kw["dimension_semantics"] = sem
    return pltpu.CompilerParams(**kw)


def _sigmoid(x):
    return 1.0 / (1.0 + jnp.exp(-x))


def _softplus(x):
    return jnp.maximum(x, 0.0) + jnp.log(1.0 + jnp.exp(-jnp.abs(x)))


def _dot(a, b, dn):
    return lax.dot_general(a, b, (dn, ((), ())), preferred_element_type=F32)


def _nn(a, b):
    return _dot(a, b, ((1,), (0,)))


def _nt(a, b):
    return _dot(a, b, ((1,), (1,)))


def _tn(a, b):
    return _dot(a, b, ((0,), (0,)))


def _split(x, parts):
    out = []
    r = x
    for _ in range(parts):
        p = r.astype(BF16)
        out.append(p)
        r = r - p.astype(F32)
    return out


def _sel_left(sel, x, parts=3):
    n = x.shape[1]
    r = _nn(sel, jnp.concatenate(_split(x, parts), axis=1))
    out = r[:, 0:n]
    for i in range(1, parts):
        out = out + r[:, i * n:(i + 1) * n]
    return out


def _sel_right(x, sel_stacked, parts=3):
    return _nn(jnp.concatenate(_split(x, parts), axis=1), sel_stacked)


def _mm(a, b, dims, out_dtype, name, tm, tn, tk, b_layer=None, out_buf=None):
    a_list = list(a) if isinstance(a, (list, tuple)) else [a]
    b_list = list(b) if isinstance(b, (list, tuple)) else [b]
    if dims in ("nn", "nt"):
        assert len(b_list) == 1
        m = a_list[0].shape[0]
        segs = [x.shape[1] for x in a_list]
        k = sum(segs)
        bshape = b_list[0].shape[-2:]
        n = bshape[1] if dims == "nn" else bshape[0]
        assert (bshape[0] if dims == "nn" else bshape[1]) == k
    else:
        assert len(a_list) == 1 and b_layer is None
        k, m = a_list[0].shape
        segs = [x.shape[1] for x in b_list]
        n = sum(segs)
    tm, tn, tk = min(tm, m), min(tn, n), min(tk, k)
    if dims == "tn":
        tn = min(tn, min(segs))
    else:
        tk = min(tk, min(segs))
    unit = tk if dims != "tn" else tn
    assert m % tm == 0 and n % tn == 0 and k % tk == 0 and all(s % unit == 0 for s in segs), (name, m, n, k, segs)
    nk = k // tk
    starts = [sum(segs[:s]) // unit for s in range(len(segs))]
    counts = [s // unit for s in segs]
    nseg = len(segs)
    dn = {"nn": ((1,), (0,)), "nt": ((1,), (1,)), "tn": ((0,), (0,))}[dims]

    def body(*refs):
        a_refs = refs[:len(a_list)]
        b_refs = refs[len(a_list):len(a_list) + len(b_list)]
        rest = refs[len(a_list) + len(b_list) + (0 if out_buf is None else 1):]
        o_ref = rest[0]
        acc = rest[1] if nk > 1 else None
        kk = pl.program_id(2)
        sel = kk if dims != "tn" else pl.program_id(1)

        def step(a_ref, b_ref):
            p = _dot(a_ref[...].astype(BF16), b_ref[...].astype(BF16), dn)
            if nk == 1:
                o_ref[...] = p.astype(out_dtype)
                return

            @pl.when(kk == 0)
            def _():
                acc[...] = p

            @pl.when(kk > 0)
            def _():
                acc[...] += p

        if nseg == 1:
            step(a_refs[0], b_refs[0])
        else:
            for s in range(nseg):
                @pl.when(jnp.logical_and(sel >= starts[s], sel < starts[s] + counts[s]))
                def _(s=s):
                    step(a_refs[s] if dims != "tn" else a_refs[0], b_refs[0] if dims != "tn" else b_refs[s])

        if nk > 1:
            @pl.when(kk == nk - 1)
            def _():
                o_ref[...] = acc[...].astype(out_dtype)

    def seg_index(v, s):
        return v if nseg == 1 else jnp.clip(v - starts[s], 0, counts[s] - 1)

    lead = () if b_layer is None else (b_layer,)
    none = () if b_layer is None else (None,)
    if dims == "nn":
        a_specs = [pl.BlockSpec((tm, tk), lambda i, j, kk, s=s: (i, seg_index(kk, s))) for s in range(nseg)]
        b_specs = [pl.BlockSpec(none + (tk, tn), lambda i, j, kk: lead + (kk, j))]
    elif dims == "nt":
        a_specs = [pl.BlockSpec((tm, tk), lambda i, j, kk, s=s: (i, seg_index(kk, s))) for s in range(nseg)]
        b_specs = [pl.BlockSpec(none + (tn, tk), lambda i, j, kk: lead + (j, kk))]
    else:
        a_specs = [pl.BlockSpec((tk, tm), lambda i, j, kk: (kk, i))]
        b_specs = [pl.BlockSpec((tk, tn), lambda i, j, kk, s=s: (kk, seg_index(j, s))) for s in range(nseg)]
    args = a_list + b_list
    in_specs = a_specs + b_specs
    aliases = {}
    if out_buf is None:
        out_shape = jax.ShapeDtypeStruct((m, n), out_dtype)
        out_spec = pl.BlockSpec((tm, tn), lambda i, j, kk: (i, j))
    else:
        buf, slab = out_buf
        assert buf.shape[1:] == (m, n) and buf.dtype == out_dtype
        out_shape = jax.ShapeDtypeStruct(buf.shape, out_dtype)
        out_spec = pl.BlockSpec((None, tm, tn), lambda i, j, kk: (slab, i, j))
        aliases = {len(args): 0}
        args = args + [buf]
        in_specs = in_specs + [ANY]
    return pl.pallas_call(
        body,
        out_shape=out_shape,
        grid=(m // tm, n // tn, nk),
        in_specs=in_specs,
        out_specs=out_spec,
        scratch_shapes=[] if nk == 1 else [pltpu.VMEM((tm, tn), F32)],
        input_output_aliases=aliases,
        compiler_params=_params(("parallel", "parallel", "arbitrary")),
        name=name,
    )(*args)


def _row_tile(t, want):
    tm = min(want, t)
    assert t % tm == 0
    return tm


def _norm_fwd(x, w, out_dtype, name, resid=None):
    t, d = x.shape
    tm = _row_tile(t, 512)

    def body(*refs):
        if resid is None:
            x_ref, w_ref, o_ref = refs
        else:
            x_ref, w_ref, r_ref, o_ref = refs
        xv = x_ref[...]
        r = lax.rsqrt(jnp.mean(xv * xv, axis=-1, keepdims=True) + EPS)
        y = (xv * r) * w_ref[...]
        if resid is not None:
            y = r_ref[...] + y
        o_ref[...] = y.astype(out_dtype)

    row = pl.BlockSpec((tm, d), lambda i: (i, 0))
    vec = pl.BlockSpec((1, d), lambda i: (0, 0))
    args = [x, w] + ([] if resid is None else [resid])
    return pl.pallas_call(
        body, out_shape=jax.ShapeDtypeStruct((t, d), out_dtype), grid=(t // tm,),
        in_specs=[row, vec] + ([] if resid is None else [row]), out_specs=row,
        compiler_params=_params(("parallel",)), name=name)(*args)


def _norm_bwd(src, w, dy, out_dtype, name, resid=None):
    t, d = src.shape
    tm = _row_tile(t, 512)

    def body(*refs):
        if resid is None:
            x_ref, w_ref, g_ref, o_ref, dw_ref = refs
        else:
            x_ref, w_ref, g_ref, r_ref, o_ref, dw_ref = refs
        xv = x_ref[...]
        g = g_ref[...].astype(F32)
        r = lax.rsqrt(jnp.mean(xv * xv, axis=-1, keepdims=True) + EPS)
        xh = xv * r
        gh = g * w_ref[...]
        mean = jnp.mean(gh * xh, axis=-1, keepdims=True)
        dx = r * (gh - xh * mean)
        if resid is not None:
            dx = r_ref[...] + dx
        o_ref[...] = dx.astype(out_dtype)
        part = jnp.sum(g * xh, axis=0, keepdims=True)

        @pl.when(pl.program_id(0) == 0)
        def _():
            dw_ref[...] = part

        @pl.when(pl.program_id(0) > 0)
        def _():
            dw_ref[...] += part

    row = pl.BlockSpec((tm, d), lambda i: (i, 0))
    vec = pl.BlockSpec((1, d), lambda i: (0, 0))
    args = [src, w, dy] + ([] if resid is None else [resid])
    return pl.pallas_call(
        body,
        out_shape=(jax.ShapeDtypeStruct((t, d), out_dtype), jax.ShapeDtypeStruct((1, d), F32)),
        grid=(t // tm,),
        in_specs=[row, vec, row] + ([] if resid is None else [row]),
        out_specs=(row, vec),
        compiler_params=_params(("arbitrary",)), name=name)(*args)


def _loss_head(y, target, name):
    t, d = y.shape
    tm = _row_tile(t, 512)

    def body(y_ref, t_ref, dy_ref, l_ref):
        e = y_ref[...] - t_ref[...]
        dy_ref[...] = e * (1.0 / d)
        col = jnp.sum(e * e, axis=0, keepdims=True)
        s = jnp.sum(col, axis=1, keepdims=True) * (0.5 / d)
        part = jnp.broadcast_to(s, (1, LANES))

        @pl.when(pl.program_id(0) == 0)
        def _():
            l_ref[...] = part

        @pl.when(pl.program_id(0) > 0)
        def _():
            l_ref[...] += part

    row = pl.BlockSpec((tm, d), lambda i: (i, 0))
    return pl.pallas_call(
        body,
        out_shape=(jax.ShapeDtypeStruct((t, d), F32), jax.ShapeDtypeStruct((1, LANES), F32)),
        grid=(t // tm,), in_specs=[row, row],
        out_specs=(row, pl.BlockSpec((1, LANES), lambda i: (0, 0))),
        compiler_params=_params(("arbitrary",)), name=name)(y, target)


def _window(ref, c, rows, seq, before, after):
    r0 = pl.multiple_of(c * rows, rows)
    parts = []
    if before:
        h0 = pl.multiple_of(jnp.maximum(r0 - before, 0), before)
        halo = ref[pl.ds(h0, before), :].astype(F32)
        parts.append(jnp.where(c > 0, halo, 0.0))
    parts.append(ref[pl.ds(r0, rows), :].astype(F32))
    if after:
        h1 = pl.multiple_of(jnp.minimum(r0 + rows, seq - after), after)
        halo = ref[pl.ds(h1, after), :].astype(F32)
        parts.append(jnp.where(c < seq // rows - 1, halo, 0.0))
    return parts[0] if len(parts) == 1 else jnp.concatenate(parts, axis=0)


def _lag(x, k):
    return pltpu.roll(x, k, 0) if k else x


def _lead(x, k):
    return pltpu.roll(x, x.shape[0] - k, 0) if k else x


SHIFT_ROWS = 128
SHIFT_COLS = 256


HALO = 16


def _conv3(ext, w, bias):
    acc = bias + w[2:3, :] * ext[HALO:, :]
    acc = acc + w[1:2, :] * _lag(ext, 1)[HALO:, :]
    return acc + w[0:1, :] * _lag(ext, 2)[HALO:, :]


def _ffn_act_fwd(hpre, cw, cb, name):
    b, seq, f2 = hpre.shape
    cbk = SHIFT_COLS
    nj = f2 // (2 * cbk)
    rows = min(SHIFT_ROWS, seq)

    def body(g_ref, v_ref, wg_ref, wv_ref, bg_ref, bv_ref, o_ref):
        def chunk(c, carry):
            gate = _conv3(_window(g_ref, c, rows, seq, HALO, 0), wg_ref[...], bg_ref[...])
            val = _conv3(_window(v_ref, c, rows, seq, HALO, 0), wv_ref[...], bv_ref[...])
            a = gate * _sigmoid(gate) * val
            o_ref[pl.ds(pl.multiple_of(c * rows, rows), rows), :] = a.astype(BF16)
            return carry

        lax.fori_loop(0, seq // rows, chunk, 0)

    blk = lambda off: pl.BlockSpec((None, seq, cbk), lambda i, j: (i, 0, j + off))
    wsp = lambda r, off: pl.BlockSpec((r, cbk), lambda i, j: (0, j + off))
    return pl.pallas_call(
        body, out_shape=jax.ShapeDtypeStruct((b, seq, f2 // 2), BF16), grid=(b, nj),
        in_specs=[blk(0), blk(nj), wsp(FFN_CONV, 0), wsp(FFN_CONV, nj), wsp(1, 0), wsp(1, nj)],
        out_specs=blk(0),
        compiler_params=_params(("parallel", "parallel")), name=name)(hpre, hpre, cw, cw, cb, cb)


def _ffn_act_bwd(hpre, da, cw, cb, name):
    b, seq, f2 = hpre.shape
    cbk = SHIFT_COLS
    nj = f2 // (2 * cbk)
    rows = min(SHIFT_ROWS, seq)

    def body(g_ref, v_ref, da_ref, wg_ref, wv_ref, bg_ref, bv_ref, og_ref, ov_ref, dwg_ref, dwv_ref, dbg_ref, dbv_ref):
        wg, wv = wg_ref[...], wv_ref[...]

        def shifted(ref, c):
            ext = _window(ref, c, rows, seq, HALO, HALO)
            return [_lag(ext, k)[HALO:, :] for k in range(FFN_CONV)]

        def back(dpre, w, o_ref, c, xs, carry):
            dx = w[2:3, :] * dpre + w[1:2, :] * _lead(dpre, 1) + w[0:1, :] * _lead(dpre, 2)
            o_ref[pl.ds(pl.multiple_of(c * rows, rows), rows), :] = dx[:rows, :].astype(BF16)
            dp = dpre[:rows, :]
            return tuple(carry[k] + jnp.sum(dp * xs[k][:rows, :], axis=0, keepdims=True) for k in range(FFN_CONV)) + (
                carry[FFN_CONV] + jnp.sum(dp, axis=0, keepdims=True),)

        def chunk(c, carry):
            cg, cv = carry
            gs, vs = shifted(g_ref, c), shifted(v_ref, c)
            gate = bg_ref[...] + wg[2:3, :] * gs[0] + wg[1:2, :] * gs[1] + wg[0:1, :] * gs[2]
            val = bv_ref[...] + wv[2:3, :] * vs[0] + wv[1:2, :] * vs[1] + wv[0:1, :] * vs[2]
            dav = _window(da_ref, c, rows, seq, 0, HALO)
            sg = _sigmoid(gate)
            cg = back(dav * val * (sg * (1.0 + gate * (1.0 - sg))), wg, og_ref, c, gs, cg)
            cv = back(dav * (gate * sg), wv, ov_ref, c, vs, cv)
            return cg, cv

        z = jnp.zeros((1, cbk), F32)
        cg, cv = lax.fori_loop(0, seq // rows, chunk, ((z,) * (FFN_CONV + 1), (z,) * (FFN_CONV + 1)))
        dwg = jnp.concatenate([cg[2], cg[1], cg[0]], axis=0)
        dwv = jnp.concatenate([cv[2], cv[1], cv[0]], axis=0)

        @pl.when(pl.program_id(1) == 0)
        def _():
            dwg_ref[...] = dwg
            dwv_ref[...] = dwv
            dbg_ref[...] = cg[FFN_CONV]
            dbv_ref[...] = cv[FFN_CONV]

        @pl.when(pl.program_id(1) > 0)
        def _():
            dwg_ref[...] += dwg
            dwv_ref[...] += dwv
            dbg_ref[...] += cg[FFN_CONV]
            dbv_ref[...] += cv[FFN_CONV]

    blk = lambda off: pl.BlockSpec((None, seq, cbk), lambda j, i: (i, 0, j + off))
    wsp = lambda r, off: pl.BlockSpec((r, cbk), lambda j, i: (0, j + off))
    half = jax.ShapeDtypeStruct((b, seq, f2 // 2), BF16)
    dwshape = jax.ShapeDtypeStruct((FFN_CONV, f2 // 2), F32)
    dbshape = jax.ShapeDtypeStruct((1, f2 // 2), F32)
    dg, dv, dwg, dwv, dbg, dbv = pl.pallas_call(
        body,
        out_shape=(half, half, dwshape, dwshape, dbshape, dbshape),
        grid=(nj, b),
        in_specs=[blk(0), blk(nj), blk(0), wsp(FFN_CONV, 0), wsp(FFN_CONV, nj), wsp(1, 0), wsp(1, nj)],
        out_specs=(blk(0), blk(0), wsp(FFN_CONV, 0), wsp(FFN_CONV, 0), wsp(1, 0), wsp(1, 0)),
        compiler_params=_params(("parallel", "arbitrary")), name=name)(hpre, hpre, da, cw, cw, cb, cb)
    return dg, dv, jnp.concatenate([dwg, dwv], axis=1), jnp.concatenate([dbg, dbv], axis=1)


def _ssd_conv_fwd(zx, cw, cb, d_inner, name):
    b, seq, _ = zx.shape
    xbc = cw.shape[1]
    cbk = SHIFT_COLS
    off = d_inner // cbk
    rows = min(SHIFT_ROWS, seq)

    def body(h_ref, w_ref, b_ref, o_ref):
        w = w_ref[...]
        bias = b_ref[...]

        def chunk(c, carry):
            ext = _window(h_ref, c, rows, seq, 8, 0)
            acc = bias + w[3:4, :] * ext[8:, :]
            for k in range(1, SSD_CONV):
                acc = acc + w[3 - k:4 - k, :] * _lag(ext, k)[8:, :]
            o_ref[pl.ds(pl.multiple_of(c * rows, rows), rows), :] = acc * _sigmoid(acc)
            return carry

        lax.fori_loop(0, seq // rows, chunk, 0)

    return pl.pallas_call(
        body, out_shape=jax.ShapeDtypeStruct((b, seq, xbc), F32), grid=(b, xbc // cbk),
        in_specs=[pl.BlockSpec((None, seq, cbk), lambda i, j: (i, 0, j + off)),
                  pl.BlockSpec((SSD_CONV, cbk), lambda i, j: (0, j)),
                  pl.BlockSpec((1, cbk), lambda i, j: (0, j))],
        out_specs=pl.BlockSpec((None, seq, cbk), lambda i, j: (i, 0, j)),
        compiler_params=_params(("parallel", "parallel")), name=name)(zx, cw, cb)


def _ssd_conv_bwd(zx, dparts, cw, cb, d_inner, name):
    b, seq, _ = zx.shape
    xbc = cw.shape[1]
    cbk = SHIFT_COLS
    off = d_inner // cbk
    rows = min(SHIFT_ROWS, seq)
    nblk = [p.shape[2] // cbk for p in dparts]
    first = [sum(nblk[:s]) for s in range(len(dparts))]
    assert sum(nblk) == xbc // cbk

    def body(h_ref, gx_ref, gb_ref, gc_ref, w_ref, b_ref, o_ref, dw_ref, db_ref):
        w = w_ref[...]
        bias = b_ref[...]
        j = pl.program_id(0)

        def chunk(c, carry):
            dws, dbias = carry
            ext = _window(h_ref, c, rows, seq, 8, 8)
            xs = [_lag(ext, k)[8:, :] for k in range(SSD_CONV)]
            pre = bias + w[3:4, :] * xs[0]
            for k in range(1, SSD_CONV):
                pre = pre + w[3 - k:4 - k, :] * xs[k]
            s = _sigmoid(pre)
            gsel = jnp.where(j < first[1], _window(gx_ref, c, rows, seq, 0, 8),
                             jnp.where(j < first[2], _window(gb_ref, c, rows, seq, 0, 8),
                                       _window(gc_ref, c, rows, seq, 0, 8)))
            dpre = gsel * (s * (1.0 + pre * (1.0 - s)))
            dx = w[3:4, :] * dpre
            for k in range(1, SSD_CONV):
                dx = dx + w[3 - k:4 - k, :] * _lead(dpre, k)
            o_ref[pl.ds(pl.multiple_of(c * rows, rows), rows), :] = dx[:rows, :].astype(BF16)
            dp = dpre[:rows, :]
            dws = tuple(dws[k] + jnp.sum(dp * xs[k][:rows, :], axis=0, keepdims=True) for k in range(SSD_CONV))
            dbias = dbias + jnp.sum(dp, axis=0, keepdims=True)
            return dws, dbias

        z = jnp.zeros((1, cbk), F32)
        dws, dbias = lax.fori_loop(0, seq // rows, chunk, ((z,) * SSD_CONV, z))
        dwv = jnp.concatenate([dws[3 - i] for i in range(SSD_CONV)], axis=0)

        @pl.when(pl.program_id(1) == 0)
        def _():
            dw_ref[...] = dwv
            db_ref[...] = dbias

        @pl.when(pl.program_id(1) > 0)
        def _():
            dw_ref[...] += dwv
            db_ref[...] += dbias

    return pl.pallas_call(
        body,
        out_shape=(jax.ShapeDtypeStruct((b, seq, xbc), BF16), jax.ShapeDtypeStruct((SSD_CONV, xbc), F32),
                   jax.ShapeDtypeStruct((1, xbc), F32)),
        grid=(xbc // cbk, b),
        in_specs=[pl.BlockSpec((None, seq, cbk), lambda j, i: (i, 0, j + off))] + [
                  pl.BlockSpec((None, seq, cbk), lambda j, i, s=s: (i, 0, jnp.clip(j - first[s], 0, nblk[s] - 1)))
                  for s in range(3)] + [
                  pl.BlockSpec((SSD_CONV, cbk), lambda j, i: (0, j)),
                  pl.BlockSpec((1, cbk), lambda j, i: (0, j))],
        out_specs=(pl.BlockSpec((None, seq, cbk), lambda j, i: (i, 0, j)),
                   pl.BlockSpec((SSD_CONV, cbk), lambda j, i: (0, j)),
                   pl.BlockSpec((1, cbk), lambda j, i: (0, j))),
        compiler_params=_params(("parallel", "arbitrary")), name=name)(zx, *dparts, cw, cb)


def _pool_sums(q, g, lead):
    sh = _lead if lead else _lag
    s2 = q + sh(q, 1)
    s4 = s2 + sh(s2, 2)
    s8 = s4 + sh(s4, 4)
    s16 = s8 + sh(s8, 8)
    return jnp.where(g == 0, s2, jnp.where(g == 1, s4, jnp.where(g == 2, s8, s16)))


def _pool_count(r0, n, g, shape):
    t = (r0 + lax.broadcasted_iota(jnp.int32, shape, 0) + 1).astype(F32)
    return jnp.minimum(t, (2 << g).astype(F32))


def _pool_fwd(h, pw, scale, name):
    b, seq, d = h.shape
    dg = d // 4
    rows = min(SHIFT_ROWS, seq)

    def body(h_ref, w_ref, s_ref, o_ref):
        g = pl.program_id(1)
        wmat = w_ref[...]
        sc = s_ref[...]

        def chunk(c, carry):
            r0 = c * rows
            ext = _window(h_ref, c, rows, seq, 16, 0)
            sums = _pool_sums(ext, g, False)[16:, :]
            mixed = sums / _pool_count(r0, rows, g, (rows, dg)) - ext[16:, :]
            o_ref[pl.ds(pl.multiple_of(r0, rows), rows), :] = _nn(mixed.astype(BF16), wmat) * sc
            return carry

        lax.fori_loop(0, seq // rows, chunk, 0)

    return pl.pallas_call(
        body, out_shape=jax.ShapeDtypeStruct((b, seq, d), F32), grid=(b, 4),
        in_specs=[pl.BlockSpec((None, seq, dg), lambda i, g: (i, 0, g)),
                  pl.BlockSpec((None, dg, dg), lambda i, g: (g, 0, 0)),
                  pl.BlockSpec((1, dg), lambda i, g: (0, g))],
        out_specs=pl.BlockSpec((None, seq, dg), lambda i, g: (i, 0, g)),
        compiler_params=_params(("parallel", "parallel")), name=name)(h, pw, scale)


def _pool_bwd(h, dout, pw, scale, name):
    b, seq, d = h.shape
    dg = d // 4
    rows = min(SHIFT_ROWS, seq)

    def body(h_ref, g_ref, w_ref, s_ref, o_ref, dw_ref, ds_ref, dw_acc):
        g = pl.program_id(0)
        wmat = w_ref[...]
        sc = s_ref[...]
        dw_acc[...] = jnp.zeros_like(dw_acc)

        def chunk(c, dsc):
            r0 = c * rows
            ext = _window(h_ref, c, rows, seq, 16, 0)
            sums = _pool_sums(ext, g, False)[16:, :]
            mixed = (sums / _pool_count(r0, rows, g, (rows, dg)) - ext[16:, :]).astype(BF16)
            gext = _window(g_ref, c, rows, seq, 0, 16)
            dsc = dsc + jnp.sum(gext[:rows, :] * _nn(mixed, wmat), axis=0, keepdims=True)
            dpre = (gext * sc).astype(BF16)
            dw_acc[...] += _tn(mixed, dpre[:rows, :])
            dmix = _nt(dpre, wmat)
            q = dmix / _pool_count(r0, rows + 16, g, (rows + 16, dg))
            back = _pool_sums(q, g, True)
            o_ref[pl.ds(pl.multiple_of(r0, rows), rows), :] = back[:rows, :] - dmix[:rows, :]
            return dsc

        dsc = lax.fori_loop(0, seq // rows, chunk, jnp.zeros((1, dg), F32))

        @pl.when(pl.program_id(1) == 0)
        def _():
            dw_ref[...] = dw_acc[...]
            ds_ref[...] = dsc

        @pl.when(pl.program_id(1) > 0)
        def _():
            dw_ref[...] += dw_acc[...]
            ds_ref[...] += dsc

    return pl.pallas_call(
        body,
        out_shape=(jax.ShapeDtypeStruct((b, seq, d), F32), jax.ShapeDtypeStruct((4, dg, dg), F32),
                   jax.ShapeDtypeStruct((1, d), F32)),
        grid=(4, b),
        in_specs=[pl.BlockSpec((None, seq, dg), lambda g, i: (i, 0, g)),
                  pl.BlockSpec((None, seq, dg), lambda g, i: (i, 0, g)),
                  pl.BlockSpec((None, dg, dg), lambda g, i: (g, 0, 0)),
                  pl.BlockSpec((1, dg), lambda g, i: (0, g))],
        out_specs=(pl.BlockSpec((None, seq, dg), lambda g, i: (i, 0, g)),
                   pl.BlockSpec((None, dg, dg), lambda g, i: (g, 0, 0)),
                   pl.BlockSpec((1, dg), lambda g, i: (0, g))),
        scratch_shapes=[pltpu.VMEM((dg, dg), F32)],
        compiler_params=_params(("parallel", "arbitrary")), name=name)(h, dout, pw, scale)


def _head_of(channel):
    return jnp.right_shift(channel, HEAD_DIM.bit_length() - 1)


def _ssd_consts(gw):
    q = CHUNK
    row = lax.broadcasted_iota(jnp.int32, (q, q), 0)
    col = lax.broadcasted_iota(jnp.int32, (q, q), 1)
    tril = (row >= col).astype(BF16)
    triu = (row <= col).astype(BF16)
    e = (_head_of(lax.broadcasted_iota(jnp.int32, (LANES, gw), 1))
         == lax.broadcasted_iota(jnp.int32, (LANES, gw), 0)).astype(BF16)
    et = (_head_of(lax.broadcasted_iota(jnp.int32, (gw, LANES), 0))
          == lax.broadcasted_iota(jnp.int32, (gw, LANES), 1)).astype(BF16)
    return row, col, tril, triu, e, et


def _ssd_common(dtr, dtb, alog, gw):
    q = CHUNK
    row, col, tril, triu, e, et = _ssd_consts(gw)
    dt = _softplus(dtr + dtb)
    a_row = -jnp.exp(alog)
    acum = _sel_left(tril, dt * a_row)
    ac_last = jnp.sum(jnp.where(row == q - 1, acum, 0.0), axis=0, keepdims=True)
    eac = jnp.exp(acum)
    de = jnp.exp(ac_last - acum)
    e2 = jnp.concatenate([e, e], axis=0)
    expand = _sel_right(jnp.concatenate([dt, eac, de], axis=0), e2, 2)
    dt_x, eac_x, de_x = expand[0:q], expand[q:2 * q], expand[2 * q:3 * q]
    acum_t = acum.T
    cd_col = jnp.exp(acum_t[:, q - 1:q])
    et3 = jnp.concatenate([et, et, et], axis=1)
    cdmat = _nn(et3, jnp.concatenate(_split(jnp.broadcast_to(cd_col, (LANES, D_STATE)), 3), axis=0))
    consts = dict(row=row, col=col, tril=tril, triu=triu, e=e, et=et)
    return dt, a_row, acum, acum_t, ac_last, eac, de, dt_x, eac_x, de_x, cdmat, consts


def _decay(acum, acum_t, j, row, col):
    diff = acum[:, j:j + 1] - acum_t[j:j + 1, :]
    return jnp.exp(jnp.where(row >= col, diff, -1e30))


def _ssd_fwd(xc, zx, dtb, alog, dskip, nw, d_inner, name):
    b, seq, xbc = xc.shape
    q = CHUNK
    nc = seq // q
    gw = d_inner // N_GROUPS
    nh = gw // HEAD_DIM
    xb0 = d_inner // D_STATE
    xc0 = xb0 + N_GROUPS
    dt0 = (d_inner + xbc) // LANES

    def body(x_ref, b_ref, c_ref, z_ref, dtr_ref, dtb_ref, al_ref, dsk_ref, nw_ref, y_ref, yn_ref, st_ref, s_ref):
        @pl.when(pl.program_id(2) == 0)
        def _():
            s_ref[...] = jnp.zeros_like(s_ref)

        prev = s_ref[...]
        st_ref[...] = prev
        x = x_ref[...]
        bm = b_ref[...].astype(BF16)
        cm = c_ref[...].astype(BF16)
        (dt, a_row, acum, acum_t, ac_last, eac, de, dt_x, eac_x, de_x, cdmat, k) = _ssd_common(
            dtr_ref[...], dtb_ref[0:1, :], al_ref[0:1, :], gw)
        xdt = x * dt_x
        xdt_b = xdt.astype(BF16)
        cb = _nt(cm, bm)
        half = _head_of(lax.broadcasted_iota(jnp.int32, (q, LANES), 1))
        pairs = []
        for j in range(nh):
            pc = (j // 2) * LANES
            m = (cb * _decay(acum, acum_t, j, k["row"], k["col"])).astype(BF16)
            yj = jnp.where(half == j % 2, _nn(m, xdt_b[:, pc:pc + LANES]), 0.0)
            if j % 2 == 0:
                pairs.append(yj)
            else:
                pairs[-1] = pairs[-1] + yj
        prev_b = prev.astype(BF16)
        y = dsk_ref[0:1, :] * x + jnp.concatenate(pairs, axis=1) + eac_x * _nt(cm, prev_b)
        s_ref[...] = cdmat * prev + _tn((xdt * de_x).astype(BF16), bm)
        y_ref[...] = y
        z = z_ref[...]
        yg = y * (z * _sigmoid(z))
        r = lax.rsqrt(jnp.mean(yg * yg, axis=-1, keepdims=True) + EPS)
        yn_ref[...] = ((yg * r) * nw_ref[0:1, :]).astype(BF16)

    par = lambda w: pl.BlockSpec((None, 8, w), lambda i, g, c: (g, 0, 0))
    return pl.pallas_call(
        body,
        out_shape=(jax.ShapeDtypeStruct((b, seq, d_inner), F32), jax.ShapeDtypeStruct((b, seq, d_inner), BF16),
                   jax.ShapeDtypeStruct((b, nc, N_GROUPS, gw, D_STATE), F32)),
        grid=(b, N_GROUPS, nc),
        in_specs=[pl.BlockSpec((None, q, gw), lambda i, g, c: (i, c, g)),
                  pl.BlockSpec((None, q, D_STATE), lambda i, g, c: (i, c, xb0 + g)),
                  pl.BlockSpec((None, q, D_STATE), lambda i, g, c: (i, c, xc0 + g)),
                  pl.BlockSpec((None, q, gw), lambda i, g, c: (i, c, g)),
                  pl.BlockSpec((None, q, LANES), lambda i, g, c: (i, c, dt0 + g)),
                  par(LANES), par(LANES), par(gw), par(gw)],
        out_specs=(pl.BlockSpec((None, q, gw), lambda i, g, c: (i, c, g)),
                   pl.BlockSpec((None, q, gw), lambda i, g, c: (i, c, g)),
                   pl.BlockSpec((None, None, None, gw, D_STATE), lambda i, g, c: (i, c, g, 0, 0))),
        scratch_shapes=[pltpu.VMEM((gw, D_STATE), F32)],
        compiler_params=_params(("parallel", "parallel", "arbitrary")), name=name,
    )(xc, xc, xc, zx, zx, dtb, alog, dskip, nw)


def _ssd_bwd(xc, zx, y, dyn, st, dtb, alog, dskip, nw, d_inner, name):
    b, seq, xbc = xc.shape
    q = CHUNK
    nc = seq // q
    gw = d_inner // N_GROUPS
    nh = gw // HEAD_DIM
    xb0 = d_inner // D_STATE
    xc0 = xb0 + N_GROUPS
    dt0 = (d_inner + xbc) // LANES

    def body(x_ref, b_ref, c_ref, z_ref, dtr_ref, y_ref, g_ref, st_ref, dtb_ref, al_ref, dsk_ref, nw_ref,
             dz_ref, dx_ref, db_ref, dc_ref, ddt_ref, dnw_ref, dd_ref, dal_ref, dbias_ref,
             ds_ref, colbuf, rowbuf):
        first = jnp.logical_and(pl.program_id(1) == 0, pl.program_id(2) == 0)

        @pl.when(pl.program_id(2) == 0)
        def _():
            ds_ref[...] = jnp.zeros_like(ds_ref)

        x = x_ref[...]
        bm = b_ref[...].astype(BF16)
        cm = c_ref[...].astype(BF16)
        z = z_ref[...]
        y = y_ref[...]
        prev = st_ref[...]
        dtr = dtr_ref[...] + dtb_ref[0:1, :]
        (dt, a_row, acum, acum_t, ac_last, eac, de, dt_x, eac_x, de_x, cdmat, k) = _ssd_common(
            dtr_ref[...], dtb_ref[0:1, :], al_ref[0:1, :], gw)
        row, col = k["row"], k["col"]
        et2 = jnp.concatenate([k["et"], k["et"]], axis=0)

        sz = _sigmoid(z)
        silu_z = z * sz
        yg = y * silu_z
        r = lax.rsqrt(jnp.mean(yg * yg, axis=-1, keepdims=True) + EPS)
        xh = yg * r
        dyn = g_ref[...]
        gh = dyn * nw_ref[0:1, :]
        dyg = r * (gh - xh * jnp.mean(gh * xh, axis=-1, keepdims=True))
        dnw = jnp.sum(dyn * xh, axis=0, keepdims=True)
        g = dyg * silu_z
        dz_ref[...] = (dyg * y * (sz * (1.0 + z * (1.0 - sz)))).astype(BF16)
        dd = _sel_right(jnp.broadcast_to(jnp.sum(g * x, axis=0, keepdims=True), (8, gw)), et2, 2)

        xdt = x * dt_x
        xdt_b = xdt.astype(BF16)
        g_b = g.astype(BF16)
        prev_b = prev.astype(BF16)
        cb = _nt(cm, bm)

        cp = _nt(cm, prev_b)
        ge = g * eac_x
        dac = _sel_right(ge * cp, et2, 2)
        ge_b = ge.astype(BF16)
        dcm = _nn(ge_b, prev_b)
        dprev = _tn(ge_b, cm)

        colbuf[...] = jnp.zeros_like(colbuf)
        rowbuf[...] = jnp.zeros_like(rowbuf)
        dcb = jnp.zeros((q, q), F32)
        half = _head_of(lax.broadcasted_iota(jnp.int32, (q, LANES), 1))
        pairs = []
        for j in range(nh):
            pc = (j // 2) * LANES
            dec = _decay(acum, acum_t, j, row, col)
            m = cb * dec
            gj = jnp.where(half == j % 2, g[:, pc:pc + LANES], 0.0).astype(BF16)
            dm = _nt(gj, xdt_b[:, pc:pc + LANES])
            w = dm * m
            colbuf[:, j:j + 1] = jnp.sum(w, axis=1, keepdims=True)
            rowbuf[j:j + 1, :] = jnp.sum(w, axis=0, keepdims=True)
            dcb = dcb + dm * dec
            dj = jnp.where(half == j % 2, _tn(m.astype(BF16), g_b[:, pc:pc + LANES]), 0.0)
            if j % 2 == 0:
                pairs.append(dj)
            else:
                pairs[-1] = pairs[-1] + dj
        dxdt = jnp.concatenate(pairs, axis=1)
        dcb_b = dcb.astype(BF16)
        dcm = dcm + _nn(dcb_b, bm)
        dbm = _tn(dcb_b, cm)

        ds = ds_ref[...]
        ds_b = ds.astype(BF16)
        u = _nt(bm, ds_b)
        dxdt = dxdt + u * de_x
        dde = _sel_right(u * xdt, et2, 2)
        dbm = dbm + _nn((xdt * de_x).astype(BF16), ds_b)
        pm = jnp.concatenate(_split(ds * prev, 2), axis=1)
        t2 = _tn(pm, k["et"])
        dcd_row = jnp.sum(t2[0:D_STATE] + t2[D_STATE:2 * D_STATE], axis=0, keepdims=True)
        last = dcd_row * jnp.exp(ac_last) + jnp.sum(dde * de, axis=0, keepdims=True)
        dac = dac + colbuf[...] - rowbuf[...].T - dde * de + jnp.where(row == q - 1, last, 0.0)
        ds_ref[...] = cdmat * ds + dprev

        dadt = _sel_left(k["triu"], dac)
        ddt = _sel_right(dxdt * x, et2, 2) + dadt * a_row
        dal = jnp.sum(dadt * dt, axis=0, keepdims=True) * a_row
        lane = lax.broadcasted_iota(jnp.int32, (q, LANES), 1)
        ddtr = jnp.where(lane < nh, ddt * _sigmoid(dtr), 0.0)
        ddt_ref[...] = ddtr.astype(BF16)
        dbias = jnp.sum(ddtr, axis=0, keepdims=True)
        dx_ref[...] = dxdt * dt_x + dsk_ref[0:1, :] * g
        db_ref[...] = dbm
        dc_ref[...] = dcm

        @pl.when(first)
        def _():
            dnw_ref[...] = jnp.broadcast_to(dnw, (8, gw))
            dd_ref[...] = dd
            dal_ref[...] = jnp.broadcast_to(dal, (8, LANES))
            dbias_ref[...] = jnp.broadcast_to(dbias, (8, LANES))

        @pl.when(jnp.logical_not(first))
        def _():
            dnw_ref[...] += jnp.broadcast_to(dnw, (8, gw))
            dd_ref[...] += dd
            dal_ref[...] += jnp.broadcast_to(dal, (8, LANES))
            dbias_ref[...] += jnp.broadcast_to(dbias, (8, LANES))

    rc = lambda c: nc - 1 - c
    par = lambda w: pl.BlockSpec((None, 8, w), lambda g, i, c: (g, 0, 0))
    blk = lambda w: pl.BlockSpec((None, q, w), lambda g, i, c: (i, rc(c), g))
    return pl.pallas_call(
        body,
        out_shape=(jax.ShapeDtypeStruct((b, seq, d_inner), BF16),
                   jax.ShapeDtypeStruct((b, seq, d_inner), F32),
                   jax.ShapeDtypeStruct((b, seq, N_GROUPS * D_STATE), F32),
                   jax.ShapeDtypeStruct((b, seq, N_GROUPS * D_STATE), F32),
                   jax.ShapeDtypeStruct((b, seq, N_GROUPS * LANES), BF16),
                   jax.ShapeDtypeStruct((N_GROUPS, 8, gw), F32),
                   jax.ShapeDtypeStruct((N_GROUPS, 8, LANES), F32),
                   jax.ShapeDtypeStruct((N_GROUPS, 8, LANES), F32),
                   jax.ShapeDtypeStruct((N_GROUPS, 8, LANES), F32)),
        grid=(N_GROUPS, b, nc),
        in_specs=[blk(gw),
                  pl.BlockSpec((None, q, D_STATE), lambda g, i, c: (i, rc(c), xb0 + g)),
                  pl.BlockSpec((None, q, D_STATE), lambda g, i, c: (i, rc(c), xc0 + g)),
                  blk(gw),
                  pl.BlockSpec((None, q, LANES), lambda g, i, c: (i, rc(c), dt0 + g)),
                  blk(gw), blk(gw),
                  pl.BlockSpec((None, None, None, gw, D_STATE), lambda g, i, c: (i, rc(c), g, 0, 0)),
                  par(LANES), par(LANES), par(gw), par(gw)],
        out_specs=(blk(gw), blk(gw), blk(D_STATE), blk(D_STATE), blk(LANES),
                   par(gw), par(LANES), par(LANES), par(LANES)),
        scratch_shapes=[pltpu.VMEM((gw, D_STATE), F32), pltpu.VMEM((q, LANES), F32), pltpu.VMEM((LANES, q), F32)],
        compiler_params=_params(("parallel", "arbitrary", "arbitrary")), name=name,
    )(xc, xc, xc, zx, zx, y, dyn, st, dtb, alog, dskip, nw)


def _adamw(w, g, m, v, name):
    rows, cols = w.shape
    tr = rows
    for cand in (512, 256, 128, 64, 32, 16, 8):
        if rows % cand == 0 and cand * cols * 4 <= 2 * 1024 * 1024:
            tr = cand
            break
    c1 = 1.0 - ADAM_B1 ** ADAM_STEP
    c2 = 1.0 - ADAM_B2 ** ADAM_STEP

    def body(w_ref, g_ref, m_ref, v_ref, d_ref, mo_ref, vo_ref):
        gv = g_ref[...]
        mn = ADAM_B1 * m_ref[...] + (1.0 - ADAM_B1) * gv
        vn = ADAM_B2 * v_ref[...] + (1.0 - ADAM_B2) * (gv * gv)
        mo_ref[...] = mn
        vo_ref[...] = vn
        d_ref[...] = -ADAM_LR * ((mn / c1) / (jnp.sqrt(vn / c2) + ADAM_EPS) + ADAM_WD * w_ref[...])

    spec = pl.BlockSpec((tr, cols), lambda i: (i, 0))
    shp = jax.ShapeDtypeStruct((rows, cols), F32)
    return pl.pallas_call(body, out_shape=(shp, shp, shp), grid=(rows // tr,), in_specs=[spec] * 4,
                          out_specs=(spec,) * 3, compiler_params=_params(("parallel",)), name=name)(w, g, m, v)


def _pick_rows(rows, row_bytes, limit=1 << 20):
    for cand in (2048, 1024, 512, 256, 128, 64, 32, 16):
        if rows % cand == 0 and cand * row_bytes <= limit:
            return cand
    return rows


def _as3d(a, lead):
    return a.reshape(a.shape[:lead] + (-1, a.shape[-1]))


def _pair_sum(g, got, core, name):
    h = got.shape[0]
    g3, got3 = _as3d(g, 1), _as3d(got, 1)
    _, rows, cols = got3.shape
    tr = _pick_rows(rows, cols * 4)

    def body(c_ref, g_ref, r_ref, o_ref):
        o_ref[...] = (g_ref[...] + r_ref[...]).astype(BF16)

    out = pl.pallas_call(
        body, out_shape=jax.ShapeDtypeStruct(got3.shape, BF16),
        grid_spec=pltpu.PrefetchScalarGridSpec(
            num_scalar_prefetch=1, grid=(h, rows // tr),
            in_specs=[pl.BlockSpec((None, tr, cols), lambda l, i, c_ref: (c_ref[0] * h + l, i, 0)),
                      pl.BlockSpec((None, tr, cols), lambda l, i, c_ref: (l, i, 0))],
            out_specs=pl.BlockSpec((None, tr, cols), lambda l, i, c_ref: (l, i, 0))),
        compiler_params=_params(("parallel", "parallel")), name=name)(core, g3, got3)
    return out.reshape(got.shape)


def _sum4(q, core, name):
    q4 = _as3d(q, 2)
    _, h, rows, cols = q4.shape
    tr = _pick_rows(rows, cols * 4)

    def body(c_ref, q0, q1, q2, q3, o_ref):
        o_ref[...] = ((q0[...].astype(F32) + q1[...].astype(F32)) + q2[...].astype(F32)) + q3[...].astype(F32)

    out = pl.pallas_call(
        body, out_shape=jax.ShapeDtypeStruct((2 * h, rows, cols), F32),
        grid_spec=pltpu.PrefetchScalarGridSpec(
            num_scalar_prefetch=1, grid=(h, rows // tr),
            in_specs=[pl.BlockSpec((None, None, tr, cols), lambda l, i, c_ref, k=k: (k, l, i, 0))
                      for k in range(N_CHIPS)],
            out_specs=pl.BlockSpec((None, tr, cols), lambda l, i, c_ref: (c_ref[0] * h + l, i, 0))),
        compiler_params=_params(("parallel", "parallel")), name=name)(core, q4, q4, q4, q4)
    return out.reshape((2 * h,) + q.shape[2:])


def _coords():
    return lax.axis_index("x"), lax.axis_index("y"), lax.axis_index("c")


def _other_chips(x, y):
    return [(1 - x, y), (x, 1 - y), (1 - x, 1 - y)]


def _allgather_halves(src, name):
    rows, cols = src.shape

    def body(x_ref, o_ref, send, recv, local):
        x, y, c = _coords()
        sib = (x, y, 1 - c)
        chips = _other_chips(x, y)

        def slot(h, cx, cy):
            return o_ref.at[h, 2 * cx + cy]

        def copy(kk, dst, to, src_ref):
            return pltpu.make_async_remote_copy(src_ref=src_ref, dst_ref=dst, send_sem=send.at[kk],
                                                recv_sem=recv.at[kk], device_id=to, device_id_type=MESH)

        mine = pltpu.make_async_copy(x_ref, slot(c, x, y), local)
        mine.start()
        first = [copy(0, slot(c, x, y), sib, x_ref)]
        first += [copy(1 + j, slot(c, x, y), (*chip, c), x_ref) for j, chip in enumerate(chips)]
        for cp in first:
            cp.start()
        passed = [copy(4 + j, slot(c, *chip), sib, slot(c, *chip)) for j, chip in enumerate(chips)]
        for j, chip in enumerate(chips):
            copy(1 + j, slot(c, *chip), (x, y, c), x_ref).wait_recv()
            passed[j].start()
        copy(0, slot(1 - c, x, y), (x, y, c), x_ref).wait_recv()
        for j, chip in enumerate(chips):
            copy(4 + j, slot(1 - c, *chip), (x, y, c), x_ref).wait_recv()
        for cp in first + passed:
            cp.wait_send()
        mine.wait()

    return pl.pallas_call(
        body, out_shape=jax.ShapeDtypeStruct((2, N_CHIPS, rows, cols), src.dtype),
        in_specs=[ANY], out_specs=ANY,
        scratch_shapes=[pltpu.SemaphoreType.DMA((7,)), pltpu.SemaphoreType.DMA((7,)), pltpu.SemaphoreType.DMA],
        name=name)(src)


MIXW = (("ssd_w_in", None), ("ssd_w_out", 0), ("pool_w", 1))
FFNW = (("ffn_w_up", 1), ("ffn_w_down", 0))


def _chip_window(axis, ref, layers, k):
    if axis is None:
        return ref.at[layers, k]
    n = ref.shape[1 + axis] // N_CHIPS
    sl = pl.ds(pl.multiple_of(k * n, LANES if 1 + axis == len(ref.shape) - 1 else 8), n)
    idx = [layers] + [slice(None)] * (len(ref.shape) - 1)
    idx[1 + axis] = sl
    return ref.at[tuple(idx)]


def _full_shape(axis, shard_shape):
    if axis is None:
        return (shard_shape[0], N_CHIPS) + tuple(shard_shape[1:])
    full = list(shard_shape)
    full[1 + axis] *= N_CHIPS
    return tuple(full)


HBM_SPEC = pl.BlockSpec(memory_space=pltpu.HBM)
SEM_SPEC = pl.BlockSpec(memory_space=pltpu.SEMAPHORE)


def _dma_sems(count):
    return pltpu.SemaphoreType.DMA((max(count, 1),))


def _wait_for(copy, kind):
    if kind == "recv":
        copy.wait_recv()
    elif kind == "send":
        copy.wait_send()
    else:
        copy.wait()


def _comm_fused(stages, counts, srcs, lands, name, inplace=False):
    ns, nl, k = len(srcs), len(lands), len(stages)

    def body(*refs):
        src_refs = refs[:ns]
        land_refs = refs[ns + (nl if inplace else 0):ns + (nl if inplace else 0) + nl]
        sem_refs = refs[len(refs) - 3 * k:]
        for s, stage_fn in enumerate(stages):
            starts, waits = stage_fn(src_refs, land_refs, tuple(sem_refs[3 * s:3 * s + 3]))
            for cp in starts:
                cp.start()
            for cp, kind in waits:
                _wait_for(cp, kind)

    scratch = []
    for cnt in counts:
        scratch += [_dma_sems(c) for c in cnt]
    outs = pl.pallas_call(
        body, out_shape=tuple(jax.ShapeDtypeStruct(a.shape, a.dtype) for a in lands),
        in_specs=[ANY] * (ns + (nl if inplace else 0)), out_specs=(ANY,) * nl,
        input_output_aliases={ns + i: i for i in range(nl)} if inplace else {},
        scratch_shapes=scratch, name=name)(*srcs, *(lands if inplace else ()))
    return list(outs)


class _SplitComm:
    def __init__(self, stages, counts, srcs, lands, name):
        self.stages, self.counts, self.name = stages, counts, name
        self.ns = len(srcs)
        self.data = [pltpu.with_memory_space_constraint(a, pltpu.HBM) for a in list(srcs) + list(lands)]
        self.sems = None
        self.step = 0

    def advance(self, after=None):
        i, k, nd, ns = self.step, len(self.stages), len(self.data), self.ns
        first, last = i == 0, i == k
        stages = self.stages

        def body(*refs):
            data = refs[:nd]
            pos = nd
            if not first:
                old = tuple(refs[pos:pos + 3])
                pos += 4
            if not last:
                new = tuple(refs[pos:pos + 3])
            if not first:
                for cp, kind in stages[i - 1](data[:ns], data[ns:], old)[1]:
                    _wait_for(cp, kind)
            if not last:
                for cp in stages[i](data[:ns], data[ns:], new)[0]:
                    cp.start()
                refs[len(refs) - 1][...] = jnp.zeros((8, LANES), F32)

        args = list(self.data)
        in_specs = [HBM_SPEC] * nd
        if not first:
            args += list(self.sems) + [after]
            in_specs += [SEM_SPEC] * 3 + [ANY]
        out_shape, out_specs = [], []
        if not last:
            out_shape += [_dma_sems(c) for c in self.counts[i]]
            out_specs += [SEM_SPEC] * 3
        out_shape += [pltpu.HBM(a.shape, a.dtype) for a in self.data]
        out_specs += [HBM_SPEC] * nd
        if not last:
            out_shape.append(jax.ShapeDtypeStruct((8, LANES), F32))
            out_specs.append(pl.BlockSpec(memory_space=pltpu.VMEM))
        off = 0 if last else 3
        outs = pl.pallas_call(
            body, out_shape=tuple(out_shape), in_specs=in_specs, out_specs=tuple(out_specs),
            input_output_aliases={d: off + d for d in range(nd)},
            compiler_params=pltpu.CompilerParams(has_side_effects=pltpu.SideEffectType.DATAFLOW_SIDE_EFFECTING),
            name=f"{self.name}_{i}")(*args)
        self.sems = None if last else outs[:3]
        self.data = list(outs[off:off + nd])
        self.step += 1
        return None if last else outs[len(outs) - 1]

    def lands(self):
        return self.data[self.ns:]


def _gather_stages(spec):
    n = len(spec)

    def parts(srcs, lands):
        x, y, c = _coords()
        out = []
        for w, (_, axis) in enumerate(spec):
            h = srcs[w].shape[0] // 2
            mine, theirs = pl.ds(c * h, h), pl.ds((1 - c) * h, h)
            out.append((srcs[w].at[mine], lambda layers, k, w=w, axis=axis: _chip_window(axis, lands[w], layers, k),
                        mine, theirs))
        return x, y, c, 2 * x + y, (x, y, 1 - c), _other_chips(x, y), out

    def remote(src, dst, send, recv, idx, to):
        return pltpu.make_async_remote_copy(src_ref=src, dst_ref=dst, send_sem=send.at[idx], recv_sem=recv.at[idx],
                                            device_id=to, device_id_type=MESH)

    def stage0(srcs, lands, sems):
        send, recv, local = sems
        x, y, c, me, sib, chips, ps = parts(srcs, lands)
        starts, waits = [], []
        for w, (src, dst, mine, theirs) in enumerate(ps):
            lc = pltpu.make_async_copy(src, dst(mine, me), local.at[w])
            first = [remote(src, dst(mine, me), send, recv, 4 * w, sib)]
            first += [remote(src, dst(mine, me), send, recv, 4 * w + 1 + j, (cx, cy, c)) for j, (cx, cy) in enumerate(chips)]
            starts += [lc] + first
            waits.append((remote(src, dst(theirs, me), send, recv, 4 * w, (x, y, c)), "recv"))
            waits += [(remote(src, dst(mine, 2 * cx + cy), send, recv, 4 * w + 1 + j, (x, y, c)), "recv")
                      for j, (cx, cy) in enumerate(chips)]
            waits += [(cp, "send") for cp in first] + [(lc, "local")]
        return starts, waits

    def stage1(srcs, lands, sems):
        send, recv, _ = sems
        x, y, c, me, sib, chips, ps = parts(srcs, lands)
        starts, waits = [], []
        for w, (src, dst, mine, theirs) in enumerate(ps):
            for j, (cx, cy) in enumerate(chips):
                blk = dst(mine, 2 * cx + cy)
                fwd = remote(blk, blk, send, recv, 3 * w + j, sib)
                starts.append(fwd)
                waits.append((remote(src, dst(theirs, 2 * cx + cy), send, recv, 3 * w + j, (x, y, c)), "recv"))
                waits.append((fwd, "send"))
        return starts, waits

    return [stage0, stage1], [(4 * n, 4 * n, n), (3 * n, 3 * n, 0)]


def _swap_stages(spec):
    n = len(spec)

    def stage(srcs, lands, sems):
        send, recv, _ = sems
        x, y, c = _coords()
        starts, waits = [], []
        for w in range(n):
            h = srcs[w].shape[0] // 2
            cp = pltpu.make_async_remote_copy(src_ref=srcs[w].at[pl.ds((1 - c) * h, h)], dst_ref=lands[w],
                                              send_sem=send.at[w], recv_sem=recv.at[w],
                                              device_id=(x, y, 1 - c), device_id_type=MESH)
            starts.append(cp)
            waits += [(cp, "recv"), (cp, "send")]
        return starts, waits

    return [stage], [(n, n, 0)]


def _scatter_stages(spec):
    n = len(spec)

    def stage(srcs, lands, sems):
        send, recv, local = sems
        x, y, c = _coords()
        me = 2 * x + y
        starts, waits = [], []
        for w, (_, axis) in enumerate(spec):
            layers = pl.ds(0, srcs[w].shape[0])
            own = _chip_window(axis, srcs[w], layers, me)
            lc = pltpu.make_async_copy(own, lands[w].at[me], local.at[w])
            starts.append(lc)
            for j, (cx, cy) in enumerate(_other_chips(x, y)):
                cp = pltpu.make_async_remote_copy(src_ref=_chip_window(axis, srcs[w], layers, 2 * cx + cy),
                                                  dst_ref=lands[w].at[me], send_sem=send.at[3 * w + j],
                                                  recv_sem=recv.at[3 * w + j], device_id=(cx, cy, c), device_id_type=MESH)
                starts.append(cp)
                waits.append((pltpu.make_async_remote_copy(
                    src_ref=own, dst_ref=lands[w].at[2 * cx + cy], send_sem=send.at[3 * w + j], recv_sem=recv.at[3 * w + j],
                    device_id=(x, y, c), device_id_type=MESH), "recv"))
                waits.append((cp, "send"))
            waits.append((lc, "local"))
        return starts, waits

    return [stage], [(3 * n, 3 * n, n)]


def _share_stages(spec):
    n = len(spec)

    def stage(srcs, lands, sems):
        send, recv, _ = sems
        x, y, c = _coords()
        starts, waits = [], []
        for w in range(n):
            h = lands[w].shape[0] // 2
            mine, theirs = lands[w].at[pl.ds(c * h, h)], lands[w].at[pl.ds((1 - c) * h, h)]
            cp = pltpu.make_async_remote_copy(src_ref=mine, dst_ref=mine, send_sem=send.at[w], recv_sem=recv.at[w],
                                              device_id=(x, y, 1 - c), device_id_type=MESH)
            starts.append(cp)
            waits.append((pltpu.make_async_remote_copy(src_ref=theirs, dst_ref=theirs, send_sem=send.at[w],
                                                       recv_sem=recv.at[w], device_id=(x, y, c), device_id_type=MESH),
                          "recv"))
            waits.append((cp, "send"))
        return starts, waits

    return [stage], [(n, n, 0)]


def _shard_of(p, axis):
    if axis is None:
        return (p.shape[0],) + tuple(p.shape[2:])
    s = list(p.shape)
    s[1 + axis] //= N_CHIPS
    return tuple(s)


def _reduce_grads(spec, gs, core, tag, split_after=None):
    stages, counts = _swap_stages(spec)
    got = _comm_fused(stages, counts, gs,
                      [jax.ShapeDtypeStruct((g.shape[0] // 2,) + g.shape[1:], g.dtype) for g in gs], "swap_" + tag)
    pair = [_pair_sum(a, r, core, "pair_sum_" + n) for a, r, (n, _) in zip(gs, got, spec)]
    stages, counts = _scatter_stages(spec)
    land_shapes = [jax.ShapeDtypeStruct((N_CHIPS,) + _shard_of(p, axis), p.dtype) for p, (_, axis) in zip(pair, spec)]
    work = None
    if split_after is None:
        parts = _comm_fused(stages, counts, pair, land_shapes, "scatter_" + tag)
    else:
        comm = _SplitComm(stages, counts, pair, [lax.empty(s.shape, s.dtype) for s in land_shapes], "scatter_" + tag)
        token = comm.advance()
        work = split_after(token)
        comm.advance(after=jax.tree.leaves(work)[0])
        parts = comm.lands()
    halves = [_sum4(q, core, "sum4_" + n) for q, (n, _) in zip(parts, spec)]
    stages, counts = _share_stages(spec)
    shards = _comm_fused(stages, counts, [], halves, "share_" + tag, inplace=True)
    return shards, work


def _allreduce_small(vec, name):
    rows, cols = vec.shape

    def body(x_ref, o_ref, buf, send, recv):
        x, y, c = _coords()
        me = 4 * x + 2 * y + c
        buf[me] = x_ref[...]
        cps = []
        for kk in range(1, 8):
            dx, dy, dc = (kk >> 2) & 1, (kk >> 1) & 1, kk & 1
            to = (1 - x if dx else x, 1 - y if dy else y, 1 - c if dc else c)
            cp = pltpu.make_async_remote_copy(src_ref=x_ref, dst_ref=buf.at[me], send_sem=send.at[kk - 1],
                                              recv_sem=recv.at[kk - 1], device_id=to, device_id_type=MESH)
            cp.start()
            cps.append((cp, 4 * to[0] + 2 * to[1] + to[2]))
        for kk, (cp, frm) in enumerate(cps):
            pltpu.make_async_remote_copy(src_ref=x_ref, dst_ref=buf.at[frm], send_sem=send.at[kk],
                                         recv_sem=recv.at[kk], device_id=(x, y, c), device_id_type=MESH).wait_recv()
        for cp, _ in cps:
            cp.wait_send()
        acc = buf[0]
        for kk in range(1, 8):
            acc = acc + buf[kk]
        o_ref[...] = acc

    vm = pl.BlockSpec(memory_space=pltpu.VMEM)
    return pl.pallas_call(
        body, out_shape=jax.ShapeDtypeStruct((rows, cols), F32), in_specs=[vm], out_specs=vm,
        scratch_shapes=[pltpu.VMEM((8, rows, cols), F32), pltpu.SemaphoreType.DMA((7,)), pltpu.SemaphoreType.DMA((7,))],
        compiler_params=_params(), name=name)(vec)


SMALL = (("ssd_conv_w", 2), ("pool_scale", 1), ("ffn_conv_w", 2))
REPL = ("ssd_conv_b", "ssd_dt_bias", "ssd_a_log", "ssd_d", "ssd_norm_w", "ffn_conv_b",
        "norm_mix_pre", "norm_mix_post", "norm_ffn_pre", "norm_ffn_post")
WEIGHTS = ("ssd_w_in", "ssd_conv_w", "ssd_conv_b", "ssd_dt_bias", "ssd_a_log", "ssd_d", "ssd_norm_w", "ssd_w_out",
           "pool_w", "pool_scale", "ffn_w_up", "ffn_conv_w", "ffn_conv_b", "ffn_w_down", "norm_mix_pre",
           "norm_mix_post", "norm_ffn_pre", "norm_ffn_post")


def _flat_rows(n):
    unit = 2 * 16 * FLAT_COLS
    return 2 * 16 * ((n + unit - 1) // unit)


def _flatten_shards(arrs, dtype):
    flat = jnp.concatenate([a.astype(dtype).reshape(-1) for a in arrs])
    rows = _flat_rows(flat.shape[0])
    flat = jnp.pad(flat, (0, rows * FLAT_COLS - flat.shape[0]))
    return flat.reshape(2, rows // 2, FLAT_COLS)


def _unflatten_full(gathered, shard_shapes, axes):
    per_chip = jnp.swapaxes(gathered, 0, 1).reshape(N_CHIPS, -1)
    out, off = [], 0
    for shp, ax in zip(shard_shapes, axes):
        n = math.prod(shp)
        pieces = [per_chip[k, off:off + n].reshape(shp) for k in range(N_CHIPS)]
        out.append(jnp.concatenate(pieces, axis=ax))
        off += n
    return out


def kernel(x, ssd_w_in, ssd_conv_w, ssd_conv_b, ssd_dt_bias, ssd_a_log, ssd_d, ssd_norm_w, ssd_w_out, pool_w, pool_scale, ffn_w_up, ffn_conv_w, ffn_conv_b, ffn_w_down, norm_mix_pre, norm_mix_post, norm_ffn_pre, norm_ffn_post, loss_target, m_ssd_w_in, m_ssd_conv_w, m_ssd_conv_b, m_ssd_dt_bias, m_ssd_a_log, m_ssd_d, m_ssd_norm_w, m_ssd_w_out, m_pool_w, m_pool_scale, m_ffn_w_up, m_ffn_conv_w, m_ffn_conv_b, m_ffn_w_down, m_norm_mix_pre, m_norm_mix_post, m_norm_ffn_pre, m_norm_ffn_post, v_ssd_w_in, v_ssd_conv_w, v_ssd_conv_b, v_ssd_dt_bias, v_ssd_a_log, v_ssd_d, v_ssd_norm_w, v_ssd_w_out, v_pool_w, v_pool_scale, v_ffn_w_up, v_ffn_conv_w, v_ffn_conv_b, v_ffn_w_down, v_norm_mix_pre, v_norm_mix_post, v_norm_ffn_pre, v_norm_ffn_post):
    wts = dict(ssd_w_in=ssd_w_in, ssd_conv_w=ssd_conv_w, ssd_conv_b=ssd_conv_b, ssd_dt_bias=ssd_dt_bias,
               ssd_a_log=ssd_a_log, ssd_d=ssd_d, ssd_norm_w=ssd_norm_w, ssd_w_out=ssd_w_out, pool_w=pool_w,
               pool_scale=pool_scale, ffn_w_up=ffn_w_up, ffn_conv_w=ffn_conv_w, ffn_conv_b=ffn_conv_b,
               ffn_w_down=ffn_w_down, norm_mix_pre=norm_mix_pre, norm_mix_post=norm_mix_post,
               norm_ffn_pre=norm_ffn_pre, norm_ffn_post=norm_ffn_post)
    mom = dict(ssd_w_in=m_ssd_w_in, ssd_conv_w=m_ssd_conv_w, ssd_conv_b=m_ssd_conv_b, ssd_dt_bias=m_ssd_dt_bias,
               ssd_a_log=m_ssd_a_log, ssd_d=m_ssd_d, ssd_norm_w=m_ssd_norm_w, ssd_w_out=m_ssd_w_out, pool_w=m_pool_w,
               pool_scale=m_pool_scale, ffn_w_up=m_ffn_w_up, ffn_conv_w=m_ffn_conv_w, ffn_conv_b=m_ffn_conv_b,
               ffn_w_down=m_ffn_w_down, norm_mix_pre=m_norm_mix_pre, norm_mix_post=m_norm_mix_post,
               norm_ffn_pre=m_norm_ffn_pre, norm_ffn_post=m_norm_ffn_post)
    var = dict(ssd_w_in=v_ssd_w_in, ssd_conv_w=v_ssd_conv_w, ssd_conv_b=v_ssd_conv_b, ssd_dt_bias=v_ssd_dt_bias,
               ssd_a_log=v_ssd_a_log, ssd_d=v_ssd_d, ssd_norm_w=v_ssd_norm_w, ssd_w_out=v_ssd_w_out, pool_w=v_pool_w,
               pool_scale=v_pool_scale, ffn_w_up=v_ffn_w_up, ffn_conv_w=v_ffn_conv_w, ffn_conv_b=v_ffn_conv_b,
               ffn_w_down=v_ffn_w_down, norm_mix_pre=v_norm_mix_pre, norm_mix_post=v_norm_mix_post,
               norm_ffn_pre=v_norm_ffn_pre, norm_ffn_post=v_norm_ffn_post)

    bl, seq, d = x.shape
    t = bl * seq
    depth = norm_mix_pre.shape[0]
    n_ssd = ssd_w_out.shape[0]
    d_inner = ssd_w_out.shape[1] * N_CHIPS
    nheads = d_inner // HEAD_DIM
    hpg = nheads // N_GROUPS
    gw = d_inner // N_GROUPS
    xbc = ssd_conv_w.shape[2] * N_CHIPS
    f2 = ffn_w_up.shape[2] * N_CHIPS
    ff = f2 // 2
    dg = d // 4
    cy = lax.axis_index("c")
    chip = 2 * lax.axis_index("x") + lax.axis_index("y")

    small_shapes = [wts[n].shape for n, _ in SMALL]
    small_axes = [a for _, a in SMALL]
    small_flat = _flatten_shards([wts[n] for n, _ in SMALL], F32)
    small_half = lax.dynamic_index_in_dim(small_flat, cy, 0, keepdims=False)
    small_all = _allgather_halves(small_half, "gather_small")
    conv_w, p_scale, f_conv_w = _unflatten_full(small_all, small_shapes, small_axes)
    def full_shapes(spec, shards):
        return [jax.ShapeDtypeStruct(_full_shape(axis, s.shape), s.dtype) for s, (_, axis) in zip(shards, spec)]

    stages, counts = _gather_stages(MIXW)
    mix_shards = [wts[n].astype(BF16) for n, _ in MIXW]
    w_in_cm, w_out, w_pool = _comm_fused(stages, counts, mix_shards, full_shapes(MIXW, mix_shards), "gather_mixers")
    w_in = jnp.swapaxes(w_in_cm, 1, 2).reshape(n_ssd, d, -1)
    stages, counts = _gather_stages(FFNW)
    ffn_shards, _ = lax.optimization_barrier(([wts[n].astype(BF16) for n, _ in FFNW], w_pool))
    ffn_gather = _SplitComm(stages, counts, ffn_shards,
                            [lax.empty(s.shape, s.dtype) for s in full_shapes(FFNW, ffn_shards)], "gather_ffn")
    gather_token = ffn_gather.advance()

    def pad_heads(a):
        lead = a.shape[:-1]
        a = a.reshape(lead + (N_GROUPS, hpg))
        a = jnp.pad(a, [(0, 0)] * len(lead) + [(0, 0), (0, LANES - hpg)])
        return a.reshape(lead + (N_GROUPS * LANES,))

    def unpad_heads(a):
        lead = a.shape[:-1]
        return a.reshape(lead + (N_GROUPS, LANES))[..., :hpg].reshape(lead + (nheads,))

    def group_rows(a, width):
        return jnp.broadcast_to(a.reshape(N_GROUPS, 1, width), (N_GROUPS, 8, width))

    w_in_p = jnp.concatenate([w_in[..., :d_inner + xbc], pad_heads(w_in[..., d_inner + xbc:])], axis=-1)
    zw = w_in_p.shape[-1]

    x2, _ = lax.optimization_barrier((x.reshape(t, d), gather_token))
    tgt2 = loss_target.reshape(t, d)
    w_up = w_down = None

    saved = []
    cur = x2
    for i in range(depth):
        j = i // 2
        sv = dict(x_in=cur)
        if i % 2 == 0:
            h = _norm_fwd(cur, norm_mix_pre[i:i + 1], BF16, f"norm_pre_b")
            zx = _mm(h, w_in_p, "nn", F32, "mm_ssd_in", 2048, 512, d, b_layer=j).reshape(bl, seq, zw)
            xc = _ssd_conv_fwd(zx, conv_w[j], ssd_conv_b[j:j + 1], d_inner, "ssd_conv_fwd")
            dtb = group_rows(pad_heads(ssd_dt_bias[j]), LANES)
            alog = group_rows(pad_heads(ssd_a_log[j]), LANES)
            dskip = group_rows(jnp.repeat(ssd_d[j], HEAD_DIM), gw)
            nw = group_rows(ssd_norm_w[j], gw)
            y, yn, st = _ssd_fwd(xc, zx, dtb, alog, dskip, nw, d_inner, "ssd_fwd")
            if i == 0:
                yn, _ = lax.optimization_barrier((yn, ffn_gather.advance(after=yn)))
            mix = _mm(yn.reshape(t, d_inner), w_out, "nn", F32, "mm_ssd_out", 512, 512, d_inner, b_layer=j)
            sv.update(h=h, zx=zx, xc=xc, y=y, yn=yn, st=st, dtb=dtb, alog=alog, dskip=dskip, nw=nw)
        else:
            h = _norm_fwd(cur, norm_mix_pre[i:i + 1], F32, "norm_pre_f")
            mix = _pool_fwd(h.reshape(bl, seq, d), w_pool[j], p_scale[j:j + 1], "pool_fwd").reshape(t, d)
            sv.update(h=h)
        sv.update(mix=mix)
        mid = _norm_fwd(mix, norm_mix_post[i:i + 1], F32, "norm_post", resid=cur)
        u = _norm_fwd(mid, norm_ffn_pre[i:i + 1], BF16, "norm_pre_b")
        if i == 0:
            ffn_gather.advance(after=u)
            w_up, w_down = ffn_gather.lands()
        hpre = _mm(u, w_up, "nn", BF16, "mm_up", 2048, 512, d, b_layer=i).reshape(bl, seq, f2)
        act = _ffn_act_fwd(hpre, f_conv_w[i], ffn_conv_b[i:i + 1], "ffn_act_fwd").reshape(t, ff)
        fo = _mm(act, w_down, "nn", F32, "mm_down", 1024, 512, ff, b_layer=i)
        cur = _norm_fwd(fo, norm_ffn_post[i:i + 1], F32, "norm_post", resid=mid)
        sv.update(mid=mid, u=u, hpre=hpre, act=act, fo=fo)
        saved.append(sv)

    dcur, loss_part = _loss_head(cur, tgt2, "loss_head")

    g = {n: [None] * wts[n].shape[0] for n in WEIGHTS}
    gbuf = dict(up=lax.empty((depth, d, f2), F32), down=lax.empty((depth, ff, d), F32),
                out=lax.empty((n_ssd, d_inner, d), F32), win=lax.empty((n_ssd, d, zw), F32))
    core = cy.reshape(1).astype(jnp.int32)

    def mixer_bwd(i, dmid):
        j = i // 2
        sv = saved[i]
        if i % 2 == 0:
            dmix, g["norm_mix_post"][i] = _norm_bwd(sv["mix"], norm_mix_post[i:i + 1], dmid, BF16, "norm_bwd_b")
            dyn = _mm(dmix, w_out, "nt", F32, "mm_ssd_out_dx", 1024, 1024, d, b_layer=j)
            gbuf["out"] = _mm(sv["yn"].reshape(t, d_inner), dmix, "tn", F32, "mm_ssd_out_dw", 1024, 512, 2048,
                              out_buf=(gbuf["out"], j))
            dz, dxs, dbm, dcm, ddt, dnw, dd, dal, dbias = _ssd_bwd(
                sv["xc"], sv["zx"], sv["y"], dyn.reshape(bl, seq, d_inner), sv["st"], sv["dtb"], sv["alog"],
                sv["dskip"], sv["nw"], d_inner, "ssd_bwd")
            g["ssd_norm_w"][j] = dnw[:, 0, :].reshape(d_inner)
            g["ssd_d"][j] = dd[:, 0, :hpg].reshape(nheads)
            g["ssd_a_log"][j] = dal[:, 0, :hpg].reshape(nheads)
            g["ssd_dt_bias"][j] = dbias[:, 0, :hpg].reshape(nheads)
            dxbc, dcw, dcb = _ssd_conv_bwd(sv["zx"], (dxs, dbm, dcm), conv_w[j], ssd_conv_b[j:j + 1], d_inner,
                                           "ssd_conv_bwd")
            g["ssd_conv_w"][j] = dcw
            g["ssd_conv_b"][j] = dcb[0]
            dzs = [dz.reshape(t, d_inner), dxbc.reshape(t, xbc), ddt.reshape(t, N_GROUPS * LANES)]
            dh = _mm(dzs, w_in_p, "nt", F32, "mm_ssd_in_dx", 1024, d, 512, b_layer=j)
            gbuf["win"] = _mm(sv["h"], dzs, "tn", F32, "mm_ssd_in_dw", 1024, 512, 2048, out_buf=(gbuf["win"], j))
        else:
            dmix, g["norm_mix_post"][i] = _norm_bwd(sv["mix"], norm_mix_post[i:i + 1], dmid, F32, "norm_bwd_f")
            dh3, g["pool_w"][j], dps = _pool_bwd(sv["h"].reshape(bl, seq, d), dmix.reshape(bl, seq, d), w_pool[j],
                                                 p_scale[j:j + 1], "pool_bwd")
            g["pool_scale"][j] = dps[0]
            dh = dh3.reshape(t, d)
        dx_in, g["norm_mix_pre"][i] = _norm_bwd(sv["x_in"], norm_mix_pre[i:i + 1], dh, F32, "norm_bwd_r", resid=dmid)
        return dx_in

    ffn_grads = None
    for i in reversed(range(depth)):
        sv = saved[i]
        dfo, g["norm_ffn_post"][i] = _norm_bwd(sv["fo"], norm_ffn_post[i:i + 1], dcur, BF16, "norm_bwd_b")
        dact = _mm(dfo, w_down, "nt", BF16, "mm_down_dx", 1024, ff // 2, d, b_layer=i)
        gbuf["down"] = _mm(sv["act"], dfo, "tn", F32, "mm_down_dw", ff // 2, 512, 2048, out_buf=(gbuf["down"], i))
        dhg, dhv, dcw, dcb = _ffn_act_bwd(sv["hpre"], dact.reshape(bl, seq, ff), f_conv_w[i], ffn_conv_b[i:i + 1],
                                          "ffn_act_bwd")
        g["ffn_conv_w"][i] = dcw
        g["ffn_conv_b"][i] = dcb[0]
        dhs = [dhg.reshape(t, ff), dhv.reshape(t, ff)]
        du = _mm(dhs, w_up, "nt", F32, "mm_up_dx", 1024, d, ff // 2, b_layer=i)
        gbuf["up"] = _mm(sv["u"], dhs, "tn", F32, "mm_up_dw", 512, ff // 2, 2048, out_buf=(gbuf["up"], i))
        dmid, g["norm_ffn_pre"][i] = _norm_bwd(sv["mid"], norm_ffn_pre[i:i + 1], du, F32, "norm_bwd_r", resid=dcur)
        if i > 0:
            dcur = mixer_bwd(i, dmid)
        else:
            def last_mixer(token):
                return mixer_bwd(0, lax.optimization_barrier((dmid, token))[0])

            ffn_grads, dcur = _reduce_grads(FFNW, [gbuf["up"], gbuf["down"]], core, "ffn", split_after=last_mixer)

    grad_x = dcur.reshape(bl, seq, d)
    for n in ("norm_mix_pre", "norm_mix_post", "norm_ffn_pre", "norm_ffn_post"):
        g[n] = [a[0] for a in g[n]]
    small_names = [n for n, _ in SMALL] + list(REPL)
    full = {n: jnp.stack(g[n], axis=0) for n in small_names}

    g_in = jnp.concatenate([gbuf["win"][..., :d_inner + xbc], unpad_heads(gbuf["win"][..., d_inner + xbc:])], axis=-1)
    g_in_cm = jnp.swapaxes(g_in.reshape(n_ssd, d, N_CHIPS, -1), 1, 2)
    mix_grads, _ = _reduce_grads(MIXW, [g_in_cm, gbuf["out"], jnp.stack(g["pool_w"], axis=0)], core, "mixers")
    big_grads = {n: s for s, (n, _) in zip(mix_grads + ffn_grads, MIXW + FFNW)}

    vec = jnp.concatenate([full[n].reshape(-1) for n in small_names] + [loss_part[0, :1]])
    nvec = vec.shape[0]
    vrows = 8 * ((nvec + 8 * FLAT_COLS - 1) // (8 * FLAT_COLS))
    vec = jnp.pad(vec, (0, vrows * FLAT_COLS - nvec)).reshape(vrows, FLAT_COLS)
    tot = _allreduce_small(vec, "allreduce_small").reshape(-1)
    small_grads, off = {}, 0
    for n in small_names:
        cnt = math.prod(full[n].shape)
        small_grads[n] = tot[off:off + cnt].reshape(full[n].shape)
        off += cnt
    loss = tot[off]
    for n, ax in SMALL:
        w = wts[n].shape[ax]
        small_grads[n] = lax.dynamic_slice_in_dim(small_grads[n], chip * w, w, axis=ax)

    grads, deltas, new_m, new_v = {}, {}, {}, {}
    for n in WEIGHTS:
        gr = big_grads[n] if n in big_grads else small_grads[n]
        shp = wts[n].shape
        two = (math.prod(shp[:-1]), shp[-1])
        dl, mn, vn = _adamw(wts[n].reshape(two), gr.reshape(two), mom[n].reshape(two), var[n].reshape(two),
                            "adamw_" + n)
        grads[n], deltas[n], new_m[n], new_v[n] = gr, dl.reshape(shp), mn.reshape(shp), vn.reshape(shp)

    return (loss, grad_x, *[grads[n] for n in WEIGHTS], *[deltas[n] for n in WEIGHTS],
            *[new_m[n] for n in WEIGHTS], *[new_v[n] for n in WEIGHTS])
```

```python
import functools
import math

import jax
import jax.numpy as jnp
from jax import lax
from jax.experimental import pallas as pl
from jax.experimental.pallas import tpu as pltpu

F32 = jnp.float32
BF16 = jnp.bfloat16
MESH = pl.DeviceIdType.MESH
ANY = pl.BlockSpec(memory_space=pl.ANY)

HEAD_DIM = 64
D_STATE = 128
CHUNK = 128
N_GROUPS = 4
SSD_CONV = 4
FFN_CONV = 3
EPS = 1e-6
N_CHIPS = 4
LANES = 128
FLAT_COLS = 1024

ADAM_LR = 0.001
ADAM_B1 = 0.9
ADAM_B2 = 0.999
ADAM_EPS = 1e-08
ADAM_WD = 0.01
ADAM_STEP = 10

VMEM_LIMIT_BYTES = 56 * 1024 * 1024


def _params(sem=None):
    kw = dict(vmem_limit_bytes=VMEM_LIMIT_BYTES)
    if sem is not None:
        kw["dimension_semantics"] = sem
    return pltpu.CompilerParams(**kw)


def _sigmoid(x):
    return 1.0 / (1.0 + jnp.exp(-x))


def _softplus(x):
    return jnp.maximum(x, 0.0) + jnp.log(1.0 + jnp.exp(-jnp.abs(x)))


def _dot(a, b, dn):
    return lax.dot_general(a, b, (dn, ((), ())), preferred_element_type=F32)


def _nn(a, b):
    return _dot(a, b, ((1,), (0,)))


def _nt(a, b):
    return _dot(a, b, ((1,), (1,)))


def _tn(a, b):
    return _dot(a, b, ((0,), (0,)))


def _split(x, parts):
    out = []
    r = x
    for _ in range(parts):
        p = r.astype(BF16)
        out.append(p)
        r = r - p.astype(F32)
    return out


def _sel_left(sel, x, parts=3):
    n = x.shape[1]
    r = _nn(sel, jnp.concatenate(_split(x, parts), axis=1))
    out = r[:, 0:n]
    for i in range(1, parts):
        out = out + r[:, i * n:(i + 1) * n]
    return out


def _sel_right(x, sel_stacked, parts=3):
    return _nn(jnp.concatenate(_split(x, parts), axis=1), sel_stacked)


def _mm(a, b, dims, out_dtype, name, tm, tn, tk, b_layer=None, out_buf=None):
    a_list = list(a) if isinstance(a, (list, tuple)) else [a]
    b_list = list(b) if isinstance(b, (list, tuple)) else [b]
    if dims in ("nn", "nt"):
        assert len(b_list) == 1
        m = a_list[0].shape[0]
        segs = [x.shape[1] for x in a_list]
        k = sum(segs)
        bshape = b_list[0].shape[-2:]
        n = bshape[1] if dims == "nn" else bshape[0]
        assert (bshape[0] if dims == "nn" else bshape[1]) == k
    else:
        assert len(a_list) == 1 and b_layer is None
        k, m = a_list[0].shape
        segs = [x.shape[1] for x in b_list]
        n = sum(segs)
    tm, tn, tk = min(tm, m), min(tn, n), min(tk, k)
    if dims == "tn":
        tn = min(tn, min(segs))
    else:
        tk = min(tk, min(segs))
    unit = tk if dims != "tn" else tn
    assert m % tm == 0 and n % tn == 0 and k % tk == 0 and all(s % unit == 0 for s in segs), (name, m, n, k, segs)
    nk = k // tk
    starts = [sum(segs[:s]) // unit for s in range(len(segs))]
    counts = [s // unit for s in segs]
    nseg = len(segs)
    dn = {"nn": ((1,), (0,)), "nt": ((1,), (1,)), "tn": ((0,), (0,))}[dims]

    def body(*refs):
        a_refs = refs[:len(a_list)]
        b_refs = refs[len(a_list):len(a_list) + len(b_list)]
        rest = refs[len(a_list) + len(b_list) + (0 if out_buf is None else 1):]
        o_ref = rest[0]
        if out_buf is not None:
            rest[1][...] = jnp.zeros((8, LANES), F32)
            rest = rest[1:]
        acc = rest[1] if nk > 1 else None
        kk = pl.program_id(2)
        sel = kk if dims != "tn" else pl.program_id(1)

        def step(a_ref, b_ref):
            p = _dot(a_ref[...].astype(BF16), b_ref[...].astype(BF16), dn)
            if nk == 1:
                o_ref[...] = p.astype(out_dtype)
                return

            @pl.when(kk == 0)
            def _():
                acc[...] = p

            @pl.when(kk > 0)
            def _():
                acc[...] += p

        if nseg == 1:
            step(a_refs[0], b_refs[0])
        else:
            for s in range(nseg):
                @pl.when(jnp.logical_and(sel >= starts[s], sel < starts[s] + counts[s]))
                def _(s=s):
                    step(a_refs[s] if dims != "tn" else a_refs[0], b_refs[0] if dims != "tn" else b_refs[s])

        if nk > 1:
            @pl.when(kk == nk - 1)
            def _():
                o_ref[...] = acc[...].astype(out_dtype)

    def seg_index(v, s):
        return v if nseg == 1 else jnp.clip(v - starts[s], 0, counts[s] - 1)

    lead = () if b_layer is None else (b_layer,)
    none = () if b_layer is None else (None,)
    if dims == "nn":
        a_specs = [pl.BlockSpec((tm, tk), lambda i, j, kk, s=s: (i, seg_index(kk, s))) for s in range(nseg)]
        b_specs = [pl.BlockSpec(none + (tk, tn), lambda i, j, kk: lead + (kk, j))]
    elif dims == "nt":
        a_specs = [pl.BlockSpec((tm, tk), lambda i, j, kk, s=s: (i, seg_index(kk, s))) for s in range(nseg)]
        b_specs = [pl.BlockSpec(none + (tn, tk), lambda i, j, kk: lead + (j, kk))]
    else:
        a_specs = [pl.BlockSpec((tk, tm), lambda i, j, kk: (kk, i))]
        b_specs = [pl.BlockSpec((tk, tn), lambda i, j, kk, s=s: (kk, seg_index(j, s))) for s in range(nseg)]
    args = a_list + b_list
    in_specs = a_specs + b_specs
    aliases = {}
    if out_buf is None:
        out_shape = jax.ShapeDtypeStruct((m, n), out_dtype)
        out_spec = pl.BlockSpec((tm, tn), lambda i, j, kk: (i, j))
    else:
        buf, slab = out_buf
        assert buf.shape[1:] == (m, n) and buf.dtype == out_dtype
        out_shape = (jax.ShapeDtypeStruct(buf.shape, out_dtype), jax.ShapeDtypeStruct((8, LANES), F32))
        out_spec = (pl.BlockSpec((None, tm, tn), lambda i, j, kk: (slab, i, j)),
                    pl.BlockSpec((8, LANES), lambda i, j, kk: (0, 0)))
        aliases = {len(args): 0}
        args = args + [buf]
        in_specs = in_specs + [ANY]
    return pl.pallas_call(
        body,
        out_shape=out_shape,
        grid=(m // tm, n // tn, nk),
        in_specs=in_specs,
        out_specs=out_spec,
        scratch_shapes=[] if nk == 1 else [pltpu.VMEM((tm, tn), F32)],
        input_output_aliases=aliases,
        compiler_params=_params(("parallel", "parallel", "arbitrary") if out_buf is None else ("arbitrary",) * 3),
        name=name,
    )(*args)


def _row_tile(t, want):
    tm = min(want, t)
    assert t % tm == 0
    return tm


def _norm_fwd(x, w, out_dtype, name, resid=None, after=()):
    t, d = x.shape
    tm = _row_tile(t, 512)
    after = [a for a in after if a is not None]

    def body(*refs):
        refs = refs[:len(refs) - 1 - len(after)] + refs[len(refs) - 1:]
        if resid is None:
            x_ref, w_ref, o_ref = refs
        else:
            x_ref, w_ref, r_ref, o_ref = refs
        xv = x_ref[...]
        r = lax.rsqrt(jnp.mean(xv * xv, axis=-1, keepdims=True) + EPS)
        y = (xv * r) * w_ref[...]
        if resid is not None:
            y = r_ref[...] + y
        o_ref[...] = y.astype(out_dtype)

    row = pl.BlockSpec((tm, d), lambda i: (i, 0))
    vec = pl.BlockSpec((1, d), lambda i: (0, 0))
    args = [x, w] + ([] if resid is None else [resid]) + after
    return pl.pallas_call(
        body, out_shape=jax.ShapeDtypeStruct((t, d), out_dtype), grid=(t // tm,),
        in_specs=[row, vec] + ([] if resid is None else [row]) + [ANY] * len(after), out_specs=row,
        compiler_params=_params(("parallel",)), name=name)(*args)


def _norm_bwd(src, w, dy, out_dtype, name, resid=None, after=()):
    t, d = src.shape
    tm = _row_tile(t, 512)
    after = [a for a in after if a is not None]

    def body(*refs):
        refs = refs[:len(refs) - 2 - len(after)] + refs[len(refs) - 2:]
        if resid is None:
            x_ref, w_ref, g_ref, o_ref, dw_ref = refs
        else:
            x_ref, w_ref, g_ref, r_ref, o_ref, dw_ref = refs
        xv = x_ref[...]
        g = g_ref[...].astype(F32)
        r = lax.rsqrt(jnp.mean(xv * xv, axis=-1, keepdims=True) + EPS)
        xh = xv * r
        gh = g * w_ref[...]
        mean = jnp.mean(gh * xh, axis=-1, keepdims=True)
        dx = r * (gh - xh * mean)
        if resid is not None:
            dx = r_ref[...] + dx
        o_ref[...] = dx.astype(out_dtype)
        part = jnp.sum(g * xh, axis=0, keepdims=True)

        @pl.when(pl.program_id(0) == 0)
        def _():
            dw_ref[...] = part

        @pl.when(pl.program_id(0) > 0)
        def _():
            dw_ref[...] += part

    row = pl.BlockSpec((tm, d), lambda i: (i, 0))
    vec = pl.BlockSpec((1, d), lambda i: (0, 0))
    args = [src, w, dy] + ([] if resid is None else [resid]) + after
    return pl.pallas_call(
        body,
        out_shape=(jax.ShapeDtypeStruct((t, d), out_dtype), jax.ShapeDtypeStruct((1, d), F32)),
        grid=(t // tm,),
        in_specs=[row, vec, row] + ([] if resid is None else [row]) + [ANY] * len(after),
        out_specs=(row, vec),
        compiler_params=_params(("arbitrary",)), name=name)(*args)


def _loss_head(y, target, name):
    t, d = y.shape
    tm = _row_tile(t, 512)

    def body(y_ref, t_ref, dy_ref, l_ref):
        e = y_ref[...] - t_ref[...]
        dy_ref[...] = e * (1.0 / d)
        col = jnp.sum(e * e, axis=0, keepdims=True)
        s = jnp.sum(col, axis=1, keepdims=True) * (0.5 / d)
        part = jnp.broadcast_to(s, (1, LANES))

        @pl.when(pl.program_id(0) == 0)
        def _():
            l_ref[...] = part

        @pl.when(pl.program_id(0) > 0)
        def _():
            l_ref[...] += part

    row = pl.BlockSpec((tm, d), lambda i: (i, 0))
    return pl.pallas_call(
        body,
        out_shape=(jax.ShapeDtypeStruct((t, d), F32), jax.ShapeDtypeStruct((1, LANES), F32)),
        grid=(t // tm,), in_specs=[row, row],
        out_specs=(row, pl.BlockSpec((1, LANES), lambda i: (0, 0))),
        compiler_params=_params(("arbitrary",)), name=name)(y, target)


def _window(ref, c, rows, seq, before, after):
    r0 = pl.multiple_of(c * rows, rows)
    parts = []
    if before:
        h0 = pl.multiple_of(jnp.maximum(r0 - before, 0), before)
        halo = ref[pl.ds(h0, before), :].astype(F32)
        parts.append(jnp.where(c > 0, halo, 0.0))
    parts.append(ref[pl.ds(r0, rows), :].astype(F32))
    if after:
        h1 = pl.multiple_of(jnp.minimum(r0 + rows, seq - after), after)
        halo = ref[pl.ds(h1, after), :].astype(F32)
        parts.append(jnp.where(c < seq // rows - 1, halo, 0.0))
    return parts[0] if len(parts) == 1 else jnp.concatenate(parts, axis=0)


def _lag(x, k):
    return pltpu.roll(x, k, 0) if k else x


def _lead(x, k):
    return pltpu.roll(x, x.shape[0] - k, 0) if k else x


SHIFT_ROWS = 128
SHIFT_COLS = 256


HALO = 16


def _conv3(ext, w, bias):
    acc = bias + w[2:3, :] * ext[HALO:, :]
    acc = acc + w[1:2, :] * _lag(ext, 1)[HALO:, :]
    return acc + w[0:1, :] * _lag(ext, 2)[HALO:, :]


def _ffn_act_fwd(hpre, cw, cb, name):
    b, seq, f2 = hpre.shape
    cbk = SHIFT_COLS
    nj = f2 // (2 * cbk)
    rows = min(SHIFT_ROWS, seq)

    def body(g_ref, v_ref, wg_ref, wv_ref, bg_ref, bv_ref, o_ref):
        def chunk(c, carry):
            gate = _conv3(_window(g_ref, c, rows, seq, HALO, 0), wg_ref[...], bg_ref[...])
            val = _conv3(_window(v_ref, c, rows, seq, HALO, 0), wv_ref[...], bv_ref[...])
            a = gate * _sigmoid(gate) * val
            o_ref[pl.ds(pl.multiple_of(c * rows, rows), rows), :] = a.astype(BF16)
            return carry

        lax.fori_loop(0, seq // rows, chunk, 0)

    blk = lambda off: pl.BlockSpec((None, seq, cbk), lambda i, j: (i, 0, j + off))
    wsp = lambda r, off: pl.BlockSpec((r, cbk), lambda i, j: (0, j + off))
    return pl.pallas_call(
        body, out_shape=jax.ShapeDtypeStruct((b, seq, f2 // 2), BF16), grid=(b, nj),
        in_specs=[blk(0), blk(nj), wsp(FFN_CONV, 0), wsp(FFN_CONV, nj), wsp(1, 0), wsp(1, nj)],
        out_specs=blk(0),
        compiler_params=_params(("parallel", "parallel")), name=name)(hpre, hpre, cw, cw, cb, cb)


def _ffn_act_bwd(hpre, da, cw, cb, name):
    b, seq, f2 = hpre.shape
    cbk = SHIFT_COLS
    nj = f2 // (2 * cbk)
    rows = min(SHIFT_ROWS, seq)

    def body(g_ref, v_ref, da_ref, wg_ref, wv_ref, bg_ref, bv_ref, og_ref, ov_ref, dwg_ref, dwv_ref, dbg_ref, dbv_ref):
        wg, wv = wg_ref[...], wv_ref[...]

        def shifted(ref, c):
            ext = _window(ref, c, rows, seq, HALO, HALO)
            return [_lag(ext, k)[HALO:, :] for k in range(FFN_CONV)]

        def back(dpre, w, o_ref, c, xs, carry):
            dx = w[2:3, :] * dpre + w[1:2, :] * _lead(dpre, 1) + w[0:1, :] * _lead(dpre, 2)
            o_ref[pl.ds(pl.multiple_of(c * rows, rows), rows), :] = dx[:rows, :].astype(BF16)
            dp = dpre[:rows, :]
            return tuple(carry[k] + jnp.sum(dp * xs[k][:rows, :], axis=0, keepdims=True) for k in range(FFN_CONV)) + (
                carry[FFN_CONV] + jnp.sum(dp, axis=0, keepdims=True),)

        def chunk(c, carry):
            cg, cv = carry
            gs, vs = shifted(g_ref, c), shifted(v_ref, c)
            gate = bg_ref[...] + wg[2:3, :] * gs[0] + wg[1:2, :] * gs[1] + wg[0:1, :] * gs[2]
            val = bv_ref[...] + wv[2:3, :] * vs[0] + wv[1:2, :] * vs[1] + wv[0:1, :] * vs[2]
            dav = _window(da_ref, c, rows, seq, 0, HALO)
            sg = _sigmoid(gate)
            cg = back(dav * val * (sg * (1.0 + gate * (1.0 - sg))), wg, og_ref, c, gs, cg)
            cv = back(dav * (gate * sg), wv, ov_ref, c, vs, cv)
            return cg, cv

        z = jnp.zeros((1, cbk), F32)
        cg, cv = lax.fori_loop(0, seq // rows, chunk, ((z,) * (FFN_CONV + 1), (z,) * (FFN_CONV + 1)))
        dwg = jnp.concatenate([cg[2], cg[1], cg[0]], axis=0)
        dwv = jnp.concatenate([cv[2], cv[1], cv[0]], axis=0)

        @pl.when(pl.program_id(1) == 0)
        def _():
            dwg_ref[...] = dwg
            dwv_ref[...] = dwv
            dbg_ref[...] = cg[FFN_CONV]
            dbv_ref[...] = cv[FFN_CONV]

        @pl.when(pl.program_id(1) > 0)
        def _():
            dwg_ref[...] += dwg
            dwv_ref[...] += dwv
            dbg_ref[...] += cg[FFN_CONV]
            dbv_ref[...] += cv[FFN_CONV]

    blk = lambda off: pl.BlockSpec((None, seq, cbk), lambda j, i: (i, 0, j + off))
    wsp = lambda r, off: pl.BlockSpec((r, cbk), lambda j, i: (0, j + off))
    half = jax.ShapeDtypeStruct((b, seq, f2 // 2), BF16)
    dwshape = jax.ShapeDtypeStruct((FFN_CONV, f2 // 2), F32)
    dbshape = jax.ShapeDtypeStruct((1, f2 // 2), F32)
    dg, dv, dwg, dwv, dbg, dbv = pl.pallas_call(
        body,
        out_shape=(half, half, dwshape, dwshape, dbshape, dbshape),
        grid=(nj, b),
        in_specs=[blk(0), blk(nj), blk(0), wsp(FFN_CONV, 0), wsp(FFN_CONV, nj), wsp(1, 0), wsp(1, nj)],
        out_specs=(blk(0), blk(0), wsp(FFN_CONV, 0), wsp(FFN_CONV, 0), wsp(1, 0), wsp(1, 0)),
        compiler_params=_params(("parallel", "arbitrary")), name=name)(hpre, hpre, da, cw, cw, cb, cb)
    return dg, dv, jnp.concatenate([dwg, dwv], axis=1), jnp.concatenate([dbg, dbv], axis=1)


def _ssd_conv_fwd(zx, cw, cb, d_inner, name):
    b, seq, _ = zx.shape
    xbc = cw.shape[1]
    cbk = SHIFT_COLS
    off = d_inner // cbk
    rows = min(SHIFT_ROWS, seq)

    def body(h_ref, w_ref, b_ref, o_ref):
        w = w_ref[...]
        bias = b_ref[...]

        def chunk(c, carry):
            ext = _window(h_ref, c, rows, seq, 8, 0)
            acc = bias + w[3:4, :] * ext[8:, :]
            for k in range(1, SSD_CONV):
                acc = acc + w[3 - k:4 - k, :] * _lag(ext, k)[8:, :]
            o_ref[pl.ds(pl.multiple_of(c * rows, rows), rows), :] = acc * _sigmoid(acc)
            return carry

        lax.fori_loop(0, seq // rows, chunk, 0)

    return pl.pallas_call(
        body, out_shape=jax.ShapeDtypeStruct((b, seq, xbc), F32), grid=(b, xbc // cbk),
        in_specs=[pl.BlockSpec((None, seq, cbk), lambda i, j: (i, 0, j + off)),
                  pl.BlockSpec((SSD_CONV, cbk), lambda i, j: (0, j)),
                  pl.BlockSpec((1, cbk), lambda i, j: (0, j))],
        out_specs=pl.BlockSpec((None, seq, cbk), lambda i, j: (i, 0, j)),
        compiler_params=_params(("parallel", "parallel")), name=name)(zx, cw, cb)


def _ssd_conv_bwd(zx, dparts, cw, cb, d_inner, name):
    b, seq, _ = zx.shape
    xbc = cw.shape[1]
    cbk = SHIFT_COLS
    off = d_inner // cbk
    rows = min(SHIFT_ROWS, seq)
    nblk = [p.shape[2] // cbk for p in dparts]
    first = [sum(nblk[:s]) for s in range(len(dparts))]
    assert sum(nblk) == xbc // cbk

    def body(h_ref, gx_ref, gb_ref, gc_ref, w_ref, b_ref, o_ref, dw_ref, db_ref):
        w = w_ref[...]
        bias = b_ref[...]
        j = pl.program_id(0)

        def chunk(c, carry):
            dws, dbias = carry
            ext = _window(h_ref, c, rows, seq, 8, 8)
            xs = [_lag(ext, k)[8:, :] for k in range(SSD_CONV)]
            pre = bias + w[3:4, :] * xs[0]
            for k in range(1, SSD_CONV):
                pre = pre + w[3 - k:4 - k, :] * xs[k]
            s = _sigmoid(pre)
            gsel = jnp.where(j < first[1], _window(gx_ref, c, rows, seq, 0, 8),
                             jnp.where(j < first[2], _window(gb_ref, c, rows, seq, 0, 8),
                                       _window(gc_ref, c, rows, seq, 0, 8)))
            dpre = gsel * (s * (1.0 + pre * (1.0 - s)))
            dx = w[3:4, :] * dpre
            for k in range(1, SSD_CONV):
                dx = dx + w[3 - k:4 - k, :] * _lead(dpre, k)
            o_ref[pl.ds(pl.multiple_of(c * rows, rows), rows), :] = dx[:rows, :].astype(BF16)
            dp = dpre[:rows, :]
            dws = tuple(dws[k] + jnp.sum(dp * xs[k][:rows, :], axis=0, keepdims=True) for k in range(SSD_CONV))
            dbias = dbias + jnp.sum(dp, axis=0, keepdims=True)
            return dws, dbias

        z = jnp.zeros((1, cbk), F32)
        dws, dbias = lax.fori_loop(0, seq // rows, chunk, ((z,) * SSD_CONV, z))
        dwv = jnp.concatenate([dws[3 - i] for i in range(SSD_CONV)], axis=0)

        @pl.when(pl.program_id(1) == 0)
        def _():
            dw_ref[...] = dwv
            db_ref[...] = dbias

        @pl.when(pl.program_id(1) > 0)
        def _():
            dw_ref[...] += dwv
            db_ref[...] += dbias

    return pl.pallas_call(
        body,
        out_shape=(jax.ShapeDtypeStruct((b, seq, xbc), BF16), jax.ShapeDtypeStruct((SSD_CONV, xbc), F32),
                   jax.ShapeDtypeStruct((1, xbc), F32)),
        grid=(xbc // cbk, b),
        in_specs=[pl.BlockSpec((None, seq, cbk), lambda j, i: (i, 0, j + off))] + [
                  pl.BlockSpec((None, seq, cbk), lambda j, i, s=s: (i, 0, jnp.clip(j - first[s], 0, nblk[s] - 1)))
                  for s in range(3)] + [
                  pl.BlockSpec((SSD_CONV, cbk), lambda j, i: (0, j)),
                  pl.BlockSpec((1, cbk), lambda j, i: (0, j))],
        out_specs=(pl.BlockSpec((None, seq, cbk), lambda j, i: (i, 0, j)),
                   pl.BlockSpec((SSD_CONV, cbk), lambda j, i: (0, j)),
                   pl.BlockSpec((1, cbk), lambda j, i: (0, j))),
        compiler_params=_params(("parallel", "arbitrary")), name=name)(zx, *dparts, cw, cb)


def _pool_sums(q, g, lead):
    sh = _lead if lead else _lag
    s2 = q + sh(q, 1)
    s4 = s2 + sh(s2, 2)
    s8 = s4 + sh(s4, 4)
    s16 = s8 + sh(s8, 8)
    return jnp.where(g == 0, s2, jnp.where(g == 1, s4, jnp.where(g == 2, s8, s16)))


def _pool_count(r0, n, g, shape):
    t = (r0 + lax.broadcasted_iota(jnp.int32, shape, 0) + 1).astype(F32)
    return jnp.minimum(t, (2 << g).astype(F32))


def _pool_fwd(h, pw, scale, name):
    b, seq, d = h.shape
    dg = d // 4
    rows = min(SHIFT_ROWS, seq)

    def body(h_ref, w_ref, s_ref, o_ref):
        g = pl.program_id(1)
        wmat = w_ref[...]
        sc = s_ref[...]

        def chunk(c, carry):
            r0 = c * rows
            ext = _window(h_ref, c, rows, seq, 16, 0)
            sums = _pool_sums(ext, g, False)[16:, :]
            mixed = sums / _pool_count(r0, rows, g, (rows, dg)) - ext[16:, :]
            o_ref[pl.ds(pl.multiple_of(r0, rows), rows), :] = _nn(mixed.astype(BF16), wmat) * sc
            return carry

        lax.fori_loop(0, seq // rows, chunk, 0)

    return pl.pallas_call(
        body, out_shape=jax.ShapeDtypeStruct((b, seq, d), F32), grid=(b, 4),
        in_specs=[pl.BlockSpec((None, seq, dg), lambda i, g: (i, 0, g)),
                  pl.BlockSpec((None, dg, dg), lambda i, g: (g, 0, 0)),
                  pl.BlockSpec((1, dg), lambda i, g: (0, g))],
        out_specs=pl.BlockSpec((None, seq, dg), lambda i, g: (i, 0, g)),
        compiler_params=_params(("parallel", "parallel")), name=name)(h, pw, scale)


def _pool_bwd(h, dout, pw, scale, name):
    b, seq, d = h.shape
    dg = d // 4
    rows = min(SHIFT_ROWS, seq)

    def body(h_ref, g_ref, w_ref, s_ref, o_ref, dw_ref, ds_ref, dw_acc):
        g = pl.program_id(0)
        wmat = w_ref[...]
        sc = s_ref[...]
        dw_acc[...] = jnp.zeros_like(dw_acc)

        def chunk(c, dsc):
            r0 = c * rows
            ext = _window(h_ref, c, rows, seq, 16, 0)
            sums = _pool_sums(ext, g, False)[16:, :]
            mixed = (sums / _pool_count(r0, rows, g, (rows, dg)) - ext[16:, :]).astype(BF16)
            gext = _window(g_ref, c, rows, seq, 0, 16)
            dsc = dsc + jnp.sum(gext[:rows, :] * _nn(mixed, wmat), axis=0, keepdims=True)
            dpre = (gext * sc).astype(BF16)
            dw_acc[...] += _tn(mixed, dpre[:rows, :])
            dmix = _nt(dpre, wmat)
            q = dmix / _pool_count(r0, rows + 16, g, (rows + 16, dg))
            back = _pool_sums(q, g, True)
            o_ref[pl.ds(pl.multiple_of(r0, rows), rows), :] = back[:rows, :] - dmix[:rows, :]
            return dsc

        dsc = lax.fori_loop(0, seq // rows, chunk, jnp.zeros((1, dg), F32))

        @pl.when(pl.program_id(1) == 0)
        def _():
            dw_ref[...] = dw_acc[...]
            ds_ref[...] = dsc

        @pl.when(pl.program_id(1) > 0)
        def _():
            dw_ref[...] += dw_acc[...]
            ds_ref[...] += dsc

    return pl.pallas_call(
        body,
        out_shape=(jax.ShapeDtypeStruct((b, seq, d), F32), jax.ShapeDtypeStruct((4, dg, dg), F32),
                   jax.ShapeDtypeStruct((1, d), F32)),
        grid=(4, b),
        in_specs=[pl.BlockSpec((None, seq, dg), lambda g, i: (i, 0, g)),
                  pl.BlockSpec((None, seq, dg), lambda g, i: (i, 0, g)),
                  pl.BlockSpec((None, dg, dg), lambda g, i: (g, 0, 0)),
                  pl.BlockSpec((1, dg), lambda g, i: (0, g))],
        out_specs=(pl.BlockSpec((None, seq, dg), lambda g, i: (i, 0, g)),
                   pl.BlockSpec((None, dg, dg), lambda g, i: (g, 0, 0)),
                   pl.BlockSpec((1, dg), lambda g, i: (0, g))),
        scratch_shapes=[pltpu.VMEM((dg, dg), F32)],
        compiler_params=_params(("parallel", "arbitrary")), name=name)(h, dout, pw, scale)


def _head_of(channel):
    return jnp.right_shift(channel, HEAD_DIM.bit_length() - 1)


def _ssd_consts(gw):
    q = CHUNK
    row = lax.broadcasted_iota(jnp.int32, (q, q), 0)
    col = lax.broadcasted_iota(jnp.int32, (q, q), 1)
    tril = (row >= col).astype(BF16)
    triu = (row <= col).astype(BF16)
    e = (_head_of(lax.broadcasted_iota(jnp.int32, (LANES, gw), 1))
         == lax.broadcasted_iota(jnp.int32, (LANES, gw), 0)).astype(BF16)
    et = (_head_of(lax.broadcasted_iota(jnp.int32, (gw, LANES), 0))
          == lax.broadcasted_iota(jnp.int32, (gw, LANES), 1)).astype(BF16)
    return row, col, tril, triu, e, et


def _ssd_common(dtr, dtb, alog, gw):
    q = CHUNK
    row, col, tril, triu, e, et = _ssd_consts(gw)
    dt = _softplus(dtr + dtb)
    a_row = -jnp.exp(alog)
    acum = _sel_left(tril, dt * a_row)
    ac_last = jnp.sum(jnp.where(row == q - 1, acum, 0.0), axis=0, keepdims=True)
    eac = jnp.exp(acum)
    de = jnp.exp(ac_last - acum)
    e2 = jnp.concatenate([e, e], axis=0)
    expand = _sel_right(jnp.concatenate([dt, eac, de], axis=0), e2, 2)
    dt_x, eac_x, de_x = expand[0:q], expand[q:2 * q], expand[2 * q:3 * q]
    acum_t = acum.T
    cd_col = jnp.exp(acum_t[:, q - 1:q])
    et3 = jnp.concatenate([et, et, et], axis=1)
    cdmat = _nn(et3, jnp.concatenate(_split(jnp.broadcast_to(cd_col, (LANES, D_STATE)), 3), axis=0))
    consts = dict(row=row, col=col, tril=tril, triu=triu, e=e, et=et)
    return dt, a_row, acum, acum_t, ac_last, eac, de, dt_x, eac_x, de_x, cdmat, consts


def _decay(acum, acum_t, j, row, col):
    diff = acum[:, j:j + 1] - acum_t[j:j + 1, :]
    return jnp.exp(jnp.where(row >= col, diff, -1e30))


def _ssd_fwd(xc, zx, dtb, alog, dskip, nw, d_inner, name):
    b, seq, xbc = xc.shape
    q = CHUNK
    nc = seq // q
    gw = d_inner // N_GROUPS
    nh = gw // HEAD_DIM
    xb0 = d_inner // D_STATE
    xc0 = xb0 + N_GROUPS
    dt0 = (d_inner + xbc) // LANES

    def body(x_ref, b_ref, c_ref, z_ref, dtr_ref, dtb_ref, al_ref, dsk_ref, nw_ref, y_ref, yn_ref, st_ref, s_ref):
        @pl.when(pl.program_id(2) == 0)
        def _():
            s_ref[...] = jnp.zeros_like(s_ref)

        prev = s_ref[...]
        st_ref[...] = prev
        x = x_ref[...]
        bm = b_ref[...].astype(BF16)
        cm = c_ref[...].astype(BF16)
        (dt, a_row, acum, acum_t, ac_last, eac, de, dt_x, eac_x, de_x, cdmat, k) = _ssd_common(
            dtr_ref[...], dtb_ref[0:1, :], al_ref[0:1, :], gw)
        xdt = x * dt_x
        xdt_b = xdt.astype(BF16)
        cb = _nt(cm, bm)
        half = _head_of(lax.broadcasted_iota(jnp.int32, (q, LANES), 1))
        pairs = []
        for j in range(nh):
            pc = (j // 2) * LANES
            m = (cb * _decay(acum, acum_t, j, k["row"], k["col"])).astype(BF16)
            yj = jnp.where(half == j % 2, _nn(m, xdt_b[:, pc:pc + LANES]), 0.0)
            if j % 2 == 0:
                pairs.append(yj)
            else:
                pairs[-1] = pairs[-1] + yj
        prev_b = prev.astype(BF16)
        y = dsk_ref[0:1, :] * x + jnp.concatenate(pairs, axis=1) + eac_x * _nt(cm, prev_b)
        s_ref[...] = cdmat * prev + _tn((xdt * de_x).astype(BF16), bm)
        y_ref[...] = y
        z = z_ref[...]
        yg = y * (z * _sigmoid(z))
        r = lax.rsqrt(jnp.mean(yg * yg, axis=-1, keepdims=True) + EPS)
        yn_ref[...] = ((yg * r) * nw_ref[0:1, :]).astype(BF16)

    par = lambda w: pl.BlockSpec((None, 8, w), lambda i, g, c: (g, 0, 0))
    return pl.pallas_call(
        body,
        out_shape=(jax.ShapeDtypeStruct((b, seq, d_inner), F32), jax.ShapeDtypeStruct((b, seq, d_inner), BF16),
                   jax.ShapeDtypeStruct((b, nc, N_GROUPS, gw, D_STATE), F32)),
        grid=(b, N_GROUPS, nc),
        in_specs=[pl.BlockSpec((None, q, gw), lambda i, g, c: (i, c, g)),
                  pl.BlockSpec((None, q, D_STATE), lambda i, g, c: (i, c, xb0 + g)),
                  pl.BlockSpec((None, q, D_STATE), lambda i, g, c: (i, c, xc0 + g)),
                  pl.BlockSpec((None, q, gw), lambda i, g, c: (i, c, g)),
                  pl.BlockSpec((None, q, LANES), lambda i, g, c: (i, c, dt0 + g)),
                  par(LANES), par(LANES), par(gw), par(gw)],
        out_specs=(pl.BlockSpec((None, q, gw), lambda i, g, c: (i, c, g)),
                   pl.BlockSpec((None, q, gw), lambda i, g, c: (i, c, g)),
                   pl.BlockSpec((None, None, None, gw, D_STATE), lambda i, g, c: (i, c, g, 0, 0))),
        scratch_shapes=[pltpu.VMEM((gw, D_STATE), F32)],
        compiler_params=_params(("parallel", "parallel", "arbitrary")), name=name,
    )(xc, xc, xc, zx, zx, dtb, alog, dskip, nw)


def _ssd_bwd(xc, zx, y, dyn, st, dtb, alog, dskip, nw, d_inner, name):
    b, seq, xbc = xc.shape
    q = CHUNK
    nc = seq // q
    gw = d_inner // N_GROUPS
    nh = gw // HEAD_DIM
    xb0 = d_inner // D_STATE
    xc0 = xb0 + N_GROUPS
    dt0 = (d_inner + xbc) // LANES

    def body(x_ref, b_ref, c_ref, z_ref, dtr_ref, y_ref, g_ref, st_ref, dtb_ref, al_ref, dsk_ref, nw_ref,
             dz_ref, dx_ref, db_ref, dc_ref, ddt_ref, dnw_ref, dd_ref, dal_ref, dbias_ref,
             ds_ref, colbuf, rowbuf):
        first = jnp.logical_and(pl.program_id(1) == 0, pl.program_id(2) == 0)

        @pl.when(pl.program_id(2) == 0)
        def _():
            ds_ref[...] = jnp.zeros_like(ds_ref)

        x = x_ref[...]
        bm = b_ref[...].astype(BF16)
        cm = c_ref[...].astype(BF16)
        z = z_ref[...]
        y = y_ref[...]
        prev = st_ref[...]
        dtr = dtr_ref[...] + dtb_ref[0:1, :]
        (dt, a_row, acum, acum_t, ac_last, eac, de, dt_x, eac_x, de_x, cdmat, k) = _ssd_common(
            dtr_ref[...], dtb_ref[0:1, :], al_ref[0:1, :], gw)
        row, col = k["row"], k["col"]
        et2 = jnp.concatenate([k["et"], k["et"]], axis=0)

        sz = _sigmoid(z)
        silu_z = z * sz
        yg = y * silu_z
        r = lax.rsqrt(jnp.mean(yg * yg, axis=-1, keepdims=True) + EPS)
        xh = yg * r
        dyn = g_ref[...]
        gh = dyn * nw_ref[0:1, :]
        dyg = r * (gh - xh * jnp.mean(gh * xh, axis=-1, keepdims=True))
        dnw = jnp.sum(dyn * xh, axis=0, keepdims=True)
        g = dyg * silu_z
        dz_ref[...] = (dyg * y * (sz * (1.0 + z * (1.0 - sz)))).astype(BF16)
        dd = _sel_right(jnp.broadcast_to(jnp.sum(g * x, axis=0, keepdims=True), (8, gw)), et2, 2)

        xdt = x * dt_x
        xdt_b = xdt.astype(BF16)
        g_b = g.astype(BF16)
        prev_b = prev.astype(BF16)
        cb = _nt(cm, bm)

        cp = _nt(cm, prev_b)
        ge = g * eac_x
        dac = _sel_right(ge * cp, et2, 2)
        ge_b = ge.astype(BF16)
        dcm = _nn(ge_b, prev_b)
        dprev = _tn(ge_b, cm)

        colbuf[...] = jnp.zeros_like(colbuf)
        rowbuf[...] = jnp.zeros_like(rowbuf)
        dcb = jnp.zeros((q, q), F32)
        half = _head_of(lax.broadcasted_iota(jnp.int32, (q, LANES), 1))
        pairs = []
        for j in range(nh):
            pc = (j // 2) * LANES
            dec = _decay(acum, acum_t, j, row, col)
            m = cb * dec
            gj = jnp.where(half == j % 2, g[:, pc:pc + LANES], 0.0).astype(BF16)
            dm = _nt(gj, xdt_b[:, pc:pc + LANES])
            w = dm * m
            colbuf[:, j:j + 1] = jnp.sum(w, axis=1, keepdims=True)
            rowbuf[j:j + 1, :] = jnp.sum(w, axis=0, keepdims=True)
            dcb = dcb + dm * dec
            dj = jnp.where(half == j % 2, _tn(m.astype(BF16), g_b[:, pc:pc + LANES]), 0.0)
            if j % 2 == 0:
                pairs.append(dj)
            else:
                pairs[-1] = pairs[-1] + dj
        dxdt = jnp.concatenate(pairs, axis=1)
        dcb_b = dcb.astype(BF16)
        dcm = dcm + _nn(dcb_b, bm)
        dbm = _tn(dcb_b, cm)

        ds = ds_ref[...]
        ds_b = ds.astype(BF16)
        u = _nt(bm, ds_b)
        dxdt = dxdt + u * de_x
        dde = _sel_right(u * xdt, et2, 2)
        dbm = dbm + _nn((xdt * de_x).astype(BF16), ds_b)
        pm = jnp.concatenate(_split(ds * prev, 2), axis=1)
        t2 = _tn(pm, k["et"])
        dcd_row = jnp.sum(t2[0:D_STATE] + t2[D_STATE:2 * D_STATE], axis=0, keepdims=True)
        last = dcd_row * jnp.exp(ac_last) + jnp.sum(dde * de, axis=0, keepdims=True)
        dac = dac + colbuf[...] - rowbuf[...].T - dde * de + jnp.where(row == q - 1, last, 0.0)
        ds_ref[...] = cdmat * ds + dprev

        dadt = _sel_left(k["triu"], dac)
        ddt = _sel_right(dxdt * x, et2, 2) + dadt * a_row
        dal = jnp.sum(dadt * dt, axis=0, keepdims=True) * a_row
        lane = lax.broadcasted_iota(jnp.int32, (q, LANES), 1)
        ddtr = jnp.where(lane < nh, ddt * _sigmoid(dtr), 0.0)
        ddt_ref[...] = ddtr.astype(BF16)
        dbias = jnp.sum(ddtr, axis=0, keepdims=True)
        dx_ref[...] = dxdt * dt_x + dsk_ref[0:1, :] * g
        db_ref[...] = dbm
        dc_ref[...] = dcm

        @pl.when(first)
        def _():
            dnw_ref[...] = jnp.broadcast_to(dnw, (8, gw))
            dd_ref[...] = dd
            dal_ref[...] = jnp.broadcast_to(dal, (8, LANES))
            dbias_ref[...] = jnp.broadcast_to(dbias, (8, LANES))

        @pl.when(jnp.logical_not(first))
        def _():
            dnw_ref[...] += jnp.broadcast_to(dnw, (8, gw))
            dd_ref[...] += dd
            dal_ref[...] += jnp.broadcast_to(dal, (8, LANES))
            dbias_ref[...] += jnp.broadcast_to(dbias, (8, LANES))

    rc = lambda c: nc - 1 - c
    par = lambda w: pl.BlockSpec((None, 8, w), lambda g, i, c: (g, 0, 0))
    blk = lambda w: pl.BlockSpec((None, q, w), lambda g, i, c: (i, rc(c), g))
    return pl.pallas_call(
        body,
        out_shape=(jax.ShapeDtypeStruct((b, seq, d_inner), BF16),
                   jax.ShapeDtypeStruct((b, seq, d_inner), F32),
                   jax.ShapeDtypeStruct((b, seq, N_GROUPS * D_STATE), F32),
                   jax.ShapeDtypeStruct((b, seq, N_GROUPS * D_STATE), F32),
                   jax.ShapeDtypeStruct((b, seq, N_GROUPS * LANES), BF16),
                   jax.ShapeDtypeStruct((N_GROUPS, 8, gw), F32),
                   jax.ShapeDtypeStruct((N_GROUPS, 8, LANES), F32),
                   jax.ShapeDtypeStruct((N_GROUPS, 8, LANES), F32),
                   jax.ShapeDtypeStruct((N_GROUPS, 8, LANES), F32)),
        grid=(N_GROUPS, b, nc),
        in_specs=[blk(gw),
                  pl.BlockSpec((None, q, D_STATE), lambda g, i, c: (i, rc(c), xb0 + g)),
                  pl.BlockSpec((None, q, D_STATE), lambda g, i, c: (i, rc(c), xc0 + g)),
                  blk(gw),
                  pl.BlockSpec((None, q, LANES), lambda g, i, c: (i, rc(c), dt0 + g)),
                  blk(gw), blk(gw),
                  pl.BlockSpec((None, None, None, gw, D_STATE), lambda g, i, c: (i, rc(c), g, 0, 0)),
                  par(LANES), par(LANES), par(gw), par(gw)],
        out_specs=(blk(gw), blk(gw), blk(D_STATE), blk(D_STATE), blk(LANES),
                   par(gw), par(LANES), par(LANES), par(LANES)),
        scratch_shapes=[pltpu.VMEM((gw, D_STATE), F32), pltpu.VMEM((q, LANES), F32), pltpu.VMEM((LANES, q), F32)],
        compiler_params=_params(("parallel", "arbitrary", "arbitrary")), name=name,
    )(xc, xc, xc, zx, zx, y, dyn, st, dtb, alog, dskip, nw)


def _adamw(w, g, m, v, name):
    rows, cols = w.shape
    tr = rows
    for cand in (512, 256, 128, 64, 32, 16, 8):
        if rows % cand == 0 and cand * cols * 4 <= 2 * 1024 * 1024:
            tr = cand
            break
    c1 = 1.0 - ADAM_B1 ** ADAM_STEP
    c2 = 1.0 - ADAM_B2 ** ADAM_STEP

    def body(w_ref, g_ref, m_ref, v_ref, d_ref, mo_ref, vo_ref):
        gv = g_ref[...]
        mn = ADAM_B1 * m_ref[...] + (1.0 - ADAM_B1) * gv
        vn = ADAM_B2 * v_ref[...] + (1.0 - ADAM_B2) * (gv * gv)
        mo_ref[...] = mn
        vo_ref[...] = vn
        d_ref[...] = -ADAM_LR * ((mn / c1) / (jnp.sqrt(vn / c2) + ADAM_EPS) + ADAM_WD * w_ref[...])

    spec = pl.BlockSpec((tr, cols), lambda i: (i, 0))
    shp = jax.ShapeDtypeStruct((rows, cols), F32)
    return pl.pallas_call(body, out_shape=(shp, shp, shp), grid=(rows // tr,), in_specs=[spec] * 4,
                          out_specs=(spec,) * 3, compiler_params=_params(("parallel",)), name=name)(w, g, m, v)


def _pick_rows(rows, row_bytes, limit=1 << 20):
    for cand in (2048, 1024, 512, 256, 128, 64, 32, 16):
        if rows % cand == 0 and cand * row_bytes <= limit:
            return cand
    return rows


def _as3d(a, lead):
    return a.reshape(a.shape[:lead] + (-1, a.shape[-1]))


def _pair_sum(g, got, core, name):
    h = got.shape[0]
    g3, got3 = _as3d(g, 1), _as3d(got, 1)
    _, rows, cols = got3.shape
    tr = _pick_rows(rows, cols * 4)

    def body(c_ref, g_ref, r_ref, o_ref):
        o_ref[...] = (g_ref[...] + r_ref[...]).astype(BF16)

    out = pl.pallas_call(
        body, out_shape=jax.ShapeDtypeStruct(got3.shape, BF16),
        grid_spec=pltpu.PrefetchScalarGridSpec(
            num_scalar_prefetch=1, grid=(h, rows // tr),
            in_specs=[pl.BlockSpec((None, tr, cols), lambda l, i, c_ref: (c_ref[0] * h + l, i, 0)),
                      pl.BlockSpec((None, tr, cols), lambda l, i, c_ref: (l, i, 0))],
            out_specs=pl.BlockSpec((None, tr, cols), lambda l, i, c_ref: (l, i, 0))),
        compiler_params=_params(("parallel", "parallel")), name=name)(core, g3, got3)
    return out.reshape(got.shape)


def _sum4(q, core, name):
    q4 = _as3d(q, 2)
    _, h, rows, cols = q4.shape
    tr = _pick_rows(rows, cols * 4)

    def body(c_ref, q0, q1, q2, q3, o_ref):
        o_ref[...] = ((q0[...].astype(F32) + q1[...].astype(F32)) + q2[...].astype(F32)) + q3[...].astype(F32)

    out = pl.pallas_call(
        body, out_shape=jax.ShapeDtypeStruct((2 * h, rows, cols), F32),
        grid_spec=pltpu.PrefetchScalarGridSpec(
            num_scalar_prefetch=1, grid=(h, rows // tr),
            in_specs=[pl.BlockSpec((None, None, tr, cols), lambda l, i, c_ref, k=k: (k, l, i, 0))
                      for k in range(N_CHIPS)],
            out_specs=pl.BlockSpec((None, tr, cols), lambda l, i, c_ref: (c_ref[0] * h + l, i, 0))),
        compiler_params=_params(("parallel", "parallel")), name=name)(core, q4, q4, q4, q4)
    return out.reshape((2 * h,) + q.shape[2:])


def _coords():
    return lax.axis_index("x"), lax.axis_index("y"), lax.axis_index("c")


def _other_chips(x, y):
    return [(1 - x, y), (x, 1 - y), (1 - x, 1 - y)]


def _allgather_halves(src, name):
    rows, cols = src.shape

    def body(x_ref, o_ref, send, recv, local):
        x, y, c = _coords()
        sib = (x, y, 1 - c)
        chips = _other_chips(x, y)

        def slot(h, cx, cy):
            return o_ref.at[h, 2 * cx + cy]

        def copy(kk, dst, to, src_ref):
            return pltpu.make_async_remote_copy(src_ref=src_ref, dst_ref=dst, send_sem=send.at[kk],
                                                recv_sem=recv.at[kk], device_id=to, device_id_type=MESH)

        mine = pltpu.make_async_copy(x_ref, slot(c, x, y), local)
        mine.start()
        first = [copy(0, slot(c, x, y), sib, x_ref)]
        first += [copy(1 + j, slot(c, x, y), (*chip, c), x_ref) for j, chip in enumerate(chips)]
        for cp in first:
            cp.start()
        passed = [copy(4 + j, slot(c, *chip), sib, slot(c, *chip)) for j, chip in enumerate(chips)]
        for j, chip in enumerate(chips):
            copy(1 + j, slot(c, *chip), (x, y, c), x_ref).wait_recv()
            passed[j].start()
        copy(0, slot(1 - c, x, y), (x, y, c), x_ref).wait_recv()
        for j, chip in enumerate(chips):
            copy(4 + j, slot(1 - c, *chip), (x, y, c), x_ref).wait_recv()
        for cp in first + passed:
            cp.wait_send()
        mine.wait()

    return pl.pallas_call(
        body, out_shape=jax.ShapeDtypeStruct((2, N_CHIPS, rows, cols), src.dtype),
        in_specs=[ANY], out_specs=ANY,
        scratch_shapes=[pltpu.SemaphoreType.DMA((7,)), pltpu.SemaphoreType.DMA((7,)), pltpu.SemaphoreType.DMA],
        name=name)(src)


MIXW = (("ssd_w_in", None), ("ssd_w_out", 0), ("pool_w", 1))
FFNW = (("ffn_w_up", 1), ("ffn_w_down", 0))


def _chip_window(axis, ref, layers, k):
    if axis is None:
        return ref.at[layers, k]
    n = ref.shape[1 + axis] // N_CHIPS
    sl = pl.ds(pl.multiple_of(k * n, LANES if 1 + axis == len(ref.shape) - 1 else 8), n)
    idx = [layers] + [slice(None)] * (len(ref.shape) - 1)
    idx[1 + axis] = sl
    return ref.at[tuple(idx)]


def _full_shape(axis, shard_shape):
    if axis is None:
        return (shard_shape[0], N_CHIPS) + tuple(shard_shape[1:])
    full = list(shard_shape)
    full[1 + axis] *= N_CHIPS
    return tuple(full)


HBM_SPEC = pl.BlockSpec(memory_space=pltpu.HBM)
SEM_SPEC = pl.BlockSpec(memory_space=pltpu.SEMAPHORE)


def _dma_sems(count):
    return pltpu.SemaphoreType.DMA((max(count, 1),))


def _wait_for(copy, kind):
    if kind == "recv":
        copy.wait_recv()
    elif kind == "send":
        copy.wait_send()
    else:
        copy.wait()


def _comm_fused(stages, counts, srcs, lands, name, inplace=False):
    ns, nl, k = len(srcs), len(lands), len(stages)

    def body(*refs):
        src_refs = refs[:ns]
        land_refs = refs[ns + (nl if inplace else 0):ns + (nl if inplace else 0) + nl]
        sem_refs = refs[len(refs) - 3 * k:]
        for s, stage_fn in enumerate(stages):
            starts, waits = stage_fn(src_refs, land_refs, tuple(sem_refs[3 * s:3 * s + 3]))
            for cp in starts:
                cp.start()
            for cp, kind in waits:
                _wait_for(cp, kind)

    scratch = []
    for cnt in counts:
        scratch += [_dma_sems(c) for c in cnt]
    outs = pl.pallas_call(
        body, out_shape=tuple(jax.ShapeDtypeStruct(a.shape, a.dtype) for a in lands),
        in_specs=[ANY] * (ns + (nl if inplace else 0)), out_specs=(ANY,) * nl,
        input_output_aliases={ns + i: i for i in range(nl)} if inplace else {},
        scratch_shapes=scratch, name=name)(*srcs, *(lands if inplace else ()))
    return list(outs)


class _SplitComm:
    def __init__(self, stages, counts, srcs, lands, name):
        self.stages, self.counts, self.name = stages, counts, name
        self.ns = len(srcs)
        self.data = [pltpu.with_memory_space_constraint(a, pltpu.HBM) for a in list(srcs) + list(lands)]
        self.sems = None
        self.step = 0

    def advance(self, after=None):
        i, k, nd, ns = self.step, len(self.stages), len(self.data), self.ns
        first, last = i == 0, i == k
        stages = self.stages

        def body(*refs):
            data = refs[:nd]
            pos = nd
            if not first:
                old = tuple(refs[pos:pos + 3])
                pos += 4
            if not last:
                new = tuple(refs[pos:pos + 3])
            if not first:
                for cp, kind in stages[i - 1](data[:ns], data[ns:], old)[1]:
                    _wait_for(cp, kind)
            if not last:
                for cp in stages[i](data[:ns], data[ns:], new)[0]:
                    cp.start()
                refs[len(refs) - 1][...] = jnp.zeros((8, LANES), F32)

        args = list(self.data)
        in_specs = [HBM_SPEC] * nd
        if not first:
            args += list(self.sems) + [after]
            in_specs += [SEM_SPEC] * 3 + [ANY]
        out_shape, out_specs = [], []
        if not last:
            out_shape += [_dma_sems(c) for c in self.counts[i]]
            out_specs += [SEM_SPEC] * 3
        out_shape += [pltpu.HBM(a.shape, a.dtype) for a in self.data]
        out_specs += [HBM_SPEC] * nd
        if not last:
            out_shape.append(jax.ShapeDtypeStruct((8, LANES), F32))
            out_specs.append(pl.BlockSpec(memory_space=pltpu.VMEM))
        off = 0 if last else 3
        outs = pl.pallas_call(
            body, out_shape=tuple(out_shape), in_specs=in_specs, out_specs=tuple(out_specs),
            input_output_aliases={d: off + d for d in range(nd)},
            compiler_params=pltpu.CompilerParams(has_side_effects=pltpu.SideEffectType.DATAFLOW_SIDE_EFFECTING),
            name=f"{self.name}_{i}")(*args)
        self.sems = None if last else outs[:3]
        self.data = list(outs[off:off + nd])
        self.step += 1
        return None if last else outs[len(outs) - 1]

    def lands(self):
        return self.data[self.ns:]


def _gather_stages(spec):
    n = len(spec)

    def parts(srcs, lands):
        x, y, c = _coords()
        out = []
        for w, (_, axis) in enumerate(spec):
            h = srcs[w].shape[0] // 2
            mine, theirs = pl.ds(c * h, h), pl.ds((1 - c) * h, h)
            out.append((srcs[w].at[mine], lambda layers, k, w=w, axis=axis: _chip_window(axis, lands[w], layers, k),
                        mine, theirs))
        return x, y, c, 2 * x + y, (x, y, 1 - c), _other_chips(x, y), out

    def remote(src, dst, send, recv, idx, to):
        return pltpu.make_async_remote_copy(src_ref=src, dst_ref=dst, send_sem=send.at[idx], recv_sem=recv.at[idx],
                                            device_id=to, device_id_type=MESH)

    def stage0(srcs, lands, sems):
        send, recv, local = sems
        x, y, c, me, sib, chips, ps = parts(srcs, lands)
        starts, waits = [], []
        for w, (src, dst, mine, theirs) in enumerate(ps):
            lc = pltpu.make_async_copy(src, dst(mine, me), local.at[w])
            first = [remote(src, dst(mine, me), send, recv, 4 * w, sib)]
            first += [remote(src, dst(mine, me), send, recv, 4 * w + 1 + j, (cx, cy, c)) for j, (cx, cy) in enumerate(chips)]
            starts += [lc] + first
            waits.append((remote(src, dst(theirs, me), send, recv, 4 * w, (x, y, c)), "recv"))
            waits += [(remote(src, dst(mine, 2 * cx + cy), send, recv, 4 * w + 1 + j, (x, y, c)), "recv")
                      for j, (cx, cy) in enumerate(chips)]
            waits += [(cp, "send") for cp in first] + [(lc, "local")]
        return starts, waits

    def stage1(srcs, lands, sems):
        send, recv, _ = sems
        x, y, c, me, sib, chips, ps = parts(srcs, lands)
        starts, waits = [], []
        for w, (src, dst, mine, theirs) in enumerate(ps):
            for j, (cx, cy) in enumerate(chips):
                blk = dst(mine, 2 * cx + cy)
                fwd = remote(blk, blk, send, recv, 3 * w + j, sib)
                starts.append(fwd)
                waits.append((remote(src, dst(theirs, 2 * cx + cy), send, recv, 3 * w + j, (x, y, c)), "recv"))
                waits.append((fwd, "send"))
        return starts, waits

    return [stage0, stage1], [(4 * n, 4 * n, n), (3 * n, 3 * n, 0)]


def _swap_stages(spec):
    n = len(spec)

    def stage(srcs, lands, sems):
        send, recv, _ = sems
        x, y, c = _coords()
        starts, waits = [], []
        for w in range(n):
            h = srcs[w].shape[0] // 2
            cp = pltpu.make_async_remote_copy(src_ref=srcs[w].at[pl.ds((1 - c) * h, h)], dst_ref=lands[w],
                                              send_sem=send.at[w], recv_sem=recv.at[w],
                                              device_id=(x, y, 1 - c), device_id_type=MESH)
            starts.append(cp)
            waits += [(cp, "recv"), (cp, "send")]
        return starts, waits

    return [stage], [(n, n, 0)]


def _scatter_stages(spec):
    n = len(spec)

    def stage(srcs, lands, sems):
        send, recv, local = sems
        x, y, c = _coords()
        me = 2 * x + y
        starts, waits = [], []
        for w, (_, axis) in enumerate(spec):
            layers = pl.ds(0, srcs[w].shape[0])
            own = _chip_window(axis, srcs[w], layers, me)
            lc = pltpu.make_async_copy(own, lands[w].at[me], local.at[w])
            starts.append(lc)
            for j, (cx, cy) in enumerate(_other_chips(x, y)):
                cp = pltpu.make_async_remote_copy(src_ref=_chip_window(axis, srcs[w], layers, 2 * cx + cy),
                                                  dst_ref=lands[w].at[me], send_sem=send.at[3 * w + j],
                                                  recv_sem=recv.at[3 * w + j], device_id=(cx, cy, c), device_id_type=MESH)
                starts.append(cp)
                waits.append((pltpu.make_async_remote_copy(
                    src_ref=own, dst_ref=lands[w].at[2 * cx + cy], send_sem=send.at[3 * w + j], recv_sem=recv.at[3 * w + j],
                    device_id=(x, y, c), device_id_type=MESH), "recv"))
                waits.append((cp, "send"))
            waits.append((lc, "local"))
        return starts, waits

    return [stage], [(3 * n, 3 * n, n)]


def _share_stages(spec):
    n = len(spec)

    def stage(srcs, lands, sems):
        send, recv, _ = sems
        x, y, c = _coords()
        starts, waits = [], []
        for w in range(n):
            h = lands[w].shape[0] // 2
            mine, theirs = lands[w].at[pl.ds(c * h, h)], lands[w].at[pl.ds((1 - c) * h, h)]
            cp = pltpu.make_async_remote_copy(src_ref=mine, dst_ref=mine, send_sem=send.at[w], recv_sem=recv.at[w],
                                              device_id=(x, y, 1 - c), device_id_type=MESH)
            starts.append(cp)
            waits.append((pltpu.make_async_remote_copy(src_ref=theirs, dst_ref=theirs, send_sem=send.at[w],
                                                       recv_sem=recv.at[w], device_id=(x, y, c), device_id_type=MESH),
                          "recv"))
            waits.append((cp, "send"))
        return starts, waits

    return [stage], [(n, n, 0)]


def _shard_of(p, axis):
    if axis is None:
        return (p.shape[0],) + tuple(p.shape[2:])
    s = list(p.shape)
    s[1 + axis] //= N_CHIPS
    return tuple(s)


def _reduce_grads(spec, gs, core, tag, split_after=None):
    stages, counts = _swap_stages(spec)
    got = _comm_fused(stages, counts, gs,
                      [jax.ShapeDtypeStruct((g.shape[0] // 2,) + g.shape[1:], g.dtype) for g in gs], "swap_" + tag)
    pair = [_pair_sum(a, r, core, "pair_sum_" + n) for a, r, (n, _) in zip(gs, got, spec)]
    stages, counts = _scatter_stages(spec)
    land_shapes = [jax.ShapeDtypeStruct((N_CHIPS,) + _shard_of(p, axis), p.dtype) for p, (_, axis) in zip(pair, spec)]
    work = None
    if split_after is None:
        parts = _comm_fused(stages, counts, pair, land_shapes, "scatter_" + tag)
    else:
        comm = _SplitComm(stages, counts, pair, [lax.empty(s.shape, s.dtype) for s in land_shapes], "scatter_" + tag)
        token = comm.advance()
        work = split_after(token)
        comm.advance(after=jax.tree.leaves(work)[0])
        parts = comm.lands()
    halves = [_sum4(q, core, "sum4_" + n) for q, (n, _) in zip(parts, spec)]
    stages, counts = _share_stages(spec)
    shards = _comm_fused(stages, counts, [], halves, "share_" + tag, inplace=True)
    return shards, work


def _allreduce_small(vec, name):
    rows, cols = vec.shape

    def body(x_ref, o_ref, buf, send, recv):
        x, y, c = _coords()
        me = 4 * x + 2 * y + c
        buf[me] = x_ref[...]
        cps = []
        for kk in range(1, 8):
            dx, dy, dc = (kk >> 2) & 1, (kk >> 1) & 1, kk & 1
            to = (1 - x if dx else x, 1 - y if dy else y, 1 - c if dc else c)
            cp = pltpu.make_async_remote_copy(src_ref=x_ref, dst_ref=buf.at[me], send_sem=send.at[kk - 1],
                                              recv_sem=recv.at[kk - 1], device_id=to, device_id_type=MESH)
            cp.start()
            cps.append((cp, 4 * to[0] + 2 * to[1] + to[2]))
        for kk, (cp, frm) in enumerate(cps):
            pltpu.make_async_remote_copy(src_ref=x_ref, dst_ref=buf.at[frm], send_sem=send.at[kk],
                                         recv_sem=recv.at[kk], device_id=(x, y, c), device_id_type=MESH).wait_recv()
        for cp, _ in cps:
            cp.wait_send()
        acc = buf[0]
        for kk in range(1, 8):
            acc = acc + buf[kk]
        o_ref[...] = acc

    vm = pl.BlockSpec(memory_space=pltpu.VMEM)
    return pl.pallas_call(
        body, out_shape=jax.ShapeDtypeStruct((rows, cols), F32), in_specs=[vm], out_specs=vm,
        scratch_shapes=[pltpu.VMEM((8, rows, cols), F32), pltpu.SemaphoreType.DMA((7,)), pltpu.SemaphoreType.DMA((7,))],
        compiler_params=_params(), name=name)(vec)


SMALL = (("ssd_conv_w", 2), ("pool_scale", 1), ("ffn_conv_w", 2))
REPL = ("ssd_conv_b", "ssd_dt_bias", "ssd_a_log", "ssd_d", "ssd_norm_w", "ffn_conv_b",
        "norm_mix_pre", "norm_mix_post", "norm_ffn_pre", "norm_ffn_post")
WEIGHTS = ("ssd_w_in", "ssd_conv_w", "ssd_conv_b", "ssd_dt_bias", "ssd_a_log", "ssd_d", "ssd_norm_w", "ssd_w_out",
           "pool_w", "pool_scale", "ffn_w_up", "ffn_conv_w", "ffn_conv_b", "ffn_w_down", "norm_mix_pre",
           "norm_mix_post", "norm_ffn_pre", "norm_ffn_post")


def _flat_rows(n):
    unit = 2 * 16 * FLAT_COLS
    return 2 * 16 * ((n + unit - 1) // unit)


def _flatten_shards(arrs, dtype):
    flat = jnp.concatenate([a.astype(dtype).reshape(-1) for a in arrs])
    rows = _flat_rows(flat.shape[0])
    flat = jnp.pad(flat, (0, rows * FLAT_COLS - flat.shape[0]))
    return flat.reshape(2, rows // 2, FLAT_COLS)


def _unflatten_full(gathered, shard_shapes, axes):
    per_chip = jnp.swapaxes(gathered, 0, 1).reshape(N_CHIPS, -1)
    out, off = [], 0
    for shp, ax in zip(shard_shapes, axes):
        n = math.prod(shp)
        pieces = [per_chip[k, off:off + n].reshape(shp) for k in range(N_CHIPS)]
        out.append(jnp.concatenate(pieces, axis=ax))
        off += n
    return out


def kernel(x, ssd_w_in, ssd_conv_w, ssd_conv_b, ssd_dt_bias, ssd_a_log, ssd_d, ssd_norm_w, ssd_w_out, pool_w, pool_scale, ffn_w_up, ffn_conv_w, ffn_conv_b, ffn_w_down, norm_mix_pre, norm_mix_post, norm_ffn_pre, norm_ffn_post, loss_target, m_ssd_w_in, m_ssd_conv_w, m_ssd_conv_b, m_ssd_dt_bias, m_ssd_a_log, m_ssd_d, m_ssd_norm_w, m_ssd_w_out, m_pool_w, m_pool_scale, m_ffn_w_up, m_ffn_conv_w, m_ffn_conv_b, m_ffn_w_down, m_norm_mix_pre, m_norm_mix_post, m_norm_ffn_pre, m_norm_ffn_post, v_ssd_w_in, v_ssd_conv_w, v_ssd_conv_b, v_ssd_dt_bias, v_ssd_a_log, v_ssd_d, v_ssd_norm_w, v_ssd_w_out, v_pool_w, v_pool_scale, v_ffn_w_up, v_ffn_conv_w, v_ffn_conv_b, v_ffn_w_down, v_norm_mix_pre, v_norm_mix_post, v_norm_ffn_pre, v_norm_ffn_post):
    wts = dict(ssd_w_in=ssd_w_in, ssd_conv_w=ssd_conv_w, ssd_conv_b=ssd_conv_b, ssd_dt_bias=ssd_dt_bias,
               ssd_a_log=ssd_a_log, ssd_d=ssd_d, ssd_norm_w=ssd_norm_w, ssd_w_out=ssd_w_out, pool_w=pool_w,
               pool_scale=pool_scale, ffn_w_up=ffn_w_up, ffn_conv_w=ffn_conv_w, ffn_conv_b=ffn_conv_b,
               ffn_w_down=ffn_w_down, norm_mix_pre=norm_mix_pre, norm_mix_post=norm_mix_post,
               norm_ffn_pre=norm_ffn_pre, norm_ffn_post=norm_ffn_post)
    mom = dict(ssd_w_in=m_ssd_w_in, ssd_conv_w=m_ssd_conv_w, ssd_conv_b=m_ssd_conv_b, ssd_dt_bias=m_ssd_dt_bias,
               ssd_a_log=m_ssd_a_log, ssd_d=m_ssd_d, ssd_norm_w=m_ssd_norm_w, ssd_w_out=m_ssd_w_out, pool_w=m_pool_w,
               pool_scale=m_pool_scale, ffn_w_up=m_ffn_w_up, ffn_conv_w=m_ffn_conv_w, ffn_conv_b=m_ffn_conv_b,
               ffn_w_down=m_ffn_w_down, norm_mix_pre=m_norm_mix_pre, norm_mix_post=m_norm_mix_post,
               norm_ffn_pre=m_norm_ffn_pre, norm_ffn_post=m_norm_ffn_post)
    var = dict(ssd_w_in=v_ssd_w_in, ssd_conv_w=v_ssd_conv_w, ssd_conv_b=v_ssd_conv_b, ssd_dt_bias=v_ssd_dt_bias,
               ssd_a_log=v_ssd_a_log, ssd_d=v_ssd_d, ssd_norm_w=v_ssd_norm_w, ssd_w_out=v_ssd_w_out, pool_w=v_pool_w,
               pool_scale=v_pool_scale, ffn_w_up=v_ffn_w_up, ffn_conv_w=v_ffn_conv_w, ffn_conv_b=v_ffn_conv_b,
               ffn_w_down=v_ffn_w_down, norm_mix_pre=v_norm_mix_pre, norm_mix_post=v_norm_mix_post,
               norm_ffn_pre=v_norm_ffn_pre, norm_ffn_post=v_norm_ffn_post)

    bl, seq, d = x.shape
    t = bl * seq
    depth = norm_mix_pre.shape[0]
    n_ssd = ssd_w_out.shape[0]
    d_inner = ssd_w_out.shape[1] * N_CHIPS
    nheads = d_inner // HEAD_DIM
    hpg = nheads // N_GROUPS
    gw = d_inner // N_GROUPS
    xbc = ssd_conv_w.shape[2] * N_CHIPS
    f2 = ffn_w_up.shape[2] * N_CHIPS
    ff = f2 // 2
    dg = d // 4
    cy = lax.axis_index("c")
    chip = 2 * lax.axis_index("x") + lax.axis_index("y")

    small_shapes = [wts[n].shape for n, _ in SMALL]
    small_axes = [a for _, a in SMALL]
    small_flat = _flatten_shards([wts[n] for n, _ in SMALL], F32)
    small_half = lax.dynamic_index_in_dim(small_flat, cy, 0, keepdims=False)
    small_all = _allgather_halves(small_half, "gather_small")
    conv_w, p_scale, f_conv_w = _unflatten_full(small_all, small_shapes, small_axes)
    def full_shapes(spec, shards):
        return [jax.ShapeDtypeStruct(_full_shape(axis, s.shape), s.dtype) for s, (_, axis) in zip(shards, spec)]

    stages, counts = _gather_stages(MIXW)
    mix_shards = [wts[n].astype(BF16) for n, _ in MIXW]
    w_in_cm, w_out, w_pool = _comm_fused(stages, counts, mix_shards, full_shapes(MIXW, mix_shards), "gather_mixers")
    w_in = jnp.swapaxes(w_in_cm, 1, 2).reshape(n_ssd, d, -1)
    stages, counts = _gather_stages(FFNW)
    ffn_shards = [wts[n].astype(BF16) for n, _ in FFNW]
    ffn_gather = _SplitComm(stages, counts, ffn_shards + [w_pool],
                            [lax.empty(s.shape, s.dtype) for s in full_shapes(FFNW, ffn_shards)], "gather_ffn")
    gather_token = ffn_gather.advance()

    def pad_heads(a):
        lead = a.shape[:-1]
        a = a.reshape(lead + (N_GROUPS, hpg))
        a = jnp.pad(a, [(0, 0)] * len(lead) + [(0, 0), (0, LANES - hpg)])
        return a.reshape(lead + (N_GROUPS * LANES,))

    def unpad_heads(a):
        lead = a.shape[:-1]
        return a.reshape(lead + (N_GROUPS, LANES))[..., :hpg].reshape(lead + (nheads,))

    def group_rows(a, width):
        return jnp.broadcast_to(a.reshape(N_GROUPS, 1, width), (N_GROUPS, 8, width))

    w_in_p = jnp.concatenate([w_in[..., :d_inner + xbc], pad_heads(w_in[..., d_inner + xbc:])], axis=-1)
    zw = w_in_p.shape[-1]

    x2 = x.reshape(t, d)
    tgt2 = loss_target.reshape(t, d)
    w_up = w_down = None

    saved = []
    cur = x2
    tokens = []
    for i in range(depth):
        j = i // 2
        sv = dict(x_in=cur)
        if i % 2 == 0:
            h = _norm_fwd(cur, norm_mix_pre[i:i + 1], BF16, "norm_pre_b", after=[gather_token] if i == 0 else ())
            zx = _mm(h, w_in_p, "nn", F32, "mm_ssd_in", 2048, 512, d, b_layer=j).reshape(bl, seq, zw)
            xc = _ssd_conv_fwd(zx, conv_w[j], ssd_conv_b[j:j + 1], d_inner, "ssd_conv_fwd")
            dtb = group_rows(pad_heads(ssd_dt_bias[j]), LANES)
            alog = group_rows(pad_heads(ssd_a_log[j]), LANES)
            dskip = group_rows(jnp.repeat(ssd_d[j], HEAD_DIM), gw)
            nw = group_rows(ssd_norm_w[j], gw)
            y, yn, st = _ssd_fwd(xc, zx, dtb, alog, dskip, nw, d_inner, "ssd_fwd")
            if i == 0:
                tokens.append(ffn_gather.advance(after=yn))
            mix = _mm(yn.reshape(t, d_inner), w_out, "nn", F32, "mm_ssd_out", 512, 512, d_inner, b_layer=j)
            sv.update(h=h, zx=zx, xc=xc, y=y, yn=yn, st=st, dtb=dtb, alog=alog, dskip=dskip, nw=nw)
        else:
            h = _norm_fwd(cur, norm_mix_pre[i:i + 1], F32, "norm_pre_f")
            mix = _pool_fwd(h.reshape(bl, seq, d), w_pool[j], p_scale[j:j + 1], "pool_fwd").reshape(t, d)
            sv.update(h=h)
        sv.update(mix=mix)
        mid = _norm_fwd(mix, norm_mix_post[i:i + 1], F32, "norm_post", resid=cur, after=tokens)
        tokens = []
        u = _norm_fwd(mid, norm_ffn_pre[i:i + 1], BF16, "norm_pre_b")
        if i == 0:
            ffn_gather.advance(after=u)
            w_up, w_down = ffn_gather.lands()
        hpre = _mm(u, w_up, "nn", BF16, "mm_up", 2048, 512, d, b_layer=i).reshape(bl, seq, f2)
        act = _ffn_act_fwd(hpre, f_conv_w[i], ffn_conv_b[i:i + 1], "ffn_act_fwd").reshape(t, ff)
        fo = _mm(act, w_down, "nn", F32, "mm_down", 1024, 512, ff, b_layer=i)
        cur = _norm_fwd(fo, norm_ffn_post[i:i + 1], F32, "norm_post", resid=mid)
        sv.update(mid=mid, u=u, hpre=hpre, act=act, fo=fo)
        saved.append(sv)

    dcur, loss_part = _loss_head(cur, tgt2, "loss_head")

    g = {n: [None] * wts[n].shape[0] for n in WEIGHTS}
    gbuf = dict(up=lax.empty((depth, d, f2), F32), down=lax.empty((depth, ff, d), F32),
                out=lax.empty((n_ssd, d_inner, d), F32), win=lax.empty((n_ssd, d, zw), F32))
    core = cy.reshape(1).astype(jnp.int32)

    def mixer_bwd(i, dmid, behind=()):
        j = i // 2
        sv = saved[i]
        done = []
        if i % 2 == 0:
            dmix, g["norm_mix_post"][i] = _norm_bwd(sv["mix"], norm_mix_post[i:i + 1], dmid, BF16, "norm_bwd_b",
                                                    after=behind)
            dyn = _mm(dmix, w_out, "nt", F32, "mm_ssd_out_dx", 1024, 1024, d, b_layer=j)
            gbuf["out"], tok = _mm(sv["yn"].reshape(t, d_inner), dmix, "tn", F32, "mm_ssd_out_dw", 1024, 512, 2048,
                                   out_buf=(gbuf["out"], j))
            done.append(tok)
            dz, dxs, dbm, dcm, ddt, dnw, dd, dal, dbias = _ssd_bwd(
                sv["xc"], sv["zx"], sv["y"], dyn.reshape(bl, seq, d_inner), sv["st"], sv["dtb"], sv["alog"],
                sv["dskip"], sv["nw"], d_inner, "ssd_bwd")
            g["ssd_norm_w"][j] = dnw[:, 0, :].reshape(d_inner)
            g["ssd_d"][j] = dd[:, 0, :hpg].reshape(nheads)
            g["ssd_a_log"][j] = dal[:, 0, :hpg].reshape(nheads)
            g["ssd_dt_bias"][j] = dbias[:, 0, :hpg].reshape(nheads)
            dxbc, dcw, dcb = _ssd_conv_bwd(sv["zx"], (dxs, dbm, dcm), conv_w[j], ssd_conv_b[j:j + 1], d_inner,
                                           "ssd_conv_bwd")
            g["ssd_conv_w"][j] = dcw
            g["ssd_conv_b"][j] = dcb[0]
            dzs = [dz.reshape(t, d_inner), dxbc.reshape(t, xbc), ddt.reshape(t, N_GROUPS * LANES)]
            dh = _mm(dzs, w_in_p, "nt", F32, "mm_ssd_in_dx", 1024, d, 512, b_layer=j)
            gbuf["win"], tok = _mm(sv["h"], dzs, "tn", F32, "mm_ssd_in_dw", 1024, 512, 2048, out_buf=(gbuf["win"], j))
            done.append(tok)
        else:
            dmix, g["norm_mix_post"][i] = _norm_bwd(sv["mix"], norm_mix_post[i:i + 1], dmid, F32, "norm_bwd_f",
                                                    after=behind)
            dh3, g["pool_w"][j], dps = _pool_bwd(sv["h"].reshape(bl, seq, d), dmix.reshape(bl, seq, d), w_pool[j],
                                                 p_scale[j:j + 1], "pool_bwd")
            g["pool_scale"][j] = dps[0]
            dh = dh3.reshape(t, d)
        dx_in, g["norm_mix_pre"][i] = _norm_bwd(sv["x_in"], norm_mix_pre[i:i + 1], dh, F32, "norm_bwd_r", resid=dmid,
                                                after=done)
        return dx_in

    ffn_grads = None
    for i in reversed(range(depth)):
        sv = saved[i]
        dfo, g["norm_ffn_post"][i] = _norm_bwd(sv["fo"], norm_ffn_post[i:i + 1], dcur, BF16, "norm_bwd_b")
        dact = _mm(dfo, w_down, "nt", BF16, "mm_down_dx", 1024, ff // 2, d, b_layer=i)
        gbuf["down"], tok_down = _mm(sv["act"], dfo, "tn", F32, "mm_down_dw", ff // 2, 512, 2048,
                                     out_buf=(gbuf["down"], i))
        dhg, dhv, dcw, dcb = _ffn_act_bwd(sv["hpre"], dact.reshape(bl, seq, ff), f_conv_w[i], ffn_conv_b[i:i + 1],
                                          "ffn_act_bwd")
        g["ffn_conv_w"][i] = dcw
        g["ffn_conv_b"][i] = dcb[0]
        dhs = [dhg.reshape(t, ff), dhv.reshape(t, ff)]
        du = _mm(dhs, w_up, "nt", F32, "mm_up_dx", 1024, d, ff // 2, b_layer=i)
        gbuf["up"], tok_up = _mm(sv["u"], dhs, "tn", F32, "mm_up_dw", 512, ff // 2, 2048, out_buf=(gbuf["up"], i))
        dmid, g["norm_ffn_pre"][i] = _norm_bwd(sv["mid"], norm_ffn_pre[i:i + 1], du, F32, "norm_bwd_r", resid=dcur,
                                               after=[tok_down, tok_up])
        if i > 0:
            dcur = mixer_bwd(i, dmid)
        else:
            def last_mixer(token):
                return mixer_bwd(0, dmid, behind=[token])

            ffn_grads, dcur = _reduce_grads(FFNW, [gbuf["up"], gbuf["down"]], core, "ffn", split_after=last_mixer)

    grad_x = dcur.reshape(bl, seq, d)
    for n in ("norm_mix_pre", "norm_mix_post", "norm_ffn_pre", "norm_ffn_post"):
        g[n] = [a[0] for a in g[n]]
    small_names = [n for n, _ in SMALL] + list(REPL)
    full = {n: jnp.stack(g[n], axis=0) for n in small_names}

    g_in = jnp.concatenate([gbuf["win"][..., :d_inner + xbc], unpad_heads(gbuf["win"][..., d_inner + xbc:])], axis=-1)
    g_in_cm = jnp.swapaxes(g_in.reshape(n_ssd, d, N_CHIPS, -1), 1, 2)
    mix_grads, _ = _reduce_grads(MIXW, [g_in_cm, gbuf["out"], jnp.stack(g["pool_w"], axis=0)], core, "mixers")
    big_grads = {n: s for s, (n, _) in zip(mix_grads + ffn_grads, MIXW + FFNW)}

    vec = jnp.concatenate([full[n].reshape(-1) for n in small_names] + [loss_part[0, :1]])
    nvec = vec.shape[0]
    vrows = 8 * ((nvec + 8 * FLAT_COLS - 1) // (8 * FLAT_COLS))
    vec = jnp.pad(vec, (0, vrows * FLAT_COLS - nvec)).reshape(vrows, FLAT_COLS)
    tot = _allreduce_small(vec, "allreduce_small").reshape(-1)
    small_grads, off = {}, 0
    for n in small_names:
        cnt = math.prod(full[n].shape)
        small_grads[n] = tot[off:off + cnt].reshape(full[n].shape)
        off += cnt
    loss = tot[off]
    for n, ax in SMALL:
        w = wts[n].shape[ax]
        small_grads[n] = lax.dynamic_slice_in_dim(small_grads[n], chip * w, w, axis=ax)

    grads, deltas, new_m, new_v = {}, {}, {}, {}
    for n in WEIGHTS:
        gr = big_grads[n] if n in big_grads else small_grads[n]
        shp = wts[n].shape
        two = (math.prod(shp[:-1]), shp[-1])
        dl, mn, vn = _adamw(wts[n].reshape(two), gr.reshape(two), mom[n].reshape(two), var[n].reshape(two),
                            "adamw_" + n)
        grads[n], deltas[n], new_m[n], new_v[n] = gr, dl.reshape(shp), mn.reshape(shp), vn.reshape(shp)

    return (loss, grad_x, *[grads[n] for n in WEIGHTS], *[deltas[n] for n in WEIGHTS],
            *[new_m[n] for n in WEIGHTS], *[new_v[n] for n in WEIGHTS])
```

```python
import functools
import math

import jax
import jax.numpy as jnp
from jax import lax
from jax.experimental import pallas as pl
from jax.experimental.pallas import tpu as pltpu

F32 = jnp.float32
BF16 = jnp.bfloat16
MESH = pl.DeviceIdType.MESH
ANY = pl.BlockSpec(memory_space=pl.ANY)

HEAD_DIM = 64
D_STATE = 128
CHUNK = 128
N_GROUPS = 4
SSD_CONV = 4
FFN_CONV = 3
EPS = 1e-6
N_CHIPS = 4
LANES = 128
FLAT_COLS = 1024

ADAM_LR = 0.001
ADAM_B1 = 0.9
ADAM_B2 = 0.999
ADAM_EPS = 1e-08
ADAM_WD = 0.01
ADAM_STEP = 10

VMEM_LIMIT_BYTES = 56 * 1024 * 1024


def _params(sem=None):
    kw = dict(vmem_limit_bytes=VMEM_LIMIT_BYTES)
    if sem is not None:
        kw["dimension_semantics"] = sem
    return pltpu.CompilerParams(**kw)


def _sigmoid(x):
    return 1.0 / (1.0 + jnp.exp(-x))


def _softplus(x):
    return jnp.maximum(x, 0.0) + jnp.log(1.0 + jnp.exp(-jnp.abs(x)))


def _dot(a, b, dn):
    return lax.dot_general(a, b, (dn, ((), ())), preferred_element_type=F32)


def _nn(a, b):
    return _dot(a, b, ((1,), (0,)))


def _nt(a, b):
    return _dot(a, b, ((1,), (1,)))


def _tn(a, b):
    return _dot(a, b, ((0,), (0,)))


def _split(x, parts):
    out = []
    r = x
    for _ in range(parts):
        p = r.astype(BF16)
        out.append(p)
        r = r - p.astype(F32)
    return out


def _sel_left(sel, x, parts=3):
    n = x.shape[1]
    r = _nn(sel, jnp.concatenate(_split(x, parts), axis=1))
    out = r[:, 0:n]
    for i in range(1, parts):
        out = out + r[:, i * n:(i + 1) * n]
    return out


def _sel_right(x, sel_stacked, parts=3):
    return _nn(jnp.concatenate(_split(x, parts), axis=1), sel_stacked)


def _mm(a, b, dims, out_dtype, name, tm, tn, tk, b_layer=None, out_buf=None):
    a_list = list(a) if isinstance(a, (list, tuple)) else [a]
    b_list = list(b) if isinstance(b, (list, tuple)) else [b]
    if dims in ("nn", "nt"):
        assert len(b_list) == 1
        m = a_list[0].shape[0]
        segs = [x.shape[1] for x in a_list]
        k = sum(segs)
        bshape = b_list[0].shape[-2:]
        n = bshape[1] if dims == "nn" else bshape[0]
        assert (bshape[0] if dims == "nn" else bshape[1]) == k
    else:
        assert len(a_list) == 1 and b_layer is None
        k, m = a_list[0].shape
        segs = [x.shape[1] for x in b_list]
        n = sum(segs)
    tm, tn, tk = min(tm, m), min(tn, n), min(tk, k)
    if dims == "tn":
        tn = min(tn, min(segs))
    else:
        tk = min(tk, min(segs))
    unit = tk if dims != "tn" else tn
    assert m % tm == 0 and n % tn == 0 and k % tk == 0 and all(s % unit == 0 for s in segs), (name, m, n, k, segs)
    nk = k // tk
    starts = [sum(segs[:s]) // unit for s in range(len(segs))]
    counts = [s // unit for s in segs]
    nseg = len(segs)
    dn = {"nn": ((1,), (0,)), "nt": ((1,), (1,)), "tn": ((0,), (0,))}[dims]

    def body(*refs):
        a_refs = refs[:len(a_list)]
        b_refs = refs[len(a_list):len(a_list) + len(b_list)]
        rest = refs[len(a_list) + len(b_list) + (0 if out_buf is None else 1):]
        o_ref = rest[0]
        if out_buf is not None:
            rest[1][...] = jnp.zeros((8, LANES), F32)
            rest = rest[1:]
        acc = rest[1] if nk > 1 else None
        kk = pl.program_id(2)
        sel = kk if dims != "tn" else pl.program_id(1)

        def step(a_ref, b_ref):
            p = _dot(a_ref[...].astype(BF16), b_ref[...].astype(BF16), dn)
            if nk == 1:
                o_ref[...] = p.astype(out_dtype)
                return

            @pl.when(kk == 0)
            def _():
                acc[...] = p

            @pl.when(kk > 0)
            def _():
                acc[...] += p

        if nseg == 1:
            step(a_refs[0], b_refs[0])
        else:
            for s in range(nseg):
                @pl.when(jnp.logical_and(sel >= starts[s], sel < starts[s] + counts[s]))
                def _(s=s):
                    step(a_refs[s] if dims != "tn" else a_refs[0], b_refs[0] if dims != "tn" else b_refs[s])

        if nk > 1:
            @pl.when(kk == nk - 1)
            def _():
                o_ref[...] = acc[...].astype(out_dtype)

    def seg_index(v, s):
        return v if nseg == 1 else jnp.clip(v - starts[s], 0, counts[s] - 1)

    lead = () if b_layer is None else (b_layer,)
    none = () if b_layer is None else (None,)
    if dims == "nn":
        a_specs = [pl.BlockSpec((tm, tk), lambda i, j, kk, s=s: (i, seg_index(kk, s))) for s in range(nseg)]
        b_specs = [pl.BlockSpec(none + (tk, tn), lambda i, j, kk: lead + (kk, j))]
    elif dims == "nt":
        a_specs = [pl.BlockSpec((tm, tk), lambda i, j, kk, s=s: (i, seg_index(kk, s))) for s in range(nseg)]
        b_specs = [pl.BlockSpec(none + (tn, tk), lambda i, j, kk: lead + (j, kk))]
    else:
        a_specs = [pl.BlockSpec((tk, tm), lambda i, j, kk: (kk, i))]
        b_specs = [pl.BlockSpec((tk, tn), lambda i, j, kk, s=s: (kk, seg_index(j, s))) for s in range(nseg)]
    args = a_list + b_list
    in_specs = a_specs + b_specs
    aliases = {}
    if out_buf is None:
        out_shape = jax.ShapeDtypeStruct((m, n), out_dtype)
        out_spec = pl.BlockSpec((tm, tn), lambda i, j, kk: (i, j))
    else:
        buf, slab = out_buf
        assert buf.shape[1:] == (m, n) and buf.dtype == out_dtype
        out_shape = (jax.ShapeDtypeStruct(buf.shape, out_dtype), jax.ShapeDtypeStruct((8, LANES), F32))
        out_spec = (pl.BlockSpec((None, tm, tn), lambda i, j, kk: (slab, i, j)),
                    pl.BlockSpec((8, LANES), lambda i, j, kk: (0, 0)))
        aliases = {len(args): 0}
        args = args + [buf]
        in_specs = in_specs + [ANY]
    return pl.pallas_call(
        body,
        out_shape=out_shape,
        grid=(m // tm, n // tn, nk),
        in_specs=in_specs,
        out_specs=out_spec,
        scratch_shapes=[] if nk == 1 else [pltpu.VMEM((tm, tn), F32)],
        input_output_aliases=aliases,
        compiler_params=_params(("parallel", "parallel", "arbitrary") if out_buf is None else ("arbitrary",) * 3),
        name=name,
    )(*args)


def _row_tile(t, want):
    tm = min(want, t)
    assert t % tm == 0
    return tm


def _norm_fwd(x, w, out_dtype, name, resid=None, after=()):
    t, d = x.shape
    tm = _row_tile(t, 512)
    after = [a for a in after if a is not None]

    def body(*refs):
        refs = refs[:len(refs) - 1 - len(after)] + refs[len(refs) - 1:]
        if resid is None:
            x_ref, w_ref, o_ref = refs
        else:
            x_ref, w_ref, r_ref, o_ref = refs
        xv = x_ref[...]
        r = lax.rsqrt(jnp.mean(xv * xv, axis=-1, keepdims=True) + EPS)
        y = (xv * r) * w_ref[...]
        if resid is not None:
            y = r_ref[...] + y
        o_ref[...] = y.astype(out_dtype)

    row = pl.BlockSpec((tm, d), lambda i: (i, 0))
    vec = pl.BlockSpec((1, d), lambda i: (0, 0))
    args = [x, w] + ([] if resid is None else [resid]) + after
    return pl.pallas_call(
        body, out_shape=jax.ShapeDtypeStruct((t, d), out_dtype), grid=(t // tm,),
        in_specs=[row, vec] + ([] if resid is None else [row]) + [ANY] * len(after), out_specs=row,
        compiler_params=_params(("parallel",)), name=name)(*args)


def _norm_bwd(src, w, dy, out_dtype, name, resid=None, after=()):
    t, d = src.shape
    tm = _row_tile(t, 512)
    after = [a for a in after if a is not None]

    def body(*refs):
        refs = refs[:len(refs) - 2 - len(after)] + refs[len(refs) - 2:]
        if resid is None:
            x_ref, w_ref, g_ref, o_ref, dw_ref = refs
        else:
            x_ref, w_ref, g_ref, r_ref, o_ref, dw_ref = refs
        xv = x_ref[...]
        g = g_ref[...].astype(F32)
        r = lax.rsqrt(jnp.mean(xv * xv, axis=-1, keepdims=True) + EPS)
        xh = xv * r
        gh = g * w_ref[...]
        mean = jnp.mean(gh * xh, axis=-1, keepdims=True)
        dx = r * (gh - xh * mean)
        if resid is not None:
            dx = r_ref[...] + dx
        o_ref[...] = dx.astype(out_dtype)
        part = jnp.sum(g * xh, axis=0, keepdims=True)

        @pl.when(pl.program_id(0) == 0)
        def _():
            dw_ref[...] = part

        @pl.when(pl.program_id(0) > 0)
        def _():
            dw_ref[...] += part

    row = pl.BlockSpec((tm, d), lambda i: (i, 0))
    vec = pl.BlockSpec((1, d), lambda i: (0, 0))
    args = [src, w, dy] + ([] if resid is None else [resid]) + after
    return pl.pallas_call(
        body,
        out_shape=(jax.ShapeDtypeStruct((t, d), out_dtype), jax.ShapeDtypeStruct((1, d), F32)),
        grid=(t // tm,),
        in_specs=[row, vec, row] + ([] if resid is None else [row]) + [ANY] * len(after),
        out_specs=(row, vec),
        compiler_params=_params(("arbitrary",)), name=name)(*args)


def _loss_head(y, target, name):
    t, d = y.shape
    tm = _row_tile(t, 512)

    def body(y_ref, t_ref, dy_ref, l_ref):
        e = y_ref[...] - t_ref[...]
        dy_ref[...] = e * (1.0 / d)
        col = jnp.sum(e * e, axis=0, keepdims=True)
        s = jnp.sum(col, axis=1, keepdims=True) * (0.5 / d)
        part = jnp.broadcast_to(s, (1, LANES))

        @pl.when(pl.program_id(0) == 0)
        def _():
            l_ref[...] = part

        @pl.when(pl.program_id(0) > 0)
        def _():
            l_ref[...] += part

    row = pl.BlockSpec((tm, d), lambda i: (i, 0))
    return pl.pallas_call(
        body,
        out_shape=(jax.ShapeDtypeStruct((t, d), F32), jax.ShapeDtypeStruct((1, LANES), F32)),
        grid=(t // tm,), in_specs=[row, row],
        out_specs=(row, pl.BlockSpec((1, LANES), lambda i: (0, 0))),
        compiler_params=_params(("arbitrary",)), name=name)(y, target)


def _window(ref, c, rows, seq, before, after):
    r0 = pl.multiple_of(c * rows, rows)
    parts = []
    if before:
        h0 = pl.multiple_of(jnp.maximum(r0 - before, 0), before)
        halo = ref[pl.ds(h0, before), :].astype(F32)
        parts.append(jnp.where(c > 0, halo, 0.0))
    parts.append(ref[pl.ds(r0, rows), :].astype(F32))
    if after:
        h1 = pl.multiple_of(jnp.minimum(r0 + rows, seq - after), after)
        halo = ref[pl.ds(h1, after), :].astype(F32)
        parts.append(jnp.where(c < seq // rows - 1, halo, 0.0))
    return parts[0] if len(parts) == 1 else jnp.concatenate(parts, axis=0)


def _lag(x, k):
    return pltpu.roll(x, k, 0) if k else x


def _lead(x, k):
    return pltpu.roll(x, x.shape[0] - k, 0) if k else x


SHIFT_ROWS = 128
SHIFT_COLS = 256


HALO = 16


def _conv3(ext, w, bias):
    acc = bias + w[2:3, :] * ext[HALO:, :]
    acc = acc + w[1:2, :] * _lag(ext, 1)[HALO:, :]
    return acc + w[0:1, :] * _lag(ext, 2)[HALO:, :]


def _ffn_act_fwd(hpre, cw, cb, name):
    b, seq, f2 = hpre.shape
    cbk = SHIFT_COLS
    nj = f2 // (2 * cbk)
    rows = min(SHIFT_ROWS, seq)

    def body(g_ref, v_ref, wg_ref, wv_ref, bg_ref, bv_ref, o_ref, pg_ref, pv_ref):
        def chunk(c, carry):
            gate = _conv3(_window(g_ref, c, rows, seq, HALO, 0), wg_ref[...], bg_ref[...])
            val = _conv3(_window(v_ref, c, rows, seq, HALO, 0), wv_ref[...], bv_ref[...])
            a = gate * _sigmoid(gate) * val
            here = pl.ds(pl.multiple_of(c * rows, rows), rows)
            o_ref[here, :] = a.astype(BF16)
            pg_ref[here, :] = gate.astype(BF16)
            pv_ref[here, :] = val.astype(BF16)
            return carry

        lax.fori_loop(0, seq // rows, chunk, 0)

    blk = lambda off: pl.BlockSpec((None, seq, cbk), lambda i, j: (i, 0, j + off))
    wsp = lambda r, off: pl.BlockSpec((r, cbk), lambda i, j: (0, j + off))
    half = jax.ShapeDtypeStruct((b, seq, f2 // 2), BF16)
    return pl.pallas_call(
        body, out_shape=(half, half, half), grid=(b, nj),
        in_specs=[blk(0), blk(nj), wsp(FFN_CONV, 0), wsp(FFN_CONV, nj), wsp(1, 0), wsp(1, nj)],
        out_specs=(blk(0), blk(0), blk(0)),
        compiler_params=_params(("parallel", "parallel")), name=name)(hpre, hpre, cw, cw, cb, cb)


def _ffn_act_bwd(hpre, pre_g, pre_v, da, cw, name):
    b, seq, f2 = hpre.shape
    cbk = SHIFT_COLS
    nj = f2 // (2 * cbk)
    rows = min(SHIFT_ROWS, seq)

    def body(g_ref, v_ref, pg_ref, pv_ref, da_ref, wg_ref, wv_ref, og_ref, ov_ref, dwg_ref, dwv_ref, dbg_ref, dbv_ref):
        wg, wv = wg_ref[...], wv_ref[...]

        def back(dpre, w, o_ref, x_ref, c, carry):
            here = pl.ds(pl.multiple_of(c * rows, rows), rows)
            leads = [dpre, _lead(dpre, 1), _lead(dpre, 2)]
            dx = w[2:3, :] * leads[0] + w[1:2, :] * leads[1] + w[0:1, :] * leads[2]
            o_ref[here, :] = dx[:rows, :].astype(BF16)
            x0 = x_ref[here, :].astype(F32)
            return tuple(carry[k] + jnp.sum(leads[k][:rows, :] * x0, axis=0, keepdims=True) for k in range(FFN_CONV)) + (
                carry[FFN_CONV] + jnp.sum(dpre[:rows, :], axis=0, keepdims=True),)

        def chunk(c, carry):
            cg, cv = carry
            gate = _window(pg_ref, c, rows, seq, 0, HALO)
            val = _window(pv_ref, c, rows, seq, 0, HALO)
            dav = _window(da_ref, c, rows, seq, 0, HALO)
            sg = _sigmoid(gate)
            cg = back(dav * val * (sg * (1.0 + gate * (1.0 - sg))), wg, og_ref, g_ref, c, cg)
            cv = back(dav * (gate * sg), wv, ov_ref, v_ref, c, cv)
            return cg, cv

        z = jnp.zeros((1, cbk), F32)
        cg, cv = lax.fori_loop(0, seq // rows, chunk, ((z,) * (FFN_CONV + 1), (z,) * (FFN_CONV + 1)))
        dwg = jnp.concatenate([cg[2], cg[1], cg[0]], axis=0)
        dwv = jnp.concatenate([cv[2], cv[1], cv[0]], axis=0)

        @pl.when(pl.program_id(1) == 0)
        def _():
            dwg_ref[...] = dwg
            dwv_ref[...] = dwv
            dbg_ref[...] = cg[FFN_CONV]
            dbv_ref[...] = cv[FFN_CONV]

        @pl.when(pl.program_id(1) > 0)
        def _():
            dwg_ref[...] += dwg
            dwv_ref[...] += dwv
            dbg_ref[...] += cg[FFN_CONV]
            dbv_ref[...] += cv[FFN_CONV]

    blk = lambda off: pl.BlockSpec((None, seq, cbk), lambda j, i: (i, 0, j + off))
    wsp = lambda r, off: pl.BlockSpec((r, cbk), lambda j, i: (0, j + off))
    half = jax.ShapeDtypeStruct((b, seq, f2 // 2), BF16)
    dwshape = jax.ShapeDtypeStruct((FFN_CONV, f2 // 2), F32)
    dbshape = jax.ShapeDtypeStruct((1, f2 // 2), F32)
    dg, dv, dwg, dwv, dbg, dbv = pl.pallas_call(
        body,
        out_shape=(half, half, dwshape, dwshape, dbshape, dbshape),
        grid=(nj, b),
        in_specs=[blk(0), blk(nj), blk(0), blk(0), blk(0), wsp(FFN_CONV, 0), wsp(FFN_CONV, nj)],
        out_specs=(blk(0), blk(0), wsp(FFN_CONV, 0), wsp(FFN_CONV, 0), wsp(1, 0), wsp(1, 0)),
        compiler_params=_params(("parallel", "arbitrary")), name=name)(hpre, hpre, pre_g, pre_v, da, cw, cw)
    return dg, dv, jnp.concatenate([dwg, dwv], axis=1), jnp.concatenate([dbg, dbv], axis=1)


def _ssd_conv_fwd(zx, cw, cb, d_inner, name):
    b, seq, _ = zx.shape
    xbc = cw.shape[1]
    cbk = SHIFT_COLS
    off = d_inner // cbk
    rows = min(SHIFT_ROWS, seq)

    def body(h_ref, w_ref, b_ref, o_ref, p_ref):
        w = w_ref[...]
        bias = b_ref[...]

        def chunk(c, carry):
            ext = _window(h_ref, c, rows, seq, 8, 0)
            acc = bias + w[3:4, :] * ext[8:, :]
            for k in range(1, SSD_CONV):
                acc = acc + w[3 - k:4 - k, :] * _lag(ext, k)[8:, :]
            here = pl.ds(pl.multiple_of(c * rows, rows), rows)
            o_ref[here, :] = acc * _sigmoid(acc)
            p_ref[here, :] = acc.astype(BF16)
            return carry

        lax.fori_loop(0, seq // rows, chunk, 0)

    blk = pl.BlockSpec((None, seq, cbk), lambda i, j: (i, 0, j))
    return pl.pallas_call(
        body, out_shape=(jax.ShapeDtypeStruct((b, seq, xbc), F32), jax.ShapeDtypeStruct((b, seq, xbc), BF16)),
        grid=(b, xbc // cbk),
        in_specs=[pl.BlockSpec((None, seq, cbk), lambda i, j: (i, 0, j + off)),
                  pl.BlockSpec((SSD_CONV, cbk), lambda i, j: (0, j)),
                  pl.BlockSpec((1, cbk), lambda i, j: (0, j))],
        out_specs=(blk, blk),
        compiler_params=_params(("parallel", "parallel")), name=name)(zx, cw, cb)


def _ssd_conv_bwd(zx, pre, dparts, cw, d_inner, name):
    b, seq, _ = zx.shape
    xbc = cw.shape[1]
    cbk = SHIFT_COLS
    off = d_inner // cbk
    rows = min(SHIFT_ROWS, seq)
    nblk = [p.shape[2] // cbk for p in dparts]
    first = [sum(nblk[:s]) for s in range(len(dparts))]
    assert sum(nblk) == xbc // cbk

    def body(h_ref, p_ref, gx_ref, gb_ref, gc_ref, w_ref, o_ref, dw_ref, db_ref):
        w = w_ref[...]
        j = pl.program_id(0)

        def chunk(c, carry):
            dws, dbias = carry
            here = pl.ds(pl.multiple_of(c * rows, rows), rows)
            pre = _window(p_ref, c, rows, seq, 0, HALO)
            s = _sigmoid(pre)
            gsel = jnp.where(j < first[1], _window(gx_ref, c, rows, seq, 0, HALO),
                             jnp.where(j < first[2], _window(gb_ref, c, rows, seq, 0, HALO),
                                       _window(gc_ref, c, rows, seq, 0, HALO)))
            dpre = gsel * (s * (1.0 + pre * (1.0 - s)))
            leads = [dpre] + [_lead(dpre, k) for k in range(1, SSD_CONV)]
            dx = w[3:4, :] * leads[0]
            for k in range(1, SSD_CONV):
                dx = dx + w[3 - k:4 - k, :] * leads[k]
            o_ref[here, :] = dx[:rows, :].astype(BF16)
            x0 = h_ref[here, :]
            dws = tuple(dws[k] + jnp.sum(leads[k][:rows, :] * x0, axis=0, keepdims=True) for k in range(SSD_CONV))
            dbias = dbias + jnp.sum(dpre[:rows, :], axis=0, keepdims=True)
            return dws, dbias

        z = jnp.zeros((1, cbk), F32)
        dws, dbias = lax.fori_loop(0, seq // rows, chunk, ((z,) * SSD_CONV, z))
        dwv = jnp.concatenate([dws[3 - i] for i in range(SSD_CONV)], axis=0)

        @pl.when(pl.program_id(1) == 0)
        def _():
            dw_ref[...] = dwv
            db_ref[...] = dbias

        @pl.when(pl.program_id(1) > 0)
        def _():
            dw_ref[...] += dwv
            db_ref[...] += dbias

    return pl.pallas_call(
        body,
        out_shape=(jax.ShapeDtypeStruct((b, seq, xbc), BF16), jax.ShapeDtypeStruct((SSD_CONV, xbc), F32),
                   jax.ShapeDtypeStruct((1, xbc), F32)),
        grid=(xbc // cbk, b),
        in_specs=[pl.BlockSpec((None, seq, cbk), lambda j, i: (i, 0, j + off)),
                  pl.BlockSpec((None, seq, cbk), lambda j, i: (i, 0, j))] + [
                  pl.BlockSpec((None, seq, cbk), lambda j, i, s=s: (i, 0, jnp.clip(j - first[s], 0, nblk[s] - 1)))
                  for s in range(3)] + [
                  pl.BlockSpec((SSD_CONV, cbk), lambda j, i: (0, j))],
        out_specs=(pl.BlockSpec((None, seq, cbk), lambda j, i: (i, 0, j)),
                   pl.BlockSpec((SSD_CONV, cbk), lambda j, i: (0, j)),
                   pl.BlockSpec((1, cbk), lambda j, i: (0, j))),
        compiler_params=_params(("parallel", "arbitrary")), name=name)(zx, pre, *dparts, cw)


def _pool_sums(q, g, lead):
    sh = _lead if lead else _lag
    s2 = q + sh(q, 1)
    s4 = s2 + sh(s2, 2)
    s8 = s4 + sh(s4, 4)
    s16 = s8 + sh(s8, 8)
    return jnp.where(g == 0, s2, jnp.where(g == 1, s4, jnp.where(g == 2, s8, s16)))


def _pool_count(r0, n, g, shape):
    t = (r0 + lax.broadcasted_iota(jnp.int32, shape, 0) + 1).astype(F32)
    return jnp.minimum(t, (2 << g).astype(F32))


def _pool_fwd(h, pw, scale, name):
    b, seq, d = h.shape
    dg = d // 4
    rows = min(SHIFT_ROWS, seq)

    def body(h_ref, w_ref, s_ref, o_ref):
        g = pl.program_id(1)
        wmat = w_ref[...]
        sc = s_ref[...]

        def chunk(c, carry):
            r0 = c * rows
            ext = _window(h_ref, c, rows, seq, 16, 0)
            sums = _pool_sums(ext, g, False)[16:, :]
            mixed = sums / _pool_count(r0, rows, g, (rows, dg)) - ext[16:, :]
            o_ref[pl.ds(pl.multiple_of(r0, rows), rows), :] = _nn(mixed.astype(BF16), wmat) * sc
            return carry

        lax.fori_loop(0, seq // rows, chunk, 0)

    return pl.pallas_call(
        body, out_shape=jax.ShapeDtypeStruct((b, seq, d), F32), grid=(b, 4),
        in_specs=[pl.BlockSpec((None, seq, dg), lambda i, g: (i, 0, g)),
                  pl.BlockSpec((None, dg, dg), lambda i, g: (g, 0, 0)),
                  pl.BlockSpec((1, dg), lambda i, g: (0, g))],
        out_specs=pl.BlockSpec((None, seq, dg), lambda i, g: (i, 0, g)),
        compiler_params=_params(("parallel", "parallel")), name=name)(h, pw, scale)


def _pool_bwd(h, dout, pw, scale, name):
    b, seq, d = h.shape
    dg = d // 4
    rows = min(SHIFT_ROWS, seq)

    def body(h_ref, g_ref, w_ref, s_ref, o_ref, dw_ref, ds_ref, dw_acc):
        g = pl.program_id(0)
        wmat = w_ref[...]
        sc = s_ref[...]
        dw_acc[...] = jnp.zeros_like(dw_acc)

        def chunk(c, dsc):
            r0 = c * rows
            ext = _window(h_ref, c, rows, seq, 16, 0)
            sums = _pool_sums(ext, g, False)[16:, :]
            mixed = (sums / _pool_count(r0, rows, g, (rows, dg)) - ext[16:, :]).astype(BF16)
            gext = _window(g_ref, c, rows, seq, 0, 16)
            dsc = dsc + jnp.sum(gext[:rows, :] * _nn(mixed, wmat), axis=0, keepdims=True)
            dpre = (gext * sc).astype(BF16)
            dw_acc[...] += _tn(mixed, dpre[:rows, :])
            dmix = _nt(dpre, wmat)
            q = dmix / _pool_count(r0, rows + 16, g, (rows + 16, dg))
            back = _pool_sums(q, g, True)
            o_ref[pl.ds(pl.multiple_of(r0, rows), rows), :] = back[:rows, :] - dmix[:rows, :]
            return dsc

        dsc = lax.fori_loop(0, seq // rows, chunk, jnp.zeros((1, dg), F32))

        @pl.when(pl.program_id(1) == 0)
        def _():
            dw_ref[...] = dw_acc[...]
            ds_ref[...] = dsc

        @pl.when(pl.program_id(1) > 0)
        def _():
            dw_ref[...] += dw_acc[...]
            ds_ref[...] += dsc

    return pl.pallas_call(
        body,
        out_shape=(jax.ShapeDtypeStruct((b, seq, d), F32), jax.ShapeDtypeStruct((4, dg, dg), F32),
                   jax.ShapeDtypeStruct((1, d), F32)),
        grid=(4, b),
        in_specs=[pl.BlockSpec((None, seq, dg), lambda g, i: (i, 0, g)),
                  pl.BlockSpec((None, seq, dg), lambda g, i: (i, 0, g)),
                  pl.BlockSpec((None, dg, dg), lambda g, i: (g, 0, 0)),
                  pl.BlockSpec((1, dg), lambda g, i: (0, g))],
        out_specs=(pl.BlockSpec((None, seq, dg), lambda g, i: (i, 0, g)),
                   pl.BlockSpec((None, dg, dg), lambda g, i: (g, 0, 0)),
                   pl.BlockSpec((1, dg), lambda g, i: (0, g))),
        scratch_shapes=[pltpu.VMEM((dg, dg), F32)],
        compiler_params=_params(("parallel", "arbitrary")), name=name)(h, dout, pw, scale)


def _head_of(channel):
    return jnp.right_shift(channel, HEAD_DIM.bit_length() - 1)


def _ssd_consts(gw):
    q = CHUNK
    row = lax.broadcasted_iota(jnp.int32, (q, q), 0)
    col = lax.broadcasted_iota(jnp.int32, (q, q), 1)
    tril = (row >= col).astype(BF16)
    triu = (row <= col).astype(BF16)
    e = (_head_of(lax.broadcasted_iota(jnp.int32, (LANES, gw), 1))
         == lax.broadcasted_iota(jnp.int32, (LANES, gw), 0)).astype(BF16)
    et = (_head_of(lax.broadcasted_iota(jnp.int32, (gw, LANES), 0))
          == lax.broadcasted_iota(jnp.int32, (gw, LANES), 1)).astype(BF16)
    return row, col, tril, triu, e, et


def _ssd_common(dtr, dtb, alog, gw):
    q = CHUNK
    row, col, tril, triu, e, et = _ssd_consts(gw)
    dt = _softplus(dtr + dtb)
    a_row = -jnp.exp(alog)
    acum = _sel_left(tril, dt * a_row)
    ac_last = jnp.sum(jnp.where(row == q - 1, acum, 0.0), axis=0, keepdims=True)
    eac = jnp.exp(acum)
    de = jnp.exp(ac_last - acum)
    e2 = jnp.concatenate([e, e], axis=0)
    expand = _sel_right(jnp.concatenate([dt, eac, de], axis=0), e2, 2)
    dt_x, eac_x, de_x = expand[0:q], expand[q:2 * q], expand[2 * q:3 * q]
    acum_t = acum.T
    cd_col = jnp.exp(acum_t[:, q - 1:q])
    et3 = jnp.concatenate([et, et, et], axis=1)
    cdmat = _nn(et3, jnp.concatenate(_split(jnp.broadcast_to(cd_col, (LANES, D_STATE)), 3), axis=0))
    consts = dict(row=row, col=col, tril=tril, triu=triu, e=e, et=et)
    return dt, a_row, acum, acum_t, ac_last, eac, de, dt_x, eac_x, de_x, cdmat, consts


def _decay(acum, acum_t, j, row, col):
    diff = acum[:, j:j + 1] - acum_t[j:j + 1, :]
    return jnp.exp(jnp.where(row >= col, diff, -1e30))


def _ssd_fwd(xc, zx, dtb, alog, dskip, nw, d_inner, name):
    b, seq, xbc = xc.shape
    q = CHUNK
    nc = seq // q
    gw = d_inner // N_GROUPS
    nh = gw // HEAD_DIM
    xb0 = d_inner // D_STATE
    xc0 = xb0 + N_GROUPS
    dt0 = (d_inner + xbc) // LANES

    def body(x_ref, b_ref, c_ref, z_ref, dtr_ref, dtb_ref, al_ref, dsk_ref, nw_ref, y_ref, yn_ref, st_ref, s_ref):
        @pl.when(pl.program_id(2) == 0)
        def _():
            s_ref[...] = jnp.zeros_like(s_ref)

        prev = s_ref[...]
        st_ref[...] = prev
        x = x_ref[...]
        bm = b_ref[...].astype(BF16)
        cm = c_ref[...].astype(BF16)
        (dt, a_row, acum, acum_t, ac_last, eac, de, dt_x, eac_x, de_x, cdmat, k) = _ssd_common(
            dtr_ref[...], dtb_ref[0:1, :], al_ref[0:1, :], gw)
        xdt = x * dt_x
        xdt_b = xdt.astype(BF16)
        cb = _nt(cm, bm)
        half = _head_of(lax.broadcasted_iota(jnp.int32, (q, LANES), 1))
        pairs = []
        for j in range(nh):
            pc = (j // 2) * LANES
            m = (cb * _decay(acum, acum_t, j, k["row"], k["col"])).astype(BF16)
            yj = jnp.where(half == j % 2, _nn(m, xdt_b[:, pc:pc + LANES]), 0.0)
            if j % 2 == 0:
                pairs.append(yj)
            else:
                pairs[-1] = pairs[-1] + yj
        prev_b = prev.astype(BF16)
        y = dsk_ref[0:1, :] * x + jnp.concatenate(pairs, axis=1) + eac_x * _nt(cm, prev_b)
        s_ref[...] = cdmat * prev + _tn((xdt * de_x).astype(BF16), bm)
        y_ref[...] = y
        z = z_ref[...]
        yg = y * (z * _sigmoid(z))
        r = lax.rsqrt(jnp.mean(yg * yg, axis=-1, keepdims=True) + EPS)
        yn_ref[...] = ((yg * r) * nw_ref[0:1, :]).astype(BF16)

    par = lambda w: pl.BlockSpec((None, 8, w), lambda i, g, c: (g, 0, 0))
    return pl.pallas_call(
        body,
        out_shape=(jax.ShapeDtypeStruct((b, seq, d_inner), F32), jax.ShapeDtypeStruct((b, seq, d_inner), BF16),
                   jax.ShapeDtypeStruct((b, nc, N_GROUPS, gw, D_STATE), F32)),
        grid=(b, N_GROUPS, nc),
        in_specs=[pl.BlockSpec((None, q, gw), lambda i, g, c: (i, c, g)),
                  pl.BlockSpec((None, q, D_STATE), lambda i, g, c: (i, c, xb0 + g)),
                  pl.BlockSpec((None, q, D_STATE), lambda i, g, c: (i, c, xc0 + g)),
                  pl.BlockSpec((None, q, gw), lambda i, g, c: (i, c, g)),
                  pl.BlockSpec((None, q, LANES), lambda i, g, c: (i, c, dt0 + g)),
                  par(LANES), par(LANES), par(gw), par(gw)],
        out_specs=(pl.BlockSpec((None, q, gw), lambda i, g, c: (i, c, g)),
                   pl.BlockSpec((None, q, gw), lambda i, g, c: (i, c, g)),
                   pl.BlockSpec((None, None, None, gw, D_STATE), lambda i, g, c: (i, c, g, 0, 0))),
        scratch_shapes=[pltpu.VMEM((gw, D_STATE), F32)],
        compiler_params=_params(("parallel", "parallel", "arbitrary")), name=name,
    )(xc, xc, xc, zx, zx, dtb, alog, dskip, nw)


def _ssd_bwd(xc, zx, y, dyn, st, dtb, alog, dskip, nw, d_inner, name):
    b, seq, xbc = xc.shape
    q = CHUNK
    nc = seq // q
    gw = d_inner // N_GROUPS
    nh = gw // HEAD_DIM
    xb0 = d_inner // D_STATE
    xc0 = xb0 + N_GROUPS
    dt0 = (d_inner + xbc) // LANES

    def body(x_ref, b_ref, c_ref, z_ref, dtr_ref, y_ref, g_ref, st_ref, dtb_ref, al_ref, dsk_ref, nw_ref,
             dz_ref, dx_ref, db_ref, dc_ref, ddt_ref, dnw_ref, dd_ref, dal_ref, dbias_ref,
             ds_ref, colbuf, rowbuf):
        first = jnp.logical_and(pl.program_id(1) == 0, pl.program_id(2) == 0)

        @pl.when(pl.program_id(2) == 0)
        def _():
            ds_ref[...] = jnp.zeros_like(ds_ref)

        x = x_ref[...]
        bm = b_ref[...].astype(BF16)
        cm = c_ref[...].astype(BF16)
        z = z_ref[...]
        y = y_ref[...]
        prev = st_ref[...]
        dtr = dtr_ref[...] + dtb_ref[0:1, :]
        (dt, a_row, acum, acum_t, ac_last, eac, de, dt_x, eac_x, de_x, cdmat, k) = _ssd_common(
            dtr_ref[...], dtb_ref[0:1, :], al_ref[0:1, :], gw)
        row, col = k["row"], k["col"]
        et2 = jnp.concatenate([k["et"], k["et"]], axis=0)

        sz = _sigmoid(z)
        silu_z = z * sz
        yg = y * silu_z
        r = lax.rsqrt(jnp.mean(yg * yg, axis=-1, keepdims=True) + EPS)
        xh = yg * r
        dyn = g_ref[...]
        gh = dyn * nw_ref[0:1, :]
        dyg = r * (gh - xh * jnp.mean(gh * xh, axis=-1, keepdims=True))
        dnw = jnp.sum(dyn * xh, axis=0, keepdims=True)
        g = dyg * silu_z
        dz_ref[...] = (dyg * y * (sz * (1.0 + z * (1.0 - sz)))).astype(BF16)
        dd = _sel_right(jnp.broadcast_to(jnp.sum(g * x, axis=0, keepdims=True), (8, gw)), et2, 2)

        xdt = x * dt_x
        xdt_b = xdt.astype(BF16)
        g_b = g.astype(BF16)
        prev_b = prev.astype(BF16)
        cb = _nt(cm, bm)

        cp = _nt(cm, prev_b)
        ge = g * eac_x
        dac = _sel_right(ge * cp, et2, 2)
        ge_b = ge.astype(BF16)
        dcm = _nn(ge_b, prev_b)
        dprev = _tn(ge_b, cm)

        colbuf[...] = jnp.zeros_like(colbuf)
        rowbuf[...] = jnp.zeros_like(rowbuf)
        dcb = jnp.zeros((q, q), F32)
        half = _head_of(lax.broadcasted_iota(jnp.int32, (q, LANES), 1))
        pairs = []
        for j in range(nh):
            pc = (j // 2) * LANES
            dec = _decay(acum, acum_t, j, row, col)
            m = cb * dec
            gj = jnp.where(half == j % 2, g[:, pc:pc + LANES], 0.0).astype(BF16)
            dm = _nt(gj, xdt_b[:, pc:pc + LANES])
            w = dm * m
            colbuf[:, j:j + 1] = jnp.sum(w, axis=1, keepdims=True)
            rowbuf[j:j + 1, :] = jnp.sum(w, axis=0, keepdims=True)
            dcb = dcb + dm * dec
            dj = jnp.where(half == j % 2, _tn(m.astype(BF16), g_b[:, pc:pc + LANES]), 0.0)
            if j % 2 == 0:
                pairs.append(dj)
            else:
                pairs[-1] = pairs[-1] + dj
        dxdt = jnp.concatenate(pairs, axis=1)
        dcb_b = dcb.astype(BF16)
        dcm = dcm + _nn(dcb_b, bm)
        dbm = _tn(dcb_b, cm)

        ds = ds_ref[...]
        ds_b = ds.astype(BF16)
        u = _nt(bm, ds_b)
        dxdt = dxdt + u * de_x
        dde = _sel_right(u * xdt, et2, 2)
        dbm = dbm + _nn((xdt * de_x).astype(BF16), ds_b)
        pm = jnp.concatenate(_split(ds * prev, 2), axis=1)
        t2 = _tn(pm, k["et"])
        dcd_row = jnp.sum(t2[0:D_STATE] + t2[D_STATE:2 * D_STATE], axis=0, keepdims=True)
        last = dcd_row * jnp.exp(ac_last) + jnp.sum(dde * de, axis=0, keepdims=True)
        dac = dac + colbuf[...] - rowbuf[...].T - dde * de + jnp.where(row == q - 1, last, 0.0)
        ds_ref[...] = cdmat * ds + dprev

        dadt = _sel_left(k["triu"], dac)
        ddt = _sel_right(dxdt * x, et2, 2) + dadt * a_row
        dal = jnp.sum(dadt * dt, axis=0, keepdims=True) * a_row
        lane = lax.broadcasted_iota(jnp.int32, (q, LANES), 1)
        ddtr = jnp.where(lane < nh, ddt * _sigmoid(dtr), 0.0)
        ddt_ref[...] = ddtr.astype(BF16)
        dbias = jnp.sum(ddtr, axis=0, keepdims=True)
        dx_ref[...] = dxdt * dt_x + dsk_ref[0:1, :] * g
        db_ref[...] = dbm
        dc_ref[...] = dcm

        @pl.when(first)
        def _():
            dnw_ref[...] = jnp.broadcast_to(dnw, (8, gw))
            dd_ref[...] = dd
            dal_ref[...] = jnp.broadcast_to(dal, (8, LANES))
            dbias_ref[...] = jnp.broadcast_to(dbias, (8, LANES))

        @pl.when(jnp.logical_not(first))
        def _():
            dnw_ref[...] += jnp.broadcast_to(dnw, (8, gw))
            dd_ref[...] += dd
            dal_ref[...] += jnp.broadcast_to(dal, (8, LANES))
            dbias_ref[...] += jnp.broadcast_to(dbias, (8, LANES))

    rc = lambda c: nc - 1 - c
    par = lambda w: pl.BlockSpec((None, 8, w), lambda g, i, c: (g, 0, 0))
    blk = lambda w: pl.BlockSpec((None, q, w), lambda g, i, c: (i, rc(c), g))
    return pl.pallas_call(
        body,
        out_shape=(jax.ShapeDtypeStruct((b, seq, d_inner), BF16),
                   jax.ShapeDtypeStruct((b, seq, d_inner), F32),
                   jax.ShapeDtypeStruct((b, seq, N_GROUPS * D_STATE), F32),
                   jax.ShapeDtypeStruct((b, seq, N_GROUPS * D_STATE), F32),
                   jax.ShapeDtypeStruct((b, seq, N_GROUPS * LANES), BF16),
                   jax.ShapeDtypeStruct((N_GROUPS, 8, gw), F32),
                   jax.ShapeDtypeStruct((N_GROUPS, 8, LANES), F32),
                   jax.ShapeDtypeStruct((N_GROUPS, 8, LANES), F32),
                   jax.ShapeDtypeStruct((N_GROUPS, 8, LANES), F32)),
        grid=(N_GROUPS, b, nc),
        in_specs=[blk(gw),
                  pl.BlockSpec((None, q, D_STATE), lambda g, i, c: (i, rc(c), xb0 + g)),
                  pl.BlockSpec((None, q, D_STATE), lambda g, i, c: (i, rc(c), xc0 + g)),
                  blk(gw),
                  pl.BlockSpec((None, q, LANES), lambda g, i, c: (i, rc(c), dt0 + g)),
                  blk(gw), blk(gw),
                  pl.BlockSpec((None, None, None, gw, D_STATE), lambda g, i, c: (i, rc(c), g, 0, 0)),
                  par(LANES), par(LANES), par(gw), par(gw)],
        out_specs=(blk(gw), blk(gw), blk(D_STATE), blk(D_STATE), blk(LANES),
                   par(gw), par(LANES), par(LANES), par(LANES)),
        scratch_shapes=[pltpu.VMEM((gw, D_STATE), F32), pltpu.VMEM((q, LANES), F32), pltpu.VMEM((LANES, q), F32)],
        compiler_params=_params(("parallel", "arbitrary", "arbitrary")), name=name,
    )(xc, xc, xc, zx, zx, y, dyn, st, dtb, alog, dskip, nw)


def _adamw(w, g, m, v, name):
    rows, cols = w.shape
    tr = rows
    for cand in (512, 256, 128, 64, 32, 16, 8):
        if rows % cand == 0 and cand * cols * 4 <= 2 * 1024 * 1024:
            tr = cand
            break
    c1 = 1.0 - ADAM_B1 ** ADAM_STEP
    c2 = 1.0 - ADAM_B2 ** ADAM_STEP

    def body(w_ref, g_ref, m_ref, v_ref, d_ref, mo_ref, vo_ref):
        gv = g_ref[...]
        mn = ADAM_B1 * m_ref[...] + (1.0 - ADAM_B1) * gv
        vn = ADAM_B2 * v_ref[...] + (1.0 - ADAM_B2) * (gv * gv)
        mo_ref[...] = mn
        vo_ref[...] = vn
        d_ref[...] = -ADAM_LR * ((mn / c1) / (jnp.sqrt(vn / c2) + ADAM_EPS) + ADAM_WD * w_ref[...])

    spec = pl.BlockSpec((tr, cols), lambda i: (i, 0))
    shp = jax.ShapeDtypeStruct((rows, cols), F32)
    return pl.pallas_call(body, out_shape=(shp, shp, shp), grid=(rows // tr,), in_specs=[spec] * 4,
                          out_specs=(spec,) * 3, compiler_params=_params(("parallel",)), name=name)(w, g, m, v)


def _pick_rows(rows, row_bytes, limit=1 << 20):
    for cand in (2048, 1024, 512, 256, 128, 64, 32, 16):
        if rows % cand == 0 and cand * row_bytes <= limit:
            return cand
    return rows


def _as3d(a, lead):
    return a.reshape(a.shape[:lead] + (-1, a.shape[-1]))


def _pair_sum(g, got, core, name):
    h = got.shape[0]
    g3, got3 = _as3d(g, 1), _as3d(got, 1)
    _, rows, cols = got3.shape
    tr = _pick_rows(rows, cols * 4)

    def body(c_ref, g_ref, r_ref, o_ref):
        o_ref[...] = (g_ref[...] + r_ref[...]).astype(BF16)

    out = pl.pallas_call(
        body, out_shape=jax.ShapeDtypeStruct(got3.shape, BF16),
        grid_spec=pltpu.PrefetchScalarGridSpec(
            num_scalar_prefetch=1, grid=(h, rows // tr),
            in_specs=[pl.BlockSpec((None, tr, cols), lambda l, i, c_ref: (c_ref[0] * h + l, i, 0)),
                      pl.BlockSpec((None, tr, cols), lambda l, i, c_ref: (l, i, 0))],
            out_specs=pl.BlockSpec((None, tr, cols), lambda l, i, c_ref: (l, i, 0))),
        compiler_params=_params(("parallel", "parallel")), name=name)(core, g3, got3)
    return out.reshape(got.shape)


def _sum4(q, core, name):
    q4 = _as3d(q, 2)
    _, h, rows, cols = q4.shape
    tr = _pick_rows(rows, cols * 4)

    def body(c_ref, q0, q1, q2, q3, o_ref):
        o_ref[...] = ((q0[...].astype(F32) + q1[...].astype(F32)) + q2[...].astype(F32)) + q3[...].astype(F32)

    out = pl.pallas_call(
        body, out_shape=jax.ShapeDtypeStruct((2 * h, rows, cols), F32),
        grid_spec=pltpu.PrefetchScalarGridSpec(
            num_scalar_prefetch=1, grid=(h, rows // tr),
            in_specs=[pl.BlockSpec((None, None, tr, cols), lambda l, i, c_ref, k=k: (k, l, i, 0))
                      for k in range(N_CHIPS)],
            out_specs=pl.BlockSpec((None, tr, cols), lambda l, i, c_ref: (c_ref[0] * h + l, i, 0))),
        compiler_params=_params(("parallel", "parallel")), name=name)(core, q4, q4, q4, q4)
    return out.reshape((2 * h,) + q.shape[2:])


def _coords():
    return lax.axis_index("x"), lax.axis_index("y"), lax.axis_index("c")


def _other_chips(x, y):
    return [(1 - x, y), (x, 1 - y), (1 - x, 1 - y)]


def _allgather_halves(src, name):
    rows, cols = src.shape

    def body(x_ref, o_ref, send, recv, local):
        x, y, c = _coords()
        sib = (x, y, 1 - c)
        chips = _other_chips(x, y)

        def slot(h, cx, cy):
            return o_ref.at[h, 2 * cx + cy]

        def copy(kk, dst, to, src_ref):
            return pltpu.make_async_remote_copy(src_ref=src_ref, dst_ref=dst, send_sem=send.at[kk],
                                                recv_sem=recv.at[kk], device_id=to, device_id_type=MESH)

        mine = pltpu.make_async_copy(x_ref, slot(c, x, y), local)
        mine.start()
        first = [copy(0, slot(c, x, y), sib, x_ref)]
        first += [copy(1 + j, slot(c, x, y), (*chip, c), x_ref) for j, chip in enumerate(chips)]
        for cp in first:
            cp.start()
        passed = [copy(4 + j, slot(c, *chip), sib, slot(c, *chip)) for j, chip in enumerate(chips)]
        for j, chip in enumerate(chips):
            copy(1 + j, slot(c, *chip), (x, y, c), x_ref).wait_recv()
            passed[j].start()
        copy(0, slot(1 - c, x, y), (x, y, c), x_ref).wait_recv()
        for j, chip in enumerate(chips):
            copy(4 + j, slot(1 - c, *chip), (x, y, c), x_ref).wait_recv()
        for cp in first + passed:
            cp.wait_send()
        mine.wait()

    return pl.pallas_call(
        body, out_shape=jax.ShapeDtypeStruct((2, N_CHIPS, rows, cols), src.dtype),
        in_specs=[ANY], out_specs=ANY,
        scratch_shapes=[pltpu.SemaphoreType.DMA((7,)), pltpu.SemaphoreType.DMA((7,)), pltpu.SemaphoreType.DMA],
        name=name)(src)


MIXW = (("ssd_w_in", None), ("ssd_w_out", 0), ("pool_w", 1))
FFNW = (("ffn_w_up", 1), ("ffn_w_down", 0))


def _chip_window(axis, ref, layers, k):
    if axis is None:
        return ref.at[layers, k]
    n = ref.shape[1 + axis] // N_CHIPS
    sl = pl.ds(pl.multiple_of(k * n, LANES if 1 + axis == len(ref.shape) - 1 else 8), n)
    idx = [layers] + [slice(None)] * (len(ref.shape) - 1)
    idx[1 + axis] = sl
    return ref.at[tuple(idx)]


def _full_shape(axis, shard_shape):
    if axis is None:
        return (shard_shape[0], N_CHIPS) + tuple(shard_shape[1:])
    full = list(shard_shape)
    full[1 + axis] *= N_CHIPS
    return tuple(full)


HBM_SPEC = pl.BlockSpec(memory_space=pltpu.HBM)
SEM_SPEC = pl.BlockSpec(memory_space=pltpu.SEMAPHORE)


def _dma_sems(count):
    return pltpu.SemaphoreType.DMA((max(count, 1),))


def _wait_for(copy, kind):
    if kind == "recv":
        copy.wait_recv()
    elif kind == "send":
        copy.wait_send()
    else:
        copy.wait()


def _comm_fused(stages, counts, srcs, lands, name, inplace=False):
    ns, nl, k = len(srcs), len(lands), len(stages)

    def body(*refs):
        src_refs = refs[:ns]
        land_refs = refs[ns + (nl if inplace else 0):ns + (nl if inplace else 0) + nl]
        sem_refs = refs[len(refs) - 3 * k:]
        for s, stage_fn in enumerate(stages):
            starts, waits = stage_fn(src_refs, land_refs, tuple(sem_refs[3 * s:3 * s + 3]))
            for cp in starts:
                cp.start()
            for cp, kind in waits:
                _wait_for(cp, kind)

    scratch = []
    for cnt in counts:
        scratch += [_dma_sems(c) for c in cnt]
    outs = pl.pallas_call(
        body, out_shape=tuple(jax.ShapeDtypeStruct(a.shape, a.dtype) for a in lands),
        in_specs=[ANY] * (ns + (nl if inplace else 0)), out_specs=(ANY,) * nl,
        input_output_aliases={ns + i: i for i in range(nl)} if inplace else {},
        scratch_shapes=scratch, name=name)(*srcs, *(lands if inplace else ()))
    return list(outs)


class _SplitComm:
    def __init__(self, stages, counts, srcs, lands, name):
        self.stages, self.counts, self.name = stages, counts, name
        self.ns = len(srcs)
        self.data = [pltpu.with_memory_space_constraint(a, pltpu.HBM) for a in list(srcs) + list(lands)]
        self.sems = None
        self.step = 0

    def advance(self, after=None):
        i, k, nd, ns = self.step, len(self.stages), len(self.data), self.ns
        first, last = i == 0, i == k
        stages = self.stages

        def body(*refs):
            data = refs[:nd]
            pos = nd
            if not first:
                old = tuple(refs[pos:pos + 3])
                pos += 4
            if not last:
                new = tuple(refs[pos:pos + 3])
            if not first:
                for cp, kind in stages[i - 1](data[:ns], data[ns:], old)[1]:
                    _wait_for(cp, kind)
            if not last:
                for cp in stages[i](data[:ns], data[ns:], new)[0]:
                    cp.start()
                refs[len(refs) - 1][...] = jnp.zeros((8, LANES), F32)

        args = list(self.data)
        in_specs = [HBM_SPEC] * nd
        if not first:
            args += list(self.sems) + [after]
            in_specs += [SEM_SPEC] * 3 + [ANY]
        out_shape, out_specs = [], []
        if not last:
            out_shape += [_dma_sems(c) for c in self.counts[i]]
            out_specs += [SEM_SPEC] * 3
        out_shape += [pltpu.HBM(a.shape, a.dtype) for a in self.data]
        out_specs += [HBM_SPEC] * nd
        if not last:
            out_shape.append(jax.ShapeDtypeStruct((8, LANES), F32))
            out_specs.append(pl.BlockSpec(memory_space=pltpu.VMEM))
        off = 0 if last else 3
        outs = pl.pallas_call(
            body, out_shape=tuple(out_shape), in_specs=in_specs, out_specs=tuple(out_specs),
            input_output_aliases={d: off + d for d in range(nd)},
            compiler_params=pltpu.CompilerParams(has_side_effects=pltpu.SideEffectType.DATAFLOW_SIDE_EFFECTING),
            name=f"{self.name}_{i}")(*args)
        self.sems = None if last else outs[:3]
        self.data = list(outs[off:off + nd])
        self.step += 1
        return None if last else outs[len(outs) - 1]

    def lands(self):
        return self.data[self.ns:]


def _gather_stages(spec):
    n = len(spec)

    def parts(srcs, lands):
        x, y, c = _coords()
        out = []
        for w, (_, axis) in enumerate(spec):
            h = srcs[w].shape[0] // 2
            mine, theirs = pl.ds(c * h, h), pl.ds((1 - c) * h, h)
            out.append((srcs[w].at[mine], lambda layers, k, w=w, axis=axis: _chip_window(axis, lands[w], layers, k),
                        mine, theirs))
        return x, y, c, 2 * x + y, (x, y, 1 - c), _other_chips(x, y), out

    def remote(src, dst, send, recv, idx, to):
        return pltpu.make_async_remote_copy(src_ref=src, dst_ref=dst, send_sem=send.at[idx], recv_sem=recv.at[idx],
                                            device_id=to, device_id_type=MESH)

    def stage0(srcs, lands, sems):
        send, recv, local = sems
        x, y, c, me, sib, chips, ps = parts(srcs, lands)
        starts, waits = [], []
        for w, (src, dst, mine, theirs) in enumerate(ps):
            lc = pltpu.make_async_copy(src, dst(mine, me), local.at[w])
            first = [remote(src, dst(mine, me), send, recv, 4 * w, sib)]
            first += [remote(src, dst(mine, me), send, recv, 4 * w + 1 + j, (cx, cy, c)) for j, (cx, cy) in enumerate(chips)]
            starts += [lc] + first
            waits.append((remote(src, dst(theirs, me), send, recv, 4 * w, (x, y, c)), "recv"))
            waits += [(remote(src, dst(mine, 2 * cx + cy), send, recv, 4 * w + 1 + j, (x, y, c)), "recv")
                      for j, (cx, cy) in enumerate(chips)]
            waits += [(cp, "send") for cp in first] + [(lc, "local")]
        return starts, waits

    def stage1(srcs, lands, sems):
        send, recv, _ = sems
        x, y, c, me, sib, chips, ps = parts(srcs, lands)
        starts, waits = [], []
        for w, (src, dst, mine, theirs) in enumerate(ps):
            for j, (cx, cy) in enumerate(chips):
                blk = dst(mine, 2 * cx + cy)
                fwd = remote(blk, blk, send, recv, 3 * w + j, sib)
                starts.append(fwd)
                waits.append((remote(src, dst(theirs, 2 * cx + cy), send, recv, 3 * w + j, (x, y, c)), "recv"))
                waits.append((fwd, "send"))
        return starts, waits

    return [stage0, stage1], [(4 * n, 4 * n, n), (3 * n, 3 * n, 0)]


def _swap_stages(spec):
    n = len(spec)

    def stage(srcs, lands, sems):
        send, recv, _ = sems
        x, y, c = _coords()
        starts, waits = [], []
        for w in range(n):
            h = srcs[w].shape[0] // 2
            cp = pltpu.make_async_remote_copy(src_ref=srcs[w].at[pl.ds((1 - c) * h, h)], dst_ref=lands[w],
                                              send_sem=send.at[w], recv_sem=recv.at[w],
                                              device_id=(x, y, 1 - c), device_id_type=MESH)
            starts.append(cp)
            waits += [(cp, "recv"), (cp, "send")]
        return starts, waits

    return [stage], [(n, n, 0)]


def _scatter_stages(spec):
    n = len(spec)

    def stage(srcs, lands, sems):
        send, recv, local = sems
        x, y, c = _coords()
        me = 2 * x + y
        starts, waits = [], []
        for w, (_, axis) in enumerate(spec):
            layers = pl.ds(0, srcs[w].shape[0])
            own = _chip_window(axis, srcs[w], layers, me)
            lc = pltpu.make_async_copy(own, lands[w].at[me], local.at[w])
            starts.append(lc)
            for j, (cx, cy) in enumerate(_other_chips(x, y)):
                cp = pltpu.make_async_remote_copy(src_ref=_chip_window(axis, srcs[w], layers, 2 * cx + cy),
                                                  dst_ref=lands[w].at[me], send_sem=send.at[3 * w + j],
                                                  recv_sem=recv.at[3 * w + j], device_id=(cx, cy, c), device_id_type=MESH)
                starts.append(cp)
                waits.append((pltpu.make_async_remote_copy(
                    src_ref=own, dst_ref=lands[w].at[2 * cx + cy], send_sem=send.at[3 * w + j], recv_sem=recv.at[3 * w + j],
                    device_id=(x, y, c), device_id_type=MESH), "recv"))
                waits.append((cp, "send"))
            waits.append((lc, "local"))
        return starts, waits

    return [stage], [(3 * n, 3 * n, n)]


def _share_stages(spec):
    n = len(spec)

    def stage(srcs, lands, sems):
        send, recv, _ = sems
        x, y, c = _coords()
        starts, waits = [], []
        for w in range(n):
            h = lands[w].shape[0] // 2
            mine, theirs = lands[w].at[pl.ds(c * h, h)], lands[w].at[pl.ds((1 - c) * h, h)]
            cp = pltpu.make_async_remote_copy(src_ref=mine, dst_ref=mine, send_sem=send.at[w], recv_sem=recv.at[w],
                                              device_id=(x, y, 1 - c), device_id_type=MESH)
            starts.append(cp)
            waits.append((pltpu.make_async_remote_copy(src_ref=theirs, dst_ref=theirs, send_sem=send.at[w],
                                                       recv_sem=recv.at[w], device_id=(x, y, c), device_id_type=MESH),
                          "recv"))
            waits.append((cp, "send"))
        return starts, waits

    return [stage], [(n, n, 0)]


def _shard_of(p, axis):
    if axis is None:
        return (p.shape[0],) + tuple(p.shape[2:])
    s = list(p.shape)
    s[1 + axis] //= N_CHIPS
    return tuple(s)


def _reduce_begin(spec, gs, core, tag):
    stages, counts = _swap_stages(spec)
    got = _comm_fused(stages, counts, gs,
                      [jax.ShapeDtypeStruct((g.shape[0] // 2,) + g.shape[1:], g.dtype) for g in gs], "swap_" + tag)
    pair = [_pair_sum(a, r, core, "pair_sum_" + n) for a, r, (n, _) in zip(gs, got, spec)]
    stages, counts = _scatter_stages(spec)
    lands = [lax.empty((N_CHIPS,) + _shard_of(p, axis), p.dtype) for p, (_, axis) in zip(pair, spec)]
    comm = _SplitComm(stages, counts, pair, lands, "scatter_" + tag)
    return comm, comm.advance()


def _reduce_finish(spec, comm, core, tag, after):
    comm.advance(after=after)
    halves = [_sum4(q, core, "sum4_" + n) for q, (n, _) in zip(comm.lands(), spec)]
    stages, counts = _share_stages(spec)
    return _comm_fused(stages, counts, [], halves, "share_" + tag, inplace=True)


def _allreduce_small(vec, name):
    rows, cols = vec.shape

    def body(x_ref, o_ref, buf, send, recv):
        x, y, c = _coords()
        me = 4 * x + 2 * y + c
        buf[me] = x_ref[...]
        cps = []
        for kk in range(1, 8):
            dx, dy, dc = (kk >> 2) & 1, (kk >> 1) & 1, kk & 1
            to = (1 - x if dx else x, 1 - y if dy else y, 1 - c if dc else c)
            cp = pltpu.make_async_remote_copy(src_ref=x_ref, dst_ref=buf.at[me], send_sem=send.at[kk - 1],
                                              recv_sem=recv.at[kk - 1], device_id=to, device_id_type=MESH)
            cp.start()
            cps.append((cp, 4 * to[0] + 2 * to[1] + to[2]))
        for kk, (cp, frm) in enumerate(cps):
            pltpu.make_async_remote_copy(src_ref=x_ref, dst_ref=buf.at[frm], send_sem=send.at[kk],
                                         recv_sem=recv.at[kk], device_id=(x, y, c), device_id_type=MESH).wait_recv()
        for cp, _ in cps:
            cp.wait_send()
        acc = buf[0]
        for kk in range(1, 8):
            acc = acc + buf[kk]
        o_ref[...] = acc

    vm = pl.BlockSpec(memory_space=pltpu.VMEM)
    return pl.pallas_call(
        body, out_shape=jax.ShapeDtypeStruct((rows, cols), F32), in_specs=[vm], out_specs=vm,
        scratch_shapes=[pltpu.VMEM((8, rows, cols), F32), pltpu.SemaphoreType.DMA((7,)), pltpu.SemaphoreType.DMA((7,))],
        compiler_params=_params(), name=name)(vec)


SMALL = (("ssd_conv_w", 2), ("pool_scale", 1), ("ffn_conv_w", 2))
REPL = ("ssd_conv_b", "ssd_dt_bias", "ssd_a_log", "ssd_d", "ssd_norm_w", "ffn_conv_b",
        "norm_mix_pre", "norm_mix_post", "norm_ffn_pre", "norm_ffn_post")
WEIGHTS = ("ssd_w_in", "ssd_conv_w", "ssd_conv_b", "ssd_dt_bias", "ssd_a_log", "ssd_d", "ssd_norm_w", "ssd_w_out",
           "pool_w", "pool_scale", "ffn_w_up", "ffn_conv_w", "ffn_conv_b", "ffn_w_down", "norm_mix_pre",
           "norm_mix_post", "norm_ffn_pre", "norm_ffn_post")


def _flat_rows(n):
    unit = 2 * 16 * FLAT_COLS
    return 2 * 16 * ((n + unit - 1) // unit)


def _flatten_shards(arrs, dtype):
    flat = jnp.concatenate([a.astype(dtype).reshape(-1) for a in arrs])
    rows = _flat_rows(flat.shape[0])
    flat = jnp.pad(flat, (0, rows * FLAT_COLS - flat.shape[0]))
    return flat.reshape(2, rows // 2, FLAT_COLS)


def _unflatten_full(gathered, shard_shapes, axes):
    per_chip = jnp.swapaxes(gathered, 0, 1).reshape(N_CHIPS, -1)
    out, off = [], 0
    for shp, ax in zip(shard_shapes, axes):
        n = math.prod(shp)
        pieces = [per_chip[k, off:off + n].reshape(shp) for k in range(N_CHIPS)]
        out.append(jnp.concatenate(pieces, axis=ax))
        off += n
    return out


def kernel(x, ssd_w_in, ssd_conv_w, ssd_conv_b, ssd_dt_bias, ssd_a_log, ssd_d, ssd_norm_w, ssd_w_out, pool_w, pool_scale, ffn_w_up, ffn_conv_w, ffn_conv_b, ffn_w_down, norm_mix_pre, norm_mix_post, norm_ffn_pre, norm_ffn_post, loss_target, m_ssd_w_in, m_ssd_conv_w, m_ssd_conv_b, m_ssd_dt_bias, m_ssd_a_log, m_ssd_d, m_ssd_norm_w, m_ssd_w_out, m_pool_w, m_pool_scale, m_ffn_w_up, m_ffn_conv_w, m_ffn_conv_b, m_ffn_w_down, m_norm_mix_pre, m_norm_mix_post, m_norm_ffn_pre, m_norm_ffn_post, v_ssd_w_in, v_ssd_conv_w, v_ssd_conv_b, v_ssd_dt_bias, v_ssd_a_log, v_ssd_d, v_ssd_norm_w, v_ssd_w_out, v_pool_w, v_pool_scale, v_ffn_w_up, v_ffn_conv_w, v_ffn_conv_b, v_ffn_w_down, v_norm_mix_pre, v_norm_mix_post, v_norm_ffn_pre, v_norm_ffn_post):
    wts = dict(ssd_w_in=ssd_w_in, ssd_conv_w=ssd_conv_w, ssd_conv_b=ssd_conv_b, ssd_dt_bias=ssd_dt_bias,
               ssd_a_log=ssd_a_log, ssd_d=ssd_d, ssd_norm_w=ssd_norm_w, ssd_w_out=ssd_w_out, pool_w=pool_w,
               pool_scale=pool_scale, ffn_w_up=ffn_w_up, ffn_conv_w=ffn_conv_w, ffn_conv_b=ffn_conv_b,
               ffn_w_down=ffn_w_down, norm_mix_pre=norm_mix_pre, norm_mix_post=norm_mix_post,
               norm_ffn_pre=norm_ffn_pre, norm_ffn_post=norm_ffn_post)
    mom = dict(ssd_w_in=m_ssd_w_in, ssd_conv_w=m_ssd_conv_w, ssd_conv_b=m_ssd_conv_b, ssd_dt_bias=m_ssd_dt_bias,
               ssd_a_log=m_ssd_a_log, ssd_d=m_ssd_d, ssd_norm_w=m_ssd_norm_w, ssd_w_out=m_ssd_w_out, pool_w=m_pool_w,
               pool_scale=m_pool_scale, ffn_w_up=m_ffn_w_up, ffn_conv_w=m_ffn_conv_w, ffn_conv_b=m_ffn_conv_b,
               ffn_w_down=m_ffn_w_down, norm_mix_pre=m_norm_mix_pre, norm_mix_post=m_norm_mix_post,
               norm_ffn_pre=m_norm_ffn_pre, norm_ffn_post=m_norm_ffn_post)
    var = dict(ssd_w_in=v_ssd_w_in, ssd_conv_w=v_ssd_conv_w, ssd_conv_b=v_ssd_conv_b, ssd_dt_bias=v_ssd_dt_bias,
               ssd_a_log=v_ssd_a_log, ssd_d=v_ssd_d, ssd_norm_w=v_ssd_norm_w, ssd_w_out=v_ssd_w_out, pool_w=v_pool_w,
               pool_scale=v_pool_scale, ffn_w_up=v_ffn_w_up, ffn_conv_w=v_ffn_conv_w, ffn_conv_b=v_ffn_conv_b,
               ffn_w_down=v_ffn_w_down, norm_mix_pre=v_norm_mix_pre, norm_mix_post=v_norm_mix_post,
               norm_ffn_pre=v_norm_ffn_pre, norm_ffn_post=v_norm_ffn_post)

    bl, seq, d = x.shape
    t = bl * seq
    depth = norm_mix_pre.shape[0]
    n_ssd = ssd_w_out.shape[0]
    d_inner = ssd_w_out.shape[1] * N_CHIPS
    nheads = d_inner // HEAD_DIM
    hpg = nheads // N_GROUPS
    gw = d_inner // N_GROUPS
    xbc = ssd_conv_w.shape[2] * N_CHIPS
    f2 = ffn_w_up.shape[2] * N_CHIPS
    ff = f2 // 2
    dg = d // 4
    cy = lax.axis_index("c")
    chip = 2 * lax.axis_index("x") + lax.axis_index("y")

    small_shapes = [wts[n].shape for n, _ in SMALL]
    small_axes = [a for _, a in SMALL]
    small_flat = _flatten_shards([wts[n] for n, _ in SMALL], F32)
    small_half = lax.dynamic_index_in_dim(small_flat, cy, 0, keepdims=False)
    small_all = _allgather_halves(small_half, "gather_small")
    conv_w, p_scale, f_conv_w = _unflatten_full(small_all, small_shapes, small_axes)
    def full_shapes(spec, shards):
        return [jax.ShapeDtypeStruct(_full_shape(axis, s.shape), s.dtype) for s, (_, axis) in zip(shards, spec)]

    stages, counts = _gather_stages(MIXW)
    mix_shards = [wts[n].astype(BF16) for n, _ in MIXW]
    w_in_cm, w_out, w_pool = _comm_fused(stages, counts, mix_shards, full_shapes(MIXW, mix_shards), "gather_mixers")
    w_in = jnp.swapaxes(w_in_cm, 1, 2).reshape(n_ssd, d, -1)
    stages, counts = _gather_stages(FFNW)
    ffn_shards = [wts[n].astype(BF16) for n, _ in FFNW]
    ffn_gather = _SplitComm(stages, counts, ffn_shards + [w_pool],
                            [lax.empty(s.shape, s.dtype) for s in full_shapes(FFNW, ffn_shards)], "gather_ffn")
    gather_token = ffn_gather.advance()

    def pad_heads(a):
        lead = a.shape[:-1]
        a = a.reshape(lead + (N_GROUPS, hpg))
        a = jnp.pad(a, [(0, 0)] * len(lead) + [(0, 0), (0, LANES - hpg)])
        return a.reshape(lead + (N_GROUPS * LANES,))

    def unpad_heads(a):
        lead = a.shape[:-1]
        return a.reshape(lead + (N_GROUPS, LANES))[..., :hpg].reshape(lead + (nheads,))

    def group_rows(a, width):
        return jnp.broadcast_to(a.reshape(N_GROUPS, 1, width), (N_GROUPS, 8, width))

    w_in_p = jnp.concatenate([w_in[..., :d_inner + xbc], pad_heads(w_in[..., d_inner + xbc:])], axis=-1)
    zw = w_in_p.shape[-1]

    x2 = x.reshape(t, d)
    tgt2 = loss_target.reshape(t, d)
    w_up = w_down = None

    saved = []
    cur = x2
    tokens = []
    for i in range(depth):
        j = i // 2
        sv = dict(x_in=cur)
        if i % 2 == 0:
            h = _norm_fwd(cur, norm_mix_pre[i:i + 1], BF16, "norm_pre_b", after=[gather_token] if i == 0 else ())
            zx = _mm(h, w_in_p, "nn", F32, "mm_ssd_in", 2048, 512, d, b_layer=j).reshape(bl, seq, zw)
            xc, xpre = _ssd_conv_fwd(zx, conv_w[j], ssd_conv_b[j:j + 1], d_inner, "ssd_conv_fwd")
            dtb = group_rows(pad_heads(ssd_dt_bias[j]), LANES)
            alog = group_rows(pad_heads(ssd_a_log[j]), LANES)
            dskip = group_rows(jnp.repeat(ssd_d[j], HEAD_DIM), gw)
            nw = group_rows(ssd_norm_w[j], gw)
            y, yn, st = _ssd_fwd(xc, zx, dtb, alog, dskip, nw, d_inner, "ssd_fwd")
            if i == 0:
                tokens.append(ffn_gather.advance(after=yn))
            mix = _mm(yn.reshape(t, d_inner), w_out, "nn", F32, "mm_ssd_out", 512, 512, d_inner, b_layer=j)
            sv.update(h=h, zx=zx, xc=xc, xpre=xpre, y=y, yn=yn, st=st, dtb=dtb, alog=alog, dskip=dskip, nw=nw)
        else:
            h = _norm_fwd(cur, norm_mix_pre[i:i + 1], F32, "norm_pre_f")
            mix = _pool_fwd(h.reshape(bl, seq, d), w_pool[j], p_scale[j:j + 1], "pool_fwd").reshape(t, d)
            sv.update(h=h)
        sv.update(mix=mix)
        mid = _norm_fwd(mix, norm_mix_post[i:i + 1], F32, "norm_post", resid=cur, after=tokens)
        tokens = []
        u = _norm_fwd(mid, norm_ffn_pre[i:i + 1], BF16, "norm_pre_b")
        if i == 0:
            ffn_gather.advance(after=u)
            w_up, w_down = ffn_gather.lands()
        hpre = _mm(u, w_up, "nn", BF16, "mm_up", 2048, 512, d, b_layer=i).reshape(bl, seq, f2)
        act, pre_g, pre_v = _ffn_act_fwd(hpre, f_conv_w[i], ffn_conv_b[i:i + 1], "ffn_act_fwd")
        act = act.reshape(t, ff)
        fo = _mm(act, w_down, "nn", F32, "mm_down", 1024, 512, ff, b_layer=i)
        cur = _norm_fwd(fo, norm_ffn_post[i:i + 1], F32, "norm_post", resid=mid)
        sv.update(mid=mid, u=u, hpre=hpre, pre_g=pre_g, pre_v=pre_v, act=act, fo=fo)
        saved.append(sv)

    dcur, loss_part = _loss_head(cur, tgt2, "loss_head")

    g = {n: [None] * wts[n].shape[0] for n in WEIGHTS}
    gbuf = dict(up=lax.empty((depth, d, f2), F32), down=lax.empty((depth, ff, d), F32),
                out=lax.empty((n_ssd, d_inner, d), F32), win=lax.empty((n_ssd, d, zw), F32))
    core = cy.reshape(1).astype(jnp.int32)

    def mixer_bwd(i, dmid, behind=()):
        j = i // 2
        sv = saved[i]
        done = []
        if i % 2 == 0:
            dmix, g["norm_mix_post"][i] = _norm_bwd(sv["mix"], norm_mix_post[i:i + 1], dmid, BF16, "norm_bwd_b",
                                                    after=behind)
            dyn = _mm(dmix, w_out, "nt", F32, "mm_ssd_out_dx", 1024, 1024, d, b_layer=j)
            gbuf["out"], tok = _mm(sv["yn"].reshape(t, d_inner), dmix, "tn", F32, "mm_ssd_out_dw", 1024, 512, 2048,
                                   out_buf=(gbuf["out"], j))
            done.append(tok)
            dz, dxs, dbm, dcm, ddt, dnw, dd, dal, dbias = _ssd_bwd(
                sv["xc"], sv["zx"], sv["y"], dyn.reshape(bl, seq, d_inner), sv["st"], sv["dtb"], sv["alog"],
                sv["dskip"], sv["nw"], d_inner, "ssd_bwd")
            g["ssd_norm_w"][j] = dnw[:, 0, :].reshape(d_inner)
            g["ssd_d"][j] = dd[:, 0, :hpg].reshape(nheads)
            g["ssd_a_log"][j] = dal[:, 0, :hpg].reshape(nheads)
            g["ssd_dt_bias"][j] = dbias[:, 0, :hpg].reshape(nheads)
            dxbc, dcw, dcb = _ssd_conv_bwd(sv["zx"], sv["xpre"], (dxs, dbm, dcm), conv_w[j], d_inner, "ssd_conv_bwd")
            g["ssd_conv_w"][j] = dcw
            g["ssd_conv_b"][j] = dcb[0]
            dzs = [dz.reshape(t, d_inner), dxbc.reshape(t, xbc), ddt.reshape(t, N_GROUPS * LANES)]
            dh = _mm(dzs, w_in_p, "nt", F32, "mm_ssd_in_dx", 1024, d, 512, b_layer=j)
            gbuf["win"], tok = _mm(sv["h"], dzs, "tn", F32, "mm_ssd_in_dw", 1024, 512, 2048, out_buf=(gbuf["win"], j))
            done.append(tok)
        else:
            dmix, g["norm_mix_post"][i] = _norm_bwd(sv["mix"], norm_mix_post[i:i + 1], dmid, F32, "norm_bwd_f",
                                                    after=behind)
            dh3, g["pool_w"][j], dps = _pool_bwd(sv["h"].reshape(bl, seq, d), dmix.reshape(bl, seq, d), w_pool[j],
                                                 p_scale[j:j + 1], "pool_bwd")
            g["pool_scale"][j] = dps[0]
            dh = dh3.reshape(t, d)
        dx_in, g["norm_mix_pre"][i] = _norm_bwd(sv["x_in"], norm_mix_pre[i:i + 1], dh, F32, "norm_bwd_r", resid=dmid,
                                                after=done)
        return dx_in

    ffn_comm = None
    for i in reversed(range(depth)):
        sv = saved[i]
        dfo, g["norm_ffn_post"][i] = _norm_bwd(sv["fo"], norm_ffn_post[i:i + 1], dcur, BF16, "norm_bwd_b")
        dact = _mm(dfo, w_down, "nt", BF16, "mm_down_dx", 1024, ff // 2, d, b_layer=i)
        gbuf["down"], tok_down = _mm(sv["act"], dfo, "tn", F32, "mm_down_dw", ff // 2, 512, 2048,
                                     out_buf=(gbuf["down"], i))
        dhg, dhv, dcw, dcb = _ffn_act_bwd(sv["hpre"], sv["pre_g"], sv["pre_v"], dact.reshape(bl, seq, ff), f_conv_w[i],
                                          "ffn_act_bwd")
        g["ffn_conv_w"][i] = dcw
        g["ffn_conv_b"][i] = dcb[0]
        dhs = [dhg.reshape(t, ff), dhv.reshape(t, ff)]
        du = _mm(dhs, w_up, "nt", F32, "mm_up_dx", 1024, d, ff // 2, b_layer=i)
        gbuf["up"], tok_up = _mm(sv["u"], dhs, "tn", F32, "mm_up_dw", 512, ff // 2, 2048, out_buf=(gbuf["up"], i))
        dmid, g["norm_ffn_pre"][i] = _norm_bwd(sv["mid"], norm_ffn_pre[i:i + 1], du, F32, "norm_bwd_r", resid=dcur,
                                               after=[tok_down, tok_up])
        if i > 0:
            dcur = mixer_bwd(i, dmid)
        else:
            ffn_comm, ffn_token = _reduce_begin(FFNW, [gbuf["up"], gbuf["down"]], core, "ffn")
            dcur = mixer_bwd(0, dmid, behind=[ffn_token])

    grad_x = dcur.reshape(bl, seq, d)
    for n in ("norm_mix_pre", "norm_mix_post", "norm_ffn_pre", "norm_ffn_post"):
        g[n] = [a[0] for a in g[n]]
    small_names = [n for n, _ in SMALL] + list(REPL)
    full = {n: jnp.stack(g[n], axis=0) for n in small_names}

    g_in = jnp.concatenate([gbuf["win"][..., :d_inner + xbc], unpad_heads(gbuf["win"][..., d_inner + xbc:])], axis=-1)
    g_in_cm = jnp.swapaxes(g_in.reshape(n_ssd, d, N_CHIPS, -1), 1, 2)
    mix_comm, mix_token = _reduce_begin(MIXW, [g_in_cm, gbuf["out"], jnp.stack(g["pool_w"], axis=0)], core, "mixers")

    grads, deltas, new_m, new_v = {}, {}, {}, {}

    def adamw(n, gr):
        shp = wts[n].shape
        two = (math.prod(shp[:-1]), shp[-1])
        dl, mn, vn = _adamw(wts[n].reshape(two), gr.reshape(two), mom[n].reshape(two), var[n].reshape(two),
                            "adamw_" + n)
        grads[n], deltas[n], new_m[n], new_v[n] = gr, dl.reshape(shp), mn.reshape(shp), vn.reshape(shp)
        return dl

    ffn_grads = _reduce_finish(FFNW, ffn_comm, core, "ffn", after=mix_token)
    for gr, (n, _) in zip(ffn_grads, FFNW):
        last = adamw(n, gr)

    vec = jnp.concatenate([full[n].reshape(-1) for n in small_names] + [loss_part[0, :1]])
    nvec = vec.shape[0]
    vrows = 8 * ((nvec + 8 * FLAT_COLS - 1) // (8 * FLAT_COLS))
    vec = jnp.pad(vec, (0, vrows * FLAT_COLS - nvec)).reshape(vrows, FLAT_COLS)
    tot = _allreduce_small(vec, "allreduce_small").reshape(-1)
    small_grads, off = {}, 0
    for n in small_names:
        cnt = math.prod(full[n].shape)
        small_grads[n] = tot[off:off + cnt].reshape(full[n].shape)
        off += cnt
    loss = tot[off]
    for n, ax in SMALL:
        w = wts[n].shape[ax]
        small_grads[n] = lax.dynamic_slice_in_dim(small_grads[n], chip * w, w, axis=ax)

    for n in small_names:
        last = adamw(n, small_grads[n])
    mix_grads = _reduce_finish(MIXW, mix_comm, core, "mixers", after=last)
    for gr, (n, _) in zip(mix_grads, MIXW):
        adamw(n, gr)

    return (loss, grad_x, *[grads[n] for n in WEIGHTS], *[deltas[n] for n in WEIGHTS],
            *[new_m[n] for n in WEIGHTS], *[new_v[n] for n in WEIGHTS])
```

```python
import functools
import math

import jax
import jax.numpy as jnp
from jax import lax
from jax.experimental import pallas as pl
from jax.experimental.pallas import tpu as pltpu

F32 = jnp.float32
BF16 = jnp.bfloat16
MESH = pl.DeviceIdType.MESH
ANY = pl.BlockSpec(memory_space=pl.ANY)

HEAD_DIM = 64
D_STATE = 128
CHUNK = 128
N_GROUPS = 4
SSD_CONV = 4
FFN_CONV = 3
EPS = 1e-6
N_CHIPS = 4
LANES = 128
FLAT_COLS = 1024

ADAM_LR = 0.001
ADAM_B1 = 0.9
ADAM_B2 = 0.999
ADAM_EPS = 1e-08
ADAM_WD = 0.01
ADAM_STEP = 10

VMEM_LIMIT_BYTES = 56 * 1024 * 1024


def _params(sem=None):
    kw = dict(vmem_limit_bytes=VMEM_LIMIT_BYTES)
    if sem is not None:
        kw["dimension_semantics"] = sem
    return pltpu.CompilerParams(**kw)


def _sigmoid(x):
    return 1.0 / (1.0 + jnp.exp(-x))


def _softplus(x):
    return jnp.maximum(x, 0.0) + jnp.log(1.0 + jnp.exp(-jnp.abs(x)))


def _dot(a, b, dn):
    return lax.dot_general(a, b, (dn, ((), ())), preferred_element_type=F32)


def _nn(a, b):
    return _dot(a, b, ((1,), (0,)))


def _nt(a, b):
    return _dot(a, b, ((1,), (1,)))


def _tn(a, b):
    return _dot(a, b, ((0,), (0,)))


def _split(x, parts):
    out = []
    r = x
    for _ in range(parts):
        p = r.astype(BF16)
        out.append(p)
        r = r - p.astype(F32)
    return out


def _sel_left(sel, x, parts=3):
    n = x.shape[1]
    r = _nn(sel, jnp.concatenate(_split(x, parts), axis=1))
    out = r[:, 0:n]
    for i in range(1, parts):
        out = out + r[:, i * n:(i + 1) * n]
    return out


def _sel_right(x, sel_stacked, parts=3):
    return _nn(jnp.concatenate(_split(x, parts), axis=1), sel_stacked)


def _mm(a, b, dims, out_dtype, name, tm, tn, tk, b_layer=None, out_buf=None):
    a_list = list(a) if isinstance(a, (list, tuple)) else [a]
    b_list = list(b) if isinstance(b, (list, tuple)) else [b]
    if dims in ("nn", "nt"):
        assert len(b_list) == 1
        m = a_list[0].shape[0]
        segs = [x.shape[1] for x in a_list]
        k = sum(segs)
        bshape = b_list[0].shape[-2:]
        n = bshape[1] if dims == "nn" else bshape[0]
        assert (bshape[0] if dims == "nn" else bshape[1]) == k
    else:
        assert len(a_list) == 1 and b_layer is None
        k, m = a_list[0].shape
        segs = [x.shape[1] for x in b_list]
        n = sum(segs)
    tm, tn, tk = min(tm, m), min(tn, n), min(tk, k)
    if dims == "tn":
        tn = min(tn, min(segs))
    else:
        tk = min(tk, min(segs))
    unit = tk if dims != "tn" else tn
    assert m % tm == 0 and n % tn == 0 and k % tk == 0 and all(s % unit == 0 for s in segs), (name, m, n, k, segs)
    nk = k // tk
    starts = [sum(segs[:s]) // unit for s in range(len(segs))]
    counts = [s // unit for s in segs]
    nseg = len(segs)
    dn = {"nn": ((1,), (0,)), "nt": ((1,), (1,)), "tn": ((0,), (0,))}[dims]

    def body(*refs):
        a_refs = refs[:len(a_list)]
        b_refs = refs[len(a_list):len(a_list) + len(b_list)]
        rest = refs[len(a_list) + len(b_list) + (0 if out_buf is None else 1):]
        o_ref = rest[0]
        if out_buf is not None:
            rest[1][...] = jnp.zeros((8, LANES), F32)
            rest = rest[1:]
        acc = rest[1] if nk > 1 else None
        kk = pl.program_id(2)
        sel = kk if dims != "tn" else pl.program_id(1)

        def step(a_ref, b_ref):
            p = _dot(a_ref[...].astype(BF16), b_ref[...].astype(BF16), dn)
            if nk == 1:
                o_ref[...] = p.astype(out_dtype)
                return

            @pl.when(kk == 0)
            def _():
                acc[...] = p

            @pl.when(kk > 0)
            def _():
                acc[...] += p

        if nseg == 1:
            step(a_refs[0], b_refs[0])
        else:
            for s in range(nseg):
                @pl.when(jnp.logical_and(sel >= starts[s], sel < starts[s] + counts[s]))
                def _(s=s):
                    step(a_refs[s] if dims != "tn" else a_refs[0], b_refs[0] if dims != "tn" else b_refs[s])

        if nk > 1:
            @pl.when(kk == nk - 1)
            def _():
                o_ref[...] = acc[...].astype(out_dtype)

    def seg_index(v, s):
        return v if nseg == 1 else jnp.clip(v - starts[s], 0, counts[s] - 1)

    lead = () if b_layer is None else (b_layer,)
    none = () if b_layer is None else (None,)
    if dims == "nn":
        a_specs = [pl.BlockSpec((tm, tk), lambda i, j, kk, s=s: (i, seg_index(kk, s))) for s in range(nseg)]
        b_specs = [pl.BlockSpec(none + (tk, tn), lambda i, j, kk: lead + (kk, j))]
    elif dims == "nt":
        a_specs = [pl.BlockSpec((tm, tk), lambda i, j, kk, s=s: (i, seg_index(kk, s))) for s in range(nseg)]
        b_specs = [pl.BlockSpec(none + (tn, tk), lambda i, j, kk: lead + (j, kk))]
    else:
        a_specs = [pl.BlockSpec((tk, tm), lambda i, j, kk: (kk, i))]
        b_specs = [pl.BlockSpec((tk, tn), lambda i, j, kk, s=s: (kk, seg_index(j, s))) for s in range(nseg)]
    args = a_list + b_list
    in_specs = a_specs + b_specs
    aliases = {}
    if out_buf is None:
        out_shape = jax.ShapeDtypeStruct((m, n), out_dtype)
        out_spec = pl.BlockSpec((tm, tn), lambda i, j, kk: (i, j))
    else:
        buf, slab = out_buf
        assert buf.shape[1:] == (m, n) and buf.dtype == out_dtype
        out_shape = (jax.ShapeDtypeStruct(buf.shape, out_dtype), jax.ShapeDtypeStruct((8, LANES), F32))
        out_spec = (pl.BlockSpec((None, tm, tn), lambda i, j, kk: (slab, i, j)),
                    pl.BlockSpec((8, LANES), lambda i, j, kk: (0, 0)))
        aliases = {len(args): 0}
        args = args + [buf]
        in_specs = in_specs + [ANY]
    return pl.pallas_call(
        body,
        out_shape=out_shape,
        grid=(m // tm, n // tn, nk),
        in_specs=in_specs,
        out_specs=out_spec,
        scratch_shapes=[] if nk == 1 else [pltpu.VMEM((tm, tn), F32)],
        input_output_aliases=aliases,
        compiler_params=_params(("parallel", "parallel", "arbitrary") if out_buf is None else ("arbitrary",) * 3),
        name=name,
    )(*args)


def _row_tile(t, want):
    tm = min(want, t)
    assert t % tm == 0
    return tm


def _norm_fwd(x, w, out_dtype, name, resid=None, after=()):
    t, d = x.shape
    tm = _row_tile(t, 512)
    after = [a for a in after if a is not None]

    def body(*refs):
        refs = refs[:len(refs) - 1 - len(after)] + refs[len(refs) - 1:]
        if resid is None:
            x_ref, w_ref, o_ref = refs
        else:
            x_ref, w_ref, r_ref, o_ref = refs
        xv = x_ref[...]
        r = lax.rsqrt(jnp.mean(xv * xv, axis=-1, keepdims=True) + EPS)
        y = (xv * r) * w_ref[...]
        if resid is not None:
            y = r_ref[...] + y
        o_ref[...] = y.astype(out_dtype)

    row = pl.BlockSpec((tm, d), lambda i: (i, 0))
    vec = pl.BlockSpec((1, d), lambda i: (0, 0))
    args = [x, w] + ([] if resid is None else [resid]) + after
    return pl.pallas_call(
        body, out_shape=jax.ShapeDtypeStruct((t, d), out_dtype), grid=(t // tm,),
        in_specs=[row, vec] + ([] if resid is None else [row]) + [ANY] * len(after), out_specs=row,
        compiler_params=_params(("parallel",)), name=name)(*args)


def _norm_post_pre(m, w_post, resid, w_pre, pre_dtype, name, after=()):
    t, d = m.shape
    tm = _row_tile(t, 512)
    after = [a for a in after if a is not None]

    def body(m_ref, w1_ref, r_ref, w2_ref, *rest):
        x_ref, u_ref = rest[len(after):]
        mv = m_ref[...]
        r1 = lax.rsqrt(jnp.mean(mv * mv, axis=-1, keepdims=True) + EPS)
        xv = r_ref[...] + (mv * r1) * w1_ref[...]
        x_ref[...] = xv
        r2 = lax.rsqrt(jnp.mean(xv * xv, axis=-1, keepdims=True) + EPS)
        u_ref[...] = ((xv * r2) * w2_ref[...]).astype(pre_dtype)

    row = pl.BlockSpec((tm, d), lambda i: (i, 0))
    vec = pl.BlockSpec((1, d), lambda i: (0, 0))
    return pl.pallas_call(
        body, out_shape=(jax.ShapeDtypeStruct((t, d), F32), jax.ShapeDtypeStruct((t, d), pre_dtype)), grid=(t // tm,),
        in_specs=[row, vec, row, vec] + [ANY] * len(after), out_specs=(row, row),
        compiler_params=_params(("parallel",)), name=name)(m, w_post, resid, w_pre, *after)


def _norm_bwd(src, w, dy, out_dtype, name, resid=None, after=()):
    t, d = src.shape
    tm = _row_tile(t, 512)
    after = [a for a in after if a is not None]

    def body(*refs):
        refs = refs[:len(refs) - 2 - len(after)] + refs[len(refs) - 2:]
        if resid is None:
            x_ref, w_ref, g_ref, o_ref, dw_ref = refs
        else:
            x_ref, w_ref, g_ref, r_ref, o_ref, dw_ref = refs
        xv = x_ref[...]
        g = g_ref[...].astype(F32)
        r = lax.rsqrt(jnp.mean(xv * xv, axis=-1, keepdims=True) + EPS)
        xh = xv * r
        gh = g * w_ref[...]
        mean = jnp.mean(gh * xh, axis=-1, keepdims=True)
        dx = r * (gh - xh * mean)
        if resid is not None:
            dx = r_ref[...] + dx
        o_ref[...] = dx.astype(out_dtype)
        part = jnp.sum(g * xh, axis=0, keepdims=True)

        @pl.when(pl.program_id(0) == 0)
        def _():
            dw_ref[...] = part

        @pl.when(pl.program_id(0) > 0)
        def _():
            dw_ref[...] += part

    row = pl.BlockSpec((tm, d), lambda i: (i, 0))
    vec = pl.BlockSpec((1, d), lambda i: (0, 0))
    args = [src, w, dy] + ([] if resid is None else [resid]) + after
    return pl.pallas_call(
        body,
        out_shape=(jax.ShapeDtypeStruct((t, d), out_dtype), jax.ShapeDtypeStruct((1, d), F32)),
        grid=(t // tm,),
        in_specs=[row, vec, row] + ([] if resid is None else [row]) + [ANY] * len(after),
        out_specs=(row, vec),
        compiler_params=_params(("arbitrary",)), name=name)(*args)


def _loss_head(y, target, name):
    t, d = y.shape
    tm = _row_tile(t, 512)

    def body(y_ref, t_ref, dy_ref, l_ref):
        e = y_ref[...] - t_ref[...]
        dy_ref[...] = e * (1.0 / d)
        col = jnp.sum(e * e, axis=0, keepdims=True)
        s = jnp.sum(col, axis=1, keepdims=True) * (0.5 / d)
        part = jnp.broadcast_to(s, (1, LANES))

        @pl.when(pl.program_id(0) == 0)
        def _():
            l_ref[...] = part

        @pl.when(pl.program_id(0) > 0)
        def _():
            l_ref[...] += part

    row = pl.BlockSpec((tm, d), lambda i: (i, 0))
    return pl.pallas_call(
        body,
        out_shape=(jax.ShapeDtypeStruct((t, d), F32), jax.ShapeDtypeStruct((1, LANES), F32)),
        grid=(t // tm,), in_specs=[row, row],
        out_specs=(row, pl.BlockSpec((1, LANES), lambda i: (0, 0))),
        compiler_params=_params(("arbitrary",)), name=name)(y, target)


def _window(ref, c, rows, seq, before, after):
    r0 = pl.multiple_of(c * rows, rows)
    parts = []
    if before:
        h0 = pl.multiple_of(jnp.maximum(r0 - before, 0), before)
        halo = ref[pl.ds(h0, before), :].astype(F32)
        parts.append(jnp.where(c > 0, halo, 0.0))
    parts.append(ref[pl.ds(r0, rows), :].astype(F32))
    if after:
        h1 = pl.multiple_of(jnp.minimum(r0 + rows, seq - after), after)
        halo = ref[pl.ds(h1, after), :].astype(F32)
        parts.append(jnp.where(c < seq // rows - 1, halo, 0.0))
    return parts[0] if len(parts) == 1 else jnp.concatenate(parts, axis=0)


def _lag(x, k):
    return pltpu.roll(x, k, 0) if k else x


def _lead(x, k):
    return pltpu.roll(x, x.shape[0] - k, 0) if k else x


SHIFT_ROWS = 128
SHIFT_COLS = 256


HALO = 16


def _conv3(ext, w, bias):
    acc = bias + w[2:3, :] * ext[HALO:, :]
    acc = acc + w[1:2, :] * _lag(ext, 1)[HALO:, :]
    return acc + w[0:1, :] * _lag(ext, 2)[HALO:, :]


def _ffn_act_fwd(hpre, cw, cb, name):
    b, seq, f2 = hpre.shape
    cbk = SHIFT_COLS
    nj = f2 // (2 * cbk)
    rows = min(SHIFT_ROWS, seq)

    def body(g_ref, v_ref, wg_ref, wv_ref, bg_ref, bv_ref, o_ref, pg_ref, pv_ref):
        def chunk(c, carry):
            gate = _conv3(_window(g_ref, c, rows, seq, HALO, 0), wg_ref[...], bg_ref[...])
            val = _conv3(_window(v_ref, c, rows, seq, HALO, 0), wv_ref[...], bv_ref[...])
            a = gate * _sigmoid(gate) * val
            here = pl.ds(pl.multiple_of(c * rows, rows), rows)
            o_ref[here, :] = a.astype(BF16)
            pg_ref[here, :] = gate.astype(BF16)
            pv_ref[here, :] = val.astype(BF16)
            return carry

        lax.fori_loop(0, seq // rows, chunk, 0)

    blk = lambda off: pl.BlockSpec((None, seq, cbk), lambda i, j: (i, 0, j + off))
    wsp = lambda r, off: pl.BlockSpec((r, cbk), lambda i, j: (0, j + off))
    half = jax.ShapeDtypeStruct((b, seq, f2 // 2), BF16)
    return pl.pallas_call(
        body, out_shape=(half, half, half), grid=(b, nj),
        in_specs=[blk(0), blk(nj), wsp(FFN_CONV, 0), wsp(FFN_CONV, nj), wsp(1, 0), wsp(1, nj)],
        out_specs=(blk(0), blk(0), blk(0)),
        compiler_params=_params(("parallel", "parallel")), name=name)(hpre, hpre, cw, cw, cb, cb)


def _ffn_act_bwd(hpre, pre_g, pre_v, da, cw, name):
    b, seq, f2 = hpre.shape
    cbk = SHIFT_COLS
    nj = f2 // (2 * cbk)
    rows = min(SHIFT_ROWS, seq)

    def body(g_ref, v_ref, pg_ref, pv_ref, da_ref, wg_ref, wv_ref, og_ref, ov_ref, dwg_ref, dwv_ref, dbg_ref, dbv_ref):
        wg, wv = wg_ref[...], wv_ref[...]

        def back(dpre, w, o_ref, x_ref, c, carry):
            here = pl.ds(pl.multiple_of(c * rows, rows), rows)
            leads = [dpre, _lead(dpre, 1), _lead(dpre, 2)]
            dx = w[2:3, :] * leads[0] + w[1:2, :] * leads[1] + w[0:1, :] * leads[2]
            o_ref[here, :] = dx[:rows, :].astype(BF16)
            x0 = x_ref[here, :].astype(F32)
            return tuple(carry[k] + jnp.sum(leads[k][:rows, :] * x0, axis=0, keepdims=True) for k in range(FFN_CONV)) + (
                carry[FFN_CONV] + jnp.sum(dpre[:rows, :], axis=0, keepdims=True),)

        def chunk(c, carry):
            cg, cv = carry
            gate = _window(pg_ref, c, rows, seq, 0, HALO)
            val = _window(pv_ref, c, rows, seq, 0, HALO)
            dav = _window(da_ref, c, rows, seq, 0, HALO)
            sg = _sigmoid(gate)
            cg = back(dav * val * (sg * (1.0 + gate * (1.0 - sg))), wg, og_ref, g_ref, c, cg)
            cv = back(dav * (gate * sg), wv, ov_ref, v_ref, c, cv)
            return cg, cv

        z = jnp.zeros((1, cbk), F32)
        cg, cv = lax.fori_loop(0, seq // rows, chunk, ((z,) * (FFN_CONV + 1), (z,) * (FFN_CONV + 1)))
        dwg = jnp.concatenate([cg[2], cg[1], cg[0]], axis=0)
        dwv = jnp.concatenate([cv[2], cv[1], cv[0]], axis=0)

        @pl.when(pl.program_id(1) == 0)
        def _():
            dwg_ref[...] = dwg
            dwv_ref[...] = dwv
            dbg_ref[...] = cg[FFN_CONV]
            dbv_ref[...] = cv[FFN_CONV]

        @pl.when(pl.program_id(1) > 0)
        def _():
            dwg_ref[...] += dwg
            dwv_ref[...] += dwv
            dbg_ref[...] += cg[FFN_CONV]
            dbv_ref[...] += cv[FFN_CONV]

    blk = lambda off: pl.BlockSpec((None, seq, cbk), lambda j, i: (i, 0, j + off))
    wsp = lambda r, off: pl.BlockSpec((r, cbk), lambda j, i: (0, j + off))
    half = jax.ShapeDtypeStruct((b, seq, f2 // 2), BF16)
    dwshape = jax.ShapeDtypeStruct((FFN_CONV, f2 // 2), F32)
    dbshape = jax.ShapeDtypeStruct((1, f2 // 2), F32)
    dg, dv, dwg, dwv, dbg, dbv = pl.pallas_call(
        body,
        out_shape=(half, half, dwshape, dwshape, dbshape, dbshape),
        grid=(nj, b),
        in_specs=[blk(0), blk(nj), blk(0), blk(0), blk(0), wsp(FFN_CONV, 0), wsp(FFN_CONV, nj)],
        out_specs=(blk(0), blk(0), wsp(FFN_CONV, 0), wsp(FFN_CONV, 0), wsp(1, 0), wsp(1, 0)),
        compiler_params=_params(("parallel", "arbitrary")), name=name)(hpre, hpre, pre_g, pre_v, da, cw, cw)
    return dg, dv, jnp.concatenate([dwg, dwv], axis=1), jnp.concatenate([dbg, dbv], axis=1)


def _ssd_conv_fwd(zx, cw, cb, d_inner, name):
    b, seq, _ = zx.shape
    xbc = cw.shape[1]
    cbk = SHIFT_COLS
    off = d_inner // cbk
    rows = min(SHIFT_ROWS, seq)

    def body(h_ref, w_ref, b_ref, o_ref, p_ref):
        w = w_ref[...]
        bias = b_ref[...]

        def chunk(c, carry):
            ext = _window(h_ref, c, rows, seq, 8, 0)
            acc = bias + w[3:4, :] * ext[8:, :]
            for k in range(1, SSD_CONV):
                acc = acc + w[3 - k:4 - k, :] * _lag(ext, k)[8:, :]
            here = pl.ds(pl.multiple_of(c * rows, rows), rows)
            o_ref[here, :] = acc * _sigmoid(acc)
            p_ref[here, :] = acc.astype(BF16)
            return carry

        lax.fori_loop(0, seq // rows, chunk, 0)

    blk = pl.BlockSpec((None, seq, cbk), lambda i, j: (i, 0, j))
    return pl.pallas_call(
        body, out_shape=(jax.ShapeDtypeStruct((b, seq, xbc), F32), jax.ShapeDtypeStruct((b, seq, xbc), BF16)),
        grid=(b, xbc // cbk),
        in_specs=[pl.BlockSpec((None, seq, cbk), lambda i, j: (i, 0, j + off)),
                  pl.BlockSpec((SSD_CONV, cbk), lambda i, j: (0, j)),
                  pl.BlockSpec((1, cbk), lambda i, j: (0, j))],
        out_specs=(blk, blk),
        compiler_params=_params(("parallel", "parallel")), name=name)(zx, cw, cb)


def _ssd_conv_bwd(zx, pre, dparts, cw, d_inner, name):
    b, seq, _ = zx.shape
    xbc = cw.shape[1]
    cbk = SHIFT_COLS
    off = d_inner // cbk
    rows = min(SHIFT_ROWS, seq)
    nblk = [p.shape[2] // cbk for p in dparts]
    first = [sum(nblk[:s]) for s in range(len(dparts))]
    assert sum(nblk) == xbc // cbk

    def body(h_ref, p_ref, gx_ref, gb_ref, gc_ref, w_ref, o_ref, dw_ref, db_ref):
        w = w_ref[...]
        j = pl.program_id(0)

        def chunk(c, carry):
            dws, dbias = carry
            here = pl.ds(pl.multiple_of(c * rows, rows), rows)
            pre = _window(p_ref, c, rows, seq, 0, HALO)
            s = _sigmoid(pre)
            gsel = jnp.where(j < first[1], _window(gx_ref, c, rows, seq, 0, HALO),
                             jnp.where(j < first[2], _window(gb_ref, c, rows, seq, 0, HALO),
                                       _window(gc_ref, c, rows, seq, 0, HALO)))
            dpre = gsel * (s * (1.0 + pre * (1.0 - s)))
            leads = [dpre] + [_lead(dpre, k) for k in range(1, SSD_CONV)]
            dx = w[3:4, :] * leads[0]
            for k in range(1, SSD_CONV):
                dx = dx + w[3 - k:4 - k, :] * leads[k]
            o_ref[here, :] = dx[:rows, :].astype(BF16)
            x0 = h_ref[here, :]
            dws = tuple(dws[k] + jnp.sum(leads[k][:rows, :] * x0, axis=0, keepdims=True) for k in range(SSD_CONV))
            dbias = dbias + jnp.sum(dpre[:rows, :], axis=0, keepdims=True)
            return dws, dbias

        z = jnp.zeros((1, cbk), F32)
        dws, dbias = lax.fori_loop(0, seq // rows, chunk, ((z,) * SSD_CONV, z))
        dwv = jnp.concatenate([dws[3 - i] for i in range(SSD_CONV)], axis=0)

        @pl.when(pl.program_id(1) == 0)
        def _():
            dw_ref[...] = dwv
            db_ref[...] = dbias

        @pl.when(pl.program_id(1) > 0)
        def _():
            dw_ref[...] += dwv
            db_ref[...] += dbias

    return pl.pallas_call(
        body,
        out_shape=(jax.ShapeDtypeStruct((b, seq, xbc), BF16), jax.ShapeDtypeStruct((SSD_CONV, xbc), F32),
                   jax.ShapeDtypeStruct((1, xbc), F32)),
        grid=(xbc // cbk, b),
        in_specs=[pl.BlockSpec((None, seq, cbk), lambda j, i: (i, 0, j + off)),
                  pl.BlockSpec((None, seq, cbk), lambda j, i: (i, 0, j))] + [
                  pl.BlockSpec((None, seq, cbk), lambda j, i, s=s: (i, 0, jnp.clip(j - first[s], 0, nblk[s] - 1)))
                  for s in range(3)] + [
                  pl.BlockSpec((SSD_CONV, cbk), lambda j, i: (0, j))],
        out_specs=(pl.BlockSpec((None, seq, cbk), lambda j, i: (i, 0, j)),
                   pl.BlockSpec((SSD_CONV, cbk), lambda j, i: (0, j)),
                   pl.BlockSpec((1, cbk), lambda j, i: (0, j))),
        compiler_params=_params(("parallel", "arbitrary")), name=name)(zx, pre, *dparts, cw)


def _pool_sums(q, g, lead):
    sh = _lead if lead else _lag
    s2 = q + sh(q, 1)
    s4 = s2 + sh(s2, 2)
    s8 = s4 + sh(s4, 4)
    s16 = s8 + sh(s8, 8)
    return jnp.where(g == 0, s2, jnp.where(g == 1, s4, jnp.where(g == 2, s8, s16)))


def _pool_count(r0, n, g, shape):
    t = (r0 + lax.broadcasted_iota(jnp.int32, shape, 0) + 1).astype(F32)
    return jnp.minimum(t, (2 << g).astype(F32))


def _pool_fwd(h, pw, scale, name):
    b, seq, d = h.shape
    dg = d // 4
    rows = min(SHIFT_ROWS, seq)

    def body(h_ref, w_ref, s_ref, o_ref):
        g = pl.program_id(1)
        wmat = w_ref[...]
        sc = s_ref[...]

        def chunk(c, carry):
            r0 = c * rows
            ext = _window(h_ref, c, rows, seq, 16, 0)
            sums = _pool_sums(ext, g, False)[16:, :]
            mixed = sums / _pool_count(r0, rows, g, (rows, dg)) - ext[16:, :]
            o_ref[pl.ds(pl.multiple_of(r0, rows), rows), :] = _nn(mixed.astype(BF16), wmat) * sc
            return carry

        lax.fori_loop(0, seq // rows, chunk, 0)

    return pl.pallas_call(
        body, out_shape=jax.ShapeDtypeStruct((b, seq, d), F32), grid=(b, 4),
        in_specs=[pl.BlockSpec((None, seq, dg), lambda i, g: (i, 0, g)),
                  pl.BlockSpec((None, dg, dg), lambda i, g: (g, 0, 0)),
                  pl.BlockSpec((1, dg), lambda i, g: (0, g))],
        out_specs=pl.BlockSpec((None, seq, dg), lambda i, g: (i, 0, g)),
        compiler_params=_params(("parallel", "parallel")), name=name)(h, pw, scale)


def _pool_bwd(h, dout, pw, scale, name):
    b, seq, d = h.shape
    dg = d // 4
    rows = min(SHIFT_ROWS, seq)

    def body(h_ref, g_ref, w_ref, s_ref, o_ref, dw_ref, ds_ref, dw_acc):
        g = pl.program_id(0)
        wmat = w_ref[...]
        sc = s_ref[...]
        dw_acc[...] = jnp.zeros_like(dw_acc)

        def chunk(c, dsc):
            r0 = c * rows
            ext = _window(h_ref, c, rows, seq, 16, 0)
            sums = _pool_sums(ext, g, False)[16:, :]
            mixed = (sums / _pool_count(r0, rows, g, (rows, dg)) - ext[16:, :]).astype(BF16)
            gext = _window(g_ref, c, rows, seq, 0, 16)
            dsc = dsc + jnp.sum(gext[:rows, :] * _nn(mixed, wmat), axis=0, keepdims=True)
            dpre = (gext * sc).astype(BF16)
            dw_acc[...] += _tn(mixed, dpre[:rows, :])
            dmix = _nt(dpre, wmat)
            q = dmix / _pool_count(r0, rows + 16, g, (rows + 16, dg))
            back = _pool_sums(q, g, True)
            o_ref[pl.ds(pl.multiple_of(r0, rows), rows), :] = back[:rows, :] - dmix[:rows, :]
            return dsc

        dsc = lax.fori_loop(0, seq // rows, chunk, jnp.zeros((1, dg), F32))

        @pl.when(pl.program_id(1) == 0)
        def _():
            dw_ref[...] = dw_acc[...]
            ds_ref[...] = dsc

        @pl.when(pl.program_id(1) > 0)
        def _():
            dw_ref[...] += dw_acc[...]
            ds_ref[...] += dsc

    return pl.pallas_call(
        body,
        out_shape=(jax.ShapeDtypeStruct((b, seq, d), F32), jax.ShapeDtypeStruct((4, dg, dg), F32),
                   jax.ShapeDtypeStruct((1, d), F32)),
        grid=(4, b),
        in_specs=[pl.BlockSpec((None, seq, dg), lambda g, i: (i, 0, g)),
                  pl.BlockSpec((None, seq, dg), lambda g, i: (i, 0, g)),
                  pl.BlockSpec((None, dg, dg), lambda g, i: (g, 0, 0)),
                  pl.BlockSpec((1, dg), lambda g, i: (0, g))],
        out_specs=(pl.BlockSpec((None, seq, dg), lambda g, i: (i, 0, g)),
                   pl.BlockSpec((None, dg, dg), lambda g, i: (g, 0, 0)),
                   pl.BlockSpec((1, dg), lambda g, i: (0, g))),
        scratch_shapes=[pltpu.VMEM((dg, dg), F32)],
        compiler_params=_params(("parallel", "arbitrary")), name=name)(h, dout, pw, scale)


def _head_of(channel):
    return jnp.right_shift(channel, HEAD_DIM.bit_length() - 1)


def _ssd_consts(gw):
    q = CHUNK
    row = lax.broadcasted_iota(jnp.int32, (q, q), 0)
    col = lax.broadcasted_iota(jnp.int32, (q, q), 1)
    tril = (row >= col).astype(BF16)
    triu = (row <= col).astype(BF16)
    e = (_head_of(lax.broadcasted_iota(jnp.int32, (LANES, gw), 1))
         == lax.broadcasted_iota(jnp.int32, (LANES, gw), 0)).astype(BF16)
    et = (_head_of(lax.broadcasted_iota(jnp.int32, (gw, LANES), 0))
          == lax.broadcasted_iota(jnp.int32, (gw, LANES), 1)).astype(BF16)
    return row, col, tril, triu, e, et


def _ssd_common(dtr, dtb, alog, gw):
    q = CHUNK
    row, col, tril, triu, e, et = _ssd_consts(gw)
    dt = _softplus(dtr + dtb)
    a_row = -jnp.exp(alog)
    acum = _sel_left(tril, dt * a_row)
    ac_last = jnp.sum(jnp.where(row == q - 1, acum, 0.0), axis=0, keepdims=True)
    eac = jnp.exp(acum)
    de = jnp.exp(ac_last - acum)
    e2 = jnp.concatenate([e, e], axis=0)
    expand = _sel_right(jnp.concatenate([dt, eac, de], axis=0), e2, 2)
    dt_x, eac_x, de_x = expand[0:q], expand[q:2 * q], expand[2 * q:3 * q]
    acum_t = acum.T
    cd_col = jnp.exp(acum_t[:, q - 1:q])
    et3 = jnp.concatenate([et, et, et], axis=1)
    cdmat = _nn(et3, jnp.concatenate(_split(jnp.broadcast_to(cd_col, (LANES, D_STATE)), 3), axis=0))
    consts = dict(row=row, col=col, tril=tril, triu=triu, e=e, et=et)
    return dt, a_row, acum, acum_t, ac_last, eac, de, dt_x, eac_x, de_x, cdmat, consts


def _decay(acum, acum_t, j, row, col):
    diff = acum[:, j:j + 1] - acum_t[j:j + 1, :]
    return jnp.exp(jnp.where(row >= col, diff, -1e30))


def _ssd_fwd(xc, zx, dtb, alog, dskip, nw, d_inner, name):
    b, seq, xbc = xc.shape
    q = CHUNK
    nc = seq // q
    gw = d_inner // N_GROUPS
    nh = gw // HEAD_DIM
    xb0 = d_inner // D_STATE
    xc0 = xb0 + N_GROUPS
    dt0 = (d_inner + xbc) // LANES

    def body(x_ref, b_ref, c_ref, z_ref, dtr_ref, dtb_ref, al_ref, dsk_ref, nw_ref, y_ref, yn_ref, st_ref, s_ref):
        @pl.when(pl.program_id(2) == 0)
        def _():
            s_ref[...] = jnp.zeros_like(s_ref)

        prev = s_ref[...]
        st_ref[...] = prev
        x = x_ref[...]
        bm = b_ref[...].astype(BF16)
        cm = c_ref[...].astype(BF16)
        (dt, a_row, acum, acum_t, ac_last, eac, de, dt_x, eac_x, de_x, cdmat, k) = _ssd_common(
            dtr_ref[...], dtb_ref[0:1, :], al_ref[0:1, :], gw)
        xdt = x * dt_x
        xdt_b = xdt.astype(BF16)
        cb = _nt(cm, bm)
        half = _head_of(lax.broadcasted_iota(jnp.int32, (q, LANES), 1))
        pairs = []
        for j in range(nh):
            pc = (j // 2) * LANES
            m = (cb * _decay(acum, acum_t, j, k["row"], k["col"])).astype(BF16)
            yj = jnp.where(half == j % 2, _nn(m, xdt_b[:, pc:pc + LANES]), 0.0)
            if j % 2 == 0:
                pairs.append(yj)
            else:
                pairs[-1] = pairs[-1] + yj
        prev_b = prev.astype(BF16)
        y = dsk_ref[0:1, :] * x + jnp.concatenate(pairs, axis=1) + eac_x * _nt(cm, prev_b)
        s_ref[...] = cdmat * prev + _tn((xdt * de_x).astype(BF16), bm)
        y_ref[...] = y
        z = z_ref[...]
        yg = y * (z * _sigmoid(z))
        r = lax.rsqrt(jnp.mean(yg * yg, axis=-1, keepdims=True) + EPS)
        yn_ref[...] = ((yg * r) * nw_ref[0:1, :]).astype(BF16)

    par = lambda w: pl.BlockSpec((None, 8, w), lambda i, g, c: (g, 0, 0))
    return pl.pallas_call(
        body,
        out_shape=(jax.ShapeDtypeStruct((b, seq, d_inner), F32), jax.ShapeDtypeStruct((b, seq, d_inner), BF16),
                   jax.ShapeDtypeStruct((b, nc, N_GROUPS, gw, D_STATE), F32)),
        grid=(b, N_GROUPS, nc),
        in_specs=[pl.BlockSpec((None, q, gw), lambda i, g, c: (i, c, g)),
                  pl.BlockSpec((None, q, D_STATE), lambda i, g, c: (i, c, xb0 + g)),
                  pl.BlockSpec((None, q, D_STATE), lambda i, g, c: (i, c, xc0 + g)),
                  pl.BlockSpec((None, q, gw), lambda i, g, c: (i, c, g)),
                  pl.BlockSpec((None, q, LANES), lambda i, g, c: (i, c, dt0 + g)),
                  par(LANES), par(LANES), par(gw), par(gw)],
        out_specs=(pl.BlockSpec((None, q, gw), lambda i, g, c: (i, c, g)),
                   pl.BlockSpec((None, q, gw), lambda i, g, c: (i, c, g)),
                   pl.BlockSpec((None, None, None, gw, D_STATE), lambda i, g, c: (i, c, g, 0, 0))),
        scratch_shapes=[pltpu.VMEM((gw, D_STATE), F32)],
        compiler_params=_params(("parallel", "parallel", "arbitrary")), name=name,
    )(xc, xc, xc, zx, zx, dtb, alog, dskip, nw)


def _ssd_bwd(xc, zx, y, dyn, st, dtb, alog, dskip, nw, d_inner, name):
    b, seq, xbc = xc.shape
    q = CHUNK
    nc = seq // q
    gw = d_inner // N_GROUPS
    nh = gw // HEAD_DIM
    xb0 = d_inner // D_STATE
    xc0 = xb0 + N_GROUPS
    dt0 = (d_inner + xbc) // LANES

    def body(x_ref, b_ref, c_ref, z_ref, dtr_ref, y_ref, g_ref, st_ref, dtb_ref, al_ref, dsk_ref, nw_ref,
             dz_ref, dx_ref, db_ref, dc_ref, ddt_ref, dnw_ref, dd_ref, dal_ref, dbias_ref,
             ds_ref, colbuf, rowbuf):
        first = jnp.logical_and(pl.program_id(1) == 0, pl.program_id(2) == 0)

        @pl.when(pl.program_id(2) == 0)
        def _():
            ds_ref[...] = jnp.zeros_like(ds_ref)

        x = x_ref[...]
        bm = b_ref[...].astype(BF16)
        cm = c_ref[...].astype(BF16)
        z = z_ref[...]
        y = y_ref[...]
        prev = st_ref[...]
        dtr = dtr_ref[...] + dtb_ref[0:1, :]
        (dt, a_row, acum, acum_t, ac_last, eac, de, dt_x, eac_x, de_x, cdmat, k) = _ssd_common(
            dtr_ref[...], dtb_ref[0:1, :], al_ref[0:1, :], gw)
        row, col = k["row"], k["col"]
        et2 = jnp.concatenate([k["et"], k["et"]], axis=0)

        sz = _sigmoid(z)
        silu_z = z * sz
        yg = y * silu_z
        r = lax.rsqrt(jnp.mean(yg * yg, axis=-1, keepdims=True) + EPS)
        xh = yg * r
        dyn = g_ref[...]
        gh = dyn * nw_ref[0:1, :]
        dyg = r * (gh - xh * jnp.mean(gh * xh, axis=-1, keepdims=True))
        dnw = jnp.sum(dyn * xh, axis=0, keepdims=True)
        g = dyg * silu_z
        dz_ref[...] = (dyg * y * (sz * (1.0 + z * (1.0 - sz)))).astype(BF16)
        dd = _sel_right(jnp.broadcast_to(jnp.sum(g * x, axis=0, keepdims=True), (8, gw)), et2, 2)

        xdt = x * dt_x
        xdt_b = xdt.astype(BF16)
        g_b = g.astype(BF16)
        prev_b = prev.astype(BF16)
        cb = _nt(cm, bm)

        cp = _nt(cm, prev_b)
        ge = g * eac_x
        dac = _sel_right(ge * cp, et2, 2)
        ge_b = ge.astype(BF16)
        dcm = _nn(ge_b, prev_b)
        dprev = _tn(ge_b, cm)

        colbuf[...] = jnp.zeros_like(colbuf)
        rowbuf[...] = jnp.zeros_like(rowbuf)
        dcb = jnp.zeros((q, q), F32)
        half = _head_of(lax.broadcasted_iota(jnp.int32, (q, LANES), 1))
        pairs = []
        for j in range(nh):
            pc = (j // 2) * LANES
            dec = _decay(acum, acum_t, j, row, col)
            m = cb * dec
            gj = jnp.where(half == j % 2, g[:, pc:pc + LANES], 0.0).astype(BF16)
            dm = _nt(gj, xdt_b[:, pc:pc + LANES])
            w = dm * m
            colbuf[:, j:j + 1] = jnp.sum(w, axis=1, keepdims=True)
            rowbuf[j:j + 1, :] = jnp.sum(w, axis=0, keepdims=True)
            dcb = dcb + dm * dec
            dj = jnp.where(half == j % 2, _tn(m.astype(BF16), g_b[:, pc:pc + LANES]), 0.0)
            if j % 2 == 0:
                pairs.append(dj)
            else:
                pairs[-1] = pairs[-1] + dj
        dxdt = jnp.concatenate(pairs, axis=1)
        dcb_b = dcb.astype(BF16)
        dcm = dcm + _nn(dcb_b, bm)
        dbm = _tn(dcb_b, cm)

        ds = ds_ref[...]
        ds_b = ds.astype(BF16)
        u = _nt(bm, ds_b)
        dxdt = dxdt + u * de_x
        dde = _sel_right(u * xdt, et2, 2)
        dbm = dbm + _nn((xdt * de_x).astype(BF16), ds_b)
        pm = jnp.concatenate(_split(ds * prev, 2), axis=1)
        t2 = _tn(pm, k["et"])
        dcd_row = jnp.sum(t2[0:D_STATE] + t2[D_STATE:2 * D_STATE], axis=0, keepdims=True)
        last = dcd_row * jnp.exp(ac_last) + jnp.sum(dde * de, axis=0, keepdims=True)
        dac = dac + colbuf[...] - rowbuf[...].T - dde * de + jnp.where(row == q - 1, last, 0.0)
        ds_ref[...] = cdmat * ds + dprev

        dadt = _sel_left(k["triu"], dac)
        ddt = _sel_right(dxdt * x, et2, 2) + dadt * a_row
        dal = jnp.sum(dadt * dt, axis=0, keepdims=True) * a_row
        lane = lax.broadcasted_iota(jnp.int32, (q, LANES), 1)
        ddtr = jnp.where(lane < nh, ddt * _sigmoid(dtr), 0.0)
        ddt_ref[...] = ddtr.astype(BF16)
        dbias = jnp.sum(ddtr, axis=0, keepdims=True)
        dx_ref[...] = dxdt * dt_x + dsk_ref[0:1, :] * g
        db_ref[...] = dbm
        dc_ref[...] = dcm

        @pl.when(first)
        def _():
            dnw_ref[...] = jnp.broadcast_to(dnw, (8, gw))
            dd_ref[...] = dd
            dal_ref[...] = jnp.broadcast_to(dal, (8, LANES))
            dbias_ref[...] = jnp.broadcast_to(dbias, (8, LANES))

        @pl.when(jnp.logical_not(first))
        def _():
            dnw_ref[...] += jnp.broadcast_to(dnw, (8, gw))
            dd_ref[...] += dd
            dal_ref[...] += jnp.broadcast_to(dal, (8, LANES))
            dbias_ref[...] += jnp.broadcast_to(dbias, (8, LANES))

    rc = lambda c: nc - 1 - c
    par = lambda w: pl.BlockSpec((None, 8, w), lambda g, i, c: (g, 0, 0))
    blk = lambda w: pl.BlockSpec((None, q, w), lambda g, i, c: (i, rc(c), g))
    return pl.pallas_call(
        body,
        out_shape=(jax.ShapeDtypeStruct((b, seq, d_inner), BF16),
                   jax.ShapeDtypeStruct((b, seq, d_inner), F32),
                   jax.ShapeDtypeStruct((b, seq, N_GROUPS * D_STATE), F32),
                   jax.ShapeDtypeStruct((b, seq, N_GROUPS * D_STATE), F32),
                   jax.ShapeDtypeStruct((b, seq, N_GROUPS * LANES), BF16),
                   jax.ShapeDtypeStruct((N_GROUPS, 8, gw), F32),
                   jax.ShapeDtypeStruct((N_GROUPS, 8, LANES), F32),
                   jax.ShapeDtypeStruct((N_GROUPS, 8, LANES), F32),
                   jax.ShapeDtypeStruct((N_GROUPS, 8, LANES), F32)),
        grid=(N_GROUPS, b, nc),
        in_specs=[blk(gw),
                  pl.BlockSpec((None, q, D_STATE), lambda g, i, c: (i, rc(c), xb0 + g)),
                  pl.BlockSpec((None, q, D_STATE), lambda g, i, c: (i, rc(c), xc0 + g)),
                  blk(gw),
                  pl.BlockSpec((None, q, LANES), lambda g, i, c: (i, rc(c), dt0 + g)),
                  blk(gw), blk(gw),
                  pl.BlockSpec((None, None, None, gw, D_STATE), lambda g, i, c: (i, rc(c), g, 0, 0)),
                  par(LANES), par(LANES), par(gw), par(gw)],
        out_specs=(blk(gw), blk(gw), blk(D_STATE), blk(D_STATE), blk(LANES),
                   par(gw), par(LANES), par(LANES), par(LANES)),
        scratch_shapes=[pltpu.VMEM((gw, D_STATE), F32), pltpu.VMEM((q, LANES), F32), pltpu.VMEM((LANES, q), F32)],
        compiler_params=_params(("parallel", "arbitrary", "arbitrary")), name=name,
    )(xc, xc, xc, zx, zx, y, dyn, st, dtb, alog, dskip, nw)


def _adamw(w, g, m, v, name):
    rows, cols = w.shape
    tr = rows
    for cand in (512, 256, 128, 64, 32, 16, 8):
        if rows % cand == 0 and cand * cols * 4 <= 2 * 1024 * 1024:
            tr = cand
            break
    c1 = 1.0 - ADAM_B1 ** ADAM_STEP
    c2 = 1.0 - ADAM_B2 ** ADAM_STEP

    def body(w_ref, g_ref, m_ref, v_ref, d_ref, mo_ref, vo_ref):
        gv = g_ref[...]
        mn = ADAM_B1 * m_ref[...] + (1.0 - ADAM_B1) * gv
        vn = ADAM_B2 * v_ref[...] + (1.0 - ADAM_B2) * (gv * gv)
        mo_ref[...] = mn
        vo_ref[...] = vn
        d_ref[...] = -ADAM_LR * ((mn / c1) / (jnp.sqrt(vn / c2) + ADAM_EPS) + ADAM_WD * w_ref[...])

    spec = pl.BlockSpec((tr, cols), lambda i: (i, 0))
    shp = jax.ShapeDtypeStruct((rows, cols), F32)
    return pl.pallas_call(body, out_shape=(shp, shp, shp), grid=(rows // tr,), in_specs=[spec] * 4,
                          out_specs=(spec,) * 3, compiler_params=_params(("parallel",)), name=name)(w, g, m, v)


def _pick_rows(rows, row_bytes, limit=1 << 20):
    for cand in (2048, 1024, 512, 256, 128, 64, 32, 16):
        if rows % cand == 0 and cand * row_bytes <= limit:
            return cand
    return rows


def _as3d(a, lead):
    return a.reshape(a.shape[:lead] + (-1, a.shape[-1]))


def _pair_sum(g, got, core, name):
    h = got.shape[0]
    g3, got3 = _as3d(g, 1), _as3d(got, 1)
    _, rows, cols = got3.shape
    tr = _pick_rows(rows, cols * 4)

    def body(c_ref, g_ref, r_ref, o_ref):
        o_ref[...] = (g_ref[...] + r_ref[...]).astype(BF16)

    out = pl.pallas_call(
        body, out_shape=jax.ShapeDtypeStruct(got3.shape, BF16),
        grid_spec=pltpu.PrefetchScalarGridSpec(
            num_scalar_prefetch=1, grid=(h, rows // tr),
            in_specs=[pl.BlockSpec((None, tr, cols), lambda l, i, c_ref: (c_ref[0] * h + l, i, 0)),
                      pl.BlockSpec((None, tr, cols), lambda l, i, c_ref: (l, i, 0))],
            out_specs=pl.BlockSpec((None, tr, cols), lambda l, i, c_ref: (l, i, 0))),
        compiler_params=_params(("parallel", "parallel")), name=name)(core, g3, got3)
    return out.reshape(got.shape)


def _sum4(q, core, name):
    q4 = _as3d(q, 2)
    _, h, rows, cols = q4.shape
    tr = _pick_rows(rows, cols * 4)

    def body(c_ref, q0, q1, q2, q3, o_ref):
        o_ref[...] = ((q0[...].astype(F32) + q1[...].astype(F32)) + q2[...].astype(F32)) + q3[...].astype(F32)

    out = pl.pallas_call(
        body, out_shape=jax.ShapeDtypeStruct((2 * h, rows, cols), F32),
        grid_spec=pltpu.PrefetchScalarGridSpec(
            num_scalar_prefetch=1, grid=(h, rows // tr),
            in_specs=[pl.BlockSpec((None, None, tr, cols), lambda l, i, c_ref, k=k: (k, l, i, 0))
                      for k in range(N_CHIPS)],
            out_specs=pl.BlockSpec((None, tr, cols), lambda l, i, c_ref: (c_ref[0] * h + l, i, 0))),
        compiler_params=_params(("parallel", "parallel")), name=name)(core, q4, q4, q4, q4)
    return out.reshape((2 * h,) + q.shape[2:])


def _coords():
    return lax.axis_index("x"), lax.axis_index("y"), lax.axis_index("c")


def _other_chips(x, y):
    return [(1 - x, y), (x, 1 - y), (1 - x, 1 - y)]


def _allgather_halves(src, name):
    rows, cols = src.shape

    def body(x_ref, o_ref, send, recv, local):
        x, y, c = _coords()
        sib = (x, y, 1 - c)
        chips = _other_chips(x, y)

        def slot(h, cx, cy):
            return o_ref.at[h, 2 * cx + cy]

        def copy(kk, dst, to, src_ref):
            return pltpu.make_async_remote_copy(src_ref=src_ref, dst_ref=dst, send_sem=send.at[kk],
                                                recv_sem=recv.at[kk], device_id=to, device_id_type=MESH)

        mine = pltpu.make_async_copy(x_ref, slot(c, x, y), local)
        mine.start()
        first = [copy(0, slot(c, x, y), sib, x_ref)]
        first += [copy(1 + j, slot(c, x, y), (*chip, c), x_ref) for j, chip in enumerate(chips)]
        for cp in first:
            cp.start()
        passed = [copy(4 + j, slot(c, *chip), sib, slot(c, *chip)) for j, chip in enumerate(chips)]
        for j, chip in enumerate(chips):
            copy(1 + j, slot(c, *chip), (x, y, c), x_ref).wait_recv()
            passed[j].start()
        copy(0, slot(1 - c, x, y), (x, y, c), x_ref).wait_recv()
        for j, chip in enumerate(chips):
            copy(4 + j, slot(1 - c, *chip), (x, y, c), x_ref).wait_recv()
        for cp in first + passed:
            cp.wait_send()
        mine.wait()

    return pl.pallas_call(
        body, out_shape=jax.ShapeDtypeStruct((2, N_CHIPS, rows, cols), src.dtype),
        in_specs=[ANY], out_specs=ANY,
        scratch_shapes=[pltpu.SemaphoreType.DMA((7,)), pltpu.SemaphoreType.DMA((7,)), pltpu.SemaphoreType.DMA],
        name=name)(src)


MIXW = (("ssd_w_in", None), ("ssd_w_out", 0), ("pool_w", 1))
FFNW = (("ffn_w_up", 1), ("ffn_w_down", 0))


def _chip_window(axis, ref, layers, k):
    if axis is None:
        return ref.at[layers, k]
    n = ref.shape[1 + axis] // N_CHIPS
    sl = pl.ds(pl.multiple_of(k * n, LANES if 1 + axis == len(ref.shape) - 1 else 8), n)
    idx = [layers] + [slice(None)] * (len(ref.shape) - 1)
    idx[1 + axis] = sl
    return ref.at[tuple(idx)]


def _full_shape(axis, shard_shape):
    if axis is None:
        return (shard_shape[0], N_CHIPS) + tuple(shard_shape[1:])
    full = list(shard_shape)
    full[1 + axis] *= N_CHIPS
    return tuple(full)


HBM_SPEC = pl.BlockSpec(memory_space=pltpu.HBM)
SEM_SPEC = pl.BlockSpec(memory_space=pltpu.SEMAPHORE)


def _dma_sems(count):
    return pltpu.SemaphoreType.DMA((max(count, 1),))


def _wait_for(copy, kind):
    if kind == "recv":
        copy.wait_recv()
    elif kind == "send":
        copy.wait_send()
    else:
        copy.wait()


def _comm_fused(stages, counts, srcs, lands, name, inplace=False):
    ns, nl, k = len(srcs), len(lands), len(stages)

    def body(*refs):
        src_refs = refs[:ns]
        land_refs = refs[ns + (nl if inplace else 0):ns + (nl if inplace else 0) + nl]
        sem_refs = refs[len(refs) - 3 * k:]
        for s, stage_fn in enumerate(stages):
            starts, waits = stage_fn(src_refs, land_refs, tuple(sem_refs[3 * s:3 * s + 3]))
            for cp in starts:
                cp.start()
            for cp, kind in waits:
                _wait_for(cp, kind)

    scratch = []
    for cnt in counts:
        scratch += [_dma_sems(c) for c in cnt]
    outs = pl.pallas_call(
        body, out_shape=tuple(jax.ShapeDtypeStruct(a.shape, a.dtype) for a in lands),
        in_specs=[ANY] * (ns + (nl if inplace else 0)), out_specs=(ANY,) * nl,
        input_output_aliases={ns + i: i for i in range(nl)} if inplace else {},
        scratch_shapes=scratch, name=name)(*srcs, *(lands if inplace else ()))
    return list(outs)


class _SplitComm:
    def __init__(self, stages, counts, srcs, lands, name):
        self.stages, self.counts, self.name = stages, counts, name
        self.ns = len(srcs)
        self.data = [pltpu.with_memory_space_constraint(a, pltpu.HBM) for a in list(srcs) + list(lands)]
        self.sems = None
        self.step = 0

    def advance(self, after=None):
        i, k, nd, ns = self.step, len(self.stages), len(self.data), self.ns
        first, last = i == 0, i == k
        stages = self.stages

        def body(*refs):
            data = refs[:nd]
            pos = nd
            if not first:
                old = tuple(refs[pos:pos + 3])
                pos += 4
            if not last:
                new = tuple(refs[pos:pos + 3])
            if not first:
                for cp, kind in stages[i - 1](data[:ns], data[ns:], old)[1]:
                    _wait_for(cp, kind)
            if not last:
                for cp in stages[i](data[:ns], data[ns:], new)[0]:
                    cp.start()
                refs[len(refs) - 1][...] = jnp.zeros((8, LANES), F32)

        args = list(self.data)
        in_specs = [HBM_SPEC] * nd
        if not first:
            args += list(self.sems) + [after]
            in_specs += [SEM_SPEC] * 3 + [ANY]
        out_shape, out_specs = [], []
        if not last:
            out_shape += [_dma_sems(c) for c in self.counts[i]]
            out_specs += [SEM_SPEC] * 3
        out_shape += [pltpu.HBM(a.shape, a.dtype) for a in self.data]
        out_specs += [HBM_SPEC] * nd
        if not last:
            out_shape.append(jax.ShapeDtypeStruct((8, LANES), F32))
            out_specs.append(pl.BlockSpec(memory_space=pltpu.VMEM))
        off = 0 if last else 3
        outs = pl.pallas_call(
            body, out_shape=tuple(out_shape), in_specs=in_specs, out_specs=tuple(out_specs),
            input_output_aliases={d: off + d for d in range(nd)},
            compiler_params=pltpu.CompilerParams(has_side_effects=pltpu.SideEffectType.DATAFLOW_SIDE_EFFECTING),
            name=f"{self.name}_{i}")(*args)
        self.sems = None if last else outs[:3]
        self.data = list(outs[off:off + nd])
        self.step += 1
        return None if last else outs[len(outs) - 1]

    def lands(self):
        return self.data[self.ns:]


def _gather_stages(spec):
    n = len(spec)

    def parts(srcs, lands):
        x, y, c = _coords()
        out = []
        for w, (_, axis) in enumerate(spec):
            h = srcs[w].shape[0] // 2
            mine, theirs = pl.ds(c * h, h), pl.ds((1 - c) * h, h)
            out.append((srcs[w].at[mine], lambda layers, k, w=w, axis=axis: _chip_window(axis, lands[w], layers, k),
                        mine, theirs))
        return x, y, c, 2 * x + y, (x, y, 1 - c), _other_chips(x, y), out

    def remote(src, dst, send, recv, idx, to):
        return pltpu.make_async_remote_copy(src_ref=src, dst_ref=dst, send_sem=send.at[idx], recv_sem=recv.at[idx],
                                            device_id=to, device_id_type=MESH)

    def stage0(srcs, lands, sems):
        send, recv, local = sems
        x, y, c, me, sib, chips, ps = parts(srcs, lands)
        starts, waits = [], []
        for w, (src, dst, mine, theirs) in enumerate(ps):
            lc = pltpu.make_async_copy(src, dst(mine, me), local.at[w])
            first = [remote(src, dst(mine, me), send, recv, 4 * w, sib)]
            first += [remote(src, dst(mine, me), send, recv, 4 * w + 1 + j, (cx, cy, c)) for j, (cx, cy) in enumerate(chips)]
            starts += [lc] + first
            waits.append((remote(src, dst(theirs, me), send, recv, 4 * w, (x, y, c)), "recv"))
            waits += [(remote(src, dst(mine, 2 * cx + cy), send, recv, 4 * w + 1 + j, (x, y, c)), "recv")
                      for j, (cx, cy) in enumerate(chips)]
            waits += [(cp, "send") for cp in first] + [(lc, "local")]
        return starts, waits

    def stage1(srcs, lands, sems):
        send, recv, _ = sems
        x, y, c, me, sib, chips, ps = parts(srcs, lands)
        starts, waits = [], []
        for w, (src, dst, mine, theirs) in enumerate(ps):
            for j, (cx, cy) in enumerate(chips):
                blk = dst(mine, 2 * cx + cy)
                fwd = remote(blk, blk, send, recv, 3 * w + j, sib)
                starts.append(fwd)
                waits.append((remote(src, dst(theirs, 2 * cx + cy), send, recv, 3 * w + j, (x, y, c)), "recv"))
                waits.append((fwd, "send"))
        return starts, waits

    return [stage0, stage1], [(4 * n, 4 * n, n), (3 * n, 3 * n, 0)]


def _swap_stages(spec):
    n = len(spec)

    def stage(srcs, lands, sems):
        send, recv, _ = sems
        x, y, c = _coords()
        starts, waits = [], []
        for w in range(n):
            h = srcs[w].shape[0] // 2
            cp = pltpu.make_async_remote_copy(src_ref=srcs[w].at[pl.ds((1 - c) * h, h)], dst_ref=lands[w],
                                              send_sem=send.at[w], recv_sem=recv.at[w],
                                              device_id=(x, y, 1 - c), device_id_type=MESH)
            starts.append(cp)
            waits += [(cp, "recv"), (cp, "send")]
        return starts, waits

    return [stage], [(n, n, 0)]


def _scatter_stages(spec):
    n = len(spec)

    def stage(srcs, lands, sems):
        send, recv, local = sems
        x, y, c = _coords()
        me = 2 * x + y
        starts, waits = [], []
        for w, (_, axis) in enumerate(spec):
            layers = pl.ds(0, srcs[w].shape[0])
            own = _chip_window(axis, srcs[w], layers, me)
            lc = pltpu.make_async_copy(own, lands[w].at[me], local.at[w])
            starts.append(lc)
            for j, (cx, cy) in enumerate(_other_chips(x, y)):
                cp = pltpu.make_async_remote_copy(src_ref=_chip_window(axis, srcs[w], layers, 2 * cx + cy),
                                                  dst_ref=lands[w].at[me], send_sem=send.at[3 * w + j],
                                                  recv_sem=recv.at[3 * w + j], device_id=(cx, cy, c), device_id_type=MESH)
                starts.append(cp)
                waits.append((pltpu.make_async_remote_copy(
                    src_ref=own, dst_ref=lands[w].at[2 * cx + cy], send_sem=send.at[3 * w + j], recv_sem=recv.at[3 * w + j],
                    device_id=(x, y, c), device_id_type=MESH), "recv"))
                waits.append((cp, "send"))
            waits.append((lc, "local"))
        return starts, waits

    return [stage], [(3 * n, 3 * n, n)]


def _share_stages(spec):
    n = len(spec)

    def stage(srcs, lands, sems):
        send, recv, _ = sems
        x, y, c = _coords()
        starts, waits = [], []
        for w in range(n):
            h = lands[w].shape[0] // 2
            mine, theirs = lands[w].at[pl.ds(c * h, h)], lands[w].at[pl.ds((1 - c) * h, h)]
            cp = pltpu.make_async_remote_copy(src_ref=mine, dst_ref=mine, send_sem=send.at[w], recv_sem=recv.at[w],
                                              device_id=(x, y, 1 - c), device_id_type=MESH)
            starts.append(cp)
            waits.append((pltpu.make_async_remote_copy(src_ref=theirs, dst_ref=theirs, send_sem=send.at[w],
                                                       recv_sem=recv.at[w], device_id=(x, y, c), device_id_type=MESH),
                          "recv"))
            waits.append((cp, "send"))
        return starts, waits

    return [stage], [(n, n, 0)]


def _shard_of(p, axis):
    if axis is None:
        return (p.shape[0],) + tuple(p.shape[2:])
    s = list(p.shape)
    s[1 + axis] //= N_CHIPS
    return tuple(s)


def _reduce_begin(spec, gs, core, tag):
    stages, counts = _swap_stages(spec)
    got = _comm_fused(stages, counts, gs,
                      [jax.ShapeDtypeStruct((g.shape[0] // 2,) + g.shape[1:], g.dtype) for g in gs], "swap_" + tag)
    pair = [_pair_sum(a, r, core, "pair_sum_" + n) for a, r, (n, _) in zip(gs, got, spec)]
    stages, counts = _scatter_stages(spec)
    lands = [lax.empty((N_CHIPS,) + _shard_of(p, axis), p.dtype) for p, (_, axis) in zip(pair, spec)]
    comm = _SplitComm(stages, counts, pair, lands, "scatter_" + tag)
    return comm, comm.advance()


def _reduce_finish(spec, comm, core, tag, after):
    comm.advance(after=after)
    halves = [_sum4(q, core, "sum4_" + n) for q, (n, _) in zip(comm.lands(), spec)]
    stages, counts = _share_stages(spec)
    return _comm_fused(stages, counts, [], halves, "share_" + tag, inplace=True)


def _allreduce_small(vec, name, after=()):
    rows, cols = vec.shape
    after = list(after)

    def body(x_ref, *rest):
        o_ref, buf, send, recv = rest[len(after):]
        x, y, c = _coords()
        me = 4 * x + 2 * y + c
        buf[me] = x_ref[...]
        cps = []
        for kk in range(1, 8):
            dx, dy, dc = (kk >> 2) & 1, (kk >> 1) & 1, kk & 1
            to = (1 - x if dx else x, 1 - y if dy else y, 1 - c if dc else c)
            cp = pltpu.make_async_remote_copy(src_ref=x_ref, dst_ref=buf.at[me], send_sem=send.at[kk - 1],
                                              recv_sem=recv.at[kk - 1], device_id=to, device_id_type=MESH)
            cp.start()
            cps.append((cp, 4 * to[0] + 2 * to[1] + to[2]))
        for kk, (cp, frm) in enumerate(cps):
            pltpu.make_async_remote_copy(src_ref=x_ref, dst_ref=buf.at[frm], send_sem=send.at[kk],
                                         recv_sem=recv.at[kk], device_id=(x, y, c), device_id_type=MESH).wait_recv()
        for cp, _ in cps:
            cp.wait_send()
        acc = buf[0]
        for kk in range(1, 8):
            acc = acc + buf[kk]
        o_ref[...] = acc

    vm = pl.BlockSpec(memory_space=pltpu.VMEM)
    return pl.pallas_call(
        body, out_shape=jax.ShapeDtypeStruct((rows, cols), F32), in_specs=[vm] + [ANY] * len(after), out_specs=vm,
        scratch_shapes=[pltpu.VMEM((8, rows, cols), F32), pltpu.SemaphoreType.DMA((7,)), pltpu.SemaphoreType.DMA((7,))],
        compiler_params=_params(), name=name)(vec, *after)


SMALL = (("ssd_conv_w", 2), ("pool_scale", 1), ("ffn_conv_w", 2))
REPL = ("ssd_conv_b", "ssd_dt_bias", "ssd_a_log", "ssd_d", "ssd_norm_w", "ffn_conv_b",
        "norm_mix_pre", "norm_mix_post", "norm_ffn_pre", "norm_ffn_post")
WEIGHTS = ("ssd_w_in", "ssd_conv_w", "ssd_conv_b", "ssd_dt_bias", "ssd_a_log", "ssd_d", "ssd_norm_w", "ssd_w_out",
           "pool_w", "pool_scale", "ffn_w_up", "ffn_conv_w", "ffn_conv_b", "ffn_w_down", "norm_mix_pre",
           "norm_mix_post", "norm_ffn_pre", "norm_ffn_post")


def _flat_rows(n):
    unit = 2 * 16 * FLAT_COLS
    return 2 * 16 * ((n + unit - 1) // unit)


def _flatten_shards(arrs, dtype):
    flat = jnp.concatenate([a.astype(dtype).reshape(-1) for a in arrs])
    rows = _flat_rows(flat.shape[0])
    flat = jnp.pad(flat, (0, rows * FLAT_COLS - flat.shape[0]))
    return flat.reshape(2, rows // 2, FLAT_COLS)


def _unflatten_full(gathered, shard_shapes, axes):
    per_chip = jnp.swapaxes(gathered, 0, 1).reshape(N_CHIPS, -1)
    out, off = [], 0
    for shp, ax in zip(shard_shapes, axes):
        n = math.prod(shp)
        pieces = [per_chip[k, off:off + n].reshape(shp) for k in range(N_CHIPS)]
        out.append(jnp.concatenate(pieces, axis=ax))
        off += n
    return out


def kernel(x, ssd_w_in, ssd_conv_w, ssd_conv_b, ssd_dt_bias, ssd_a_log, ssd_d, ssd_norm_w, ssd_w_out, pool_w, pool_scale, ffn_w_up, ffn_conv_w, ffn_conv_b, ffn_w_down, norm_mix_pre, norm_mix_post, norm_ffn_pre, norm_ffn_post, loss_target, m_ssd_w_in, m_ssd_conv_w, m_ssd_conv_b, m_ssd_dt_bias, m_ssd_a_log, m_ssd_d, m_ssd_norm_w, m_ssd_w_out, m_pool_w, m_pool_scale, m_ffn_w_up, m_ffn_conv_w, m_ffn_conv_b, m_ffn_w_down, m_norm_mix_pre, m_norm_mix_post, m_norm_ffn_pre, m_norm_ffn_post, v_ssd_w_in, v_ssd_conv_w, v_ssd_conv_b, v_ssd_dt_bias, v_ssd_a_log, v_ssd_d, v_ssd_norm_w, v_ssd_w_out, v_pool_w, v_pool_scale, v_ffn_w_up, v_ffn_conv_w, v_ffn_conv_b, v_ffn_w_down, v_norm_mix_pre, v_norm_mix_post, v_norm_ffn_pre, v_norm_ffn_post):
    wts = dict(ssd_w_in=ssd_w_in, ssd_conv_w=ssd_conv_w, ssd_conv_b=ssd_conv_b, ssd_dt_bias=ssd_dt_bias,
               ssd_a_log=ssd_a_log, ssd_d=ssd_d, ssd_norm_w=ssd_norm_w, ssd_w_out=ssd_w_out, pool_w=pool_w,
               pool_scale=pool_scale, ffn_w_up=ffn_w_up, ffn_conv_w=ffn_conv_w, ffn_conv_b=ffn_conv_b,
               ffn_w_down=ffn_w_down, norm_mix_pre=norm_mix_pre, norm_mix_post=norm_mix_post,
               norm_ffn_pre=norm_ffn_pre, norm_ffn_post=norm_ffn_post)
    mom = dict(ssd_w_in=m_ssd_w_in, ssd_conv_w=m_ssd_conv_w, ssd_conv_b=m_ssd_conv_b, ssd_dt_bias=m_ssd_dt_bias,
               ssd_a_log=m_ssd_a_log, ssd_d=m_ssd_d, ssd_norm_w=m_ssd_norm_w, ssd_w_out=m_ssd_w_out, pool_w=m_pool_w,
               pool_scale=m_pool_scale, ffn_w_up=m_ffn_w_up, ffn_conv_w=m_ffn_conv_w, ffn_conv_b=m_ffn_conv_b,
               ffn_w_down=m_ffn_w_down, norm_mix_pre=m_norm_mix_pre, norm_mix_post=m_norm_mix_post,
               norm_ffn_pre=m_norm_ffn_pre, norm_ffn_post=m_norm_ffn_post)
    var = dict(ssd_w_in=v_ssd_w_in, ssd_conv_w=v_ssd_conv_w, ssd_conv_b=v_ssd_conv_b, ssd_dt_bias=v_ssd_dt_bias,
               ssd_a_log=v_ssd_a_log, ssd_d=v_ssd_d, ssd_norm_w=v_ssd_norm_w, ssd_w_out=v_ssd_w_out, pool_w=v_pool_w,
               pool_scale=v_pool_scale, ffn_w_up=v_ffn_w_up, ffn_conv_w=v_ffn_conv_w, ffn_conv_b=v_ffn_conv_b,
               ffn_w_down=v_ffn_w_down, norm_mix_pre=v_norm_mix_pre, norm_mix_post=v_norm_mix_post,
               norm_ffn_pre=v_norm_ffn_pre, norm_ffn_post=v_norm_ffn_post)

    bl, seq, d = x.shape
    t = bl * seq
    depth = norm_mix_pre.shape[0]
    n_ssd = ssd_w_out.shape[0]
    d_inner = ssd_w_out.shape[1] * N_CHIPS
    nheads = d_inner // HEAD_DIM
    hpg = nheads // N_GROUPS
    gw = d_inner // N_GROUPS
    xbc = ssd_conv_w.shape[2] * N_CHIPS
    f2 = ffn_w_up.shape[2] * N_CHIPS
    ff = f2 // 2
    dg = d // 4
    cy = lax.axis_index("c")
    chip = 2 * lax.axis_index("x") + lax.axis_index("y")

    small_shapes = [wts[n].shape for n, _ in SMALL]
    small_axes = [a for _, a in SMALL]
    small_flat = _flatten_shards([wts[n] for n, _ in SMALL], F32)
    small_half = lax.dynamic_index_in_dim(small_flat, cy, 0, keepdims=False)
    small_all = _allgather_halves(small_half, "gather_small")
    conv_w, p_scale, f_conv_w = _unflatten_full(small_all, small_shapes, small_axes)
    def full_shapes(spec, shards):
        return [jax.ShapeDtypeStruct(_full_shape(axis, s.shape), s.dtype) for s, (_, axis) in zip(shards, spec)]

    stages, counts = _gather_stages(MIXW)
    mix_shards = [wts[n].astype(BF16) for n, _ in MIXW]
    w_in_cm, w_out, w_pool = _comm_fused(stages, counts, mix_shards, full_shapes(MIXW, mix_shards), "gather_mixers")
    w_in = jnp.swapaxes(w_in_cm, 1, 2).reshape(n_ssd, d, -1)
    stages, counts = _gather_stages(FFNW)
    ffn_shards = [wts[n].astype(BF16) for n, _ in FFNW]
    ffn_gather = _SplitComm(stages, counts, ffn_shards + [w_pool],
                            [lax.empty(s.shape, s.dtype) for s in full_shapes(FFNW, ffn_shards)], "gather_ffn")
    gather_token = ffn_gather.advance()

    def pad_heads(a):
        lead = a.shape[:-1]
        a = a.reshape(lead + (N_GROUPS, hpg))
        a = jnp.pad(a, [(0, 0)] * len(lead) + [(0, 0), (0, LANES - hpg)])
        return a.reshape(lead + (N_GROUPS * LANES,))

    def unpad_heads(a):
        lead = a.shape[:-1]
        return a.reshape(lead + (N_GROUPS, LANES))[..., :hpg].reshape(lead + (nheads,))

    def group_rows(a, width):
        return jnp.broadcast_to(a.reshape(N_GROUPS, 1, width), (N_GROUPS, 8, width))

    w_in_p = jnp.concatenate([w_in[..., :d_inner + xbc], pad_heads(w_in[..., d_inner + xbc:])], axis=-1)
    zw = w_in_p.shape[-1]

    x2 = x.reshape(t, d)
    tgt2 = loss_target.reshape(t, d)
    w_up = w_down = None

    saved = []
    cur = x2
    tokens = []
    h = _norm_fwd(cur, norm_mix_pre[0:1], BF16, "norm_pre_b", after=[gather_token])
    for i in range(depth):
        j = i // 2
        sv = dict(x_in=cur)
        if i % 2 == 0:
            zx = _mm(h, w_in_p, "nn", F32, "mm_ssd_in", 2048, 512, d, b_layer=j).reshape(bl, seq, zw)
            xc, xpre = _ssd_conv_fwd(zx, conv_w[j], ssd_conv_b[j:j + 1], d_inner, "ssd_conv_fwd")
            dtb = group_rows(pad_heads(ssd_dt_bias[j]), LANES)
            alog = group_rows(pad_heads(ssd_a_log[j]), LANES)
            dskip = group_rows(jnp.repeat(ssd_d[j], HEAD_DIM), gw)
            nw = group_rows(ssd_norm_w[j], gw)
            y, yn, st = _ssd_fwd(xc, zx, dtb, alog, dskip, nw, d_inner, "ssd_fwd")
            if i == 0:
                tokens.append(ffn_gather.advance(after=yn))
            mix = _mm(yn.reshape(t, d_inner), w_out, "nn", F32, "mm_ssd_out", 512, 512, d_inner, b_layer=j)
            sv.update(h=h, zx=zx, xc=xc, xpre=xpre, y=y, yn=yn, st=st, dtb=dtb, alog=alog, dskip=dskip, nw=nw)
        else:
            mix = _pool_fwd(h.reshape(bl, seq, d), w_pool[j], p_scale[j:j + 1], "pool_fwd").reshape(t, d)
            sv.update(h=h)
        sv.update(mix=mix)
        mid, u = _norm_post_pre(mix, norm_mix_post[i:i + 1], cur, norm_ffn_pre[i:i + 1], BF16, "norm_post_pre_b",
                                after=tokens)
        tokens = []
        if i == 0:
            ffn_gather.advance(after=u)
            w_up, w_down = ffn_gather.lands()
        hpre = _mm(u, w_up, "nn", BF16, "mm_up", 2048, 512, d, b_layer=i).reshape(bl, seq, f2)
        act, pre_g, pre_v = _ffn_act_fwd(hpre, f_conv_w[i], ffn_conv_b[i:i + 1], "ffn_act_fwd")
        act = act.reshape(t, ff)
        fo = _mm(act, w_down, "nn", F32, "mm_down", 1024, 512, ff, b_layer=i)
        if i + 1 == depth:
            cur = _norm_fwd(fo, norm_ffn_post[i:i + 1], F32, "norm_post", resid=mid)
        elif i % 2 == 0:
            cur, h = _norm_post_pre(fo, norm_ffn_post[i:i + 1], mid, norm_mix_pre[i + 1:i + 2], F32, "norm_post_pre_f")
        else:
            cur, h = _norm_post_pre(fo, norm_ffn_post[i:i + 1], mid, norm_mix_pre[i + 1:i + 2], BF16, "norm_post_pre_b")
        sv.update(mid=mid, u=u, hpre=hpre, pre_g=pre_g, pre_v=pre_v, act=act, fo=fo)
        saved.append(sv)

    dcur, loss_part = _loss_head(cur, tgt2, "loss_head")

    g = {n: [None] * wts[n].shape[0] for n in WEIGHTS}
    gbuf = dict(up=lax.empty((depth, d, f2), F32), down=lax.empty((depth, ff, d), F32),
                out=lax.empty((n_ssd, d_inner, d), F32), win=lax.empty((n_ssd, d, zw), F32))
    core = cy.reshape(1).astype(jnp.int32)

    def mixer_bwd(i, dmid, behind=()):
        j = i // 2
        sv = saved[i]
        done = []
        if i % 2 == 0:
            dmix, g["norm_mix_post"][i] = _norm_bwd(sv["mix"], norm_mix_post[i:i + 1], dmid, BF16, "norm_bwd_b",
                                                    after=behind)
            dyn = _mm(dmix, w_out, "nt", F32, "mm_ssd_out_dx", 1024, 1024, d, b_layer=j)
            gbuf["out"], tok = _mm(sv["yn"].reshape(t, d_inner), dmix, "tn", F32, "mm_ssd_out_dw", 1024, 512, 2048,
                                   out_buf=(gbuf["out"], j))
            done.append(tok)
            dz, dxs, dbm, dcm, ddt, dnw, dd, dal, dbias = _ssd_bwd(
                sv["xc"], sv["zx"], sv["y"], dyn.reshape(bl, seq, d_inner), sv["st"], sv["dtb"], sv["alog"],
                sv["dskip"], sv["nw"], d_inner, "ssd_bwd")
            g["ssd_norm_w"][j] = dnw[:, 0, :].reshape(d_inner)
            g["ssd_d"][j] = dd[:, 0, :hpg].reshape(nheads)
            g["ssd_a_log"][j] = dal[:, 0, :hpg].reshape(nheads)
            g["ssd_dt_bias"][j] = dbias[:, 0, :hpg].reshape(nheads)
            dxbc, dcw, dcb = _ssd_conv_bwd(sv["zx"], sv["xpre"], (dxs, dbm, dcm), conv_w[j], d_inner, "ssd_conv_bwd")
            g["ssd_conv_w"][j] = dcw
            g["ssd_conv_b"][j] = dcb[0]
            dzs = [dz.reshape(t, d_inner), dxbc.reshape(t, xbc), ddt.reshape(t, N_GROUPS * LANES)]
            dh = _mm(dzs, w_in_p, "nt", F32, "mm_ssd_in_dx", 1024, d, 512, b_layer=j)
            gbuf["win"], tok = _mm(sv["h"], dzs, "tn", F32, "mm_ssd_in_dw", 1024, 512, 2048, out_buf=(gbuf["win"], j))
            done.append(tok)
        else:
            dmix, g["norm_mix_post"][i] = _norm_bwd(sv["mix"], norm_mix_post[i:i + 1], dmid, F32, "norm_bwd_f",
                                                    after=behind)
            dh3, g["pool_w"][j], dps = _pool_bwd(sv["h"].reshape(bl, seq, d), dmix.reshape(bl, seq, d), w_pool[j],
                                                 p_scale[j:j + 1], "pool_bwd")
            g["pool_scale"][j] = dps[0]
            dh = dh3.reshape(t, d)
        dx_in, g["norm_mix_pre"][i] = _norm_bwd(sv["x_in"], norm_mix_pre[i:i + 1], dh, F32, "norm_bwd_r", resid=dmid,
                                                after=done)
        return dx_in

    ffn_comm = None
    for i in reversed(range(depth)):
        sv = saved[i]
        dfo, g["norm_ffn_post"][i] = _norm_bwd(sv["fo"], norm_ffn_post[i:i + 1], dcur, BF16, "norm_bwd_b")
        dact = _mm(dfo, w_down, "nt", BF16, "mm_down_dx", 1024, ff // 2, d, b_layer=i)
        gbuf["down"], tok_down = _mm(sv["act"], dfo, "tn", F32, "mm_down_dw", ff // 2, 512, 2048,
                                     out_buf=(gbuf["down"], i))
        dhg, dhv, dcw, dcb = _ffn_act_bwd(sv["hpre"], sv["pre_g"], sv["pre_v"], dact.reshape(bl, seq, ff), f_conv_w[i],
                                          "ffn_act_bwd")
        g["ffn_conv_w"][i] = dcw
        g["ffn_conv_b"][i] = dcb[0]
        dhs = [dhg.reshape(t, ff), dhv.reshape(t, ff)]
        du = _mm(dhs, w_up, "nt", F32, "mm_up_dx", 1024, d, ff // 2, b_layer=i)
        gbuf["up"], tok_up = _mm(sv["u"], dhs, "tn", F32, "mm_up_dw", 512, ff // 2, 2048, out_buf=(gbuf["up"], i))
        dmid, g["norm_ffn_pre"][i] = _norm_bwd(sv["mid"], norm_ffn_pre[i:i + 1], du, F32, "norm_bwd_r", resid=dcur,
                                               after=[tok_down, tok_up])
        if i > 0:
            dcur = mixer_bwd(i, dmid)
        else:
            ffn_comm, ffn_token = _reduce_begin(FFNW, [gbuf["up"], gbuf["down"]], core, "ffn")
            dcur = mixer_bwd(0, dmid, behind=[ffn_token])

    grad_x = dcur.reshape(bl, seq, d)
    for n in ("norm_mix_pre", "norm_mix_post", "norm_ffn_pre", "norm_ffn_post"):
        g[n] = [a[0] for a in g[n]]
    small_names = [n for n, _ in SMALL] + list(REPL)
    full = {n: jnp.stack(g[n], axis=0) for n in small_names}

    g_in = jnp.concatenate([gbuf["win"][..., :d_inner + xbc], unpad_heads(gbuf["win"][..., d_inner + xbc:])], axis=-1)
    g_in_cm = jnp.swapaxes(g_in.reshape(n_ssd, d, N_CHIPS, -1), 1, 2)
    mix_comm, mix_token = _reduce_begin(MIXW, [g_in_cm, gbuf["out"], jnp.stack(g["pool_w"], axis=0)], core, "mixers")

    grads, deltas, new_m, new_v = {}, {}, {}, {}

    def adamw(n, gr):
        shp = wts[n].shape
        two = (math.prod(shp[:-1]), shp[-1])
        dl, mn, vn = _adamw(wts[n].reshape(two), gr.reshape(two), mom[n].reshape(two), var[n].reshape(two),
                            "adamw_" + n)
        grads[n], deltas[n], new_m[n], new_v[n] = gr, dl.reshape(shp), mn.reshape(shp), vn.reshape(shp)
        return dl

    vec = jnp.concatenate([full[n].reshape(-1) for n in small_names] + [loss_part[0, :1]])
    nvec = vec.shape[0]
    vrows = 8 * ((nvec + 8 * FLAT_COLS - 1) // (8 * FLAT_COLS))
    vec = jnp.pad(vec, (0, vrows * FLAT_COLS - nvec)).reshape(vrows, FLAT_COLS)
    tot2 = _allreduce_small(vec, "allreduce_small", after=[mix_token])
    tot = tot2.reshape(-1)
    ffn_grads = _reduce_finish(FFNW, ffn_comm, core, "ffn", after=tot2)
    for gr, (n, _) in zip(ffn_grads, FFNW):
        last = adamw(n, gr)
    mix_grads = _reduce_finish(MIXW, mix_comm, core, "mixers", after=last)
    for gr, (n, _) in zip(mix_grads, MIXW):
        adamw(n, gr)
    small_grads, off = {}, 0
    for n in small_names:
        cnt = math.prod(full[n].shape)
        small_grads[n] = tot[off:off + cnt].reshape(full[n].shape)
        off += cnt
    loss = tot[off]
    for n, ax in SMALL:
        w = wts[n].shape[ax]
        small_grads[n] = lax.dynamic_slice_in_dim(small_grads[n], chip * w, w, axis=ax)

    for n in small_names:
        adamw(n, small_grads[n])

    return (loss, grad_x, *[grads[n] for n in WEIGHTS], *[deltas[n] for n in WEIGHTS],
            *[new_m[n] for n in WEIGHTS], *[new_v[n] for n in WEIGHTS])
```

```python
import functools
import math

import jax
import jax.numpy as jnp
from jax import lax
from jax.experimental import pallas as pl
from jax.experimental.pallas import tpu as pltpu

F32 = jnp.float32
BF16 = jnp.bfloat16
MESH = pl.DeviceIdType.MESH
ANY = pl.BlockSpec(memory_space=pl.ANY)

HEAD_DIM = 64
D_STATE = 128
CHUNK = 128
N_GROUPS = 4
SSD_CONV = 4
FFN_CONV = 3
EPS = 1e-6
N_CHIPS = 4
LANES = 128
FLAT_COLS = 1024

ADAM_LR = 0.001
ADAM_B1 = 0.9
ADAM_B2 = 0.999
ADAM_EPS = 1e-08
ADAM_WD = 0.01
ADAM_STEP = 10

VMEM_LIMIT_BYTES = 56 * 1024 * 1024


def _params(sem=None):
    kw = dict(vmem_limit_bytes=VMEM_LIMIT_BYTES)
    if sem is not None:
        kw["dimension_semantics"] = sem
    return pltpu.CompilerParams(**kw)


def _sigmoid(x):
    return 1.0 / (1.0 + jnp.exp(-x))


def _softplus(x):
    return jnp.maximum(x, 0.0) + jnp.log(1.0 + jnp.exp(-jnp.abs(x)))


def _dot(a, b, dn):
    return lax.dot_general(a, b, (dn, ((), ())), preferred_element_type=F32)


def _nn(a, b):
    return _dot(a, b, ((1,), (0,)))


def _nt(a, b):
    return _dot(a, b, ((1,), (1,)))


def _tn(a, b):
    return _dot(a, b, ((0,), (0,)))


def _split(x, parts):
    out = []
    r = x
    for _ in range(parts):
        p = r.astype(BF16)
        out.append(p)
        r = r - p.astype(F32)
    return out


def _sel_left(sel, x, parts=3):
    n = x.shape[1]
    r = _nn(sel, jnp.concatenate(_split(x, parts), axis=1))
    out = r[:, 0:n]
    for i in range(1, parts):
        out = out + r[:, i * n:(i + 1) * n]
    return out


def _sel_right(x, sel_stacked, parts=3):
    return _nn(jnp.concatenate(_split(x, parts), axis=1), sel_stacked)


def _mm(a, b, dims, out_dtype, name, tm, tn, tk, b_layer=None, out_buf=None):
    a_list = list(a) if isinstance(a, (list, tuple)) else [a]
    b_list = list(b) if isinstance(b, (list, tuple)) else [b]
    if dims in ("nn", "nt"):
        assert len(b_list) == 1
        m = a_list[0].shape[0]
        segs = [x.shape[1] for x in a_list]
        k = sum(segs)
        bshape = b_list[0].shape[-2:]
        n = bshape[1] if dims == "nn" else bshape[0]
        assert (bshape[0] if dims == "nn" else bshape[1]) == k
    else:
        assert len(a_list) == 1 and b_layer is None
        k, m = a_list[0].shape
        segs = [x.shape[1] for x in b_list]
        n = sum(segs)
    tm, tn, tk = min(tm, m), min(tn, n), min(tk, k)
    if dims == "tn":
        tn = min(tn, min(segs))
    else:
        tk = min(tk, min(segs))
    unit = tk if dims != "tn" else tn
    assert m % tm == 0 and n % tn == 0 and k % tk == 0 and all(s % unit == 0 for s in segs), (name, m, n, k, segs)
    nk = k // tk
    starts = [sum(segs[:s]) // unit for s in range(len(segs))]
    counts = [s // unit for s in segs]
    nseg = len(segs)
    dn = {"nn": ((1,), (0,)), "nt": ((1,), (1,)), "tn": ((0,), (0,))}[dims]

    def body(*refs):
        a_refs = refs[:len(a_list)]
        b_refs = refs[len(a_list):len(a_list) + len(b_list)]
        rest = refs[len(a_list) + len(b_list) + (0 if out_buf is None else 1):]
        o_ref = rest[0]
        if out_buf is not None:
            rest[1][...] = jnp.zeros((8, LANES), F32)
            rest = rest[1:]
        acc = rest[1] if nk > 1 else None
        kk = pl.program_id(2)
        sel = kk if dims != "tn" else pl.program_id(1)

        def step(a_ref, b_ref):
            p = _dot(a_ref[...].astype(BF16), b_ref[...].astype(BF16), dn)
            if nk == 1:
                o_ref[...] = p.astype(out_dtype)
                return

            @pl.when(kk == 0)
            def _():
                acc[...] = p

            @pl.when(kk > 0)
            def _():
                acc[...] += p

        if nseg == 1:
            step(a_refs[0], b_refs[0])
        else:
            for s in range(nseg):
                @pl.when(jnp.logical_and(sel >= starts[s], sel < starts[s] + counts[s]))
                def _(s=s):
                    step(a_refs[s] if dims != "tn" else a_refs[0], b_refs[0] if dims != "tn" else b_refs[s])

        if nk > 1:
            @pl.when(kk == nk - 1)
            def _():
                o_ref[...] = acc[...].astype(out_dtype)

    def seg_index(v, s):
        return v if nseg == 1 else jnp.clip(v - starts[s], 0, counts[s] - 1)

    lead = () if b_layer is None else (b_layer,)
    none = () if b_layer is None else (None,)
    if dims == "nn":
        a_specs = [pl.BlockSpec((tm, tk), lambda i, j, kk, s=s: (i, seg_index(kk, s))) for s in range(nseg)]
        b_specs = [pl.BlockSpec(none + (tk, tn), lambda i, j, kk: lead + (kk, j))]
    elif dims == "nt":
        a_specs = [pl.BlockSpec((tm, tk), lambda i, j, kk, s=s: (i, seg_index(kk, s))) for s in range(nseg)]
        b_specs = [pl.BlockSpec(none + (tn, tk), lambda i, j, kk: lead + (j, kk))]
    else:
        a_specs = [pl.BlockSpec((tk, tm), lambda i, j, kk: (kk, i))]
        b_specs = [pl.BlockSpec((tk, tn), lambda i, j, kk, s=s: (kk, seg_index(j, s))) for s in range(nseg)]
    args = a_list + b_list
    in_specs = a_specs + b_specs
    aliases = {}
    if out_buf is None:
        out_shape = jax.ShapeDtypeStruct((m, n), out_dtype)
        out_spec = pl.BlockSpec((tm, tn), lambda i, j, kk: (i, j))
    else:
        buf, slab = out_buf
        assert buf.shape[1:] == (m, n) and buf.dtype == out_dtype
        out_shape = (jax.ShapeDtypeStruct(buf.shape, out_dtype), jax.ShapeDtypeStruct((8, LANES), F32))
        out_spec = (pl.BlockSpec((None, tm, tn), lambda i, j, kk: (slab, i, j)),
                    pl.BlockSpec((8, LANES), lambda i, j, kk: (0, 0)))
        aliases = {len(args): 0}
        args = args + [buf]
        in_specs = in_specs + [ANY]
    return pl.pallas_call(
        body,
        out_shape=out_shape,
        grid=(m // tm, n // tn, nk),
        in_specs=in_specs,
        out_specs=out_spec,
        scratch_shapes=[] if nk == 1 else [pltpu.VMEM((tm, tn), F32)],
        input_output_aliases=aliases,
        compiler_params=_params(("parallel", "parallel", "arbitrary") if out_buf is None else ("arbitrary",) * 3),
        name=name,
    )(*args)


def _row_tile(t, want):
    tm = min(want, t)
    assert t % tm == 0
    return tm


def _norm_fwd(x, w, out_dtype, name, resid=None, after=()):
    t, d = x.shape
    tm = _row_tile(t, 512)
    after = [a for a in after if a is not None]

    def body(*refs):
        refs = refs[:len(refs) - 1 - len(after)] + refs[len(refs) - 1:]
        if resid is None:
            x_ref, w_ref, o_ref = refs
        else:
            x_ref, w_ref, r_ref, o_ref = refs
        xv = x_ref[...]
        r = lax.rsqrt(jnp.mean(xv * xv, axis=-1, keepdims=True) + EPS)
        y = (xv * r) * w_ref[...]
        if resid is not None:
            y = r_ref[...] + y
        o_ref[...] = y.astype(out_dtype)

    row = pl.BlockSpec((tm, d), lambda i: (i, 0))
    vec = pl.BlockSpec((1, d), lambda i: (0, 0))
    args = [x, w] + ([] if resid is None else [resid]) + after
    return pl.pallas_call(
        body, out_shape=jax.ShapeDtypeStruct((t, d), out_dtype), grid=(t // tm,),
        in_specs=[row, vec] + ([] if resid is None else [row]) + [ANY] * len(after), out_specs=row,
        compiler_params=_params(("parallel",)), name=name)(*args)


def _norm_post_pre(m, w_post, resid, w_pre, pre_dtype, name, after=()):
    t, d = m.shape
    tm = _row_tile(t, 512)
    after = [a for a in after if a is not None]

    def body(m_ref, w1_ref, r_ref, w2_ref, *rest):
        x_ref, u_ref = rest[len(after):]
        mv = m_ref[...]
        r1 = lax.rsqrt(jnp.mean(mv * mv, axis=-1, keepdims=True) + EPS)
        xv = r_ref[...] + (mv * r1) * w1_ref[...]
        x_ref[...] = xv
        r2 = lax.rsqrt(jnp.mean(xv * xv, axis=-1, keepdims=True) + EPS)
        u_ref[...] = ((xv * r2) * w2_ref[...]).astype(pre_dtype)

    row = pl.BlockSpec((tm, d), lambda i: (i, 0))
    vec = pl.BlockSpec((1, d), lambda i: (0, 0))
    return pl.pallas_call(
        body, out_shape=(jax.ShapeDtypeStruct((t, d), F32), jax.ShapeDtypeStruct((t, d), pre_dtype)), grid=(t // tm,),
        in_specs=[row, vec, row, vec] + [ANY] * len(after), out_specs=(row, row),
        compiler_params=_params(("parallel",)), name=name)(m, w_post, resid, w_pre, *after)


def _norm_bwd(src, w, dy, out_dtype, name, resid=None, after=()):
    t, d = src.shape
    tm = _row_tile(t, 512)
    after = [a for a in after if a is not None]

    def body(*refs):
        refs = refs[:len(refs) - 2 - len(after)] + refs[len(refs) - 2:]
        if resid is None:
            x_ref, w_ref, g_ref, o_ref, dw_ref = refs
        else:
            x_ref, w_ref, g_ref, r_ref, o_ref, dw_ref = refs
        xv = x_ref[...]
        g = g_ref[...].astype(F32)
        r = lax.rsqrt(jnp.mean(xv * xv, axis=-1, keepdims=True) + EPS)
        xh = xv * r
        gh = g * w_ref[...]
        mean = jnp.mean(gh * xh, axis=-1, keepdims=True)
        dx = r * (gh - xh * mean)
        if resid is not None:
            dx = r_ref[...] + dx
        o_ref[...] = dx.astype(out_dtype)
        part = jnp.sum(g * xh, axis=0, keepdims=True)

        @pl.when(pl.program_id(0) == 0)
        def _():
            dw_ref[...] = part

        @pl.when(pl.program_id(0) > 0)
        def _():
            dw_ref[...] += part

    row = pl.BlockSpec((tm, d), lambda i: (i, 0))
    vec = pl.BlockSpec((1, d), lambda i: (0, 0))
    args = [src, w, dy] + ([] if resid is None else [resid]) + after
    return pl.pallas_call(
        body,
        out_shape=(jax.ShapeDtypeStruct((t, d), out_dtype), jax.ShapeDtypeStruct((1, d), F32)),
        grid=(t // tm,),
        in_specs=[row, vec, row] + ([] if resid is None else [row]) + [ANY] * len(after),
        out_specs=(row, vec),
        compiler_params=_params(("arbitrary",)), name=name)(*args)


def _loss_head(y, target, name):
    t, d = y.shape
    tm = _row_tile(t, 512)

    def body(y_ref, t_ref, dy_ref, l_ref):
        e = y_ref[...] - t_ref[...]
        dy_ref[...] = e * (1.0 / d)
        col = jnp.sum(e * e, axis=0, keepdims=True)
        s = jnp.sum(col, axis=1, keepdims=True) * (0.5 / d)
        part = jnp.broadcast_to(s, (1, LANES))

        @pl.when(pl.program_id(0) == 0)
        def _():
            l_ref[...] = part

        @pl.when(pl.program_id(0) > 0)
        def _():
            l_ref[...] += part

    row = pl.BlockSpec((tm, d), lambda i: (i, 0))
    return pl.pallas_call(
        body,
        out_shape=(jax.ShapeDtypeStruct((t, d), F32), jax.ShapeDtypeStruct((1, LANES), F32)),
        grid=(t // tm,), in_specs=[row, row],
        out_specs=(row, pl.BlockSpec((1, LANES), lambda i: (0, 0))),
        compiler_params=_params(("arbitrary",)), name=name)(y, target)


def _window(ref, c, rows, seq, before, after):
    r0 = pl.multiple_of(c * rows, rows)
    parts = []
    if before:
        h0 = pl.multiple_of(jnp.maximum(r0 - before, 0), before)
        halo = ref[pl.ds(h0, before), :].astype(F32)
        parts.append(jnp.where(c > 0, halo, 0.0))
    parts.append(ref[pl.ds(r0, rows), :].astype(F32))
    if after:
        h1 = pl.multiple_of(jnp.minimum(r0 + rows, seq - after), after)
        halo = ref[pl.ds(h1, after), :].astype(F32)
        parts.append(jnp.where(c < seq // rows - 1, halo, 0.0))
    return parts[0] if len(parts) == 1 else jnp.concatenate(parts, axis=0)


def _lag(x, k):
    return pltpu.roll(x, k, 0) if k else x


def _lead(x, k):
    return pltpu.roll(x, x.shape[0] - k, 0) if k else x


SHIFT_ROWS = 128
SHIFT_COLS = 256


HALO = 16


def _conv3(ext, w, bias):
    acc = bias + w[2:3, :] * ext[HALO:, :]
    acc = acc + w[1:2, :] * _lag(ext, 1)[HALO:, :]
    return acc + w[0:1, :] * _lag(ext, 2)[HALO:, :]


def _ffn_act_fwd(hpre, cw, cb, name):
    b, seq, f2 = hpre.shape
    cbk = SHIFT_COLS
    nj = f2 // (2 * cbk)
    rows = min(SHIFT_ROWS, seq)

    def body(g_ref, v_ref, wg_ref, wv_ref, bg_ref, bv_ref, o_ref, pg_ref, pv_ref):
        def chunk(c, carry):
            gate = _conv3(_window(g_ref, c, rows, seq, HALO, 0), wg_ref[...], bg_ref[...])
            val = _conv3(_window(v_ref, c, rows, seq, HALO, 0), wv_ref[...], bv_ref[...])
            a = gate * _sigmoid(gate) * val
            here = pl.ds(pl.multiple_of(c * rows, rows), rows)
            o_ref[here, :] = a.astype(BF16)
            pg_ref[here, :] = gate.astype(BF16)
            pv_ref[here, :] = val.astype(BF16)
            return carry

        lax.fori_loop(0, seq // rows, chunk, 0)

    blk = lambda off: pl.BlockSpec((None, seq, cbk), lambda i, j: (i, 0, j + off))
    wsp = lambda r, off: pl.BlockSpec((r, cbk), lambda i, j: (0, j + off))
    half = jax.ShapeDtypeStruct((b, seq, f2 // 2), BF16)
    return pl.pallas_call(
        body, out_shape=(half, half, half), grid=(b, nj),
        in_specs=[blk(0), blk(nj), wsp(FFN_CONV, 0), wsp(FFN_CONV, nj), wsp(1, 0), wsp(1, nj)],
        out_specs=(blk(0), blk(0), blk(0)),
        compiler_params=_params(("parallel", "parallel")), name=name)(hpre, hpre, cw, cw, cb, cb)


def _ffn_act_bwd(hpre, pre_g, pre_v, da, cw, name):
    b, seq, f2 = hpre.shape
    cbk = SHIFT_COLS
    nj = f2 // (2 * cbk)
    rows = min(SHIFT_ROWS, seq)

    def body(g_ref, v_ref, pg_ref, pv_ref, da_ref, wg_ref, wv_ref, og_ref, ov_ref, dwg_ref, dwv_ref, dbg_ref, dbv_ref):
        wg, wv = wg_ref[...], wv_ref[...]

        def back(dpre, w, o_ref, x_ref, c, carry):
            here = pl.ds(pl.multiple_of(c * rows, rows), rows)
            leads = [dpre, _lead(dpre, 1), _lead(dpre, 2)]
            dx = w[2:3, :] * leads[0] + w[1:2, :] * leads[1] + w[0:1, :] * leads[2]
            o_ref[here, :] = dx[:rows, :].astype(BF16)
            x0 = x_ref[here, :].astype(F32)
            return tuple(carry[k] + jnp.sum(leads[k][:rows, :] * x0, axis=0, keepdims=True) for k in range(FFN_CONV)) + (
                carry[FFN_CONV] + jnp.sum(dpre[:rows, :], axis=0, keepdims=True),)

        def chunk(c, carry):
            cg, cv = carry
            gate = _window(pg_ref, c, rows, seq, 0, HALO)
            val = _window(pv_ref, c, rows, seq, 0, HALO)
            dav = _window(da_ref, c, rows, seq, 0, HALO)
            sg = _sigmoid(gate)
            cg = back(dav * val * (sg * (1.0 + gate * (1.0 - sg))), wg, og_ref, g_ref, c, cg)
            cv = back(dav * (gate * sg), wv, ov_ref, v_ref, c, cv)
            return cg, cv

        z = jnp.zeros((1, cbk), F32)
        cg, cv = lax.fori_loop(0, seq // rows, chunk, ((z,) * (FFN_CONV + 1), (z,) * (FFN_CONV + 1)))
        dwg = jnp.concatenate([cg[2], cg[1], cg[0]], axis=0)
        dwv = jnp.concatenate([cv[2], cv[1], cv[0]], axis=0)

        @pl.when(pl.program_id(1) == 0)
        def _():
            dwg_ref[...] = dwg
            dwv_ref[...] = dwv
            dbg_ref[...] = cg[FFN_CONV]
            dbv_ref[...] = cv[FFN_CONV]

        @pl.when(pl.program_id(1) > 0)
        def _():
            dwg_ref[...] += dwg
            dwv_ref[...] += dwv
            dbg_ref[...] += cg[FFN_CONV]
            dbv_ref[...] += cv[FFN_CONV]

    blk = lambda off: pl.BlockSpec((None, seq, cbk), lambda j, i: (i, 0, j + off))
    wsp = lambda r, off: pl.BlockSpec((r, cbk), lambda j, i: (0, j + off))
    half = jax.ShapeDtypeStruct((b, seq, f2 // 2), BF16)
    dwshape = jax.ShapeDtypeStruct((FFN_CONV, f2 // 2), F32)
    dbshape = jax.ShapeDtypeStruct((1, f2 // 2), F32)
    dg, dv, dwg, dwv, dbg, dbv = pl.pallas_call(
        body,
        out_shape=(half, half, dwshape, dwshape, dbshape, dbshape),
        grid=(nj, b),
        in_specs=[blk(0), blk(nj), blk(0), blk(0), blk(0), wsp(FFN_CONV, 0), wsp(FFN_CONV, nj)],
        out_specs=(blk(0), blk(0), wsp(FFN_CONV, 0), wsp(FFN_CONV, 0), wsp(1, 0), wsp(1, 0)),
        compiler_params=_params(("parallel", "arbitrary")), name=name)(hpre, hpre, pre_g, pre_v, da, cw, cw)
    return dg, dv, jnp.concatenate([dwg, dwv], axis=1), jnp.concatenate([dbg, dbv], axis=1)


def _ssd_conv_fwd(zx, cw, cb, d_inner, name):
    b, seq, _ = zx.shape
    xbc = cw.shape[1]
    cbk = SHIFT_COLS
    off = d_inner // cbk
    rows = min(SHIFT_ROWS, seq)

    def body(h_ref, w_ref, b_ref, o_ref, p_ref):
        w = w_ref[...]
        bias = b_ref[...]

        def chunk(c, carry):
            ext = _window(h_ref, c, rows, seq, 8, 0)
            acc = bias + w[3:4, :] * ext[8:, :]
            for k in range(1, SSD_CONV):
                acc = acc + w[3 - k:4 - k, :] * _lag(ext, k)[8:, :]
            here = pl.ds(pl.multiple_of(c * rows, rows), rows)
            o_ref[here, :] = acc * _sigmoid(acc)
            p_ref[here, :] = acc.astype(BF16)
            return carry

        lax.fori_loop(0, seq // rows, chunk, 0)

    blk = pl.BlockSpec((None, seq, cbk), lambda i, j: (i, 0, j))
    return pl.pallas_call(
        body, out_shape=(jax.ShapeDtypeStruct((b, seq, xbc), F32), jax.ShapeDtypeStruct((b, seq, xbc), BF16)),
        grid=(b, xbc // cbk),
        in_specs=[pl.BlockSpec((None, seq, cbk), lambda i, j: (i, 0, j + off)),
                  pl.BlockSpec((SSD_CONV, cbk), lambda i, j: (0, j)),
                  pl.BlockSpec((1, cbk), lambda i, j: (0, j))],
        out_specs=(blk, blk),
        compiler_params=_params(("parallel", "parallel")), name=name)(zx, cw, cb)


def _ssd_conv_bwd(zx, pre, dparts, cw, d_inner, name):
    b, seq, _ = zx.shape
    xbc = cw.shape[1]
    cbk = SHIFT_COLS
    off = d_inner // cbk
    rows = min(SHIFT_ROWS, seq)
    nblk = [p.shape[2] // cbk for p in dparts]
    first = [sum(nblk[:s]) for s in range(len(dparts))]
    assert sum(nblk) == xbc // cbk

    def body(h_ref, p_ref, gx_ref, gb_ref, gc_ref, w_ref, o_ref, dw_ref, db_ref):
        w = w_ref[...]
        j = pl.program_id(0)

        def chunk(c, carry):
            dws, dbias = carry
            here = pl.ds(pl.multiple_of(c * rows, rows), rows)
            pre = _window(p_ref, c, rows, seq, 0, HALO)
            s = _sigmoid(pre)
            gsel = jnp.where(j < first[1], _window(gx_ref, c, rows, seq, 0, HALO),
                             jnp.where(j < first[2], _window(gb_ref, c, rows, seq, 0, HALO),
                                       _window(gc_ref, c, rows, seq, 0, HALO)))
            dpre = gsel * (s * (1.0 + pre * (1.0 - s)))
            leads = [dpre] + [_lead(dpre, k) for k in range(1, SSD_CONV)]
            dx = w[3:4, :] * leads[0]
            for k in range(1, SSD_CONV):
                dx = dx + w[3 - k:4 - k, :] * leads[k]
            o_ref[here, :] = dx[:rows, :].astype(BF16)
            x0 = h_ref[here, :]
            dws = tuple(dws[k] + jnp.sum(leads[k][:rows, :] * x0, axis=0, keepdims=True) for k in range(SSD_CONV))
            dbias = dbias + jnp.sum(dpre[:rows, :], axis=0, keepdims=True)
            return dws, dbias

        z = jnp.zeros((1, cbk), F32)
        dws, dbias = lax.fori_loop(0, seq // rows, chunk, ((z,) * SSD_CONV, z))
        dwv = jnp.concatenate([dws[3 - i] for i in range(SSD_CONV)], axis=0)

        @pl.when(pl.program_id(1) == 0)
        def _():
            dw_ref[...] = dwv
            db_ref[...] = dbias

        @pl.when(pl.program_id(1) > 0)
        def _():
            dw_ref[...] += dwv
            db_ref[...] += dbias

    return pl.pallas_call(
        body,
        out_shape=(jax.ShapeDtypeStruct((b, seq, xbc), BF16), jax.ShapeDtypeStruct((SSD_CONV, xbc), F32),
                   jax.ShapeDtypeStruct((1, xbc), F32)),
        grid=(xbc // cbk, b),
        in_specs=[pl.BlockSpec((None, seq, cbk), lambda j, i: (i, 0, j + off)),
                  pl.BlockSpec((None, seq, cbk), lambda j, i: (i, 0, j))] + [
                  pl.BlockSpec((None, seq, cbk), lambda j, i, s=s: (i, 0, jnp.clip(j - first[s], 0, nblk[s] - 1)))
                  for s in range(3)] + [
                  pl.BlockSpec((SSD_CONV, cbk), lambda j, i: (0, j))],
        out_specs=(pl.BlockSpec((None, seq, cbk), lambda j, i: (i, 0, j)),
                   pl.BlockSpec((SSD_CONV, cbk), lambda j, i: (0, j)),
                   pl.BlockSpec((1, cbk), lambda j, i: (0, j))),
        compiler_params=_params(("parallel", "arbitrary")), name=name)(zx, pre, *dparts, cw)


def _pool_sums(q, g, lead):
    sh = _lead if lead else _lag
    s2 = q + sh(q, 1)
    s4 = s2 + sh(s2, 2)
    s8 = s4 + sh(s4, 4)
    s16 = s8 + sh(s8, 8)
    return jnp.where(g == 0, s2, jnp.where(g == 1, s4, jnp.where(g == 2, s8, s16)))


def _pool_count(r0, n, g, shape):
    t = (r0 + lax.broadcasted_iota(jnp.int32, shape, 0) + 1).astype(F32)
    return jnp.minimum(t, (2 << g).astype(F32))


def _pool_fwd(h, pw, scale, name):
    b, seq, d = h.shape
    dg = d // 4
    rows = min(SHIFT_ROWS, seq)

    def body(h_ref, w_ref, s_ref, o_ref):
        g = pl.program_id(1)
        wmat = w_ref[...]
        sc = s_ref[...]

        def chunk(c, carry):
            r0 = c * rows
            ext = _window(h_ref, c, rows, seq, 16, 0)
            sums = _pool_sums(ext, g, False)[16:, :]
            mixed = sums / _pool_count(r0, rows, g, (rows, dg)) - ext[16:, :]
            o_ref[pl.ds(pl.multiple_of(r0, rows), rows), :] = _nn(mixed.astype(BF16), wmat) * sc
            return carry

        lax.fori_loop(0, seq // rows, chunk, 0)

    return pl.pallas_call(
        body, out_shape=jax.ShapeDtypeStruct((b, seq, d), F32), grid=(b, 4),
        in_specs=[pl.BlockSpec((None, seq, dg), lambda i, g: (i, 0, g)),
                  pl.BlockSpec((None, dg, dg), lambda i, g: (g, 0, 0)),
                  pl.BlockSpec((1, dg), lambda i, g: (0, g))],
        out_specs=pl.BlockSpec((None, seq, dg), lambda i, g: (i, 0, g)),
        compiler_params=_params(("parallel", "parallel")), name=name)(h, pw, scale)


def _pool_bwd(h, dout, pw, scale, name):
    b, seq, d = h.shape
    dg = d // 4
    rows = min(SHIFT_ROWS, seq)

    def body(h_ref, g_ref, w_ref, s_ref, o_ref, dw_ref, ds_ref, dw_acc):
        g = pl.program_id(0)
        wmat = w_ref[...]
        sc = s_ref[...]
        dw_acc[...] = jnp.zeros_like(dw_acc)

        def chunk(c, dsc):
            r0 = c * rows
            ext = _window(h_ref, c, rows, seq, 16, 0)
            sums = _pool_sums(ext, g, False)[16:, :]
            mixed = (sums / _pool_count(r0, rows, g, (rows, dg)) - ext[16:, :]).astype(BF16)
            gext = _window(g_ref, c, rows, seq, 0, 16)
            dsc = dsc + jnp.sum(gext[:rows, :] * _nn(mixed, wmat), axis=0, keepdims=True)
            dpre = (gext * sc).astype(BF16)
            dw_acc[...] += _tn(mixed, dpre[:rows, :])
            dmix = _nt(dpre, wmat)
            q = dmix / _pool_count(r0, rows + 16, g, (rows + 16, dg))
            back = _pool_sums(q, g, True)
            o_ref[pl.ds(pl.multiple_of(r0, rows), rows), :] = back[:rows, :] - dmix[:rows, :]
            return dsc

        dsc = lax.fori_loop(0, seq // rows, chunk, jnp.zeros((1, dg), F32))

        @pl.when(pl.program_id(1) == 0)
        def _():
            dw_ref[...] = dw_acc[...]
            ds_ref[...] = dsc

        @pl.when(pl.program_id(1) > 0)
        def _():
            dw_ref[...] += dw_acc[...]
            ds_ref[...] += dsc

    return pl.pallas_call(
        body,
        out_shape=(jax.ShapeDtypeStruct((b, seq, d), F32), jax.ShapeDtypeStruct((4, dg, dg), F32),
                   jax.ShapeDtypeStruct((1, d), F32)),
        grid=(4, b),
        in_specs=[pl.BlockSpec((None, seq, dg), lambda g, i: (i, 0, g)),
                  pl.BlockSpec((None, seq, dg), lambda g, i: (i, 0, g)),
                  pl.BlockSpec((None, dg, dg), lambda g, i: (g, 0, 0)),
                  pl.BlockSpec((1, dg), lambda g, i: (0, g))],
        out_specs=(pl.BlockSpec((None, seq, dg), lambda g, i: (i, 0, g)),
                   pl.BlockSpec((None, dg, dg), lambda g, i: (g, 0, 0)),
                   pl.BlockSpec((1, dg), lambda g, i: (0, g))),
        scratch_shapes=[pltpu.VMEM((dg, dg), F32)],
        compiler_params=_params(("parallel", "arbitrary")), name=name)(h, dout, pw, scale)


def _head_of(channel):
    return jnp.right_shift(channel, HEAD_DIM.bit_length() - 1)


def _ssd_consts(gw):
    q = CHUNK
    row = lax.broadcasted_iota(jnp.int32, (q, q), 0)
    col = lax.broadcasted_iota(jnp.int32, (q, q), 1)
    tril = (row >= col).astype(BF16)
    triu = (row <= col).astype(BF16)
    e = (_head_of(lax.broadcasted_iota(jnp.int32, (LANES, gw), 1))
         == lax.broadcasted_iota(jnp.int32, (LANES, gw), 0)).astype(BF16)
    et = (_head_of(lax.broadcasted_iota(jnp.int32, (gw, LANES), 0))
          == lax.broadcasted_iota(jnp.int32, (gw, LANES), 1)).astype(BF16)
    return row, col, tril, triu, e, et


def _ssd_common(dtr, dtb, alog, gw):
    q = CHUNK
    row, col, tril, triu, e, et = _ssd_consts(gw)
    dt = _softplus(dtr + dtb)
    a_row = -jnp.exp(alog)
    acum = _sel_left(tril, dt * a_row)
    ac_last = jnp.sum(jnp.where(row == q - 1, acum, 0.0), axis=0, keepdims=True)
    eac = jnp.exp(acum)
    de = jnp.exp(ac_last - acum)
    e2 = jnp.concatenate([e, e], axis=0)
    expand = _sel_right(jnp.concatenate([dt, eac, de], axis=0), e2, 2)
    dt_x, eac_x, de_x = expand[0:q], expand[q:2 * q], expand[2 * q:3 * q]
    acum_t = acum.T
    cd_col = jnp.exp(acum_t[:, q - 1:q])
    et3 = jnp.concatenate([et, et, et], axis=1)
    cdmat = _nn(et3, jnp.concatenate(_split(jnp.broadcast_to(cd_col, (LANES, D_STATE)), 3), axis=0))
    consts = dict(row=row, col=col, tril=tril, triu=triu, e=e, et=et)
    return dt, a_row, acum, acum_t, ac_last, eac, de, dt_x, eac_x, de_x, cdmat, consts


def _decay(acum, acum_t, j, row, col):
    diff = acum[:, j:j + 1] - acum_t[j:j + 1, :]
    return jnp.exp(jnp.where(row >= col, diff, -1e30))


def _ssd_fwd(xc, zx, dtb, alog, dskip, nw, d_inner, name):
    b, seq, xbc = xc.shape
    q = CHUNK
    nc = seq // q
    gw = d_inner // N_GROUPS
    nh = gw // HEAD_DIM
    xb0 = d_inner // D_STATE
    xc0 = xb0 + N_GROUPS
    dt0 = (d_inner + xbc) // LANES

    nb = 2 if b % 2 == 0 else 1

    def body(x_ref, b_ref, c_ref, z_ref, dtr_ref, dtb_ref, al_ref, dsk_ref, nw_ref, y_ref, yn_ref, st_ref, s_ref):
        @pl.when(pl.program_id(2) == 0)
        def _():
            s_ref[...] = jnp.zeros_like(s_ref)

        for s in range(nb):
            one(s, x_ref.at[s], b_ref.at[s], c_ref.at[s], z_ref.at[s], dtr_ref.at[s], dtb_ref, al_ref, dsk_ref, nw_ref,
                y_ref.at[s], yn_ref.at[s], st_ref.at[s], s_ref.at[s])

    def one(s, x_ref, b_ref, c_ref, z_ref, dtr_ref, dtb_ref, al_ref, dsk_ref, nw_ref, y_ref, yn_ref, st_ref, s_ref):
        prev = s_ref[...]
        st_ref[...] = prev
        x = x_ref[...]
        bm = b_ref[...].astype(BF16)
        cm = c_ref[...].astype(BF16)
        (dt, a_row, acum, acum_t, ac_last, eac, de, dt_x, eac_x, de_x, cdmat, k) = _ssd_common(
            dtr_ref[...], dtb_ref[0:1, :], al_ref[0:1, :], gw)
        xdt = x * dt_x
        xdt_b = xdt.astype(BF16)
        cb = _nt(cm, bm)
        half = _head_of(lax.broadcasted_iota(jnp.int32, (q, LANES), 1))
        pairs = []
        for j in range(nh):
            pc = (j // 2) * LANES
            m = (cb * _decay(acum, acum_t, j, k["row"], k["col"])).astype(BF16)
            yj = jnp.where(half == j % 2, _nn(m, xdt_b[:, pc:pc + LANES]), 0.0)
            if j % 2 == 0:
                pairs.append(yj)
            else:
                pairs[-1] = pairs[-1] + yj
        prev_b = prev.astype(BF16)
        y = dsk_ref[0:1, :] * x + jnp.concatenate(pairs, axis=1) + eac_x * _nt(cm, prev_b)
        s_ref[...] = cdmat * prev + _tn((xdt * de_x).astype(BF16), bm)
        y_ref[...] = y
        z = z_ref[...]
        yg = y * (z * _sigmoid(z))
        r = lax.rsqrt(jnp.mean(yg * yg, axis=-1, keepdims=True) + EPS)
        yn_ref[...] = ((yg * r) * nw_ref[0:1, :]).astype(BF16)

    par = lambda w: pl.BlockSpec((None, 8, w), lambda i, g, c: (g, 0, 0))
    return pl.pallas_call(
        body,
        out_shape=(jax.ShapeDtypeStruct((b, seq, d_inner), F32), jax.ShapeDtypeStruct((b, seq, d_inner), BF16),
                   jax.ShapeDtypeStruct((b, nc, N_GROUPS, gw, D_STATE), F32)),
        grid=(b // nb, N_GROUPS, nc),
        in_specs=[pl.BlockSpec((nb, q, gw), lambda i, g, c: (i, c, g)),
                  pl.BlockSpec((nb, q, D_STATE), lambda i, g, c: (i, c, xb0 + g)),
                  pl.BlockSpec((nb, q, D_STATE), lambda i, g, c: (i, c, xc0 + g)),
                  pl.BlockSpec((nb, q, gw), lambda i, g, c: (i, c, g)),
                  pl.BlockSpec((nb, q, LANES), lambda i, g, c: (i, c, dt0 + g)),
                  par(LANES), par(LANES), par(gw), par(gw)],
        out_specs=(pl.BlockSpec((nb, q, gw), lambda i, g, c: (i, c, g)),
                   pl.BlockSpec((nb, q, gw), lambda i, g, c: (i, c, g)),
                   pl.BlockSpec((nb, None, None, gw, D_STATE), lambda i, g, c: (i, c, g, 0, 0))),
        scratch_shapes=[pltpu.VMEM((nb, gw, D_STATE), F32)],
        compiler_params=_params(("parallel", "parallel", "arbitrary")), name=name,
    )(xc, xc, xc, zx, zx, dtb, alog, dskip, nw)


def _ssd_bwd(xc, zx, y, dyn, st, dtb, alog, dskip, nw, d_inner, name):
    b, seq, xbc = xc.shape
    q = CHUNK
    nc = seq // q
    gw = d_inner // N_GROUPS
    nh = gw // HEAD_DIM
    xb0 = d_inner // D_STATE
    xc0 = xb0 + N_GROUPS
    dt0 = (d_inner + xbc) // LANES

    nb = 2 if b % 2 == 0 else 1

    def body(x_ref, b_ref, c_ref, z_ref, dtr_ref, y_ref, g_ref, st_ref, dtb_ref, al_ref, dsk_ref, nw_ref,
             dz_ref, dx_ref, db_ref, dc_ref, ddt_ref, dnw_ref, dd_ref, dal_ref, dbias_ref,
             ds_ref, colbuf, rowbuf):
        first = jnp.logical_and(pl.program_id(1) == 0, pl.program_id(2) == 0)

        @pl.when(pl.program_id(2) == 0)
        def _():
            ds_ref[...] = jnp.zeros_like(ds_ref)

        sums = [one(x_ref.at[s], b_ref.at[s], c_ref.at[s], z_ref.at[s], dtr_ref.at[s], y_ref.at[s], g_ref.at[s],
                    st_ref.at[s], dtb_ref, al_ref, dsk_ref, nw_ref, dz_ref.at[s], dx_ref.at[s], db_ref.at[s],
                    dc_ref.at[s], ddt_ref.at[s], ds_ref.at[s], colbuf.at[s], rowbuf.at[s]) for s in range(nb)]
        dnw, dd, dal, dbias = [sums[0][i] if nb == 1 else sums[0][i] + sums[1][i] for i in range(4)]

        @pl.when(first)
        def _():
            dnw_ref[...] = jnp.broadcast_to(dnw, (8, gw))
            dd_ref[...] = dd
            dal_ref[...] = jnp.broadcast_to(dal, (8, LANES))
            dbias_ref[...] = jnp.broadcast_to(dbias, (8, LANES))

        @pl.when(jnp.logical_not(first))
        def _():
            dnw_ref[...] += jnp.broadcast_to(dnw, (8, gw))
            dd_ref[...] += dd
            dal_ref[...] += jnp.broadcast_to(dal, (8, LANES))
            dbias_ref[...] += jnp.broadcast_to(dbias, (8, LANES))

    def one(x_ref, b_ref, c_ref, z_ref, dtr_ref, y_ref, g_ref, st_ref, dtb_ref, al_ref, dsk_ref, nw_ref,
            dz_ref, dx_ref, db_ref, dc_ref, ddt_ref, ds_ref, colbuf, rowbuf):
        x = x_ref[...]
        bm = b_ref[...].astype(BF16)
        cm = c_ref[...].astype(BF16)
        z = z_ref[...]
        y = y_ref[...]
        prev = st_ref[...]
        dtr = dtr_ref[...] + dtb_ref[0:1, :]
        (dt, a_row, acum, acum_t, ac_last, eac, de, dt_x, eac_x, de_x, cdmat, k) = _ssd_common(
            dtr_ref[...], dtb_ref[0:1, :], al_ref[0:1, :], gw)
        row, col = k["row"], k["col"]
        et2 = jnp.concatenate([k["et"], k["et"]], axis=0)

        sz = _sigmoid(z)
        silu_z = z * sz
        yg = y * silu_z
        r = lax.rsqrt(jnp.mean(yg * yg, axis=-1, keepdims=True) + EPS)
        xh = yg * r
        dyn = g_ref[...]
        gh = dyn * nw_ref[0:1, :]
        dyg = r * (gh - xh * jnp.mean(gh * xh, axis=-1, keepdims=True))
        dnw = jnp.sum(dyn * xh, axis=0, keepdims=True)
        g = dyg * silu_z
        dz_ref[...] = (dyg * y * (sz * (1.0 + z * (1.0 - sz)))).astype(BF16)
        dd = _sel_right(jnp.broadcast_to(jnp.sum(g * x, axis=0, keepdims=True), (8, gw)), et2, 2)

        xdt = x * dt_x
        xdt_b = xdt.astype(BF16)
        g_b = g.astype(BF16)
        prev_b = prev.astype(BF16)
        cb = _nt(cm, bm)

        cp = _nt(cm, prev_b)
        ge = g * eac_x
        dac = _sel_right(ge * cp, et2, 2)
        ge_b = ge.astype(BF16)
        dcm = _nn(ge_b, prev_b)
        dprev = _tn(ge_b, cm)

        colbuf[...] = jnp.zeros_like(colbuf)
        rowbuf[...] = jnp.zeros_like(rowbuf)
        dcb = jnp.zeros((q, q), F32)
        half = _head_of(lax.broadcasted_iota(jnp.int32, (q, LANES), 1))
        pairs = []
        for j in range(nh):
            pc = (j // 2) * LANES
            dec = _decay(acum, acum_t, j, row, col)
            m = cb * dec
            gj = jnp.where(half == j % 2, g[:, pc:pc + LANES], 0.0).astype(BF16)
            dm = _nt(gj, xdt_b[:, pc:pc + LANES])
            w = dm * m
            colbuf[:, j:j + 1] = jnp.sum(w, axis=1, keepdims=True)
            rowbuf[j:j + 1, :] = jnp.sum(w, axis=0, keepdims=True)
            dcb = dcb + dm * dec
            dj = jnp.where(half == j % 2, _tn(m.astype(BF16), g_b[:, pc:pc + LANES]), 0.0)
            if j % 2 == 0:
                pairs.append(dj)
            else:
                pairs[-1] = pairs[-1] + dj
        dxdt = jnp.concatenate(pairs, axis=1)
        dcb_b = dcb.astype(BF16)
        dcm = dcm + _nn(dcb_b, bm)
        dbm = _tn(dcb_b, cm)

        ds = ds_ref[...]
        ds_b = ds.astype(BF16)
        u = _nt(bm, ds_b)
        dxdt = dxdt + u * de_x
        dde = _sel_right(u * xdt, et2, 2)
        dbm = dbm + _nn((xdt * de_x).astype(BF16), ds_b)
        pm = jnp.concatenate(_split(ds * prev, 2), axis=1)
        t2 = _tn(pm, k["et"])
        dcd_row = jnp.sum(t2[0:D_STATE] + t2[D_STATE:2 * D_STATE], axis=0, keepdims=True)
        last = dcd_row * jnp.exp(ac_last) + jnp.sum(dde * de, axis=0, keepdims=True)
        dac = dac + colbuf[...] - rowbuf[...].T - dde * de + jnp.where(row == q - 1, last, 0.0)
        ds_ref[...] = cdmat * ds + dprev

        dadt = _sel_left(k["triu"], dac)
        ddt = _sel_right(dxdt * x, et2, 2) + dadt * a_row
        dal = jnp.sum(dadt * dt, axis=0, keepdims=True) * a_row
        lane = lax.broadcasted_iota(jnp.int32, (q, LANES), 1)
        ddtr = jnp.where(lane < nh, ddt * _sigmoid(dtr), 0.0)
        ddt_ref[...] = ddtr.astype(BF16)
        dbias = jnp.sum(ddtr, axis=0, keepdims=True)
        dx_ref[...] = dxdt * dt_x + dsk_ref[0:1, :] * g
        db_ref[...] = dbm
        dc_ref[...] = dcm
        return dnw, dd, dal, dbias

    rc = lambda c: nc - 1 - c
    par = lambda w: pl.BlockSpec((None, 8, w), lambda g, i, c: (g, 0, 0))
    blk = lambda w: pl.BlockSpec((nb, q, w), lambda g, i, c: (i, rc(c), g))
    return pl.pallas_call(
        body,
        out_shape=(jax.ShapeDtypeStruct((b, seq, d_inner), BF16),
                   jax.ShapeDtypeStruct((b, seq, d_inner), F32),
                   jax.ShapeDtypeStruct((b, seq, N_GROUPS * D_STATE), F32),
                   jax.ShapeDtypeStruct((b, seq, N_GROUPS * D_STATE), F32),
                   jax.ShapeDtypeStruct((b, seq, N_GROUPS * LANES), BF16),
                   jax.ShapeDtypeStruct((N_GROUPS, 8, gw), F32),
                   jax.ShapeDtypeStruct((N_GROUPS, 8, LANES), F32),
                   jax.ShapeDtypeStruct((N_GROUPS, 8, LANES), F32),
                   jax.ShapeDtypeStruct((N_GROUPS, 8, LANES), F32)),
        grid=(N_GROUPS, b // nb, nc),
        in_specs=[blk(gw),
                  pl.BlockSpec((nb, q, D_STATE), lambda g, i, c: (i, rc(c), xb0 + g)),
                  pl.BlockSpec((nb, q, D_STATE), lambda g, i, c: (i, rc(c), xc0 + g)),
                  blk(gw),
                  pl.BlockSpec((nb, q, LANES), lambda g, i, c: (i, rc(c), dt0 + g)),
                  blk(gw), blk(gw),
                  pl.BlockSpec((nb, None, None, gw, D_STATE), lambda g, i, c: (i, rc(c), g, 0, 0)),
                  par(LANES), par(LANES), par(gw), par(gw)],
        out_specs=(blk(gw), blk(gw), blk(D_STATE), blk(D_STATE), blk(LANES),
                   par(gw), par(LANES), par(LANES), par(LANES)),
        scratch_shapes=[pltpu.VMEM((nb, gw, D_STATE), F32), pltpu.VMEM((nb, q, LANES), F32),
                        pltpu.VMEM((nb, LANES, q), F32)],
        compiler_params=_params(("parallel", "arbitrary", "arbitrary")), name=name,
    )(xc, xc, xc, zx, zx, y, dyn, st, dtb, alog, dskip, nw)


def _adamw(w, g, m, v, name):
    rows, cols = w.shape
    tr = rows
    for cand in (512, 256, 128, 64, 32, 16, 8):
        if rows % cand == 0 and cand * cols * 4 <= 2 * 1024 * 1024:
            tr = cand
            break
    c1 = 1.0 - ADAM_B1 ** ADAM_STEP
    c2 = 1.0 - ADAM_B2 ** ADAM_STEP

    def body(w_ref, g_ref, m_ref, v_ref, d_ref, mo_ref, vo_ref):
        gv = g_ref[...]
        mn = ADAM_B1 * m_ref[...] + (1.0 - ADAM_B1) * gv
        vn = ADAM_B2 * v_ref[...] + (1.0 - ADAM_B2) * (gv * gv)
        mo_ref[...] = mn
        vo_ref[...] = vn
        d_ref[...] = -ADAM_LR * ((mn / c1) / (jnp.sqrt(vn / c2) + ADAM_EPS) + ADAM_WD * w_ref[...])

    spec = pl.BlockSpec((tr, cols), lambda i: (i, 0))
    shp = jax.ShapeDtypeStruct((rows, cols), F32)
    return pl.pallas_call(body, out_shape=(shp, shp, shp), grid=(rows // tr,), in_specs=[spec] * 4,
                          out_specs=(spec,) * 3, compiler_params=_params(("parallel",)), name=name)(w, g, m, v)


def _pick_rows(rows, row_bytes, limit=1 << 20):
    for cand in (2048, 1024, 512, 256, 128, 64, 32, 16):
        if rows % cand == 0 and cand * row_bytes <= limit:
            return cand
    return rows


def _as3d(a, lead):
    return a.reshape(a.shape[:lead] + (-1, a.shape[-1]))


def _pair_sum(g, got, core, name):
    h = got.shape[0]
    g3, got3 = _as3d(g, 1), _as3d(got, 1)
    _, rows, cols = got3.shape
    tr = _pick_rows(rows, cols * 4)

    def body(c_ref, g_ref, r_ref, o_ref):
        o_ref[...] = (g_ref[...] + r_ref[...]).astype(BF16)

    out = pl.pallas_call(
        body, out_shape=jax.ShapeDtypeStruct(got3.shape, BF16),
        grid_spec=pltpu.PrefetchScalarGridSpec(
            num_scalar_prefetch=1, grid=(h, rows // tr),
            in_specs=[pl.BlockSpec((None, tr, cols), lambda l, i, c_ref: (c_ref[0] * h + l, i, 0)),
                      pl.BlockSpec((None, tr, cols), lambda l, i, c_ref: (l, i, 0))],
            out_specs=pl.BlockSpec((None, tr, cols), lambda l, i, c_ref: (l, i, 0))),
        compiler_params=_params(("parallel", "parallel")), name=name)(core, g3, got3)
    return out.reshape(got.shape)


def _sum4(q, core, name):
    q4 = _as3d(q, 2)
    _, h, rows, cols = q4.shape
    tr = _pick_rows(rows, cols * 4)

    def body(c_ref, q0, q1, q2, q3, o_ref):
        o_ref[...] = ((q0[...].astype(F32) + q1[...].astype(F32)) + q2[...].astype(F32)) + q3[...].astype(F32)

    out = pl.pallas_call(
        body, out_shape=jax.ShapeDtypeStruct((2 * h, rows, cols), F32),
        grid_spec=pltpu.PrefetchScalarGridSpec(
            num_scalar_prefetch=1, grid=(h, rows // tr),
            in_specs=[pl.BlockSpec((None, None, tr, cols), lambda l, i, c_ref, k=k: (k, l, i, 0))
                      for k in range(N_CHIPS)],
            out_specs=pl.BlockSpec((None, tr, cols), lambda l, i, c_ref: (c_ref[0] * h + l, i, 0))),
        compiler_params=_params(("parallel", "parallel")), name=name)(core, q4, q4, q4, q4)
    return out.reshape((2 * h,) + q.shape[2:])


def _coords():
    return lax.axis_index("x"), lax.axis_index("y"), lax.axis_index("c")


def _other_chips(x, y):
    return [(1 - x, y), (x, 1 - y), (1 - x, 1 - y)]


def _allgather_halves(src, name):
    rows, cols = src.shape

    def body(x_ref, o_ref, send, recv, local):
        x, y, c = _coords()
        sib = (x, y, 1 - c)
        chips = _other_chips(x, y)

        def slot(h, cx, cy):
            return o_ref.at[h, 2 * cx + cy]

        def copy(kk, dst, to, src_ref):
            return pltpu.make_async_remote_copy(src_ref=src_ref, dst_ref=dst, send_sem=send.at[kk],
                                                recv_sem=recv.at[kk], device_id=to, device_id_type=MESH)

        mine = pltpu.make_async_copy(x_ref, slot(c, x, y), local)
        mine.start()
        first = [copy(0, slot(c, x, y), sib, x_ref)]
        first += [copy(1 + j, slot(c, x, y), (*chip, c), x_ref) for j, chip in enumerate(chips)]
        for cp in first:
            cp.start()
        passed = [copy(4 + j, slot(c, *chip), sib, slot(c, *chip)) for j, chip in enumerate(chips)]
        for j, chip in enumerate(chips):
            copy(1 + j, slot(c, *chip), (x, y, c), x_ref).wait_recv()
            passed[j].start()
        copy(0, slot(1 - c, x, y), (x, y, c), x_ref).wait_recv()
        for j, chip in enumerate(chips):
            copy(4 + j, slot(1 - c, *chip), (x, y, c), x_ref).wait_recv()
        for cp in first + passed:
            cp.wait_send()
        mine.wait()

    return pl.pallas_call(
        body, out_shape=jax.ShapeDtypeStruct((2, N_CHIPS, rows, cols), src.dtype),
        in_specs=[ANY], out_specs=ANY,
        scratch_shapes=[pltpu.SemaphoreType.DMA((7,)), pltpu.SemaphoreType.DMA((7,)), pltpu.SemaphoreType.DMA],
        name=name)(src)


MIXW = (("ssd_w_in", None), ("ssd_w_out", 0), ("pool_w", 1))
FFNW = (("ffn_w_up", 1), ("ffn_w_down", 0))


def _chip_window(axis, ref, layers, k):
    if axis is None:
        return ref.at[layers, k]
    n = ref.shape[1 + axis] // N_CHIPS
    sl = pl.ds(pl.multiple_of(k * n, LANES if 1 + axis == len(ref.shape) - 1 else 8), n)
    idx = [layers] + [slice(None)] * (len(ref.shape) - 1)
    idx[1 + axis] = sl
    return ref.at[tuple(idx)]


def _full_shape(axis, shard_shape):
    if axis is None:
        return (shard_shape[0], N_CHIPS) + tuple(shard_shape[1:])
    full = list(shard_shape)
    full[1 + axis] *= N_CHIPS
    return tuple(full)


HBM_SPEC = pl.BlockSpec(memory_space=pltpu.HBM)
SEM_SPEC = pl.BlockSpec(memory_space=pltpu.SEMAPHORE)


def _dma_sems(count):
    return pltpu.SemaphoreType.DMA((max(count, 1),))


def _wait_for(copy, kind):
    if kind == "recv":
        copy.wait_recv()
    elif kind == "send":
        copy.wait_send()
    else:
        copy.wait()


def _comm_fused(stages, counts, srcs, lands, name, inplace=False):
    ns, nl, k = len(srcs), len(lands), len(stages)

    def body(*refs):
        src_refs = refs[:ns]
        land_refs = refs[ns + (nl if inplace else 0):ns + (nl if inplace else 0) + nl]
        sem_refs = refs[len(refs) - 3 * k:]
        for s, stage_fn in enumerate(stages):
            starts, waits = stage_fn(src_refs, land_refs, tuple(sem_refs[3 * s:3 * s + 3]))
            for cp in starts:
                cp.start()
            for cp, kind in waits:
                _wait_for(cp, kind)

    scratch = []
    for cnt in counts:
        scratch += [_dma_sems(c) for c in cnt]
    outs = pl.pallas_call(
        body, out_shape=tuple(jax.ShapeDtypeStruct(a.shape, a.dtype) for a in lands),
        in_specs=[ANY] * (ns + (nl if inplace else 0)), out_specs=(ANY,) * nl,
        input_output_aliases={ns + i: i for i in range(nl)} if inplace else {},
        scratch_shapes=scratch, name=name)(*srcs, *(lands if inplace else ()))
    return list(outs)


class _SplitComm:
    def __init__(self, stages, counts, srcs, lands, name):
        self.stages, self.counts, self.name = stages, counts, name
        self.ns = len(srcs)
        self.data = [pltpu.with_memory_space_constraint(a, pltpu.HBM) for a in list(srcs) + list(lands)]
        self.sems = None
        self.step = 0

    def advance(self, after=None):
        i, k, nd, ns = self.step, len(self.stages), len(self.data), self.ns
        first, last = i == 0, i == k
        stages = self.stages

        def body(*refs):
            data = refs[:nd]
            pos = nd
            if not first:
                old = tuple(refs[pos:pos + 3])
                pos += 4
            if not last:
                new = tuple(refs[pos:pos + 3])
            if not first:
                for cp, kind in stages[i - 1](data[:ns], data[ns:], old)[1]:
                    _wait_for(cp, kind)
            if not last:
                for cp in stages[i](data[:ns], data[ns:], new)[0]:
                    cp.start()
                refs[len(refs) - 1][...] = jnp.zeros((8, LANES), F32)

        args = list(self.data)
        in_specs = [HBM_SPEC] * nd
        if not first:
            args += list(self.sems) + [after]
            in_specs += [SEM_SPEC] * 3 + [ANY]
        out_shape, out_specs = [], []
        if not last:
            out_shape += [_dma_sems(c) for c in self.counts[i]]
            out_specs += [SEM_SPEC] * 3
        out_shape += [pltpu.HBM(a.shape, a.dtype) for a in self.data]
        out_specs += [HBM_SPEC] * nd
        if not last:
            out_shape.append(jax.ShapeDtypeStruct((8, LANES), F32))
            out_specs.append(pl.BlockSpec(memory_space=pltpu.VMEM))
        off = 0 if last else 3
        outs = pl.pallas_call(
            body, out_shape=tuple(out_shape), in_specs=in_specs, out_specs=tuple(out_specs),
            input_output_aliases={d: off + d for d in range(nd)},
            compiler_params=pltpu.CompilerParams(has_side_effects=pltpu.SideEffectType.DATAFLOW_SIDE_EFFECTING),
            name=f"{self.name}_{i}")(*args)
        self.sems = None if last else outs[:3]
        self.data = list(outs[off:off + nd])
        self.step += 1
        return None if last else outs[len(outs) - 1]

    def lands(self):
        return self.data[self.ns:]


def _gather_stages(spec):
    n = len(spec)

    def parts(srcs, lands):
        x, y, c = _coords()
        out = []
        for w, (_, axis) in enumerate(spec):
            h = srcs[w].shape[0] // 2
            mine, theirs = pl.ds(c * h, h), pl.ds((1 - c) * h, h)
            out.append((srcs[w].at[mine], lambda layers, k, w=w, axis=axis: _chip_window(axis, lands[w], layers, k),
                        mine, theirs))
        return x, y, c, 2 * x + y, (x, y, 1 - c), _other_chips(x, y), out

    def remote(src, dst, send, recv, idx, to):
        return pltpu.make_async_remote_copy(src_ref=src, dst_ref=dst, send_sem=send.at[idx], recv_sem=recv.at[idx],
                                            device_id=to, device_id_type=MESH)

    def stage0(srcs, lands, sems):
        send, recv, local = sems
        x, y, c, me, sib, chips, ps = parts(srcs, lands)
        starts, waits = [], []
        for w, (src, dst, mine, theirs) in enumerate(ps):
            lc = pltpu.make_async_copy(src, dst(mine, me), local.at[w])
            first = [remote(src, dst(mine, me), send, recv, 4 * w, sib)]
            first += [remote(src, dst(mine, me), send, recv, 4 * w + 1 + j, (cx, cy, c)) for j, (cx, cy) in enumerate(chips)]
            starts += [lc] + first
            waits.append((remote(src, dst(theirs, me), send, recv, 4 * w, (x, y, c)), "recv"))
            waits += [(remote(src, dst(mine, 2 * cx + cy), send, recv, 4 * w + 1 + j, (x, y, c)), "recv")
                      for j, (cx, cy) in enumerate(chips)]
            waits += [(cp, "send") for cp in first] + [(lc, "local")]
        return starts, waits

    def stage1(srcs, lands, sems):
        send, recv, _ = sems
        x, y, c, me, sib, chips, ps = parts(srcs, lands)
        starts, waits = [], []
        for w, (src, dst, mine, theirs) in enumerate(ps):
            for j, (cx, cy) in enumerate(chips):
                blk = dst(mine, 2 * cx + cy)
                fwd = remote(blk, blk, send, recv, 3 * w + j, sib)
                starts.append(fwd)
                waits.append((remote(src, dst(theirs, 2 * cx + cy), send, recv, 3 * w + j, (x, y, c)), "recv"))
                waits.append((fwd, "send"))
        return starts, waits

    return [stage0, stage1], [(4 * n, 4 * n, n), (3 * n, 3 * n, 0)]


def _swap_stages(spec):
    n = len(spec)

    def stage(srcs, lands, sems):
        send, recv, _ = sems
        x, y, c = _coords()
        starts, waits = [], []
        for w in range(n):
            h = srcs[w].shape[0] // 2
            cp = pltpu.make_async_remote_copy(src_ref=srcs[w].at[pl.ds((1 - c) * h, h)], dst_ref=lands[w],
                                              send_sem=send.at[w], recv_sem=recv.at[w],
                                              device_id=(x, y, 1 - c), device_id_type=MESH)
            starts.append(cp)
            waits += [(cp, "recv"), (cp, "send")]
        return starts, waits

    return [stage], [(n, n, 0)]


def _scatter_stages(spec):
    n = len(spec)

    def stage(srcs, lands, sems):
        send, recv, local = sems
        x, y, c = _coords()
        me = 2 * x + y
        starts, waits = [], []
        for w, (_, axis) in enumerate(spec):
            layers = pl.ds(0, srcs[w].shape[0])
            own = _chip_window(axis, srcs[w], layers, me)
            lc = pltpu.make_async_copy(own, lands[w].at[me], local.at[w])
            starts.append(lc)
            for j, (cx, cy) in enumerate(_other_chips(x, y)):
                cp = pltpu.make_async_remote_copy(src_ref=_chip_window(axis, srcs[w], layers, 2 * cx + cy),
                                                  dst_ref=lands[w].at[me], send_sem=send.at[3 * w + j],
                                                  recv_sem=recv.at[3 * w + j], device_id=(cx, cy, c), device_id_type=MESH)
                starts.append(cp)
                waits.append((pltpu.make_async_remote_copy(
                    src_ref=own, dst_ref=lands[w].at[2 * cx + cy], send_sem=send.at[3 * w + j], recv_sem=recv.at[3 * w + j],
                    device_id=(x, y, c), device_id_type=MESH), "recv"))
                waits.append((cp, "send"))
            waits.append((lc, "local"))
        return starts, waits

    return [stage], [(3 * n, 3 * n, n)]


def _share_stages(spec):
    n = len(spec)

    def stage(srcs, lands, sems):
        send, recv, _ = sems
        x, y, c = _coords()
        starts, waits = [], []
        for w in range(n):
            h = lands[w].shape[0] // 2
            mine, theirs = lands[w].at[pl.ds(c * h, h)], lands[w].at[pl.ds((1 - c) * h, h)]
            cp = pltpu.make_async_remote_copy(src_ref=mine, dst_ref=mine, send_sem=send.at[w], recv_sem=recv.at[w],
                                              device_id=(x, y, 1 - c), device_id_type=MESH)
            starts.append(cp)
            waits.append((pltpu.make_async_remote_copy(src_ref=theirs, dst_ref=theirs, send_sem=send.at[w],
                                                       recv_sem=recv.at[w], device_id=(x, y, c), device_id_type=MESH),
                          "recv"))
            waits.append((cp, "send"))
        return starts, waits

    return [stage], [(n, n, 0)]


def _shard_of(p, axis):
    if axis is None:
        return (p.shape[0],) + tuple(p.shape[2:])
    s = list(p.shape)
    s[1 + axis] //= N_CHIPS
    return tuple(s)


def _reduce_begin(spec, gs, core, tag):
    stages, counts = _swap_stages(spec)
    got = _comm_fused(stages, counts, gs,
                      [jax.ShapeDtypeStruct((g.shape[0] // 2,) + g.shape[1:], g.dtype) for g in gs], "swap_" + tag)
    pair = [_pair_sum(a, r, core, "pair_sum_" + n) for a, r, (n, _) in zip(gs, got, spec)]
    stages, counts = _scatter_stages(spec)
    lands = [lax.empty((N_CHIPS,) + _shard_of(p, axis), p.dtype) for p, (_, axis) in zip(pair, spec)]
    comm = _SplitComm(stages, counts, pair, lands, "scatter_" + tag)
    return comm, comm.advance()


def _reduce_finish(spec, comm, core, tag, after):
    comm.advance(after=after)
    halves = [_sum4(q, core, "sum4_" + n) for q, (n, _) in zip(comm.lands(), spec)]
    stages, counts = _share_stages(spec)
    return _comm_fused(stages, counts, [], halves, "share_" + tag, inplace=True)


def _allreduce_small(vec, name, after=()):
    rows, cols = vec.shape
    after = list(after)

    def body(x_ref, *rest):
        o_ref, buf, send, recv = rest[len(after):]
        x, y, c = _coords()
        me = 4 * x + 2 * y + c
        buf[me] = x_ref[...]
        cps = []
        for kk in range(1, 8):
            dx, dy, dc = (kk >> 2) & 1, (kk >> 1) & 1, kk & 1
            to = (1 - x if dx else x, 1 - y if dy else y, 1 - c if dc else c)
            cp = pltpu.make_async_remote_copy(src_ref=x_ref, dst_ref=buf.at[me], send_sem=send.at[kk - 1],
                                              recv_sem=recv.at[kk - 1], device_id=to, device_id_type=MESH)
            cp.start()
            cps.append((cp, 4 * to[0] + 2 * to[1] + to[2]))
        for kk, (cp, frm) in enumerate(cps):
            pltpu.make_async_remote_copy(src_ref=x_ref, dst_ref=buf.at[frm], send_sem=send.at[kk],
                                         recv_sem=recv.at[kk], device_id=(x, y, c), device_id_type=MESH).wait_recv()
        for cp, _ in cps:
            cp.wait_send()
        acc = buf[0]
        for kk in range(1, 8):
            acc = acc + buf[kk]
        o_ref[...] = acc

    vm = pl.BlockSpec(memory_space=pltpu.VMEM)
    return pl.pallas_call(
        body, out_shape=jax.ShapeDtypeStruct((rows, cols), F32), in_specs=[vm] + [ANY] * len(after), out_specs=vm,
        scratch_shapes=[pltpu.VMEM((8, rows, cols), F32), pltpu.SemaphoreType.DMA((7,)), pltpu.SemaphoreType.DMA((7,))],
        compiler_params=_params(), name=name)(vec, *after)


SMALL = (("ssd_conv_w", 2), ("pool_scale", 1), ("ffn_conv_w", 2))
REPL = ("ssd_conv_b", "ssd_dt_bias", "ssd_a_log", "ssd_d", "ssd_norm_w", "ffn_conv_b",
        "norm_mix_pre", "norm_mix_post", "norm_ffn_pre", "norm_ffn_post")
WEIGHTS = ("ssd_w_in", "ssd_conv_w", "ssd_conv_b", "ssd_dt_bias", "ssd_a_log", "ssd_d", "ssd_norm_w", "ssd_w_out",
           "pool_w", "pool_scale", "ffn_w_up", "ffn_conv_w", "ffn_conv_b", "ffn_w_down", "norm_mix_pre",
           "norm_mix_post", "norm_ffn_pre", "norm_ffn_post")


def _flat_rows(n):
    unit = 2 * 16 * FLAT_COLS
    return 2 * 16 * ((n + unit - 1) // unit)


def _flatten_shards(arrs, dtype):
    flat = jnp.concatenate([a.astype(dtype).reshape(-1) for a in arrs])
    rows = _flat_rows(flat.shape[0])
    flat = jnp.pad(flat, (0, rows * FLAT_COLS - flat.shape[0]))
    return flat.reshape(2, rows // 2, FLAT_COLS)


def _unflatten_full(gathered, shard_shapes, axes):
    per_chip = jnp.swapaxes(gathered, 0, 1).reshape(N_CHIPS, -1)
    out, off = [], 0
    for shp, ax in zip(shard_shapes, axes):
        n = math.prod(shp)
        pieces = [per_chip[k, off:off + n].reshape(shp) for k in range(N_CHIPS)]
        out.append(jnp.concatenate(pieces, axis=ax))
        off += n
    return out


def kernel(x, ssd_w_in, ssd_conv_w, ssd_conv_b, ssd_dt_bias, ssd_a_log, ssd_d, ssd_norm_w, ssd_w_out, pool_w, pool_scale, ffn_w_up, ffn_conv_w, ffn_conv_b, ffn_w_down, norm_mix_pre, norm_mix_post, norm_ffn_pre, norm_ffn_post, loss_target, m_ssd_w_in, m_ssd_conv_w, m_ssd_conv_b, m_ssd_dt_bias, m_ssd_a_log, m_ssd_d, m_ssd_norm_w, m_ssd_w_out, m_pool_w, m_pool_scale, m_ffn_w_up, m_ffn_conv_w, m_ffn_conv_b, m_ffn_w_down, m_norm_mix_pre, m_norm_mix_post, m_norm_ffn_pre, m_norm_ffn_post, v_ssd_w_in, v_ssd_conv_w, v_ssd_conv_b, v_ssd_dt_bias, v_ssd_a_log, v_ssd_d, v_ssd_norm_w, v_ssd_w_out, v_pool_w, v_pool_scale, v_ffn_w_up, v_ffn_conv_w, v_ffn_conv_b, v_ffn_w_down, v_norm_mix_pre, v_norm_mix_post, v_norm_ffn_pre, v_norm_ffn_post):
    wts = dict(ssd_w_in=ssd_w_in, ssd_conv_w=ssd_conv_w, ssd_conv_b=ssd_conv_b, ssd_dt_bias=ssd_dt_bias,
               ssd_a_log=ssd_a_log, ssd_d=ssd_d, ssd_norm_w=ssd_norm_w, ssd_w_out=ssd_w_out, pool_w=pool_w,
               pool_scale=pool_scale, ffn_w_up=ffn_w_up, ffn_conv_w=ffn_conv_w, ffn_conv_b=ffn_conv_b,
               ffn_w_down=ffn_w_down, norm_mix_pre=norm_mix_pre, norm_mix_post=norm_mix_post,
               norm_ffn_pre=norm_ffn_pre, norm_ffn_post=norm_ffn_post)
    mom = dict(ssd_w_in=m_ssd_w_in, ssd_conv_w=m_ssd_conv_w, ssd_conv_b=m_ssd_conv_b, ssd_dt_bias=m_ssd_dt_bias,
               ssd_a_log=m_ssd_a_log, ssd_d=m_ssd_d, ssd_norm_w=m_ssd_norm_w, ssd_w_out=m_ssd_w_out, pool_w=m_pool_w,
               pool_scale=m_pool_scale, ffn_w_up=m_ffn_w_up, ffn_conv_w=m_ffn_conv_w, ffn_conv_b=m_ffn_conv_b,
               ffn_w_down=m_ffn_w_down, norm_mix_pre=m_norm_mix_pre, norm_mix_post=m_norm_mix_post,
               norm_ffn_pre=m_norm_ffn_pre, norm_ffn_post=m_norm_ffn_post)
    var = dict(ssd_w_in=v_ssd_w_in, ssd_conv_w=v_ssd_conv_w, ssd_conv_b=v_ssd_conv_b, ssd_dt_bias=v_ssd_dt_bias,
               ssd_a_log=v_ssd_a_log, ssd_d=v_ssd_d, ssd_norm_w=v_ssd_norm_w, ssd_w_out=v_ssd_w_out, pool_w=v_pool_w,
               pool_scale=v_pool_scale, ffn_w_up=v_ffn_w_up, ffn_conv_w=v_ffn_conv_w, ffn_conv_b=v_ffn_conv_b,
               ffn_w_down=v_ffn_w_down, norm_mix_pre=v_norm_mix_pre, norm_mix_post=v_norm_mix_post,
               norm_ffn_pre=v_norm_ffn_pre, norm_ffn_post=v_norm_ffn_post)

    bl, seq, d = x.shape
    t = bl * seq
    depth = norm_mix_pre.shape[0]
    n_ssd = ssd_w_out.shape[0]
    d_inner = ssd_w_out.shape[1] * N_CHIPS
    nheads = d_inner // HEAD_DIM
    hpg = nheads // N_GROUPS
    gw = d_inner // N_GROUPS
    xbc = ssd_conv_w.shape[2] * N_CHIPS
    f2 = ffn_w_up.shape[2] * N_CHIPS
    ff = f2 // 2
    dg = d // 4
    cy = lax.axis_index("c")
    chip = 2 * lax.axis_index("x") + lax.axis_index("y")

    small_shapes = [wts[n].shape for n, _ in SMALL]
    small_axes = [a for _, a in SMALL]
    small_flat = _flatten_shards([wts[n] for n, _ in SMALL], F32)
    small_half = lax.dynamic_index_in_dim(small_flat, cy, 0, keepdims=False)
    small_all = _allgather_halves(small_half, "gather_small")
    conv_w, p_scale, f_conv_w = _unflatten_full(small_all, small_shapes, small_axes)
    def full_shapes(spec, shards):
        return [jax.ShapeDtypeStruct(_full_shape(axis, s.shape), s.dtype) for s, (_, axis) in zip(shards, spec)]

    stages, counts = _gather_stages(MIXW)
    mix_shards = [wts[n].astype(BF16) for n, _ in MIXW]
    w_in_cm, w_out, w_pool = _comm_fused(stages, counts, mix_shards, full_shapes(MIXW, mix_shards), "gather_mixers")
    w_in = jnp.swapaxes(w_in_cm, 1, 2).reshape(n_ssd, d, -1)
    stages, counts = _gather_stages(FFNW)
    ffn_shards = [wts[n].astype(BF16) for n, _ in FFNW]
    ffn_gather = _SplitComm(stages, counts, ffn_shards + [w_pool],
                            [lax.empty(s.shape, s.dtype) for s in full_shapes(FFNW, ffn_shards)], "gather_ffn")
    gather_token = ffn_gather.advance()

    def pad_heads(a):
        lead = a.shape[:-1]
        a = a.reshape(lead + (N_GROUPS, hpg))
        a = jnp.pad(a, [(0, 0)] * len(lead) + [(0, 0), (0, LANES - hpg)])
        return a.reshape(lead + (N_GROUPS * LANES,))

    def unpad_heads(a):
        lead = a.shape[:-1]
        return a.reshape(lead + (N_GROUPS, LANES))[..., :hpg].reshape(lead + (nheads,))

    def group_rows(a, width):
        return jnp.broadcast_to(a.reshape(N_GROUPS, 1, width), (N_GROUPS, 8, width))

    w_in_p = jnp.concatenate([w_in[..., :d_inner + xbc], pad_heads(w_in[..., d_inner + xbc:])], axis=-1)
    zw = w_in_p.shape[-1]

    x2 = x.reshape(t, d)
    tgt2 = loss_target.reshape(t, d)
    w_up = w_down = None

    saved = []
    cur = x2
    tokens = []
    h = _norm_fwd(cur, norm_mix_pre[0:1], BF16, "norm_pre_b", after=[gather_token])
    for i in range(depth):
        j = i // 2
        sv = dict(x_in=cur)
        if i % 2 == 0:
            zx = _mm(h, w_in_p, "nn", F32, "mm_ssd_in", 2048, 512, d, b_layer=j).reshape(bl, seq, zw)
            xc, xpre = _ssd_conv_fwd(zx, conv_w[j], ssd_conv_b[j:j + 1], d_inner, "ssd_conv_fwd")
            dtb = group_rows(pad_heads(ssd_dt_bias[j]), LANES)
            alog = group_rows(pad_heads(ssd_a_log[j]), LANES)
            dskip = group_rows(jnp.repeat(ssd_d[j], HEAD_DIM), gw)
            nw = group_rows(ssd_norm_w[j], gw)
            y, yn, st = _ssd_fwd(xc, zx, dtb, alog, dskip, nw, d_inner, "ssd_fwd")
            if i == 0:
                tokens.append(ffn_gather.advance(after=yn))
            mix = _mm(yn.reshape(t, d_inner), w_out, "nn", F32, "mm_ssd_out", 512, 512, d_inner, b_layer=j)
            sv.update(h=h, zx=zx, xc=xc, xpre=xpre, y=y, yn=yn, st=st, dtb=dtb, alog=alog, dskip=dskip, nw=nw)
        else:
            mix = _pool_fwd(h.reshape(bl, seq, d), w_pool[j], p_scale[j:j + 1], "pool_fwd").reshape(t, d)
            sv.update(h=h)
        sv.update(mix=mix)
        mid, u = _norm_post_pre(mix, norm_mix_post[i:i + 1], cur, norm_ffn_pre[i:i + 1], BF16, "norm_post_pre_b",
                                after=tokens)
        tokens = []
        if i == 0:
            ffn_gather.advance(after=u)
            w_up, w_down = ffn_gather.lands()
        hpre = _mm(u, w_up, "nn", BF16, "mm_up", 2048, 512, d, b_layer=i).reshape(bl, seq, f2)
        act, pre_g, pre_v = _ffn_act_fwd(hpre, f_conv_w[i], ffn_conv_b[i:i + 1], "ffn_act_fwd")
        act = act.reshape(t, ff)
        fo = _mm(act, w_down, "nn", F32, "mm_down", 1024, 512, ff, b_layer=i)
        if i + 1 == depth:
            cur = _norm_fwd(fo, norm_ffn_post[i:i + 1], F32, "norm_post", resid=mid)
        elif i % 2 == 0:
            cur, h = _norm_post_pre(fo, norm_ffn_post[i:i + 1], mid, norm_mix_pre[i + 1:i + 2], F32, "norm_post_pre_f")
        else:
            cur, h = _norm_post_pre(fo, norm_ffn_post[i:i + 1], mid, norm_mix_pre[i + 1:i + 2], BF16, "norm_post_pre_b")
        sv.update(mid=mid, u=u, hpre=hpre, pre_g=pre_g, pre_v=pre_v, act=act, fo=fo)
        saved.append(sv)

    dcur, loss_part = _loss_head(cur, tgt2, "loss_head")

    g = {n: [None] * wts[n].shape[0] for n in WEIGHTS}
    gbuf = dict(up=lax.empty((depth, d, f2), F32), down=lax.empty((depth, ff, d), F32),
                out=lax.empty((n_ssd, d_inner, d), F32), win=lax.empty((n_ssd, d, zw), F32))
    core = cy.reshape(1).astype(jnp.int32)

    def mixer_bwd(i, dmid, behind=()):
        j = i // 2
        sv = saved[i]
        done = []
        if i % 2 == 0:
            dmix, g["norm_mix_post"][i] = _norm_bwd(sv["mix"], norm_mix_post[i:i + 1], dmid, BF16, "norm_bwd_b",
                                                    after=behind)
            dyn = _mm(dmix, w_out, "nt", F32, "mm_ssd_out_dx", 1024, 1024, d, b_layer=j)
            gbuf["out"], tok = _mm(sv["yn"].reshape(t, d_inner), dmix, "tn", F32, "mm_ssd_out_dw", 1024, 512, 2048,
                                   out_buf=(gbuf["out"], j))
            done.append(tok)
            dz, dxs, dbm, dcm, ddt, dnw, dd, dal, dbias = _ssd_bwd(
                sv["xc"], sv["zx"], sv["y"], dyn.reshape(bl, seq, d_inner), sv["st"], sv["dtb"], sv["alog"],
                sv["dskip"], sv["nw"], d_inner, "ssd_bwd")
            g["ssd_norm_w"][j] = dnw[:, 0, :].reshape(d_inner)
            g["ssd_d"][j] = dd[:, 0, :hpg].reshape(nheads)
            g["ssd_a_log"][j] = dal[:, 0, :hpg].reshape(nheads)
            g["ssd_dt_bias"][j] = dbias[:, 0, :hpg].reshape(nheads)
            dxbc, dcw, dcb = _ssd_conv_bwd(sv["zx"], sv["xpre"], (dxs, dbm, dcm), conv_w[j], d_inner, "ssd_conv_bwd")
            g["ssd_conv_w"][j] = dcw
            g["ssd_conv_b"][j] = dcb[0]
            dzs = [dz.reshape(t, d_inner), dxbc.reshape(t, xbc), ddt.reshape(t, N_GROUPS * LANES)]
            dh = _mm(dzs, w_in_p, "nt", F32, "mm_ssd_in_dx", 1024, d, 512, b_layer=j)
            gbuf["win"], tok = _mm(sv["h"], dzs, "tn", F32, "mm_ssd_in_dw", 1024, 512, 2048, out_buf=(gbuf["win"], j))
            done.append(tok)
        else:
            dmix, g["norm_mix_post"][i] = _norm_bwd(sv["mix"], norm_mix_post[i:i + 1], dmid, F32, "norm_bwd_f",
                                                    after=behind)
            dh3, g["pool_w"][j], dps = _pool_bwd(sv["h"].reshape(bl, seq, d), dmix.reshape(bl, seq, d), w_pool[j],
                                                 p_scale[j:j + 1], "pool_bwd")
            g["pool_scale"][j] = dps[0]
            dh = dh3.reshape(t, d)
        dx_in, g["norm_mix_pre"][i] = _norm_bwd(sv["x_in"], norm_mix_pre[i:i + 1], dh, F32, "norm_bwd_r", resid=dmid,
                                                after=done)
        return dx_in

    ffn_comm = None
    for i in reversed(range(depth)):
        sv = saved[i]
        dfo, g["norm_ffn_post"][i] = _norm_bwd(sv["fo"], norm_ffn_post[i:i + 1], dcur, BF16, "norm_bwd_b")
        dact = _mm(dfo, w_down, "nt", BF16, "mm_down_dx", 1024, ff // 2, d, b_layer=i)
        gbuf["down"], tok_down = _mm(sv["act"], dfo, "tn", F32, "mm_down_dw", ff // 2, 512, 2048,
                                     out_buf=(gbuf["down"], i))
        dhg, dhv, dcw, dcb = _ffn_act_bwd(sv["hpre"], sv["pre_g"], sv["pre_v"], dact.reshape(bl, seq, ff), f_conv_w[i],
                                          "ffn_act_bwd")
        g["ffn_conv_w"][i] = dcw
        g["ffn_conv_b"][i] = dcb[0]
        dhs = [dhg.reshape(t, ff), dhv.reshape(t, ff)]
        du = _mm(dhs, w_up, "nt", F32, "mm_up_dx", 1024, d, ff, b_layer=i)
        gbuf["up"], tok_up = _mm(sv["u"], dhs, "tn", F32, "mm_up_dw", 512, ff // 2, 2048, out_buf=(gbuf["up"], i))
        dmid, g["norm_ffn_pre"][i] = _norm_bwd(sv["mid"], norm_ffn_pre[i:i + 1], du, F32, "norm_bwd_r", resid=dcur,
                                               after=[tok_down, tok_up])
        if i > 0:
            dcur = mixer_bwd(i, dmid)
        else:
            ffn_comm, ffn_token = _reduce_begin(FFNW, [gbuf["up"], gbuf["down"]], core, "ffn")
            dcur = mixer_bwd(0, dmid, behind=[ffn_token])

    grad_x = dcur.reshape(bl, seq, d)
    for n in ("norm_mix_pre", "norm_mix_post", "norm_ffn_pre", "norm_ffn_post"):
        g[n] = [a[0] for a in g[n]]
    small_names = [n for n, _ in SMALL] + list(REPL)
    full = {n: jnp.stack(g[n], axis=0) for n in small_names}

    g_in = jnp.concatenate([gbuf["win"][..., :d_inner + xbc], unpad_heads(gbuf["win"][..., d_inner + xbc:])], axis=-1)
    g_in_cm = jnp.swapaxes(g_in.reshape(n_ssd, d, N_CHIPS, -1), 1, 2)
    mix_comm, mix_token = _reduce_begin(MIXW, [g_in_cm, gbuf["out"], jnp.stack(g["pool_w"], axis=0)], core, "mixers")

    grads, deltas, new_m, new_v = {}, {}, {}, {}

    def adamw(n, gr):
        shp = wts[n].shape
        two = (math.prod(shp[:-1]), shp[-1])
        dl, mn, vn = _adamw(wts[n].reshape(two), gr.reshape(two), mom[n].reshape(two), var[n].reshape(two),
                            "adamw_" + n)
        grads[n], deltas[n], new_m[n], new_v[n] = gr, dl.reshape(shp), mn.reshape(shp), vn.reshape(shp)
        return dl

    ffn_grads = _reduce_finish(FFNW, ffn_comm, core, "ffn", after=mix_token)
    for gr, (n, _) in zip(ffn_grads, FFNW):
        last = adamw(n, gr)
    mix_grads = _reduce_finish(MIXW, mix_comm, core, "mixers", after=last)
    for gr, (n, _) in zip(mix_grads, MIXW):
        adamw(n, gr)

    vec = jnp.concatenate([full[n].reshape(-1) for n in small_names] + [loss_part[0, :1]])
    nvec = vec.shape[0]
    vrows = 8 * ((nvec + 8 * FLAT_COLS - 1) // (8 * FLAT_COLS))
    vec = jnp.pad(vec, (0, vrows * FLAT_COLS - nvec)).reshape(vrows, FLAT_COLS)
    tot = _allreduce_small(vec, "allreduce_small", after=[mix_grads[0]]).reshape(-1)
    small_grads, off = {}, 0
    for n in small_names:
        cnt = math.prod(full[n].shape)
        small_grads[n] = tot[off:off + cnt].reshape(full[n].shape)
        off += cnt
    loss = tot[off]
    for n, ax in SMALL:
        w = wts[n].shape[ax]
        small_grads[n] = lax.dynamic_slice_in_dim(small_grads[n], chip * w, w, axis=ax)

    for n in small_names:
        adamw(n, small_grads[n])

    return (loss, grad_x, *[grads[n] for n in WEIGHTS], *[deltas[n] for n in WEIGHTS],
            *[new_m[n] for n in WEIGHTS], *[new_v[n] for n in WEIGHTS])
```

```python
import functools
import math

import jax
import jax.numpy as jnp
from jax import lax
from jax.experimental import pallas as pl
from jax.experimental.pallas import tpu as pltpu

F32 = jnp.float32
BF16 = jnp.bfloat16
MESH = pl.DeviceIdType.MESH
ANY = pl.BlockSpec(memory_space=pl.ANY)

HEAD_DIM = 64
D_STATE = 128
CHUNK = 128
N_GROUPS = 4
SSD_CONV = 4
FFN_CONV = 3
EPS = 1e-6
N_CHIPS = 4
LANES = 128
FLAT_COLS = 1024

ADAM_LR = 0.001
ADAM_B1 = 0.9
ADAM_B2 = 0.999
ADAM_EPS = 1e-08
ADAM_WD = 0.01
ADAM_STEP = 10

VMEM_LIMIT_BYTES = 56 * 1024 * 1024


def _params(sem=None):
    kw = dict(vmem_limit_bytes=VMEM_LIMIT_BYTES)
    if sem is not None:
        kw["dimension_semantics"] = sem
    return pltpu.CompilerParams(**kw)


def _sigmoid(x):
    return 0.5 * jnp.tanh(0.5 * x) + 0.5


def _softplus(x):
    return jnp.maximum(x, 0.0) + jnp.log(1.0 + jnp.exp(-jnp.abs(x)))


def _dot(a, b, dn):
    return lax.dot_general(a, b, (dn, ((), ())), preferred_element_type=F32)


def _nn(a, b):
    return _dot(a, b, ((1,), (0,)))


def _nt(a, b):
    return _dot(a, b, ((1,), (1,)))


def _tn(a, b):
    return _dot(a, b, ((0,), (0,)))


def _split(x, parts):
    out = []
    r = x
    for _ in range(parts):
        p = r.astype(BF16)
        out.append(p)
        r = r - p.astype(F32)
    return out


def _sel_left(sel, x, parts=3):
    n = x.shape[1]
    r = _nn(sel, jnp.concatenate(_split(x, parts), axis=1))
    out = r[:, 0:n]
    for i in range(1, parts):
        out = out + r[:, i * n:(i + 1) * n]
    return out


def _sel_right(x, sel_stacked, parts=3):
    return _nn(jnp.concatenate(_split(x, parts), axis=1), sel_stacked)


def _mm(a, b, dims, out_dtype, name, tm, tn, tk, b_layer=None, out_buf=None):
    a_list = list(a) if isinstance(a, (list, tuple)) else [a]
    b_list = list(b) if isinstance(b, (list, tuple)) else [b]
    if dims in ("nn", "nt"):
        assert len(b_list) == 1
        m = a_list[0].shape[0]
        segs = [x.shape[1] for x in a_list]
        k = sum(segs)
        bshape = b_list[0].shape[-2:]
        n = bshape[1] if dims == "nn" else bshape[0]
        assert (bshape[0] if dims == "nn" else bshape[1]) == k
    else:
        assert len(a_list) == 1 and b_layer is None
        k, m = a_list[0].shape
        segs = [x.shape[1] for x in b_list]
        n = sum(segs)
    nseg = len(segs)
    tm, tn = min(tm, m), min(tn, n)
    if dims == "tn":
        tk = min(tk, k)
        tn = min(tn, min(segs))
        units = [tn] * nseg
        nk = k // tk
        assert k % tk == 0
    else:
        units = [min(u, s) for u, s in zip(tk if isinstance(tk, (list, tuple)) else [tk] * nseg, segs)]
        nk = sum(s // u for s, u in zip(segs, units))
    assert m % tm == 0 and n % tn == 0 and all(s % u == 0 for s, u in zip(segs, units)), (name, m, n, k, segs, units)
    counts = [s // u for s, u in zip(segs, units)]
    starts = [sum(counts[:s]) for s in range(nseg)]
    assert all(sum(segs[:s]) % units[s] == 0 for s in range(nseg)), (name, segs, units)
    first_block = [sum(segs[:s]) // units[s] for s in range(nseg)]
    dn = {"nn": ((1,), (0,)), "nt": ((1,), (1,)), "tn": ((0,), (0,))}[dims]

    same = len(set(units)) == 1
    nb_ops = len(b_list) if dims == "tn" else (1 if same else nseg)

    def body(*refs):
        a_refs = refs[:len(a_list)]
        b_refs = refs[len(a_list):len(a_list) + nb_ops]
        rest = refs[len(a_list) + nb_ops + (0 if out_buf is None else 1):]
        o_ref = rest[0]
        if out_buf is not None:
            rest[1][...] = jnp.zeros((8, LANES), F32)
            rest = rest[1:]
        acc = rest[1] if nk > 1 else None
        kk = pl.program_id(2)
        sel = kk if dims != "tn" else pl.program_id(1)

        def step(a_ref, b_ref):
            p = _dot(a_ref[...].astype(BF16), b_ref[...].astype(BF16), dn)
            if nk == 1:
                o_ref[...] = p.astype(out_dtype)
                return

            @pl.when(kk == 0)
            def _():
                acc[...] = p

            @pl.when(kk > 0)
            def _():
                acc[...] += p

        if nseg == 1:
            step(a_refs[0], b_refs[0])
        else:
            for s in range(nseg):
                @pl.when(jnp.logical_and(sel >= starts[s], sel < starts[s] + counts[s]))
                def _(s=s):
                    step(a_refs[s] if dims != "tn" else a_refs[0], b_refs[s if nb_ops > 1 else 0])

        if nk > 1:
            @pl.when(kk == nk - 1)
            def _():
                o_ref[...] = acc[...].astype(out_dtype)

    def seg_index(v, s):
        return v if nseg == 1 else jnp.clip(v - starts[s], 0, counts[s] - 1)

    lead = () if b_layer is None else (b_layer,)
    none = () if b_layer is None else (None,)
    def b_block(kk, s):
        return kk if same else first_block[s] + seg_index(kk, s)

    if dims == "nn":
        a_specs = [pl.BlockSpec((tm, units[s]), lambda i, j, kk, s=s: (i, seg_index(kk, s))) for s in range(nseg)]
        b_specs = [pl.BlockSpec(none + (units[s], tn), lambda i, j, kk, s=s: lead + (b_block(kk, s), j))
                   for s in range(nb_ops)]
    elif dims == "nt":
        a_specs = [pl.BlockSpec((tm, units[s]), lambda i, j, kk, s=s: (i, seg_index(kk, s))) for s in range(nseg)]
        b_specs = [pl.BlockSpec(none + (tn, units[s]), lambda i, j, kk, s=s: lead + (j, b_block(kk, s)))
                   for s in range(nb_ops)]
    else:
        a_specs = [pl.BlockSpec((tk, tm), lambda i, j, kk: (kk, i))]
        b_specs = [pl.BlockSpec((tk, tn), lambda i, j, kk, s=s: (kk, seg_index(j, s))) for s in range(nseg)]
    args = a_list + (b_list * nb_ops if dims != "tn" else b_list)
    in_specs = a_specs + b_specs
    aliases = {}
    if out_buf is None:
        out_shape = jax.ShapeDtypeStruct((m, n), out_dtype)
        out_spec = pl.BlockSpec((tm, tn), lambda i, j, kk: (i, j))
    else:
        buf, slab = out_buf
        assert buf.shape[1:] == (m, n) and buf.dtype == out_dtype
        out_shape = (jax.ShapeDtypeStruct(buf.shape, out_dtype), jax.ShapeDtypeStruct((8, LANES), F32))
        out_spec = (pl.BlockSpec((None, tm, tn), lambda i, j, kk: (slab, i, j)),
                    pl.BlockSpec((8, LANES), lambda i, j, kk: (0, 0)))
        aliases = {len(args): 0}
        args = args + [buf]
        in_specs = in_specs + [ANY]
    return pl.pallas_call(
        body,
        out_shape=out_shape,
        grid=(m // tm, n // tn, nk),
        in_specs=in_specs,
        out_specs=out_spec,
        scratch_shapes=[] if nk == 1 else [pltpu.VMEM((tm, tn), F32)],
        input_output_aliases=aliases,
        compiler_params=_params(("parallel", "parallel", "arbitrary") if out_buf is None else ("arbitrary",) * 3),
        name=name,
    )(*args)


def _row_tile(t, want):
    tm = min(want, t)
    assert t % tm == 0
    return tm


def _norm_fwd(x, w, out_dtype, name, resid=None, after=()):
    t, d = x.shape
    tm = _row_tile(t, 512)
    after = [a for a in after if a is not None]

    def body(*refs):
        refs = refs[:len(refs) - 1 - len(after)] + refs[len(refs) - 1:]
        if resid is None:
            x_ref, w_ref, o_ref = refs
        else:
            x_ref, w_ref, r_ref, o_ref = refs
        xv = x_ref[...]
        r = lax.rsqrt(jnp.mean(xv * xv, axis=-1, keepdims=True) + EPS)
        y = (xv * r) * w_ref[...]
        if resid is not None:
            y = r_ref[...] + y
        o_ref[...] = y.astype(out_dtype)

    row = pl.BlockSpec((tm, d), lambda i: (i, 0))
    vec = pl.BlockSpec((1, d), lambda i: (0, 0))
    args = [x, w] + ([] if resid is None else [resid]) + after
    return pl.pallas_call(
        body, out_shape=jax.ShapeDtypeStruct((t, d), out_dtype), grid=(t // tm,),
        in_specs=[row, vec] + ([] if resid is None else [row]) + [ANY] * len(after), out_specs=row,
        compiler_params=_params(("parallel",)), name=name)(*args)


def _norm_post_pre(m, w_post, resid, w_pre, pre_dtype, name, after=()):
    t, d = m.shape
    tm = _row_tile(t, 512)
    after = [a for a in after if a is not None]

    def body(m_ref, w1_ref, r_ref, w2_ref, *rest):
        x_ref, u_ref = rest[len(after):]
        mv = m_ref[...]
        r1 = lax.rsqrt(jnp.mean(mv * mv, axis=-1, keepdims=True) + EPS)
        xv = r_ref[...] + (mv * r1) * w1_ref[...]
        x_ref[...] = xv
        r2 = lax.rsqrt(jnp.mean(xv * xv, axis=-1, keepdims=True) + EPS)
        u_ref[...] = ((xv * r2) * w2_ref[...]).astype(pre_dtype)

    row = pl.BlockSpec((tm, d), lambda i: (i, 0))
    vec = pl.BlockSpec((1, d), lambda i: (0, 0))
    return pl.pallas_call(
        body, out_shape=(jax.ShapeDtypeStruct((t, d), F32), jax.ShapeDtypeStruct((t, d), pre_dtype)), grid=(t // tm,),
        in_specs=[row, vec, row, vec] + [ANY] * len(after), out_specs=(row, row),
        compiler_params=_params(("parallel",)), name=name)(m, w_post, resid, w_pre, *after)


def _norm_bwd(src, w, dy, out_dtype, name, resid=None, after=()):
    t, d = src.shape
    tm = _row_tile(t, 512)
    after = [a for a in after if a is not None]

    def body(*refs):
        refs = refs[:len(refs) - 2 - len(after)] + refs[len(refs) - 2:]
        if resid is None:
            x_ref, w_ref, g_ref, o_ref, dw_ref = refs
        else:
            x_ref, w_ref, g_ref, r_ref, o_ref, dw_ref = refs
        xv = x_ref[...]
        g = g_ref[...].astype(F32)
        r = lax.rsqrt(jnp.mean(xv * xv, axis=-1, keepdims=True) + EPS)
        xh = xv * r
        gh = g * w_ref[...]
        mean = jnp.mean(gh * xh, axis=-1, keepdims=True)
        dx = r * (gh - xh * mean)
        if resid is not None:
            dx = r_ref[...] + dx
        o_ref[...] = dx.astype(out_dtype)
        part = jnp.sum(g * xh, axis=0, keepdims=True)

        @pl.when(pl.program_id(0) == 0)
        def _():
            dw_ref[...] = part

        @pl.when(pl.program_id(0) > 0)
        def _():
            dw_ref[...] += part

    row = pl.BlockSpec((tm, d), lambda i: (i, 0))
    vec = pl.BlockSpec((1, d), lambda i: (0, 0))
    args = [src, w, dy] + ([] if resid is None else [resid]) + after
    return pl.pallas_call(
        body,
        out_shape=(jax.ShapeDtypeStruct((t, d), out_dtype), jax.ShapeDtypeStruct((1, d), F32)),
        grid=(t // tm,),
        in_specs=[row, vec, row] + ([] if resid is None else [row]) + [ANY] * len(after),
        out_specs=(row, vec),
        compiler_params=_params(("arbitrary",)), name=name)(*args)


def _loss_head(y, target, name):
    t, d = y.shape
    tm = _row_tile(t, 512)

    def body(y_ref, t_ref, dy_ref, l_ref):
        e = y_ref[...] - t_ref[...]
        dy_ref[...] = e * (1.0 / d)
        col = jnp.sum(e * e, axis=0, keepdims=True)
        s = jnp.sum(col, axis=1, keepdims=True) * (0.5 / d)
        part = jnp.broadcast_to(s, (1, LANES))

        @pl.when(pl.program_id(0) == 0)
        def _():
            l_ref[...] = part

        @pl.when(pl.program_id(0) > 0)
        def _():
            l_ref[...] += part

    row = pl.BlockSpec((tm, d), lambda i: (i, 0))
    return pl.pallas_call(
        body,
        out_shape=(jax.ShapeDtypeStruct((t, d), F32), jax.ShapeDtypeStruct((1, LANES), F32)),
        grid=(t // tm,), in_specs=[row, row],
        out_specs=(row, pl.BlockSpec((1, LANES), lambda i: (0, 0))),
        compiler_params=_params(("arbitrary",)), name=name)(y, target)


def _window(ref, c, rows, seq, before, after):
    r0 = pl.multiple_of(c * rows, rows)
    parts = []
    if before:
        h0 = pl.multiple_of(jnp.maximum(r0 - before, 0), before)
        halo = ref[pl.ds(h0, before), :].astype(F32)
        parts.append(jnp.where(c > 0, halo, 0.0))
    parts.append(ref[pl.ds(r0, rows), :].astype(F32))
    if after:
        h1 = pl.multiple_of(jnp.minimum(r0 + rows, seq - after), after)
        halo = ref[pl.ds(h1, after), :].astype(F32)
        parts.append(jnp.where(c < seq // rows - 1, halo, 0.0))
    return parts[0] if len(parts) == 1 else jnp.concatenate(parts, axis=0)


def _lag(x, k):
    return pltpu.roll(x, k, 0) if k else x


def _lead(x, k):
    return pltpu.roll(x, x.shape[0] - k, 0) if k else x


SHIFT_ROWS = 128
SHIFT_COLS = 256


HALO = 16


def _conv3(ext, w, bias):
    acc = bias + w[2:3, :] * ext[HALO:, :]
    acc = acc + w[1:2, :] * _lag(ext, 1)[HALO:, :]
    return acc + w[0:1, :] * _lag(ext, 2)[HALO:, :]


def _ffn_act_fwd(hpre, cw, cb, name):
    b, seq, f2 = hpre.shape
    cbk = SHIFT_COLS
    nj = f2 // (2 * cbk)
    rows = min(SHIFT_ROWS, seq)

    def body(g_ref, v_ref, wg_ref, wv_ref, bg_ref, bv_ref, o_ref, pg_ref, pv_ref):
        def chunk(c, carry):
            gate = _conv3(_window(g_ref, c, rows, seq, HALO, 0), wg_ref[...], bg_ref[...])
            val = _conv3(_window(v_ref, c, rows, seq, HALO, 0), wv_ref[...], bv_ref[...])
            a = gate * _sigmoid(gate) * val
            here = pl.ds(pl.multiple_of(c * rows, rows), rows)
            o_ref[here, :] = a.astype(BF16)
            pg_ref[here, :] = gate.astype(BF16)
            pv_ref[here, :] = val.astype(BF16)
            return carry

        lax.fori_loop(0, seq // rows, chunk, 0)

    blk = lambda off: pl.BlockSpec((None, seq, cbk), lambda i, j: (i, 0, j + off))
    wsp = lambda r, off: pl.BlockSpec((r, cbk), lambda i, j: (0, j + off))
    half = jax.ShapeDtypeStruct((b, seq, f2 // 2), BF16)
    return pl.pallas_call(
        body, out_shape=(half, half, half), grid=(b, nj),
        in_specs=[blk(0), blk(nj), wsp(FFN_CONV, 0), wsp(FFN_CONV, nj), wsp(1, 0), wsp(1, nj)],
        out_specs=(blk(0), blk(0), blk(0)),
        compiler_params=_params(("parallel", "parallel")), name=name)(hpre, hpre, cw, cw, cb, cb)


def _ffn_act_bwd(hpre, pre_g, pre_v, da, cw, name):
    b, seq, f2 = hpre.shape
    cbk = SHIFT_COLS
    nj = f2 // (2 * cbk)
    rows = min(SHIFT_ROWS, seq)

    def body(g_ref, v_ref, pg_ref, pv_ref, da_ref, wg_ref, wv_ref, og_ref, ov_ref, dwg_ref, dwv_ref, dbg_ref, dbv_ref):
        wg, wv = wg_ref[...], wv_ref[...]

        def back(dpre, w, o_ref, x_ref, c, carry):
            here = pl.ds(pl.multiple_of(c * rows, rows), rows)
            leads = [dpre, _lead(dpre, 1), _lead(dpre, 2)]
            dx = w[2:3, :] * leads[0] + w[1:2, :] * leads[1] + w[0:1, :] * leads[2]
            o_ref[here, :] = dx[:rows, :].astype(BF16)
            x0 = x_ref[here, :].astype(F32)
            return tuple(carry[k] + jnp.sum(leads[k][:rows, :] * x0, axis=0, keepdims=True) for k in range(FFN_CONV)) + (
                carry[FFN_CONV] + jnp.sum(dpre[:rows, :], axis=0, keepdims=True),)

        def chunk(c, carry):
            cg, cv = carry
            gate = _window(pg_ref, c, rows, seq, 0, HALO)
            val = _window(pv_ref, c, rows, seq, 0, HALO)
            dav = _window(da_ref, c, rows, seq, 0, HALO)
            sg = _sigmoid(gate)
            cg = back(dav * val * (sg * (1.0 + gate * (1.0 - sg))), wg, og_ref, g_ref, c, cg)
            cv = back(dav * (gate * sg), wv, ov_ref, v_ref, c, cv)
            return cg, cv

        z = jnp.zeros((1, cbk), F32)
        cg, cv = lax.fori_loop(0, seq // rows, chunk, ((z,) * (FFN_CONV + 1), (z,) * (FFN_CONV + 1)))
        dwg = jnp.concatenate([cg[2], cg[1], cg[0]], axis=0)
        dwv = jnp.concatenate([cv[2], cv[1], cv[0]], axis=0)

        @pl.when(pl.program_id(1) == 0)
        def _():
            dwg_ref[...] = dwg
            dwv_ref[...] = dwv
            dbg_ref[...] = cg[FFN_CONV]
            dbv_ref[...] = cv[FFN_CONV]

        @pl.when(pl.program_id(1) > 0)
        def _():
            dwg_ref[...] += dwg
            dwv_ref[...] += dwv
            dbg_ref[...] += cg[FFN_CONV]
            dbv_ref[...] += cv[FFN_CONV]

    blk = lambda off: pl.BlockSpec((None, seq, cbk), lambda j, i: (i, 0, j + off))
    wsp = lambda r, off: pl.BlockSpec((r, cbk), lambda j, i: (0, j + off))
    half = jax.ShapeDtypeStruct((b, seq, f2 // 2), BF16)
    dwshape = jax.ShapeDtypeStruct((FFN_CONV, f2 // 2), F32)
    dbshape = jax.ShapeDtypeStruct((1, f2 // 2), F32)
    dg, dv, dwg, dwv, dbg, dbv = pl.pallas_call(
        body,
        out_shape=(half, half, dwshape, dwshape, dbshape, dbshape),
        grid=(nj, b),
        in_specs=[blk(0), blk(nj), blk(0), blk(0), blk(0), wsp(FFN_CONV, 0), wsp(FFN_CONV, nj)],
        out_specs=(blk(0), blk(0), wsp(FFN_CONV, 0), wsp(FFN_CONV, 0), wsp(1, 0), wsp(1, 0)),
        compiler_params=_params(("parallel", "arbitrary")), name=name)(hpre, hpre, pre_g, pre_v, da, cw, cw)
    return dg, dv, jnp.concatenate([dwg, dwv], axis=1), jnp.concatenate([dbg, dbv], axis=1)


def _ssd_conv_fwd(zx, cw, cb, d_inner, name):
    b, seq, _ = zx.shape
    xbc = cw.shape[1]
    cbk = SHIFT_COLS
    off = d_inner // cbk
    rows = min(SHIFT_ROWS, seq)

    def body(h_ref, w_ref, b_ref, o_ref, p_ref):
        w = w_ref[...]
        bias = b_ref[...]

        def chunk(c, carry):
            ext = _window(h_ref, c, rows, seq, 8, 0)
            acc = bias + w[3:4, :] * ext[8:, :]
            for k in range(1, SSD_CONV):
                acc = acc + w[3 - k:4 - k, :] * _lag(ext, k)[8:, :]
            here = pl.ds(pl.multiple_of(c * rows, rows), rows)
            o_ref[here, :] = acc * _sigmoid(acc)
            p_ref[here, :] = acc.astype(BF16)
            return carry

        lax.fori_loop(0, seq // rows, chunk, 0)

    blk = pl.BlockSpec((None, seq, cbk), lambda i, j: (i, 0, j))
    return pl.pallas_call(
        body, out_shape=(jax.ShapeDtypeStruct((b, seq, xbc), F32), jax.ShapeDtypeStruct((b, seq, xbc), BF16)),
        grid=(b, xbc // cbk),
        in_specs=[pl.BlockSpec((None, seq, cbk), lambda i, j: (i, 0, j + off)),
                  pl.BlockSpec((SSD_CONV, cbk), lambda i, j: (0, j)),
                  pl.BlockSpec((1, cbk), lambda i, j: (0, j))],
        out_specs=(blk, blk),
        compiler_params=_params(("parallel", "parallel")), name=name)(zx, cw, cb)


def _ssd_conv_bwd(zx, pre, dparts, cw, d_inner, name):
    b, seq, _ = zx.shape
    xbc = cw.shape[1]
    cbk = SHIFT_COLS
    off = d_inner // cbk
    rows = min(SHIFT_ROWS, seq)
    nblk = [p.shape[2] // cbk for p in dparts]
    first = [sum(nblk[:s]) for s in range(len(dparts))]
    assert sum(nblk) == xbc // cbk

    def body(h_ref, p_ref, gx_ref, gb_ref, gc_ref, w_ref, o_ref, dw_ref, db_ref):
        w = w_ref[...]
        j = pl.program_id(0)

        def chunk(c, carry):
            dws, dbias = carry
            here = pl.ds(pl.multiple_of(c * rows, rows), rows)
            pre = _window(p_ref, c, rows, seq, 0, HALO)
            s = _sigmoid(pre)
            gsel = jnp.where(j < first[1], _window(gx_ref, c, rows, seq, 0, HALO),
                             jnp.where(j < first[2], _window(gb_ref, c, rows, seq, 0, HALO),
                                       _window(gc_ref, c, rows, seq, 0, HALO)))
            dpre = gsel * (s * (1.0 + pre * (1.0 - s)))
            leads = [dpre] + [_lead(dpre, k) for k in range(1, SSD_CONV)]
            dx = w[3:4, :] * leads[0]
            for k in range(1, SSD_CONV):
                dx = dx + w[3 - k:4 - k, :] * leads[k]
            o_ref[here, :] = dx[:rows, :].astype(BF16)
            x0 = h_ref[here, :]
            dws = tuple(dws[k] + jnp.sum(leads[k][:rows, :] * x0, axis=0, keepdims=True) for k in range(SSD_CONV))
            dbias = dbias + jnp.sum(dpre[:rows, :], axis=0, keepdims=True)
            return dws, dbias

        z = jnp.zeros((1, cbk), F32)
        dws, dbias = lax.fori_loop(0, seq // rows, chunk, ((z,) * SSD_CONV, z))
        dwv = jnp.concatenate([dws[3 - i] for i in range(SSD_CONV)], axis=0)

        @pl.when(pl.program_id(1) == 0)
        def _():
            dw_ref[...] = dwv
            db_ref[...] = dbias

        @pl.when(pl.program_id(1) > 0)
        def _():
            dw_ref[...] += dwv
            db_ref[...] += dbias

    return pl.pallas_call(
        body,
        out_shape=(jax.ShapeDtypeStruct((b, seq, xbc), BF16), jax.ShapeDtypeStruct((SSD_CONV, xbc), F32),
                   jax.ShapeDtypeStruct((1, xbc), F32)),
        grid=(xbc // cbk, b),
        in_specs=[pl.BlockSpec((None, seq, cbk), lambda j, i: (i, 0, j + off)),
                  pl.BlockSpec((None, seq, cbk), lambda j, i: (i, 0, j))] + [
                  pl.BlockSpec((None, seq, cbk), lambda j, i, s=s: (i, 0, jnp.clip(j - first[s], 0, nblk[s] - 1)))
                  for s in range(3)] + [
                  pl.BlockSpec((SSD_CONV, cbk), lambda j, i: (0, j))],
        out_specs=(pl.BlockSpec((None, seq, cbk), lambda j, i: (i, 0, j)),
                   pl.BlockSpec((SSD_CONV, cbk), lambda j, i: (0, j)),
                   pl.BlockSpec((1, cbk), lambda j, i: (0, j))),
        compiler_params=_params(("parallel", "arbitrary")), name=name)(zx, pre, *dparts, cw)


def _pool_sums(q, g, lead):
    sh = _lead if lead else _lag
    s2 = q + sh(q, 1)
    s4 = s2 + sh(s2, 2)
    s8 = s4 + sh(s4, 4)
    s16 = s8 + sh(s8, 8)
    return jnp.where(g == 0, s2, jnp.where(g == 1, s4, jnp.where(g == 2, s8, s16)))


def _pool_count(r0, n, g, shape):
    t = (r0 + lax.broadcasted_iota(jnp.int32, shape, 0) + 1).astype(F32)
    return jnp.minimum(t, (2 << g).astype(F32))


def _pool_fwd(h, pw, scale, name):
    b, seq, d = h.shape
    dg = d // 4
    rows = min(SHIFT_ROWS, seq)

    def body(h_ref, w_ref, s_ref, o_ref):
        g = pl.program_id(1)
        wmat = w_ref[...]
        sc = s_ref[...]

        def chunk(c, carry):
            r0 = c * rows
            ext = _window(h_ref, c, rows, seq, 16, 0)
            sums = _pool_sums(ext, g, False)[16:, :]
            mixed = sums / _pool_count(r0, rows, g, (rows, dg)) - ext[16:, :]
            o_ref[pl.ds(pl.multiple_of(r0, rows), rows), :] = _nn(mixed.astype(BF16), wmat) * sc
            return carry

        lax.fori_loop(0, seq // rows, chunk, 0)

    return pl.pallas_call(
        body, out_shape=jax.ShapeDtypeStruct((b, seq, d), F32), grid=(b, 4),
        in_specs=[pl.BlockSpec((None, seq, dg), lambda i, g: (i, 0, g)),
                  pl.BlockSpec((None, dg, dg), lambda i, g: (g, 0, 0)),
                  pl.BlockSpec((1, dg), lambda i, g: (0, g))],
        out_specs=pl.BlockSpec((None, seq, dg), lambda i, g: (i, 0, g)),
        compiler_params=_params(("parallel", "parallel")), name=name)(h, pw, scale)


def _pool_bwd(h, dout, pw, scale, name):
    b, seq, d = h.shape
    dg = d // 4
    rows = min(SHIFT_ROWS, seq)

    def body(h_ref, g_ref, w_ref, s_ref, o_ref, dw_ref, ds_ref, dw_acc):
        g = pl.program_id(0)
        wmat = w_ref[...]
        sc = s_ref[...]
        dw_acc[...] = jnp.zeros_like(dw_acc)

        def chunk(c, dsc):
            r0 = c * rows
            ext = _window(h_ref, c, rows, seq, 16, 0)
            sums = _pool_sums(ext, g, False)[16:, :]
            mixed = (sums / _pool_count(r0, rows, g, (rows, dg)) - ext[16:, :]).astype(BF16)
            gext = _window(g_ref, c, rows, seq, 0, 16)
            dsc = dsc + jnp.sum(gext[:rows, :] * _nn(mixed, wmat), axis=0, keepdims=True)
            dpre = (gext * sc).astype(BF16)
            dw_acc[...] += _tn(mixed, dpre[:rows, :])
            dmix = _nt(dpre, wmat)
            q = dmix / _pool_count(r0, rows + 16, g, (rows + 16, dg))
            back = _pool_sums(q, g, True)
            o_ref[pl.ds(pl.multiple_of(r0, rows), rows), :] = back[:rows, :] - dmix[:rows, :]
            return dsc

        dsc = lax.fori_loop(0, seq // rows, chunk, jnp.zeros((1, dg), F32))

        @pl.when(pl.program_id(1) == 0)
        def _():
            dw_ref[...] = dw_acc[...]
            ds_ref[...] = dsc

        @pl.when(pl.program_id(1) > 0)
        def _():
            dw_ref[...] += dw_acc[...]
            ds_ref[...] += dsc

    return pl.pallas_call(
        body,
        out_shape=(jax.ShapeDtypeStruct((b, seq, d), F32), jax.ShapeDtypeStruct((4, dg, dg), F32),
                   jax.ShapeDtypeStruct((1, d), F32)),
        grid=(4, b),
        in_specs=[pl.BlockSpec((None, seq, dg), lambda g, i: (i, 0, g)),
                  pl.BlockSpec((None, seq, dg), lambda g, i: (i, 0, g)),
                  pl.BlockSpec((None, dg, dg), lambda g, i: (g, 0, 0)),
                  pl.BlockSpec((1, dg), lambda g, i: (0, g))],
        out_specs=(pl.BlockSpec((None, seq, dg), lambda g, i: (i, 0, g)),
                   pl.BlockSpec((None, dg, dg), lambda g, i: (g, 0, 0)),
                   pl.BlockSpec((1, dg), lambda g, i: (0, g))),
        scratch_shapes=[pltpu.VMEM((dg, dg), F32)],
        compiler_params=_params(("parallel", "arbitrary")), name=name)(h, dout, pw, scale)


def _head_of(channel):
    return jnp.right_shift(channel, HEAD_DIM.bit_length() - 1)


def _ssd_consts(gw):
    q = CHUNK
    row = lax.broadcasted_iota(jnp.int32, (q, q), 0)
    col = lax.broadcasted_iota(jnp.int32, (q, q), 1)
    tril = (row >= col).astype(BF16)
    triu = (row <= col).astype(BF16)
    e = (_head_of(lax.broadcasted_iota(jnp.int32, (LANES, gw), 1))
         == lax.broadcasted_iota(jnp.int32, (LANES, gw), 0)).astype(BF16)
    et = (_head_of(lax.broadcasted_iota(jnp.int32, (gw, LANES), 0))
          == lax.broadcasted_iota(jnp.int32, (gw, LANES), 1)).astype(BF16)
    return row, col, tril, triu, e, et


def _ssd_common(dtr, dtb, alog, gw):
    q = CHUNK
    row, col, tril, triu, e, et = _ssd_consts(gw)
    dt = _softplus(dtr + dtb)
    a_row = -jnp.exp(alog)
    acum = _sel_left(tril, dt * a_row)
    ac_last = jnp.sum(jnp.where(row == q - 1, acum, 0.0), axis=0, keepdims=True)
    eac = jnp.exp(acum)
    de = jnp.exp(ac_last - acum)
    e2 = jnp.concatenate([e, e], axis=0)
    expand = _sel_right(jnp.concatenate([dt, eac, de], axis=0), e2, 2)
    dt_x, eac_x, de_x = expand[0:q], expand[q:2 * q], expand[2 * q:3 * q]
    acum_t = acum.T
    cd_col = jnp.exp(acum_t[:, q - 1:q])
    et3 = jnp.concatenate([et, et, et], axis=1)
    cdmat = _nn(et3, jnp.concatenate(_split(jnp.broadcast_to(cd_col, (LANES, D_STATE)), 3), axis=0))
    consts = dict(row=row, col=col, tril=tril, triu=triu, e=e, et=et)
    return dt, a_row, acum, acum_t, ac_last, eac, de, dt_x, eac_x, de_x, cdmat, consts


def _decay(acum, acum_t, j, row, col):
    diff = acum[:, j:j + 1] - acum_t[j:j + 1, :]
    return jnp.exp(jnp.where(row >= col, diff, -1e30))


def _ssd_fwd(xc, zx, dtb, alog, dskip, nw, d_inner, name):
    b, seq, xbc = xc.shape
    q = CHUNK
    nc = seq // q
    gw = d_inner // N_GROUPS
    nh = gw // HEAD_DIM
    xb0 = d_inner // D_STATE
    xc0 = xb0 + N_GROUPS
    dt0 = (d_inner + xbc) // LANES

    nb = max(n for n in (4, 2, 1) if b % n == 0)

    def body(x_ref, b_ref, c_ref, z_ref, dtr_ref, dtb_ref, al_ref, dsk_ref, nw_ref, y_ref, yn_ref, st_ref, s_ref):
        @pl.when(pl.program_id(2) == 0)
        def _():
            s_ref[...] = jnp.zeros_like(s_ref)

        for s in range(nb):
            one(s, x_ref.at[s], b_ref.at[s], c_ref.at[s], z_ref.at[s], dtr_ref.at[s], dtb_ref, al_ref, dsk_ref, nw_ref,
                y_ref.at[s], yn_ref.at[s], st_ref.at[s], s_ref.at[s])

    def one(s, x_ref, b_ref, c_ref, z_ref, dtr_ref, dtb_ref, al_ref, dsk_ref, nw_ref, y_ref, yn_ref, st_ref, s_ref):
        prev = s_ref[...]
        st_ref[...] = prev
        x = x_ref[...]
        bm = b_ref[...].astype(BF16)
        cm = c_ref[...].astype(BF16)
        (dt, a_row, acum, acum_t, ac_last, eac, de, dt_x, eac_x, de_x, cdmat, k) = _ssd_common(
            dtr_ref[...], dtb_ref[0:1, :], al_ref[0:1, :], gw)
        xdt = x * dt_x
        xdt_b = xdt.astype(BF16)
        cb = _nt(cm, bm)
        half = _head_of(lax.broadcasted_iota(jnp.int32, (q, LANES), 1))
        pairs = []
        for j in range(nh):
            pc = (j // 2) * LANES
            m = (cb * _decay(acum, acum_t, j, k["row"], k["col"])).astype(BF16)
            yj = jnp.where(half == j % 2, _nn(m, xdt_b[:, pc:pc + LANES]), 0.0)
            if j % 2 == 0:
                pairs.append(yj)
            else:
                pairs[-1] = pairs[-1] + yj
        prev_b = prev.astype(BF16)
        y = dsk_ref[0:1, :] * x + jnp.concatenate(pairs, axis=1) + eac_x * _nt(cm, prev_b)
        s_ref[...] = cdmat * prev + _tn((xdt * de_x).astype(BF16), bm)
        y_ref[...] = y
        z = z_ref[...]
        yg = y * (z * _sigmoid(z))
        r = lax.rsqrt(jnp.mean(yg * yg, axis=-1, keepdims=True) + EPS)
        yn_ref[...] = ((yg * r) * nw_ref[0:1, :]).astype(BF16)

    par = lambda w: pl.BlockSpec((None, 8, w), lambda i, g, c: (g, 0, 0))
    return pl.pallas_call(
        body,
        out_shape=(jax.ShapeDtypeStruct((b, seq, d_inner), F32), jax.ShapeDtypeStruct((b, seq, d_inner), BF16),
                   jax.ShapeDtypeStruct((b, nc, N_GROUPS, gw, D_STATE), F32)),
        grid=(b // nb, N_GROUPS, nc),
        in_specs=[pl.BlockSpec((nb, q, gw), lambda i, g, c: (i, c, g)),
                  pl.BlockSpec((nb, q, D_STATE), lambda i, g, c: (i, c, xb0 + g)),
                  pl.BlockSpec((nb, q, D_STATE), lambda i, g, c: (i, c, xc0 + g)),
                  pl.BlockSpec((nb, q, gw), lambda i, g, c: (i, c, g)),
                  pl.BlockSpec((nb, q, LANES), lambda i, g, c: (i, c, dt0 + g)),
                  par(LANES), par(LANES), par(gw), par(gw)],
        out_specs=(pl.BlockSpec((nb, q, gw), lambda i, g, c: (i, c, g)),
                   pl.BlockSpec((nb, q, gw), lambda i, g, c: (i, c, g)),
                   pl.BlockSpec((nb, None, None, gw, D_STATE), lambda i, g, c: (i, c, g, 0, 0))),
        scratch_shapes=[pltpu.VMEM((nb, gw, D_STATE), F32)],
        compiler_params=_params(("parallel", "parallel", "arbitrary")), name=name,
    )(xc, xc, xc, zx, zx, dtb, alog, dskip, nw)


def _ssd_bwd(xc, zx, y, dyn, st, dtb, alog, dskip, nw, d_inner, name):
    b, seq, xbc = xc.shape
    q = CHUNK
    nc = seq // q
    gw = d_inner // N_GROUPS
    nh = gw // HEAD_DIM
    xb0 = d_inner // D_STATE
    xc0 = xb0 + N_GROUPS
    dt0 = (d_inner + xbc) // LANES

    nb = max(n for n in (4, 2, 1) if b % n == 0)

    def body(x_ref, b_ref, c_ref, z_ref, dtr_ref, y_ref, g_ref, st_ref, dtb_ref, al_ref, dsk_ref, nw_ref,
             dz_ref, dx_ref, db_ref, dc_ref, ddt_ref, dnw_ref, dd_ref, dal_ref, dbias_ref,
             ds_ref, colbuf, rowbuf):
        first = jnp.logical_and(pl.program_id(1) == 0, pl.program_id(2) == 0)

        @pl.when(pl.program_id(2) == 0)
        def _():
            ds_ref[...] = jnp.zeros_like(ds_ref)

        sums = [one(x_ref.at[s], b_ref.at[s], c_ref.at[s], z_ref.at[s], dtr_ref.at[s], y_ref.at[s], g_ref.at[s],
                    st_ref.at[s], dtb_ref, al_ref, dsk_ref, nw_ref, dz_ref.at[s], dx_ref.at[s], db_ref.at[s],
                    dc_ref.at[s], ddt_ref.at[s], ds_ref.at[s], colbuf.at[s], rowbuf.at[s]) for s in range(nb)]
        dnw, dd, dal, dbias = [functools.reduce(lambda p, r: p + r, [sm[i] for sm in sums]) for i in range(4)]

        @pl.when(first)
        def _():
            dnw_ref[...] = jnp.broadcast_to(dnw, (8, gw))
            dd_ref[...] = dd
            dal_ref[...] = jnp.broadcast_to(dal, (8, LANES))
            dbias_ref[...] = jnp.broadcast_to(dbias, (8, LANES))

        @pl.when(jnp.logical_not(first))
        def _():
            dnw_ref[...] += jnp.broadcast_to(dnw, (8, gw))
            dd_ref[...] += dd
            dal_ref[...] += jnp.broadcast_to(dal, (8, LANES))
            dbias_ref[...] += jnp.broadcast_to(dbias, (8, LANES))

    def one(x_ref, b_ref, c_ref, z_ref, dtr_ref, y_ref, g_ref, st_ref, dtb_ref, al_ref, dsk_ref, nw_ref,
            dz_ref, dx_ref, db_ref, dc_ref, ddt_ref, ds_ref, colbuf, rowbuf):
        x = x_ref[...]
        bm = b_ref[...].astype(BF16)
        cm = c_ref[...].astype(BF16)
        z = z_ref[...]
        y = y_ref[...]
        prev = st_ref[...]
        dtr = dtr_ref[...] + dtb_ref[0:1, :]
        (dt, a_row, acum, acum_t, ac_last, eac, de, dt_x, eac_x, de_x, cdmat, k) = _ssd_common(
            dtr_ref[...], dtb_ref[0:1, :], al_ref[0:1, :], gw)
        row, col = k["row"], k["col"]
        et2 = jnp.concatenate([k["et"], k["et"]], axis=0)

        sz = _sigmoid(z)
        silu_z = z * sz
        yg = y * silu_z
        r = lax.rsqrt(jnp.mean(yg * yg, axis=-1, keepdims=True) + EPS)
        xh = yg * r
        dyn = g_ref[...]
        gh = dyn * nw_ref[0:1, :]
        dyg = r * (gh - xh * jnp.mean(gh * xh, axis=-1, keepdims=True))
        dnw = jnp.sum(dyn * xh, axis=0, keepdims=True)
        g = dyg * silu_z
        dz_ref[...] = (dyg * y * (sz * (1.0 + z * (1.0 - sz)))).astype(BF16)
        dd = _sel_right(jnp.broadcast_to(jnp.sum(g * x, axis=0, keepdims=True), (8, gw)), et2, 2)

        xdt = x * dt_x
        xdt_b = xdt.astype(BF16)
        g_b = g.astype(BF16)
        prev_b = prev.astype(BF16)
        cb = _nt(cm, bm)

        cp = _nt(cm, prev_b)
        ge = g * eac_x
        dac = _sel_right(ge * cp, et2, 2)
        ge_b = ge.astype(BF16)
        dcm = _nn(ge_b, prev_b)
        dprev = _tn(ge_b, cm)

        colbuf[...] = jnp.zeros_like(colbuf)
        rowbuf[...] = jnp.zeros_like(rowbuf)
        dcb = jnp.zeros((q, q), F32)
        half = _head_of(lax.broadcasted_iota(jnp.int32, (q, LANES), 1))
        pairs = []
        for j in range(nh):
            pc = (j // 2) * LANES
            dec = _decay(acum, acum_t, j, row, col)
            m = cb * dec
            gj = jnp.where(half == j % 2, g[:, pc:pc + LANES], 0.0).astype(BF16)
            dm = _nt(gj, xdt_b[:, pc:pc + LANES])
            w = dm * m
            colbuf[:, j:j + 1] = jnp.sum(w, axis=1, keepdims=True)
            rowbuf[j:j + 1, :] = jnp.sum(w, axis=0, keepdims=True)
            dcb = dcb + dm * dec
            dj = jnp.where(half == j % 2, _tn(m.astype(BF16), g_b[:, pc:pc + LANES]), 0.0)
            if j % 2 == 0:
                pairs.append(dj)
            else:
                pairs[-1] = pairs[-1] + dj
        dxdt = jnp.concatenate(pairs, axis=1)
        dcb_b = dcb.astype(BF16)
        dcm = dcm + _nn(dcb_b, bm)
        dbm = _tn(dcb_b, cm)

        ds = ds_ref[...]
        ds_b = ds.astype(BF16)
        u = _nt(bm, ds_b)
        dxdt = dxdt + u * de_x
        dde = _sel_right(u * xdt, et2, 2)
        dbm = dbm + _nn((xdt * de_x).astype(BF16), ds_b)
        pm = jnp.concatenate(_split(ds * prev, 2), axis=1)
        t2 = _tn(pm, k["et"])
        dcd_row = jnp.sum(t2[0:D_STATE] + t2[D_STATE:2 * D_STATE], axis=0, keepdims=True)
        last = dcd_row * jnp.exp(ac_last) + jnp.sum(dde * de, axis=0, keepdims=True)
        dac = dac + colbuf[...] - rowbuf[...].T - dde * de + jnp.where(row == q - 1, last, 0.0)
        ds_ref[...] = cdmat * ds + dprev

        dadt = _sel_left(k["triu"], dac)
        ddt = _sel_right(dxdt * x, et2, 2) + dadt * a_row
        dal = jnp.sum(dadt * dt, axis=0, keepdims=True) * a_row
        lane = lax.broadcasted_iota(jnp.int32, (q, LANES), 1)
        ddtr = jnp.where(lane < nh, ddt * _sigmoid(dtr), 0.0)
        ddt_ref[...] = ddtr.astype(BF16)
        dbias = jnp.sum(ddtr, axis=0, keepdims=True)
        dx_ref[...] = dxdt * dt_x + dsk_ref[0:1, :] * g
        db_ref[...] = dbm
        dc_ref[...] = dcm
        return dnw, dd, dal, dbias

    rc = lambda c: nc - 1 - c
    par = lambda w: pl.BlockSpec((None, 8, w), lambda g, i, c: (g, 0, 0))
    blk = lambda w: pl.BlockSpec((nb, q, w), lambda g, i, c: (i, rc(c), g))
    return pl.pallas_call(
        body,
        out_shape=(jax.ShapeDtypeStruct((b, seq, d_inner), BF16),
                   jax.ShapeDtypeStruct((b, seq, d_inner), F32),
                   jax.ShapeDtypeStruct((b, seq, N_GROUPS * D_STATE), F32),
                   jax.ShapeDtypeStruct((b, seq, N_GROUPS * D_STATE), F32),
                   jax.ShapeDtypeStruct((b, seq, N_GROUPS * LANES), BF16),
                   jax.ShapeDtypeStruct((N_GROUPS, 8, gw), F32),
                   jax.ShapeDtypeStruct((N_GROUPS, 8, LANES), F32),
                   jax.ShapeDtypeStruct((N_GROUPS, 8, LANES), F32),
                   jax.ShapeDtypeStruct((N_GROUPS, 8, LANES), F32)),
        grid=(N_GROUPS, b // nb, nc),
        in_specs=[blk(gw),
                  pl.BlockSpec((nb, q, D_STATE), lambda g, i, c: (i, rc(c), xb0 + g)),
                  pl.BlockSpec((nb, q, D_STATE), lambda g, i, c: (i, rc(c), xc0 + g)),
                  blk(gw),
                  pl.BlockSpec((nb, q, LANES), lambda g, i, c: (i, rc(c), dt0 + g)),
                  blk(gw), blk(gw),
                  pl.BlockSpec((nb, None, None, gw, D_STATE), lambda g, i, c: (i, rc(c), g, 0, 0)),
                  par(LANES), par(LANES), par(gw), par(gw)],
        out_specs=(blk(gw), blk(gw), blk(D_STATE), blk(D_STATE), blk(LANES),
                   par(gw), par(LANES), par(LANES), par(LANES)),
        scratch_shapes=[pltpu.VMEM((nb, gw, D_STATE), F32), pltpu.VMEM((nb, q, LANES), F32),
                        pltpu.VMEM((nb, LANES, q), F32)],
        compiler_params=_params(("parallel", "arbitrary", "arbitrary")), name=name,
    )(xc, xc, xc, zx, zx, y, dyn, st, dtb, alog, dskip, nw)


def _adamw(w, g, m, v, name):
    rows, cols = w.shape
    tr = rows
    for cand in (512, 256, 128, 64, 32, 16, 8):
        if rows % cand == 0 and cand * cols * 4 <= 2 * 1024 * 1024:
            tr = cand
            break
    c1 = 1.0 - ADAM_B1 ** ADAM_STEP
    c2 = 1.0 - ADAM_B2 ** ADAM_STEP

    def body(w_ref, g_ref, m_ref, v_ref, d_ref, mo_ref, vo_ref):
        gv = g_ref[...]
        mn = ADAM_B1 * m_ref[...] + (1.0 - ADAM_B1) * gv
        vn = ADAM_B2 * v_ref[...] + (1.0 - ADAM_B2) * (gv * gv)
        mo_ref[...] = mn
        vo_ref[...] = vn
        d_ref[...] = -ADAM_LR * ((mn / c1) / (jnp.sqrt(vn / c2) + ADAM_EPS) + ADAM_WD * w_ref[...])

    spec = pl.BlockSpec((tr, cols), lambda i: (i, 0))
    shp = jax.ShapeDtypeStruct((rows, cols), F32)
    return pl.pallas_call(body, out_shape=(shp, shp, shp), grid=(rows // tr,), in_specs=[spec] * 4,
                          out_specs=(spec,) * 3, compiler_params=_params(("parallel",)), name=name)(w, g, m, v)


def _pick_rows(rows, row_bytes, limit=1 << 20):
    for cand in (2048, 1024, 512, 256, 128, 64, 32, 16):
        if rows % cand == 0 and cand * row_bytes <= limit:
            return cand
    return rows


def _as3d(a, lead):
    return a.reshape(a.shape[:lead] + (-1, a.shape[-1]))


def _pair_sum(g, got, core, name):
    h = got.shape[0]
    g3, got3 = _as3d(g, 1), _as3d(got, 1)
    _, rows, cols = got3.shape
    tr = _pick_rows(rows, cols * 4)

    def body(c_ref, g_ref, r_ref, o_ref):
        o_ref[...] = (g_ref[...] + r_ref[...]).astype(BF16)

    out = pl.pallas_call(
        body, out_shape=jax.ShapeDtypeStruct(got3.shape, BF16),
        grid_spec=pltpu.PrefetchScalarGridSpec(
            num_scalar_prefetch=1, grid=(h, rows // tr),
            in_specs=[pl.BlockSpec((None, tr, cols), lambda l, i, c_ref: (c_ref[0] * h + l, i, 0)),
                      pl.BlockSpec((None, tr, cols), lambda l, i, c_ref: (l, i, 0))],
            out_specs=pl.BlockSpec((None, tr, cols), lambda l, i, c_ref: (l, i, 0))),
        compiler_params=_params(("parallel", "parallel")), name=name)(core, g3, got3)
    return out.reshape(got.shape)


def _sum4(q, core, name):
    q4 = _as3d(q, 2)
    _, h, rows, cols = q4.shape
    tr = _pick_rows(rows, cols * 4)

    def body(c_ref, q0, q1, q2, q3, o_ref):
        o_ref[...] = ((q0[...].astype(F32) + q1[...].astype(F32)) + q2[...].astype(F32)) + q3[...].astype(F32)

    out = pl.pallas_call(
        body, out_shape=jax.ShapeDtypeStruct((2 * h, rows, cols), F32),
        grid_spec=pltpu.PrefetchScalarGridSpec(
            num_scalar_prefetch=1, grid=(h, rows // tr),
            in_specs=[pl.BlockSpec((None, None, tr, cols), lambda l, i, c_ref, k=k: (k, l, i, 0))
                      for k in range(N_CHIPS)],
            out_specs=pl.BlockSpec((None, tr, cols), lambda l, i, c_ref: (c_ref[0] * h + l, i, 0))),
        compiler_params=_params(("parallel", "parallel")), name=name)(core, q4, q4, q4, q4)
    return out.reshape((2 * h,) + q.shape[2:])


def _coords():
    return lax.axis_index("x"), lax.axis_index("y"), lax.axis_index("c")


def _other_chips(x, y):
    return [(1 - x, y), (x, 1 - y), (1 - x, 1 - y)]


def _allgather_halves(src, name):
    rows, cols = src.shape

    def body(x_ref, o_ref, send, recv, local):
        x, y, c = _coords()
        sib = (x, y, 1 - c)
        chips = _other_chips(x, y)

        def slot(h, cx, cy):
            return o_ref.at[h, 2 * cx + cy]

        def copy(kk, dst, to, src_ref):
            return pltpu.make_async_remote_copy(src_ref=src_ref, dst_ref=dst, send_sem=send.at[kk],
                                                recv_sem=recv.at[kk], device_id=to, device_id_type=MESH)

        mine = pltpu.make_async_copy(x_ref, slot(c, x, y), local)
        mine.start()
        first = [copy(0, slot(c, x, y), sib, x_ref)]
        first += [copy(1 + j, slot(c, x, y), (*chip, c), x_ref) for j, chip in enumerate(chips)]
        for cp in first:
            cp.start()
        passed = [copy(4 + j, slot(c, *chip), sib, slot(c, *chip)) for j, chip in enumerate(chips)]
        for j, chip in enumerate(chips):
            copy(1 + j, slot(c, *chip), (x, y, c), x_ref).wait_recv()
            passed[j].start()
        copy(0, slot(1 - c, x, y), (x, y, c), x_ref).wait_recv()
        for j, chip in enumerate(chips):
            copy(4 + j, slot(1 - c, *chip), (x, y, c), x_ref).wait_recv()
        for cp in first + passed:
            cp.wait_send()
        mine.wait()

    return pl.pallas_call(
        body, out_shape=jax.ShapeDtypeStruct((2, N_CHIPS, rows, cols), src.dtype),
        in_specs=[ANY], out_specs=ANY,
        scratch_shapes=[pltpu.SemaphoreType.DMA((7,)), pltpu.SemaphoreType.DMA((7,)), pltpu.SemaphoreType.DMA],
        name=name)(src)


MIXW = (("ssd_w_in", None), ("ssd_w_out", 0), ("pool_w", 1))
FFNW = (("ffn_w_up", 1), ("ffn_w_down", 0))


def _chip_window(axis, ref, layers, k):
    if axis is None:
        return ref.at[layers, k]
    n = ref.shape[1 + axis] // N_CHIPS
    sl = pl.ds(pl.multiple_of(k * n, LANES if 1 + axis == len(ref.shape) - 1 else 8), n)
    idx = [layers] + [slice(None)] * (len(ref.shape) - 1)
    idx[1 + axis] = sl
    return ref.at[tuple(idx)]


def _full_shape(axis, shard_shape):
    if axis is None:
        return (shard_shape[0], N_CHIPS) + tuple(shard_shape[1:])
    full = list(shard_shape)
    full[1 + axis] *= N_CHIPS
    return tuple(full)


HBM_SPEC = pl.BlockSpec(memory_space=pltpu.HBM)
SEM_SPEC = pl.BlockSpec(memory_space=pltpu.SEMAPHORE)


def _dma_sems(count):
    return pltpu.SemaphoreType.DMA((max(count, 1),))


def _wait_for(copy, kind):
    if kind == "recv":
        copy.wait_recv()
    elif kind == "send":
        copy.wait_send()
    else:
        copy.wait()


def _comm_fused(stages, counts, srcs, lands, name, inplace=False):
    ns, nl, k = len(srcs), len(lands), len(stages)

    def body(*refs):
        src_refs = refs[:ns]
        land_refs = refs[ns + (nl if inplace else 0):ns + (nl if inplace else 0) + nl]
        sem_refs = refs[len(refs) - 3 * k:]
        for s, stage_fn in enumerate(stages):
            starts, waits = stage_fn(src_refs, land_refs, tuple(sem_refs[3 * s:3 * s + 3]))
            for cp in starts:
                cp.start()
            for cp, kind in waits:
                _wait_for(cp, kind)

    scratch = []
    for cnt in counts:
        scratch += [_dma_sems(c) for c in cnt]
    outs = pl.pallas_call(
        body, out_shape=tuple(jax.ShapeDtypeStruct(a.shape, a.dtype) for a in lands),
        in_specs=[ANY] * (ns + (nl if inplace else 0)), out_specs=(ANY,) * nl,
        input_output_aliases={ns + i: i for i in range(nl)} if inplace else {},
        scratch_shapes=scratch, name=name)(*srcs, *(lands if inplace else ()))
    return list(outs)


class _SplitComm:
    def __init__(self, stages, counts, srcs, lands, name):
        self.stages, self.counts, self.name = stages, counts, name
        self.ns = len(srcs)
        self.data = [pltpu.with_memory_space_constraint(a, pltpu.HBM) for a in list(srcs) + list(lands)]
        self.sems = None
        self.step = 0

    def advance(self, after=None):
        i, k, nd, ns = self.step, len(self.stages), len(self.data), self.ns
        first, last = i == 0, i == k
        stages = self.stages

        def body(*refs):
            data = refs[:nd]
            pos = nd
            if not first:
                old = tuple(refs[pos:pos + 3])
                pos += 4
            if not last:
                new = tuple(refs[pos:pos + 3])
            if not first:
                for cp, kind in stages[i - 1](data[:ns], data[ns:], old)[1]:
                    _wait_for(cp, kind)
            if not last:
                for cp in stages[i](data[:ns], data[ns:], new)[0]:
                    cp.start()
                refs[len(refs) - 1][...] = jnp.zeros((8, LANES), F32)

        args = list(self.data)
        in_specs = [HBM_SPEC] * nd
        if not first:
            args += list(self.sems) + [after]
            in_specs += [SEM_SPEC] * 3 + [ANY]
        out_shape, out_specs = [], []
        if not last:
            out_shape += [_dma_sems(c) for c in self.counts[i]]
            out_specs += [SEM_SPEC] * 3
        out_shape += [pltpu.HBM(a.shape, a.dtype) for a in self.data]
        out_specs += [HBM_SPEC] * nd
        if not last:
            out_shape.append(jax.ShapeDtypeStruct((8, LANES), F32))
            out_specs.append(pl.BlockSpec(memory_space=pltpu.VMEM))
        off = 0 if last else 3
        outs = pl.pallas_call(
            body, out_shape=tuple(out_shape), in_specs=in_specs, out_specs=tuple(out_specs),
            input_output_aliases={d: off + d for d in range(nd)},
            compiler_params=pltpu.CompilerParams(has_side_effects=pltpu.SideEffectType.DATAFLOW_SIDE_EFFECTING),
            name=f"{self.name}_{i}")(*args)
        self.sems = None if last else outs[:3]
        self.data = list(outs[off:off + nd])
        self.step += 1
        return None if last else outs[len(outs) - 1]

    def lands(self):
        return self.data[self.ns:]


def _gather_stages(spec):
    n = len(spec)

    def parts(srcs, lands):
        x, y, c = _coords()
        out = []
        for w, (_, axis) in enumerate(spec):
            h = srcs[w].shape[0] // 2
            mine, theirs = pl.ds(c * h, h), pl.ds((1 - c) * h, h)
            out.append((srcs[w].at[mine], lambda layers, k, w=w, axis=axis: _chip_window(axis, lands[w], layers, k),
                        mine, theirs))
        return x, y, c, 2 * x + y, (x, y, 1 - c), _other_chips(x, y), out

    def remote(src, dst, send, recv, idx, to):
        return pltpu.make_async_remote_copy(src_ref=src, dst_ref=dst, send_sem=send.at[idx], recv_sem=recv.at[idx],
                                            device_id=to, device_id_type=MESH)

    def stage0(srcs, lands, sems):
        send, recv, local = sems
        x, y, c, me, sib, chips, ps = parts(srcs, lands)
        starts, waits = [], []
        for w, (src, dst, mine, theirs) in enumerate(ps):
            lc = pltpu.make_async_copy(src, dst(mine, me), local.at[w])
            first = [remote(src, dst(mine, me), send, recv, 4 * w, sib)]
            first += [remote(src, dst(mine, me), send, recv, 4 * w + 1 + j, (cx, cy, c)) for j, (cx, cy) in enumerate(chips)]
            starts += [lc] + first
            waits.append((remote(src, dst(theirs, me), send, recv, 4 * w, (x, y, c)), "recv"))
            waits += [(remote(src, dst(mine, 2 * cx + cy), send, recv, 4 * w + 1 + j, (x, y, c)), "recv")
                      for j, (cx, cy) in enumerate(chips)]
            waits += [(cp, "send") for cp in first] + [(lc, "local")]
        return starts, waits

    def stage1(srcs, lands, sems):
        send, recv, _ = sems
        x, y, c, me, sib, chips, ps = parts(srcs, lands)
        starts, waits = [], []
        for w, (src, dst, mine, theirs) in enumerate(ps):
            for j, (cx, cy) in enumerate(chips):
                blk = dst(mine, 2 * cx + cy)
                fwd = remote(blk, blk, send, recv, 3 * w + j, sib)
                starts.append(fwd)
                waits.append((remote(src, dst(theirs, 2 * cx + cy), send, recv, 3 * w + j, (x, y, c)), "recv"))
                waits.append((fwd, "send"))
        return starts, waits

    return [stage0, stage1], [(4 * n, 4 * n, n), (3 * n, 3 * n, 0)]


def _swap_stages(spec):
    n = len(spec)

    def stage(srcs, lands, sems):
        send, recv, _ = sems
        x, y, c = _coords()
        starts, waits = [], []
        for w in range(n):
            h = srcs[w].shape[0] // 2
            cp = pltpu.make_async_remote_copy(src_ref=srcs[w].at[pl.ds((1 - c) * h, h)], dst_ref=lands[w],
                                              send_sem=send.at[w], recv_sem=recv.at[w],
                                              device_id=(x, y, 1 - c), device_id_type=MESH)
            starts.append(cp)
            waits += [(cp, "recv"), (cp, "send")]
        return starts, waits

    return [stage], [(n, n, 0)]


def _scatter_stages(spec):
    n = len(spec)

    def stage(srcs, lands, sems):
        send, recv, local = sems
        x, y, c = _coords()
        me = 2 * x + y
        starts, waits = [], []
        for w, (_, axis) in enumerate(spec):
            layers = pl.ds(0, srcs[w].shape[0])
            own = _chip_window(axis, srcs[w], layers, me)
            lc = pltpu.make_async_copy(own, lands[w].at[me], local.at[w])
            starts.append(lc)
            for j, (cx, cy) in enumerate(_other_chips(x, y)):
                cp = pltpu.make_async_remote_copy(src_ref=_chip_window(axis, srcs[w], layers, 2 * cx + cy),
                                                  dst_ref=lands[w].at[me], send_sem=send.at[3 * w + j],
                                                  recv_sem=recv.at[3 * w + j], device_id=(cx, cy, c), device_id_type=MESH)
                starts.append(cp)
                waits.append((pltpu.make_async_remote_copy(
                    src_ref=own, dst_ref=lands[w].at[2 * cx + cy], send_sem=send.at[3 * w + j], recv_sem=recv.at[3 * w + j],
                    device_id=(x, y, c), device_id_type=MESH), "recv"))
                waits.append((cp, "send"))
            waits.append((lc, "local"))
        return starts, waits

    return [stage], [(3 * n, 3 * n, n)]


def _share_stages(spec):
    n = len(spec)

    def stage(srcs, lands, sems):
        send, recv, _ = sems
        x, y, c = _coords()
        starts, waits = [], []
        for w in range(n):
            h = lands[w].shape[0] // 2
            mine, theirs = lands[w].at[pl.ds(c * h, h)], lands[w].at[pl.ds((1 - c) * h, h)]
            cp = pltpu.make_async_remote_copy(src_ref=mine, dst_ref=mine, send_sem=send.at[w], recv_sem=recv.at[w],
                                              device_id=(x, y, 1 - c), device_id_type=MESH)
            starts.append(cp)
            waits.append((pltpu.make_async_remote_copy(src_ref=theirs, dst_ref=theirs, send_sem=send.at[w],
                                                       recv_sem=recv.at[w], device_id=(x, y, c), device_id_type=MESH),
                          "recv"))
            waits.append((cp, "send"))
        return starts, waits

    return [stage], [(n, n, 0)]


def _shard_of(p, axis):
    if axis is None:
        return (p.shape[0],) + tuple(p.shape[2:])
    s = list(p.shape)
    s[1 + axis] //= N_CHIPS
    return tuple(s)


def _reduce_begin(spec, gs, core, tag):
    stages, counts = _swap_stages(spec)
    got = _comm_fused(stages, counts, gs,
                      [jax.ShapeDtypeStruct((g.shape[0] // 2,) + g.shape[1:], g.dtype) for g in gs], "swap_" + tag)
    pair = [_pair_sum(a, r, core, "pair_sum_" + n) for a, r, (n, _) in zip(gs, got, spec)]
    stages, counts = _scatter_stages(spec)
    lands = [lax.empty((N_CHIPS,) + _shard_of(p, axis), p.dtype) for p, (_, axis) in zip(pair, spec)]
    comm = _SplitComm(stages, counts, pair, lands, "scatter_" + tag)
    return comm, comm.advance()


def _reduce_finish(spec, comm, core, tag, after):
    comm.advance(after=after)
    halves = [_sum4(q, core, "sum4_" + n) for q, (n, _) in zip(comm.lands(), spec)]
    stages, counts = _share_stages(spec)
    return _comm_fused(stages, counts, [], halves, "share_" + tag, inplace=True)


def _allreduce_small(vec, name, after=()):
    rows, cols = vec.shape
    after = list(after)

    def body(x_ref, *rest):
        o_ref, buf, send, recv = rest[len(after):]
        x, y, c = _coords()
        me = 4 * x + 2 * y + c
        buf[me] = x_ref[...]
        cps = []
        for kk in range(1, 8):
            dx, dy, dc = (kk >> 2) & 1, (kk >> 1) & 1, kk & 1
            to = (1 - x if dx else x, 1 - y if dy else y, 1 - c if dc else c)
            cp = pltpu.make_async_remote_copy(src_ref=x_ref, dst_ref=buf.at[me], send_sem=send.at[kk - 1],
                                              recv_sem=recv.at[kk - 1], device_id=to, device_id_type=MESH)
            cp.start()
            cps.append((cp, 4 * to[0] + 2 * to[1] + to[2]))
        for kk, (cp, frm) in enumerate(cps):
            pltpu.make_async_remote_copy(src_ref=x_ref, dst_ref=buf.at[frm], send_sem=send.at[kk],
                                         recv_sem=recv.at[kk], device_id=(x, y, c), device_id_type=MESH).wait_recv()
        for cp, _ in cps:
            cp.wait_send()
        acc = buf[0]
        for kk in range(1, 8):
            acc = acc + buf[kk]
        o_ref[...] = acc

    vm = pl.BlockSpec(memory_space=pltpu.VMEM)
    return pl.pallas_call(
        body, out_shape=jax.ShapeDtypeStruct((rows, cols), F32), in_specs=[vm] + [ANY] * len(after), out_specs=vm,
        scratch_shapes=[pltpu.VMEM((8, rows, cols), F32), pltpu.SemaphoreType.DMA((7,)), pltpu.SemaphoreType.DMA((7,))],
        compiler_params=_params(), name=name)(vec, *after)


SMALL = (("ssd_conv_w", 2), ("pool_scale", 1), ("ffn_conv_w", 2))
REPL = ("ssd_conv_b", "ssd_dt_bias", "ssd_a_log", "ssd_d", "ssd_norm_w", "ffn_conv_b",
        "norm_mix_pre", "norm_mix_post", "norm_ffn_pre", "norm_ffn_post")
WEIGHTS = ("ssd_w_in", "ssd_conv_w", "ssd_conv_b", "ssd_dt_bias", "ssd_a_log", "ssd_d", "ssd_norm_w", "ssd_w_out",
           "pool_w", "pool_scale", "ffn_w_up", "ffn_conv_w", "ffn_conv_b", "ffn_w_down", "norm_mix_pre",
           "norm_mix_post", "norm_ffn_pre", "norm_ffn_post")


def _flat_rows(n):
    unit = 2 * 16 * FLAT_COLS
    return 2 * 16 * ((n + unit - 1) // unit)


def _flatten_shards(arrs, dtype):
    flat = jnp.concatenate([a.astype(dtype).reshape(-1) for a in arrs])
    rows = _flat_rows(flat.shape[0])
    flat = jnp.pad(flat, (0, rows * FLAT_COLS - flat.shape[0]))
    return flat.reshape(2, rows // 2, FLAT_COLS)


def _unflatten_full(gathered, shard_shapes, axes):
    per_chip = jnp.swapaxes(gathered, 0, 1).reshape(N_CHIPS, -1)
    out, off = [], 0
    for shp, ax in zip(shard_shapes, axes):
        n = math.prod(shp)
        pieces = [per_chip[k, off:off + n].reshape(shp) for k in range(N_CHIPS)]
        out.append(jnp.concatenate(pieces, axis=ax))
        off += n
    return out


def kernel(x, ssd_w_in, ssd_conv_w, ssd_conv_b, ssd_dt_bias, ssd_a_log, ssd_d, ssd_norm_w, ssd_w_out, pool_w, pool_scale, ffn_w_up, ffn_conv_w, ffn_conv_b, ffn_w_down, norm_mix_pre, norm_mix_post, norm_ffn_pre, norm_ffn_post, loss_target, m_ssd_w_in, m_ssd_conv_w, m_ssd_conv_b, m_ssd_dt_bias, m_ssd_a_log, m_ssd_d, m_ssd_norm_w, m_ssd_w_out, m_pool_w, m_pool_scale, m_ffn_w_up, m_ffn_conv_w, m_ffn_conv_b, m_ffn_w_down, m_norm_mix_pre, m_norm_mix_post, m_norm_ffn_pre, m_norm_ffn_post, v_ssd_w_in, v_ssd_conv_w, v_ssd_conv_b, v_ssd_dt_bias, v_ssd_a_log, v_ssd_d, v_ssd_norm_w, v_ssd_w_out, v_pool_w, v_pool_scale, v_ffn_w_up, v_ffn_conv_w, v_ffn_conv_b, v_ffn_w_down, v_norm_mix_pre, v_norm_mix_post, v_norm_ffn_pre, v_norm_ffn_post):
    wts = dict(ssd_w_in=ssd_w_in, ssd_conv_w=ssd_conv_w, ssd_conv_b=ssd_conv_b, ssd_dt_bias=ssd_dt_bias,
               ssd_a_log=ssd_a_log, ssd_d=ssd_d, ssd_norm_w=ssd_norm_w, ssd_w_out=ssd_w_out, pool_w=pool_w,
               pool_scale=pool_scale, ffn_w_up=ffn_w_up, ffn_conv_w=ffn_conv_w, ffn_conv_b=ffn_conv_b,
               ffn_w_down=ffn_w_down, norm_mix_pre=norm_mix_pre, norm_mix_post=norm_mix_post,
               norm_ffn_pre=norm_ffn_pre, norm_ffn_post=norm_ffn_post)
    mom = dict(ssd_w_in=m_ssd_w_in, ssd_conv_w=m_ssd_conv_w, ssd_conv_b=m_ssd_conv_b, ssd_dt_bias=m_ssd_dt_bias,
               ssd_a_log=m_ssd_a_log, ssd_d=m_ssd_d, ssd_norm_w=m_ssd_norm_w, ssd_w_out=m_ssd_w_out, pool_w=m_pool_w,
               pool_scale=m_pool_scale, ffn_w_up=m_ffn_w_up, ffn_conv_w=m_ffn_conv_w, ffn_conv_b=m_ffn_conv_b,
               ffn_w_down=m_ffn_w_down, norm_mix_pre=m_norm_mix_pre, norm_mix_post=m_norm_mix_post,
               norm_ffn_pre=m_norm_ffn_pre, norm_ffn_post=m_norm_ffn_post)
    var = dict(ssd_w_in=v_ssd_w_in, ssd_conv_w=v_ssd_conv_w, ssd_conv_b=v_ssd_conv_b, ssd_dt_bias=v_ssd_dt_bias,
               ssd_a_log=v_ssd_a_log, ssd_d=v_ssd_d, ssd_norm_w=v_ssd_norm_w, ssd_w_out=v_ssd_w_out, pool_w=v_pool_w,
               pool_scale=v_pool_scale, ffn_w_up=v_ffn_w_up, ffn_conv_w=v_ffn_conv_w, ffn_conv_b=v_ffn_conv_b,
               ffn_w_down=v_ffn_w_down, norm_mix_pre=v_norm_mix_pre, norm_mix_post=v_norm_mix_post,
               norm_ffn_pre=v_norm_ffn_pre, norm_ffn_post=v_norm_ffn_post)

    bl, seq, d = x.shape
    t = bl * seq
    depth = norm_mix_pre.shape[0]
    n_ssd = ssd_w_out.shape[0]
    d_inner = ssd_w_out.shape[1] * N_CHIPS
    nheads = d_inner // HEAD_DIM
    hpg = nheads // N_GROUPS
    gw = d_inner // N_GROUPS
    xbc = ssd_conv_w.shape[2] * N_CHIPS
    f2 = ffn_w_up.shape[2] * N_CHIPS
    ff = f2 // 2
    dg = d // 4
    cy = lax.axis_index("c")
    chip = 2 * lax.axis_index("x") + lax.axis_index("y")

    small_shapes = [wts[n].shape for n, _ in SMALL]
    small_axes = [a for _, a in SMALL]
    small_flat = _flatten_shards([wts[n] for n, _ in SMALL], F32)
    small_half = lax.dynamic_index_in_dim(small_flat, cy, 0, keepdims=False)
    small_all = _allgather_halves(small_half, "gather_small")
    conv_w, p_scale, f_conv_w = _unflatten_full(small_all, small_shapes, small_axes)
    def full_shapes(spec, shards):
        return [jax.ShapeDtypeStruct(_full_shape(axis, s.shape), s.dtype) for s, (_, axis) in zip(shards, spec)]

    stages, counts = _gather_stages(MIXW)
    mix_shards = [wts[n].astype(BF16) for n, _ in MIXW]
    w_in_cm, w_out, w_pool = _comm_fused(stages, counts, mix_shards, full_shapes(MIXW, mix_shards), "gather_mixers")
    w_in = jnp.swapaxes(w_in_cm, 1, 2).reshape(n_ssd, d, -1)
    stages, counts = _gather_stages(FFNW)
    ffn_shards = [wts[n].astype(BF16) for n, _ in FFNW]
    ffn_gather = _SplitComm(stages, counts, ffn_shards + [w_pool],
                            [lax.empty(s.shape, s.dtype) for s in full_shapes(FFNW, ffn_shards)], "gather_ffn")
    gather_token = ffn_gather.advance()

    def pad_heads(a):
        lead = a.shape[:-1]
        a = a.reshape(lead + (N_GROUPS, hpg))
        a = jnp.pad(a, [(0, 0)] * len(lead) + [(0, 0), (0, LANES - hpg)])
        return a.reshape(lead + (N_GROUPS * LANES,))

    def unpad_heads(a):
        lead = a.shape[:-1]
        return a.reshape(lead + (N_GROUPS, LANES))[..., :hpg].reshape(lead + (nheads,))

    def group_rows(a, width):
        return jnp.broadcast_to(a.reshape(N_GROUPS, 1, width), (N_GROUPS, 8, width))

    w_in_p = jnp.concatenate([w_in[..., :d_inner + xbc], pad_heads(w_in[..., d_inner + xbc:])], axis=-1)
    zw = w_in_p.shape[-1]

    x2 = x.reshape(t, d)
    tgt2 = loss_target.reshape(t, d)
    w_up = w_down = None

    saved = []
    cur = x2
    tokens = []
    h = _norm_fwd(cur, norm_mix_pre[0:1], BF16, "norm_pre_b", after=[gather_token])
    for i in range(depth):
        j = i // 2
        sv = dict(x_in=cur)
        if i % 2 == 0:
            zx = _mm(h, w_in_p, "nn", F32, "mm_ssd_in", 2048, 512, d, b_layer=j).reshape(bl, seq, zw)
            xc, xpre = _ssd_conv_fwd(zx, conv_w[j], ssd_conv_b[j:j + 1], d_inner, "ssd_conv_fwd")
            dtb = group_rows(pad_heads(ssd_dt_bias[j]), LANES)
            alog = group_rows(pad_heads(ssd_a_log[j]), LANES)
            dskip = group_rows(jnp.repeat(ssd_d[j], HEAD_DIM), gw)
            nw = group_rows(ssd_norm_w[j], gw)
            y, yn, st = _ssd_fwd(xc, zx, dtb, alog, dskip, nw, d_inner, "ssd_fwd")
            if i == 0:
                tokens.append(ffn_gather.advance(after=yn))
            mix = _mm(yn.reshape(t, d_inner), w_out, "nn", F32, "mm_ssd_out", 512, 512, d_inner, b_layer=j)
            sv.update(h=h, zx=zx, xc=xc, xpre=xpre, y=y, yn=yn, st=st, dtb=dtb, alog=alog, dskip=dskip, nw=nw)
        else:
            mix = _pool_fwd(h.reshape(bl, seq, d), w_pool[j], p_scale[j:j + 1], "pool_fwd").reshape(t, d)
            sv.update(h=h)
        sv.update(mix=mix)
        mid, u = _norm_post_pre(mix, norm_mix_post[i:i + 1], cur, norm_ffn_pre[i:i + 1], BF16, "norm_post_pre_b",
                                after=tokens)
        tokens = []
        if i == 0:
            ffn_gather.advance(after=u)
            w_up, w_down = ffn_gather.lands()
        hpre = _mm(u, w_up, "nn", BF16, "mm_up", 2048, 512, d, b_layer=i).reshape(bl, seq, f2)
        act, pre_g, pre_v = _ffn_act_fwd(hpre, f_conv_w[i], ffn_conv_b[i:i + 1], "ffn_act_fwd")
        act = act.reshape(t, ff)
        fo = _mm(act, w_down, "nn", F32, "mm_down", 1024, 512, ff, b_layer=i)
        if i + 1 == depth:
            cur = _norm_fwd(fo, norm_ffn_post[i:i + 1], F32, "norm_post", resid=mid)
        elif i % 2 == 0:
            cur, h = _norm_post_pre(fo, norm_ffn_post[i:i + 1], mid, norm_mix_pre[i + 1:i + 2], F32, "norm_post_pre_f")
        else:
            cur, h = _norm_post_pre(fo, norm_ffn_post[i:i + 1], mid, norm_mix_pre[i + 1:i + 2], BF16, "norm_post_pre_b")
        sv.update(mid=mid, u=u, hpre=hpre, pre_g=pre_g, pre_v=pre_v, act=act, fo=fo)
        saved.append(sv)

    dcur, loss_part = _loss_head(cur, tgt2, "loss_head")

    g = {n: [None] * wts[n].shape[0] for n in WEIGHTS}
    gbuf = dict(up=lax.empty((depth, d, f2), F32), down=lax.empty((depth, ff, d), F32),
                out=lax.empty((n_ssd, d_inner, d), F32), win=lax.empty((n_ssd, d, zw), F32))
    core = cy.reshape(1).astype(jnp.int32)

    def mixer_bwd(i, dmid, behind=()):
        j = i // 2
        sv = saved[i]
        done = []
        if i % 2 == 0:
            dmix, g["norm_mix_post"][i] = _norm_bwd(sv["mix"], norm_mix_post[i:i + 1], dmid, BF16, "norm_bwd_b",
                                                    after=behind)
            dyn = _mm(dmix, w_out, "nt", F32, "mm_ssd_out_dx", 1024, 1024, d, b_layer=j)
            gbuf["out"], tok = _mm(sv["yn"].reshape(t, d_inner), dmix, "tn", F32, "mm_ssd_out_dw", 1024, 512, 2048,
                                   out_buf=(gbuf["out"], j))
            done.append(tok)
            dz, dxs, dbm, dcm, ddt, dnw, dd, dal, dbias = _ssd_bwd(
                sv["xc"], sv["zx"], sv["y"], dyn.reshape(bl, seq, d_inner), sv["st"], sv["dtb"], sv["alog"],
                sv["dskip"], sv["nw"], d_inner, "ssd_bwd")
            g["ssd_norm_w"][j] = dnw[:, 0, :].reshape(d_inner)
            g["ssd_d"][j] = dd[:, 0, :hpg].reshape(nheads)
            g["ssd_a_log"][j] = dal[:, 0, :hpg].reshape(nheads)
            g["ssd_dt_bias"][j] = dbias[:, 0, :hpg].reshape(nheads)
            dxbc, dcw, dcb = _ssd_conv_bwd(sv["zx"], sv["xpre"], (dxs, dbm, dcm), conv_w[j], d_inner, "ssd_conv_bwd")
            g["ssd_conv_w"][j] = dcw
            g["ssd_conv_b"][j] = dcb[0]
            dzs = [dz.reshape(t, d_inner), dxbc.reshape(t, xbc), ddt.reshape(t, N_GROUPS * LANES)]
            dh = _mm(dzs, w_in_p, "nt", F32, "mm_ssd_in_dx", 1024, d, [1024, 1024, 512], b_layer=j)
            gbuf["win"], tok = _mm(sv["h"], dzs, "tn", F32, "mm_ssd_in_dw", 1024, 512, 2048, out_buf=(gbuf["win"], j))
            done.append(tok)
        else:
            dmix, g["norm_mix_post"][i] = _norm_bwd(sv["mix"], norm_mix_post[i:i + 1], dmid, F32, "norm_bwd_f",
                                                    after=behind)
            dh3, g["pool_w"][j], dps = _pool_bwd(sv["h"].reshape(bl, seq, d), dmix.reshape(bl, seq, d), w_pool[j],
                                                 p_scale[j:j + 1], "pool_bwd")
            g["pool_scale"][j] = dps[0]
            dh = dh3.reshape(t, d)
        dx_in, g["norm_mix_pre"][i] = _norm_bwd(sv["x_in"], norm_mix_pre[i:i + 1], dh, F32, "norm_bwd_r", resid=dmid,
                                                after=done)
        return dx_in

    ffn_comm = None
    for i in reversed(range(depth)):
        sv = saved[i]
        dfo, g["norm_ffn_post"][i] = _norm_bwd(sv["fo"], norm_ffn_post[i:i + 1], dcur, BF16, "norm_bwd_b")
        dact = _mm(dfo, w_down, "nt", BF16, "mm_down_dx", 1024, ff // 2, d, b_layer=i)
        gbuf["down"], tok_down = _mm(sv["act"], dfo, "tn", F32, "mm_down_dw", ff // 2, 512, 2048,
                                     out_buf=(gbuf["down"], i))
        dhg, dhv, dcw, dcb = _ffn_act_bwd(sv["hpre"], sv["pre_g"], sv["pre_v"], dact.reshape(bl, seq, ff), f_conv_w[i],
                                          "ffn_act_bwd")
        g["ffn_conv_w"][i] = dcw
        g["ffn_conv_b"][i] = dcb[0]
        dhs = [dhg.reshape(t, ff), dhv.reshape(t, ff)]
        du = _mm(dhs, w_up, "nt", F32, "mm_up_dx", 1024, d, ff, b_layer=i)
        gbuf["up"], tok_up = _mm(sv["u"], dhs, "tn", F32, "mm_up_dw", 512, ff // 2, 2048, out_buf=(gbuf["up"], i))
        dmid, g["norm_ffn_pre"][i] = _norm_bwd(sv["mid"], norm_ffn_pre[i:i + 1], du, F32, "norm_bwd_r", resid=dcur,
                                               after=[tok_down, tok_up])
        if i > 0:
            dcur = mixer_bwd(i, dmid)
        else:
            ffn_comm, ffn_token = _reduce_begin(FFNW, [gbuf["up"], gbuf["down"]], core, "ffn")
            dcur = mixer_bwd(0, dmid, behind=[ffn_token])

    grad_x = dcur.reshape(bl, seq, d)
    for n in ("norm_mix_pre", "norm_mix_post", "norm_ffn_pre", "norm_ffn_post"):
        g[n] = [a[0] for a in g[n]]
    small_names = [n for n, _ in SMALL] + list(REPL)
    full = {n: jnp.stack(g[n], axis=0) for n in small_names}

    g_in = jnp.concatenate([gbuf["win"][..., :d_inner + xbc], unpad_heads(gbuf["win"][..., d_inner + xbc:])], axis=-1)
    g_in_cm = jnp.swapaxes(g_in.reshape(n_ssd, d, N_CHIPS, -1), 1, 2)
    mix_comm, mix_token = _reduce_begin(MIXW, [g_in_cm, gbuf["out"], jnp.stack(g["pool_w"], axis=0)], core, "mixers")

    grads, deltas, new_m, new_v = {}, {}, {}, {}

    def adamw(n, gr):
        shp = wts[n].shape
        two = (math.prod(shp[:-1]), shp[-1])
        dl, mn, vn = _adamw(wts[n].reshape(two), gr.reshape(two), mom[n].reshape(two), var[n].reshape(two),
                            "adamw_" + n)
        grads[n], deltas[n], new_m[n], new_v[n] = gr, dl.reshape(shp), mn.reshape(shp), vn.reshape(shp)
        return dl

    ffn_grads = _reduce_finish(FFNW, ffn_comm, core, "ffn", after=mix_token)
    for gr, (n, _) in zip(ffn_grads, FFNW):
        last = adamw(n, gr)
    mix_grads = _reduce_finish(MIXW, mix_comm, core, "mixers", after=last)
    for gr, (n, _) in zip(mix_grads, MIXW):
        adamw(n, gr)

    vec = jnp.concatenate([full[n].reshape(-1) for n in small_names] + [loss_part[0, :1]])
    nvec = vec.shape[0]
    vrows = 8 * ((nvec + 8 * FLAT_COLS - 1) // (8 * FLAT_COLS))
    vec = jnp.pad(vec, (0, vrows * FLAT_COLS - nvec)).reshape(vrows, FLAT_COLS)
    tot = _allreduce_small(vec, "allreduce_small", after=[mix_grads[0]]).reshape(-1)
    small_grads, off = {}, 0
    for n in small_names:
        cnt = math.prod(full[n].shape)
        small_grads[n] = tot[off:off + cnt].reshape(full[n].shape)
        off += cnt
    loss = tot[off]
    for n, ax in SMALL:
        w = wts[n].shape[ax]
        small_grads[n] = lax.dynamic_slice_in_dim(small_grads[n], chip * w, w, axis=ax)

    for n in small_names:
        adamw(n, small_grads[n])

    return (loss, grad_x, *[grads[n] for n in WEIGHTS], *[deltas[n] for n in WEIGHTS],
            *[new_m[n] for n in WEIGHTS], *[new_v[n] for n in WEIGHTS])
```

```python
import functools
import math

import jax
import jax.numpy as jnp
from jax import lax
from jax.experimental import pallas as pl
from jax.experimental.pallas import tpu as pltpu

F32 = jnp.float32
BF16 = jnp.bfloat16
MESH = pl.DeviceIdType.MESH
ANY = pl.BlockSpec(memory_space=pl.ANY)

HEAD_DIM = 64
D_STATE = 128
CHUNK = 128
N_GROUPS = 4
SSD_CONV = 4
FFN_CONV = 3
EPS = 1e-6
N_CHIPS = 4
LANES = 128
FLAT_COLS = 1024

ADAM_LR = 0.001
ADAM_B1 = 0.9
ADAM_B2 = 0.999
ADAM_EPS = 1e-08
ADAM_WD = 0.01
ADAM_STEP = 10

VMEM_LIMIT_BYTES = 56 * 1024 * 1024


def _params(sem=None):
    kw = dict(vmem_limit_bytes=VMEM_LIMIT_BYTES)
    if sem is not None:
        kw["dimension_semantics"] = sem
    return pltpu.CompilerParams(**kw)


def _sigmoid(x):
    return 0.5 * jnp.tanh(0.5 * x) + 0.5


def _softplus(x):
    return jnp.maximum(x, 0.0) + jnp.log(1.0 + jnp.exp(-jnp.abs(x)))


def _dot(a, b, dn):
    return lax.dot_general(a, b, (dn, ((), ())), preferred_element_type=F32)


def _nn(a, b):
    return _dot(a, b, ((1,), (0,)))


def _nt(a, b):
    return _dot(a, b, ((1,), (1,)))


def _tn(a, b):
    return _dot(a, b, ((0,), (0,)))


def _split(x, parts):
    out = []
    r = x
    for _ in range(parts):
        p = r.astype(BF16)
        out.append(p)
        r = r - p.astype(F32)
    return out


def _sel_left(sel, x, parts=3):
    n = x.shape[1]
    r = _nn(sel, jnp.concatenate(_split(x, parts), axis=1))
    out = r[:, 0:n]
    for i in range(1, parts):
        out = out + r[:, i * n:(i + 1) * n]
    return out


def _sel_right(x, sel_stacked, parts=3):
    return _nn(jnp.concatenate(_split(x, parts), axis=1), sel_stacked)


def _mm(a, b, dims, out_dtype, name, tm, tn, tk, b_layer=None, out_buf=None):
    a_list = list(a) if isinstance(a, (list, tuple)) else [a]
    b_list = list(b) if isinstance(b, (list, tuple)) else [b]
    if dims in ("nn", "nt"):
        assert len(b_list) == 1
        m = a_list[0].shape[0]
        segs = [x.shape[1] for x in a_list]
        k = sum(segs)
        bshape = b_list[0].shape[-2:]
        n = bshape[1] if dims == "nn" else bshape[0]
        assert (bshape[0] if dims == "nn" else bshape[1]) == k
    else:
        assert len(a_list) == 1 and b_layer is None
        k, m = a_list[0].shape
        segs = [x.shape[1] for x in b_list]
        n = sum(segs)
    nseg = len(segs)
    tm, tn = min(tm, m), min(tn, n)
    if dims == "tn":
        tk = min(tk, k)
        tn = min(tn, min(segs))
        units = [tn] * nseg
        nk = k // tk
        assert k % tk == 0
    else:
        units = [min(u, s) for u, s in zip(tk if isinstance(tk, (list, tuple)) else [tk] * nseg, segs)]
        nk = sum(s // u for s, u in zip(segs, units))
    assert m % tm == 0 and n % tn == 0 and all(s % u == 0 for s, u in zip(segs, units)), (name, m, n, k, segs, units)
    counts = [s // u for s, u in zip(segs, units)]
    starts = [sum(counts[:s]) for s in range(nseg)]
    assert all(sum(segs[:s]) % units[s] == 0 for s in range(nseg)), (name, segs, units)
    first_block = [sum(segs[:s]) // units[s] for s in range(nseg)]
    dn = {"nn": ((1,), (0,)), "nt": ((1,), (1,)), "tn": ((0,), (0,))}[dims]

    same = len(set(units)) == 1
    nb_ops = len(b_list) if dims == "tn" else (1 if same else nseg)

    def body(*refs):
        a_refs = refs[:len(a_list)]
        b_refs = refs[len(a_list):len(a_list) + nb_ops]
        rest = refs[len(a_list) + nb_ops + (0 if out_buf is None else 1):]
        o_ref = rest[0]
        if out_buf is not None:
            rest[1][...] = jnp.zeros((8, LANES), F32)
            rest = rest[1:]
        acc = rest[1] if nk > 1 else None
        kk = pl.program_id(2)
        sel = kk if dims != "tn" else pl.program_id(1)

        def step(a_ref, b_ref):
            p = _dot(a_ref[...].astype(BF16), b_ref[...].astype(BF16), dn)
            if nk == 1:
                o_ref[...] = p.astype(out_dtype)
                return

            @pl.when(kk == 0)
            def _():
                acc[...] = p

            @pl.when(kk > 0)
            def _():
                acc[...] += p

        if nseg == 1:
            step(a_refs[0], b_refs[0])
        else:
            for s in range(nseg):
                @pl.when(jnp.logical_and(sel >= starts[s], sel < starts[s] + counts[s]))
                def _(s=s):
                    step(a_refs[s] if dims != "tn" else a_refs[0], b_refs[s if nb_ops > 1 else 0])

        if nk > 1:
            @pl.when(kk == nk - 1)
            def _():
                o_ref[...] = acc[...].astype(out_dtype)

    def seg_index(v, s):
        return v if nseg == 1 else jnp.clip(v - starts[s], 0, counts[s] - 1)

    lead = () if b_layer is None else (b_layer,)
    none = () if b_layer is None else (None,)
    def b_block(kk, s):
        return kk if same else first_block[s] + seg_index(kk, s)

    if dims == "nn":
        a_specs = [pl.BlockSpec((tm, units[s]), lambda i, j, kk, s=s: (i, seg_index(kk, s))) for s in range(nseg)]
        b_specs = [pl.BlockSpec(none + (units[s], tn), lambda i, j, kk, s=s: lead + (b_block(kk, s), j))
                   for s in range(nb_ops)]
    elif dims == "nt":
        a_specs = [pl.BlockSpec((tm, units[s]), lambda i, j, kk, s=s: (i, seg_index(kk, s))) for s in range(nseg)]
        b_specs = [pl.BlockSpec(none + (tn, units[s]), lambda i, j, kk, s=s: lead + (j, b_block(kk, s)))
                   for s in range(nb_ops)]
    else:
        a_specs = [pl.BlockSpec((tk, tm), lambda i, j, kk: (kk, i))]
        b_specs = [pl.BlockSpec((tk, tn), lambda i, j, kk, s=s: (kk, seg_index(j, s))) for s in range(nseg)]
    args = a_list + (b_list * nb_ops if dims != "tn" else b_list)
    in_specs = a_specs + b_specs
    aliases = {}
    if out_buf is None:
        out_shape = jax.ShapeDtypeStruct((m, n), out_dtype)
        out_spec = pl.BlockSpec((tm, tn), lambda i, j, kk: (i, j))
    else:
        buf, slab = out_buf
        assert buf.shape[1:] == (m, n) and buf.dtype == out_dtype
        out_shape = (jax.ShapeDtypeStruct(buf.shape, out_dtype), jax.ShapeDtypeStruct((8, LANES), F32))
        out_spec = (pl.BlockSpec((None, tm, tn), lambda i, j, kk: (slab, i, j)),
                    pl.BlockSpec((8, LANES), lambda i, j, kk: (0, 0)))
        aliases = {len(args): 0}
        args = args + [buf]
        in_specs = in_specs + [ANY]
    return pl.pallas_call(
        body,
        out_shape=out_shape,
        grid=(m // tm, n // tn, nk),
        in_specs=in_specs,
        out_specs=out_spec,
        scratch_shapes=[] if nk == 1 else [pltpu.VMEM((tm, tn), F32)],
        input_output_aliases=aliases,
        compiler_params=_params(("parallel", "parallel", "arbitrary") if out_buf is None else ("arbitrary",) * 3),
        name=name,
    )(*args)


def _row_tile(t, want):
    tm = min(want, t)
    assert t % tm == 0
    return tm


def _norm_fwd(x, w, out_dtype, name, resid=None, after=()):
    t, d = x.shape
    tm = _row_tile(t, 512)
    after = [a for a in after if a is not None]

    def body(*refs):
        refs = refs[:len(refs) - 1 - len(after)] + refs[len(refs) - 1:]
        if resid is None:
            x_ref, w_ref, o_ref = refs
        else:
            x_ref, w_ref, r_ref, o_ref = refs
        xv = x_ref[...]
        r = lax.rsqrt(jnp.mean(xv * xv, axis=-1, keepdims=True) + EPS)
        y = (xv * r) * w_ref[...]
        if resid is not None:
            y = r_ref[...] + y
        o_ref[...] = y.astype(out_dtype)

    row = pl.BlockSpec((tm, d), lambda i: (i, 0))
    vec = pl.BlockSpec((1, d), lambda i: (0, 0))
    args = [x, w] + ([] if resid is None else [resid]) + after
    return pl.pallas_call(
        body, out_shape=jax.ShapeDtypeStruct((t, d), out_dtype), grid=(t // tm,),
        in_specs=[row, vec] + ([] if resid is None else [row]) + [ANY] * len(after), out_specs=row,
        compiler_params=_params(("parallel",)), name=name)(*args)


def _norm_post_pre(m, w_post, resid, w_pre, pre_dtype, name, after=()):
    t, d = m.shape
    tm = _row_tile(t, 512)
    after = [a for a in after if a is not None]

    def body(m_ref, w1_ref, r_ref, w2_ref, *rest):
        x_ref, u_ref = rest[len(after):]
        mv = m_ref[...]
        r1 = lax.rsqrt(jnp.mean(mv * mv, axis=-1, keepdims=True) + EPS)
        xv = r_ref[...] + (mv * r1) * w1_ref[...]
        x_ref[...] = xv
        r2 = lax.rsqrt(jnp.mean(xv * xv, axis=-1, keepdims=True) + EPS)
        u_ref[...] = ((xv * r2) * w2_ref[...]).astype(pre_dtype)

    row = pl.BlockSpec((tm, d), lambda i: (i, 0))
    vec = pl.BlockSpec((1, d), lambda i: (0, 0))
    return pl.pallas_call(
        body, out_shape=(jax.ShapeDtypeStruct((t, d), F32), jax.ShapeDtypeStruct((t, d), pre_dtype)), grid=(t // tm,),
        in_specs=[row, vec, row, vec] + [ANY] * len(after), out_specs=(row, row),
        compiler_params=_params(("parallel",)), name=name)(m, w_post, resid, w_pre, *after)


def _norm_bwd(src, w, dy, out_dtype, name, resid=None, after=()):
    t, d = src.shape
    tm = _row_tile(t, 512)
    after = [a for a in after if a is not None]

    def body(*refs):
        refs = refs[:len(refs) - 2 - len(after)] + refs[len(refs) - 2:]
        if resid is None:
            x_ref, w_ref, g_ref, o_ref, dw_ref = refs
        else:
            x_ref, w_ref, g_ref, r_ref, o_ref, dw_ref = refs
        xv = x_ref[...]
        g = g_ref[...].astype(F32)
        r = lax.rsqrt(jnp.mean(xv * xv, axis=-1, keepdims=True) + EPS)
        xh = xv * r
        gh = g * w_ref[...]
        mean = jnp.mean(gh * xh, axis=-1, keepdims=True)
        dx = r * (gh - xh * mean)
        if resid is not None:
            dx = r_ref[...] + dx
        o_ref[...] = dx.astype(out_dtype)
        part = jnp.sum(g * xh, axis=0, keepdims=True)

        @pl.when(pl.program_id(0) == 0)
        def _():
            dw_ref[...] = part

        @pl.when(pl.program_id(0) > 0)
        def _():
            dw_ref[...] += part

    row = pl.BlockSpec((tm, d), lambda i: (i, 0))
    vec = pl.BlockSpec((1, d), lambda i: (0, 0))
    args = [src, w, dy] + ([] if resid is None else [resid]) + after
    return pl.pallas_call(
        body,
        out_shape=(jax.ShapeDtypeStruct((t, d), out_dtype), jax.ShapeDtypeStruct((1, d), F32)),
        grid=(t // tm,),
        in_specs=[row, vec, row] + ([] if resid is None else [row]) + [ANY] * len(after),
        out_specs=(row, vec),
        compiler_params=_params(("arbitrary",)), name=name)(*args)


def _loss_head(y, target, name):
    t, d = y.shape
    tm = _row_tile(t, 512)

    def body(y_ref, t_ref, dy_ref, l_ref):
        e = y_ref[...] - t_ref[...]
        dy_ref[...] = e * (1.0 / d)
        col = jnp.sum(e * e, axis=0, keepdims=True)
        s = jnp.sum(col, axis=1, keepdims=True) * (0.5 / d)
        part = jnp.broadcast_to(s, (1, LANES))

        @pl.when(pl.program_id(0) == 0)
        def _():
            l_ref[...] = part

        @pl.when(pl.program_id(0) > 0)
        def _():
            l_ref[...] += part

    row = pl.BlockSpec((tm, d), lambda i: (i, 0))
    return pl.pallas_call(
        body,
        out_shape=(jax.ShapeDtypeStruct((t, d), F32), jax.ShapeDtypeStruct((1, LANES), F32)),
        grid=(t // tm,), in_specs=[row, row],
        out_specs=(row, pl.BlockSpec((1, LANES), lambda i: (0, 0))),
        compiler_params=_params(("arbitrary",)), name=name)(y, target)


def _window(ref, c, rows, seq, before, after):
    r0 = pl.multiple_of(c * rows, rows)
    parts = []
    if before:
        h0 = pl.multiple_of(jnp.maximum(r0 - before, 0), before)
        halo = ref[pl.ds(h0, before), :].astype(F32)
        parts.append(jnp.where(c > 0, halo, 0.0))
    parts.append(ref[pl.ds(r0, rows), :].astype(F32))
    if after:
        h1 = pl.multiple_of(jnp.minimum(r0 + rows, seq - after), after)
        halo = ref[pl.ds(h1, after), :].astype(F32)
        parts.append(jnp.where(c < seq // rows - 1, halo, 0.0))
    return parts[0] if len(parts) == 1 else jnp.concatenate(parts, axis=0)


def _lag(x, k):
    return pltpu.roll(x, k, 0) if k else x


def _lead(x, k):
    return pltpu.roll(x, x.shape[0] - k, 0) if k else x


SHIFT_ROWS = 128
SHIFT_COLS = 256


HALO = 16


def _conv3(ext, w, bias):
    acc = bias + w[2:3, :] * ext[HALO:, :]
    acc = acc + w[1:2, :] * _lag(ext, 1)[HALO:, :]
    return acc + w[0:1, :] * _lag(ext, 2)[HALO:, :]


def _ffn_act_fwd(hpre, cw, cb, name):
    b, seq, f2 = hpre.shape
    cbk = SHIFT_COLS
    nj = f2 // (2 * cbk)
    rows = min(SHIFT_ROWS, seq)

    def body(g_ref, v_ref, wg_ref, wv_ref, bg_ref, bv_ref, o_ref, pg_ref, pv_ref):
        def chunk(c, carry):
            gate = _conv3(_window(g_ref, c, rows, seq, HALO, 0), wg_ref[...], bg_ref[...])
            val = _conv3(_window(v_ref, c, rows, seq, HALO, 0), wv_ref[...], bv_ref[...])
            a = gate * _sigmoid(gate) * val
            here = pl.ds(pl.multiple_of(c * rows, rows), rows)
            o_ref[here, :] = a.astype(BF16)
            pg_ref[here, :] = gate.astype(BF16)
            pv_ref[here, :] = val.astype(BF16)
            return carry

        lax.fori_loop(0, seq // rows, chunk, 0)

    blk = lambda off: pl.BlockSpec((None, seq, cbk), lambda i, j: (i, 0, j + off))
    wsp = lambda r, off: pl.BlockSpec((r, cbk), lambda i, j: (0, j + off))
    half = jax.ShapeDtypeStruct((b, seq, f2 // 2), BF16)
    return pl.pallas_call(
        body, out_shape=(half, half, half), grid=(b, nj),
        in_specs=[blk(0), blk(nj), wsp(FFN_CONV, 0), wsp(FFN_CONV, nj), wsp(1, 0), wsp(1, nj)],
        out_specs=(blk(0), blk(0), blk(0)),
        compiler_params=_params(("parallel", "parallel")), name=name)(hpre, hpre, cw, cw, cb, cb)


def _ffn_act_bwd(hpre, pre_g, pre_v, da, cw, name):
    b, seq, f2 = hpre.shape
    cbk = SHIFT_COLS
    nj = f2 // (2 * cbk)
    rows = min(SHIFT_ROWS, seq)

    def body(g_ref, v_ref, pg_ref, pv_ref, da_ref, wg_ref, wv_ref, og_ref, ov_ref, dwg_ref, dwv_ref, dbg_ref, dbv_ref):
        wg, wv = wg_ref[...], wv_ref[...]

        def back(dpre, w, o_ref, x_ref, c, carry):
            here = pl.ds(pl.multiple_of(c * rows, rows), rows)
            leads = [dpre, _lead(dpre, 1), _lead(dpre, 2)]
            dx = w[2:3, :] * leads[0] + w[1:2, :] * leads[1] + w[0:1, :] * leads[2]
            o_ref[here, :] = dx[:rows, :].astype(BF16)
            x0 = x_ref[here, :].astype(F32)
            return tuple(carry[k] + jnp.sum(leads[k][:rows, :] * x0, axis=0, keepdims=True) for k in range(FFN_CONV)) + (
                carry[FFN_CONV] + jnp.sum(dpre[:rows, :], axis=0, keepdims=True),)

        def chunk(c, carry):
            cg, cv = carry
            gate = _window(pg_ref, c, rows, seq, 0, HALO)
            val = _window(pv_ref, c, rows, seq, 0, HALO)
            dav = _window(da_ref, c, rows, seq, 0, HALO)
            sg = _sigmoid(gate)
            cg = back(dav * val * (sg * (1.0 + gate * (1.0 - sg))), wg, og_ref, g_ref, c, cg)
            cv = back(dav * (gate * sg), wv, ov_ref, v_ref, c, cv)
            return cg, cv

        z = jnp.zeros((1, cbk), F32)
        cg, cv = lax.fori_loop(0, seq // rows, chunk, ((z,) * (FFN_CONV + 1), (z,) * (FFN_CONV + 1)))
        dwg = jnp.concatenate([cg[2], cg[1], cg[0]], axis=0)
        dwv = jnp.concatenate([cv[2], cv[1], cv[0]], axis=0)

        @pl.when(pl.program_id(1) == 0)
        def _():
            dwg_ref[...] = dwg
            dwv_ref[...] = dwv
            dbg_ref[...] = cg[FFN_CONV]
            dbv_ref[...] = cv[FFN_CONV]

        @pl.when(pl.program_id(1) > 0)
        def _():
            dwg_ref[...] += dwg
            dwv_ref[...] += dwv
            dbg_ref[...] += cg[FFN_CONV]
            dbv_ref[...] += cv[FFN_CONV]

    blk = lambda off: pl.BlockSpec((None, seq, cbk), lambda j, i: (i, 0, j + off))
    wsp = lambda r, off: pl.BlockSpec((r, cbk), lambda j, i: (0, j + off))
    half = jax.ShapeDtypeStruct((b, seq, f2 // 2), BF16)
    dwshape = jax.ShapeDtypeStruct((FFN_CONV, f2 // 2), F32)
    dbshape = jax.ShapeDtypeStruct((1, f2 // 2), F32)
    dg, dv, dwg, dwv, dbg, dbv = pl.pallas_call(
        body,
        out_shape=(half, half, dwshape, dwshape, dbshape, dbshape),
        grid=(nj, b),
        in_specs=[blk(0), blk(nj), blk(0), blk(0), blk(0), wsp(FFN_CONV, 0), wsp(FFN_CONV, nj)],
        out_specs=(blk(0), blk(0), wsp(FFN_CONV, 0), wsp(FFN_CONV, 0), wsp(1, 0), wsp(1, 0)),
        compiler_params=_params(("parallel", "arbitrary")), name=name)(hpre, hpre, pre_g, pre_v, da, cw, cw)
    return dg, dv, jnp.concatenate([dwg, dwv], axis=1), jnp.concatenate([dbg, dbv], axis=1)


def _ssd_conv_fwd(zx, cw, cb, d_inner, name):
    b, seq, _ = zx.shape
    xbc = cw.shape[1]
    cbk = SHIFT_COLS
    off = d_inner // cbk
    rows = min(SHIFT_ROWS, seq)

    def body(h_ref, w_ref, b_ref, o_ref, p_ref):
        w = w_ref[...]
        bias = b_ref[...]

        def chunk(c, carry):
            ext = _window(h_ref, c, rows, seq, 8, 0)
            acc = bias + w[3:4, :] * ext[8:, :]
            for k in range(1, SSD_CONV):
                acc = acc + w[3 - k:4 - k, :] * _lag(ext, k)[8:, :]
            here = pl.ds(pl.multiple_of(c * rows, rows), rows)
            o_ref[here, :] = acc * _sigmoid(acc)
            p_ref[here, :] = acc.astype(BF16)
            return carry

        lax.fori_loop(0, seq // rows, chunk, 0)

    blk = pl.BlockSpec((None, seq, cbk), lambda i, j: (i, 0, j))
    return pl.pallas_call(
        body, out_shape=(jax.ShapeDtypeStruct((b, seq, xbc), F32), jax.ShapeDtypeStruct((b, seq, xbc), BF16)),
        grid=(b, xbc // cbk),
        in_specs=[pl.BlockSpec((None, seq, cbk), lambda i, j: (i, 0, j + off)),
                  pl.BlockSpec((SSD_CONV, cbk), lambda i, j: (0, j)),
                  pl.BlockSpec((1, cbk), lambda i, j: (0, j))],
        out_specs=(blk, blk),
        compiler_params=_params(("parallel", "parallel")), name=name)(zx, cw, cb)


def _ssd_conv_bwd(zx, pre, dparts, cw, d_inner, name):
    b, seq, _ = zx.shape
    xbc = cw.shape[1]
    cbk = SHIFT_COLS
    off = d_inner // cbk
    rows = min(SHIFT_ROWS, seq)
    nblk = [p.shape[2] // cbk for p in dparts]
    first = [sum(nblk[:s]) for s in range(len(dparts))]
    assert sum(nblk) == xbc // cbk

    def body(h_ref, p_ref, gx_ref, gb_ref, gc_ref, w_ref, o_ref, dw_ref, db_ref):
        w = w_ref[...]
        j = pl.program_id(0)

        def chunk(c, carry):
            dws, dbias = carry
            here = pl.ds(pl.multiple_of(c * rows, rows), rows)
            pre = _window(p_ref, c, rows, seq, 0, HALO)
            s = _sigmoid(pre)
            gsel = jnp.where(j < first[1], _window(gx_ref, c, rows, seq, 0, HALO),
                             jnp.where(j < first[2], _window(gb_ref, c, rows, seq, 0, HALO),
                                       _window(gc_ref, c, rows, seq, 0, HALO)))
            dpre = gsel * (s * (1.0 + pre * (1.0 - s)))
            leads = [dpre] + [_lead(dpre, k) for k in range(1, SSD_CONV)]
            dx = w[3:4, :] * leads[0]
            for k in range(1, SSD_CONV):
                dx = dx + w[3 - k:4 - k, :] * leads[k]
            o_ref[here, :] = dx[:rows, :].astype(BF16)
            x0 = h_ref[here, :]
            dws = tuple(dws[k] + jnp.sum(leads[k][:rows, :] * x0, axis=0, keepdims=True) for k in range(SSD_CONV))
            dbias = dbias + jnp.sum(dpre[:rows, :], axis=0, keepdims=True)
            return dws, dbias

        z = jnp.zeros((1, cbk), F32)
        dws, dbias = lax.fori_loop(0, seq // rows, chunk, ((z,) * SSD_CONV, z))
        dwv = jnp.concatenate([dws[3 - i] for i in range(SSD_CONV)], axis=0)

        @pl.when(pl.program_id(1) == 0)
        def _():
            dw_ref[...] = dwv
            db_ref[...] = dbias

        @pl.when(pl.program_id(1) > 0)
        def _():
            dw_ref[...] += dwv
            db_ref[...] += dbias

    return pl.pallas_call(
        body,
        out_shape=(jax.ShapeDtypeStruct((b, seq, xbc), BF16), jax.ShapeDtypeStruct((SSD_CONV, xbc), F32),
                   jax.ShapeDtypeStruct((1, xbc), F32)),
        grid=(xbc // cbk, b),
        in_specs=[pl.BlockSpec((None, seq, cbk), lambda j, i: (i, 0, j + off)),
                  pl.BlockSpec((None, seq, cbk), lambda j, i: (i, 0, j))] + [
                  pl.BlockSpec((None, seq, cbk), lambda j, i, s=s: (i, 0, jnp.clip(j - first[s], 0, nblk[s] - 1)))
                  for s in range(3)] + [
                  pl.BlockSpec((SSD_CONV, cbk), lambda j, i: (0, j))],
        out_specs=(pl.BlockSpec((None, seq, cbk), lambda j, i: (i, 0, j)),
                   pl.BlockSpec((SSD_CONV, cbk), lambda j, i: (0, j)),
                   pl.BlockSpec((1, cbk), lambda j, i: (0, j))),
        compiler_params=_params(("parallel", "arbitrary")), name=name)(zx, pre, *dparts, cw)


def _pool_sums(q, g, lead):
    sh = _lead if lead else _lag
    s2 = q + sh(q, 1)
    s4 = s2 + sh(s2, 2)
    s8 = s4 + sh(s4, 4)
    s16 = s8 + sh(s8, 8)
    return jnp.where(g == 0, s2, jnp.where(g == 1, s4, jnp.where(g == 2, s8, s16)))


def _pool_count(r0, n, g, shape):
    t = (r0 + lax.broadcasted_iota(jnp.int32, shape, 0) + 1).astype(F32)
    return jnp.minimum(t, (2 << g).astype(F32))


def _pool_fwd(h, pw, scale, name):
    b, seq, d = h.shape
    dg = d // 4
    rows = min(SHIFT_ROWS, seq)

    def body(h_ref, w_ref, s_ref, o_ref):
        g = pl.program_id(1)
        wmat = w_ref[...]
        sc = s_ref[...]

        def chunk(c, carry):
            r0 = c * rows
            ext = _window(h_ref, c, rows, seq, 16, 0)
            sums = _pool_sums(ext, g, False)[16:, :]
            mixed = sums / _pool_count(r0, rows, g, (rows, dg)) - ext[16:, :]
            o_ref[pl.ds(pl.multiple_of(r0, rows), rows), :] = _nn(mixed.astype(BF16), wmat) * sc
            return carry

        lax.fori_loop(0, seq // rows, chunk, 0)

    return pl.pallas_call(
        body, out_shape=jax.ShapeDtypeStruct((b, seq, d), F32), grid=(b, 4),
        in_specs=[pl.BlockSpec((None, seq, dg), lambda i, g: (i, 0, g)),
                  pl.BlockSpec((None, dg, dg), lambda i, g: (g, 0, 0)),
                  pl.BlockSpec((1, dg), lambda i, g: (0, g))],
        out_specs=pl.BlockSpec((None, seq, dg), lambda i, g: (i, 0, g)),
        compiler_params=_params(("parallel", "parallel")), name=name)(h, pw, scale)


def _pool_bwd(h, dout, pw, scale, name):
    b, seq, d = h.shape
    dg = d // 4
    rows = min(SHIFT_ROWS, seq)

    def body(h_ref, g_ref, w_ref, s_ref, o_ref, dw_ref, ds_ref, dw_acc):
        g = pl.program_id(0)
        wmat = w_ref[...]
        sc = s_ref[...]
        dw_acc[...] = jnp.zeros_like(dw_acc)

        def chunk(c, dsc):
            r0 = c * rows
            ext = _window(h_ref, c, rows, seq, 16, 0)
            sums = _pool_sums(ext, g, False)[16:, :]
            mixed = (sums / _pool_count(r0, rows, g, (rows, dg)) - ext[16:, :]).astype(BF16)
            gext = _window(g_ref, c, rows, seq, 0, 16)
            dsc = dsc + jnp.sum(gext[:rows, :] * _nn(mixed, wmat), axis=0, keepdims=True)
            dpre = (gext * sc).astype(BF16)
            dw_acc[...] += _tn(mixed, dpre[:rows, :])
            dmix = _nt(dpre, wmat)
            q = dmix / _pool_count(r0, rows + 16, g, (rows + 16, dg))
            back = _pool_sums(q, g, True)
            o_ref[pl.ds(pl.multiple_of(r0, rows), rows), :] = back[:rows, :] - dmix[:rows, :]
            return dsc

        dsc = lax.fori_loop(0, seq // rows, chunk, jnp.zeros((1, dg), F32))

        @pl.when(pl.program_id(1) == 0)
        def _():
            dw_ref[...] = dw_acc[...]
            ds_ref[...] = dsc

        @pl.when(pl.program_id(1) > 0)
        def _():
            dw_ref[...] += dw_acc[...]
            ds_ref[...] += dsc

    return pl.pallas_call(
        body,
        out_shape=(jax.ShapeDtypeStruct((b, seq, d), F32), jax.ShapeDtypeStruct((4, dg, dg), F32),
                   jax.ShapeDtypeStruct((1, d), F32)),
        grid=(4, b),
        in_specs=[pl.BlockSpec((None, seq, dg), lambda g, i: (i, 0, g)),
                  pl.BlockSpec((None, seq, dg), lambda g, i: (i, 0, g)),
                  pl.BlockSpec((None, dg, dg), lambda g, i: (g, 0, 0)),
                  pl.BlockSpec((1, dg), lambda g, i: (0, g))],
        out_specs=(pl.BlockSpec((None, seq, dg), lambda g, i: (i, 0, g)),
                   pl.BlockSpec((None, dg, dg), lambda g, i: (g, 0, 0)),
                   pl.BlockSpec((1, dg), lambda g, i: (0, g))),
        scratch_shapes=[pltpu.VMEM((dg, dg), F32)],
        compiler_params=_params(("parallel", "arbitrary")), name=name)(h, dout, pw, scale)


def _head_of(channel):
    return jnp.right_shift(channel, HEAD_DIM.bit_length() - 1)


def _ssd_consts(gw):
    q = CHUNK
    row = lax.broadcasted_iota(jnp.int32, (q, q), 0)
    col = lax.broadcasted_iota(jnp.int32, (q, q), 1)
    tril = (row >= col).astype(BF16)
    triu = (row <= col).astype(BF16)
    e = (_head_of(lax.broadcasted_iota(jnp.int32, (LANES, gw), 1))
         == lax.broadcasted_iota(jnp.int32, (LANES, gw), 0)).astype(BF16)
    et = (_head_of(lax.broadcasted_iota(jnp.int32, (gw, LANES), 0))
          == lax.broadcasted_iota(jnp.int32, (gw, LANES), 1)).astype(BF16)
    return row, col, tril, triu, e, et


def _ssd_common(dtr, dtb, alog, gw):
    q = CHUNK
    row, col, tril, triu, e, et = _ssd_consts(gw)
    dt = _softplus(dtr + dtb)
    a_row = -jnp.exp(alog)
    acum = _sel_left(tril, dt * a_row)
    ac_last = jnp.sum(jnp.where(row == q - 1, acum, 0.0), axis=0, keepdims=True)
    eac = jnp.exp(acum)
    de = jnp.exp(ac_last - acum)
    e2 = jnp.concatenate([e, e], axis=0)
    expand = _sel_right(jnp.concatenate([dt, eac, de], axis=0), e2, 2)
    dt_x, eac_x, de_x = expand[0:q], expand[q:2 * q], expand[2 * q:3 * q]
    acum_t = acum.T
    cd_col = jnp.exp(acum_t[:, q - 1:q])
    et3 = jnp.concatenate([et, et, et], axis=1)
    cdmat = _nn(et3, jnp.concatenate(_split(jnp.broadcast_to(cd_col, (LANES, D_STATE)), 3), axis=0))
    consts = dict(row=row, col=col, tril=tril, triu=triu, e=e, et=et)
    return dt, a_row, acum, acum_t, ac_last, eac, de, dt_x, eac_x, de_x, cdmat, consts


def _decay(acum, acum_t, j, row, col):
    diff = acum[:, j:j + 1] - acum_t[j:j + 1, :]
    return jnp.exp(jnp.where(row >= col, diff, -1e30))


def _ssd_fwd(xc, zx, dtb, alog, dskip, nw, d_inner, name):
    b, seq, xbc = xc.shape
    q = CHUNK
    nc = seq // q
    gw = d_inner // N_GROUPS
    nh = gw // HEAD_DIM
    xb0 = d_inner // D_STATE
    xc0 = xb0 + N_GROUPS
    dt0 = (d_inner + xbc) // LANES

    nb = max(n for n in (4, 2, 1) if b % n == 0)

    def body(x_ref, b_ref, c_ref, z_ref, dtr_ref, dtb_ref, al_ref, dsk_ref, nw_ref, y_ref, yn_ref, st_ref, s_ref):
        @pl.when(pl.program_id(2) == 0)
        def _():
            s_ref[...] = jnp.zeros_like(s_ref)

        for s in range(nb):
            one(s, x_ref.at[s], b_ref.at[s], c_ref.at[s], z_ref.at[s], dtr_ref.at[s], dtb_ref, al_ref, dsk_ref, nw_ref,
                y_ref.at[s], yn_ref.at[s], st_ref.at[s], s_ref.at[s])

    def one(s, x_ref, b_ref, c_ref, z_ref, dtr_ref, dtb_ref, al_ref, dsk_ref, nw_ref, y_ref, yn_ref, st_ref, s_ref):
        prev = s_ref[...]
        st_ref[...] = prev
        x = x_ref[...]
        bm = b_ref[...].astype(BF16)
        cm = c_ref[...].astype(BF16)
        (dt, a_row, acum, acum_t, ac_last, eac, de, dt_x, eac_x, de_x, cdmat, k) = _ssd_common(
            dtr_ref[...], dtb_ref[0:1, :], al_ref[0:1, :], gw)
        xdt = x * dt_x
        xdt_b = xdt.astype(BF16)
        cb = _nt(cm, bm)
        half = _head_of(lax.broadcasted_iota(jnp.int32, (q, LANES), 1))
        pairs = []
        for j in range(nh):
            pc = (j // 2) * LANES
            m = (cb * _decay(acum, acum_t, j, k["row"], k["col"])).astype(BF16)
            yj = jnp.where(half == j % 2, _nn(m, xdt_b[:, pc:pc + LANES]), 0.0)
            if j % 2 == 0:
                pairs.append(yj)
            else:
                pairs[-1] = pairs[-1] + yj
        prev_b = prev.astype(BF16)
        y = dsk_ref[0:1, :] * x + jnp.concatenate(pairs, axis=1) + eac_x * _nt(cm, prev_b)
        s_ref[...] = cdmat * prev + _tn((xdt * de_x).astype(BF16), bm)
        y_ref[...] = y
        z = z_ref[...]
        yg = y * (z * _sigmoid(z))
        r = lax.rsqrt(jnp.mean(yg * yg, axis=-1, keepdims=True) + EPS)
        yn_ref[...] = ((yg * r) * nw_ref[0:1, :]).astype(BF16)

    par = lambda w: pl.BlockSpec((None, 8, w), lambda i, g, c: (g, 0, 0))
    return pl.pallas_call(
        body,
        out_shape=(jax.ShapeDtypeStruct((b, seq, d_inner), F32), jax.ShapeDtypeStruct((b, seq, d_inner), BF16),
                   jax.ShapeDtypeStruct((b, nc, N_GROUPS, gw, D_STATE), F32)),
        grid=(b // nb, N_GROUPS, nc),
        in_specs=[pl.BlockSpec((nb, q, gw), lambda i, g, c: (i, c, g)),
                  pl.BlockSpec((nb, q, D_STATE), lambda i, g, c: (i, c, xb0 + g)),
                  pl.BlockSpec((nb, q, D_STATE), lambda i, g, c: (i, c, xc0 + g)),
                  pl.BlockSpec((nb, q, gw), lambda i, g, c: (i, c, g)),
                  pl.BlockSpec((nb, q, LANES), lambda i, g, c: (i, c, dt0 + g)),
                  par(LANES), par(LANES), par(gw), par(gw)],
        out_specs=(pl.BlockSpec((nb, q, gw), lambda i, g, c: (i, c, g)),
                   pl.BlockSpec((nb, q, gw), lambda i, g, c: (i, c, g)),
                   pl.BlockSpec((nb, None, None, gw, D_STATE), lambda i, g, c: (i, c, g, 0, 0))),
        scratch_shapes=[pltpu.VMEM((nb, gw, D_STATE), F32)],
        compiler_params=_params(("parallel", "parallel", "arbitrary")), name=name,
    )(xc, xc, xc, zx, zx, dtb, alog, dskip, nw)


def _ssd_bwd(xc, zx, y, dyn, st, dtb, alog, dskip, nw, d_inner, name):
    b, seq, xbc = xc.shape
    q = CHUNK
    nc = seq // q
    gw = d_inner // N_GROUPS
    nh = gw // HEAD_DIM
    xb0 = d_inner // D_STATE
    xc0 = xb0 + N_GROUPS
    dt0 = (d_inner + xbc) // LANES

    nb = max(n for n in (4, 2, 1) if b % n == 0)

    def body(x_ref, b_ref, c_ref, z_ref, dtr_ref, y_ref, g_ref, st_ref, dtb_ref, al_ref, dsk_ref, nw_ref,
             dz_ref, dx_ref, db_ref, dc_ref, ddt_ref, dnw_ref, dd_ref, dal_ref, dbias_ref,
             ds_ref, colbuf, rowbuf):
        first = jnp.logical_and(pl.program_id(1) == 0, pl.program_id(2) == 0)

        @pl.when(pl.program_id(2) == 0)
        def _():
            ds_ref[...] = jnp.zeros_like(ds_ref)

        sums = [one(x_ref.at[s], b_ref.at[s], c_ref.at[s], z_ref.at[s], dtr_ref.at[s], y_ref.at[s], g_ref.at[s],
                    st_ref.at[s], dtb_ref, al_ref, dsk_ref, nw_ref, dz_ref.at[s], dx_ref.at[s], db_ref.at[s],
                    dc_ref.at[s], ddt_ref.at[s], ds_ref.at[s], colbuf.at[s], rowbuf.at[s]) for s in range(nb)]
        dnw, dd, dal, dbias = [functools.reduce(lambda p, r: p + r, [sm[i] for sm in sums]) for i in range(4)]

        @pl.when(first)
        def _():
            dnw_ref[...] = jnp.broadcast_to(dnw, (8, gw))
            dd_ref[...] = dd
            dal_ref[...] = jnp.broadcast_to(dal, (8, LANES))
            dbias_ref[...] = jnp.broadcast_to(dbias, (8, LANES))

        @pl.when(jnp.logical_not(first))
        def _():
            dnw_ref[...] += jnp.broadcast_to(dnw, (8, gw))
            dd_ref[...] += dd
            dal_ref[...] += jnp.broadcast_to(dal, (8, LANES))
            dbias_ref[...] += jnp.broadcast_to(dbias, (8, LANES))

    def one(x_ref, b_ref, c_ref, z_ref, dtr_ref, y_ref, g_ref, st_ref, dtb_ref, al_ref, dsk_ref, nw_ref,
            dz_ref, dx_ref, db_ref, dc_ref, ddt_ref, ds_ref, colbuf, rowbuf):
        x = x_ref[...]
        bm = b_ref[...].astype(BF16)
        cm = c_ref[...].astype(BF16)
        z = z_ref[...]
        y = y_ref[...]
        prev = st_ref[...]
        dtr = dtr_ref[...] + dtb_ref[0:1, :]
        (dt, a_row, acum, acum_t, ac_last, eac, de, dt_x, eac_x, de_x, cdmat, k) = _ssd_common(
            dtr_ref[...], dtb_ref[0:1, :], al_ref[0:1, :], gw)
        row, col = k["row"], k["col"]
        et2 = jnp.concatenate([k["et"], k["et"]], axis=0)

        sz = _sigmoid(z)
        silu_z = z * sz
        yg = y * silu_z
        r = lax.rsqrt(jnp.mean(yg * yg, axis=-1, keepdims=True) + EPS)
        xh = yg * r
        dyn = g_ref[...]
        gh = dyn * nw_ref[0:1, :]
        dyg = r * (gh - xh * jnp.mean(gh * xh, axis=-1, keepdims=True))
        dnw = jnp.sum(dyn * xh, axis=0, keepdims=True)
        g = dyg * silu_z
        dz_ref[...] = (dyg * y * (sz * (1.0 + z * (1.0 - sz)))).astype(BF16)
        dd = _sel_right(jnp.broadcast_to(jnp.sum(g * x, axis=0, keepdims=True), (8, gw)), et2, 2)

        xdt = x * dt_x
        xdt_b = xdt.astype(BF16)
        g_b = g.astype(BF16)
        prev_b = prev.astype(BF16)
        cb = _nt(cm, bm)

        cp = _nt(cm, prev_b)
        ge = g * eac_x
        dac = _sel_right(ge * cp, et2, 2)
        ge_b = ge.astype(BF16)
        dcm = _nn(ge_b, prev_b)
        dprev = _tn(ge_b, cm)

        colbuf[...] = jnp.zeros_like(colbuf)
        rowbuf[...] = jnp.zeros_like(rowbuf)
        dcb = jnp.zeros((q, q), F32)
        half = _head_of(lax.broadcasted_iota(jnp.int32, (q, LANES), 1))
        pairs = []
        for j in range(nh):
            pc = (j // 2) * LANES
            dec = _decay(acum, acum_t, j, row, col)
            m = cb * dec
            gj = jnp.where(half == j % 2, g[:, pc:pc + LANES], 0.0).astype(BF16)
            dm = _nt(gj, xdt_b[:, pc:pc + LANES])
            w = dm * m
            colbuf[:, j:j + 1] = jnp.sum(w, axis=1, keepdims=True)
            rowbuf[j:j + 1, :] = jnp.sum(w, axis=0, keepdims=True)
            dcb = dcb + dm * dec
            dj = jnp.where(half == j % 2, _tn(m.astype(BF16), g_b[:, pc:pc + LANES]), 0.0)
            if j % 2 == 0:
                pairs.append(dj)
            else:
                pairs[-1] = pairs[-1] + dj
        dxdt = jnp.concatenate(pairs, axis=1)
        dcb_b = dcb.astype(BF16)
        dcm = dcm + _nn(dcb_b, bm)
        dbm = _tn(dcb_b, cm)

        ds = ds_ref[...]
        ds_b = ds.astype(BF16)
        u = _nt(bm, ds_b)
        dxdt = dxdt + u * de_x
        dde = _sel_right(u * xdt, et2, 2)
        dbm = dbm + _nn((xdt * de_x).astype(BF16), ds_b)
        pm = jnp.concatenate(_split(ds * prev, 2), axis=1)
        t2 = _tn(pm, k["et"])
        dcd_row = jnp.sum(t2[0:D_STATE] + t2[D_STATE:2 * D_STATE], axis=0, keepdims=True)
        last = dcd_row * jnp.exp(ac_last) + jnp.sum(dde * de, axis=0, keepdims=True)
        dac = dac + colbuf[...] - rowbuf[...].T - dde * de + jnp.where(row == q - 1, last, 0.0)
        ds_ref[...] = cdmat * ds + dprev

        dadt = _sel_left(k["triu"], dac)
        ddt = _sel_right(dxdt * x, et2, 2) + dadt * a_row
        dal = jnp.sum(dadt * dt, axis=0, keepdims=True) * a_row
        lane = lax.broadcasted_iota(jnp.int32, (q, LANES), 1)
        ddtr = jnp.where(lane < nh, ddt * _sigmoid(dtr), 0.0)
        ddt_ref[...] = ddtr.astype(BF16)
        dbias = jnp.sum(ddtr, axis=0, keepdims=True)
        dx_ref[...] = dxdt * dt_x + dsk_ref[0:1, :] * g
        db_ref[...] = dbm
        dc_ref[...] = dcm
        return dnw, dd, dal, dbias

    rc = lambda c: nc - 1 - c
    par = lambda w: pl.BlockSpec((None, 8, w), lambda g, i, c: (g, 0, 0))
    blk = lambda w: pl.BlockSpec((nb, q, w), lambda g, i, c: (i, rc(c), g))
    return pl.pallas_call(
        body,
        out_shape=(jax.ShapeDtypeStruct((b, seq, d_inner), BF16),
                   jax.ShapeDtypeStruct((b, seq, d_inner), F32),
                   jax.ShapeDtypeStruct((b, seq, N_GROUPS * D_STATE), F32),
                   jax.ShapeDtypeStruct((b, seq, N_GROUPS * D_STATE), F32),
                   jax.ShapeDtypeStruct((b, seq, N_GROUPS * LANES), BF16),
                   jax.ShapeDtypeStruct((N_GROUPS, 8, gw), F32),
                   jax.ShapeDtypeStruct((N_GROUPS, 8, LANES), F32),
                   jax.ShapeDtypeStruct((N_GROUPS, 8, LANES), F32),
                   jax.ShapeDtypeStruct((N_GROUPS, 8, LANES), F32)),
        grid=(N_GROUPS, b // nb, nc),
        in_specs=[blk(gw),
                  pl.BlockSpec((nb, q, D_STATE), lambda g, i, c: (i, rc(c), xb0 + g)),
                  pl.BlockSpec((nb, q, D_STATE), lambda g, i, c: (i, rc(c), xc0 + g)),
                  blk(gw),
                  pl.BlockSpec((nb, q, LANES), lambda g, i, c: (i, rc(c), dt0 + g)),
                  blk(gw), blk(gw),
                  pl.BlockSpec((nb, None, None, gw, D_STATE), lambda g, i, c: (i, rc(c), g, 0, 0)),
                  par(LANES), par(LANES), par(gw), par(gw)],
        out_specs=(blk(gw), blk(gw), blk(D_STATE), blk(D_STATE), blk(LANES),
                   par(gw), par(LANES), par(LANES), par(LANES)),
        scratch_shapes=[pltpu.VMEM((nb, gw, D_STATE), F32), pltpu.VMEM((nb, q, LANES), F32),
                        pltpu.VMEM((nb, LANES, q), F32)],
        compiler_params=_params(("parallel", "arbitrary", "arbitrary")), name=name,
    )(xc, xc, xc, zx, zx, y, dyn, st, dtb, alog, dskip, nw)


def _adamw(w, g, m, v, name):
    rows, cols = w.shape
    tr = rows
    for cand in (512, 256, 128, 64, 32, 16, 8):
        if rows % cand == 0 and cand * cols * 4 <= 2 * 1024 * 1024:
            tr = cand
            break
    c1 = 1.0 - ADAM_B1 ** ADAM_STEP
    c2 = 1.0 - ADAM_B2 ** ADAM_STEP

    def body(w_ref, g_ref, m_ref, v_ref, d_ref, mo_ref, vo_ref):
        gv = g_ref[...]
        mn = ADAM_B1 * m_ref[...] + (1.0 - ADAM_B1) * gv
        vn = ADAM_B2 * v_ref[...] + (1.0 - ADAM_B2) * (gv * gv)
        mo_ref[...] = mn
        vo_ref[...] = vn
        d_ref[...] = -ADAM_LR * ((mn / c1) / (jnp.sqrt(vn / c2) + ADAM_EPS) + ADAM_WD * w_ref[...])

    spec = pl.BlockSpec((tr, cols), lambda i: (i, 0))
    shp = jax.ShapeDtypeStruct((rows, cols), F32)
    return pl.pallas_call(body, out_shape=(shp, shp, shp), grid=(rows // tr,), in_specs=[spec] * 4,
                          out_specs=(spec,) * 3, compiler_params=_params(("parallel",)), name=name)(w, g, m, v)


def _pick_rows(rows, row_bytes, limit=1 << 20):
    for cand in (2048, 1024, 512, 256, 128, 64, 32, 16):
        if rows % cand == 0 and cand * row_bytes <= limit:
            return cand
    return rows


def _as3d(a, lead):
    return a.reshape(a.shape[:lead] + (-1, a.shape[-1]))


def _pair_sum(g, got, core, name):
    h = got.shape[0]
    g3, got3 = _as3d(g, 1), _as3d(got, 1)
    _, rows, cols = got3.shape
    tr = _pick_rows(rows, cols * 4)

    def body(c_ref, g_ref, r_ref, o_ref):
        o_ref[...] = (g_ref[...] + r_ref[...]).astype(BF16)

    out = pl.pallas_call(
        body, out_shape=jax.ShapeDtypeStruct(got3.shape, BF16),
        grid_spec=pltpu.PrefetchScalarGridSpec(
            num_scalar_prefetch=1, grid=(h, rows // tr),
            in_specs=[pl.BlockSpec((None, tr, cols), lambda l, i, c_ref: (c_ref[0] * h + l, i, 0)),
                      pl.BlockSpec((None, tr, cols), lambda l, i, c_ref: (l, i, 0))],
            out_specs=pl.BlockSpec((None, tr, cols), lambda l, i, c_ref: (l, i, 0))),
        compiler_params=_params(("parallel", "parallel")), name=name)(core, g3, got3)
    return out.reshape(got.shape)


def _sum4(q, core, name):
    q4 = _as3d(q, 2)
    _, h, rows, cols = q4.shape
    tr = _pick_rows(rows, cols * 4)

    def body(c_ref, q0, q1, q2, q3, o_ref):
        o_ref[...] = ((q0[...].astype(F32) + q1[...].astype(F32)) + q2[...].astype(F32)) + q3[...].astype(F32)

    out = pl.pallas_call(
        body, out_shape=jax.ShapeDtypeStruct((2 * h, rows, cols), F32),
        grid_spec=pltpu.PrefetchScalarGridSpec(
            num_scalar_prefetch=1, grid=(h, rows // tr),
            in_specs=[pl.BlockSpec((None, None, tr, cols), lambda l, i, c_ref, k=k: (k, l, i, 0))
                      for k in range(N_CHIPS)],
            out_specs=pl.BlockSpec((None, tr, cols), lambda l, i, c_ref: (c_ref[0] * h + l, i, 0))),
        compiler_params=_params(("parallel", "parallel")), name=name)(core, q4, q4, q4, q4)
    return out.reshape((2 * h,) + q.shape[2:])


def _coords():
    return lax.axis_index("x"), lax.axis_index("y"), lax.axis_index("c")


def _other_chips(x, y):
    return [(1 - x, y), (x, 1 - y), (1 - x, 1 - y)]


def _allgather_halves(src, name):
    rows, cols = src.shape

    def body(x_ref, o_ref, send, recv, local):
        x, y, c = _coords()
        sib = (x, y, 1 - c)
        chips = _other_chips(x, y)

        def slot(h, cx, cy):
            return o_ref.at[h, 2 * cx + cy]

        def copy(kk, dst, to, src_ref):
            return pltpu.make_async_remote_copy(src_ref=src_ref, dst_ref=dst, send_sem=send.at[kk],
                                                recv_sem=recv.at[kk], device_id=to, device_id_type=MESH)

        mine = pltpu.make_async_copy(x_ref, slot(c, x, y), local)
        mine.start()
        first = [copy(0, slot(c, x, y), sib, x_ref)]
        first += [copy(1 + j, slot(c, x, y), (*chip, c), x_ref) for j, chip in enumerate(chips)]
        for cp in first:
            cp.start()
        passed = [copy(4 + j, slot(c, *chip), sib, slot(c, *chip)) for j, chip in enumerate(chips)]
        for j, chip in enumerate(chips):
            copy(1 + j, slot(c, *chip), (x, y, c), x_ref).wait_recv()
            passed[j].start()
        copy(0, slot(1 - c, x, y), (x, y, c), x_ref).wait_recv()
        for j, chip in enumerate(chips):
            copy(4 + j, slot(1 - c, *chip), (x, y, c), x_ref).wait_recv()
        for cp in first + passed:
            cp.wait_send()
        mine.wait()

    return pl.pallas_call(
        body, out_shape=jax.ShapeDtypeStruct((2, N_CHIPS, rows, cols), src.dtype),
        in_specs=[ANY], out_specs=ANY,
        scratch_shapes=[pltpu.SemaphoreType.DMA((7,)), pltpu.SemaphoreType.DMA((7,)), pltpu.SemaphoreType.DMA],
        name=name)(src)


MIXW = (("ssd_w_in", None), ("ssd_w_out", 0), ("pool_w", 1))
FFNW = (("ffn_w_up", 1), ("ffn_w_down", 0))


def _chip_window(axis, ref, layers, k):
    if axis is None:
        return ref.at[layers, k]
    n = ref.shape[1 + axis] // N_CHIPS
    sl = pl.ds(pl.multiple_of(k * n, LANES if 1 + axis == len(ref.shape) - 1 else 8), n)
    idx = [layers] + [slice(None)] * (len(ref.shape) - 1)
    idx[1 + axis] = sl
    return ref.at[tuple(idx)]


def _full_shape(axis, shard_shape):
    if axis is None:
        return (shard_shape[0], N_CHIPS) + tuple(shard_shape[1:])
    full = list(shard_shape)
    full[1 + axis] *= N_CHIPS
    return tuple(full)


HBM_SPEC = pl.BlockSpec(memory_space=pltpu.HBM)
SEM_SPEC = pl.BlockSpec(memory_space=pltpu.SEMAPHORE)


def _dma_sems(count):
    return pltpu.SemaphoreType.DMA((max(count, 1),))


def _wait_for(copy, kind):
    if kind == "recv":
        copy.wait_recv()
    elif kind == "send":
        copy.wait_send()
    else:
        copy.wait()


def _comm_fused(stages, counts, srcs, lands, name, inplace=False):
    ns, nl, k = len(srcs), len(lands), len(stages)

    def body(*refs):
        src_refs = refs[:ns]
        land_refs = refs[ns + (nl if inplace else 0):ns + (nl if inplace else 0) + nl]
        sem_refs = refs[len(refs) - 3 * k:]
        for s, stage_fn in enumerate(stages):
            starts, waits = stage_fn(src_refs, land_refs, tuple(sem_refs[3 * s:3 * s + 3]))
            for cp in starts:
                cp.start()
            for cp, kind in waits:
                _wait_for(cp, kind)

    scratch = []
    for cnt in counts:
        scratch += [_dma_sems(c) for c in cnt]
    outs = pl.pallas_call(
        body, out_shape=tuple(jax.ShapeDtypeStruct(a.shape, a.dtype) for a in lands),
        in_specs=[ANY] * (ns + (nl if inplace else 0)), out_specs=(ANY,) * nl,
        input_output_aliases={ns + i: i for i in range(nl)} if inplace else {},
        scratch_shapes=scratch, name=name)(*srcs, *(lands if inplace else ()))
    return list(outs)


class _SplitComm:
    def __init__(self, stages, counts, srcs, lands, name):
        self.stages, self.counts, self.name = stages, counts, name
        self.ns = len(srcs)
        self.data = [pltpu.with_memory_space_constraint(a, pltpu.HBM) for a in list(srcs) + list(lands)]
        self.sems = None
        self.step = 0

    def advance(self, after=None):
        i, k, nd, ns = self.step, len(self.stages), len(self.data), self.ns
        first, last = i == 0, i == k
        stages = self.stages

        def body(*refs):
            data = refs[:nd]
            pos = nd
            if not first:
                old = tuple(refs[pos:pos + 3])
                pos += 4
            if not last:
                new = tuple(refs[pos:pos + 3])
            if not first:
                for cp, kind in stages[i - 1](data[:ns], data[ns:], old)[1]:
                    _wait_for(cp, kind)
            if not last:
                for cp in stages[i](data[:ns], data[ns:], new)[0]:
                    cp.start()
                refs[len(refs) - 1][...] = jnp.zeros((8, LANES), F32)

        args = list(self.data)
        in_specs = [HBM_SPEC] * nd
        if not first:
            args += list(self.sems) + [after]
            in_specs += [SEM_SPEC] * 3 + [ANY]
        out_shape, out_specs = [], []
        if not last:
            out_shape += [_dma_sems(c) for c in self.counts[i]]
            out_specs += [SEM_SPEC] * 3
        out_shape += [pltpu.HBM(a.shape, a.dtype) for a in self.data]
        out_specs += [HBM_SPEC] * nd
        if not last:
            out_shape.append(jax.ShapeDtypeStruct((8, LANES), F32))
            out_specs.append(pl.BlockSpec(memory_space=pltpu.VMEM))
        off = 0 if last else 3
        outs = pl.pallas_call(
            body, out_shape=tuple(out_shape), in_specs=in_specs, out_specs=tuple(out_specs),
            input_output_aliases={d: off + d for d in range(nd)},
            compiler_params=pltpu.CompilerParams(has_side_effects=pltpu.SideEffectType.DATAFLOW_SIDE_EFFECTING),
            name=f"{self.name}_{i}")(*args)
        self.sems = None if last else outs[:3]
        self.data = list(outs[off:off + nd])
        self.step += 1
        return None if last else outs[len(outs) - 1]

    def lands(self):
        return self.data[self.ns:]


def _gather_stages(spec):
    n = len(spec)

    def parts(srcs, lands):
        x, y, c = _coords()
        out = []
        for w, (_, axis) in enumerate(spec):
            h = srcs[w].shape[0] // 2
            mine, theirs = pl.ds(c * h, h), pl.ds((1 - c) * h, h)
            out.append((srcs[w].at[mine], lambda layers, k, w=w, axis=axis: _chip_window(axis, lands[w], layers, k),
                        mine, theirs))
        return x, y, c, 2 * x + y, (x, y, 1 - c), _other_chips(x, y), out

    def remote(src, dst, send, recv, idx, to):
        return pltpu.make_async_remote_copy(src_ref=src, dst_ref=dst, send_sem=send.at[idx], recv_sem=recv.at[idx],
                                            device_id=to, device_id_type=MESH)

    def stage0(srcs, lands, sems):
        send, recv, local = sems
        x, y, c, me, sib, chips, ps = parts(srcs, lands)
        starts, waits = [], []
        for w, (src, dst, mine, theirs) in enumerate(ps):
            lc = pltpu.make_async_copy(src, dst(mine, me), local.at[w])
            first = [remote(src, dst(mine, me), send, recv, 4 * w, sib)]
            first += [remote(src, dst(mine, me), send, recv, 4 * w + 1 + j, (cx, cy, c)) for j, (cx, cy) in enumerate(chips)]
            starts += [lc] + first
            waits.append((remote(src, dst(theirs, me), send, recv, 4 * w, (x, y, c)), "recv"))
            waits += [(remote(src, dst(mine, 2 * cx + cy), send, recv, 4 * w + 1 + j, (x, y, c)), "recv")
                      for j, (cx, cy) in enumerate(chips)]
            waits += [(cp, "send") for cp in first] + [(lc, "local")]
        return starts, waits

    def stage1(srcs, lands, sems):
        send, recv, _ = sems
        x, y, c, me, sib, chips, ps = parts(srcs, lands)
        starts, waits = [], []
        for w, (src, dst, mine, theirs) in enumerate(ps):
            for j, (cx, cy) in enumerate(chips):
                blk = dst(mine, 2 * cx + cy)
                fwd = remote(blk, blk, send, recv, 3 * w + j, sib)
                starts.append(fwd)
                waits.append((remote(src, dst(theirs, 2 * cx + cy), send, recv, 3 * w + j, (x, y, c)), "recv"))
                waits.append((fwd, "send"))
        return starts, waits

    return [stage0, stage1], [(4 * n, 4 * n, n), (3 * n, 3 * n, 0)]


def _swap_stages(spec):
    n = len(spec)

    def stage(srcs, lands, sems):
        send, recv, _ = sems
        x, y, c = _coords()
        starts, waits = [], []
        for w in range(n):
            h = srcs[w].shape[0] // 2
            cp = pltpu.make_async_remote_copy(src_ref=srcs[w].at[pl.ds((1 - c) * h, h)], dst_ref=lands[w],
                                              send_sem=send.at[w], recv_sem=recv.at[w],
                                              device_id=(x, y, 1 - c), device_id_type=MESH)
            starts.append(cp)
            waits += [(cp, "recv"), (cp, "send")]
        return starts, waits

    return [stage], [(n, n, 0)]


def _scatter_stages(spec):
    n = len(spec)

    def stage(srcs, lands, sems):
        send, recv, local = sems
        x, y, c = _coords()
        me = 2 * x + y
        starts, waits = [], []
        for w, (_, axis) in enumerate(spec):
            layers = pl.ds(0, srcs[w].shape[0])
            own = _chip_window(axis, srcs[w], layers, me)
            lc = pltpu.make_async_copy(own, lands[w].at[me], local.at[w])
            starts.append(lc)
            for j, (cx, cy) in enumerate(_other_chips(x, y)):
                cp = pltpu.make_async_remote_copy(src_ref=_chip_window(axis, srcs[w], layers, 2 * cx + cy),
                                                  dst_ref=lands[w].at[me], send_sem=send.at[3 * w + j],
                                                  recv_sem=recv.at[3 * w + j], device_id=(cx, cy, c), device_id_type=MESH)
                starts.append(cp)
                waits.append((pltpu.make_async_remote_copy(
                    src_ref=own, dst_ref=lands[w].at[2 * cx + cy], send_sem=send.at[3 * w + j], recv_sem=recv.at[3 * w + j],
                    device_id=(x, y, c), device_id_type=MESH), "recv"))
                waits.append((cp, "send"))
            waits.append((lc, "local"))
        return starts, waits

    return [stage], [(3 * n, 3 * n, n)]


def _share_stages(spec):
    n = len(spec)

    def stage(srcs, lands, sems):
        send, recv, _ = sems
        x, y, c = _coords()
        starts, waits = [], []
        for w in range(n):
            h = lands[w].shape[0] // 2
            mine, theirs = lands[w].at[pl.ds(c * h, h)], lands[w].at[pl.ds((1 - c) * h, h)]
            cp = pltpu.make_async_remote_copy(src_ref=mine, dst_ref=mine, send_sem=send.at[w], recv_sem=recv.at[w],
                                              device_id=(x, y, 1 - c), device_id_type=MESH)
            starts.append(cp)
            waits.append((pltpu.make_async_remote_copy(src_ref=theirs, dst_ref=theirs, send_sem=send.at[w],
                                                       recv_sem=recv.at[w], device_id=(x, y, c), device_id_type=MESH),
                          "recv"))
            waits.append((cp, "send"))
        return starts, waits

    return [stage], [(n, n, 0)]


def _shard_of(p, axis):
    if axis is None:
        return (p.shape[0],) + tuple(p.shape[2:])
    s = list(p.shape)
    s[1 + axis] //= N_CHIPS
    return tuple(s)


def _reduce_begin(spec, gs, core, tag):
    stages, counts = _swap_stages(spec)
    got = _comm_fused(stages, counts, gs,
                      [jax.ShapeDtypeStruct((g.shape[0] // 2,) + g.shape[1:], g.dtype) for g in gs], "swap_" + tag)
    pair = [_pair_sum(a, r, core, "pair_sum_" + n) for a, r, (n, _) in zip(gs, got, spec)]
    stages, counts = _scatter_stages(spec)
    lands = [lax.empty((N_CHIPS,) + _shard_of(p, axis), p.dtype) for p, (_, axis) in zip(pair, spec)]
    comm = _SplitComm(stages, counts, pair, lands, "scatter_" + tag)
    return comm, comm.advance()


def _reduce_finish(spec, comm, core, tag, after):
    comm.advance(after=after)
    halves = [_sum4(q, core, "sum4_" + n) for q, (n, _) in zip(comm.lands(), spec)]
    stages, counts = _share_stages(spec)
    return _comm_fused(stages, counts, [], halves, "share_" + tag, inplace=True)


def _allreduce_small(vec, name, after=()):
    rows, cols = vec.shape
    after = list(after)

    def body(x_ref, *rest):
        o_ref, buf, send, recv = rest[len(after):]
        x, y, c = _coords()
        me = 4 * x + 2 * y + c
        buf[me] = x_ref[...]
        cps = []
        for kk in range(1, 8):
            dx, dy, dc = (kk >> 2) & 1, (kk >> 1) & 1, kk & 1
            to = (1 - x if dx else x, 1 - y if dy else y, 1 - c if dc else c)
            cp = pltpu.make_async_remote_copy(src_ref=x_ref, dst_ref=buf.at[me], send_sem=send.at[kk - 1],
                                              recv_sem=recv.at[kk - 1], device_id=to, device_id_type=MESH)
            cp.start()
            cps.append((cp, 4 * to[0] + 2 * to[1] + to[2]))
        for kk, (cp, frm) in enumerate(cps):
            pltpu.make_async_remote_copy(src_ref=x_ref, dst_ref=buf.at[frm], send_sem=send.at[kk],
                                         recv_sem=recv.at[kk], device_id=(x, y, c), device_id_type=MESH).wait_recv()
        for cp, _ in cps:
            cp.wait_send()
        acc = buf[0]
        for kk in range(1, 8):
            acc = acc + buf[kk]
        o_ref[...] = acc

    vm = pl.BlockSpec(memory_space=pltpu.VMEM)
    return pl.pallas_call(
        body, out_shape=jax.ShapeDtypeStruct((rows, cols), F32), in_specs=[vm] + [ANY] * len(after), out_specs=vm,
        scratch_shapes=[pltpu.VMEM((8, rows, cols), F32), pltpu.SemaphoreType.DMA((7,)), pltpu.SemaphoreType.DMA((7,))],
        compiler_params=_params(), name=name)(vec, *after)


SMALL = (("ssd_conv_w", 2), ("pool_scale", 1), ("ffn_conv_w", 2))
REPL = ("ssd_conv_b", "ssd_dt_bias", "ssd_a_log", "ssd_d", "ssd_norm_w", "ffn_conv_b",
        "norm_mix_pre", "norm_mix_post", "norm_ffn_pre", "norm_ffn_post")
WEIGHTS = ("ssd_w_in", "ssd_conv_w", "ssd_conv_b", "ssd_dt_bias", "ssd_a_log", "ssd_d", "ssd_norm_w", "ssd_w_out",
           "pool_w", "pool_scale", "ffn_w_up", "ffn_conv_w", "ffn_conv_b", "ffn_w_down", "norm_mix_pre",
           "norm_mix_post", "norm_ffn_pre", "norm_ffn_post")


def _flat_rows(n):
    unit = 2 * 16 * FLAT_COLS
    return 2 * 16 * ((n + unit - 1) // unit)


def _flatten_shards(arrs, dtype):
    flat = jnp.concatenate([a.astype(dtype).reshape(-1) for a in arrs])
    rows = _flat_rows(flat.shape[0])
    flat = jnp.pad(flat, (0, rows * FLAT_COLS - flat.shape[0]))
    return flat.reshape(2, rows // 2, FLAT_COLS)


def _unflatten_full(gathered, shard_shapes, axes):
    per_chip = jnp.swapaxes(gathered, 0, 1).reshape(N_CHIPS, -1)
    out, off = [], 0
    for shp, ax in zip(shard_shapes, axes):
        n = math.prod(shp)
        pieces = [per_chip[k, off:off + n].reshape(shp) for k in range(N_CHIPS)]
        out.append(jnp.concatenate(pieces, axis=ax))
        off += n
    return out


def kernel(x, ssd_w_in, ssd_conv_w, ssd_conv_b, ssd_dt_bias, ssd_a_log, ssd_d, ssd_norm_w, ssd_w_out, pool_w, pool_scale, ffn_w_up, ffn_conv_w, ffn_conv_b, ffn_w_down, norm_mix_pre, norm_mix_post, norm_ffn_pre, norm_ffn_post, loss_target, m_ssd_w_in, m_ssd_conv_w, m_ssd_conv_b, m_ssd_dt_bias, m_ssd_a_log, m_ssd_d, m_ssd_norm_w, m_ssd_w_out, m_pool_w, m_pool_scale, m_ffn_w_up, m_ffn_conv_w, m_ffn_conv_b, m_ffn_w_down, m_norm_mix_pre, m_norm_mix_post, m_norm_ffn_pre, m_norm_ffn_post, v_ssd_w_in, v_ssd_conv_w, v_ssd_conv_b, v_ssd_dt_bias, v_ssd_a_log, v_ssd_d, v_ssd_norm_w, v_ssd_w_out, v_pool_w, v_pool_scale, v_ffn_w_up, v_ffn_conv_w, v_ffn_conv_b, v_ffn_w_down, v_norm_mix_pre, v_norm_mix_post, v_norm_ffn_pre, v_norm_ffn_post):
    wts = dict(ssd_w_in=ssd_w_in, ssd_conv_w=ssd_conv_w, ssd_conv_b=ssd_conv_b, ssd_dt_bias=ssd_dt_bias,
               ssd_a_log=ssd_a_log, ssd_d=ssd_d, ssd_norm_w=ssd_norm_w, ssd_w_out=ssd_w_out, pool_w=pool_w,
               pool_scale=pool_scale, ffn_w_up=ffn_w_up, ffn_conv_w=ffn_conv_w, ffn_conv_b=ffn_conv_b,
               ffn_w_down=ffn_w_down, norm_mix_pre=norm_mix_pre, norm_mix_post=norm_mix_post,
               norm_ffn_pre=norm_ffn_pre, norm_ffn_post=norm_ffn_post)
    mom = dict(ssd_w_in=m_ssd_w_in, ssd_conv_w=m_ssd_conv_w, ssd_conv_b=m_ssd_conv_b, ssd_dt_bias=m_ssd_dt_bias,
               ssd_a_log=m_ssd_a_log, ssd_d=m_ssd_d, ssd_norm_w=m_ssd_norm_w, ssd_w_out=m_ssd_w_out, pool_w=m_pool_w,
               pool_scale=m_pool_scale, ffn_w_up=m_ffn_w_up, ffn_conv_w=m_ffn_conv_w, ffn_conv_b=m_ffn_conv_b,
               ffn_w_down=m_ffn_w_down, norm_mix_pre=m_norm_mix_pre, norm_mix_post=m_norm_mix_post,
               norm_ffn_pre=m_norm_ffn_pre, norm_ffn_post=m_norm_ffn_post)
    var = dict(ssd_w_in=v_ssd_w_in, ssd_conv_w=v_ssd_conv_w, ssd_conv_b=v_ssd_conv_b, ssd_dt_bias=v_ssd_dt_bias,
               ssd_a_log=v_ssd_a_log, ssd_d=v_ssd_d, ssd_norm_w=v_ssd_norm_w, ssd_w_out=v_ssd_w_out, pool_w=v_pool_w,
               pool_scale=v_pool_scale, ffn_w_up=v_ffn_w_up, ffn_conv_w=v_ffn_conv_w, ffn_conv_b=v_ffn_conv_b,
               ffn_w_down=v_ffn_w_down, norm_mix_pre=v_norm_mix_pre, norm_mix_post=v_norm_mix_post,
               norm_ffn_pre=v_norm_ffn_pre, norm_ffn_post=v_norm_ffn_post)

    bl, seq, d = x.shape
    t = bl * seq
    depth = norm_mix_pre.shape[0]
    n_ssd = ssd_w_out.shape[0]
    d_inner = ssd_w_out.shape[1] * N_CHIPS
    nheads = d_inner // HEAD_DIM
    hpg = nheads // N_GROUPS
    gw = d_inner // N_GROUPS
    xbc = ssd_conv_w.shape[2] * N_CHIPS
    f2 = ffn_w_up.shape[2] * N_CHIPS
    ff = f2 // 2
    dg = d // 4
    cy = lax.axis_index("c")
    chip = 2 * lax.axis_index("x") + lax.axis_index("y")

    small_shapes = [wts[n].shape for n, _ in SMALL]
    small_axes = [a for _, a in SMALL]
    small_flat = _flatten_shards([wts[n] for n, _ in SMALL], F32)
    small_half = lax.dynamic_index_in_dim(small_flat, cy, 0, keepdims=False)
    small_all = _allgather_halves(small_half, "gather_small")
    conv_w, p_scale, f_conv_w = _unflatten_full(small_all, small_shapes, small_axes)
    def full_shapes(spec, shards):
        return [jax.ShapeDtypeStruct(_full_shape(axis, s.shape), s.dtype) for s, (_, axis) in zip(shards, spec)]

    def row_halves(a):
        return a.reshape((2, a.shape[0] // 2) + a.shape[1:])

    def join_w_in(g):
        return jnp.concatenate([g[:, k] for k in range(N_CHIPS)], axis=-1).reshape(d, -1)

    def join_w_out(g):
        r2 = g.shape[1] // N_CHIPS
        return jnp.concatenate([g[hf, k * r2:(k + 1) * r2] for k in range(N_CHIPS) for hf in range(2)], axis=0)

    ssd_spec = (("ssd_w_in", None), ("ssd_w_out", 0))
    first_shards = [row_halves(wts[n][0].astype(BF16)) for n, _ in ssd_spec]
    stages, counts = _gather_stages(ssd_spec)
    g_in0, g_out0 = _comm_fused(stages, counts, first_shards, full_shapes(ssd_spec, first_shards), "gather_first")
    w_in, w_out = [join_w_in(g_in0)], [join_w_out(g_out0)]
    rest_spec = ssd_spec * (n_ssd - 1) + (("pool_w", 1),) + FFNW
    rest_shards = [row_halves(wts[n][jj].astype(BF16)) for jj in range(1, n_ssd) for n, _ in ssd_spec]
    rest_shards += [wts["pool_w"].astype(BF16)] + [wts[n].astype(BF16) for n, _ in FFNW]
    stages, counts = _gather_stages(rest_spec)
    ffn_gather = _SplitComm(stages, counts, rest_shards + [g_out0],
                            [lax.empty(s.shape, s.dtype) for s in full_shapes(rest_spec, rest_shards)], "gather_rest")
    gather_token = ffn_gather.advance()

    def pad_heads(a):
        lead = a.shape[:-1]
        a = a.reshape(lead + (N_GROUPS, hpg))
        a = jnp.pad(a, [(0, 0)] * len(lead) + [(0, 0), (0, LANES - hpg)])
        return a.reshape(lead + (N_GROUPS * LANES,))

    def unpad_heads(a):
        lead = a.shape[:-1]
        return a.reshape(lead + (N_GROUPS, LANES))[..., :hpg].reshape(lead + (nheads,))

    def group_rows(a, width):
        return jnp.broadcast_to(a.reshape(N_GROUPS, 1, width), (N_GROUPS, 8, width))

    def pad_w_in(w):
        return jnp.concatenate([w[..., :d_inner + xbc], pad_heads(w[..., d_inner + xbc:])], axis=-1)

    w_in_p = [pad_w_in(w_in[0])]
    zw = w_in_p[0].shape[-1]
    w_pool = None

    x2 = x.reshape(t, d)
    tgt2 = loss_target.reshape(t, d)
    w_up = w_down = None

    saved = []
    cur = x2
    tokens = []
    h = _norm_fwd(cur, norm_mix_pre[0:1], BF16, "norm_pre_b", after=[gather_token])
    for i in range(depth):
        j = i // 2
        sv = dict(x_in=cur)
        if i % 2 == 0:
            zx = _mm(h, w_in_p[j], "nn", F32, "mm_ssd_in", 2048, 512, d).reshape(bl, seq, zw)
            xc, xpre = _ssd_conv_fwd(zx, conv_w[j], ssd_conv_b[j:j + 1], d_inner, "ssd_conv_fwd")
            dtb = group_rows(pad_heads(ssd_dt_bias[j]), LANES)
            alog = group_rows(pad_heads(ssd_a_log[j]), LANES)
            dskip = group_rows(jnp.repeat(ssd_d[j], HEAD_DIM), gw)
            nw = group_rows(ssd_norm_w[j], gw)
            y, yn, st = _ssd_fwd(xc, zx, dtb, alog, dskip, nw, d_inner, "ssd_fwd")
            if i == 0:
                tokens.append(ffn_gather.advance(after=yn))
            mix = _mm(yn.reshape(t, d_inner), w_out[j], "nn", F32, "mm_ssd_out", 512, 512, d_inner)
            sv.update(h=h, zx=zx, xc=xc, xpre=xpre, y=y, yn=yn, st=st, dtb=dtb, alog=alog, dskip=dskip, nw=nw)
        else:
            mix = _pool_fwd(h.reshape(bl, seq, d), w_pool[j], p_scale[j:j + 1], "pool_fwd").reshape(t, d)
            sv.update(h=h)
        sv.update(mix=mix)
        mid, u = _norm_post_pre(mix, norm_mix_post[i:i + 1], cur, norm_ffn_pre[i:i + 1], BF16, "norm_post_pre_b",
                                after=tokens)
        tokens = []
        if i == 0:
            ffn_gather.advance(after=u)
            rest = ffn_gather.lands()
            for jj in range(1, n_ssd):
                w_in_p.append(pad_w_in(join_w_in(rest[2 * (jj - 1)])))
                w_out.append(join_w_out(rest[2 * (jj - 1) + 1]))
            w_pool, w_up, w_down = rest[2 * (n_ssd - 1):]
        hpre = _mm(u, w_up, "nn", BF16, "mm_up", 2048, 512, d, b_layer=i).reshape(bl, seq, f2)
        act, pre_g, pre_v = _ffn_act_fwd(hpre, f_conv_w[i], ffn_conv_b[i:i + 1], "ffn_act_fwd")
        act = act.reshape(t, ff)
        fo = _mm(act, w_down, "nn", F32, "mm_down", 1024, 512, ff, b_layer=i)
        if i + 1 == depth:
            cur = _norm_fwd(fo, norm_ffn_post[i:i + 1], F32, "norm_post", resid=mid)
        elif i % 2 == 0:
            cur, h = _norm_post_pre(fo, norm_ffn_post[i:i + 1], mid, norm_mix_pre[i + 1:i + 2], F32, "norm_post_pre_f")
        else:
            cur, h = _norm_post_pre(fo, norm_ffn_post[i:i + 1], mid, norm_mix_pre[i + 1:i + 2], BF16, "norm_post_pre_b")
        sv.update(mid=mid, u=u, hpre=hpre, pre_g=pre_g, pre_v=pre_v, act=act, fo=fo)
        saved.append(sv)

    dcur, loss_part = _loss_head(cur, tgt2, "loss_head")

    g = {n: [None] * wts[n].shape[0] for n in WEIGHTS}
    gbuf = dict(up=lax.empty((depth, d, f2), F32), down=lax.empty((depth, ff, d), F32),
                out=lax.empty((n_ssd, d_inner, d), F32), win=lax.empty((n_ssd, d, zw), F32))
    core = cy.reshape(1).astype(jnp.int32)

    def mixer_bwd(i, dmid, behind=()):
        j = i // 2
        sv = saved[i]
        done = []
        if i % 2 == 0:
            dmix, g["norm_mix_post"][i] = _norm_bwd(sv["mix"], norm_mix_post[i:i + 1], dmid, BF16, "norm_bwd_b",
                                                    after=behind)
            dyn = _mm(dmix, w_out[j], "nt", F32, "mm_ssd_out_dx", 1024, 1024, d)
            gbuf["out"], tok = _mm(sv["yn"].reshape(t, d_inner), dmix, "tn", F32, "mm_ssd_out_dw", 1024, 512, 2048,
                                   out_buf=(gbuf["out"], j))
            done.append(tok)
            dz, dxs, dbm, dcm, ddt, dnw, dd, dal, dbias = _ssd_bwd(
                sv["xc"], sv["zx"], sv["y"], dyn.reshape(bl, seq, d_inner), sv["st"], sv["dtb"], sv["alog"],
                sv["dskip"], sv["nw"], d_inner, "ssd_bwd")
            g["ssd_norm_w"][j] = dnw[:, 0, :].reshape(d_inner)
            g["ssd_d"][j] = dd[:, 0, :hpg].reshape(nheads)
            g["ssd_a_log"][j] = dal[:, 0, :hpg].reshape(nheads)
            g["ssd_dt_bias"][j] = dbias[:, 0, :hpg].reshape(nheads)
            dxbc, dcw, dcb = _ssd_conv_bwd(sv["zx"], sv["xpre"], (dxs, dbm, dcm), conv_w[j], d_inner, "ssd_conv_bwd")
            g["ssd_conv_w"][j] = dcw
            g["ssd_conv_b"][j] = dcb[0]
            dzs = [dz.reshape(t, d_inner), dxbc.reshape(t, xbc), ddt.reshape(t, N_GROUPS * LANES)]
            dh = _mm(dzs, w_in_p[j], "nt", BF16, "mm_ssd_in_dx", 1024, d, [1024, 1024, 512])
            gbuf["win"], tok = _mm(sv["h"], dzs, "tn", F32, "mm_ssd_in_dw", 1024, 512, 2048, out_buf=(gbuf["win"], j))
            done.append(tok)
        else:
            dmix, g["norm_mix_post"][i] = _norm_bwd(sv["mix"], norm_mix_post[i:i + 1], dmid, F32, "norm_bwd_f",
                                                    after=behind)
            dh3, g["pool_w"][j], dps = _pool_bwd(sv["h"].reshape(bl, seq, d), dmix.reshape(bl, seq, d), w_pool[j],
                                                 p_scale[j:j + 1], "pool_bwd")
            g["pool_scale"][j] = dps[0]
            dh = dh3.reshape(t, d)
        dx_in, g["norm_mix_pre"][i] = _norm_bwd(sv["x_in"], norm_mix_pre[i:i + 1], dh, F32, "norm_bwd_r", resid=dmid,
                                                after=done)
        return dx_in

    ffn_comm = None
    for i in reversed(range(depth)):
        sv = saved[i]
        dfo, g["norm_ffn_post"][i] = _norm_bwd(sv["fo"], norm_ffn_post[i:i + 1], dcur, BF16, "norm_bwd_b")
        dact = _mm(dfo, w_down, "nt", BF16, "mm_down_dx", 1024, ff // 2, d, b_layer=i)
        gbuf["down"], tok_down = _mm(sv["act"], dfo, "tn", F32, "mm_down_dw", ff // 2, 512, 2048,
                                     out_buf=(gbuf["down"], i))
        dhg, dhv, dcw, dcb = _ffn_act_bwd(sv["hpre"], sv["pre_g"], sv["pre_v"], dact.reshape(bl, seq, ff), f_conv_w[i],
                                          "ffn_act_bwd")
        g["ffn_conv_w"][i] = dcw
        g["ffn_conv_b"][i] = dcb[0]
        dhs = [dhg.reshape(t, ff), dhv.reshape(t, ff)]
        du = _mm(dhs, w_up, "nt", BF16, "mm_up_dx", 1024, d, ff, b_layer=i)
        gbuf["up"], tok_up = _mm(sv["u"], dhs, "tn", F32, "mm_up_dw", 512, ff // 2, 2048, out_buf=(gbuf["up"], i))
        dmid, g["norm_ffn_pre"][i] = _norm_bwd(sv["mid"], norm_ffn_pre[i:i + 1], du, F32, "norm_bwd_r", resid=dcur,
                                               after=[tok_down, tok_up])
        if i > 0:
            dcur = mixer_bwd(i, dmid)
        else:
            ffn_comm, ffn_token = _reduce_begin(FFNW, [gbuf["up"], gbuf["down"]], core, "ffn")
            dcur = mixer_bwd(0, dmid, behind=[ffn_token])

    grad_x = dcur.reshape(bl, seq, d)
    for n in ("norm_mix_pre", "norm_mix_post", "norm_ffn_pre", "norm_ffn_post"):
        g[n] = [a[0] for a in g[n]]
    small_names = [n for n, _ in SMALL] + list(REPL)
    full = {n: jnp.stack(g[n], axis=0) for n in small_names}

    g_in = jnp.concatenate([gbuf["win"][..., :d_inner + xbc], unpad_heads(gbuf["win"][..., d_inner + xbc:])], axis=-1)
    g_in_cm = jnp.swapaxes(g_in.reshape(n_ssd, d, N_CHIPS, -1), 1, 2)
    mix_comm, mix_token = _reduce_begin(MIXW, [g_in_cm, gbuf["out"], jnp.stack(g["pool_w"], axis=0)], core, "mixers")

    grads, deltas, new_m, new_v = {}, {}, {}, {}

    def adamw(n, gr):
        shp = wts[n].shape
        two = (math.prod(shp[:-1]), shp[-1])
        dl, mn, vn = _adamw(wts[n].reshape(two), gr.reshape(two), mom[n].reshape(two), var[n].reshape(two),
                            "adamw_" + n)
        grads[n], deltas[n], new_m[n], new_v[n] = gr, dl.reshape(shp), mn.reshape(shp), vn.reshape(shp)
        return dl

    ffn_grads = _reduce_finish(FFNW, ffn_comm, core, "ffn", after=mix_token)
    for gr, (n, _) in zip(ffn_grads, FFNW):
        last = adamw(n, gr)
    mix_grads = _reduce_finish(MIXW, mix_comm, core, "mixers", after=last)
    for gr, (n, _) in zip(mix_grads, MIXW):
        adamw(n, gr)

    vec = jnp.concatenate([full[n].reshape(-1) for n in small_names] + [loss_part[0, :1]])
    nvec = vec.shape[0]
    vrows = 8 * ((nvec + 8 * FLAT_COLS - 1) // (8 * FLAT_COLS))
    vec = jnp.pad(vec, (0, vrows * FLAT_COLS - nvec)).reshape(vrows, FLAT_COLS)
    tot = _allreduce_small(vec, "allreduce_small", after=[mix_grads[0]]).reshape(-1)
    small_grads, off = {}, 0
    for n in small_names:
        cnt = math.prod(full[n].shape)
        small_grads[n] = tot[off:off + cnt].reshape(full[n].shape)
        off += cnt
    loss = tot[off]
    for n, ax in SMALL:
        w = wts[n].shape[ax]
        small_grads[n] = lax.dynamic_slice_in_dim(small_grads[n], chip * w, w, axis=ax)

    for n in small_names:
        adamw(n, small_grads[n])

    return (loss, grad_x, *[grads[n] for n in WEIGHTS], *[deltas[n] for n in WEIGHTS],
            *[new_m[n] for n in WEIGHTS], *[new_v[n] for n in WEIGHTS])
```

```python
import functools
import math

import jax
import jax.numpy as jnp
from jax import lax
from jax.experimental import pallas as pl
from jax.experimental.pallas import tpu as pltpu

F32 = jnp.float32
BF16 = jnp.bfloat16
MESH = pl.DeviceIdType.MESH
ANY = pl.BlockSpec(memory_space=pl.ANY)

HEAD_DIM = 64
D_STATE = 128
CHUNK = 128
N_GROUPS = 4
SSD_CONV = 4
FFN_CONV = 3
EPS = 1e-6
N_CHIPS = 4
LANES = 128
FLAT_COLS = 1024

ADAM_LR = 0.001
ADAM_B1 = 0.9
ADAM_B2 = 0.999
ADAM_EPS = 1e-08
ADAM_WD = 0.01
ADAM_STEP = 10

VMEM_LIMIT_BYTES = 56 * 1024 * 1024


def _params(sem=None):
    kw = dict(vmem_limit_bytes=VMEM_LIMIT_BYTES)
    if sem is not None:
        kw["dimension_semantics"] = sem
    return pltpu.CompilerParams(**kw)


def _sigmoid(x):
    return 0.5 * jnp.tanh(0.5 * x) + 0.5


def _softplus(x):
    return jnp.maximum(x, 0.0) + jnp.log(1.0 + jnp.exp(-jnp.abs(x)))


def _dot(a, b, dn):
    return lax.dot_general(a, b, (dn, ((), ())), preferred_element_type=F32)


def _nn(a, b):
    return _dot(a, b, ((1,), (0,)))


def _nt(a, b):
    return _dot(a, b, ((1,), (1,)))


def _tn(a, b):
    return _dot(a, b, ((0,), (0,)))


def _split(x, parts):
    out = []
    r = x
    for _ in range(parts):
        p = r.astype(BF16)
        out.append(p)
        r = r - p.astype(F32)
    return out


def _sel_left(sel, x, parts=3):
    n = x.shape[1]
    r = _nn(sel, jnp.concatenate(_split(x, parts), axis=1))
    out = r[:, 0:n]
    for i in range(1, parts):
        out = out + r[:, i * n:(i + 1) * n]
    return out


def _sel_right(x, sel_stacked, parts=3):
    return _nn(jnp.concatenate(_split(x, parts), axis=1), sel_stacked)


def _mm(a, b, dims, out_dtype, name, tm, tn, tk, b_layer=None, out_buf=None):
    a_list = list(a) if isinstance(a, (list, tuple)) else [a]
    b_list = list(b) if isinstance(b, (list, tuple)) else [b]
    if dims in ("nn", "nt"):
        assert len(b_list) == 1
        m = a_list[0].shape[0]
        segs = [x.shape[1] for x in a_list]
        k = sum(segs)
        bshape = b_list[0].shape[-2:]
        n = bshape[1] if dims == "nn" else bshape[0]
        assert (bshape[0] if dims == "nn" else bshape[1]) == k
    else:
        assert len(a_list) == 1 and b_layer is None
        k, m = a_list[0].shape
        segs = [x.shape[1] for x in b_list]
        n = sum(segs)
    nseg = len(segs)
    tm, tn = min(tm, m), min(tn, n)
    if dims == "tn":
        tk = min(tk, k)
        tn = min(tn, min(segs))
        units = [tn] * nseg
        nk = k // tk
        assert k % tk == 0
    else:
        units = [min(u, s) for u, s in zip(tk if isinstance(tk, (list, tuple)) else [tk] * nseg, segs)]
        nk = sum(s // u for s, u in zip(segs, units))
    assert m % tm == 0 and n % tn == 0 and all(s % u == 0 for s, u in zip(segs, units)), (name, m, n, k, segs, units)
    counts = [s // u for s, u in zip(segs, units)]
    starts = [sum(counts[:s]) for s in range(nseg)]
    assert all(sum(segs[:s]) % units[s] == 0 for s in range(nseg)), (name, segs, units)
    first_block = [sum(segs[:s]) // units[s] for s in range(nseg)]
    dn = {"nn": ((1,), (0,)), "nt": ((1,), (1,)), "tn": ((0,), (0,))}[dims]

    same = len(set(units)) == 1
    nb_ops = len(b_list) if dims == "tn" else (1 if same else nseg)

    def body(*refs):
        a_refs = refs[:len(a_list)]
        b_refs = refs[len(a_list):len(a_list) + nb_ops]
        rest = refs[len(a_list) + nb_ops + (0 if out_buf is None else 1):]
        o_ref = rest[0]
        if out_buf is not None:
            rest[1][...] = jnp.zeros((8, LANES), F32)
            rest = rest[1:]
        acc = rest[1] if nk > 1 else None
        kk = pl.program_id(2)
        sel = kk if dims != "tn" else pl.program_id(1)

        def step(a_ref, b_ref):
            p = _dot(a_ref[...].astype(BF16), b_ref[...].astype(BF16), dn)
            if nk == 1:
                o_ref[...] = p.astype(out_dtype)
                return

            @pl.when(kk == 0)
            def _():
                acc[...] = p

            @pl.when(kk > 0)
            def _():
                acc[...] += p

        if nseg == 1:
            step(a_refs[0], b_refs[0])
        else:
            for s in range(nseg):
                @pl.when(jnp.logical_and(sel >= starts[s], sel < starts[s] + counts[s]))
                def _(s=s):
                    step(a_refs[s] if dims != "tn" else a_refs[0], b_refs[s if nb_ops > 1 else 0])

        if nk > 1:
            @pl.when(kk == nk - 1)
            def _():
                o_ref[...] = acc[...].astype(out_dtype)

    def seg_index(v, s):
        return v if nseg == 1 else jnp.clip(v - starts[s], 0, counts[s] - 1)

    lead = () if b_layer is None else (b_layer,)
    none = () if b_layer is None else (None,)
    def b_block(kk, s):
        return kk if same else first_block[s] + seg_index(kk, s)

    if dims == "nn":
        a_specs = [pl.BlockSpec((tm, units[s]), lambda i, j, kk, s=s: (i, seg_index(kk, s))) for s in range(nseg)]
        b_specs = [pl.BlockSpec(none + (units[s], tn), lambda i, j, kk, s=s: lead + (b_block(kk, s), j))
                   for s in range(nb_ops)]
    elif dims == "nt":
        a_specs = [pl.BlockSpec((tm, units[s]), lambda i, j, kk, s=s: (i, seg_index(kk, s))) for s in range(nseg)]
        b_specs = [pl.BlockSpec(none + (tn, units[s]), lambda i, j, kk, s=s: lead + (j, b_block(kk, s)))
                   for s in range(nb_ops)]
    else:
        a_specs = [pl.BlockSpec((tk, tm), lambda i, j, kk: (kk, i))]
        b_specs = [pl.BlockSpec((tk, tn), lambda i, j, kk, s=s: (kk, seg_index(j, s))) for s in range(nseg)]
    args = a_list + (b_list * nb_ops if dims != "tn" else b_list)
    in_specs = a_specs + b_specs
    aliases = {}
    if out_buf is None:
        out_shape = jax.ShapeDtypeStruct((m, n), out_dtype)
        out_spec = pl.BlockSpec((tm, tn), lambda i, j, kk: (i, j))
    else:
        buf, slab = out_buf
        assert buf.shape[1:] == (m, n) and buf.dtype == out_dtype
        out_shape = (jax.ShapeDtypeStruct(buf.shape, out_dtype), jax.ShapeDtypeStruct((8, LANES), F32))
        out_spec = (pl.BlockSpec((None, tm, tn), lambda i, j, kk: (slab, i, j)),
                    pl.BlockSpec((8, LANES), lambda i, j, kk: (0, 0)))
        aliases = {len(args): 0}
        args = args + [buf]
        in_specs = in_specs + [ANY]
    return pl.pallas_call(
        body,
        out_shape=out_shape,
        grid=(m // tm, n // tn, nk),
        in_specs=in_specs,
        out_specs=out_spec,
        scratch_shapes=[] if nk == 1 else [pltpu.VMEM((tm, tn), F32)],
        input_output_aliases=aliases,
        compiler_params=_params(("parallel", "parallel", "arbitrary") if out_buf is None else ("arbitrary",) * 3),
        name=name,
    )(*args)


def _row_tile(t, want):
    tm = min(want, t)
    assert t % tm == 0
    return tm


def _norm_fwd(x, w, out_dtype, name, resid=None, after=()):
    t, d = x.shape
    tm = _row_tile(t, 512)
    after = [a for a in after if a is not None]

    def body(*refs):
        refs = refs[:len(refs) - 1 - len(after)] + refs[len(refs) - 1:]
        if resid is None:
            x_ref, w_ref, o_ref = refs
        else:
            x_ref, w_ref, r_ref, o_ref = refs
        xv = x_ref[...]
        r = lax.rsqrt(jnp.mean(xv * xv, axis=-1, keepdims=True) + EPS)
        y = (xv * r) * w_ref[...]
        if resid is not None:
            y = r_ref[...] + y
        o_ref[...] = y.astype(out_dtype)

    row = pl.BlockSpec((tm, d), lambda i: (i, 0))
    vec = pl.BlockSpec((1, d), lambda i: (0, 0))
    args = [x, w] + ([] if resid is None else [resid]) + after
    return pl.pallas_call(
        body, out_shape=jax.ShapeDtypeStruct((t, d), out_dtype), grid=(t // tm,),
        in_specs=[row, vec] + ([] if resid is None else [row]) + [ANY] * len(after), out_specs=row,
        compiler_params=_params(("parallel",)), name=name)(*args)


def _norm_post_pre(m, w_post, resid, w_pre, pre_dtype, name, after=()):
    t, d = m.shape
    tm = _row_tile(t, 512)
    after = [a for a in after if a is not None]

    def body(m_ref, w1_ref, r_ref, w2_ref, *rest):
        x_ref, u_ref = rest[len(after):]
        mv = m_ref[...]
        r1 = lax.rsqrt(jnp.mean(mv * mv, axis=-1, keepdims=True) + EPS)
        xv = r_ref[...] + (mv * r1) * w1_ref[...]
        x_ref[...] = xv
        r2 = lax.rsqrt(jnp.mean(xv * xv, axis=-1, keepdims=True) + EPS)
        u_ref[...] = ((xv * r2) * w2_ref[...]).astype(pre_dtype)

    row = pl.BlockSpec((tm, d), lambda i: (i, 0))
    vec = pl.BlockSpec((1, d), lambda i: (0, 0))
    return pl.pallas_call(
        body, out_shape=(jax.ShapeDtypeStruct((t, d), F32), jax.ShapeDtypeStruct((t, d), pre_dtype)), grid=(t // tm,),
        in_specs=[row, vec, row, vec] + [ANY] * len(after), out_specs=(row, row),
        compiler_params=_params(("parallel",)), name=name)(m, w_post, resid, w_pre, *after)


def _norm_bwd(src, w, dy, out_dtype, name, resid=None, after=()):
    t, d = src.shape
    tm = _row_tile(t, 512)
    after = [a for a in after if a is not None]

    def body(*refs):
        refs = refs[:len(refs) - 2 - len(after)] + refs[len(refs) - 2:]
        if resid is None:
            x_ref, w_ref, g_ref, o_ref, dw_ref = refs
        else:
            x_ref, w_ref, g_ref, r_ref, o_ref, dw_ref = refs
        xv = x_ref[...]
        g = g_ref[...].astype(F32)
        r = lax.rsqrt(jnp.mean(xv * xv, axis=-1, keepdims=True) + EPS)
        xh = xv * r
        gh = g * w_ref[...]
        mean = jnp.mean(gh * xh, axis=-1, keepdims=True)
        dx = r * (gh - xh * mean)
        if resid is not None:
            dx = r_ref[...] + dx
        o_ref[...] = dx.astype(out_dtype)
        part = jnp.sum(g * xh, axis=0, keepdims=True)

        @pl.when(pl.program_id(0) == 0)
        def _():
            dw_ref[...] = part

        @pl.when(pl.program_id(0) > 0)
        def _():
            dw_ref[...] += part

    row = pl.BlockSpec((tm, d), lambda i: (i, 0))
    vec = pl.BlockSpec((1, d), lambda i: (0, 0))
    args = [src, w, dy] + ([] if resid is None else [resid]) + after
    return pl.pallas_call(
        body,
        out_shape=(jax.ShapeDtypeStruct((t, d), out_dtype), jax.ShapeDtypeStruct((1, d), F32)),
        grid=(t // tm,),
        in_specs=[row, vec, row] + ([] if resid is None else [row]) + [ANY] * len(after),
        out_specs=(row, vec),
        compiler_params=_params(("arbitrary",)), name=name)(*args)


def _loss_head(y, target, name):
    t, d = y.shape
    tm = _row_tile(t, 512)

    def body(y_ref, t_ref, dy_ref, l_ref):
        e = y_ref[...] - t_ref[...]
        dy_ref[...] = e * (1.0 / d)
        col = jnp.sum(e * e, axis=0, keepdims=True)
        s = jnp.sum(col, axis=1, keepdims=True) * (0.5 / d)
        part = jnp.broadcast_to(s, (1, LANES))

        @pl.when(pl.program_id(0) == 0)
        def _():
            l_ref[...] = part

        @pl.when(pl.program_id(0) > 0)
        def _():
            l_ref[...] += part

    row = pl.BlockSpec((tm, d), lambda i: (i, 0))
    return pl.pallas_call(
        body,
        out_shape=(jax.ShapeDtypeStruct((t, d), F32), jax.ShapeDtypeStruct((1, LANES), F32)),
        grid=(t // tm,), in_specs=[row, row],
        out_specs=(row, pl.BlockSpec((1, LANES), lambda i: (0, 0))),
        compiler_params=_params(("arbitrary",)), name=name)(y, target)


def _window(ref, c, rows, seq, before, after):
    r0 = pl.multiple_of(c * rows, rows)
    parts = []
    if before:
        h0 = pl.multiple_of(jnp.maximum(r0 - before, 0), before)
        halo = ref[pl.ds(h0, before), :].astype(F32)
        parts.append(jnp.where(c > 0, halo, 0.0))
    parts.append(ref[pl.ds(r0, rows), :].astype(F32))
    if after:
        h1 = pl.multiple_of(jnp.minimum(r0 + rows, seq - after), after)
        halo = ref[pl.ds(h1, after), :].astype(F32)
        parts.append(jnp.where(c < seq // rows - 1, halo, 0.0))
    return parts[0] if len(parts) == 1 else jnp.concatenate(parts, axis=0)


def _lag(x, k):
    return pltpu.roll(x, k, 0) if k else x


def _lead(x, k):
    return pltpu.roll(x, x.shape[0] - k, 0) if k else x


SHIFT_ROWS = 128
SHIFT_COLS = 256


HALO = 16


def _conv3(ext, w, bias):
    acc = bias + w[2:3, :] * ext[HALO:, :]
    acc = acc + w[1:2, :] * _lag(ext, 1)[HALO:, :]
    return acc + w[0:1, :] * _lag(ext, 2)[HALO:, :]


def _ffn_act_fwd(hpre, cw, cb, name):
    b, seq, f2 = hpre.shape
    cbk = SHIFT_COLS
    nj = f2 // (2 * cbk)
    rows = min(SHIFT_ROWS, seq)

    def body(g_ref, v_ref, wg_ref, wv_ref, bg_ref, bv_ref, o_ref, pg_ref, pv_ref):
        def chunk(c, carry):
            gate = _conv3(_window(g_ref, c, rows, seq, HALO, 0), wg_ref[...], bg_ref[...])
            val = _conv3(_window(v_ref, c, rows, seq, HALO, 0), wv_ref[...], bv_ref[...])
            a = gate * _sigmoid(gate) * val
            here = pl.ds(pl.multiple_of(c * rows, rows), rows)
            o_ref[here, :] = a.astype(BF16)
            pg_ref[here, :] = gate.astype(BF16)
            pv_ref[here, :] = val.astype(BF16)
            return carry

        lax.fori_loop(0, seq // rows, chunk, 0)

    blk = lambda off: pl.BlockSpec((None, seq, cbk), lambda i, j: (i, 0, j + off))
    wsp = lambda r, off: pl.BlockSpec((r, cbk), lambda i, j: (0, j + off))
    half = jax.ShapeDtypeStruct((b, seq, f2 // 2), BF16)
    return pl.pallas_call(
        body, out_shape=(half, half, half), grid=(b, nj),
        in_specs=[blk(0), blk(nj), wsp(FFN_CONV, 0), wsp(FFN_CONV, nj), wsp(1, 0), wsp(1, nj)],
        out_specs=(blk(0), blk(0), blk(0)),
        compiler_params=_params(("parallel", "parallel")), name=name)(hpre, hpre, cw, cw, cb, cb)


def _ffn_act_bwd(hpre, pre_g, pre_v, da, cw, name):
    b, seq, f2 = hpre.shape
    cbk = SHIFT_COLS
    nj = f2 // (2 * cbk)
    rows = min(SHIFT_ROWS, seq)

    def body(g_ref, v_ref, pg_ref, pv_ref, da_ref, wg_ref, wv_ref, og_ref, ov_ref, dwg_ref, dwv_ref, dbg_ref, dbv_ref):
        wg, wv = wg_ref[...], wv_ref[...]

        def back(dpre, w, o_ref, x_ref, c, carry):
            here = pl.ds(pl.multiple_of(c * rows, rows), rows)
            leads = [dpre, _lead(dpre, 1), _lead(dpre, 2)]
            dx = w[2:3, :] * leads[0] + w[1:2, :] * leads[1] + w[0:1, :] * leads[2]
            o_ref[here, :] = dx[:rows, :].astype(BF16)
            x0 = x_ref[here, :].astype(F32)
            return tuple(carry[k] + jnp.sum(leads[k][:rows, :] * x0, axis=0, keepdims=True) for k in range(FFN_CONV)) + (
                carry[FFN_CONV] + jnp.sum(dpre[:rows, :], axis=0, keepdims=True),)

        def chunk(c, carry):
            cg, cv = carry
            gate = _window(pg_ref, c, rows, seq, 0, HALO)
            val = _window(pv_ref, c, rows, seq, 0, HALO)
            dav = _window(da_ref, c, rows, seq, 0, HALO)
            sg = _sigmoid(gate)
            cg = back(dav * val * (sg * (1.0 + gate * (1.0 - sg))), wg, og_ref, g_ref, c, cg)
            cv = back(dav * (gate * sg), wv, ov_ref, v_ref, c, cv)
            return cg, cv

        z = jnp.zeros((1, cbk), F32)
        cg, cv = lax.fori_loop(0, seq // rows, chunk, ((z,) * (FFN_CONV + 1), (z,) * (FFN_CONV + 1)))
        dwg = jnp.concatenate([cg[2], cg[1], cg[0]], axis=0)
        dwv = jnp.concatenate([cv[2], cv[1], cv[0]], axis=0)

        @pl.when(pl.program_id(1) == 0)
        def _():
            dwg_ref[...] = dwg
            dwv_ref[...] = dwv
            dbg_ref[...] = cg[FFN_CONV]
            dbv_ref[...] = cv[FFN_CONV]

        @pl.when(pl.program_id(1) > 0)
        def _():
            dwg_ref[...] += dwg
            dwv_ref[...] += dwv
            dbg_ref[...] += cg[FFN_CONV]
            dbv_ref[...] += cv[FFN_CONV]

    blk = lambda off: pl.BlockSpec((None, seq, cbk), lambda j, i: (i, 0, j + off))
    wsp = lambda r, off: pl.BlockSpec((r, cbk), lambda j, i: (0, j + off))
    half = jax.ShapeDtypeStruct((b, seq, f2 // 2), BF16)
    dwshape = jax.ShapeDtypeStruct((FFN_CONV, f2 // 2), F32)
    dbshape = jax.ShapeDtypeStruct((1, f2 // 2), F32)
    dg, dv, dwg, dwv, dbg, dbv = pl.pallas_call(
        body,
        out_shape=(half, half, dwshape, dwshape, dbshape, dbshape),
        grid=(nj, b),
        in_specs=[blk(0), blk(nj), blk(0), blk(0), blk(0), wsp(FFN_CONV, 0), wsp(FFN_CONV, nj)],
        out_specs=(blk(0), blk(0), wsp(FFN_CONV, 0), wsp(FFN_CONV, 0), wsp(1, 0), wsp(1, 0)),
        compiler_params=_params(("parallel", "arbitrary")), name=name)(hpre, hpre, pre_g, pre_v, da, cw, cw)
    return dg, dv, jnp.concatenate([dwg, dwv], axis=1), jnp.concatenate([dbg, dbv], axis=1)


def _ssd_conv_fwd(zx, cw, cb, d_inner, name):
    b, seq, _ = zx.shape
    xbc = cw.shape[1]
    cbk = SHIFT_COLS
    off = d_inner // cbk
    rows = min(SHIFT_ROWS, seq)

    def body(h_ref, w_ref, b_ref, o_ref, p_ref):
        w = w_ref[...]
        bias = b_ref[...]

        def chunk(c, carry):
            ext = _window(h_ref, c, rows, seq, 8, 0)
            acc = bias + w[3:4, :] * ext[8:, :]
            for k in range(1, SSD_CONV):
                acc = acc + w[3 - k:4 - k, :] * _lag(ext, k)[8:, :]
            here = pl.ds(pl.multiple_of(c * rows, rows), rows)
            o_ref[here, :] = acc * _sigmoid(acc)
            p_ref[here, :] = acc.astype(BF16)
            return carry

        lax.fori_loop(0, seq // rows, chunk, 0)

    blk = pl.BlockSpec((None, seq, cbk), lambda i, j: (i, 0, j))
    return pl.pallas_call(
        body, out_shape=(jax.ShapeDtypeStruct((b, seq, xbc), F32), jax.ShapeDtypeStruct((b, seq, xbc), BF16)),
        grid=(b, xbc // cbk),
        in_specs=[pl.BlockSpec((None, seq, cbk), lambda i, j: (i, 0, j + off)),
                  pl.BlockSpec((SSD_CONV, cbk), lambda i, j: (0, j)),
                  pl.BlockSpec((1, cbk), lambda i, j: (0, j))],
        out_specs=(blk, blk),
        compiler_params=_params(("parallel", "parallel")), name=name)(zx, cw, cb)


def _ssd_conv_bwd(zx, pre, dparts, cw, d_inner, name):
    b, seq, _ = zx.shape
    xbc = cw.shape[1]
    cbk = SHIFT_COLS
    off = d_inner // cbk
    rows = min(SHIFT_ROWS, seq)
    nblk = [p.shape[2] // cbk for p in dparts]
    first = [sum(nblk[:s]) for s in range(len(dparts))]
    assert sum(nblk) == xbc // cbk

    def body(h_ref, p_ref, gx_ref, gb_ref, gc_ref, w_ref, o_ref, dw_ref, db_ref):
        w = w_ref[...]
        j = pl.program_id(0)

        def chunk(c, carry):
            dws, dbias = carry
            here = pl.ds(pl.multiple_of(c * rows, rows), rows)
            pre = _window(p_ref, c, rows, seq, 0, HALO)
            s = _sigmoid(pre)
            gsel = jnp.where(j < first[1], _window(gx_ref, c, rows, seq, 0, HALO),
                             jnp.where(j < first[2], _window(gb_ref, c, rows, seq, 0, HALO),
                                       _window(gc_ref, c, rows, seq, 0, HALO)))
            dpre = gsel * (s * (1.0 + pre * (1.0 - s)))
            leads = [dpre] + [_lead(dpre, k) for k in range(1, SSD_CONV)]
            dx = w[3:4, :] * leads[0]
            for k in range(1, SSD_CONV):
                dx = dx + w[3 - k:4 - k, :] * leads[k]
            o_ref[here, :] = dx[:rows, :].astype(BF16)
            x0 = h_ref[here, :]
            dws = tuple(dws[k] + jnp.sum(leads[k][:rows, :] * x0, axis=0, keepdims=True) for k in range(SSD_CONV))
            dbias = dbias + jnp.sum(dpre[:rows, :], axis=0, keepdims=True)
            return dws, dbias

        z = jnp.zeros((1, cbk), F32)
        dws, dbias = lax.fori_loop(0, seq // rows, chunk, ((z,) * SSD_CONV, z))
        dwv = jnp.concatenate([dws[3 - i] for i in range(SSD_CONV)], axis=0)

        @pl.when(pl.program_id(1) == 0)
        def _():
            dw_ref[...] = dwv
            db_ref[...] = dbias

        @pl.when(pl.program_id(1) > 0)
        def _():
            dw_ref[...] += dwv
            db_ref[...] += dbias

    return pl.pallas_call(
        body,
        out_shape=(jax.ShapeDtypeStruct((b, seq, xbc), BF16), jax.ShapeDtypeStruct((SSD_CONV, xbc), F32),
                   jax.ShapeDtypeStruct((1, xbc), F32)),
        grid=(xbc // cbk, b),
        in_specs=[pl.BlockSpec((None, seq, cbk), lambda j, i: (i, 0, j + off)),
                  pl.BlockSpec((None, seq, cbk), lambda j, i: (i, 0, j))] + [
                  pl.BlockSpec((None, seq, cbk), lambda j, i, s=s: (i, 0, jnp.clip(j - first[s], 0, nblk[s] - 1)))
                  for s in range(3)] + [
                  pl.BlockSpec((SSD_CONV, cbk), lambda j, i: (0, j))],
        out_specs=(pl.BlockSpec((None, seq, cbk), lambda j, i: (i, 0, j)),
                   pl.BlockSpec((SSD_CONV, cbk), lambda j, i: (0, j)),
                   pl.BlockSpec((1, cbk), lambda j, i: (0, j))),
        compiler_params=_params(("parallel", "arbitrary")), name=name)(zx, pre, *dparts, cw)


def _pool_sums(q, g, lead):
    sh = _lead if lead else _lag
    s2 = q + sh(q, 1)
    s4 = s2 + sh(s2, 2)
    s8 = s4 + sh(s4, 4)
    s16 = s8 + sh(s8, 8)
    return jnp.where(g == 0, s2, jnp.where(g == 1, s4, jnp.where(g == 2, s8, s16)))


def _pool_count(r0, n, g, shape):
    t = (r0 + lax.broadcasted_iota(jnp.int32, shape, 0) + 1).astype(F32)
    return jnp.minimum(t, (2 << g).astype(F32))


def _pool_fwd(h, pw, scale, name):
    b, seq, d = h.shape
    dg = d // 4
    rows = min(SHIFT_ROWS, seq)

    def body(h_ref, w_ref, s_ref, o_ref):
        g = pl.program_id(1)
        wmat = w_ref[...]
        sc = s_ref[...]

        def chunk(c, carry):
            r0 = c * rows
            ext = _window(h_ref, c, rows, seq, 16, 0)
            sums = _pool_sums(ext, g, False)[16:, :]
            mixed = sums / _pool_count(r0, rows, g, (rows, dg)) - ext[16:, :]
            o_ref[pl.ds(pl.multiple_of(r0, rows), rows), :] = _nn(mixed.astype(BF16), wmat) * sc
            return carry

        lax.fori_loop(0, seq // rows, chunk, 0)

    return pl.pallas_call(
        body, out_shape=jax.ShapeDtypeStruct((b, seq, d), F32), grid=(b, 4),
        in_specs=[pl.BlockSpec((None, seq, dg), lambda i, g: (i, 0, g)),
                  pl.BlockSpec((None, dg, dg), lambda i, g: (g, 0, 0)),
                  pl.BlockSpec((1, dg), lambda i, g: (0, g))],
        out_specs=pl.BlockSpec((None, seq, dg), lambda i, g: (i, 0, g)),
        compiler_params=_params(("parallel", "parallel")), name=name)(h, pw, scale)


def _pool_bwd(h, dout, pw, scale, name):
    b, seq, d = h.shape
    dg = d // 4
    rows = min(SHIFT_ROWS, seq)

    def body(h_ref, g_ref, w_ref, s_ref, o_ref, dw_ref, ds_ref, dw_acc):
        g = pl.program_id(0)
        wmat = w_ref[...]
        sc = s_ref[...]
        dw_acc[...] = jnp.zeros_like(dw_acc)

        def chunk(c, dsc):
            r0 = c * rows
            ext = _window(h_ref, c, rows, seq, 16, 0)
            sums = _pool_sums(ext, g, False)[16:, :]
            mixed = (sums / _pool_count(r0, rows, g, (rows, dg)) - ext[16:, :]).astype(BF16)
            gext = _window(g_ref, c, rows, seq, 0, 16)
            dsc = dsc + jnp.sum(gext[:rows, :] * _nn(mixed, wmat), axis=0, keepdims=True)
            dpre = (gext * sc).astype(BF16)
            dw_acc[...] += _tn(mixed, dpre[:rows, :])
            dmix = _nt(dpre, wmat)
            q = dmix / _pool_count(r0, rows + 16, g, (rows + 16, dg))
            back = _pool_sums(q, g, True)
            o_ref[pl.ds(pl.multiple_of(r0, rows), rows), :] = back[:rows, :] - dmix[:rows, :]
            return dsc

        dsc = lax.fori_loop(0, seq // rows, chunk, jnp.zeros((1, dg), F32))

        @pl.when(pl.program_id(1) == 0)
        def _():
            dw_ref[...] = dw_acc[...]
            ds_ref[...] = dsc

        @pl.when(pl.program_id(1) > 0)
        def _():
            dw_ref[...] += dw_acc[...]
            ds_ref[...] += dsc

    return pl.pallas_call(
        body,
        out_shape=(jax.ShapeDtypeStruct((b, seq, d), F32), jax.ShapeDtypeStruct((4, dg, dg), F32),
                   jax.ShapeDtypeStruct((1, d), F32)),
        grid=(4, b),
        in_specs=[pl.BlockSpec((None, seq, dg), lambda g, i: (i, 0, g)),
                  pl.BlockSpec((None, seq, dg), lambda g, i: (i, 0, g)),
                  pl.BlockSpec((None, dg, dg), lambda g, i: (g, 0, 0)),
                  pl.BlockSpec((1, dg), lambda g, i: (0, g))],
        out_specs=(pl.BlockSpec((None, seq, dg), lambda g, i: (i, 0, g)),
                   pl.BlockSpec((None, dg, dg), lambda g, i: (g, 0, 0)),
                   pl.BlockSpec((1, dg), lambda g, i: (0, g))),
        scratch_shapes=[pltpu.VMEM((dg, dg), F32)],
        compiler_params=_params(("parallel", "arbitrary")), name=name)(h, dout, pw, scale)


def _head_of(channel):
    return jnp.right_shift(channel, HEAD_DIM.bit_length() - 1)


def _ssd_consts(gw):
    q = CHUNK
    row = lax.broadcasted_iota(jnp.int32, (q, q), 0)
    col = lax.broadcasted_iota(jnp.int32, (q, q), 1)
    tril = (row >= col).astype(BF16)
    triu = (row <= col).astype(BF16)
    e = (_head_of(lax.broadcasted_iota(jnp.int32, (LANES, gw), 1))
         == lax.broadcasted_iota(jnp.int32, (LANES, gw), 0)).astype(BF16)
    et = (_head_of(lax.broadcasted_iota(jnp.int32, (gw, LANES), 0))
          == lax.broadcasted_iota(jnp.int32, (gw, LANES), 1)).astype(BF16)
    return row, col, tril, triu, e, et


def _ssd_common(dtr, dtb, alog, gw):
    q = CHUNK
    row, col, tril, triu, e, et = _ssd_consts(gw)
    dt = _softplus(dtr + dtb)
    a_row = -jnp.exp(alog)
    acum = _sel_left(tril, dt * a_row)
    ac_last = jnp.sum(jnp.where(row == q - 1, acum, 0.0), axis=0, keepdims=True)
    eac = jnp.exp(acum)
    de = jnp.exp(ac_last - acum)
    e2 = jnp.concatenate([e, e], axis=0)
    expand = _sel_right(jnp.concatenate([dt, eac, de], axis=0), e2, 2)
    dt_x, eac_x, de_x = expand[0:q], expand[q:2 * q], expand[2 * q:3 * q]
    acum_t = acum.T
    cd_col = jnp.exp(acum_t[:, q - 1:q])
    et3 = jnp.concatenate([et, et, et], axis=1)
    cdmat = _nn(et3, jnp.concatenate(_split(jnp.broadcast_to(cd_col, (LANES, D_STATE)), 3), axis=0))
    consts = dict(row=row, col=col, tril=tril, triu=triu, e=e, et=et)
    return dt, a_row, acum, acum_t, ac_last, eac, de, dt_x, eac_x, de_x, cdmat, consts


def _decay(acum, acum_t, j, row, col):
    diff = acum[:, j:j + 1] - acum_t[j:j + 1, :]
    return jnp.exp(jnp.where(row >= col, diff, -1e30))


def _ssd_fwd(xc, zx, dtb, alog, dskip, nw, d_inner, name):
    b, seq, xbc = xc.shape
    q = CHUNK
    nc = seq // q
    gw = d_inner // N_GROUPS
    nh = gw // HEAD_DIM
    xb0 = d_inner // D_STATE
    xc0 = xb0 + N_GROUPS
    dt0 = (d_inner + xbc) // LANES

    nb = max(n for n in (4, 2, 1) if b % n == 0)

    def body(x_ref, b_ref, c_ref, z_ref, dtr_ref, dtb_ref, al_ref, dsk_ref, nw_ref, y_ref, yn_ref, st_ref, s_ref):
        @pl.when(pl.program_id(2) == 0)
        def _():
            s_ref[...] = jnp.zeros_like(s_ref)

        for s in range(nb):
            one(s, x_ref.at[s], b_ref.at[s], c_ref.at[s], z_ref.at[s], dtr_ref.at[s], dtb_ref, al_ref, dsk_ref, nw_ref,
                y_ref.at[s], yn_ref.at[s], st_ref.at[s], s_ref.at[s])

    def one(s, x_ref, b_ref, c_ref, z_ref, dtr_ref, dtb_ref, al_ref, dsk_ref, nw_ref, y_ref, yn_ref, st_ref, s_ref):
        prev = s_ref[...]
        st_ref[...] = prev
        x = x_ref[...]
        bm = b_ref[...].astype(BF16)
        cm = c_ref[...].astype(BF16)
        (dt, a_row, acum, acum_t, ac_last, eac, de, dt_x, eac_x, de_x, cdmat, k) = _ssd_common(
            dtr_ref[...], dtb_ref[0:1, :], al_ref[0:1, :], gw)
        xdt = x * dt_x
        xdt_b = xdt.astype(BF16)
        cb = _nt(cm, bm)
        half = _head_of(lax.broadcasted_iota(jnp.int32, (q, LANES), 1))
        pairs = []
        for j in range(nh):
            pc = (j // 2) * LANES
            m = (cb * _decay(acum, acum_t, j, k["row"], k["col"])).astype(BF16)
            yj = jnp.where(half == j % 2, _nn(m, xdt_b[:, pc:pc + LANES]), 0.0)
            if j % 2 == 0:
                pairs.append(yj)
            else:
                pairs[-1] = pairs[-1] + yj
        prev_b = prev.astype(BF16)
        y = dsk_ref[0:1, :] * x + jnp.concatenate(pairs, axis=1) + eac_x * _nt(cm, prev_b)
        s_ref[...] = cdmat * prev + _tn((xdt * de_x).astype(BF16), bm)
        y_ref[...] = y
        z = z_ref[...]
        yg = y * (z * _sigmoid(z))
        r = lax.rsqrt(jnp.mean(yg * yg, axis=-1, keepdims=True) + EPS)
        yn_ref[...] = ((yg * r) * nw_ref[0:1, :]).astype(BF16)

    par = lambda w: pl.BlockSpec((None, 8, w), lambda i, g, c: (g, 0, 0))
    return pl.pallas_call(
        body,
        out_shape=(jax.ShapeDtypeStruct((b, seq, d_inner), F32), jax.ShapeDtypeStruct((b, seq, d_inner), BF16),
                   jax.ShapeDtypeStruct((b, nc, N_GROUPS, gw, D_STATE), F32)),
        grid=(b // nb, N_GROUPS, nc),
        in_specs=[pl.BlockSpec((nb, q, gw), lambda i, g, c: (i, c, g)),
                  pl.BlockSpec((nb, q, D_STATE), lambda i, g, c: (i, c, xb0 + g)),
                  pl.BlockSpec((nb, q, D_STATE), lambda i, g, c: (i, c, xc0 + g)),
                  pl.BlockSpec((nb, q, gw), lambda i, g, c: (i, c, g)),
                  pl.BlockSpec((nb, q, LANES), lambda i, g, c: (i, c, dt0 + g)),
                  par(LANES), par(LANES), par(gw), par(gw)],
        out_specs=(pl.BlockSpec((nb, q, gw), lambda i, g, c: (i, c, g)),
                   pl.BlockSpec((nb, q, gw), lambda i, g, c: (i, c, g)),
                   pl.BlockSpec((nb, None, None, gw, D_STATE), lambda i, g, c: (i, c, g, 0, 0))),
        scratch_shapes=[pltpu.VMEM((nb, gw, D_STATE), F32)],
        compiler_params=_params(("parallel", "parallel", "arbitrary")), name=name,
    )(xc, xc, xc, zx, zx, dtb, alog, dskip, nw)


def _ssd_bwd(xc, zx, y, dyn, st, dtb, alog, dskip, nw, d_inner, name):
    b, seq, xbc = xc.shape
    q = CHUNK
    nc = seq // q
    gw = d_inner // N_GROUPS
    nh = gw // HEAD_DIM
    xb0 = d_inner // D_STATE
    xc0 = xb0 + N_GROUPS
    dt0 = (d_inner + xbc) // LANES

    nb = max(n for n in (4, 2, 1) if b % n == 0)

    def body(x_ref, b_ref, c_ref, z_ref, dtr_ref, y_ref, g_ref, st_ref, dtb_ref, al_ref, dsk_ref, nw_ref,
             dz_ref, dx_ref, db_ref, dc_ref, ddt_ref, dnw_ref, dd_ref, dal_ref, dbias_ref,
             ds_ref, colbuf, rowbuf):
        first = jnp.logical_and(pl.program_id(1) == 0, pl.program_id(2) == 0)

        @pl.when(pl.program_id(2) == 0)
        def _():
            ds_ref[...] = jnp.zeros_like(ds_ref)

        sums = [one(x_ref.at[s], b_ref.at[s], c_ref.at[s], z_ref.at[s], dtr_ref.at[s], y_ref.at[s], g_ref.at[s],
                    st_ref.at[s], dtb_ref, al_ref, dsk_ref, nw_ref, dz_ref.at[s], dx_ref.at[s], db_ref.at[s],
                    dc_ref.at[s], ddt_ref.at[s], ds_ref.at[s], colbuf.at[s], rowbuf.at[s]) for s in range(nb)]
        dnw, dd, dal, dbias = [functools.reduce(lambda p, r: p + r, [sm[i] for sm in sums]) for i in range(4)]

        @pl.when(first)
        def _():
            dnw_ref[...] = jnp.broadcast_to(dnw, (8, gw))
            dd_ref[...] = dd
            dal_ref[...] = jnp.broadcast_to(dal, (8, LANES))
            dbias_ref[...] = jnp.broadcast_to(dbias, (8, LANES))

        @pl.when(jnp.logical_not(first))
        def _():
            dnw_ref[...] += jnp.broadcast_to(dnw, (8, gw))
            dd_ref[...] += dd
            dal_ref[...] += jnp.broadcast_to(dal, (8, LANES))
            dbias_ref[...] += jnp.broadcast_to(dbias, (8, LANES))

    def one(x_ref, b_ref, c_ref, z_ref, dtr_ref, y_ref, g_ref, st_ref, dtb_ref, al_ref, dsk_ref, nw_ref,
            dz_ref, dx_ref, db_ref, dc_ref, ddt_ref, ds_ref, colbuf, rowbuf):
        x = x_ref[...]
        bm = b_ref[...].astype(BF16)
        cm = c_ref[...].astype(BF16)
        z = z_ref[...]
        y = y_ref[...]
        prev = st_ref[...]
        dtr = dtr_ref[...] + dtb_ref[0:1, :]
        (dt, a_row, acum, acum_t, ac_last, eac, de, dt_x, eac_x, de_x, cdmat, k) = _ssd_common(
            dtr_ref[...], dtb_ref[0:1, :], al_ref[0:1, :], gw)
        row, col = k["row"], k["col"]
        et2 = jnp.concatenate([k["et"], k["et"]], axis=0)

        sz = _sigmoid(z)
        silu_z = z * sz
        yg = y * silu_z
        r = lax.rsqrt(jnp.mean(yg * yg, axis=-1, keepdims=True) + EPS)
        xh = yg * r
        dyn = g_ref[...]
        gh = dyn * nw_ref[0:1, :]
        dyg = r * (gh - xh * jnp.mean(gh * xh, axis=-1, keepdims=True))
        dnw = jnp.sum(dyn * xh, axis=0, keepdims=True)
        g = dyg * silu_z
        dz_ref[...] = (dyg * y * (sz * (1.0 + z * (1.0 - sz)))).astype(BF16)
        dd = _sel_right(jnp.broadcast_to(jnp.sum(g * x, axis=0, keepdims=True), (8, gw)), et2, 2)

        xdt = x * dt_x
        xdt_b = xdt.astype(BF16)
        g_b = g.astype(BF16)
        prev_b = prev.astype(BF16)
        cb = _nt(cm, bm)

        cp = _nt(cm, prev_b)
        ge = g * eac_x
        dac = _sel_right(ge * cp, et2, 2)
        ge_b = ge.astype(BF16)
        dcm = _nn(ge_b, prev_b)
        dprev = _tn(ge_b, cm)

        colbuf[...] = jnp.zeros_like(colbuf)
        rowbuf[...] = jnp.zeros_like(rowbuf)
        dcb = jnp.zeros((q, q), F32)
        half = _head_of(lax.broadcasted_iota(jnp.int32, (q, LANES), 1))
        pairs = []
        for j in range(nh):
            pc = (j // 2) * LANES
            dec = _decay(acum, acum_t, j, row, col)
            m = cb * dec
            gj = jnp.where(half == j % 2, g[:, pc:pc + LANES], 0.0).astype(BF16)
            dm = _nt(gj, xdt_b[:, pc:pc + LANES])
            w = dm * m
            colbuf[:, j:j + 1] = jnp.sum(w, axis=1, keepdims=True)
            rowbuf[j:j + 1, :] = jnp.sum(w, axis=0, keepdims=True)
            dcb = dcb + dm * dec
            dj = jnp.where(half == j % 2, _tn(m.astype(BF16), g_b[:, pc:pc + LANES]), 0.0)
            if j % 2 == 0:
                pairs.append(dj)
            else:
                pairs[-1] = pairs[-1] + dj
        dxdt = jnp.concatenate(pairs, axis=1)
        dcb_b = dcb.astype(BF16)
        dcm = dcm + _nn(dcb_b, bm)
        dbm = _tn(dcb_b, cm)

        ds = ds_ref[...]
        ds_b = ds.astype(BF16)
        u = _nt(bm, ds_b)
        dxdt = dxdt + u * de_x
        dde = _sel_right(u * xdt, et2, 2)
        dbm = dbm + _nn((xdt * de_x).astype(BF16), ds_b)
        pm = jnp.concatenate(_split(ds * prev, 2), axis=1)
        t2 = _tn(pm, k["et"])
        dcd_row = jnp.sum(t2[0:D_STATE] + t2[D_STATE:2 * D_STATE], axis=0, keepdims=True)
        last = dcd_row * jnp.exp(ac_last) + jnp.sum(dde * de, axis=0, keepdims=True)
        dac = dac + colbuf[...] - rowbuf[...].T - dde * de + jnp.where(row == q - 1, last, 0.0)
        ds_ref[...] = cdmat * ds + dprev

        dadt = _sel_left(k["triu"], dac)
        ddt = _sel_right(dxdt * x, et2, 2) + dadt * a_row
        dal = jnp.sum(dadt * dt, axis=0, keepdims=True) * a_row
        lane = lax.broadcasted_iota(jnp.int32, (q, LANES), 1)
        ddtr = jnp.where(lane < nh, ddt * _sigmoid(dtr), 0.0)
        ddt_ref[...] = ddtr.astype(BF16)
        dbias = jnp.sum(ddtr, axis=0, keepdims=True)
        dx_ref[...] = dxdt * dt_x + dsk_ref[0:1, :] * g
        db_ref[...] = dbm
        dc_ref[...] = dcm
        return dnw, dd, dal, dbias

    rc = lambda c: nc - 1 - c
    par = lambda w: pl.BlockSpec((None, 8, w), lambda g, i, c: (g, 0, 0))
    blk = lambda w: pl.BlockSpec((nb, q, w), lambda g, i, c: (i, rc(c), g))
    return pl.pallas_call(
        body,
        out_shape=(jax.ShapeDtypeStruct((b, seq, d_inner), BF16),
                   jax.ShapeDtypeStruct((b, seq, d_inner), F32),
                   jax.ShapeDtypeStruct((b, seq, N_GROUPS * D_STATE), F32),
                   jax.ShapeDtypeStruct((b, seq, N_GROUPS * D_STATE), F32),
                   jax.ShapeDtypeStruct((b, seq, N_GROUPS * LANES), BF16),
                   jax.ShapeDtypeStruct((N_GROUPS, 8, gw), F32),
                   jax.ShapeDtypeStruct((N_GROUPS, 8, LANES), F32),
                   jax.ShapeDtypeStruct((N_GROUPS, 8, LANES), F32),
                   jax.ShapeDtypeStruct((N_GROUPS, 8, LANES), F32)),
        grid=(N_GROUPS, b // nb, nc),
        in_specs=[blk(gw),
                  pl.BlockSpec((nb, q, D_STATE), lambda g, i, c: (i, rc(c), xb0 + g)),
                  pl.BlockSpec((nb, q, D_STATE), lambda g, i, c: (i, rc(c), xc0 + g)),
                  blk(gw),
                  pl.BlockSpec((nb, q, LANES), lambda g, i, c: (i, rc(c), dt0 + g)),
                  blk(gw), blk(gw),
                  pl.BlockSpec((nb, None, None, gw, D_STATE), lambda g, i, c: (i, rc(c), g, 0, 0)),
                  par(LANES), par(LANES), par(gw), par(gw)],
        out_specs=(blk(gw), blk(gw), blk(D_STATE), blk(D_STATE), blk(LANES),
                   par(gw), par(LANES), par(LANES), par(LANES)),
        scratch_shapes=[pltpu.VMEM((nb, gw, D_STATE), F32), pltpu.VMEM((nb, q, LANES), F32),
                        pltpu.VMEM((nb, LANES, q), F32)],
        compiler_params=_params(("parallel", "arbitrary", "arbitrary")), name=name,
    )(xc, xc, xc, zx, zx, y, dyn, st, dtb, alog, dskip, nw)


def _adamw(w, g, m, v, name):
    rows, cols = w.shape
    tr = rows
    for cand in (512, 256, 128, 64, 32, 16, 8):
        if rows % cand == 0 and cand * cols * 4 <= 2 * 1024 * 1024:
            tr = cand
            break
    c1 = 1.0 - ADAM_B1 ** ADAM_STEP
    c2 = 1.0 - ADAM_B2 ** ADAM_STEP

    def body(w_ref, g_ref, m_ref, v_ref, d_ref, mo_ref, vo_ref):
        gv = g_ref[...]
        mn = ADAM_B1 * m_ref[...] + (1.0 - ADAM_B1) * gv
        vn = ADAM_B2 * v_ref[...] + (1.0 - ADAM_B2) * (gv * gv)
        mo_ref[...] = mn
        vo_ref[...] = vn
        d_ref[...] = -ADAM_LR * ((mn / c1) / (jnp.sqrt(vn / c2) + ADAM_EPS) + ADAM_WD * w_ref[...])

    spec = pl.BlockSpec((tr, cols), lambda i: (i, 0))
    shp = jax.ShapeDtypeStruct((rows, cols), F32)
    return pl.pallas_call(body, out_shape=(shp, shp, shp), grid=(rows // tr,), in_specs=[spec] * 4,
                          out_specs=(spec,) * 3, compiler_params=_params(("parallel",)), name=name)(w, g, m, v)


def _pick_rows(rows, row_bytes, limit=1 << 20):
    for cand in (2048, 1024, 512, 256, 128, 64, 32, 16):
        if rows % cand == 0 and cand * row_bytes <= limit:
            return cand
    return rows


def _as3d(a, lead):
    return a.reshape(a.shape[:lead] + (-1, a.shape[-1]))


def _pair_sum(g, got, core, name):
    h = got.shape[0]
    g3, got3 = _as3d(g, 1), _as3d(got, 1)
    _, rows, cols = got3.shape
    tr = _pick_rows(rows, cols * 4)

    def body(c_ref, g_ref, r_ref, o_ref):
        o_ref[...] = (g_ref[...] + r_ref[...]).astype(BF16)

    out = pl.pallas_call(
        body, out_shape=jax.ShapeDtypeStruct(got3.shape, BF16),
        grid_spec=pltpu.PrefetchScalarGridSpec(
            num_scalar_prefetch=1, grid=(h, rows // tr),
            in_specs=[pl.BlockSpec((None, tr, cols), lambda l, i, c_ref: (c_ref[0] * h + l, i, 0)),
                      pl.BlockSpec((None, tr, cols), lambda l, i, c_ref: (l, i, 0))],
            out_specs=pl.BlockSpec((None, tr, cols), lambda l, i, c_ref: (l, i, 0))),
        compiler_params=_params(("parallel", "parallel")), name=name)(core, g3, got3)
    return out.reshape(got.shape)


def _sum4(q, core, name):
    q4 = _as3d(q, 2)
    _, h, rows, cols = q4.shape
    tr = _pick_rows(rows, cols * 4)

    def body(c_ref, q0, q1, q2, q3, o_ref):
        o_ref[...] = ((q0[...].astype(F32) + q1[...].astype(F32)) + q2[...].astype(F32)) + q3[...].astype(F32)

    out = pl.pallas_call(
        body, out_shape=jax.ShapeDtypeStruct((2 * h, rows, cols), F32),
        grid_spec=pltpu.PrefetchScalarGridSpec(
            num_scalar_prefetch=1, grid=(h, rows // tr),
            in_specs=[pl.BlockSpec((None, None, tr, cols), lambda l, i, c_ref, k=k: (k, l, i, 0))
                      for k in range(N_CHIPS)],
            out_specs=pl.BlockSpec((None, tr, cols), lambda l, i, c_ref: (c_ref[0] * h + l, i, 0))),
        compiler_params=_params(("parallel", "parallel")), name=name)(core, q4, q4, q4, q4)
    return out.reshape((2 * h,) + q.shape[2:])


def _coords():
    return lax.axis_index("x"), lax.axis_index("y"), lax.axis_index("c")


def _other_chips(x, y):
    return [(1 - x, y), (x, 1 - y), (1 - x, 1 - y)]


def _allgather_halves(src, name):
    rows, cols = src.shape

    def body(x_ref, o_ref, send, recv, local):
        x, y, c = _coords()
        sib = (x, y, 1 - c)
        chips = _other_chips(x, y)

        def slot(h, cx, cy):
            return o_ref.at[h, 2 * cx + cy]

        def copy(kk, dst, to, src_ref):
            return pltpu.make_async_remote_copy(src_ref=src_ref, dst_ref=dst, send_sem=send.at[kk],
                                                recv_sem=recv.at[kk], device_id=to, device_id_type=MESH)

        mine = pltpu.make_async_copy(x_ref, slot(c, x, y), local)
        mine.start()
        first = [copy(0, slot(c, x, y), sib, x_ref)]
        first += [copy(1 + j, slot(c, x, y), (*chip, c), x_ref) for j, chip in enumerate(chips)]
        for cp in first:
            cp.start()
        passed = [copy(4 + j, slot(c, *chip), sib, slot(c, *chip)) for j, chip in enumerate(chips)]
        for j, chip in enumerate(chips):
            copy(1 + j, slot(c, *chip), (x, y, c), x_ref).wait_recv()
            passed[j].start()
        copy(0, slot(1 - c, x, y), (x, y, c), x_ref).wait_recv()
        for j, chip in enumerate(chips):
            copy(4 + j, slot(1 - c, *chip), (x, y, c), x_ref).wait_recv()
        for cp in first + passed:
            cp.wait_send()
        mine.wait()

    return pl.pallas_call(
        body, out_shape=jax.ShapeDtypeStruct((2, N_CHIPS, rows, cols), src.dtype),
        in_specs=[ANY], out_specs=ANY,
        scratch_shapes=[pltpu.SemaphoreType.DMA((7,)), pltpu.SemaphoreType.DMA((7,)), pltpu.SemaphoreType.DMA],
        name=name)(src)


MIXW = (("ssd_w_in", None), ("ssd_w_out", 0), ("pool_w", 1))
FFNW = (("ffn_w_up", 1), ("ffn_w_down", 0))


def _chip_window(axis, ref, layers, k):
    if axis is None:
        return ref.at[layers, k]
    n = ref.shape[1 + axis] // N_CHIPS
    sl = pl.ds(pl.multiple_of(k * n, LANES if 1 + axis == len(ref.shape) - 1 else 8), n)
    idx = [layers] + [slice(None)] * (len(ref.shape) - 1)
    idx[1 + axis] = sl
    return ref.at[tuple(idx)]


def _full_shape(axis, shard_shape):
    if axis is None:
        return (shard_shape[0], N_CHIPS) + tuple(shard_shape[1:])
    full = list(shard_shape)
    full[1 + axis] *= N_CHIPS
    return tuple(full)


HBM_SPEC = pl.BlockSpec(memory_space=pltpu.HBM)
SEM_SPEC = pl.BlockSpec(memory_space=pltpu.SEMAPHORE)


def _dma_sems(count):
    return pltpu.SemaphoreType.DMA((max(count, 1),))


def _wait_for(copy, kind):
    if kind == "recv":
        copy.wait_recv()
    elif kind == "send":
        copy.wait_send()
    else:
        copy.wait()


def _comm_fused(stages, counts, srcs, lands, name, inplace=False):
    ns, nl, k = len(srcs), len(lands), len(stages)

    def body(*refs):
        src_refs = refs[:ns]
        land_refs = refs[ns + (nl if inplace else 0):ns + (nl if inplace else 0) + nl]
        sem_refs = refs[len(refs) - 3 * k:]
        for s, stage_fn in enumerate(stages):
            starts, waits = stage_fn(src_refs, land_refs, tuple(sem_refs[3 * s:3 * s + 3]))
            for cp in starts:
                cp.start()
            for cp, kind in waits:
                _wait_for(cp, kind)

    scratch = []
    for cnt in counts:
        scratch += [_dma_sems(c) for c in cnt]
    outs = pl.pallas_call(
        body, out_shape=tuple(jax.ShapeDtypeStruct(a.shape, a.dtype) for a in lands),
        in_specs=[ANY] * (ns + (nl if inplace else 0)), out_specs=(ANY,) * nl,
        input_output_aliases={ns + i: i for i in range(nl)} if inplace else {},
        scratch_shapes=scratch, name=name)(*srcs, *(lands if inplace else ()))
    return list(outs)


class _SplitComm:
    def __init__(self, stages, counts, srcs, lands, name):
        self.stages, self.counts, self.name = stages, counts, name
        self.ns = len(srcs)
        self.data = [pltpu.with_memory_space_constraint(a, pltpu.HBM) for a in list(srcs) + list(lands)]
        self.sems = None
        self.step = 0

    def advance(self, after=None):
        i, k, nd, ns = self.step, len(self.stages), len(self.data), self.ns
        first, last = i == 0, i == k
        stages = self.stages
        after = list(after) if isinstance(after, (list, tuple)) else [after]

        def body(*refs):
            data = refs[:nd]
            pos = nd
            if not first:
                old = tuple(refs[pos:pos + 3])
                pos += 3 + len(after)
            if not last:
                new = tuple(refs[pos:pos + 3])
            if not first:
                for cp, kind in stages[i - 1](data[:ns], data[ns:], old)[1]:
                    _wait_for(cp, kind)
            if not last:
                for cp in stages[i](data[:ns], data[ns:], new)[0]:
                    cp.start()
                refs[len(refs) - 1][...] = jnp.zeros((8, LANES), F32)

        args = list(self.data)
        in_specs = [HBM_SPEC] * nd
        if not first:
            args += list(self.sems) + after
            in_specs += [SEM_SPEC] * 3 + [ANY] * len(after)
        out_shape, out_specs = [], []
        if not last:
            out_shape += [_dma_sems(c) for c in self.counts[i]]
            out_specs += [SEM_SPEC] * 3
        out_shape += [pltpu.HBM(a.shape, a.dtype) for a in self.data]
        out_specs += [HBM_SPEC] * nd
        if not last:
            out_shape.append(jax.ShapeDtypeStruct((8, LANES), F32))
            out_specs.append(pl.BlockSpec(memory_space=pltpu.VMEM))
        off = 0 if last else 3
        outs = pl.pallas_call(
            body, out_shape=tuple(out_shape), in_specs=in_specs, out_specs=tuple(out_specs),
            input_output_aliases={d: off + d for d in range(nd)},
            compiler_params=pltpu.CompilerParams(has_side_effects=pltpu.SideEffectType.DATAFLOW_SIDE_EFFECTING),
            name=f"{self.name}_{i}")(*args)
        self.sems = None if last else outs[:3]
        self.data = list(outs[off:off + nd])
        self.step += 1
        return None if last else outs[len(outs) - 1]

    def lands(self):
        return self.data[self.ns:]


def _gather_stages(spec):
    n = len(spec)

    def parts(srcs, lands):
        x, y, c = _coords()
        out = []
        for w, (_, axis) in enumerate(spec):
            h = srcs[w].shape[0] // 2
            mine, theirs = pl.ds(c * h, h), pl.ds((1 - c) * h, h)
            out.append((srcs[w].at[mine], lambda layers, k, w=w, axis=axis: _chip_window(axis, lands[w], layers, k),
                        mine, theirs))
        return x, y, c, 2 * x + y, (x, y, 1 - c), _other_chips(x, y), out

    def remote(src, dst, send, recv, idx, to):
        return pltpu.make_async_remote_copy(src_ref=src, dst_ref=dst, send_sem=send.at[idx], recv_sem=recv.at[idx],
                                            device_id=to, device_id_type=MESH)

    def stage0(srcs, lands, sems):
        send, recv, local = sems
        x, y, c, me, sib, chips, ps = parts(srcs, lands)
        starts, waits = [], []
        for w, (src, dst, mine, theirs) in enumerate(ps):
            lc = pltpu.make_async_copy(src, dst(mine, me), local.at[w])
            first = [remote(src, dst(mine, me), send, recv, 4 * w, sib)]
            first += [remote(src, dst(mine, me), send, recv, 4 * w + 1 + j, (cx, cy, c)) for j, (cx, cy) in enumerate(chips)]
            starts += [lc] + first
            waits.append((remote(src, dst(theirs, me), send, recv, 4 * w, (x, y, c)), "recv"))
            waits += [(remote(src, dst(mine, 2 * cx + cy), send, recv, 4 * w + 1 + j, (x, y, c)), "recv")
                      for j, (cx, cy) in enumerate(chips)]
            waits += [(cp, "send") for cp in first] + [(lc, "local")]
        return starts, waits

    def stage1(srcs, lands, sems):
        send, recv, _ = sems
        x, y, c, me, sib, chips, ps = parts(srcs, lands)
        starts, waits = [], []
        for w, (src, dst, mine, theirs) in enumerate(ps):
            for j, (cx, cy) in enumerate(chips):
                blk = dst(mine, 2 * cx + cy)
                fwd = remote(blk, blk, send, recv, 3 * w + j, sib)
                starts.append(fwd)
                waits.append((remote(src, dst(theirs, 2 * cx + cy), send, recv, 3 * w + j, (x, y, c)), "recv"))
                waits.append((fwd, "send"))
        return starts, waits

    return [stage0, stage1], [(4 * n, 4 * n, n), (3 * n, 3 * n, 0)]


def _swap_stages(spec):
    n = len(spec)

    def stage(srcs, lands, sems):
        send, recv, _ = sems
        x, y, c = _coords()
        starts, waits = [], []
        for w in range(n):
            h = srcs[w].shape[0] // 2
            cp = pltpu.make_async_remote_copy(src_ref=srcs[w].at[pl.ds((1 - c) * h, h)], dst_ref=lands[w],
                                              send_sem=send.at[w], recv_sem=recv.at[w],
                                              device_id=(x, y, 1 - c), device_id_type=MESH)
            starts.append(cp)
            waits += [(cp, "recv"), (cp, "send")]
        return starts, waits

    return [stage], [(n, n, 0)]


def _scatter_stages(spec):
    n = len(spec)

    def stage(srcs, lands, sems):
        send, recv, local = sems
        x, y, c = _coords()
        me = 2 * x + y
        starts, waits = [], []
        for w, (_, axis) in enumerate(spec):
            layers = pl.ds(0, srcs[w].shape[0])
            own = _chip_window(axis, srcs[w], layers, me)
            lc = pltpu.make_async_copy(own, lands[w].at[me], local.at[w])
            starts.append(lc)
            for j, (cx, cy) in enumerate(_other_chips(x, y)):
                cp = pltpu.make_async_remote_copy(src_ref=_chip_window(axis, srcs[w], layers, 2 * cx + cy),
                                                  dst_ref=lands[w].at[me], send_sem=send.at[3 * w + j],
                                                  recv_sem=recv.at[3 * w + j], device_id=(cx, cy, c), device_id_type=MESH)
                starts.append(cp)
                waits.append((pltpu.make_async_remote_copy(
                    src_ref=own, dst_ref=lands[w].at[2 * cx + cy], send_sem=send.at[3 * w + j], recv_sem=recv.at[3 * w + j],
                    device_id=(x, y, c), device_id_type=MESH), "recv"))
                waits.append((cp, "send"))
            waits.append((lc, "local"))
        return starts, waits

    return [stage], [(3 * n, 3 * n, n)]


def _share_stages(spec):
    n = len(spec)

    def stage(srcs, lands, sems):
        send, recv, _ = sems
        x, y, c = _coords()
        starts, waits = [], []
        for w in range(n):
            h = lands[w].shape[0] // 2
            mine, theirs = lands[w].at[pl.ds(c * h, h)], lands[w].at[pl.ds((1 - c) * h, h)]
            cp = pltpu.make_async_remote_copy(src_ref=mine, dst_ref=mine, send_sem=send.at[w], recv_sem=recv.at[w],
                                              device_id=(x, y, 1 - c), device_id_type=MESH)
            starts.append(cp)
            waits.append((pltpu.make_async_remote_copy(src_ref=theirs, dst_ref=theirs, send_sem=send.at[w],
                                                       recv_sem=recv.at[w], device_id=(x, y, c), device_id_type=MESH),
                          "recv"))
            waits.append((cp, "send"))
        return starts, waits

    return [stage], [(n, n, 0)]


def _shard_of(p, axis):
    if axis is None:
        return (p.shape[0],) + tuple(p.shape[2:])
    s = list(p.shape)
    s[1 + axis] //= N_CHIPS
    return tuple(s)


def _gather8_stages():
    def stage(srcs, lands, sems):
        send, recv, local = sems
        x, y, c = _coords()
        me = 4 * x + 2 * y + c
        lc = pltpu.make_async_copy(srcs[0], lands[0].at[me], local.at[0])
        starts, waits = [lc], []
        for kk in range(1, 8):
            to = (1 - x if kk & 4 else x, 1 - y if kk & 2 else y, 1 - c if kk & 1 else c)
            cp = pltpu.make_async_remote_copy(src_ref=srcs[0], dst_ref=lands[0].at[me], send_sem=send.at[kk - 1],
                                              recv_sem=recv.at[kk - 1], device_id=to, device_id_type=MESH)
            starts.append(cp)
            waits.append((pltpu.make_async_remote_copy(
                src_ref=srcs[0], dst_ref=lands[0].at[4 * to[0] + 2 * to[1] + to[2]], send_sem=send.at[kk - 1],
                recv_sem=recv.at[kk - 1], device_id=(x, y, c), device_id_type=MESH), "recv"))
            waits.append((cp, "send"))
        waits.append((lc, "local"))
        return starts, waits

    return [stage], [(7, 7, 1)]


def _sum8(buf, name):
    _, rows, cols = buf.shape
    tr = _pick_rows(rows, cols * 4)

    def body(*refs):
        acc = refs[0][...]
        for r in refs[1:8]:
            acc = acc + r[...]
        refs[8][...] = acc

    return pl.pallas_call(
        body, out_shape=jax.ShapeDtypeStruct((rows, cols), F32), grid=(rows // tr,),
        in_specs=[pl.BlockSpec((None, tr, cols), lambda i, k=k: (k, i, 0)) for k in range(8)],
        out_specs=pl.BlockSpec((tr, cols), lambda i: (i, 0)),
        compiler_params=_params(("parallel",)), name=name)(*([buf] * 8))


def _reduce_begin(spec, gs, core, tag, riders=()):
    stages, counts = _swap_stages(spec)
    got = _comm_fused(stages, counts, list(gs) + list(riders),
                      [jax.ShapeDtypeStruct((g.shape[0] // 2,) + g.shape[1:], g.dtype) for g in gs], "swap_" + tag)
    pair = [_pair_sum(a, r, core, "pair_sum_" + n) for a, r, (n, _) in zip(gs, got, spec)]
    stages, counts = _scatter_stages(spec)
    lands = [lax.empty((N_CHIPS,) + _shard_of(p, axis), p.dtype) for p, (_, axis) in zip(pair, spec)]
    comm = _SplitComm(stages, counts, pair, lands, "scatter_" + tag)
    return comm, comm.advance()


def _reduce_finish(spec, comm, core, tag, after):
    comm.advance(after=after)
    halves = [_sum4(q, core, "sum4_" + n) for q, (n, _) in zip(comm.lands(), spec)]
    stages, counts = _share_stages(spec)
    return _comm_fused(stages, counts, [], halves, "share_" + tag, inplace=True)


SMALL = (("ssd_conv_w", 2), ("pool_scale", 1), ("ffn_conv_w", 2))
REPL = ("ssd_conv_b", "ssd_dt_bias", "ssd_a_log", "ssd_d", "ssd_norm_w", "ffn_conv_b",
        "norm_mix_pre", "norm_mix_post", "norm_ffn_pre", "norm_ffn_post")
WEIGHTS = ("ssd_w_in", "ssd_conv_w", "ssd_conv_b", "ssd_dt_bias", "ssd_a_log", "ssd_d", "ssd_norm_w", "ssd_w_out",
           "pool_w", "pool_scale", "ffn_w_up", "ffn_conv_w", "ffn_conv_b", "ffn_w_down", "norm_mix_pre",
           "norm_mix_post", "norm_ffn_pre", "norm_ffn_post")


def _flat_rows(n):
    unit = 2 * 16 * FLAT_COLS
    return 2 * 16 * ((n + unit - 1) // unit)


def _flatten_shards(arrs, dtype):
    flat = jnp.concatenate([a.astype(dtype).reshape(-1) for a in arrs])
    rows = _flat_rows(flat.shape[0])
    flat = jnp.pad(flat, (0, rows * FLAT_COLS - flat.shape[0]))
    return flat.reshape(2, rows // 2, FLAT_COLS)


def _unflatten_full(gathered, shard_shapes, axes):
    per_chip = jnp.swapaxes(gathered, 0, 1).reshape(N_CHIPS, -1)
    out, off = [], 0
    for shp, ax in zip(shard_shapes, axes):
        n = math.prod(shp)
        pieces = [per_chip[k, off:off + n].reshape(shp) for k in range(N_CHIPS)]
        out.append(jnp.concatenate(pieces, axis=ax))
        off += n
    return out


def kernel(x, ssd_w_in, ssd_conv_w, ssd_conv_b, ssd_dt_bias, ssd_a_log, ssd_d, ssd_norm_w, ssd_w_out, pool_w, pool_scale, ffn_w_up, ffn_conv_w, ffn_conv_b, ffn_w_down, norm_mix_pre, norm_mix_post, norm_ffn_pre, norm_ffn_post, loss_target, m_ssd_w_in, m_ssd_conv_w, m_ssd_conv_b, m_ssd_dt_bias, m_ssd_a_log, m_ssd_d, m_ssd_norm_w, m_ssd_w_out, m_pool_w, m_pool_scale, m_ffn_w_up, m_ffn_conv_w, m_ffn_conv_b, m_ffn_w_down, m_norm_mix_pre, m_norm_mix_post, m_norm_ffn_pre, m_norm_ffn_post, v_ssd_w_in, v_ssd_conv_w, v_ssd_conv_b, v_ssd_dt_bias, v_ssd_a_log, v_ssd_d, v_ssd_norm_w, v_ssd_w_out, v_pool_w, v_pool_scale, v_ffn_w_up, v_ffn_conv_w, v_ffn_conv_b, v_ffn_w_down, v_norm_mix_pre, v_norm_mix_post, v_norm_ffn_pre, v_norm_ffn_post):
    wts = dict(ssd_w_in=ssd_w_in, ssd_conv_w=ssd_conv_w, ssd_conv_b=ssd_conv_b, ssd_dt_bias=ssd_dt_bias,
               ssd_a_log=ssd_a_log, ssd_d=ssd_d, ssd_norm_w=ssd_norm_w, ssd_w_out=ssd_w_out, pool_w=pool_w,
               pool_scale=pool_scale, ffn_w_up=ffn_w_up, ffn_conv_w=ffn_conv_w, ffn_conv_b=ffn_conv_b,
               ffn_w_down=ffn_w_down, norm_mix_pre=norm_mix_pre, norm_mix_post=norm_mix_post,
               norm_ffn_pre=norm_ffn_pre, norm_ffn_post=norm_ffn_post)
    mom = dict(ssd_w_in=m_ssd_w_in, ssd_conv_w=m_ssd_conv_w, ssd_conv_b=m_ssd_conv_b, ssd_dt_bias=m_ssd_dt_bias,
               ssd_a_log=m_ssd_a_log, ssd_d=m_ssd_d, ssd_norm_w=m_ssd_norm_w, ssd_w_out=m_ssd_w_out, pool_w=m_pool_w,
               pool_scale=m_pool_scale, ffn_w_up=m_ffn_w_up, ffn_conv_w=m_ffn_conv_w, ffn_conv_b=m_ffn_conv_b,
               ffn_w_down=m_ffn_w_down, norm_mix_pre=m_norm_mix_pre, norm_mix_post=m_norm_mix_post,
               norm_ffn_pre=m_norm_ffn_pre, norm_ffn_post=m_norm_ffn_post)
    var = dict(ssd_w_in=v_ssd_w_in, ssd_conv_w=v_ssd_conv_w, ssd_conv_b=v_ssd_conv_b, ssd_dt_bias=v_ssd_dt_bias,
               ssd_a_log=v_ssd_a_log, ssd_d=v_ssd_d, ssd_norm_w=v_ssd_norm_w, ssd_w_out=v_ssd_w_out, pool_w=v_pool_w,
               pool_scale=v_pool_scale, ffn_w_up=v_ffn_w_up, ffn_conv_w=v_ffn_conv_w, ffn_conv_b=v_ffn_conv_b,
               ffn_w_down=v_ffn_w_down, norm_mix_pre=v_norm_mix_pre, norm_mix_post=v_norm_mix_post,
               norm_ffn_pre=v_norm_ffn_pre, norm_ffn_post=v_norm_ffn_post)

    bl, seq, d = x.shape
    t = bl * seq
    depth = norm_mix_pre.shape[0]
    n_ssd = ssd_w_out.shape[0]
    d_inner = ssd_w_out.shape[1] * N_CHIPS
    nheads = d_inner // HEAD_DIM
    hpg = nheads // N_GROUPS
    gw = d_inner // N_GROUPS
    xbc = ssd_conv_w.shape[2] * N_CHIPS
    f2 = ffn_w_up.shape[2] * N_CHIPS
    ff = f2 // 2
    dg = d // 4
    cy = lax.axis_index("c")
    chip = 2 * lax.axis_index("x") + lax.axis_index("y")

    small_shapes = [wts[n].shape for n, _ in SMALL]
    small_axes = [a for _, a in SMALL]
    small_flat = _flatten_shards([wts[n] for n, _ in SMALL], F32)
    small_half = lax.dynamic_index_in_dim(small_flat, cy, 0, keepdims=False)
    small_all = _allgather_halves(small_half, "gather_small")
    conv_w, p_scale, f_conv_w = _unflatten_full(small_all, small_shapes, small_axes)
    def full_shapes(spec, shards):
        return [jax.ShapeDtypeStruct(_full_shape(axis, s.shape), s.dtype) for s, (_, axis) in zip(shards, spec)]

    def row_halves(a):
        return a.reshape((2, a.shape[0] // 2) + a.shape[1:])

    def join_w_in(g):
        return jnp.concatenate([g[:, k] for k in range(N_CHIPS)], axis=-1).reshape(d, -1)

    def join_w_out(g):
        r2 = g.shape[1] // N_CHIPS
        return jnp.concatenate([g[hf, k * r2:(k + 1) * r2] for k in range(N_CHIPS) for hf in range(2)], axis=0)

    ssd_spec = (("ssd_w_in", None), ("ssd_w_out", 0))
    first_shards = [row_halves(wts[n][0].astype(BF16)) for n, _ in ssd_spec]
    stages, counts = _gather_stages(ssd_spec)
    g_in0, g_out0 = _comm_fused(stages, counts, first_shards, full_shapes(ssd_spec, first_shards), "gather_first")
    w_in, w_out = [join_w_in(g_in0)], [join_w_out(g_out0)]
    rest_spec = ssd_spec * (n_ssd - 1) + (("pool_w", 1),) + FFNW
    rest_shards = [row_halves(wts[n][jj].astype(BF16)) for jj in range(1, n_ssd) for n, _ in ssd_spec]
    rest_shards += [wts["pool_w"].astype(BF16)] + [wts[n].astype(BF16) for n, _ in FFNW]
    stages, counts = _gather_stages(rest_spec)
    ffn_gather = _SplitComm(stages, counts, rest_shards + [g_out0],
                            [lax.empty(s.shape, s.dtype) for s in full_shapes(rest_spec, rest_shards)], "gather_rest")
    gather_token = ffn_gather.advance()

    def pad_heads(a):
        lead = a.shape[:-1]
        a = a.reshape(lead + (N_GROUPS, hpg))
        a = jnp.pad(a, [(0, 0)] * len(lead) + [(0, 0), (0, LANES - hpg)])
        return a.reshape(lead + (N_GROUPS * LANES,))

    def unpad_heads(a):
        lead = a.shape[:-1]
        return a.reshape(lead + (N_GROUPS, LANES))[..., :hpg].reshape(lead + (nheads,))

    def group_rows(a, width):
        return jnp.broadcast_to(a.reshape(N_GROUPS, 1, width), (N_GROUPS, 8, width))

    def pad_w_in(w):
        return jnp.concatenate([w[..., :d_inner + xbc], pad_heads(w[..., d_inner + xbc:])], axis=-1)

    w_in_p = [pad_w_in(w_in[0])]
    zw = w_in_p[0].shape[-1]
    w_pool = None

    x2 = x.reshape(t, d)
    tgt2 = loss_target.reshape(t, d)
    w_up = w_down = None

    saved = []
    cur = x2
    tokens = []
    h = _norm_fwd(cur, norm_mix_pre[0:1], BF16, "norm_pre_b", after=[gather_token])
    for i in range(depth):
        j = i // 2
        sv = dict(x_in=cur)
        if i % 2 == 0:
            zx = _mm(h, w_in_p[j], "nn", F32, "mm_ssd_in", 2048, 512, d).reshape(bl, seq, zw)
            xc, xpre = _ssd_conv_fwd(zx, conv_w[j], ssd_conv_b[j:j + 1], d_inner, "ssd_conv_fwd")
            dtb = group_rows(pad_heads(ssd_dt_bias[j]), LANES)
            alog = group_rows(pad_heads(ssd_a_log[j]), LANES)
            dskip = group_rows(jnp.repeat(ssd_d[j], HEAD_DIM), gw)
            nw = group_rows(ssd_norm_w[j], gw)
            y, yn, st = _ssd_fwd(xc, zx, dtb, alog, dskip, nw, d_inner, "ssd_fwd")
            if i == 0:
                tokens.append(ffn_gather.advance(after=yn))
            mix = _mm(yn.reshape(t, d_inner), w_out[j], "nn", F32, "mm_ssd_out", 512, 512, d_inner)
            sv.update(h=h, zx=zx, xc=xc, xpre=xpre, y=y, yn=yn, st=st, dtb=dtb, alog=alog, dskip=dskip, nw=nw)
        else:
            mix = _pool_fwd(h.reshape(bl, seq, d), w_pool[j], p_scale[j:j + 1], "pool_fwd").reshape(t, d)
            sv.update(h=h)
        sv.update(mix=mix)
        mid, u = _norm_post_pre(mix, norm_mix_post[i:i + 1], cur, norm_ffn_pre[i:i + 1], BF16, "norm_post_pre_b",
                                after=tokens)
        tokens = []
        if i == 0:
            ffn_gather.advance(after=u)
            rest = ffn_gather.lands()
            for jj in range(1, n_ssd):
                w_in_p.append(pad_w_in(join_w_in(rest[2 * (jj - 1)])))
                w_out.append(join_w_out(rest[2 * (jj - 1) + 1]))
            w_pool, w_up, w_down = rest[2 * (n_ssd - 1):]
        hpre = _mm(u, w_up, "nn", BF16, "mm_up", 2048, 512, d, b_layer=i).reshape(bl, seq, f2)
        act, pre_g, pre_v = _ffn_act_fwd(hpre, f_conv_w[i], ffn_conv_b[i:i + 1], "ffn_act_fwd")
        act = act.reshape(t, ff)
        fo = _mm(act, w_down, "nn", F32, "mm_down", 1024, 512, ff, b_layer=i)
        if i + 1 == depth:
            cur = _norm_fwd(fo, norm_ffn_post[i:i + 1], F32, "norm_post", resid=mid)
        elif i % 2 == 0:
            cur, h = _norm_post_pre(fo, norm_ffn_post[i:i + 1], mid, norm_mix_pre[i + 1:i + 2], F32, "norm_post_pre_f")
        else:
            cur, h = _norm_post_pre(fo, norm_ffn_post[i:i + 1], mid, norm_mix_pre[i + 1:i + 2], BF16, "norm_post_pre_b")
        sv.update(mid=mid, u=u, hpre=hpre, pre_g=pre_g, pre_v=pre_v, act=act, fo=fo)
        saved.append(sv)

    dcur, loss_part = _loss_head(cur, tgt2, "loss_head")

    g = {n: [None] * wts[n].shape[0] for n in WEIGHTS}
    gbuf = dict(up=lax.empty((depth, d, f2), F32), down=lax.empty((depth, ff, d), F32),
                out=lax.empty((n_ssd, d_inner, d), F32), win=lax.empty((n_ssd, d, zw), F32))
    core = cy.reshape(1).astype(jnp.int32)

    def mixer_bwd(i, dmid, behind=()):
        j = i // 2
        sv = saved[i]
        done = []
        if i % 2 == 0:
            dmix, g["norm_mix_post"][i] = _norm_bwd(sv["mix"], norm_mix_post[i:i + 1], dmid, BF16, "norm_bwd_b",
                                                    after=behind)
            dyn = _mm(dmix, w_out[j], "nt", F32, "mm_ssd_out_dx", 1024, 1024, d)
            gbuf["out"], tok = _mm(sv["yn"].reshape(t, d_inner), dmix, "tn", F32, "mm_ssd_out_dw", 1024, 512, 2048,
                                   out_buf=(gbuf["out"], j))
            done.append(tok)
            dz, dxs, dbm, dcm, ddt, dnw, dd, dal, dbias = _ssd_bwd(
                sv["xc"], sv["zx"], sv["y"], dyn.reshape(bl, seq, d_inner), sv["st"], sv["dtb"], sv["alog"],
                sv["dskip"], sv["nw"], d_inner, "ssd_bwd")
            g["ssd_norm_w"][j] = dnw[:, 0, :].reshape(d_inner)
            g["ssd_d"][j] = dd[:, 0, :hpg].reshape(nheads)
            g["ssd_a_log"][j] = dal[:, 0, :hpg].reshape(nheads)
            g["ssd_dt_bias"][j] = dbias[:, 0, :hpg].reshape(nheads)
            dxbc, dcw, dcb = _ssd_conv_bwd(sv["zx"], sv["xpre"], (dxs, dbm, dcm), conv_w[j], d_inner, "ssd_conv_bwd")
            g["ssd_conv_w"][j] = dcw
            g["ssd_conv_b"][j] = dcb[0]
            dzs = [dz.reshape(t, d_inner), dxbc.reshape(t, xbc), ddt.reshape(t, N_GROUPS * LANES)]
            dh = _mm(dzs, w_in_p[j], "nt", BF16, "mm_ssd_in_dx", 1024, d, [1024, 1024, 512])
            gbuf["win"], tok = _mm(sv["h"], dzs, "tn", F32, "mm_ssd_in_dw", 1024, 512, 2048, out_buf=(gbuf["win"], j))
            done.append(tok)
        else:
            dmix, g["norm_mix_post"][i] = _norm_bwd(sv["mix"], norm_mix_post[i:i + 1], dmid, F32, "norm_bwd_f",
                                                    after=behind)
            dh3, g["pool_w"][j], dps = _pool_bwd(sv["h"].reshape(bl, seq, d), dmix.reshape(bl, seq, d), w_pool[j],
                                                 p_scale[j:j + 1], "pool_bwd")
            g["pool_scale"][j] = dps[0]
            dh = dh3.reshape(t, d)
        dx_in, g["norm_mix_pre"][i] = _norm_bwd(sv["x_in"], norm_mix_pre[i:i + 1], dh, F32, "norm_bwd_r", resid=dmid,
                                                after=done)
        return dx_in

    ffn_comm = None
    for i in reversed(range(depth)):
        sv = saved[i]
        dfo, g["norm_ffn_post"][i] = _norm_bwd(sv["fo"], norm_ffn_post[i:i + 1], dcur, BF16, "norm_bwd_b")
        dact = _mm(dfo, w_down, "nt", BF16, "mm_down_dx", 1024, ff // 2, d, b_layer=i)
        gbuf["down"], tok_down = _mm(sv["act"], dfo, "tn", F32, "mm_down_dw", ff // 2, 512, 2048,
                                     out_buf=(gbuf["down"], i))
        dhg, dhv, dcw, dcb = _ffn_act_bwd(sv["hpre"], sv["pre_g"], sv["pre_v"], dact.reshape(bl, seq, ff), f_conv_w[i],
                                          "ffn_act_bwd")
        g["ffn_conv_w"][i] = dcw
        g["ffn_conv_b"][i] = dcb[0]
        dhs = [dhg.reshape(t, ff), dhv.reshape(t, ff)]
        du = _mm(dhs, w_up, "nt", BF16, "mm_up_dx", 1024, d, ff, b_layer=i)
        gbuf["up"], tok_up = _mm(sv["u"], dhs, "tn", F32, "mm_up_dw", 512, ff // 2, 2048, out_buf=(gbuf["up"], i))
        dmid, g["norm_ffn_pre"][i] = _norm_bwd(sv["mid"], norm_ffn_pre[i:i + 1], du, F32, "norm_bwd_r", resid=dcur,
                                               after=[tok_down, tok_up])
        if i > 0:
            dcur = mixer_bwd(i, dmid)
        else:
            ffn_comm, ffn_token = _reduce_begin(FFNW, [gbuf["up"], gbuf["down"]], core, "ffn")
            dcur = mixer_bwd(0, dmid, behind=[ffn_token])

    grad_x = dcur.reshape(bl, seq, d)
    for n in ("norm_mix_pre", "norm_mix_post", "norm_ffn_pre", "norm_ffn_post"):
        g[n] = [a[0] for a in g[n]]
    small_names = [n for n, _ in SMALL] + list(REPL)
    full = {n: jnp.stack(g[n], axis=0) for n in small_names}

    g_in = jnp.concatenate([gbuf["win"][..., :d_inner + xbc], unpad_heads(gbuf["win"][..., d_inner + xbc:])], axis=-1)
    g_in_cm = jnp.swapaxes(g_in.reshape(n_ssd, d, N_CHIPS, -1), 1, 2)
    vec = jnp.concatenate([full[n].reshape(-1) for n in small_names] + [loss_part[0, :1]])
    nvec = vec.shape[0]
    vrows = 16 * ((nvec + 16 * FLAT_COLS - 1) // (16 * FLAT_COLS))
    vec = jnp.pad(vec, (0, vrows * FLAT_COLS - nvec)).reshape(vrows, FLAT_COLS)
    stages, counts = _gather8_stages()
    small_comm = _SplitComm(stages, counts, [vec], [lax.empty((8, vrows, FLAT_COLS), F32)], "gather_small_grads")
    small_token = small_comm.advance()
    mix_comm, mix_token = _reduce_begin(MIXW, [g_in_cm, gbuf["out"], jnp.stack(g["pool_w"], axis=0)], core, "mixers",
                                        riders=[small_token])

    grads, deltas, new_m, new_v = {}, {}, {}, {}

    def adamw(n, gr):
        shp = wts[n].shape
        two = (math.prod(shp[:-1]), shp[-1])
        dl, mn, vn = _adamw(wts[n].reshape(two), gr.reshape(two), mom[n].reshape(two), var[n].reshape(two),
                            "adamw_" + n)
        grads[n], deltas[n], new_m[n], new_v[n] = gr, dl.reshape(shp), mn.reshape(shp), vn.reshape(shp)
        return dl

    small_comm.advance(after=mix_token)
    tot = _sum8(small_comm.lands()[0], "sum_small").reshape(-1)
    small_grads, off = {}, 0
    for n in small_names:
        cnt = math.prod(full[n].shape)
        small_grads[n] = tot[off:off + cnt].reshape(full[n].shape)
        off += cnt
    loss = tot[off]
    for n, ax in SMALL:
        w = wts[n].shape[ax]
        small_grads[n] = lax.dynamic_slice_in_dim(small_grads[n], chip * w, w, axis=ax)

    behind = [adamw(n, small_grads[n]) for n in small_names][-1:]
    ffn_grads = _reduce_finish(FFNW, ffn_comm, core, "ffn", after=mix_token)
    behind += [adamw(n, gr) for gr, (n, _) in zip(ffn_grads, FFNW)]
    mix_grads = _reduce_finish(MIXW, mix_comm, core, "mixers", after=behind)
    for gr, (n, _) in zip(mix_grads, MIXW):
        adamw(n, gr)

    return (loss, grad_x, *[grads[n] for n in WEIGHTS], *[deltas[n] for n in WEIGHTS],
            *[new_m[n] for n in WEIGHTS], *[new_v[n] for n in WEIGHTS])
```

```python
import functools
import math

import jax
import jax.numpy as jnp
from jax import lax
from jax.experimental import pallas as pl
from jax.experimental.pallas import tpu as pltpu

F32 = jnp.float32
BF16 = jnp.bfloat16
MESH = pl.DeviceIdType.MESH
ANY = pl.BlockSpec(memory_space=pl.ANY)

HEAD_DIM = 64
D_STATE = 128
CHUNK = 128
N_GROUPS = 4
SSD_CONV = 4
FFN_CONV = 3
EPS = 1e-6
N_CHIPS = 4
LANES = 128
FLAT_COLS = 1024

ADAM_LR = 0.001
ADAM_B1 = 0.9
ADAM_B2 = 0.999
ADAM_EPS = 1e-08
ADAM_WD = 0.01
ADAM_STEP = 10

VMEM_LIMIT_BYTES = 56 * 1024 * 1024


def _params(sem=None):
    kw = dict(vmem_limit_bytes=VMEM_LIMIT_BYTES)
    if sem is not None:
        kw["dimension_semantics"] = sem
    return pltpu.CompilerParams(**kw)


def _sigmoid(x):
    return 0.5 * jnp.tanh(0.5 * x) + 0.5


def _softplus(x):
    return jnp.maximum(x, 0.0) + jnp.log(1.0 + jnp.exp(-jnp.abs(x)))


def _dot(a, b, dn):
    return lax.dot_general(a, b, (dn, ((), ())), preferred_element_type=F32)


def _nn(a, b):
    return _dot(a, b, ((1,), (0,)))


def _nt(a, b):
    return _dot(a, b, ((1,), (1,)))


def _tn(a, b):
    return _dot(a, b, ((0,), (0,)))


def _split(x, parts):
    out = []
    r = x
    for _ in range(parts):
        p = r.astype(BF16)
        out.append(p)
        r = r - p.astype(F32)
    return out


def _sel_left(sel, x, parts=3):
    n = x.shape[1]
    r = _nn(sel, jnp.concatenate(_split(x, parts), axis=1))
    out = r[:, 0:n]
    for i in range(1, parts):
        out = out + r[:, i * n:(i + 1) * n]
    return out


def _sel_right(x, sel_stacked, parts=3):
    return _nn(jnp.concatenate(_split(x, parts), axis=1), sel_stacked)


def _mm(a, b, dims, out_dtype, name, tm, tn, tk, b_layer=None, out_buf=None):
    a_list = list(a) if isinstance(a, (list, tuple)) else [a]
    b_list = list(b) if isinstance(b, (list, tuple)) else [b]
    if dims in ("nn", "nt"):
        assert len(b_list) == 1
        m = a_list[0].shape[0]
        segs = [x.shape[1] for x in a_list]
        k = sum(segs)
        bshape = b_list[0].shape[-2:]
        n = bshape[1] if dims == "nn" else bshape[0]
        assert (bshape[0] if dims == "nn" else bshape[1]) == k
    else:
        assert len(a_list) == 1 and b_layer is None
        k, m = a_list[0].shape
        segs = [x.shape[1] for x in b_list]
        n = sum(segs)
    nseg = len(segs)
    tm, tn = min(tm, m), min(tn, n)
    if dims == "tn":
        tk = min(tk, k)
        tn = min(tn, min(segs))
        units = [tn] * nseg
        nk = k // tk
        assert k % tk == 0
    else:
        units = [min(u, s) for u, s in zip(tk if isinstance(tk, (list, tuple)) else [tk] * nseg, segs)]
        nk = sum(s // u for s, u in zip(segs, units))
    assert m % tm == 0 and n % tn == 0 and all(s % u == 0 for s, u in zip(segs, units)), (name, m, n, k, segs, units)
    counts = [s // u for s, u in zip(segs, units)]
    starts = [sum(counts[:s]) for s in range(nseg)]
    assert all(sum(segs[:s]) % units[s] == 0 for s in range(nseg)), (name, segs, units)
    first_block = [sum(segs[:s]) // units[s] for s in range(nseg)]
    dn = {"nn": ((1,), (0,)), "nt": ((1,), (1,)), "tn": ((0,), (0,))}[dims]

    same = len(set(units)) == 1
    nb_ops = len(b_list) if dims == "tn" else (1 if same else nseg)

    def body(*refs):
        a_refs = refs[:len(a_list)]
        b_refs = refs[len(a_list):len(a_list) + nb_ops]
        rest = refs[len(a_list) + nb_ops + (0 if out_buf is None else 1):]
        o_ref = rest[0]
        if out_buf is not None:
            rest[1][...] = jnp.zeros((8, LANES), F32)
            rest = rest[1:]
        acc = rest[1] if nk > 1 else None
        kk = pl.program_id(2)
        sel = kk if dims != "tn" else pl.program_id(1)

        def step(a_ref, b_ref):
            p = _dot(a_ref[...].astype(BF16), b_ref[...].astype(BF16), dn)
            if nk == 1:
                o_ref[...] = p.astype(out_dtype)
                return

            @pl.when(kk == 0)
            def _():
                acc[...] = p

            @pl.when(kk > 0)
            def _():
                acc[...] += p

        if nseg == 1:
            step(a_refs[0], b_refs[0])
        else:
            for s in range(nseg):
                @pl.when(jnp.logical_and(sel >= starts[s], sel < starts[s] + counts[s]))
                def _(s=s):
                    step(a_refs[s] if dims != "tn" else a_refs[0], b_refs[s if nb_ops > 1 else 0])

        if nk > 1:
            @pl.when(kk == nk - 1)
            def _():
                o_ref[...] = acc[...].astype(out_dtype)

    def seg_index(v, s):
        return v if nseg == 1 else jnp.clip(v - starts[s], 0, counts[s] - 1)

    lead = () if b_layer is None else (b_layer,)
    none = () if b_layer is None else (None,)
    def b_block(kk, s):
        return kk if same else first_block[s] + seg_index(kk, s)

    if dims == "nn":
        a_specs = [pl.BlockSpec((tm, units[s]), lambda i, j, kk, s=s: (i, seg_index(kk, s))) for s in range(nseg)]
        b_specs = [pl.BlockSpec(none + (units[s], tn), lambda i, j, kk, s=s: lead + (b_block(kk, s), j))
                   for s in range(nb_ops)]
    elif dims == "nt":
        a_specs = [pl.BlockSpec((tm, units[s]), lambda i, j, kk, s=s: (i, seg_index(kk, s))) for s in range(nseg)]
        b_specs = [pl.BlockSpec(none + (tn, units[s]), lambda i, j, kk, s=s: lead + (j, b_block(kk, s)))
                   for s in range(nb_ops)]
    else:
        a_specs = [pl.BlockSpec((tk, tm), lambda i, j, kk: (kk, i))]
        b_specs = [pl.BlockSpec((tk, tn), lambda i, j, kk, s=s: (kk, seg_index(j, s))) for s in range(nseg)]
    args = a_list + (b_list * nb_ops if dims != "tn" else b_list)
    in_specs = a_specs + b_specs
    aliases = {}
    if out_buf is None:
        out_shape = jax.ShapeDtypeStruct((m, n), out_dtype)
        out_spec = pl.BlockSpec((tm, tn), lambda i, j, kk: (i, j))
    else:
        buf, slab = out_buf
        assert buf.shape[1:] == (m, n) and buf.dtype == out_dtype
        out_shape = (jax.ShapeDtypeStruct(buf.shape, out_dtype), jax.ShapeDtypeStruct((8, LANES), F32))
        out_spec = (pl.BlockSpec((None, tm, tn), lambda i, j, kk: (slab, i, j)),
                    pl.BlockSpec((8, LANES), lambda i, j, kk: (0, 0)))
        aliases = {len(args): 0}
        args = args + [buf]
        in_specs = in_specs + [ANY]
    return pl.pallas_call(
        body,
        out_shape=out_shape,
        grid=(m // tm, n // tn, nk),
        in_specs=in_specs,
        out_specs=out_spec,
        scratch_shapes=[] if nk == 1 else [pltpu.VMEM((tm, tn), F32)],
        input_output_aliases=aliases,
        compiler_params=_params(("parallel", "parallel", "arbitrary") if out_buf is None else ("arbitrary",) * 3),
        name=name,
    )(*args)


def _row_tile(t, want):
    tm = min(want, t)
    assert t % tm == 0
    return tm


def _norm_fwd(x, w, out_dtype, name, resid=None, after=()):
    t, d = x.shape
    tm = _row_tile(t, 512)
    after = [a for a in after if a is not None]

    def body(*refs):
        refs = refs[:len(refs) - 1 - len(after)] + refs[len(refs) - 1:]
        if resid is None:
            x_ref, w_ref, o_ref = refs
        else:
            x_ref, w_ref, r_ref, o_ref = refs
        xv = x_ref[...]
        r = lax.rsqrt(jnp.mean(xv * xv, axis=-1, keepdims=True) + EPS)
        y = (xv * r) * w_ref[...]
        if resid is not None:
            y = r_ref[...] + y
        o_ref[...] = y.astype(out_dtype)

    row = pl.BlockSpec((tm, d), lambda i: (i, 0))
    vec = pl.BlockSpec((1, d), lambda i: (0, 0))
    args = [x, w] + ([] if resid is None else [resid]) + after
    return pl.pallas_call(
        body, out_shape=jax.ShapeDtypeStruct((t, d), out_dtype), grid=(t // tm,),
        in_specs=[row, vec] + ([] if resid is None else [row]) + [ANY] * len(after), out_specs=row,
        compiler_params=_params(("parallel",)), name=name)(*args)


def _norm_post_pre(m, w_post, resid, w_pre, pre_dtype, name, after=()):
    t, d = m.shape
    tm = _row_tile(t, 512)
    after = [a for a in after if a is not None]

    def body(m_ref, w1_ref, r_ref, w2_ref, *rest):
        x_ref, u_ref = rest[len(after):]
        mv = m_ref[...]
        r1 = lax.rsqrt(jnp.mean(mv * mv, axis=-1, keepdims=True) + EPS)
        xv = r_ref[...] + (mv * r1) * w1_ref[...]
        x_ref[...] = xv
        r2 = lax.rsqrt(jnp.mean(xv * xv, axis=-1, keepdims=True) + EPS)
        u_ref[...] = ((xv * r2) * w2_ref[...]).astype(pre_dtype)

    row = pl.BlockSpec((tm, d), lambda i: (i, 0))
    vec = pl.BlockSpec((1, d), lambda i: (0, 0))
    return pl.pallas_call(
        body, out_shape=(jax.ShapeDtypeStruct((t, d), F32), jax.ShapeDtypeStruct((t, d), pre_dtype)), grid=(t // tm,),
        in_specs=[row, vec, row, vec] + [ANY] * len(after), out_specs=(row, row),
        compiler_params=_params(("parallel",)), name=name)(m, w_post, resid, w_pre, *after)


def _norm_bwd(src, w, dy, out_dtype, name, resid=None, after=()):
    t, d = src.shape
    tm = _row_tile(t, 512)
    after = [a for a in after if a is not None]

    def body(*refs):
        refs = refs[:len(refs) - 2 - len(after)] + refs[len(refs) - 2:]
        if resid is None:
            x_ref, w_ref, g_ref, o_ref, dw_ref = refs
        else:
            x_ref, w_ref, g_ref, r_ref, o_ref, dw_ref = refs
        xv = x_ref[...]
        g = g_ref[...].astype(F32)
        r = lax.rsqrt(jnp.mean(xv * xv, axis=-1, keepdims=True) + EPS)
        xh = xv * r
        gh = g * w_ref[...]
        mean = jnp.mean(gh * xh, axis=-1, keepdims=True)
        dx = r * (gh - xh * mean)
        if resid is not None:
            dx = r_ref[...] + dx
        o_ref[...] = dx.astype(out_dtype)
        part = jnp.sum(g * xh, axis=0, keepdims=True)

        @pl.when(pl.program_id(0) == 0)
        def _():
            dw_ref[...] = part

        @pl.when(pl.program_id(0) > 0)
        def _():
            dw_ref[...] += part

    row = pl.BlockSpec((tm, d), lambda i: (i, 0))
    vec = pl.BlockSpec((1, d), lambda i: (0, 0))
    args = [src, w, dy] + ([] if resid is None else [resid]) + after
    return pl.pallas_call(
        body,
        out_shape=(jax.ShapeDtypeStruct((t, d), out_dtype), jax.ShapeDtypeStruct((1, d), F32)),
        grid=(t // tm,),
        in_specs=[row, vec, row] + ([] if resid is None else [row]) + [ANY] * len(after),
        out_specs=(row, vec),
        compiler_params=_params(("arbitrary",)), name=name)(*args)


def _loss_head(y, target, name):
    t, d = y.shape
    tm = _row_tile(t, 512)

    def body(y_ref, t_ref, dy_ref, l_ref):
        e = y_ref[...] - t_ref[...]
        dy_ref[...] = e * (1.0 / d)
        col = jnp.sum(e * e, axis=0, keepdims=True)
        s = jnp.sum(col, axis=1, keepdims=True) * (0.5 / d)
        part = jnp.broadcast_to(s, (1, LANES))

        @pl.when(pl.program_id(0) == 0)
        def _():
            l_ref[...] = part

        @pl.when(pl.program_id(0) > 0)
        def _():
            l_ref[...] += part

    row = pl.BlockSpec((tm, d), lambda i: (i, 0))
    return pl.pallas_call(
        body,
        out_shape=(jax.ShapeDtypeStruct((t, d), F32), jax.ShapeDtypeStruct((1, LANES), F32)),
        grid=(t // tm,), in_specs=[row, row],
        out_specs=(row, pl.BlockSpec((1, LANES), lambda i: (0, 0))),
        compiler_params=_params(("arbitrary",)), name=name)(y, target)


def _window(ref, c, rows, seq, before, after):
    r0 = pl.multiple_of(c * rows, rows)
    parts = []
    if before:
        h0 = pl.multiple_of(jnp.maximum(r0 - before, 0), before)
        halo = ref[pl.ds(h0, before), :].astype(F32)
        parts.append(jnp.where(c > 0, halo, 0.0))
    parts.append(ref[pl.ds(r0, rows), :].astype(F32))
    if after:
        h1 = pl.multiple_of(jnp.minimum(r0 + rows, seq - after), after)
        halo = ref[pl.ds(h1, after), :].astype(F32)
        parts.append(jnp.where(c < seq // rows - 1, halo, 0.0))
    return parts[0] if len(parts) == 1 else jnp.concatenate(parts, axis=0)


def _lag(x, k):
    return pltpu.roll(x, k, 0) if k else x


def _lead(x, k):
    return pltpu.roll(x, x.shape[0] - k, 0) if k else x


SHIFT_ROWS = 128
SHIFT_COLS = 256


HALO = 16


def _conv3(ext, w, bias):
    acc = bias + w[2:3, :] * ext[HALO:, :]
    acc = acc + w[1:2, :] * _lag(ext, 1)[HALO:, :]
    return acc + w[0:1, :] * _lag(ext, 2)[HALO:, :]


def _ffn_act_fwd(hpre, cw, cb, name):
    b, seq, f2 = hpre.shape
    cbk = SHIFT_COLS
    nj = f2 // (2 * cbk)
    rows = min(SHIFT_ROWS, seq)

    def body(g_ref, v_ref, wg_ref, wv_ref, bg_ref, bv_ref, o_ref, pg_ref, pv_ref):
        def chunk(c, carry):
            gate = _conv3(_window(g_ref, c, rows, seq, HALO, 0), wg_ref[...], bg_ref[...])
            val = _conv3(_window(v_ref, c, rows, seq, HALO, 0), wv_ref[...], bv_ref[...])
            a = gate * _sigmoid(gate) * val
            here = pl.ds(pl.multiple_of(c * rows, rows), rows)
            o_ref[here, :] = a.astype(BF16)
            pg_ref[here, :] = gate.astype(BF16)
            pv_ref[here, :] = val.astype(BF16)
            return carry

        lax.fori_loop(0, seq // rows, chunk, 0)

    blk = lambda off: pl.BlockSpec((None, seq, cbk), lambda i, j: (i, 0, j + off))
    wsp = lambda r, off: pl.BlockSpec((r, cbk), lambda i, j: (0, j + off))
    half = jax.ShapeDtypeStruct((b, seq, f2 // 2), BF16)
    return pl.pallas_call(
        body, out_shape=(half, half, half), grid=(b, nj),
        in_specs=[blk(0), blk(nj), wsp(FFN_CONV, 0), wsp(FFN_CONV, nj), wsp(1, 0), wsp(1, nj)],
        out_specs=(blk(0), blk(0), blk(0)),
        compiler_params=_params(("parallel", "parallel")), name=name)(hpre, hpre, cw, cw, cb, cb)


def _ffn_act_bwd(hpre, pre_g, pre_v, da, cw, name):
    b, seq, f2 = hpre.shape
    cbk = SHIFT_COLS
    nj = f2 // (2 * cbk)
    rows = min(SHIFT_ROWS, seq)

    def body(g_ref, v_ref, pg_ref, pv_ref, da_ref, wg_ref, wv_ref, og_ref, ov_ref, dwg_ref, dwv_ref, dbg_ref, dbv_ref):
        wg, wv = wg_ref[...], wv_ref[...]

        def back(dpre, w, o_ref, x_ref, c, carry):
            here = pl.ds(pl.multiple_of(c * rows, rows), rows)
            leads = [dpre, _lead(dpre, 1), _lead(dpre, 2)]
            dx = w[2:3, :] * leads[0] + w[1:2, :] * leads[1] + w[0:1, :] * leads[2]
            o_ref[here, :] = dx[:rows, :].astype(BF16)
            x0 = x_ref[here, :].astype(F32)
            return tuple(carry[k] + jnp.sum(leads[k][:rows, :] * x0, axis=0, keepdims=True) for k in range(FFN_CONV)) + (
                carry[FFN_CONV] + jnp.sum(dpre[:rows, :], axis=0, keepdims=True),)

        def chunk(c, carry):
            cg, cv = carry
            gate = _window(pg_ref, c, rows, seq, 0, HALO)
            val = _window(pv_ref, c, rows, seq, 0, HALO)
            dav = _window(da_ref, c, rows, seq, 0, HALO)
            sg = _sigmoid(gate)
            cg = back(dav * val * (sg * (1.0 + gate * (1.0 - sg))), wg, og_ref, g_ref, c, cg)
            cv = back(dav * (gate * sg), wv, ov_ref, v_ref, c, cv)
            return cg, cv

        z = jnp.zeros((1, cbk), F32)
        cg, cv = lax.fori_loop(0, seq // rows, chunk, ((z,) * (FFN_CONV + 1), (z,) * (FFN_CONV + 1)))
        dwg = jnp.concatenate([cg[2], cg[1], cg[0]], axis=0)
        dwv = jnp.concatenate([cv[2], cv[1], cv[0]], axis=0)

        @pl.when(pl.program_id(1) == 0)
        def _():
            dwg_ref[...] = dwg
            dwv_ref[...] = dwv
            dbg_ref[...] = cg[FFN_CONV]
            dbv_ref[...] = cv[FFN_CONV]

        @pl.when(pl.program_id(1) > 0)
        def _():
            dwg_ref[...] += dwg
            dwv_ref[...] += dwv
            dbg_ref[...] += cg[FFN_CONV]
            dbv_ref[...] += cv[FFN_CONV]

    blk = lambda off: pl.BlockSpec((None, seq, cbk), lambda j, i: (i, 0, j + off))
    wsp = lambda r, off: pl.BlockSpec((r, cbk), lambda j, i: (0, j + off))
    half = jax.ShapeDtypeStruct((b, seq, f2 // 2), BF16)
    dwshape = jax.ShapeDtypeStruct((FFN_CONV, f2 // 2), F32)
    dbshape = jax.ShapeDtypeStruct((1, f2 // 2), F32)
    dg, dv, dwg, dwv, dbg, dbv = pl.pallas_call(
        body,
        out_shape=(half, half, dwshape, dwshape, dbshape, dbshape),
        grid=(nj, b),
        in_specs=[blk(0), blk(nj), blk(0), blk(0), blk(0), wsp(FFN_CONV, 0), wsp(FFN_CONV, nj)],
        out_specs=(blk(0), blk(0), wsp(FFN_CONV, 0), wsp(FFN_CONV, 0), wsp(1, 0), wsp(1, 0)),
        compiler_params=_params(("parallel", "arbitrary")), name=name)(hpre, hpre, pre_g, pre_v, da, cw, cw)
    return dg, dv, jnp.concatenate([dwg, dwv], axis=1), jnp.concatenate([dbg, dbv], axis=1)


def _ssd_conv_fwd(zx, cw, cb, d_inner, name):
    b, seq, _ = zx.shape
    xbc = cw.shape[1]
    cbk = SHIFT_COLS
    off = d_inner // cbk
    rows = min(SHIFT_ROWS, seq)

    def body(h_ref, w_ref, b_ref, o_ref, p_ref):
        w = w_ref[...]
        bias = b_ref[...]

        def chunk(c, carry):
            ext = _window(h_ref, c, rows, seq, 8, 0)
            acc = bias + w[3:4, :] * ext[8:, :]
            for k in range(1, SSD_CONV):
                acc = acc + w[3 - k:4 - k, :] * _lag(ext, k)[8:, :]
            here = pl.ds(pl.multiple_of(c * rows, rows), rows)
            o_ref[here, :] = acc * _sigmoid(acc)
            p_ref[here, :] = acc.astype(BF16)
            return carry

        lax.fori_loop(0, seq // rows, chunk, 0)

    blk = pl.BlockSpec((None, seq, cbk), lambda i, j: (i, 0, j))
    return pl.pallas_call(
        body, out_shape=(jax.ShapeDtypeStruct((b, seq, xbc), F32), jax.ShapeDtypeStruct((b, seq, xbc), BF16)),
        grid=(b, xbc // cbk),
        in_specs=[pl.BlockSpec((None, seq, cbk), lambda i, j: (i, 0, j + off)),
                  pl.BlockSpec((SSD_CONV, cbk), lambda i, j: (0, j)),
                  pl.BlockSpec((1, cbk), lambda i, j: (0, j))],
        out_specs=(blk, blk),
        compiler_params=_params(("parallel", "parallel")), name=name)(zx, cw, cb)


def _ssd_conv_bwd(zx, pre, dparts, cw, d_inner, name):
    b, seq, _ = zx.shape
    xbc = cw.shape[1]
    cbk = SHIFT_COLS
    off = d_inner // cbk
    rows = min(SHIFT_ROWS, seq)
    nblk = [p.shape[2] // cbk for p in dparts]
    first = [sum(nblk[:s]) for s in range(len(dparts))]
    assert sum(nblk) == xbc // cbk

    def body(h_ref, p_ref, gx_ref, gb_ref, gc_ref, w_ref, o_ref, dw_ref, db_ref):
        w = w_ref[...]
        j = pl.program_id(0)

        def chunk(c, carry):
            dws, dbias = carry
            here = pl.ds(pl.multiple_of(c * rows, rows), rows)
            pre = _window(p_ref, c, rows, seq, 0, HALO)
            s = _sigmoid(pre)
            gsel = jnp.where(j < first[1], _window(gx_ref, c, rows, seq, 0, HALO),
                             jnp.where(j < first[2], _window(gb_ref, c, rows, seq, 0, HALO),
                                       _window(gc_ref, c, rows, seq, 0, HALO)))
            dpre = gsel * (s * (1.0 + pre * (1.0 - s)))
            leads = [dpre] + [_lead(dpre, k) for k in range(1, SSD_CONV)]
            dx = w[3:4, :] * leads[0]
            for k in range(1, SSD_CONV):
                dx = dx + w[3 - k:4 - k, :] * leads[k]
            o_ref[here, :] = dx[:rows, :].astype(BF16)
            x0 = h_ref[here, :]
            dws = tuple(dws[k] + jnp.sum(leads[k][:rows, :] * x0, axis=0, keepdims=True) for k in range(SSD_CONV))
            dbias = dbias + jnp.sum(dpre[:rows, :], axis=0, keepdims=True)
            return dws, dbias

        z = jnp.zeros((1, cbk), F32)
        dws, dbias = lax.fori_loop(0, seq // rows, chunk, ((z,) * SSD_CONV, z))
        dwv = jnp.concatenate([dws[3 - i] for i in range(SSD_CONV)], axis=0)

        @pl.when(pl.program_id(1) == 0)
        def _():
            dw_ref[...] = dwv
            db_ref[...] = dbias

        @pl.when(pl.program_id(1) > 0)
        def _():
            dw_ref[...] += dwv
            db_ref[...] += dbias

    return pl.pallas_call(
        body,
        out_shape=(jax.ShapeDtypeStruct((b, seq, xbc), BF16), jax.ShapeDtypeStruct((SSD_CONV, xbc), F32),
                   jax.ShapeDtypeStruct((1, xbc), F32)),
        grid=(xbc // cbk, b),
        in_specs=[pl.BlockSpec((None, seq, cbk), lambda j, i: (i, 0, j + off)),
                  pl.BlockSpec((None, seq, cbk), lambda j, i: (i, 0, j))] + [
                  pl.BlockSpec((None, seq, cbk), lambda j, i, s=s: (i, 0, jnp.clip(j - first[s], 0, nblk[s] - 1)))
                  for s in range(3)] + [
                  pl.BlockSpec((SSD_CONV, cbk), lambda j, i: (0, j))],
        out_specs=(pl.BlockSpec((None, seq, cbk), lambda j, i: (i, 0, j)),
                   pl.BlockSpec((SSD_CONV, cbk), lambda j, i: (0, j)),
                   pl.BlockSpec((1, cbk), lambda j, i: (0, j))),
        compiler_params=_params(("parallel", "arbitrary")), name=name)(zx, pre, *dparts, cw)


def _pool_sums(q, g, lead):
    sh = _lead if lead else _lag
    s2 = q + sh(q, 1)
    s4 = s2 + sh(s2, 2)
    s8 = s4 + sh(s4, 4)
    s16 = s8 + sh(s8, 8)
    return jnp.where(g == 0, s2, jnp.where(g == 1, s4, jnp.where(g == 2, s8, s16)))


def _pool_count(r0, n, g, shape):
    t = (r0 + lax.broadcasted_iota(jnp.int32, shape, 0) + 1).astype(F32)
    return jnp.minimum(t, (2 << g).astype(F32))


def _pool_fwd(h, pw, scale, name):
    b, seq, d = h.shape
    dg = d // 4
    rows = min(SHIFT_ROWS, seq)

    def body(h_ref, w_ref, s_ref, o_ref):
        g = pl.program_id(1)
        wmat = w_ref[...]
        sc = s_ref[...]

        def chunk(c, carry):
            r0 = c * rows
            ext = _window(h_ref, c, rows, seq, 16, 0)
            sums = _pool_sums(ext, g, False)[16:, :]
            mixed = sums / _pool_count(r0, rows, g, (rows, dg)) - ext[16:, :]
            o_ref[pl.ds(pl.multiple_of(r0, rows), rows), :] = _nn(mixed.astype(BF16), wmat) * sc
            return carry

        lax.fori_loop(0, seq // rows, chunk, 0)

    return pl.pallas_call(
        body, out_shape=jax.ShapeDtypeStruct((b, seq, d), F32), grid=(b, 4),
        in_specs=[pl.BlockSpec((None, seq, dg), lambda i, g: (i, 0, g)),
                  pl.BlockSpec((None, dg, dg), lambda i, g: (g, 0, 0)),
                  pl.BlockSpec((1, dg), lambda i, g: (0, g))],
        out_specs=pl.BlockSpec((None, seq, dg), lambda i, g: (i, 0, g)),
        compiler_params=_params(("parallel", "parallel")), name=name)(h, pw, scale)


def _pool_bwd(h, dout, pw, scale, name):
    b, seq, d = h.shape
    dg = d // 4
    rows = min(SHIFT_ROWS, seq)

    def body(h_ref, g_ref, w_ref, s_ref, o_ref, dw_ref, ds_ref, dw_acc):
        g = pl.program_id(0)
        wmat = w_ref[...]
        sc = s_ref[...]
        dw_acc[...] = jnp.zeros_like(dw_acc)

        def chunk(c, dsc):
            r0 = c * rows
            ext = _window(h_ref, c, rows, seq, 16, 0)
            sums = _pool_sums(ext, g, False)[16:, :]
            mixed = (sums / _pool_count(r0, rows, g, (rows, dg)) - ext[16:, :]).astype(BF16)
            gext = _window(g_ref, c, rows, seq, 0, 16)
            dsc = dsc + jnp.sum(gext[:rows, :] * _nn(mixed, wmat), axis=0, keepdims=True)
            dpre = (gext * sc).astype(BF16)
            dw_acc[...] += _tn(mixed, dpre[:rows, :])
            dmix = _nt(dpre, wmat)
            q = dmix / _pool_count(r0, rows + 16, g, (rows + 16, dg))
            back = _pool_sums(q, g, True)
            o_ref[pl.ds(pl.multiple_of(r0, rows), rows), :] = back[:rows, :] - dmix[:rows, :]
            return dsc

        dsc = lax.fori_loop(0, seq // rows, chunk, jnp.zeros((1, dg), F32))

        @pl.when(pl.program_id(1) == 0)
        def _():
            dw_ref[...] = dw_acc[...]
            ds_ref[...] = dsc

        @pl.when(pl.program_id(1) > 0)
        def _():
            dw_ref[...] += dw_acc[...]
            ds_ref[...] += dsc

    return pl.pallas_call(
        body,
        out_shape=(jax.ShapeDtypeStruct((b, seq, d), F32), jax.ShapeDtypeStruct((4, dg, dg), F32),
                   jax.ShapeDtypeStruct((1, d), F32)),
        grid=(4, b),
        in_specs=[pl.BlockSpec((None, seq, dg), lambda g, i: (i, 0, g)),
                  pl.BlockSpec((None, seq, dg), lambda g, i: (i, 0, g)),
                  pl.BlockSpec((None, dg, dg), lambda g, i: (g, 0, 0)),
                  pl.BlockSpec((1, dg), lambda g, i: (0, g))],
        out_specs=(pl.BlockSpec((None, seq, dg), lambda g, i: (i, 0, g)),
                   pl.BlockSpec((None, dg, dg), lambda g, i: (g, 0, 0)),
                   pl.BlockSpec((1, dg), lambda g, i: (0, g))),
        scratch_shapes=[pltpu.VMEM((dg, dg), F32)],
        compiler_params=_params(("parallel", "arbitrary")), name=name)(h, dout, pw, scale)


def _head_of(channel):
    return jnp.right_shift(channel, HEAD_DIM.bit_length() - 1)


def _ssd_consts(gw):
    q = CHUNK
    row = lax.broadcasted_iota(jnp.int32, (q, q), 0)
    col = lax.broadcasted_iota(jnp.int32, (q, q), 1)
    tril = (row >= col).astype(BF16)
    triu = (row <= col).astype(BF16)
    e = (_head_of(lax.broadcasted_iota(jnp.int32, (LANES, gw), 1))
         == lax.broadcasted_iota(jnp.int32, (LANES, gw), 0)).astype(BF16)
    et = (_head_of(lax.broadcasted_iota(jnp.int32, (gw, LANES), 0))
          == lax.broadcasted_iota(jnp.int32, (gw, LANES), 1)).astype(BF16)
    return row, col, tril, triu, e, et


def _ssd_common(dtr, dtb, alog, gw):
    q = CHUNK
    row, col, tril, triu, e, et = _ssd_consts(gw)
    dt = _softplus(dtr + dtb)
    a_row = -jnp.exp(alog)
    acum = _sel_left(tril, dt * a_row)
    ac_last = jnp.sum(jnp.where(row == q - 1, acum, 0.0), axis=0, keepdims=True)
    eac = jnp.exp(acum)
    de = jnp.exp(ac_last - acum)
    e2 = jnp.concatenate([e, e], axis=0)
    expand = _sel_right(jnp.concatenate([dt, eac, de], axis=0), e2, 2)
    dt_x, eac_x, de_x = expand[0:q], expand[q:2 * q], expand[2 * q:3 * q]
    acum_t = acum.T
    cd_col = jnp.exp(acum_t[:, q - 1:q])
    et3 = jnp.concatenate([et, et, et], axis=1)
    cdmat = _nn(et3, jnp.concatenate(_split(jnp.broadcast_to(cd_col, (LANES, D_STATE)), 3), axis=0))
    consts = dict(row=row, col=col, tril=tril, triu=triu, e=e, et=et)
    return dt, a_row, acum, acum_t, ac_last, eac, de, dt_x, eac_x, de_x, cdmat, consts


def _decay(acum, acum_t, j, row, col):
    diff = acum[:, j:j + 1] - acum_t[j:j + 1, :]
    return jnp.exp(jnp.where(row >= col, diff, -1e30))


def _ssd_fwd(xc, zx, dtb, alog, dskip, nw, d_inner, name):
    b, seq, xbc = xc.shape
    q = CHUNK
    nc = seq // q
    gw = d_inner // N_GROUPS
    nh = gw // HEAD_DIM
    xb0 = d_inner // D_STATE
    xc0 = xb0 + N_GROUPS
    dt0 = (d_inner + xbc) // LANES

    nb = max(n for n in (4, 2, 1) if b % n == 0)

    def body(x_ref, b_ref, c_ref, z_ref, dtr_ref, dtb_ref, al_ref, dsk_ref, nw_ref, y_ref, yn_ref, st_ref, s_ref):
        @pl.when(pl.program_id(2) == 0)
        def _():
            s_ref[...] = jnp.zeros_like(s_ref)

        for s in range(nb):
            one(s, x_ref.at[s], b_ref.at[s], c_ref.at[s], z_ref.at[s], dtr_ref.at[s], dtb_ref, al_ref, dsk_ref, nw_ref,
                y_ref.at[s], yn_ref.at[s], st_ref.at[s], s_ref.at[s])

    def one(s, x_ref, b_ref, c_ref, z_ref, dtr_ref, dtb_ref, al_ref, dsk_ref, nw_ref, y_ref, yn_ref, st_ref, s_ref):
        prev = s_ref[...]
        st_ref[...] = prev
        x = x_ref[...]
        bm = b_ref[...].astype(BF16)
        cm = c_ref[...].astype(BF16)
        (dt, a_row, acum, acum_t, ac_last, eac, de, dt_x, eac_x, de_x, cdmat, k) = _ssd_common(
            dtr_ref[...], dtb_ref[0:1, :], al_ref[0:1, :], gw)
        xdt = x * dt_x
        xdt_b = xdt.astype(BF16)
        cb = _nt(cm, bm)
        half = _head_of(lax.broadcasted_iota(jnp.int32, (q, LANES), 1))
        pairs = []
        for j in range(nh):
            pc = (j // 2) * LANES
            m = (cb * _decay(acum, acum_t, j, k["row"], k["col"])).astype(BF16)
            yj = jnp.where(half == j % 2, _nn(m, xdt_b[:, pc:pc + LANES]), 0.0)
            if j % 2 == 0:
                pairs.append(yj)
            else:
                pairs[-1] = pairs[-1] + yj
        prev_b = prev.astype(BF16)
        y = dsk_ref[0:1, :] * x + jnp.concatenate(pairs, axis=1) + eac_x * _nt(cm, prev_b)
        s_ref[...] = cdmat * prev + _tn((xdt * de_x).astype(BF16), bm)
        y_ref[...] = y
        z = z_ref[...]
        yg = y * (z * _sigmoid(z))
        r = lax.rsqrt(jnp.mean(yg * yg, axis=-1, keepdims=True) + EPS)
        yn_ref[...] = ((yg * r) * nw_ref[0:1, :]).astype(BF16)

    par = lambda w: pl.BlockSpec((None, 8, w), lambda i, g, c: (g, 0, 0))
    return pl.pallas_call(
        body,
        out_shape=(jax.ShapeDtypeStruct((b, seq, d_inner), F32), jax.ShapeDtypeStruct((b, seq, d_inner), BF16),
                   jax.ShapeDtypeStruct((b, nc, N_GROUPS, gw, D_STATE), F32)),
        grid=(b // nb, N_GROUPS, nc),
        in_specs=[pl.BlockSpec((nb, q, gw), lambda i, g, c: (i, c, g)),
                  pl.BlockSpec((nb, q, D_STATE), lambda i, g, c: (i, c, xb0 + g)),
                  pl.BlockSpec((nb, q, D_STATE), lambda i, g, c: (i, c, xc0 + g)),
                  pl.BlockSpec((nb, q, gw), lambda i, g, c: (i, c, g)),
                  pl.BlockSpec((nb, q, LANES), lambda i, g, c: (i, c, dt0 + g)),
                  par(LANES), par(LANES), par(gw), par(gw)],
        out_specs=(pl.BlockSpec((nb, q, gw), lambda i, g, c: (i, c, g)),
                   pl.BlockSpec((nb, q, gw), lambda i, g, c: (i, c, g)),
                   pl.BlockSpec((nb, None, None, gw, D_STATE), lambda i, g, c: (i, c, g, 0, 0))),
        scratch_shapes=[pltpu.VMEM((nb, gw, D_STATE), F32)],
        compiler_params=_params(("parallel", "parallel", "arbitrary")), name=name,
    )(xc, xc, xc, zx, zx, dtb, alog, dskip, nw)


def _ssd_bwd(xc, zx, y, dyn, st, dtb, alog, dskip, nw, d_inner, name):
    b, seq, xbc = xc.shape
    q = CHUNK
    nc = seq // q
    gw = d_inner // N_GROUPS
    nh = gw // HEAD_DIM
    xb0 = d_inner // D_STATE
    xc0 = xb0 + N_GROUPS
    dt0 = (d_inner + xbc) // LANES

    nb = max(n for n in (4, 2, 1) if b % n == 0)

    def body(x_ref, b_ref, c_ref, z_ref, dtr_ref, y_ref, g_ref, st_ref, dtb_ref, al_ref, dsk_ref, nw_ref,
             dz_ref, dx_ref, db_ref, dc_ref, ddt_ref, dnw_ref, dd_ref, dal_ref, dbias_ref,
             ds_ref, colbuf, rowbuf):
        first = jnp.logical_and(pl.program_id(1) == 0, pl.program_id(2) == 0)

        @pl.when(pl.program_id(2) == 0)
        def _():
            ds_ref[...] = jnp.zeros_like(ds_ref)

        sums = [one(x_ref.at[s], b_ref.at[s], c_ref.at[s], z_ref.at[s], dtr_ref.at[s], y_ref.at[s], g_ref.at[s],
                    st_ref.at[s], dtb_ref, al_ref, dsk_ref, nw_ref, dz_ref.at[s], dx_ref.at[s], db_ref.at[s],
                    dc_ref.at[s], ddt_ref.at[s], ds_ref.at[s], colbuf.at[s], rowbuf.at[s]) for s in range(nb)]
        dnw, dd, dal, dbias = [functools.reduce(lambda p, r: p + r, [sm[i] for sm in sums]) for i in range(4)]

        @pl.when(first)
        def _():
            dnw_ref[...] = jnp.broadcast_to(dnw, (8, gw))
            dd_ref[...] = dd
            dal_ref[...] = jnp.broadcast_to(dal, (8, LANES))
            dbias_ref[...] = jnp.broadcast_to(dbias, (8, LANES))

        @pl.when(jnp.logical_not(first))
        def _():
            dnw_ref[...] += jnp.broadcast_to(dnw, (8, gw))
            dd_ref[...] += dd
            dal_ref[...] += jnp.broadcast_to(dal, (8, LANES))
            dbias_ref[...] += jnp.broadcast_to(dbias, (8, LANES))

    def one(x_ref, b_ref, c_ref, z_ref, dtr_ref, y_ref, g_ref, st_ref, dtb_ref, al_ref, dsk_ref, nw_ref,
            dz_ref, dx_ref, db_ref, dc_ref, ddt_ref, ds_ref, colbuf, rowbuf):
        x = x_ref[...]
        bm = b_ref[...].astype(BF16)
        cm = c_ref[...].astype(BF16)
        z = z_ref[...]
        y = y_ref[...]
        prev = st_ref[...]
        dtr = dtr_ref[...] + dtb_ref[0:1, :]
        (dt, a_row, acum, acum_t, ac_last, eac, de, dt_x, eac_x, de_x, cdmat, k) = _ssd_common(
            dtr_ref[...], dtb_ref[0:1, :], al_ref[0:1, :], gw)
        row, col = k["row"], k["col"]
        et2 = jnp.concatenate([k["et"], k["et"]], axis=0)

        sz = _sigmoid(z)
        silu_z = z * sz
        yg = y * silu_z
        r = lax.rsqrt(jnp.mean(yg * yg, axis=-1, keepdims=True) + EPS)
        xh = yg * r
        dyn = g_ref[...]
        gh = dyn * nw_ref[0:1, :]
        dyg = r * (gh - xh * jnp.mean(gh * xh, axis=-1, keepdims=True))
        dnw = jnp.sum(dyn * xh, axis=0, keepdims=True)
        g = dyg * silu_z
        dz_ref[...] = (dyg * y * (sz * (1.0 + z * (1.0 - sz)))).astype(BF16)
        dd = _sel_right(jnp.broadcast_to(jnp.sum(g * x, axis=0, keepdims=True), (8, gw)), et2, 2)

        xdt = x * dt_x
        xdt_b = xdt.astype(BF16)
        g_b = g.astype(BF16)
        prev_b = prev.astype(BF16)
        cb = _nt(cm, bm)

        cp = _nt(cm, prev_b)
        ge = g * eac_x
        dac = _sel_right(ge * cp, et2, 2)
        ge_b = ge.astype(BF16)
        dcm = _nn(ge_b, prev_b)
        dprev = _tn(ge_b, cm)

        colbuf[...] = jnp.zeros_like(colbuf)
        rowbuf[...] = jnp.zeros_like(rowbuf)
        dcb = jnp.zeros((q, q), F32)
        half = _head_of(lax.broadcasted_iota(jnp.int32, (q, LANES), 1))
        pairs = []
        for j in range(nh):
            pc = (j // 2) * LANES
            dec = _decay(acum, acum_t, j, row, col)
            m = cb * dec
            gj = jnp.where(half == j % 2, g[:, pc:pc + LANES], 0.0).astype(BF16)
            dm = _nt(gj, xdt_b[:, pc:pc + LANES])
            w = dm * m
            colbuf[:, j:j + 1] = jnp.sum(w, axis=1, keepdims=True)
            rowbuf[j:j + 1, :] = jnp.sum(w, axis=0, keepdims=True)
            dcb = dcb + dm * dec
            dj = jnp.where(half == j % 2, _tn(m.astype(BF16), g_b[:, pc:pc + LANES]), 0.0)
            if j % 2 == 0:
                pairs.append(dj)
            else:
                pairs[-1] = pairs[-1] + dj
        dxdt = jnp.concatenate(pairs, axis=1)
        dcb_b = dcb.astype(BF16)
        dcm = dcm + _nn(dcb_b, bm)
        dbm = _tn(dcb_b, cm)

        ds = ds_ref[...]
        ds_b = ds.astype(BF16)
        u = _nt(bm, ds_b)
        dxdt = dxdt + u * de_x
        dde = _sel_right(u * xdt, et2, 2)
        dbm = dbm + _nn((xdt * de_x).astype(BF16), ds_b)
        pm = jnp.concatenate(_split(ds * prev, 2), axis=1)
        t2 = _tn(pm, k["et"])
        dcd_row = jnp.sum(t2[0:D_STATE] + t2[D_STATE:2 * D_STATE], axis=0, keepdims=True)
        last = dcd_row * jnp.exp(ac_last) + jnp.sum(dde * de, axis=0, keepdims=True)
        dac = dac + colbuf[...] - rowbuf[...].T - dde * de + jnp.where(row == q - 1, last, 0.0)
        ds_ref[...] = cdmat * ds + dprev

        dadt = _sel_left(k["triu"], dac)
        ddt = _sel_right(dxdt * x, et2, 2) + dadt * a_row
        dal = jnp.sum(dadt * dt, axis=0, keepdims=True) * a_row
        lane = lax.broadcasted_iota(jnp.int32, (q, LANES), 1)
        ddtr = jnp.where(lane < nh, ddt * _sigmoid(dtr), 0.0)
        ddt_ref[...] = ddtr.astype(BF16)
        dbias = jnp.sum(ddtr, axis=0, keepdims=True)
        dx_ref[...] = dxdt * dt_x + dsk_ref[0:1, :] * g
        db_ref[...] = dbm
        dc_ref[...] = dcm
        return dnw, dd, dal, dbias

    rc = lambda c: nc - 1 - c
    par = lambda w: pl.BlockSpec((None, 8, w), lambda g, i, c: (g, 0, 0))
    blk = lambda w: pl.BlockSpec((nb, q, w), lambda g, i, c: (i, rc(c), g))
    return pl.pallas_call(
        body,
        out_shape=(jax.ShapeDtypeStruct((b, seq, d_inner), BF16),
                   jax.ShapeDtypeStruct((b, seq, d_inner), F32),
                   jax.ShapeDtypeStruct((b, seq, N_GROUPS * D_STATE), F32),
                   jax.ShapeDtypeStruct((b, seq, N_GROUPS * D_STATE), F32),
                   jax.ShapeDtypeStruct((b, seq, N_GROUPS * LANES), BF16),
                   jax.ShapeDtypeStruct((N_GROUPS, 8, gw), F32),
                   jax.ShapeDtypeStruct((N_GROUPS, 8, LANES), F32),
                   jax.ShapeDtypeStruct((N_GROUPS, 8, LANES), F32),
                   jax.ShapeDtypeStruct((N_GROUPS, 8, LANES), F32)),
        grid=(N_GROUPS, b // nb, nc),
        in_specs=[blk(gw),
                  pl.BlockSpec((nb, q, D_STATE), lambda g, i, c: (i, rc(c), xb0 + g)),
                  pl.BlockSpec((nb, q, D_STATE), lambda g, i, c: (i, rc(c), xc0 + g)),
                  blk(gw),
                  pl.BlockSpec((nb, q, LANES), lambda g, i, c: (i, rc(c), dt0 + g)),
                  blk(gw), blk(gw),
                  pl.BlockSpec((nb, None, None, gw, D_STATE), lambda g, i, c: (i, rc(c), g, 0, 0)),
                  par(LANES), par(LANES), par(gw), par(gw)],
        out_specs=(blk(gw), blk(gw), blk(D_STATE), blk(D_STATE), blk(LANES),
                   par(gw), par(LANES), par(LANES), par(LANES)),
        scratch_shapes=[pltpu.VMEM((nb, gw, D_STATE), F32), pltpu.VMEM((nb, q, LANES), F32),
                        pltpu.VMEM((nb, LANES, q), F32)],
        compiler_params=_params(("parallel", "arbitrary", "arbitrary")), name=name,
    )(xc, xc, xc, zx, zx, y, dyn, st, dtb, alog, dskip, nw)


def _adamw(w, g, m, v, name):
    rows, cols = w.shape
    tr = rows
    for cand in (512, 256, 128, 64, 32, 16, 8):
        if rows % cand == 0 and cand * cols * 4 <= 2 * 1024 * 1024:
            tr = cand
            break
    c1 = 1.0 - ADAM_B1 ** ADAM_STEP
    c2 = 1.0 - ADAM_B2 ** ADAM_STEP

    def body(w_ref, g_ref, m_ref, v_ref, d_ref, mo_ref, vo_ref):
        gv = g_ref[...]
        mn = ADAM_B1 * m_ref[...] + (1.0 - ADAM_B1) * gv
        vn = ADAM_B2 * v_ref[...] + (1.0 - ADAM_B2) * (gv * gv)
        mo_ref[...] = mn
        vo_ref[...] = vn
        d_ref[...] = -ADAM_LR * ((mn / c1) / (jnp.sqrt(vn / c2) + ADAM_EPS) + ADAM_WD * w_ref[...])

    spec = pl.BlockSpec((tr, cols), lambda i: (i, 0))
    shp = jax.ShapeDtypeStruct((rows, cols), F32)
    return pl.pallas_call(body, out_shape=(shp, shp, shp), grid=(rows // tr,), in_specs=[spec] * 4,
                          out_specs=(spec,) * 3, compiler_params=_params(("parallel",)), name=name)(w, g, m, v)


def _pick_rows(rows, row_bytes, limit=1 << 20):
    for cand in (2048, 1024, 512, 256, 128, 64, 32, 16):
        if rows % cand == 0 and cand * row_bytes <= limit:
            return cand
    return rows


def _as3d(a, lead):
    return a.reshape(a.shape[:lead] + (-1, a.shape[-1]))


def _pair_sum(g, got, core, name):
    h = got.shape[0]
    g3, got3 = _as3d(g, 1), _as3d(got, 1)
    _, rows, cols = got3.shape
    tr = _pick_rows(rows, cols * 4)

    def body(c_ref, g_ref, r_ref, o_ref):
        o_ref[...] = (g_ref[...] + r_ref[...]).astype(BF16)

    out = pl.pallas_call(
        body, out_shape=jax.ShapeDtypeStruct(got3.shape, BF16),
        grid_spec=pltpu.PrefetchScalarGridSpec(
            num_scalar_prefetch=1, grid=(h, rows // tr),
            in_specs=[pl.BlockSpec((None, tr, cols), lambda l, i, c_ref: (c_ref[0] * h + l, i, 0)),
                      pl.BlockSpec((None, tr, cols), lambda l, i, c_ref: (l, i, 0))],
            out_specs=pl.BlockSpec((None, tr, cols), lambda l, i, c_ref: (l, i, 0))),
        compiler_params=_params(("parallel", "parallel")), name=name)(core, g3, got3)
    return out.reshape(got.shape)


def _sum4(q, core, name):
    q4 = _as3d(q, 2)
    _, h, rows, cols = q4.shape
    tr = _pick_rows(rows, cols * 4)

    def body(c_ref, q0, q1, q2, q3, o_ref):
        o_ref[...] = ((q0[...].astype(F32) + q1[...].astype(F32)) + q2[...].astype(F32)) + q3[...].astype(F32)

    out = pl.pallas_call(
        body, out_shape=jax.ShapeDtypeStruct((2 * h, rows, cols), F32),
        grid_spec=pltpu.PrefetchScalarGridSpec(
            num_scalar_prefetch=1, grid=(h, rows // tr),
            in_specs=[pl.BlockSpec((None, None, tr, cols), lambda l, i, c_ref, k=k: (k, l, i, 0))
                      for k in range(N_CHIPS)],
            out_specs=pl.BlockSpec((None, tr, cols), lambda l, i, c_ref: (c_ref[0] * h + l, i, 0))),
        compiler_params=_params(("parallel", "parallel")), name=name)(core, q4, q4, q4, q4)
    return out.reshape((2 * h,) + q.shape[2:])


def _coords():
    return lax.axis_index("x"), lax.axis_index("y"), lax.axis_index("c")


def _other_chips(x, y):
    return [(1 - x, y), (x, 1 - y), (1 - x, 1 - y)]


def _allgather_halves(src, name):
    rows, cols = src.shape

    def body(x_ref, o_ref, send, recv, local):
        x, y, c = _coords()
        sib = (x, y, 1 - c)
        chips = _other_chips(x, y)

        def slot(h, cx, cy):
            return o_ref.at[h, 2 * cx + cy]

        def copy(kk, dst, to, src_ref):
            return pltpu.make_async_remote_copy(src_ref=src_ref, dst_ref=dst, send_sem=send.at[kk],
                                                recv_sem=recv.at[kk], device_id=to, device_id_type=MESH)

        mine = pltpu.make_async_copy(x_ref, slot(c, x, y), local)
        mine.start()
        first = [copy(0, slot(c, x, y), sib, x_ref)]
        first += [copy(1 + j, slot(c, x, y), (*chip, c), x_ref) for j, chip in enumerate(chips)]
        for cp in first:
            cp.start()
        passed = [copy(4 + j, slot(c, *chip), sib, slot(c, *chip)) for j, chip in enumerate(chips)]
        for j, chip in enumerate(chips):
            copy(1 + j, slot(c, *chip), (x, y, c), x_ref).wait_recv()
            passed[j].start()
        copy(0, slot(1 - c, x, y), (x, y, c), x_ref).wait_recv()
        for j, chip in enumerate(chips):
            copy(4 + j, slot(1 - c, *chip), (x, y, c), x_ref).wait_recv()
        for cp in first + passed:
            cp.wait_send()
        mine.wait()

    return pl.pallas_call(
        body, out_shape=jax.ShapeDtypeStruct((2, N_CHIPS, rows, cols), src.dtype),
        in_specs=[ANY], out_specs=ANY,
        scratch_shapes=[pltpu.SemaphoreType.DMA((7,)), pltpu.SemaphoreType.DMA((7,)), pltpu.SemaphoreType.DMA],
        name=name)(src)


MIXW = (("ssd_w_in", None), ("ssd_w_out", 0), ("pool_w", 1))
FFNW = (("ffn_w_up", 1), ("ffn_w_down", 0))


def _chip_window(axis, ref, layers, k):
    if axis is None:
        return ref.at[layers, k]
    n = ref.shape[1 + axis] // N_CHIPS
    sl = pl.ds(pl.multiple_of(k * n, LANES if 1 + axis == len(ref.shape) - 1 else 8), n)
    idx = [layers] + [slice(None)] * (len(ref.shape) - 1)
    idx[1 + axis] = sl
    return ref.at[tuple(idx)]


def _full_shape(axis, shard_shape):
    if axis is None:
        return (shard_shape[0], N_CHIPS) + tuple(shard_shape[1:])
    full = list(shard_shape)
    full[1 + axis] *= N_CHIPS
    return tuple(full)


HBM_SPEC = pl.BlockSpec(memory_space=pltpu.HBM)
SEM_SPEC = pl.BlockSpec(memory_space=pltpu.SEMAPHORE)


def _dma_sems(count):
    return pltpu.SemaphoreType.DMA((max(count, 1),))


def _wait_for(copy, kind):
    if kind == "recv":
        copy.wait_recv()
    elif kind == "send":
        copy.wait_send()
    else:
        copy.wait()


def _comm_fused(stages, counts, srcs, lands, name, inplace=False):
    ns, nl, k = len(srcs), len(lands), len(stages)

    def body(*refs):
        src_refs = refs[:ns]
        land_refs = refs[ns + (nl if inplace else 0):ns + (nl if inplace else 0) + nl]
        sem_refs = refs[len(refs) - 3 * k:]
        for s, stage_fn in enumerate(stages):
            starts, waits = stage_fn(src_refs, land_refs, tuple(sem_refs[3 * s:3 * s + 3]))
            for cp in starts:
                cp.start()
            for cp, kind in waits:
                _wait_for(cp, kind)

    scratch = []
    for cnt in counts:
        scratch += [_dma_sems(c) for c in cnt]
    outs = pl.pallas_call(
        body, out_shape=tuple(jax.ShapeDtypeStruct(a.shape, a.dtype) for a in lands),
        in_specs=[ANY] * (ns + (nl if inplace else 0)), out_specs=(ANY,) * nl,
        input_output_aliases={ns + i: i for i in range(nl)} if inplace else {},
        scratch_shapes=scratch, name=name)(*srcs, *(lands if inplace else ()))
    return list(outs)


class _SplitComm:
    def __init__(self, stages, counts, srcs, lands, name):
        self.stages, self.counts, self.name = stages, counts, name
        self.ns = len(srcs)
        self.data = [pltpu.with_memory_space_constraint(a, pltpu.HBM) for a in list(srcs) + list(lands)]
        self.sems = None
        self.step = 0

    def advance(self, after=None):
        i, k, nd, ns = self.step, len(self.stages), len(self.data), self.ns
        first, last = i == 0, i == k
        stages = self.stages
        after = list(after) if isinstance(after, (list, tuple)) else [after]

        def body(*refs):
            data = refs[:nd]
            pos = nd
            if not first:
                old = tuple(refs[pos:pos + 3])
                pos += 3 + len(after)
            if not last:
                new = tuple(refs[pos:pos + 3])
            if not first:
                for cp, kind in stages[i - 1](data[:ns], data[ns:], old)[1]:
                    _wait_for(cp, kind)
            if not last:
                for cp in stages[i](data[:ns], data[ns:], new)[0]:
                    cp.start()
                refs[len(refs) - 1][...] = jnp.zeros((8, LANES), F32)

        args = list(self.data)
        in_specs = [HBM_SPEC] * nd
        if not first:
            args += list(self.sems) + after
            in_specs += [SEM_SPEC] * 3 + [ANY] * len(after)
        out_shape, out_specs = [], []
        if not last:
            out_shape += [_dma_sems(c) for c in self.counts[i]]
            out_specs += [SEM_SPEC] * 3
        out_shape += [pltpu.HBM(a.shape, a.dtype) for a in self.data]
        out_specs += [HBM_SPEC] * nd
        if not last:
            out_shape.append(jax.ShapeDtypeStruct((8, LANES), F32))
            out_specs.append(pl.BlockSpec(memory_space=pltpu.VMEM))
        off = 0 if last else 3
        outs = pl.pallas_call(
            body, out_shape=tuple(out_shape), in_specs=in_specs, out_specs=tuple(out_specs),
            input_output_aliases={d: off + d for d in range(nd)},
            compiler_params=pltpu.CompilerParams(has_side_effects=pltpu.SideEffectType.DATAFLOW_SIDE_EFFECTING),
            name=f"{self.name}_{i}")(*args)
        self.sems = None if last else outs[:3]
        self.data = list(outs[off:off + nd])
        self.step += 1
        return None if last else outs[len(outs) - 1]

    def lands(self):
        return self.data[self.ns:]


def _gather_stages(spec):
    n = len(spec)

    def parts(srcs, lands):
        x, y, c = _coords()
        out = []
        for w, (_, axis) in enumerate(spec):
            h = srcs[w].shape[0] // 2
            mine, theirs = pl.ds(c * h, h), pl.ds((1 - c) * h, h)
            out.append((srcs[w].at[mine], lambda layers, k, w=w, axis=axis: _chip_window(axis, lands[w], layers, k),
                        mine, theirs))
        return x, y, c, 2 * x + y, (x, y, 1 - c), _other_chips(x, y), out

    def remote(src, dst, send, recv, idx, to):
        return pltpu.make_async_remote_copy(src_ref=src, dst_ref=dst, send_sem=send.at[idx], recv_sem=recv.at[idx],
                                            device_id=to, device_id_type=MESH)

    def stage0(srcs, lands, sems):
        send, recv, local = sems
        x, y, c, me, sib, chips, ps = parts(srcs, lands)
        starts, waits = [], []
        for w, (src, dst, mine, theirs) in enumerate(ps):
            lc = pltpu.make_async_copy(src, dst(mine, me), local.at[w])
            first = [remote(src, dst(mine, me), send, recv, 4 * w, sib)]
            first += [remote(src, dst(mine, me), send, recv, 4 * w + 1 + j, (cx, cy, c)) for j, (cx, cy) in enumerate(chips)]
            starts += [lc] + first
            waits.append((remote(src, dst(theirs, me), send, recv, 4 * w, (x, y, c)), "recv"))
            waits += [(remote(src, dst(mine, 2 * cx + cy), send, recv, 4 * w + 1 + j, (x, y, c)), "recv")
                      for j, (cx, cy) in enumerate(chips)]
            waits += [(cp, "send") for cp in first] + [(lc, "local")]
        return starts, waits

    def stage1(srcs, lands, sems):
        send, recv, _ = sems
        x, y, c, me, sib, chips, ps = parts(srcs, lands)
        starts, waits = [], []
        for w, (src, dst, mine, theirs) in enumerate(ps):
            for j, (cx, cy) in enumerate(chips):
                blk = dst(mine, 2 * cx + cy)
                fwd = remote(blk, blk, send, recv, 3 * w + j, sib)
                starts.append(fwd)
                waits.append((remote(src, dst(theirs, 2 * cx + cy), send, recv, 3 * w + j, (x, y, c)), "recv"))
                waits.append((fwd, "send"))
        return starts, waits

    return [stage0, stage1], [(4 * n, 4 * n, n), (3 * n, 3 * n, 0)]


def _swap_stages(spec):
    n = len(spec)

    def stage(srcs, lands, sems):
        send, recv, _ = sems
        x, y, c = _coords()
        starts, waits = [], []
        for w in range(n):
            h = srcs[w].shape[0] // 2
            cp = pltpu.make_async_remote_copy(src_ref=srcs[w].at[pl.ds((1 - c) * h, h)], dst_ref=lands[w],
                                              send_sem=send.at[w], recv_sem=recv.at[w],
                                              device_id=(x, y, 1 - c), device_id_type=MESH)
            starts.append(cp)
            waits += [(cp, "recv"), (cp, "send")]
        return starts, waits

    return [stage], [(n, n, 0)]


def _scatter_stages(spec):
    n = len(spec)

    def stage(srcs, lands, sems):
        send, recv, local = sems
        x, y, c = _coords()
        me = 2 * x + y
        starts, waits = [], []
        for w, (_, axis) in enumerate(spec):
            layers = pl.ds(0, srcs[w].shape[0])
            own = _chip_window(axis, srcs[w], layers, me)
            lc = pltpu.make_async_copy(own, lands[w].at[me], local.at[w])
            starts.append(lc)
            for j, (cx, cy) in enumerate(_other_chips(x, y)):
                cp = pltpu.make_async_remote_copy(src_ref=_chip_window(axis, srcs[w], layers, 2 * cx + cy),
                                                  dst_ref=lands[w].at[me], send_sem=send.at[3 * w + j],
                                                  recv_sem=recv.at[3 * w + j], device_id=(cx, cy, c), device_id_type=MESH)
                starts.append(cp)
                waits.append((pltpu.make_async_remote_copy(
                    src_ref=own, dst_ref=lands[w].at[2 * cx + cy], send_sem=send.at[3 * w + j], recv_sem=recv.at[3 * w + j],
                    device_id=(x, y, c), device_id_type=MESH), "recv"))
                waits.append((cp, "send"))
            waits.append((lc, "local"))
        return starts, waits

    return [stage], [(3 * n, 3 * n, n)]


def _share_stages(spec):
    n = len(spec)

    def stage(srcs, lands, sems):
        send, recv, _ = sems
        x, y, c = _coords()
        starts, waits = [], []
        for w in range(n):
            h = lands[w].shape[0] // 2
            mine, theirs = lands[w].at[pl.ds(c * h, h)], lands[w].at[pl.ds((1 - c) * h, h)]
            cp = pltpu.make_async_remote_copy(src_ref=mine, dst_ref=mine, send_sem=send.at[w], recv_sem=recv.at[w],
                                              device_id=(x, y, 1 - c), device_id_type=MESH)
            starts.append(cp)
            waits.append((pltpu.make_async_remote_copy(src_ref=theirs, dst_ref=theirs, send_sem=send.at[w],
                                                       recv_sem=recv.at[w], device_id=(x, y, c), device_id_type=MESH),
                          "recv"))
            waits.append((cp, "send"))
        return starts, waits

    return [stage], [(n, n, 0)]


def _shard_of(p, axis):
    if axis is None:
        return (p.shape[0],) + tuple(p.shape[2:])
    s = list(p.shape)
    s[1 + axis] //= N_CHIPS
    return tuple(s)


def _gather8_stages():
    def stage(srcs, lands, sems):
        send, recv, local = sems
        x, y, c = _coords()
        me = 4 * x + 2 * y + c
        lc = pltpu.make_async_copy(srcs[0], lands[0].at[me], local.at[0])
        starts, waits = [lc], []
        for kk in range(1, 8):
            to = (1 - x if kk & 4 else x, 1 - y if kk & 2 else y, 1 - c if kk & 1 else c)
            cp = pltpu.make_async_remote_copy(src_ref=srcs[0], dst_ref=lands[0].at[me], send_sem=send.at[kk - 1],
                                              recv_sem=recv.at[kk - 1], device_id=to, device_id_type=MESH)
            starts.append(cp)
            waits.append((pltpu.make_async_remote_copy(
                src_ref=srcs[0], dst_ref=lands[0].at[4 * to[0] + 2 * to[1] + to[2]], send_sem=send.at[kk - 1],
                recv_sem=recv.at[kk - 1], device_id=(x, y, c), device_id_type=MESH), "recv"))
            waits.append((cp, "send"))
        waits.append((lc, "local"))
        return starts, waits

    return [stage], [(7, 7, 1)]


def _sum8(buf, name):
    _, rows, cols = buf.shape
    tr = _pick_rows(rows, cols * 4)

    def body(*refs):
        acc = refs[0][...]
        for r in refs[1:8]:
            acc = acc + r[...]
        refs[8][...] = acc

    return pl.pallas_call(
        body, out_shape=jax.ShapeDtypeStruct((rows, cols), F32), grid=(rows // tr,),
        in_specs=[pl.BlockSpec((None, tr, cols), lambda i, k=k: (k, i, 0)) for k in range(8)],
        out_specs=pl.BlockSpec((tr, cols), lambda i: (i, 0)),
        compiler_params=_params(("parallel",)), name=name)(*([buf] * 8))


def _reduce_begin(spec, gs, core, tag, riders=()):
    stages, counts = _swap_stages(spec)
    got = _comm_fused(stages, counts, list(gs) + list(riders),
                      [jax.ShapeDtypeStruct((g.shape[0] // 2,) + g.shape[1:], g.dtype) for g in gs], "swap_" + tag)
    pair = [_pair_sum(a, r, core, "pair_sum_" + n) for a, r, (n, _) in zip(gs, got, spec)]
    stages, counts = _scatter_stages(spec)
    lands = [lax.empty((N_CHIPS,) + _shard_of(p, axis), p.dtype) for p, (_, axis) in zip(pair, spec)]
    comm = _SplitComm(stages, counts, pair, lands, "scatter_" + tag)
    return comm, comm.advance()


def _reduce_finish(spec, comm, core, tag, after):
    comm.advance(after=after)
    halves = [_sum4(q, core, "sum4_" + n) for q, (n, _) in zip(comm.lands(), spec)]
    stages, counts = _share_stages(spec)
    return _comm_fused(stages, counts, [], halves, "share_" + tag, inplace=True)


SMALL = (("ssd_conv_w", 2), ("pool_scale", 1), ("ffn_conv_w", 2))
REPL = ("ssd_conv_b", "ssd_dt_bias", "ssd_a_log", "ssd_d", "ssd_norm_w", "ffn_conv_b",
        "norm_mix_pre", "norm_mix_post", "norm_ffn_pre", "norm_ffn_post")
WEIGHTS = ("ssd_w_in", "ssd_conv_w", "ssd_conv_b", "ssd_dt_bias", "ssd_a_log", "ssd_d", "ssd_norm_w", "ssd_w_out",
           "pool_w", "pool_scale", "ffn_w_up", "ffn_conv_w", "ffn_conv_b", "ffn_w_down", "norm_mix_pre",
           "norm_mix_post", "norm_ffn_pre", "norm_ffn_post")


def _flat_rows(n):
    unit = 2 * 16 * FLAT_COLS
    return 2 * 16 * ((n + unit - 1) // unit)


def _flatten_shards(arrs, dtype):
    flat = jnp.concatenate([a.astype(dtype).reshape(-1) for a in arrs])
    rows = _flat_rows(flat.shape[0])
    flat = jnp.pad(flat, (0, rows * FLAT_COLS - flat.shape[0]))
    return flat.reshape(2, rows // 2, FLAT_COLS)


def _unflatten_full(gathered, shard_shapes, axes):
    per_chip = jnp.swapaxes(gathered, 0, 1).reshape(N_CHIPS, -1)
    out, off = [], 0
    for shp, ax in zip(shard_shapes, axes):
        n = math.prod(shp)
        pieces = [per_chip[k, off:off + n].reshape(shp) for k in range(N_CHIPS)]
        out.append(jnp.concatenate(pieces, axis=ax))
        off += n
    return out


def kernel(x, ssd_w_in, ssd_conv_w, ssd_conv_b, ssd_dt_bias, ssd_a_log, ssd_d, ssd_norm_w, ssd_w_out, pool_w, pool_scale, ffn_w_up, ffn_conv_w, ffn_conv_b, ffn_w_down, norm_mix_pre, norm_mix_post, norm_ffn_pre, norm_ffn_post, loss_target, m_ssd_w_in, m_ssd_conv_w, m_ssd_conv_b, m_ssd_dt_bias, m_ssd_a_log, m_ssd_d, m_ssd_norm_w, m_ssd_w_out, m_pool_w, m_pool_scale, m_ffn_w_up, m_ffn_conv_w, m_ffn_conv_b, m_ffn_w_down, m_norm_mix_pre, m_norm_mix_post, m_norm_ffn_pre, m_norm_ffn_post, v_ssd_w_in, v_ssd_conv_w, v_ssd_conv_b, v_ssd_dt_bias, v_ssd_a_log, v_ssd_d, v_ssd_norm_w, v_ssd_w_out, v_pool_w, v_pool_scale, v_ffn_w_up, v_ffn_conv_w, v_ffn_conv_b, v_ffn_w_down, v_norm_mix_pre, v_norm_mix_post, v_norm_ffn_pre, v_norm_ffn_post):
    wts = dict(ssd_w_in=ssd_w_in, ssd_conv_w=ssd_conv_w, ssd_conv_b=ssd_conv_b, ssd_dt_bias=ssd_dt_bias,
               ssd_a_log=ssd_a_log, ssd_d=ssd_d, ssd_norm_w=ssd_norm_w, ssd_w_out=ssd_w_out, pool_w=pool_w,
               pool_scale=pool_scale, ffn_w_up=ffn_w_up, ffn_conv_w=ffn_conv_w, ffn_conv_b=ffn_conv_b,
               ffn_w_down=ffn_w_down, norm_mix_pre=norm_mix_pre, norm_mix_post=norm_mix_post,
               norm_ffn_pre=norm_ffn_pre, norm_ffn_post=norm_ffn_post)
    mom = dict(ssd_w_in=m_ssd_w_in, ssd_conv_w=m_ssd_conv_w, ssd_conv_b=m_ssd_conv_b, ssd_dt_bias=m_ssd_dt_bias,
               ssd_a_log=m_ssd_a_log, ssd_d=m_ssd_d, ssd_norm_w=m_ssd_norm_w, ssd_w_out=m_ssd_w_out, pool_w=m_pool_w,
               pool_scale=m_pool_scale, ffn_w_up=m_ffn_w_up, ffn_conv_w=m_ffn_conv_w, ffn_conv_b=m_ffn_conv_b,
               ffn_w_down=m_ffn_w_down, norm_mix_pre=m_norm_mix_pre, norm_mix_post=m_norm_mix_post,
               norm_ffn_pre=m_norm_ffn_pre, norm_ffn_post=m_norm_ffn_post)
    var = dict(ssd_w_in=v_ssd_w_in, ssd_conv_w=v_ssd_conv_w, ssd_conv_b=v_ssd_conv_b, ssd_dt_bias=v_ssd_dt_bias,
               ssd_a_log=v_ssd_a_log, ssd_d=v_ssd_d, ssd_norm_w=v_ssd_norm_w, ssd_w_out=v_ssd_w_out, pool_w=v_pool_w,
               pool_scale=v_pool_scale, ffn_w_up=v_ffn_w_up, ffn_conv_w=v_ffn_conv_w, ffn_conv_b=v_ffn_conv_b,
               ffn_w_down=v_ffn_w_down, norm_mix_pre=v_norm_mix_pre, norm_mix_post=v_norm_mix_post,
               norm_ffn_pre=v_norm_ffn_pre, norm_ffn_post=v_norm_ffn_post)

    bl, seq, d = x.shape
    t = bl * seq
    depth = norm_mix_pre.shape[0]
    n_ssd = ssd_w_out.shape[0]
    d_inner = ssd_w_out.shape[1] * N_CHIPS
    nheads = d_inner // HEAD_DIM
    hpg = nheads // N_GROUPS
    gw = d_inner // N_GROUPS
    xbc = ssd_conv_w.shape[2] * N_CHIPS
    f2 = ffn_w_up.shape[2] * N_CHIPS
    ff = f2 // 2
    dg = d // 4
    cy = lax.axis_index("c")
    chip = 2 * lax.axis_index("x") + lax.axis_index("y")

    small_shapes = [wts[n].shape for n, _ in SMALL]
    small_axes = [a for _, a in SMALL]
    small_flat = _flatten_shards([wts[n] for n, _ in SMALL], F32)
    small_half = lax.dynamic_index_in_dim(small_flat, cy, 0, keepdims=False)
    small_all = _allgather_halves(small_half, "gather_small")
    conv_w, p_scale, f_conv_w = _unflatten_full(small_all, small_shapes, small_axes)
    def full_shapes(spec, shards):
        return [jax.ShapeDtypeStruct(_full_shape(axis, s.shape), s.dtype) for s, (_, axis) in zip(shards, spec)]

    def row_halves(a):
        return a.reshape((2, a.shape[0] // 2) + a.shape[1:])

    def join_w_in(g):
        return jnp.concatenate([g[:, k] for k in range(N_CHIPS)], axis=-1).reshape(d, -1)

    def join_w_out(g):
        r2 = g.shape[1] // N_CHIPS
        return jnp.concatenate([g[hf, k * r2:(k + 1) * r2] for k in range(N_CHIPS) for hf in range(2)], axis=0)

    ssd_spec = (("ssd_w_in", None), ("ssd_w_out", 0))
    first_shards = [row_halves(wts[n][0].astype(BF16)) for n, _ in ssd_spec]
    stages, counts = _gather_stages(ssd_spec)
    g_in0, g_out0 = _comm_fused(stages, counts, first_shards, full_shapes(ssd_spec, first_shards), "gather_first")
    w_in, w_out = [join_w_in(g_in0)], [join_w_out(g_out0)]
    rest_spec = ssd_spec * (n_ssd - 1) + (("pool_w", 1),) + FFNW
    rest_shards = [row_halves(wts[n][jj].astype(BF16)) for jj in range(1, n_ssd) for n, _ in ssd_spec]
    rest_shards += [wts["pool_w"].astype(BF16)] + [wts[n].astype(BF16) for n, _ in FFNW]
    stages, counts = _gather_stages(rest_spec)
    ffn_gather = _SplitComm(stages, counts, rest_shards + [g_out0],
                            [lax.empty(s.shape, s.dtype) for s in full_shapes(rest_spec, rest_shards)], "gather_rest")
    gather_token = ffn_gather.advance()

    def pad_heads(a):
        lead = a.shape[:-1]
        a = a.reshape(lead + (N_GROUPS, hpg))
        a = jnp.pad(a, [(0, 0)] * len(lead) + [(0, 0), (0, LANES - hpg)])
        return a.reshape(lead + (N_GROUPS * LANES,))

    def unpad_heads(a):
        lead = a.shape[:-1]
        return a.reshape(lead + (N_GROUPS, LANES))[..., :hpg].reshape(lead + (nheads,))

    def group_rows(a, width):
        return jnp.broadcast_to(a.reshape(N_GROUPS, 1, width), (N_GROUPS, 8, width))

    def pad_w_in(w):
        return jnp.concatenate([w[..., :d_inner + xbc], pad_heads(w[..., d_inner + xbc:])], axis=-1)

    w_in_p = [pad_w_in(w_in[0])]
    zw = w_in_p[0].shape[-1]
    w_pool = None

    x2 = x.reshape(t, d)
    tgt2 = loss_target.reshape(t, d)
    w_up = w_down = None

    saved = []
    cur = x2
    tokens = []
    h = _norm_fwd(cur, norm_mix_pre[0:1], BF16, "norm_pre_b", after=[gather_token])
    for i in range(depth):
        j = i // 2
        sv = dict(x_in=cur)
        if i % 2 == 0:
            zx = _mm(h, w_in_p[j], "nn", F32, "mm_ssd_in", 2048, 512, d).reshape(bl, seq, zw)
            xc, xpre = _ssd_conv_fwd(zx, conv_w[j], ssd_conv_b[j:j + 1], d_inner, "ssd_conv_fwd")
            dtb = group_rows(pad_heads(ssd_dt_bias[j]), LANES)
            alog = group_rows(pad_heads(ssd_a_log[j]), LANES)
            dskip = group_rows(jnp.repeat(ssd_d[j], HEAD_DIM), gw)
            nw = group_rows(ssd_norm_w[j], gw)
            y, yn, st = _ssd_fwd(xc, zx, dtb, alog, dskip, nw, d_inner, "ssd_fwd")
            if i == 0:
                tokens.append(ffn_gather.advance(after=yn))
            mix = _mm(yn.reshape(t, d_inner), w_out[j], "nn", F32, "mm_ssd_out", 512, 512, d_inner)
            sv.update(h=h, zx=zx, xc=xc, xpre=xpre, y=y, yn=yn, st=st, dtb=dtb, alog=alog, dskip=dskip, nw=nw)
        else:
            mix = _pool_fwd(h.reshape(bl, seq, d), w_pool[j], p_scale[j:j + 1], "pool_fwd").reshape(t, d)
            sv.update(h=h)
        sv.update(mix=mix)
        mid, u = _norm_post_pre(mix, norm_mix_post[i:i + 1], cur, norm_ffn_pre[i:i + 1], BF16, "norm_post_pre_b",
                                after=tokens)
        tokens = []
        if i == 0:
            ffn_gather.advance(after=u)
            rest = ffn_gather.lands()
            for jj in range(1, n_ssd):
                w_in_p.append(pad_w_in(join_w_in(rest[2 * (jj - 1)])))
                w_out.append(join_w_out(rest[2 * (jj - 1) + 1]))
            w_pool, w_up, w_down = rest[2 * (n_ssd - 1):]
        hpre = _mm(u, w_up, "nn", BF16, "mm_up", 2048, 512, d, b_layer=i).reshape(bl, seq, f2)
        act, pre_g, pre_v = _ffn_act_fwd(hpre, f_conv_w[i], ffn_conv_b[i:i + 1], "ffn_act_fwd")
        act = act.reshape(t, ff)
        fo = _mm(act, w_down, "nn", F32, "mm_down", 1024, 512, ff, b_layer=i)
        if i + 1 == depth:
            cur = _norm_fwd(fo, norm_ffn_post[i:i + 1], F32, "norm_post", resid=mid)
        elif i % 2 == 0:
            cur, h = _norm_post_pre(fo, norm_ffn_post[i:i + 1], mid, norm_mix_pre[i + 1:i + 2], F32, "norm_post_pre_f")
        else:
            cur, h = _norm_post_pre(fo, norm_ffn_post[i:i + 1], mid, norm_mix_pre[i + 1:i + 2], BF16, "norm_post_pre_b")
        sv.update(mid=mid, u=u, hpre=hpre, pre_g=pre_g, pre_v=pre_v, act=act, fo=fo)
        saved.append(sv)

    dcur, loss_part = _loss_head(cur, tgt2, "loss_head")

    g = {n: [None] * wts[n].shape[0] for n in WEIGHTS}
    gbuf = dict(up=lax.empty((depth, d, f2), F32), down=lax.empty((depth, ff, d), F32),
                out=lax.empty((n_ssd, d_inner, d), F32), win=lax.empty((n_ssd, d, zw), F32))
    core = cy.reshape(1).astype(jnp.int32)

    def mixer_bwd(i, dmid, behind=()):
        j = i // 2
        sv = saved[i]
        done = []
        if i % 2 == 0:
            dmix, g["norm_mix_post"][i] = _norm_bwd(sv["mix"], norm_mix_post[i:i + 1], dmid, BF16, "norm_bwd_b",
                                                    after=behind)
            dyn = _mm(dmix, w_out[j], "nt", F32, "mm_ssd_out_dx", 1024, 1024, d)
            gbuf["out"], tok = _mm(sv["yn"].reshape(t, d_inner), dmix, "tn", F32, "mm_ssd_out_dw", 1024, 1024, 2048,
                                   out_buf=(gbuf["out"], j))
            done.append(tok)
            dz, dxs, dbm, dcm, ddt, dnw, dd, dal, dbias = _ssd_bwd(
                sv["xc"], sv["zx"], sv["y"], dyn.reshape(bl, seq, d_inner), sv["st"], sv["dtb"], sv["alog"],
                sv["dskip"], sv["nw"], d_inner, "ssd_bwd")
            g["ssd_norm_w"][j] = dnw[:, 0, :].reshape(d_inner)
            g["ssd_d"][j] = dd[:, 0, :hpg].reshape(nheads)
            g["ssd_a_log"][j] = dal[:, 0, :hpg].reshape(nheads)
            g["ssd_dt_bias"][j] = dbias[:, 0, :hpg].reshape(nheads)
            dxbc, dcw, dcb = _ssd_conv_bwd(sv["zx"], sv["xpre"], (dxs, dbm, dcm), conv_w[j], d_inner, "ssd_conv_bwd")
            g["ssd_conv_w"][j] = dcw
            g["ssd_conv_b"][j] = dcb[0]
            dzs = [dz.reshape(t, d_inner), dxbc.reshape(t, xbc), ddt.reshape(t, N_GROUPS * LANES)]
            dh = _mm(dzs, w_in_p[j], "nt", BF16, "mm_ssd_in_dx", 1024, d, [1024, 1024, 512])
            gbuf["win"], tok = _mm(sv["h"], dzs, "tn", F32, "mm_ssd_in_dw", 1024, 512, 2048, out_buf=(gbuf["win"], j))
            done.append(tok)
        else:
            dmix, g["norm_mix_post"][i] = _norm_bwd(sv["mix"], norm_mix_post[i:i + 1], dmid, F32, "norm_bwd_f",
                                                    after=behind)
            dh3, g["pool_w"][j], dps = _pool_bwd(sv["h"].reshape(bl, seq, d), dmix.reshape(bl, seq, d), w_pool[j],
                                                 p_scale[j:j + 1], "pool_bwd")
            g["pool_scale"][j] = dps[0]
            dh = dh3.reshape(t, d)
        dx_in, g["norm_mix_pre"][i] = _norm_bwd(sv["x_in"], norm_mix_pre[i:i + 1], dh, F32, "norm_bwd_r", resid=dmid,
                                                after=done)
        return dx_in

    ffn_comm = None
    for i in reversed(range(depth)):
        sv = saved[i]
        dfo, g["norm_ffn_post"][i] = _norm_bwd(sv["fo"], norm_ffn_post[i:i + 1], dcur, BF16, "norm_bwd_b")
        dact = _mm(dfo, w_down, "nt", BF16, "mm_down_dx", 1024, ff // 2, d, b_layer=i)
        gbuf["down"], tok_down = _mm(sv["act"], dfo, "tn", F32, "mm_down_dw", ff // 2, 1024, 2048,
                                     out_buf=(gbuf["down"], i))
        dhg, dhv, dcw, dcb = _ffn_act_bwd(sv["hpre"], sv["pre_g"], sv["pre_v"], dact.reshape(bl, seq, ff), f_conv_w[i],
                                          "ffn_act_bwd")
        g["ffn_conv_w"][i] = dcw
        g["ffn_conv_b"][i] = dcb[0]
        dhs = [dhg.reshape(t, ff), dhv.reshape(t, ff)]
        du = _mm(dhs, w_up, "nt", BF16, "mm_up_dx", 1024, d, ff, b_layer=i)
        gbuf["up"], tok_up = _mm(sv["u"], dhs, "tn", F32, "mm_up_dw", 1024, ff // 2, 2048, out_buf=(gbuf["up"], i))
        dmid, g["norm_ffn_pre"][i] = _norm_bwd(sv["mid"], norm_ffn_pre[i:i + 1], du, F32, "norm_bwd_r", resid=dcur,
                                               after=[tok_down, tok_up])
        if i > 0:
            dcur = mixer_bwd(i, dmid)
        else:
            ffn_comm, ffn_token = _reduce_begin(FFNW, [gbuf["up"], gbuf["down"]], core, "ffn")
            dcur = mixer_bwd(0, dmid, behind=[ffn_token])

    grad_x = dcur.reshape(bl, seq, d)
    for n in ("norm_mix_pre", "norm_mix_post", "norm_ffn_pre", "norm_ffn_post"):
        g[n] = [a[0] for a in g[n]]
    small_names = [n for n, _ in SMALL] + list(REPL)
    full = {n: jnp.stack(g[n], axis=0) for n in small_names}

    g_in = jnp.concatenate([gbuf["win"][..., :d_inner + xbc], unpad_heads(gbuf["win"][..., d_inner + xbc:])], axis=-1)
    g_in_cm = jnp.swapaxes(g_in.reshape(n_ssd, d, N_CHIPS, -1), 1, 2)
    vec = jnp.concatenate([full[n].reshape(-1) for n in small_names] + [loss_part[0, :1]])
    nvec = vec.shape[0]
    vrows = 16 * ((nvec + 16 * FLAT_COLS - 1) // (16 * FLAT_COLS))
    vec = jnp.pad(vec, (0, vrows * FLAT_COLS - nvec)).reshape(vrows, FLAT_COLS)
    stages, counts = _gather8_stages()
    small_comm = _SplitComm(stages, counts, [vec], [lax.empty((8, vrows, FLAT_COLS), F32)], "gather_small_grads")
    small_token = small_comm.advance()
    mix_comm, mix_token = _reduce_begin(MIXW, [g_in_cm, gbuf["out"], jnp.stack(g["pool_w"], axis=0)], core, "mixers",
                                        riders=[small_token])

    grads, deltas, new_m, new_v = {}, {}, {}, {}

    def adamw(n, gr):
        shp = wts[n].shape
        two = (math.prod(shp[:-1]), shp[-1])
        dl, mn, vn = _adamw(wts[n].reshape(two), gr.reshape(two), mom[n].reshape(two), var[n].reshape(two),
                            "adamw_" + n)
        grads[n], deltas[n], new_m[n], new_v[n] = gr, dl.reshape(shp), mn.reshape(shp), vn.reshape(shp)
        return dl

    small_comm.advance(after=mix_token)
    tot = _sum8(small_comm.lands()[0], "sum_small").reshape(-1)
    small_grads, off = {}, 0
    for n in small_names:
        cnt = math.prod(full[n].shape)
        small_grads[n] = tot[off:off + cnt].reshape(full[n].shape)
        off += cnt
    loss = tot[off]
    for n, ax in SMALL:
        w = wts[n].shape[ax]
        small_grads[n] = lax.dynamic_slice_in_dim(small_grads[n], chip * w, w, axis=ax)

    behind = [adamw(n, small_grads[n]) for n in small_names][-1:]
    ffn_grads = _reduce_finish(FFNW, ffn_comm, core, "ffn", after=mix_token)
    behind += [adamw(n, gr) for gr, (n, _) in zip(ffn_grads, FFNW)]
    mix_grads = _reduce_finish(MIXW, mix_comm, core, "mixers", after=behind)
    for gr, (n, _) in zip(mix_grads, MIXW):
        adamw(n, gr)

    return (loss, grad_x, *[grads[n] for n in WEIGHTS], *[deltas[n] for n in WEIGHTS],
            *[new_m[n] for n in WEIGHTS], *[new_v[n] for n in WEIGHTS])
```

```python
import functools
import math

import jax
import jax.numpy as jnp
from jax import lax
from jax.experimental import pallas as pl
from jax.experimental.pallas import tpu as pltpu

F32 = jnp.float32
BF16 = jnp.bfloat16
MESH = pl.DeviceIdType.MESH
ANY = pl.BlockSpec(memory_space=pl.ANY)

HEAD_DIM = 64
D_STATE = 128
CHUNK = 128
N_GROUPS = 4
SSD_CONV = 4
FFN_CONV = 3
EPS = 1e-6
N_CHIPS = 4
LANES = 128
FLAT_COLS = 1024

ADAM_LR = 0.001
ADAM_B1 = 0.9
ADAM_B2 = 0.999
ADAM_EPS = 1e-08
ADAM_WD = 0.01
ADAM_STEP = 10

VMEM_LIMIT_BYTES = 56 * 1024 * 1024


def _params(sem=None):
    kw = dict(vmem_limit_bytes=VMEM_LIMIT_BYTES)
    if sem is not None:
        kw["dimension_semantics"] = sem
    return pltpu.CompilerParams(**kw)


def _sigmoid(x):
    return 0.5 * jnp.tanh(0.5 * x) + 0.5


def _softplus(x):
    return jnp.maximum(x, 0.0) + jnp.log(1.0 + jnp.exp(-jnp.abs(x)))


def _dot(a, b, dn):
    return lax.dot_general(a, b, (dn, ((), ())), preferred_element_type=F32)


def _nn(a, b):
    return _dot(a, b, ((1,), (0,)))


def _nt(a, b):
    return _dot(a, b, ((1,), (1,)))


def _tn(a, b):
    return _dot(a, b, ((0,), (0,)))


def _split(x, parts):
    out = []
    r = x
    for _ in range(parts):
        p = r.astype(BF16)
        out.append(p)
        r = r - p.astype(F32)
    return out


def _sel_left(sel, x, parts=3):
    n = x.shape[1]
    r = _nn(sel, jnp.concatenate(_split(x, parts), axis=1))
    out = r[:, 0:n]
    for i in range(1, parts):
        out = out + r[:, i * n:(i + 1) * n]
    return out


def _sel_right(x, sel_stacked, parts=3):
    return _nn(jnp.concatenate(_split(x, parts), axis=1), sel_stacked)


def _mm(a, b, dims, out_dtype, name, tm, tn, tk, b_layer=None, out_buf=None):
    a_list = list(a) if isinstance(a, (list, tuple)) else [a]
    b_list = list(b) if isinstance(b, (list, tuple)) else [b]
    if dims in ("nn", "nt"):
        assert len(b_list) == 1
        m = a_list[0].shape[0]
        segs = [x.shape[1] for x in a_list]
        k = sum(segs)
        bshape = b_list[0].shape[-2:]
        n = bshape[1] if dims == "nn" else bshape[0]
        assert (bshape[0] if dims == "nn" else bshape[1]) == k
    else:
        assert len(a_list) == 1 and b_layer is None
        k, m = a_list[0].shape
        segs = [x.shape[1] for x in b_list]
        n = sum(segs)
    nseg = len(segs)
    tm, tn = min(tm, m), min(tn, n)
    if dims == "tn":
        tk = min(tk, k)
        tn = min(tn, min(segs))
        units = [tn] * nseg
        nk = k // tk
        assert k % tk == 0
    else:
        units = [min(u, s) for u, s in zip(tk if isinstance(tk, (list, tuple)) else [tk] * nseg, segs)]
        nk = sum(s // u for s, u in zip(segs, units))
    assert m % tm == 0 and n % tn == 0 and all(s % u == 0 for s, u in zip(segs, units)), (name, m, n, k, segs, units)
    counts = [s // u for s, u in zip(segs, units)]
    starts = [sum(counts[:s]) for s in range(nseg)]
    assert all(sum(segs[:s]) % units[s] == 0 for s in range(nseg)), (name, segs, units)
    first_block = [sum(segs[:s]) // units[s] for s in range(nseg)]
    dn = {"nn": ((1,), (0,)), "nt": ((1,), (1,)), "tn": ((0,), (0,))}[dims]

    same = len(set(units)) == 1
    nb_ops = len(b_list) if dims == "tn" else (1 if same else nseg)

    def body(*refs):
        a_refs = refs[:len(a_list)]
        b_refs = refs[len(a_list):len(a_list) + nb_ops]
        rest = refs[len(a_list) + nb_ops + (0 if out_buf is None else 1):]
        o_ref = rest[0]
        if out_buf is not None:
            rest[1][...] = jnp.zeros((8, LANES), F32)
            rest = rest[1:]
        acc = rest[1] if nk > 1 else None
        kk = pl.program_id(2)
        sel = kk if dims != "tn" else pl.program_id(1)

        def step(a_ref, b_ref):
            p = _dot(a_ref[...].astype(BF16), b_ref[...].astype(BF16), dn)
            if nk == 1:
                o_ref[...] = p.astype(out_dtype)
                return

            @pl.when(kk == 0)
            def _():
                acc[...] = p

            @pl.when(kk > 0)
            def _():
                acc[...] += p

        if nseg == 1:
            step(a_refs[0], b_refs[0])
        else:
            for s in range(nseg):
                @pl.when(jnp.logical_and(sel >= starts[s], sel < starts[s] + counts[s]))
                def _(s=s):
                    step(a_refs[s] if dims != "tn" else a_refs[0], b_refs[s if nb_ops > 1 else 0])

        if nk > 1:
            @pl.when(kk == nk - 1)
            def _():
                o_ref[...] = acc[...].astype(out_dtype)

    def seg_index(v, s):
        return v if nseg == 1 else jnp.clip(v - starts[s], 0, counts[s] - 1)

    lead = () if b_layer is None else (b_layer,)
    none = () if b_layer is None else (None,)
    def b_block(kk, s):
        return kk if same else first_block[s] + seg_index(kk, s)

    if dims == "nn":
        a_specs = [pl.BlockSpec((tm, units[s]), lambda i, j, kk, s=s: (i, seg_index(kk, s))) for s in range(nseg)]
        b_specs = [pl.BlockSpec(none + (units[s], tn), lambda i, j, kk, s=s: lead + (b_block(kk, s), j))
                   for s in range(nb_ops)]
    elif dims == "nt":
        a_specs = [pl.BlockSpec((tm, units[s]), lambda i, j, kk, s=s: (i, seg_index(kk, s))) for s in range(nseg)]
        b_specs = [pl.BlockSpec(none + (tn, units[s]), lambda i, j, kk, s=s: lead + (j, b_block(kk, s)))
                   for s in range(nb_ops)]
    else:
        a_specs = [pl.BlockSpec((tk, tm), lambda i, j, kk: (kk, i))]
        b_specs = [pl.BlockSpec((tk, tn), lambda i, j, kk, s=s: (kk, seg_index(j, s))) for s in range(nseg)]
    args = a_list + (b_list * nb_ops if dims != "tn" else b_list)
    in_specs = a_specs + b_specs
    aliases = {}
    if out_buf is None:
        out_shape = jax.ShapeDtypeStruct((m, n), out_dtype)
        out_spec = pl.BlockSpec((tm, tn), lambda i, j, kk: (i, j))
    else:
        buf, slab = out_buf
        assert buf.shape[1:] == (m, n) and buf.dtype == out_dtype
        out_shape = (jax.ShapeDtypeStruct(buf.shape, out_dtype), jax.ShapeDtypeStruct((8, LANES), F32))
        out_spec = (pl.BlockSpec((None, tm, tn), lambda i, j, kk: (slab, i, j)),
                    pl.BlockSpec((8, LANES), lambda i, j, kk: (0, 0)))
        aliases = {len(args): 0}
        args = args + [buf]
        in_specs = in_specs + [ANY]
    return pl.pallas_call(
        body,
        out_shape=out_shape,
        grid=(m // tm, n // tn, nk),
        in_specs=in_specs,
        out_specs=out_spec,
        scratch_shapes=[] if nk == 1 else [pltpu.VMEM((tm, tn), F32)],
        input_output_aliases=aliases,
        compiler_params=_params(("parallel", "parallel", "arbitrary") if out_buf is None else ("arbitrary",) * 3),
        name=name,
    )(*args)


def _row_tile(t, want):
    tm = min(want, t)
    assert t % tm == 0
    return tm


def _norm_fwd(x, w, out_dtype, name, resid=None, after=()):
    t, d = x.shape
    tm = _row_tile(t, 512)
    after = [a for a in after if a is not None]

    def body(*refs):
        refs = refs[:len(refs) - 1 - len(after)] + refs[len(refs) - 1:]
        if resid is None:
            x_ref, w_ref, o_ref = refs
        else:
            x_ref, w_ref, r_ref, o_ref = refs
        xv = x_ref[...]
        r = lax.rsqrt(jnp.mean(xv * xv, axis=-1, keepdims=True) + EPS)
        y = (xv * r) * w_ref[...]
        if resid is not None:
            y = r_ref[...] + y
        o_ref[...] = y.astype(out_dtype)

    row = pl.BlockSpec((tm, d), lambda i: (i, 0))
    vec = pl.BlockSpec((1, d), lambda i: (0, 0))
    args = [x, w] + ([] if resid is None else [resid]) + after
    return pl.pallas_call(
        body, out_shape=jax.ShapeDtypeStruct((t, d), out_dtype), grid=(t // tm,),
        in_specs=[row, vec] + ([] if resid is None else [row]) + [ANY] * len(after), out_specs=row,
        compiler_params=_params(("parallel",)), name=name)(*args)


def _norm_post_pre(m, w_post, resid, w_pre, pre_dtype, name, after=()):
    t, d = m.shape
    tm = _row_tile(t, 512)
    after = [a for a in after if a is not None]

    def body(m_ref, w1_ref, r_ref, w2_ref, *rest):
        x_ref, u_ref = rest[len(after):]
        mv = m_ref[...]
        r1 = lax.rsqrt(jnp.mean(mv * mv, axis=-1, keepdims=True) + EPS)
        xv = r_ref[...] + (mv * r1) * w1_ref[...]
        x_ref[...] = xv
        r2 = lax.rsqrt(jnp.mean(xv * xv, axis=-1, keepdims=True) + EPS)
        u_ref[...] = ((xv * r2) * w2_ref[...]).astype(pre_dtype)

    row = pl.BlockSpec((tm, d), lambda i: (i, 0))
    vec = pl.BlockSpec((1, d), lambda i: (0, 0))
    return pl.pallas_call(
        body, out_shape=(jax.ShapeDtypeStruct((t, d), F32), jax.ShapeDtypeStruct((t, d), pre_dtype)), grid=(t // tm,),
        in_specs=[row, vec, row, vec] + [ANY] * len(after), out_specs=(row, row),
        compiler_params=_params(("parallel",)), name=name)(m, w_post, resid, w_pre, *after)


def _norm_bwd(src, w, dy, out_dtype, name, resid=None, after=()):
    t, d = src.shape
    tm = _row_tile(t, 512)
    after = [a for a in after if a is not None]

    def body(*refs):
        refs = refs[:len(refs) - 2 - len(after)] + refs[len(refs) - 2:]
        if resid is None:
            x_ref, w_ref, g_ref, o_ref, dw_ref = refs
        else:
            x_ref, w_ref, g_ref, r_ref, o_ref, dw_ref = refs
        xv = x_ref[...]
        g = g_ref[...].astype(F32)
        r = lax.rsqrt(jnp.mean(xv * xv, axis=-1, keepdims=True) + EPS)
        xh = xv * r
        gh = g * w_ref[...]
        mean = jnp.mean(gh * xh, axis=-1, keepdims=True)
        dx = r * (gh - xh * mean)
        if resid is not None:
            dx = r_ref[...] + dx
        o_ref[...] = dx.astype(out_dtype)
        part = jnp.sum(g * xh, axis=0, keepdims=True)

        @pl.when(pl.program_id(0) == 0)
        def _():
            dw_ref[...] = part

        @pl.when(pl.program_id(0) > 0)
        def _():
            dw_ref[...] += part

    row = pl.BlockSpec((tm, d), lambda i: (i, 0))
    vec = pl.BlockSpec((1, d), lambda i: (0, 0))
    args = [src, w, dy] + ([] if resid is None else [resid]) + after
    return pl.pallas_call(
        body,
        out_shape=(jax.ShapeDtypeStruct((t, d), out_dtype), jax.ShapeDtypeStruct((1, d), F32)),
        grid=(t // tm,),
        in_specs=[row, vec, row] + ([] if resid is None else [row]) + [ANY] * len(after),
        out_specs=(row, vec),
        compiler_params=_params(("arbitrary",)), name=name)(*args)


def _loss_head(y, target, name):
    t, d = y.shape
    tm = _row_tile(t, 512)

    def body(y_ref, t_ref, dy_ref, l_ref):
        e = y_ref[...] - t_ref[...]
        dy_ref[...] = e * (1.0 / d)
        col = jnp.sum(e * e, axis=0, keepdims=True)
        s = jnp.sum(col, axis=1, keepdims=True) * (0.5 / d)
        part = jnp.broadcast_to(s, (1, LANES))

        @pl.when(pl.program_id(0) == 0)
        def _():
            l_ref[...] = part

        @pl.when(pl.program_id(0) > 0)
        def _():
            l_ref[...] += part

    row = pl.BlockSpec((tm, d), lambda i: (i, 0))
    return pl.pallas_call(
        body,
        out_shape=(jax.ShapeDtypeStruct((t, d), F32), jax.ShapeDtypeStruct((1, LANES), F32)),
        grid=(t // tm,), in_specs=[row, row],
        out_specs=(row, pl.BlockSpec((1, LANES), lambda i: (0, 0))),
        compiler_params=_params(("arbitrary",)), name=name)(y, target)


def _window(ref, c, rows, seq, before, after):
    r0 = pl.multiple_of(c * rows, rows)
    parts = []
    if before:
        h0 = pl.multiple_of(jnp.maximum(r0 - before, 0), before)
        halo = ref[pl.ds(h0, before), :].astype(F32)
        parts.append(jnp.where(c > 0, halo, 0.0))
    parts.append(ref[pl.ds(r0, rows), :].astype(F32))
    if after:
        h1 = pl.multiple_of(jnp.minimum(r0 + rows, seq - after), after)
        halo = ref[pl.ds(h1, after), :].astype(F32)
        parts.append(jnp.where(c < seq // rows - 1, halo, 0.0))
    return parts[0] if len(parts) == 1 else jnp.concatenate(parts, axis=0)


def _lag(x, k):
    return pltpu.roll(x, k, 0) if k else x


def _lead(x, k):
    return pltpu.roll(x, x.shape[0] - k, 0) if k else x


SHIFT_ROWS = 128
SHIFT_COLS = 256


HALO = 16


def _conv3(ext, w, bias):
    acc = bias + w[2:3, :] * ext[HALO:, :]
    acc = acc + w[1:2, :] * _lag(ext, 1)[HALO:, :]
    return acc + w[0:1, :] * _lag(ext, 2)[HALO:, :]


def _ffn_act_fwd(hpre, cw, cb, name):
    b, seq, f2 = hpre.shape
    cbk = SHIFT_COLS
    nj = f2 // (2 * cbk)
    rows = min(SHIFT_ROWS, seq)

    def body(g_ref, v_ref, wg_ref, wv_ref, bg_ref, bv_ref, o_ref, pg_ref, pv_ref):
        def chunk(c, carry):
            gate = _conv3(_window(g_ref, c, rows, seq, HALO, 0), wg_ref[...], bg_ref[...])
            val = _conv3(_window(v_ref, c, rows, seq, HALO, 0), wv_ref[...], bv_ref[...])
            a = gate * _sigmoid(gate) * val
            here = pl.ds(pl.multiple_of(c * rows, rows), rows)
            o_ref[here, :] = a.astype(BF16)
            pg_ref[here, :] = gate.astype(BF16)
            pv_ref[here, :] = val.astype(BF16)
            return carry

        lax.fori_loop(0, seq // rows, chunk, 0)

    blk = lambda off: pl.BlockSpec((None, seq, cbk), lambda i, j: (i, 0, j + off))
    wsp = lambda r, off: pl.BlockSpec((r, cbk), lambda i, j: (0, j + off))
    half = jax.ShapeDtypeStruct((b, seq, f2 // 2), BF16)
    return pl.pallas_call(
        body, out_shape=(half, half, half), grid=(b, nj),
        in_specs=[blk(0), blk(nj), wsp(FFN_CONV, 0), wsp(FFN_CONV, nj), wsp(1, 0), wsp(1, nj)],
        out_specs=(blk(0), blk(0), blk(0)),
        compiler_params=_params(("parallel", "parallel")), name=name)(hpre, hpre, cw, cw, cb, cb)


def _ffn_act_bwd(hpre, pre_g, pre_v, da, cw, name):
    b, seq, f2 = hpre.shape
    cbk = SHIFT_COLS
    nj = f2 // (2 * cbk)
    rows = min(SHIFT_ROWS, seq)

    def body(g_ref, v_ref, pg_ref, pv_ref, da_ref, wg_ref, wv_ref, og_ref, ov_ref, dwg_ref, dwv_ref, dbg_ref, dbv_ref):
        wg, wv = wg_ref[...], wv_ref[...]

        def back(dpre, w, o_ref, x_ref, c, carry):
            here = pl.ds(pl.multiple_of(c * rows, rows), rows)
            leads = [dpre, _lead(dpre, 1), _lead(dpre, 2)]
            dx = w[2:3, :] * leads[0] + w[1:2, :] * leads[1] + w[0:1, :] * leads[2]
            o_ref[here, :] = dx[:rows, :].astype(BF16)
            x0 = x_ref[here, :].astype(F32)
            return tuple(carry[k] + jnp.sum(leads[k][:rows, :] * x0, axis=0, keepdims=True) for k in range(FFN_CONV)) + (
                carry[FFN_CONV] + jnp.sum(dpre[:rows, :], axis=0, keepdims=True),)

        def chunk(c, carry):
            cg, cv = carry
            gate = _window(pg_ref, c, rows, seq, 0, HALO)
            val = _window(pv_ref, c, rows, seq, 0, HALO)
            dav = _window(da_ref, c, rows, seq, 0, HALO)
            sg = _sigmoid(gate)
            cg = back(dav * val * (sg * (1.0 + gate * (1.0 - sg))), wg, og_ref, g_ref, c, cg)
            cv = back(dav * (gate * sg), wv, ov_ref, v_ref, c, cv)
            return cg, cv

        z = jnp.zeros((1, cbk), F32)
        cg, cv = lax.fori_loop(0, seq // rows, chunk, ((z,) * (FFN_CONV + 1), (z,) * (FFN_CONV + 1)))
        dwg = jnp.concatenate([cg[2], cg[1], cg[0]], axis=0)
        dwv = jnp.concatenate([cv[2], cv[1], cv[0]], axis=0)

        @pl.when(pl.program_id(1) == 0)
        def _():
            dwg_ref[...] = dwg
            dwv_ref[...] = dwv
            dbg_ref[...] = cg[FFN_CONV]
            dbv_ref[...] = cv[FFN_CONV]

        @pl.when(pl.program_id(1) > 0)
        def _():
            dwg_ref[...] += dwg
            dwv_ref[...] += dwv
            dbg_ref[...] += cg[FFN_CONV]
            dbv_ref[...] += cv[FFN_CONV]

    blk = lambda off: pl.BlockSpec((None, seq, cbk), lambda j, i: (i, 0, j + off))
    wsp = lambda r, off: pl.BlockSpec((r, cbk), lambda j, i: (0, j + off))
    half = jax.ShapeDtypeStruct((b, seq, f2 // 2), BF16)
    dwshape = jax.ShapeDtypeStruct((FFN_CONV, f2 // 2), F32)
    dbshape = jax.ShapeDtypeStruct((1, f2 // 2), F32)
    dg, dv, dwg, dwv, dbg, dbv = pl.pallas_call(
        body,
        out_shape=(half, half, dwshape, dwshape, dbshape, dbshape),
        grid=(nj, b),
        in_specs=[blk(0), blk(nj), blk(0), blk(0), blk(0), wsp(FFN_CONV, 0), wsp(FFN_CONV, nj)],
        out_specs=(blk(0), blk(0), wsp(FFN_CONV, 0), wsp(FFN_CONV, 0), wsp(1, 0), wsp(1, 0)),
        compiler_params=_params(("parallel", "arbitrary")), name=name)(hpre, hpre, pre_g, pre_v, da, cw, cw)
    return dg, dv, jnp.concatenate([dwg, dwv], axis=1), jnp.concatenate([dbg, dbv], axis=1)


def _ssd_conv_fwd(zx, cw, cb, d_inner, name):
    b, seq, _ = zx.shape
    xbc = cw.shape[1]
    cbk = SHIFT_COLS
    off = d_inner // cbk
    rows = min(SHIFT_ROWS, seq)

    def body(h_ref, w_ref, b_ref, o_ref, p_ref):
        w = w_ref[...]
        bias = b_ref[...]

        def chunk(c, carry):
            ext = _window(h_ref, c, rows, seq, 8, 0)
            acc = bias + w[3:4, :] * ext[8:, :]
            for k in range(1, SSD_CONV):
                acc = acc + w[3 - k:4 - k, :] * _lag(ext, k)[8:, :]
            here = pl.ds(pl.multiple_of(c * rows, rows), rows)
            o_ref[here, :] = acc * _sigmoid(acc)
            p_ref[here, :] = acc.astype(BF16)
            return carry

        lax.fori_loop(0, seq // rows, chunk, 0)

    blk = pl.BlockSpec((None, seq, cbk), lambda i, j: (i, 0, j))
    return pl.pallas_call(
        body, out_shape=(jax.ShapeDtypeStruct((b, seq, xbc), F32), jax.ShapeDtypeStruct((b, seq, xbc), BF16)),
        grid=(b, xbc // cbk),
        in_specs=[pl.BlockSpec((None, seq, cbk), lambda i, j: (i, 0, j + off)),
                  pl.BlockSpec((SSD_CONV, cbk), lambda i, j: (0, j)),
                  pl.BlockSpec((1, cbk), lambda i, j: (0, j))],
        out_specs=(blk, blk),
        compiler_params=_params(("parallel", "parallel")), name=name)(zx, cw, cb)


def _ssd_conv_bwd(zx, pre, dparts, ddt, dzx, cw, d_inner, name):
    b, seq, zw = zx.shape
    xbc = cw.shape[1]
    cbk = SHIFT_COLS
    off = d_inner // cbk
    rows = min(SHIFT_ROWS, seq)
    nblk = [p.shape[2] // cbk for p in dparts]
    first = [sum(nblk[:s]) for s in range(len(dparts))]
    nconv = xbc // cbk
    ncopy = ddt.shape[2] // cbk
    assert sum(nblk) == nconv and (off + nconv + ncopy) * cbk == zw and dzx.shape == (b, seq, zw)

    def body(h_ref, p_ref, gx_ref, gb_ref, gc_ref, t_ref, w_ref, z_ref, o_ref, dw_ref, db_ref):
        j = pl.program_id(0)

        @pl.when(j < nconv)
        def _():
            conv(h_ref, p_ref, gx_ref, gb_ref, gc_ref, w_ref, o_ref, dw_ref, db_ref)

        @pl.when(j >= nconv)
        def _():
            o_ref[...] = t_ref[...]

    def conv(h_ref, p_ref, gx_ref, gb_ref, gc_ref, w_ref, o_ref, dw_ref, db_ref):
        w = w_ref[...]
        j = pl.program_id(0)

        def chunk(c, carry):
            dws, dbias = carry
            here = pl.ds(pl.multiple_of(c * rows, rows), rows)
            pre = _window(p_ref, c, rows, seq, 0, HALO)
            s = _sigmoid(pre)
            gsel = jnp.where(j < first[1], _window(gx_ref, c, rows, seq, 0, HALO),
                             jnp.where(j < first[2], _window(gb_ref, c, rows, seq, 0, HALO),
                                       _window(gc_ref, c, rows, seq, 0, HALO)))
            dpre = gsel * (s * (1.0 + pre * (1.0 - s)))
            leads = [dpre] + [_lead(dpre, k) for k in range(1, SSD_CONV)]
            dx = w[3:4, :] * leads[0]
            for k in range(1, SSD_CONV):
                dx = dx + w[3 - k:4 - k, :] * leads[k]
            o_ref[here, :] = dx[:rows, :].astype(BF16)
            x0 = h_ref[here, :]
            dws = tuple(dws[k] + jnp.sum(leads[k][:rows, :] * x0, axis=0, keepdims=True) for k in range(SSD_CONV))
            dbias = dbias + jnp.sum(dpre[:rows, :], axis=0, keepdims=True)
            return dws, dbias

        z = jnp.zeros((1, cbk), F32)
        dws, dbias = lax.fori_loop(0, seq // rows, chunk, ((z,) * SSD_CONV, z))
        dwv = jnp.concatenate([dws[3 - i] for i in range(SSD_CONV)], axis=0)

        @pl.when(pl.program_id(1) == 0)
        def _():
            dw_ref[...] = dwv
            db_ref[...] = dbias

        @pl.when(pl.program_id(1) > 0)
        def _():
            dw_ref[...] += dwv
            db_ref[...] += dbias

    conv_j = lambda j: jnp.minimum(j, nconv - 1)
    return pl.pallas_call(
        body,
        out_shape=(jax.ShapeDtypeStruct((b, seq, zw), BF16), jax.ShapeDtypeStruct((SSD_CONV, xbc), F32),
                   jax.ShapeDtypeStruct((1, xbc), F32)),
        grid=(nconv + ncopy, b),
        in_specs=[pl.BlockSpec((None, seq, cbk), lambda j, i: (i, 0, conv_j(j) + off)),
                  pl.BlockSpec((None, seq, cbk), lambda j, i: (i, 0, conv_j(j)))] + [
                  pl.BlockSpec((None, seq, cbk), lambda j, i, s=s: (i, 0, jnp.clip(j - first[s], 0, nblk[s] - 1)))
                  for s in range(3)] + [
                  pl.BlockSpec((None, seq, cbk), lambda j, i: (i, 0, jnp.clip(j - nconv, 0, ncopy - 1))),
                  pl.BlockSpec((SSD_CONV, cbk), lambda j, i: (0, conv_j(j))),
                  ANY],
        out_specs=(pl.BlockSpec((None, seq, cbk), lambda j, i: (i, 0, j + off)),
                   pl.BlockSpec((SSD_CONV, cbk), lambda j, i: (0, conv_j(j))),
                   pl.BlockSpec((1, cbk), lambda j, i: (0, conv_j(j)))),
        input_output_aliases={7: 0},
        compiler_params=_params(("arbitrary", "arbitrary")), name=name)(zx, pre, *dparts, ddt, cw, dzx)


def _pool_sums(q, g, lead):
    sh = _lead if lead else _lag
    s2 = q + sh(q, 1)
    s4 = s2 + sh(s2, 2)
    s8 = s4 + sh(s4, 4)
    s16 = s8 + sh(s8, 8)
    return jnp.where(g == 0, s2, jnp.where(g == 1, s4, jnp.where(g == 2, s8, s16)))


def _pool_count(r0, n, g, shape):
    t = (r0 + lax.broadcasted_iota(jnp.int32, shape, 0) + 1).astype(F32)
    return jnp.minimum(t, (2 << g).astype(F32))


def _pool_fwd(h, pw, scale, name):
    b, seq, d = h.shape
    dg = d // 4
    rows = min(SHIFT_ROWS, seq)

    def body(h_ref, w_ref, s_ref, o_ref):
        g = pl.program_id(1)
        wmat = w_ref[...]
        sc = s_ref[...]

        def chunk(c, carry):
            r0 = c * rows
            ext = _window(h_ref, c, rows, seq, 16, 0)
            sums = _pool_sums(ext, g, False)[16:, :]
            mixed = sums / _pool_count(r0, rows, g, (rows, dg)) - ext[16:, :]
            o_ref[pl.ds(pl.multiple_of(r0, rows), rows), :] = _nn(mixed.astype(BF16), wmat) * sc
            return carry

        lax.fori_loop(0, seq // rows, chunk, 0)

    return pl.pallas_call(
        body, out_shape=jax.ShapeDtypeStruct((b, seq, d), F32), grid=(b, 4),
        in_specs=[pl.BlockSpec((None, seq, dg), lambda i, g: (i, 0, g)),
                  pl.BlockSpec((None, dg, dg), lambda i, g: (g, 0, 0)),
                  pl.BlockSpec((1, dg), lambda i, g: (0, g))],
        out_specs=pl.BlockSpec((None, seq, dg), lambda i, g: (i, 0, g)),
        compiler_params=_params(("parallel", "parallel")), name=name)(h, pw, scale)


def _pool_bwd(h, dout, pw, scale, name):
    b, seq, d = h.shape
    dg = d // 4
    rows = min(SHIFT_ROWS, seq)

    def body(h_ref, g_ref, w_ref, s_ref, o_ref, dw_ref, ds_ref, dw_acc):
        g = pl.program_id(0)
        wmat = w_ref[...]
        sc = s_ref[...]
        dw_acc[...] = jnp.zeros_like(dw_acc)

        def chunk(c, dsc):
            r0 = c * rows
            ext = _window(h_ref, c, rows, seq, 16, 0)
            sums = _pool_sums(ext, g, False)[16:, :]
            mixed = (sums / _pool_count(r0, rows, g, (rows, dg)) - ext[16:, :]).astype(BF16)
            gext = _window(g_ref, c, rows, seq, 0, 16)
            dsc = dsc + jnp.sum(gext[:rows, :] * _nn(mixed, wmat), axis=0, keepdims=True)
            dpre = (gext * sc).astype(BF16)
            dw_acc[...] += _tn(mixed, dpre[:rows, :])
            dmix = _nt(dpre, wmat)
            q = dmix / _pool_count(r0, rows + 16, g, (rows + 16, dg))
            back = _pool_sums(q, g, True)
            o_ref[pl.ds(pl.multiple_of(r0, rows), rows), :] = back[:rows, :] - dmix[:rows, :]
            return dsc

        dsc = lax.fori_loop(0, seq // rows, chunk, jnp.zeros((1, dg), F32))

        @pl.when(pl.program_id(1) == 0)
        def _():
            dw_ref[...] = dw_acc[...]
            ds_ref[...] = dsc

        @pl.when(pl.program_id(1) > 0)
        def _():
            dw_ref[...] += dw_acc[...]
            ds_ref[...] += dsc

    return pl.pallas_call(
        body,
        out_shape=(jax.ShapeDtypeStruct((b, seq, d), F32), jax.ShapeDtypeStruct((4, dg, dg), F32),
                   jax.ShapeDtypeStruct((1, d), F32)),
        grid=(4, b),
        in_specs=[pl.BlockSpec((None, seq, dg), lambda g, i: (i, 0, g)),
                  pl.BlockSpec((None, seq, dg), lambda g, i: (i, 0, g)),
                  pl.BlockSpec((None, dg, dg), lambda g, i: (g, 0, 0)),
                  pl.BlockSpec((1, dg), lambda g, i: (0, g))],
        out_specs=(pl.BlockSpec((None, seq, dg), lambda g, i: (i, 0, g)),
                   pl.BlockSpec((None, dg, dg), lambda g, i: (g, 0, 0)),
                   pl.BlockSpec((1, dg), lambda g, i: (0, g))),
        scratch_shapes=[pltpu.VMEM((dg, dg), F32)],
        compiler_params=_params(("parallel", "arbitrary")), name=name)(h, dout, pw, scale)


def _head_of(channel):
    return jnp.right_shift(channel, HEAD_DIM.bit_length() - 1)


def _ssd_consts(gw):
    q = CHUNK
    row = lax.broadcasted_iota(jnp.int32, (q, q), 0)
    col = lax.broadcasted_iota(jnp.int32, (q, q), 1)
    tril = (row >= col).astype(BF16)
    triu = (row <= col).astype(BF16)
    e = (_head_of(lax.broadcasted_iota(jnp.int32, (LANES, gw), 1))
         == lax.broadcasted_iota(jnp.int32, (LANES, gw), 0)).astype(BF16)
    et = (_head_of(lax.broadcasted_iota(jnp.int32, (gw, LANES), 0))
          == lax.broadcasted_iota(jnp.int32, (gw, LANES), 1)).astype(BF16)
    return row, col, tril, triu, e, et


def _ssd_common(dtr, dtb, alog, gw):
    q = CHUNK
    row, col, tril, triu, e, et = _ssd_consts(gw)
    dt = _softplus(dtr + dtb)
    a_row = -jnp.exp(alog)
    acum = _sel_left(tril, dt * a_row)
    ac_last = jnp.sum(jnp.where(row == q - 1, acum, 0.0), axis=0, keepdims=True)
    eac = jnp.exp(acum)
    de = jnp.exp(ac_last - acum)
    e2 = jnp.concatenate([e, e], axis=0)
    expand = _sel_right(jnp.concatenate([dt, eac, de], axis=0), e2, 2)
    dt_x, eac_x, de_x = expand[0:q], expand[q:2 * q], expand[2 * q:3 * q]
    acum_t = acum.T
    cd_col = jnp.exp(acum_t[:, q - 1:q])
    et3 = jnp.concatenate([et, et, et], axis=1)
    cdmat = _nn(et3, jnp.concatenate(_split(jnp.broadcast_to(cd_col, (LANES, D_STATE)), 3), axis=0))
    consts = dict(row=row, col=col, tril=tril, triu=triu, e=e, et=et)
    return dt, a_row, acum, acum_t, ac_last, eac, de, dt_x, eac_x, de_x, cdmat, consts


def _decay(acum, acum_t, j, row, col):
    diff = acum[:, j:j + 1] - acum_t[j:j + 1, :]
    return jnp.exp(jnp.where(row >= col, diff, -1e30))


def _ssd_fwd(xc, zx, dtb, alog, dskip, nw, d_inner, name):
    b, seq, xbc = xc.shape
    q = CHUNK
    nc = seq // q
    gw = d_inner // N_GROUPS
    nh = gw // HEAD_DIM
    xb0 = d_inner // D_STATE
    xc0 = xb0 + N_GROUPS
    dt0 = (d_inner + xbc) // LANES

    nb = max(n for n in (4, 2, 1) if b % n == 0)

    def body(x_ref, b_ref, c_ref, z_ref, dtr_ref, dtb_ref, al_ref, dsk_ref, nw_ref, y_ref, yn_ref, st_ref, s_ref):
        @pl.when(pl.program_id(2) == 0)
        def _():
            s_ref[...] = jnp.zeros_like(s_ref)

        for s in range(nb):
            one(s, x_ref.at[s], b_ref.at[s], c_ref.at[s], z_ref.at[s], dtr_ref.at[s], dtb_ref, al_ref, dsk_ref, nw_ref,
                y_ref.at[s], yn_ref.at[s], st_ref.at[s], s_ref.at[s])

    def one(s, x_ref, b_ref, c_ref, z_ref, dtr_ref, dtb_ref, al_ref, dsk_ref, nw_ref, y_ref, yn_ref, st_ref, s_ref):
        prev = s_ref[...]
        st_ref[...] = prev
        x = x_ref[...]
        bm = b_ref[...].astype(BF16)
        cm = c_ref[...].astype(BF16)
        (dt, a_row, acum, acum_t, ac_last, eac, de, dt_x, eac_x, de_x, cdmat, k) = _ssd_common(
            dtr_ref[...], dtb_ref[0:1, :], al_ref[0:1, :], gw)
        xdt = x * dt_x
        xdt_b = xdt.astype(BF16)
        cb = _nt(cm, bm)
        half = _head_of(lax.broadcasted_iota(jnp.int32, (q, LANES), 1))
        pairs = []
        for j in range(nh):
            pc = (j // 2) * LANES
            m = (cb * _decay(acum, acum_t, j, k["row"], k["col"])).astype(BF16)
            yj = jnp.where(half == j % 2, _nn(m, xdt_b[:, pc:pc + LANES]), 0.0)
            if j % 2 == 0:
                pairs.append(yj)
            else:
                pairs[-1] = pairs[-1] + yj
        prev_b = prev.astype(BF16)
        y = dsk_ref[0:1, :] * x + jnp.concatenate(pairs, axis=1) + eac_x * _nt(cm, prev_b)
        s_ref[...] = cdmat * prev + _tn((xdt * de_x).astype(BF16), bm)
        y_ref[...] = y
        z = z_ref[...]
        yg = y * (z * _sigmoid(z))
        r = lax.rsqrt(jnp.mean(yg * yg, axis=-1, keepdims=True) + EPS)
        yn_ref[...] = ((yg * r) * nw_ref[0:1, :]).astype(BF16)

    par = lambda w: pl.BlockSpec((None, 8, w), lambda i, g, c: (g, 0, 0))
    return pl.pallas_call(
        body,
        out_shape=(jax.ShapeDtypeStruct((b, seq, d_inner), F32), jax.ShapeDtypeStruct((b, seq, d_inner), BF16),
                   jax.ShapeDtypeStruct((b, nc, N_GROUPS, gw, D_STATE), F32)),
        grid=(b // nb, N_GROUPS, nc),
        in_specs=[pl.BlockSpec((nb, q, gw), lambda i, g, c: (i, c, g)),
                  pl.BlockSpec((nb, q, D_STATE), lambda i, g, c: (i, c, xb0 + g)),
                  pl.BlockSpec((nb, q, D_STATE), lambda i, g, c: (i, c, xc0 + g)),
                  pl.BlockSpec((nb, q, gw), lambda i, g, c: (i, c, g)),
                  pl.BlockSpec((nb, q, LANES), lambda i, g, c: (i, c, dt0 + g)),
                  par(LANES), par(LANES), par(gw), par(gw)],
        out_specs=(pl.BlockSpec((nb, q, gw), lambda i, g, c: (i, c, g)),
                   pl.BlockSpec((nb, q, gw), lambda i, g, c: (i, c, g)),
                   pl.BlockSpec((nb, None, None, gw, D_STATE), lambda i, g, c: (i, c, g, 0, 0))),
        scratch_shapes=[pltpu.VMEM((nb, gw, D_STATE), F32)],
        compiler_params=_params(("parallel", "parallel", "arbitrary")), name=name,
    )(xc, xc, xc, zx, zx, dtb, alog, dskip, nw)


def _ssd_bwd(xc, zx, y, dyn, st, dtb, alog, dskip, nw, d_inner, name):
    b, seq, xbc = xc.shape
    q = CHUNK
    nc = seq // q
    gw = d_inner // N_GROUPS
    nh = gw // HEAD_DIM
    xb0 = d_inner // D_STATE
    xc0 = xb0 + N_GROUPS
    dt0 = (d_inner + xbc) // LANES

    nb = max(n for n in (4, 2, 1) if b % n == 0)

    def body(x_ref, b_ref, c_ref, z_ref, dtr_ref, y_ref, g_ref, st_ref, dtb_ref, al_ref, dsk_ref, nw_ref,
             dz_ref, dx_ref, db_ref, dc_ref, ddt_ref, dnw_ref, dd_ref, dal_ref, dbias_ref,
             ds_ref, colbuf, rowbuf):
        first = jnp.logical_and(pl.program_id(1) == 0, pl.program_id(2) == 0)

        @pl.when(pl.program_id(2) == 0)
        def _():
            ds_ref[...] = jnp.zeros_like(ds_ref)

        sums = [one(x_ref.at[s], b_ref.at[s], c_ref.at[s], z_ref.at[s], dtr_ref.at[s], y_ref.at[s], g_ref.at[s],
                    st_ref.at[s], dtb_ref, al_ref, dsk_ref, nw_ref, dz_ref.at[s], dx_ref.at[s], db_ref.at[s],
                    dc_ref.at[s], ddt_ref.at[s], ds_ref.at[s], colbuf.at[s], rowbuf.at[s]) for s in range(nb)]
        dnw, dd, dal, dbias = [functools.reduce(lambda p, r: p + r, [sm[i] for sm in sums]) for i in range(4)]

        @pl.when(first)
        def _():
            dnw_ref[...] = jnp.broadcast_to(dnw, (8, gw))
            dd_ref[...] = dd
            dal_ref[...] = jnp.broadcast_to(dal, (8, LANES))
            dbias_ref[...] = jnp.broadcast_to(dbias, (8, LANES))

        @pl.when(jnp.logical_not(first))
        def _():
            dnw_ref[...] += jnp.broadcast_to(dnw, (8, gw))
            dd_ref[...] += dd
            dal_ref[...] += jnp.broadcast_to(dal, (8, LANES))
            dbias_ref[...] += jnp.broadcast_to(dbias, (8, LANES))

    def one(x_ref, b_ref, c_ref, z_ref, dtr_ref, y_ref, g_ref, st_ref, dtb_ref, al_ref, dsk_ref, nw_ref,
            dz_ref, dx_ref, db_ref, dc_ref, ddt_ref, ds_ref, colbuf, rowbuf):
        x = x_ref[...]
        bm = b_ref[...].astype(BF16)
        cm = c_ref[...].astype(BF16)
        z = z_ref[...]
        y = y_ref[...]
        prev = st_ref[...]
        dtr = dtr_ref[...] + dtb_ref[0:1, :]
        (dt, a_row, acum, acum_t, ac_last, eac, de, dt_x, eac_x, de_x, cdmat, k) = _ssd_common(
            dtr_ref[...], dtb_ref[0:1, :], al_ref[0:1, :], gw)
        row, col = k["row"], k["col"]
        et2 = jnp.concatenate([k["et"], k["et"]], axis=0)

        sz = _sigmoid(z)
        silu_z = z * sz
        yg = y * silu_z
        r = lax.rsqrt(jnp.mean(yg * yg, axis=-1, keepdims=True) + EPS)
        xh = yg * r
        dyn = g_ref[...]
        gh = dyn * nw_ref[0:1, :]
        dyg = r * (gh - xh * jnp.mean(gh * xh, axis=-1, keepdims=True))
        dnw = jnp.sum(dyn * xh, axis=0, keepdims=True)
        g = dyg * silu_z
        dz_ref[...] = (dyg * y * (sz * (1.0 + z * (1.0 - sz)))).astype(BF16)
        dd = _sel_right(jnp.broadcast_to(jnp.sum(g * x, axis=0, keepdims=True), (8, gw)), et2, 2)

        xdt = x * dt_x
        xdt_b = xdt.astype(BF16)
        g_b = g.astype(BF16)
        prev_b = prev.astype(BF16)
        cb = _nt(cm, bm)

        cp = _nt(cm, prev_b)
        ge = g * eac_x
        dac = _sel_right(ge * cp, et2, 2)
        ge_b = ge.astype(BF16)
        dcm = _nn(ge_b, prev_b)
        dprev = _tn(ge_b, cm)

        colbuf[...] = jnp.zeros_like(colbuf)
        rowbuf[...] = jnp.zeros_like(rowbuf)
        dcb = jnp.zeros((q, q), F32)
        half = _head_of(lax.broadcasted_iota(jnp.int32, (q, LANES), 1))
        pairs = []
        for j in range(nh):
            pc = (j // 2) * LANES
            dec = _decay(acum, acum_t, j, row, col)
            m = cb * dec
            gj = jnp.where(half == j % 2, g[:, pc:pc + LANES], 0.0).astype(BF16)
            dm = _nt(gj, xdt_b[:, pc:pc + LANES])
            w = dm * m
            colbuf[:, j:j + 1] = jnp.sum(w, axis=1, keepdims=True)
            rowbuf[j:j + 1, :] = jnp.sum(w, axis=0, keepdims=True)
            dcb = dcb + dm * dec
            dj = jnp.where(half == j % 2, _tn(m.astype(BF16), g_b[:, pc:pc + LANES]), 0.0)
            if j % 2 == 0:
                pairs.append(dj)
            else:
                pairs[-1] = pairs[-1] + dj
        dxdt = jnp.concatenate(pairs, axis=1)
        dcb_b = dcb.astype(BF16)
        dcm = dcm + _nn(dcb_b, bm)
        dbm = _tn(dcb_b, cm)

        ds = ds_ref[...]
        ds_b = ds.astype(BF16)
        u = _nt(bm, ds_b)
        dxdt = dxdt + u * de_x
        dde = _sel_right(u * xdt, et2, 2)
        dbm = dbm + _nn((xdt * de_x).astype(BF16), ds_b)
        pm = jnp.concatenate(_split(ds * prev, 2), axis=1)
        t2 = _tn(pm, k["et"])
        dcd_row = jnp.sum(t2[0:D_STATE] + t2[D_STATE:2 * D_STATE], axis=0, keepdims=True)
        last = dcd_row * jnp.exp(ac_last) + jnp.sum(dde * de, axis=0, keepdims=True)
        dac = dac + colbuf[...] - rowbuf[...].T - dde * de + jnp.where(row == q - 1, last, 0.0)
        ds_ref[...] = cdmat * ds + dprev

        dadt = _sel_left(k["triu"], dac)
        ddt = _sel_right(dxdt * x, et2, 2) + dadt * a_row
        dal = jnp.sum(dadt * dt, axis=0, keepdims=True) * a_row
        lane = lax.broadcasted_iota(jnp.int32, (q, LANES), 1)
        ddtr = jnp.where(lane < nh, ddt * _sigmoid(dtr), 0.0)
        ddt_ref[...] = ddtr.astype(BF16)
        dbias = jnp.sum(ddtr, axis=0, keepdims=True)
        dx_ref[...] = dxdt * dt_x + dsk_ref[0:1, :] * g
        db_ref[...] = dbm
        dc_ref[...] = dcm
        return dnw, dd, dal, dbias

    rc = lambda c: nc - 1 - c
    par = lambda w: pl.BlockSpec((None, 8, w), lambda g, i, c: (g, 0, 0))
    blk = lambda w: pl.BlockSpec((nb, q, w), lambda g, i, c: (i, rc(c), g))
    return pl.pallas_call(
        body,
        out_shape=(jax.ShapeDtypeStruct((b, seq, zx.shape[2]), BF16),
                   jax.ShapeDtypeStruct((b, seq, d_inner), F32),
                   jax.ShapeDtypeStruct((b, seq, N_GROUPS * D_STATE), F32),
                   jax.ShapeDtypeStruct((b, seq, N_GROUPS * D_STATE), F32),
                   jax.ShapeDtypeStruct((b, seq, N_GROUPS * LANES), BF16),
                   jax.ShapeDtypeStruct((N_GROUPS, 8, gw), F32),
                   jax.ShapeDtypeStruct((N_GROUPS, 8, LANES), F32),
                   jax.ShapeDtypeStruct((N_GROUPS, 8, LANES), F32),
                   jax.ShapeDtypeStruct((N_GROUPS, 8, LANES), F32)),
        grid=(N_GROUPS, b // nb, nc),
        in_specs=[blk(gw),
                  pl.BlockSpec((nb, q, D_STATE), lambda g, i, c: (i, rc(c), xb0 + g)),
                  pl.BlockSpec((nb, q, D_STATE), lambda g, i, c: (i, rc(c), xc0 + g)),
                  blk(gw),
                  pl.BlockSpec((nb, q, LANES), lambda g, i, c: (i, rc(c), dt0 + g)),
                  blk(gw), blk(gw),
                  pl.BlockSpec((nb, None, None, gw, D_STATE), lambda g, i, c: (i, rc(c), g, 0, 0)),
                  par(LANES), par(LANES), par(gw), par(gw)],
        out_specs=(blk(gw), blk(gw), blk(D_STATE), blk(D_STATE), blk(LANES),
                   par(gw), par(LANES), par(LANES), par(LANES)),
        scratch_shapes=[pltpu.VMEM((nb, gw, D_STATE), F32), pltpu.VMEM((nb, q, LANES), F32),
                        pltpu.VMEM((nb, LANES, q), F32)],
        compiler_params=_params(("parallel", "arbitrary", "arbitrary")), name=name,
    )(xc, xc, xc, zx, zx, y, dyn, st, dtb, alog, dskip, nw)


def _adamw(w, g, m, v, name):
    rows, cols = w.shape
    tr = rows
    for cand in (512, 256, 128, 64, 32, 16, 8):
        if rows % cand == 0 and cand * cols * 4 <= 2 * 1024 * 1024:
            tr = cand
            break
    c1 = 1.0 - ADAM_B1 ** ADAM_STEP
    c2 = 1.0 - ADAM_B2 ** ADAM_STEP

    def body(w_ref, g_ref, m_ref, v_ref, d_ref, mo_ref, vo_ref):
        gv = g_ref[...]
        mn = ADAM_B1 * m_ref[...] + (1.0 - ADAM_B1) * gv
        vn = ADAM_B2 * v_ref[...] + (1.0 - ADAM_B2) * (gv * gv)
        mo_ref[...] = mn
        vo_ref[...] = vn
        d_ref[...] = -ADAM_LR * ((mn / c1) / (jnp.sqrt(vn / c2) + ADAM_EPS) + ADAM_WD * w_ref[...])

    spec = pl.BlockSpec((tr, cols), lambda i: (i, 0))
    shp = jax.ShapeDtypeStruct((rows, cols), F32)
    return pl.pallas_call(body, out_shape=(shp, shp, shp), grid=(rows // tr,), in_specs=[spec] * 4,
                          out_specs=(spec,) * 3, compiler_params=_params(("parallel",)), name=name)(w, g, m, v)


def _pick_rows(rows, row_bytes, limit=1 << 20):
    for cand in (2048, 1024, 512, 256, 128, 64, 32, 16):
        if rows % cand == 0 and cand * row_bytes <= limit:
            return cand
    return rows


def _as3d(a, lead):
    return a.reshape(a.shape[:lead] + (-1, a.shape[-1]))


def _pair_sum(g, got, core, name):
    h = got.shape[0]
    g3, got3 = _as3d(g, 1), _as3d(got, 1)
    _, rows, cols = got3.shape
    tr = _pick_rows(rows, cols * 4)

    def body(c_ref, g_ref, r_ref, o_ref):
        o_ref[...] = (g_ref[...] + r_ref[...]).astype(BF16)

    out = pl.pallas_call(
        body, out_shape=jax.ShapeDtypeStruct(got3.shape, BF16),
        grid_spec=pltpu.PrefetchScalarGridSpec(
            num_scalar_prefetch=1, grid=(h, rows // tr),
            in_specs=[pl.BlockSpec((None, tr, cols), lambda l, i, c_ref: (c_ref[0] * h + l, i, 0)),
                      pl.BlockSpec((None, tr, cols), lambda l, i, c_ref: (l, i, 0))],
            out_specs=pl.BlockSpec((None, tr, cols), lambda l, i, c_ref: (l, i, 0))),
        compiler_params=_params(("parallel", "parallel")), name=name)(core, g3, got3)
    return out.reshape(got.shape)


def _sum4(q, core, name):
    q4 = _as3d(q, 2)
    _, h, rows, cols = q4.shape
    tr = _pick_rows(rows, cols * 4)

    def body(c_ref, q0, q1, q2, q3, o_ref):
        o_ref[...] = ((q0[...].astype(F32) + q1[...].astype(F32)) + q2[...].astype(F32)) + q3[...].astype(F32)

    out = pl.pallas_call(
        body, out_shape=jax.ShapeDtypeStruct((2 * h, rows, cols), F32),
        grid_spec=pltpu.PrefetchScalarGridSpec(
            num_scalar_prefetch=1, grid=(h, rows // tr),
            in_specs=[pl.BlockSpec((None, None, tr, cols), lambda l, i, c_ref, k=k: (k, l, i, 0))
                      for k in range(N_CHIPS)],
            out_specs=pl.BlockSpec((None, tr, cols), lambda l, i, c_ref: (c_ref[0] * h + l, i, 0))),
        compiler_params=_params(("parallel", "parallel")), name=name)(core, q4, q4, q4, q4)
    return out.reshape((2 * h,) + q.shape[2:])


def _coords():
    return lax.axis_index("x"), lax.axis_index("y"), lax.axis_index("c")


def _other_chips(x, y):
    return [(1 - x, y), (x, 1 - y), (1 - x, 1 - y)]


def _allgather_halves(src, name):
    rows, cols = src.shape

    def body(x_ref, o_ref, send, recv, local):
        x, y, c = _coords()
        sib = (x, y, 1 - c)
        chips = _other_chips(x, y)

        def slot(h, cx, cy):
            return o_ref.at[h, 2 * cx + cy]

        def copy(kk, dst, to, src_ref):
            return pltpu.make_async_remote_copy(src_ref=src_ref, dst_ref=dst, send_sem=send.at[kk],
                                                recv_sem=recv.at[kk], device_id=to, device_id_type=MESH)

        mine = pltpu.make_async_copy(x_ref, slot(c, x, y), local)
        mine.start()
        first = [copy(0, slot(c, x, y), sib, x_ref)]
        first += [copy(1 + j, slot(c, x, y), (*chip, c), x_ref) for j, chip in enumerate(chips)]
        for cp in first:
            cp.start()
        passed = [copy(4 + j, slot(c, *chip), sib, slot(c, *chip)) for j, chip in enumerate(chips)]
        for j, chip in enumerate(chips):
            copy(1 + j, slot(c, *chip), (x, y, c), x_ref).wait_recv()
            passed[j].start()
        copy(0, slot(1 - c, x, y), (x, y, c), x_ref).wait_recv()
        for j, chip in enumerate(chips):
            copy(4 + j, slot(1 - c, *chip), (x, y, c), x_ref).wait_recv()
        for cp in first + passed:
            cp.wait_send()
        mine.wait()

    return pl.pallas_call(
        body, out_shape=jax.ShapeDtypeStruct((2, N_CHIPS, rows, cols), src.dtype),
        in_specs=[ANY], out_specs=ANY,
        scratch_shapes=[pltpu.SemaphoreType.DMA((7,)), pltpu.SemaphoreType.DMA((7,)), pltpu.SemaphoreType.DMA],
        name=name)(src)


MIXW = (("ssd_w_in", None), ("ssd_w_out", 0), ("pool_w", 1))
FFNW = (("ffn_w_up", 1), ("ffn_w_down", 0))


def _chip_window(axis, ref, layers, k):
    if axis is None:
        return ref.at[layers, k]
    n = ref.shape[1 + axis] // N_CHIPS
    sl = pl.ds(pl.multiple_of(k * n, LANES if 1 + axis == len(ref.shape) - 1 else 8), n)
    idx = [layers] + [slice(None)] * (len(ref.shape) - 1)
    idx[1 + axis] = sl
    return ref.at[tuple(idx)]


def _full_shape(axis, shard_shape):
    if axis is None:
        return (shard_shape[0], N_CHIPS) + tuple(shard_shape[1:])
    full = list(shard_shape)
    full[1 + axis] *= N_CHIPS
    return tuple(full)


HBM_SPEC = pl.BlockSpec(memory_space=pltpu.HBM)
SEM_SPEC = pl.BlockSpec(memory_space=pltpu.SEMAPHORE)


def _dma_sems(count):
    return pltpu.SemaphoreType.DMA((max(count, 1),))


def _wait_for(copy, kind):
    if kind == "recv":
        copy.wait_recv()
    elif kind == "send":
        copy.wait_send()
    else:
        copy.wait()


def _comm_fused(stages, counts, srcs, lands, name, inplace=False):
    ns, nl, k = len(srcs), len(lands), len(stages)

    def body(*refs):
        src_refs = refs[:ns]
        land_refs = refs[ns + (nl if inplace else 0):ns + (nl if inplace else 0) + nl]
        sem_refs = refs[len(refs) - 3 * k:]
        for s, stage_fn in enumerate(stages):
            starts, waits = stage_fn(src_refs, land_refs, tuple(sem_refs[3 * s:3 * s + 3]))
            for cp in starts:
                cp.start()
            for cp, kind in waits:
                _wait_for(cp, kind)

    scratch = []
    for cnt in counts:
        scratch += [_dma_sems(c) for c in cnt]
    outs = pl.pallas_call(
        body, out_shape=tuple(jax.ShapeDtypeStruct(a.shape, a.dtype) for a in lands),
        in_specs=[ANY] * (ns + (nl if inplace else 0)), out_specs=(ANY,) * nl,
        input_output_aliases={ns + i: i for i in range(nl)} if inplace else {},
        scratch_shapes=scratch, name=name)(*srcs, *(lands if inplace else ()))
    return list(outs)


class _SplitComm:
    def __init__(self, stages, counts, srcs, lands, name):
        self.stages, self.counts, self.name = stages, counts, name
        self.ns = len(srcs)
        self.data = [pltpu.with_memory_space_constraint(a, pltpu.HBM) for a in list(srcs) + list(lands)]
        self.sems = None
        self.step = 0

    def advance(self, after=None):
        i, k, nd, ns = self.step, len(self.stages), len(self.data), self.ns
        first, last = i == 0, i == k
        stages = self.stages
        after = list(after) if isinstance(after, (list, tuple)) else [after]

        def body(*refs):
            data = refs[:nd]
            pos = nd
            if not first:
                old = tuple(refs[pos:pos + 3])
                pos += 3 + len(after)
            if not last:
                new = tuple(refs[pos:pos + 3])
            if not first:
                for cp, kind in stages[i - 1](data[:ns], data[ns:], old)[1]:
                    _wait_for(cp, kind)
            if not last:
                for cp in stages[i](data[:ns], data[ns:], new)[0]:
                    cp.start()
                refs[len(refs) - 1][...] = jnp.zeros((8, LANES), F32)

        args = list(self.data)
        in_specs = [HBM_SPEC] * nd
        if not first:
            args += list(self.sems) + after
            in_specs += [SEM_SPEC] * 3 + [ANY] * len(after)
        out_shape, out_specs = [], []
        if not last:
            out_shape += [_dma_sems(c) for c in self.counts[i]]
            out_specs += [SEM_SPEC] * 3
        out_shape += [pltpu.HBM(a.shape, a.dtype) for a in self.data]
        out_specs += [HBM_SPEC] * nd
        if not last:
            out_shape.append(jax.ShapeDtypeStruct((8, LANES), F32))
            out_specs.append(pl.BlockSpec(memory_space=pltpu.VMEM))
        off = 0 if last else 3
        outs = pl.pallas_call(
            body, out_shape=tuple(out_shape), in_specs=in_specs, out_specs=tuple(out_specs),
            input_output_aliases={d: off + d for d in range(nd)},
            compiler_params=pltpu.CompilerParams(has_side_effects=pltpu.SideEffectType.DATAFLOW_SIDE_EFFECTING),
            name=f"{self.name}_{i}")(*args)
        self.sems = None if last else outs[:3]
        self.data = list(outs[off:off + nd])
        self.step += 1
        return None if last else outs[len(outs) - 1]

    def lands(self):
        return self.data[self.ns:]


def _gather_stages(spec):
    n = len(spec)

    def parts(srcs, lands):
        x, y, c = _coords()
        out = []
        for w, (_, axis) in enumerate(spec):
            h = srcs[w].shape[0] // 2
            mine, theirs = pl.ds(c * h, h), pl.ds((1 - c) * h, h)
            out.append((srcs[w].at[mine], lambda layers, k, w=w, axis=axis: _chip_window(axis, lands[w], layers, k),
                        mine, theirs))
        return x, y, c, 2 * x + y, (x, y, 1 - c), _other_chips(x, y), out

    def remote(src, dst, send, recv, idx, to):
        return pltpu.make_async_remote_copy(src_ref=src, dst_ref=dst, send_sem=send.at[idx], recv_sem=recv.at[idx],
                                            device_id=to, device_id_type=MESH)

    def stage0(srcs, lands, sems):
        send, recv, local = sems
        x, y, c, me, sib, chips, ps = parts(srcs, lands)
        starts, waits = [], []
        for w, (src, dst, mine, theirs) in enumerate(ps):
            lc = pltpu.make_async_copy(src, dst(mine, me), local.at[w])
            first = [remote(src, dst(mine, me), send, recv, 4 * w, sib)]
            first += [remote(src, dst(mine, me), send, recv, 4 * w + 1 + j, (cx, cy, c)) for j, (cx, cy) in enumerate(chips)]
            starts += [lc] + first
            waits.append((remote(src, dst(theirs, me), send, recv, 4 * w, (x, y, c)), "recv"))
            waits += [(remote(src, dst(mine, 2 * cx + cy), send, recv, 4 * w + 1 + j, (x, y, c)), "recv")
                      for j, (cx, cy) in enumerate(chips)]
            waits += [(cp, "send") for cp in first] + [(lc, "local")]
        return starts, waits

    def stage1(srcs, lands, sems):
        send, recv, _ = sems
        x, y, c, me, sib, chips, ps = parts(srcs, lands)
        starts, waits = [], []
        for w, (src, dst, mine, theirs) in enumerate(ps):
            for j, (cx, cy) in enumerate(chips):
                blk = dst(mine, 2 * cx + cy)
                fwd = remote(blk, blk, send, recv, 3 * w + j, sib)
                starts.append(fwd)
                waits.append((remote(src, dst(theirs, 2 * cx + cy), send, recv, 3 * w + j, (x, y, c)), "recv"))
                waits.append((fwd, "send"))
        return starts, waits

    return [stage0, stage1], [(4 * n, 4 * n, n), (3 * n, 3 * n, 0)]


def _swap_stages(spec):
    n = len(spec)

    def stage(srcs, lands, sems):
        send, recv, _ = sems
        x, y, c = _coords()
        starts, waits = [], []
        for w in range(n):
            h = srcs[w].shape[0] // 2
            cp = pltpu.make_async_remote_copy(src_ref=srcs[w].at[pl.ds((1 - c) * h, h)], dst_ref=lands[w],
                                              send_sem=send.at[w], recv_sem=recv.at[w],
                                              device_id=(x, y, 1 - c), device_id_type=MESH)
            starts.append(cp)
            waits += [(cp, "recv"), (cp, "send")]
        return starts, waits

    return [stage], [(n, n, 0)]


def _scatter_stages(spec):
    n = len(spec)

    def stage(srcs, lands, sems):
        send, recv, local = sems
        x, y, c = _coords()
        me = 2 * x + y
        starts, waits = [], []
        for w, (_, axis) in enumerate(spec):
            layers = pl.ds(0, srcs[w].shape[0])
            own = _chip_window(axis, srcs[w], layers, me)
            lc = pltpu.make_async_copy(own, lands[w].at[me], local.at[w])
            starts.append(lc)
            for j, (cx, cy) in enumerate(_other_chips(x, y)):
                cp = pltpu.make_async_remote_copy(src_ref=_chip_window(axis, srcs[w], layers, 2 * cx + cy),
                                                  dst_ref=lands[w].at[me], send_sem=send.at[3 * w + j],
                                                  recv_sem=recv.at[3 * w + j], device_id=(cx, cy, c), device_id_type=MESH)
                starts.append(cp)
                waits.append((pltpu.make_async_remote_copy(
                    src_ref=own, dst_ref=lands[w].at[2 * cx + cy], send_sem=send.at[3 * w + j], recv_sem=recv.at[3 * w + j],
                    device_id=(x, y, c), device_id_type=MESH), "recv"))
                waits.append((cp, "send"))
            waits.append((lc, "local"))
        return starts, waits

    return [stage], [(3 * n, 3 * n, n)]


def _share_stages(spec):
    n = len(spec)

    def stage(srcs, lands, sems):
        send, recv, _ = sems
        x, y, c = _coords()
        starts, waits = [], []
        for w in range(n):
            h = lands[w].shape[0] // 2
            mine, theirs = lands[w].at[pl.ds(c * h, h)], lands[w].at[pl.ds((1 - c) * h, h)]
            cp = pltpu.make_async_remote_copy(src_ref=mine, dst_ref=mine, send_sem=send.at[w], recv_sem=recv.at[w],
                                              device_id=(x, y, 1 - c), device_id_type=MESH)
            starts.append(cp)
            waits.append((pltpu.make_async_remote_copy(src_ref=theirs, dst_ref=theirs, send_sem=send.at[w],
                                                       recv_sem=recv.at[w], device_id=(x, y, c), device_id_type=MESH),
                          "recv"))
            waits.append((cp, "send"))
        return starts, waits

    return [stage], [(n, n, 0)]


def _shard_of(p, axis):
    if axis is None:
        return (p.shape[0],) + tuple(p.shape[2:])
    s = list(p.shape)
    s[1 + axis] //= N_CHIPS
    return tuple(s)


def _gather8_stages():
    def stage(srcs, lands, sems):
        send, recv, local = sems
        x, y, c = _coords()
        me = 4 * x + 2 * y + c
        lc = pltpu.make_async_copy(srcs[0], lands[0].at[me], local.at[0])
        starts, waits = [lc], []
        for kk in range(1, 8):
            to = (1 - x if kk & 4 else x, 1 - y if kk & 2 else y, 1 - c if kk & 1 else c)
            cp = pltpu.make_async_remote_copy(src_ref=srcs[0], dst_ref=lands[0].at[me], send_sem=send.at[kk - 1],
                                              recv_sem=recv.at[kk - 1], device_id=to, device_id_type=MESH)
            starts.append(cp)
            waits.append((pltpu.make_async_remote_copy(
                src_ref=srcs[0], dst_ref=lands[0].at[4 * to[0] + 2 * to[1] + to[2]], send_sem=send.at[kk - 1],
                recv_sem=recv.at[kk - 1], device_id=(x, y, c), device_id_type=MESH), "recv"))
            waits.append((cp, "send"))
        waits.append((lc, "local"))
        return starts, waits

    return [stage], [(7, 7, 1)]


def _sum8(buf, name):
    _, rows, cols = buf.shape
    tr = _pick_rows(rows, cols * 4)

    def body(*refs):
        acc = refs[0][...]
        for r in refs[1:8]:
            acc = acc + r[...]
        refs[8][...] = acc

    return pl.pallas_call(
        body, out_shape=jax.ShapeDtypeStruct((rows, cols), F32), grid=(rows // tr,),
        in_specs=[pl.BlockSpec((None, tr, cols), lambda i, k=k: (k, i, 0)) for k in range(8)],
        out_specs=pl.BlockSpec((tr, cols), lambda i: (i, 0)),
        compiler_params=_params(("parallel",)), name=name)(*([buf] * 8))


def _reduce_begin(spec, gs, core, tag, riders=()):
    stages, counts = _swap_stages(spec)
    got = _comm_fused(stages, counts, list(gs) + list(riders),
                      [jax.ShapeDtypeStruct((g.shape[0] // 2,) + g.shape[1:], g.dtype) for g in gs], "swap_" + tag)
    pair = [_pair_sum(a, r, core, "pair_sum_" + n) for a, r, (n, _) in zip(gs, got, spec)]
    stages, counts = _scatter_stages(spec)
    lands = [lax.empty((N_CHIPS,) + _shard_of(p, axis), p.dtype) for p, (_, axis) in zip(pair, spec)]
    comm = _SplitComm(stages, counts, pair, lands, "scatter_" + tag)
    return comm, comm.advance()


def _reduce_finish(spec, comm, core, tag, after):
    comm.advance(after=after)
    halves = [_sum4(q, core, "sum4_" + n) for q, (n, _) in zip(comm.lands(), spec)]
    stages, counts = _share_stages(spec)
    return _comm_fused(stages, counts, [], halves, "share_" + tag, inplace=True)


SMALL = (("ssd_conv_w", 2), ("pool_scale", 1), ("ffn_conv_w", 2))
REPL = ("ssd_conv_b", "ssd_dt_bias", "ssd_a_log", "ssd_d", "ssd_norm_w", "ffn_conv_b",
        "norm_mix_pre", "norm_mix_post", "norm_ffn_pre", "norm_ffn_post")
WEIGHTS = ("ssd_w_in", "ssd_conv_w", "ssd_conv_b", "ssd_dt_bias", "ssd_a_log", "ssd_d", "ssd_norm_w", "ssd_w_out",
           "pool_w", "pool_scale", "ffn_w_up", "ffn_conv_w", "ffn_conv_b", "ffn_w_down", "norm_mix_pre",
           "norm_mix_post", "norm_ffn_pre", "norm_ffn_post")


def _flat_rows(n):
    unit = 2 * 16 * FLAT_COLS
    return 2 * 16 * ((n + unit - 1) // unit)


def _flatten_shards(arrs, dtype):
    flat = jnp.concatenate([a.astype(dtype).reshape(-1) for a in arrs])
    rows = _flat_rows(flat.shape[0])
    flat = jnp.pad(flat, (0, rows * FLAT_COLS - flat.shape[0]))
    return flat.reshape(2, rows // 2, FLAT_COLS)


def _unflatten_full(gathered, shard_shapes, axes):
    per_chip = jnp.swapaxes(gathered, 0, 1).reshape(N_CHIPS, -1)
    out, off = [], 0
    for shp, ax in zip(shard_shapes, axes):
        n = math.prod(shp)
        pieces = [per_chip[k, off:off + n].reshape(shp) for k in range(N_CHIPS)]
        out.append(jnp.concatenate(pieces, axis=ax))
        off += n
    return out


def kernel(x, ssd_w_in, ssd_conv_w, ssd_conv_b, ssd_dt_bias, ssd_a_log, ssd_d, ssd_norm_w, ssd_w_out, pool_w, pool_scale, ffn_w_up, ffn_conv_w, ffn_conv_b, ffn_w_down, norm_mix_pre, norm_mix_post, norm_ffn_pre, norm_ffn_post, loss_target, m_ssd_w_in, m_ssd_conv_w, m_ssd_conv_b, m_ssd_dt_bias, m_ssd_a_log, m_ssd_d, m_ssd_norm_w, m_ssd_w_out, m_pool_w, m_pool_scale, m_ffn_w_up, m_ffn_conv_w, m_ffn_conv_b, m_ffn_w_down, m_norm_mix_pre, m_norm_mix_post, m_norm_ffn_pre, m_norm_ffn_post, v_ssd_w_in, v_ssd_conv_w, v_ssd_conv_b, v_ssd_dt_bias, v_ssd_a_log, v_ssd_d, v_ssd_norm_w, v_ssd_w_out, v_pool_w, v_pool_scale, v_ffn_w_up, v_ffn_conv_w, v_ffn_conv_b, v_ffn_w_down, v_norm_mix_pre, v_norm_mix_post, v_norm_ffn_pre, v_norm_ffn_post):
    wts = dict(ssd_w_in=ssd_w_in, ssd_conv_w=ssd_conv_w, ssd_conv_b=ssd_conv_b, ssd_dt_bias=ssd_dt_bias,
               ssd_a_log=ssd_a_log, ssd_d=ssd_d, ssd_norm_w=ssd_norm_w, ssd_w_out=ssd_w_out, pool_w=pool_w,
               pool_scale=pool_scale, ffn_w_up=ffn_w_up, ffn_conv_w=ffn_conv_w, ffn_conv_b=ffn_conv_b,
               ffn_w_down=ffn_w_down, norm_mix_pre=norm_mix_pre, norm_mix_post=norm_mix_post,
               norm_ffn_pre=norm_ffn_pre, norm_ffn_post=norm_ffn_post)
    mom = dict(ssd_w_in=m_ssd_w_in, ssd_conv_w=m_ssd_conv_w, ssd_conv_b=m_ssd_conv_b, ssd_dt_bias=m_ssd_dt_bias,
               ssd_a_log=m_ssd_a_log, ssd_d=m_ssd_d, ssd_norm_w=m_ssd_norm_w, ssd_w_out=m_ssd_w_out, pool_w=m_pool_w,
               pool_scale=m_pool_scale, ffn_w_up=m_ffn_w_up, ffn_conv_w=m_ffn_conv_w, ffn_conv_b=m_ffn_conv_b,
               ffn_w_down=m_ffn_w_down, norm_mix_pre=m_norm_mix_pre, norm_mix_post=m_norm_mix_post,
               norm_ffn_pre=m_norm_ffn_pre, norm_ffn_post=m_norm_ffn_post)
    var = dict(ssd_w_in=v_ssd_w_in, ssd_conv_w=v_ssd_conv_w, ssd_conv_b=v_ssd_conv_b, ssd_dt_bias=v_ssd_dt_bias,
               ssd_a_log=v_ssd_a_log, ssd_d=v_ssd_d, ssd_norm_w=v_ssd_norm_w, ssd_w_out=v_ssd_w_out, pool_w=v_pool_w,
               pool_scale=v_pool_scale, ffn_w_up=v_ffn_w_up, ffn_conv_w=v_ffn_conv_w, ffn_conv_b=v_ffn_conv_b,
               ffn_w_down=v_ffn_w_down, norm_mix_pre=v_norm_mix_pre, norm_mix_post=v_norm_mix_post,
               norm_ffn_pre=v_norm_ffn_pre, norm_ffn_post=v_norm_ffn_post)

    bl, seq, d = x.shape
    t = bl * seq
    depth = norm_mix_pre.shape[0]
    n_ssd = ssd_w_out.shape[0]
    d_inner = ssd_w_out.shape[1] * N_CHIPS
    nheads = d_inner // HEAD_DIM
    hpg = nheads // N_GROUPS
    gw = d_inner // N_GROUPS
    xbc = ssd_conv_w.shape[2] * N_CHIPS
    f2 = ffn_w_up.shape[2] * N_CHIPS
    ff = f2 // 2
    dg = d // 4
    cy = lax.axis_index("c")
    chip = 2 * lax.axis_index("x") + lax.axis_index("y")

    small_shapes = [wts[n].shape for n, _ in SMALL]
    small_axes = [a for _, a in SMALL]
    small_flat = _flatten_shards([wts[n] for n, _ in SMALL], F32)
    small_half = lax.dynamic_index_in_dim(small_flat, cy, 0, keepdims=False)
    small_all = _allgather_halves(small_half, "gather_small")
    conv_w, p_scale, f_conv_w = _unflatten_full(small_all, small_shapes, small_axes)
    def full_shapes(spec, shards):
        return [jax.ShapeDtypeStruct(_full_shape(axis, s.shape), s.dtype) for s, (_, axis) in zip(shards, spec)]

    def row_halves(a):
        return a.reshape((2, a.shape[0] // 2) + a.shape[1:])

    def join_w_in(g):
        return jnp.concatenate([g[:, k] for k in range(N_CHIPS)], axis=-1).reshape(d, -1)

    def join_w_out(g):
        r2 = g.shape[1] // N_CHIPS
        return jnp.concatenate([g[hf, k * r2:(k + 1) * r2] for k in range(N_CHIPS) for hf in range(2)], axis=0)

    ssd_spec = (("ssd_w_in", None), ("ssd_w_out", 0))
    first_shards = [row_halves(wts[n][0].astype(BF16)) for n, _ in ssd_spec]
    stages, counts = _gather_stages(ssd_spec)
    g_in0, g_out0 = _comm_fused(stages, counts, first_shards, full_shapes(ssd_spec, first_shards), "gather_first")
    w_in, w_out = [join_w_in(g_in0)], [join_w_out(g_out0)]
    rest_spec = ssd_spec * (n_ssd - 1) + (("pool_w", 1),) + FFNW
    rest_shards = [row_halves(wts[n][jj].astype(BF16)) for jj in range(1, n_ssd) for n, _ in ssd_spec]
    rest_shards += [wts["pool_w"].astype(BF16)] + [wts[n].astype(BF16) for n, _ in FFNW]
    stages, counts = _gather_stages(rest_spec)
    ffn_gather = _SplitComm(stages, counts, rest_shards + [g_out0],
                            [lax.empty(s.shape, s.dtype) for s in full_shapes(rest_spec, rest_shards)], "gather_rest")
    gather_token = ffn_gather.advance()

    def pad_heads(a):
        lead = a.shape[:-1]
        a = a.reshape(lead + (N_GROUPS, hpg))
        a = jnp.pad(a, [(0, 0)] * len(lead) + [(0, 0), (0, LANES - hpg)])
        return a.reshape(lead + (N_GROUPS * LANES,))

    def unpad_heads(a):
        lead = a.shape[:-1]
        return a.reshape(lead + (N_GROUPS, LANES))[..., :hpg].reshape(lead + (nheads,))

    def group_rows(a, width):
        return jnp.broadcast_to(a.reshape(N_GROUPS, 1, width), (N_GROUPS, 8, width))

    def pad_w_in(w):
        return jnp.concatenate([w[..., :d_inner + xbc], pad_heads(w[..., d_inner + xbc:])], axis=-1)

    w_in_p = [pad_w_in(w_in[0])]
    zw = w_in_p[0].shape[-1]
    w_pool = None

    x2 = x.reshape(t, d)
    tgt2 = loss_target.reshape(t, d)
    w_up = w_down = None

    saved = []
    cur = x2
    tokens = []
    h = _norm_fwd(cur, norm_mix_pre[0:1], BF16, "norm_pre_b", after=[gather_token])
    for i in range(depth):
        j = i // 2
        sv = dict(x_in=cur)
        if i % 2 == 0:
            zx = _mm(h, w_in_p[j], "nn", F32, "mm_ssd_in", 2048, 512, d).reshape(bl, seq, zw)
            xc, xpre = _ssd_conv_fwd(zx, conv_w[j], ssd_conv_b[j:j + 1], d_inner, "ssd_conv_fwd")
            dtb = group_rows(pad_heads(ssd_dt_bias[j]), LANES)
            alog = group_rows(pad_heads(ssd_a_log[j]), LANES)
            dskip = group_rows(jnp.repeat(ssd_d[j], HEAD_DIM), gw)
            nw = group_rows(ssd_norm_w[j], gw)
            y, yn, st = _ssd_fwd(xc, zx, dtb, alog, dskip, nw, d_inner, "ssd_fwd")
            if i == 0:
                tokens.append(ffn_gather.advance(after=yn))
            mix = _mm(yn.reshape(t, d_inner), w_out[j], "nn", F32, "mm_ssd_out", 512, 512, d_inner)
            sv.update(h=h, zx=zx, xc=xc, xpre=xpre, y=y, yn=yn, st=st, dtb=dtb, alog=alog, dskip=dskip, nw=nw)
        else:
            mix = _pool_fwd(h.reshape(bl, seq, d), w_pool[j], p_scale[j:j + 1], "pool_fwd").reshape(t, d)
            sv.update(h=h)
        sv.update(mix=mix)
        mid, u = _norm_post_pre(mix, norm_mix_post[i:i + 1], cur, norm_ffn_pre[i:i + 1], BF16, "norm_post_pre_b",
                                after=tokens)
        tokens = []
        if i == 0:
            ffn_gather.advance(after=u)
            rest = ffn_gather.lands()
            for jj in range(1, n_ssd):
                w_in_p.append(pad_w_in(join_w_in(rest[2 * (jj - 1)])))
                w_out.append(join_w_out(rest[2 * (jj - 1) + 1]))
            w_pool, w_up, w_down = rest[2 * (n_ssd - 1):]
        hpre = _mm(u, w_up, "nn", BF16, "mm_up", 2048, 512, d, b_layer=i).reshape(bl, seq, f2)
        act, pre_g, pre_v = _ffn_act_fwd(hpre, f_conv_w[i], ffn_conv_b[i:i + 1], "ffn_act_fwd")
        act = act.reshape(t, ff)
        fo = _mm(act, w_down, "nn", F32, "mm_down", 1024, 512, ff, b_layer=i)
        if i + 1 == depth:
            cur = _norm_fwd(fo, norm_ffn_post[i:i + 1], F32, "norm_post", resid=mid)
        elif i % 2 == 0:
            cur, h = _norm_post_pre(fo, norm_ffn_post[i:i + 1], mid, norm_mix_pre[i + 1:i + 2], F32, "norm_post_pre_f")
        else:
            cur, h = _norm_post_pre(fo, norm_ffn_post[i:i + 1], mid, norm_mix_pre[i + 1:i + 2], BF16, "norm_post_pre_b")
        sv.update(mid=mid, u=u, hpre=hpre, pre_g=pre_g, pre_v=pre_v, act=act, fo=fo)
        saved.append(sv)

    dcur, loss_part = _loss_head(cur, tgt2, "loss_head")

    g = {n: [None] * wts[n].shape[0] for n in WEIGHTS}
    gbuf = dict(up=lax.empty((depth, d, f2), F32), down=lax.empty((depth, ff, d), F32),
                out=lax.empty((n_ssd, d_inner, d), F32), win=lax.empty((n_ssd, d, zw), F32))
    core = cy.reshape(1).astype(jnp.int32)

    def mixer_bwd(i, dmid, behind=()):
        j = i // 2
        sv = saved[i]
        done = []
        if i % 2 == 0:
            dmix, g["norm_mix_post"][i] = _norm_bwd(sv["mix"], norm_mix_post[i:i + 1], dmid, BF16, "norm_bwd_b",
                                                    after=behind)
            dyn = _mm(dmix, w_out[j], "nt", F32, "mm_ssd_out_dx", 1024, 1024, d)
            gbuf["out"], tok = _mm(sv["yn"].reshape(t, d_inner), dmix, "tn", F32, "mm_ssd_out_dw", 1024, 1024, 2048,
                                   out_buf=(gbuf["out"], j))
            done.append(tok)
            dz, dxs, dbm, dcm, ddt, dnw, dd, dal, dbias = _ssd_bwd(
                sv["xc"], sv["zx"], sv["y"], dyn.reshape(bl, seq, d_inner), sv["st"], sv["dtb"], sv["alog"],
                sv["dskip"], sv["nw"], d_inner, "ssd_bwd")
            g["ssd_norm_w"][j] = dnw[:, 0, :].reshape(d_inner)
            g["ssd_d"][j] = dd[:, 0, :hpg].reshape(nheads)
            g["ssd_a_log"][j] = dal[:, 0, :hpg].reshape(nheads)
            g["ssd_dt_bias"][j] = dbias[:, 0, :hpg].reshape(nheads)
            dzx, dcw, dcb = _ssd_conv_bwd(sv["zx"], sv["xpre"], (dxs, dbm, dcm), ddt, dz, conv_w[j], d_inner,
                                          "ssd_conv_bwd")
            g["ssd_conv_w"][j] = dcw
            g["ssd_conv_b"][j] = dcb[0]
            dzx = dzx.reshape(t, zw)
            dh = _mm(dzx, w_in_p[j], "nt", BF16, "mm_ssd_in_dx", 1024, d, zw // 2)
            gbuf["win"], tok = _mm(sv["h"], dzx, "tn", F32, "mm_ssd_in_dw", 1024, zw // 4, 2048, out_buf=(gbuf["win"], j))
            done.append(tok)
        else:
            dmix, g["norm_mix_post"][i] = _norm_bwd(sv["mix"], norm_mix_post[i:i + 1], dmid, F32, "norm_bwd_f",
                                                    after=behind)
            dh3, g["pool_w"][j], dps = _pool_bwd(sv["h"].reshape(bl, seq, d), dmix.reshape(bl, seq, d), w_pool[j],
                                                 p_scale[j:j + 1], "pool_bwd")
            g["pool_scale"][j] = dps[0]
            dh = dh3.reshape(t, d)
        dx_in, g["norm_mix_pre"][i] = _norm_bwd(sv["x_in"], norm_mix_pre[i:i + 1], dh, F32, "norm_bwd_r", resid=dmid,
                                                after=done)
        return dx_in

    ffn_comm = None
    for i in reversed(range(depth)):
        sv = saved[i]
        dfo, g["norm_ffn_post"][i] = _norm_bwd(sv["fo"], norm_ffn_post[i:i + 1], dcur, BF16, "norm_bwd_b")
        dact = _mm(dfo, w_down, "nt", BF16, "mm_down_dx", 1024, ff // 2, d, b_layer=i)
        gbuf["down"], tok_down = _mm(sv["act"], dfo, "tn", F32, "mm_down_dw", ff // 2, 1024, 2048,
                                     out_buf=(gbuf["down"], i))
        dhg, dhv, dcw, dcb = _ffn_act_bwd(sv["hpre"], sv["pre_g"], sv["pre_v"], dact.reshape(bl, seq, ff), f_conv_w[i],
                                          "ffn_act_bwd")
        g["ffn_conv_w"][i] = dcw
        g["ffn_conv_b"][i] = dcb[0]
        dhs = [dhg.reshape(t, ff), dhv.reshape(t, ff)]
        du = _mm(dhs, w_up, "nt", BF16, "mm_up_dx", 1024, d, ff, b_layer=i)
        gbuf["up"], tok_up = _mm(sv["u"], dhs, "tn", F32, "mm_up_dw", 1024, ff // 2, 2048, out_buf=(gbuf["up"], i))
        dmid, g["norm_ffn_pre"][i] = _norm_bwd(sv["mid"], norm_ffn_pre[i:i + 1], du, F32, "norm_bwd_r", resid=dcur,
                                               after=[tok_down, tok_up])
        if i > 0:
            dcur = mixer_bwd(i, dmid)
        else:
            ffn_comm, ffn_token = _reduce_begin(FFNW, [gbuf["up"], gbuf["down"]], core, "ffn")
            dcur = mixer_bwd(0, dmid, behind=[ffn_token])

    grad_x = dcur.reshape(bl, seq, d)
    for n in ("norm_mix_pre", "norm_mix_post", "norm_ffn_pre", "norm_ffn_post"):
        g[n] = [a[0] for a in g[n]]
    small_names = [n for n, _ in SMALL] + list(REPL)
    full = {n: jnp.stack(g[n], axis=0) for n in small_names}

    g_in = jnp.concatenate([gbuf["win"][..., :d_inner + xbc], unpad_heads(gbuf["win"][..., d_inner + xbc:])], axis=-1)
    g_in_cm = jnp.swapaxes(g_in.reshape(n_ssd, d, N_CHIPS, -1), 1, 2)
    vec = jnp.concatenate([full[n].reshape(-1) for n in small_names] + [loss_part[0, :1]])
    nvec = vec.shape[0]
    vrows = 16 * ((nvec + 16 * FLAT_COLS - 1) // (16 * FLAT_COLS))
    vec = jnp.pad(vec, (0, vrows * FLAT_COLS - nvec)).reshape(vrows, FLAT_COLS)
    stages, counts = _gather8_stages()
    small_comm = _SplitComm(stages, counts, [vec], [lax.empty((8, vrows, FLAT_COLS), F32)], "gather_small_grads")
    small_token = small_comm.advance()
    mix_comm, mix_token = _reduce_begin(MIXW, [g_in_cm, gbuf["out"], jnp.stack(g["pool_w"], axis=0)], core, "mixers",
                                        riders=[small_token])

    grads, deltas, new_m, new_v = {}, {}, {}, {}

    def adamw(n, gr):
        shp = wts[n].shape
        two = (math.prod(shp[:-1]), shp[-1])
        dl, mn, vn = _adamw(wts[n].reshape(two), gr.reshape(two), mom[n].reshape(two), var[n].reshape(two),
                            "adamw_" + n)
        grads[n], deltas[n], new_m[n], new_v[n] = gr, dl.reshape(shp), mn.reshape(shp), vn.reshape(shp)
        return dl

    small_comm.advance(after=mix_token)
    tot = _sum8(small_comm.lands()[0], "sum_small").reshape(-1)
    small_grads, off = {}, 0
    for n in small_names:
        cnt = math.prod(full[n].shape)
        small_grads[n] = tot[off:off + cnt].reshape(full[n].shape)
        off += cnt
    loss = tot[off]
    for n, ax in SMALL:
        w = wts[n].shape[ax]
        small_grads[n] = lax.dynamic_slice_in_dim(small_grads[n], chip * w, w, axis=ax)

    behind = [adamw(n, small_grads[n]) for n in small_names][-1:]
    ffn_grads = _reduce_finish(FFNW, ffn_comm, core, "ffn", after=mix_token)
    behind += [adamw(n, gr) for gr, (n, _) in zip(ffn_grads, FFNW)]
    mix_grads = _reduce_finish(MIXW, mix_comm, core, "mixers", after=behind)
    for gr, (n, _) in zip(mix_grads, MIXW):
        adamw(n, gr)

    return (loss, grad_x, *[grads[n] for n in WEIGHTS], *[deltas[n] for n in WEIGHTS],
            *[new_m[n] for n in WEIGHTS], *[new_v[n] for n in WEIGHTS])
```

```python
import functools
import math

import jax
import jax.numpy as jnp
from jax import lax
from jax.experimental import pallas as pl
from jax.experimental.pallas import tpu as pltpu

F32 = jnp.float32
BF16 = jnp.bfloat16
MESH = pl.DeviceIdType.MESH
ANY = pl.BlockSpec(memory_space=pl.ANY)

HEAD_DIM = 64
D_STATE = 128
CHUNK = 128
N_GROUPS = 4
SSD_CONV = 4
FFN_CONV = 3
EPS = 1e-6
N_CHIPS = 4
LANES = 128
FLAT_COLS = 1024

ADAM_LR = 0.001
ADAM_B1 = 0.9
ADAM_B2 = 0.999
ADAM_EPS = 1e-08
ADAM_WD = 0.01
ADAM_STEP = 10

VMEM_LIMIT_BYTES = 56 * 1024 * 1024


def _params(sem=None):
    kw = dict(vmem_limit_bytes=VMEM_LIMIT_BYTES)
    if sem is not None:
        kw["dimension_semantics"] = sem
    return pltpu.CompilerParams(**kw)


def _sigmoid(x):
    return 0.5 * jnp.tanh(0.5 * x) + 0.5


def _softplus(x):
    return jnp.maximum(x, 0.0) + jnp.log(1.0 + jnp.exp(-jnp.abs(x)))


def _dot(a, b, dn):
    return lax.dot_general(a, b, (dn, ((), ())), preferred_element_type=F32)


def _nn(a, b):
    return _dot(a, b, ((1,), (0,)))


def _nt(a, b):
    return _dot(a, b, ((1,), (1,)))


def _tn(a, b):
    return _dot(a, b, ((0,), (0,)))


def _split(x, parts):
    out = []
    r = x
    for _ in range(parts):
        p = r.astype(BF16)
        out.append(p)
        r = r - p.astype(F32)
    return out


def _sel_left(sel, x, parts=3):
    n = x.shape[1]
    r = _nn(sel, jnp.concatenate(_split(x, parts), axis=1))
    out = r[:, 0:n]
    for i in range(1, parts):
        out = out + r[:, i * n:(i + 1) * n]
    return out


def _sel_right(x, sel_stacked, parts=3):
    return _nn(jnp.concatenate(_split(x, parts), axis=1), sel_stacked)


def _mm(a, b, dims, out_dtype, name, tm, tn, tk, b_layer=None, out_buf=None):
    a_list = list(a) if isinstance(a, (list, tuple)) else [a]
    b_list = list(b) if isinstance(b, (list, tuple)) else [b]
    if dims in ("nn", "nt"):
        assert len(b_list) == 1
        m = a_list[0].shape[0]
        segs = [x.shape[1] for x in a_list]
        k = sum(segs)
        bshape = b_list[0].shape[-2:]
        n = bshape[1] if dims == "nn" else bshape[0]
        assert (bshape[0] if dims == "nn" else bshape[1]) == k
    else:
        assert len(a_list) == 1 and b_layer is None
        k, m = a_list[0].shape
        segs = [x.shape[1] for x in b_list]
        n = sum(segs)
    nseg = len(segs)
    tm, tn = min(tm, m), min(tn, n)
    if dims == "tn":
        tk = min(tk, k)
        tn = min(tn, min(segs))
        units = [tn] * nseg
        nk = k // tk
        assert k % tk == 0
    else:
        units = [min(u, s) for u, s in zip(tk if isinstance(tk, (list, tuple)) else [tk] * nseg, segs)]
        nk = sum(s // u for s, u in zip(segs, units))
    assert m % tm == 0 and n % tn == 0 and all(s % u == 0 for s, u in zip(segs, units)), (name, m, n, k, segs, units)
    counts = [s // u for s, u in zip(segs, units)]
    starts = [sum(counts[:s]) for s in range(nseg)]
    assert all(sum(segs[:s]) % units[s] == 0 for s in range(nseg)), (name, segs, units)
    first_block = [sum(segs[:s]) // units[s] for s in range(nseg)]
    dn = {"nn": ((1,), (0,)), "nt": ((1,), (1,)), "tn": ((0,), (0,))}[dims]

    same = len(set(units)) == 1
    nb_ops = len(b_list) if dims == "tn" else (1 if same else nseg)

    def body(*refs):
        a_refs = refs[:len(a_list)]
        b_refs = refs[len(a_list):len(a_list) + nb_ops]
        rest = refs[len(a_list) + nb_ops + (0 if out_buf is None else 1):]
        o_ref = rest[0]
        if out_buf is not None:
            rest[1][...] = jnp.zeros((8, LANES), F32)
            rest = rest[1:]
        acc = rest[1] if nk > 1 else None
        kk = pl.program_id(2)
        sel = kk if dims != "tn" else pl.program_id(1)

        def step(a_ref, b_ref):
            p = _dot(a_ref[...].astype(BF16), b_ref[...].astype(BF16), dn)
            if nk == 1:
                o_ref[...] = p.astype(out_dtype)
                return

            @pl.when(kk == 0)
            def _():
                acc[...] = p

            @pl.when(kk > 0)
            def _():
                acc[...] += p

        if nseg == 1:
            step(a_refs[0], b_refs[0])
        else:
            for s in range(nseg):
                @pl.when(jnp.logical_and(sel >= starts[s], sel < starts[s] + counts[s]))
                def _(s=s):
                    step(a_refs[s] if dims != "tn" else a_refs[0], b_refs[s if nb_ops > 1 else 0])

        if nk > 1:
            @pl.when(kk == nk - 1)
            def _():
                o_ref[...] = acc[...].astype(out_dtype)

    def seg_index(v, s):
        return v if nseg == 1 else jnp.clip(v - starts[s], 0, counts[s] - 1)

    lead = () if b_layer is None else (b_layer,)
    none = () if b_layer is None else (None,)
    def b_block(kk, s):
        return kk if same else first_block[s] + seg_index(kk, s)

    if dims == "nn":
        a_specs = [pl.BlockSpec((tm, units[s]), lambda i, j, kk, s=s: (i, seg_index(kk, s))) for s in range(nseg)]
        b_specs = [pl.BlockSpec(none + (units[s], tn), lambda i, j, kk, s=s: lead + (b_block(kk, s), j))
                   for s in range(nb_ops)]
    elif dims == "nt":
        a_specs = [pl.BlockSpec((tm, units[s]), lambda i, j, kk, s=s: (i, seg_index(kk, s))) for s in range(nseg)]
        b_specs = [pl.BlockSpec(none + (tn, units[s]), lambda i, j, kk, s=s: lead + (j, b_block(kk, s)))
                   for s in range(nb_ops)]
    else:
        a_specs = [pl.BlockSpec((tk, tm), lambda i, j, kk: (kk, i))]
        b_specs = [pl.BlockSpec((tk, tn), lambda i, j, kk, s=s: (kk, seg_index(j, s))) for s in range(nseg)]
    args = a_list + (b_list * nb_ops if dims != "tn" else b_list)
    in_specs = a_specs + b_specs
    aliases = {}
    if out_buf is None:
        out_shape = jax.ShapeDtypeStruct((m, n), out_dtype)
        out_spec = pl.BlockSpec((tm, tn), lambda i, j, kk: (i, j))
    else:
        buf, slab = out_buf
        assert buf.shape[1:] == (m, n) and buf.dtype == out_dtype
        out_shape = (jax.ShapeDtypeStruct(buf.shape, out_dtype), jax.ShapeDtypeStruct((8, LANES), F32))
        out_spec = (pl.BlockSpec((None, tm, tn), lambda i, j, kk: (slab, i, j)),
                    pl.BlockSpec((8, LANES), lambda i, j, kk: (0, 0)))
        aliases = {len(args): 0}
        args = args + [buf]
        in_specs = in_specs + [ANY]
    return pl.pallas_call(
        body,
        out_shape=out_shape,
        grid=(m // tm, n // tn, nk),
        in_specs=in_specs,
        out_specs=out_spec,
        scratch_shapes=[] if nk == 1 else [pltpu.VMEM((tm, tn), F32)],
        input_output_aliases=aliases,
        compiler_params=_params(("parallel", "parallel", "arbitrary") if out_buf is None else ("arbitrary",) * 3),
        name=name,
    )(*args)


def _row_tile(t, want):
    tm = min(want, t)
    assert t % tm == 0
    return tm


def _norm_fwd(x, w, out_dtype, name, resid=None, after=()):
    t, d = x.shape
    tm = _row_tile(t, 512)
    after = [a for a in after if a is not None]

    def body(*refs):
        refs = refs[:len(refs) - 1 - len(after)] + refs[len(refs) - 1:]
        if resid is None:
            x_ref, w_ref, o_ref = refs
        else:
            x_ref, w_ref, r_ref, o_ref = refs
        xv = x_ref[...]
        r = lax.rsqrt(jnp.mean(xv * xv, axis=-1, keepdims=True) + EPS)
        y = (xv * r) * w_ref[...]
        if resid is not None:
            y = r_ref[...] + y
        o_ref[...] = y.astype(out_dtype)

    row = pl.BlockSpec((tm, d), lambda i: (i, 0))
    vec = pl.BlockSpec((1, d), lambda i: (0, 0))
    args = [x, w] + ([] if resid is None else [resid]) + after
    return pl.pallas_call(
        body, out_shape=jax.ShapeDtypeStruct((t, d), out_dtype), grid=(t // tm,),
        in_specs=[row, vec] + ([] if resid is None else [row]) + [ANY] * len(after), out_specs=row,
        compiler_params=_params(("parallel",)), name=name)(*args)


def _norm_post_pre(m, w_post, resid, w_pre, pre_dtype, name, after=()):
    t, d = m.shape
    tm = _row_tile(t, 512)
    after = [a for a in after if a is not None]

    def body(m_ref, w1_ref, r_ref, w2_ref, *rest):
        x_ref, u_ref = rest[len(after):]
        mv = m_ref[...]
        r1 = lax.rsqrt(jnp.mean(mv * mv, axis=-1, keepdims=True) + EPS)
        xv = r_ref[...] + (mv * r1) * w1_ref[...]
        x_ref[...] = xv
        r2 = lax.rsqrt(jnp.mean(xv * xv, axis=-1, keepdims=True) + EPS)
        u_ref[...] = ((xv * r2) * w2_ref[...]).astype(pre_dtype)

    row = pl.BlockSpec((tm, d), lambda i: (i, 0))
    vec = pl.BlockSpec((1, d), lambda i: (0, 0))
    return pl.pallas_call(
        body, out_shape=(jax.ShapeDtypeStruct((t, d), F32), jax.ShapeDtypeStruct((t, d), pre_dtype)), grid=(t // tm,),
        in_specs=[row, vec, row, vec] + [ANY] * len(after), out_specs=(row, row),
        compiler_params=_params(("parallel",)), name=name)(m, w_post, resid, w_pre, *after)


def _norm_bwd(src, w, dy, out_dtype, name, resid=None, after=()):
    t, d = src.shape
    tm = _row_tile(t, 512)
    after = [a for a in after if a is not None]

    def body(*refs):
        refs = refs[:len(refs) - 2 - len(after)] + refs[len(refs) - 2:]
        if resid is None:
            x_ref, w_ref, g_ref, o_ref, dw_ref = refs
        else:
            x_ref, w_ref, g_ref, r_ref, o_ref, dw_ref = refs
        xv = x_ref[...]
        g = g_ref[...].astype(F32)
        r = lax.rsqrt(jnp.mean(xv * xv, axis=-1, keepdims=True) + EPS)
        xh = xv * r
        gh = g * w_ref[...]
        mean = jnp.mean(gh * xh, axis=-1, keepdims=True)
        dx = r * (gh - xh * mean)
        if resid is not None:
            dx = r_ref[...] + dx
        o_ref[...] = dx.astype(out_dtype)
        part = jnp.sum(g * xh, axis=0, keepdims=True)

        @pl.when(pl.program_id(0) == 0)
        def _():
            dw_ref[...] = part

        @pl.when(pl.program_id(0) > 0)
        def _():
            dw_ref[...] += part

    row = pl.BlockSpec((tm, d), lambda i: (i, 0))
    vec = pl.BlockSpec((1, d), lambda i: (0, 0))
    args = [src, w, dy] + ([] if resid is None else [resid]) + after
    return pl.pallas_call(
        body,
        out_shape=(jax.ShapeDtypeStruct((t, d), out_dtype), jax.ShapeDtypeStruct((1, d), F32)),
        grid=(t // tm,),
        in_specs=[row, vec, row] + ([] if resid is None else [row]) + [ANY] * len(after),
        out_specs=(row, vec),
        compiler_params=_params(("arbitrary",)), name=name)(*args)


def _loss_head(y, target, name):
    t, d = y.shape
    tm = _row_tile(t, 512)

    def body(y_ref, t_ref, dy_ref, l_ref):
        e = y_ref[...] - t_ref[...]
        dy_ref[...] = e * (1.0 / d)
        col = jnp.sum(e * e, axis=0, keepdims=True)
        s = jnp.sum(col, axis=1, keepdims=True) * (0.5 / d)
        part = jnp.broadcast_to(s, (1, LANES))

        @pl.when(pl.program_id(0) == 0)
        def _():
            l_ref[...] = part

        @pl.when(pl.program_id(0) > 0)
        def _():
            l_ref[...] += part

    row = pl.BlockSpec((tm, d), lambda i: (i, 0))
    return pl.pallas_call(
        body,
        out_shape=(jax.ShapeDtypeStruct((t, d), F32), jax.ShapeDtypeStruct((1, LANES), F32)),
        grid=(t // tm,), in_specs=[row, row],
        out_specs=(row, pl.BlockSpec((1, LANES), lambda i: (0, 0))),
        compiler_params=_params(("arbitrary",)), name=name)(y, target)


def _window(ref, c, rows, seq, before, after):
    r0 = pl.multiple_of(c * rows, rows)
    parts = []
    if before:
        h0 = pl.multiple_of(jnp.maximum(r0 - before, 0), before)
        halo = ref[pl.ds(h0, before), :].astype(F32)
        parts.append(jnp.where(c > 0, halo, 0.0))
    parts.append(ref[pl.ds(r0, rows), :].astype(F32))
    if after:
        h1 = pl.multiple_of(jnp.minimum(r0 + rows, seq - after), after)
        halo = ref[pl.ds(h1, after), :].astype(F32)
        parts.append(jnp.where(c < seq // rows - 1, halo, 0.0))
    return parts[0] if len(parts) == 1 else jnp.concatenate(parts, axis=0)


def _lag(x, k):
    return pltpu.roll(x, k, 0) if k else x


def _lead(x, k):
    return pltpu.roll(x, x.shape[0] - k, 0) if k else x


SHIFT_ROWS = 128
SHIFT_COLS = 256


HALO = 16


def _conv3(ext, w, bias):
    acc = bias + w[2:3, :] * ext[HALO:, :]
    acc = acc + w[1:2, :] * _lag(ext, 1)[HALO:, :]
    return acc + w[0:1, :] * _lag(ext, 2)[HALO:, :]


def _ffn_act_fwd(hpre, cw, cb, name):
    b, seq, f2 = hpre.shape
    cbk = SHIFT_COLS
    nj = f2 // (2 * cbk)
    rows = min(SHIFT_ROWS, seq)

    def body(g_ref, v_ref, wg_ref, wv_ref, bg_ref, bv_ref, o_ref, pg_ref, pv_ref):
        def chunk(c, carry):
            gate = _conv3(_window(g_ref, c, rows, seq, HALO, 0), wg_ref[...], bg_ref[...])
            val = _conv3(_window(v_ref, c, rows, seq, HALO, 0), wv_ref[...], bv_ref[...])
            a = gate * _sigmoid(gate) * val
            here = pl.ds(pl.multiple_of(c * rows, rows), rows)
            o_ref[here, :] = a.astype(BF16)
            pg_ref[here, :] = gate.astype(BF16)
            pv_ref[here, :] = val.astype(BF16)
            return carry

        lax.fori_loop(0, seq // rows, chunk, 0)

    blk = lambda off: pl.BlockSpec((None, seq, cbk), lambda i, j: (i, 0, j + off))
    wsp = lambda r, off: pl.BlockSpec((r, cbk), lambda i, j: (0, j + off))
    half = jax.ShapeDtypeStruct((b, seq, f2 // 2), BF16)
    return pl.pallas_call(
        body, out_shape=(half, half, half), grid=(b, nj),
        in_specs=[blk(0), blk(nj), wsp(FFN_CONV, 0), wsp(FFN_CONV, nj), wsp(1, 0), wsp(1, nj)],
        out_specs=(blk(0), blk(0), blk(0)),
        compiler_params=_params(("parallel", "parallel")), name=name)(hpre, hpre, cw, cw, cb, cb)


def _ffn_act_bwd(hpre, pre_g, pre_v, da, cw, name):
    b, seq, f2 = hpre.shape
    cbk = SHIFT_COLS
    nj = f2 // (2 * cbk)
    rows = min(SHIFT_ROWS, seq)

    def body(g_ref, v_ref, pg_ref, pv_ref, da_ref, wg_ref, wv_ref, og_ref, ov_ref, dwg_ref, dwv_ref, dbg_ref, dbv_ref):
        wg, wv = wg_ref[...], wv_ref[...]

        def back(dpre, w, o_ref, x_ref, c, carry):
            here = pl.ds(pl.multiple_of(c * rows, rows), rows)
            leads = [dpre, _lead(dpre, 1), _lead(dpre, 2)]
            dx = w[2:3, :] * leads[0] + w[1:2, :] * leads[1] + w[0:1, :] * leads[2]
            o_ref[here, :] = dx[:rows, :].astype(BF16)
            x0 = x_ref[here, :].astype(F32)
            return tuple(carry[k] + jnp.sum(leads[k][:rows, :] * x0, axis=0, keepdims=True) for k in range(FFN_CONV)) + (
                carry[FFN_CONV] + jnp.sum(dpre[:rows, :], axis=0, keepdims=True),)

        def chunk(c, carry):
            cg, cv = carry
            gate = _window(pg_ref, c, rows, seq, 0, HALO)
            val = _window(pv_ref, c, rows, seq, 0, HALO)
            dav = _window(da_ref, c, rows, seq, 0, HALO)
            sg = _sigmoid(gate)
            cg = back(dav * val * (sg * (1.0 + gate * (1.0 - sg))), wg, og_ref, g_ref, c, cg)
            cv = back(dav * (gate * sg), wv, ov_ref, v_ref, c, cv)
            return cg, cv

        z = jnp.zeros((1, cbk), F32)
        cg, cv = lax.fori_loop(0, seq // rows, chunk, ((z,) * (FFN_CONV + 1), (z,) * (FFN_CONV + 1)))
        dwg = jnp.concatenate([cg[2], cg[1], cg[0]], axis=0)
        dwv = jnp.concatenate([cv[2], cv[1], cv[0]], axis=0)

        @pl.when(pl.program_id(1) == 0)
        def _():
            dwg_ref[...] = dwg
            dwv_ref[...] = dwv
            dbg_ref[...] = cg[FFN_CONV]
            dbv_ref[...] = cv[FFN_CONV]

        @pl.when(pl.program_id(1) > 0)
        def _():
            dwg_ref[...] += dwg
            dwv_ref[...] += dwv
            dbg_ref[...] += cg[FFN_CONV]
            dbv_ref[...] += cv[FFN_CONV]

    blk = lambda off: pl.BlockSpec((None, seq, cbk), lambda j, i: (i, 0, j + off))
    wsp = lambda r, off: pl.BlockSpec((r, cbk), lambda j, i: (0, j + off))
    half = jax.ShapeDtypeStruct((b, seq, f2 // 2), BF16)
    dwshape = jax.ShapeDtypeStruct((FFN_CONV, f2 // 2), F32)
    dbshape = jax.ShapeDtypeStruct((1, f2 // 2), F32)
    dg, dv, dwg, dwv, dbg, dbv = pl.pallas_call(
        body,
        out_shape=(half, half, dwshape, dwshape, dbshape, dbshape),
        grid=(nj, b),
        in_specs=[blk(0), blk(nj), blk(0), blk(0), blk(0), wsp(FFN_CONV, 0), wsp(FFN_CONV, nj)],
        out_specs=(blk(0), blk(0), wsp(FFN_CONV, 0), wsp(FFN_CONV, 0), wsp(1, 0), wsp(1, 0)),
        compiler_params=_params(("parallel", "arbitrary")), name=name)(hpre, hpre, pre_g, pre_v, da, cw, cw)
    return dg, dv, jnp.concatenate([dwg, dwv], axis=1), jnp.concatenate([dbg, dbv], axis=1)


def _ssd_conv_fwd(zx, cw, cb, d_inner, name):
    b, seq, _ = zx.shape
    xbc = cw.shape[1]
    cbk = SHIFT_COLS
    off = d_inner // cbk
    rows = min(SHIFT_ROWS, seq)

    def body(h_ref, w_ref, b_ref, o_ref, p_ref):
        w = w_ref[...]
        bias = b_ref[...]

        def chunk(c, carry):
            ext = _window(h_ref, c, rows, seq, HALO, 0)
            acc = bias + w[3:4, :] * ext[HALO:, :]
            for k in range(1, SSD_CONV):
                acc = acc + w[3 - k:4 - k, :] * _lag(ext, k)[HALO:, :]
            here = pl.ds(pl.multiple_of(c * rows, rows), rows)
            o_ref[here, :] = acc * _sigmoid(acc)
            p_ref[here, :] = acc.astype(BF16)
            return carry

        lax.fori_loop(0, seq // rows, chunk, 0)

    blk = pl.BlockSpec((None, seq, cbk), lambda i, j: (i, 0, j))
    return pl.pallas_call(
        body, out_shape=(jax.ShapeDtypeStruct((b, seq, xbc), F32), jax.ShapeDtypeStruct((b, seq, xbc), BF16)),
        grid=(b, xbc // cbk),
        in_specs=[pl.BlockSpec((None, seq, cbk), lambda i, j: (i, 0, j + off)),
                  pl.BlockSpec((SSD_CONV, cbk), lambda i, j: (0, j)),
                  pl.BlockSpec((1, cbk), lambda i, j: (0, j))],
        out_specs=(blk, blk),
        compiler_params=_params(("parallel", "parallel")), name=name)(zx, cw, cb)


def _ssd_conv_bwd(zx, pre, dparts, ddt, dzx, cw, d_inner, name):
    b, seq, zw = zx.shape
    xbc = cw.shape[1]
    cbk = SHIFT_COLS
    off = d_inner // cbk
    rows = min(SHIFT_ROWS, seq)
    nblk = [p.shape[2] // cbk for p in dparts]
    first = [sum(nblk[:s]) for s in range(len(dparts))]
    nconv = xbc // cbk
    ncopy = ddt.shape[2] // cbk
    assert sum(nblk) == nconv and (off + nconv + ncopy) * cbk == zw and dzx.shape == (b, seq, zw)

    def body(h_ref, p_ref, gx_ref, gb_ref, gc_ref, t_ref, w_ref, z_ref, o_ref, dw_ref, db_ref):
        j = pl.program_id(0)

        @pl.when(j < nconv)
        def _():
            conv(h_ref, p_ref, gx_ref, gb_ref, gc_ref, w_ref, o_ref, dw_ref, db_ref)

        @pl.when(j >= nconv)
        def _():
            o_ref[...] = t_ref[...]

    def conv(h_ref, p_ref, gx_ref, gb_ref, gc_ref, w_ref, o_ref, dw_ref, db_ref):
        w = w_ref[...]
        j = pl.program_id(0)

        def chunk(c, carry):
            dws, dbias = carry
            here = pl.ds(pl.multiple_of(c * rows, rows), rows)
            pre = _window(p_ref, c, rows, seq, 0, HALO)
            s = _sigmoid(pre)
            gsel = jnp.where(j < first[1], _window(gx_ref, c, rows, seq, 0, HALO),
                             jnp.where(j < first[2], _window(gb_ref, c, rows, seq, 0, HALO),
                                       _window(gc_ref, c, rows, seq, 0, HALO)))
            dpre = gsel * (s * (1.0 + pre * (1.0 - s)))
            leads = [dpre] + [_lead(dpre, k) for k in range(1, SSD_CONV)]
            dx = w[3:4, :] * leads[0]
            for k in range(1, SSD_CONV):
                dx = dx + w[3 - k:4 - k, :] * leads[k]
            o_ref[here, :] = dx[:rows, :].astype(BF16)
            x0 = h_ref[here, :].astype(F32)
            dws = tuple(dws[k] + jnp.sum(leads[k][:rows, :] * x0, axis=0, keepdims=True) for k in range(SSD_CONV))
            dbias = dbias + jnp.sum(dpre[:rows, :], axis=0, keepdims=True)
            return dws, dbias

        z = jnp.zeros((1, cbk), F32)
        dws, dbias = lax.fori_loop(0, seq // rows, chunk, ((z,) * SSD_CONV, z))
        dwv = jnp.concatenate([dws[3 - i] for i in range(SSD_CONV)], axis=0)

        @pl.when(pl.program_id(1) == 0)
        def _():
            dw_ref[...] = dwv
            db_ref[...] = dbias

        @pl.when(pl.program_id(1) > 0)
        def _():
            dw_ref[...] += dwv
            db_ref[...] += dbias

    conv_j = lambda j: jnp.minimum(j, nconv - 1)
    return pl.pallas_call(
        body,
        out_shape=(jax.ShapeDtypeStruct((b, seq, zw), BF16), jax.ShapeDtypeStruct((SSD_CONV, xbc), F32),
                   jax.ShapeDtypeStruct((1, xbc), F32)),
        grid=(nconv + ncopy, b),
        in_specs=[pl.BlockSpec((None, seq, cbk), lambda j, i: (i, 0, conv_j(j) + off)),
                  pl.BlockSpec((None, seq, cbk), lambda j, i: (i, 0, conv_j(j)))] + [
                  pl.BlockSpec((None, seq, cbk), lambda j, i, s=s: (i, 0, jnp.clip(j - first[s], 0, nblk[s] - 1)))
                  for s in range(3)] + [
                  pl.BlockSpec((None, seq, cbk), lambda j, i: (i, 0, jnp.clip(j - nconv, 0, ncopy - 1))),
                  pl.BlockSpec((SSD_CONV, cbk), lambda j, i: (0, conv_j(j))),
                  ANY],
        out_specs=(pl.BlockSpec((None, seq, cbk), lambda j, i: (i, 0, j + off)),
                   pl.BlockSpec((SSD_CONV, cbk), lambda j, i: (0, conv_j(j))),
                   pl.BlockSpec((1, cbk), lambda j, i: (0, conv_j(j)))),
        input_output_aliases={7: 0},
        compiler_params=_params(("arbitrary", "arbitrary")), name=name)(zx, pre, *dparts, ddt, cw, dzx)


def _pool_sums(q, g, lead):
    sh = _lead if lead else _lag
    s2 = q + sh(q, 1)
    s4 = s2 + sh(s2, 2)
    s8 = s4 + sh(s4, 4)
    s16 = s8 + sh(s8, 8)
    return jnp.where(g == 0, s2, jnp.where(g == 1, s4, jnp.where(g == 2, s8, s16)))


def _pool_count(r0, n, g, shape):
    t = (r0 + lax.broadcasted_iota(jnp.int32, shape, 0) + 1).astype(F32)
    return jnp.minimum(t, (2 << g).astype(F32))


def _pool_fwd(h, pw, scale, name):
    b, seq, d = h.shape
    dg = d // 4
    rows = min(SHIFT_ROWS, seq)

    def body(h_ref, w_ref, s_ref, o_ref):
        g = pl.program_id(1)
        wmat = w_ref[...]
        sc = s_ref[...]

        def chunk(c, carry):
            r0 = c * rows
            ext = _window(h_ref, c, rows, seq, 16, 0)
            sums = _pool_sums(ext, g, False)[16:, :]
            mixed = sums / _pool_count(r0, rows, g, (rows, dg)) - ext[16:, :]
            o_ref[pl.ds(pl.multiple_of(r0, rows), rows), :] = _nn(mixed.astype(BF16), wmat) * sc
            return carry

        lax.fori_loop(0, seq // rows, chunk, 0)

    return pl.pallas_call(
        body, out_shape=jax.ShapeDtypeStruct((b, seq, d), F32), grid=(b, 4),
        in_specs=[pl.BlockSpec((None, seq, dg), lambda i, g: (i, 0, g)),
                  pl.BlockSpec((None, dg, dg), lambda i, g: (g, 0, 0)),
                  pl.BlockSpec((1, dg), lambda i, g: (0, g))],
        out_specs=pl.BlockSpec((None, seq, dg), lambda i, g: (i, 0, g)),
        compiler_params=_params(("parallel", "parallel")), name=name)(h, pw, scale)


def _pool_bwd(h, dout, pw, scale, name):
    b, seq, d = h.shape
    dg = d // 4
    rows = min(SHIFT_ROWS, seq)

    def body(h_ref, g_ref, w_ref, s_ref, o_ref, dw_ref, ds_ref, dw_acc):
        g = pl.program_id(0)
        wmat = w_ref[...]
        sc = s_ref[...]
        dw_acc[...] = jnp.zeros_like(dw_acc)

        def chunk(c, dsc):
            r0 = c * rows
            ext = _window(h_ref, c, rows, seq, 16, 0)
            sums = _pool_sums(ext, g, False)[16:, :]
            mixed = (sums / _pool_count(r0, rows, g, (rows, dg)) - ext[16:, :]).astype(BF16)
            gext = _window(g_ref, c, rows, seq, 0, 16)
            dsc = dsc + jnp.sum(gext[:rows, :] * _nn(mixed, wmat), axis=0, keepdims=True)
            dpre = (gext * sc).astype(BF16)
            dw_acc[...] += _tn(mixed, dpre[:rows, :])
            dmix = _nt(dpre, wmat)
            q = dmix / _pool_count(r0, rows + 16, g, (rows + 16, dg))
            back = _pool_sums(q, g, True)
            o_ref[pl.ds(pl.multiple_of(r0, rows), rows), :] = back[:rows, :] - dmix[:rows, :]
            return dsc

        dsc = lax.fori_loop(0, seq // rows, chunk, jnp.zeros((1, dg), F32))

        @pl.when(pl.program_id(1) == 0)
        def _():
            dw_ref[...] = dw_acc[...]
            ds_ref[...] = dsc

        @pl.when(pl.program_id(1) > 0)
        def _():
            dw_ref[...] += dw_acc[...]
            ds_ref[...] += dsc

    return pl.pallas_call(
        body,
        out_shape=(jax.ShapeDtypeStruct((b, seq, d), F32), jax.ShapeDtypeStruct((4, dg, dg), F32),
                   jax.ShapeDtypeStruct((1, d), F32)),
        grid=(4, b),
        in_specs=[pl.BlockSpec((None, seq, dg), lambda g, i: (i, 0, g)),
                  pl.BlockSpec((None, seq, dg), lambda g, i: (i, 0, g)),
                  pl.BlockSpec((None, dg, dg), lambda g, i: (g, 0, 0)),
                  pl.BlockSpec((1, dg), lambda g, i: (0, g))],
        out_specs=(pl.BlockSpec((None, seq, dg), lambda g, i: (i, 0, g)),
                   pl.BlockSpec((None, dg, dg), lambda g, i: (g, 0, 0)),
                   pl.BlockSpec((1, dg), lambda g, i: (0, g))),
        scratch_shapes=[pltpu.VMEM((dg, dg), F32)],
        compiler_params=_params(("parallel", "arbitrary")), name=name)(h, dout, pw, scale)


def _head_of(channel):
    return jnp.right_shift(channel, HEAD_DIM.bit_length() - 1)


def _ssd_consts(gw):
    q = CHUNK
    row = lax.broadcasted_iota(jnp.int32, (q, q), 0)
    col = lax.broadcasted_iota(jnp.int32, (q, q), 1)
    tril = (row >= col).astype(BF16)
    triu = (row <= col).astype(BF16)
    e = (_head_of(lax.broadcasted_iota(jnp.int32, (LANES, gw), 1))
         == lax.broadcasted_iota(jnp.int32, (LANES, gw), 0)).astype(BF16)
    et = (_head_of(lax.broadcasted_iota(jnp.int32, (gw, LANES), 0))
          == lax.broadcasted_iota(jnp.int32, (gw, LANES), 1)).astype(BF16)
    return row, col, tril, triu, e, et


def _ssd_common(dtr, dtb, alog, gw):
    q = CHUNK
    row, col, tril, triu, e, et = _ssd_consts(gw)
    dt = _softplus(dtr + dtb)
    a_row = -jnp.exp(alog)
    acum = _sel_left(tril, dt * a_row)
    ac_last = jnp.sum(jnp.where(row == q - 1, acum, 0.0), axis=0, keepdims=True)
    eac = jnp.exp(acum)
    de = jnp.exp(ac_last - acum)
    e2 = jnp.concatenate([e, e], axis=0)
    expand = _sel_right(jnp.concatenate([dt, eac, de], axis=0), e2, 2)
    dt_x, eac_x, de_x = expand[0:q], expand[q:2 * q], expand[2 * q:3 * q]
    acum_t = acum.T
    cd_col = jnp.exp(acum_t[:, q - 1:q])
    et3 = jnp.concatenate([et, et, et], axis=1)
    cdmat = _nn(et3, jnp.concatenate(_split(jnp.broadcast_to(cd_col, (LANES, D_STATE)), 3), axis=0))
    consts = dict(row=row, col=col, tril=tril, triu=triu, e=e, et=et)
    return dt, a_row, acum, acum_t, ac_last, eac, de, dt_x, eac_x, de_x, cdmat, consts


def _decay(acum, acum_t, j, row, col):
    diff = acum[:, j:j + 1] - acum_t[j:j + 1, :]
    return jnp.exp(jnp.where(row >= col, diff, -1e30))


def _ssd_fwd(xc, zx, dtr, dtb, alog, dskip, nw, d_inner, name):
    b, seq, xbc = xc.shape
    q = CHUNK
    nc = seq // q
    gw = d_inner // N_GROUPS
    nh = gw // HEAD_DIM
    xb0 = d_inner // D_STATE
    xc0 = xb0 + N_GROUPS
    dt0 = (d_inner + xbc) // LANES

    nb = max(n for n in (4, 2, 1) if b % n == 0)

    def body(x_ref, b_ref, c_ref, z_ref, dtr_ref, dtb_ref, al_ref, dsk_ref, nw_ref, y_ref, yn_ref, st_ref, s_ref):
        @pl.when(pl.program_id(2) == 0)
        def _():
            s_ref[...] = jnp.zeros_like(s_ref)

        for s in range(nb):
            one(s, x_ref.at[s], b_ref.at[s], c_ref.at[s], z_ref.at[s], dtr_ref.at[s], dtb_ref, al_ref, dsk_ref, nw_ref,
                y_ref.at[s], yn_ref.at[s], st_ref.at[s], s_ref.at[s])

    def one(s, x_ref, b_ref, c_ref, z_ref, dtr_ref, dtb_ref, al_ref, dsk_ref, nw_ref, y_ref, yn_ref, st_ref, s_ref):
        prev = s_ref[...]
        st_ref[...] = prev
        x = x_ref[...]
        bm = b_ref[...].astype(BF16)
        cm = c_ref[...].astype(BF16)
        (dt, a_row, acum, acum_t, ac_last, eac, de, dt_x, eac_x, de_x, cdmat, k) = _ssd_common(
            dtr_ref[...], dtb_ref[0:1, :], al_ref[0:1, :], gw)
        xdt = x * dt_x
        xdt_b = xdt.astype(BF16)
        cb = _nt(cm, bm)
        half = _head_of(lax.broadcasted_iota(jnp.int32, (q, LANES), 1))
        pairs = []
        for j in range(nh):
            pc = (j // 2) * LANES
            m = (cb * _decay(acum, acum_t, j, k["row"], k["col"])).astype(BF16)
            yj = jnp.where(half == j % 2, _nn(m, xdt_b[:, pc:pc + LANES]), 0.0)
            if j % 2 == 0:
                pairs.append(yj)
            else:
                pairs[-1] = pairs[-1] + yj
        prev_b = prev.astype(BF16)
        y = dsk_ref[0:1, :] * x + jnp.concatenate(pairs, axis=1) + eac_x * _nt(cm, prev_b)
        s_ref[...] = cdmat * prev + _tn((xdt * de_x).astype(BF16), bm)
        y_ref[...] = y
        z = z_ref[...].astype(F32)
        yg = y * (z * _sigmoid(z))
        r = lax.rsqrt(jnp.mean(yg * yg, axis=-1, keepdims=True) + EPS)
        yn_ref[...] = ((yg * r) * nw_ref[0:1, :]).astype(BF16)

    par = lambda w: pl.BlockSpec((None, 8, w), lambda i, g, c: (g, 0, 0))
    return pl.pallas_call(
        body,
        out_shape=(jax.ShapeDtypeStruct((b, seq, d_inner), F32), jax.ShapeDtypeStruct((b, seq, d_inner), BF16),
                   jax.ShapeDtypeStruct((b, nc, N_GROUPS, gw, D_STATE), F32)),
        grid=(b // nb, N_GROUPS, nc),
        in_specs=[pl.BlockSpec((nb, q, gw), lambda i, g, c: (i, c, g)),
                  pl.BlockSpec((nb, q, D_STATE), lambda i, g, c: (i, c, xb0 + g)),
                  pl.BlockSpec((nb, q, D_STATE), lambda i, g, c: (i, c, xc0 + g)),
                  pl.BlockSpec((nb, q, gw), lambda i, g, c: (i, c, g)),
                  pl.BlockSpec((nb, q, LANES), lambda i, g, c: (i, c, g)),
                  par(LANES), par(LANES), par(gw), par(gw)],
        out_specs=(pl.BlockSpec((nb, q, gw), lambda i, g, c: (i, c, g)),
                   pl.BlockSpec((nb, q, gw), lambda i, g, c: (i, c, g)),
                   pl.BlockSpec((nb, None, None, gw, D_STATE), lambda i, g, c: (i, c, g, 0, 0))),
        scratch_shapes=[pltpu.VMEM((nb, gw, D_STATE), F32)],
        compiler_params=_params(("parallel", "parallel", "arbitrary")), name=name,
    )(xc, xc, xc, zx, dtr, dtb, alog, dskip, nw)


def _ssd_bwd(xc, zx, dtr, y, dyn, st, dtb, alog, dskip, nw, d_inner, name):
    b, seq, xbc = xc.shape
    q = CHUNK
    nc = seq // q
    gw = d_inner // N_GROUPS
    nh = gw // HEAD_DIM
    xb0 = d_inner // D_STATE
    xc0 = xb0 + N_GROUPS
    dt0 = (d_inner + xbc) // LANES

    nb = max(n for n in (4, 2, 1) if b % n == 0)

    def body(x_ref, b_ref, c_ref, z_ref, dtr_ref, y_ref, g_ref, st_ref, dtb_ref, al_ref, dsk_ref, nw_ref,
             dz_ref, dx_ref, db_ref, dc_ref, ddt_ref, dnw_ref, dd_ref, dal_ref, dbias_ref,
             ds_ref, colbuf, rowbuf):
        first = jnp.logical_and(pl.program_id(1) == 0, pl.program_id(2) == 0)

        @pl.when(pl.program_id(2) == 0)
        def _():
            ds_ref[...] = jnp.zeros_like(ds_ref)

        sums = [one(x_ref.at[s], b_ref.at[s], c_ref.at[s], z_ref.at[s], dtr_ref.at[s], y_ref.at[s], g_ref.at[s],
                    st_ref.at[s], dtb_ref, al_ref, dsk_ref, nw_ref, dz_ref.at[s], dx_ref.at[s], db_ref.at[s],
                    dc_ref.at[s], ddt_ref.at[s], ds_ref.at[s], colbuf.at[s], rowbuf.at[s]) for s in range(nb)]
        dnw, dd, dal, dbias = [functools.reduce(lambda p, r: p + r, [sm[i] for sm in sums]) for i in range(4)]

        @pl.when(first)
        def _():
            dnw_ref[...] = jnp.broadcast_to(dnw, (8, gw))
            dd_ref[...] = dd
            dal_ref[...] = jnp.broadcast_to(dal, (8, LANES))
            dbias_ref[...] = jnp.broadcast_to(dbias, (8, LANES))

        @pl.when(jnp.logical_not(first))
        def _():
            dnw_ref[...] += jnp.broadcast_to(dnw, (8, gw))
            dd_ref[...] += dd
            dal_ref[...] += jnp.broadcast_to(dal, (8, LANES))
            dbias_ref[...] += jnp.broadcast_to(dbias, (8, LANES))

    def one(x_ref, b_ref, c_ref, z_ref, dtr_ref, y_ref, g_ref, st_ref, dtb_ref, al_ref, dsk_ref, nw_ref,
            dz_ref, dx_ref, db_ref, dc_ref, ddt_ref, ds_ref, colbuf, rowbuf):
        x = x_ref[...]
        bm = b_ref[...].astype(BF16)
        cm = c_ref[...].astype(BF16)
        z = z_ref[...].astype(F32)
        y = y_ref[...]
        prev = st_ref[...]
        dtr = dtr_ref[...] + dtb_ref[0:1, :]
        (dt, a_row, acum, acum_t, ac_last, eac, de, dt_x, eac_x, de_x, cdmat, k) = _ssd_common(
            dtr_ref[...], dtb_ref[0:1, :], al_ref[0:1, :], gw)
        row, col = k["row"], k["col"]
        et2 = jnp.concatenate([k["et"], k["et"]], axis=0)

        sz = _sigmoid(z)
        silu_z = z * sz
        yg = y * silu_z
        r = lax.rsqrt(jnp.mean(yg * yg, axis=-1, keepdims=True) + EPS)
        xh = yg * r
        dyn = g_ref[...].astype(F32)
        gh = dyn * nw_ref[0:1, :]
        dyg = r * (gh - xh * jnp.mean(gh * xh, axis=-1, keepdims=True))
        dnw = jnp.sum(dyn * xh, axis=0, keepdims=True)
        g = dyg * silu_z
        dz_ref[...] = (dyg * y * (sz * (1.0 + z * (1.0 - sz)))).astype(BF16)
        dd = _sel_right(jnp.broadcast_to(jnp.sum(g * x, axis=0, keepdims=True), (8, gw)), et2, 2)

        xdt = x * dt_x
        xdt_b = xdt.astype(BF16)
        g_b = g.astype(BF16)
        prev_b = prev.astype(BF16)
        cb = _nt(cm, bm)

        cp = _nt(cm, prev_b)
        ge = g * eac_x
        dac = _sel_right(ge * cp, et2, 2)
        ge_b = ge.astype(BF16)
        dcm = _nn(ge_b, prev_b)
        dprev = _tn(ge_b, cm)

        colbuf[...] = jnp.zeros_like(colbuf)
        rowbuf[...] = jnp.zeros_like(rowbuf)
        dcb = jnp.zeros((q, q), F32)
        half = _head_of(lax.broadcasted_iota(jnp.int32, (q, LANES), 1))
        pairs = []
        for j in range(nh):
            pc = (j // 2) * LANES
            dec = _decay(acum, acum_t, j, row, col)
            m = cb * dec
            gj = jnp.where(half == j % 2, g[:, pc:pc + LANES], 0.0).astype(BF16)
            dm = _nt(gj, xdt_b[:, pc:pc + LANES])
            w = dm * m
            colbuf[:, j:j + 1] = jnp.sum(w, axis=1, keepdims=True)
            rowbuf[j:j + 1, :] = jnp.sum(w, axis=0, keepdims=True)
            dcb = dcb + dm * dec
            dj = jnp.where(half == j % 2, _tn(m.astype(BF16), g_b[:, pc:pc + LANES]), 0.0)
            if j % 2 == 0:
                pairs.append(dj)
            else:
                pairs[-1] = pairs[-1] + dj
        dxdt = jnp.concatenate(pairs, axis=1)
        dcb_b = dcb.astype(BF16)
        dcm = dcm + _nn(dcb_b, bm)
        dbm = _tn(dcb_b, cm)

        ds = ds_ref[...]
        ds_b = ds.astype(BF16)
        u = _nt(bm, ds_b)
        dxdt = dxdt + u * de_x
        dde = _sel_right(u * xdt, et2, 2)
        dbm = dbm + _nn((xdt * de_x).astype(BF16), ds_b)
        pm = jnp.concatenate(_split(ds * prev, 2), axis=1)
        t2 = _tn(pm, k["et"])
        dcd_row = jnp.sum(t2[0:D_STATE] + t2[D_STATE:2 * D_STATE], axis=0, keepdims=True)
        last = dcd_row * jnp.exp(ac_last) + jnp.sum(dde * de, axis=0, keepdims=True)
        dac = dac + colbuf[...] - rowbuf[...].T - dde * de + jnp.where(row == q - 1, last, 0.0)
        ds_ref[...] = cdmat * ds + dprev

        dadt = _sel_left(k["triu"], dac)
        ddt = _sel_right(dxdt * x, et2, 2) + dadt * a_row
        dal = jnp.sum(dadt * dt, axis=0, keepdims=True) * a_row
        lane = lax.broadcasted_iota(jnp.int32, (q, LANES), 1)
        ddtr = jnp.where(lane < nh, ddt * _sigmoid(dtr), 0.0)
        ddt_ref[...] = ddtr.astype(BF16)
        dbias = jnp.sum(ddtr, axis=0, keepdims=True)
        dx_ref[...] = dxdt * dt_x + dsk_ref[0:1, :] * g
        db_ref[...] = dbm
        dc_ref[...] = dcm
        return dnw, dd, dal, dbias

    rc = lambda c: nc - 1 - c
    par = lambda w: pl.BlockSpec((None, 8, w), lambda g, i, c: (g, 0, 0))
    blk = lambda w: pl.BlockSpec((nb, q, w), lambda g, i, c: (i, rc(c), g))
    return pl.pallas_call(
        body,
        out_shape=(jax.ShapeDtypeStruct((b, seq, zx.shape[2]), BF16),
                   jax.ShapeDtypeStruct((b, seq, d_inner), F32),
                   jax.ShapeDtypeStruct((b, seq, N_GROUPS * D_STATE), F32),
                   jax.ShapeDtypeStruct((b, seq, N_GROUPS * D_STATE), F32),
                   jax.ShapeDtypeStruct((b, seq, N_GROUPS * LANES), BF16),
                   jax.ShapeDtypeStruct((N_GROUPS, 8, gw), F32),
                   jax.ShapeDtypeStruct((N_GROUPS, 8, LANES), F32),
                   jax.ShapeDtypeStruct((N_GROUPS, 8, LANES), F32),
                   jax.ShapeDtypeStruct((N_GROUPS, 8, LANES), F32)),
        grid=(N_GROUPS, b // nb, nc),
        in_specs=[blk(gw),
                  pl.BlockSpec((nb, q, D_STATE), lambda g, i, c: (i, rc(c), xb0 + g)),
                  pl.BlockSpec((nb, q, D_STATE), lambda g, i, c: (i, rc(c), xc0 + g)),
                  blk(gw),
                  pl.BlockSpec((nb, q, LANES), lambda g, i, c: (i, rc(c), g)),
                  blk(gw), blk(gw),
                  pl.BlockSpec((nb, None, None, gw, D_STATE), lambda g, i, c: (i, rc(c), g, 0, 0)),
                  par(LANES), par(LANES), par(gw), par(gw)],
        out_specs=(blk(gw), blk(gw), blk(D_STATE), blk(D_STATE), blk(LANES),
                   par(gw), par(LANES), par(LANES), par(LANES)),
        scratch_shapes=[pltpu.VMEM((nb, gw, D_STATE), F32), pltpu.VMEM((nb, q, LANES), F32),
                        pltpu.VMEM((nb, LANES, q), F32)],
        compiler_params=_params(("parallel", "arbitrary", "arbitrary")), name=name,
    )(xc, xc, xc, zx, dtr, y, dyn, st, dtb, alog, dskip, nw)


def _adamw(w, g, m, v, name):
    rows, cols = w.shape
    tr = rows
    for cand in (512, 256, 128, 64, 32, 16, 8):
        if rows % cand == 0 and cand * cols * 4 <= 2 * 1024 * 1024:
            tr = cand
            break
    c1 = 1.0 - ADAM_B1 ** ADAM_STEP
    c2 = 1.0 - ADAM_B2 ** ADAM_STEP

    def body(w_ref, g_ref, m_ref, v_ref, d_ref, mo_ref, vo_ref):
        gv = g_ref[...]
        mn = ADAM_B1 * m_ref[...] + (1.0 - ADAM_B1) * gv
        vn = ADAM_B2 * v_ref[...] + (1.0 - ADAM_B2) * (gv * gv)
        mo_ref[...] = mn
        vo_ref[...] = vn
        d_ref[...] = -ADAM_LR * ((mn / c1) / (jnp.sqrt(vn / c2) + ADAM_EPS) + ADAM_WD * w_ref[...])

    spec = pl.BlockSpec((tr, cols), lambda i: (i, 0))
    shp = jax.ShapeDtypeStruct((rows, cols), F32)
    return pl.pallas_call(body, out_shape=(shp, shp, shp), grid=(rows // tr,), in_specs=[spec] * 4,
                          out_specs=(spec,) * 3, compiler_params=_params(("parallel",)), name=name)(w, g, m, v)


def _pick_rows(rows, row_bytes, limit=1 << 20):
    for cand in (2048, 1024, 512, 256, 128, 64, 32, 16):
        if rows % cand == 0 and cand * row_bytes <= limit:
            return cand
    return rows


def _as3d(a, lead):
    return a.reshape(a.shape[:lead] + (-1, a.shape[-1]))


def _pair_sum(g, got, core, name):
    h = got.shape[0]
    g3, got3 = _as3d(g, 1), _as3d(got, 1)
    _, rows, cols = got3.shape
    tr = _pick_rows(rows, cols * 4)

    def body(c_ref, g_ref, r_ref, o_ref):
        o_ref[...] = (g_ref[...] + r_ref[...]).astype(BF16)

    out = pl.pallas_call(
        body, out_shape=jax.ShapeDtypeStruct(got3.shape, BF16),
        grid_spec=pltpu.PrefetchScalarGridSpec(
            num_scalar_prefetch=1, grid=(h, rows // tr),
            in_specs=[pl.BlockSpec((None, tr, cols), lambda l, i, c_ref: (c_ref[0] * h + l, i, 0)),
                      pl.BlockSpec((None, tr, cols), lambda l, i, c_ref: (l, i, 0))],
            out_specs=pl.BlockSpec((None, tr, cols), lambda l, i, c_ref: (l, i, 0))),
        compiler_params=_params(("parallel", "parallel")), name=name)(core, g3, got3)
    return out.reshape(got.shape)


def _sum4(q, core, name):
    q4 = _as3d(q, 2)
    _, h, rows, cols = q4.shape
    tr = _pick_rows(rows, cols * 4)

    def body(c_ref, q0, q1, q2, q3, o_ref):
        o_ref[...] = ((q0[...].astype(F32) + q1[...].astype(F32)) + q2[...].astype(F32)) + q3[...].astype(F32)

    out = pl.pallas_call(
        body, out_shape=jax.ShapeDtypeStruct((2 * h, rows, cols), F32),
        grid_spec=pltpu.PrefetchScalarGridSpec(
            num_scalar_prefetch=1, grid=(h, rows // tr),
            in_specs=[pl.BlockSpec((None, None, tr, cols), lambda l, i, c_ref, k=k: (k, l, i, 0))
                      for k in range(N_CHIPS)],
            out_specs=pl.BlockSpec((None, tr, cols), lambda l, i, c_ref: (c_ref[0] * h + l, i, 0))),
        compiler_params=_params(("parallel", "parallel")), name=name)(core, q4, q4, q4, q4)
    return out.reshape((2 * h,) + q.shape[2:])


def _coords():
    return lax.axis_index("x"), lax.axis_index("y"), lax.axis_index("c")


def _other_chips(x, y):
    return [(1 - x, y), (x, 1 - y), (1 - x, 1 - y)]


def _allgather_halves(src, name):
    rows, cols = src.shape

    def body(x_ref, o_ref, send, recv, local):
        x, y, c = _coords()
        sib = (x, y, 1 - c)
        chips = _other_chips(x, y)

        def slot(h, cx, cy):
            return o_ref.at[h, 2 * cx + cy]

        def copy(kk, dst, to, src_ref):
            return pltpu.make_async_remote_copy(src_ref=src_ref, dst_ref=dst, send_sem=send.at[kk],
                                                recv_sem=recv.at[kk], device_id=to, device_id_type=MESH)

        mine = pltpu.make_async_copy(x_ref, slot(c, x, y), local)
        mine.start()
        first = [copy(0, slot(c, x, y), sib, x_ref)]
        first += [copy(1 + j, slot(c, x, y), (*chip, c), x_ref) for j, chip in enumerate(chips)]
        for cp in first:
            cp.start()
        passed = [copy(4 + j, slot(c, *chip), sib, slot(c, *chip)) for j, chip in enumerate(chips)]
        for j, chip in enumerate(chips):
            copy(1 + j, slot(c, *chip), (x, y, c), x_ref).wait_recv()
            passed[j].start()
        copy(0, slot(1 - c, x, y), (x, y, c), x_ref).wait_recv()
        for j, chip in enumerate(chips):
            copy(4 + j, slot(1 - c, *chip), (x, y, c), x_ref).wait_recv()
        for cp in first + passed:
            cp.wait_send()
        mine.wait()

    return pl.pallas_call(
        body, out_shape=jax.ShapeDtypeStruct((2, N_CHIPS, rows, cols), src.dtype),
        in_specs=[ANY], out_specs=ANY,
        scratch_shapes=[pltpu.SemaphoreType.DMA((7,)), pltpu.SemaphoreType.DMA((7,)), pltpu.SemaphoreType.DMA],
        name=name)(src)


MIXW = (("ssd_w_in", None), ("ssd_w_out", 0), ("pool_w", 1))
FFNW = (("ffn_w_up", 1), ("ffn_w_down", 0))


def _chip_window(axis, ref, layers, k):
    if axis is None:
        return ref.at[layers, k]
    n = ref.shape[1 + axis] // N_CHIPS
    sl = pl.ds(pl.multiple_of(k * n, LANES if 1 + axis == len(ref.shape) - 1 else 8), n)
    idx = [layers] + [slice(None)] * (len(ref.shape) - 1)
    idx[1 + axis] = sl
    return ref.at[tuple(idx)]


def _full_shape(axis, shard_shape):
    if axis is None:
        return (shard_shape[0], N_CHIPS) + tuple(shard_shape[1:])
    full = list(shard_shape)
    full[1 + axis] *= N_CHIPS
    return tuple(full)


HBM_SPEC = pl.BlockSpec(memory_space=pltpu.HBM)
SEM_SPEC = pl.BlockSpec(memory_space=pltpu.SEMAPHORE)


def _dma_sems(count):
    return pltpu.SemaphoreType.DMA((max(count, 1),))


def _wait_for(copy, kind):
    if kind == "recv":
        copy.wait_recv()
    elif kind == "send":
        copy.wait_send()
    else:
        copy.wait()


def _comm_fused(stages, counts, srcs, lands, name, inplace=False):
    ns, nl, k = len(srcs), len(lands), len(stages)

    def body(*refs):
        src_refs = refs[:ns]
        land_refs = refs[ns + (nl if inplace else 0):ns + (nl if inplace else 0) + nl]
        sem_refs = refs[len(refs) - 3 * k:]
        for s, stage_fn in enumerate(stages):
            starts, waits = stage_fn(src_refs, land_refs, tuple(sem_refs[3 * s:3 * s + 3]))
            for cp in starts:
                cp.start()
            for cp, kind in waits:
                _wait_for(cp, kind)

    scratch = []
    for cnt in counts:
        scratch += [_dma_sems(c) for c in cnt]
    outs = pl.pallas_call(
        body, out_shape=tuple(jax.ShapeDtypeStruct(a.shape, a.dtype) for a in lands),
        in_specs=[ANY] * (ns + (nl if inplace else 0)), out_specs=(ANY,) * nl,
        input_output_aliases={ns + i: i for i in range(nl)} if inplace else {},
        scratch_shapes=scratch, name=name)(*srcs, *(lands if inplace else ()))
    return list(outs)


class _SplitComm:
    def __init__(self, stages, counts, srcs, lands, name):
        self.stages, self.counts, self.name = stages, counts, name
        self.ns = len(srcs)
        self.data = [pltpu.with_memory_space_constraint(a, pltpu.HBM) for a in list(srcs) + list(lands)]
        self.sems = None
        self.step = 0

    def advance(self, after=None):
        i, k, nd, ns = self.step, len(self.stages), len(self.data), self.ns
        first, last = i == 0, i == k
        stages = self.stages
        after = list(after) if isinstance(after, (list, tuple)) else [after]

        def body(*refs):
            data = refs[:nd]
            pos = nd
            if not first:
                old = tuple(refs[pos:pos + 3])
                pos += 3 + len(after)
            if not last:
                new = tuple(refs[pos:pos + 3])
            if not first:
                for cp, kind in stages[i - 1](data[:ns], data[ns:], old)[1]:
                    _wait_for(cp, kind)
            if not last:
                for cp in stages[i](data[:ns], data[ns:], new)[0]:
                    cp.start()
                refs[len(refs) - 1][...] = jnp.zeros((8, LANES), F32)

        args = list(self.data)
        in_specs = [HBM_SPEC] * nd
        if not first:
            args += list(self.sems) + after
            in_specs += [SEM_SPEC] * 3 + [ANY] * len(after)
        out_shape, out_specs = [], []
        if not last:
            out_shape += [_dma_sems(c) for c in self.counts[i]]
            out_specs += [SEM_SPEC] * 3
        out_shape += [pltpu.HBM(a.shape, a.dtype) for a in self.data]
        out_specs += [HBM_SPEC] * nd
        if not last:
            out_shape.append(jax.ShapeDtypeStruct((8, LANES), F32))
            out_specs.append(pl.BlockSpec(memory_space=pltpu.VMEM))
        off = 0 if last else 3
        outs = pl.pallas_call(
            body, out_shape=tuple(out_shape), in_specs=in_specs, out_specs=tuple(out_specs),
            input_output_aliases={d: off + d for d in range(nd)},
            compiler_params=pltpu.CompilerParams(has_side_effects=pltpu.SideEffectType.DATAFLOW_SIDE_EFFECTING),
            name=f"{self.name}_{i}")(*args)
        self.sems = None if last else outs[:3]
        self.data = list(outs[off:off + nd])
        self.step += 1
        return None if last else outs[len(outs) - 1]

    def lands(self):
        return self.data[self.ns:]


def _gather_stages(spec):
    n = len(spec)

    def parts(srcs, lands):
        x, y, c = _coords()
        out = []
        for w, (_, axis) in enumerate(spec):
            h = srcs[w].shape[0] // 2
            mine, theirs = pl.ds(c * h, h), pl.ds((1 - c) * h, h)
            out.append((srcs[w].at[mine], lambda layers, k, w=w, axis=axis: _chip_window(axis, lands[w], layers, k),
                        mine, theirs))
        return x, y, c, 2 * x + y, (x, y, 1 - c), _other_chips(x, y), out

    def remote(src, dst, send, recv, idx, to):
        return pltpu.make_async_remote_copy(src_ref=src, dst_ref=dst, send_sem=send.at[idx], recv_sem=recv.at[idx],
                                            device_id=to, device_id_type=MESH)

    def stage0(srcs, lands, sems):
        send, recv, local = sems
        x, y, c, me, sib, chips, ps = parts(srcs, lands)
        starts, waits = [], []
        for w, (src, dst, mine, theirs) in enumerate(ps):
            lc = pltpu.make_async_copy(src, dst(mine, me), local.at[w])
            first = [remote(src, dst(mine, me), send, recv, 4 * w, sib)]
            first += [remote(src, dst(mine, me), send, recv, 4 * w + 1 + j, (cx, cy, c)) for j, (cx, cy) in enumerate(chips)]
            starts += [lc] + first
            waits.append((remote(src, dst(theirs, me), send, recv, 4 * w, (x, y, c)), "recv"))
            waits += [(remote(src, dst(mine, 2 * cx + cy), send, recv, 4 * w + 1 + j, (x, y, c)), "recv")
                      for j, (cx, cy) in enumerate(chips)]
            waits += [(cp, "send") for cp in first] + [(lc, "local")]
        return starts, waits

    def stage1(srcs, lands, sems):
        send, recv, _ = sems
        x, y, c, me, sib, chips, ps = parts(srcs, lands)
        starts, waits = [], []
        for w, (src, dst, mine, theirs) in enumerate(ps):
            for j, (cx, cy) in enumerate(chips):
                blk = dst(mine, 2 * cx + cy)
                fwd = remote(blk, blk, send, recv, 3 * w + j, sib)
                starts.append(fwd)
                waits.append((remote(src, dst(theirs, 2 * cx + cy), send, recv, 3 * w + j, (x, y, c)), "recv"))
                waits.append((fwd, "send"))
        return starts, waits

    return [stage0, stage1], [(4 * n, 4 * n, n), (3 * n, 3 * n, 0)]


def _swap_stages(spec):
    n = len(spec)

    def stage(srcs, lands, sems):
        send, recv, _ = sems
        x, y, c = _coords()
        starts, waits = [], []
        for w in range(n):
            h = srcs[w].shape[0] // 2
            cp = pltpu.make_async_remote_copy(src_ref=srcs[w].at[pl.ds((1 - c) * h, h)], dst_ref=lands[w],
                                              send_sem=send.at[w], recv_sem=recv.at[w],
                                              device_id=(x, y, 1 - c), device_id_type=MESH)
            starts.append(cp)
            waits += [(cp, "recv"), (cp, "send")]
        return starts, waits

    return [stage], [(n, n, 0)]


def _scatter_stages(spec):
    n = len(spec)

    def stage(srcs, lands, sems):
        send, recv, local = sems
        x, y, c = _coords()
        me = 2 * x + y
        starts, waits = [], []
        for w, (_, axis) in enumerate(spec):
            layers = pl.ds(0, srcs[w].shape[0])
            own = _chip_window(axis, srcs[w], layers, me)
            lc = pltpu.make_async_copy(own, lands[w].at[me], local.at[w])
            starts.append(lc)
            for j, (cx, cy) in enumerate(_other_chips(x, y)):
                cp = pltpu.make_async_remote_copy(src_ref=_chip_window(axis, srcs[w], layers, 2 * cx + cy),
                                                  dst_ref=lands[w].at[me], send_sem=send.at[3 * w + j],
                                                  recv_sem=recv.at[3 * w + j], device_id=(cx, cy, c), device_id_type=MESH)
                starts.append(cp)
                waits.append((pltpu.make_async_remote_copy(
                    src_ref=own, dst_ref=lands[w].at[2 * cx + cy], send_sem=send.at[3 * w + j], recv_sem=recv.at[3 * w + j],
                    device_id=(x, y, c), device_id_type=MESH), "recv"))
                waits.append((cp, "send"))
            waits.append((lc, "local"))
        return starts, waits

    return [stage], [(3 * n, 3 * n, n)]


def _share_stages(spec):
    n = len(spec)

    def stage(srcs, lands, sems):
        send, recv, _ = sems
        x, y, c = _coords()
        starts, waits = [], []
        for w in range(n):
            h = lands[w].shape[0] // 2
            mine, theirs = lands[w].at[pl.ds(c * h, h)], lands[w].at[pl.ds((1 - c) * h, h)]
            cp = pltpu.make_async_remote_copy(src_ref=mine, dst_ref=mine, send_sem=send.at[w], recv_sem=recv.at[w],
                                              device_id=(x, y, 1 - c), device_id_type=MESH)
            starts.append(cp)
            waits.append((pltpu.make_async_remote_copy(src_ref=theirs, dst_ref=theirs, send_sem=send.at[w],
                                                       recv_sem=recv.at[w], device_id=(x, y, c), device_id_type=MESH),
                          "recv"))
            waits.append((cp, "send"))
        return starts, waits

    return [stage], [(n, n, 0)]


def _shard_of(p, axis):
    if axis is None:
        return (p.shape[0],) + tuple(p.shape[2:])
    s = list(p.shape)
    s[1 + axis] //= N_CHIPS
    return tuple(s)


def _gather8_stages():
    def stage(srcs, lands, sems):
        send, recv, local = sems
        x, y, c = _coords()
        me = 4 * x + 2 * y + c
        lc = pltpu.make_async_copy(srcs[0], lands[0].at[me], local.at[0])
        starts, waits = [lc], []
        for kk in range(1, 8):
            to = (1 - x if kk & 4 else x, 1 - y if kk & 2 else y, 1 - c if kk & 1 else c)
            cp = pltpu.make_async_remote_copy(src_ref=srcs[0], dst_ref=lands[0].at[me], send_sem=send.at[kk - 1],
                                              recv_sem=recv.at[kk - 1], device_id=to, device_id_type=MESH)
            starts.append(cp)
            waits.append((pltpu.make_async_remote_copy(
                src_ref=srcs[0], dst_ref=lands[0].at[4 * to[0] + 2 * to[1] + to[2]], send_sem=send.at[kk - 1],
                recv_sem=recv.at[kk - 1], device_id=(x, y, c), device_id_type=MESH), "recv"))
            waits.append((cp, "send"))
        waits.append((lc, "local"))
        return starts, waits

    return [stage], [(7, 7, 1)]


def _sum8(buf, name):
    _, rows, cols = buf.shape
    tr = _pick_rows(rows, cols * 4)

    def body(*refs):
        acc = refs[0][...]
        for r in refs[1:8]:
            acc = acc + r[...]
        refs[8][...] = acc

    return pl.pallas_call(
        body, out_shape=jax.ShapeDtypeStruct((rows, cols), F32), grid=(rows // tr,),
        in_specs=[pl.BlockSpec((None, tr, cols), lambda i, k=k: (k, i, 0)) for k in range(8)],
        out_specs=pl.BlockSpec((tr, cols), lambda i: (i, 0)),
        compiler_params=_params(("parallel",)), name=name)(*([buf] * 8))


def _reduce_begin(spec, gs, core, tag, riders=()):
    stages, counts = _swap_stages(spec)
    got = _comm_fused(stages, counts, list(gs) + list(riders),
                      [jax.ShapeDtypeStruct((g.shape[0] // 2,) + g.shape[1:], g.dtype) for g in gs], "swap_" + tag)
    pair = [_pair_sum(a, r, core, "pair_sum_" + n) for a, r, (n, _) in zip(gs, got, spec)]
    stages, counts = _scatter_stages(spec)
    lands = [lax.empty((N_CHIPS,) + _shard_of(p, axis), p.dtype) for p, (_, axis) in zip(pair, spec)]
    comm = _SplitComm(stages, counts, pair, lands, "scatter_" + tag)
    return comm, comm.advance()


def _reduce_finish(spec, comm, core, tag, after):
    comm.advance(after=after)
    halves = [_sum4(q, core, "sum4_" + n) for q, (n, _) in zip(comm.lands(), spec)]
    stages, counts = _share_stages(spec)
    return _comm_fused(stages, counts, [], halves, "share_" + tag, inplace=True)


SMALL = (("ssd_conv_w", 2), ("pool_scale", 1), ("ffn_conv_w", 2))
REPL = ("ssd_conv_b", "ssd_dt_bias", "ssd_a_log", "ssd_d", "ssd_norm_w", "ffn_conv_b",
        "norm_mix_pre", "norm_mix_post", "norm_ffn_pre", "norm_ffn_post")
WEIGHTS = ("ssd_w_in", "ssd_conv_w", "ssd_conv_b", "ssd_dt_bias", "ssd_a_log", "ssd_d", "ssd_norm_w", "ssd_w_out",
           "pool_w", "pool_scale", "ffn_w_up", "ffn_conv_w", "ffn_conv_b", "ffn_w_down", "norm_mix_pre",
           "norm_mix_post", "norm_ffn_pre", "norm_ffn_post")


def _flat_rows(n):
    unit = 2 * 16 * FLAT_COLS
    return 2 * 16 * ((n + unit - 1) // unit)


def _flatten_shards(arrs, dtype):
    flat = jnp.concatenate([a.astype(dtype).reshape(-1) for a in arrs])
    rows = _flat_rows(flat.shape[0])
    flat = jnp.pad(flat, (0, rows * FLAT_COLS - flat.shape[0]))
    return flat.reshape(2, rows // 2, FLAT_COLS)


def _unflatten_full(gathered, shard_shapes, axes):
    per_chip = jnp.swapaxes(gathered, 0, 1).reshape(N_CHIPS, -1)
    out, off = [], 0
    for shp, ax in zip(shard_shapes, axes):
        n = math.prod(shp)
        pieces = [per_chip[k, off:off + n].reshape(shp) for k in range(N_CHIPS)]
        out.append(jnp.concatenate(pieces, axis=ax))
        off += n
    return out


def kernel(x, ssd_w_in, ssd_conv_w, ssd_conv_b, ssd_dt_bias, ssd_a_log, ssd_d, ssd_norm_w, ssd_w_out, pool_w, pool_scale, ffn_w_up, ffn_conv_w, ffn_conv_b, ffn_w_down, norm_mix_pre, norm_mix_post, norm_ffn_pre, norm_ffn_post, loss_target, m_ssd_w_in, m_ssd_conv_w, m_ssd_conv_b, m_ssd_dt_bias, m_ssd_a_log, m_ssd_d, m_ssd_norm_w, m_ssd_w_out, m_pool_w, m_pool_scale, m_ffn_w_up, m_ffn_conv_w, m_ffn_conv_b, m_ffn_w_down, m_norm_mix_pre, m_norm_mix_post, m_norm_ffn_pre, m_norm_ffn_post, v_ssd_w_in, v_ssd_conv_w, v_ssd_conv_b, v_ssd_dt_bias, v_ssd_a_log, v_ssd_d, v_ssd_norm_w, v_ssd_w_out, v_pool_w, v_pool_scale, v_ffn_w_up, v_ffn_conv_w, v_ffn_conv_b, v_ffn_w_down, v_norm_mix_pre, v_norm_mix_post, v_norm_ffn_pre, v_norm_ffn_post):
    wts = dict(ssd_w_in=ssd_w_in, ssd_conv_w=ssd_conv_w, ssd_conv_b=ssd_conv_b, ssd_dt_bias=ssd_dt_bias,
               ssd_a_log=ssd_a_log, ssd_d=ssd_d, ssd_norm_w=ssd_norm_w, ssd_w_out=ssd_w_out, pool_w=pool_w,
               pool_scale=pool_scale, ffn_w_up=ffn_w_up, ffn_conv_w=ffn_conv_w, ffn_conv_b=ffn_conv_b,
               ffn_w_down=ffn_w_down, norm_mix_pre=norm_mix_pre, norm_mix_post=norm_mix_post,
               norm_ffn_pre=norm_ffn_pre, norm_ffn_post=norm_ffn_post)
    mom = dict(ssd_w_in=m_ssd_w_in, ssd_conv_w=m_ssd_conv_w, ssd_conv_b=m_ssd_conv_b, ssd_dt_bias=m_ssd_dt_bias,
               ssd_a_log=m_ssd_a_log, ssd_d=m_ssd_d, ssd_norm_w=m_ssd_norm_w, ssd_w_out=m_ssd_w_out, pool_w=m_pool_w,
               pool_scale=m_pool_scale, ffn_w_up=m_ffn_w_up, ffn_conv_w=m_ffn_conv_w, ffn_conv_b=m_ffn_conv_b,
               ffn_w_down=m_ffn_w_down, norm_mix_pre=m_norm_mix_pre, norm_mix_post=m_norm_mix_post,
               norm_ffn_pre=m_norm_ffn_pre, norm_ffn_post=m_norm_ffn_post)
    var = dict(ssd_w_in=v_ssd_w_in, ssd_conv_w=v_ssd_conv_w, ssd_conv_b=v_ssd_conv_b, ssd_dt_bias=v_ssd_dt_bias,
               ssd_a_log=v_ssd_a_log, ssd_d=v_ssd_d, ssd_norm_w=v_ssd_norm_w, ssd_w_out=v_ssd_w_out, pool_w=v_pool_w,
               pool_scale=v_pool_scale, ffn_w_up=v_ffn_w_up, ffn_conv_w=v_ffn_conv_w, ffn_conv_b=v_ffn_conv_b,
               ffn_w_down=v_ffn_w_down, norm_mix_pre=v_norm_mix_pre, norm_mix_post=v_norm_mix_post,
               norm_ffn_pre=v_norm_ffn_pre, norm_ffn_post=v_norm_ffn_post)

    bl, seq, d = x.shape
    t = bl * seq
    depth = norm_mix_pre.shape[0]
    n_ssd = ssd_w_out.shape[0]
    d_inner = ssd_w_out.shape[1] * N_CHIPS
    nheads = d_inner // HEAD_DIM
    hpg = nheads // N_GROUPS
    gw = d_inner // N_GROUPS
    xbc = ssd_conv_w.shape[2] * N_CHIPS
    f2 = ffn_w_up.shape[2] * N_CHIPS
    ff = f2 // 2
    dg = d // 4
    cy = lax.axis_index("c")
    chip = 2 * lax.axis_index("x") + lax.axis_index("y")

    small_shapes = [wts[n].shape for n, _ in SMALL]
    small_axes = [a for _, a in SMALL]
    small_flat = _flatten_shards([wts[n] for n, _ in SMALL], F32)
    small_half = lax.dynamic_index_in_dim(small_flat, cy, 0, keepdims=False)
    small_all = _allgather_halves(small_half, "gather_small")
    conv_w, p_scale, f_conv_w = _unflatten_full(small_all, small_shapes, small_axes)
    def full_shapes(spec, shards):
        return [jax.ShapeDtypeStruct(_full_shape(axis, s.shape), s.dtype) for s, (_, axis) in zip(shards, spec)]

    def row_halves(a):
        return a.reshape((2, a.shape[0] // 2) + a.shape[1:])

    def join_w_in(g):
        return jnp.concatenate([g[:, k] for k in range(N_CHIPS)], axis=-1).reshape(d, -1)

    def join_w_out(g):
        r2 = g.shape[1] // N_CHIPS
        return jnp.concatenate([g[hf, k * r2:(k + 1) * r2] for k in range(N_CHIPS) for hf in range(2)], axis=0)

    ssd_spec = (("ssd_w_in", None), ("ssd_w_out", 0))
    first_shards = [row_halves(wts[n][0].astype(BF16)) for n, _ in ssd_spec]
    stages, counts = _gather_stages(ssd_spec)
    g_in0, g_out0 = _comm_fused(stages, counts, first_shards, full_shapes(ssd_spec, first_shards), "gather_first")
    w_in, w_out = [join_w_in(g_in0)], [join_w_out(g_out0)]
    rest_spec = ssd_spec * (n_ssd - 1) + (("pool_w", 1),) + FFNW
    rest_shards = [row_halves(wts[n][jj].astype(BF16)) for jj in range(1, n_ssd) for n, _ in ssd_spec]
    rest_shards += [wts["pool_w"].astype(BF16)] + [wts[n].astype(BF16) for n, _ in FFNW]
    stages, counts = _gather_stages(rest_spec)
    ffn_gather = _SplitComm(stages, counts, rest_shards + [g_out0],
                            [lax.empty(s.shape, s.dtype) for s in full_shapes(rest_spec, rest_shards)], "gather_rest")
    gather_token = ffn_gather.advance()

    def pad_heads(a):
        lead = a.shape[:-1]
        a = a.reshape(lead + (N_GROUPS, hpg))
        a = jnp.pad(a, [(0, 0)] * len(lead) + [(0, 0), (0, LANES - hpg)])
        return a.reshape(lead + (N_GROUPS * LANES,))

    def unpad_heads(a):
        lead = a.shape[:-1]
        return a.reshape(lead + (N_GROUPS, LANES))[..., :hpg].reshape(lead + (nheads,))

    def group_rows(a, width):
        return jnp.broadcast_to(a.reshape(N_GROUPS, 1, width), (N_GROUPS, 8, width))

    def pad_w_in(w):
        return jnp.concatenate([w[..., :d_inner + xbc], pad_heads(w[..., d_inner + xbc:])], axis=-1)

    w_in_p = [pad_w_in(w_in[0])]
    zw = w_in_p[0].shape[-1]
    w_pool = None

    x2 = x.reshape(t, d)
    tgt2 = loss_target.reshape(t, d)
    w_up = w_down = None

    saved = []
    cur = x2
    tokens = []
    h = _norm_fwd(cur, norm_mix_pre[0:1], BF16, "norm_pre_b", after=[gather_token])
    for i in range(depth):
        j = i // 2
        sv = dict(x_in=cur)
        if i % 2 == 0:
            zx = _mm(h, w_in_p[j], "nn", BF16, "mm_ssd_in", 2048, 512, d).reshape(bl, seq, zw)
            dtr = _mm(h, w_in_p[j][:, d_inner + xbc:], "nn", F32, "mm_ssd_dt", 2048, 512, d).reshape(bl, seq, -1)
            xc, xpre = _ssd_conv_fwd(zx, conv_w[j], ssd_conv_b[j:j + 1], d_inner, "ssd_conv_fwd")
            dtb = group_rows(pad_heads(ssd_dt_bias[j]), LANES)
            alog = group_rows(pad_heads(ssd_a_log[j]), LANES)
            dskip = group_rows(jnp.repeat(ssd_d[j], HEAD_DIM), gw)
            nw = group_rows(ssd_norm_w[j], gw)
            y, yn, st = _ssd_fwd(xc, zx, dtr, dtb, alog, dskip, nw, d_inner, "ssd_fwd")
            if i == 0:
                tokens.append(ffn_gather.advance(after=yn))
            mix = _mm(yn.reshape(t, d_inner), w_out[j], "nn", F32, "mm_ssd_out", 2048, 512, d_inner)
            sv.update(h=h, zx=zx, dtr=dtr, xc=xc, xpre=xpre, y=y, yn=yn, st=st, dtb=dtb, alog=alog, dskip=dskip, nw=nw)
        else:
            mix = _pool_fwd(h.reshape(bl, seq, d), w_pool[j], p_scale[j:j + 1], "pool_fwd").reshape(t, d)
            sv.update(h=h)
        sv.update(mix=mix)
        mid, u = _norm_post_pre(mix, norm_mix_post[i:i + 1], cur, norm_ffn_pre[i:i + 1], BF16, "norm_post_pre_b",
                                after=tokens)
        tokens = []
        if i == 0:
            ffn_gather.advance(after=u)
            rest = ffn_gather.lands()
            for jj in range(1, n_ssd):
                w_in_p.append(pad_w_in(join_w_in(rest[2 * (jj - 1)])))
                w_out.append(join_w_out(rest[2 * (jj - 1) + 1]))
            w_pool, w_up, w_down = rest[2 * (n_ssd - 1):]
        hpre = _mm(u, w_up, "nn", BF16, "mm_up", 2048, 512, d, b_layer=i).reshape(bl, seq, f2)
        act, pre_g, pre_v = _ffn_act_fwd(hpre, f_conv_w[i], ffn_conv_b[i:i + 1], "ffn_act_fwd")
        act = act.reshape(t, ff)
        fo = _mm(act, w_down, "nn", F32, "mm_down", 2048, 512, ff, b_layer=i)
        if i + 1 == depth:
            cur = _norm_fwd(fo, norm_ffn_post[i:i + 1], F32, "norm_post", resid=mid)
        elif i % 2 == 0:
            cur, h = _norm_post_pre(fo, norm_ffn_post[i:i + 1], mid, norm_mix_pre[i + 1:i + 2], F32, "norm_post_pre_f")
        else:
            cur, h = _norm_post_pre(fo, norm_ffn_post[i:i + 1], mid, norm_mix_pre[i + 1:i + 2], BF16, "norm_post_pre_b")
        sv.update(mid=mid, u=u, hpre=hpre, pre_g=pre_g, pre_v=pre_v, act=act, fo=fo)
        saved.append(sv)

    dcur, loss_part = _loss_head(cur, tgt2, "loss_head")

    g = {n: [None] * wts[n].shape[0] for n in WEIGHTS}
    gbuf = dict(up=lax.empty((depth, d, f2), F32), down=lax.empty((depth, ff, d), F32),
                out=lax.empty((n_ssd, d_inner, d), F32), win=lax.empty((n_ssd, d, zw), F32))
    core = cy.reshape(1).astype(jnp.int32)

    def mixer_bwd(i, dmid, behind=()):
        j = i // 2
        sv = saved[i]
        done = []
        if i % 2 == 0:
            dmix, g["norm_mix_post"][i] = _norm_bwd(sv["mix"], norm_mix_post[i:i + 1], dmid, BF16, "norm_bwd_b",
                                                    after=behind)
            dyn = _mm(dmix, w_out[j], "nt", BF16, "mm_ssd_out_dx", 1024, 1024, d)
            gbuf["out"], tok = _mm(sv["yn"].reshape(t, d_inner), dmix, "tn", F32, "mm_ssd_out_dw", 1024, 1024, 2048,
                                   out_buf=(gbuf["out"], j))
            done.append(tok)
            dz, dxs, dbm, dcm, ddt, dnw, dd, dal, dbias = _ssd_bwd(
                sv["xc"], sv["zx"], sv["dtr"], sv["y"], dyn.reshape(bl, seq, d_inner), sv["st"], sv["dtb"], sv["alog"],
                sv["dskip"], sv["nw"], d_inner, "ssd_bwd")
            g["ssd_norm_w"][j] = dnw[:, 0, :].reshape(d_inner)
            g["ssd_d"][j] = dd[:, 0, :hpg].reshape(nheads)
            g["ssd_a_log"][j] = dal[:, 0, :hpg].reshape(nheads)
            g["ssd_dt_bias"][j] = dbias[:, 0, :hpg].reshape(nheads)
            dzx, dcw, dcb = _ssd_conv_bwd(sv["zx"], sv["xpre"], (dxs, dbm, dcm), ddt, dz, conv_w[j], d_inner,
                                          "ssd_conv_bwd")
            g["ssd_conv_w"][j] = dcw
            g["ssd_conv_b"][j] = dcb[0]
            dzx = dzx.reshape(t, zw)
            dh = _mm(dzx, w_in_p[j], "nt", BF16, "mm_ssd_in_dx", 1024, d, zw // 2)
            gbuf["win"], tok = _mm(sv["h"], dzx, "tn", F32, "mm_ssd_in_dw", 1024, zw // 4, 2048, out_buf=(gbuf["win"], j))
            done.append(tok)
        else:
            dmix, g["norm_mix_post"][i] = _norm_bwd(sv["mix"], norm_mix_post[i:i + 1], dmid, F32, "norm_bwd_f",
                                                    after=behind)
            dh3, g["pool_w"][j], dps = _pool_bwd(sv["h"].reshape(bl, seq, d), dmix.reshape(bl, seq, d), w_pool[j],
                                                 p_scale[j:j + 1], "pool_bwd")
            g["pool_scale"][j] = dps[0]
            dh = dh3.reshape(t, d)
        dx_in, g["norm_mix_pre"][i] = _norm_bwd(sv["x_in"], norm_mix_pre[i:i + 1], dh, F32, "norm_bwd_r", resid=dmid,
                                                after=done)
        return dx_in

    ffn_comm = None
    for i in reversed(range(depth)):
        sv = saved[i]
        dfo, g["norm_ffn_post"][i] = _norm_bwd(sv["fo"], norm_ffn_post[i:i + 1], dcur, BF16, "norm_bwd_b")
        dact = _mm(dfo, w_down, "nt", BF16, "mm_down_dx", 1024, ff // 2, d, b_layer=i)
        gbuf["down"], tok_down = _mm(sv["act"], dfo, "tn", F32, "mm_down_dw", ff // 2, 1024, 2048,
                                     out_buf=(gbuf["down"], i))
        dhg, dhv, dcw, dcb = _ffn_act_bwd(sv["hpre"], sv["pre_g"], sv["pre_v"], dact.reshape(bl, seq, ff), f_conv_w[i],
                                          "ffn_act_bwd")
        g["ffn_conv_w"][i] = dcw
        g["ffn_conv_b"][i] = dcb[0]
        dhs = [dhg.reshape(t, ff), dhv.reshape(t, ff)]
        du = _mm(dhs, w_up, "nt", BF16, "mm_up_dx", 1024, d, ff, b_layer=i)
        gbuf["up"], tok_up = _mm(sv["u"], dhs, "tn", F32, "mm_up_dw", 1024, ff // 2, 2048, out_buf=(gbuf["up"], i))
        dmid, g["norm_ffn_pre"][i] = _norm_bwd(sv["mid"], norm_ffn_pre[i:i + 1], du, F32, "norm_bwd_r", resid=dcur,
                                               after=[tok_down, tok_up])
        if i > 0:
            dcur = mixer_bwd(i, dmid)
        else:
            ffn_comm, ffn_token = _reduce_begin(FFNW, [gbuf["up"], gbuf["down"]], core, "ffn")
            dcur = mixer_bwd(0, dmid, behind=[ffn_token])

    grad_x = dcur.reshape(bl, seq, d)
    for n in ("norm_mix_pre", "norm_mix_post", "norm_ffn_pre", "norm_ffn_post"):
        g[n] = [a[0] for a in g[n]]
    small_names = [n for n, _ in SMALL] + list(REPL)
    full = {n: jnp.stack(g[n], axis=0) for n in small_names}

    g_in = jnp.concatenate([gbuf["win"][..., :d_inner + xbc], unpad_heads(gbuf["win"][..., d_inner + xbc:])], axis=-1)
    g_in_cm = jnp.swapaxes(g_in.reshape(n_ssd, d, N_CHIPS, -1), 1, 2)
    vec = jnp.concatenate([full[n].reshape(-1) for n in small_names] + [loss_part[0, :1]])
    nvec = vec.shape[0]
    vrows = 16 * ((nvec + 16 * FLAT_COLS - 1) // (16 * FLAT_COLS))
    vec = jnp.pad(vec, (0, vrows * FLAT_COLS - nvec)).reshape(vrows, FLAT_COLS)
    stages, counts = _gather8_stages()
    small_comm = _SplitComm(stages, counts, [vec], [lax.empty((8, vrows, FLAT_COLS), F32)], "gather_small_grads")
    small_token = small_comm.advance()
    mix_comm, mix_token = _reduce_begin(MIXW, [g_in_cm, gbuf["out"], jnp.stack(g["pool_w"], axis=0)], core, "mixers",
                                        riders=[small_token])

    grads, deltas, new_m, new_v = {}, {}, {}, {}

    def adamw(n, gr):
        shp = wts[n].shape
        two = (math.prod(shp[:-1]), shp[-1])
        dl, mn, vn = _adamw(wts[n].reshape(two), gr.reshape(two), mom[n].reshape(two), var[n].reshape(two),
                            "adamw_" + n)
        grads[n], deltas[n], new_m[n], new_v[n] = gr, dl.reshape(shp), mn.reshape(shp), vn.reshape(shp)
        return dl

    small_comm.advance(after=mix_token)
    tot = _sum8(small_comm.lands()[0], "sum_small").reshape(-1)
    small_grads, off = {}, 0
    for n in small_names:
        cnt = math.prod(full[n].shape)
        small_grads[n] = tot[off:off + cnt].reshape(full[n].shape)
        off += cnt
    loss = tot[off]
    for n, ax in SMALL:
        w = wts[n].shape[ax]
        small_grads[n] = lax.dynamic_slice_in_dim(small_grads[n], chip * w, w, axis=ax)

    behind = [adamw(n, small_grads[n]) for n in small_names][-1:]
    ffn_grads = _reduce_finish(FFNW, ffn_comm, core, "ffn", after=mix_token)
    behind += [adamw(n, gr) for gr, (n, _) in zip(ffn_grads, FFNW)]
    mix_grads = _reduce_finish(MIXW, mix_comm, core, "mixers", after=behind)
    for gr, (n, _) in zip(mix_grads, MIXW):
        adamw(n, gr)

    return (loss, grad_x, *[grads[n] for n in WEIGHTS], *[deltas[n] for n in WEIGHTS],
            *[new_m[n] for n in WEIGHTS], *[new_v[n] for n in WEIGHTS])
```

```python
import functools
import math

import jax
import jax.numpy as jnp
from jax import lax
from jax.experimental import pallas as pl
from jax.experimental.pallas import tpu as pltpu

F32 = jnp.float32
BF16 = jnp.bfloat16
MESH = pl.DeviceIdType.MESH
ANY = pl.BlockSpec(memory_space=pl.ANY)

HEAD_DIM = 64
D_STATE = 128
CHUNK = 128
N_GROUPS = 4
SSD_CONV = 4
FFN_CONV = 3
EPS = 1e-6
N_CHIPS = 4
LANES = 128
FLAT_COLS = 1024

ADAM_LR = 0.001
ADAM_B1 = 0.9
ADAM_B2 = 0.999
ADAM_EPS = 1e-08
ADAM_WD = 0.01
ADAM_STEP = 10

VMEM_LIMIT_BYTES = 56 * 1024 * 1024


def _params(sem=None):
    kw = dict(vmem_limit_bytes=VMEM_LIMIT_BYTES)
    if sem is not None:
        kw["dimension_semantics"] = sem
    return pltpu.CompilerParams(**kw)


def _sigmoid(x):
    return 0.5 * jnp.tanh(0.5 * x) + 0.5


def _softplus(x):
    return jnp.maximum(x, 0.0) + jnp.log(1.0 + jnp.exp(-jnp.abs(x)))


def _dot(a, b, dn):
    return lax.dot_general(a, b, (dn, ((), ())), preferred_element_type=F32)


def _nn(a, b):
    return _dot(a, b, ((1,), (0,)))


def _nt(a, b):
    return _dot(a, b, ((1,), (1,)))


def _tn(a, b):
    return _dot(a, b, ((0,), (0,)))


def _split(x, parts):
    out = []
    r = x
    for _ in range(parts):
        p = r.astype(BF16)
        out.append(p)
        r = r - p.astype(F32)
    return out


def _sel_left(sel, x, parts=3):
    n = x.shape[1]
    r = _nn(sel, jnp.concatenate(_split(x, parts), axis=1))
    out = r[:, 0:n]
    for i in range(1, parts):
        out = out + r[:, i * n:(i + 1) * n]
    return out


def _sel_right(x, sel_stacked, parts=3):
    return _nn(jnp.concatenate(_split(x, parts), axis=1), sel_stacked)


def _mm(a, b, dims, out_dtype, name, tm, tn, tk, b_layer=None, out_buf=None):
    a_list = list(a) if isinstance(a, (list, tuple)) else [a]
    b_list = list(b) if isinstance(b, (list, tuple)) else [b]
    if dims in ("nn", "nt"):
        assert len(b_list) == 1
        m = a_list[0].shape[0]
        segs = [x.shape[1] for x in a_list]
        k = sum(segs)
        bshape = b_list[0].shape[-2:]
        n = bshape[1] if dims == "nn" else bshape[0]
        assert (bshape[0] if dims == "nn" else bshape[1]) == k
    else:
        assert len(a_list) == 1 and b_layer is None
        k, m = a_list[0].shape
        segs = [x.shape[1] for x in b_list]
        n = sum(segs)
    nseg = len(segs)
    tm, tn = min(tm, m), min(tn, n)
    if dims == "tn":
        tk = min(tk, k)
        tn = min(tn, min(segs))
        units = [tn] * nseg
        nk = k // tk
        assert k % tk == 0
    else:
        units = [min(u, s) for u, s in zip(tk if isinstance(tk, (list, tuple)) else [tk] * nseg, segs)]
        nk = sum(s // u for s, u in zip(segs, units))
    assert m % tm == 0 and n % tn == 0 and all(s % u == 0 for s, u in zip(segs, units)), (name, m, n, k, segs, units)
    counts = [s // u for s, u in zip(segs, units)]
    starts = [sum(counts[:s]) for s in range(nseg)]
    assert all(sum(segs[:s]) % units[s] == 0 for s in range(nseg)), (name, segs, units)
    first_block = [sum(segs[:s]) // units[s] for s in range(nseg)]
    dn = {"nn": ((1,), (0,)), "nt": ((1,), (1,)), "tn": ((0,), (0,))}[dims]

    same = len(set(units)) == 1
    nb_ops = len(b_list) if dims == "tn" else (1 if same else nseg)

    def body(*refs):
        a_refs = refs[:len(a_list)]
        b_refs = refs[len(a_list):len(a_list) + nb_ops]
        rest = refs[len(a_list) + nb_ops + (0 if out_buf is None else 1):]
        o_ref = rest[0]
        if out_buf is not None:
            rest[1][...] = jnp.zeros((8, LANES), F32)
            rest = rest[1:]
        acc = rest[1] if nk > 1 else None
        kk = pl.program_id(2)
        sel = kk if dims != "tn" else pl.program_id(1)

        def step(a_ref, b_ref):
            p = _dot(a_ref[...].astype(BF16), b_ref[...].astype(BF16), dn)
            if nk == 1:
                o_ref[...] = p.astype(out_dtype)
                return

            @pl.when(kk == 0)
            def _():
                acc[...] = p

            @pl.when(kk > 0)
            def _():
                acc[...] += p

        if nseg == 1:
            step(a_refs[0], b_refs[0])
        else:
            for s in range(nseg):
                @pl.when(jnp.logical_and(sel >= starts[s], sel < starts[s] + counts[s]))
                def _(s=s):
                    step(a_refs[s] if dims != "tn" else a_refs[0], b_refs[s if nb_ops > 1 else 0])

        if nk > 1:
            @pl.when(kk == nk - 1)
            def _():
                o_ref[...] = acc[...].astype(out_dtype)

    def seg_index(v, s):
        return v if nseg == 1 else jnp.clip(v - starts[s], 0, counts[s] - 1)

    lead = () if b_layer is None else (b_layer,)
    none = () if b_layer is None else (None,)
    def b_block(kk, s):
        return kk if same else first_block[s] + seg_index(kk, s)

    if dims == "nn":
        a_specs = [pl.BlockSpec((tm, units[s]), lambda i, j, kk, s=s: (i, seg_index(kk, s))) for s in range(nseg)]
        b_specs = [pl.BlockSpec(none + (units[s], tn), lambda i, j, kk, s=s: lead + (b_block(kk, s), j))
                   for s in range(nb_ops)]
    elif dims == "nt":
        a_specs = [pl.BlockSpec((tm, units[s]), lambda i, j, kk, s=s: (i, seg_index(kk, s))) for s in range(nseg)]
        b_specs = [pl.BlockSpec(none + (tn, units[s]), lambda i, j, kk, s=s: lead + (j, b_block(kk, s)))
                   for s in range(nb_ops)]
    else:
        a_specs = [pl.BlockSpec((tk, tm), lambda i, j, kk: (kk, i))]
        b_specs = [pl.BlockSpec((tk, tn), lambda i, j, kk, s=s: (kk, seg_index(j, s))) for s in range(nseg)]
    args = a_list + (b_list * nb_ops if dims != "tn" else b_list)
    in_specs = a_specs + b_specs
    aliases = {}
    if out_buf is None:
        out_shape = jax.ShapeDtypeStruct((m, n), out_dtype)
        out_spec = pl.BlockSpec((tm, tn), lambda i, j, kk: (i, j))
    else:
        buf, slab = out_buf
        assert buf.shape[1:] == (m, n) and buf.dtype == out_dtype
        out_shape = (jax.ShapeDtypeStruct(buf.shape, out_dtype), jax.ShapeDtypeStruct((8, LANES), F32))
        out_spec = (pl.BlockSpec((None, tm, tn), lambda i, j, kk: (slab, i, j)),
                    pl.BlockSpec((8, LANES), lambda i, j, kk: (0, 0)))
        aliases = {len(args): 0}
        args = args + [buf]
        in_specs = in_specs + [ANY]
    return pl.pallas_call(
        body,
        out_shape=out_shape,
        grid=(m // tm, n // tn, nk),
        in_specs=in_specs,
        out_specs=out_spec,
        scratch_shapes=[] if nk == 1 else [pltpu.VMEM((tm, tn), F32)],
        input_output_aliases=aliases,
        compiler_params=_params(("parallel", "parallel", "arbitrary") if out_buf is None else ("arbitrary",) * 3),
        name=name,
    )(*args)


def _row_tile(t, want):
    tm = min(want, t)
    assert t % tm == 0
    return tm


def _norm_fwd(x, w, out_dtype, name, resid=None, after=()):
    t, d = x.shape
    tm = _row_tile(t, 512)
    after = [a for a in after if a is not None]

    def body(*refs):
        refs = refs[:len(refs) - 1 - len(after)] + refs[len(refs) - 1:]
        if resid is None:
            x_ref, w_ref, o_ref = refs
        else:
            x_ref, w_ref, r_ref, o_ref = refs
        xv = x_ref[...]
        r = lax.rsqrt(jnp.mean(xv * xv, axis=-1, keepdims=True) + EPS)
        y = (xv * r) * w_ref[...]
        if resid is not None:
            y = r_ref[...] + y
        o_ref[...] = y.astype(out_dtype)

    row = pl.BlockSpec((tm, d), lambda i: (i, 0))
    vec = pl.BlockSpec((1, d), lambda i: (0, 0))
    args = [x, w] + ([] if resid is None else [resid]) + after
    return pl.pallas_call(
        body, out_shape=jax.ShapeDtypeStruct((t, d), out_dtype), grid=(t // tm,),
        in_specs=[row, vec] + ([] if resid is None else [row]) + [ANY] * len(after), out_specs=row,
        compiler_params=_params(("parallel",)), name=name)(*args)


def _norm_post_pre(m, w_post, resid, w_pre, pre_dtype, name, after=()):
    t, d = m.shape
    tm = _row_tile(t, 512)
    after = [a for a in after if a is not None]

    def body(m_ref, w1_ref, r_ref, w2_ref, *rest):
        x_ref, u_ref = rest[len(after):]
        mv = m_ref[...]
        r1 = lax.rsqrt(jnp.mean(mv * mv, axis=-1, keepdims=True) + EPS)
        xv = r_ref[...] + (mv * r1) * w1_ref[...]
        x_ref[...] = xv
        r2 = lax.rsqrt(jnp.mean(xv * xv, axis=-1, keepdims=True) + EPS)
        u_ref[...] = ((xv * r2) * w2_ref[...]).astype(pre_dtype)

    row = pl.BlockSpec((tm, d), lambda i: (i, 0))
    vec = pl.BlockSpec((1, d), lambda i: (0, 0))
    return pl.pallas_call(
        body, out_shape=(jax.ShapeDtypeStruct((t, d), F32), jax.ShapeDtypeStruct((t, d), pre_dtype)), grid=(t // tm,),
        in_specs=[row, vec, row, vec] + [ANY] * len(after), out_specs=(row, row),
        compiler_params=_params(("parallel",)), name=name)(m, w_post, resid, w_pre, *after)


def _norm_bwd(src, w, dy, out_dtype, name, resid=None, after=()):
    t, d = src.shape
    tm = _row_tile(t, 512)
    after = [a for a in after if a is not None]

    def body(*refs):
        refs = refs[:len(refs) - 2 - len(after)] + refs[len(refs) - 2:]
        if resid is None:
            x_ref, w_ref, g_ref, o_ref, dw_ref = refs
        else:
            x_ref, w_ref, g_ref, r_ref, o_ref, dw_ref = refs
        xv = x_ref[...]
        g = g_ref[...].astype(F32)
        r = lax.rsqrt(jnp.mean(xv * xv, axis=-1, keepdims=True) + EPS)
        xh = xv * r
        gh = g * w_ref[...]
        mean = jnp.mean(gh * xh, axis=-1, keepdims=True)
        dx = r * (gh - xh * mean)
        if resid is not None:
            dx = r_ref[...] + dx
        o_ref[...] = dx.astype(out_dtype)
        part = jnp.sum(g * xh, axis=0, keepdims=True)

        @pl.when(pl.program_id(0) == 0)
        def _():
            dw_ref[...] = part

        @pl.when(pl.program_id(0) > 0)
        def _():
            dw_ref[...] += part

    row = pl.BlockSpec((tm, d), lambda i: (i, 0))
    vec = pl.BlockSpec((1, d), lambda i: (0, 0))
    args = [src, w, dy] + ([] if resid is None else [resid]) + after
    return pl.pallas_call(
        body,
        out_shape=(jax.ShapeDtypeStruct((t, d), out_dtype), jax.ShapeDtypeStruct((1, d), F32)),
        grid=(t // tm,),
        in_specs=[row, vec, row] + ([] if resid is None else [row]) + [ANY] * len(after),
        out_specs=(row, vec),
        compiler_params=_params(("arbitrary",)), name=name)(*args)


def _loss_head(y, target, name):
    t, d = y.shape
    tm = _row_tile(t, 512)

    def body(y_ref, t_ref, dy_ref, l_ref):
        e = y_ref[...] - t_ref[...]
        dy_ref[...] = e * (1.0 / d)
        col = jnp.sum(e * e, axis=0, keepdims=True)
        s = jnp.sum(col, axis=1, keepdims=True) * (0.5 / d)
        part = jnp.broadcast_to(s, (1, LANES))

        @pl.when(pl.program_id(0) == 0)
        def _():
            l_ref[...] = part

        @pl.when(pl.program_id(0) > 0)
        def _():
            l_ref[...] += part

    row = pl.BlockSpec((tm, d), lambda i: (i, 0))
    return pl.pallas_call(
        body,
        out_shape=(jax.ShapeDtypeStruct((t, d), F32), jax.ShapeDtypeStruct((1, LANES), F32)),
        grid=(t // tm,), in_specs=[row, row],
        out_specs=(row, pl.BlockSpec((1, LANES), lambda i: (0, 0))),
        compiler_params=_params(("arbitrary",)), name=name)(y, target)


def _window(ref, c, rows, seq, before, after, keep=None):
    r0 = pl.multiple_of(c * rows, rows)
    parts = []
    if before:
        h0 = pl.multiple_of(jnp.maximum(r0 - before, 0), before)
        halo = ref[pl.ds(h0, before), :].astype(F32)
        halo = halo if keep is None else halo[before - keep:, :]
        parts.append(jnp.where(c > 0, halo, 0.0))
    parts.append(ref[pl.ds(r0, rows), :].astype(F32))
    if after:
        h1 = pl.multiple_of(jnp.minimum(r0 + rows, seq - after), after)
        halo = ref[pl.ds(h1, after), :].astype(F32)
        halo = halo if keep is None else halo[:keep, :]
        parts.append(jnp.where(c < seq // rows - 1, halo, 0.0))
    return parts[0] if len(parts) == 1 else jnp.concatenate(parts, axis=0)


def _lag(x, k):
    return pltpu.roll(x, k, 0) if k else x


def _lead(x, k):
    return pltpu.roll(x, x.shape[0] - k, 0) if k else x


SHIFT_ROWS = 128
SHIFT_COLS = 256


HALO = 16
KEEP = 8


def _conv3(ext, w, bias):
    acc = bias + w[2:3, :] * ext[KEEP:, :]
    acc = acc + w[1:2, :] * _lag(ext, 1)[KEEP:, :]
    return acc + w[0:1, :] * _lag(ext, 2)[KEEP:, :]


def _ffn_act_fwd(hpre, cw, cb, name):
    b, seq, f2 = hpre.shape
    cbk = SHIFT_COLS
    nj = f2 // (2 * cbk)
    rows = min(SHIFT_ROWS, seq)

    def body(g_ref, v_ref, wg_ref, wv_ref, bg_ref, bv_ref, o_ref, pg_ref, pv_ref):
        def chunk(c, carry):
            gate = _conv3(_window(g_ref, c, rows, seq, HALO, 0, KEEP), wg_ref[...], bg_ref[...])
            val = _conv3(_window(v_ref, c, rows, seq, HALO, 0, KEEP), wv_ref[...], bv_ref[...])
            a = gate * _sigmoid(gate) * val
            here = pl.ds(pl.multiple_of(c * rows, rows), rows)
            o_ref[here, :] = a.astype(BF16)
            pg_ref[here, :] = gate.astype(BF16)
            pv_ref[here, :] = val.astype(BF16)
            return carry

        lax.fori_loop(0, seq // rows, chunk, 0)

    blk = lambda off: pl.BlockSpec((None, seq, cbk), lambda i, j: (i, 0, j + off))
    wsp = lambda r, off: pl.BlockSpec((r, cbk), lambda i, j: (0, j + off))
    half = jax.ShapeDtypeStruct((b, seq, f2 // 2), BF16)
    return pl.pallas_call(
        body, out_shape=(half, half, half), grid=(b, nj),
        in_specs=[blk(0), blk(nj), wsp(FFN_CONV, 0), wsp(FFN_CONV, nj), wsp(1, 0), wsp(1, nj)],
        out_specs=(blk(0), blk(0), blk(0)),
        compiler_params=_params(("parallel", "parallel")), name=name)(hpre, hpre, cw, cw, cb, cb)


def _ffn_act_bwd(hpre, pre_g, pre_v, da, cw, name):
    b, seq, f2 = hpre.shape
    cbk = SHIFT_COLS
    nj = f2 // (2 * cbk)
    rows = min(SHIFT_ROWS, seq)

    def body(g_ref, v_ref, pg_ref, pv_ref, da_ref, wg_ref, wv_ref, og_ref, ov_ref, dwg_ref, dwv_ref, dbg_ref, dbv_ref):
        wg, wv = wg_ref[...], wv_ref[...]

        def back(dpre, w, o_ref, x_ref, c, carry):
            here = pl.ds(pl.multiple_of(c * rows, rows), rows)
            leads = [dpre, _lead(dpre, 1), _lead(dpre, 2)]
            dx = w[2:3, :] * leads[0] + w[1:2, :] * leads[1] + w[0:1, :] * leads[2]
            o_ref[here, :] = dx[:rows, :].astype(BF16)
            x0 = x_ref[here, :].astype(F32)
            return tuple(carry[k] + jnp.sum(leads[k][:rows, :] * x0, axis=0, keepdims=True) for k in range(FFN_CONV)) + (
                carry[FFN_CONV] + jnp.sum(dpre[:rows, :], axis=0, keepdims=True),)

        def chunk(c, carry):
            cg, cv = carry
            gate = _window(pg_ref, c, rows, seq, 0, HALO, KEEP)
            val = _window(pv_ref, c, rows, seq, 0, HALO, KEEP)
            dav = _window(da_ref, c, rows, seq, 0, HALO, KEEP)
            sg = _sigmoid(gate)
            cg = back(dav * val * (sg * (1.0 + gate * (1.0 - sg))), wg, og_ref, g_ref, c, cg)
            cv = back(dav * (gate * sg), wv, ov_ref, v_ref, c, cv)
            return cg, cv

        z = jnp.zeros((1, cbk), F32)
        cg, cv = lax.fori_loop(0, seq // rows, chunk, ((z,) * (FFN_CONV + 1), (z,) * (FFN_CONV + 1)))
        dwg = jnp.concatenate([cg[2], cg[1], cg[0]], axis=0)
        dwv = jnp.concatenate([cv[2], cv[1], cv[0]], axis=0)

        @pl.when(pl.program_id(1) == 0)
        def _():
            dwg_ref[...] = dwg
            dwv_ref[...] = dwv
            dbg_ref[...] = cg[FFN_CONV]
            dbv_ref[...] = cv[FFN_CONV]

        @pl.when(pl.program_id(1) > 0)
        def _():
            dwg_ref[...] += dwg
            dwv_ref[...] += dwv
            dbg_ref[...] += cg[FFN_CONV]
            dbv_ref[...] += cv[FFN_CONV]

    blk = lambda off: pl.BlockSpec((None, seq, cbk), lambda j, i: (i, 0, j + off))
    wsp = lambda r, off: pl.BlockSpec((r, cbk), lambda j, i: (0, j + off))
    half = jax.ShapeDtypeStruct((b, seq, f2 // 2), BF16)
    dwshape = jax.ShapeDtypeStruct((FFN_CONV, f2 // 2), F32)
    dbshape = jax.ShapeDtypeStruct((1, f2 // 2), F32)
    dg, dv, dwg, dwv, dbg, dbv = pl.pallas_call(
        body,
        out_shape=(half, half, dwshape, dwshape, dbshape, dbshape),
        grid=(nj, b),
        in_specs=[blk(0), blk(nj), blk(0), blk(0), blk(0), wsp(FFN_CONV, 0), wsp(FFN_CONV, nj)],
        out_specs=(blk(0), blk(0), wsp(FFN_CONV, 0), wsp(FFN_CONV, 0), wsp(1, 0), wsp(1, 0)),
        compiler_params=_params(("parallel", "arbitrary")), name=name)(hpre, hpre, pre_g, pre_v, da, cw, cw)
    return dg, dv, jnp.concatenate([dwg, dwv], axis=1), jnp.concatenate([dbg, dbv], axis=1)


def _ssd_conv_fwd(zx, cw, cb, d_inner, name):
    b, seq, _ = zx.shape
    xbc = cw.shape[1]
    cbk = SHIFT_COLS
    off = d_inner // cbk
    rows = min(SHIFT_ROWS, seq)

    def body(h_ref, w_ref, b_ref, o_ref, p_ref):
        w = w_ref[...]
        bias = b_ref[...]

        def chunk(c, carry):
            ext = _window(h_ref, c, rows, seq, HALO, 0, KEEP)
            acc = bias + w[3:4, :] * ext[KEEP:, :]
            for k in range(1, SSD_CONV):
                acc = acc + w[3 - k:4 - k, :] * _lag(ext, k)[KEEP:, :]
            here = pl.ds(pl.multiple_of(c * rows, rows), rows)
            o_ref[here, :] = acc * _sigmoid(acc)
            p_ref[here, :] = acc.astype(BF16)
            return carry

        lax.fori_loop(0, seq // rows, chunk, 0)

    blk = pl.BlockSpec((None, seq, cbk), lambda i, j: (i, 0, j))
    return pl.pallas_call(
        body, out_shape=(jax.ShapeDtypeStruct((b, seq, xbc), F32), jax.ShapeDtypeStruct((b, seq, xbc), BF16)),
        grid=(b, xbc // cbk),
        in_specs=[pl.BlockSpec((None, seq, cbk), lambda i, j: (i, 0, j + off)),
                  pl.BlockSpec((SSD_CONV, cbk), lambda i, j: (0, j)),
                  pl.BlockSpec((1, cbk), lambda i, j: (0, j))],
        out_specs=(blk, blk),
        compiler_params=_params(("parallel", "parallel")), name=name)(zx, cw, cb)


def _ssd_conv_bwd(zx, pre, dparts, ddt, dzx, cw, d_inner, name):
    b, seq, zw = zx.shape
    xbc = cw.shape[1]
    cbk = SHIFT_COLS
    off = d_inner // cbk
    rows = min(SHIFT_ROWS, seq)
    nblk = [p.shape[2] // cbk for p in dparts]
    first = [sum(nblk[:s]) for s in range(len(dparts))]
    nconv = xbc // cbk
    ncopy = ddt.shape[2] // cbk
    assert sum(nblk) == nconv and (off + nconv + ncopy) * cbk == zw and dzx.shape == (b, seq, zw)

    def body(h_ref, p_ref, gx_ref, gb_ref, gc_ref, t_ref, w_ref, z_ref, o_ref, dw_ref, db_ref):
        j = pl.program_id(0)

        @pl.when(j < nconv)
        def _():
            conv(h_ref, p_ref, gx_ref, gb_ref, gc_ref, w_ref, o_ref, dw_ref, db_ref)

        @pl.when(j >= nconv)
        def _():
            o_ref[...] = t_ref[...]

    def conv(h_ref, p_ref, gx_ref, gb_ref, gc_ref, w_ref, o_ref, dw_ref, db_ref):
        w = w_ref[...]
        j = pl.program_id(0)

        def chunk(c, carry):
            dws, dbias = carry
            here = pl.ds(pl.multiple_of(c * rows, rows), rows)
            pre = _window(p_ref, c, rows, seq, 0, HALO, KEEP)
            s = _sigmoid(pre)
            gsel = jnp.where(j < first[1], _window(gx_ref, c, rows, seq, 0, HALO, KEEP),
                             jnp.where(j < first[2], _window(gb_ref, c, rows, seq, 0, HALO, KEEP),
                                       _window(gc_ref, c, rows, seq, 0, HALO, KEEP)))
            dpre = gsel * (s * (1.0 + pre * (1.0 - s)))
            leads = [dpre] + [_lead(dpre, k) for k in range(1, SSD_CONV)]
            dx = w[3:4, :] * leads[0]
            for k in range(1, SSD_CONV):
                dx = dx + w[3 - k:4 - k, :] * leads[k]
            o_ref[here, :] = dx[:rows, :].astype(BF16)
            x0 = h_ref[here, :].astype(F32)
            dws = tuple(dws[k] + jnp.sum(leads[k][:rows, :] * x0, axis=0, keepdims=True) for k in range(SSD_CONV))
            dbias = dbias + jnp.sum(dpre[:rows, :], axis=0, keepdims=True)
            return dws, dbias

        z = jnp.zeros((1, cbk), F32)
        dws, dbias = lax.fori_loop(0, seq // rows, chunk, ((z,) * SSD_CONV, z))
        dwv = jnp.concatenate([dws[3 - i] for i in range(SSD_CONV)], axis=0)

        @pl.when(pl.program_id(1) == 0)
        def _():
            dw_ref[...] = dwv
            db_ref[...] = dbias

        @pl.when(pl.program_id(1) > 0)
        def _():
            dw_ref[...] += dwv
            db_ref[...] += dbias

    conv_j = lambda j: jnp.minimum(j, nconv - 1)
    return pl.pallas_call(
        body,
        out_shape=(jax.ShapeDtypeStruct((b, seq, zw), BF16), jax.ShapeDtypeStruct((SSD_CONV, xbc), F32),
                   jax.ShapeDtypeStruct((1, xbc), F32)),
        grid=(nconv + ncopy, b),
        in_specs=[pl.BlockSpec((None, seq, cbk), lambda j, i: (i, 0, conv_j(j) + off)),
                  pl.BlockSpec((None, seq, cbk), lambda j, i: (i, 0, conv_j(j)))] + [
                  pl.BlockSpec((None, seq, cbk), lambda j, i, s=s: (i, 0, jnp.clip(j - first[s], 0, nblk[s] - 1)))
                  for s in range(3)] + [
                  pl.BlockSpec((None, seq, cbk), lambda j, i: (i, 0, jnp.clip(j - nconv, 0, ncopy - 1))),
                  pl.BlockSpec((SSD_CONV, cbk), lambda j, i: (0, conv_j(j))),
                  ANY],
        out_specs=(pl.BlockSpec((None, seq, cbk), lambda j, i: (i, 0, j + off)),
                   pl.BlockSpec((SSD_CONV, cbk), lambda j, i: (0, conv_j(j))),
                   pl.BlockSpec((1, cbk), lambda j, i: (0, conv_j(j)))),
        input_output_aliases={7: 0},
        compiler_params=_params(("arbitrary", "arbitrary")), name=name)(zx, pre, *dparts, ddt, cw, dzx)


def _pool_sums(q, g, lead):
    sh = _lead if lead else _lag
    s2 = q + sh(q, 1)
    s4 = s2 + sh(s2, 2)
    s8 = s4 + sh(s4, 4)
    s16 = s8 + sh(s8, 8)
    return jnp.where(g == 0, s2, jnp.where(g == 1, s4, jnp.where(g == 2, s8, s16)))


def _pool_count(r0, n, g, shape):
    t = (r0 + lax.broadcasted_iota(jnp.int32, shape, 0) + 1).astype(F32)
    return jnp.minimum(t, (2 << g).astype(F32))


def _pool_fwd(h, pw, scale, name):
    b, seq, d = h.shape
    dg = d // 4
    rows = min(SHIFT_ROWS, seq)

    def body(h_ref, w_ref, s_ref, o_ref):
        g = pl.program_id(1)
        wmat = w_ref[...]
        sc = s_ref[...]

        def chunk(c, carry):
            r0 = c * rows
            ext = _window(h_ref, c, rows, seq, 16, 0)
            sums = _pool_sums(ext, g, False)[16:, :]
            mixed = sums / _pool_count(r0, rows, g, (rows, dg)) - ext[16:, :]
            o_ref[pl.ds(pl.multiple_of(r0, rows), rows), :] = _nn(mixed.astype(BF16), wmat) * sc
            return carry

        lax.fori_loop(0, seq // rows, chunk, 0)

    return pl.pallas_call(
        body, out_shape=jax.ShapeDtypeStruct((b, seq, d), F32), grid=(b, 4),
        in_specs=[pl.BlockSpec((None, seq, dg), lambda i, g: (i, 0, g)),
                  pl.BlockSpec((None, dg, dg), lambda i, g: (g, 0, 0)),
                  pl.BlockSpec((1, dg), lambda i, g: (0, g))],
        out_specs=pl.BlockSpec((None, seq, dg), lambda i, g: (i, 0, g)),
        compiler_params=_params(("parallel", "parallel")), name=name)(h, pw, scale)


def _pool_bwd(h, dout, pw, scale, name):
    b, seq, d = h.shape
    dg = d // 4
    rows = min(SHIFT_ROWS, seq)

    def body(h_ref, g_ref, w_ref, s_ref, o_ref, dw_ref, ds_ref, dw_acc):
        g = pl.program_id(0)
        wmat = w_ref[...]
        sc = s_ref[...]
        dw_acc[...] = jnp.zeros_like(dw_acc)

        def chunk(c, dsc):
            r0 = c * rows
            ext = _window(h_ref, c, rows, seq, 16, 0)
            sums = _pool_sums(ext, g, False)[16:, :]
            mixed = (sums / _pool_count(r0, rows, g, (rows, dg)) - ext[16:, :]).astype(BF16)
            gext = _window(g_ref, c, rows, seq, 0, 16)
            dsc = dsc + jnp.sum(gext[:rows, :] * _nn(mixed, wmat), axis=0, keepdims=True)
            dpre = (gext * sc).astype(BF16)
            dw_acc[...] += _tn(mixed, dpre[:rows, :])
            dmix = _nt(dpre, wmat)
            q = dmix / _pool_count(r0, rows + 16, g, (rows + 16, dg))
            back = _pool_sums(q, g, True)
            o_ref[pl.ds(pl.multiple_of(r0, rows), rows), :] = back[:rows, :] - dmix[:rows, :]
            return dsc

        dsc = lax.fori_loop(0, seq // rows, chunk, jnp.zeros((1, dg), F32))

        @pl.when(pl.program_id(1) == 0)
        def _():
            dw_ref[...] = dw_acc[...]
            ds_ref[...] = dsc

        @pl.when(pl.program_id(1) > 0)
        def _():
            dw_ref[...] += dw_acc[...]
            ds_ref[...] += dsc

    return pl.pallas_call(
        body,
        out_shape=(jax.ShapeDtypeStruct((b, seq, d), F32), jax.ShapeDtypeStruct((4, dg, dg), F32),
                   jax.ShapeDtypeStruct((1, d), F32)),
        grid=(4, b),
        in_specs=[pl.BlockSpec((None, seq, dg), lambda g, i: (i, 0, g)),
                  pl.BlockSpec((None, seq, dg), lambda g, i: (i, 0, g)),
                  pl.BlockSpec((None, dg, dg), lambda g, i: (g, 0, 0)),
                  pl.BlockSpec((1, dg), lambda g, i: (0, g))],
        out_specs=(pl.BlockSpec((None, seq, dg), lambda g, i: (i, 0, g)),
                   pl.BlockSpec((None, dg, dg), lambda g, i: (g, 0, 0)),
                   pl.BlockSpec((1, dg), lambda g, i: (0, g))),
        scratch_shapes=[pltpu.VMEM((dg, dg), F32)],
        compiler_params=_params(("parallel", "arbitrary")), name=name)(h, dout, pw, scale)


def _head_of(channel):
    return jnp.right_shift(channel, HEAD_DIM.bit_length() - 1)


def _ssd_consts(gw):
    q = CHUNK
    row = lax.broadcasted_iota(jnp.int32, (q, q), 0)
    col = lax.broadcasted_iota(jnp.int32, (q, q), 1)
    tril = (row >= col).astype(BF16)
    triu = (row <= col).astype(BF16)
    e = (_head_of(lax.broadcasted_iota(jnp.int32, (LANES, gw), 1))
         == lax.broadcasted_iota(jnp.int32, (LANES, gw), 0)).astype(BF16)
    et = (_head_of(lax.broadcasted_iota(jnp.int32, (gw, LANES), 0))
          == lax.broadcasted_iota(jnp.int32, (gw, LANES), 1)).astype(BF16)
    return row, col, tril, triu, e, et


def _ssd_common(dtr, dtb, alog, gw):
    q = CHUNK
    row, col, tril, triu, e, et = _ssd_consts(gw)
    dt = _softplus(dtr + dtb)
    a_row = -jnp.exp(alog)
    acum = _sel_left(tril, dt * a_row)
    ac_last = jnp.sum(jnp.where(row == q - 1, acum, 0.0), axis=0, keepdims=True)
    eac = jnp.exp(acum)
    de = jnp.exp(ac_last - acum)
    e2 = jnp.concatenate([e, e], axis=0)
    expand = _sel_right(jnp.concatenate([dt, eac, de], axis=0), e2, 2)
    dt_x, eac_x, de_x = expand[0:q], expand[q:2 * q], expand[2 * q:3 * q]
    acum_t = acum.T
    cd_col = jnp.exp(acum_t[:, q - 1:q])
    et3 = jnp.concatenate([et, et, et], axis=1)
    cdmat = _nn(et3, jnp.concatenate(_split(jnp.broadcast_to(cd_col, (LANES, D_STATE)), 3), axis=0))
    consts = dict(row=row, col=col, tril=tril, triu=triu, e=e, et=et)
    return dt, a_row, acum, acum_t, ac_last, eac, de, dt_x, eac_x, de_x, cdmat, consts


def _decay(acum, acum_t, j, row, col):
    diff = acum[:, j:j + 1] - acum_t[j:j + 1, :]
    return jnp.exp(jnp.where(row >= col, diff, -1e30))


def _ssd_fwd(xc, zx, dtr, dtb, alog, dskip, nw, d_inner, name):
    b, seq, xbc = xc.shape
    q = CHUNK
    nc = seq // q
    gw = d_inner // N_GROUPS
    nh = gw // HEAD_DIM
    xb0 = d_inner // D_STATE
    xc0 = xb0 + N_GROUPS
    dt0 = (d_inner + xbc) // LANES

    nb = max(n for n in (4, 2, 1) if b % n == 0)

    def body(x_ref, b_ref, c_ref, z_ref, dtr_ref, dtb_ref, al_ref, dsk_ref, nw_ref, y_ref, yn_ref, st_ref, s_ref):
        @pl.when(pl.program_id(2) == 0)
        def _():
            s_ref[...] = jnp.zeros_like(s_ref)

        for s in range(nb):
            one(s, x_ref.at[s], b_ref.at[s], c_ref.at[s], z_ref.at[s], dtr_ref.at[s], dtb_ref, al_ref, dsk_ref, nw_ref,
                y_ref.at[s], yn_ref.at[s], st_ref.at[s], s_ref.at[s])

    def one(s, x_ref, b_ref, c_ref, z_ref, dtr_ref, dtb_ref, al_ref, dsk_ref, nw_ref, y_ref, yn_ref, st_ref, s_ref):
        prev = s_ref[...]
        st_ref[...] = prev
        x = x_ref[...]
        bm = b_ref[...].astype(BF16)
        cm = c_ref[...].astype(BF16)
        (dt, a_row, acum, acum_t, ac_last, eac, de, dt_x, eac_x, de_x, cdmat, k) = _ssd_common(
            dtr_ref[...], dtb_ref[0:1, :], al_ref[0:1, :], gw)
        xdt = x * dt_x
        xdt_b = xdt.astype(BF16)
        cb = _nt(cm, bm)
        half = _head_of(lax.broadcasted_iota(jnp.int32, (q, LANES), 1))
        pairs = []
        for j in range(nh):
            pc = (j // 2) * LANES
            m = (cb * _decay(acum, acum_t, j, k["row"], k["col"])).astype(BF16)
            yj = jnp.where(half == j % 2, _nn(m, xdt_b[:, pc:pc + LANES]), 0.0)
            if j % 2 == 0:
                pairs.append(yj)
            else:
                pairs[-1] = pairs[-1] + yj
        prev_b = prev.astype(BF16)
        y = dsk_ref[0:1, :] * x + jnp.concatenate(pairs, axis=1) + eac_x * _nt(cm, prev_b)
        s_ref[...] = cdmat * prev + _tn((xdt * de_x).astype(BF16), bm)
        y_ref[...] = y
        z = z_ref[...].astype(F32)
        yg = y * (z * _sigmoid(z))
        r = lax.rsqrt(jnp.mean(yg * yg, axis=-1, keepdims=True) + EPS)
        yn_ref[...] = ((yg * r) * nw_ref[0:1, :]).astype(BF16)

    par = lambda w: pl.BlockSpec((None, 8, w), lambda i, g, c: (g, 0, 0))
    return pl.pallas_call(
        body,
        out_shape=(jax.ShapeDtypeStruct((b, seq, d_inner), F32), jax.ShapeDtypeStruct((b, seq, d_inner), BF16),
                   jax.ShapeDtypeStruct((b, nc, N_GROUPS, gw, D_STATE), F32)),
        grid=(b // nb, N_GROUPS, nc),
        in_specs=[pl.BlockSpec((nb, q, gw), lambda i, g, c: (i, c, g)),
                  pl.BlockSpec((nb, q, D_STATE), lambda i, g, c: (i, c, xb0 + g)),
                  pl.BlockSpec((nb, q, D_STATE), lambda i, g, c: (i, c, xc0 + g)),
                  pl.BlockSpec((nb, q, gw), lambda i, g, c: (i, c, g)),
                  pl.BlockSpec((nb, q, LANES), lambda i, g, c: (i, c, g)),
                  par(LANES), par(LANES), par(gw), par(gw)],
        out_specs=(pl.BlockSpec((nb, q, gw), lambda i, g, c: (i, c, g)),
                   pl.BlockSpec((nb, q, gw), lambda i, g, c: (i, c, g)),
                   pl.BlockSpec((nb, None, None, gw, D_STATE), lambda i, g, c: (i, c, g, 0, 0))),
        scratch_shapes=[pltpu.VMEM((nb, gw, D_STATE), F32)],
        compiler_params=_params(("parallel", "parallel", "arbitrary")), name=name,
    )(xc, xc, xc, zx, dtr, dtb, alog, dskip, nw)


def _ssd_bwd(xc, zx, dtr, y, dyn, st, dtb, alog, dskip, nw, d_inner, name):
    b, seq, xbc = xc.shape
    q = CHUNK
    nc = seq // q
    gw = d_inner // N_GROUPS
    nh = gw // HEAD_DIM
    xb0 = d_inner // D_STATE
    xc0 = xb0 + N_GROUPS
    dt0 = (d_inner + xbc) // LANES

    nb = max(n for n in (4, 2, 1) if b % n == 0)

    def body(x_ref, b_ref, c_ref, z_ref, dtr_ref, y_ref, g_ref, st_ref, dtb_ref, al_ref, dsk_ref, nw_ref,
             dz_ref, dx_ref, db_ref, dc_ref, ddt_ref, dnw_ref, dd_ref, dal_ref, dbias_ref,
             ds_ref, colbuf, rowbuf):
        first = jnp.logical_and(pl.program_id(1) == 0, pl.program_id(2) == 0)

        @pl.when(pl.program_id(2) == 0)
        def _():
            ds_ref[...] = jnp.zeros_like(ds_ref)

        sums = [one(x_ref.at[s], b_ref.at[s], c_ref.at[s], z_ref.at[s], dtr_ref.at[s], y_ref.at[s], g_ref.at[s],
                    st_ref.at[s], dtb_ref, al_ref, dsk_ref, nw_ref, dz_ref.at[s], dx_ref.at[s], db_ref.at[s],
                    dc_ref.at[s], ddt_ref.at[s], ds_ref.at[s], colbuf.at[s], rowbuf.at[s]) for s in range(nb)]
        dnw, dd, dal, dbias = [functools.reduce(lambda p, r: p + r, [sm[i] for sm in sums]) for i in range(4)]

        @pl.when(first)
        def _():
            dnw_ref[...] = jnp.broadcast_to(dnw, (8, gw))
            dd_ref[...] = dd
            dal_ref[...] = jnp.broadcast_to(dal, (8, LANES))
            dbias_ref[...] = jnp.broadcast_to(dbias, (8, LANES))

        @pl.when(jnp.logical_not(first))
        def _():
            dnw_ref[...] += jnp.broadcast_to(dnw, (8, gw))
            dd_ref[...] += dd
            dal_ref[...] += jnp.broadcast_to(dal, (8, LANES))
            dbias_ref[...] += jnp.broadcast_to(dbias, (8, LANES))

    def one(x_ref, b_ref, c_ref, z_ref, dtr_ref, y_ref, g_ref, st_ref, dtb_ref, al_ref, dsk_ref, nw_ref,
            dz_ref, dx_ref, db_ref, dc_ref, ddt_ref, ds_ref, colbuf, rowbuf):
        x = x_ref[...]
        bm = b_ref[...].astype(BF16)
        cm = c_ref[...].astype(BF16)
        z = z_ref[...].astype(F32)
        y = y_ref[...]
        prev = st_ref[...]
        dtr = dtr_ref[...] + dtb_ref[0:1, :]
        (dt, a_row, acum, acum_t, ac_last, eac, de, dt_x, eac_x, de_x, cdmat, k) = _ssd_common(
            dtr_ref[...], dtb_ref[0:1, :], al_ref[0:1, :], gw)
        row, col = k["row"], k["col"]
        et2 = jnp.concatenate([k["et"], k["et"]], axis=0)

        sz = _sigmoid(z)
        silu_z = z * sz
        yg = y * silu_z
        r = lax.rsqrt(jnp.mean(yg * yg, axis=-1, keepdims=True) + EPS)
        xh = yg * r
        dyn = g_ref[...].astype(F32)
        gh = dyn * nw_ref[0:1, :]
        dyg = r * (gh - xh * jnp.mean(gh * xh, axis=-1, keepdims=True))
        dnw = jnp.sum(dyn * xh, axis=0, keepdims=True)
        g = dyg * silu_z
        dz_ref[...] = (dyg * y * (sz * (1.0 + z * (1.0 - sz)))).astype(BF16)
        dd = _sel_right(jnp.broadcast_to(jnp.sum(g * x, axis=0, keepdims=True), (8, gw)), et2, 2)

        xdt = x * dt_x
        xdt_b = xdt.astype(BF16)
        g_b = g.astype(BF16)
        prev_b = prev.astype(BF16)
        cb = _nt(cm, bm)

        cp = _nt(cm, prev_b)
        ge = g * eac_x
        dac = _sel_right(ge * cp, et2, 2)
        ge_b = ge.astype(BF16)
        dcm = _nn(ge_b, prev_b)
        dprev = _tn(ge_b, cm)

        colbuf[...] = jnp.zeros_like(colbuf)
        rowbuf[...] = jnp.zeros_like(rowbuf)
        dcb = jnp.zeros((q, q), F32)
        half = _head_of(lax.broadcasted_iota(jnp.int32, (q, LANES), 1))
        pairs = []
        for j in range(nh):
            pc = (j // 2) * LANES
            dec = _decay(acum, acum_t, j, row, col)
            m = cb * dec
            gj = jnp.where(half == j % 2, g[:, pc:pc + LANES], 0.0).astype(BF16)
            dm = _nt(gj, xdt_b[:, pc:pc + LANES])
            w = dm * m
            colbuf[:, j:j + 1] = jnp.sum(w, axis=1, keepdims=True)
            rowbuf[j:j + 1, :] = jnp.sum(w, axis=0, keepdims=True)
            dcb = dcb + dm * dec
            dj = jnp.where(half == j % 2, _tn(m.astype(BF16), g_b[:, pc:pc + LANES]), 0.0)
            if j % 2 == 0:
                pairs.append(dj)
            else:
                pairs[-1] = pairs[-1] + dj
        dxdt = jnp.concatenate(pairs, axis=1)
        dcb_b = dcb.astype(BF16)
        dcm = dcm + _nn(dcb_b, bm)
        dbm = _tn(dcb_b, cm)

        ds = ds_ref[...]
        ds_b = ds.astype(BF16)
        u = _nt(bm, ds_b)
        dxdt = dxdt + u * de_x
        dde = _sel_right(u * xdt, et2, 2)
        dbm = dbm + _nn((xdt * de_x).astype(BF16), ds_b)
        pm = jnp.concatenate(_split(ds * prev, 2), axis=1)
        t2 = _tn(pm, k["et"])
        dcd_row = jnp.sum(t2[0:D_STATE] + t2[D_STATE:2 * D_STATE], axis=0, keepdims=True)
        last = dcd_row * jnp.exp(ac_last) + jnp.sum(dde * de, axis=0, keepdims=True)
        dac = dac + colbuf[...] - rowbuf[...].T - dde * de + jnp.where(row == q - 1, last, 0.0)
        ds_ref[...] = cdmat * ds + dprev

        dadt = _sel_left(k["triu"], dac)
        ddt = _sel_right(dxdt * x, et2, 2) + dadt * a_row
        dal = jnp.sum(dadt * dt, axis=0, keepdims=True) * a_row
        lane = lax.broadcasted_iota(jnp.int32, (q, LANES), 1)
        ddtr = jnp.where(lane < nh, ddt * _sigmoid(dtr), 0.0)
        ddt_ref[...] = ddtr.astype(BF16)
        dbias = jnp.sum(ddtr, axis=0, keepdims=True)
        dx_ref[...] = dxdt * dt_x + dsk_ref[0:1, :] * g
        db_ref[...] = dbm
        dc_ref[...] = dcm
        return dnw, dd, dal, dbias

    rc = lambda c: nc - 1 - c
    par = lambda w: pl.BlockSpec((None, 8, w), lambda g, i, c: (g, 0, 0))
    blk = lambda w: pl.BlockSpec((nb, q, w), lambda g, i, c: (i, rc(c), g))
    return pl.pallas_call(
        body,
        out_shape=(jax.ShapeDtypeStruct((b, seq, zx.shape[2]), BF16),
                   jax.ShapeDtypeStruct((b, seq, d_inner), F32),
                   jax.ShapeDtypeStruct((b, seq, N_GROUPS * D_STATE), F32),
                   jax.ShapeDtypeStruct((b, seq, N_GROUPS * D_STATE), F32),
                   jax.ShapeDtypeStruct((b, seq, N_GROUPS * LANES), BF16),
                   jax.ShapeDtypeStruct((N_GROUPS, 8, gw), F32),
                   jax.ShapeDtypeStruct((N_GROUPS, 8, LANES), F32),
                   jax.ShapeDtypeStruct((N_GROUPS, 8, LANES), F32),
                   jax.ShapeDtypeStruct((N_GROUPS, 8, LANES), F32)),
        grid=(N_GROUPS, b // nb, nc),
        in_specs=[blk(gw),
                  pl.BlockSpec((nb, q, D_STATE), lambda g, i, c: (i, rc(c), xb0 + g)),
                  pl.BlockSpec((nb, q, D_STATE), lambda g, i, c: (i, rc(c), xc0 + g)),
                  blk(gw),
                  pl.BlockSpec((nb, q, LANES), lambda g, i, c: (i, rc(c), g)),
                  blk(gw), blk(gw),
                  pl.BlockSpec((nb, None, None, gw, D_STATE), lambda g, i, c: (i, rc(c), g, 0, 0)),
                  par(LANES), par(LANES), par(gw), par(gw)],
        out_specs=(blk(gw), blk(gw), blk(D_STATE), blk(D_STATE), blk(LANES),
                   par(gw), par(LANES), par(LANES), par(LANES)),
        scratch_shapes=[pltpu.VMEM((nb, gw, D_STATE), F32), pltpu.VMEM((nb, q, LANES), F32),
                        pltpu.VMEM((nb, LANES, q), F32)],
        compiler_params=_params(("parallel", "arbitrary", "arbitrary")), name=name,
    )(xc, xc, xc, zx, dtr, y, dyn, st, dtb, alog, dskip, nw)


def _adamw(w, g, m, v, name):
    rows, cols = w.shape
    tr = rows
    for cand in (512, 256, 128, 64, 32, 16, 8):
        if rows % cand == 0 and cand * cols * 4 <= 2 * 1024 * 1024:
            tr = cand
            break
    c1 = 1.0 - ADAM_B1 ** ADAM_STEP
    c2 = 1.0 - ADAM_B2 ** ADAM_STEP

    def body(w_ref, g_ref, m_ref, v_ref, d_ref, mo_ref, vo_ref):
        gv = g_ref[...]
        mn = ADAM_B1 * m_ref[...] + (1.0 - ADAM_B1) * gv
        vn = ADAM_B2 * v_ref[...] + (1.0 - ADAM_B2) * (gv * gv)
        mo_ref[...] = mn
        vo_ref[...] = vn
        d_ref[...] = -ADAM_LR * ((mn / c1) / (jnp.sqrt(vn / c2) + ADAM_EPS) + ADAM_WD * w_ref[...])

    spec = pl.BlockSpec((tr, cols), lambda i: (i, 0))
    shp = jax.ShapeDtypeStruct((rows, cols), F32)
    return pl.pallas_call(body, out_shape=(shp, shp, shp), grid=(rows // tr,), in_specs=[spec] * 4,
                          out_specs=(spec,) * 3, compiler_params=_params(("parallel",)), name=name)(w, g, m, v)


def _pick_rows(rows, row_bytes, limit=1 << 20):
    for cand in (2048, 1024, 512, 256, 128, 64, 32, 16):
        if rows % cand == 0 and cand * row_bytes <= limit:
            return cand
    return rows


def _as3d(a, lead):
    return a.reshape(a.shape[:lead] + (-1, a.shape[-1]))


def _pair_sum(g, got, core, name):
    h = got.shape[0]
    g3, got3 = _as3d(g, 1), _as3d(got, 1)
    _, rows, cols = got3.shape
    tr = _pick_rows(rows, cols * 4)

    def body(c_ref, g_ref, r_ref, o_ref):
        o_ref[...] = (g_ref[...] + r_ref[...]).astype(BF16)

    out = pl.pallas_call(
        body, out_shape=jax.ShapeDtypeStruct(got3.shape, BF16),
        grid_spec=pltpu.PrefetchScalarGridSpec(
            num_scalar_prefetch=1, grid=(h, rows // tr),
            in_specs=[pl.BlockSpec((None, tr, cols), lambda l, i, c_ref: (c_ref[0] * h + l, i, 0)),
                      pl.BlockSpec((None, tr, cols), lambda l, i, c_ref: (l, i, 0))],
            out_specs=pl.BlockSpec((None, tr, cols), lambda l, i, c_ref: (l, i, 0))),
        compiler_params=_params(("parallel", "parallel")), name=name)(core, g3, got3)
    return out.reshape(got.shape)


def _sum4(q, core, name):
    q4 = _as3d(q, 2)
    _, h, rows, cols = q4.shape
    tr = _pick_rows(rows, cols * 4)

    def body(c_ref, q0, q1, q2, q3, o_ref):
        o_ref[...] = ((q0[...].astype(F32) + q1[...].astype(F32)) + q2[...].astype(F32)) + q3[...].astype(F32)

    out = pl.pallas_call(
        body, out_shape=jax.ShapeDtypeStruct((2 * h, rows, cols), F32),
        grid_spec=pltpu.PrefetchScalarGridSpec(
            num_scalar_prefetch=1, grid=(h, rows // tr),
            in_specs=[pl.BlockSpec((None, None, tr, cols), lambda l, i, c_ref, k=k: (k, l, i, 0))
                      for k in range(N_CHIPS)],
            out_specs=pl.BlockSpec((None, tr, cols), lambda l, i, c_ref: (c_ref[0] * h + l, i, 0))),
        compiler_params=_params(("parallel", "parallel")), name=name)(core, q4, q4, q4, q4)
    return out.reshape((2 * h,) + q.shape[2:])


def _coords():
    return lax.axis_index("x"), lax.axis_index("y"), lax.axis_index("c")


def _other_chips(x, y):
    return [(1 - x, y), (x, 1 - y), (1 - x, 1 - y)]


def _allgather_halves(src, name):
    rows, cols = src.shape

    def body(x_ref, o_ref, send, recv, local):
        x, y, c = _coords()
        sib = (x, y, 1 - c)
        chips = _other_chips(x, y)

        def slot(h, cx, cy):
            return o_ref.at[h, 2 * cx + cy]

        def copy(kk, dst, to, src_ref):
            return pltpu.make_async_remote_copy(src_ref=src_ref, dst_ref=dst, send_sem=send.at[kk],
                                                recv_sem=recv.at[kk], device_id=to, device_id_type=MESH)

        mine = pltpu.make_async_copy(x_ref, slot(c, x, y), local)
        mine.start()
        first = [copy(0, slot(c, x, y), sib, x_ref)]
        first += [copy(1 + j, slot(c, x, y), (*chip, c), x_ref) for j, chip in enumerate(chips)]
        for cp in first:
            cp.start()
        passed = [copy(4 + j, slot(c, *chip), sib, slot(c, *chip)) for j, chip in enumerate(chips)]
        for j, chip in enumerate(chips):
            copy(1 + j, slot(c, *chip), (x, y, c), x_ref).wait_recv()
            passed[j].start()
        copy(0, slot(1 - c, x, y), (x, y, c), x_ref).wait_recv()
        for j, chip in enumerate(chips):
            copy(4 + j, slot(1 - c, *chip), (x, y, c), x_ref).wait_recv()
        for cp in first + passed:
            cp.wait_send()
        mine.wait()

    return pl.pallas_call(
        body, out_shape=jax.ShapeDtypeStruct((2, N_CHIPS, rows, cols), src.dtype),
        in_specs=[ANY], out_specs=ANY,
        scratch_shapes=[pltpu.SemaphoreType.DMA((7,)), pltpu.SemaphoreType.DMA((7,)), pltpu.SemaphoreType.DMA],
        name=name)(src)


MIXW = (("ssd_w_in", None), ("ssd_w_out", 0), ("pool_w", 1))
FFNW = (("ffn_w_up", 1), ("ffn_w_down", 0))


def _chip_window(axis, ref, layers, k):
    if axis is None:
        return ref.at[layers, k]
    n = ref.shape[1 + axis] // N_CHIPS
    sl = pl.ds(pl.multiple_of(k * n, LANES if 1 + axis == len(ref.shape) - 1 else 8), n)
    idx = [layers] + [slice(None)] * (len(ref.shape) - 1)
    idx[1 + axis] = sl
    return ref.at[tuple(idx)]


def _full_shape(axis, shard_shape):
    if axis is None:
        return (shard_shape[0], N_CHIPS) + tuple(shard_shape[1:])
    full = list(shard_shape)
    full[1 + axis] *= N_CHIPS
    return tuple(full)


HBM_SPEC = pl.BlockSpec(memory_space=pltpu.HBM)
SEM_SPEC = pl.BlockSpec(memory_space=pltpu.SEMAPHORE)


def _dma_sems(count):
    return pltpu.SemaphoreType.DMA((max(count, 1),))


def _wait_for(copy, kind):
    if kind == "recv":
        copy.wait_recv()
    elif kind == "send":
        copy.wait_send()
    else:
        copy.wait()


def _comm_fused(stages, counts, srcs, lands, name, inplace=False):
    ns, nl, k = len(srcs), len(lands), len(stages)

    def body(*refs):
        src_refs = refs[:ns]
        land_refs = refs[ns + (nl if inplace else 0):ns + (nl if inplace else 0) + nl]
        sem_refs = refs[len(refs) - 3 * k:]
        for s, stage_fn in enumerate(stages):
            starts, waits = stage_fn(src_refs, land_refs, tuple(sem_refs[3 * s:3 * s + 3]))
            for cp in starts:
                cp.start()
            for cp, kind in waits:
                _wait_for(cp, kind)

    scratch = []
    for cnt in counts:
        scratch += [_dma_sems(c) for c in cnt]
    outs = pl.pallas_call(
        body, out_shape=tuple(jax.ShapeDtypeStruct(a.shape, a.dtype) for a in lands),
        in_specs=[ANY] * (ns + (nl if inplace else 0)), out_specs=(ANY,) * nl,
        input_output_aliases={ns + i: i for i in range(nl)} if inplace else {},
        scratch_shapes=scratch, name=name)(*srcs, *(lands if inplace else ()))
    return list(outs)


class _SplitComm:
    def __init__(self, stages, counts, srcs, lands, name):
        self.stages, self.counts, self.name = stages, counts, name
        self.ns = len(srcs)
        self.data = [pltpu.with_memory_space_constraint(a, pltpu.HBM) for a in list(srcs) + list(lands)]
        self.sems = None
        self.step = 0

    def advance(self, after=None):
        i, k, nd, ns = self.step, len(self.stages), len(self.data), self.ns
        first, last = i == 0, i == k
        stages = self.stages
        after = list(after) if isinstance(after, (list, tuple)) else [after]

        def body(*refs):
            data = refs[:nd]
            pos = nd
            if not first:
                old = tuple(refs[pos:pos + 3])
                pos += 3 + len(after)
            if not last:
                new = tuple(refs[pos:pos + 3])
            if not first:
                for cp, kind in stages[i - 1](data[:ns], data[ns:], old)[1]:
                    _wait_for(cp, kind)
            if not last:
                for cp in stages[i](data[:ns], data[ns:], new)[0]:
                    cp.start()
                refs[len(refs) - 1][...] = jnp.zeros((8, LANES), F32)

        args = list(self.data)
        in_specs = [HBM_SPEC] * nd
        if not first:
            args += list(self.sems) + after
            in_specs += [SEM_SPEC] * 3 + [ANY] * len(after)
        out_shape, out_specs = [], []
        if not last:
            out_shape += [_dma_sems(c) for c in self.counts[i]]
            out_specs += [SEM_SPEC] * 3
        out_shape += [pltpu.HBM(a.shape, a.dtype) for a in self.data]
        out_specs += [HBM_SPEC] * nd
        if not last:
            out_shape.append(jax.ShapeDtypeStruct((8, LANES), F32))
            out_specs.append(pl.BlockSpec(memory_space=pltpu.VMEM))
        off = 0 if last else 3
        outs = pl.pallas_call(
            body, out_shape=tuple(out_shape), in_specs=in_specs, out_specs=tuple(out_specs),
            input_output_aliases={d: off + d for d in range(nd)},
            compiler_params=pltpu.CompilerParams(has_side_effects=pltpu.SideEffectType.DATAFLOW_SIDE_EFFECTING),
            name=f"{self.name}_{i}")(*args)
        self.sems = None if last else outs[:3]
        self.data = list(outs[off:off + nd])
        self.step += 1
        return None if last else outs[len(outs) - 1]

    def lands(self):
        return self.data[self.ns:]


def _gather_stages(spec):
    n = len(spec)

    def parts(srcs, lands):
        x, y, c = _coords()
        out = []
        for w, (_, axis) in enumerate(spec):
            h = srcs[w].shape[0] // 2
            mine, theirs = pl.ds(c * h, h), pl.ds((1 - c) * h, h)
            out.append((srcs[w].at[mine], lambda layers, k, w=w, axis=axis: _chip_window(axis, lands[w], layers, k),
                        mine, theirs))
        return x, y, c, 2 * x + y, (x, y, 1 - c), _other_chips(x, y), out

    def remote(src, dst, send, recv, idx, to):
        return pltpu.make_async_remote_copy(src_ref=src, dst_ref=dst, send_sem=send.at[idx], recv_sem=recv.at[idx],
                                            device_id=to, device_id_type=MESH)

    def stage0(srcs, lands, sems):
        send, recv, local = sems
        x, y, c, me, sib, chips, ps = parts(srcs, lands)
        starts, waits = [], []
        for w, (src, dst, mine, theirs) in enumerate(ps):
            lc = pltpu.make_async_copy(src, dst(mine, me), local.at[w])
            first = [remote(src, dst(mine, me), send, recv, 4 * w, sib)]
            first += [remote(src, dst(mine, me), send, recv, 4 * w + 1 + j, (cx, cy, c)) for j, (cx, cy) in enumerate(chips)]
            starts += [lc] + first
            waits.append((remote(src, dst(theirs, me), send, recv, 4 * w, (x, y, c)), "recv"))
            waits += [(remote(src, dst(mine, 2 * cx + cy), send, recv, 4 * w + 1 + j, (x, y, c)), "recv")
                      for j, (cx, cy) in enumerate(chips)]
            waits += [(cp, "send") for cp in first] + [(lc, "local")]
        return starts, waits

    def stage1(srcs, lands, sems):
        send, recv, _ = sems
        x, y, c, me, sib, chips, ps = parts(srcs, lands)
        starts, waits = [], []
        for w, (src, dst, mine, theirs) in enumerate(ps):
            for j, (cx, cy) in enumerate(chips):
                blk = dst(mine, 2 * cx + cy)
                fwd = remote(blk, blk, send, recv, 3 * w + j, sib)
                starts.append(fwd)
                waits.append((remote(src, dst(theirs, 2 * cx + cy), send, recv, 3 * w + j, (x, y, c)), "recv"))
                waits.append((fwd, "send"))
        return starts, waits

    return [stage0, stage1], [(4 * n, 4 * n, n), (3 * n, 3 * n, 0)]


def _swap_stages(spec):
    n = len(spec)

    def stage(srcs, lands, sems):
        send, recv, _ = sems
        x, y, c = _coords()
        starts, waits = [], []
        for w in range(n):
            h = srcs[w].shape[0] // 2
            cp = pltpu.make_async_remote_copy(src_ref=srcs[w].at[pl.ds((1 - c) * h, h)], dst_ref=lands[w],
                                              send_sem=send.at[w], recv_sem=recv.at[w],
                                              device_id=(x, y, 1 - c), device_id_type=MESH)
            starts.append(cp)
            waits += [(cp, "recv"), (cp, "send")]
        return starts, waits

    return [stage], [(n, n, 0)]


def _scatter_stages(spec):
    n = len(spec)

    def stage(srcs, lands, sems):
        send, recv, local = sems
        x, y, c = _coords()
        me = 2 * x + y
        starts, waits = [], []
        for w, (_, axis) in enumerate(spec):
            layers = pl.ds(0, srcs[w].shape[0])
            own = _chip_window(axis, srcs[w], layers, me)
            lc = pltpu.make_async_copy(own, lands[w].at[me], local.at[w])
            starts.append(lc)
            for j, (cx, cy) in enumerate(_other_chips(x, y)):
                cp = pltpu.make_async_remote_copy(src_ref=_chip_window(axis, srcs[w], layers, 2 * cx + cy),
                                                  dst_ref=lands[w].at[me], send_sem=send.at[3 * w + j],
                                                  recv_sem=recv.at[3 * w + j], device_id=(cx, cy, c), device_id_type=MESH)
                starts.append(cp)
                waits.append((pltpu.make_async_remote_copy(
                    src_ref=own, dst_ref=lands[w].at[2 * cx + cy], send_sem=send.at[3 * w + j], recv_sem=recv.at[3 * w + j],
                    device_id=(x, y, c), device_id_type=MESH), "recv"))
                waits.append((cp, "send"))
            waits.append((lc, "local"))
        return starts, waits

    return [stage], [(3 * n, 3 * n, n)]


def _share_stages(spec):
    n = len(spec)

    def stage(srcs, lands, sems):
        send, recv, _ = sems
        x, y, c = _coords()
        starts, waits = [], []
        for w in range(n):
            h = lands[w].shape[0] // 2
            mine, theirs = lands[w].at[pl.ds(c * h, h)], lands[w].at[pl.ds((1 - c) * h, h)]
            cp = pltpu.make_async_remote_copy(src_ref=mine, dst_ref=mine, send_sem=send.at[w], recv_sem=recv.at[w],
                                              device_id=(x, y, 1 - c), device_id_type=MESH)
            starts.append(cp)
            waits.append((pltpu.make_async_remote_copy(src_ref=theirs, dst_ref=theirs, send_sem=send.at[w],
                                                       recv_sem=recv.at[w], device_id=(x, y, c), device_id_type=MESH),
                          "recv"))
            waits.append((cp, "send"))
        return starts, waits

    return [stage], [(n, n, 0)]


def _shard_of(p, axis):
    if axis is None:
        return (p.shape[0],) + tuple(p.shape[2:])
    s = list(p.shape)
    s[1 + axis] //= N_CHIPS
    return tuple(s)


def _gather8_stages():
    def stage(srcs, lands, sems):
        send, recv, local = sems
        x, y, c = _coords()
        me = 4 * x + 2 * y + c
        lc = pltpu.make_async_copy(srcs[0], lands[0].at[me], local.at[0])
        starts, waits = [lc], []
        for kk in range(1, 8):
            to = (1 - x if kk & 4 else x, 1 - y if kk & 2 else y, 1 - c if kk & 1 else c)
            cp = pltpu.make_async_remote_copy(src_ref=srcs[0], dst_ref=lands[0].at[me], send_sem=send.at[kk - 1],
                                              recv_sem=recv.at[kk - 1], device_id=to, device_id_type=MESH)
            starts.append(cp)
            waits.append((pltpu.make_async_remote_copy(
                src_ref=srcs[0], dst_ref=lands[0].at[4 * to[0] + 2 * to[1] + to[2]], send_sem=send.at[kk - 1],
                recv_sem=recv.at[kk - 1], device_id=(x, y, c), device_id_type=MESH), "recv"))
            waits.append((cp, "send"))
        waits.append((lc, "local"))
        return starts, waits

    return [stage], [(7, 7, 1)]


def _sum8(buf, name):
    _, rows, cols = buf.shape
    tr = _pick_rows(rows, cols * 4)

    def body(*refs):
        acc = refs[0][...]
        for r in refs[1:8]:
            acc = acc + r[...]
        refs[8][...] = acc

    return pl.pallas_call(
        body, out_shape=jax.ShapeDtypeStruct((rows, cols), F32), grid=(rows // tr,),
        in_specs=[pl.BlockSpec((None, tr, cols), lambda i, k=k: (k, i, 0)) for k in range(8)],
        out_specs=pl.BlockSpec((tr, cols), lambda i: (i, 0)),
        compiler_params=_params(("parallel",)), name=name)(*([buf] * 8))


def _reduce_begin(spec, gs, core, tag, riders=()):
    stages, counts = _swap_stages(spec)
    got = _comm_fused(stages, counts, list(gs) + list(riders),
                      [jax.ShapeDtypeStruct((g.shape[0] // 2,) + g.shape[1:], g.dtype) for g in gs], "swap_" + tag)
    pair = [_pair_sum(a, r, core, "pair_sum_" + n) for a, r, (n, _) in zip(gs, got, spec)]
    stages, counts = _scatter_stages(spec)
    lands = [lax.empty((N_CHIPS,) + _shard_of(p, axis), p.dtype) for p, (_, axis) in zip(pair, spec)]
    comm = _SplitComm(stages, counts, pair, lands, "scatter_" + tag)
    return comm, comm.advance()


def _reduce_finish(spec, comm, core, tag, after):
    comm.advance(after=after)
    halves = [_sum4(q, core, "sum4_" + n) for q, (n, _) in zip(comm.lands(), spec)]
    stages, counts = _share_stages(spec)
    return _comm_fused(stages, counts, [], halves, "share_" + tag, inplace=True)


SMALL = (("ssd_conv_w", 2), ("pool_scale", 1), ("ffn_conv_w", 2))
REPL = ("ssd_conv_b", "ssd_dt_bias", "ssd_a_log", "ssd_d", "ssd_norm_w", "ffn_conv_b",
        "norm_mix_pre", "norm_mix_post", "norm_ffn_pre", "norm_ffn_post")
WEIGHTS = ("ssd_w_in", "ssd_conv_w", "ssd_conv_b", "ssd_dt_bias", "ssd_a_log", "ssd_d", "ssd_norm_w", "ssd_w_out",
           "pool_w", "pool_scale", "ffn_w_up", "ffn_conv_w", "ffn_conv_b", "ffn_w_down", "norm_mix_pre",
           "norm_mix_post", "norm_ffn_pre", "norm_ffn_post")


def _flat_rows(n):
    unit = 2 * 16 * FLAT_COLS
    return 2 * 16 * ((n + unit - 1) // unit)


def _flatten_shards(arrs, dtype):
    flat = jnp.concatenate([a.astype(dtype).reshape(-1) for a in arrs])
    rows = _flat_rows(flat.shape[0])
    flat = jnp.pad(flat, (0, rows * FLAT_COLS - flat.shape[0]))
    return flat.reshape(2, rows // 2, FLAT_COLS)


def _unflatten_full(gathered, shard_shapes, axes):
    per_chip = jnp.swapaxes(gathered, 0, 1).reshape(N_CHIPS, -1)
    out, off = [], 0
    for shp, ax in zip(shard_shapes, axes):
        n = math.prod(shp)
        pieces = [per_chip[k, off:off + n].reshape(shp) for k in range(N_CHIPS)]
        out.append(jnp.concatenate(pieces, axis=ax))
        off += n
    return out


def kernel(x, ssd_w_in, ssd_conv_w, ssd_conv_b, ssd_dt_bias, ssd_a_log, ssd_d, ssd_norm_w, ssd_w_out, pool_w, pool_scale, ffn_w_up, ffn_conv_w, ffn_conv_b, ffn_w_down, norm_mix_pre, norm_mix_post, norm_ffn_pre, norm_ffn_post, loss_target, m_ssd_w_in, m_ssd_conv_w, m_ssd_conv_b, m_ssd_dt_bias, m_ssd_a_log, m_ssd_d, m_ssd_norm_w, m_ssd_w_out, m_pool_w, m_pool_scale, m_ffn_w_up, m_ffn_conv_w, m_ffn_conv_b, m_ffn_w_down, m_norm_mix_pre, m_norm_mix_post, m_norm_ffn_pre, m_norm_ffn_post, v_ssd_w_in, v_ssd_conv_w, v_ssd_conv_b, v_ssd_dt_bias, v_ssd_a_log, v_ssd_d, v_ssd_norm_w, v_ssd_w_out, v_pool_w, v_pool_scale, v_ffn_w_up, v_ffn_conv_w, v_ffn_conv_b, v_ffn_w_down, v_norm_mix_pre, v_norm_mix_post, v_norm_ffn_pre, v_norm_ffn_post):
    wts = dict(ssd_w_in=ssd_w_in, ssd_conv_w=ssd_conv_w, ssd_conv_b=ssd_conv_b, ssd_dt_bias=ssd_dt_bias,
               ssd_a_log=ssd_a_log, ssd_d=ssd_d, ssd_norm_w=ssd_norm_w, ssd_w_out=ssd_w_out, pool_w=pool_w,
               pool_scale=pool_scale, ffn_w_up=ffn_w_up, ffn_conv_w=ffn_conv_w, ffn_conv_b=ffn_conv_b,
               ffn_w_down=ffn_w_down, norm_mix_pre=norm_mix_pre, norm_mix_post=norm_mix_post,
               norm_ffn_pre=norm_ffn_pre, norm_ffn_post=norm_ffn_post)
    mom = dict(ssd_w_in=m_ssd_w_in, ssd_conv_w=m_ssd_conv_w, ssd_conv_b=m_ssd_conv_b, ssd_dt_bias=m_ssd_dt_bias,
               ssd_a_log=m_ssd_a_log, ssd_d=m_ssd_d, ssd_norm_w=m_ssd_norm_w, ssd_w_out=m_ssd_w_out, pool_w=m_pool_w,
               pool_scale=m_pool_scale, ffn_w_up=m_ffn_w_up, ffn_conv_w=m_ffn_conv_w, ffn_conv_b=m_ffn_conv_b,
               ffn_w_down=m_ffn_w_down, norm_mix_pre=m_norm_mix_pre, norm_mix_post=m_norm_mix_post,
               norm_ffn_pre=m_norm_ffn_pre, norm_ffn_post=m_norm_ffn_post)
    var = dict(ssd_w_in=v_ssd_w_in, ssd_conv_w=v_ssd_conv_w, ssd_conv_b=v_ssd_conv_b, ssd_dt_bias=v_ssd_dt_bias,
               ssd_a_log=v_ssd_a_log, ssd_d=v_ssd_d, ssd_norm_w=v_ssd_norm_w, ssd_w_out=v_ssd_w_out, pool_w=v_pool_w,
               pool_scale=v_pool_scale, ffn_w_up=v_ffn_w_up, ffn_conv_w=v_ffn_conv_w, ffn_conv_b=v_ffn_conv_b,
               ffn_w_down=v_ffn_w_down, norm_mix_pre=v_norm_mix_pre, norm_mix_post=v_norm_mix_post,
               norm_ffn_pre=v_norm_ffn_pre, norm_ffn_post=v_norm_ffn_post)

    bl, seq, d = x.shape
    t = bl * seq
    depth = norm_mix_pre.shape[0]
    n_ssd = ssd_w_out.shape[0]
    d_inner = ssd_w_out.shape[1] * N_CHIPS
    nheads = d_inner // HEAD_DIM
    hpg = nheads // N_GROUPS
    gw = d_inner // N_GROUPS
    xbc = ssd_conv_w.shape[2] * N_CHIPS
    f2 = ffn_w_up.shape[2] * N_CHIPS
    ff = f2 // 2
    dg = d // 4
    cy = lax.axis_index("c")
    chip = 2 * lax.axis_index("x") + lax.axis_index("y")

    small_shapes = [wts[n].shape for n, _ in SMALL]
    small_axes = [a for _, a in SMALL]
    small_flat = _flatten_shards([wts[n] for n, _ in SMALL], F32)
    small_half = lax.dynamic_index_in_dim(small_flat, cy, 0, keepdims=False)
    small_all = _allgather_halves(small_half, "gather_small")
    conv_w, p_scale, f_conv_w = _unflatten_full(small_all, small_shapes, small_axes)
    def full_shapes(spec, shards):
        return [jax.ShapeDtypeStruct(_full_shape(axis, s.shape), s.dtype) for s, (_, axis) in zip(shards, spec)]

    def row_halves(a):
        return a.reshape((2, a.shape[0] // 2) + a.shape[1:])

    def join_w_in(g):
        return jnp.concatenate([g[:, k] for k in range(N_CHIPS)], axis=-1).reshape(d, -1)

    def join_w_out(g):
        r2 = g.shape[1] // N_CHIPS
        return jnp.concatenate([g[hf, k * r2:(k + 1) * r2] for k in range(N_CHIPS) for hf in range(2)], axis=0)

    ssd_spec = (("ssd_w_in", None), ("ssd_w_out", 0))
    first_shards = [row_halves(wts[n][0].astype(BF16)) for n, _ in ssd_spec]
    stages, counts = _gather_stages(ssd_spec)
    g_in0, g_out0 = _comm_fused(stages, counts, first_shards, full_shapes(ssd_spec, first_shards), "gather_first")
    w_in, w_out = [join_w_in(g_in0)], [join_w_out(g_out0)]
    rest_spec = ssd_spec * (n_ssd - 1) + (("pool_w", 1),) + FFNW
    rest_shards = [row_halves(wts[n][jj].astype(BF16)) for jj in range(1, n_ssd) for n, _ in ssd_spec]
    rest_shards += [wts["pool_w"].astype(BF16)] + [wts[n].astype(BF16) for n, _ in FFNW]
    stages, counts = _gather_stages(rest_spec)
    ffn_gather = _SplitComm(stages, counts, rest_shards + [g_out0],
                            [lax.empty(s.shape, s.dtype) for s in full_shapes(rest_spec, rest_shards)], "gather_rest")
    gather_token = ffn_gather.advance()

    def pad_heads(a):
        lead = a.shape[:-1]
        a = a.reshape(lead + (N_GROUPS, hpg))
        a = jnp.pad(a, [(0, 0)] * len(lead) + [(0, 0), (0, LANES - hpg)])
        return a.reshape(lead + (N_GROUPS * LANES,))

    def unpad_heads(a):
        lead = a.shape[:-1]
        return a.reshape(lead + (N_GROUPS, LANES))[..., :hpg].reshape(lead + (nheads,))

    def group_rows(a, width):
        return jnp.broadcast_to(a.reshape(N_GROUPS, 1, width), (N_GROUPS, 8, width))

    def pad_w_in(w):
        return jnp.concatenate([w[..., :d_inner + xbc], pad_heads(w[..., d_inner + xbc:])], axis=-1)

    w_in_p = [pad_w_in(w_in[0])]
    zw = w_in_p[0].shape[-1]
    w_pool = None

    x2 = x.reshape(t, d)
    tgt2 = loss_target.reshape(t, d)
    w_up = w_down = None

    saved = []
    cur = x2
    tokens = []
    h = _norm_fwd(cur, norm_mix_pre[0:1], BF16, "norm_pre_b", after=[gather_token])
    for i in range(depth):
        j = i // 2
        sv = dict(x_in=cur)
        if i % 2 == 0:
            zx = _mm(h, w_in_p[j], "nn", BF16, "mm_ssd_in", 2048, 512, d).reshape(bl, seq, zw)
            dtr = _mm(h, w_in_p[j][:, d_inner + xbc:], "nn", F32, "mm_ssd_dt", 2048, 512, d).reshape(bl, seq, -1)
            xc, xpre = _ssd_conv_fwd(zx, conv_w[j], ssd_conv_b[j:j + 1], d_inner, "ssd_conv_fwd")
            dtb = group_rows(pad_heads(ssd_dt_bias[j]), LANES)
            alog = group_rows(pad_heads(ssd_a_log[j]), LANES)
            dskip = group_rows(jnp.repeat(ssd_d[j], HEAD_DIM), gw)
            nw = group_rows(ssd_norm_w[j], gw)
            y, yn, st = _ssd_fwd(xc, zx, dtr, dtb, alog, dskip, nw, d_inner, "ssd_fwd")
            if i == 0:
                tokens.append(ffn_gather.advance(after=yn))
            mix = _mm(yn.reshape(t, d_inner), w_out[j], "nn", F32, "mm_ssd_out", 2048, 512, d_inner)
            sv.update(h=h, zx=zx, dtr=dtr, xc=xc, xpre=xpre, y=y, yn=yn, st=st, dtb=dtb, alog=alog, dskip=dskip, nw=nw)
        else:
            mix = _pool_fwd(h.reshape(bl, seq, d), w_pool[j], p_scale[j:j + 1], "pool_fwd").reshape(t, d)
            sv.update(h=h)
        sv.update(mix=mix)
        mid, u = _norm_post_pre(mix, norm_mix_post[i:i + 1], cur, norm_ffn_pre[i:i + 1], BF16, "norm_post_pre_b",
                                after=tokens)
        tokens = []
        if i == 0:
            ffn_gather.advance(after=u)
            rest = ffn_gather.lands()
            for jj in range(1, n_ssd):
                w_in_p.append(pad_w_in(join_w_in(rest[2 * (jj - 1)])))
                w_out.append(join_w_out(rest[2 * (jj - 1) + 1]))
            w_pool, w_up, w_down = rest[2 * (n_ssd - 1):]
        hpre = _mm(u, w_up, "nn", BF16, "mm_up", 2048, 512, d, b_layer=i).reshape(bl, seq, f2)
        act, pre_g, pre_v = _ffn_act_fwd(hpre, f_conv_w[i], ffn_conv_b[i:i + 1], "ffn_act_fwd")
        act = act.reshape(t, ff)
        fo = _mm(act, w_down, "nn", F32, "mm_down", 2048, 512, ff, b_layer=i)
        if i + 1 == depth:
            cur = _norm_fwd(fo, norm_ffn_post[i:i + 1], F32, "norm_post", resid=mid)
        elif i % 2 == 0:
            cur, h = _norm_post_pre(fo, norm_ffn_post[i:i + 1], mid, norm_mix_pre[i + 1:i + 2], F32, "norm_post_pre_f")
        else:
            cur, h = _norm_post_pre(fo, norm_ffn_post[i:i + 1], mid, norm_mix_pre[i + 1:i + 2], BF16, "norm_post_pre_b")
        sv.update(mid=mid, u=u, hpre=hpre, pre_g=pre_g, pre_v=pre_v, act=act, fo=fo)
        saved.append(sv)

    dcur, loss_part = _loss_head(cur, tgt2, "loss_head")

    g = {n: [None] * wts[n].shape[0] for n in WEIGHTS}
    gbuf = dict(up=lax.empty((depth, d, f2), F32), down=lax.empty((depth, ff, d), F32),
                out=lax.empty((n_ssd, d_inner, d), F32), win=lax.empty((n_ssd, d, zw), F32))
    core = cy.reshape(1).astype(jnp.int32)

    def mixer_bwd(i, dmid, behind=()):
        j = i // 2
        sv = saved[i]
        done = []
        if i % 2 == 0:
            dmix, g["norm_mix_post"][i] = _norm_bwd(sv["mix"], norm_mix_post[i:i + 1], dmid, BF16, "norm_bwd_b",
                                                    after=behind)
            dyn = _mm(dmix, w_out[j], "nt", BF16, "mm_ssd_out_dx", 1024, 1024, d)
            gbuf["out"], tok = _mm(sv["yn"].reshape(t, d_inner), dmix, "tn", F32, "mm_ssd_out_dw", 1024, 1024, 2048,
                                   out_buf=(gbuf["out"], j))
            done.append(tok)
            dz, dxs, dbm, dcm, ddt, dnw, dd, dal, dbias = _ssd_bwd(
                sv["xc"], sv["zx"], sv["dtr"], sv["y"], dyn.reshape(bl, seq, d_inner), sv["st"], sv["dtb"], sv["alog"],
                sv["dskip"], sv["nw"], d_inner, "ssd_bwd")
            g["ssd_norm_w"][j] = dnw[:, 0, :].reshape(d_inner)
            g["ssd_d"][j] = dd[:, 0, :hpg].reshape(nheads)
            g["ssd_a_log"][j] = dal[:, 0, :hpg].reshape(nheads)
            g["ssd_dt_bias"][j] = dbias[:, 0, :hpg].reshape(nheads)
            dzx, dcw, dcb = _ssd_conv_bwd(sv["zx"], sv["xpre"], (dxs, dbm, dcm), ddt, dz, conv_w[j], d_inner,
                                          "ssd_conv_bwd")
            g["ssd_conv_w"][j] = dcw
            g["ssd_conv_b"][j] = dcb[0]
            dzx = dzx.reshape(t, zw)
            dh = _mm(dzx, w_in_p[j], "nt", BF16, "mm_ssd_in_dx", 1024, d, zw // 2)
            gbuf["win"], tok = _mm(sv["h"], dzx, "tn", F32, "mm_ssd_in_dw", 1024, zw // 4, 2048, out_buf=(gbuf["win"], j))
            done.append(tok)
        else:
            dmix, g["norm_mix_post"][i] = _norm_bwd(sv["mix"], norm_mix_post[i:i + 1], dmid, F32, "norm_bwd_f",
                                                    after=behind)
            dh3, g["pool_w"][j], dps = _pool_bwd(sv["h"].reshape(bl, seq, d), dmix.reshape(bl, seq, d), w_pool[j],
                                                 p_scale[j:j + 1], "pool_bwd")
            g["pool_scale"][j] = dps[0]
            dh = dh3.reshape(t, d)
        dx_in, g["norm_mix_pre"][i] = _norm_bwd(sv["x_in"], norm_mix_pre[i:i + 1], dh, F32, "norm_bwd_r", resid=dmid,
                                                after=done)
        return dx_in

    ffn_comm = None
    for i in reversed(range(depth)):
        sv = saved[i]
        dfo, g["norm_ffn_post"][i] = _norm_bwd(sv["fo"], norm_ffn_post[i:i + 1], dcur, BF16, "norm_bwd_b")
        dact = _mm(dfo, w_down, "nt", BF16, "mm_down_dx", 1024, ff // 2, d, b_layer=i)
        gbuf["down"], tok_down = _mm(sv["act"], dfo, "tn", F32, "mm_down_dw", ff // 2, 1024, 2048,
                                     out_buf=(gbuf["down"], i))
        dhg, dhv, dcw, dcb = _ffn_act_bwd(sv["hpre"], sv["pre_g"], sv["pre_v"], dact.reshape(bl, seq, ff), f_conv_w[i],
                                          "ffn_act_bwd")
        g["ffn_conv_w"][i] = dcw
        g["ffn_conv_b"][i] = dcb[0]
        dhs = [dhg.reshape(t, ff), dhv.reshape(t, ff)]
        du = _mm(dhs, w_up, "nt", BF16, "mm_up_dx", 1024, d, ff, b_layer=i)
        gbuf["up"], tok_up = _mm(sv["u"], dhs, "tn", F32, "mm_up_dw", 1024, ff // 2, 2048, out_buf=(gbuf["up"], i))
        dmid, g["norm_ffn_pre"][i] = _norm_bwd(sv["mid"], norm_ffn_pre[i:i + 1], du, F32, "norm_bwd_r", resid=dcur,
                                               after=[tok_down, tok_up])
        if i > 0:
            dcur = mixer_bwd(i, dmid)
        else:
            ffn_comm, ffn_token = _reduce_begin(FFNW, [gbuf["up"], gbuf["down"]], core, "ffn")
            dcur = mixer_bwd(0, dmid, behind=[ffn_token])

    grad_x = dcur.reshape(bl, seq, d)
    for n in ("norm_mix_pre", "norm_mix_post", "norm_ffn_pre", "norm_ffn_post"):
        g[n] = [a[0] for a in g[n]]
    small_names = [n for n, _ in SMALL] + list(REPL)
    full = {n: jnp.stack(g[n], axis=0) for n in small_names}

    g_in = jnp.concatenate([gbuf["win"][..., :d_inner + xbc], unpad_heads(gbuf["win"][..., d_inner + xbc:])], axis=-1)
    g_in_cm = jnp.swapaxes(g_in.reshape(n_ssd, d, N_CHIPS, -1), 1, 2)
    vec = jnp.concatenate([full[n].reshape(-1) for n in small_names] + [loss_part[0, :1]])
    nvec = vec.shape[0]
    vrows = 16 * ((nvec + 16 * FLAT_COLS - 1) // (16 * FLAT_COLS))
    vec = jnp.pad(vec, (0, vrows * FLAT_COLS - nvec)).reshape(vrows, FLAT_COLS)
    stages, counts = _gather8_stages()
    small_comm = _SplitComm(stages, counts, [vec], [lax.empty((8, vrows, FLAT_COLS), F32)], "gather_small_grads")
    small_token = small_comm.advance()
    mix_comm, mix_token = _reduce_begin(MIXW, [g_in_cm, gbuf["out"], jnp.stack(g["pool_w"], axis=0)], core, "mixers",
                                        riders=[small_token])

    grads, deltas, new_m, new_v = {}, {}, {}, {}

    def adamw(n, gr):
        shp = wts[n].shape
        two = (math.prod(shp[:-1]), shp[-1])
        dl, mn, vn = _adamw(wts[n].reshape(two), gr.reshape(two), mom[n].reshape(two), var[n].reshape(two),
                            "adamw_" + n)
        grads[n], deltas[n], new_m[n], new_v[n] = gr, dl.reshape(shp), mn.reshape(shp), vn.reshape(shp)
        return dl

    small_comm.advance(after=mix_token)
    tot = _sum8(small_comm.lands()[0], "sum_small").reshape(-1)
    small_grads, off = {}, 0
    for n in small_names:
        cnt = math.prod(full[n].shape)
        small_grads[n] = tot[off:off + cnt].reshape(full[n].shape)
        off += cnt
    loss = tot[off]
    for n, ax in SMALL:
        w = wts[n].shape[ax]
        small_grads[n] = lax.dynamic_slice_in_dim(small_grads[n], chip * w, w, axis=ax)

    behind = [adamw(n, small_grads[n]) for n in small_names][-1:]
    ffn_grads = _reduce_finish(FFNW, ffn_comm, core, "ffn", after=mix_token)
    behind += [adamw(n, gr) for gr, (n, _) in zip(ffn_grads, FFNW)]
    mix_grads = _reduce_finish(MIXW, mix_comm, core, "mixers", after=behind)
    for gr, (n, _) in zip(mix_grads, MIXW):
        adamw(n, gr)

    return (loss, grad_x, *[grads[n] for n in WEIGHTS], *[deltas[n] for n in WEIGHTS],
            *[new_m[n] for n in WEIGHTS], *[new_v[n] for n in WEIGHTS])
```

```python
import functools
import math

import jax
import jax.numpy as jnp
from jax import lax
from jax.experimental import pallas as pl
from jax.experimental.pallas import tpu as pltpu

F32 = jnp.float32
BF16 = jnp.bfloat16
MESH = pl.DeviceIdType.MESH
ANY = pl.BlockSpec(memory_space=pl.ANY)

HEAD_DIM = 64
D_STATE = 128
CHUNK = 128
N_GROUPS = 4
SSD_CONV = 4
FFN_CONV = 3
EPS = 1e-6
N_CHIPS = 4
LANES = 128
FLAT_COLS = 1024

ADAM_LR = 0.001
ADAM_B1 = 0.9
ADAM_B2 = 0.999
ADAM_EPS = 1e-08
ADAM_WD = 0.01
ADAM_STEP = 10

VMEM_LIMIT_BYTES = 56 * 1024 * 1024


def _params(sem=None):
    kw = dict(vmem_limit_bytes=VMEM_LIMIT_BYTES)
    if sem is not None:
        kw["dimension_semantics"] = sem
    return pltpu.CompilerParams(**kw)


def _sigmoid(x):
    return 0.5 * jnp.tanh(0.5 * x) + 0.5


def _softplus(x):
    return jnp.maximum(x, 0.0) + jnp.log(1.0 + jnp.exp(-jnp.abs(x)))


def _dot(a, b, dn):
    return lax.dot_general(a, b, (dn, ((), ())), preferred_element_type=F32)


def _nn(a, b):
    return _dot(a, b, ((1,), (0,)))


def _nt(a, b):
    return _dot(a, b, ((1,), (1,)))


def _tn(a, b):
    return _dot(a, b, ((0,), (0,)))


def _split(x, parts):
    out = []
    r = x
    for _ in range(parts):
        p = r.astype(BF16)
        out.append(p)
        r = r - p.astype(F32)
    return out


def _sel_left(sel, x, parts=3):
    n = x.shape[1]
    r = _nn(sel, jnp.concatenate(_split(x, parts), axis=1))
    out = r[:, 0:n]
    for i in range(1, parts):
        out = out + r[:, i * n:(i + 1) * n]
    return out


def _sel_right(x, sel_stacked, parts=3):
    return _nn(jnp.concatenate(_split(x, parts), axis=1), sel_stacked)


def _mm(a, b, dims, out_dtype, name, tm, tn, tk, b_layer=None, out_buf=None):
    a_list = list(a) if isinstance(a, (list, tuple)) else [a]
    b_list = list(b) if isinstance(b, (list, tuple)) else [b]
    if dims in ("nn", "nt"):
        assert len(b_list) == 1
        m = a_list[0].shape[0]
        segs = [x.shape[1] for x in a_list]
        k = sum(segs)
        bshape = b_list[0].shape[-2:]
        n = bshape[1] if dims == "nn" else bshape[0]
        assert (bshape[0] if dims == "nn" else bshape[1]) == k
    else:
        assert len(a_list) == 1 and b_layer is None
        k, m = a_list[0].shape
        segs = [x.shape[1] for x in b_list]
        n = sum(segs)
    nseg = len(segs)
    tm, tn = min(tm, m), min(tn, n)
    if dims == "tn":
        tk = min(tk, k)
        tn = min(tn, min(segs))
        units = [tn] * nseg
        nk = k // tk
        assert k % tk == 0
    else:
        units = [min(u, s) for u, s in zip(tk if isinstance(tk, (list, tuple)) else [tk] * nseg, segs)]
        nk = sum(s // u for s, u in zip(segs, units))
    assert m % tm == 0 and n % tn == 0 and all(s % u == 0 for s, u in zip(segs, units)), (name, m, n, k, segs, units)
    counts = [s // u for s, u in zip(segs, units)]
    starts = [sum(counts[:s]) for s in range(nseg)]
    assert all(sum(segs[:s]) % units[s] == 0 for s in range(nseg)), (name, segs, units)
    first_block = [sum(segs[:s]) // units[s] for s in range(nseg)]
    dn = {"nn": ((1,), (0,)), "nt": ((1,), (1,)), "tn": ((0,), (0,))}[dims]

    same = len(set(units)) == 1
    nb_ops = len(b_list) if dims == "tn" else (1 if same else nseg)

    def body(*refs):
        a_refs = refs[:len(a_list)]
        b_refs = refs[len(a_list):len(a_list) + nb_ops]
        rest = refs[len(a_list) + nb_ops + (0 if out_buf is None else 1):]
        o_ref = rest[0]
        if out_buf is not None:
            rest[1][...] = jnp.zeros((8, LANES), F32)
            rest = rest[1:]
        acc = rest[1] if nk > 1 else None
        kk = pl.program_id(2)
        sel = kk if dims != "tn" else pl.program_id(1)

        def step(a_ref, b_ref):
            p = _dot(a_ref[...].astype(BF16), b_ref[...].astype(BF16), dn)
            if nk == 1:
                o_ref[...] = p.astype(out_dtype)
                return

            @pl.when(kk == 0)
            def _():
                acc[...] = p

            @pl.when(kk > 0)
            def _():
                acc[...] += p

        if nseg == 1:
            step(a_refs[0], b_refs[0])
        else:
            for s in range(nseg):
                @pl.when(jnp.logical_and(sel >= starts[s], sel < starts[s] + counts[s]))
                def _(s=s):
                    step(a_refs[s] if dims != "tn" else a_refs[0], b_refs[s if nb_ops > 1 else 0])

        if nk > 1:
            @pl.when(kk == nk - 1)
            def _():
                o_ref[...] = acc[...].astype(out_dtype)

    def seg_index(v, s):
        return v if nseg == 1 else jnp.clip(v - starts[s], 0, counts[s] - 1)

    lead = () if b_layer is None else (b_layer,)
    none = () if b_layer is None else (None,)
    def b_block(kk, s):
        return kk if same else first_block[s] + seg_index(kk, s)

    if dims == "nn":
        a_specs = [pl.BlockSpec((tm, units[s]), lambda i, j, kk, s=s: (i, seg_index(kk, s))) for s in range(nseg)]
        b_specs = [pl.BlockSpec(none + (units[s], tn), lambda i, j, kk, s=s: lead + (b_block(kk, s), j))
                   for s in range(nb_ops)]
    elif dims == "nt":
        a_specs = [pl.BlockSpec((tm, units[s]), lambda i, j, kk, s=s: (i, seg_index(kk, s))) for s in range(nseg)]
        b_specs = [pl.BlockSpec(none + (tn, units[s]), lambda i, j, kk, s=s: lead + (j, b_block(kk, s)))
                   for s in range(nb_ops)]
    else:
        a_specs = [pl.BlockSpec((tk, tm), lambda i, j, kk: (kk, i))]
        b_specs = [pl.BlockSpec((tk, tn), lambda i, j, kk, s=s: (kk, seg_index(j, s))) for s in range(nseg)]
    args = a_list + (b_list * nb_ops if dims != "tn" else b_list)
    in_specs = a_specs + b_specs
    aliases = {}
    if out_buf is None:
        out_shape = jax.ShapeDtypeStruct((m, n), out_dtype)
        out_spec = pl.BlockSpec((tm, tn), lambda i, j, kk: (i, j))
    else:
        buf, slab = out_buf
        assert buf.shape[1:] == (m, n) and buf.dtype == out_dtype
        out_shape = (jax.ShapeDtypeStruct(buf.shape, out_dtype), jax.ShapeDtypeStruct((8, LANES), F32))
        out_spec = (pl.BlockSpec((None, tm, tn), lambda i, j, kk: (slab, i, j)),
                    pl.BlockSpec((8, LANES), lambda i, j, kk: (0, 0)))
        aliases = {len(args): 0}
        args = args + [buf]
        in_specs = in_specs + [ANY]
    return pl.pallas_call(
        body,
        out_shape=out_shape,
        grid=(m // tm, n // tn, nk),
        in_specs=in_specs,
        out_specs=out_spec,
        scratch_shapes=[] if nk == 1 else [pltpu.VMEM((tm, tn), F32)],
        input_output_aliases=aliases,
        compiler_params=_params(("parallel", "parallel", "arbitrary") if out_buf is None else ("arbitrary",) * 3),
        name=name,
    )(*args)


def _row_tile(t, want):
    tm = min(want, t)
    assert t % tm == 0
    return tm


def _norm_fwd(x, w, out_dtype, name, resid=None, after=()):
    t, d = x.shape
    tm = _row_tile(t, 512)
    after = [a for a in after if a is not None]

    def body(*refs):
        refs = refs[:len(refs) - 1 - len(after)] + refs[len(refs) - 1:]
        if resid is None:
            x_ref, w_ref, o_ref = refs
        else:
            x_ref, w_ref, r_ref, o_ref = refs
        xv = x_ref[...]
        r = lax.rsqrt(jnp.mean(xv * xv, axis=-1, keepdims=True) + EPS)
        y = (xv * r) * w_ref[...]
        if resid is not None:
            y = r_ref[...] + y
        o_ref[...] = y.astype(out_dtype)

    row = pl.BlockSpec((tm, d), lambda i: (i, 0))
    vec = pl.BlockSpec((1, d), lambda i: (0, 0))
    args = [x, w] + ([] if resid is None else [resid]) + after
    return pl.pallas_call(
        body, out_shape=jax.ShapeDtypeStruct((t, d), out_dtype), grid=(t // tm,),
        in_specs=[row, vec] + ([] if resid is None else [row]) + [ANY] * len(after), out_specs=row,
        compiler_params=_params(("parallel",)), name=name)(*args)


def _norm_post_pre(m, w_post, resid, w_pre, pre_dtype, name, after=()):
    t, d = m.shape
    tm = _row_tile(t, 512)
    after = [a for a in after if a is not None]

    def body(m_ref, w1_ref, r_ref, w2_ref, *rest):
        x_ref, u_ref = rest[len(after):]
        mv = m_ref[...]
        r1 = lax.rsqrt(jnp.mean(mv * mv, axis=-1, keepdims=True) + EPS)
        xv = r_ref[...] + (mv * r1) * w1_ref[...]
        x_ref[...] = xv
        r2 = lax.rsqrt(jnp.mean(xv * xv, axis=-1, keepdims=True) + EPS)
        u_ref[...] = ((xv * r2) * w2_ref[...]).astype(pre_dtype)

    row = pl.BlockSpec((tm, d), lambda i: (i, 0))
    vec = pl.BlockSpec((1, d), lambda i: (0, 0))
    return pl.pallas_call(
        body, out_shape=(jax.ShapeDtypeStruct((t, d), F32), jax.ShapeDtypeStruct((t, d), pre_dtype)), grid=(t // tm,),
        in_specs=[row, vec, row, vec] + [ANY] * len(after), out_specs=(row, row),
        compiler_params=_params(("parallel",)), name=name)(m, w_post, resid, w_pre, *after)


def _norm_bwd(src, w, dy, out_dtype, name, resid=None, after=()):
    t, d = src.shape
    tm = _row_tile(t, 512)
    after = [a for a in after if a is not None]

    def body(*refs):
        refs = refs[:len(refs) - 2 - len(after)] + refs[len(refs) - 2:]
        if resid is None:
            x_ref, w_ref, g_ref, o_ref, dw_ref = refs
        else:
            x_ref, w_ref, g_ref, r_ref, o_ref, dw_ref = refs
        xv = x_ref[...]
        g = g_ref[...].astype(F32)
        r = lax.rsqrt(jnp.mean(xv * xv, axis=-1, keepdims=True) + EPS)
        xh = xv * r
        gh = g * w_ref[...]
        mean = jnp.mean(gh * xh, axis=-1, keepdims=True)
        dx = r * (gh - xh * mean)
        if resid is not None:
            dx = r_ref[...] + dx
        o_ref[...] = dx.astype(out_dtype)
        part = jnp.sum(g * xh, axis=0, keepdims=True)

        @pl.when(pl.program_id(0) == 0)
        def _():
            dw_ref[...] = part

        @pl.when(pl.program_id(0) > 0)
        def _():
            dw_ref[...] += part

    row = pl.BlockSpec((tm, d), lambda i: (i, 0))
    vec = pl.BlockSpec((1, d), lambda i: (0, 0))
    args = [src, w, dy] + ([] if resid is None else [resid]) + after
    return pl.pallas_call(
        body,
        out_shape=(jax.ShapeDtypeStruct((t, d), out_dtype), jax.ShapeDtypeStruct((1, d), F32)),
        grid=(t // tm,),
        in_specs=[row, vec, row] + ([] if resid is None else [row]) + [ANY] * len(after),
        out_specs=(row, vec),
        compiler_params=_params(("arbitrary",)), name=name)(*args)


def _norm_bwd2(src1, w1, dy1, resid, src2, w2, out2_dtype, name, after=()):
    t, d = src1.shape
    tm = _row_tile(t, 512)
    after = [a for a in after if a is not None]

    def back(xv, w, g):
        r = lax.rsqrt(jnp.mean(xv * xv, axis=-1, keepdims=True) + EPS)
        xh = xv * r
        gh = g * w
        return r * (gh - xh * jnp.mean(gh * xh, axis=-1, keepdims=True)), jnp.sum(g * xh, axis=0, keepdims=True)

    def body(x1_ref, w1_ref, g1_ref, r_ref, x2_ref, w2_ref, *rest):
        d1_ref, d2_ref, dw1_ref, dw2_ref = rest[len(after):]
        d1, p1 = back(x1_ref[...], w1_ref[...], g1_ref[...].astype(F32))
        d1 = r_ref[...] + d1
        d1_ref[...] = d1
        d2, p2 = back(x2_ref[...], w2_ref[...], d1)
        d2_ref[...] = d2.astype(out2_dtype)

        @pl.when(pl.program_id(0) == 0)
        def _():
            dw1_ref[...] = p1
            dw2_ref[...] = p2

        @pl.when(pl.program_id(0) > 0)
        def _():
            dw1_ref[...] += p1
            dw2_ref[...] += p2

    row = pl.BlockSpec((tm, d), lambda i: (i, 0))
    vec = pl.BlockSpec((1, d), lambda i: (0, 0))
    return pl.pallas_call(
        body,
        out_shape=(jax.ShapeDtypeStruct((t, d), F32), jax.ShapeDtypeStruct((t, d), out2_dtype),
                   jax.ShapeDtypeStruct((1, d), F32), jax.ShapeDtypeStruct((1, d), F32)),
        grid=(t // tm,),
        in_specs=[row, vec, row, row, row, vec] + [ANY] * len(after),
        out_specs=(row, row, vec, vec),
        compiler_params=_params(("arbitrary",)), name=name)(src1, w1, dy1, resid, src2, w2, *after)


def _loss_head(y, target, name):
    t, d = y.shape
    tm = _row_tile(t, 512)

    def body(y_ref, t_ref, dy_ref, l_ref):
        e = y_ref[...] - t_ref[...]
        dy_ref[...] = e * (1.0 / d)
        col = jnp.sum(e * e, axis=0, keepdims=True)
        s = jnp.sum(col, axis=1, keepdims=True) * (0.5 / d)
        part = jnp.broadcast_to(s, (1, LANES))

        @pl.when(pl.program_id(0) == 0)
        def _():
            l_ref[...] = part

        @pl.when(pl.program_id(0) > 0)
        def _():
            l_ref[...] += part

    row = pl.BlockSpec((tm, d), lambda i: (i, 0))
    return pl.pallas_call(
        body,
        out_shape=(jax.ShapeDtypeStruct((t, d), F32), jax.ShapeDtypeStruct((1, LANES), F32)),
        grid=(t // tm,), in_specs=[row, row],
        out_specs=(row, pl.BlockSpec((1, LANES), lambda i: (0, 0))),
        compiler_params=_params(("arbitrary",)), name=name)(y, target)


def _window(ref, c, rows, seq, before, after, keep=None):
    r0 = pl.multiple_of(c * rows, rows)
    parts = []
    if before:
        h0 = pl.multiple_of(jnp.maximum(r0 - before, 0), before)
        halo = ref[pl.ds(h0, before), :].astype(F32)
        halo = halo if keep is None else halo[before - keep:, :]
        parts.append(jnp.where(c > 0, halo, 0.0))
    parts.append(ref[pl.ds(r0, rows), :].astype(F32))
    if after:
        h1 = pl.multiple_of(jnp.minimum(r0 + rows, seq - after), after)
        halo = ref[pl.ds(h1, after), :].astype(F32)
        halo = halo if keep is None else halo[:keep, :]
        parts.append(jnp.where(c < seq // rows - 1, halo, 0.0))
    return parts[0] if len(parts) == 1 else jnp.concatenate(parts, axis=0)


def _lag(x, k):
    return pltpu.roll(x, k, 0) if k else x


def _lead(x, k):
    return pltpu.roll(x, x.shape[0] - k, 0) if k else x


SHIFT_ROWS = 128
SHIFT_COLS = 256


HALO = 16
KEEP = 8


def _conv3(ext, w, bias):
    acc = bias + w[2:3, :] * ext[KEEP:, :]
    acc = acc + w[1:2, :] * _lag(ext, 1)[KEEP:, :]
    return acc + w[0:1, :] * _lag(ext, 2)[KEEP:, :]


def _ffn_act_fwd(hpre, cw, cb, name):
    b, seq, f2 = hpre.shape
    cbk = SHIFT_COLS
    nj = f2 // (2 * cbk)
    rows = min(SHIFT_ROWS, seq)

    def body(g_ref, v_ref, wg_ref, wv_ref, bg_ref, bv_ref, o_ref, pg_ref, pv_ref):
        def chunk(c, carry):
            gate = _conv3(_window(g_ref, c, rows, seq, HALO, 0, KEEP), wg_ref[...], bg_ref[...])
            val = _conv3(_window(v_ref, c, rows, seq, HALO, 0, KEEP), wv_ref[...], bv_ref[...])
            a = gate * _sigmoid(gate) * val
            here = pl.ds(pl.multiple_of(c * rows, rows), rows)
            o_ref[here, :] = a.astype(BF16)
            pg_ref[here, :] = gate.astype(BF16)
            pv_ref[here, :] = val.astype(BF16)
            return carry

        lax.fori_loop(0, seq // rows, chunk, 0)

    blk = lambda off: pl.BlockSpec((None, seq, cbk), lambda i, j: (i, 0, j + off))
    wsp = lambda r, off: pl.BlockSpec((r, cbk), lambda i, j: (0, j + off))
    half = jax.ShapeDtypeStruct((b, seq, f2 // 2), BF16)
    return pl.pallas_call(
        body, out_shape=(half, half, half), grid=(b, nj),
        in_specs=[blk(0), blk(nj), wsp(FFN_CONV, 0), wsp(FFN_CONV, nj), wsp(1, 0), wsp(1, nj)],
        out_specs=(blk(0), blk(0), blk(0)),
        compiler_params=_params(("parallel", "parallel")), name=name)(hpre, hpre, cw, cw, cb, cb)


def _ffn_act_bwd(hpre, pre_g, pre_v, da, cw, name):
    b, seq, f2 = hpre.shape
    cbk = SHIFT_COLS
    nj = f2 // (2 * cbk)
    rows = min(SHIFT_ROWS, seq)

    def body(g_ref, v_ref, pg_ref, pv_ref, da_ref, wg_ref, wv_ref, og_ref, ov_ref, dwg_ref, dwv_ref, dbg_ref, dbv_ref):
        wg, wv = wg_ref[...], wv_ref[...]

        def back(dpre, w, o_ref, x_ref, c, carry):
            here = pl.ds(pl.multiple_of(c * rows, rows), rows)
            leads = [dpre, _lead(dpre, 1), _lead(dpre, 2)]
            dx = w[2:3, :] * leads[0] + w[1:2, :] * leads[1] + w[0:1, :] * leads[2]
            o_ref[here, :] = dx[:rows, :].astype(BF16)
            x0 = x_ref[here, :].astype(F32)
            return tuple(carry[k] + jnp.sum(leads[k][:rows, :] * x0, axis=0, keepdims=True) for k in range(FFN_CONV)) + (
                carry[FFN_CONV] + jnp.sum(dpre[:rows, :], axis=0, keepdims=True),)

        def chunk(c, carry):
            cg, cv = carry
            gate = _window(pg_ref, c, rows, seq, 0, HALO, KEEP)
            val = _window(pv_ref, c, rows, seq, 0, HALO, KEEP)
            dav = _window(da_ref, c, rows, seq, 0, HALO, KEEP)
            sg = _sigmoid(gate)
            cg = back(dav * val * (sg * (1.0 + gate * (1.0 - sg))), wg, og_ref, g_ref, c, cg)
            cv = back(dav * (gate * sg), wv, ov_ref, v_ref, c, cv)
            return cg, cv

        z = jnp.zeros((1, cbk), F32)
        cg, cv = lax.fori_loop(0, seq // rows, chunk, ((z,) * (FFN_CONV + 1), (z,) * (FFN_CONV + 1)))
        dwg = jnp.concatenate([cg[2], cg[1], cg[0]], axis=0)
        dwv = jnp.concatenate([cv[2], cv[1], cv[0]], axis=0)

        @pl.when(pl.program_id(1) == 0)
        def _():
            dwg_ref[...] = dwg
            dwv_ref[...] = dwv
            dbg_ref[...] = cg[FFN_CONV]
            dbv_ref[...] = cv[FFN_CONV]

        @pl.when(pl.program_id(1) > 0)
        def _():
            dwg_ref[...] += dwg
            dwv_ref[...] += dwv
            dbg_ref[...] += cg[FFN_CONV]
            dbv_ref[...] += cv[FFN_CONV]

    blk = lambda off: pl.BlockSpec((None, seq, cbk), lambda j, i: (i, 0, j + off))
    wsp = lambda r, off: pl.BlockSpec((r, cbk), lambda j, i: (0, j + off))
    half = jax.ShapeDtypeStruct((b, seq, f2 // 2), BF16)
    dwshape = jax.ShapeDtypeStruct((FFN_CONV, f2 // 2), F32)
    dbshape = jax.ShapeDtypeStruct((1, f2 // 2), F32)
    dg, dv, dwg, dwv, dbg, dbv = pl.pallas_call(
        body,
        out_shape=(half, half, dwshape, dwshape, dbshape, dbshape),
        grid=(nj, b),
        in_specs=[blk(0), blk(nj), blk(0), blk(0), blk(0), wsp(FFN_CONV, 0), wsp(FFN_CONV, nj)],
        out_specs=(blk(0), blk(0), wsp(FFN_CONV, 0), wsp(FFN_CONV, 0), wsp(1, 0), wsp(1, 0)),
        compiler_params=_params(("parallel", "arbitrary")), name=name)(hpre, hpre, pre_g, pre_v, da, cw, cw)
    return dg, dv, jnp.concatenate([dwg, dwv], axis=1), jnp.concatenate([dbg, dbv], axis=1)


def _ssd_conv_fwd(zx, cw, cb, d_inner, name):
    b, seq, _ = zx.shape
    xbc = cw.shape[1]
    cbk = SHIFT_COLS
    off = d_inner // cbk
    rows = min(SHIFT_ROWS, seq)

    def body(h_ref, w_ref, b_ref, o_ref, p_ref):
        w = w_ref[...]
        bias = b_ref[...]

        def chunk(c, carry):
            ext = _window(h_ref, c, rows, seq, HALO, 0, KEEP)
            acc = bias + w[3:4, :] * ext[KEEP:, :]
            for k in range(1, SSD_CONV):
                acc = acc + w[3 - k:4 - k, :] * _lag(ext, k)[KEEP:, :]
            here = pl.ds(pl.multiple_of(c * rows, rows), rows)
            o_ref[here, :] = acc * _sigmoid(acc)
            p_ref[here, :] = acc.astype(BF16)
            return carry

        lax.fori_loop(0, seq // rows, chunk, 0)

    blk = pl.BlockSpec((None, seq, cbk), lambda i, j: (i, 0, j))
    return pl.pallas_call(
        body, out_shape=(jax.ShapeDtypeStruct((b, seq, xbc), F32), jax.ShapeDtypeStruct((b, seq, xbc), BF16)),
        grid=(b, xbc // cbk),
        in_specs=[pl.BlockSpec((None, seq, cbk), lambda i, j: (i, 0, j + off)),
                  pl.BlockSpec((SSD_CONV, cbk), lambda i, j: (0, j)),
                  pl.BlockSpec((1, cbk), lambda i, j: (0, j))],
        out_specs=(blk, blk),
        compiler_params=_params(("parallel", "parallel")), name=name)(zx, cw, cb)


def _ssd_conv_bwd(zx, pre, dparts, ddt, dzx, cw, d_inner, name):
    b, seq, zw = zx.shape
    xbc = cw.shape[1]
    cbk = SHIFT_COLS
    off = d_inner // cbk
    rows = min(SHIFT_ROWS, seq)
    nblk = [p.shape[2] // cbk for p in dparts]
    first = [sum(nblk[:s]) for s in range(len(dparts))]
    nconv = xbc // cbk
    ncopy = ddt.shape[2] // cbk
    assert sum(nblk) == nconv and (off + nconv + ncopy) * cbk == zw and dzx.shape == (b, seq, zw)

    def body(h_ref, p_ref, gx_ref, gb_ref, gc_ref, t_ref, w_ref, z_ref, o_ref, dw_ref, db_ref):
        j = pl.program_id(0)

        @pl.when(j < nconv)
        def _():
            conv(h_ref, p_ref, gx_ref, gb_ref, gc_ref, w_ref, o_ref, dw_ref, db_ref)

        @pl.when(j >= nconv)
        def _():
            o_ref[...] = t_ref[...]

    def conv(h_ref, p_ref, gx_ref, gb_ref, gc_ref, w_ref, o_ref, dw_ref, db_ref):
        w = w_ref[...]
        j = pl.program_id(0)

        def chunk(c, carry):
            dws, dbias = carry
            here = pl.ds(pl.multiple_of(c * rows, rows), rows)
            pre = _window(p_ref, c, rows, seq, 0, HALO, KEEP)
            s = _sigmoid(pre)
            gsel = jnp.where(j < first[1], _window(gx_ref, c, rows, seq, 0, HALO, KEEP),
                             jnp.where(j < first[2], _window(gb_ref, c, rows, seq, 0, HALO, KEEP),
                                       _window(gc_ref, c, rows, seq, 0, HALO, KEEP)))
            dpre = gsel * (s * (1.0 + pre * (1.0 - s)))
            leads = [dpre] + [_lead(dpre, k) for k in range(1, SSD_CONV)]
            dx = w[3:4, :] * leads[0]
            for k in range(1, SSD_CONV):
                dx = dx + w[3 - k:4 - k, :] * leads[k]
            o_ref[here, :] = dx[:rows, :].astype(BF16)
            x0 = h_ref[here, :].astype(F32)
            dws = tuple(dws[k] + jnp.sum(leads[k][:rows, :] * x0, axis=0, keepdims=True) for k in range(SSD_CONV))
            dbias = dbias + jnp.sum(dpre[:rows, :], axis=0, keepdims=True)
            return dws, dbias

        z = jnp.zeros((1, cbk), F32)
        dws, dbias = lax.fori_loop(0, seq // rows, chunk, ((z,) * SSD_CONV, z))
        dwv = jnp.concatenate([dws[3 - i] for i in range(SSD_CONV)], axis=0)

        @pl.when(pl.program_id(1) == 0)
        def _():
            dw_ref[...] = dwv
            db_ref[...] = dbias

        @pl.when(pl.program_id(1) > 0)
        def _():
            dw_ref[...] += dwv
            db_ref[...] += dbias

    conv_j = lambda j: jnp.minimum(j, nconv - 1)
    return pl.pallas_call(
        body,
        out_shape=(jax.ShapeDtypeStruct((b, seq, zw), BF16), jax.ShapeDtypeStruct((SSD_CONV, xbc), F32),
                   jax.ShapeDtypeStruct((1, xbc), F32)),
        grid=(nconv + ncopy, b),
        in_specs=[pl.BlockSpec((None, seq, cbk), lambda j, i: (i, 0, conv_j(j) + off)),
                  pl.BlockSpec((None, seq, cbk), lambda j, i: (i, 0, conv_j(j)))] + [
                  pl.BlockSpec((None, seq, cbk), lambda j, i, s=s: (i, 0, jnp.clip(j - first[s], 0, nblk[s] - 1)))
                  for s in range(3)] + [
                  pl.BlockSpec((None, seq, cbk), lambda j, i: (i, 0, jnp.clip(j - nconv, 0, ncopy - 1))),
                  pl.BlockSpec((SSD_CONV, cbk), lambda j, i: (0, conv_j(j))),
                  ANY],
        out_specs=(pl.BlockSpec((None, seq, cbk), lambda j, i: (i, 0, j + off)),
                   pl.BlockSpec((SSD_CONV, cbk), lambda j, i: (0, conv_j(j))),
                   pl.BlockSpec((1, cbk), lambda j, i: (0, conv_j(j)))),
        input_output_aliases={7: 0},
        compiler_params=_params(("arbitrary", "arbitrary")), name=name)(zx, pre, *dparts, ddt, cw, dzx)


def _pool_sums(q, g, lead):
    sh = _lead if lead else _lag
    s2 = q + sh(q, 1)
    s4 = s2 + sh(s2, 2)
    s8 = s4 + sh(s4, 4)
    s16 = s8 + sh(s8, 8)
    return jnp.where(g == 0, s2, jnp.where(g == 1, s4, jnp.where(g == 2, s8, s16)))


def _pool_count(r0, n, g, shape):
    t = (r0 + lax.broadcasted_iota(jnp.int32, shape, 0) + 1).astype(F32)
    return jnp.minimum(t, (2 << g).astype(F32))


def _pool_fwd(h, pw, scale, name):
    b, seq, d = h.shape
    dg = d // 4
    rows = min(SHIFT_ROWS, seq)

    def body(h_ref, w_ref, s_ref, o_ref):
        g = pl.program_id(1)
        wmat = w_ref[...]
        sc = s_ref[...]

        def chunk(c, carry):
            r0 = c * rows
            ext = _window(h_ref, c, rows, seq, 16, 0)
            sums = _pool_sums(ext, g, False)[16:, :]
            mixed = sums / _pool_count(r0, rows, g, (rows, dg)) - ext[16:, :]
            o_ref[pl.ds(pl.multiple_of(r0, rows), rows), :] = _nn(mixed.astype(BF16), wmat) * sc
            return carry

        lax.fori_loop(0, seq // rows, chunk, 0)

    return pl.pallas_call(
        body, out_shape=jax.ShapeDtypeStruct((b, seq, d), F32), grid=(b, 4),
        in_specs=[pl.BlockSpec((None, seq, dg), lambda i, g: (i, 0, g)),
                  pl.BlockSpec((None, dg, dg), lambda i, g: (g, 0, 0)),
                  pl.BlockSpec((1, dg), lambda i, g: (0, g))],
        out_specs=pl.BlockSpec((None, seq, dg), lambda i, g: (i, 0, g)),
        compiler_params=_params(("parallel", "parallel")), name=name)(h, pw, scale)


def _pool_bwd(h, dout, pw, scale, name):
    b, seq, d = h.shape
    dg = d // 4
    rows = min(SHIFT_ROWS, seq)

    def body(h_ref, g_ref, w_ref, s_ref, o_ref, dw_ref, ds_ref, dw_acc):
        g = pl.program_id(0)
        wmat = w_ref[...]
        sc = s_ref[...]
        dw_acc[...] = jnp.zeros_like(dw_acc)

        def chunk(c, dsc):
            r0 = c * rows
            ext = _window(h_ref, c, rows, seq, 16, 0)
            sums = _pool_sums(ext, g, False)[16:, :]
            mixed = (sums / _pool_count(r0, rows, g, (rows, dg)) - ext[16:, :]).astype(BF16)
            gext = _window(g_ref, c, rows, seq, 0, 16)
            dsc = dsc + jnp.sum(gext[:rows, :] * _nn(mixed, wmat), axis=0, keepdims=True)
            dpre = (gext * sc).astype(BF16)
            dw_acc[...] += _tn(mixed, dpre[:rows, :])
            dmix = _nt(dpre, wmat)
            q = dmix / _pool_count(r0, rows + 16, g, (rows + 16, dg))
            back = _pool_sums(q, g, True)
            o_ref[pl.ds(pl.multiple_of(r0, rows), rows), :] = back[:rows, :] - dmix[:rows, :]
            return dsc

        dsc = lax.fori_loop(0, seq // rows, chunk, jnp.zeros((1, dg), F32))

        @pl.when(pl.program_id(1) == 0)
        def _():
            dw_ref[...] = dw_acc[...]
            ds_ref[...] = dsc

        @pl.when(pl.program_id(1) > 0)
        def _():
            dw_ref[...] += dw_acc[...]
            ds_ref[...] += dsc

    return pl.pallas_call(
        body,
        out_shape=(jax.ShapeDtypeStruct((b, seq, d), F32), jax.ShapeDtypeStruct((4, dg, dg), F32),
                   jax.ShapeDtypeStruct((1, d), F32)),
        grid=(4, b),
        in_specs=[pl.BlockSpec((None, seq, dg), lambda g, i: (i, 0, g)),
                  pl.BlockSpec((None, seq, dg), lambda g, i: (i, 0, g)),
                  pl.BlockSpec((None, dg, dg), lambda g, i: (g, 0, 0)),
                  pl.BlockSpec((1, dg), lambda g, i: (0, g))],
        out_specs=(pl.BlockSpec((None, seq, dg), lambda g, i: (i, 0, g)),
                   pl.BlockSpec((None, dg, dg), lambda g, i: (g, 0, 0)),
                   pl.BlockSpec((1, dg), lambda g, i: (0, g))),
        scratch_shapes=[pltpu.VMEM((dg, dg), F32)],
        compiler_params=_params(("parallel", "arbitrary")), name=name)(h, dout, pw, scale)


def _head_of(channel):
    return jnp.right_shift(channel, HEAD_DIM.bit_length() - 1)


def _ssd_consts(gw):
    q = CHUNK
    row = lax.broadcasted_iota(jnp.int32, (q, q), 0)
    col = lax.broadcasted_iota(jnp.int32, (q, q), 1)
    tril = (row >= col).astype(BF16)
    triu = (row <= col).astype(BF16)
    e = (_head_of(lax.broadcasted_iota(jnp.int32, (LANES, gw), 1))
         == lax.broadcasted_iota(jnp.int32, (LANES, gw), 0)).astype(BF16)
    et = (_head_of(lax.broadcasted_iota(jnp.int32, (gw, LANES), 0))
          == lax.broadcasted_iota(jnp.int32, (gw, LANES), 1)).astype(BF16)
    return row, col, tril, triu, e, et


def _ssd_common(dtr, dtb, alog, gw):
    q = CHUNK
    row, col, tril, triu, e, et = _ssd_consts(gw)
    dt = _softplus(dtr + dtb)
    a_row = -jnp.exp(alog)
    acum = _sel_left(tril, dt * a_row)
    ac_last = jnp.sum(jnp.where(row == q - 1, acum, 0.0), axis=0, keepdims=True)
    eac = jnp.exp(acum)
    de = jnp.exp(ac_last - acum)
    e2 = jnp.concatenate([e, e], axis=0)
    expand = _sel_right(jnp.concatenate([dt, eac, de], axis=0), e2, 2)
    dt_x, eac_x, de_x = expand[0:q], expand[q:2 * q], expand[2 * q:3 * q]
    acum_t = acum.T
    cd_col = jnp.exp(acum_t[:, q - 1:q])
    et3 = jnp.concatenate([et, et, et], axis=1)
    cdmat = _nn(et3, jnp.concatenate(_split(jnp.broadcast_to(cd_col, (LANES, D_STATE)), 3), axis=0))
    consts = dict(row=row, col=col, tril=tril, triu=triu, e=e, et=et)
    return dt, a_row, acum, acum_t, ac_last, eac, de, dt_x, eac_x, de_x, cdmat, consts


def _decay(acum, acum_t, j, row, col):
    diff = acum[:, j:j + 1] - acum_t[j:j + 1, :]
    return jnp.exp(jnp.where(row >= col, diff, -1e30))


def _ssd_fwd(xc, zx, dtr, dtb, alog, dskip, nw, d_inner, name):
    b, seq, xbc = xc.shape
    q = CHUNK
    nc = seq // q
    gw = d_inner // N_GROUPS
    nh = gw // HEAD_DIM
    xb0 = d_inner // D_STATE
    xc0 = xb0 + N_GROUPS
    dt0 = (d_inner + xbc) // LANES

    nb = max(n for n in (4, 2, 1) if b % n == 0)

    def body(x_ref, b_ref, c_ref, z_ref, dtr_ref, dtb_ref, al_ref, dsk_ref, nw_ref, y_ref, yn_ref, st_ref, s_ref):
        @pl.when(pl.program_id(2) == 0)
        def _():
            s_ref[...] = jnp.zeros_like(s_ref)

        for s in range(nb):
            one(s, x_ref.at[s], b_ref.at[s], c_ref.at[s], z_ref.at[s], dtr_ref.at[s], dtb_ref, al_ref, dsk_ref, nw_ref,
                y_ref.at[s], yn_ref.at[s], st_ref.at[s], s_ref.at[s])

    def one(s, x_ref, b_ref, c_ref, z_ref, dtr_ref, dtb_ref, al_ref, dsk_ref, nw_ref, y_ref, yn_ref, st_ref, s_ref):
        prev = s_ref[...]
        st_ref[...] = prev
        x = x_ref[...]
        bm = b_ref[...].astype(BF16)
        cm = c_ref[...].astype(BF16)
        (dt, a_row, acum, acum_t, ac_last, eac, de, dt_x, eac_x, de_x, cdmat, k) = _ssd_common(
            dtr_ref[...], dtb_ref[0:1, :], al_ref[0:1, :], gw)
        xdt = x * dt_x
        xdt_b = xdt.astype(BF16)
        cb = _nt(cm, bm)
        half = _head_of(lax.broadcasted_iota(jnp.int32, (q, LANES), 1))
        pairs = []
        for j in range(nh):
            pc = (j // 2) * LANES
            m = (cb * _decay(acum, acum_t, j, k["row"], k["col"])).astype(BF16)
            yj = jnp.where(half == j % 2, _nn(m, xdt_b[:, pc:pc + LANES]), 0.0)
            if j % 2 == 0:
                pairs.append(yj)
            else:
                pairs[-1] = pairs[-1] + yj
        prev_b = prev.astype(BF16)
        y = dsk_ref[0:1, :] * x + jnp.concatenate(pairs, axis=1) + eac_x * _nt(cm, prev_b)
        s_ref[...] = cdmat * prev + _tn((xdt * de_x).astype(BF16), bm)
        y_ref[...] = y
        z = z_ref[...].astype(F32)
        yg = y * (z * _sigmoid(z))
        r = lax.rsqrt(jnp.mean(yg * yg, axis=-1, keepdims=True) + EPS)
        yn_ref[...] = ((yg * r) * nw_ref[0:1, :]).astype(BF16)

    par = lambda w: pl.BlockSpec((None, 8, w), lambda i, g, c: (g, 0, 0))
    return pl.pallas_call(
        body,
        out_shape=(jax.ShapeDtypeStruct((b, seq, d_inner), F32), jax.ShapeDtypeStruct((b, seq, d_inner), BF16),
                   jax.ShapeDtypeStruct((b, nc, N_GROUPS, gw, D_STATE), F32)),
        grid=(b // nb, N_GROUPS, nc),
        in_specs=[pl.BlockSpec((nb, q, gw), lambda i, g, c: (i, c, g)),
                  pl.BlockSpec((nb, q, D_STATE), lambda i, g, c: (i, c, xb0 + g)),
                  pl.BlockSpec((nb, q, D_STATE), lambda i, g, c: (i, c, xc0 + g)),
                  pl.BlockSpec((nb, q, gw), lambda i, g, c: (i, c, g)),
                  pl.BlockSpec((nb, q, LANES), lambda i, g, c: (i, c, g)),
                  par(LANES), par(LANES), par(gw), par(gw)],
        out_specs=(pl.BlockSpec((nb, q, gw), lambda i, g, c: (i, c, g)),
                   pl.BlockSpec((nb, q, gw), lambda i, g, c: (i, c, g)),
                   pl.BlockSpec((nb, None, None, gw, D_STATE), lambda i, g, c: (i, c, g, 0, 0))),
        scratch_shapes=[pltpu.VMEM((nb, gw, D_STATE), F32)],
        compiler_params=_params(("parallel", "parallel", "arbitrary")), name=name,
    )(xc, xc, xc, zx, dtr, dtb, alog, dskip, nw)


def _ssd_bwd(xc, zx, dtr, y, dyn, st, dtb, alog, dskip, nw, d_inner, name):
    b, seq, xbc = xc.shape
    q = CHUNK
    nc = seq // q
    gw = d_inner // N_GROUPS
    nh = gw // HEAD_DIM
    xb0 = d_inner // D_STATE
    xc0 = xb0 + N_GROUPS
    dt0 = (d_inner + xbc) // LANES

    nb = max(n for n in (4, 2, 1) if b % n == 0)

    def body(x_ref, b_ref, c_ref, z_ref, dtr_ref, y_ref, g_ref, st_ref, dtb_ref, al_ref, dsk_ref, nw_ref,
             dz_ref, dx_ref, db_ref, dc_ref, ddt_ref, dnw_ref, dd_ref, dal_ref, dbias_ref,
             ds_ref, colbuf, rowbuf):
        first = jnp.logical_and(pl.program_id(1) == 0, pl.program_id(2) == 0)

        @pl.when(pl.program_id(2) == 0)
        def _():
            ds_ref[...] = jnp.zeros_like(ds_ref)

        sums = [one(x_ref.at[s], b_ref.at[s], c_ref.at[s], z_ref.at[s], dtr_ref.at[s], y_ref.at[s], g_ref.at[s],
                    st_ref.at[s], dtb_ref, al_ref, dsk_ref, nw_ref, dz_ref.at[s], dx_ref.at[s], db_ref.at[s],
                    dc_ref.at[s], ddt_ref.at[s], ds_ref.at[s], colbuf.at[s], rowbuf.at[s]) for s in range(nb)]
        dnw, dd, dal, dbias = [functools.reduce(lambda p, r: p + r, [sm[i] for sm in sums]) for i in range(4)]

        @pl.when(first)
        def _():
            dnw_ref[...] = jnp.broadcast_to(dnw, (8, gw))
            dd_ref[...] = dd
            dal_ref[...] = jnp.broadcast_to(dal, (8, LANES))
            dbias_ref[...] = jnp.broadcast_to(dbias, (8, LANES))

        @pl.when(jnp.logical_not(first))
        def _():
            dnw_ref[...] += jnp.broadcast_to(dnw, (8, gw))
            dd_ref[...] += dd
            dal_ref[...] += jnp.broadcast_to(dal, (8, LANES))
            dbias_ref[...] += jnp.broadcast_to(dbias, (8, LANES))

    def one(x_ref, b_ref, c_ref, z_ref, dtr_ref, y_ref, g_ref, st_ref, dtb_ref, al_ref, dsk_ref, nw_ref,
            dz_ref, dx_ref, db_ref, dc_ref, ddt_ref, ds_ref, colbuf, rowbuf):
        x = x_ref[...]
        bm = b_ref[...].astype(BF16)
        cm = c_ref[...].astype(BF16)
        z = z_ref[...].astype(F32)
        y = y_ref[...]
        prev = st_ref[...]
        dtr = dtr_ref[...] + dtb_ref[0:1, :]
        (dt, a_row, acum, acum_t, ac_last, eac, de, dt_x, eac_x, de_x, cdmat, k) = _ssd_common(
            dtr_ref[...], dtb_ref[0:1, :], al_ref[0:1, :], gw)
        row, col = k["row"], k["col"]
        et2 = jnp.concatenate([k["et"], k["et"]], axis=0)

        sz = _sigmoid(z)
        silu_z = z * sz
        yg = y * silu_z
        r = lax.rsqrt(jnp.mean(yg * yg, axis=-1, keepdims=True) + EPS)
        xh = yg * r
        dyn = g_ref[...].astype(F32)
        gh = dyn * nw_ref[0:1, :]
        dyg = r * (gh - xh * jnp.mean(gh * xh, axis=-1, keepdims=True))
        dnw = jnp.sum(dyn * xh, axis=0, keepdims=True)
        g = dyg * silu_z
        dz_ref[...] = (dyg * y * (sz * (1.0 + z * (1.0 - sz)))).astype(BF16)
        dd = _sel_right(jnp.broadcast_to(jnp.sum(g * x, axis=0, keepdims=True), (8, gw)), et2, 2)

        xdt = x * dt_x
        xdt_b = xdt.astype(BF16)
        g_b = g.astype(BF16)
        prev_b = prev.astype(BF16)
        cb = _nt(cm, bm)

        cp = _nt(cm, prev_b)
        ge = g * eac_x
        dac = _sel_right(ge * cp, et2, 2)
        ge_b = ge.astype(BF16)
        dcm = _nn(ge_b, prev_b)
        dprev = _tn(ge_b, cm)

        colbuf[...] = jnp.zeros_like(colbuf)
        rowbuf[...] = jnp.zeros_like(rowbuf)
        dcb = jnp.zeros((q, q), F32)
        half = _head_of(lax.broadcasted_iota(jnp.int32, (q, LANES), 1))
        pairs = []
        for j in range(nh):
            pc = (j // 2) * LANES
            dec = _decay(acum, acum_t, j, row, col)
            m = cb * dec
            gj = jnp.where(half == j % 2, g[:, pc:pc + LANES], 0.0).astype(BF16)
            dm = _nt(gj, xdt_b[:, pc:pc + LANES])
            w = dm * m
            colbuf[:, j:j + 1] = jnp.sum(w, axis=1, keepdims=True)
            rowbuf[j:j + 1, :] = jnp.sum(w, axis=0, keepdims=True)
            dcb = dcb + dm * dec
            dj = jnp.where(half == j % 2, _tn(m.astype(BF16), g_b[:, pc:pc + LANES]), 0.0)
            if j % 2 == 0:
                pairs.append(dj)
            else:
                pairs[-1] = pairs[-1] + dj
        dxdt = jnp.concatenate(pairs, axis=1)
        dcb_b = dcb.astype(BF16)
        dcm = dcm + _nn(dcb_b, bm)
        dbm = _tn(dcb_b, cm)

        ds = ds_ref[...]
        ds_b = ds.astype(BF16)
        u = _nt(bm, ds_b)
        dxdt = dxdt + u * de_x
        dde = _sel_right(u * xdt, et2, 2)
        dbm = dbm + _nn((xdt * de_x).astype(BF16), ds_b)
        pm = jnp.concatenate(_split(ds * prev, 2), axis=1)
        t2 = _tn(pm, k["et"])
        dcd_row = jnp.sum(t2[0:D_STATE] + t2[D_STATE:2 * D_STATE], axis=0, keepdims=True)
        last = dcd_row * jnp.exp(ac_last) + jnp.sum(dde * de, axis=0, keepdims=True)
        dac = dac + colbuf[...] - rowbuf[...].T - dde * de + jnp.where(row == q - 1, last, 0.0)
        ds_ref[...] = cdmat * ds + dprev

        dadt = _sel_left(k["triu"], dac)
        ddt = _sel_right(dxdt * x, et2, 2) + dadt * a_row
        dal = jnp.sum(dadt * dt, axis=0, keepdims=True) * a_row
        lane = lax.broadcasted_iota(jnp.int32, (q, LANES), 1)
        ddtr = jnp.where(lane < nh, ddt * _sigmoid(dtr), 0.0)
        ddt_ref[...] = ddtr.astype(BF16)
        dbias = jnp.sum(ddtr, axis=0, keepdims=True)
        dx_ref[...] = dxdt * dt_x + dsk_ref[0:1, :] * g
        db_ref[...] = dbm
        dc_ref[...] = dcm
        return dnw, dd, dal, dbias

    rc = lambda c: nc - 1 - c
    par = lambda w: pl.BlockSpec((None, 8, w), lambda g, i, c: (g, 0, 0))
    blk = lambda w: pl.BlockSpec((nb, q, w), lambda g, i, c: (i, rc(c), g))
    return pl.pallas_call(
        body,
        out_shape=(jax.ShapeDtypeStruct((b, seq, zx.shape[2]), BF16),
                   jax.ShapeDtypeStruct((b, seq, d_inner), F32),
                   jax.ShapeDtypeStruct((b, seq, N_GROUPS * D_STATE), F32),
                   jax.ShapeDtypeStruct((b, seq, N_GROUPS * D_STATE), F32),
                   jax.ShapeDtypeStruct((b, seq, N_GROUPS * LANES), BF16),
                   jax.ShapeDtypeStruct((N_GROUPS, 8, gw), F32),
                   jax.ShapeDtypeStruct((N_GROUPS, 8, LANES), F32),
                   jax.ShapeDtypeStruct((N_GROUPS, 8, LANES), F32),
                   jax.ShapeDtypeStruct((N_GROUPS, 8, LANES), F32)),
        grid=(N_GROUPS, b // nb, nc),
        in_specs=[blk(gw),
                  pl.BlockSpec((nb, q, D_STATE), lambda g, i, c: (i, rc(c), xb0 + g)),
                  pl.BlockSpec((nb, q, D_STATE), lambda g, i, c: (i, rc(c), xc0 + g)),
                  blk(gw),
                  pl.BlockSpec((nb, q, LANES), lambda g, i, c: (i, rc(c), g)),
                  blk(gw), blk(gw),
                  pl.BlockSpec((nb, None, None, gw, D_STATE), lambda g, i, c: (i, rc(c), g, 0, 0)),
                  par(LANES), par(LANES), par(gw), par(gw)],
        out_specs=(blk(gw), blk(gw), blk(D_STATE), blk(D_STATE), blk(LANES),
                   par(gw), par(LANES), par(LANES), par(LANES)),
        scratch_shapes=[pltpu.VMEM((nb, gw, D_STATE), F32), pltpu.VMEM((nb, q, LANES), F32),
                        pltpu.VMEM((nb, LANES, q), F32)],
        compiler_params=_params(("parallel", "arbitrary", "arbitrary")), name=name,
    )(xc, xc, xc, zx, dtr, y, dyn, st, dtb, alog, dskip, nw)


def _adamw(w, g, m, v, name):
    rows, cols = w.shape
    tr = rows
    for cand in (512, 256, 128, 64, 32, 16, 8):
        if rows % cand == 0 and cand * cols * 4 <= 2 * 1024 * 1024:
            tr = cand
            break
    c1 = 1.0 - ADAM_B1 ** ADAM_STEP
    c2 = 1.0 - ADAM_B2 ** ADAM_STEP

    def body(w_ref, g_ref, m_ref, v_ref, d_ref, mo_ref, vo_ref):
        gv = g_ref[...]
        mn = ADAM_B1 * m_ref[...] + (1.0 - ADAM_B1) * gv
        vn = ADAM_B2 * v_ref[...] + (1.0 - ADAM_B2) * (gv * gv)
        mo_ref[...] = mn
        vo_ref[...] = vn
        d_ref[...] = -ADAM_LR * ((mn / c1) / (jnp.sqrt(vn / c2) + ADAM_EPS) + ADAM_WD * w_ref[...])

    spec = pl.BlockSpec((tr, cols), lambda i: (i, 0))
    shp = jax.ShapeDtypeStruct((rows, cols), F32)
    return pl.pallas_call(body, out_shape=(shp, shp, shp), grid=(rows // tr,), in_specs=[spec] * 4,
                          out_specs=(spec,) * 3, compiler_params=_params(("parallel",)), name=name)(w, g, m, v)


def _pick_rows(rows, row_bytes, limit=1 << 20):
    for cand in (2048, 1024, 512, 256, 128, 64, 32, 16):
        if rows % cand == 0 and cand * row_bytes <= limit:
            return cand
    return rows


def _as3d(a, lead):
    return a.reshape(a.shape[:lead] + (-1, a.shape[-1]))


def _pair_sum(g, got, core, name):
    h = got.shape[0]
    g3, got3 = _as3d(g, 1), _as3d(got, 1)
    _, rows, cols = got3.shape
    tr = _pick_rows(rows, cols * 4)

    def body(c_ref, g_ref, r_ref, o_ref):
        o_ref[...] = (g_ref[...] + r_ref[...]).astype(BF16)

    out = pl.pallas_call(
        body, out_shape=jax.ShapeDtypeStruct(got3.shape, BF16),
        grid_spec=pltpu.PrefetchScalarGridSpec(
            num_scalar_prefetch=1, grid=(h, rows // tr),
            in_specs=[pl.BlockSpec((None, tr, cols), lambda l, i, c_ref: (c_ref[0] * h + l, i, 0)),
                      pl.BlockSpec((None, tr, cols), lambda l, i, c_ref: (l, i, 0))],
            out_specs=pl.BlockSpec((None, tr, cols), lambda l, i, c_ref: (l, i, 0))),
        compiler_params=_params(("parallel", "parallel")), name=name)(core, g3, got3)
    return out.reshape(got.shape)


def _sum4(q, core, name):
    q4 = _as3d(q, 2)
    _, h, rows, cols = q4.shape
    tr = _pick_rows(rows, cols * 4)

    def body(c_ref, q0, q1, q2, q3, o_ref):
        o_ref[...] = ((q0[...].astype(F32) + q1[...].astype(F32)) + q2[...].astype(F32)) + q3[...].astype(F32)

    out = pl.pallas_call(
        body, out_shape=jax.ShapeDtypeStruct((2 * h, rows, cols), F32),
        grid_spec=pltpu.PrefetchScalarGridSpec(
            num_scalar_prefetch=1, grid=(h, rows // tr),
            in_specs=[pl.BlockSpec((None, None, tr, cols), lambda l, i, c_ref, k=k: (k, l, i, 0))
                      for k in range(N_CHIPS)],
            out_specs=pl.BlockSpec((None, tr, cols), lambda l, i, c_ref: (c_ref[0] * h + l, i, 0))),
        compiler_params=_params(("parallel", "parallel")), name=name)(core, q4, q4, q4, q4)
    return out.reshape((2 * h,) + q.shape[2:])


def _coords():
    return lax.axis_index("x"), lax.axis_index("y"), lax.axis_index("c")


def _other_chips(x, y):
    return [(1 - x, y), (x, 1 - y), (1 - x, 1 - y)]


def _allgather_halves(src, name):
    rows, cols = src.shape

    def body(x_ref, o_ref, send, recv, local):
        x, y, c = _coords()
        sib = (x, y, 1 - c)
        chips = _other_chips(x, y)

        def slot(h, cx, cy):
            return o_ref.at[h, 2 * cx + cy]

        def copy(kk, dst, to, src_ref):
            return pltpu.make_async_remote_copy(src_ref=src_ref, dst_ref=dst, send_sem=send.at[kk],
                                                recv_sem=recv.at[kk], device_id=to, device_id_type=MESH)

        mine = pltpu.make_async_copy(x_ref, slot(c, x, y), local)
        mine.start()
        first = [copy(0, slot(c, x, y), sib, x_ref)]
        first += [copy(1 + j, slot(c, x, y), (*chip, c), x_ref) for j, chip in enumerate(chips)]
        for cp in first:
            cp.start()
        passed = [copy(4 + j, slot(c, *chip), sib, slot(c, *chip)) for j, chip in enumerate(chips)]
        for j, chip in enumerate(chips):
            copy(1 + j, slot(c, *chip), (x, y, c), x_ref).wait_recv()
            passed[j].start()
        copy(0, slot(1 - c, x, y), (x, y, c), x_ref).wait_recv()
        for j, chip in enumerate(chips):
            copy(4 + j, slot(1 - c, *chip), (x, y, c), x_ref).wait_recv()
        for cp in first + passed:
            cp.wait_send()
        mine.wait()

    return pl.pallas_call(
        body, out_shape=jax.ShapeDtypeStruct((2, N_CHIPS, rows, cols), src.dtype),
        in_specs=[ANY], out_specs=ANY,
        scratch_shapes=[pltpu.SemaphoreType.DMA((7,)), pltpu.SemaphoreType.DMA((7,)), pltpu.SemaphoreType.DMA],
        name=name)(src)


MIXW = (("ssd_w_in", None), ("ssd_w_out", 0), ("pool_w", 1))
FFNW = (("ffn_w_up", 1), ("ffn_w_down", 0))


def _chip_window(axis, ref, layers, k):
    if axis is None:
        return ref.at[layers, k]
    n = ref.shape[1 + axis] // N_CHIPS
    sl = pl.ds(pl.multiple_of(k * n, LANES if 1 + axis == len(ref.shape) - 1 else 8), n)
    idx = [layers] + [slice(None)] * (len(ref.shape) - 1)
    idx[1 + axis] = sl
    return ref.at[tuple(idx)]


def _full_shape(axis, shard_shape):
    if axis is None:
        return (shard_shape[0], N_CHIPS) + tuple(shard_shape[1:])
    full = list(shard_shape)
    full[1 + axis] *= N_CHIPS
    return tuple(full)


HBM_SPEC = pl.BlockSpec(memory_space=pltpu.HBM)
SEM_SPEC = pl.BlockSpec(memory_space=pltpu.SEMAPHORE)


def _dma_sems(count):
    return pltpu.SemaphoreType.DMA((max(count, 1),))


def _wait_for(copy, kind):
    if kind == "recv":
        copy.wait_recv()
    elif kind == "send":
        copy.wait_send()
    else:
        copy.wait()


def _comm_fused(stages, counts, srcs, lands, name, inplace=False):
    ns, nl, k = len(srcs), len(lands), len(stages)

    def body(*refs):
        src_refs = refs[:ns]
        land_refs = refs[ns + (nl if inplace else 0):ns + (nl if inplace else 0) + nl]
        sem_refs = refs[len(refs) - 3 * k:]
        for s, stage_fn in enumerate(stages):
            starts, waits = stage_fn(src_refs, land_refs, tuple(sem_refs[3 * s:3 * s + 3]))
            for cp in starts:
                cp.start()
            for cp, kind in waits:
                _wait_for(cp, kind)

    scratch = []
    for cnt in counts:
        scratch += [_dma_sems(c) for c in cnt]
    outs = pl.pallas_call(
        body, out_shape=tuple(jax.ShapeDtypeStruct(a.shape, a.dtype) for a in lands),
        in_specs=[ANY] * (ns + (nl if inplace else 0)), out_specs=(ANY,) * nl,
        input_output_aliases={ns + i: i for i in range(nl)} if inplace else {},
        scratch_shapes=scratch, name=name)(*srcs, *(lands if inplace else ()))
    return list(outs)


class _SplitComm:
    def __init__(self, stages, counts, srcs, lands, name):
        self.stages, self.counts, self.name = stages, counts, name
        self.ns = len(srcs)
        self.data = [pltpu.with_memory_space_constraint(a, pltpu.HBM) for a in list(srcs) + list(lands)]
        self.sems = None
        self.step = 0

    def advance(self, after=None):
        i, k, nd, ns = self.step, len(self.stages), len(self.data), self.ns
        first, last = i == 0, i == k
        stages = self.stages
        after = list(after) if isinstance(after, (list, tuple)) else [after]

        def body(*refs):
            data = refs[:nd]
            pos = nd
            if not first:
                old = tuple(refs[pos:pos + 3])
                pos += 3 + len(after)
            if not last:
                new = tuple(refs[pos:pos + 3])
            if not first:
                for cp, kind in stages[i - 1](data[:ns], data[ns:], old)[1]:
                    _wait_for(cp, kind)
            if not last:
                for cp in stages[i](data[:ns], data[ns:], new)[0]:
                    cp.start()
                refs[len(refs) - 1][...] = jnp.zeros((8, LANES), F32)

        args = list(self.data)
        in_specs = [HBM_SPEC] * nd
        if not first:
            args += list(self.sems) + after
            in_specs += [SEM_SPEC] * 3 + [ANY] * len(after)
        out_shape, out_specs = [], []
        if not last:
            out_shape += [_dma_sems(c) for c in self.counts[i]]
            out_specs += [SEM_SPEC] * 3
        out_shape += [pltpu.HBM(a.shape, a.dtype) for a in self.data]
        out_specs += [HBM_SPEC] * nd
        if not last:
            out_shape.append(jax.ShapeDtypeStruct((8, LANES), F32))
            out_specs.append(pl.BlockSpec(memory_space=pltpu.VMEM))
        off = 0 if last else 3
        outs = pl.pallas_call(
            body, out_shape=tuple(out_shape), in_specs=in_specs, out_specs=tuple(out_specs),
            input_output_aliases={d: off + d for d in range(nd)},
            compiler_params=pltpu.CompilerParams(has_side_effects=pltpu.SideEffectType.DATAFLOW_SIDE_EFFECTING),
            name=f"{self.name}_{i}")(*args)
        self.sems = None if last else outs[:3]
        self.data = list(outs[off:off + nd])
        self.step += 1
        return None if last else outs[len(outs) - 1]

    def lands(self):
        return self.data[self.ns:]


def _gather_stages(spec):
    n = len(spec)

    def parts(srcs, lands):
        x, y, c = _coords()
        out = []
        for w, (_, axis) in enumerate(spec):
            h = srcs[w].shape[0] // 2
            mine, theirs = pl.ds(c * h, h), pl.ds((1 - c) * h, h)
            out.append((srcs[w].at[mine], lambda layers, k, w=w, axis=axis: _chip_window(axis, lands[w], layers, k),
                        mine, theirs))
        return x, y, c, 2 * x + y, (x, y, 1 - c), _other_chips(x, y), out

    def remote(src, dst, send, recv, idx, to):
        return pltpu.make_async_remote_copy(src_ref=src, dst_ref=dst, send_sem=send.at[idx], recv_sem=recv.at[idx],
                                            device_id=to, device_id_type=MESH)

    def stage0(srcs, lands, sems):
        send, recv, local = sems
        x, y, c, me, sib, chips, ps = parts(srcs, lands)
        starts, waits = [], []
        for w, (src, dst, mine, theirs) in enumerate(ps):
            lc = pltpu.make_async_copy(src, dst(mine, me), local.at[w])
            first = [remote(src, dst(mine, me), send, recv, 4 * w, sib)]
            first += [remote(src, dst(mine, me), send, recv, 4 * w + 1 + j, (cx, cy, c)) for j, (cx, cy) in enumerate(chips)]
            starts += [lc] + first
            waits.append((remote(src, dst(theirs, me), send, recv, 4 * w, (x, y, c)), "recv"))
            waits += [(remote(src, dst(mine, 2 * cx + cy), send, recv, 4 * w + 1 + j, (x, y, c)), "recv")
                      for j, (cx, cy) in enumerate(chips)]
            waits += [(cp, "send") for cp in first] + [(lc, "local")]
        return starts, waits

    def stage1(srcs, lands, sems):
        send, recv, _ = sems
        x, y, c, me, sib, chips, ps = parts(srcs, lands)
        starts, waits = [], []
        for w, (src, dst, mine, theirs) in enumerate(ps):
            for j, (cx, cy) in enumerate(chips):
                blk = dst(mine, 2 * cx + cy)
                fwd = remote(blk, blk, send, recv, 3 * w + j, sib)
                starts.append(fwd)
                waits.append((remote(src, dst(theirs, 2 * cx + cy), send, recv, 3 * w + j, (x, y, c)), "recv"))
                waits.append((fwd, "send"))
        return starts, waits

    return [stage0, stage1], [(4 * n, 4 * n, n), (3 * n, 3 * n, 0)]


def _swap_stages(spec):
    n = len(spec)

    def stage(srcs, lands, sems):
        send, recv, _ = sems
        x, y, c = _coords()
        starts, waits = [], []
        for w in range(n):
            h = srcs[w].shape[0] // 2
            cp = pltpu.make_async_remote_copy(src_ref=srcs[w].at[pl.ds((1 - c) * h, h)], dst_ref=lands[w],
                                              send_sem=send.at[w], recv_sem=recv.at[w],
                                              device_id=(x, y, 1 - c), device_id_type=MESH)
            starts.append(cp)
            waits += [(cp, "recv"), (cp, "send")]
        return starts, waits

    return [stage], [(n, n, 0)]


def _scatter_stages(spec):
    n = len(spec)

    def stage(srcs, lands, sems):
        send, recv, local = sems
        x, y, c = _coords()
        me = 2 * x + y
        starts, waits = [], []
        for w, (_, axis) in enumerate(spec):
            layers = pl.ds(0, srcs[w].shape[0])
            own = _chip_window(axis, srcs[w], layers, me)
            lc = pltpu.make_async_copy(own, lands[w].at[me], local.at[w])
            starts.append(lc)
            for j, (cx, cy) in enumerate(_other_chips(x, y)):
                cp = pltpu.make_async_remote_copy(src_ref=_chip_window(axis, srcs[w], layers, 2 * cx + cy),
                                                  dst_ref=lands[w].at[me], send_sem=send.at[3 * w + j],
                                                  recv_sem=recv.at[3 * w + j], device_id=(cx, cy, c), device_id_type=MESH)
                starts.append(cp)
                waits.append((pltpu.make_async_remote_copy(
                    src_ref=own, dst_ref=lands[w].at[2 * cx + cy], send_sem=send.at[3 * w + j], recv_sem=recv.at[3 * w + j],
                    device_id=(x, y, c), device_id_type=MESH), "recv"))
                waits.append((cp, "send"))
            waits.append((lc, "local"))
        return starts, waits

    return [stage], [(3 * n, 3 * n, n)]


def _share_stages(spec):
    n = len(spec)

    def stage(srcs, lands, sems):
        send, recv, _ = sems
        x, y, c = _coords()
        starts, waits = [], []
        for w in range(n):
            h = lands[w].shape[0] // 2
            mine, theirs = lands[w].at[pl.ds(c * h, h)], lands[w].at[pl.ds((1 - c) * h, h)]
            cp = pltpu.make_async_remote_copy(src_ref=mine, dst_ref=mine, send_sem=send.at[w], recv_sem=recv.at[w],
                                              device_id=(x, y, 1 - c), device_id_type=MESH)
            starts.append(cp)
            waits.append((pltpu.make_async_remote_copy(src_ref=theirs, dst_ref=theirs, send_sem=send.at[w],
                                                       recv_sem=recv.at[w], device_id=(x, y, c), device_id_type=MESH),
                          "recv"))
            waits.append((cp, "send"))
        return starts, waits

    return [stage], [(n, n, 0)]


def _shard_of(p, axis):
    if axis is None:
        return (p.shape[0],) + tuple(p.shape[2:])
    s = list(p.shape)
    s[1 + axis] //= N_CHIPS
    return tuple(s)


def _gather8_stages():
    def stage(srcs, lands, sems):
        send, recv, local = sems
        x, y, c = _coords()
        me = 4 * x + 2 * y + c
        lc = pltpu.make_async_copy(srcs[0], lands[0].at[me], local.at[0])
        starts, waits = [lc], []
        for kk in range(1, 8):
            to = (1 - x if kk & 4 else x, 1 - y if kk & 2 else y, 1 - c if kk & 1 else c)
            cp = pltpu.make_async_remote_copy(src_ref=srcs[0], dst_ref=lands[0].at[me], send_sem=send.at[kk - 1],
                                              recv_sem=recv.at[kk - 1], device_id=to, device_id_type=MESH)
            starts.append(cp)
            waits.append((pltpu.make_async_remote_copy(
                src_ref=srcs[0], dst_ref=lands[0].at[4 * to[0] + 2 * to[1] + to[2]], send_sem=send.at[kk - 1],
                recv_sem=recv.at[kk - 1], device_id=(x, y, c), device_id_type=MESH), "recv"))
            waits.append((cp, "send"))
        waits.append((lc, "local"))
        return starts, waits

    return [stage], [(7, 7, 1)]


def _sum8(buf, name):
    _, rows, cols = buf.shape
    tr = _pick_rows(rows, cols * 4)

    def body(*refs):
        acc = refs[0][...]
        for r in refs[1:8]:
            acc = acc + r[...]
        refs[8][...] = acc

    return pl.pallas_call(
        body, out_shape=jax.ShapeDtypeStruct((rows, cols), F32), grid=(rows // tr,),
        in_specs=[pl.BlockSpec((None, tr, cols), lambda i, k=k: (k, i, 0)) for k in range(8)],
        out_specs=pl.BlockSpec((tr, cols), lambda i: (i, 0)),
        compiler_params=_params(("parallel",)), name=name)(*([buf] * 8))


def _reduce_begin(spec, gs, core, tag, riders=()):
    stages, counts = _swap_stages(spec)
    got = _comm_fused(stages, counts, list(gs) + list(riders),
                      [jax.ShapeDtypeStruct((g.shape[0] // 2,) + g.shape[1:], g.dtype) for g in gs], "swap_" + tag)
    pair = [_pair_sum(a, r, core, "pair_sum_" + n) for a, r, (n, _) in zip(gs, got, spec)]
    stages, counts = _scatter_stages(spec)
    lands = [lax.empty((N_CHIPS,) + _shard_of(p, axis), p.dtype) for p, (_, axis) in zip(pair, spec)]
    comm = _SplitComm(stages, counts, pair, lands, "scatter_" + tag)
    return comm, comm.advance()


def _reduce_finish(spec, comm, core, tag, after):
    comm.advance(after=after)
    halves = [_sum4(q, core, "sum4_" + n) for q, (n, _) in zip(comm.lands(), spec)]
    stages, counts = _share_stages(spec)
    return _comm_fused(stages, counts, [], halves, "share_" + tag, inplace=True)


SMALL = (("ssd_conv_w", 2), ("pool_scale", 1), ("ffn_conv_w", 2))
REPL = ("ssd_conv_b", "ssd_dt_bias", "ssd_a_log", "ssd_d", "ssd_norm_w", "ffn_conv_b",
        "norm_mix_pre", "norm_mix_post", "norm_ffn_pre", "norm_ffn_post")
WEIGHTS = ("ssd_w_in", "ssd_conv_w", "ssd_conv_b", "ssd_dt_bias", "ssd_a_log", "ssd_d", "ssd_norm_w", "ssd_w_out",
           "pool_w", "pool_scale", "ffn_w_up", "ffn_conv_w", "ffn_conv_b", "ffn_w_down", "norm_mix_pre",
           "norm_mix_post", "norm_ffn_pre", "norm_ffn_post")


def _flat_rows(n):
    unit = 2 * 16 * FLAT_COLS
    return 2 * 16 * ((n + unit - 1) // unit)


def _flatten_shards(arrs, dtype):
    flat = jnp.concatenate([a.astype(dtype).reshape(-1) for a in arrs])
    rows = _flat_rows(flat.shape[0])
    flat = jnp.pad(flat, (0, rows * FLAT_COLS - flat.shape[0]))
    return flat.reshape(2, rows // 2, FLAT_COLS)


def _unflatten_full(gathered, shard_shapes, axes):
    per_chip = jnp.swapaxes(gathered, 0, 1).reshape(N_CHIPS, -1)
    out, off = [], 0
    for shp, ax in zip(shard_shapes, axes):
        n = math.prod(shp)
        pieces = [per_chip[k, off:off + n].reshape(shp) for k in range(N_CHIPS)]
        out.append(jnp.concatenate(pieces, axis=ax))
        off += n
    return out


def kernel(x, ssd_w_in, ssd_conv_w, ssd_conv_b, ssd_dt_bias, ssd_a_log, ssd_d, ssd_norm_w, ssd_w_out, pool_w, pool_scale, ffn_w_up, ffn_conv_w, ffn_conv_b, ffn_w_down, norm_mix_pre, norm_mix_post, norm_ffn_pre, norm_ffn_post, loss_target, m_ssd_w_in, m_ssd_conv_w, m_ssd_conv_b, m_ssd_dt_bias, m_ssd_a_log, m_ssd_d, m_ssd_norm_w, m_ssd_w_out, m_pool_w, m_pool_scale, m_ffn_w_up, m_ffn_conv_w, m_ffn_conv_b, m_ffn_w_down, m_norm_mix_pre, m_norm_mix_post, m_norm_ffn_pre, m_norm_ffn_post, v_ssd_w_in, v_ssd_conv_w, v_ssd_conv_b, v_ssd_dt_bias, v_ssd_a_log, v_ssd_d, v_ssd_norm_w, v_ssd_w_out, v_pool_w, v_pool_scale, v_ffn_w_up, v_ffn_conv_w, v_ffn_conv_b, v_ffn_w_down, v_norm_mix_pre, v_norm_mix_post, v_norm_ffn_pre, v_norm_ffn_post):
    wts = dict(ssd_w_in=ssd_w_in, ssd_conv_w=ssd_conv_w, ssd_conv_b=ssd_conv_b, ssd_dt_bias=ssd_dt_bias,
               ssd_a_log=ssd_a_log, ssd_d=ssd_d, ssd_norm_w=ssd_norm_w, ssd_w_out=ssd_w_out, pool_w=pool_w,
               pool_scale=pool_scale, ffn_w_up=ffn_w_up, ffn_conv_w=ffn_conv_w, ffn_conv_b=ffn_conv_b,
               ffn_w_down=ffn_w_down, norm_mix_pre=norm_mix_pre, norm_mix_post=norm_mix_post,
               norm_ffn_pre=norm_ffn_pre, norm_ffn_post=norm_ffn_post)
    mom = dict(ssd_w_in=m_ssd_w_in, ssd_conv_w=m_ssd_conv_w, ssd_conv_b=m_ssd_conv_b, ssd_dt_bias=m_ssd_dt_bias,
               ssd_a_log=m_ssd_a_log, ssd_d=m_ssd_d, ssd_norm_w=m_ssd_norm_w, ssd_w_out=m_ssd_w_out, pool_w=m_pool_w,
               pool_scale=m_pool_scale, ffn_w_up=m_ffn_w_up, ffn_conv_w=m_ffn_conv_w, ffn_conv_b=m_ffn_conv_b,
               ffn_w_down=m_ffn_w_down, norm_mix_pre=m_norm_mix_pre, norm_mix_post=m_norm_mix_post,
               norm_ffn_pre=m_norm_ffn_pre, norm_ffn_post=m_norm_ffn_post)
    var = dict(ssd_w_in=v_ssd_w_in, ssd_conv_w=v_ssd_conv_w, ssd_conv_b=v_ssd_conv_b, ssd_dt_bias=v_ssd_dt_bias,
               ssd_a_log=v_ssd_a_log, ssd_d=v_ssd_d, ssd_norm_w=v_ssd_norm_w, ssd_w_out=v_ssd_w_out, pool_w=v_pool_w,
               pool_scale=v_pool_scale, ffn_w_up=v_ffn_w_up, ffn_conv_w=v_ffn_conv_w, ffn_conv_b=v_ffn_conv_b,
               ffn_w_down=v_ffn_w_down, norm_mix_pre=v_norm_mix_pre, norm_mix_post=v_norm_mix_post,
               norm_ffn_pre=v_norm_ffn_pre, norm_ffn_post=v_norm_ffn_post)

    bl, seq, d = x.shape
    t = bl * seq
    depth = norm_mix_pre.shape[0]
    n_ssd = ssd_w_out.shape[0]
    d_inner = ssd_w_out.shape[1] * N_CHIPS
    nheads = d_inner // HEAD_DIM
    hpg = nheads // N_GROUPS
    gw = d_inner // N_GROUPS
    xbc = ssd_conv_w.shape[2] * N_CHIPS
    f2 = ffn_w_up.shape[2] * N_CHIPS
    ff = f2 // 2
    dg = d // 4
    cy = lax.axis_index("c")
    chip = 2 * lax.axis_index("x") + lax.axis_index("y")

    small_shapes = [wts[n].shape for n, _ in SMALL]
    small_axes = [a for _, a in SMALL]
    small_flat = _flatten_shards([wts[n] for n, _ in SMALL], F32)
    small_half = lax.dynamic_index_in_dim(small_flat, cy, 0, keepdims=False)
    small_all = _allgather_halves(small_half, "gather_small")
    conv_w, p_scale, f_conv_w = _unflatten_full(small_all, small_shapes, small_axes)
    def full_shapes(spec, shards):
        return [jax.ShapeDtypeStruct(_full_shape(axis, s.shape), s.dtype) for s, (_, axis) in zip(shards, spec)]

    def row_halves(a):
        return a.reshape((2, a.shape[0] // 2) + a.shape[1:])

    def join_w_in(g):
        return jnp.concatenate([g[:, k] for k in range(N_CHIPS)], axis=-1).reshape(d, -1)

    def join_w_out(g):
        r2 = g.shape[1] // N_CHIPS
        return jnp.concatenate([g[hf, k * r2:(k + 1) * r2] for k in range(N_CHIPS) for hf in range(2)], axis=0)

    ssd_spec = (("ssd_w_in", None), ("ssd_w_out", 0))
    first_shards = [row_halves(wts[n][0].astype(BF16)) for n, _ in ssd_spec]
    stages, counts = _gather_stages(ssd_spec)
    g_in0, g_out0 = _comm_fused(stages, counts, first_shards, full_shapes(ssd_spec, first_shards), "gather_first")
    w_in, w_out = [join_w_in(g_in0)], [join_w_out(g_out0)]
    rest_spec = ssd_spec * (n_ssd - 1) + (("pool_w", 1),) + FFNW
    rest_shards = [row_halves(wts[n][jj].astype(BF16)) for jj in range(1, n_ssd) for n, _ in ssd_spec]
    rest_shards += [wts["pool_w"].astype(BF16)] + [wts[n].astype(BF16) for n, _ in FFNW]
    stages, counts = _gather_stages(rest_spec)
    ffn_gather = _SplitComm(stages, counts, rest_shards + [g_out0],
                            [lax.empty(s.shape, s.dtype) for s in full_shapes(rest_spec, rest_shards)], "gather_rest")
    gather_token = ffn_gather.advance()

    def pad_heads(a):
        lead = a.shape[:-1]
        a = a.reshape(lead + (N_GROUPS, hpg))
        a = jnp.pad(a, [(0, 0)] * len(lead) + [(0, 0), (0, LANES - hpg)])
        return a.reshape(lead + (N_GROUPS * LANES,))

    def unpad_heads(a):
        lead = a.shape[:-1]
        return a.reshape(lead + (N_GROUPS, LANES))[..., :hpg].reshape(lead + (nheads,))

    def group_rows(a, width):
        return jnp.broadcast_to(a.reshape(N_GROUPS, 1, width), (N_GROUPS, 8, width))

    def pad_w_in(w):
        return jnp.concatenate([w[..., :d_inner + xbc], pad_heads(w[..., d_inner + xbc:])], axis=-1)

    w_in_p = [pad_w_in(w_in[0])]
    zw = w_in_p[0].shape[-1]
    w_pool = None

    x2 = x.reshape(t, d)
    tgt2 = loss_target.reshape(t, d)
    w_up = w_down = None

    saved = []
    cur = x2
    tokens = []
    h = _norm_fwd(cur, norm_mix_pre[0:1], BF16, "norm_pre_b", after=[gather_token])
    for i in range(depth):
        j = i // 2
        sv = dict(x_in=cur)
        if i % 2 == 0:
            zx = _mm(h, w_in_p[j], "nn", BF16, "mm_ssd_in", 2048, 512, d).reshape(bl, seq, zw)
            dtr = _mm(h, w_in_p[j][:, d_inner + xbc:], "nn", F32, "mm_ssd_dt", 2048, 512, d).reshape(bl, seq, -1)
            xc, xpre = _ssd_conv_fwd(zx, conv_w[j], ssd_conv_b[j:j + 1], d_inner, "ssd_conv_fwd")
            dtb = group_rows(pad_heads(ssd_dt_bias[j]), LANES)
            alog = group_rows(pad_heads(ssd_a_log[j]), LANES)
            dskip = group_rows(jnp.repeat(ssd_d[j], HEAD_DIM), gw)
            nw = group_rows(ssd_norm_w[j], gw)
            y, yn, st = _ssd_fwd(xc, zx, dtr, dtb, alog, dskip, nw, d_inner, "ssd_fwd")
            if i == 0:
                tokens.append(ffn_gather.advance(after=yn))
            mix = _mm(yn.reshape(t, d_inner), w_out[j], "nn", F32, "mm_ssd_out", 2048, 512, d_inner)
            sv.update(h=h, zx=zx, dtr=dtr, xc=xc, xpre=xpre, y=y, yn=yn, st=st, dtb=dtb, alog=alog, dskip=dskip, nw=nw)
        else:
            mix = _pool_fwd(h.reshape(bl, seq, d), w_pool[j], p_scale[j:j + 1], "pool_fwd").reshape(t, d)
            sv.update(h=h)
        sv.update(mix=mix)
        mid, u = _norm_post_pre(mix, norm_mix_post[i:i + 1], cur, norm_ffn_pre[i:i + 1], BF16, "norm_post_pre_b",
                                after=tokens)
        tokens = []
        if i == 0:
            ffn_gather.advance(after=u)
            rest = ffn_gather.lands()
            for jj in range(1, n_ssd):
                w_in_p.append(pad_w_in(join_w_in(rest[2 * (jj - 1)])))
                w_out.append(join_w_out(rest[2 * (jj - 1) + 1]))
            w_pool, w_up, w_down = rest[2 * (n_ssd - 1):]
        hpre = _mm(u, w_up, "nn", BF16, "mm_up", 2048, 512, d, b_layer=i).reshape(bl, seq, f2)
        act, pre_g, pre_v = _ffn_act_fwd(hpre, f_conv_w[i], ffn_conv_b[i:i + 1], "ffn_act_fwd")
        act = act.reshape(t, ff)
        fo = _mm(act, w_down, "nn", F32, "mm_down", 2048, 512, ff, b_layer=i)
        if i + 1 == depth:
            cur = _norm_fwd(fo, norm_ffn_post[i:i + 1], F32, "norm_post", resid=mid)
        elif i % 2 == 0:
            cur, h = _norm_post_pre(fo, norm_ffn_post[i:i + 1], mid, norm_mix_pre[i + 1:i + 2], F32, "norm_post_pre_f")
        else:
            cur, h = _norm_post_pre(fo, norm_ffn_post[i:i + 1], mid, norm_mix_pre[i + 1:i + 2], BF16, "norm_post_pre_b")
        sv.update(mid=mid, u=u, hpre=hpre, pre_g=pre_g, pre_v=pre_v, act=act, fo=fo)
        saved.append(sv)

    dcur, loss_part = _loss_head(cur, tgt2, "loss_head")

    g = {n: [None] * wts[n].shape[0] for n in WEIGHTS}
    gbuf = dict(up=lax.empty((depth, d, f2), F32), down=lax.empty((depth, ff, d), F32),
                out=lax.empty((n_ssd, d_inner, d), F32), win=lax.empty((n_ssd, d, zw), F32))
    core = cy.reshape(1).astype(jnp.int32)

    def mixer_bwd(i, dmid, dmix):
        j = i // 2
        sv = saved[i]
        done = []
        if i % 2 == 0:
            dyn = _mm(dmix, w_out[j], "nt", BF16, "mm_ssd_out_dx", 1024, 1024, d)
            gbuf["out"], tok = _mm(sv["yn"].reshape(t, d_inner), dmix, "tn", F32, "mm_ssd_out_dw", 1024, 1024, 2048,
                                   out_buf=(gbuf["out"], j))
            done.append(tok)
            dz, dxs, dbm, dcm, ddt, dnw, dd, dal, dbias = _ssd_bwd(
                sv["xc"], sv["zx"], sv["dtr"], sv["y"], dyn.reshape(bl, seq, d_inner), sv["st"], sv["dtb"], sv["alog"],
                sv["dskip"], sv["nw"], d_inner, "ssd_bwd")
            g["ssd_norm_w"][j] = dnw[:, 0, :].reshape(d_inner)
            g["ssd_d"][j] = dd[:, 0, :hpg].reshape(nheads)
            g["ssd_a_log"][j] = dal[:, 0, :hpg].reshape(nheads)
            g["ssd_dt_bias"][j] = dbias[:, 0, :hpg].reshape(nheads)
            dzx, dcw, dcb = _ssd_conv_bwd(sv["zx"], sv["xpre"], (dxs, dbm, dcm), ddt, dz, conv_w[j], d_inner,
                                          "ssd_conv_bwd")
            g["ssd_conv_w"][j] = dcw
            g["ssd_conv_b"][j] = dcb[0]
            dzx = dzx.reshape(t, zw)
            dh = _mm(dzx, w_in_p[j], "nt", BF16, "mm_ssd_in_dx", 1024, d, zw // 2)
            gbuf["win"], tok = _mm(sv["h"], dzx, "tn", F32, "mm_ssd_in_dw", 1024, zw // 4, 2048, out_buf=(gbuf["win"], j))
            done.append(tok)
        else:
            dh3, g["pool_w"][j], dps = _pool_bwd(sv["h"].reshape(bl, seq, d), dmix.reshape(bl, seq, d), w_pool[j],
                                                 p_scale[j:j + 1], "pool_bwd")
            g["pool_scale"][j] = dps[0]
            dh = dh3.reshape(t, d)
        if i == 0:
            dx_in, g["norm_mix_pre"][i] = _norm_bwd(sv["x_in"], norm_mix_pre[i:i + 1], dh, F32, "norm_bwd_r", resid=dmid,
                                                    after=done)
            return dx_in, None
        dx_in, dfo_prev, g["norm_mix_pre"][i], g["norm_ffn_post"][i - 1] = _norm_bwd2(
            sv["x_in"], norm_mix_pre[i:i + 1], dh, dmid, saved[i - 1]["fo"], norm_ffn_post[i - 1:i], BF16,
            "norm_bwd_in_post", after=done)
        return dx_in, dfo_prev

    ffn_comm = None
    dfo, g["norm_ffn_post"][depth - 1] = _norm_bwd(saved[depth - 1]["fo"], norm_ffn_post[depth - 1:depth], dcur, BF16,
                                                   "norm_bwd_b")
    for i in reversed(range(depth)):
        sv = saved[i]
        dact = _mm(dfo, w_down, "nt", BF16, "mm_down_dx", 1024, ff // 2, d, b_layer=i)
        gbuf["down"], tok_down = _mm(sv["act"], dfo, "tn", F32, "mm_down_dw", ff // 2, 1024, 2048,
                                     out_buf=(gbuf["down"], i))
        dhg, dhv, dcw, dcb = _ffn_act_bwd(sv["hpre"], sv["pre_g"], sv["pre_v"], dact.reshape(bl, seq, ff), f_conv_w[i],
                                          "ffn_act_bwd")
        g["ffn_conv_w"][i] = dcw
        g["ffn_conv_b"][i] = dcb[0]
        dhs = [dhg.reshape(t, ff), dhv.reshape(t, ff)]
        du = _mm(dhs, w_up, "nt", BF16, "mm_up_dx", 1024, d, ff, b_layer=i)
        gbuf["up"], tok_up = _mm(sv["u"], dhs, "tn", F32, "mm_up_dw", 1024, ff // 2, 2048, out_buf=(gbuf["up"], i))
        behind = [tok_down, tok_up]
        if i == 0:
            ffn_comm, ffn_token = _reduce_begin(FFNW, [gbuf["up"], gbuf["down"]], core, "ffn")
            behind.append(ffn_token)
        dmid, dmix, g["norm_ffn_pre"][i], g["norm_mix_post"][i] = _norm_bwd2(
            sv["mid"], norm_ffn_pre[i:i + 1], du, dcur, sv["mix"], norm_mix_post[i:i + 1], BF16 if i % 2 == 0 else F32,
            "norm_bwd_pre_post_b" if i % 2 == 0 else "norm_bwd_pre_post_f", after=behind)
        dcur, dfo = mixer_bwd(i, dmid, dmix)

    grad_x = dcur.reshape(bl, seq, d)
    for n in ("norm_mix_pre", "norm_mix_post", "norm_ffn_pre", "norm_ffn_post"):
        g[n] = [a[0] for a in g[n]]
    small_names = [n for n, _ in SMALL] + list(REPL)
    full = {n: jnp.stack(g[n], axis=0) for n in small_names}

    g_in = jnp.concatenate([gbuf["win"][..., :d_inner + xbc], unpad_heads(gbuf["win"][..., d_inner + xbc:])], axis=-1)
    g_in_cm = jnp.swapaxes(g_in.reshape(n_ssd, d, N_CHIPS, -1), 1, 2)
    vec = jnp.concatenate([full[n].reshape(-1) for n in small_names] + [loss_part[0, :1]])
    nvec = vec.shape[0]
    vrows = 16 * ((nvec + 16 * FLAT_COLS - 1) // (16 * FLAT_COLS))
    vec = jnp.pad(vec, (0, vrows * FLAT_COLS - nvec)).reshape(vrows, FLAT_COLS)
    stages, counts = _gather8_stages()
    small_comm = _SplitComm(stages, counts, [vec], [lax.empty((8, vrows, FLAT_COLS), F32)], "gather_small_grads")
    small_token = small_comm.advance()
    mix_comm, mix_token = _reduce_begin(MIXW, [g_in_cm, gbuf["out"], jnp.stack(g["pool_w"], axis=0)], core, "mixers",
                                        riders=[small_token])

    grads, deltas, new_m, new_v = {}, {}, {}, {}

    def adamw(n, gr):
        shp = wts[n].shape
        two = (math.prod(shp[:-1]), shp[-1])
        dl, mn, vn = _adamw(wts[n].reshape(two), gr.reshape(two), mom[n].reshape(two), var[n].reshape(two),
                            "adamw_" + n)
        grads[n], deltas[n], new_m[n], new_v[n] = gr, dl.reshape(shp), mn.reshape(shp), vn.reshape(shp)
        return dl

    small_comm.advance(after=mix_token)
    tot = _sum8(small_comm.lands()[0], "sum_small").reshape(-1)
    small_grads, off = {}, 0
    for n in small_names:
        cnt = math.prod(full[n].shape)
        small_grads[n] = tot[off:off + cnt].reshape(full[n].shape)
        off += cnt
    loss = tot[off]
    for n, ax in SMALL:
        w = wts[n].shape[ax]
        small_grads[n] = lax.dynamic_slice_in_dim(small_grads[n], chip * w, w, axis=ax)

    behind = [adamw(n, small_grads[n]) for n in small_names][-1:]
    ffn_grads = _reduce_finish(FFNW, ffn_comm, core, "ffn", after=mix_token)
    behind += [adamw(n, gr) for gr, (n, _) in zip(ffn_grads, FFNW)]
    mix_grads = _reduce_finish(MIXW, mix_comm, core, "mixers", after=behind)
    for gr, (n, _) in zip(mix_grads, MIXW):
        adamw(n, gr)

    return (loss, grad_x, *[grads[n] for n in WEIGHTS], *[deltas[n] for n in WEIGHTS],
            *[new_m[n] for n in WEIGHTS], *[new_v[n] for n in WEIGHTS])
```

```python
import functools
import math

import jax
import jax.numpy as jnp
from jax import lax
from jax.experimental import pallas as pl
from jax.experimental.pallas import tpu as pltpu

F32 = jnp.float32
BF16 = jnp.bfloat16
MESH = pl.DeviceIdType.MESH
ANY = pl.BlockSpec(memory_space=pl.ANY)

HEAD_DIM = 64
D_STATE = 128
CHUNK = 128
N_GROUPS = 4
SSD_CONV = 4
FFN_CONV = 3
EPS = 1e-6
N_CHIPS = 4
LANES = 128
FLAT_COLS = 1024

ADAM_LR = 0.001
ADAM_B1 = 0.9
ADAM_B2 = 0.999
ADAM_EPS = 1e-08
ADAM_WD = 0.01
ADAM_STEP = 10

VMEM_LIMIT_BYTES = 56 * 1024 * 1024


def _params(sem=None):
    kw = dict(vmem_limit_bytes=VMEM_LIMIT_BYTES)
    if sem is not None:
        kw["dimension_semantics"] = sem
    return pltpu.CompilerParams(**kw)


def _sigmoid(x):
    return 0.5 * jnp.tanh(0.5 * x) + 0.5


def _softplus(x):
    return jnp.maximum(x, 0.0) + jnp.log(1.0 + jnp.exp(-jnp.abs(x)))


def _dot(a, b, dn):
    return lax.dot_general(a, b, (dn, ((), ())), preferred_element_type=F32)


def _nn(a, b):
    return _dot(a, b, ((1,), (0,)))


def _nt(a, b):
    return _dot(a, b, ((1,), (1,)))


def _tn(a, b):
    return _dot(a, b, ((0,), (0,)))


def _split(x, parts):
    out = []
    r = x
    for _ in range(parts):
        p = r.astype(BF16)
        out.append(p)
        r = r - p.astype(F32)
    return out


def _sel_left(sel, x, parts=3):
    n = x.shape[1]
    r = _nn(sel, jnp.concatenate(_split(x, parts), axis=1))
    out = r[:, 0:n]
    for i in range(1, parts):
        out = out + r[:, i * n:(i + 1) * n]
    return out


def _sel_right(x, sel_stacked, parts=3):
    return _nn(jnp.concatenate(_split(x, parts), axis=1), sel_stacked)


def _mm(a, b, dims, out_dtype, name, tm, tn, tk, b_layer=None, out_buf=None, after=()):
    a_list = list(a) if isinstance(a, (list, tuple)) else [a]
    b_list = list(b) if isinstance(b, (list, tuple)) else [b]
    if dims in ("nn", "nt"):
        assert len(b_list) == 1
        m = a_list[0].shape[0]
        segs = [x.shape[1] for x in a_list]
        k = sum(segs)
        bshape = b_list[0].shape[-2:]
        n = bshape[1] if dims == "nn" else bshape[0]
        assert (bshape[0] if dims == "nn" else bshape[1]) == k
    else:
        assert len(a_list) == 1 and b_layer is None
        k, m = a_list[0].shape
        segs = [x.shape[1] for x in b_list]
        n = sum(segs)
    nseg = len(segs)
    tm, tn = min(tm, m), min(tn, n)
    if dims == "tn":
        tk = min(tk, k)
        tn = min(tn, min(segs))
        units = [tn] * nseg
        nk = k // tk
        assert k % tk == 0
    else:
        units = [min(u, s) for u, s in zip(tk if isinstance(tk, (list, tuple)) else [tk] * nseg, segs)]
        nk = sum(s // u for s, u in zip(segs, units))
    assert m % tm == 0 and n % tn == 0 and all(s % u == 0 for s, u in zip(segs, units)), (name, m, n, k, segs, units)
    counts = [s // u for s, u in zip(segs, units)]
    starts = [sum(counts[:s]) for s in range(nseg)]
    assert all(sum(segs[:s]) % units[s] == 0 for s in range(nseg)), (name, segs, units)
    first_block = [sum(segs[:s]) // units[s] for s in range(nseg)]
    dn = {"nn": ((1,), (0,)), "nt": ((1,), (1,)), "tn": ((0,), (0,))}[dims]

    same = len(set(units)) == 1
    nb_ops = len(b_list) if dims == "tn" else (1 if same else nseg)

    def body(*refs):
        a_refs = refs[:len(a_list)]
        b_refs = refs[len(a_list):len(a_list) + nb_ops]
        rest = refs[len(a_list) + nb_ops + (0 if out_buf is None else 1) + len(after):]
        o_ref = rest[0]
        if out_buf is not None:
            rest[1][...] = jnp.zeros((8, LANES), F32)
            rest = rest[1:]
        acc = rest[1] if nk > 1 else None
        kk = pl.program_id(2)
        sel = kk if dims != "tn" else pl.program_id(1)

        def step(a_ref, b_ref):
            p = _dot(a_ref[...].astype(BF16), b_ref[...].astype(BF16), dn)
            if nk == 1:
                o_ref[...] = p.astype(out_dtype)
                return

            @pl.when(kk == 0)
            def _():
                acc[...] = p

            @pl.when(kk > 0)
            def _():
                acc[...] += p

        if nseg == 1:
            step(a_refs[0], b_refs[0])
        else:
            for s in range(nseg):
                @pl.when(jnp.logical_and(sel >= starts[s], sel < starts[s] + counts[s]))
                def _(s=s):
                    step(a_refs[s] if dims != "tn" else a_refs[0], b_refs[s if nb_ops > 1 else 0])

        if nk > 1:
            @pl.when(kk == nk - 1)
            def _():
                o_ref[...] = acc[...].astype(out_dtype)

    def seg_index(v, s):
        return v if nseg == 1 else jnp.clip(v - starts[s], 0, counts[s] - 1)

    lead = () if b_layer is None else (b_layer,)
    none = () if b_layer is None else (None,)
    def b_block(kk, s):
        return kk if same else first_block[s] + seg_index(kk, s)

    if dims == "nn":
        a_specs = [pl.BlockSpec((tm, units[s]), lambda i, j, kk, s=s: (i, seg_index(kk, s))) for s in range(nseg)]
        b_specs = [pl.BlockSpec(none + (units[s], tn), lambda i, j, kk, s=s: lead + (b_block(kk, s), j))
                   for s in range(nb_ops)]
    elif dims == "nt":
        a_specs = [pl.BlockSpec((tm, units[s]), lambda i, j, kk, s=s: (i, seg_index(kk, s))) for s in range(nseg)]
        b_specs = [pl.BlockSpec(none + (tn, units[s]), lambda i, j, kk, s=s: lead + (j, b_block(kk, s)))
                   for s in range(nb_ops)]
    else:
        a_specs = [pl.BlockSpec((tk, tm), lambda i, j, kk: (kk, i))]
        b_specs = [pl.BlockSpec((tk, tn), lambda i, j, kk, s=s: (kk, seg_index(j, s))) for s in range(nseg)]
    args = a_list + (b_list * nb_ops if dims != "tn" else b_list)
    in_specs = a_specs + b_specs
    aliases = {}
    if out_buf is None:
        out_shape = jax.ShapeDtypeStruct((m, n), out_dtype)
        out_spec = pl.BlockSpec((tm, tn), lambda i, j, kk: (i, j))
    else:
        buf, slab = out_buf
        assert buf.shape[1:] == (m, n) and buf.dtype == out_dtype
        out_shape = (jax.ShapeDtypeStruct(buf.shape, out_dtype), jax.ShapeDtypeStruct((8, LANES), F32))
        out_spec = (pl.BlockSpec((None, tm, tn), lambda i, j, kk: (slab, i, j)),
                    pl.BlockSpec((8, LANES), lambda i, j, kk: (0, 0)))
        aliases = {len(args): 0}
        args = args + [buf]
        in_specs = in_specs + [ANY]
    after = [x for x in after if x is not None]
    args = args + after
    in_specs = in_specs + [ANY] * len(after)
    return pl.pallas_call(
        body,
        out_shape=out_shape,
        grid=(m // tm, n // tn, nk),
        in_specs=in_specs,
        out_specs=out_spec,
        scratch_shapes=[] if nk == 1 else [pltpu.VMEM((tm, tn), F32)],
        input_output_aliases=aliases,
        compiler_params=_params(("parallel", "parallel", "arbitrary") if out_buf is None else ("arbitrary",) * 3),
        name=name,
    )(*args)


def _row_tile(t, want):
    tm = min(want, t)
    assert t % tm == 0
    return tm


def _norm_fwd(x, w, out_dtype, name, resid=None, after=()):
    t, d = x.shape
    tm = _row_tile(t, 512)
    after = [a for a in after if a is not None]

    def body(*refs):
        refs = refs[:len(refs) - 1 - len(after)] + refs[len(refs) - 1:]
        if resid is None:
            x_ref, w_ref, o_ref = refs
        else:
            x_ref, w_ref, r_ref, o_ref = refs
        xv = x_ref[...]
        r = lax.rsqrt(jnp.mean(xv * xv, axis=-1, keepdims=True) + EPS)
        y = (xv * r) * w_ref[...]
        if resid is not None:
            y = r_ref[...] + y
        o_ref[...] = y.astype(out_dtype)

    row = pl.BlockSpec((tm, d), lambda i: (i, 0))
    vec = pl.BlockSpec((1, d), lambda i: (0, 0))
    args = [x, w] + ([] if resid is None else [resid]) + after
    return pl.pallas_call(
        body, out_shape=jax.ShapeDtypeStruct((t, d), out_dtype), grid=(t // tm,),
        in_specs=[row, vec] + ([] if resid is None else [row]) + [ANY] * len(after), out_specs=row,
        compiler_params=_params(("parallel",)), name=name)(*args)


def _norm_post_pre(m, w_post, resid, w_pre, pre_dtype, name, after=()):
    t, d = m.shape
    tm = _row_tile(t, 512)
    after = [a for a in after if a is not None]

    def body(m_ref, w1_ref, r_ref, w2_ref, *rest):
        x_ref, u_ref = rest[len(after):]
        mv = m_ref[...]
        r1 = lax.rsqrt(jnp.mean(mv * mv, axis=-1, keepdims=True) + EPS)
        xv = r_ref[...] + (mv * r1) * w1_ref[...]
        x_ref[...] = xv
        r2 = lax.rsqrt(jnp.mean(xv * xv, axis=-1, keepdims=True) + EPS)
        u_ref[...] = ((xv * r2) * w2_ref[...]).astype(pre_dtype)

    row = pl.BlockSpec((tm, d), lambda i: (i, 0))
    vec = pl.BlockSpec((1, d), lambda i: (0, 0))
    return pl.pallas_call(
        body, out_shape=(jax.ShapeDtypeStruct((t, d), F32), jax.ShapeDtypeStruct((t, d), pre_dtype)), grid=(t // tm,),
        in_specs=[row, vec, row, vec] + [ANY] * len(after), out_specs=(row, row),
        compiler_params=_params(("parallel",)), name=name)(m, w_post, resid, w_pre, *after)


def _norm_bwd(src, w, dy, out_dtype, name, resid=None, after=()):
    t, d = src.shape
    tm = _row_tile(t, 512)
    after = [a for a in after if a is not None]

    def body(*refs):
        refs = refs[:len(refs) - 2 - len(after)] + refs[len(refs) - 2:]
        if resid is None:
            x_ref, w_ref, g_ref, o_ref, dw_ref = refs
        else:
            x_ref, w_ref, g_ref, r_ref, o_ref, dw_ref = refs
        xv = x_ref[...]
        g = g_ref[...].astype(F32)
        r = lax.rsqrt(jnp.mean(xv * xv, axis=-1, keepdims=True) + EPS)
        xh = xv * r
        gh = g * w_ref[...]
        mean = jnp.mean(gh * xh, axis=-1, keepdims=True)
        dx = r * (gh - xh * mean)
        if resid is not None:
            dx = r_ref[...] + dx
        o_ref[...] = dx.astype(out_dtype)
        part = jnp.sum(g * xh, axis=0, keepdims=True)

        @pl.when(pl.program_id(0) == 0)
        def _():
            dw_ref[...] = part

        @pl.when(pl.program_id(0) > 0)
        def _():
            dw_ref[...] += part

    row = pl.BlockSpec((tm, d), lambda i: (i, 0))
    vec = pl.BlockSpec((1, d), lambda i: (0, 0))
    args = [src, w, dy] + ([] if resid is None else [resid]) + after
    return pl.pallas_call(
        body,
        out_shape=(jax.ShapeDtypeStruct((t, d), out_dtype), jax.ShapeDtypeStruct((1, d), F32)),
        grid=(t // tm,),
        in_specs=[row, vec, row] + ([] if resid is None else [row]) + [ANY] * len(after),
        out_specs=(row, vec),
        compiler_params=_params(("arbitrary",)), name=name)(*args)


def _norm_bwd2(src1, w1, dy1, resid, src2, w2, out2_dtype, name, after=()):
    t, d = src1.shape
    tm = _row_tile(t, 512)
    after = [a for a in after if a is not None]

    def back(xv, w, g):
        r = lax.rsqrt(jnp.mean(xv * xv, axis=-1, keepdims=True) + EPS)
        xh = xv * r
        gh = g * w
        return r * (gh - xh * jnp.mean(gh * xh, axis=-1, keepdims=True)), jnp.sum(g * xh, axis=0, keepdims=True)

    def body(x1_ref, w1_ref, g1_ref, r_ref, x2_ref, w2_ref, *rest):
        d1_ref, d2_ref, dw1_ref, dw2_ref = rest[len(after):]
        d1, p1 = back(x1_ref[...], w1_ref[...], g1_ref[...].astype(F32))
        d1 = r_ref[...] + d1
        d1_ref[...] = d1
        d2, p2 = back(x2_ref[...], w2_ref[...], d1)
        d2_ref[...] = d2.astype(out2_dtype)

        @pl.when(pl.program_id(0) == 0)
        def _():
            dw1_ref[...] = p1
            dw2_ref[...] = p2

        @pl.when(pl.program_id(0) > 0)
        def _():
            dw1_ref[...] += p1
            dw2_ref[...] += p2

    row = pl.BlockSpec((tm, d), lambda i: (i, 0))
    vec = pl.BlockSpec((1, d), lambda i: (0, 0))
    return pl.pallas_call(
        body,
        out_shape=(jax.ShapeDtypeStruct((t, d), F32), jax.ShapeDtypeStruct((t, d), out2_dtype),
                   jax.ShapeDtypeStruct((1, d), F32), jax.ShapeDtypeStruct((1, d), F32)),
        grid=(t // tm,),
        in_specs=[row, vec, row, row, row, vec] + [ANY] * len(after),
        out_specs=(row, row, vec, vec),
        compiler_params=_params(("arbitrary",)), name=name)(src1, w1, dy1, resid, src2, w2, *after)


def _loss_head(y, target, name):
    t, d = y.shape
    tm = _row_tile(t, 512)

    def body(y_ref, t_ref, dy_ref, l_ref):
        e = y_ref[...] - t_ref[...]
        dy_ref[...] = e * (1.0 / d)
        col = jnp.sum(e * e, axis=0, keepdims=True)
        s = jnp.sum(col, axis=1, keepdims=True) * (0.5 / d)
        part = jnp.broadcast_to(s, (1, LANES))

        @pl.when(pl.program_id(0) == 0)
        def _():
            l_ref[...] = part

        @pl.when(pl.program_id(0) > 0)
        def _():
            l_ref[...] += part

    row = pl.BlockSpec((tm, d), lambda i: (i, 0))
    return pl.pallas_call(
        body,
        out_shape=(jax.ShapeDtypeStruct((t, d), F32), jax.ShapeDtypeStruct((1, LANES), F32)),
        grid=(t // tm,), in_specs=[row, row],
        out_specs=(row, pl.BlockSpec((1, LANES), lambda i: (0, 0))),
        compiler_params=_params(("arbitrary",)), name=name)(y, target)


def _window(ref, c, rows, seq, before, after, keep=None):
    r0 = pl.multiple_of(c * rows, rows)
    parts = []
    if before:
        h0 = pl.multiple_of(jnp.maximum(r0 - before, 0), before)
        halo = ref[pl.ds(h0, before), :].astype(F32)
        halo = halo if keep is None else halo[before - keep:, :]
        parts.append(jnp.where(c > 0, halo, 0.0))
    parts.append(ref[pl.ds(r0, rows), :].astype(F32))
    if after:
        h1 = pl.multiple_of(jnp.minimum(r0 + rows, seq - after), after)
        halo = ref[pl.ds(h1, after), :].astype(F32)
        halo = halo if keep is None else halo[:keep, :]
        parts.append(jnp.where(c < seq // rows - 1, halo, 0.0))
    return parts[0] if len(parts) == 1 else jnp.concatenate(parts, axis=0)


def _lag(x, k):
    return pltpu.roll(x, k, 0) if k else x


def _lead(x, k):
    return pltpu.roll(x, x.shape[0] - k, 0) if k else x


SHIFT_ROWS = 128
SHIFT_COLS = 256


HALO = 16
KEEP = 8


def _conv3(ext, w, bias):
    acc = bias + w[2:3, :] * ext[KEEP:, :]
    acc = acc + w[1:2, :] * _lag(ext, 1)[KEEP:, :]
    return acc + w[0:1, :] * _lag(ext, 2)[KEEP:, :]


def _ffn_act_fwd(hpre, cw, cb, name):
    b, seq, f2 = hpre.shape
    cbk = SHIFT_COLS
    nj = f2 // (2 * cbk)
    rows = min(SHIFT_ROWS, seq)

    def body(g_ref, v_ref, wg_ref, wv_ref, bg_ref, bv_ref, o_ref, pg_ref, pv_ref):
        def chunk(c, carry):
            gate = _conv3(_window(g_ref, c, rows, seq, HALO, 0, KEEP), wg_ref[...], bg_ref[...])
            val = _conv3(_window(v_ref, c, rows, seq, HALO, 0, KEEP), wv_ref[...], bv_ref[...])
            a = gate * _sigmoid(gate) * val
            here = pl.ds(pl.multiple_of(c * rows, rows), rows)
            o_ref[here, :] = a.astype(BF16)
            pg_ref[here, :] = gate.astype(BF16)
            pv_ref[here, :] = val.astype(BF16)
            return carry

        lax.fori_loop(0, seq // rows, chunk, 0)

    blk = lambda off: pl.BlockSpec((None, seq, cbk), lambda i, j: (i, 0, j + off))
    wsp = lambda r, off: pl.BlockSpec((r, cbk), lambda i, j: (0, j + off))
    half = jax.ShapeDtypeStruct((b, seq, f2 // 2), BF16)
    return pl.pallas_call(
        body, out_shape=(half, half, half), grid=(b, nj),
        in_specs=[blk(0), blk(nj), wsp(FFN_CONV, 0), wsp(FFN_CONV, nj), wsp(1, 0), wsp(1, nj)],
        out_specs=(blk(0), blk(0), blk(0)),
        compiler_params=_params(("parallel", "parallel")), name=name)(hpre, hpre, cw, cw, cb, cb)


def _ffn_act_bwd(hpre, pre_g, pre_v, da, cw, name):
    b, seq, f2 = hpre.shape
    cbk = SHIFT_COLS
    nj = f2 // (2 * cbk)
    rows = min(SHIFT_ROWS, seq)

    def body(g_ref, v_ref, pg_ref, pv_ref, da_ref, wg_ref, wv_ref, og_ref, ov_ref, dwg_ref, dwv_ref, dbg_ref, dbv_ref):
        wg, wv = wg_ref[...], wv_ref[...]

        def back(dpre, w, o_ref, x_ref, c, carry):
            here = pl.ds(pl.multiple_of(c * rows, rows), rows)
            leads = [dpre, _lead(dpre, 1), _lead(dpre, 2)]
            dx = w[2:3, :] * leads[0] + w[1:2, :] * leads[1] + w[0:1, :] * leads[2]
            o_ref[here, :] = dx[:rows, :].astype(BF16)
            x0 = x_ref[here, :].astype(F32)
            return tuple(carry[k] + jnp.sum(leads[k][:rows, :] * x0, axis=0, keepdims=True) for k in range(FFN_CONV)) + (
                carry[FFN_CONV] + jnp.sum(dpre[:rows, :], axis=0, keepdims=True),)

        def chunk(c, carry):
            cg, cv = carry
            gate = _window(pg_ref, c, rows, seq, 0, HALO, KEEP)
            val = _window(pv_ref, c, rows, seq, 0, HALO, KEEP)
            dav = _window(da_ref, c, rows, seq, 0, HALO, KEEP)
            sg = _sigmoid(gate)
            cg = back(dav * val * (sg * (1.0 + gate * (1.0 - sg))), wg, og_ref, g_ref, c, cg)
            cv = back(dav * (gate * sg), wv, ov_ref, v_ref, c, cv)
            return cg, cv

        z = jnp.zeros((1, cbk), F32)
        cg, cv = lax.fori_loop(0, seq // rows, chunk, ((z,) * (FFN_CONV + 1), (z,) * (FFN_CONV + 1)))
        dwg = jnp.concatenate([cg[2], cg[1], cg[0]], axis=0)
        dwv = jnp.concatenate([cv[2], cv[1], cv[0]], axis=0)

        @pl.when(pl.program_id(1) == 0)
        def _():
            dwg_ref[...] = dwg
            dwv_ref[...] = dwv
            dbg_ref[...] = cg[FFN_CONV]
            dbv_ref[...] = cv[FFN_CONV]

        @pl.when(pl.program_id(1) > 0)
        def _():
            dwg_ref[...] += dwg
            dwv_ref[...] += dwv
            dbg_ref[...] += cg[FFN_CONV]
            dbv_ref[...] += cv[FFN_CONV]

    blk = lambda off: pl.BlockSpec((None, seq, cbk), lambda j, i: (i, 0, j + off))
    wsp = lambda r, off: pl.BlockSpec((r, cbk), lambda j, i: (0, j + off))
    half = jax.ShapeDtypeStruct((b, seq, f2 // 2), BF16)
    dwshape = jax.ShapeDtypeStruct((FFN_CONV, f2 // 2), F32)
    dbshape = jax.ShapeDtypeStruct((1, f2 // 2), F32)
    dg, dv, dwg, dwv, dbg, dbv = pl.pallas_call(
        body,
        out_shape=(half, half, dwshape, dwshape, dbshape, dbshape),
        grid=(nj, b),
        in_specs=[blk(0), blk(nj), blk(0), blk(0), blk(0), wsp(FFN_CONV, 0), wsp(FFN_CONV, nj)],
        out_specs=(blk(0), blk(0), wsp(FFN_CONV, 0), wsp(FFN_CONV, 0), wsp(1, 0), wsp(1, 0)),
        compiler_params=_params(("parallel", "arbitrary")), name=name)(hpre, hpre, pre_g, pre_v, da, cw, cw)
    return dg, dv, jnp.concatenate([dwg, dwv], axis=1), jnp.concatenate([dbg, dbv], axis=1)


def _ssd_conv_fwd(zx, cw, cb, d_inner, name):
    b, seq, _ = zx.shape
    xbc = cw.shape[1]
    cbk = SHIFT_COLS
    off = d_inner // cbk
    rows = min(SHIFT_ROWS, seq)

    def body(h_ref, w_ref, b_ref, o_ref, p_ref):
        w = w_ref[...]
        bias = b_ref[...]

        def chunk(c, carry):
            ext = _window(h_ref, c, rows, seq, HALO, 0, KEEP)
            acc = bias + w[3:4, :] * ext[KEEP:, :]
            for k in range(1, SSD_CONV):
                acc = acc + w[3 - k:4 - k, :] * _lag(ext, k)[KEEP:, :]
            here = pl.ds(pl.multiple_of(c * rows, rows), rows)
            o_ref[here, :] = acc * _sigmoid(acc)
            p_ref[here, :] = acc.astype(BF16)
            return carry

        lax.fori_loop(0, seq // rows, chunk, 0)

    blk = pl.BlockSpec((None, seq, cbk), lambda i, j: (i, 0, j))
    return pl.pallas_call(
        body, out_shape=(jax.ShapeDtypeStruct((b, seq, xbc), F32), jax.ShapeDtypeStruct((b, seq, xbc), BF16)),
        grid=(b, xbc // cbk),
        in_specs=[pl.BlockSpec((None, seq, cbk), lambda i, j: (i, 0, j + off)),
                  pl.BlockSpec((SSD_CONV, cbk), lambda i, j: (0, j)),
                  pl.BlockSpec((1, cbk), lambda i, j: (0, j))],
        out_specs=(blk, blk),
        compiler_params=_params(("parallel", "parallel")), name=name)(zx, cw, cb)


def _ssd_conv_bwd(zx, pre, dparts, ddt, dzx, cw, d_inner, name):
    b, seq, zw = zx.shape
    xbc = cw.shape[1]
    cbk = SHIFT_COLS
    off = d_inner // cbk
    rows = min(SHIFT_ROWS, seq)
    nblk = [p.shape[2] // cbk for p in dparts]
    first = [sum(nblk[:s]) for s in range(len(dparts))]
    nconv = xbc // cbk
    ncopy = ddt.shape[2] // cbk
    assert sum(nblk) == nconv and (off + nconv + ncopy) * cbk == zw and dzx.shape == (b, seq, zw)

    def body(h_ref, p_ref, gx_ref, gb_ref, gc_ref, t_ref, w_ref, z_ref, o_ref, dw_ref, db_ref):
        j = pl.program_id(0)

        @pl.when(j < nconv)
        def _():
            conv(h_ref, p_ref, gx_ref, gb_ref, gc_ref, w_ref, o_ref, dw_ref, db_ref)

        @pl.when(j >= nconv)
        def _():
            o_ref[...] = t_ref[...]

    def conv(h_ref, p_ref, gx_ref, gb_ref, gc_ref, w_ref, o_ref, dw_ref, db_ref):
        w = w_ref[...]
        j = pl.program_id(0)

        def chunk(c, carry):
            dws, dbias = carry
            here = pl.ds(pl.multiple_of(c * rows, rows), rows)
            pre = _window(p_ref, c, rows, seq, 0, HALO, KEEP)
            s = _sigmoid(pre)
            gsel = jnp.where(j < first[1], _window(gx_ref, c, rows, seq, 0, HALO, KEEP),
                             jnp.where(j < first[2], _window(gb_ref, c, rows, seq, 0, HALO, KEEP),
                                       _window(gc_ref, c, rows, seq, 0, HALO, KEEP)))
            dpre = gsel * (s * (1.0 + pre * (1.0 - s)))
            leads = [dpre] + [_lead(dpre, k) for k in range(1, SSD_CONV)]
            dx = w[3:4, :] * leads[0]
            for k in range(1, SSD_CONV):
                dx = dx + w[3 - k:4 - k, :] * leads[k]
            o_ref[here, :] = dx[:rows, :].astype(BF16)
            x0 = h_ref[here, :].astype(F32)
            dws = tuple(dws[k] + jnp.sum(leads[k][:rows, :] * x0, axis=0, keepdims=True) for k in range(SSD_CONV))
            dbias = dbias + jnp.sum(dpre[:rows, :], axis=0, keepdims=True)
            return dws, dbias

        z = jnp.zeros((1, cbk), F32)
        dws, dbias = lax.fori_loop(0, seq // rows, chunk, ((z,) * SSD_CONV, z))
        dwv = jnp.concatenate([dws[3 - i] for i in range(SSD_CONV)], axis=0)

        @pl.when(pl.program_id(1) == 0)
        def _():
            dw_ref[...] = dwv
            db_ref[...] = dbias

        @pl.when(pl.program_id(1) > 0)
        def _():
            dw_ref[...] += dwv
            db_ref[...] += dbias

    conv_j = lambda j: jnp.minimum(j, nconv - 1)
    return pl.pallas_call(
        body,
        out_shape=(jax.ShapeDtypeStruct((b, seq, zw), BF16), jax.ShapeDtypeStruct((SSD_CONV, xbc), F32),
                   jax.ShapeDtypeStruct((1, xbc), F32)),
        grid=(nconv + ncopy, b),
        in_specs=[pl.BlockSpec((None, seq, cbk), lambda j, i: (i, 0, conv_j(j) + off)),
                  pl.BlockSpec((None, seq, cbk), lambda j, i: (i, 0, conv_j(j)))] + [
                  pl.BlockSpec((None, seq, cbk), lambda j, i, s=s: (i, 0, jnp.clip(j - first[s], 0, nblk[s] - 1)))
                  for s in range(3)] + [
                  pl.BlockSpec((None, seq, cbk), lambda j, i: (i, 0, jnp.clip(j - nconv, 0, ncopy - 1))),
                  pl.BlockSpec((SSD_CONV, cbk), lambda j, i: (0, conv_j(j))),
                  ANY],
        out_specs=(pl.BlockSpec((None, seq, cbk), lambda j, i: (i, 0, j + off)),
                   pl.BlockSpec((SSD_CONV, cbk), lambda j, i: (0, conv_j(j))),
                   pl.BlockSpec((1, cbk), lambda j, i: (0, conv_j(j)))),
        input_output_aliases={7: 0},
        compiler_params=_params(("arbitrary", "arbitrary")), name=name)(zx, pre, *dparts, ddt, cw, dzx)


def _pool_sums(q, g, lead):
    sh = _lead if lead else _lag
    s2 = q + sh(q, 1)
    s4 = s2 + sh(s2, 2)
    s8 = s4 + sh(s4, 4)
    s16 = s8 + sh(s8, 8)
    return jnp.where(g == 0, s2, jnp.where(g == 1, s4, jnp.where(g == 2, s8, s16)))


def _pool_count(r0, n, g, shape):
    t = (r0 + lax.broadcasted_iota(jnp.int32, shape, 0) + 1).astype(F32)
    return jnp.minimum(t, (2 << g).astype(F32))


def _pool_fwd(h, pw, scale, name):
    b, seq, d = h.shape
    dg = d // 4
    rows = min(SHIFT_ROWS, seq)

    def body(h_ref, w_ref, s_ref, o_ref):
        g = pl.program_id(1)
        wmat = w_ref[...]
        sc = s_ref[...]

        def chunk(c, carry):
            r0 = c * rows
            ext = _window(h_ref, c, rows, seq, 16, 0)
            sums = _pool_sums(ext, g, False)[16:, :]
            mixed = sums / _pool_count(r0, rows, g, (rows, dg)) - ext[16:, :]
            o_ref[pl.ds(pl.multiple_of(r0, rows), rows), :] = _nn(mixed.astype(BF16), wmat) * sc
            return carry

        lax.fori_loop(0, seq // rows, chunk, 0)

    return pl.pallas_call(
        body, out_shape=jax.ShapeDtypeStruct((b, seq, d), F32), grid=(b, 4),
        in_specs=[pl.BlockSpec((None, seq, dg), lambda i, g: (i, 0, g)),
                  pl.BlockSpec((None, dg, dg), lambda i, g: (g, 0, 0)),
                  pl.BlockSpec((1, dg), lambda i, g: (0, g))],
        out_specs=pl.BlockSpec((None, seq, dg), lambda i, g: (i, 0, g)),
        compiler_params=_params(("parallel", "parallel")), name=name)(h, pw, scale)


def _pool_bwd(h, dout, pw, scale, name):
    b, seq, d = h.shape
    dg = d // 4
    rows = min(SHIFT_ROWS, seq)

    def body(h_ref, g_ref, w_ref, s_ref, o_ref, dw_ref, ds_ref, dw_acc):
        g = pl.program_id(0)
        wmat = w_ref[...]
        sc = s_ref[...]
        dw_acc[...] = jnp.zeros_like(dw_acc)

        def chunk(c, dsc):
            r0 = c * rows
            ext = _window(h_ref, c, rows, seq, 16, 0)
            sums = _pool_sums(ext, g, False)[16:, :]
            mixed = (sums / _pool_count(r0, rows, g, (rows, dg)) - ext[16:, :]).astype(BF16)
            gext = _window(g_ref, c, rows, seq, 0, 16)
            dsc = dsc + jnp.sum(gext[:rows, :] * _nn(mixed, wmat), axis=0, keepdims=True)
            dpre = (gext * sc).astype(BF16)
            dw_acc[...] += _tn(mixed, dpre[:rows, :])
            dmix = _nt(dpre, wmat)
            q = dmix / _pool_count(r0, rows + 16, g, (rows + 16, dg))
            back = _pool_sums(q, g, True)
            o_ref[pl.ds(pl.multiple_of(r0, rows), rows), :] = back[:rows, :] - dmix[:rows, :]
            return dsc

        dsc = lax.fori_loop(0, seq // rows, chunk, jnp.zeros((1, dg), F32))

        @pl.when(pl.program_id(1) == 0)
        def _():
            dw_ref[...] = dw_acc[...]
            ds_ref[...] = dsc

        @pl.when(pl.program_id(1) > 0)
        def _():
            dw_ref[...] += dw_acc[...]
            ds_ref[...] += dsc

    return pl.pallas_call(
        body,
        out_shape=(jax.ShapeDtypeStruct((b, seq, d), F32), jax.ShapeDtypeStruct((4, dg, dg), F32),
                   jax.ShapeDtypeStruct((1, d), F32)),
        grid=(4, b),
        in_specs=[pl.BlockSpec((None, seq, dg), lambda g, i: (i, 0, g)),
                  pl.BlockSpec((None, seq, dg), lambda g, i: (i, 0, g)),
                  pl.BlockSpec((None, dg, dg), lambda g, i: (g, 0, 0)),
                  pl.BlockSpec((1, dg), lambda g, i: (0, g))],
        out_specs=(pl.BlockSpec((None, seq, dg), lambda g, i: (i, 0, g)),
                   pl.BlockSpec((None, dg, dg), lambda g, i: (g, 0, 0)),
                   pl.BlockSpec((1, dg), lambda g, i: (0, g))),
        scratch_shapes=[pltpu.VMEM((dg, dg), F32)],
        compiler_params=_params(("parallel", "arbitrary")), name=name)(h, dout, pw, scale)


def _head_of(channel):
    return jnp.right_shift(channel, HEAD_DIM.bit_length() - 1)


def _ssd_consts(gw):
    q = CHUNK
    row = lax.broadcasted_iota(jnp.int32, (q, q), 0)
    col = lax.broadcasted_iota(jnp.int32, (q, q), 1)
    tril = (row >= col).astype(BF16)
    triu = (row <= col).astype(BF16)
    e = (_head_of(lax.broadcasted_iota(jnp.int32, (LANES, gw), 1))
         == lax.broadcasted_iota(jnp.int32, (LANES, gw), 0)).astype(BF16)
    et = (_head_of(lax.broadcasted_iota(jnp.int32, (gw, LANES), 0))
          == lax.broadcasted_iota(jnp.int32, (gw, LANES), 1)).astype(BF16)
    return row, col, tril, triu, e, et


def _ssd_common(dtr, dtb, alog, gw):
    q = CHUNK
    row, col, tril, triu, e, et = _ssd_consts(gw)
    dt = _softplus(dtr + dtb)
    a_row = -jnp.exp(alog)
    acum = _sel_left(tril, dt * a_row)
    ac_last = jnp.sum(jnp.where(row == q - 1, acum, 0.0), axis=0, keepdims=True)
    eac = jnp.exp(acum)
    de = jnp.exp(ac_last - acum)
    e2 = jnp.concatenate([e, e], axis=0)
    expand = _sel_right(jnp.concatenate([dt, eac, de], axis=0), e2, 2)
    dt_x, eac_x, de_x = expand[0:q], expand[q:2 * q], expand[2 * q:3 * q]
    acum_t = acum.T
    cd_col = jnp.exp(acum_t[:, q - 1:q])
    et3 = jnp.concatenate([et, et, et], axis=1)
    cdmat = _nn(et3, jnp.concatenate(_split(jnp.broadcast_to(cd_col, (LANES, D_STATE)), 3), axis=0))
    consts = dict(row=row, col=col, tril=tril, triu=triu, e=e, et=et)
    return dt, a_row, acum, acum_t, ac_last, eac, de, dt_x, eac_x, de_x, cdmat, consts


def _decay(acum, acum_t, j, row, col):
    diff = acum[:, j:j + 1] - acum_t[j:j + 1, :]
    return jnp.exp(jnp.where(row >= col, diff, -1e30))


def _ssd_fwd(xc, zx, dtr, dtb, alog, dskip, nw, d_inner, name):
    b, seq, xbc = xc.shape
    q = CHUNK
    nc = seq // q
    gw = d_inner // N_GROUPS
    nh = gw // HEAD_DIM
    xb0 = d_inner // D_STATE
    xc0 = xb0 + N_GROUPS

    nb = max(n for n in (4, 2, 1) if b % n == 0)

    def body(x_ref, b_ref, c_ref, z_ref, dtr_ref, dtb_ref, al_ref, dsk_ref, nw_ref, y_ref, yn_ref, st_ref, s_ref):
        @pl.when(pl.program_id(2) == 0)
        def _():
            s_ref[...] = jnp.zeros_like(s_ref)

        for s in range(nb):
            one(s, x_ref.at[s], b_ref.at[s], c_ref.at[s], z_ref.at[s], dtr_ref.at[s], dtb_ref, al_ref, dsk_ref, nw_ref,
                y_ref.at[s], yn_ref.at[s], st_ref.at[s], s_ref.at[s])

    def one(s, x_ref, b_ref, c_ref, z_ref, dtr_ref, dtb_ref, al_ref, dsk_ref, nw_ref, y_ref, yn_ref, st_ref, s_ref):
        prev = s_ref[...]
        st_ref[...] = prev
        x = x_ref[...]
        bm = b_ref[...].astype(BF16)
        cm = c_ref[...].astype(BF16)
        (dt, a_row, acum, acum_t, ac_last, eac, de, dt_x, eac_x, de_x, cdmat, k) = _ssd_common(
            dtr_ref[...], dtb_ref[0:1, :], al_ref[0:1, :], gw)
        xdt = x * dt_x
        xdt_b = xdt.astype(BF16)
        cb = _nt(cm, bm)
        half = _head_of(lax.broadcasted_iota(jnp.int32, (q, LANES), 1))
        pairs = []
        for j in range(nh):
            pc = (j // 2) * LANES
            m = (cb * _decay(acum, acum_t, j, k["row"], k["col"])).astype(BF16)
            yj = jnp.where(half == j % 2, _nn(m, xdt_b[:, pc:pc + LANES]), 0.0)
            if j % 2 == 0:
                pairs.append(yj)
            else:
                pairs[-1] = pairs[-1] + yj
        prev_b = prev.astype(BF16)
        y = dsk_ref[0:1, :] * x + jnp.concatenate(pairs, axis=1) + eac_x * _nt(cm, prev_b)
        s_ref[...] = cdmat * prev + _tn((xdt * de_x).astype(BF16), bm)
        y_ref[...] = y
        z = z_ref[...].astype(F32)
        yg = y * (z * _sigmoid(z))
        r = lax.rsqrt(jnp.mean(yg * yg, axis=-1, keepdims=True) + EPS)
        yn_ref[...] = ((yg * r) * nw_ref[0:1, :]).astype(BF16)

    par = lambda w: pl.BlockSpec((None, 8, w), lambda i, g, c: (g, 0, 0))
    return pl.pallas_call(
        body,
        out_shape=(jax.ShapeDtypeStruct((b, seq, d_inner), F32), jax.ShapeDtypeStruct((b, seq, d_inner), BF16),
                   jax.ShapeDtypeStruct((b, nc, N_GROUPS, gw, D_STATE), F32)),
        grid=(b // nb, N_GROUPS, nc),
        in_specs=[pl.BlockSpec((nb, q, gw), lambda i, g, c: (i, c, g)),
                  pl.BlockSpec((nb, q, D_STATE), lambda i, g, c: (i, c, xb0 + g)),
                  pl.BlockSpec((nb, q, D_STATE), lambda i, g, c: (i, c, xc0 + g)),
                  pl.BlockSpec((nb, q, gw), lambda i, g, c: (i, c, g)),
                  pl.BlockSpec((nb, q, LANES), lambda i, g, c: (i, c, g)),
                  par(LANES), par(LANES), par(gw), par(gw)],
        out_specs=(pl.BlockSpec((nb, q, gw), lambda i, g, c: (i, c, g)),
                   pl.BlockSpec((nb, q, gw), lambda i, g, c: (i, c, g)),
                   pl.BlockSpec((nb, None, None, gw, D_STATE), lambda i, g, c: (i, c, g, 0, 0))),
        scratch_shapes=[pltpu.VMEM((nb, gw, D_STATE), F32)],
        compiler_params=_params(("parallel", "parallel", "arbitrary")), name=name,
    )(xc, xc, xc, zx, dtr, dtb, alog, dskip, nw)


def _ssd_bwd(xc, zx, dtr, y, dyn, st, dtb, alog, dskip, nw, d_inner, name):
    b, seq, xbc = xc.shape
    q = CHUNK
    nc = seq // q
    gw = d_inner // N_GROUPS
    nh = gw // HEAD_DIM
    xb0 = d_inner // D_STATE
    xc0 = xb0 + N_GROUPS

    nb = max(n for n in (4, 2, 1) if b % n == 0)

    def body(x_ref, b_ref, c_ref, z_ref, dtr_ref, y_ref, g_ref, st_ref, dtb_ref, al_ref, dsk_ref, nw_ref,
             dz_ref, dx_ref, db_ref, dc_ref, ddt_ref, dnw_ref, dd_ref, dal_ref, dbias_ref,
             ds_ref, colbuf, rowbuf):
        first = jnp.logical_and(pl.program_id(1) == 0, pl.program_id(2) == 0)

        @pl.when(pl.program_id(2) == 0)
        def _():
            ds_ref[...] = jnp.zeros_like(ds_ref)

        sums = [one(x_ref.at[s], b_ref.at[s], c_ref.at[s], z_ref.at[s], dtr_ref.at[s], y_ref.at[s], g_ref.at[s],
                    st_ref.at[s], dtb_ref, al_ref, dsk_ref, nw_ref, dz_ref.at[s], dx_ref.at[s], db_ref.at[s],
                    dc_ref.at[s], ddt_ref.at[s], ds_ref.at[s], colbuf.at[s], rowbuf.at[s]) for s in range(nb)]
        dnw, dd, dal, dbias = [functools.reduce(lambda p, r: p + r, [sm[i] for sm in sums]) for i in range(4)]

        @pl.when(first)
        def _():
            dnw_ref[...] = jnp.broadcast_to(dnw, (8, gw))
            dd_ref[...] = dd
            dal_ref[...] = jnp.broadcast_to(dal, (8, LANES))
            dbias_ref[...] = jnp.broadcast_to(dbias, (8, LANES))

        @pl.when(jnp.logical_not(first))
        def _():
            dnw_ref[...] += jnp.broadcast_to(dnw, (8, gw))
            dd_ref[...] += dd
            dal_ref[...] += jnp.broadcast_to(dal, (8, LANES))
            dbias_ref[...] += jnp.broadcast_to(dbias, (8, LANES))

    def one(x_ref, b_ref, c_ref, z_ref, dtr_ref, y_ref, g_ref, st_ref, dtb_ref, al_ref, dsk_ref, nw_ref,
            dz_ref, dx_ref, db_ref, dc_ref, ddt_ref, ds_ref, colbuf, rowbuf):
        x = x_ref[...]
        bm = b_ref[...].astype(BF16)
        cm = c_ref[...].astype(BF16)
        z = z_ref[...].astype(F32)
        y = y_ref[...]
        prev = st_ref[...]
        dtr = dtr_ref[...] + dtb_ref[0:1, :]
        (dt, a_row, acum, acum_t, ac_last, eac, de, dt_x, eac_x, de_x, cdmat, k) = _ssd_common(
            dtr_ref[...], dtb_ref[0:1, :], al_ref[0:1, :], gw)
        row, col = k["row"], k["col"]
        et2 = jnp.concatenate([k["et"], k["et"]], axis=0)

        sz = _sigmoid(z)
        silu_z = z * sz
        yg = y * silu_z
        r = lax.rsqrt(jnp.mean(yg * yg, axis=-1, keepdims=True) + EPS)
        xh = yg * r
        dyn = g_ref[...].astype(F32)
        gh = dyn * nw_ref[0:1, :]
        dyg = r * (gh - xh * jnp.mean(gh * xh, axis=-1, keepdims=True))
        dnw = jnp.sum(dyn * xh, axis=0, keepdims=True)
        g = dyg * silu_z
        dz_ref[...] = (dyg * y * (sz * (1.0 + z * (1.0 - sz)))).astype(BF16)
        dd = _sel_right(jnp.broadcast_to(jnp.sum(g * x, axis=0, keepdims=True), (8, gw)), et2, 2)

        xdt = x * dt_x
        xdt_b = xdt.astype(BF16)
        g_b = g.astype(BF16)
        prev_b = prev.astype(BF16)
        cb = _nt(cm, bm)

        cp = _nt(cm, prev_b)
        ge = g * eac_x
        dac = _sel_right(ge * cp, et2, 2)
        ge_b = ge.astype(BF16)
        dcm = _nn(ge_b, prev_b)
        dprev = _tn(ge_b, cm)

        colbuf[...] = jnp.zeros_like(colbuf)
        rowbuf[...] = jnp.zeros_like(rowbuf)
        dcb = jnp.zeros((q, q), F32)
        half = _head_of(lax.broadcasted_iota(jnp.int32, (q, LANES), 1))
        pairs = []
        for j in range(nh):
            pc = (j // 2) * LANES
            dec = _decay(acum, acum_t, j, row, col)
            m = cb * dec
            gj = jnp.where(half == j % 2, g[:, pc:pc + LANES], 0.0).astype(BF16)
            dm = _nt(gj, xdt_b[:, pc:pc + LANES])
            w = dm * m
            colbuf[:, j:j + 1] = jnp.sum(w, axis=1, keepdims=True)
            rowbuf[j:j + 1, :] = jnp.sum(w, axis=0, keepdims=True)
            dcb = dcb + dm * dec
            dj = jnp.where(half == j % 2, _tn(m.astype(BF16), g_b[:, pc:pc + LANES]), 0.0)
            if j % 2 == 0:
                pairs.append(dj)
            else:
                pairs[-1] = pairs[-1] + dj
        dxdt = jnp.concatenate(pairs, axis=1)
        dcb_b = dcb.astype(BF16)
        dcm = dcm + _nn(dcb_b, bm)
        dbm = _tn(dcb_b, cm)

        ds = ds_ref[...]
        ds_b = ds.astype(BF16)
        u = _nt(bm, ds_b)
        dxdt = dxdt + u * de_x
        dde = _sel_right(u * xdt, et2, 2)
        dbm = dbm + _nn((xdt * de_x).astype(BF16), ds_b)
        pm = jnp.concatenate(_split(ds * prev, 2), axis=1)
        t2 = _tn(pm, k["et"])
        dcd_row = jnp.sum(t2[0:D_STATE] + t2[D_STATE:2 * D_STATE], axis=0, keepdims=True)
        last = dcd_row * jnp.exp(ac_last) + jnp.sum(dde * de, axis=0, keepdims=True)
        dac = dac + colbuf[...] - rowbuf[...].T - dde * de + jnp.where(row == q - 1, last, 0.0)
        ds_ref[...] = cdmat * ds + dprev

        dadt = _sel_left(k["triu"], dac)
        ddt = _sel_right(dxdt * x, et2, 2) + dadt * a_row
        dal = jnp.sum(dadt * dt, axis=0, keepdims=True) * a_row
        lane = lax.broadcasted_iota(jnp.int32, (q, LANES), 1)
        ddtr = jnp.where(lane < nh, ddt * _sigmoid(dtr), 0.0)
        ddt_ref[...] = ddtr.astype(BF16)
        dbias = jnp.sum(ddtr, axis=0, keepdims=True)
        dx_ref[...] = dxdt * dt_x + dsk_ref[0:1, :] * g
        db_ref[...] = dbm
        dc_ref[...] = dcm
        return dnw, dd, dal, dbias

    rc = lambda c: nc - 1 - c
    par = lambda w: pl.BlockSpec((None, 8, w), lambda g, i, c: (g, 0, 0))
    blk = lambda w: pl.BlockSpec((nb, q, w), lambda g, i, c: (i, rc(c), g))
    return pl.pallas_call(
        body,
        out_shape=(jax.ShapeDtypeStruct((b, seq, zx.shape[2]), BF16),
                   jax.ShapeDtypeStruct((b, seq, d_inner), F32),
                   jax.ShapeDtypeStruct((b, seq, N_GROUPS * D_STATE), F32),
                   jax.ShapeDtypeStruct((b, seq, N_GROUPS * D_STATE), F32),
                   jax.ShapeDtypeStruct((b, seq, N_GROUPS * LANES), BF16),
                   jax.ShapeDtypeStruct((N_GROUPS, 8, gw), F32),
                   jax.ShapeDtypeStruct((N_GROUPS, 8, LANES), F32),
                   jax.ShapeDtypeStruct((N_GROUPS, 8, LANES), F32),
                   jax.ShapeDtypeStruct((N_GROUPS, 8, LANES), F32)),
        grid=(N_GROUPS, b // nb, nc),
        in_specs=[blk(gw),
                  pl.BlockSpec((nb, q, D_STATE), lambda g, i, c: (i, rc(c), xb0 + g)),
                  pl.BlockSpec((nb, q, D_STATE), lambda g, i, c: (i, rc(c), xc0 + g)),
                  blk(gw),
                  pl.BlockSpec((nb, q, LANES), lambda g, i, c: (i, rc(c), g)),
                  blk(gw), blk(gw),
                  pl.BlockSpec((nb, None, None, gw, D_STATE), lambda g, i, c: (i, rc(c), g, 0, 0)),
                  par(LANES), par(LANES), par(gw), par(gw)],
        out_specs=(blk(gw), blk(gw), blk(D_STATE), blk(D_STATE), blk(LANES),
                   par(gw), par(LANES), par(LANES), par(LANES)),
        scratch_shapes=[pltpu.VMEM((nb, gw, D_STATE), F32), pltpu.VMEM((nb, q, LANES), F32),
                        pltpu.VMEM((nb, LANES, q), F32)],
        compiler_params=_params(("parallel", "arbitrary", "arbitrary")), name=name,
    )(xc, xc, xc, zx, dtr, y, dyn, st, dtb, alog, dskip, nw)


def _adamw(w, g, m, v, name):
    rows, cols = w.shape
    tr = rows
    for cand in (512, 256, 128, 64, 32, 16, 8):
        if rows % cand == 0 and cand * cols * 4 <= 2 * 1024 * 1024:
            tr = cand
            break
    c1 = 1.0 - ADAM_B1 ** ADAM_STEP
    c2 = 1.0 - ADAM_B2 ** ADAM_STEP

    def body(w_ref, g_ref, m_ref, v_ref, d_ref, mo_ref, vo_ref):
        gv = g_ref[...]
        mn = ADAM_B1 * m_ref[...] + (1.0 - ADAM_B1) * gv
        vn = ADAM_B2 * v_ref[...] + (1.0 - ADAM_B2) * (gv * gv)
        mo_ref[...] = mn
        vo_ref[...] = vn
        d_ref[...] = -ADAM_LR * ((mn / c1) / (jnp.sqrt(vn / c2) + ADAM_EPS) + ADAM_WD * w_ref[...])

    spec = pl.BlockSpec((tr, cols), lambda i: (i, 0))
    shp = jax.ShapeDtypeStruct((rows, cols), F32)
    return pl.pallas_call(body, out_shape=(shp, shp, shp), grid=(rows // tr,), in_specs=[spec] * 4,
                          out_specs=(spec,) * 3, compiler_params=_params(("parallel",)), name=name)(w, g, m, v)


def _pick_rows(rows, row_bytes, limit=1 << 20):
    for cand in (2048, 1024, 512, 256, 128, 64, 32, 16):
        if rows % cand == 0 and cand * row_bytes <= limit:
            return cand
    return rows


def _as3d(a, lead):
    return a.reshape(a.shape[:lead] + (-1, a.shape[-1]))


def _pair_sum(g, got, core, name):
    h = got.shape[0]
    g3, got3 = _as3d(g, 1), _as3d(got, 1)
    _, rows, cols = got3.shape
    tr = _pick_rows(rows, cols * 4)

    def body(c_ref, g_ref, r_ref, o_ref):
        o_ref[...] = (g_ref[...] + r_ref[...]).astype(BF16)

    out = pl.pallas_call(
        body, out_shape=jax.ShapeDtypeStruct(got3.shape, BF16),
        grid_spec=pltpu.PrefetchScalarGridSpec(
            num_scalar_prefetch=1, grid=(h, rows // tr),
            in_specs=[pl.BlockSpec((None, tr, cols), lambda l, i, c_ref: (c_ref[0] * h + l, i, 0)),
                      pl.BlockSpec((None, tr, cols), lambda l, i, c_ref: (l, i, 0))],
            out_specs=pl.BlockSpec((None, tr, cols), lambda l, i, c_ref: (l, i, 0))),
        compiler_params=_params(("parallel", "parallel")), name=name)(core, g3, got3)
    return out.reshape(got.shape)


def _sum4(q, core, name):
    q4 = _as3d(q, 2)
    _, h, rows, cols = q4.shape
    tr = _pick_rows(rows, cols * 4)

    def body(c_ref, q0, q1, q2, q3, o_ref):
        o_ref[...] = ((q0[...].astype(F32) + q1[...].astype(F32)) + q2[...].astype(F32)) + q3[...].astype(F32)

    out = pl.pallas_call(
        body, out_shape=jax.ShapeDtypeStruct((2 * h, rows, cols), F32),
        grid_spec=pltpu.PrefetchScalarGridSpec(
            num_scalar_prefetch=1, grid=(h, rows // tr),
            in_specs=[pl.BlockSpec((None, None, tr, cols), lambda l, i, c_ref, k=k: (k, l, i, 0))
                      for k in range(N_CHIPS)],
            out_specs=pl.BlockSpec((None, tr, cols), lambda l, i, c_ref: (c_ref[0] * h + l, i, 0))),
        compiler_params=_params(("parallel", "parallel")), name=name)(core, q4, q4, q4, q4)
    return out.reshape((2 * h,) + q.shape[2:])


def _coords():
    return lax.axis_index("x"), lax.axis_index("y"), lax.axis_index("c")


def _other_chips(x, y):
    return [(1 - x, y), (x, 1 - y), (1 - x, 1 - y)]


def _allgather_halves(src, name):
    rows, cols = src.shape

    def body(x_ref, o_ref, send, recv, local):
        x, y, c = _coords()
        sib = (x, y, 1 - c)
        chips = _other_chips(x, y)

        def slot(h, cx, cy):
            return o_ref.at[h, 2 * cx + cy]

        def copy(kk, dst, to, src_ref):
            return pltpu.make_async_remote_copy(src_ref=src_ref, dst_ref=dst, send_sem=send.at[kk],
                                                recv_sem=recv.at[kk], device_id=to, device_id_type=MESH)

        mine = pltpu.make_async_copy(x_ref, slot(c, x, y), local)
        mine.start()
        first = [copy(0, slot(c, x, y), sib, x_ref)]
        first += [copy(1 + j, slot(c, x, y), (*chip, c), x_ref) for j, chip in enumerate(chips)]
        for cp in first:
            cp.start()
        passed = [copy(4 + j, slot(c, *chip), sib, slot(c, *chip)) for j, chip in enumerate(chips)]
        for j, chip in enumerate(chips):
            copy(1 + j, slot(c, *chip), (x, y, c), x_ref).wait_recv()
            passed[j].start()
        copy(0, slot(1 - c, x, y), (x, y, c), x_ref).wait_recv()
        for j, chip in enumerate(chips):
            copy(4 + j, slot(1 - c, *chip), (x, y, c), x_ref).wait_recv()
        for cp in first + passed:
            cp.wait_send()
        mine.wait()

    return pl.pallas_call(
        body, out_shape=jax.ShapeDtypeStruct((2, N_CHIPS, rows, cols), src.dtype),
        in_specs=[ANY], out_specs=ANY,
        scratch_shapes=[pltpu.SemaphoreType.DMA((7,)), pltpu.SemaphoreType.DMA((7,)), pltpu.SemaphoreType.DMA],
        name=name)(src)


MIXW = (("ssd_w_in", None), ("ssd_w_out", 0), ("pool_w", 1))
FFNW = (("ffn_w_up", 1), ("ffn_w_down", 0))


def _chip_window(axis, ref, layers, k):
    if axis is None:
        return ref.at[layers, k]
    n = ref.shape[1 + axis] // N_CHIPS
    sl = pl.ds(pl.multiple_of(k * n, LANES if 1 + axis == len(ref.shape) - 1 else 8), n)
    idx = [layers] + [slice(None)] * (len(ref.shape) - 1)
    idx[1 + axis] = sl
    return ref.at[tuple(idx)]


def _full_shape(axis, shard_shape):
    if axis is None:
        return (shard_shape[0], N_CHIPS) + tuple(shard_shape[1:])
    full = list(shard_shape)
    full[1 + axis] *= N_CHIPS
    return tuple(full)


HBM_SPEC = pl.BlockSpec(memory_space=pltpu.HBM)
SEM_SPEC = pl.BlockSpec(memory_space=pltpu.SEMAPHORE)


def _dma_sems(count):
    return pltpu.SemaphoreType.DMA((max(count, 1),))


def _wait_for(copy, kind):
    if kind == "recv":
        copy.wait_recv()
    elif kind == "send":
        copy.wait_send()
    else:
        copy.wait()


def _comm_fused(stages, counts, srcs, lands, name, inplace=False):
    ns, nl, k = len(srcs), len(lands), len(stages)

    def body(*refs):
        src_refs = refs[:ns]
        land_refs = refs[ns + (nl if inplace else 0):ns + (nl if inplace else 0) + nl]
        sem_refs = refs[len(refs) - 3 * k:]
        for s, stage_fn in enumerate(stages):
            starts, waits = stage_fn(src_refs, land_refs, tuple(sem_refs[3 * s:3 * s + 3]))
            for cp in starts:
                cp.start()
            for cp, kind in waits:
                _wait_for(cp, kind)

    scratch = []
    for cnt in counts:
        scratch += [_dma_sems(c) for c in cnt]
    outs = pl.pallas_call(
        body, out_shape=tuple(jax.ShapeDtypeStruct(a.shape, a.dtype) for a in lands),
        in_specs=[ANY] * (ns + (nl if inplace else 0)), out_specs=(ANY,) * nl,
        input_output_aliases={ns + i: i for i in range(nl)} if inplace else {},
        scratch_shapes=scratch, name=name)(*srcs, *(lands if inplace else ()))
    return list(outs)


class _SplitComm:
    def __init__(self, stages, counts, srcs, lands, name):
        self.stages, self.counts, self.name = stages, counts, name
        self.ns = len(srcs)
        self.data = [pltpu.with_memory_space_constraint(a, pltpu.HBM) for a in list(srcs) + list(lands)]
        self.sems = None
        self.step = 0

    def advance(self, after=None):
        i, k, nd, ns = self.step, len(self.stages), len(self.data), self.ns
        first, last = i == 0, i == k
        stages = self.stages
        after = list(after) if isinstance(after, (list, tuple)) else [after]

        def body(*refs):
            data = refs[:nd]
            pos = nd
            if not first:
                old = tuple(refs[pos:pos + 3])
                pos += 3 + len(after)
            if not last:
                new = tuple(refs[pos:pos + 3])
            if not first:
                for cp, kind in stages[i - 1](data[:ns], data[ns:], old)[1]:
                    _wait_for(cp, kind)
            if not last:
                for cp in stages[i](data[:ns], data[ns:], new)[0]:
                    cp.start()
                refs[len(refs) - 1][...] = jnp.zeros((8, LANES), F32)

        args = list(self.data)
        in_specs = [HBM_SPEC] * nd
        if not first:
            args += list(self.sems) + after
            in_specs += [SEM_SPEC] * 3 + [ANY] * len(after)
        out_shape, out_specs = [], []
        if not last:
            out_shape += [_dma_sems(c) for c in self.counts[i]]
            out_specs += [SEM_SPEC] * 3
        out_shape += [pltpu.HBM(a.shape, a.dtype) for a in self.data]
        out_specs += [HBM_SPEC] * nd
        if not last:
            out_shape.append(jax.ShapeDtypeStruct((8, LANES), F32))
            out_specs.append(pl.BlockSpec(memory_space=pltpu.VMEM))
        off = 0 if last else 3
        outs = pl.pallas_call(
            body, out_shape=tuple(out_shape), in_specs=in_specs, out_specs=tuple(out_specs),
            input_output_aliases={d: off + d for d in range(nd)},
            compiler_params=pltpu.CompilerParams(has_side_effects=pltpu.SideEffectType.DATAFLOW_SIDE_EFFECTING),
            name=f"{self.name}_{i}")(*args)
        self.sems = None if last else outs[:3]
        self.data = list(outs[off:off + nd])
        self.step += 1
        return None if last else outs[len(outs) - 1]

    def lands(self):
        return self.data[self.ns:]


def _gather_stages(spec):
    n = len(spec)

    def parts(srcs, lands):
        x, y, c = _coords()
        out = []
        for w, (_, axis) in enumerate(spec):
            h = srcs[w].shape[0] // 2
            mine, theirs = pl.ds(c * h, h), pl.ds((1 - c) * h, h)
            out.append((srcs[w].at[mine], lambda layers, k, w=w, axis=axis: _chip_window(axis, lands[w], layers, k),
                        mine, theirs))
        return x, y, c, 2 * x + y, (x, y, 1 - c), _other_chips(x, y), out

    def remote(src, dst, send, recv, idx, to):
        return pltpu.make_async_remote_copy(src_ref=src, dst_ref=dst, send_sem=send.at[idx], recv_sem=recv.at[idx],
                                            device_id=to, device_id_type=MESH)

    def stage0(srcs, lands, sems):
        send, recv, local = sems
        x, y, c, me, sib, chips, ps = parts(srcs, lands)
        starts, waits = [], []
        for w, (src, dst, mine, theirs) in enumerate(ps):
            lc = pltpu.make_async_copy(src, dst(mine, me), local.at[w])
            first = [remote(src, dst(mine, me), send, recv, 4 * w, sib)]
            first += [remote(src, dst(mine, me), send, recv, 4 * w + 1 + j, (cx, cy, c)) for j, (cx, cy) in enumerate(chips)]
            starts += [lc] + first
            waits.append((remote(src, dst(theirs, me), send, recv, 4 * w, (x, y, c)), "recv"))
            waits += [(remote(src, dst(mine, 2 * cx + cy), send, recv, 4 * w + 1 + j, (x, y, c)), "recv")
                      for j, (cx, cy) in enumerate(chips)]
            waits += [(cp, "send") for cp in first] + [(lc, "local")]
        return starts, waits

    def stage1(srcs, lands, sems):
        send, recv, _ = sems
        x, y, c, me, sib, chips, ps = parts(srcs, lands)
        starts, waits = [], []
        for w, (src, dst, mine, theirs) in enumerate(ps):
            for j, (cx, cy) in enumerate(chips):
                blk = dst(mine, 2 * cx + cy)
                fwd = remote(blk, blk, send, recv, 3 * w + j, sib)
                starts.append(fwd)
                waits.append((remote(src, dst(theirs, 2 * cx + cy), send, recv, 3 * w + j, (x, y, c)), "recv"))
                waits.append((fwd, "send"))
        return starts, waits

    return [stage0, stage1], [(4 * n, 4 * n, n), (3 * n, 3 * n, 0)]


def _swap_stages(spec):
    n = len(spec)

    def stage(srcs, lands, sems):
        send, recv, _ = sems
        x, y, c = _coords()
        starts, waits = [], []
        for w in range(n):
            h = srcs[w].shape[0] // 2
            cp = pltpu.make_async_remote_copy(src_ref=srcs[w].at[pl.ds((1 - c) * h, h)], dst_ref=lands[w],
                                              send_sem=send.at[w], recv_sem=recv.at[w],
                                              device_id=(x, y, 1 - c), device_id_type=MESH)
            starts.append(cp)
            waits += [(cp, "recv"), (cp, "send")]
        return starts, waits

    return [stage], [(n, n, 0)]


def _scatter_stages(spec):
    n = len(spec)

    def stage(srcs, lands, sems):
        send, recv, local = sems
        x, y, c = _coords()
        me = 2 * x + y
        starts, waits = [], []
        for w, (_, axis) in enumerate(spec):
            layers = pl.ds(0, srcs[w].shape[0])
            own = _chip_window(axis, srcs[w], layers, me)
            lc = pltpu.make_async_copy(own, lands[w].at[me], local.at[w])
            starts.append(lc)
            for j, (cx, cy) in enumerate(_other_chips(x, y)):
                cp = pltpu.make_async_remote_copy(src_ref=_chip_window(axis, srcs[w], layers, 2 * cx + cy),
                                                  dst_ref=lands[w].at[me], send_sem=send.at[3 * w + j],
                                                  recv_sem=recv.at[3 * w + j], device_id=(cx, cy, c), device_id_type=MESH)
                starts.append(cp)
                waits.append((pltpu.make_async_remote_copy(
                    src_ref=own, dst_ref=lands[w].at[2 * cx + cy], send_sem=send.at[3 * w + j], recv_sem=recv.at[3 * w + j],
                    device_id=(x, y, c), device_id_type=MESH), "recv"))
                waits.append((cp, "send"))
            waits.append((lc, "local"))
        return starts, waits

    return [stage], [(3 * n, 3 * n, n)]


def _share_stages(spec):
    n = len(spec)

    def stage(srcs, lands, sems):
        send, recv, _ = sems
        x, y, c = _coords()
        starts, waits = [], []
        for w in range(n):
            h = lands[w].shape[0] // 2
            mine, theirs = lands[w].at[pl.ds(c * h, h)], lands[w].at[pl.ds((1 - c) * h, h)]
            cp = pltpu.make_async_remote_copy(src_ref=mine, dst_ref=mine, send_sem=send.at[w], recv_sem=recv.at[w],
                                              device_id=(x, y, 1 - c), device_id_type=MESH)
            starts.append(cp)
            waits.append((pltpu.make_async_remote_copy(src_ref=theirs, dst_ref=theirs, send_sem=send.at[w],
                                                       recv_sem=recv.at[w], device_id=(x, y, c), device_id_type=MESH),
                          "recv"))
            waits.append((cp, "send"))
        return starts, waits

    return [stage], [(n, n, 0)]


def _shard_of(p, axis):
    if axis is None:
        return (p.shape[0],) + tuple(p.shape[2:])
    s = list(p.shape)
    s[1 + axis] //= N_CHIPS
    return tuple(s)


def _gather8_stages():
    def stage(srcs, lands, sems):
        send, recv, local = sems
        x, y, c = _coords()
        me = 4 * x + 2 * y + c
        lc = pltpu.make_async_copy(srcs[0], lands[0].at[me], local.at[0])
        starts, waits = [lc], []
        for kk in range(1, 8):
            to = (1 - x if kk & 4 else x, 1 - y if kk & 2 else y, 1 - c if kk & 1 else c)
            cp = pltpu.make_async_remote_copy(src_ref=srcs[0], dst_ref=lands[0].at[me], send_sem=send.at[kk - 1],
                                              recv_sem=recv.at[kk - 1], device_id=to, device_id_type=MESH)
            starts.append(cp)
            waits.append((pltpu.make_async_remote_copy(
                src_ref=srcs[0], dst_ref=lands[0].at[4 * to[0] + 2 * to[1] + to[2]], send_sem=send.at[kk - 1],
                recv_sem=recv.at[kk - 1], device_id=(x, y, c), device_id_type=MESH), "recv"))
            waits.append((cp, "send"))
        waits.append((lc, "local"))
        return starts, waits

    return [stage], [(7, 7, 1)]


def _sum8(buf, name):
    _, rows, cols = buf.shape
    tr = _pick_rows(rows, cols * 4)

    def body(*refs):
        acc = refs[0][...]
        for r in refs[1:8]:
            acc = acc + r[...]
        refs[8][...] = acc

    return pl.pallas_call(
        body, out_shape=jax.ShapeDtypeStruct((rows, cols), F32), grid=(rows // tr,),
        in_specs=[pl.BlockSpec((None, tr, cols), lambda i, k=k: (k, i, 0)) for k in range(8)],
        out_specs=pl.BlockSpec((tr, cols), lambda i: (i, 0)),
        compiler_params=_params(("parallel",)), name=name)(*([buf] * 8))


def _reduce_begin(spec, gs, core, tag, riders=(), during=None):
    stages, counts = _swap_stages(spec)
    got_shapes = [jax.ShapeDtypeStruct((g.shape[0] // 2,) + g.shape[1:], g.dtype) for g in gs]
    if during is None:
        got = _comm_fused(stages, counts, list(gs) + list(riders), got_shapes, "swap_" + tag)
    else:
        swap = _SplitComm(stages, counts, list(gs) + list(riders), [lax.empty(s.shape, s.dtype) for s in got_shapes],
                          "swap_" + tag)
        swap.advance(after=during(swap.advance()))
        gs, got = swap.data[:len(gs)], swap.lands()
    pair = [_pair_sum(a, r, core, "pair_sum_" + n) for a, r, (n, _) in zip(gs, got, spec)]
    stages, counts = _scatter_stages(spec)
    lands = [lax.empty((N_CHIPS,) + _shard_of(p, axis), p.dtype) for p, (_, axis) in zip(pair, spec)]
    comm = _SplitComm(stages, counts, pair, lands, "scatter_" + tag)
    return comm, comm.advance()


def _reduce_finish(spec, comm, core, tag, after):
    comm.advance(after=after)
    halves = [_sum4(q, core, "sum4_" + n) for q, (n, _) in zip(comm.lands(), spec)]
    stages, counts = _share_stages(spec)
    return _comm_fused(stages, counts, [], halves, "share_" + tag, inplace=True)


SMALL = (("ssd_conv_w", 2), ("pool_scale", 1), ("ffn_conv_w", 2))
REPL = ("ssd_conv_b", "ssd_dt_bias", "ssd_a_log", "ssd_d", "ssd_norm_w", "ffn_conv_b",
        "norm_mix_pre", "norm_mix_post", "norm_ffn_pre", "norm_ffn_post")
WEIGHTS = ("ssd_w_in", "ssd_conv_w", "ssd_conv_b", "ssd_dt_bias", "ssd_a_log", "ssd_d", "ssd_norm_w", "ssd_w_out",
           "pool_w", "pool_scale", "ffn_w_up", "ffn_conv_w", "ffn_conv_b", "ffn_w_down", "norm_mix_pre",
           "norm_mix_post", "norm_ffn_pre", "norm_ffn_post")


def _flat_rows(n):
    unit = 2 * 16 * FLAT_COLS
    return 2 * 16 * ((n + unit - 1) // unit)


def _flatten_shards(arrs, dtype):
    flat = jnp.concatenate([a.astype(dtype).reshape(-1) for a in arrs])
    rows = _flat_rows(flat.shape[0])
    flat = jnp.pad(flat, (0, rows * FLAT_COLS - flat.shape[0]))
    return flat.reshape(2, rows // 2, FLAT_COLS)


def _unflatten_full(gathered, shard_shapes, axes):
    per_chip = jnp.swapaxes(gathered, 0, 1).reshape(N_CHIPS, -1)
    out, off = [], 0
    for shp, ax in zip(shard_shapes, axes):
        n = math.prod(shp)
        pieces = [per_chip[k, off:off + n].reshape(shp) for k in range(N_CHIPS)]
        out.append(jnp.concatenate(pieces, axis=ax))
        off += n
    return out


def kernel(x, ssd_w_in, ssd_conv_w, ssd_conv_b, ssd_dt_bias, ssd_a_log, ssd_d, ssd_norm_w, ssd_w_out, pool_w, pool_scale, ffn_w_up, ffn_conv_w, ffn_conv_b, ffn_w_down, norm_mix_pre, norm_mix_post, norm_ffn_pre, norm_ffn_post, loss_target, m_ssd_w_in, m_ssd_conv_w, m_ssd_conv_b, m_ssd_dt_bias, m_ssd_a_log, m_ssd_d, m_ssd_norm_w, m_ssd_w_out, m_pool_w, m_pool_scale, m_ffn_w_up, m_ffn_conv_w, m_ffn_conv_b, m_ffn_w_down, m_norm_mix_pre, m_norm_mix_post, m_norm_ffn_pre, m_norm_ffn_post, v_ssd_w_in, v_ssd_conv_w, v_ssd_conv_b, v_ssd_dt_bias, v_ssd_a_log, v_ssd_d, v_ssd_norm_w, v_ssd_w_out, v_pool_w, v_pool_scale, v_ffn_w_up, v_ffn_conv_w, v_ffn_conv_b, v_ffn_w_down, v_norm_mix_pre, v_norm_mix_post, v_norm_ffn_pre, v_norm_ffn_post):
    wts = dict(ssd_w_in=ssd_w_in, ssd_conv_w=ssd_conv_w, ssd_conv_b=ssd_conv_b, ssd_dt_bias=ssd_dt_bias,
               ssd_a_log=ssd_a_log, ssd_d=ssd_d, ssd_norm_w=ssd_norm_w, ssd_w_out=ssd_w_out, pool_w=pool_w,
               pool_scale=pool_scale, ffn_w_up=ffn_w_up, ffn_conv_w=ffn_conv_w, ffn_conv_b=ffn_conv_b,
               ffn_w_down=ffn_w_down, norm_mix_pre=norm_mix_pre, norm_mix_post=norm_mix_post,
               norm_ffn_pre=norm_ffn_pre, norm_ffn_post=norm_ffn_post)
    mom = dict(ssd_w_in=m_ssd_w_in, ssd_conv_w=m_ssd_conv_w, ssd_conv_b=m_ssd_conv_b, ssd_dt_bias=m_ssd_dt_bias,
               ssd_a_log=m_ssd_a_log, ssd_d=m_ssd_d, ssd_norm_w=m_ssd_norm_w, ssd_w_out=m_ssd_w_out, pool_w=m_pool_w,
               pool_scale=m_pool_scale, ffn_w_up=m_ffn_w_up, ffn_conv_w=m_ffn_conv_w, ffn_conv_b=m_ffn_conv_b,
               ffn_w_down=m_ffn_w_down, norm_mix_pre=m_norm_mix_pre, norm_mix_post=m_norm_mix_post,
               norm_ffn_pre=m_norm_ffn_pre, norm_ffn_post=m_norm_ffn_post)
    var = dict(ssd_w_in=v_ssd_w_in, ssd_conv_w=v_ssd_conv_w, ssd_conv_b=v_ssd_conv_b, ssd_dt_bias=v_ssd_dt_bias,
               ssd_a_log=v_ssd_a_log, ssd_d=v_ssd_d, ssd_norm_w=v_ssd_norm_w, ssd_w_out=v_ssd_w_out, pool_w=v_pool_w,
               pool_scale=v_pool_scale, ffn_w_up=v_ffn_w_up, ffn_conv_w=v_ffn_conv_w, ffn_conv_b=v_ffn_conv_b,
               ffn_w_down=v_ffn_w_down, norm_mix_pre=v_norm_mix_pre, norm_mix_post=v_norm_mix_post,
               norm_ffn_pre=v_norm_ffn_pre, norm_ffn_post=v_norm_ffn_post)

    bl, seq, d = x.shape
    t = bl * seq
    depth = norm_mix_pre.shape[0]
    n_ssd = ssd_w_out.shape[0]
    d_inner = ssd_w_out.shape[1] * N_CHIPS
    nheads = d_inner // HEAD_DIM
    hpg = nheads // N_GROUPS
    gw = d_inner // N_GROUPS
    xbc = ssd_conv_w.shape[2] * N_CHIPS
    f2 = ffn_w_up.shape[2] * N_CHIPS
    ff = f2 // 2
    dg = d // 4
    cy = lax.axis_index("c")
    chip = 2 * lax.axis_index("x") + lax.axis_index("y")

    small_shapes = [wts[n].shape for n, _ in SMALL]
    small_axes = [a for _, a in SMALL]
    small_flat = _flatten_shards([wts[n] for n, _ in SMALL], F32)
    small_half = lax.dynamic_index_in_dim(small_flat, cy, 0, keepdims=False)
    small_all = _allgather_halves(small_half, "gather_small")
    conv_w, p_scale, f_conv_w = _unflatten_full(small_all, small_shapes, small_axes)
    def full_shapes(spec, shards):
        return [jax.ShapeDtypeStruct(_full_shape(axis, s.shape), s.dtype) for s, (_, axis) in zip(shards, spec)]

    def row_halves(a):
        return a.reshape((2, a.shape[0] // 2) + a.shape[1:])

    def join_w_in(g):
        return jnp.concatenate([g[:, k] for k in range(N_CHIPS)], axis=-1).reshape(d, -1)

    def join_w_out(g):
        r2 = g.shape[1] // N_CHIPS
        return jnp.concatenate([g[hf, k * r2:(k + 1) * r2] for k in range(N_CHIPS) for hf in range(2)], axis=0)

    ssd_spec = (("ssd_w_in", None), ("ssd_w_out", 0))
    first_shards = [row_halves(wts[n][0].astype(BF16)) for n, _ in ssd_spec]
    stages, counts = _gather_stages(ssd_spec)
    g_in0, g_out0 = _comm_fused(stages, counts, first_shards, full_shapes(ssd_spec, first_shards), "gather_first")
    w_in, w_out = [join_w_in(g_in0)], [join_w_out(g_out0)]
    rest_spec = ssd_spec * (n_ssd - 1) + (("pool_w", 1),) + FFNW
    rest_shards = [row_halves(wts[n][jj].astype(BF16)) for jj in range(1, n_ssd) for n, _ in ssd_spec]
    rest_shards += [wts["pool_w"].astype(BF16)] + [wts[n].astype(BF16) for n, _ in FFNW]
    stages, counts = _gather_stages(rest_spec)
    ffn_gather = _SplitComm(stages, counts, rest_shards + [g_out0],
                            [lax.empty(s.shape, s.dtype) for s in full_shapes(rest_spec, rest_shards)], "gather_rest")
    gather_token = ffn_gather.advance()

    def pad_heads(a):
        lead = a.shape[:-1]
        a = a.reshape(lead + (N_GROUPS, hpg))
        a = jnp.pad(a, [(0, 0)] * len(lead) + [(0, 0), (0, LANES - hpg)])
        return a.reshape(lead + (N_GROUPS * LANES,))

    def unpad_heads(a):
        lead = a.shape[:-1]
        return a.reshape(lead + (N_GROUPS, LANES))[..., :hpg].reshape(lead + (nheads,))

    def group_rows(a, width):
        return jnp.broadcast_to(a.reshape(N_GROUPS, 1, width), (N_GROUPS, 8, width))

    def pad_w_in(w):
        return jnp.concatenate([w[..., :d_inner + xbc], pad_heads(w[..., d_inner + xbc:])], axis=-1)

    w_in_p = [pad_w_in(w_in[0])]
    zw = w_in_p[0].shape[-1]
    w_pool = None

    x2 = x.reshape(t, d)
    tgt2 = loss_target.reshape(t, d)
    w_up = w_down = None

    saved = []
    cur = x2
    tokens = []
    h = _norm_fwd(cur, norm_mix_pre[0:1], BF16, "norm_pre_b", after=[gather_token])
    for i in range(depth):
        j = i // 2
        sv = dict(x_in=cur)
        if i % 2 == 0:
            zx = _mm(h, w_in_p[j], "nn", BF16, "mm_ssd_in", 2048, 512, d).reshape(bl, seq, zw)
            dtr = _mm(h, w_in_p[j][:, d_inner + xbc:], "nn", F32, "mm_ssd_dt", 2048, 512, d).reshape(bl, seq, -1)
            xc, xpre = _ssd_conv_fwd(zx, conv_w[j], ssd_conv_b[j:j + 1], d_inner, "ssd_conv_fwd")
            dtb = group_rows(pad_heads(ssd_dt_bias[j]), LANES)
            alog = group_rows(pad_heads(ssd_a_log[j]), LANES)
            dskip = group_rows(jnp.repeat(ssd_d[j], HEAD_DIM), gw)
            nw = group_rows(ssd_norm_w[j], gw)
            y, yn, st = _ssd_fwd(xc, zx, dtr, dtb, alog, dskip, nw, d_inner, "ssd_fwd")
            if i == 0:
                tokens.append(ffn_gather.advance(after=yn))
            mix = _mm(yn.reshape(t, d_inner), w_out[j], "nn", F32, "mm_ssd_out", 2048, 512, d_inner)
            sv.update(h=h, zx=zx, dtr=dtr, xc=xc, xpre=xpre, y=y, yn=yn, st=st, dtb=dtb, alog=alog, dskip=dskip, nw=nw)
        else:
            mix = _pool_fwd(h.reshape(bl, seq, d), w_pool[j], p_scale[j:j + 1], "pool_fwd").reshape(t, d)
            sv.update(h=h)
        sv.update(mix=mix)
        mid, u = _norm_post_pre(mix, norm_mix_post[i:i + 1], cur, norm_ffn_pre[i:i + 1], BF16, "norm_post_pre_b",
                                after=tokens)
        tokens = []
        if i == 0:
            ffn_gather.advance(after=u)
            rest = ffn_gather.lands()
            for jj in range(1, n_ssd):
                w_in_p.append(pad_w_in(join_w_in(rest[2 * (jj - 1)])))
                w_out.append(join_w_out(rest[2 * (jj - 1) + 1]))
            w_pool, w_up, w_down = rest[2 * (n_ssd - 1):]
        hpre = _mm(u, w_up, "nn", BF16, "mm_up", 2048, 512, d, b_layer=i).reshape(bl, seq, f2)
        act, pre_g, pre_v = _ffn_act_fwd(hpre, f_conv_w[i], ffn_conv_b[i:i + 1], "ffn_act_fwd")
        act = act.reshape(t, ff)
        fo = _mm(act, w_down, "nn", F32, "mm_down", 2048, 512, ff, b_layer=i)
        if i + 1 == depth:
            cur = _norm_fwd(fo, norm_ffn_post[i:i + 1], F32, "norm_post", resid=mid)
        elif i % 2 == 0:
            cur, h = _norm_post_pre(fo, norm_ffn_post[i:i + 1], mid, norm_mix_pre[i + 1:i + 2], F32, "norm_post_pre_f")
        else:
            cur, h = _norm_post_pre(fo, norm_ffn_post[i:i + 1], mid, norm_mix_pre[i + 1:i + 2], BF16, "norm_post_pre_b")
        sv.update(mid=mid, u=u, hpre=hpre, pre_g=pre_g, pre_v=pre_v, act=act, fo=fo)
        saved.append(sv)

    dcur, loss_part = _loss_head(cur, tgt2, "loss_head")

    g = {n: [None] * wts[n].shape[0] for n in WEIGHTS}
    gbuf = dict(up=lax.empty((depth, d, f2), F32), down=lax.empty((depth, ff, d), F32),
                out=lax.empty((n_ssd, d_inner, d), F32), win=lax.empty((n_ssd, d, zw), F32))
    core = cy.reshape(1).astype(jnp.int32)

    def mixer_bwd(i, dmid, dmix, behind=()):
        j = i // 2
        sv = saved[i]
        done = []
        if i % 2 == 0:
            dyn = _mm(dmix, w_out[j], "nt", BF16, "mm_ssd_out_dx", 1024, 1024, d, after=behind)
            gbuf["out"], tok = _mm(sv["yn"].reshape(t, d_inner), dmix, "tn", F32, "mm_ssd_out_dw", 1024, 1024, 2048,
                                   out_buf=(gbuf["out"], j))
            done.append(tok)
            dz, dxs, dbm, dcm, ddt, dnw, dd, dal, dbias = _ssd_bwd(
                sv["xc"], sv["zx"], sv["dtr"], sv["y"], dyn.reshape(bl, seq, d_inner), sv["st"], sv["dtb"], sv["alog"],
                sv["dskip"], sv["nw"], d_inner, "ssd_bwd")
            g["ssd_norm_w"][j] = dnw[:, 0, :].reshape(d_inner)
            g["ssd_d"][j] = dd[:, 0, :hpg].reshape(nheads)
            g["ssd_a_log"][j] = dal[:, 0, :hpg].reshape(nheads)
            g["ssd_dt_bias"][j] = dbias[:, 0, :hpg].reshape(nheads)
            dzx, dcw, dcb = _ssd_conv_bwd(sv["zx"], sv["xpre"], (dxs, dbm, dcm), ddt, dz, conv_w[j], d_inner,
                                          "ssd_conv_bwd")
            g["ssd_conv_w"][j] = dcw
            g["ssd_conv_b"][j] = dcb[0]
            dzx = dzx.reshape(t, zw)
            dh = _mm(dzx, w_in_p[j], "nt", BF16, "mm_ssd_in_dx", 1024, d, zw // 2)
            gbuf["win"], tok = _mm(sv["h"], dzx, "tn", F32, "mm_ssd_in_dw", 1024, zw // 4, 2048, out_buf=(gbuf["win"], j))
            done.append(tok)
        else:
            dh3, g["pool_w"][j], dps = _pool_bwd(sv["h"].reshape(bl, seq, d), dmix.reshape(bl, seq, d), w_pool[j],
                                                 p_scale[j:j + 1], "pool_bwd")
            g["pool_scale"][j] = dps[0]
            dh = dh3.reshape(t, d)
        if i == 0:
            dx_in, g["norm_mix_pre"][i] = _norm_bwd(sv["x_in"], norm_mix_pre[i:i + 1], dh, F32, "norm_bwd_r", resid=dmid,
                                                    after=done)
            return dx_in, None
        dx_in, dfo_prev, g["norm_mix_pre"][i], g["norm_ffn_post"][i - 1] = _norm_bwd2(
            sv["x_in"], norm_mix_pre[i:i + 1], dh, dmid, saved[i - 1]["fo"], norm_ffn_post[i - 1:i], BF16,
            "norm_bwd_in_post", after=done)
        return dx_in, dfo_prev

    ffn_comm = None
    dfo, g["norm_ffn_post"][depth - 1] = _norm_bwd(saved[depth - 1]["fo"], norm_ffn_post[depth - 1:depth], dcur, BF16,
                                                   "norm_bwd_b")
    for i in reversed(range(depth)):
        sv = saved[i]
        dact = _mm(dfo, w_down, "nt", BF16, "mm_down_dx", 1024, ff // 2, d, b_layer=i)
        gbuf["down"], tok_down = _mm(sv["act"], dfo, "tn", F32, "mm_down_dw", ff // 2, 1024, 2048,
                                     out_buf=(gbuf["down"], i))
        dhg, dhv, dcw, dcb = _ffn_act_bwd(sv["hpre"], sv["pre_g"], sv["pre_v"], dact.reshape(bl, seq, ff), f_conv_w[i],
                                          "ffn_act_bwd")
        g["ffn_conv_w"][i] = dcw
        g["ffn_conv_b"][i] = dcb[0]
        dhs = [dhg.reshape(t, ff), dhv.reshape(t, ff)]
        du = _mm(dhs, w_up, "nt", BF16, "mm_up_dx", 1024, d, ff, b_layer=i)
        gbuf["up"], tok_up = _mm(sv["u"], dhs, "tn", F32, "mm_up_dw", 1024, ff // 2, 2048, out_buf=(gbuf["up"], i))
        def pre_post(behind, i=i, sv=sv, du=du, dcur=dcur):
            return _norm_bwd2(sv["mid"], norm_ffn_pre[i:i + 1], du, dcur, sv["mix"], norm_mix_post[i:i + 1],
                              BF16 if i % 2 == 0 else F32,
                              "norm_bwd_pre_post_b" if i % 2 == 0 else "norm_bwd_pre_post_f", after=behind)

        if i > 0:
            dmid, dmix, g["norm_ffn_pre"][i], g["norm_mix_post"][i] = pre_post([tok_down, tok_up])
            dcur, dfo = mixer_bwd(i, dmid, dmix)
        else:
            held = []

            def during(token):
                held.extend(pre_post([tok_down, tok_up, token]))
                return held[0]

            ffn_comm, ffn_token = _reduce_begin(FFNW, [gbuf["up"], gbuf["down"]], core, "ffn", during=during)
            dmid, dmix, g["norm_ffn_pre"][i], g["norm_mix_post"][i] = held
            dcur, dfo = mixer_bwd(i, dmid, dmix, behind=[ffn_token])

    grad_x = dcur.reshape(bl, seq, d)
    for n in ("norm_mix_pre", "norm_mix_post", "norm_ffn_pre", "norm_ffn_post"):
        g[n] = [a[0] for a in g[n]]
    small_names = [n for n, _ in SMALL] + list(REPL)
    full = {n: jnp.stack(g[n], axis=0) for n in small_names}

    g_in = jnp.concatenate([gbuf["win"][..., :d_inner + xbc], unpad_heads(gbuf["win"][..., d_inner + xbc:])], axis=-1)
    g_in_cm = jnp.swapaxes(g_in.reshape(n_ssd, d, N_CHIPS, -1), 1, 2)
    vec = jnp.concatenate([full[n].reshape(-1) for n in small_names] + [loss_part[0, :1]])
    nvec = vec.shape[0]
    vrows = 16 * ((nvec + 16 * FLAT_COLS - 1) // (16 * FLAT_COLS))
    vec = jnp.pad(vec, (0, vrows * FLAT_COLS - nvec)).reshape(vrows, FLAT_COLS)
    stages, counts = _gather8_stages()
    small_comm = _SplitComm(stages, counts, [vec], [lax.empty((8, vrows, FLAT_COLS), F32)], "gather_small_grads")
    small_token = small_comm.advance()
    mix_comm, mix_token = _reduce_begin(MIXW, [g_in_cm, gbuf["out"], jnp.stack(g["pool_w"], axis=0)], core, "mixers",
                                        riders=[small_token])

    grads, deltas, new_m, new_v = {}, {}, {}, {}

    def adamw(n, gr):
        shp = wts[n].shape
        two = (math.prod(shp[:-1]), shp[-1])
        dl, mn, vn = _adamw(wts[n].reshape(two), gr.reshape(two), mom[n].reshape(two), var[n].reshape(two),
                            "adamw_" + n)
        grads[n], deltas[n], new_m[n], new_v[n] = gr, dl.reshape(shp), mn.reshape(shp), vn.reshape(shp)
        return dl

    small_comm.advance(after=mix_token)
    tot = _sum8(small_comm.lands()[0], "sum_small").reshape(-1)
    small_grads, off = {}, 0
    for n in small_names:
        cnt = math.prod(full[n].shape)
        small_grads[n] = tot[off:off + cnt].reshape(full[n].shape)
        off += cnt
    loss = tot[off]
    for n, ax in SMALL:
        w = wts[n].shape[ax]
        small_grads[n] = lax.dynamic_slice_in_dim(small_grads[n], chip * w, w, axis=ax)

    behind = [adamw(n, small_grads[n]) for n in small_names][-1:]
    ffn_grads = _reduce_finish(FFNW, ffn_comm, core, "ffn", after=mix_token)
    behind += [adamw(n, gr) for gr, (n, _) in zip(ffn_grads, FFNW)]
    mix_grads = _reduce_finish(MIXW, mix_comm, core, "mixers", after=behind)
    for gr, (n, _) in zip(mix_grads, MIXW):
        adamw(n, gr)

    return (loss, grad_x, *[grads[n] for n in WEIGHTS], *[deltas[n] for n in WEIGHTS],
            *[new_m[n] for n in WEIGHTS], *[new_v[n] for n in WEIGHTS])
```

```python
import functools
import math

import jax
import jax.numpy as jnp
from jax import lax
from jax.experimental import pallas as pl
from jax.experimental.pallas import tpu as pltpu

F32 = jnp.float32
BF16 = jnp.bfloat16
MESH = pl.DeviceIdType.MESH
ANY = pl.BlockSpec(memory_space=pl.ANY)

HEAD_DIM = 64
D_STATE = 128
CHUNK = 128
N_GROUPS = 4
SSD_CONV = 4
FFN_CONV = 3
EPS = 1e-6
N_CHIPS = 4
LANES = 128
FLAT_COLS = 1024

ADAM_LR = 0.001
ADAM_B1 = 0.9
ADAM_B2 = 0.999
ADAM_EPS = 1e-08
ADAM_WD = 0.01
ADAM_STEP = 10

VMEM_LIMIT_BYTES = 56 * 1024 * 1024


def _params(sem=None):
    kw = dict(vmem_limit_bytes=VMEM_LIMIT_BYTES)
    if sem is not None:
        kw["dimension_semantics"] = sem
    return pltpu.CompilerParams(**kw)


def _sigmoid(x):
    return 0.5 * jnp.tanh(0.5 * x) + 0.5


def _softplus(x):
    return jnp.maximum(x, 0.0) + jnp.log(1.0 + jnp.exp(-jnp.abs(x)))


def _dot(a, b, dn):
    return lax.dot_general(a, b, (dn, ((), ())), preferred_element_type=F32)


def _nn(a, b):
    return _dot(a, b, ((1,), (0,)))


def _nt(a, b):
    return _dot(a, b, ((1,), (1,)))


def _tn(a, b):
    return _dot(a, b, ((0,), (0,)))


def _split(x, parts):
    out = []
    r = x
    for _ in range(parts):
        p = r.astype(BF16)
        out.append(p)
        r = r - p.astype(F32)
    return out


def _sel_left(sel, x, parts=3):
    n = x.shape[1]
    r = _nn(sel, jnp.concatenate(_split(x, parts), axis=1))
    out = r[:, 0:n]
    for i in range(1, parts):
        out = out + r[:, i * n:(i + 1) * n]
    return out


def _sel_right(x, sel_stacked, parts=3):
    return _nn(jnp.concatenate(_split(x, parts), axis=1), sel_stacked)


def _mm(a, b, dims, out_dtype, name, tm, tn, tk, b_layer=None, out_buf=None, after=()):
    a_list = list(a) if isinstance(a, (list, tuple)) else [a]
    b_list = list(b) if isinstance(b, (list, tuple)) else [b]
    if dims in ("nn", "nt"):
        assert len(b_list) == 1
        m = a_list[0].shape[0]
        segs = [x.shape[1] for x in a_list]
        k = sum(segs)
        bshape = b_list[0].shape[-2:]
        n = bshape[1] if dims == "nn" else bshape[0]
        assert (bshape[0] if dims == "nn" else bshape[1]) == k
    else:
        assert len(a_list) == 1 and b_layer is None
        k, m = a_list[0].shape
        segs = [x.shape[1] for x in b_list]
        n = sum(segs)
    nseg = len(segs)
    tm, tn = min(tm, m), min(tn, n)
    if dims == "tn":
        tk = min(tk, k)
        tn = min(tn, min(segs))
        units = [tn] * nseg
        nk = k // tk
        assert k % tk == 0
    else:
        units = [min(u, s) for u, s in zip(tk if isinstance(tk, (list, tuple)) else [tk] * nseg, segs)]
        nk = sum(s // u for s, u in zip(segs, units))
    assert m % tm == 0 and n % tn == 0 and all(s % u == 0 for s, u in zip(segs, units)), (name, m, n, k, segs, units)
    counts = [s // u for s, u in zip(segs, units)]
    starts = [sum(counts[:s]) for s in range(nseg)]
    assert all(sum(segs[:s]) % units[s] == 0 for s in range(nseg)), (name, segs, units)
    first_block = [sum(segs[:s]) // units[s] for s in range(nseg)]
    dn = {"nn": ((1,), (0,)), "nt": ((1,), (1,)), "tn": ((0,), (0,))}[dims]

    same = len(set(units)) == 1
    nb_ops = len(b_list) if dims == "tn" else (1 if same else nseg)

    def body(*refs):
        a_refs = refs[:len(a_list)]
        b_refs = refs[len(a_list):len(a_list) + nb_ops]
        rest = refs[len(a_list) + nb_ops + (0 if out_buf is None else 1) + len(after):]
        o_ref = rest[0]
        if out_buf is not None:
            rest[1][...] = jnp.zeros((8, LANES), F32)
            rest = rest[1:]
        acc = rest[1] if nk > 1 else None
        kk = pl.program_id(2)
        sel = kk if dims != "tn" else pl.program_id(1)

        def step(a_ref, b_ref):
            p = _dot(a_ref[...].astype(BF16), b_ref[...].astype(BF16), dn)
            if nk == 1:
                o_ref[...] = p.astype(out_dtype)
                return

            @pl.when(kk == 0)
            def _():
                acc[...] = p

            @pl.when(kk > 0)
            def _():
                acc[...] += p

        if nseg == 1:
            step(a_refs[0], b_refs[0])
        else:
            for s in range(nseg):
                @pl.when(jnp.logical_and(sel >= starts[s], sel < starts[s] + counts[s]))
                def _(s=s):
                    step(a_refs[s] if dims != "tn" else a_refs[0], b_refs[s if nb_ops > 1 else 0])

        if nk > 1:
            @pl.when(kk == nk - 1)
            def _():
                o_ref[...] = acc[...].astype(out_dtype)

    def seg_index(v, s):
        return v if nseg == 1 else jnp.clip(v - starts[s], 0, counts[s] - 1)

    lead = () if b_layer is None else (b_layer,)
    none = () if b_layer is None else (None,)
    def b_block(kk, s):
        return kk if same else first_block[s] + seg_index(kk, s)

    if dims == "nn":
        a_specs = [pl.BlockSpec((tm, units[s]), lambda i, j, kk, s=s: (i, seg_index(kk, s))) for s in range(nseg)]
        b_specs = [pl.BlockSpec(none + (units[s], tn), lambda i, j, kk, s=s: lead + (b_block(kk, s), j))
                   for s in range(nb_ops)]
    elif dims == "nt":
        a_specs = [pl.BlockSpec((tm, units[s]), lambda i, j, kk, s=s: (i, seg_index(kk, s))) for s in range(nseg)]
        b_specs = [pl.BlockSpec(none + (tn, units[s]), lambda i, j, kk, s=s: lead + (j, b_block(kk, s)))
                   for s in range(nb_ops)]
    else:
        a_specs = [pl.BlockSpec((tk, tm), lambda i, j, kk: (kk, i))]
        b_specs = [pl.BlockSpec((tk, tn), lambda i, j, kk, s=s: (kk, seg_index(j, s))) for s in range(nseg)]
    args = a_list + (b_list * nb_ops if dims != "tn" else b_list)
    in_specs = a_specs + b_specs
    aliases = {}
    if out_buf is None:
        out_shape = jax.ShapeDtypeStruct((m, n), out_dtype)
        out_spec = pl.BlockSpec((tm, tn), lambda i, j, kk: (i, j))
    else:
        buf, slab = out_buf
        assert buf.shape[1:] == (m, n) and buf.dtype == out_dtype
        out_shape = (jax.ShapeDtypeStruct(buf.shape, out_dtype), jax.ShapeDtypeStruct((8, LANES), F32))
        out_spec = (pl.BlockSpec((None, tm, tn), lambda i, j, kk: (slab, i, j)),
                    pl.BlockSpec((8, LANES), lambda i, j, kk: (0, 0)))
        aliases = {len(args): 0}
        args = args + [buf]
        in_specs = in_specs + [ANY]
    after = [x for x in after if x is not None]
    args = args + after
    in_specs = in_specs + [ANY] * len(after)
    return pl.pallas_call(
        body,
        out_shape=out_shape,
        grid=(m // tm, n // tn, nk),
        in_specs=in_specs,
        out_specs=out_spec,
        scratch_shapes=[] if nk == 1 else [pltpu.VMEM((tm, tn), F32)],
        input_output_aliases=aliases,
        compiler_params=_params(("parallel", "parallel", "arbitrary") if out_buf is None else ("arbitrary",) * 3),
        name=name,
    )(*args)


def _row_tile(t, want):
    tm = min(want, t)
    assert t % tm == 0
    return tm


def _norm_fwd(x, w, out_dtype, name, resid=None, after=()):
    t, d = x.shape
    tm = _row_tile(t, 512)
    after = [a for a in after if a is not None]

    def body(*refs):
        refs = refs[:len(refs) - 1 - len(after)] + refs[len(refs) - 1:]
        if resid is None:
            x_ref, w_ref, o_ref = refs
        else:
            x_ref, w_ref, r_ref, o_ref = refs
        xv = x_ref[...]
        r = lax.rsqrt(jnp.mean(xv * xv, axis=-1, keepdims=True) + EPS)
        y = (xv * r) * w_ref[...]
        if resid is not None:
            y = r_ref[...] + y
        o_ref[...] = y.astype(out_dtype)

    row = pl.BlockSpec((tm, d), lambda i: (i, 0))
    vec = pl.BlockSpec((1, d), lambda i: (0, 0))
    args = [x, w] + ([] if resid is None else [resid]) + after
    return pl.pallas_call(
        body, out_shape=jax.ShapeDtypeStruct((t, d), out_dtype), grid=(t // tm,),
        in_specs=[row, vec] + ([] if resid is None else [row]) + [ANY] * len(after), out_specs=row,
        compiler_params=_params(("parallel",)), name=name)(*args)


def _norm_post_pre(m, w_post, resid, w_pre, pre_dtype, name, after=()):
    t, d = m.shape
    tm = _row_tile(t, 512)
    after = [a for a in after if a is not None]

    def body(m_ref, w1_ref, r_ref, w2_ref, *rest):
        x_ref, u_ref = rest[len(after):]
        mv = m_ref[...]
        r1 = lax.rsqrt(jnp.mean(mv * mv, axis=-1, keepdims=True) + EPS)
        xv = r_ref[...] + (mv * r1) * w1_ref[...]
        x_ref[...] = xv
        r2 = lax.rsqrt(jnp.mean(xv * xv, axis=-1, keepdims=True) + EPS)
        u_ref[...] = ((xv * r2) * w2_ref[...]).astype(pre_dtype)

    row = pl.BlockSpec((tm, d), lambda i: (i, 0))
    vec = pl.BlockSpec((1, d), lambda i: (0, 0))
    return pl.pallas_call(
        body, out_shape=(jax.ShapeDtypeStruct((t, d), F32), jax.ShapeDtypeStruct((t, d), pre_dtype)), grid=(t // tm,),
        in_specs=[row, vec, row, vec] + [ANY] * len(after), out_specs=(row, row),
        compiler_params=_params(("parallel",)), name=name)(m, w_post, resid, w_pre, *after)


def _norm_bwd(src, w, dy, out_dtype, name, resid=None, after=()):
    t, d = src.shape
    tm = _row_tile(t, 512)
    after = [a for a in after if a is not None]

    def body(*refs):
        refs = refs[:len(refs) - 2 - len(after)] + refs[len(refs) - 2:]
        if resid is None:
            x_ref, w_ref, g_ref, o_ref, dw_ref = refs
        else:
            x_ref, w_ref, g_ref, r_ref, o_ref, dw_ref = refs
        xv = x_ref[...]
        g = g_ref[...].astype(F32)
        r = lax.rsqrt(jnp.mean(xv * xv, axis=-1, keepdims=True) + EPS)
        xh = xv * r
        gh = g * w_ref[...]
        mean = jnp.mean(gh * xh, axis=-1, keepdims=True)
        dx = r * (gh - xh * mean)
        if resid is not None:
            dx = r_ref[...] + dx
        o_ref[...] = dx.astype(out_dtype)
        part = jnp.sum(g * xh, axis=0, keepdims=True)

        @pl.when(pl.program_id(0) == 0)
        def _():
            dw_ref[...] = part

        @pl.when(pl.program_id(0) > 0)
        def _():
            dw_ref[...] += part

    row = pl.BlockSpec((tm, d), lambda i: (i, 0))
    vec = pl.BlockSpec((1, d), lambda i: (0, 0))
    args = [src, w, dy] + ([] if resid is None else [resid]) + after
    return pl.pallas_call(
        body,
        out_shape=(jax.ShapeDtypeStruct((t, d), out_dtype), jax.ShapeDtypeStruct((1, d), F32)),
        grid=(t // tm,),
        in_specs=[row, vec, row] + ([] if resid is None else [row]) + [ANY] * len(after),
        out_specs=(row, vec),
        compiler_params=_params(("arbitrary",)), name=name)(*args)


def _norm_bwd2(src1, w1, dy1, resid, src2, w2, out2_dtype, name, after=()):
    t, d = src1.shape
    tm = _row_tile(t, 512)
    after = [a for a in after if a is not None]

    def back(xv, w, g):
        r = lax.rsqrt(jnp.mean(xv * xv, axis=-1, keepdims=True) + EPS)
        xh = xv * r
        gh = g * w
        return r * (gh - xh * jnp.mean(gh * xh, axis=-1, keepdims=True)), jnp.sum(g * xh, axis=0, keepdims=True)

    def body(x1_ref, w1_ref, g1_ref, r_ref, x2_ref, w2_ref, *rest):
        d1_ref, d2_ref, dw1_ref, dw2_ref = rest[len(after):]
        d1, p1 = back(x1_ref[...], w1_ref[...], g1_ref[...].astype(F32))
        d1 = r_ref[...] + d1
        d1_ref[...] = d1
        d2, p2 = back(x2_ref[...], w2_ref[...], d1)
        d2_ref[...] = d2.astype(out2_dtype)

        @pl.when(pl.program_id(0) == 0)
        def _():
            dw1_ref[...] = p1
            dw2_ref[...] = p2

        @pl.when(pl.program_id(0) > 0)
        def _():
            dw1_ref[...] += p1
            dw2_ref[...] += p2

    row = pl.BlockSpec((tm, d), lambda i: (i, 0))
    vec = pl.BlockSpec((1, d), lambda i: (0, 0))
    return pl.pallas_call(
        body,
        out_shape=(jax.ShapeDtypeStruct((t, d), F32), jax.ShapeDtypeStruct((t, d), out2_dtype),
                   jax.ShapeDtypeStruct((1, d), F32), jax.ShapeDtypeStruct((1, d), F32)),
        grid=(t // tm,),
        in_specs=[row, vec, row, row, row, vec] + [ANY] * len(after),
        out_specs=(row, row, vec, vec),
        compiler_params=_params(("arbitrary",)), name=name)(src1, w1, dy1, resid, src2, w2, *after)


def _loss_head(y, target, name):
    t, d = y.shape
    tm = _row_tile(t, 512)

    def body(y_ref, t_ref, dy_ref, l_ref):
        e = y_ref[...] - t_ref[...]
        dy_ref[...] = e * (1.0 / d)
        col = jnp.sum(e * e, axis=0, keepdims=True)
        s = jnp.sum(col, axis=1, keepdims=True) * (0.5 / d)
        part = jnp.broadcast_to(s, (1, LANES))

        @pl.when(pl.program_id(0) == 0)
        def _():
            l_ref[...] = part

        @pl.when(pl.program_id(0) > 0)
        def _():
            l_ref[...] += part

    row = pl.BlockSpec((tm, d), lambda i: (i, 0))
    return pl.pallas_call(
        body,
        out_shape=(jax.ShapeDtypeStruct((t, d), F32), jax.ShapeDtypeStruct((1, LANES), F32)),
        grid=(t // tm,), in_specs=[row, row],
        out_specs=(row, pl.BlockSpec((1, LANES), lambda i: (0, 0))),
        compiler_params=_params(("arbitrary",)), name=name)(y, target)


def _window(ref, c, rows, seq, before, after, keep=None):
    r0 = pl.multiple_of(c * rows, rows)
    parts = []
    if before:
        h0 = pl.multiple_of(jnp.maximum(r0 - before, 0), before)
        halo = ref[pl.ds(h0, before), :].astype(F32)
        halo = halo if keep is None else halo[before - keep:, :]
        parts.append(jnp.where(c > 0, halo, 0.0))
    parts.append(ref[pl.ds(r0, rows), :].astype(F32))
    if after:
        h1 = pl.multiple_of(jnp.minimum(r0 + rows, seq - after), after)
        halo = ref[pl.ds(h1, after), :].astype(F32)
        halo = halo if keep is None else halo[:keep, :]
        parts.append(jnp.where(c < seq // rows - 1, halo, 0.0))
    return parts[0] if len(parts) == 1 else jnp.concatenate(parts, axis=0)


def _lag(x, k):
    return pltpu.roll(x, k, 0) if k else x


def _lead(x, k):
    return pltpu.roll(x, x.shape[0] - k, 0) if k else x


SHIFT_ROWS = 128
POOL_ROWS = 512
SHIFT_COLS = 256


HALO = 16
KEEP = 8


def _conv3(ext, w, bias):
    acc = bias + w[2:3, :] * ext[KEEP:, :]
    acc = acc + w[1:2, :] * _lag(ext, 1)[KEEP:, :]
    return acc + w[0:1, :] * _lag(ext, 2)[KEEP:, :]


def _ffn_act_fwd(hpre, cw, cb, name):
    b, seq, f2 = hpre.shape
    cbk = SHIFT_COLS
    nj = f2 // (2 * cbk)
    rows = min(SHIFT_ROWS, seq)

    def body(g_ref, v_ref, wg_ref, wv_ref, bg_ref, bv_ref, o_ref, pg_ref, pv_ref):
        def chunk(c, carry):
            gate = _conv3(_window(g_ref, c, rows, seq, HALO, 0, KEEP), wg_ref[...], bg_ref[...])
            val = _conv3(_window(v_ref, c, rows, seq, HALO, 0, KEEP), wv_ref[...], bv_ref[...])
            a = gate * _sigmoid(gate) * val
            here = pl.ds(pl.multiple_of(c * rows, rows), rows)
            o_ref[here, :] = a.astype(BF16)
            pg_ref[here, :] = gate.astype(BF16)
            pv_ref[here, :] = val.astype(BF16)
            return carry

        lax.fori_loop(0, seq // rows, chunk, 0)

    blk = lambda off: pl.BlockSpec((None, seq, cbk), lambda i, j: (i, 0, j + off))
    wsp = lambda r, off: pl.BlockSpec((r, cbk), lambda i, j: (0, j + off))
    half = jax.ShapeDtypeStruct((b, seq, f2 // 2), BF16)
    return pl.pallas_call(
        body, out_shape=(half, half, half), grid=(b, nj),
        in_specs=[blk(0), blk(nj), wsp(FFN_CONV, 0), wsp(FFN_CONV, nj), wsp(1, 0), wsp(1, nj)],
        out_specs=(blk(0), blk(0), blk(0)),
        compiler_params=_params(("parallel", "parallel")), name=name)(hpre, hpre, cw, cw, cb, cb)


def _ffn_act_bwd(hpre, pre_g, pre_v, da, cw, name):
    b, seq, f2 = hpre.shape
    cbk = SHIFT_COLS
    nj = f2 // (2 * cbk)
    rows = min(SHIFT_ROWS, seq)

    def body(g_ref, v_ref, pg_ref, pv_ref, da_ref, wg_ref, wv_ref, og_ref, ov_ref, dwg_ref, dwv_ref, dbg_ref, dbv_ref):
        wg, wv = wg_ref[...], wv_ref[...]

        def back(dpre, w, o_ref, x_ref, c, carry):
            here = pl.ds(pl.multiple_of(c * rows, rows), rows)
            leads = [dpre, _lead(dpre, 1), _lead(dpre, 2)]
            dx = w[2:3, :] * leads[0] + w[1:2, :] * leads[1] + w[0:1, :] * leads[2]
            o_ref[here, :] = dx[:rows, :].astype(BF16)
            x0 = x_ref[here, :].astype(F32)
            return tuple(carry[k] + jnp.sum(leads[k][:rows, :] * x0, axis=0, keepdims=True) for k in range(FFN_CONV)) + (
                carry[FFN_CONV] + jnp.sum(dpre[:rows, :], axis=0, keepdims=True),)

        def chunk(c, carry):
            cg, cv = carry
            gate = _window(pg_ref, c, rows, seq, 0, HALO, KEEP)
            val = _window(pv_ref, c, rows, seq, 0, HALO, KEEP)
            dav = _window(da_ref, c, rows, seq, 0, HALO, KEEP)
            sg = _sigmoid(gate)
            cg = back(dav * val * (sg * (1.0 + gate * (1.0 - sg))), wg, og_ref, g_ref, c, cg)
            cv = back(dav * (gate * sg), wv, ov_ref, v_ref, c, cv)
            return cg, cv

        z = jnp.zeros((1, cbk), F32)
        cg, cv = lax.fori_loop(0, seq // rows, chunk, ((z,) * (FFN_CONV + 1), (z,) * (FFN_CONV + 1)))
        dwg = jnp.concatenate([cg[2], cg[1], cg[0]], axis=0)
        dwv = jnp.concatenate([cv[2], cv[1], cv[0]], axis=0)

        @pl.when(pl.program_id(1) == 0)
        def _():
            dwg_ref[...] = dwg
            dwv_ref[...] = dwv
            dbg_ref[...] = cg[FFN_CONV]
            dbv_ref[...] = cv[FFN_CONV]

        @pl.when(pl.program_id(1) > 0)
        def _():
            dwg_ref[...] += dwg
            dwv_ref[...] += dwv
            dbg_ref[...] += cg[FFN_CONV]
            dbv_ref[...] += cv[FFN_CONV]

    blk = lambda off: pl.BlockSpec((None, seq, cbk), lambda j, i: (i, 0, j + off))
    wsp = lambda r, off: pl.BlockSpec((r, cbk), lambda j, i: (0, j + off))
    half = jax.ShapeDtypeStruct((b, seq, f2 // 2), BF16)
    dwshape = jax.ShapeDtypeStruct((FFN_CONV, f2 // 2), F32)
    dbshape = jax.ShapeDtypeStruct((1, f2 // 2), F32)
    dg, dv, dwg, dwv, dbg, dbv = pl.pallas_call(
        body,
        out_shape=(half, half, dwshape, dwshape, dbshape, dbshape),
        grid=(nj, b),
        in_specs=[blk(0), blk(nj), blk(0), blk(0), blk(0), wsp(FFN_CONV, 0), wsp(FFN_CONV, nj)],
        out_specs=(blk(0), blk(0), wsp(FFN_CONV, 0), wsp(FFN_CONV, 0), wsp(1, 0), wsp(1, 0)),
        compiler_params=_params(("parallel", "arbitrary")), name=name)(hpre, hpre, pre_g, pre_v, da, cw, cw)
    return dg, dv, jnp.concatenate([dwg, dwv], axis=1), jnp.concatenate([dbg, dbv], axis=1)


def _ssd_conv_fwd(zx, cw, cb, d_inner, name):
    b, seq, _ = zx.shape
    xbc = cw.shape[1]
    cbk = SHIFT_COLS
    off = d_inner // cbk
    rows = min(SHIFT_ROWS, seq)

    def body(h_ref, w_ref, b_ref, o_ref, p_ref):
        w = w_ref[...]
        bias = b_ref[...]

        def chunk(c, carry):
            ext = _window(h_ref, c, rows, seq, HALO, 0, KEEP)
            acc = bias + w[3:4, :] * ext[KEEP:, :]
            for k in range(1, SSD_CONV):
                acc = acc + w[3 - k:4 - k, :] * _lag(ext, k)[KEEP:, :]
            here = pl.ds(pl.multiple_of(c * rows, rows), rows)
            o_ref[here, :] = acc * _sigmoid(acc)
            p_ref[here, :] = acc.astype(BF16)
            return carry

        lax.fori_loop(0, seq // rows, chunk, 0)

    blk = pl.BlockSpec((None, seq, cbk), lambda i, j: (i, 0, j))
    return pl.pallas_call(
        body, out_shape=(jax.ShapeDtypeStruct((b, seq, xbc), F32), jax.ShapeDtypeStruct((b, seq, xbc), BF16)),
        grid=(b, xbc // cbk),
        in_specs=[pl.BlockSpec((None, seq, cbk), lambda i, j: (i, 0, j + off)),
                  pl.BlockSpec((SSD_CONV, cbk), lambda i, j: (0, j)),
                  pl.BlockSpec((1, cbk), lambda i, j: (0, j))],
        out_specs=(blk, blk),
        compiler_params=_params(("parallel", "parallel")), name=name)(zx, cw, cb)


def _ssd_conv_bwd(zx, pre, dparts, ddt, dzx, cw, d_inner, name):
    b, seq, zw = zx.shape
    xbc = cw.shape[1]
    cbk = SHIFT_COLS
    off = d_inner // cbk
    rows = min(SHIFT_ROWS, seq)
    nblk = [p.shape[2] // cbk for p in dparts]
    first = [sum(nblk[:s]) for s in range(len(dparts))]
    nconv = xbc // cbk
    ncopy = ddt.shape[2] // cbk
    assert sum(nblk) == nconv and (off + nconv + ncopy) * cbk == zw and dzx.shape == (b, seq, zw)

    def body(h_ref, p_ref, gx_ref, gb_ref, gc_ref, t_ref, w_ref, z_ref, o_ref, dw_ref, db_ref):
        j = pl.program_id(0)

        @pl.when(j < nconv)
        def _():
            conv(h_ref, p_ref, gx_ref, gb_ref, gc_ref, w_ref, o_ref, dw_ref, db_ref)

        @pl.when(j >= nconv)
        def _():
            o_ref[...] = t_ref[...]

    def conv(h_ref, p_ref, gx_ref, gb_ref, gc_ref, w_ref, o_ref, dw_ref, db_ref):
        w = w_ref[...]
        j = pl.program_id(0)

        def chunk(c, carry):
            dws, dbias = carry
            here = pl.ds(pl.multiple_of(c * rows, rows), rows)
            pre = _window(p_ref, c, rows, seq, 0, HALO, KEEP)
            s = _sigmoid(pre)
            gsel = jnp.where(j < first[1], _window(gx_ref, c, rows, seq, 0, HALO, KEEP),
                             jnp.where(j < first[2], _window(gb_ref, c, rows, seq, 0, HALO, KEEP),
                                       _window(gc_ref, c, rows, seq, 0, HALO, KEEP)))
            dpre = gsel * (s * (1.0 + pre * (1.0 - s)))
            leads = [dpre] + [_lead(dpre, k) for k in range(1, SSD_CONV)]
            dx = w[3:4, :] * leads[0]
            for k in range(1, SSD_CONV):
                dx = dx + w[3 - k:4 - k, :] * leads[k]
            o_ref[here, :] = dx[:rows, :].astype(BF16)
            x0 = h_ref[here, :].astype(F32)
            dws = tuple(dws[k] + jnp.sum(leads[k][:rows, :] * x0, axis=0, keepdims=True) for k in range(SSD_CONV))
            dbias = dbias + jnp.sum(dpre[:rows, :], axis=0, keepdims=True)
            return dws, dbias

        z = jnp.zeros((1, cbk), F32)
        dws, dbias = lax.fori_loop(0, seq // rows, chunk, ((z,) * SSD_CONV, z))
        dwv = jnp.concatenate([dws[3 - i] for i in range(SSD_CONV)], axis=0)

        @pl.when(pl.program_id(1) == 0)
        def _():
            dw_ref[...] = dwv
            db_ref[...] = dbias

        @pl.when(pl.program_id(1) > 0)
        def _():
            dw_ref[...] += dwv
            db_ref[...] += dbias

    conv_j = lambda j: jnp.minimum(j, nconv - 1)
    return pl.pallas_call(
        body,
        out_shape=(jax.ShapeDtypeStruct((b, seq, zw), BF16), jax.ShapeDtypeStruct((SSD_CONV, xbc), F32),
                   jax.ShapeDtypeStruct((1, xbc), F32)),
        grid=(nconv + ncopy, b),
        in_specs=[pl.BlockSpec((None, seq, cbk), lambda j, i: (i, 0, conv_j(j) + off)),
                  pl.BlockSpec((None, seq, cbk), lambda j, i: (i, 0, conv_j(j)))] + [
                  pl.BlockSpec((None, seq, cbk), lambda j, i, s=s: (i, 0, jnp.clip(j - first[s], 0, nblk[s] - 1)))
                  for s in range(3)] + [
                  pl.BlockSpec((None, seq, cbk), lambda j, i: (i, 0, jnp.clip(j - nconv, 0, ncopy - 1))),
                  pl.BlockSpec((SSD_CONV, cbk), lambda j, i: (0, conv_j(j))),
                  ANY],
        out_specs=(pl.BlockSpec((None, seq, cbk), lambda j, i: (i, 0, j + off)),
                   pl.BlockSpec((SSD_CONV, cbk), lambda j, i: (0, conv_j(j))),
                   pl.BlockSpec((1, cbk), lambda j, i: (0, conv_j(j)))),
        input_output_aliases={7: 0},
        compiler_params=_params(("arbitrary", "arbitrary")), name=name)(zx, pre, *dparts, ddt, cw, dzx)


def _pool_sums(q, g, lead):
    sh = _lead if lead else _lag
    s2 = q + sh(q, 1)
    s4 = s2 + sh(s2, 2)
    s8 = s4 + sh(s4, 4)
    s16 = s8 + sh(s8, 8)
    return jnp.where(g == 0, s2, jnp.where(g == 1, s4, jnp.where(g == 2, s8, s16)))


def _pool_count(r0, n, g, shape):
    t = (r0 + lax.broadcasted_iota(jnp.int32, shape, 0) + 1).astype(F32)
    return jnp.minimum(t, (2 << g).astype(F32))


def _pool_fwd(h, pw, scale, name):
    b, seq, d = h.shape
    dg = d // 4
    rows = min(POOL_ROWS, seq)

    def body(h_ref, w_ref, s_ref, o_ref):
        g = pl.program_id(1)
        wmat = w_ref[...]
        sc = s_ref[...]

        def chunk(c, carry):
            r0 = c * rows
            ext = _window(h_ref, c, rows, seq, 16, 0)
            sums = _pool_sums(ext, g, False)[16:, :]
            mixed = sums / _pool_count(r0, rows, g, (rows, dg)) - ext[16:, :]
            o_ref[pl.ds(pl.multiple_of(r0, rows), rows), :] = _nn(mixed.astype(BF16), wmat) * sc
            return carry

        lax.fori_loop(0, seq // rows, chunk, 0)

    return pl.pallas_call(
        body, out_shape=jax.ShapeDtypeStruct((b, seq, d), F32), grid=(b, 4),
        in_specs=[pl.BlockSpec((None, seq, dg), lambda i, g: (i, 0, g)),
                  pl.BlockSpec((None, dg, dg), lambda i, g: (g, 0, 0)),
                  pl.BlockSpec((1, dg), lambda i, g: (0, g))],
        out_specs=pl.BlockSpec((None, seq, dg), lambda i, g: (i, 0, g)),
        compiler_params=_params(("parallel", "parallel")), name=name)(h, pw, scale)


def _pool_bwd(h, dout, pw, scale, name):
    b, seq, d = h.shape
    dg = d // 4
    rows = min(POOL_ROWS, seq)

    def body(h_ref, g_ref, w_ref, s_ref, o_ref, dw_ref, ds_ref, dw_acc):
        g = pl.program_id(0)
        wmat = w_ref[...]
        sc = s_ref[...]
        dw_acc[...] = jnp.zeros_like(dw_acc)

        def chunk(c, dsc):
            r0 = c * rows
            ext = _window(h_ref, c, rows, seq, 16, 0)
            sums = _pool_sums(ext, g, False)[16:, :]
            mixed = (sums / _pool_count(r0, rows, g, (rows, dg)) - ext[16:, :]).astype(BF16)
            gext = _window(g_ref, c, rows, seq, 0, 16)
            dsc = dsc + jnp.sum(gext[:rows, :] * _nn(mixed, wmat), axis=0, keepdims=True)
            dpre = (gext * sc).astype(BF16)
            dw_acc[...] += _tn(mixed, dpre[:rows, :])
            dmix = _nt(dpre, wmat)
            q = dmix / _pool_count(r0, rows + 16, g, (rows + 16, dg))
            back = _pool_sums(q, g, True)
            o_ref[pl.ds(pl.multiple_of(r0, rows), rows), :] = back[:rows, :] - dmix[:rows, :]
            return dsc

        dsc = lax.fori_loop(0, seq // rows, chunk, jnp.zeros((1, dg), F32))

        @pl.when(pl.program_id(1) == 0)
        def _():
            dw_ref[...] = dw_acc[...]
            ds_ref[...] = dsc

        @pl.when(pl.program_id(1) > 0)
        def _():
            dw_ref[...] += dw_acc[...]
            ds_ref[...] += dsc

    return pl.pallas_call(
        body,
        out_shape=(jax.ShapeDtypeStruct((b, seq, d), F32), jax.ShapeDtypeStruct((4, dg, dg), F32),
                   jax.ShapeDtypeStruct((1, d), F32)),
        grid=(4, b),
        in_specs=[pl.BlockSpec((None, seq, dg), lambda g, i: (i, 0, g)),
                  pl.BlockSpec((None, seq, dg), lambda g, i: (i, 0, g)),
                  pl.BlockSpec((None, dg, dg), lambda g, i: (g, 0, 0)),
                  pl.BlockSpec((1, dg), lambda g, i: (0, g))],
        out_specs=(pl.BlockSpec((None, seq, dg), lambda g, i: (i, 0, g)),
                   pl.BlockSpec((None, dg, dg), lambda g, i: (g, 0, 0)),
                   pl.BlockSpec((1, dg), lambda g, i: (0, g))),
        scratch_shapes=[pltpu.VMEM((dg, dg), F32)],
        compiler_params=_params(("parallel", "arbitrary")), name=name)(h, dout, pw, scale)


def _head_of(channel):
    return jnp.right_shift(channel, HEAD_DIM.bit_length() - 1)


def _ssd_consts(gw):
    q = CHUNK
    row = lax.broadcasted_iota(jnp.int32, (q, q), 0)
    col = lax.broadcasted_iota(jnp.int32, (q, q), 1)
    tril = (row >= col).astype(BF16)
    triu = (row <= col).astype(BF16)
    e = (_head_of(lax.broadcasted_iota(jnp.int32, (LANES, gw), 1))
         == lax.broadcasted_iota(jnp.int32, (LANES, gw), 0)).astype(BF16)
    et = (_head_of(lax.broadcasted_iota(jnp.int32, (gw, LANES), 0))
          == lax.broadcasted_iota(jnp.int32, (gw, LANES), 1)).astype(BF16)
    return row, col, tril, triu, e, et


def _ssd_common(dtr, dtb, alog, gw):
    q = CHUNK
    row, col, tril, triu, e, et = _ssd_consts(gw)
    dt = _softplus(dtr + dtb)
    a_row = -jnp.exp(alog)
    acum = _sel_left(tril, dt * a_row)
    ac_last = jnp.sum(jnp.where(row == q - 1, acum, 0.0), axis=0, keepdims=True)
    eac = jnp.exp(acum)
    de = jnp.exp(ac_last - acum)
    e2 = jnp.concatenate([e, e], axis=0)
    expand = _sel_right(jnp.concatenate([dt, eac, de], axis=0), e2, 2)
    dt_x, eac_x, de_x = expand[0:q], expand[q:2 * q], expand[2 * q:3 * q]
    acum_t = acum.T
    cd_col = jnp.exp(acum_t[:, q - 1:q])
    et3 = jnp.concatenate([et, et, et], axis=1)
    cdmat = _nn(et3, jnp.concatenate(_split(jnp.broadcast_to(cd_col, (LANES, D_STATE)), 3), axis=0))
    consts = dict(row=row, col=col, tril=tril, triu=triu, e=e, et=et)
    return dt, a_row, acum, acum_t, ac_last, eac, de, dt_x, eac_x, de_x, cdmat, consts


def _decay(acum, acum_t, j, row, col):
    diff = acum[:, j:j + 1] - acum_t[j:j + 1, :]
    return jnp.exp(jnp.where(row >= col, diff, -1e30))


def _ssd_fwd(xc, zx, dtr, dtb, alog, dskip, nw, d_inner, name):
    b, seq, xbc = xc.shape
    q = CHUNK
    nc = seq // q
    gw = d_inner // N_GROUPS
    nh = gw // HEAD_DIM
    xb0 = d_inner // D_STATE
    xc0 = xb0 + N_GROUPS

    nb = max(n for n in (4, 2, 1) if b % n == 0)

    def body(x_ref, b_ref, c_ref, z_ref, dtr_ref, dtb_ref, al_ref, dsk_ref, nw_ref, y_ref, yn_ref, st_ref, s_ref):
        @pl.when(pl.program_id(2) == 0)
        def _():
            s_ref[...] = jnp.zeros_like(s_ref)

        for s in range(nb):
            one(s, x_ref.at[s], b_ref.at[s], c_ref.at[s], z_ref.at[s], dtr_ref.at[s], dtb_ref, al_ref, dsk_ref, nw_ref,
                y_ref.at[s], yn_ref.at[s], st_ref.at[s], s_ref.at[s])

    def one(s, x_ref, b_ref, c_ref, z_ref, dtr_ref, dtb_ref, al_ref, dsk_ref, nw_ref, y_ref, yn_ref, st_ref, s_ref):
        prev = s_ref[...]
        st_ref[...] = prev
        x = x_ref[...]
        bm = b_ref[...].astype(BF16)
        cm = c_ref[...].astype(BF16)
        (dt, a_row, acum, acum_t, ac_last, eac, de, dt_x, eac_x, de_x, cdmat, k) = _ssd_common(
            dtr_ref[...], dtb_ref[0:1, :], al_ref[0:1, :], gw)
        xdt = x * dt_x
        xdt_b = xdt.astype(BF16)
        cb = _nt(cm, bm)
        half = _head_of(lax.broadcasted_iota(jnp.int32, (q, LANES), 1))
        pairs = []
        for j in range(nh):
            pc = (j // 2) * LANES
            m = (cb * _decay(acum, acum_t, j, k["row"], k["col"])).astype(BF16)
            yj = jnp.where(half == j % 2, _nn(m, xdt_b[:, pc:pc + LANES]), 0.0)
            if j % 2 == 0:
                pairs.append(yj)
            else:
                pairs[-1] = pairs[-1] + yj
        prev_b = prev.astype(BF16)
        y = dsk_ref[0:1, :] * x + jnp.concatenate(pairs, axis=1) + eac_x * _nt(cm, prev_b)
        s_ref[...] = cdmat * prev + _tn((xdt * de_x).astype(BF16), bm)
        y_ref[...] = y
        z = z_ref[...].astype(F32)
        yg = y * (z * _sigmoid(z))
        r = lax.rsqrt(jnp.mean(yg * yg, axis=-1, keepdims=True) + EPS)
        yn_ref[...] = ((yg * r) * nw_ref[0:1, :]).astype(BF16)

    par = lambda w: pl.BlockSpec((None, 8, w), lambda i, g, c: (g, 0, 0))
    return pl.pallas_call(
        body,
        out_shape=(jax.ShapeDtypeStruct((b, seq, d_inner), F32), jax.ShapeDtypeStruct((b, seq, d_inner), BF16),
                   jax.ShapeDtypeStruct((b, nc, N_GROUPS, gw, D_STATE), F32)),
        grid=(b // nb, N_GROUPS, nc),
        in_specs=[pl.BlockSpec((nb, q, gw), lambda i, g, c: (i, c, g)),
                  pl.BlockSpec((nb, q, D_STATE), lambda i, g, c: (i, c, xb0 + g)),
                  pl.BlockSpec((nb, q, D_STATE), lambda i, g, c: (i, c, xc0 + g)),
                  pl.BlockSpec((nb, q, gw), lambda i, g, c: (i, c, g)),
                  pl.BlockSpec((nb, q, LANES), lambda i, g, c: (i, c, g)),
                  par(LANES), par(LANES), par(gw), par(gw)],
        out_specs=(pl.BlockSpec((nb, q, gw), lambda i, g, c: (i, c, g)),
                   pl.BlockSpec((nb, q, gw), lambda i, g, c: (i, c, g)),
                   pl.BlockSpec((nb, None, None, gw, D_STATE), lambda i, g, c: (i, c, g, 0, 0))),
        scratch_shapes=[pltpu.VMEM((nb, gw, D_STATE), F32)],
        compiler_params=_params(("parallel", "parallel", "arbitrary")), name=name,
    )(xc, xc, xc, zx, dtr, dtb, alog, dskip, nw)


def _ssd_bwd(xc, zx, dtr, y, dyn, st, dtb, alog, dskip, nw, d_inner, name):
    b, seq, xbc = xc.shape
    q = CHUNK
    nc = seq // q
    gw = d_inner // N_GROUPS
    nh = gw // HEAD_DIM
    xb0 = d_inner // D_STATE
    xc0 = xb0 + N_GROUPS

    nb = max(n for n in (4, 2, 1) if b % n == 0)

    def body(x_ref, b_ref, c_ref, z_ref, dtr_ref, y_ref, g_ref, st_ref, dtb_ref, al_ref, dsk_ref, nw_ref,
             dz_ref, dx_ref, db_ref, dc_ref, ddt_ref, dnw_ref, dd_ref, dal_ref, dbias_ref,
             ds_ref, colbuf, rowbuf):
        first = jnp.logical_and(pl.program_id(1) == 0, pl.program_id(2) == 0)

        @pl.when(pl.program_id(2) == 0)
        def _():
            ds_ref[...] = jnp.zeros_like(ds_ref)

        sums = [one(x_ref.at[s], b_ref.at[s], c_ref.at[s], z_ref.at[s], dtr_ref.at[s], y_ref.at[s], g_ref.at[s],
                    st_ref.at[s], dtb_ref, al_ref, dsk_ref, nw_ref, dz_ref.at[s], dx_ref.at[s], db_ref.at[s],
                    dc_ref.at[s], ddt_ref.at[s], ds_ref.at[s], colbuf.at[s], rowbuf.at[s]) for s in range(nb)]
        dnw, dd, dal, dbias = [functools.reduce(lambda p, r: p + r, [sm[i] for sm in sums]) for i in range(4)]

        @pl.when(first)
        def _():
            dnw_ref[...] = jnp.broadcast_to(dnw, (8, gw))
            dd_ref[...] = dd
            dal_ref[...] = jnp.broadcast_to(dal, (8, LANES))
            dbias_ref[...] = jnp.broadcast_to(dbias, (8, LANES))

        @pl.when(jnp.logical_not(first))
        def _():
            dnw_ref[...] += jnp.broadcast_to(dnw, (8, gw))
            dd_ref[...] += dd
            dal_ref[...] += jnp.broadcast_to(dal, (8, LANES))
            dbias_ref[...] += jnp.broadcast_to(dbias, (8, LANES))

    def one(x_ref, b_ref, c_ref, z_ref, dtr_ref, y_ref, g_ref, st_ref, dtb_ref, al_ref, dsk_ref, nw_ref,
            dz_ref, dx_ref, db_ref, dc_ref, ddt_ref, ds_ref, colbuf, rowbuf):
        x = x_ref[...]
        bm = b_ref[...].astype(BF16)
        cm = c_ref[...].astype(BF16)
        z = z_ref[...].astype(F32)
        y = y_ref[...]
        prev = st_ref[...]
        dtr = dtr_ref[...] + dtb_ref[0:1, :]
        (dt, a_row, acum, acum_t, ac_last, eac, de, dt_x, eac_x, de_x, cdmat, k) = _ssd_common(
            dtr_ref[...], dtb_ref[0:1, :], al_ref[0:1, :], gw)
        row, col = k["row"], k["col"]
        et2 = jnp.concatenate([k["et"], k["et"]], axis=0)

        sz = _sigmoid(z)
        silu_z = z * sz
        yg = y * silu_z
        r = lax.rsqrt(jnp.mean(yg * yg, axis=-1, keepdims=True) + EPS)
        xh = yg * r
        dyn = g_ref[...].astype(F32)
        gh = dyn * nw_ref[0:1, :]
        dyg = r * (gh - xh * jnp.mean(gh * xh, axis=-1, keepdims=True))
        dnw = jnp.sum(dyn * xh, axis=0, keepdims=True)
        g = dyg * silu_z
        dz_ref[...] = (dyg * y * (sz * (1.0 + z * (1.0 - sz)))).astype(BF16)
        dd = _sel_right(jnp.broadcast_to(jnp.sum(g * x, axis=0, keepdims=True), (8, gw)), et2, 2)

        xdt = x * dt_x
        xdt_b = xdt.astype(BF16)
        g_b = g.astype(BF16)
        prev_b = prev.astype(BF16)
        cb = _nt(cm, bm)

        cp = _nt(cm, prev_b)
        ge = g * eac_x
        dac = _sel_right(ge * cp, et2, 2)
        ge_b = ge.astype(BF16)
        dcm = _nn(ge_b, prev_b)
        dprev = _tn(ge_b, cm)

        colbuf[...] = jnp.zeros_like(colbuf)
        rowbuf[...] = jnp.zeros_like(rowbuf)
        dcb = jnp.zeros((q, q), F32)
        half = _head_of(lax.broadcasted_iota(jnp.int32, (q, LANES), 1))
        pairs = []
        for j in range(nh):
            pc = (j // 2) * LANES
            dec = _decay(acum, acum_t, j, row, col)
            m = cb * dec
            gj = jnp.where(half == j % 2, g[:, pc:pc + LANES], 0.0).astype(BF16)
            dm = _nt(gj, xdt_b[:, pc:pc + LANES])
            w = dm * m
            colbuf[:, j:j + 1] = jnp.sum(w, axis=1, keepdims=True)
            rowbuf[j:j + 1, :] = jnp.sum(w, axis=0, keepdims=True)
            dcb = dcb + dm * dec
            dj = jnp.where(half == j % 2, _tn(m.astype(BF16), g_b[:, pc:pc + LANES]), 0.0)
            if j % 2 == 0:
                pairs.append(dj)
            else:
                pairs[-1] = pairs[-1] + dj
        dxdt = jnp.concatenate(pairs, axis=1)
        dcb_b = dcb.astype(BF16)
        dcm = dcm + _nn(dcb_b, bm)
        dbm = _tn(dcb_b, cm)

        ds = ds_ref[...]
        ds_b = ds.astype(BF16)
        u = _nt(bm, ds_b)
        dxdt = dxdt + u * de_x
        dde = _sel_right(u * xdt, et2, 2)
        dbm = dbm + _nn((xdt * de_x).astype(BF16), ds_b)
        pm = jnp.concatenate(_split(ds * prev, 2), axis=1)
        t2 = _tn(pm, k["et"])
        dcd_row = jnp.sum(t2[0:D_STATE] + t2[D_STATE:2 * D_STATE], axis=0, keepdims=True)
        last = dcd_row * jnp.exp(ac_last) + jnp.sum(dde * de, axis=0, keepdims=True)
        dac = dac + colbuf[...] - rowbuf[...].T - dde * de + jnp.where(row == q - 1, last, 0.0)
        ds_ref[...] = cdmat * ds + dprev

        dadt = _sel_left(k["triu"], dac)
        ddt = _sel_right(dxdt * x, et2, 2) + dadt * a_row
        dal = jnp.sum(dadt * dt, axis=0, keepdims=True) * a_row
        lane = lax.broadcasted_iota(jnp.int32, (q, LANES), 1)
        ddtr = jnp.where(lane < nh, ddt * _sigmoid(dtr), 0.0)
        ddt_ref[...] = ddtr.astype(BF16)
        dbias = jnp.sum(ddtr, axis=0, keepdims=True)
        dx_ref[...] = dxdt * dt_x + dsk_ref[0:1, :] * g
        db_ref[...] = dbm
        dc_ref[...] = dcm
        return dnw, dd, dal, dbias

    rc = lambda c: nc - 1 - c
    par = lambda w: pl.BlockSpec((None, 8, w), lambda g, i, c: (g, 0, 0))
    blk = lambda w: pl.BlockSpec((nb, q, w), lambda g, i, c: (i, rc(c), g))
    return pl.pallas_call(
        body,
        out_shape=(jax.ShapeDtypeStruct((b, seq, zx.shape[2]), BF16),
                   jax.ShapeDtypeStruct((b, seq, d_inner), F32),
                   jax.ShapeDtypeStruct((b, seq, N_GROUPS * D_STATE), F32),
                   jax.ShapeDtypeStruct((b, seq, N_GROUPS * D_STATE), F32),
                   jax.ShapeDtypeStruct((b, seq, N_GROUPS * LANES), BF16),
                   jax.ShapeDtypeStruct((N_GROUPS, 8, gw), F32),
                   jax.ShapeDtypeStruct((N_GROUPS, 8, LANES), F32),
                   jax.ShapeDtypeStruct((N_GROUPS, 8, LANES), F32),
                   jax.ShapeDtypeStruct((N_GROUPS, 8, LANES), F32)),
        grid=(N_GROUPS, b // nb, nc),
        in_specs=[blk(gw),
                  pl.BlockSpec((nb, q, D_STATE), lambda g, i, c: (i, rc(c), xb0 + g)),
                  pl.BlockSpec((nb, q, D_STATE), lambda g, i, c: (i, rc(c), xc0 + g)),
                  blk(gw),
                  pl.BlockSpec((nb, q, LANES), lambda g, i, c: (i, rc(c), g)),
                  blk(gw), blk(gw),
                  pl.BlockSpec((nb, None, None, gw, D_STATE), lambda g, i, c: (i, rc(c), g, 0, 0)),
                  par(LANES), par(LANES), par(gw), par(gw)],
        out_specs=(blk(gw), blk(gw), blk(D_STATE), blk(D_STATE), blk(LANES),
                   par(gw), par(LANES), par(LANES), par(LANES)),
        scratch_shapes=[pltpu.VMEM((nb, gw, D_STATE), F32), pltpu.VMEM((nb, q, LANES), F32),
                        pltpu.VMEM((nb, LANES, q), F32)],
        compiler_params=_params(("parallel", "arbitrary", "arbitrary")), name=name,
    )(xc, xc, xc, zx, dtr, y, dyn, st, dtb, alog, dskip, nw)


def _adamw(w, g, m, v, name):
    rows, cols = w.shape
    tr = rows
    for cand in (512, 256, 128, 64, 32, 16, 8):
        if rows % cand == 0 and cand * cols * 4 <= 2 * 1024 * 1024:
            tr = cand
            break
    c1 = 1.0 - ADAM_B1 ** ADAM_STEP
    c2 = 1.0 - ADAM_B2 ** ADAM_STEP

    def body(w_ref, g_ref, m_ref, v_ref, d_ref, mo_ref, vo_ref):
        gv = g_ref[...]
        mn = ADAM_B1 * m_ref[...] + (1.0 - ADAM_B1) * gv
        vn = ADAM_B2 * v_ref[...] + (1.0 - ADAM_B2) * (gv * gv)
        mo_ref[...] = mn
        vo_ref[...] = vn
        d_ref[...] = -ADAM_LR * ((mn / c1) / (jnp.sqrt(vn / c2) + ADAM_EPS) + ADAM_WD * w_ref[...])

    spec = pl.BlockSpec((tr, cols), lambda i: (i, 0))
    shp = jax.ShapeDtypeStruct((rows, cols), F32)
    return pl.pallas_call(body, out_shape=(shp, shp, shp), grid=(rows // tr,), in_specs=[spec] * 4,
                          out_specs=(spec,) * 3, compiler_params=_params(("parallel",)), name=name)(w, g, m, v)


def _pick_rows(rows, row_bytes, limit=1 << 20):
    for cand in (2048, 1024, 512, 256, 128, 64, 32, 16):
        if rows % cand == 0 and cand * row_bytes <= limit:
            return cand
    return rows


def _as3d(a, lead):
    return a.reshape(a.shape[:lead] + (-1, a.shape[-1]))


def _pair_sum(g, got, core, name):
    h = got.shape[0]
    g3, got3 = _as3d(g, 1), _as3d(got, 1)
    _, rows, cols = got3.shape
    tr = _pick_rows(rows, cols * 4)

    def body(c_ref, g_ref, r_ref, o_ref):
        o_ref[...] = (g_ref[...] + r_ref[...]).astype(BF16)

    out = pl.pallas_call(
        body, out_shape=jax.ShapeDtypeStruct(got3.shape, BF16),
        grid_spec=pltpu.PrefetchScalarGridSpec(
            num_scalar_prefetch=1, grid=(h, rows // tr),
            in_specs=[pl.BlockSpec((None, tr, cols), lambda l, i, c_ref: (c_ref[0] * h + l, i, 0)),
                      pl.BlockSpec((None, tr, cols), lambda l, i, c_ref: (l, i, 0))],
            out_specs=pl.BlockSpec((None, tr, cols), lambda l, i, c_ref: (l, i, 0))),
        compiler_params=_params(("parallel", "parallel")), name=name)(core, g3, got3)
    return out.reshape(got.shape)


def _sum4(q, core, name):
    q4 = _as3d(q, 2)
    _, h, rows, cols = q4.shape
    tr = _pick_rows(rows, cols * 4)

    def body(c_ref, q0, q1, q2, q3, o_ref):
        o_ref[...] = ((q0[...].astype(F32) + q1[...].astype(F32)) + q2[...].astype(F32)) + q3[...].astype(F32)

    out = pl.pallas_call(
        body, out_shape=jax.ShapeDtypeStruct((2 * h, rows, cols), F32),
        grid_spec=pltpu.PrefetchScalarGridSpec(
            num_scalar_prefetch=1, grid=(h, rows // tr),
            in_specs=[pl.BlockSpec((None, None, tr, cols), lambda l, i, c_ref, k=k: (k, l, i, 0))
                      for k in range(N_CHIPS)],
            out_specs=pl.BlockSpec((None, tr, cols), lambda l, i, c_ref: (c_ref[0] * h + l, i, 0))),
        compiler_params=_params(("parallel", "parallel")), name=name)(core, q4, q4, q4, q4)
    return out.reshape((2 * h,) + q.shape[2:])


def _coords():
    return lax.axis_index("x"), lax.axis_index("y"), lax.axis_index("c")


def _other_chips(x, y):
    return [(1 - x, y), (x, 1 - y), (1 - x, 1 - y)]


def _allgather_halves(src, name):
    rows, cols = src.shape

    def body(x_ref, o_ref, send, recv, local):
        x, y, c = _coords()
        sib = (x, y, 1 - c)
        chips = _other_chips(x, y)

        def slot(h, cx, cy):
            return o_ref.at[h, 2 * cx + cy]

        def copy(kk, dst, to, src_ref):
            return pltpu.make_async_remote_copy(src_ref=src_ref, dst_ref=dst, send_sem=send.at[kk],
                                                recv_sem=recv.at[kk], device_id=to, device_id_type=MESH)

        mine = pltpu.make_async_copy(x_ref, slot(c, x, y), local)
        mine.start()
        first = [copy(0, slot(c, x, y), sib, x_ref)]
        first += [copy(1 + j, slot(c, x, y), (*chip, c), x_ref) for j, chip in enumerate(chips)]
        for cp in first:
            cp.start()
        passed = [copy(4 + j, slot(c, *chip), sib, slot(c, *chip)) for j, chip in enumerate(chips)]
        for j, chip in enumerate(chips):
            copy(1 + j, slot(c, *chip), (x, y, c), x_ref).wait_recv()
            passed[j].start()
        copy(0, slot(1 - c, x, y), (x, y, c), x_ref).wait_recv()
        for j, chip in enumerate(chips):
            copy(4 + j, slot(1 - c, *chip), (x, y, c), x_ref).wait_recv()
        for cp in first + passed:
            cp.wait_send()
        mine.wait()

    return pl.pallas_call(
        body, out_shape=jax.ShapeDtypeStruct((2, N_CHIPS, rows, cols), src.dtype),
        in_specs=[ANY], out_specs=ANY,
        scratch_shapes=[pltpu.SemaphoreType.DMA((7,)), pltpu.SemaphoreType.DMA((7,)), pltpu.SemaphoreType.DMA],
        name=name)(src)


MIXW = (("ssd_w_in", None), ("ssd_w_out", 0), ("pool_w", 1))
FFNW = (("ffn_w_up", 1), ("ffn_w_down", 0))


def _chip_window(axis, ref, layers, k):
    if axis is None:
        return ref.at[layers, k]
    n = ref.shape[1 + axis] // N_CHIPS
    sl = pl.ds(pl.multiple_of(k * n, LANES if 1 + axis == len(ref.shape) - 1 else 8), n)
    idx = [layers] + [slice(None)] * (len(ref.shape) - 1)
    idx[1 + axis] = sl
    return ref.at[tuple(idx)]


def _full_shape(axis, shard_shape):
    if axis is None:
        return (shard_shape[0], N_CHIPS) + tuple(shard_shape[1:])
    full = list(shard_shape)
    full[1 + axis] *= N_CHIPS
    return tuple(full)


HBM_SPEC = pl.BlockSpec(memory_space=pltpu.HBM)
SEM_SPEC = pl.BlockSpec(memory_space=pltpu.SEMAPHORE)


def _dma_sems(count):
    return pltpu.SemaphoreType.DMA((max(count, 1),))


def _wait_for(copy, kind):
    if kind == "recv":
        copy.wait_recv()
    elif kind == "send":
        copy.wait_send()
    else:
        copy.wait()


def _comm_fused(stages, counts, srcs, lands, name, inplace=False):
    ns, nl, k = len(srcs), len(lands), len(stages)

    def body(*refs):
        src_refs = refs[:ns]
        land_refs = refs[ns + (nl if inplace else 0):ns + (nl if inplace else 0) + nl]
        sem_refs = refs[len(refs) - 3 * k:]
        for s, stage_fn in enumerate(stages):
            starts, waits = stage_fn(src_refs, land_refs, tuple(sem_refs[3 * s:3 * s + 3]))
            for cp in starts:
                cp.start()
            for cp, kind in waits:
                _wait_for(cp, kind)

    scratch = []
    for cnt in counts:
        scratch += [_dma_sems(c) for c in cnt]
    outs = pl.pallas_call(
        body, out_shape=tuple(jax.ShapeDtypeStruct(a.shape, a.dtype) for a in lands),
        in_specs=[ANY] * (ns + (nl if inplace else 0)), out_specs=(ANY,) * nl,
        input_output_aliases={ns + i: i for i in range(nl)} if inplace else {},
        scratch_shapes=scratch, name=name)(*srcs, *(lands if inplace else ()))
    return list(outs)


class _SplitComm:
    def __init__(self, stages, counts, srcs, lands, name):
        self.stages, self.counts, self.name = stages, counts, name
        self.ns = len(srcs)
        self.data = [pltpu.with_memory_space_constraint(a, pltpu.HBM) for a in list(srcs) + list(lands)]
        self.sems = None
        self.step = 0

    def advance(self, after=None):
        i, k, nd, ns = self.step, len(self.stages), len(self.data), self.ns
        first, last = i == 0, i == k
        stages = self.stages
        after = list(after) if isinstance(after, (list, tuple)) else [after]

        def body(*refs):
            data = refs[:nd]
            pos = nd
            if not first:
                old = tuple(refs[pos:pos + 3])
                pos += 3 + len(after)
            if not last:
                new = tuple(refs[pos:pos + 3])
            if not first:
                for cp, kind in stages[i - 1](data[:ns], data[ns:], old)[1]:
                    _wait_for(cp, kind)
            if not last:
                for cp in stages[i](data[:ns], data[ns:], new)[0]:
                    cp.start()
                refs[len(refs) - 1][...] = jnp.zeros((8, LANES), F32)

        args = list(self.data)
        in_specs = [HBM_SPEC] * nd
        if not first:
            args += list(self.sems) + after
            in_specs += [SEM_SPEC] * 3 + [ANY] * len(after)
        out_shape, out_specs = [], []
        if not last:
            out_shape += [_dma_sems(c) for c in self.counts[i]]
            out_specs += [SEM_SPEC] * 3
        out_shape += [pltpu.HBM(a.shape, a.dtype) for a in self.data]
        out_specs += [HBM_SPEC] * nd
        if not last:
            out_shape.append(jax.ShapeDtypeStruct((8, LANES), F32))
            out_specs.append(pl.BlockSpec(memory_space=pltpu.VMEM))
        off = 0 if last else 3
        outs = pl.pallas_call(
            body, out_shape=tuple(out_shape), in_specs=in_specs, out_specs=tuple(out_specs),
            input_output_aliases={d: off + d for d in range(nd)},
            compiler_params=pltpu.CompilerParams(has_side_effects=pltpu.SideEffectType.DATAFLOW_SIDE_EFFECTING),
            name=f"{self.name}_{i}")(*args)
        self.sems = None if last else outs[:3]
        self.data = list(outs[off:off + nd])
        self.step += 1
        return None if last else outs[len(outs) - 1]

    def lands(self):
        return self.data[self.ns:]


def _gather_stages(spec):
    n = len(spec)

    def parts(srcs, lands):
        x, y, c = _coords()
        out = []
        for w, (_, axis) in enumerate(spec):
            h = srcs[w].shape[0] // 2
            mine, theirs = pl.ds(c * h, h), pl.ds((1 - c) * h, h)
            out.append((srcs[w].at[mine], lambda layers, k, w=w, axis=axis: _chip_window(axis, lands[w], layers, k),
                        mine, theirs))
        return x, y, c, 2 * x + y, (x, y, 1 - c), _other_chips(x, y), out

    def remote(src, dst, send, recv, idx, to):
        return pltpu.make_async_remote_copy(src_ref=src, dst_ref=dst, send_sem=send.at[idx], recv_sem=recv.at[idx],
                                            device_id=to, device_id_type=MESH)

    def stage0(srcs, lands, sems):
        send, recv, local = sems
        x, y, c, me, sib, chips, ps = parts(srcs, lands)
        starts, waits = [], []
        for w, (src, dst, mine, theirs) in enumerate(ps):
            lc = pltpu.make_async_copy(src, dst(mine, me), local.at[w])
            first = [remote(src, dst(mine, me), send, recv, 4 * w, sib)]
            first += [remote(src, dst(mine, me), send, recv, 4 * w + 1 + j, (cx, cy, c)) for j, (cx, cy) in enumerate(chips)]
            starts += [lc] + first
            waits.append((remote(src, dst(theirs, me), send, recv, 4 * w, (x, y, c)), "recv"))
            waits += [(remote(src, dst(mine, 2 * cx + cy), send, recv, 4 * w + 1 + j, (x, y, c)), "recv")
                      for j, (cx, cy) in enumerate(chips)]
            waits += [(cp, "send") for cp in first] + [(lc, "local")]
        return starts, waits

    def stage1(srcs, lands, sems):
        send, recv, _ = sems
        x, y, c, me, sib, chips, ps = parts(srcs, lands)
        starts, waits = [], []
        for w, (src, dst, mine, theirs) in enumerate(ps):
            for j, (cx, cy) in enumerate(chips):
                blk = dst(mine, 2 * cx + cy)
                fwd = remote(blk, blk, send, recv, 3 * w + j, sib)
                starts.append(fwd)
                waits.append((remote(src, dst(theirs, 2 * cx + cy), send, recv, 3 * w + j, (x, y, c)), "recv"))
                waits.append((fwd, "send"))
        return starts, waits

    return [stage0, stage1], [(4 * n, 4 * n, n), (3 * n, 3 * n, 0)]


def _swap_stages(spec):
    n = len(spec)

    def stage(srcs, lands, sems):
        send, recv, _ = sems
        x, y, c = _coords()
        starts, waits = [], []
        for w in range(n):
            h = srcs[w].shape[0] // 2
            cp = pltpu.make_async_remote_copy(src_ref=srcs[w].at[pl.ds((1 - c) * h, h)], dst_ref=lands[w],
                                              send_sem=send.at[w], recv_sem=recv.at[w],
                                              device_id=(x, y, 1 - c), device_id_type=MESH)
            starts.append(cp)
            waits += [(cp, "recv"), (cp, "send")]
        return starts, waits

    return [stage], [(n, n, 0)]


def _scatter_stages(spec):
    n = len(spec)

    def stage(srcs, lands, sems):
        send, recv, local = sems
        x, y, c = _coords()
        me = 2 * x + y
        starts, waits = [], []
        for w, (_, axis) in enumerate(spec):
            layers = pl.ds(0, srcs[w].shape[0])
            own = _chip_window(axis, srcs[w], layers, me)
            lc = pltpu.make_async_copy(own, lands[w].at[me], local.at[w])
            starts.append(lc)
            for j, (cx, cy) in enumerate(_other_chips(x, y)):
                cp = pltpu.make_async_remote_copy(src_ref=_chip_window(axis, srcs[w], layers, 2 * cx + cy),
                                                  dst_ref=lands[w].at[me], send_sem=send.at[3 * w + j],
                                                  recv_sem=recv.at[3 * w + j], device_id=(cx, cy, c), device_id_type=MESH)
                starts.append(cp)
                waits.append((pltpu.make_async_remote_copy(
                    src_ref=own, dst_ref=lands[w].at[2 * cx + cy], send_sem=send.at[3 * w + j], recv_sem=recv.at[3 * w + j],
                    device_id=(x, y, c), device_id_type=MESH), "recv"))
                waits.append((cp, "send"))
            waits.append((lc, "local"))
        return starts, waits

    return [stage], [(3 * n, 3 * n, n)]


def _share_stages(spec):
    n = len(spec)

    def stage(srcs, lands, sems):
        send, recv, _ = sems
        x, y, c = _coords()
        starts, waits = [], []
        for w in range(n):
            h = lands[w].shape[0] // 2
            mine, theirs = lands[w].at[pl.ds(c * h, h)], lands[w].at[pl.ds((1 - c) * h, h)]
            cp = pltpu.make_async_remote_copy(src_ref=mine, dst_ref=mine, send_sem=send.at[w], recv_sem=recv.at[w],
                                              device_id=(x, y, 1 - c), device_id_type=MESH)
            starts.append(cp)
            waits.append((pltpu.make_async_remote_copy(src_ref=theirs, dst_ref=theirs, send_sem=send.at[w],
                                                       recv_sem=recv.at[w], device_id=(x, y, c), device_id_type=MESH),
                          "recv"))
            waits.append((cp, "send"))
        return starts, waits

    return [stage], [(n, n, 0)]


def _shard_of(p, axis):
    if axis is None:
        return (p.shape[0],) + tuple(p.shape[2:])
    s = list(p.shape)
    s[1 + axis] //= N_CHIPS
    return tuple(s)


def _gather8_stages():
    def stage(srcs, lands, sems):
        send, recv, local = sems
        x, y, c = _coords()
        me = 4 * x + 2 * y + c
        lc = pltpu.make_async_copy(srcs[0], lands[0].at[me], local.at[0])
        starts, waits = [lc], []
        for kk in range(1, 8):
            to = (1 - x if kk & 4 else x, 1 - y if kk & 2 else y, 1 - c if kk & 1 else c)
            cp = pltpu.make_async_remote_copy(src_ref=srcs[0], dst_ref=lands[0].at[me], send_sem=send.at[kk - 1],
                                              recv_sem=recv.at[kk - 1], device_id=to, device_id_type=MESH)
            starts.append(cp)
            waits.append((pltpu.make_async_remote_copy(
                src_ref=srcs[0], dst_ref=lands[0].at[4 * to[0] + 2 * to[1] + to[2]], send_sem=send.at[kk - 1],
                recv_sem=recv.at[kk - 1], device_id=(x, y, c), device_id_type=MESH), "recv"))
            waits.append((cp, "send"))
        waits.append((lc, "local"))
        return starts, waits

    return [stage], [(7, 7, 1)]


def _sum8(buf, name):
    _, rows, cols = buf.shape
    tr = _pick_rows(rows, cols * 4)

    def body(*refs):
        acc = refs[0][...]
        for r in refs[1:8]:
            acc = acc + r[...]
        refs[8][...] = acc

    return pl.pallas_call(
        body, out_shape=jax.ShapeDtypeStruct((rows, cols), F32), grid=(rows // tr,),
        in_specs=[pl.BlockSpec((None, tr, cols), lambda i, k=k: (k, i, 0)) for k in range(8)],
        out_specs=pl.BlockSpec((tr, cols), lambda i: (i, 0)),
        compiler_params=_params(("parallel",)), name=name)(*([buf] * 8))


def _reduce_begin(spec, gs, core, tag, riders=(), during=None):
    stages, counts = _swap_stages(spec)
    got_shapes = [jax.ShapeDtypeStruct((g.shape[0] // 2,) + g.shape[1:], g.dtype) for g in gs]
    if during is None:
        got = _comm_fused(stages, counts, list(gs) + list(riders), got_shapes, "swap_" + tag)
    else:
        swap = _SplitComm(stages, counts, list(gs) + list(riders), [lax.empty(s.shape, s.dtype) for s in got_shapes],
                          "swap_" + tag)
        swap.advance(after=during(swap.advance()))
        gs, got = swap.data[:len(gs)], swap.lands()
    pair = [_pair_sum(a, r, core, "pair_sum_" + n) for a, r, (n, _) in zip(gs, got, spec)]
    stages, counts = _scatter_stages(spec)
    lands = [lax.empty((N_CHIPS,) + _shard_of(p, axis), p.dtype) for p, (_, axis) in zip(pair, spec)]
    comm = _SplitComm(stages, counts, pair, lands, "scatter_" + tag)
    return comm, comm.advance()


def _reduce_finish(spec, comm, core, tag, after):
    comm.advance(after=after)
    halves = [_sum4(q, core, "sum4_" + n) for q, (n, _) in zip(comm.lands(), spec)]
    stages, counts = _share_stages(spec)
    return _comm_fused(stages, counts, [], halves, "share_" + tag, inplace=True)


SMALL = (("ssd_conv_w", 2), ("pool_scale", 1), ("ffn_conv_w", 2))
REPL = ("ssd_conv_b", "ssd_dt_bias", "ssd_a_log", "ssd_d", "ssd_norm_w", "ffn_conv_b",
        "norm_mix_pre", "norm_mix_post", "norm_ffn_pre", "norm_ffn_post")
WEIGHTS = ("ssd_w_in", "ssd_conv_w", "ssd_conv_b", "ssd_dt_bias", "ssd_a_log", "ssd_d", "ssd_norm_w", "ssd_w_out",
           "pool_w", "pool_scale", "ffn_w_up", "ffn_conv_w", "ffn_conv_b", "ffn_w_down", "norm_mix_pre",
           "norm_mix_post", "norm_ffn_pre", "norm_ffn_post")


def _flat_rows(n):
    unit = 2 * 16 * FLAT_COLS
    return 2 * 16 * ((n + unit - 1) // unit)


def _flatten_shards(arrs, dtype):
    flat = jnp.concatenate([a.astype(dtype).reshape(-1) for a in arrs])
    rows = _flat_rows(flat.shape[0])
    flat = jnp.pad(flat, (0, rows * FLAT_COLS - flat.shape[0]))
    return flat.reshape(2, rows // 2, FLAT_COLS)


def _unflatten_full(gathered, shard_shapes, axes):
    per_chip = jnp.swapaxes(gathered, 0, 1).reshape(N_CHIPS, -1)
    out, off = [], 0
    for shp, ax in zip(shard_shapes, axes):
        n = math.prod(shp)
        pieces = [per_chip[k, off:off + n].reshape(shp) for k in range(N_CHIPS)]
        out.append(jnp.concatenate(pieces, axis=ax))
        off += n
    return out


def kernel(x, ssd_w_in, ssd_conv_w, ssd_conv_b, ssd_dt_bias, ssd_a_log, ssd_d, ssd_norm_w, ssd_w_out, pool_w, pool_scale, ffn_w_up, ffn_conv_w, ffn_conv_b, ffn_w_down, norm_mix_pre, norm_mix_post, norm_ffn_pre, norm_ffn_post, loss_target, m_ssd_w_in, m_ssd_conv_w, m_ssd_conv_b, m_ssd_dt_bias, m_ssd_a_log, m_ssd_d, m_ssd_norm_w, m_ssd_w_out, m_pool_w, m_pool_scale, m_ffn_w_up, m_ffn_conv_w, m_ffn_conv_b, m_ffn_w_down, m_norm_mix_pre, m_norm_mix_post, m_norm_ffn_pre, m_norm_ffn_post, v_ssd_w_in, v_ssd_conv_w, v_ssd_conv_b, v_ssd_dt_bias, v_ssd_a_log, v_ssd_d, v_ssd_norm_w, v_ssd_w_out, v_pool_w, v_pool_scale, v_ffn_w_up, v_ffn_conv_w, v_ffn_conv_b, v_ffn_w_down, v_norm_mix_pre, v_norm_mix_post, v_norm_ffn_pre, v_norm_ffn_post):
    wts = dict(ssd_w_in=ssd_w_in, ssd_conv_w=ssd_conv_w, ssd_conv_b=ssd_conv_b, ssd_dt_bias=ssd_dt_bias,
               ssd_a_log=ssd_a_log, ssd_d=ssd_d, ssd_norm_w=ssd_norm_w, ssd_w_out=ssd_w_out, pool_w=pool_w,
               pool_scale=pool_scale, ffn_w_up=ffn_w_up, ffn_conv_w=ffn_conv_w, ffn_conv_b=ffn_conv_b,
               ffn_w_down=ffn_w_down, norm_mix_pre=norm_mix_pre, norm_mix_post=norm_mix_post,
               norm_ffn_pre=norm_ffn_pre, norm_ffn_post=norm_ffn_post)
    mom = dict(ssd_w_in=m_ssd_w_in, ssd_conv_w=m_ssd_conv_w, ssd_conv_b=m_ssd_conv_b, ssd_dt_bias=m_ssd_dt_bias,
               ssd_a_log=m_ssd_a_log, ssd_d=m_ssd_d, ssd_norm_w=m_ssd_norm_w, ssd_w_out=m_ssd_w_out, pool_w=m_pool_w,
               pool_scale=m_pool_scale, ffn_w_up=m_ffn_w_up, ffn_conv_w=m_ffn_conv_w, ffn_conv_b=m_ffn_conv_b,
               ffn_w_down=m_ffn_w_down, norm_mix_pre=m_norm_mix_pre, norm_mix_post=m_norm_mix_post,
               norm_ffn_pre=m_norm_ffn_pre, norm_ffn_post=m_norm_ffn_post)
    var = dict(ssd_w_in=v_ssd_w_in, ssd_conv_w=v_ssd_conv_w, ssd_conv_b=v_ssd_conv_b, ssd_dt_bias=v_ssd_dt_bias,
               ssd_a_log=v_ssd_a_log, ssd_d=v_ssd_d, ssd_norm_w=v_ssd_norm_w, ssd_w_out=v_ssd_w_out, pool_w=v_pool_w,
               pool_scale=v_pool_scale, ffn_w_up=v_ffn_w_up, ffn_conv_w=v_ffn_conv_w, ffn_conv_b=v_ffn_conv_b,
               ffn_w_down=v_ffn_w_down, norm_mix_pre=v_norm_mix_pre, norm_mix_post=v_norm_mix_post,
               norm_ffn_pre=v_norm_ffn_pre, norm_ffn_post=v_norm_ffn_post)

    bl, seq, d = x.shape
    t = bl * seq
    depth = norm_mix_pre.shape[0]
    n_ssd = ssd_w_out.shape[0]
    d_inner = ssd_w_out.shape[1] * N_CHIPS
    nheads = d_inner // HEAD_DIM
    hpg = nheads // N_GROUPS
    gw = d_inner // N_GROUPS
    xbc = ssd_conv_w.shape[2] * N_CHIPS
    f2 = ffn_w_up.shape[2] * N_CHIPS
    ff = f2 // 2
    dg = d // 4
    cy = lax.axis_index("c")
    chip = 2 * lax.axis_index("x") + lax.axis_index("y")

    small_shapes = [wts[n].shape for n, _ in SMALL]
    small_axes = [a for _, a in SMALL]
    small_flat = _flatten_shards([wts[n] for n, _ in SMALL], F32)
    small_half = lax.dynamic_index_in_dim(small_flat, cy, 0, keepdims=False)
    small_all = _allgather_halves(small_half, "gather_small")
    conv_w, p_scale, f_conv_w = _unflatten_full(small_all, small_shapes, small_axes)
    def full_shapes(spec, shards):
        return [jax.ShapeDtypeStruct(_full_shape(axis, s.shape), s.dtype) for s, (_, axis) in zip(shards, spec)]

    def row_halves(a):
        return a.reshape((2, a.shape[0] // 2) + a.shape[1:])

    def join_w_in(g):
        return jnp.concatenate([g[:, k] for k in range(N_CHIPS)], axis=-1).reshape(d, -1)

    def join_w_out(g):
        r2 = g.shape[1] // N_CHIPS
        return jnp.concatenate([g[hf, k * r2:(k + 1) * r2] for k in range(N_CHIPS) for hf in range(2)], axis=0)

    ssd_spec = (("ssd_w_in", None), ("ssd_w_out", 0))
    first_shards = [row_halves(wts[n][0].astype(BF16)) for n, _ in ssd_spec]
    stages, counts = _gather_stages(ssd_spec)
    g_in0, g_out0 = _comm_fused(stages, counts, first_shards, full_shapes(ssd_spec, first_shards), "gather_first")
    w_in, w_out = [join_w_in(g_in0)], [join_w_out(g_out0)]
    rest_spec = ssd_spec * (n_ssd - 1) + (("pool_w", 1),) + FFNW
    rest_shards = [row_halves(wts[n][jj].astype(BF16)) for jj in range(1, n_ssd) for n, _ in ssd_spec]
    rest_shards += [wts["pool_w"].astype(BF16)] + [wts[n].astype(BF16) for n, _ in FFNW]
    stages, counts = _gather_stages(rest_spec)
    ffn_gather = _SplitComm(stages, counts, rest_shards + [g_out0],
                            [lax.empty(s.shape, s.dtype) for s in full_shapes(rest_spec, rest_shards)], "gather_rest")
    gather_token = ffn_gather.advance()

    def pad_heads(a):
        lead = a.shape[:-1]
        a = a.reshape(lead + (N_GROUPS, hpg))
        a = jnp.pad(a, [(0, 0)] * len(lead) + [(0, 0), (0, LANES - hpg)])
        return a.reshape(lead + (N_GROUPS * LANES,))

    def unpad_heads(a):
        lead = a.shape[:-1]
        return a.reshape(lead + (N_GROUPS, LANES))[..., :hpg].reshape(lead + (nheads,))

    def group_rows(a, width):
        return jnp.broadcast_to(a.reshape(N_GROUPS, 1, width), (N_GROUPS, 8, width))

    def pad_w_in(w):
        return jnp.concatenate([w[..., :d_inner + xbc], pad_heads(w[..., d_inner + xbc:])], axis=-1)

    w_in_p = [pad_w_in(w_in[0])]
    zw = w_in_p[0].shape[-1]
    w_pool = None

    x2 = x.reshape(t, d)
    tgt2 = loss_target.reshape(t, d)
    w_up = w_down = None

    saved = []
    cur = x2
    tokens = []
    h = _norm_fwd(cur, norm_mix_pre[0:1], BF16, "norm_pre_b", after=[gather_token])
    for i in range(depth):
        j = i // 2
        sv = dict(x_in=cur)
        if i % 2 == 0:
            zx = _mm(h, w_in_p[j], "nn", BF16, "mm_ssd_in", 2048, 512, d).reshape(bl, seq, zw)
            dtr = _mm(h, w_in_p[j][:, d_inner + xbc:], "nn", F32, "mm_ssd_dt", 2048, 512, d).reshape(bl, seq, -1)
            xc, xpre = _ssd_conv_fwd(zx, conv_w[j], ssd_conv_b[j:j + 1], d_inner, "ssd_conv_fwd")
            dtb = group_rows(pad_heads(ssd_dt_bias[j]), LANES)
            alog = group_rows(pad_heads(ssd_a_log[j]), LANES)
            dskip = group_rows(jnp.repeat(ssd_d[j], HEAD_DIM), gw)
            nw = group_rows(ssd_norm_w[j], gw)
            y, yn, st = _ssd_fwd(xc, zx, dtr, dtb, alog, dskip, nw, d_inner, "ssd_fwd")
            if i == 0:
                tokens.append(ffn_gather.advance(after=yn))
            mix = _mm(yn.reshape(t, d_inner), w_out[j], "nn", F32, "mm_ssd_out", 2048, 512, d_inner)
            sv.update(h=h, zx=zx, dtr=dtr, xc=xc, xpre=xpre, y=y, yn=yn, st=st, dtb=dtb, alog=alog, dskip=dskip, nw=nw)
        else:
            mix = _pool_fwd(h.reshape(bl, seq, d), w_pool[j], p_scale[j:j + 1], "pool_fwd").reshape(t, d)
            sv.update(h=h)
        sv.update(mix=mix)
        mid, u = _norm_post_pre(mix, norm_mix_post[i:i + 1], cur, norm_ffn_pre[i:i + 1], BF16, "norm_post_pre_b",
                                after=tokens)
        tokens = []
        if i == 0:
            ffn_gather.advance(after=u)
            rest = ffn_gather.lands()
            for jj in range(1, n_ssd):
                w_in_p.append(pad_w_in(join_w_in(rest[2 * (jj - 1)])))
                w_out.append(join_w_out(rest[2 * (jj - 1) + 1]))
            w_pool, w_up, w_down = rest[2 * (n_ssd - 1):]
        hpre = _mm(u, w_up, "nn", BF16, "mm_up", 2048, 512, d, b_layer=i).reshape(bl, seq, f2)
        act, pre_g, pre_v = _ffn_act_fwd(hpre, f_conv_w[i], ffn_conv_b[i:i + 1], "ffn_act_fwd")
        act = act.reshape(t, ff)
        fo = _mm(act, w_down, "nn", F32, "mm_down", 2048, 512, ff, b_layer=i)
        if i + 1 == depth:
            cur = _norm_fwd(fo, norm_ffn_post[i:i + 1], F32, "norm_post", resid=mid)
        elif i % 2 == 0:
            cur, h = _norm_post_pre(fo, norm_ffn_post[i:i + 1], mid, norm_mix_pre[i + 1:i + 2], F32, "norm_post_pre_f")
        else:
            cur, h = _norm_post_pre(fo, norm_ffn_post[i:i + 1], mid, norm_mix_pre[i + 1:i + 2], BF16, "norm_post_pre_b")
        sv.update(mid=mid, u=u, hpre=hpre, pre_g=pre_g, pre_v=pre_v, act=act, fo=fo)
        saved.append(sv)

    dcur, loss_part = _loss_head(cur, tgt2, "loss_head")

    g = {n: [None] * wts[n].shape[0] for n in WEIGHTS}
    gbuf = dict(up=lax.empty((depth, d, f2), F32), down=lax.empty((depth, ff, d), F32),
                out=lax.empty((n_ssd, d_inner, d), F32), win=lax.empty((n_ssd, d, zw), F32))
    core = cy.reshape(1).astype(jnp.int32)

    def mixer_bwd(i, dmid, dmix, behind=()):
        j = i // 2
        sv = saved[i]
        done = []
        if i % 2 == 0:
            dyn = _mm(dmix, w_out[j], "nt", BF16, "mm_ssd_out_dx", 1024, 1024, d, after=behind)
            gbuf["out"], tok = _mm(sv["yn"].reshape(t, d_inner), dmix, "tn", F32, "mm_ssd_out_dw", 1024, 1024, 2048,
                                   out_buf=(gbuf["out"], j))
            done.append(tok)
            dz, dxs, dbm, dcm, ddt, dnw, dd, dal, dbias = _ssd_bwd(
                sv["xc"], sv["zx"], sv["dtr"], sv["y"], dyn.reshape(bl, seq, d_inner), sv["st"], sv["dtb"], sv["alog"],
                sv["dskip"], sv["nw"], d_inner, "ssd_bwd")
            g["ssd_norm_w"][j] = dnw[:, 0, :].reshape(d_inner)
            g["ssd_d"][j] = dd[:, 0, :hpg].reshape(nheads)
            g["ssd_a_log"][j] = dal[:, 0, :hpg].reshape(nheads)
            g["ssd_dt_bias"][j] = dbias[:, 0, :hpg].reshape(nheads)
            dzx, dcw, dcb = _ssd_conv_bwd(sv["zx"], sv["xpre"], (dxs, dbm, dcm), ddt, dz, conv_w[j], d_inner,
                                          "ssd_conv_bwd")
            g["ssd_conv_w"][j] = dcw
            g["ssd_conv_b"][j] = dcb[0]
            dzx = dzx.reshape(t, zw)
            dh = _mm(dzx, w_in_p[j], "nt", BF16, "mm_ssd_in_dx", 1024, d, zw // 2)
            gbuf["win"], tok = _mm(sv["h"], dzx, "tn", F32, "mm_ssd_in_dw", 1024, zw // 4, 2048, out_buf=(gbuf["win"], j))
            done.append(tok)
        else:
            dh3, g["pool_w"][j], dps = _pool_bwd(sv["h"].reshape(bl, seq, d), dmix.reshape(bl, seq, d), w_pool[j],
                                                 p_scale[j:j + 1], "pool_bwd")
            g["pool_scale"][j] = dps[0]
            dh = dh3.reshape(t, d)
        if i == 0:
            dx_in, g["norm_mix_pre"][i] = _norm_bwd(sv["x_in"], norm_mix_pre[i:i + 1], dh, F32, "norm_bwd_r", resid=dmid,
                                                    after=done)
            return dx_in, None
        dx_in, dfo_prev, g["norm_mix_pre"][i], g["norm_ffn_post"][i - 1] = _norm_bwd2(
            sv["x_in"], norm_mix_pre[i:i + 1], dh, dmid, saved[i - 1]["fo"], norm_ffn_post[i - 1:i], BF16,
            "norm_bwd_in_post", after=done)
        return dx_in, dfo_prev

    ffn_comm = None
    dfo, g["norm_ffn_post"][depth - 1] = _norm_bwd(saved[depth - 1]["fo"], norm_ffn_post[depth - 1:depth], dcur, BF16,
                                                   "norm_bwd_b")
    for i in reversed(range(depth)):
        sv = saved[i]
        dact = _mm(dfo, w_down, "nt", BF16, "mm_down_dx", 1024, ff // 2, d, b_layer=i)
        gbuf["down"], tok_down = _mm(sv["act"], dfo, "tn", F32, "mm_down_dw", ff // 2, 1024, 2048,
                                     out_buf=(gbuf["down"], i))
        dhg, dhv, dcw, dcb = _ffn_act_bwd(sv["hpre"], sv["pre_g"], sv["pre_v"], dact.reshape(bl, seq, ff), f_conv_w[i],
                                          "ffn_act_bwd")
        g["ffn_conv_w"][i] = dcw
        g["ffn_conv_b"][i] = dcb[0]
        dhs = [dhg.reshape(t, ff), dhv.reshape(t, ff)]
        du = _mm(dhs, w_up, "nt", BF16, "mm_up_dx", 1024, d, ff, b_layer=i)
        gbuf["up"], tok_up = _mm(sv["u"], dhs, "tn", F32, "mm_up_dw", 1024, ff // 2, 2048, out_buf=(gbuf["up"], i))
        def pre_post(behind, i=i, sv=sv, du=du, dcur=dcur):
            return _norm_bwd2(sv["mid"], norm_ffn_pre[i:i + 1], du, dcur, sv["mix"], norm_mix_post[i:i + 1],
                              BF16 if i % 2 == 0 else F32,
                              "norm_bwd_pre_post_b" if i % 2 == 0 else "norm_bwd_pre_post_f", after=behind)

        if i > 0:
            dmid, dmix, g["norm_ffn_pre"][i], g["norm_mix_post"][i] = pre_post([tok_down, tok_up])
            dcur, dfo = mixer_bwd(i, dmid, dmix)
        else:
            held = []

            def during(token):
                held.extend(pre_post([tok_down, tok_up, token]))
                return held[0]

            ffn_comm, ffn_token = _reduce_begin(FFNW, [gbuf["up"], gbuf["down"]], core, "ffn", during=during)
            dmid, dmix, g["norm_ffn_pre"][i], g["norm_mix_post"][i] = held
            dcur, dfo = mixer_bwd(i, dmid, dmix, behind=[ffn_token])

    grad_x = dcur.reshape(bl, seq, d)
    for n in ("norm_mix_pre", "norm_mix_post", "norm_ffn_pre", "norm_ffn_post"):
        g[n] = [a[0] for a in g[n]]
    small_names = [n for n, _ in SMALL] + list(REPL)
    full = {n: jnp.stack(g[n], axis=0) for n in small_names}

    g_in = jnp.concatenate([gbuf["win"][..., :d_inner + xbc], unpad_heads(gbuf["win"][..., d_inner + xbc:])], axis=-1)
    g_in_cm = jnp.swapaxes(g_in.reshape(n_ssd, d, N_CHIPS, -1), 1, 2)
    vec = jnp.concatenate([full[n].reshape(-1) for n in small_names] + [loss_part[0, :1]])
    nvec = vec.shape[0]
    vrows = 16 * ((nvec + 16 * FLAT_COLS - 1) // (16 * FLAT_COLS))
    vec = jnp.pad(vec, (0, vrows * FLAT_COLS - nvec)).reshape(vrows, FLAT_COLS)
    stages, counts = _gather8_stages()
    small_comm = _SplitComm(stages, counts, [vec], [lax.empty((8, vrows, FLAT_COLS), F32)], "gather_small_grads")
    small_token = small_comm.advance()
    mix_comm, mix_token = _reduce_begin(MIXW, [g_in_cm, gbuf["out"], jnp.stack(g["pool_w"], axis=0)], core, "mixers",
                                        riders=[small_token])

    grads, deltas, new_m, new_v = {}, {}, {}, {}

    def adamw(n, gr):
        shp = wts[n].shape
        two = (math.prod(shp[:-1]), shp[-1])
        dl, mn, vn = _adamw(wts[n].reshape(two), gr.reshape(two), mom[n].reshape(two), var[n].reshape(two),
                            "adamw_" + n)
        grads[n], deltas[n], new_m[n], new_v[n] = gr, dl.reshape(shp), mn.reshape(shp), vn.reshape(shp)
        return dl

    small_comm.advance(after=mix_token)
    tot = _sum8(small_comm.lands()[0], "sum_small").reshape(-1)
    small_grads, off = {}, 0
    for n in small_names:
        cnt = math.prod(full[n].shape)
        small_grads[n] = tot[off:off + cnt].reshape(full[n].shape)
        off += cnt
    loss = tot[off]
    for n, ax in SMALL:
        w = wts[n].shape[ax]
        small_grads[n] = lax.dynamic_slice_in_dim(small_grads[n], chip * w, w, axis=ax)

    behind = [adamw(n, small_grads[n]) for n in small_names][-1:]
    ffn_grads = _reduce_finish(FFNW, ffn_comm, core, "ffn", after=mix_token)
    behind += [adamw(n, gr) for gr, (n, _) in zip(ffn_grads, FFNW)]
    mix_grads = _reduce_finish(MIXW, mix_comm, core, "mixers", after=behind)
    for gr, (n, _) in zip(mix_grads, MIXW):
        adamw(n, gr)

    return (loss, grad_x, *[grads[n] for n in WEIGHTS], *[deltas[n] for n in WEIGHTS],
            *[new_m[n] for n in WEIGHTS], *[new_v[n] for n in WEIGHTS])
```

```python
import functools
import math

import jax
import jax.numpy as jnp
from jax import lax
from jax.experimental import pallas as pl
from jax.experimental.pallas import tpu as pltpu

F32 = jnp.float32
BF16 = jnp.bfloat16
MESH = pl.DeviceIdType.MESH
ANY = pl.BlockSpec(memory_space=pl.ANY)

HEAD_DIM = 64
D_STATE = 128
CHUNK = 128
N_GROUPS = 4
SSD_CONV = 4
FFN_CONV = 3
EPS = 1e-6
N_CHIPS = 4
LANES = 128
FLAT_COLS = 1024

ADAM_LR = 0.001
ADAM_B1 = 0.9
ADAM_B2 = 0.999
ADAM_EPS = 1e-08
ADAM_WD = 0.01
ADAM_STEP = 10

VMEM_LIMIT_BYTES = 56 * 1024 * 1024


def _params(sem=None):
    kw = dict(vmem_limit_bytes=VMEM_LIMIT_BYTES)
    if sem is not None:
        kw["dimension_semantics"] = sem
    return pltpu.CompilerParams(**kw)


def _sigmoid(x):
    return 0.5 * jnp.tanh(0.5 * x) + 0.5


def _softplus(x):
    return jnp.maximum(x, 0.0) + jnp.log(1.0 + jnp.exp(-jnp.abs(x)))


def _dot(a, b, dn):
    return lax.dot_general(a, b, (dn, ((), ())), preferred_element_type=F32)


def _nn(a, b):
    return _dot(a, b, ((1,), (0,)))


def _nt(a, b):
    return _dot(a, b, ((1,), (1,)))


def _tn(a, b):
    return _dot(a, b, ((0,), (0,)))


def _split(x, parts):
    out = []
    r = x
    for _ in range(parts):
        p = r.astype(BF16)
        out.append(p)
        r = r - p.astype(F32)
    return out


def _sel_left(sel, x, parts=3):
    n = x.shape[1]
    r = _nn(sel, jnp.concatenate(_split(x, parts), axis=1))
    out = r[:, 0:n]
    for i in range(1, parts):
        out = out + r[:, i * n:(i + 1) * n]
    return out


def _sel_right(x, sel_stacked, parts=3):
    return _nn(jnp.concatenate(_split(x, parts), axis=1), sel_stacked)


def _mm(a, b, dims, out_dtype, name, tm, tn, tk, b_layer=None, out_buf=None, after=()):
    a_list = list(a) if isinstance(a, (list, tuple)) else [a]
    b_list = list(b) if isinstance(b, (list, tuple)) else [b]
    if dims in ("nn", "nt"):
        assert len(b_list) == 1
        m = a_list[0].shape[0]
        segs = [x.shape[1] for x in a_list]
        k = sum(segs)
        bshape = b_list[0].shape[-2:]
        n = bshape[1] if dims == "nn" else bshape[0]
        assert (bshape[0] if dims == "nn" else bshape[1]) == k
    else:
        assert len(a_list) == 1 and b_layer is None
        k, m = a_list[0].shape
        segs = [x.shape[1] for x in b_list]
        n = sum(segs)
    nseg = len(segs)
    tm, tn = min(tm, m), min(tn, n)
    if dims == "tn":
        tk = min(tk, k)
        tn = min(tn, min(segs))
        units = [tn] * nseg
        nk = k // tk
        assert k % tk == 0
    else:
        units = [min(u, s) for u, s in zip(tk if isinstance(tk, (list, tuple)) else [tk] * nseg, segs)]
        nk = sum(s // u for s, u in zip(segs, units))
    assert m % tm == 0 and n % tn == 0 and all(s % u == 0 for s, u in zip(segs, units)), (name, m, n, k, segs, units)
    counts = [s // u for s, u in zip(segs, units)]
    starts = [sum(counts[:s]) for s in range(nseg)]
    assert all(sum(segs[:s]) % units[s] == 0 for s in range(nseg)), (name, segs, units)
    first_block = [sum(segs[:s]) // units[s] for s in range(nseg)]
    dn = {"nn": ((1,), (0,)), "nt": ((1,), (1,)), "tn": ((0,), (0,))}[dims]

    same = len(set(units)) == 1
    nb_ops = len(b_list) if dims == "tn" else (1 if same else nseg)

    def body(*refs):
        a_refs = refs[:len(a_list)]
        b_refs = refs[len(a_list):len(a_list) + nb_ops]
        rest = refs[len(a_list) + nb_ops + (0 if out_buf is None else 1) + len(after):]
        o_ref = rest[0]
        if out_buf is not None:
            rest[1][...] = jnp.zeros((8, LANES), F32)
            rest = rest[1:]
        acc = rest[1] if nk > 1 else None
        kk = pl.program_id(2)
        sel = kk if dims != "tn" else pl.program_id(1)

        def step(a_ref, b_ref):
            p = _dot(a_ref[...].astype(BF16), b_ref[...].astype(BF16), dn)
            if nk == 1:
                o_ref[...] = p.astype(out_dtype)
                return

            @pl.when(kk == 0)
            def _():
                acc[...] = p

            @pl.when(kk > 0)
            def _():
                acc[...] += p

        if nseg == 1:
            step(a_refs[0], b_refs[0])
        else:
            for s in range(nseg):
                @pl.when(jnp.logical_and(sel >= starts[s], sel < starts[s] + counts[s]))
                def _(s=s):
                    step(a_refs[s] if dims != "tn" else a_refs[0], b_refs[s if nb_ops > 1 else 0])

        if nk > 1:
            @pl.when(kk == nk - 1)
            def _():
                o_ref[...] = acc[...].astype(out_dtype)

    def seg_index(v, s):
        return v if nseg == 1 else jnp.clip(v - starts[s], 0, counts[s] - 1)

    lead = () if b_layer is None else (b_layer,)
    none = () if b_layer is None else (None,)
    def b_block(kk, s):
        return kk if same else first_block[s] + seg_index(kk, s)

    if dims == "nn":
        a_specs = [pl.BlockSpec((tm, units[s]), lambda i, j, kk, s=s: (i, seg_index(kk, s))) for s in range(nseg)]
        b_specs = [pl.BlockSpec(none + (units[s], tn), lambda i, j, kk, s=s: lead + (b_block(kk, s), j))
                   for s in range(nb_ops)]
    elif dims == "nt":
        a_specs = [pl.BlockSpec((tm, units[s]), lambda i, j, kk, s=s: (i, seg_index(kk, s))) for s in range(nseg)]
        b_specs = [pl.BlockSpec(none + (tn, units[s]), lambda i, j, kk, s=s: lead + (j, b_block(kk, s)))
                   for s in range(nb_ops)]
    else:
        a_specs = [pl.BlockSpec((tk, tm), lambda i, j, kk: (kk, i))]
        b_specs = [pl.BlockSpec((tk, tn), lambda i, j, kk, s=s: (kk, seg_index(j, s))) for s in range(nseg)]
    args = a_list + (b_list * nb_ops if dims != "tn" else b_list)
    in_specs = a_specs + b_specs
    aliases = {}
    if out_buf is None:
        out_shape = jax.ShapeDtypeStruct((m, n), out_dtype)
        out_spec = pl.BlockSpec((tm, tn), lambda i, j, kk: (i, j))
    else:
        buf, slab = out_buf
        assert buf.shape[1:] == (m, n) and buf.dtype == out_dtype
        out_shape = (jax.ShapeDtypeStruct(buf.shape, out_dtype), jax.ShapeDtypeStruct((8, LANES), F32))
        out_spec = (pl.BlockSpec((None, tm, tn), lambda i, j, kk: (slab, i, j)),
                    pl.BlockSpec((8, LANES), lambda i, j, kk: (0, 0)))
        aliases = {len(args): 0}
        args = args + [buf]
        in_specs = in_specs + [ANY]
    after = [x for x in after if x is not None]
    args = args + after
    in_specs = in_specs + [ANY] * len(after)
    return pl.pallas_call(
        body,
        out_shape=out_shape,
        grid=(m // tm, n // tn, nk),
        in_specs=in_specs,
        out_specs=out_spec,
        scratch_shapes=[] if nk == 1 else [pltpu.VMEM((tm, tn), F32)],
        input_output_aliases=aliases,
        compiler_params=_params(("parallel", "parallel", "arbitrary") if out_buf is None else ("arbitrary",) * 3),
        name=name,
    )(*args)


def _row_tile(t, want):
    tm = min(want, t)
    assert t % tm == 0
    return tm


def _norm_fwd(x, w, out_dtype, name, resid=None, after=()):
    t, d = x.shape
    tm = _row_tile(t, 512)
    after = [a for a in after if a is not None]

    def body(*refs):
        refs = refs[:len(refs) - 1 - len(after)] + refs[len(refs) - 1:]
        if resid is None:
            x_ref, w_ref, o_ref = refs
        else:
            x_ref, w_ref, r_ref, o_ref = refs
        xv = x_ref[...]
        r = lax.rsqrt(jnp.mean(xv * xv, axis=-1, keepdims=True) + EPS)
        y = (xv * r) * w_ref[...]
        if resid is not None:
            y = r_ref[...] + y
        o_ref[...] = y.astype(out_dtype)

    row = pl.BlockSpec((tm, d), lambda i: (i, 0))
    vec = pl.BlockSpec((1, d), lambda i: (0, 0))
    args = [x, w] + ([] if resid is None else [resid]) + after
    return pl.pallas_call(
        body, out_shape=jax.ShapeDtypeStruct((t, d), out_dtype), grid=(t // tm,),
        in_specs=[row, vec] + ([] if resid is None else [row]) + [ANY] * len(after), out_specs=row,
        compiler_params=_params(("parallel",)), name=name)(*args)


def _norm_post_pre(m, w_post, resid, w_pre, pre_dtype, name, after=()):
    t, d = m.shape
    tm = _row_tile(t, 512)
    after = [a for a in after if a is not None]

    def body(m_ref, w1_ref, r_ref, w2_ref, *rest):
        x_ref, u_ref = rest[len(after):]
        mv = m_ref[...]
        r1 = lax.rsqrt(jnp.mean(mv * mv, axis=-1, keepdims=True) + EPS)
        xv = r_ref[...] + (mv * r1) * w1_ref[...]
        x_ref[...] = xv
        r2 = lax.rsqrt(jnp.mean(xv * xv, axis=-1, keepdims=True) + EPS)
        u_ref[...] = ((xv * r2) * w2_ref[...]).astype(pre_dtype)

    row = pl.BlockSpec((tm, d), lambda i: (i, 0))
    vec = pl.BlockSpec((1, d), lambda i: (0, 0))
    return pl.pallas_call(
        body, out_shape=(jax.ShapeDtypeStruct((t, d), F32), jax.ShapeDtypeStruct((t, d), pre_dtype)), grid=(t // tm,),
        in_specs=[row, vec, row, vec] + [ANY] * len(after), out_specs=(row, row),
        compiler_params=_params(("parallel",)), name=name)(m, w_post, resid, w_pre, *after)


def _norm_bwd(src, w, dy, out_dtype, name, resid=None, after=()):
    t, d = src.shape
    tm = _row_tile(t, 512)
    after = [a for a in after if a is not None]

    def body(*refs):
        refs = refs[:len(refs) - 2 - len(after)] + refs[len(refs) - 2:]
        if resid is None:
            x_ref, w_ref, g_ref, o_ref, dw_ref = refs
        else:
            x_ref, w_ref, g_ref, r_ref, o_ref, dw_ref = refs
        xv = x_ref[...]
        g = g_ref[...].astype(F32)
        r = lax.rsqrt(jnp.mean(xv * xv, axis=-1, keepdims=True) + EPS)
        xh = xv * r
        gh = g * w_ref[...]
        mean = jnp.mean(gh * xh, axis=-1, keepdims=True)
        dx = r * (gh - xh * mean)
        if resid is not None:
            dx = r_ref[...] + dx
        o_ref[...] = dx.astype(out_dtype)
        part = jnp.sum(g * xh, axis=0, keepdims=True)

        @pl.when(pl.program_id(0) == 0)
        def _():
            dw_ref[...] = part

        @pl.when(pl.program_id(0) > 0)
        def _():
            dw_ref[...] += part

    row = pl.BlockSpec((tm, d), lambda i: (i, 0))
    vec = pl.BlockSpec((1, d), lambda i: (0, 0))
    args = [src, w, dy] + ([] if resid is None else [resid]) + after
    return pl.pallas_call(
        body,
        out_shape=(jax.ShapeDtypeStruct((t, d), out_dtype), jax.ShapeDtypeStruct((1, d), F32)),
        grid=(t // tm,),
        in_specs=[row, vec, row] + ([] if resid is None else [row]) + [ANY] * len(after),
        out_specs=(row, vec),
        compiler_params=_params(("arbitrary",)), name=name)(*args)


def _norm_bwd2(src1, w1, dy1, resid, src2, w2, out2_dtype, name, after=()):
    t, d = src1.shape
    tm = _row_tile(t, 512)
    after = [a for a in after if a is not None]

    def back(xv, w, g):
        r = lax.rsqrt(jnp.mean(xv * xv, axis=-1, keepdims=True) + EPS)
        xh = xv * r
        gh = g * w
        return r * (gh - xh * jnp.mean(gh * xh, axis=-1, keepdims=True)), jnp.sum(g * xh, axis=0, keepdims=True)

    def body(x1_ref, w1_ref, g1_ref, r_ref, x2_ref, w2_ref, *rest):
        d1_ref, d2_ref, dw1_ref, dw2_ref = rest[len(after):]
        d1, p1 = back(x1_ref[...], w1_ref[...], g1_ref[...].astype(F32))
        d1 = r_ref[...] + d1
        d1_ref[...] = d1
        d2, p2 = back(x2_ref[...], w2_ref[...], d1)
        d2_ref[...] = d2.astype(out2_dtype)

        @pl.when(pl.program_id(0) == 0)
        def _():
            dw1_ref[...] = p1
            dw2_ref[...] = p2

        @pl.when(pl.program_id(0) > 0)
        def _():
            dw1_ref[...] += p1
            dw2_ref[...] += p2

    row = pl.BlockSpec((tm, d), lambda i: (i, 0))
    vec = pl.BlockSpec((1, d), lambda i: (0, 0))
    return pl.pallas_call(
        body,
        out_shape=(jax.ShapeDtypeStruct((t, d), F32), jax.ShapeDtypeStruct((t, d), out2_dtype),
                   jax.ShapeDtypeStruct((1, d), F32), jax.ShapeDtypeStruct((1, d), F32)),
        grid=(t // tm,),
        in_specs=[row, vec, row, row, row, vec] + [ANY] * len(after),
        out_specs=(row, row, vec, vec),
        compiler_params=_params(("arbitrary",)), name=name)(src1, w1, dy1, resid, src2, w2, *after)


def _loss_head(y, target, name):
    t, d = y.shape
    tm = _row_tile(t, 512)

    def body(y_ref, t_ref, dy_ref, l_ref):
        e = y_ref[...] - t_ref[...]
        dy_ref[...] = e * (1.0 / d)
        col = jnp.sum(e * e, axis=0, keepdims=True)
        s = jnp.sum(col, axis=1, keepdims=True) * (0.5 / d)
        part = jnp.broadcast_to(s, (1, LANES))

        @pl.when(pl.program_id(0) == 0)
        def _():
            l_ref[...] = part

        @pl.when(pl.program_id(0) > 0)
        def _():
            l_ref[...] += part

    row = pl.BlockSpec((tm, d), lambda i: (i, 0))
    return pl.pallas_call(
        body,
        out_shape=(jax.ShapeDtypeStruct((t, d), F32), jax.ShapeDtypeStruct((1, LANES), F32)),
        grid=(t // tm,), in_specs=[row, row],
        out_specs=(row, pl.BlockSpec((1, LANES), lambda i: (0, 0))),
        compiler_params=_params(("arbitrary",)), name=name)(y, target)


def _window(ref, c, rows, seq, before, after, keep=None):
    r0 = pl.multiple_of(c * rows, rows)
    parts = []
    if before:
        h0 = pl.multiple_of(jnp.maximum(r0 - before, 0), before)
        halo = ref[pl.ds(h0, before), :].astype(F32)
        halo = halo if keep is None else halo[before - keep:, :]
        parts.append(jnp.where(c > 0, halo, 0.0))
    parts.append(ref[pl.ds(r0, rows), :].astype(F32))
    if after:
        h1 = pl.multiple_of(jnp.minimum(r0 + rows, seq - after), after)
        halo = ref[pl.ds(h1, after), :].astype(F32)
        halo = halo if keep is None else halo[:keep, :]
        parts.append(jnp.where(c < seq // rows - 1, halo, 0.0))
    return parts[0] if len(parts) == 1 else jnp.concatenate(parts, axis=0)


def _lag(x, k):
    return pltpu.roll(x, k, 0) if k else x


def _lead(x, k):
    return pltpu.roll(x, x.shape[0] - k, 0) if k else x


SHIFT_ROWS = 128
POOL_ROWS = 1024
SHIFT_COLS = 256


HALO = 16
KEEP = 8


def _conv3(ext, w, bias):
    acc = bias + w[2:3, :] * ext[KEEP:, :]
    acc = acc + w[1:2, :] * _lag(ext, 1)[KEEP:, :]
    return acc + w[0:1, :] * _lag(ext, 2)[KEEP:, :]


def _ffn_act_fwd(hpre, cw, cb, name):
    b, seq, f2 = hpre.shape
    cbk = SHIFT_COLS
    nj = f2 // (2 * cbk)
    rows = min(SHIFT_ROWS, seq)

    def body(g_ref, v_ref, wg_ref, wv_ref, bg_ref, bv_ref, o_ref, pg_ref, pv_ref):
        def chunk(c, carry):
            gate = _conv3(_window(g_ref, c, rows, seq, HALO, 0, KEEP), wg_ref[...], bg_ref[...])
            val = _conv3(_window(v_ref, c, rows, seq, HALO, 0, KEEP), wv_ref[...], bv_ref[...])
            a = gate * _sigmoid(gate) * val
            here = pl.ds(pl.multiple_of(c * rows, rows), rows)
            o_ref[here, :] = a.astype(BF16)
            pg_ref[here, :] = gate.astype(BF16)
            pv_ref[here, :] = val.astype(BF16)
            return carry

        lax.fori_loop(0, seq // rows, chunk, 0)

    blk = lambda off: pl.BlockSpec((None, seq, cbk), lambda i, j: (i, 0, j + off))
    wsp = lambda r, off: pl.BlockSpec((r, cbk), lambda i, j: (0, j + off))
    half = jax.ShapeDtypeStruct((b, seq, f2 // 2), BF16)
    return pl.pallas_call(
        body, out_shape=(half, half, half), grid=(b, nj),
        in_specs=[blk(0), blk(nj), wsp(FFN_CONV, 0), wsp(FFN_CONV, nj), wsp(1, 0), wsp(1, nj)],
        out_specs=(blk(0), blk(0), blk(0)),
        compiler_params=_params(("parallel", "parallel")), name=name)(hpre, hpre, cw, cw, cb, cb)


def _ffn_act_bwd(hpre, pre_g, pre_v, da, cw, name):
    b, seq, f2 = hpre.shape
    cbk = SHIFT_COLS
    nj = f2 // (2 * cbk)
    rows = min(SHIFT_ROWS, seq)

    def body(g_ref, v_ref, pg_ref, pv_ref, da_ref, wg_ref, wv_ref, og_ref, ov_ref, dwg_ref, dwv_ref, dbg_ref, dbv_ref):
        wg, wv = wg_ref[...], wv_ref[...]

        def back(dpre, w, o_ref, x_ref, c, carry):
            here = pl.ds(pl.multiple_of(c * rows, rows), rows)
            leads = [dpre, _lead(dpre, 1), _lead(dpre, 2)]
            dx = w[2:3, :] * leads[0] + w[1:2, :] * leads[1] + w[0:1, :] * leads[2]
            o_ref[here, :] = dx[:rows, :].astype(BF16)
            x0 = x_ref[here, :].astype(F32)
            return tuple(carry[k] + jnp.sum(leads[k][:rows, :] * x0, axis=0, keepdims=True) for k in range(FFN_CONV)) + (
                carry[FFN_CONV] + jnp.sum(dpre[:rows, :], axis=0, keepdims=True),)

        def chunk(c, carry):
            cg, cv = carry
            gate = _window(pg_ref, c, rows, seq, 0, HALO, KEEP)
            val = _window(pv_ref, c, rows, seq, 0, HALO, KEEP)
            dav = _window(da_ref, c, rows, seq, 0, HALO, KEEP)
            sg = _sigmoid(gate)
            cg = back(dav * val * (sg * (1.0 + gate * (1.0 - sg))), wg, og_ref, g_ref, c, cg)
            cv = back(dav * (gate * sg), wv, ov_ref, v_ref, c, cv)
            return cg, cv

        z = jnp.zeros((1, cbk), F32)
        cg, cv = lax.fori_loop(0, seq // rows, chunk, ((z,) * (FFN_CONV + 1), (z,) * (FFN_CONV + 1)))
        dwg = jnp.concatenate([cg[2], cg[1], cg[0]], axis=0)
        dwv = jnp.concatenate([cv[2], cv[1], cv[0]], axis=0)

        @pl.when(pl.program_id(1) == 0)
        def _():
            dwg_ref[...] = dwg
            dwv_ref[...] = dwv
            dbg_ref[...] = cg[FFN_CONV]
            dbv_ref[...] = cv[FFN_CONV]

        @pl.when(pl.program_id(1) > 0)
        def _():
            dwg_ref[...] += dwg
            dwv_ref[...] += dwv
            dbg_ref[...] += cg[FFN_CONV]
            dbv_ref[...] += cv[FFN_CONV]

    blk = lambda off: pl.BlockSpec((None, seq, cbk), lambda j, i: (i, 0, j + off))
    wsp = lambda r, off: pl.BlockSpec((r, cbk), lambda j, i: (0, j + off))
    half = jax.ShapeDtypeStruct((b, seq, f2 // 2), BF16)
    dwshape = jax.ShapeDtypeStruct((FFN_CONV, f2 // 2), F32)
    dbshape = jax.ShapeDtypeStruct((1, f2 // 2), F32)
    dg, dv, dwg, dwv, dbg, dbv = pl.pallas_call(
        body,
        out_shape=(half, half, dwshape, dwshape, dbshape, dbshape),
        grid=(nj, b),
        in_specs=[blk(0), blk(nj), blk(0), blk(0), blk(0), wsp(FFN_CONV, 0), wsp(FFN_CONV, nj)],
        out_specs=(blk(0), blk(0), wsp(FFN_CONV, 0), wsp(FFN_CONV, 0), wsp(1, 0), wsp(1, 0)),
        compiler_params=_params(("parallel", "arbitrary")), name=name)(hpre, hpre, pre_g, pre_v, da, cw, cw)
    return dg, dv, jnp.concatenate([dwg, dwv], axis=1), jnp.concatenate([dbg, dbv], axis=1)


def _ssd_conv_fwd(zx, cw, cb, d_inner, name):
    b, seq, _ = zx.shape
    xbc = cw.shape[1]
    cbk = SHIFT_COLS
    off = d_inner // cbk
    rows = min(SHIFT_ROWS, seq)

    def body(h_ref, w_ref, b_ref, o_ref, p_ref):
        w = w_ref[...]
        bias = b_ref[...]

        def chunk(c, carry):
            ext = _window(h_ref, c, rows, seq, HALO, 0, KEEP)
            acc = bias + w[3:4, :] * ext[KEEP:, :]
            for k in range(1, SSD_CONV):
                acc = acc + w[3 - k:4 - k, :] * _lag(ext, k)[KEEP:, :]
            here = pl.ds(pl.multiple_of(c * rows, rows), rows)
            o_ref[here, :] = acc * _sigmoid(acc)
            p_ref[here, :] = acc.astype(BF16)
            return carry

        lax.fori_loop(0, seq // rows, chunk, 0)

    blk = pl.BlockSpec((None, seq, cbk), lambda i, j: (i, 0, j))
    return pl.pallas_call(
        body, out_shape=(jax.ShapeDtypeStruct((b, seq, xbc), F32), jax.ShapeDtypeStruct((b, seq, xbc), BF16)),
        grid=(b, xbc // cbk),
        in_specs=[pl.BlockSpec((None, seq, cbk), lambda i, j: (i, 0, j + off)),
                  pl.BlockSpec((SSD_CONV, cbk), lambda i, j: (0, j)),
                  pl.BlockSpec((1, cbk), lambda i, j: (0, j))],
        out_specs=(blk, blk),
        compiler_params=_params(("parallel", "parallel")), name=name)(zx, cw, cb)


def _ssd_conv_bwd(zx, pre, dparts, ddt, dzx, cw, d_inner, name):
    b, seq, zw = zx.shape
    xbc = cw.shape[1]
    cbk = SHIFT_COLS
    off = d_inner // cbk
    rows = min(SHIFT_ROWS, seq)
    nblk = [p.shape[2] // cbk for p in dparts]
    first = [sum(nblk[:s]) for s in range(len(dparts))]
    nconv = xbc // cbk
    ncopy = ddt.shape[2] // cbk
    assert sum(nblk) == nconv and (off + nconv + ncopy) * cbk == zw and dzx.shape == (b, seq, zw)

    def body(h_ref, p_ref, gx_ref, gb_ref, gc_ref, t_ref, w_ref, z_ref, o_ref, dw_ref, db_ref):
        j = pl.program_id(0)

        @pl.when(j < nconv)
        def _():
            conv(h_ref, p_ref, gx_ref, gb_ref, gc_ref, w_ref, o_ref, dw_ref, db_ref)

        @pl.when(j >= nconv)
        def _():
            o_ref[...] = t_ref[...]

    def conv(h_ref, p_ref, gx_ref, gb_ref, gc_ref, w_ref, o_ref, dw_ref, db_ref):
        w = w_ref[...]
        j = pl.program_id(0)

        def chunk(c, carry):
            dws, dbias = carry
            here = pl.ds(pl.multiple_of(c * rows, rows), rows)
            pre = _window(p_ref, c, rows, seq, 0, HALO, KEEP)
            s = _sigmoid(pre)
            gsel = jnp.where(j < first[1], _window(gx_ref, c, rows, seq, 0, HALO, KEEP),
                             jnp.where(j < first[2], _window(gb_ref, c, rows, seq, 0, HALO, KEEP),
                                       _window(gc_ref, c, rows, seq, 0, HALO, KEEP)))
            dpre = gsel * (s * (1.0 + pre * (1.0 - s)))
            leads = [dpre] + [_lead(dpre, k) for k in range(1, SSD_CONV)]
            dx = w[3:4, :] * leads[0]
            for k in range(1, SSD_CONV):
                dx = dx + w[3 - k:4 - k, :] * leads[k]
            o_ref[here, :] = dx[:rows, :].astype(BF16)
            x0 = h_ref[here, :].astype(F32)
            dws = tuple(dws[k] + jnp.sum(leads[k][:rows, :] * x0, axis=0, keepdims=True) for k in range(SSD_CONV))
            dbias = dbias + jnp.sum(dpre[:rows, :], axis=0, keepdims=True)
            return dws, dbias

        z = jnp.zeros((1, cbk), F32)
        dws, dbias = lax.fori_loop(0, seq // rows, chunk, ((z,) * SSD_CONV, z))
        dwv = jnp.concatenate([dws[3 - i] for i in range(SSD_CONV)], axis=0)

        @pl.when(pl.program_id(1) == 0)
        def _():
            dw_ref[...] = dwv
            db_ref[...] = dbias

        @pl.when(pl.program_id(1) > 0)
        def _():
            dw_ref[...] += dwv
            db_ref[...] += dbias

    conv_j = lambda j: jnp.minimum(j, nconv - 1)
    return pl.pallas_call(
        body,
        out_shape=(jax.ShapeDtypeStruct((b, seq, zw), BF16), jax.ShapeDtypeStruct((SSD_CONV, xbc), F32),
                   jax.ShapeDtypeStruct((1, xbc), F32)),
        grid=(nconv + ncopy, b),
        in_specs=[pl.BlockSpec((None, seq, cbk), lambda j, i: (i, 0, conv_j(j) + off)),
                  pl.BlockSpec((None, seq, cbk), lambda j, i: (i, 0, conv_j(j)))] + [
                  pl.BlockSpec((None, seq, cbk), lambda j, i, s=s: (i, 0, jnp.clip(j - first[s], 0, nblk[s] - 1)))
                  for s in range(3)] + [
                  pl.BlockSpec((None, seq, cbk), lambda j, i: (i, 0, jnp.clip(j - nconv, 0, ncopy - 1))),
                  pl.BlockSpec((SSD_CONV, cbk), lambda j, i: (0, conv_j(j))),
                  ANY],
        out_specs=(pl.BlockSpec((None, seq, cbk), lambda j, i: (i, 0, j + off)),
                   pl.BlockSpec((SSD_CONV, cbk), lambda j, i: (0, conv_j(j))),
                   pl.BlockSpec((1, cbk), lambda j, i: (0, conv_j(j)))),
        input_output_aliases={7: 0},
        compiler_params=_params(("arbitrary", "arbitrary")), name=name)(zx, pre, *dparts, ddt, cw, dzx)


def _pool_sums(q, g, lead):
    sh = _lead if lead else _lag
    s2 = q + sh(q, 1)
    s4 = s2 + sh(s2, 2)
    s8 = s4 + sh(s4, 4)
    s16 = s8 + sh(s8, 8)
    return jnp.where(g == 0, s2, jnp.where(g == 1, s4, jnp.where(g == 2, s8, s16)))


def _pool_count(r0, n, g, shape):
    t = (r0 + lax.broadcasted_iota(jnp.int32, shape, 0) + 1).astype(F32)
    return jnp.minimum(t, (2 << g).astype(F32))


def _pool_fwd(h, pw, scale, name):
    b, seq, d = h.shape
    dg = d // 4
    rows = min(POOL_ROWS, seq)

    def body(h_ref, w_ref, s_ref, o_ref):
        g = pl.program_id(1)
        wmat = w_ref[...]
        sc = s_ref[...]

        def chunk(c, carry):
            r0 = c * rows
            ext = _window(h_ref, c, rows, seq, 16, 0)
            sums = _pool_sums(ext, g, False)[16:, :]
            mixed = sums / _pool_count(r0, rows, g, (rows, dg)) - ext[16:, :]
            o_ref[pl.ds(pl.multiple_of(r0, rows), rows), :] = _nn(mixed.astype(BF16), wmat) * sc
            return carry

        lax.fori_loop(0, seq // rows, chunk, 0)

    return pl.pallas_call(
        body, out_shape=jax.ShapeDtypeStruct((b, seq, d), F32), grid=(b, 4),
        in_specs=[pl.BlockSpec((None, seq, dg), lambda i, g: (i, 0, g)),
                  pl.BlockSpec((None, dg, dg), lambda i, g: (g, 0, 0)),
                  pl.BlockSpec((1, dg), lambda i, g: (0, g))],
        out_specs=pl.BlockSpec((None, seq, dg), lambda i, g: (i, 0, g)),
        compiler_params=_params(("parallel", "parallel")), name=name)(h, pw, scale)


def _pool_bwd(h, dout, pw, scale, name):
    b, seq, d = h.shape
    dg = d // 4
    rows = min(POOL_ROWS, seq)

    def body(h_ref, g_ref, w_ref, s_ref, o_ref, dw_ref, ds_ref, dw_acc):
        g = pl.program_id(0)
        wmat = w_ref[...]
        sc = s_ref[...]
        dw_acc[...] = jnp.zeros_like(dw_acc)

        def chunk(c, dsc):
            r0 = c * rows
            ext = _window(h_ref, c, rows, seq, 16, 0)
            sums = _pool_sums(ext, g, False)[16:, :]
            mixed = (sums / _pool_count(r0, rows, g, (rows, dg)) - ext[16:, :]).astype(BF16)
            gext = _window(g_ref, c, rows, seq, 0, 16)
            dsc = dsc + jnp.sum(gext[:rows, :] * _nn(mixed, wmat), axis=0, keepdims=True)
            dpre = (gext * sc).astype(BF16)
            dw_acc[...] += _tn(mixed, dpre[:rows, :])
            dmix = _nt(dpre, wmat)
            q = dmix / _pool_count(r0, rows + 16, g, (rows + 16, dg))
            back = _pool_sums(q, g, True)
            o_ref[pl.ds(pl.multiple_of(r0, rows), rows), :] = back[:rows, :] - dmix[:rows, :]
            return dsc

        dsc = lax.fori_loop(0, seq // rows, chunk, jnp.zeros((1, dg), F32))

        @pl.when(pl.program_id(1) == 0)
        def _():
            dw_ref[...] = dw_acc[...]
            ds_ref[...] = dsc

        @pl.when(pl.program_id(1) > 0)
        def _():
            dw_ref[...] += dw_acc[...]
            ds_ref[...] += dsc

    return pl.pallas_call(
        body,
        out_shape=(jax.ShapeDtypeStruct((b, seq, d), F32), jax.ShapeDtypeStruct((4, dg, dg), F32),
                   jax.ShapeDtypeStruct((1, d), F32)),
        grid=(4, b),
        in_specs=[pl.BlockSpec((None, seq, dg), lambda g, i: (i, 0, g)),
                  pl.BlockSpec((None, seq, dg), lambda g, i: (i, 0, g)),
                  pl.BlockSpec((None, dg, dg), lambda g, i: (g, 0, 0)),
                  pl.BlockSpec((1, dg), lambda g, i: (0, g))],
        out_specs=(pl.BlockSpec((None, seq, dg), lambda g, i: (i, 0, g)),
                   pl.BlockSpec((None, dg, dg), lambda g, i: (g, 0, 0)),
                   pl.BlockSpec((1, dg), lambda g, i: (0, g))),
        scratch_shapes=[pltpu.VMEM((dg, dg), F32)],
        compiler_params=_params(("parallel", "arbitrary")), name=name)(h, dout, pw, scale)


def _head_of(channel):
    return jnp.right_shift(channel, HEAD_DIM.bit_length() - 1)


def _ssd_consts(gw):
    q = CHUNK
    row = lax.broadcasted_iota(jnp.int32, (q, q), 0)
    col = lax.broadcasted_iota(jnp.int32, (q, q), 1)
    tril = (row >= col).astype(BF16)
    triu = (row <= col).astype(BF16)
    e = (_head_of(lax.broadcasted_iota(jnp.int32, (LANES, gw), 1))
         == lax.broadcasted_iota(jnp.int32, (LANES, gw), 0)).astype(BF16)
    et = (_head_of(lax.broadcasted_iota(jnp.int32, (gw, LANES), 0))
          == lax.broadcasted_iota(jnp.int32, (gw, LANES), 1)).astype(BF16)
    return row, col, tril, triu, e, et


def _ssd_common(dtr, dtb, alog, gw):
    q = CHUNK
    row, col, tril, triu, e, et = _ssd_consts(gw)
    dt = _softplus(dtr + dtb)
    a_row = -jnp.exp(alog)
    acum = _sel_left(tril, dt * a_row)
    ac_last = jnp.sum(jnp.where(row == q - 1, acum, 0.0), axis=0, keepdims=True)
    eac = jnp.exp(acum)
    de = jnp.exp(ac_last - acum)
    e2 = jnp.concatenate([e, e], axis=0)
    expand = _sel_right(jnp.concatenate([dt, eac, de], axis=0), e2, 2)
    dt_x, eac_x, de_x = expand[0:q], expand[q:2 * q], expand[2 * q:3 * q]
    acum_t = acum.T
    cd_col = jnp.exp(acum_t[:, q - 1:q])
    et3 = jnp.concatenate([et, et, et], axis=1)
    cdmat = _nn(et3, jnp.concatenate(_split(jnp.broadcast_to(cd_col, (LANES, D_STATE)), 3), axis=0))
    consts = dict(row=row, col=col, tril=tril, triu=triu, e=e, et=et)
    return dt, a_row, acum, acum_t, ac_last, eac, de, dt_x, eac_x, de_x, cdmat, consts


def _decay(acum, acum_t, j, row, col):
    diff = acum[:, j:j + 1] - acum_t[j:j + 1, :]
    return jnp.exp(jnp.where(row >= col, diff, -1e30))


def _ssd_fwd(xc, zx, dtr, dtb, alog, dskip, nw, d_inner, name):
    b, seq, xbc = xc.shape
    q = CHUNK
    nc = seq // q
    gw = d_inner // N_GROUPS
    nh = gw // HEAD_DIM
    xb0 = d_inner // D_STATE
    xc0 = xb0 + N_GROUPS

    nb = max(n for n in (4, 2, 1) if b % n == 0)

    def body(x_ref, b_ref, c_ref, z_ref, dtr_ref, dtb_ref, al_ref, dsk_ref, nw_ref, y_ref, yn_ref, st_ref, s_ref):
        @pl.when(pl.program_id(2) == 0)
        def _():
            s_ref[...] = jnp.zeros_like(s_ref)

        for s in range(nb):
            one(s, x_ref.at[s], b_ref.at[s], c_ref.at[s], z_ref.at[s], dtr_ref.at[s], dtb_ref, al_ref, dsk_ref, nw_ref,
                y_ref.at[s], yn_ref.at[s], st_ref.at[s], s_ref.at[s])

    def one(s, x_ref, b_ref, c_ref, z_ref, dtr_ref, dtb_ref, al_ref, dsk_ref, nw_ref, y_ref, yn_ref, st_ref, s_ref):
        prev = s_ref[...]
        st_ref[...] = prev
        x = x_ref[...]
        bm = b_ref[...].astype(BF16)
        cm = c_ref[...].astype(BF16)
        (dt, a_row, acum, acum_t, ac_last, eac, de, dt_x, eac_x, de_x, cdmat, k) = _ssd_common(
            dtr_ref[...], dtb_ref[0:1, :], al_ref[0:1, :], gw)
        xdt = x * dt_x
        xdt_b = xdt.astype(BF16)
        cb = _nt(cm, bm)
        half = _head_of(lax.broadcasted_iota(jnp.int32, (q, LANES), 1))
        pairs = []
        for j in range(nh):
            pc = (j // 2) * LANES
            m = (cb * _decay(acum, acum_t, j, k["row"], k["col"])).astype(BF16)
            yj = jnp.where(half == j % 2, _nn(m, xdt_b[:, pc:pc + LANES]), 0.0)
            if j % 2 == 0:
                pairs.append(yj)
            else:
                pairs[-1] = pairs[-1] + yj
        prev_b = prev.astype(BF16)
        y = dsk_ref[0:1, :] * x + jnp.concatenate(pairs, axis=1) + eac_x * _nt(cm, prev_b)
        s_ref[...] = cdmat * prev + _tn((xdt * de_x).astype(BF16), bm)
        y_ref[...] = y
        z = z_ref[...].astype(F32)
        yg = y * (z * _sigmoid(z))
        r = lax.rsqrt(jnp.mean(yg * yg, axis=-1, keepdims=True) + EPS)
        yn_ref[...] = ((yg * r) * nw_ref[0:1, :]).astype(BF16)

    par = lambda w: pl.BlockSpec((None, 8, w), lambda i, g, c: (g, 0, 0))
    return pl.pallas_call(
        body,
        out_shape=(jax.ShapeDtypeStruct((b, seq, d_inner), F32), jax.ShapeDtypeStruct((b, seq, d_inner), BF16),
                   jax.ShapeDtypeStruct((b, nc, N_GROUPS, gw, D_STATE), F32)),
        grid=(b // nb, N_GROUPS, nc),
        in_specs=[pl.BlockSpec((nb, q, gw), lambda i, g, c: (i, c, g)),
                  pl.BlockSpec((nb, q, D_STATE), lambda i, g, c: (i, c, xb0 + g)),
                  pl.BlockSpec((nb, q, D_STATE), lambda i, g, c: (i, c, xc0 + g)),
                  pl.BlockSpec((nb, q, gw), lambda i, g, c: (i, c, g)),
                  pl.BlockSpec((nb, q, LANES), lambda i, g, c: (i, c, g)),
                  par(LANES), par(LANES), par(gw), par(gw)],
        out_specs=(pl.BlockSpec((nb, q, gw), lambda i, g, c: (i, c, g)),
                   pl.BlockSpec((nb, q, gw), lambda i, g, c: (i, c, g)),
                   pl.BlockSpec((nb, None, None, gw, D_STATE), lambda i, g, c: (i, c, g, 0, 0))),
        scratch_shapes=[pltpu.VMEM((nb, gw, D_STATE), F32)],
        compiler_params=_params(("parallel", "parallel", "arbitrary")), name=name,
    )(xc, xc, xc, zx, dtr, dtb, alog, dskip, nw)


def _ssd_bwd(xc, zx, dtr, y, dyn, st, dtb, alog, dskip, nw, d_inner, name):
    b, seq, xbc = xc.shape
    q = CHUNK
    nc = seq // q
    gw = d_inner // N_GROUPS
    nh = gw // HEAD_DIM
    xb0 = d_inner // D_STATE
    xc0 = xb0 + N_GROUPS

    nb = max(n for n in (4, 2, 1) if b % n == 0)

    def body(x_ref, b_ref, c_ref, z_ref, dtr_ref, y_ref, g_ref, st_ref, dtb_ref, al_ref, dsk_ref, nw_ref,
             dz_ref, dx_ref, db_ref, dc_ref, ddt_ref, dnw_ref, dd_ref, dal_ref, dbias_ref,
             ds_ref, colbuf, rowbuf):
        first = jnp.logical_and(pl.program_id(1) == 0, pl.program_id(2) == 0)

        @pl.when(pl.program_id(2) == 0)
        def _():
            ds_ref[...] = jnp.zeros_like(ds_ref)

        sums = [one(x_ref.at[s], b_ref.at[s], c_ref.at[s], z_ref.at[s], dtr_ref.at[s], y_ref.at[s], g_ref.at[s],
                    st_ref.at[s], dtb_ref, al_ref, dsk_ref, nw_ref, dz_ref.at[s], dx_ref.at[s], db_ref.at[s],
                    dc_ref.at[s], ddt_ref.at[s], ds_ref.at[s], colbuf.at[s], rowbuf.at[s]) for s in range(nb)]
        dnw, dd, dal, dbias = [functools.reduce(lambda p, r: p + r, [sm[i] for sm in sums]) for i in range(4)]

        @pl.when(first)
        def _():
            dnw_ref[...] = jnp.broadcast_to(dnw, (8, gw))
            dd_ref[...] = dd
            dal_ref[...] = jnp.broadcast_to(dal, (8, LANES))
            dbias_ref[...] = jnp.broadcast_to(dbias, (8, LANES))

        @pl.when(jnp.logical_not(first))
        def _():
            dnw_ref[...] += jnp.broadcast_to(dnw, (8, gw))
            dd_ref[...] += dd
            dal_ref[...] += jnp.broadcast_to(dal, (8, LANES))
            dbias_ref[...] += jnp.broadcast_to(dbias, (8, LANES))

    def one(x_ref, b_ref, c_ref, z_ref, dtr_ref, y_ref, g_ref, st_ref, dtb_ref, al_ref, dsk_ref, nw_ref,
            dz_ref, dx_ref, db_ref, dc_ref, ddt_ref, ds_ref, colbuf, rowbuf):
        x = x_ref[...]
        bm = b_ref[...].astype(BF16)
        cm = c_ref[...].astype(BF16)
        z = z_ref[...].astype(F32)
        y = y_ref[...]
        prev = st_ref[...]
        dtr = dtr_ref[...] + dtb_ref[0:1, :]
        (dt, a_row, acum, acum_t, ac_last, eac, de, dt_x, eac_x, de_x, cdmat, k) = _ssd_common(
            dtr_ref[...], dtb_ref[0:1, :], al_ref[0:1, :], gw)
        row, col = k["row"], k["col"]
        et2 = jnp.concatenate([k["et"], k["et"]], axis=0)

        sz = _sigmoid(z)
        silu_z = z * sz
        yg = y * silu_z
        r = lax.rsqrt(jnp.mean(yg * yg, axis=-1, keepdims=True) + EPS)
        xh = yg * r
        dyn = g_ref[...].astype(F32)
        gh = dyn * nw_ref[0:1, :]
        dyg = r * (gh - xh * jnp.mean(gh * xh, axis=-1, keepdims=True))
        dnw = jnp.sum(dyn * xh, axis=0, keepdims=True)
        g = dyg * silu_z
        dz_ref[...] = (dyg * y * (sz * (1.0 + z * (1.0 - sz)))).astype(BF16)
        dd = _sel_right(jnp.broadcast_to(jnp.sum(g * x, axis=0, keepdims=True), (8, gw)), et2, 2)

        xdt = x * dt_x
        xdt_b = xdt.astype(BF16)
        g_b = g.astype(BF16)
        prev_b = prev.astype(BF16)
        cb = _nt(cm, bm)

        cp = _nt(cm, prev_b)
        ge = g * eac_x
        dac = _sel_right(ge * cp, et2, 2)
        ge_b = ge.astype(BF16)
        dcm = _nn(ge_b, prev_b)
        dprev = _tn(ge_b, cm)

        colbuf[...] = jnp.zeros_like(colbuf)
        rowbuf[...] = jnp.zeros_like(rowbuf)
        dcb = jnp.zeros((q, q), F32)
        half = _head_of(lax.broadcasted_iota(jnp.int32, (q, LANES), 1))
        pairs = []
        for j in range(nh):
            pc = (j // 2) * LANES
            dec = _decay(acum, acum_t, j, row, col)
            m = cb * dec
            gj = jnp.where(half == j % 2, g[:, pc:pc + LANES], 0.0).astype(BF16)
            dm = _nt(gj, xdt_b[:, pc:pc + LANES])
            w = dm * m
            colbuf[:, j:j + 1] = jnp.sum(w, axis=1, keepdims=True)
            rowbuf[j:j + 1, :] = jnp.sum(w, axis=0, keepdims=True)
            dcb = dcb + dm * dec
            dj = jnp.where(half == j % 2, _tn(m.astype(BF16), g_b[:, pc:pc + LANES]), 0.0)
            if j % 2 == 0:
                pairs.append(dj)
            else:
                pairs[-1] = pairs[-1] + dj
        dxdt = jnp.concatenate(pairs, axis=1)
        dcb_b = dcb.astype(BF16)
        dcm = dcm + _nn(dcb_b, bm)
        dbm = _tn(dcb_b, cm)

        ds = ds_ref[...]
        ds_b = ds.astype(BF16)
        u = _nt(bm, ds_b)
        dxdt = dxdt + u * de_x
        dde = _sel_right(u * xdt, et2, 2)
        dbm = dbm + _nn((xdt * de_x).astype(BF16), ds_b)
        pm = jnp.concatenate(_split(ds * prev, 2), axis=1)
        t2 = _tn(pm, k["et"])
        dcd_row = jnp.sum(t2[0:D_STATE] + t2[D_STATE:2 * D_STATE], axis=0, keepdims=True)
        last = dcd_row * jnp.exp(ac_last) + jnp.sum(dde * de, axis=0, keepdims=True)
        dac = dac + colbuf[...] - rowbuf[...].T - dde * de + jnp.where(row == q - 1, last, 0.0)
        ds_ref[...] = cdmat * ds + dprev

        dadt = _sel_left(k["triu"], dac)
        ddt = _sel_right(dxdt * x, et2, 2) + dadt * a_row
        dal = jnp.sum(dadt * dt, axis=0, keepdims=True) * a_row
        lane = lax.broadcasted_iota(jnp.int32, (q, LANES), 1)
        ddtr = jnp.where(lane < nh, ddt * _sigmoid(dtr), 0.0)
        ddt_ref[...] = ddtr.astype(BF16)
        dbias = jnp.sum(ddtr, axis=0, keepdims=True)
        dx_ref[...] = dxdt * dt_x + dsk_ref[0:1, :] * g
        db_ref[...] = dbm
        dc_ref[...] = dcm
        return dnw, dd, dal, dbias

    rc = lambda c: nc - 1 - c
    par = lambda w: pl.BlockSpec((None, 8, w), lambda g, i, c: (g, 0, 0))
    blk = lambda w: pl.BlockSpec((nb, q, w), lambda g, i, c: (i, rc(c), g))
    return pl.pallas_call(
        body,
        out_shape=(jax.ShapeDtypeStruct((b, seq, zx.shape[2]), BF16),
                   jax.ShapeDtypeStruct((b, seq, d_inner), F32),
                   jax.ShapeDtypeStruct((b, seq, N_GROUPS * D_STATE), F32),
                   jax.ShapeDtypeStruct((b, seq, N_GROUPS * D_STATE), F32),
                   jax.ShapeDtypeStruct((b, seq, N_GROUPS * LANES), BF16),
                   jax.ShapeDtypeStruct((N_GROUPS, 8, gw), F32),
                   jax.ShapeDtypeStruct((N_GROUPS, 8, LANES), F32),
                   jax.ShapeDtypeStruct((N_GROUPS, 8, LANES), F32),
                   jax.ShapeDtypeStruct((N_GROUPS, 8, LANES), F32)),
        grid=(N_GROUPS, b // nb, nc),
        in_specs=[blk(gw),
                  pl.BlockSpec((nb, q, D_STATE), lambda g, i, c: (i, rc(c), xb0 + g)),
                  pl.BlockSpec((nb, q, D_STATE), lambda g, i, c: (i, rc(c), xc0 + g)),
                  blk(gw),
                  pl.BlockSpec((nb, q, LANES), lambda g, i, c: (i, rc(c), g)),
                  blk(gw), blk(gw),
                  pl.BlockSpec((nb, None, None, gw, D_STATE), lambda g, i, c: (i, rc(c), g, 0, 0)),
                  par(LANES), par(LANES), par(gw), par(gw)],
        out_specs=(blk(gw), blk(gw), blk(D_STATE), blk(D_STATE), blk(LANES),
                   par(gw), par(LANES), par(LANES), par(LANES)),
        scratch_shapes=[pltpu.VMEM((nb, gw, D_STATE), F32), pltpu.VMEM((nb, q, LANES), F32),
                        pltpu.VMEM((nb, LANES, q), F32)],
        compiler_params=_params(("parallel", "arbitrary", "arbitrary")), name=name,
    )(xc, xc, xc, zx, dtr, y, dyn, st, dtb, alog, dskip, nw)


def _adamw(w, g, m, v, name):
    rows, cols = w.shape
    tr = rows
    for cand in (512, 256, 128, 64, 32, 16, 8):
        if rows % cand == 0 and cand * cols * 4 <= 2 * 1024 * 1024:
            tr = cand
            break
    c1 = 1.0 - ADAM_B1 ** ADAM_STEP
    c2 = 1.0 - ADAM_B2 ** ADAM_STEP

    def body(w_ref, g_ref, m_ref, v_ref, d_ref, mo_ref, vo_ref):
        gv = g_ref[...]
        mn = ADAM_B1 * m_ref[...] + (1.0 - ADAM_B1) * gv
        vn = ADAM_B2 * v_ref[...] + (1.0 - ADAM_B2) * (gv * gv)
        mo_ref[...] = mn
        vo_ref[...] = vn
        d_ref[...] = -ADAM_LR * ((mn / c1) / (jnp.sqrt(vn / c2) + ADAM_EPS) + ADAM_WD * w_ref[...])

    spec = pl.BlockSpec((tr, cols), lambda i: (i, 0))
    shp = jax.ShapeDtypeStruct((rows, cols), F32)
    return pl.pallas_call(body, out_shape=(shp, shp, shp), grid=(rows // tr,), in_specs=[spec] * 4,
                          out_specs=(spec,) * 3, compiler_params=_params(("parallel",)), name=name)(w, g, m, v)


def _pick_rows(rows, row_bytes, limit=1 << 20):
    for cand in (2048, 1024, 512, 256, 128, 64, 32, 16):
        if rows % cand == 0 and cand * row_bytes <= limit:
            return cand
    return rows


def _as3d(a, lead):
    return a.reshape(a.shape[:lead] + (-1, a.shape[-1]))


def _pair_sum(g, got, core, name):
    h = got.shape[0]
    g3, got3 = _as3d(g, 1), _as3d(got, 1)
    _, rows, cols = got3.shape
    tr = _pick_rows(rows, cols * 4)

    def body(c_ref, g_ref, r_ref, o_ref):
        o_ref[...] = (g_ref[...] + r_ref[...]).astype(BF16)

    out = pl.pallas_call(
        body, out_shape=jax.ShapeDtypeStruct(got3.shape, BF16),
        grid_spec=pltpu.PrefetchScalarGridSpec(
            num_scalar_prefetch=1, grid=(h, rows // tr),
            in_specs=[pl.BlockSpec((None, tr, cols), lambda l, i, c_ref: (c_ref[0] * h + l, i, 0)),
                      pl.BlockSpec((None, tr, cols), lambda l, i, c_ref: (l, i, 0))],
            out_specs=pl.BlockSpec((None, tr, cols), lambda l, i, c_ref: (l, i, 0))),
        compiler_params=_params(("parallel", "parallel")), name=name)(core, g3, got3)
    return out.reshape(got.shape)


def _sum4(q, core, name):
    q4 = _as3d(q, 2)
    _, h, rows, cols = q4.shape
    tr = _pick_rows(rows, cols * 4)

    def body(c_ref, q0, q1, q2, q3, o_ref):
        o_ref[...] = ((q0[...].astype(F32) + q1[...].astype(F32)) + q2[...].astype(F32)) + q3[...].astype(F32)

    out = pl.pallas_call(
        body, out_shape=jax.ShapeDtypeStruct((2 * h, rows, cols), F32),
        grid_spec=pltpu.PrefetchScalarGridSpec(
            num_scalar_prefetch=1, grid=(h, rows // tr),
            in_specs=[pl.BlockSpec((None, None, tr, cols), lambda l, i, c_ref, k=k: (k, l, i, 0))
                      for k in range(N_CHIPS)],
            out_specs=pl.BlockSpec((None, tr, cols), lambda l, i, c_ref: (c_ref[0] * h + l, i, 0))),
        compiler_params=_params(("parallel", "parallel")), name=name)(core, q4, q4, q4, q4)
    return out.reshape((2 * h,) + q.shape[2:])


def _coords():
    return lax.axis_index("x"), lax.axis_index("y"), lax.axis_index("c")


def _other_chips(x, y):
    return [(1 - x, y), (x, 1 - y), (1 - x, 1 - y)]


def _allgather_halves(src, name):
    rows, cols = src.shape

    def body(x_ref, o_ref, send, recv, local):
        x, y, c = _coords()
        sib = (x, y, 1 - c)
        chips = _other_chips(x, y)

        def slot(h, cx, cy):
            return o_ref.at[h, 2 * cx + cy]

        def copy(kk, dst, to, src_ref):
            return pltpu.make_async_remote_copy(src_ref=src_ref, dst_ref=dst, send_sem=send.at[kk],
                                                recv_sem=recv.at[kk], device_id=to, device_id_type=MESH)

        mine = pltpu.make_async_copy(x_ref, slot(c, x, y), local)
        mine.start()
        first = [copy(0, slot(c, x, y), sib, x_ref)]
        first += [copy(1 + j, slot(c, x, y), (*chip, c), x_ref) for j, chip in enumerate(chips)]
        for cp in first:
            cp.start()
        passed = [copy(4 + j, slot(c, *chip), sib, slot(c, *chip)) for j, chip in enumerate(chips)]
        for j, chip in enumerate(chips):
            copy(1 + j, slot(c, *chip), (x, y, c), x_ref).wait_recv()
            passed[j].start()
        copy(0, slot(1 - c, x, y), (x, y, c), x_ref).wait_recv()
        for j, chip in enumerate(chips):
            copy(4 + j, slot(1 - c, *chip), (x, y, c), x_ref).wait_recv()
        for cp in first + passed:
            cp.wait_send()
        mine.wait()

    return pl.pallas_call(
        body, out_shape=jax.ShapeDtypeStruct((2, N_CHIPS, rows, cols), src.dtype),
        in_specs=[ANY], out_specs=ANY,
        scratch_shapes=[pltpu.SemaphoreType.DMA((7,)), pltpu.SemaphoreType.DMA((7,)), pltpu.SemaphoreType.DMA],
        name=name)(src)


MIXW = (("ssd_w_in", None), ("ssd_w_out", 0), ("pool_w", 1))
FFNW = (("ffn_w_up", 1), ("ffn_w_down", 0))


def _chip_window(axis, ref, layers, k):
    if axis is None:
        return ref.at[layers, k]
    n = ref.shape[1 + axis] // N_CHIPS
    sl = pl.ds(pl.multiple_of(k * n, LANES if 1 + axis == len(ref.shape) - 1 else 8), n)
    idx = [layers] + [slice(None)] * (len(ref.shape) - 1)
    idx[1 + axis] = sl
    return ref.at[tuple(idx)]


def _full_shape(axis, shard_shape):
    if axis is None:
        return (shard_shape[0], N_CHIPS) + tuple(shard_shape[1:])
    full = list(shard_shape)
    full[1 + axis] *= N_CHIPS
    return tuple(full)


HBM_SPEC = pl.BlockSpec(memory_space=pltpu.HBM)
SEM_SPEC = pl.BlockSpec(memory_space=pltpu.SEMAPHORE)


def _dma_sems(count):
    return pltpu.SemaphoreType.DMA((max(count, 1),))


def _wait_for(copy, kind):
    if kind == "recv":
        copy.wait_recv()
    elif kind == "send":
        copy.wait_send()
    else:
        copy.wait()


def _comm_fused(stages, counts, srcs, lands, name, inplace=False):
    ns, nl, k = len(srcs), len(lands), len(stages)

    def body(*refs):
        src_refs = refs[:ns]
        land_refs = refs[ns + (nl if inplace else 0):ns + (nl if inplace else 0) + nl]
        sem_refs = refs[len(refs) - 3 * k:]
        for s, stage_fn in enumerate(stages):
            starts, waits = stage_fn(src_refs, land_refs, tuple(sem_refs[3 * s:3 * s + 3]))
            for cp in starts:
                cp.start()
            for cp, kind in waits:
                _wait_for(cp, kind)

    scratch = []
    for cnt in counts:
        scratch += [_dma_sems(c) for c in cnt]
    outs = pl.pallas_call(
        body, out_shape=tuple(jax.ShapeDtypeStruct(a.shape, a.dtype) for a in lands),
        in_specs=[ANY] * (ns + (nl if inplace else 0)), out_specs=(ANY,) * nl,
        input_output_aliases={ns + i: i for i in range(nl)} if inplace else {},
        scratch_shapes=scratch, name=name)(*srcs, *(lands if inplace else ()))
    return list(outs)


class _SplitComm:
    def __init__(self, stages, counts, srcs, lands, name):
        self.stages, self.counts, self.name = stages, counts, name
        self.ns = len(srcs)
        self.data = [pltpu.with_memory_space_constraint(a, pltpu.HBM) for a in list(srcs) + list(lands)]
        self.sems = None
        self.step = 0

    def advance(self, after=None):
        i, k, nd, ns = self.step, len(self.stages), len(self.data), self.ns
        first, last = i == 0, i == k
        stages = self.stages
        after = list(after) if isinstance(after, (list, tuple)) else [after]

        def body(*refs):
            data = refs[:nd]
            pos = nd
            if not first:
                old = tuple(refs[pos:pos + 3])
                pos += 3 + len(after)
            if not last:
                new = tuple(refs[pos:pos + 3])
            if not first:
                for cp, kind in stages[i - 1](data[:ns], data[ns:], old)[1]:
                    _wait_for(cp, kind)
            if not last:
                for cp in stages[i](data[:ns], data[ns:], new)[0]:
                    cp.start()
                refs[len(refs) - 1][...] = jnp.zeros((8, LANES), F32)

        args = list(self.data)
        in_specs = [HBM_SPEC] * nd
        if not first:
            args += list(self.sems) + after
            in_specs += [SEM_SPEC] * 3 + [ANY] * len(after)
        out_shape, out_specs = [], []
        if not last:
            out_shape += [_dma_sems(c) for c in self.counts[i]]
            out_specs += [SEM_SPEC] * 3
        out_shape += [pltpu.HBM(a.shape, a.dtype) for a in self.data]
        out_specs += [HBM_SPEC] * nd
        if not last:
            out_shape.append(jax.ShapeDtypeStruct((8, LANES), F32))
            out_specs.append(pl.BlockSpec(memory_space=pltpu.VMEM))
        off = 0 if last else 3
        outs = pl.pallas_call(
            body, out_shape=tuple(out_shape), in_specs=in_specs, out_specs=tuple(out_specs),
            input_output_aliases={d: off + d for d in range(nd)},
            compiler_params=pltpu.CompilerParams(has_side_effects=pltpu.SideEffectType.DATAFLOW_SIDE_EFFECTING),
            name=f"{self.name}_{i}")(*args)
        self.sems = None if last else outs[:3]
        self.data = list(outs[off:off + nd])
        self.step += 1
        return None if last else outs[len(outs) - 1]

    def lands(self):
        return self.data[self.ns:]


def _gather_stages(spec):
    n = len(spec)

    def parts(srcs, lands):
        x, y, c = _coords()
        out = []
        for w, (_, axis) in enumerate(spec):
            h = srcs[w].shape[0] // 2
            mine, theirs = pl.ds(c * h, h), pl.ds((1 - c) * h, h)
            out.append((srcs[w].at[mine], lambda layers, k, w=w, axis=axis: _chip_window(axis, lands[w], layers, k),
                        mine, theirs))
        return x, y, c, 2 * x + y, (x, y, 1 - c), _other_chips(x, y), out

    def remote(src, dst, send, recv, idx, to):
        return pltpu.make_async_remote_copy(src_ref=src, dst_ref=dst, send_sem=send.at[idx], recv_sem=recv.at[idx],
                                            device_id=to, device_id_type=MESH)

    def stage0(srcs, lands, sems):
        send, recv, local = sems
        x, y, c, me, sib, chips, ps = parts(srcs, lands)
        starts, waits = [], []
        for w, (src, dst, mine, theirs) in enumerate(ps):
            lc = pltpu.make_async_copy(src, dst(mine, me), local.at[w])
            first = [remote(src, dst(mine, me), send, recv, 4 * w, sib)]
            first += [remote(src, dst(mine, me), send, recv, 4 * w + 1 + j, (cx, cy, c)) for j, (cx, cy) in enumerate(chips)]
            starts += [lc] + first
            waits.append((remote(src, dst(theirs, me), send, recv, 4 * w, (x, y, c)), "recv"))
            waits += [(remote(src, dst(mine, 2 * cx + cy), send, recv, 4 * w + 1 + j, (x, y, c)), "recv")
                      for j, (cx, cy) in enumerate(chips)]
            waits += [(cp, "send") for cp in first] + [(lc, "local")]
        return starts, waits

    def stage1(srcs, lands, sems):
        send, recv, _ = sems
        x, y, c, me, sib, chips, ps = parts(srcs, lands)
        starts, waits = [], []
        for w, (src, dst, mine, theirs) in enumerate(ps):
            for j, (cx, cy) in enumerate(chips):
                blk = dst(mine, 2 * cx + cy)
                fwd = remote(blk, blk, send, recv, 3 * w + j, sib)
                starts.append(fwd)
                waits.append((remote(src, dst(theirs, 2 * cx + cy), send, recv, 3 * w + j, (x, y, c)), "recv"))
                waits.append((fwd, "send"))
        return starts, waits

    return [stage0, stage1], [(4 * n, 4 * n, n), (3 * n, 3 * n, 0)]


def _swap_stages(spec):
    n = len(spec)

    def stage(srcs, lands, sems):
        send, recv, _ = sems
        x, y, c = _coords()
        starts, waits = [], []
        for w in range(n):
            h = srcs[w].shape[0] // 2
            cp = pltpu.make_async_remote_copy(src_ref=srcs[w].at[pl.ds((1 - c) * h, h)], dst_ref=lands[w],
                                              send_sem=send.at[w], recv_sem=recv.at[w],
                                              device_id=(x, y, 1 - c), device_id_type=MESH)
            starts.append(cp)
            waits += [(cp, "recv"), (cp, "send")]
        return starts, waits

    return [stage], [(n, n, 0)]


def _scatter_stages(spec):
    n = len(spec)

    def stage(srcs, lands, sems):
        send, recv, local = sems
        x, y, c = _coords()
        me = 2 * x + y
        starts, waits = [], []
        for w, (_, axis) in enumerate(spec):
            layers = pl.ds(0, srcs[w].shape[0])
            own = _chip_window(axis, srcs[w], layers, me)
            lc = pltpu.make_async_copy(own, lands[w].at[me], local.at[w])
            starts.append(lc)
            for j, (cx, cy) in enumerate(_other_chips(x, y)):
                cp = pltpu.make_async_remote_copy(src_ref=_chip_window(axis, srcs[w], layers, 2 * cx + cy),
                                                  dst_ref=lands[w].at[me], send_sem=send.at[3 * w + j],
                                                  recv_sem=recv.at[3 * w + j], device_id=(cx, cy, c), device_id_type=MESH)
                starts.append(cp)
                waits.append((pltpu.make_async_remote_copy(
                    src_ref=own, dst_ref=lands[w].at[2 * cx + cy], send_sem=send.at[3 * w + j], recv_sem=recv.at[3 * w + j],
                    device_id=(x, y, c), device_id_type=MESH), "recv"))
                waits.append((cp, "send"))
            waits.append((lc, "local"))
        return starts, waits

    return [stage], [(3 * n, 3 * n, n)]


def _share_stages(spec):
    n = len(spec)

    def stage(srcs, lands, sems):
        send, recv, _ = sems
        x, y, c = _coords()
        starts, waits = [], []
        for w in range(n):
            h = lands[w].shape[0] // 2
            mine, theirs = lands[w].at[pl.ds(c * h, h)], lands[w].at[pl.ds((1 - c) * h, h)]
            cp = pltpu.make_async_remote_copy(src_ref=mine, dst_ref=mine, send_sem=send.at[w], recv_sem=recv.at[w],
                                              device_id=(x, y, 1 - c), device_id_type=MESH)
            starts.append(cp)
            waits.append((pltpu.make_async_remote_copy(src_ref=theirs, dst_ref=theirs, send_sem=send.at[w],
                                                       recv_sem=recv.at[w], device_id=(x, y, c), device_id_type=MESH),
                          "recv"))
            waits.append((cp, "send"))
        return starts, waits

    return [stage], [(n, n, 0)]


def _shard_of(p, axis):
    if axis is None:
        return (p.shape[0],) + tuple(p.shape[2:])
    s = list(p.shape)
    s[1 + axis] //= N_CHIPS
    return tuple(s)


def _gather8_stages():
    def stage(srcs, lands, sems):
        send, recv, local = sems
        x, y, c = _coords()
        me = 4 * x + 2 * y + c
        lc = pltpu.make_async_copy(srcs[0], lands[0].at[me], local.at[0])
        starts, waits = [lc], []
        for kk in range(1, 8):
            to = (1 - x if kk & 4 else x, 1 - y if kk & 2 else y, 1 - c if kk & 1 else c)
            cp = pltpu.make_async_remote_copy(src_ref=srcs[0], dst_ref=lands[0].at[me], send_sem=send.at[kk - 1],
                                              recv_sem=recv.at[kk - 1], device_id=to, device_id_type=MESH)
            starts.append(cp)
            waits.append((pltpu.make_async_remote_copy(
                src_ref=srcs[0], dst_ref=lands[0].at[4 * to[0] + 2 * to[1] + to[2]], send_sem=send.at[kk - 1],
                recv_sem=recv.at[kk - 1], device_id=(x, y, c), device_id_type=MESH), "recv"))
            waits.append((cp, "send"))
        waits.append((lc, "local"))
        return starts, waits

    return [stage], [(7, 7, 1)]


def _sum8(buf, name):
    _, rows, cols = buf.shape
    tr = _pick_rows(rows, cols * 4)

    def body(*refs):
        acc = refs[0][...]
        for r in refs[1:8]:
            acc = acc + r[...]
        refs[8][...] = acc

    return pl.pallas_call(
        body, out_shape=jax.ShapeDtypeStruct((rows, cols), F32), grid=(rows // tr,),
        in_specs=[pl.BlockSpec((None, tr, cols), lambda i, k=k: (k, i, 0)) for k in range(8)],
        out_specs=pl.BlockSpec((tr, cols), lambda i: (i, 0)),
        compiler_params=_params(("parallel",)), name=name)(*([buf] * 8))


def _reduce_begin(spec, gs, core, tag, riders=(), during=None):
    stages, counts = _swap_stages(spec)
    got_shapes = [jax.ShapeDtypeStruct((g.shape[0] // 2,) + g.shape[1:], g.dtype) for g in gs]
    if during is None:
        got = _comm_fused(stages, counts, list(gs) + list(riders), got_shapes, "swap_" + tag)
    else:
        swap = _SplitComm(stages, counts, list(gs) + list(riders), [lax.empty(s.shape, s.dtype) for s in got_shapes],
                          "swap_" + tag)
        swap.advance(after=during(swap.advance()))
        gs, got = swap.data[:len(gs)], swap.lands()
    pair = [_pair_sum(a, r, core, "pair_sum_" + n) for a, r, (n, _) in zip(gs, got, spec)]
    stages, counts = _scatter_stages(spec)
    lands = [lax.empty((N_CHIPS,) + _shard_of(p, axis), p.dtype) for p, (_, axis) in zip(pair, spec)]
    comm = _SplitComm(stages, counts, pair, lands, "scatter_" + tag)
    return comm, comm.advance()


def _reduce_finish(spec, comm, core, tag, after):
    comm.advance(after=after)
    halves = [_sum4(q, core, "sum4_" + n) for q, (n, _) in zip(comm.lands(), spec)]
    stages, counts = _share_stages(spec)
    return _comm_fused(stages, counts, [], halves, "share_" + tag, inplace=True)


SMALL = (("ssd_conv_w", 2), ("pool_scale", 1), ("ffn_conv_w", 2))
REPL = ("ssd_conv_b", "ssd_dt_bias", "ssd_a_log", "ssd_d", "ssd_norm_w", "ffn_conv_b",
        "norm_mix_pre", "norm_mix_post", "norm_ffn_pre", "norm_ffn_post")
WEIGHTS = ("ssd_w_in", "ssd_conv_w", "ssd_conv_b", "ssd_dt_bias", "ssd_a_log", "ssd_d", "ssd_norm_w", "ssd_w_out",
           "pool_w", "pool_scale", "ffn_w_up", "ffn_conv_w", "ffn_conv_b", "ffn_w_down", "norm_mix_pre",
           "norm_mix_post", "norm_ffn_pre", "norm_ffn_post")


def _flat_rows(n):
    unit = 2 * 16 * FLAT_COLS
    return 2 * 16 * ((n + unit - 1) // unit)


def _flatten_shards(arrs, dtype):
    flat = jnp.concatenate([a.astype(dtype).reshape(-1) for a in arrs])
    rows = _flat_rows(flat.shape[0])
    flat = jnp.pad(flat, (0, rows * FLAT_COLS - flat.shape[0]))
    return flat.reshape(2, rows // 2, FLAT_COLS)


def _unflatten_full(gathered, shard_shapes, axes):
    per_chip = jnp.swapaxes(gathered, 0, 1).reshape(N_CHIPS, -1)
    out, off = [], 0
    for shp, ax in zip(shard_shapes, axes):
        n = math.prod(shp)
        pieces = [per_chip[k, off:off + n].reshape(shp) for k in range(N_CHIPS)]
        out.append(jnp.concatenate(pieces, axis=ax))
        off += n
    return out


def kernel(x, ssd_w_in, ssd_conv_w, ssd_conv_b, ssd_dt_bias, ssd_a_log, ssd_d, ssd_norm_w, ssd_w_out, pool_w, pool_scale, ffn_w_up, ffn_conv_w, ffn_conv_b, ffn_w_down, norm_mix_pre, norm_mix_post, norm_ffn_pre, norm_ffn_post, loss_target, m_ssd_w_in, m_ssd_conv_w, m_ssd_conv_b, m_ssd_dt_bias, m_ssd_a_log, m_ssd_d, m_ssd_norm_w, m_ssd_w_out, m_pool_w, m_pool_scale, m_ffn_w_up, m_ffn_conv_w, m_ffn_conv_b, m_ffn_w_down, m_norm_mix_pre, m_norm_mix_post, m_norm_ffn_pre, m_norm_ffn_post, v_ssd_w_in, v_ssd_conv_w, v_ssd_conv_b, v_ssd_dt_bias, v_ssd_a_log, v_ssd_d, v_ssd_norm_w, v_ssd_w_out, v_pool_w, v_pool_scale, v_ffn_w_up, v_ffn_conv_w, v_ffn_conv_b, v_ffn_w_down, v_norm_mix_pre, v_norm_mix_post, v_norm_ffn_pre, v_norm_ffn_post):
    wts = dict(ssd_w_in=ssd_w_in, ssd_conv_w=ssd_conv_w, ssd_conv_b=ssd_conv_b, ssd_dt_bias=ssd_dt_bias,
               ssd_a_log=ssd_a_log, ssd_d=ssd_d, ssd_norm_w=ssd_norm_w, ssd_w_out=ssd_w_out, pool_w=pool_w,
               pool_scale=pool_scale, ffn_w_up=ffn_w_up, ffn_conv_w=ffn_conv_w, ffn_conv_b=ffn_conv_b,
               ffn_w_down=ffn_w_down, norm_mix_pre=norm_mix_pre, norm_mix_post=norm_mix_post,
               norm_ffn_pre=norm_ffn_pre, norm_ffn_post=norm_ffn_post)
    mom = dict(ssd_w_in=m_ssd_w_in, ssd_conv_w=m_ssd_conv_w, ssd_conv_b=m_ssd_conv_b, ssd_dt_bias=m_ssd_dt_bias,
               ssd_a_log=m_ssd_a_log, ssd_d=m_ssd_d, ssd_norm_w=m_ssd_norm_w, ssd_w_out=m_ssd_w_out, pool_w=m_pool_w,
               pool_scale=m_pool_scale, ffn_w_up=m_ffn_w_up, ffn_conv_w=m_ffn_conv_w, ffn_conv_b=m_ffn_conv_b,
               ffn_w_down=m_ffn_w_down, norm_mix_pre=m_norm_mix_pre, norm_mix_post=m_norm_mix_post,
               norm_ffn_pre=m_norm_ffn_pre, norm_ffn_post=m_norm_ffn_post)
    var = dict(ssd_w_in=v_ssd_w_in, ssd_conv_w=v_ssd_conv_w, ssd_conv_b=v_ssd_conv_b, ssd_dt_bias=v_ssd_dt_bias,
               ssd_a_log=v_ssd_a_log, ssd_d=v_ssd_d, ssd_norm_w=v_ssd_norm_w, ssd_w_out=v_ssd_w_out, pool_w=v_pool_w,
               pool_scale=v_pool_scale, ffn_w_up=v_ffn_w_up, ffn_conv_w=v_ffn_conv_w, ffn_conv_b=v_ffn_conv_b,
               ffn_w_down=v_ffn_w_down, norm_mix_pre=v_norm_mix_pre, norm_mix_post=v_norm_mix_post,
               norm_ffn_pre=v_norm_ffn_pre, norm_ffn_post=v_norm_ffn_post)

    bl, seq, d = x.shape
    t = bl * seq
    depth = norm_mix_pre.shape[0]
    n_ssd = ssd_w_out.shape[0]
    d_inner = ssd_w_out.shape[1] * N_CHIPS
    nheads = d_inner // HEAD_DIM
    hpg = nheads // N_GROUPS
    gw = d_inner // N_GROUPS
    xbc = ssd_conv_w.shape[2] * N_CHIPS
    f2 = ffn_w_up.shape[2] * N_CHIPS
    ff = f2 // 2
    dg = d // 4
    cy = lax.axis_index("c")
    chip = 2 * lax.axis_index("x") + lax.axis_index("y")

    small_shapes = [wts[n].shape for n, _ in SMALL]
    small_axes = [a for _, a in SMALL]
    small_flat = _flatten_shards([wts[n] for n, _ in SMALL], F32)
    small_half = lax.dynamic_index_in_dim(small_flat, cy, 0, keepdims=False)
    small_all = _allgather_halves(small_half, "gather_small")
    conv_w, p_scale, f_conv_w = _unflatten_full(small_all, small_shapes, small_axes)
    def full_shapes(spec, shards):
        return [jax.ShapeDtypeStruct(_full_shape(axis, s.shape), s.dtype) for s, (_, axis) in zip(shards, spec)]

    def row_halves(a):
        return a.reshape((2, a.shape[0] // 2) + a.shape[1:])

    def join_w_in(g):
        return jnp.concatenate([g[:, k] for k in range(N_CHIPS)], axis=-1).reshape(d, -1)

    def join_w_out(g):
        r2 = g.shape[1] // N_CHIPS
        return jnp.concatenate([g[hf, k * r2:(k + 1) * r2] for k in range(N_CHIPS) for hf in range(2)], axis=0)

    ssd_spec = (("ssd_w_in", None), ("ssd_w_out", 0))
    first_shards = [row_halves(wts[n][0].astype(BF16)) for n, _ in ssd_spec]
    stages, counts = _gather_stages(ssd_spec)
    g_in0, g_out0 = _comm_fused(stages, counts, first_shards, full_shapes(ssd_spec, first_shards), "gather_first")
    w_in, w_out = [join_w_in(g_in0)], [join_w_out(g_out0)]
    rest_spec = ssd_spec * (n_ssd - 1) + (("pool_w", 1),) + FFNW
    rest_shards = [row_halves(wts[n][jj].astype(BF16)) for jj in range(1, n_ssd) for n, _ in ssd_spec]
    rest_shards += [wts["pool_w"].astype(BF16)] + [wts[n].astype(BF16) for n, _ in FFNW]
    stages, counts = _gather_stages(rest_spec)
    ffn_gather = _SplitComm(stages, counts, rest_shards + [g_out0],
                            [lax.empty(s.shape, s.dtype) for s in full_shapes(rest_spec, rest_shards)], "gather_rest")
    gather_token = ffn_gather.advance()

    def pad_heads(a):
        lead = a.shape[:-1]
        a = a.reshape(lead + (N_GROUPS, hpg))
        a = jnp.pad(a, [(0, 0)] * len(lead) + [(0, 0), (0, LANES - hpg)])
        return a.reshape(lead + (N_GROUPS * LANES,))

    def unpad_heads(a):
        lead = a.shape[:-1]
        return a.reshape(lead + (N_GROUPS, LANES))[..., :hpg].reshape(lead + (nheads,))

    def group_rows(a, width):
        return jnp.broadcast_to(a.reshape(N_GROUPS, 1, width), (N_GROUPS, 8, width))

    def pad_w_in(w):
        return jnp.concatenate([w[..., :d_inner + xbc], pad_heads(w[..., d_inner + xbc:])], axis=-1)

    w_in_p = [pad_w_in(w_in[0])]
    zw = w_in_p[0].shape[-1]
    w_pool = None

    x2 = x.reshape(t, d)
    tgt2 = loss_target.reshape(t, d)
    w_up = w_down = None

    saved = []
    cur = x2
    tokens = []
    h = _norm_fwd(cur, norm_mix_pre[0:1], BF16, "norm_pre_b", after=[gather_token])
    for i in range(depth):
        j = i // 2
        sv = dict(x_in=cur)
        if i % 2 == 0:
            zx = _mm(h, w_in_p[j], "nn", BF16, "mm_ssd_in", 2048, 512, d).reshape(bl, seq, zw)
            dtr = _mm(h, w_in_p[j][:, d_inner + xbc:], "nn", F32, "mm_ssd_dt", 2048, 512, d).reshape(bl, seq, -1)
            xc, xpre = _ssd_conv_fwd(zx, conv_w[j], ssd_conv_b[j:j + 1], d_inner, "ssd_conv_fwd")
            dtb = group_rows(pad_heads(ssd_dt_bias[j]), LANES)
            alog = group_rows(pad_heads(ssd_a_log[j]), LANES)
            dskip = group_rows(jnp.repeat(ssd_d[j], HEAD_DIM), gw)
            nw = group_rows(ssd_norm_w[j], gw)
            y, yn, st = _ssd_fwd(xc, zx, dtr, dtb, alog, dskip, nw, d_inner, "ssd_fwd")
            if i == 0:
                tokens.append(ffn_gather.advance(after=yn))
            mix = _mm(yn.reshape(t, d_inner), w_out[j], "nn", F32, "mm_ssd_out", 2048, 512, d_inner)
            sv.update(h=h, zx=zx, dtr=dtr, xc=xc, xpre=xpre, y=y, yn=yn, st=st, dtb=dtb, alog=alog, dskip=dskip, nw=nw)
        else:
            mix = _pool_fwd(h.reshape(bl, seq, d), w_pool[j], p_scale[j:j + 1], "pool_fwd").reshape(t, d)
            sv.update(h=h)
        sv.update(mix=mix)
        mid, u = _norm_post_pre(mix, norm_mix_post[i:i + 1], cur, norm_ffn_pre[i:i + 1], BF16, "norm_post_pre_b",
                                after=tokens)
        tokens = []
        if i == 0:
            ffn_gather.advance(after=u)
            rest = ffn_gather.lands()
            for jj in range(1, n_ssd):
                w_in_p.append(pad_w_in(join_w_in(rest[2 * (jj - 1)])))
                w_out.append(join_w_out(rest[2 * (jj - 1) + 1]))
            w_pool, w_up, w_down = rest[2 * (n_ssd - 1):]
        hpre = _mm(u, w_up, "nn", BF16, "mm_up", 2048, 512, d, b_layer=i).reshape(bl, seq, f2)
        act, pre_g, pre_v = _ffn_act_fwd(hpre, f_conv_w[i], ffn_conv_b[i:i + 1], "ffn_act_fwd")
        act = act.reshape(t, ff)
        fo = _mm(act, w_down, "nn", F32, "mm_down", 2048, 512, ff, b_layer=i)
        if i + 1 == depth:
            cur = _norm_fwd(fo, norm_ffn_post[i:i + 1], F32, "norm_post", resid=mid)
        elif i % 2 == 0:
            cur, h = _norm_post_pre(fo, norm_ffn_post[i:i + 1], mid, norm_mix_pre[i + 1:i + 2], F32, "norm_post_pre_f")
        else:
            cur, h = _norm_post_pre(fo, norm_ffn_post[i:i + 1], mid, norm_mix_pre[i + 1:i + 2], BF16, "norm_post_pre_b")
        sv.update(mid=mid, u=u, hpre=hpre, pre_g=pre_g, pre_v=pre_v, act=act, fo=fo)
        saved.append(sv)

    dcur, loss_part = _loss_head(cur, tgt2, "loss_head")

    g = {n: [None] * wts[n].shape[0] for n in WEIGHTS}
    gbuf = dict(up=lax.empty((depth, d, f2), F32), down=lax.empty((depth, ff, d), F32),
                out=lax.empty((n_ssd, d_inner, d), F32), win=lax.empty((n_ssd, d, zw), F32))
    core = cy.reshape(1).astype(jnp.int32)

    def mixer_bwd(i, dmid, dmix, behind=()):
        j = i // 2
        sv = saved[i]
        done = []
        if i % 2 == 0:
            dyn = _mm(dmix, w_out[j], "nt", BF16, "mm_ssd_out_dx", 1024, 1024, d, after=behind)
            gbuf["out"], tok = _mm(sv["yn"].reshape(t, d_inner), dmix, "tn", F32, "mm_ssd_out_dw", 1024, 1024, 2048,
                                   out_buf=(gbuf["out"], j))
            done.append(tok)
            dz, dxs, dbm, dcm, ddt, dnw, dd, dal, dbias = _ssd_bwd(
                sv["xc"], sv["zx"], sv["dtr"], sv["y"], dyn.reshape(bl, seq, d_inner), sv["st"], sv["dtb"], sv["alog"],
                sv["dskip"], sv["nw"], d_inner, "ssd_bwd")
            g["ssd_norm_w"][j] = dnw[:, 0, :].reshape(d_inner)
            g["ssd_d"][j] = dd[:, 0, :hpg].reshape(nheads)
            g["ssd_a_log"][j] = dal[:, 0, :hpg].reshape(nheads)
            g["ssd_dt_bias"][j] = dbias[:, 0, :hpg].reshape(nheads)
            dzx, dcw, dcb = _ssd_conv_bwd(sv["zx"], sv["xpre"], (dxs, dbm, dcm), ddt, dz, conv_w[j], d_inner,
                                          "ssd_conv_bwd")
            g["ssd_conv_w"][j] = dcw
            g["ssd_conv_b"][j] = dcb[0]
            dzx = dzx.reshape(t, zw)
            dh = _mm(dzx, w_in_p[j], "nt", BF16, "mm_ssd_in_dx", 1024, d, zw // 2)
            gbuf["win"], tok = _mm(sv["h"], dzx, "tn", F32, "mm_ssd_in_dw", 1024, zw // 4, 2048, out_buf=(gbuf["win"], j))
            done.append(tok)
        else:
            dh3, g["pool_w"][j], dps = _pool_bwd(sv["h"].reshape(bl, seq, d), dmix.reshape(bl, seq, d), w_pool[j],
                                                 p_scale[j:j + 1], "pool_bwd")
            g["pool_scale"][j] = dps[0]
            dh = dh3.reshape(t, d)
        if i == 0:
            dx_in, g["norm_mix_pre"][i] = _norm_bwd(sv["x_in"], norm_mix_pre[i:i + 1], dh, F32, "norm_bwd_r", resid=dmid,
                                                    after=done)
            return dx_in, None
        dx_in, dfo_prev, g["norm_mix_pre"][i], g["norm_ffn_post"][i - 1] = _norm_bwd2(
            sv["x_in"], norm_mix_pre[i:i + 1], dh, dmid, saved[i - 1]["fo"], norm_ffn_post[i - 1:i], BF16,
            "norm_bwd_in_post", after=done)
        return dx_in, dfo_prev

    ffn_comm = None
    dfo, g["norm_ffn_post"][depth - 1] = _norm_bwd(saved[depth - 1]["fo"], norm_ffn_post[depth - 1:depth], dcur, BF16,
                                                   "norm_bwd_b")
    for i in reversed(range(depth)):
        sv = saved[i]
        dact = _mm(dfo, w_down, "nt", BF16, "mm_down_dx", 1024, ff // 2, d, b_layer=i)
        gbuf["down"], tok_down = _mm(sv["act"], dfo, "tn", F32, "mm_down_dw", ff // 2, 1024, 2048,
                                     out_buf=(gbuf["down"], i))
        dhg, dhv, dcw, dcb = _ffn_act_bwd(sv["hpre"], sv["pre_g"], sv["pre_v"], dact.reshape(bl, seq, ff), f_conv_w[i],
                                          "ffn_act_bwd")
        g["ffn_conv_w"][i] = dcw
        g["ffn_conv_b"][i] = dcb[0]
        dhs = [dhg.reshape(t, ff), dhv.reshape(t, ff)]
        du = _mm(dhs, w_up, "nt", BF16, "mm_up_dx", 1024, d, ff, b_layer=i)
        gbuf["up"], tok_up = _mm(sv["u"], dhs, "tn", F32, "mm_up_dw", 1024, ff // 2, 2048, out_buf=(gbuf["up"], i))
        def pre_post(behind, i=i, sv=sv, du=du, dcur=dcur):
            return _norm_bwd2(sv["mid"], norm_ffn_pre[i:i + 1], du, dcur, sv["mix"], norm_mix_post[i:i + 1],
                              BF16 if i % 2 == 0 else F32,
                              "norm_bwd_pre_post_b" if i % 2 == 0 else "norm_bwd_pre_post_f", after=behind)

        if i > 0:
            dmid, dmix, g["norm_ffn_pre"][i], g["norm_mix_post"][i] = pre_post([tok_down, tok_up])
            dcur, dfo = mixer_bwd(i, dmid, dmix)
        else:
            held = []

            def during(token):
                held.extend(pre_post([tok_down, tok_up, token]))
                return held[0]

            ffn_comm, ffn_token = _reduce_begin(FFNW, [gbuf["up"], gbuf["down"]], core, "ffn", during=during)
            dmid, dmix, g["norm_ffn_pre"][i], g["norm_mix_post"][i] = held
            dcur, dfo = mixer_bwd(i, dmid, dmix, behind=[ffn_token])

    grad_x = dcur.reshape(bl, seq, d)
    for n in ("norm_mix_pre", "norm_mix_post", "norm_ffn_pre", "norm_ffn_post"):
        g[n] = [a[0] for a in g[n]]
    small_names = [n for n, _ in SMALL] + list(REPL)
    full = {n: jnp.stack(g[n], axis=0) for n in small_names}

    g_in = jnp.concatenate([gbuf["win"][..., :d_inner + xbc], unpad_heads(gbuf["win"][..., d_inner + xbc:])], axis=-1)
    g_in_cm = jnp.swapaxes(g_in.reshape(n_ssd, d, N_CHIPS, -1), 1, 2)
    vec = jnp.concatenate([full[n].reshape(-1) for n in small_names] + [loss_part[0, :1]])
    nvec = vec.shape[0]
    vrows = 16 * ((nvec + 16 * FLAT_COLS - 1) // (16 * FLAT_COLS))
    vec = jnp.pad(vec, (0, vrows * FLAT_COLS - nvec)).reshape(vrows, FLAT_COLS)
    stages, counts = _gather8_stages()
    small_comm = _SplitComm(stages, counts, [vec], [lax.empty((8, vrows, FLAT_COLS), F32)], "gather_small_grads")
    small_token = small_comm.advance()
    mix_comm, mix_token = _reduce_begin(MIXW, [g_in_cm, gbuf["out"], jnp.stack(g["pool_w"], axis=0)], core, "mixers",
                                        riders=[small_token])

    grads, deltas, new_m, new_v = {}, {}, {}, {}

    def adamw(n, gr):
        shp = wts[n].shape
        two = (math.prod(shp[:-1]), shp[-1])
        dl, mn, vn = _adamw(wts[n].reshape(two), gr.reshape(two), mom[n].reshape(two), var[n].reshape(two),
                            "adamw_" + n)
        grads[n], deltas[n], new_m[n], new_v[n] = gr, dl.reshape(shp), mn.reshape(shp), vn.reshape(shp)
        return dl

    small_comm.advance(after=mix_token)
    tot = _sum8(small_comm.lands()[0], "sum_small").reshape(-1)
    small_grads, off = {}, 0
    for n in small_names:
        cnt = math.prod(full[n].shape)
        small_grads[n] = tot[off:off + cnt].reshape(full[n].shape)
        off += cnt
    loss = tot[off]
    for n, ax in SMALL:
        w = wts[n].shape[ax]
        small_grads[n] = lax.dynamic_slice_in_dim(small_grads[n], chip * w, w, axis=ax)

    behind = [adamw(n, small_grads[n]) for n in small_names][-1:]
    ffn_grads = _reduce_finish(FFNW, ffn_comm, core, "ffn", after=mix_token)
    behind += [adamw(n, gr) for gr, (n, _) in zip(ffn_grads, FFNW)]
    mix_grads = _reduce_finish(MIXW, mix_comm, core, "mixers", after=behind)
    for gr, (n, _) in zip(mix_grads, MIXW):
        adamw(n, gr)

    return (loss, grad_x, *[grads[n] for n in WEIGHTS], *[deltas[n] for n in WEIGHTS],
            *[new_m[n] for n in WEIGHTS], *[new_v[n] for n in WEIGHTS])
```

```python
import functools
import math

import jax
import jax.numpy as jnp
from jax import lax
from jax.experimental import pallas as pl
from jax.experimental.pallas import tpu as pltpu

F32 = jnp.float32
BF16 = jnp.bfloat16
MESH = pl.DeviceIdType.MESH
ANY = pl.BlockSpec(memory_space=pl.ANY)

HEAD_DIM = 64
D_STATE = 128
CHUNK = 128
N_GROUPS = 4
SSD_CONV = 4
FFN_CONV = 3
EPS = 1e-6
N_CHIPS = 4
LANES = 128
FLAT_COLS = 1024

ADAM_LR = 0.001
ADAM_B1 = 0.9
ADAM_B2 = 0.999
ADAM_EPS = 1e-08
ADAM_WD = 0.01
ADAM_STEP = 10

VMEM_LIMIT_BYTES = 56 * 1024 * 1024


def _params(sem=None):
    kw = dict(vmem_limit_bytes=VMEM_LIMIT_BYTES)
    if sem is not None:
        kw["dimension_semantics"] = sem
    return pltpu.CompilerParams(**kw)


def _sigmoid(x):
    return 0.5 * jnp.tanh(0.5 * x) + 0.5


def _softplus(x):
    return jnp.maximum(x, 0.0) + jnp.log(1.0 + jnp.exp(-jnp.abs(x)))


def _dot(a, b, dn):
    return lax.dot_general(a, b, (dn, ((), ())), preferred_element_type=F32)


def _nn(a, b):
    return _dot(a, b, ((1,), (0,)))


def _nt(a, b):
    return _dot(a, b, ((1,), (1,)))


def _tn(a, b):
    return _dot(a, b, ((0,), (0,)))


def _split(x, parts):
    out = []
    r = x
    for _ in range(parts):
        p = r.astype(BF16)
        out.append(p)
        r = r - p.astype(F32)
    return out


def _sel_left(sel, x, parts=3):
    n = x.shape[1]
    r = _nn(sel, jnp.concatenate(_split(x, parts), axis=1))
    out = r[:, 0:n]
    for i in range(1, parts):
        out = out + r[:, i * n:(i + 1) * n]
    return out


def _sel_right(x, sel_stacked, parts=3):
    return _nn(jnp.concatenate(_split(x, parts), axis=1), sel_stacked)


def _mm(a, b, dims, out_dtype, name, tm, tn, tk, b_layer=None, out_buf=None, after=()):
    a_list = list(a) if isinstance(a, (list, tuple)) else [a]
    b_list = list(b) if isinstance(b, (list, tuple)) else [b]
    if dims in ("nn", "nt"):
        assert len(b_list) == 1
        m = a_list[0].shape[0]
        segs = [x.shape[1] for x in a_list]
        k = sum(segs)
        bshape = b_list[0].shape[-2:]
        n = bshape[1] if dims == "nn" else bshape[0]
        assert (bshape[0] if dims == "nn" else bshape[1]) == k
    else:
        assert len(a_list) == 1 and b_layer is None
        k, m = a_list[0].shape
        segs = [x.shape[1] for x in b_list]
        n = sum(segs)
    nseg = len(segs)
    tm, tn = min(tm, m), min(tn, n)
    if dims == "tn":
        tk = min(tk, k)
        tn = min(tn, min(segs))
        units = [tn] * nseg
        nk = k // tk
        assert k % tk == 0
    else:
        units = [min(u, s) for u, s in zip(tk if isinstance(tk, (list, tuple)) else [tk] * nseg, segs)]
        nk = sum(s // u for s, u in zip(segs, units))
    assert m % tm == 0 and n % tn == 0 and all(s % u == 0 for s, u in zip(segs, units)), (name, m, n, k, segs, units)
    counts = [s // u for s, u in zip(segs, units)]
    starts = [sum(counts[:s]) for s in range(nseg)]
    assert all(sum(segs[:s]) % units[s] == 0 for s in range(nseg)), (name, segs, units)
    first_block = [sum(segs[:s]) // units[s] for s in range(nseg)]
    dn = {"nn": ((1,), (0,)), "nt": ((1,), (1,)), "tn": ((0,), (0,))}[dims]

    same = len(set(units)) == 1
    nb_ops = len(b_list) if dims == "tn" else (1 if same else nseg)

    def body(*refs):
        a_refs = refs[:len(a_list)]
        b_refs = refs[len(a_list):len(a_list) + nb_ops]
        rest = refs[len(a_list) + nb_ops + (0 if out_buf is None else 1) + len(after):]
        o_ref = rest[0]
        if out_buf is not None:
            rest[1][...] = jnp.zeros((8, LANES), F32)
            rest = rest[1:]
        acc = rest[1] if nk > 1 else None
        kk = pl.program_id(2)
        sel = kk if dims != "tn" else pl.program_id(1)

        def step(a_ref, b_ref):
            p = _dot(a_ref[...].astype(BF16), b_ref[...].astype(BF16), dn)
            if nk == 1:
                o_ref[...] = p.astype(out_dtype)
                return

            @pl.when(kk == 0)
            def _():
                acc[...] = p

            @pl.when(kk > 0)
            def _():
                acc[...] += p

        if nseg == 1:
            step(a_refs[0], b_refs[0])
        else:
            for s in range(nseg):
                @pl.when(jnp.logical_and(sel >= starts[s], sel < starts[s] + counts[s]))
                def _(s=s):
                    step(a_refs[s] if dims != "tn" else a_refs[0], b_refs[s if nb_ops > 1 else 0])

        if nk > 1:
            @pl.when(kk == nk - 1)
            def _():
                o_ref[...] = acc[...].astype(out_dtype)

    def seg_index(v, s):
        return v if nseg == 1 else jnp.clip(v - starts[s], 0, counts[s] - 1)

    lead = () if b_layer is None else (b_layer,)
    none = () if b_layer is None else (None,)
    def b_block(kk, s):
        return kk if same else first_block[s] + seg_index(kk, s)

    if dims == "nn":
        a_specs = [pl.BlockSpec((tm, units[s]), lambda i, j, kk, s=s: (i, seg_index(kk, s))) for s in range(nseg)]
        b_specs = [pl.BlockSpec(none + (units[s], tn), lambda i, j, kk, s=s: lead + (b_block(kk, s), j))
                   for s in range(nb_ops)]
    elif dims == "nt":
        a_specs = [pl.BlockSpec((tm, units[s]), lambda i, j, kk, s=s: (i, seg_index(kk, s))) for s in range(nseg)]
        b_specs = [pl.BlockSpec(none + (tn, units[s]), lambda i, j, kk, s=s: lead + (j, b_block(kk, s)))
                   for s in range(nb_ops)]
    else:
        a_specs = [pl.BlockSpec((tk, tm), lambda i, j, kk: (kk, i))]
        b_specs = [pl.BlockSpec((tk, tn), lambda i, j, kk, s=s: (kk, seg_index(j, s))) for s in range(nseg)]
    args = a_list + (b_list * nb_ops if dims != "tn" else b_list)
    in_specs = a_specs + b_specs
    aliases = {}
    if out_buf is None:
        out_shape = jax.ShapeDtypeStruct((m, n), out_dtype)
        out_spec = pl.BlockSpec((tm, tn), lambda i, j, kk: (i, j))
    else:
        buf, slab = out_buf
        assert buf.shape[1:] == (m, n) and buf.dtype == out_dtype
        out_shape = (jax.ShapeDtypeStruct(buf.shape, out_dtype), jax.ShapeDtypeStruct((8, LANES), F32))
        out_spec = (pl.BlockSpec((None, tm, tn), lambda i, j, kk: (slab, i, j)),
                    pl.BlockSpec((8, LANES), lambda i, j, kk: (0, 0)))
        aliases = {len(args): 0}
        args = args + [buf]
        in_specs = in_specs + [ANY]
    after = [x for x in after if x is not None]
    args = args + after
    in_specs = in_specs + [ANY] * len(after)
    return pl.pallas_call(
        body,
        out_shape=out_shape,
        grid=(m // tm, n // tn, nk),
        in_specs=in_specs,
        out_specs=out_spec,
        scratch_shapes=[] if nk == 1 else [pltpu.VMEM((tm, tn), F32)],
        input_output_aliases=aliases,
        compiler_params=_params(("parallel", "parallel", "arbitrary") if out_buf is None else ("arbitrary",) * 3),
        name=name,
    )(*args)


def _row_tile(t, want):
    tm = min(want, t)
    assert t % tm == 0
    return tm


def _norm_fwd(x, w, out_dtype, name, resid=None, after=()):
    t, d = x.shape
    tm = _row_tile(t, 512)
    after = [a for a in after if a is not None]

    def body(*refs):
        refs = refs[:len(refs) - 1 - len(after)] + refs[len(refs) - 1:]
        if resid is None:
            x_ref, w_ref, o_ref = refs
        else:
            x_ref, w_ref, r_ref, o_ref = refs
        xv = x_ref[...]
        r = lax.rsqrt(jnp.mean(xv * xv, axis=-1, keepdims=True) + EPS)
        y = (xv * r) * w_ref[...]
        if resid is not None:
            y = r_ref[...] + y
        o_ref[...] = y.astype(out_dtype)

    row = pl.BlockSpec((tm, d), lambda i: (i, 0))
    vec = pl.BlockSpec((1, d), lambda i: (0, 0))
    args = [x, w] + ([] if resid is None else [resid]) + after
    return pl.pallas_call(
        body, out_shape=jax.ShapeDtypeStruct((t, d), out_dtype), grid=(t // tm,),
        in_specs=[row, vec] + ([] if resid is None else [row]) + [ANY] * len(after), out_specs=row,
        compiler_params=_params(("parallel",)), name=name)(*args)


def _norm_post_pre(m, w_post, resid, w_pre, pre_dtype, name, after=()):
    t, d = m.shape
    tm = _row_tile(t, 512)
    after = [a for a in after if a is not None]

    def body(m_ref, w1_ref, r_ref, w2_ref, *rest):
        x_ref, u_ref = rest[len(after):]
        mv = m_ref[...]
        r1 = lax.rsqrt(jnp.mean(mv * mv, axis=-1, keepdims=True) + EPS)
        xv = r_ref[...] + (mv * r1) * w1_ref[...]
        x_ref[...] = xv
        r2 = lax.rsqrt(jnp.mean(xv * xv, axis=-1, keepdims=True) + EPS)
        u_ref[...] = ((xv * r2) * w2_ref[...]).astype(pre_dtype)

    row = pl.BlockSpec((tm, d), lambda i: (i, 0))
    vec = pl.BlockSpec((1, d), lambda i: (0, 0))
    return pl.pallas_call(
        body, out_shape=(jax.ShapeDtypeStruct((t, d), F32), jax.ShapeDtypeStruct((t, d), pre_dtype)), grid=(t // tm,),
        in_specs=[row, vec, row, vec] + [ANY] * len(after), out_specs=(row, row),
        compiler_params=_params(("parallel",)), name=name)(m, w_post, resid, w_pre, *after)


def _norm_bwd(src, w, dy, out_dtype, name, resid=None, after=()):
    t, d = src.shape
    tm = _row_tile(t, 512)
    after = [a for a in after if a is not None]

    def body(*refs):
        refs = refs[:len(refs) - 2 - len(after)] + refs[len(refs) - 2:]
        if resid is None:
            x_ref, w_ref, g_ref, o_ref, dw_ref = refs
        else:
            x_ref, w_ref, g_ref, r_ref, o_ref, dw_ref = refs
        xv = x_ref[...]
        g = g_ref[...].astype(F32)
        r = lax.rsqrt(jnp.mean(xv * xv, axis=-1, keepdims=True) + EPS)
        xh = xv * r
        gh = g * w_ref[...]
        mean = jnp.mean(gh * xh, axis=-1, keepdims=True)
        dx = r * (gh - xh * mean)
        if resid is not None:
            dx = r_ref[...] + dx
        o_ref[...] = dx.astype(out_dtype)
        part = jnp.sum(g * xh, axis=0, keepdims=True)

        @pl.when(pl.program_id(0) == 0)
        def _():
            dw_ref[...] = part

        @pl.when(pl.program_id(0) > 0)
        def _():
            dw_ref[...] += part

    row = pl.BlockSpec((tm, d), lambda i: (i, 0))
    vec = pl.BlockSpec((1, d), lambda i: (0, 0))
    args = [src, w, dy] + ([] if resid is None else [resid]) + after
    return pl.pallas_call(
        body,
        out_shape=(jax.ShapeDtypeStruct((t, d), out_dtype), jax.ShapeDtypeStruct((1, d), F32)),
        grid=(t // tm,),
        in_specs=[row, vec, row] + ([] if resid is None else [row]) + [ANY] * len(after),
        out_specs=(row, vec),
        compiler_params=_params(("arbitrary",)), name=name)(*args)


def _norm_bwd2(src1, w1, dy1, resid, src2, w2, out2_dtype, name, after=()):
    t, d = src1.shape
    tm = _row_tile(t, 512)
    after = [a for a in after if a is not None]

    def back(xv, w, g):
        r = lax.rsqrt(jnp.mean(xv * xv, axis=-1, keepdims=True) + EPS)
        xh = xv * r
        gh = g * w
        return r * (gh - xh * jnp.mean(gh * xh, axis=-1, keepdims=True)), jnp.sum(g * xh, axis=0, keepdims=True)

    def body(x1_ref, w1_ref, g1_ref, r_ref, x2_ref, w2_ref, *rest):
        d1_ref, d2_ref, dw1_ref, dw2_ref = rest[len(after):]
        d1, p1 = back(x1_ref[...], w1_ref[...], g1_ref[...].astype(F32))
        d1 = r_ref[...] + d1
        d1_ref[...] = d1
        d2, p2 = back(x2_ref[...], w2_ref[...], d1)
        d2_ref[...] = d2.astype(out2_dtype)

        @pl.when(pl.program_id(0) == 0)
        def _():
            dw1_ref[...] = p1
            dw2_ref[...] = p2

        @pl.when(pl.program_id(0) > 0)
        def _():
            dw1_ref[...] += p1
            dw2_ref[...] += p2

    row = pl.BlockSpec((tm, d), lambda i: (i, 0))
    vec = pl.BlockSpec((1, d), lambda i: (0, 0))
    return pl.pallas_call(
        body,
        out_shape=(jax.ShapeDtypeStruct((t, d), F32), jax.ShapeDtypeStruct((t, d), out2_dtype),
                   jax.ShapeDtypeStruct((1, d), F32), jax.ShapeDtypeStruct((1, d), F32)),
        grid=(t // tm,),
        in_specs=[row, vec, row, row, row, vec] + [ANY] * len(after),
        out_specs=(row, row, vec, vec),
        compiler_params=_params(("arbitrary",)), name=name)(src1, w1, dy1, resid, src2, w2, *after)


def _loss_head(y, target, name):
    t, d = y.shape
    tm = _row_tile(t, 512)

    def body(y_ref, t_ref, dy_ref, l_ref):
        e = y_ref[...] - t_ref[...]
        dy_ref[...] = e * (1.0 / d)
        col = jnp.sum(e * e, axis=0, keepdims=True)
        s = jnp.sum(col, axis=1, keepdims=True) * (0.5 / d)
        part = jnp.broadcast_to(s, (1, LANES))

        @pl.when(pl.program_id(0) == 0)
        def _():
            l_ref[...] = part

        @pl.when(pl.program_id(0) > 0)
        def _():
            l_ref[...] += part

    row = pl.BlockSpec((tm, d), lambda i: (i, 0))
    return pl.pallas_call(
        body,
        out_shape=(jax.ShapeDtypeStruct((t, d), F32), jax.ShapeDtypeStruct((1, LANES), F32)),
        grid=(t // tm,), in_specs=[row, row],
        out_specs=(row, pl.BlockSpec((1, LANES), lambda i: (0, 0))),
        compiler_params=_params(("arbitrary",)), name=name)(y, target)


def _window(ref, c, rows, seq, before, after, keep=None):
    r0 = pl.multiple_of(c * rows, rows)
    parts = []
    if before:
        h0 = pl.multiple_of(jnp.maximum(r0 - before, 0), before)
        halo = ref[pl.ds(h0, before), :].astype(F32)
        halo = halo if keep is None else halo[before - keep:, :]
        parts.append(jnp.where(c > 0, halo, 0.0))
    parts.append(ref[pl.ds(r0, rows), :].astype(F32))
    if after:
        h1 = pl.multiple_of(jnp.minimum(r0 + rows, seq - after), after)
        halo = ref[pl.ds(h1, after), :].astype(F32)
        halo = halo if keep is None else halo[:keep, :]
        parts.append(jnp.where(c < seq // rows - 1, halo, 0.0))
    return parts[0] if len(parts) == 1 else jnp.concatenate(parts, axis=0)


def _lag(x, k):
    return pltpu.roll(x, k, 0) if k else x


def _lead(x, k):
    return pltpu.roll(x, x.shape[0] - k, 0) if k else x


SHIFT_ROWS = 128
POOL_ROWS = 1024
SHIFT_COLS = 256


HALO = 16
KEEP = 8


def _conv3(ext, w, bias):
    acc = bias + w[2:3, :] * ext[KEEP:, :]
    acc = acc + w[1:2, :] * _lag(ext, 1)[KEEP:, :]
    return acc + w[0:1, :] * _lag(ext, 2)[KEEP:, :]


def _ffn_act_fwd(hpre, cw, cb, name):
    b, seq, f2 = hpre.shape
    cbk = SHIFT_COLS
    nj = f2 // (2 * cbk)
    rows = min(SHIFT_ROWS, seq)

    def body(g_ref, v_ref, wg_ref, wv_ref, bg_ref, bv_ref, o_ref, pg_ref, pv_ref):
        def chunk(c, carry):
            gate = _conv3(_window(g_ref, c, rows, seq, HALO, 0, KEEP), wg_ref[...], bg_ref[...])
            val = _conv3(_window(v_ref, c, rows, seq, HALO, 0, KEEP), wv_ref[...], bv_ref[...])
            a = gate * _sigmoid(gate) * val
            here = pl.ds(pl.multiple_of(c * rows, rows), rows)
            o_ref[here, :] = a.astype(BF16)
            pg_ref[here, :] = gate.astype(BF16)
            pv_ref[here, :] = val.astype(BF16)
            return carry

        lax.fori_loop(0, seq // rows, chunk, 0)

    blk = lambda off: pl.BlockSpec((None, seq, cbk), lambda i, j: (i, 0, j + off))
    wsp = lambda r, off: pl.BlockSpec((r, cbk), lambda i, j: (0, j + off))
    half = jax.ShapeDtypeStruct((b, seq, f2 // 2), BF16)
    return pl.pallas_call(
        body, out_shape=(half, half, half), grid=(b, nj),
        in_specs=[blk(0), blk(nj), wsp(FFN_CONV, 0), wsp(FFN_CONV, nj), wsp(1, 0), wsp(1, nj)],
        out_specs=(blk(0), blk(0), blk(0)),
        compiler_params=_params(("parallel", "parallel")), name=name)(hpre, hpre, cw, cw, cb, cb)


def _ffn_act_bwd(hpre, pre_g, pre_v, da, cw, name):
    b, seq, f2 = hpre.shape
    cbk = SHIFT_COLS
    nj = f2 // (2 * cbk)
    rows = min(SHIFT_ROWS, seq)

    def body(g_ref, v_ref, pg_ref, pv_ref, da_ref, wg_ref, wv_ref, og_ref, ov_ref, dwg_ref, dwv_ref, dbg_ref, dbv_ref):
        wg, wv = wg_ref[...], wv_ref[...]

        def back(dpre, w, o_ref, x_ref, c, carry):
            here = pl.ds(pl.multiple_of(c * rows, rows), rows)
            leads = [dpre, _lead(dpre, 1), _lead(dpre, 2)]
            dx = w[2:3, :] * leads[0] + w[1:2, :] * leads[1] + w[0:1, :] * leads[2]
            o_ref[here, :] = dx[:rows, :].astype(BF16)
            x0 = x_ref[here, :].astype(F32)
            return tuple(carry[k] + jnp.sum(leads[k][:rows, :] * x0, axis=0, keepdims=True) for k in range(FFN_CONV)) + (
                carry[FFN_CONV] + jnp.sum(dpre[:rows, :], axis=0, keepdims=True),)

        def chunk(c, carry):
            cg, cv = carry
            gate = _window(pg_ref, c, rows, seq, 0, HALO, KEEP)
            val = _window(pv_ref, c, rows, seq, 0, HALO, KEEP)
            dav = _window(da_ref, c, rows, seq, 0, HALO, KEEP)
            sg = _sigmoid(gate)
            cg = back(dav * val * (sg * (1.0 + gate * (1.0 - sg))), wg, og_ref, g_ref, c, cg)
            cv = back(dav * (gate * sg), wv, ov_ref, v_ref, c, cv)
            return cg, cv

        z = jnp.zeros((1, cbk), F32)
        cg, cv = lax.fori_loop(0, seq // rows, chunk, ((z,) * (FFN_CONV + 1), (z,) * (FFN_CONV + 1)))
        dwg = jnp.concatenate([cg[2], cg[1], cg[0]], axis=0)
        dwv = jnp.concatenate([cv[2], cv[1], cv[0]], axis=0)

        @pl.when(pl.program_id(1) == 0)
        def _():
            dwg_ref[...] = dwg
            dwv_ref[...] = dwv
            dbg_ref[...] = cg[FFN_CONV]
            dbv_ref[...] = cv[FFN_CONV]

        @pl.when(pl.program_id(1) > 0)
        def _():
            dwg_ref[...] += dwg
            dwv_ref[...] += dwv
            dbg_ref[...] += cg[FFN_CONV]
            dbv_ref[...] += cv[FFN_CONV]

    blk = lambda off: pl.BlockSpec((None, seq, cbk), lambda j, i: (i, 0, j + off))
    wsp = lambda r, off: pl.BlockSpec((r, cbk), lambda j, i: (0, j + off))
    half = jax.ShapeDtypeStruct((b, seq, f2 // 2), BF16)
    dwshape = jax.ShapeDtypeStruct((FFN_CONV, f2 // 2), F32)
    dbshape = jax.ShapeDtypeStruct((1, f2 // 2), F32)
    dg, dv, dwg, dwv, dbg, dbv = pl.pallas_call(
        body,
        out_shape=(half, half, dwshape, dwshape, dbshape, dbshape),
        grid=(nj, b),
        in_specs=[blk(0), blk(nj), blk(0), blk(0), blk(0), wsp(FFN_CONV, 0), wsp(FFN_CONV, nj)],
        out_specs=(blk(0), blk(0), wsp(FFN_CONV, 0), wsp(FFN_CONV, 0), wsp(1, 0), wsp(1, 0)),
        compiler_params=_params(("parallel", "arbitrary")), name=name)(hpre, hpre, pre_g, pre_v, da, cw, cw)
    return dg, dv, jnp.concatenate([dwg, dwv], axis=1), jnp.concatenate([dbg, dbv], axis=1)


def _ssd_conv_fwd(zx, cw, cb, d_inner, name):
    b, seq, _ = zx.shape
    xbc = cw.shape[1]
    cbk = SHIFT_COLS
    off = d_inner // cbk
    rows = min(SHIFT_ROWS, seq)

    def body(h_ref, w_ref, b_ref, o_ref, p_ref):
        w = w_ref[...]
        bias = b_ref[...]

        def chunk(c, carry):
            ext = _window(h_ref, c, rows, seq, HALO, 0, KEEP)
            acc = bias + w[3:4, :] * ext[KEEP:, :]
            for k in range(1, SSD_CONV):
                acc = acc + w[3 - k:4 - k, :] * _lag(ext, k)[KEEP:, :]
            here = pl.ds(pl.multiple_of(c * rows, rows), rows)
            o_ref[here, :] = acc * _sigmoid(acc)
            p_ref[here, :] = acc.astype(BF16)
            return carry

        lax.fori_loop(0, seq // rows, chunk, 0)

    blk = pl.BlockSpec((None, seq, cbk), lambda i, j: (i, 0, j))
    return pl.pallas_call(
        body, out_shape=(jax.ShapeDtypeStruct((b, seq, xbc), F32), jax.ShapeDtypeStruct((b, seq, xbc), BF16)),
        grid=(b, xbc // cbk),
        in_specs=[pl.BlockSpec((None, seq, cbk), lambda i, j: (i, 0, j + off)),
                  pl.BlockSpec((SSD_CONV, cbk), lambda i, j: (0, j)),
                  pl.BlockSpec((1, cbk), lambda i, j: (0, j))],
        out_specs=(blk, blk),
        compiler_params=_params(("parallel", "parallel")), name=name)(zx, cw, cb)


def _ssd_conv_bwd(zx, pre, dparts, ddt, dzx, cw, d_inner, name):
    b, seq, zw = zx.shape
    xbc = cw.shape[1]
    cbk = SHIFT_COLS
    off = d_inner // cbk
    rows = min(SHIFT_ROWS, seq)
    nblk = [p.shape[2] // cbk for p in dparts]
    first = [sum(nblk[:s]) for s in range(len(dparts))]
    nconv = xbc // cbk
    ncopy = ddt.shape[2] // cbk
    assert sum(nblk) == nconv and (off + nconv + ncopy) * cbk == zw and dzx.shape == (b, seq, zw)

    def body(h_ref, p_ref, gx_ref, gb_ref, gc_ref, t_ref, w_ref, z_ref, o_ref, dw_ref, db_ref):
        j = pl.program_id(0)

        @pl.when(j < nconv)
        def _():
            conv(h_ref, p_ref, gx_ref, gb_ref, gc_ref, w_ref, o_ref, dw_ref, db_ref)

        @pl.when(j >= nconv)
        def _():
            o_ref[...] = t_ref[...]

    def conv(h_ref, p_ref, gx_ref, gb_ref, gc_ref, w_ref, o_ref, dw_ref, db_ref):
        w = w_ref[...]
        j = pl.program_id(0)

        def chunk(c, carry):
            dws, dbias = carry
            here = pl.ds(pl.multiple_of(c * rows, rows), rows)
            pre = _window(p_ref, c, rows, seq, 0, HALO, KEEP)
            s = _sigmoid(pre)
            gsel = jnp.where(j < first[1], _window(gx_ref, c, rows, seq, 0, HALO, KEEP),
                             jnp.where(j < first[2], _window(gb_ref, c, rows, seq, 0, HALO, KEEP),
                                       _window(gc_ref, c, rows, seq, 0, HALO, KEEP)))
            dpre = gsel * (s * (1.0 + pre * (1.0 - s)))
            leads = [dpre] + [_lead(dpre, k) for k in range(1, SSD_CONV)]
            dx = w[3:4, :] * leads[0]
            for k in range(1, SSD_CONV):
                dx = dx + w[3 - k:4 - k, :] * leads[k]
            o_ref[here, :] = dx[:rows, :].astype(BF16)
            x0 = h_ref[here, :].astype(F32)
            dws = tuple(dws[k] + jnp.sum(leads[k][:rows, :] * x0, axis=0, keepdims=True) for k in range(SSD_CONV))
            dbias = dbias + jnp.sum(dpre[:rows, :], axis=0, keepdims=True)
            return dws, dbias

        z = jnp.zeros((1, cbk), F32)
        dws, dbias = lax.fori_loop(0, seq // rows, chunk, ((z,) * SSD_CONV, z))
        dwv = jnp.concatenate([dws[3 - i] for i in range(SSD_CONV)], axis=0)

        @pl.when(pl.program_id(1) == 0)
        def _():
            dw_ref[...] = dwv
            db_ref[...] = dbias

        @pl.when(pl.program_id(1) > 0)
        def _():
            dw_ref[...] += dwv
            db_ref[...] += dbias

    conv_j = lambda j: jnp.minimum(j, nconv - 1)
    return pl.pallas_call(
        body,
        out_shape=(jax.ShapeDtypeStruct((b, seq, zw), BF16), jax.ShapeDtypeStruct((SSD_CONV, xbc), F32),
                   jax.ShapeDtypeStruct((1, xbc), F32)),
        grid=(nconv + ncopy, b),
        in_specs=[pl.BlockSpec((None, seq, cbk), lambda j, i: (i, 0, conv_j(j) + off)),
                  pl.BlockSpec((None, seq, cbk), lambda j, i: (i, 0, conv_j(j)))] + [
                  pl.BlockSpec((None, seq, cbk), lambda j, i, s=s: (i, 0, jnp.clip(j - first[s], 0, nblk[s] - 1)))
                  for s in range(3)] + [
                  pl.BlockSpec((None, seq, cbk), lambda j, i: (i, 0, jnp.clip(j - nconv, 0, ncopy - 1))),
                  pl.BlockSpec((SSD_CONV, cbk), lambda j, i: (0, conv_j(j))),
                  ANY],
        out_specs=(pl.BlockSpec((None, seq, cbk), lambda j, i: (i, 0, j + off)),
                   pl.BlockSpec((SSD_CONV, cbk), lambda j, i: (0, conv_j(j))),
                   pl.BlockSpec((1, cbk), lambda j, i: (0, conv_j(j)))),
        input_output_aliases={7: 0},
        compiler_params=_params(("arbitrary", "arbitrary")), name=name)(zx, pre, *dparts, ddt, cw, dzx)


def _pool_sums(q, g, lead):
    sh = _lead if lead else _lag
    s2 = q + sh(q, 1)
    s4 = s2 + sh(s2, 2)
    s8 = s4 + sh(s4, 4)
    s16 = s8 + sh(s8, 8)
    return jnp.where(g == 0, s2, jnp.where(g == 1, s4, jnp.where(g == 2, s8, s16)))


def _pool_count(r0, n, g, shape):
    t = (r0 + lax.broadcasted_iota(jnp.int32, shape, 0) + 1).astype(F32)
    return jnp.minimum(t, (2 << g).astype(F32))


def _pool_fwd(h, pw, scale, name):
    b, seq, d = h.shape
    dg = d // 4
    rows = min(POOL_ROWS, seq)

    def body(h_ref, w_ref, s_ref, o_ref):
        g = pl.program_id(1)
        wmat = w_ref[...]
        sc = s_ref[...]

        def chunk(c, carry):
            r0 = c * rows
            ext = _window(h_ref, c, rows, seq, 16, 0)
            sums = _pool_sums(ext, g, False)[16:, :]
            mixed = sums / _pool_count(r0, rows, g, (rows, dg)) - ext[16:, :]
            o_ref[pl.ds(pl.multiple_of(r0, rows), rows), :] = _nn(mixed.astype(BF16), wmat) * sc
            return carry

        lax.fori_loop(0, seq // rows, chunk, 0)

    return pl.pallas_call(
        body, out_shape=jax.ShapeDtypeStruct((b, seq, d), F32), grid=(b, 4),
        in_specs=[pl.BlockSpec((None, seq, dg), lambda i, g: (i, 0, g)),
                  pl.BlockSpec((None, dg, dg), lambda i, g: (g, 0, 0)),
                  pl.BlockSpec((1, dg), lambda i, g: (0, g))],
        out_specs=pl.BlockSpec((None, seq, dg), lambda i, g: (i, 0, g)),
        compiler_params=_params(("parallel", "parallel")), name=name)(h, pw, scale)


def _pool_bwd(h, dout, pw, scale, name):
    b, seq, d = h.shape
    dg = d // 4
    rows = min(POOL_ROWS, seq)

    def body(h_ref, g_ref, w_ref, s_ref, o_ref, dw_ref, ds_ref, dw_acc):
        g = pl.program_id(0)
        wmat = w_ref[...]
        sc = s_ref[...]
        dw_acc[...] = jnp.zeros_like(dw_acc)

        def chunk(c, dsc):
            r0 = c * rows
            ext = _window(h_ref, c, rows, seq, 16, 0)
            sums = _pool_sums(ext, g, False)[16:, :]
            mixed = (sums / _pool_count(r0, rows, g, (rows, dg)) - ext[16:, :]).astype(BF16)
            gext = _window(g_ref, c, rows, seq, 0, 16)
            dsc = dsc + jnp.sum(gext[:rows, :] * _nn(mixed, wmat), axis=0, keepdims=True)
            dpre = (gext * sc).astype(BF16)
            dw_acc[...] += _tn(mixed, dpre[:rows, :])
            dmix = _nt(dpre, wmat)
            q = dmix / _pool_count(r0, rows + 16, g, (rows + 16, dg))
            back = _pool_sums(q, g, True)
            o_ref[pl.ds(pl.multiple_of(r0, rows), rows), :] = back[:rows, :] - dmix[:rows, :]
            return dsc

        dsc = lax.fori_loop(0, seq // rows, chunk, jnp.zeros((1, dg), F32))

        @pl.when(pl.program_id(1) == 0)
        def _():
            dw_ref[...] = dw_acc[...]
            ds_ref[...] = dsc

        @pl.when(pl.program_id(1) > 0)
        def _():
            dw_ref[...] += dw_acc[...]
            ds_ref[...] += dsc

    return pl.pallas_call(
        body,
        out_shape=(jax.ShapeDtypeStruct((b, seq, d), F32), jax.ShapeDtypeStruct((4, dg, dg), F32),
                   jax.ShapeDtypeStruct((1, d), F32)),
        grid=(4, b),
        in_specs=[pl.BlockSpec((None, seq, dg), lambda g, i: (i, 0, g)),
                  pl.BlockSpec((None, seq, dg), lambda g, i: (i, 0, g)),
                  pl.BlockSpec((None, dg, dg), lambda g, i: (g, 0, 0)),
                  pl.BlockSpec((1, dg), lambda g, i: (0, g))],
        out_specs=(pl.BlockSpec((None, seq, dg), lambda g, i: (i, 0, g)),
                   pl.BlockSpec((None, dg, dg), lambda g, i: (g, 0, 0)),
                   pl.BlockSpec((1, dg), lambda g, i: (0, g))),
        scratch_shapes=[pltpu.VMEM((dg, dg), F32)],
        compiler_params=_params(("parallel", "arbitrary")), name=name)(h, dout, pw, scale)


def _head_of(channel):
    return jnp.right_shift(channel, HEAD_DIM.bit_length() - 1)


def _ssd_consts(gw):
    q = CHUNK
    row = lax.broadcasted_iota(jnp.int32, (q, q), 0)
    col = lax.broadcasted_iota(jnp.int32, (q, q), 1)
    tril = (row >= col).astype(BF16)
    triu = (row <= col).astype(BF16)
    e = (_head_of(lax.broadcasted_iota(jnp.int32, (LANES, gw), 1))
         == lax.broadcasted_iota(jnp.int32, (LANES, gw), 0)).astype(BF16)
    et = (_head_of(lax.broadcasted_iota(jnp.int32, (gw, LANES), 0))
          == lax.broadcasted_iota(jnp.int32, (gw, LANES), 1)).astype(BF16)
    return row, col, tril, triu, e, et


def _ssd_common(dtr, dtb, alog, gw):
    q = CHUNK
    row, col, tril, triu, e, et = _ssd_consts(gw)
    dt = _softplus(dtr + dtb)
    a_row = -jnp.exp(alog)
    acum = _sel_left(tril, dt * a_row)
    ac_last = jnp.sum(jnp.where(row == q - 1, acum, 0.0), axis=0, keepdims=True)
    eac = jnp.exp(acum)
    de = jnp.exp(ac_last - acum)
    e2 = jnp.concatenate([e, e], axis=0)
    expand = _sel_right(jnp.concatenate([dt, eac, de], axis=0), e2, 2)
    dt_x, eac_x, de_x = expand[0:q], expand[q:2 * q], expand[2 * q:3 * q]
    acum_t = acum.T
    cd_col = jnp.exp(acum_t[:, q - 1:q])
    et3 = jnp.concatenate([et, et, et], axis=1)
    cdmat = _nn(et3, jnp.concatenate(_split(jnp.broadcast_to(cd_col, (LANES, D_STATE)), 3), axis=0))
    consts = dict(row=row, col=col, tril=tril, triu=triu, e=e, et=et)
    return dt, a_row, acum, acum_t, ac_last, eac, de, dt_x, eac_x, de_x, cdmat, consts


def _decay(acum, acum_t, j, row, col):
    diff = acum[:, j:j + 1] - acum_t[j:j + 1, :]
    return jnp.exp(jnp.where(row >= col, diff, -1e30))


def _ssd_fwd(xc, zx, dtr, dtb, alog, dskip, nw, d_inner, name):
    b, seq, xbc = xc.shape
    q = CHUNK
    nc = seq // q
    gw = d_inner // N_GROUPS
    nh = gw // HEAD_DIM
    xb0 = d_inner // D_STATE
    xc0 = xb0 + N_GROUPS

    nb = max(n for n in (4, 2, 1) if b % n == 0)

    def body(x_ref, b_ref, c_ref, z_ref, dtr_ref, dtb_ref, al_ref, dsk_ref, nw_ref, y_ref, yn_ref, st_ref, s_ref):
        @pl.when(pl.program_id(2) == 0)
        def _():
            s_ref[...] = jnp.zeros_like(s_ref)

        for s in range(nb):
            one(s, x_ref.at[s], b_ref.at[s], c_ref.at[s], z_ref.at[s], dtr_ref.at[s], dtb_ref, al_ref, dsk_ref, nw_ref,
                y_ref.at[s], yn_ref.at[s], st_ref.at[s], s_ref.at[s])

    def one(s, x_ref, b_ref, c_ref, z_ref, dtr_ref, dtb_ref, al_ref, dsk_ref, nw_ref, y_ref, yn_ref, st_ref, s_ref):
        prev = s_ref[...]
        st_ref[...] = prev
        x = x_ref[...]
        bm = b_ref[...].astype(BF16)
        cm = c_ref[...].astype(BF16)
        (dt, a_row, acum, acum_t, ac_last, eac, de, dt_x, eac_x, de_x, cdmat, k) = _ssd_common(
            dtr_ref[...], dtb_ref[0:1, :], al_ref[0:1, :], gw)
        xdt = x * dt_x
        xdt_b = xdt.astype(BF16)
        cb = _nt(cm, bm)
        half = _head_of(lax.broadcasted_iota(jnp.int32, (q, LANES), 1))
        pairs = []
        for j in range(nh):
            pc = (j // 2) * LANES
            m = (cb * _decay(acum, acum_t, j, k["row"], k["col"])).astype(BF16)
            yj = jnp.where(half == j % 2, _nn(m, xdt_b[:, pc:pc + LANES]), 0.0)
            if j % 2 == 0:
                pairs.append(yj)
            else:
                pairs[-1] = pairs[-1] + yj
        prev_b = prev.astype(BF16)
        y = dsk_ref[0:1, :] * x + jnp.concatenate(pairs, axis=1) + eac_x * _nt(cm, prev_b)
        s_ref[...] = cdmat * prev + _tn((xdt * de_x).astype(BF16), bm)
        y_ref[...] = y
        z = z_ref[...].astype(F32)
        yg = y * (z * _sigmoid(z))
        r = lax.rsqrt(jnp.mean(yg * yg, axis=-1, keepdims=True) + EPS)
        yn_ref[...] = ((yg * r) * nw_ref[0:1, :]).astype(BF16)

    par = lambda w: pl.BlockSpec((None, 8, w), lambda i, g, c: (g, 0, 0))
    return pl.pallas_call(
        body,
        out_shape=(jax.ShapeDtypeStruct((b, seq, d_inner), F32), jax.ShapeDtypeStruct((b, seq, d_inner), BF16),
                   jax.ShapeDtypeStruct((b, nc, N_GROUPS, gw, D_STATE), F32)),
        grid=(b // nb, N_GROUPS, nc),
        in_specs=[pl.BlockSpec((nb, q, gw), lambda i, g, c: (i, c, g)),
                  pl.BlockSpec((nb, q, D_STATE), lambda i, g, c: (i, c, xb0 + g)),
                  pl.BlockSpec((nb, q, D_STATE), lambda i, g, c: (i, c, xc0 + g)),
                  pl.BlockSpec((nb, q, gw), lambda i, g, c: (i, c, g)),
                  pl.BlockSpec((nb, q, LANES), lambda i, g, c: (i, c, g)),
                  par(LANES), par(LANES), par(gw), par(gw)],
        out_specs=(pl.BlockSpec((nb, q, gw), lambda i, g, c: (i, c, g)),
                   pl.BlockSpec((nb, q, gw), lambda i, g, c: (i, c, g)),
                   pl.BlockSpec((nb, None, None, gw, D_STATE), lambda i, g, c: (i, c, g, 0, 0))),
        scratch_shapes=[pltpu.VMEM((nb, gw, D_STATE), F32)],
        compiler_params=_params(("parallel", "parallel", "arbitrary")), name=name,
    )(xc, xc, xc, zx, dtr, dtb, alog, dskip, nw)


def _ssd_bwd(xc, zx, dtr, y, dyn, st, dtb, alog, dskip, nw, d_inner, name):
    b, seq, xbc = xc.shape
    q = CHUNK
    nc = seq // q
    gw = d_inner // N_GROUPS
    nh = gw // HEAD_DIM
    xb0 = d_inner // D_STATE
    xc0 = xb0 + N_GROUPS

    nb = max(n for n in (4, 2, 1) if b % n == 0)

    def body(x_ref, b_ref, c_ref, z_ref, dtr_ref, y_ref, g_ref, st_ref, dtb_ref, al_ref, dsk_ref, nw_ref,
             dz_ref, dx_ref, db_ref, dc_ref, ddt_ref, dnw_ref, dd_ref, dal_ref, dbias_ref,
             ds_ref, colbuf, rowbuf):
        first = jnp.logical_and(pl.program_id(1) == 0, pl.program_id(2) == 0)

        @pl.when(pl.program_id(2) == 0)
        def _():
            ds_ref[...] = jnp.zeros_like(ds_ref)

        sums = [one(x_ref.at[s], b_ref.at[s], c_ref.at[s], z_ref.at[s], dtr_ref.at[s], y_ref.at[s], g_ref.at[s],
                    st_ref.at[s], dtb_ref, al_ref, dsk_ref, nw_ref, dz_ref.at[s], dx_ref.at[s], db_ref.at[s],
                    dc_ref.at[s], ddt_ref.at[s], ds_ref.at[s], colbuf.at[s], rowbuf.at[s]) for s in range(nb)]
        dnw, dd, dal, dbias = [functools.reduce(lambda p, r: p + r, [sm[i] for sm in sums]) for i in range(4)]

        @pl.when(first)
        def _():
            dnw_ref[...] = jnp.broadcast_to(dnw, (8, gw))
            dd_ref[...] = dd
            dal_ref[...] = jnp.broadcast_to(dal, (8, LANES))
            dbias_ref[...] = jnp.broadcast_to(dbias, (8, LANES))

        @pl.when(jnp.logical_not(first))
        def _():
            dnw_ref[...] += jnp.broadcast_to(dnw, (8, gw))
            dd_ref[...] += dd
            dal_ref[...] += jnp.broadcast_to(dal, (8, LANES))
            dbias_ref[...] += jnp.broadcast_to(dbias, (8, LANES))

    def one(x_ref, b_ref, c_ref, z_ref, dtr_ref, y_ref, g_ref, st_ref, dtb_ref, al_ref, dsk_ref, nw_ref,
            dz_ref, dx_ref, db_ref, dc_ref, ddt_ref, ds_ref, colbuf, rowbuf):
        x = x_ref[...]
        bm = b_ref[...].astype(BF16)
        cm = c_ref[...].astype(BF16)
        z = z_ref[...].astype(F32)
        y = y_ref[...]
        prev = st_ref[...]
        dtr = dtr_ref[...] + dtb_ref[0:1, :]
        (dt, a_row, acum, acum_t, ac_last, eac, de, dt_x, eac_x, de_x, cdmat, k) = _ssd_common(
            dtr_ref[...], dtb_ref[0:1, :], al_ref[0:1, :], gw)
        row, col = k["row"], k["col"]
        et2 = jnp.concatenate([k["et"], k["et"]], axis=0)

        sz = _sigmoid(z)
        silu_z = z * sz
        yg = y * silu_z
        r = lax.rsqrt(jnp.mean(yg * yg, axis=-1, keepdims=True) + EPS)
        xh = yg * r
        dyn = g_ref[...].astype(F32)
        gh = dyn * nw_ref[0:1, :]
        dyg = r * (gh - xh * jnp.mean(gh * xh, axis=-1, keepdims=True))
        dnw = jnp.sum(dyn * xh, axis=0, keepdims=True)
        g = dyg * silu_z
        dz_ref[...] = (dyg * y * (sz * (1.0 + z * (1.0 - sz)))).astype(BF16)
        dd = _sel_right(jnp.broadcast_to(jnp.sum(g * x, axis=0, keepdims=True), (8, gw)), et2, 2)

        xdt = x * dt_x
        xdt_b = xdt.astype(BF16)
        g_b = g.astype(BF16)
        prev_b = prev.astype(BF16)
        cb = _nt(cm, bm)

        cp = _nt(cm, prev_b)
        ge = g * eac_x
        dac = _sel_right(ge * cp, et2, 2)
        ge_b = ge.astype(BF16)
        dcm = _nn(ge_b, prev_b)
        dprev = _tn(ge_b, cm)

        colbuf[...] = jnp.zeros_like(colbuf)
        rowbuf[...] = jnp.zeros_like(rowbuf)
        dcb = jnp.zeros((q, q), F32)
        half = _head_of(lax.broadcasted_iota(jnp.int32, (q, LANES), 1))
        pairs = []
        for j in range(nh):
            pc = (j // 2) * LANES
            dec = _decay(acum, acum_t, j, row, col)
            m = cb * dec
            gj = jnp.where(half == j % 2, g[:, pc:pc + LANES], 0.0).astype(BF16)
            dm = _nt(gj, xdt_b[:, pc:pc + LANES])
            w = dm * m
            colbuf[:, j:j + 1] = jnp.sum(w, axis=1, keepdims=True)
            rowbuf[j:j + 1, :] = jnp.sum(w, axis=0, keepdims=True)
            dcb = dcb + dm * dec
            dj = jnp.where(half == j % 2, _tn(m.astype(BF16), g_b[:, pc:pc + LANES]), 0.0)
            if j % 2 == 0:
                pairs.append(dj)
            else:
                pairs[-1] = pairs[-1] + dj
        dxdt = jnp.concatenate(pairs, axis=1)
        dcb_b = dcb.astype(BF16)
        dcm = dcm + _nn(dcb_b, bm)
        dbm = _tn(dcb_b, cm)

        ds = ds_ref[...]
        ds_b = ds.astype(BF16)
        u = _nt(bm, ds_b)
        dxdt = dxdt + u * de_x
        dde = _sel_right(u * xdt, et2, 2)
        dbm = dbm + _nn((xdt * de_x).astype(BF16), ds_b)
        pm = jnp.concatenate(_split(ds * prev, 2), axis=1)
        t2 = _tn(pm, k["et"])
        dcd_row = jnp.sum(t2[0:D_STATE] + t2[D_STATE:2 * D_STATE], axis=0, keepdims=True)
        last = dcd_row * jnp.exp(ac_last) + jnp.sum(dde * de, axis=0, keepdims=True)
        dac = dac + colbuf[...] - rowbuf[...].T - dde * de + jnp.where(row == q - 1, last, 0.0)
        ds_ref[...] = cdmat * ds + dprev

        dadt = _sel_left(k["triu"], dac)
        ddt = _sel_right(dxdt * x, et2, 2) + dadt * a_row
        dal = jnp.sum(dadt * dt, axis=0, keepdims=True) * a_row
        lane = lax.broadcasted_iota(jnp.int32, (q, LANES), 1)
        ddtr = jnp.where(lane < nh, ddt * _sigmoid(dtr), 0.0)
        ddt_ref[...] = ddtr.astype(BF16)
        dbias = jnp.sum(ddtr, axis=0, keepdims=True)
        dx_ref[...] = dxdt * dt_x + dsk_ref[0:1, :] * g
        db_ref[...] = dbm
        dc_ref[...] = dcm
        return dnw, dd, dal, dbias

    rc = lambda c: nc - 1 - c
    par = lambda w: pl.BlockSpec((None, 8, w), lambda g, i, c: (g, 0, 0))
    blk = lambda w: pl.BlockSpec((nb, q, w), lambda g, i, c: (i, rc(c), g))
    return pl.pallas_call(
        body,
        out_shape=(jax.ShapeDtypeStruct((b, seq, zx.shape[2]), BF16),
                   jax.ShapeDtypeStruct((b, seq, d_inner), F32),
                   jax.ShapeDtypeStruct((b, seq, N_GROUPS * D_STATE), F32),
                   jax.ShapeDtypeStruct((b, seq, N_GROUPS * D_STATE), F32),
                   jax.ShapeDtypeStruct((b, seq, N_GROUPS * LANES), BF16),
                   jax.ShapeDtypeStruct((N_GROUPS, 8, gw), F32),
                   jax.ShapeDtypeStruct((N_GROUPS, 8, LANES), F32),
                   jax.ShapeDtypeStruct((N_GROUPS, 8, LANES), F32),
                   jax.ShapeDtypeStruct((N_GROUPS, 8, LANES), F32)),
        grid=(N_GROUPS, b // nb, nc),
        in_specs=[blk(gw),
                  pl.BlockSpec((nb, q, D_STATE), lambda g, i, c: (i, rc(c), xb0 + g)),
                  pl.BlockSpec((nb, q, D_STATE), lambda g, i, c: (i, rc(c), xc0 + g)),
                  blk(gw),
                  pl.BlockSpec((nb, q, LANES), lambda g, i, c: (i, rc(c), g)),
                  blk(gw), blk(gw),
                  pl.BlockSpec((nb, None, None, gw, D_STATE), lambda g, i, c: (i, rc(c), g, 0, 0)),
                  par(LANES), par(LANES), par(gw), par(gw)],
        out_specs=(blk(gw), blk(gw), blk(D_STATE), blk(D_STATE), blk(LANES),
                   par(gw), par(LANES), par(LANES), par(LANES)),
        scratch_shapes=[pltpu.VMEM((nb, gw, D_STATE), F32), pltpu.VMEM((nb, q, LANES), F32),
                        pltpu.VMEM((nb, LANES, q), F32)],
        compiler_params=_params(("parallel", "arbitrary", "arbitrary")), name=name,
    )(xc, xc, xc, zx, dtr, y, dyn, st, dtb, alog, dskip, nw)


def _adamw(w, g, m, v, name):
    rows, cols = w.shape
    tr = rows
    for cand in (512, 256, 128, 64, 32, 16, 8):
        if rows % cand == 0 and cand * cols * 4 <= 2 * 1024 * 1024:
            tr = cand
            break
    c1 = 1.0 - ADAM_B1 ** ADAM_STEP
    c2 = 1.0 - ADAM_B2 ** ADAM_STEP

    def body(w_ref, g_ref, m_ref, v_ref, d_ref, mo_ref, vo_ref, go_ref):
        gv = g_ref[...]
        go_ref[...] = gv
        mn = ADAM_B1 * m_ref[...] + (1.0 - ADAM_B1) * gv
        vn = ADAM_B2 * v_ref[...] + (1.0 - ADAM_B2) * (gv * gv)
        mo_ref[...] = mn
        vo_ref[...] = vn
        d_ref[...] = -ADAM_LR * ((mn / c1) / (jnp.sqrt(vn / c2) + ADAM_EPS) + ADAM_WD * w_ref[...])

    spec = pl.BlockSpec((tr, cols), lambda i: (i, 0))
    shp = jax.ShapeDtypeStruct((rows, cols), F32)
    return pl.pallas_call(body, out_shape=(shp, shp, shp, shp), grid=(rows // tr,), in_specs=[spec] * 4,
                          out_specs=(spec,) * 4, compiler_params=_params(("parallel",)), name=name)(w, g, m, v)


def _pick_rows(rows, row_bytes, limit=1 << 20):
    for cand in (2048, 1024, 512, 256, 128, 64, 32, 16):
        if rows % cand == 0 and cand * row_bytes <= limit:
            return cand
    return rows


def _as3d(a, lead):
    return a.reshape(a.shape[:lead] + (-1, a.shape[-1]))


def _pair_sum(g, got, core, name):
    h = got.shape[0]
    g3, got3 = _as3d(g, 1), _as3d(got, 1)
    _, rows, cols = got3.shape
    tr = _pick_rows(rows, cols * 4)

    def body(c_ref, g_ref, r_ref, o_ref):
        o_ref[...] = (g_ref[...] + r_ref[...]).astype(BF16)

    out = pl.pallas_call(
        body, out_shape=jax.ShapeDtypeStruct(got3.shape, BF16),
        grid_spec=pltpu.PrefetchScalarGridSpec(
            num_scalar_prefetch=1, grid=(h, rows // tr),
            in_specs=[pl.BlockSpec((None, tr, cols), lambda l, i, c_ref: (c_ref[0] * h + l, i, 0)),
                      pl.BlockSpec((None, tr, cols), lambda l, i, c_ref: (l, i, 0))],
            out_specs=pl.BlockSpec((None, tr, cols), lambda l, i, c_ref: (l, i, 0))),
        compiler_params=_params(("parallel", "parallel")), name=name)(core, g3, got3)
    return out.reshape(got.shape)


def _sum4(q, core, name):
    q4 = _as3d(q, 2)
    _, h, rows, cols = q4.shape
    tr = _pick_rows(rows, cols * 4)

    def body(c_ref, q0, q1, q2, q3, o_ref):
        o_ref[...] = ((q0[...].astype(F32) + q1[...].astype(F32)) + q2[...].astype(F32)) + q3[...].astype(F32)

    out = pl.pallas_call(
        body, out_shape=jax.ShapeDtypeStruct((2 * h, rows, cols), F32),
        grid_spec=pltpu.PrefetchScalarGridSpec(
            num_scalar_prefetch=1, grid=(h, rows // tr),
            in_specs=[pl.BlockSpec((None, None, tr, cols), lambda l, i, c_ref, k=k: (k, l, i, 0))
                      for k in range(N_CHIPS)],
            out_specs=pl.BlockSpec((None, tr, cols), lambda l, i, c_ref: (c_ref[0] * h + l, i, 0))),
        compiler_params=_params(("parallel", "parallel")), name=name)(core, q4, q4, q4, q4)
    return out.reshape((2 * h,) + q.shape[2:])


def _coords():
    return lax.axis_index("x"), lax.axis_index("y"), lax.axis_index("c")


def _other_chips(x, y):
    return [(1 - x, y), (x, 1 - y), (1 - x, 1 - y)]


def _allgather_halves(src, name):
    rows, cols = src.shape

    def body(x_ref, o_ref, send, recv, local):
        x, y, c = _coords()
        sib = (x, y, 1 - c)
        chips = _other_chips(x, y)

        def slot(h, cx, cy):
            return o_ref.at[h, 2 * cx + cy]

        def copy(kk, dst, to, src_ref):
            return pltpu.make_async_remote_copy(src_ref=src_ref, dst_ref=dst, send_sem=send.at[kk],
                                                recv_sem=recv.at[kk], device_id=to, device_id_type=MESH)

        mine = pltpu.make_async_copy(x_ref, slot(c, x, y), local)
        mine.start()
        first = [copy(0, slot(c, x, y), sib, x_ref)]
        first += [copy(1 + j, slot(c, x, y), (*chip, c), x_ref) for j, chip in enumerate(chips)]
        for cp in first:
            cp.start()
        passed = [copy(4 + j, slot(c, *chip), sib, slot(c, *chip)) for j, chip in enumerate(chips)]
        for j, chip in enumerate(chips):
            copy(1 + j, slot(c, *chip), (x, y, c), x_ref).wait_recv()
            passed[j].start()
        copy(0, slot(1 - c, x, y), (x, y, c), x_ref).wait_recv()
        for j, chip in enumerate(chips):
            copy(4 + j, slot(1 - c, *chip), (x, y, c), x_ref).wait_recv()
        for cp in first + passed:
            cp.wait_send()
        mine.wait()

    return pl.pallas_call(
        body, out_shape=jax.ShapeDtypeStruct((2, N_CHIPS, rows, cols), src.dtype),
        in_specs=[ANY], out_specs=ANY,
        scratch_shapes=[pltpu.SemaphoreType.DMA((7,)), pltpu.SemaphoreType.DMA((7,)), pltpu.SemaphoreType.DMA],
        name=name)(src)


MIXW = (("ssd_w_in", None), ("ssd_w_out", 0), ("pool_w", 1))
FFNW = (("ffn_w_up", 1), ("ffn_w_down", 0))


def _chip_window(axis, ref, layers, k):
    if axis is None:
        return ref.at[layers, k]
    n = ref.shape[1 + axis] // N_CHIPS
    sl = pl.ds(pl.multiple_of(k * n, LANES if 1 + axis == len(ref.shape) - 1 else 8), n)
    idx = [layers] + [slice(None)] * (len(ref.shape) - 1)
    idx[1 + axis] = sl
    return ref.at[tuple(idx)]


def _full_shape(axis, shard_shape):
    if axis is None:
        return (shard_shape[0], N_CHIPS) + tuple(shard_shape[1:])
    full = list(shard_shape)
    full[1 + axis] *= N_CHIPS
    return tuple(full)


HBM_SPEC = pl.BlockSpec(memory_space=pltpu.HBM)
SEM_SPEC = pl.BlockSpec(memory_space=pltpu.SEMAPHORE)


def _dma_sems(count):
    return pltpu.SemaphoreType.DMA((max(count, 1),))


def _wait_for(copy, kind):
    if kind == "recv":
        copy.wait_recv()
    elif kind == "send":
        copy.wait_send()
    else:
        copy.wait()


def _comm_fused(stages, counts, srcs, lands, name, inplace=False):
    ns, nl, k = len(srcs), len(lands), len(stages)

    def body(*refs):
        src_refs = refs[:ns]
        land_refs = refs[ns + (nl if inplace else 0):ns + (nl if inplace else 0) + nl]
        sem_refs = refs[len(refs) - 3 * k:]
        for s, stage_fn in enumerate(stages):
            starts, waits = stage_fn(src_refs, land_refs, tuple(sem_refs[3 * s:3 * s + 3]))
            for cp in starts:
                cp.start()
            for cp, kind in waits:
                _wait_for(cp, kind)

    scratch = []
    for cnt in counts:
        scratch += [_dma_sems(c) for c in cnt]
    outs = pl.pallas_call(
        body, out_shape=tuple(jax.ShapeDtypeStruct(a.shape, a.dtype) for a in lands),
        in_specs=[ANY] * (ns + (nl if inplace else 0)), out_specs=(ANY,) * nl,
        input_output_aliases={ns + i: i for i in range(nl)} if inplace else {},
        scratch_shapes=scratch, name=name)(*srcs, *(lands if inplace else ()))
    return list(outs)


class _SplitComm:
    def __init__(self, stages, counts, srcs, lands, name):
        self.stages, self.counts, self.name = stages, counts, name
        self.ns = len(srcs)
        self.data = [pltpu.with_memory_space_constraint(a, pltpu.HBM) for a in list(srcs) + list(lands)]
        self.sems = None
        self.step = 0

    def advance(self, after=None):
        i, k, nd, ns = self.step, len(self.stages), len(self.data), self.ns
        first, last = i == 0, i == k
        stages = self.stages
        after = list(after) if isinstance(after, (list, tuple)) else [after]

        def body(*refs):
            data = refs[:nd]
            pos = nd
            if not first:
                old = tuple(refs[pos:pos + 3])
                pos += 3 + len(after)
            if not last:
                new = tuple(refs[pos:pos + 3])
            if not first:
                for cp, kind in stages[i - 1](data[:ns], data[ns:], old)[1]:
                    _wait_for(cp, kind)
            if not last:
                for cp in stages[i](data[:ns], data[ns:], new)[0]:
                    cp.start()
                refs[len(refs) - 1][...] = jnp.zeros((8, LANES), F32)

        args = list(self.data)
        in_specs = [HBM_SPEC] * nd
        if not first:
            args += list(self.sems) + after
            in_specs += [SEM_SPEC] * 3 + [ANY] * len(after)
        out_shape, out_specs = [], []
        if not last:
            out_shape += [_dma_sems(c) for c in self.counts[i]]
            out_specs += [SEM_SPEC] * 3
        out_shape += [pltpu.HBM(a.shape, a.dtype) for a in self.data]
        out_specs += [HBM_SPEC] * nd
        if not last:
            out_shape.append(jax.ShapeDtypeStruct((8, LANES), F32))
            out_specs.append(pl.BlockSpec(memory_space=pltpu.VMEM))
        off = 0 if last else 3
        outs = pl.pallas_call(
            body, out_shape=tuple(out_shape), in_specs=in_specs, out_specs=tuple(out_specs),
            input_output_aliases={d: off + d for d in range(nd)},
            compiler_params=pltpu.CompilerParams(has_side_effects=pltpu.SideEffectType.DATAFLOW_SIDE_EFFECTING),
            name=f"{self.name}_{i}")(*args)
        self.sems = None if last else outs[:3]
        self.data = list(outs[off:off + nd])
        self.step += 1
        return None if last else outs[len(outs) - 1]

    def lands(self):
        return self.data[self.ns:]


def _gather_stages(spec):
    n = len(spec)

    def parts(srcs, lands):
        x, y, c = _coords()
        out = []
        for w, (_, axis) in enumerate(spec):
            h = srcs[w].shape[0] // 2
            mine, theirs = pl.ds(c * h, h), pl.ds((1 - c) * h, h)
            out.append((srcs[w].at[mine], lambda layers, k, w=w, axis=axis: _chip_window(axis, lands[w], layers, k),
                        mine, theirs))
        return x, y, c, 2 * x + y, (x, y, 1 - c), _other_chips(x, y), out

    def remote(src, dst, send, recv, idx, to):
        return pltpu.make_async_remote_copy(src_ref=src, dst_ref=dst, send_sem=send.at[idx], recv_sem=recv.at[idx],
                                            device_id=to, device_id_type=MESH)

    def stage0(srcs, lands, sems):
        send, recv, local = sems
        x, y, c, me, sib, chips, ps = parts(srcs, lands)
        starts, waits = [], []
        for w, (src, dst, mine, theirs) in enumerate(ps):
            lc = pltpu.make_async_copy(src, dst(mine, me), local.at[w])
            first = [remote(src, dst(mine, me), send, recv, 4 * w, sib)]
            first += [remote(src, dst(mine, me), send, recv, 4 * w + 1 + j, (cx, cy, c)) for j, (cx, cy) in enumerate(chips)]
            starts += [lc] + first
            waits.append((remote(src, dst(theirs, me), send, recv, 4 * w, (x, y, c)), "recv"))
            waits += [(remote(src, dst(mine, 2 * cx + cy), send, recv, 4 * w + 1 + j, (x, y, c)), "recv")
                      for j, (cx, cy) in enumerate(chips)]
            waits += [(cp, "send") for cp in first] + [(lc, "local")]
        return starts, waits

    def stage1(srcs, lands, sems):
        send, recv, _ = sems
        x, y, c, me, sib, chips, ps = parts(srcs, lands)
        starts, waits = [], []
        for w, (src, dst, mine, theirs) in enumerate(ps):
            for j, (cx, cy) in enumerate(chips):
                blk = dst(mine, 2 * cx + cy)
                fwd = remote(blk, blk, send, recv, 3 * w + j, sib)
                starts.append(fwd)
                waits.append((remote(src, dst(theirs, 2 * cx + cy), send, recv, 3 * w + j, (x, y, c)), "recv"))
                waits.append((fwd, "send"))
        return starts, waits

    return [stage0, stage1], [(4 * n, 4 * n, n), (3 * n, 3 * n, 0)]


def _swap_stages(spec):
    n = len(spec)

    def stage(srcs, lands, sems):
        send, recv, _ = sems
        x, y, c = _coords()
        starts, waits = [], []
        for w in range(n):
            h = srcs[w].shape[0] // 2
            cp = pltpu.make_async_remote_copy(src_ref=srcs[w].at[pl.ds((1 - c) * h, h)], dst_ref=lands[w],
                                              send_sem=send.at[w], recv_sem=recv.at[w],
                                              device_id=(x, y, 1 - c), device_id_type=MESH)
            starts.append(cp)
            waits += [(cp, "recv"), (cp, "send")]
        return starts, waits

    return [stage], [(n, n, 0)]


def _scatter_stages(spec):
    n = len(spec)

    def stage(srcs, lands, sems):
        send, recv, local = sems
        x, y, c = _coords()
        me = 2 * x + y
        starts, waits = [], []
        for w, (_, axis) in enumerate(spec):
            layers = pl.ds(0, srcs[w].shape[0])
            own = _chip_window(axis, srcs[w], layers, me)
            lc = pltpu.make_async_copy(own, lands[w].at[me], local.at[w])
            starts.append(lc)
            for j, (cx, cy) in enumerate(_other_chips(x, y)):
                cp = pltpu.make_async_remote_copy(src_ref=_chip_window(axis, srcs[w], layers, 2 * cx + cy),
                                                  dst_ref=lands[w].at[me], send_sem=send.at[3 * w + j],
                                                  recv_sem=recv.at[3 * w + j], device_id=(cx, cy, c), device_id_type=MESH)
                starts.append(cp)
                waits.append((pltpu.make_async_remote_copy(
                    src_ref=own, dst_ref=lands[w].at[2 * cx + cy], send_sem=send.at[3 * w + j], recv_sem=recv.at[3 * w + j],
                    device_id=(x, y, c), device_id_type=MESH), "recv"))
                waits.append((cp, "send"))
            waits.append((lc, "local"))
        return starts, waits

    return [stage], [(3 * n, 3 * n, n)]


def _share_stages(spec):
    n = len(spec)

    def stage(srcs, lands, sems):
        send, recv, _ = sems
        x, y, c = _coords()
        starts, waits = [], []
        for w in range(n):
            h = lands[w].shape[0] // 2
            mine, theirs = lands[w].at[pl.ds(c * h, h)], lands[w].at[pl.ds((1 - c) * h, h)]
            cp = pltpu.make_async_remote_copy(src_ref=mine, dst_ref=mine, send_sem=send.at[w], recv_sem=recv.at[w],
                                              device_id=(x, y, 1 - c), device_id_type=MESH)
            starts.append(cp)
            waits.append((pltpu.make_async_remote_copy(src_ref=theirs, dst_ref=theirs, send_sem=send.at[w],
                                                       recv_sem=recv.at[w], device_id=(x, y, c), device_id_type=MESH),
                          "recv"))
            waits.append((cp, "send"))
        return starts, waits

    return [stage], [(n, n, 0)]


def _shard_of(p, axis):
    if axis is None:
        return (p.shape[0],) + tuple(p.shape[2:])
    s = list(p.shape)
    s[1 + axis] //= N_CHIPS
    return tuple(s)


def _gather8_stages():
    def stage(srcs, lands, sems):
        send, recv, local = sems
        x, y, c = _coords()
        me = 4 * x + 2 * y + c
        lc = pltpu.make_async_copy(srcs[0], lands[0].at[me], local.at[0])
        starts, waits = [lc], []
        for kk in range(1, 8):
            to = (1 - x if kk & 4 else x, 1 - y if kk & 2 else y, 1 - c if kk & 1 else c)
            cp = pltpu.make_async_remote_copy(src_ref=srcs[0], dst_ref=lands[0].at[me], send_sem=send.at[kk - 1],
                                              recv_sem=recv.at[kk - 1], device_id=to, device_id_type=MESH)
            starts.append(cp)
            waits.append((pltpu.make_async_remote_copy(
                src_ref=srcs[0], dst_ref=lands[0].at[4 * to[0] + 2 * to[1] + to[2]], send_sem=send.at[kk - 1],
                recv_sem=recv.at[kk - 1], device_id=(x, y, c), device_id_type=MESH), "recv"))
            waits.append((cp, "send"))
        waits.append((lc, "local"))
        return starts, waits

    return [stage], [(7, 7, 1)]


def _sum8(buf, name):
    _, rows, cols = buf.shape
    tr = _pick_rows(rows, cols * 4)

    def body(*refs):
        acc = refs[0][...]
        for r in refs[1:8]:
            acc = acc + r[...]
        refs[8][...] = acc

    return pl.pallas_call(
        body, out_shape=jax.ShapeDtypeStruct((rows, cols), F32), grid=(rows // tr,),
        in_specs=[pl.BlockSpec((None, tr, cols), lambda i, k=k: (k, i, 0)) for k in range(8)],
        out_specs=pl.BlockSpec((tr, cols), lambda i: (i, 0)),
        compiler_params=_params(("parallel",)), name=name)(*([buf] * 8))


def _reduce_begin(spec, gs, core, tag, riders=(), during=None):
    stages, counts = _swap_stages(spec)
    got_shapes = [jax.ShapeDtypeStruct((g.shape[0] // 2,) + g.shape[1:], g.dtype) for g in gs]
    if during is None:
        got = _comm_fused(stages, counts, list(gs) + list(riders), got_shapes, "swap_" + tag)
    else:
        swap = _SplitComm(stages, counts, list(gs) + list(riders), [lax.empty(s.shape, s.dtype) for s in got_shapes],
                          "swap_" + tag)
        swap.advance(after=during(swap.advance()))
        gs, got = swap.data[:len(gs)], swap.lands()
    pair = [_pair_sum(a, r, core, "pair_sum_" + n) for a, r, (n, _) in zip(gs, got, spec)]
    stages, counts = _scatter_stages(spec)
    lands = [lax.empty((N_CHIPS,) + _shard_of(p, axis), p.dtype) for p, (_, axis) in zip(pair, spec)]
    comm = _SplitComm(stages, counts, pair, lands, "scatter_" + tag)
    return comm, comm.advance()


def _reduce_finish(spec, comm, core, tag, after):
    comm.advance(after=after)
    halves = [_sum4(q, core, "sum4_" + n) for q, (n, _) in zip(comm.lands(), spec)]
    stages, counts = _share_stages(spec)
    return _comm_fused(stages, counts, [], halves, "share_" + tag, inplace=True)


SMALL = (("ssd_conv_w", 2), ("pool_scale", 1), ("ffn_conv_w", 2))
REPL = ("ssd_conv_b", "ssd_dt_bias", "ssd_a_log", "ssd_d", "ssd_norm_w", "ffn_conv_b",
        "norm_mix_pre", "norm_mix_post", "norm_ffn_pre", "norm_ffn_post")
WEIGHTS = ("ssd_w_in", "ssd_conv_w", "ssd_conv_b", "ssd_dt_bias", "ssd_a_log", "ssd_d", "ssd_norm_w", "ssd_w_out",
           "pool_w", "pool_scale", "ffn_w_up", "ffn_conv_w", "ffn_conv_b", "ffn_w_down", "norm_mix_pre",
           "norm_mix_post", "norm_ffn_pre", "norm_ffn_post")


def _flat_rows(n):
    unit = 2 * 16 * FLAT_COLS
    return 2 * 16 * ((n + unit - 1) // unit)


def _flatten_shards(arrs, dtype):
    flat = jnp.concatenate([a.astype(dtype).reshape(-1) for a in arrs])
    rows = _flat_rows(flat.shape[0])
    flat = jnp.pad(flat, (0, rows * FLAT_COLS - flat.shape[0]))
    return flat.reshape(2, rows // 2, FLAT_COLS)


def _unflatten_full(gathered, shard_shapes, axes):
    per_chip = jnp.swapaxes(gathered, 0, 1).reshape(N_CHIPS, -1)
    out, off = [], 0
    for shp, ax in zip(shard_shapes, axes):
        n = math.prod(shp)
        pieces = [per_chip[k, off:off + n].reshape(shp) for k in range(N_CHIPS)]
        out.append(jnp.concatenate(pieces, axis=ax))
        off += n
    return out


def kernel(x, ssd_w_in, ssd_conv_w, ssd_conv_b, ssd_dt_bias, ssd_a_log, ssd_d, ssd_norm_w, ssd_w_out, pool_w, pool_scale, ffn_w_up, ffn_conv_w, ffn_conv_b, ffn_w_down, norm_mix_pre, norm_mix_post, norm_ffn_pre, norm_ffn_post, loss_target, m_ssd_w_in, m_ssd_conv_w, m_ssd_conv_b, m_ssd_dt_bias, m_ssd_a_log, m_ssd_d, m_ssd_norm_w, m_ssd_w_out, m_pool_w, m_pool_scale, m_ffn_w_up, m_ffn_conv_w, m_ffn_conv_b, m_ffn_w_down, m_norm_mix_pre, m_norm_mix_post, m_norm_ffn_pre, m_norm_ffn_post, v_ssd_w_in, v_ssd_conv_w, v_ssd_conv_b, v_ssd_dt_bias, v_ssd_a_log, v_ssd_d, v_ssd_norm_w, v_ssd_w_out, v_pool_w, v_pool_scale, v_ffn_w_up, v_ffn_conv_w, v_ffn_conv_b, v_ffn_w_down, v_norm_mix_pre, v_norm_mix_post, v_norm_ffn_pre, v_norm_ffn_post):
    wts = dict(ssd_w_in=ssd_w_in, ssd_conv_w=ssd_conv_w, ssd_conv_b=ssd_conv_b, ssd_dt_bias=ssd_dt_bias,
               ssd_a_log=ssd_a_log, ssd_d=ssd_d, ssd_norm_w=ssd_norm_w, ssd_w_out=ssd_w_out, pool_w=pool_w,
               pool_scale=pool_scale, ffn_w_up=ffn_w_up, ffn_conv_w=ffn_conv_w, ffn_conv_b=ffn_conv_b,
               ffn_w_down=ffn_w_down, norm_mix_pre=norm_mix_pre, norm_mix_post=norm_mix_post,
               norm_ffn_pre=norm_ffn_pre, norm_ffn_post=norm_ffn_post)
    mom = dict(ssd_w_in=m_ssd_w_in, ssd_conv_w=m_ssd_conv_w, ssd_conv_b=m_ssd_conv_b, ssd_dt_bias=m_ssd_dt_bias,
               ssd_a_log=m_ssd_a_log, ssd_d=m_ssd_d, ssd_norm_w=m_ssd_norm_w, ssd_w_out=m_ssd_w_out, pool_w=m_pool_w,
               pool_scale=m_pool_scale, ffn_w_up=m_ffn_w_up, ffn_conv_w=m_ffn_conv_w, ffn_conv_b=m_ffn_conv_b,
               ffn_w_down=m_ffn_w_down, norm_mix_pre=m_norm_mix_pre, norm_mix_post=m_norm_mix_post,
               norm_ffn_pre=m_norm_ffn_pre, norm_ffn_post=m_norm_ffn_post)
    var = dict(ssd_w_in=v_ssd_w_in, ssd_conv_w=v_ssd_conv_w, ssd_conv_b=v_ssd_conv_b, ssd_dt_bias=v_ssd_dt_bias,
               ssd_a_log=v_ssd_a_log, ssd_d=v_ssd_d, ssd_norm_w=v_ssd_norm_w, ssd_w_out=v_ssd_w_out, pool_w=v_pool_w,
               pool_scale=v_pool_scale, ffn_w_up=v_ffn_w_up, ffn_conv_w=v_ffn_conv_w, ffn_conv_b=v_ffn_conv_b,
               ffn_w_down=v_ffn_w_down, norm_mix_pre=v_norm_mix_pre, norm_mix_post=v_norm_mix_post,
               norm_ffn_pre=v_norm_ffn_pre, norm_ffn_post=v_norm_ffn_post)

    bl, seq, d = x.shape
    t = bl * seq
    depth = norm_mix_pre.shape[0]
    n_ssd = ssd_w_out.shape[0]
    d_inner = ssd_w_out.shape[1] * N_CHIPS
    nheads = d_inner // HEAD_DIM
    hpg = nheads // N_GROUPS
    gw = d_inner // N_GROUPS
    xbc = ssd_conv_w.shape[2] * N_CHIPS
    f2 = ffn_w_up.shape[2] * N_CHIPS
    ff = f2 // 2
    dg = d // 4
    cy = lax.axis_index("c")
    chip = 2 * lax.axis_index("x") + lax.axis_index("y")

    small_shapes = [wts[n].shape for n, _ in SMALL]
    small_axes = [a for _, a in SMALL]
    small_flat = _flatten_shards([wts[n] for n, _ in SMALL], F32)
    small_half = lax.dynamic_index_in_dim(small_flat, cy, 0, keepdims=False)
    small_all = _allgather_halves(small_half, "gather_small")
    conv_w, p_scale, f_conv_w = _unflatten_full(small_all, small_shapes, small_axes)
    def full_shapes(spec, shards):
        return [jax.ShapeDtypeStruct(_full_shape(axis, s.shape), s.dtype) for s, (_, axis) in zip(shards, spec)]

    def row_halves(a):
        return a.reshape((2, a.shape[0] // 2) + a.shape[1:])

    def join_w_in(g):
        return jnp.concatenate([g[:, k] for k in range(N_CHIPS)], axis=-1).reshape(d, -1)

    def join_w_out(g):
        r2 = g.shape[1] // N_CHIPS
        return jnp.concatenate([g[hf, k * r2:(k + 1) * r2] for k in range(N_CHIPS) for hf in range(2)], axis=0)

    ssd_spec = (("ssd_w_in", None), ("ssd_w_out", 0))
    first_shards = [row_halves(wts[n][0].astype(BF16)) for n, _ in ssd_spec]
    stages, counts = _gather_stages(ssd_spec)
    g_in0, g_out0 = _comm_fused(stages, counts, first_shards, full_shapes(ssd_spec, first_shards), "gather_first")
    w_in, w_out = [join_w_in(g_in0)], [join_w_out(g_out0)]
    rest_spec = ssd_spec * (n_ssd - 1) + (("pool_w", 1),) + FFNW
    rest_shards = [row_halves(wts[n][jj].astype(BF16)) for jj in range(1, n_ssd) for n, _ in ssd_spec]
    rest_shards += [wts["pool_w"].astype(BF16)] + [wts[n].astype(BF16) for n, _ in FFNW]
    stages, counts = _gather_stages(rest_spec)
    ffn_gather = _SplitComm(stages, counts, rest_shards + [g_out0],
                            [lax.empty(s.shape, s.dtype) for s in full_shapes(rest_spec, rest_shards)], "gather_rest")
    gather_token = ffn_gather.advance()

    def pad_heads(a):
        lead = a.shape[:-1]
        a = a.reshape(lead + (N_GROUPS, hpg))
        a = jnp.pad(a, [(0, 0)] * len(lead) + [(0, 0), (0, LANES - hpg)])
        return a.reshape(lead + (N_GROUPS * LANES,))

    def unpad_heads(a):
        lead = a.shape[:-1]
        return a.reshape(lead + (N_GROUPS, LANES))[..., :hpg].reshape(lead + (nheads,))

    def group_rows(a, width):
        return jnp.broadcast_to(a.reshape(N_GROUPS, 1, width), (N_GROUPS, 8, width))

    def pad_w_in(w):
        return jnp.concatenate([w[..., :d_inner + xbc], pad_heads(w[..., d_inner + xbc:])], axis=-1)

    w_in_p = [pad_w_in(w_in[0])]
    zw = w_in_p[0].shape[-1]
    w_pool = None

    x2 = x.reshape(t, d)
    tgt2 = loss_target.reshape(t, d)
    w_up = w_down = None

    saved = []
    cur = x2
    tokens = []
    h = _norm_fwd(cur, norm_mix_pre[0:1], BF16, "norm_pre_b", after=[gather_token])
    for i in range(depth):
        j = i // 2
        sv = dict(x_in=cur)
        if i % 2 == 0:
            zx = _mm(h, w_in_p[j], "nn", BF16, "mm_ssd_in", 2048, 512, d).reshape(bl, seq, zw)
            dtr = _mm(h, w_in_p[j][:, d_inner + xbc:], "nn", F32, "mm_ssd_dt", 2048, 512, d).reshape(bl, seq, -1)
            xc, xpre = _ssd_conv_fwd(zx, conv_w[j], ssd_conv_b[j:j + 1], d_inner, "ssd_conv_fwd")
            dtb = group_rows(pad_heads(ssd_dt_bias[j]), LANES)
            alog = group_rows(pad_heads(ssd_a_log[j]), LANES)
            dskip = group_rows(jnp.repeat(ssd_d[j], HEAD_DIM), gw)
            nw = group_rows(ssd_norm_w[j], gw)
            y, yn, st = _ssd_fwd(xc, zx, dtr, dtb, alog, dskip, nw, d_inner, "ssd_fwd")
            if i == 0:
                tokens.append(ffn_gather.advance(after=yn))
            mix = _mm(yn.reshape(t, d_inner), w_out[j], "nn", F32, "mm_ssd_out", 2048, 512, d_inner)
            sv.update(h=h, zx=zx, dtr=dtr, xc=xc, xpre=xpre, y=y, yn=yn, st=st, dtb=dtb, alog=alog, dskip=dskip, nw=nw)
        else:
            mix = _pool_fwd(h.reshape(bl, seq, d), w_pool[j], p_scale[j:j + 1], "pool_fwd").reshape(t, d)
            sv.update(h=h)
        sv.update(mix=mix)
        mid, u = _norm_post_pre(mix, norm_mix_post[i:i + 1], cur, norm_ffn_pre[i:i + 1], BF16, "norm_post_pre_b",
                                after=tokens)
        tokens = []
        if i == 0:
            ffn_gather.advance(after=u)
            rest = ffn_gather.lands()
            for jj in range(1, n_ssd):
                w_in_p.append(pad_w_in(join_w_in(rest[2 * (jj - 1)])))
                w_out.append(join_w_out(rest[2 * (jj - 1) + 1]))
            w_pool, w_up, w_down = rest[2 * (n_ssd - 1):]
        hpre = _mm(u, w_up, "nn", BF16, "mm_up", 2048, 512, d, b_layer=i).reshape(bl, seq, f2)
        act, pre_g, pre_v = _ffn_act_fwd(hpre, f_conv_w[i], ffn_conv_b[i:i + 1], "ffn_act_fwd")
        act = act.reshape(t, ff)
        fo = _mm(act, w_down, "nn", F32, "mm_down", 2048, 512, ff, b_layer=i)
        if i + 1 == depth:
            cur = _norm_fwd(fo, norm_ffn_post[i:i + 1], F32, "norm_post", resid=mid)
        elif i % 2 == 0:
            cur, h = _norm_post_pre(fo, norm_ffn_post[i:i + 1], mid, norm_mix_pre[i + 1:i + 2], F32, "norm_post_pre_f")
        else:
            cur, h = _norm_post_pre(fo, norm_ffn_post[i:i + 1], mid, norm_mix_pre[i + 1:i + 2], BF16, "norm_post_pre_b")
        sv.update(mid=mid, u=u, hpre=hpre, pre_g=pre_g, pre_v=pre_v, act=act, fo=fo)
        saved.append(sv)

    dcur, loss_part = _loss_head(cur, tgt2, "loss_head")

    g = {n: [None] * wts[n].shape[0] for n in WEIGHTS}
    gbuf = dict(up=lax.empty((depth, d, f2), F32), down=lax.empty((depth, ff, d), F32),
                out=lax.empty((n_ssd, d_inner, d), F32), win=lax.empty((n_ssd, d, zw), F32))
    core = cy.reshape(1).astype(jnp.int32)

    def mixer_bwd(i, dmid, dmix, behind=()):
        j = i // 2
        sv = saved[i]
        done = []
        if i % 2 == 0:
            dyn = _mm(dmix, w_out[j], "nt", BF16, "mm_ssd_out_dx", 1024, 1024, d, after=behind)
            gbuf["out"], tok = _mm(sv["yn"].reshape(t, d_inner), dmix, "tn", F32, "mm_ssd_out_dw", 1024, 1024, 2048,
                                   out_buf=(gbuf["out"], j))
            done.append(tok)
            dz, dxs, dbm, dcm, ddt, dnw, dd, dal, dbias = _ssd_bwd(
                sv["xc"], sv["zx"], sv["dtr"], sv["y"], dyn.reshape(bl, seq, d_inner), sv["st"], sv["dtb"], sv["alog"],
                sv["dskip"], sv["nw"], d_inner, "ssd_bwd")
            g["ssd_norm_w"][j] = dnw[:, 0, :].reshape(d_inner)
            g["ssd_d"][j] = dd[:, 0, :hpg].reshape(nheads)
            g["ssd_a_log"][j] = dal[:, 0, :hpg].reshape(nheads)
            g["ssd_dt_bias"][j] = dbias[:, 0, :hpg].reshape(nheads)
            dzx, dcw, dcb = _ssd_conv_bwd(sv["zx"], sv["xpre"], (dxs, dbm, dcm), ddt, dz, conv_w[j], d_inner,
                                          "ssd_conv_bwd")
            g["ssd_conv_w"][j] = dcw
            g["ssd_conv_b"][j] = dcb[0]
            dzx = dzx.reshape(t, zw)
            dh = _mm(dzx, w_in_p[j], "nt", BF16, "mm_ssd_in_dx", 1024, d, zw // 2)
            gbuf["win"], tok = _mm(sv["h"], dzx, "tn", F32, "mm_ssd_in_dw", 1024, zw // 4, 2048, out_buf=(gbuf["win"], j))
            done.append(tok)
        else:
            dh3, g["pool_w"][j], dps = _pool_bwd(sv["h"].reshape(bl, seq, d), dmix.reshape(bl, seq, d), w_pool[j],
                                                 p_scale[j:j + 1], "pool_bwd")
            g["pool_scale"][j] = dps[0]
            dh = dh3.reshape(t, d)
        if i == 0:
            dx_in, g["norm_mix_pre"][i] = _norm_bwd(sv["x_in"], norm_mix_pre[i:i + 1], dh, F32, "norm_bwd_r", resid=dmid,
                                                    after=done)
            return dx_in, None
        dx_in, dfo_prev, g["norm_mix_pre"][i], g["norm_ffn_post"][i - 1] = _norm_bwd2(
            sv["x_in"], norm_mix_pre[i:i + 1], dh, dmid, saved[i - 1]["fo"], norm_ffn_post[i - 1:i], BF16,
            "norm_bwd_in_post", after=done)
        return dx_in, dfo_prev

    ffn_comm = None
    dfo, g["norm_ffn_post"][depth - 1] = _norm_bwd(saved[depth - 1]["fo"], norm_ffn_post[depth - 1:depth], dcur, BF16,
                                                   "norm_bwd_b")
    for i in reversed(range(depth)):
        sv = saved[i]
        dact = _mm(dfo, w_down, "nt", BF16, "mm_down_dx", 1024, ff // 2, d, b_layer=i)
        gbuf["down"], tok_down = _mm(sv["act"], dfo, "tn", F32, "mm_down_dw", ff // 2, 1024, 2048,
                                     out_buf=(gbuf["down"], i))
        dhg, dhv, dcw, dcb = _ffn_act_bwd(sv["hpre"], sv["pre_g"], sv["pre_v"], dact.reshape(bl, seq, ff), f_conv_w[i],
                                          "ffn_act_bwd")
        g["ffn_conv_w"][i] = dcw
        g["ffn_conv_b"][i] = dcb[0]
        dhs = [dhg.reshape(t, ff), dhv.reshape(t, ff)]
        du = _mm(dhs, w_up, "nt", BF16, "mm_up_dx", 1024, d, ff, b_layer=i)
        gbuf["up"], tok_up = _mm(sv["u"], dhs, "tn", F32, "mm_up_dw", 1024, ff // 2, 2048, out_buf=(gbuf["up"], i))
        def pre_post(behind, i=i, sv=sv, du=du, dcur=dcur):
            return _norm_bwd2(sv["mid"], norm_ffn_pre[i:i + 1], du, dcur, sv["mix"], norm_mix_post[i:i + 1],
                              BF16 if i % 2 == 0 else F32,
                              "norm_bwd_pre_post_b" if i % 2 == 0 else "norm_bwd_pre_post_f", after=behind)

        if i > 0:
            dmid, dmix, g["norm_ffn_pre"][i], g["norm_mix_post"][i] = pre_post([tok_down, tok_up])
            dcur, dfo = mixer_bwd(i, dmid, dmix)
        else:
            held = []

            def during(token):
                held.extend(pre_post([tok_down, tok_up, token]))
                return held[0]

            ffn_comm, ffn_token = _reduce_begin(FFNW, [gbuf["up"], gbuf["down"]], core, "ffn", during=during)
            dmid, dmix, g["norm_ffn_pre"][i], g["norm_mix_post"][i] = held
            dcur, dfo = mixer_bwd(i, dmid, dmix, behind=[ffn_token])

    grad_x = dcur.reshape(bl, seq, d)
    for n in ("norm_mix_pre", "norm_mix_post", "norm_ffn_pre", "norm_ffn_post"):
        g[n] = [a[0] for a in g[n]]
    small_names = [n for n, _ in SMALL] + list(REPL)
    full = {n: jnp.stack(g[n], axis=0) for n in small_names}

    g_in = jnp.concatenate([gbuf["win"][..., :d_inner + xbc], unpad_heads(gbuf["win"][..., d_inner + xbc:])], axis=-1)
    g_in_cm = jnp.swapaxes(g_in.reshape(n_ssd, d, N_CHIPS, -1), 1, 2)
    vec = jnp.concatenate([full[n].reshape(-1) for n in small_names] + [loss_part[0, :1]])
    nvec = vec.shape[0]
    vrows = 16 * ((nvec + 16 * FLAT_COLS - 1) // (16 * FLAT_COLS))
    vec = jnp.pad(vec, (0, vrows * FLAT_COLS - nvec)).reshape(vrows, FLAT_COLS)
    stages, counts = _gather8_stages()
    small_comm = _SplitComm(stages, counts, [vec], [lax.empty((8, vrows, FLAT_COLS), F32)], "gather_small_grads")
    small_token = small_comm.advance()
    mix_comm, mix_token = _reduce_begin(MIXW, [g_in_cm, gbuf["out"], jnp.stack(g["pool_w"], axis=0)], core, "mixers",
                                        riders=[small_token])

    grads, deltas, new_m, new_v = {}, {}, {}, {}

    def adamw(n, gr):
        shp = wts[n].shape
        two = (math.prod(shp[:-1]), shp[-1])
        dl, mn, vn, go = _adamw(wts[n].reshape(two), gr.reshape(two), mom[n].reshape(two), var[n].reshape(two),
                                "adamw_" + n)
        grads[n], deltas[n], new_m[n], new_v[n] = go.reshape(shp), dl.reshape(shp), mn.reshape(shp), vn.reshape(shp)
        return dl

    small_comm.advance(after=mix_token)
    tot = _sum8(small_comm.lands()[0], "sum_small").reshape(-1)
    small_grads, off = {}, 0
    for n in small_names:
        cnt = math.prod(full[n].shape)
        small_grads[n] = tot[off:off + cnt].reshape(full[n].shape)
        off += cnt
    loss = tot[off]
    for n, ax in SMALL:
        w = wts[n].shape[ax]
        small_grads[n] = lax.dynamic_slice_in_dim(small_grads[n], chip * w, w, axis=ax)

    behind = [adamw(n, small_grads[n]) for n in small_names][-1:]
    ffn_grads = _reduce_finish(FFNW, ffn_comm, core, "ffn", after=mix_token)
    behind += [adamw(n, gr) for gr, (n, _) in zip(ffn_grads, FFNW)]
    mix_grads = _reduce_finish(MIXW, mix_comm, core, "mixers", after=behind)
    for gr, (n, _) in zip(mix_grads, MIXW):
        adamw(n, gr)

    return (loss, grad_x, *[grads[n] for n in WEIGHTS], *[deltas[n] for n in WEIGHTS],
            *[new_m[n] for n in WEIGHTS], *[new_v[n] for n in WEIGHTS])
```

```python
import functools
import math

import jax
import jax.numpy as jnp
from jax import lax
from jax.experimental import pallas as pl
from jax.experimental.pallas import tpu as pltpu

F32 = jnp.float32
BF16 = jnp.bfloat16
MESH = pl.DeviceIdType.MESH
ANY = pl.BlockSpec(memory_space=pl.ANY)

HEAD_DIM = 64
D_STATE = 128
CHUNK = 128
N_GROUPS = 4
SSD_CONV = 4
FFN_CONV = 3
EPS = 1e-6
N_CHIPS = 4
LANES = 128
FLAT_COLS = 1024

ADAM_LR = 0.001
ADAM_B1 = 0.9
ADAM_B2 = 0.999
ADAM_EPS = 1e-08
ADAM_WD = 0.01
ADAM_STEP = 10

VMEM_LIMIT_BYTES = 56 * 1024 * 1024


def _params(sem=None):
    kw = dict(vmem_limit_bytes=VMEM_LIMIT_BYTES)
    if sem is not None:
        kw["dimension_semantics"] = sem
    return pltpu.CompilerParams(**kw)


def _sigmoid(x):
    return 0.5 * jnp.tanh(0.5 * x) + 0.5


def _softplus(x):
    return jnp.maximum(x, 0.0) + jnp.log(1.0 + jnp.exp(-jnp.abs(x)))


def _dot(a, b, dn):
    return lax.dot_general(a, b, (dn, ((), ())), preferred_element_type=F32)


def _nn(a, b):
    return _dot(a, b, ((1,), (0,)))


def _nt(a, b):
    return _dot(a, b, ((1,), (1,)))


def _tn(a, b):
    return _dot(a, b, ((0,), (0,)))


def _split(x, parts):
    out = []
    r = x
    for _ in range(parts):
        p = r.astype(BF16)
        out.append(p)
        r = r - p.astype(F32)
    return out


def _sel_left(sel, x, parts=3):
    n = x.shape[1]
    r = _nn(sel, jnp.concatenate(_split(x, parts), axis=1))
    out = r[:, 0:n]
    for i in range(1, parts):
        out = out + r[:, i * n:(i + 1) * n]
    return out


def _sel_right(x, sel_stacked, parts=3):
    return _nn(jnp.concatenate(_split(x, parts), axis=1), sel_stacked)


def _mm(a, b, dims, out_dtype, name, tm, tn, tk, b_layer=None, out_buf=None, after=()):
    a_list = list(a) if isinstance(a, (list, tuple)) else [a]
    b_list = list(b) if isinstance(b, (list, tuple)) else [b]
    if dims in ("nn", "nt"):
        assert len(b_list) == 1
        m = a_list[0].shape[0]
        segs = [x.shape[1] for x in a_list]
        k = sum(segs)
        bshape = b_list[0].shape[-2:]
        n = bshape[1] if dims == "nn" else bshape[0]
        assert (bshape[0] if dims == "nn" else bshape[1]) == k
    else:
        assert len(a_list) == 1 and b_layer is None
        k, m = a_list[0].shape
        segs = [x.shape[1] for x in b_list]
        n = sum(segs)
    nseg = len(segs)
    tm, tn = min(tm, m), min(tn, n)
    if dims == "tn":
        tk = min(tk, k)
        tn = min(tn, min(segs))
        units = [tn] * nseg
        nk = k // tk
        assert k % tk == 0
    else:
        units = [min(u, s) for u, s in zip(tk if isinstance(tk, (list, tuple)) else [tk] * nseg, segs)]
        nk = sum(s // u for s, u in zip(segs, units))
    assert m % tm == 0 and n % tn == 0 and all(s % u == 0 for s, u in zip(segs, units)), (name, m, n, k, segs, units)
    counts = [s // u for s, u in zip(segs, units)]
    starts = [sum(counts[:s]) for s in range(nseg)]
    assert all(sum(segs[:s]) % units[s] == 0 for s in range(nseg)), (name, segs, units)
    first_block = [sum(segs[:s]) // units[s] for s in range(nseg)]
    dn = {"nn": ((1,), (0,)), "nt": ((1,), (1,)), "tn": ((0,), (0,))}[dims]

    same = len(set(units)) == 1
    nb_ops = len(b_list) if dims == "tn" else (1 if same else nseg)

    def body(*refs):
        a_refs = refs[:len(a_list)]
        b_refs = refs[len(a_list):len(a_list) + nb_ops]
        rest = refs[len(a_list) + nb_ops + (0 if out_buf is None else 1) + len(after):]
        o_ref = rest[0]
        if out_buf is not None:
            rest[1][...] = jnp.zeros((8, LANES), F32)
            rest = rest[1:]
        acc = rest[1] if nk > 1 else None
        kk = pl.program_id(2)
        sel = kk if dims != "tn" else pl.program_id(1)

        def step(a_ref, b_ref):
            p = _dot(a_ref[...].astype(BF16), b_ref[...].astype(BF16), dn)
            if nk == 1:
                o_ref[...] = p.astype(out_dtype)
                return

            @pl.when(kk == 0)
            def _():
                acc[...] = p

            @pl.when(kk > 0)
            def _():
                acc[...] += p

        if nseg == 1:
            step(a_refs[0], b_refs[0])
        else:
            for s in range(nseg):
                @pl.when(jnp.logical_and(sel >= starts[s], sel < starts[s] + counts[s]))
                def _(s=s):
                    step(a_refs[s] if dims != "tn" else a_refs[0], b_refs[s if nb_ops > 1 else 0])

        if nk > 1:
            @pl.when(kk == nk - 1)
            def _():
                o_ref[...] = acc[...].astype(out_dtype)

    def seg_index(v, s):
        return v if nseg == 1 else jnp.clip(v - starts[s], 0, counts[s] - 1)

    lead = () if b_layer is None else (b_layer,)
    none = () if b_layer is None else (None,)
    def b_block(kk, s):
        return kk if same else first_block[s] + seg_index(kk, s)

    if dims == "nn":
        a_specs = [pl.BlockSpec((tm, units[s]), lambda i, j, kk, s=s: (i, seg_index(kk, s))) for s in range(nseg)]
        b_specs = [pl.BlockSpec(none + (units[s], tn), lambda i, j, kk, s=s: lead + (b_block(kk, s), j))
                   for s in range(nb_ops)]
    elif dims == "nt":
        a_specs = [pl.BlockSpec((tm, units[s]), lambda i, j, kk, s=s: (i, seg_index(kk, s))) for s in range(nseg)]
        b_specs = [pl.BlockSpec(none + (tn, units[s]), lambda i, j, kk, s=s: lead + (j, b_block(kk, s)))
                   for s in range(nb_ops)]
    else:
        a_specs = [pl.BlockSpec((tk, tm), lambda i, j, kk: (kk, i))]
        b_specs = [pl.BlockSpec((tk, tn), lambda i, j, kk, s=s: (kk, seg_index(j, s))) for s in range(nseg)]
    args = a_list + (b_list * nb_ops if dims != "tn" else b_list)
    in_specs = a_specs + b_specs
    aliases = {}
    if out_buf is None:
        out_shape = jax.ShapeDtypeStruct((m, n), out_dtype)
        out_spec = pl.BlockSpec((tm, tn), lambda i, j, kk: (i, j))
    else:
        buf, slab = out_buf
        assert buf.shape[1:] == (m, n) and buf.dtype == out_dtype
        out_shape = (jax.ShapeDtypeStruct(buf.shape, out_dtype), jax.ShapeDtypeStruct((8, LANES), F32))
        out_spec = (pl.BlockSpec((None, tm, tn), lambda i, j, kk: (slab, i, j)),
                    pl.BlockSpec((8, LANES), lambda i, j, kk: (0, 0)))
        aliases = {len(args): 0}
        args = args + [buf]
        in_specs = in_specs + [ANY]
    after = [x for x in after if x is not None]
    args = args + after
    in_specs = in_specs + [ANY] * len(after)
    return pl.pallas_call(
        body,
        out_shape=out_shape,
        grid=(m // tm, n // tn, nk),
        in_specs=in_specs,
        out_specs=out_spec,
        scratch_shapes=[] if nk == 1 else [pltpu.VMEM((tm, tn), F32)],
        input_output_aliases=aliases,
        compiler_params=_params(("parallel", "parallel", "arbitrary") if out_buf is None else ("arbitrary",) * 3),
        name=name,
    )(*args)


def _row_tile(t, want):
    tm = min(want, t)
    assert t % tm == 0
    return tm


def _norm_fwd(x, w, out_dtype, name, resid=None, after=()):
    t, d = x.shape
    tm = _row_tile(t, 512)
    after = [a for a in after if a is not None]

    def body(*refs):
        refs = refs[:len(refs) - 1 - len(after)] + refs[len(refs) - 1:]
        if resid is None:
            x_ref, w_ref, o_ref = refs
        else:
            x_ref, w_ref, r_ref, o_ref = refs
        xv = x_ref[...]
        r = lax.rsqrt(jnp.mean(xv * xv, axis=-1, keepdims=True) + EPS)
        y = (xv * r) * w_ref[...]
        if resid is not None:
            y = r_ref[...] + y
        o_ref[...] = y.astype(out_dtype)

    row = pl.BlockSpec((tm, d), lambda i: (i, 0))
    vec = pl.BlockSpec((1, d), lambda i: (0, 0))
    args = [x, w] + ([] if resid is None else [resid]) + after
    return pl.pallas_call(
        body, out_shape=jax.ShapeDtypeStruct((t, d), out_dtype), grid=(t // tm,),
        in_specs=[row, vec] + ([] if resid is None else [row]) + [ANY] * len(after), out_specs=row,
        compiler_params=_params(("parallel",)), name=name)(*args)


def _norm_post_pre(m, w_post, resid, w_pre, pre_dtype, name, after=()):
    t, d = m.shape
    tm = _row_tile(t, 512)
    after = [a for a in after if a is not None]

    def body(m_ref, w1_ref, r_ref, w2_ref, *rest):
        x_ref, u_ref = rest[len(after):]
        mv = m_ref[...]
        r1 = lax.rsqrt(jnp.mean(mv * mv, axis=-1, keepdims=True) + EPS)
        xv = r_ref[...] + (mv * r1) * w1_ref[...]
        x_ref[...] = xv
        r2 = lax.rsqrt(jnp.mean(xv * xv, axis=-1, keepdims=True) + EPS)
        u_ref[...] = ((xv * r2) * w2_ref[...]).astype(pre_dtype)

    row = pl.BlockSpec((tm, d), lambda i: (i, 0))
    vec = pl.BlockSpec((1, d), lambda i: (0, 0))
    return pl.pallas_call(
        body, out_shape=(jax.ShapeDtypeStruct((t, d), F32), jax.ShapeDtypeStruct((t, d), pre_dtype)), grid=(t // tm,),
        in_specs=[row, vec, row, vec] + [ANY] * len(after), out_specs=(row, row),
        compiler_params=_params(("parallel",)), name=name)(m, w_post, resid, w_pre, *after)


def _norm_bwd(src, w, dy, out_dtype, name, resid=None, after=()):
    t, d = src.shape
    tm = _row_tile(t, 512)
    after = [a for a in after if a is not None]

    def body(*refs):
        refs = refs[:len(refs) - 2 - len(after)] + refs[len(refs) - 2:]
        if resid is None:
            x_ref, w_ref, g_ref, o_ref, dw_ref = refs
        else:
            x_ref, w_ref, g_ref, r_ref, o_ref, dw_ref = refs
        xv = x_ref[...]
        g = g_ref[...].astype(F32)
        r = lax.rsqrt(jnp.mean(xv * xv, axis=-1, keepdims=True) + EPS)
        xh = xv * r
        gh = g * w_ref[...]
        mean = jnp.mean(gh * xh, axis=-1, keepdims=True)
        dx = r * (gh - xh * mean)
        if resid is not None:
            dx = r_ref[...] + dx
        o_ref[...] = dx.astype(out_dtype)
        part = jnp.sum(g * xh, axis=0, keepdims=True)

        @pl.when(pl.program_id(0) == 0)
        def _():
            dw_ref[...] = part

        @pl.when(pl.program_id(0) > 0)
        def _():
            dw_ref[...] += part

    row = pl.BlockSpec((tm, d), lambda i: (i, 0))
    vec = pl.BlockSpec((1, d), lambda i: (0, 0))
    args = [src, w, dy] + ([] if resid is None else [resid]) + after
    return pl.pallas_call(
        body,
        out_shape=(jax.ShapeDtypeStruct((t, d), out_dtype), jax.ShapeDtypeStruct((1, d), F32)),
        grid=(t // tm,),
        in_specs=[row, vec, row] + ([] if resid is None else [row]) + [ANY] * len(after),
        out_specs=(row, vec),
        compiler_params=_params(("arbitrary",)), name=name)(*args)


def _norm_bwd2(src1, w1, dy1, resid, src2, w2, out2_dtype, name, after=()):
    t, d = src1.shape
    tm = _row_tile(t, 512)
    after = [a for a in after if a is not None]

    def back(xv, w, g):
        r = lax.rsqrt(jnp.mean(xv * xv, axis=-1, keepdims=True) + EPS)
        xh = xv * r
        gh = g * w
        return r * (gh - xh * jnp.mean(gh * xh, axis=-1, keepdims=True)), jnp.sum(g * xh, axis=0, keepdims=True)

    def body(x1_ref, w1_ref, g1_ref, r_ref, x2_ref, w2_ref, *rest):
        d1_ref, d2_ref, dw1_ref, dw2_ref = rest[len(after):]
        d1, p1 = back(x1_ref[...], w1_ref[...], g1_ref[...].astype(F32))
        d1 = r_ref[...] + d1
        d1_ref[...] = d1
        d2, p2 = back(x2_ref[...], w2_ref[...], d1)
        d2_ref[...] = d2.astype(out2_dtype)

        @pl.when(pl.program_id(0) == 0)
        def _():
            dw1_ref[...] = p1
            dw2_ref[...] = p2

        @pl.when(pl.program_id(0) > 0)
        def _():
            dw1_ref[...] += p1
            dw2_ref[...] += p2

    row = pl.BlockSpec((tm, d), lambda i: (i, 0))
    vec = pl.BlockSpec((1, d), lambda i: (0, 0))
    return pl.pallas_call(
        body,
        out_shape=(jax.ShapeDtypeStruct((t, d), F32), jax.ShapeDtypeStruct((t, d), out2_dtype),
                   jax.ShapeDtypeStruct((1, d), F32), jax.ShapeDtypeStruct((1, d), F32)),
        grid=(t // tm,),
        in_specs=[row, vec, row, row, row, vec] + [ANY] * len(after),
        out_specs=(row, row, vec, vec),
        compiler_params=_params(("arbitrary",)), name=name)(src1, w1, dy1, resid, src2, w2, *after)


def _loss_head(y, target, name):
    t, d = y.shape
    tm = _row_tile(t, 512)

    def body(y_ref, t_ref, dy_ref, l_ref):
        e = y_ref[...] - t_ref[...]
        dy_ref[...] = e * (1.0 / d)
        col = jnp.sum(e * e, axis=0, keepdims=True)
        s = jnp.sum(col, axis=1, keepdims=True) * (0.5 / d)
        part = jnp.broadcast_to(s, (1, LANES))

        @pl.when(pl.program_id(0) == 0)
        def _():
            l_ref[...] = part

        @pl.when(pl.program_id(0) > 0)
        def _():
            l_ref[...] += part

    row = pl.BlockSpec((tm, d), lambda i: (i, 0))
    return pl.pallas_call(
        body,
        out_shape=(jax.ShapeDtypeStruct((t, d), F32), jax.ShapeDtypeStruct((1, LANES), F32)),
        grid=(t // tm,), in_specs=[row, row],
        out_specs=(row, pl.BlockSpec((1, LANES), lambda i: (0, 0))),
        compiler_params=_params(("arbitrary",)), name=name)(y, target)


def _window(ref, c, rows, seq, before, after, keep=None):
    r0 = pl.multiple_of(c * rows, rows)
    parts = []
    if before:
        h0 = pl.multiple_of(jnp.maximum(r0 - before, 0), before)
        halo = ref[pl.ds(h0, before), :].astype(F32)
        halo = halo if keep is None else halo[before - keep:, :]
        parts.append(jnp.where(c > 0, halo, 0.0))
    parts.append(ref[pl.ds(r0, rows), :].astype(F32))
    if after:
        h1 = pl.multiple_of(jnp.minimum(r0 + rows, seq - after), after)
        halo = ref[pl.ds(h1, after), :].astype(F32)
        halo = halo if keep is None else halo[:keep, :]
        parts.append(jnp.where(c < seq // rows - 1, halo, 0.0))
    return parts[0] if len(parts) == 1 else jnp.concatenate(parts, axis=0)


def _lag(x, k):
    return pltpu.roll(x, k, 0) if k else x


def _lead(x, k):
    return pltpu.roll(x, x.shape[0] - k, 0) if k else x


SHIFT_ROWS = 128
POOL_ROWS = 1024
SHIFT_COLS = 256


HALO = 16
KEEP = 8


def _conv3(ext, w, bias):
    acc = bias + w[2:3, :] * ext[KEEP:, :]
    acc = acc + w[1:2, :] * _lag(ext, 1)[KEEP:, :]
    return acc + w[0:1, :] * _lag(ext, 2)[KEEP:, :]


def _ffn_act_fwd(hpre, cw, cb, name):
    b, seq, f2 = hpre.shape
    cbk = SHIFT_COLS
    nj = f2 // (2 * cbk)
    rows = min(SHIFT_ROWS, seq)

    def body(g_ref, v_ref, wg_ref, wv_ref, bg_ref, bv_ref, o_ref, pg_ref, pv_ref):
        def chunk(c, carry):
            gate = _conv3(_window(g_ref, c, rows, seq, HALO, 0, KEEP), wg_ref[...], bg_ref[...])
            val = _conv3(_window(v_ref, c, rows, seq, HALO, 0, KEEP), wv_ref[...], bv_ref[...])
            a = gate * _sigmoid(gate) * val
            here = pl.ds(pl.multiple_of(c * rows, rows), rows)
            o_ref[here, :] = a.astype(BF16)
            pg_ref[here, :] = gate.astype(BF16)
            pv_ref[here, :] = val.astype(BF16)
            return carry

        lax.fori_loop(0, seq // rows, chunk, 0)

    blk = lambda off: pl.BlockSpec((None, seq, cbk), lambda i, j: (i, 0, j + off))
    wsp = lambda r, off: pl.BlockSpec((r, cbk), lambda i, j: (0, j + off))
    half = jax.ShapeDtypeStruct((b, seq, f2 // 2), BF16)
    return pl.pallas_call(
        body, out_shape=(half, half, half), grid=(b, nj),
        in_specs=[blk(0), blk(nj), wsp(FFN_CONV, 0), wsp(FFN_CONV, nj), wsp(1, 0), wsp(1, nj)],
        out_specs=(blk(0), blk(0), blk(0)),
        compiler_params=_params(("parallel", "parallel")), name=name)(hpre, hpre, cw, cw, cb, cb)


def _ffn_act_bwd(hpre, pre_g, pre_v, da, cw, name):
    b, seq, f2 = hpre.shape
    cbk = SHIFT_COLS
    nj = f2 // (2 * cbk)
    rows = min(SHIFT_ROWS, seq)

    def body(g_ref, v_ref, pg_ref, pv_ref, da_ref, wg_ref, wv_ref, og_ref, ov_ref, dwg_ref, dwv_ref, dbg_ref, dbv_ref):
        wg, wv = wg_ref[...], wv_ref[...]

        def back(dpre, w, o_ref, x_ref, c, carry):
            here = pl.ds(pl.multiple_of(c * rows, rows), rows)
            leads = [dpre, _lead(dpre, 1), _lead(dpre, 2)]
            dx = w[2:3, :] * leads[0] + w[1:2, :] * leads[1] + w[0:1, :] * leads[2]
            o_ref[here, :] = dx[:rows, :].astype(BF16)
            x0 = x_ref[here, :].astype(F32)
            return tuple(carry[k] + jnp.sum(leads[k][:rows, :] * x0, axis=0, keepdims=True) for k in range(FFN_CONV)) + (
                carry[FFN_CONV] + jnp.sum(dpre[:rows, :], axis=0, keepdims=True),)

        def chunk(c, carry):
            cg, cv = carry
            gate = _window(pg_ref, c, rows, seq, 0, HALO, KEEP)
            val = _window(pv_ref, c, rows, seq, 0, HALO, KEEP)
            dav = _window(da_ref, c, rows, seq, 0, HALO, KEEP)
            sg = _sigmoid(gate)
            cg = back(dav * val * (sg * (1.0 + gate * (1.0 - sg))), wg, og_ref, g_ref, c, cg)
            cv = back(dav * (gate * sg), wv, ov_ref, v_ref, c, cv)
            return cg, cv

        z = jnp.zeros((1, cbk), F32)
        cg, cv = lax.fori_loop(0, seq // rows, chunk, ((z,) * (FFN_CONV + 1), (z,) * (FFN_CONV + 1)))
        dwg = jnp.concatenate([cg[2], cg[1], cg[0]], axis=0)
        dwv = jnp.concatenate([cv[2], cv[1], cv[0]], axis=0)

        @pl.when(pl.program_id(1) == 0)
        def _():
            dwg_ref[...] = dwg
            dwv_ref[...] = dwv
            dbg_ref[...] = cg[FFN_CONV]
            dbv_ref[...] = cv[FFN_CONV]

        @pl.when(pl.program_id(1) > 0)
        def _():
            dwg_ref[...] += dwg
            dwv_ref[...] += dwv
            dbg_ref[...] += cg[FFN_CONV]
            dbv_ref[...] += cv[FFN_CONV]

    blk = lambda off: pl.BlockSpec((None, seq, cbk), lambda j, i: (i, 0, j + off))
    wsp = lambda r, off: pl.BlockSpec((r, cbk), lambda j, i: (0, j + off))
    half = jax.ShapeDtypeStruct((b, seq, f2 // 2), BF16)
    dwshape = jax.ShapeDtypeStruct((FFN_CONV, f2 // 2), F32)
    dbshape = jax.ShapeDtypeStruct((1, f2 // 2), F32)
    dg, dv, dwg, dwv, dbg, dbv = pl.pallas_call(
        body,
        out_shape=(half, half, dwshape, dwshape, dbshape, dbshape),
        grid=(nj, b),
        in_specs=[blk(0), blk(nj), blk(0), blk(0), blk(0), wsp(FFN_CONV, 0), wsp(FFN_CONV, nj)],
        out_specs=(blk(0), blk(0), wsp(FFN_CONV, 0), wsp(FFN_CONV, 0), wsp(1, 0), wsp(1, 0)),
        compiler_params=_params(("parallel", "arbitrary")), name=name)(hpre, hpre, pre_g, pre_v, da, cw, cw)
    return dg, dv, jnp.concatenate([dwg, dwv], axis=1), jnp.concatenate([dbg, dbv], axis=1)


def _ssd_conv_fwd(zx, cw, cb, d_inner, name):
    b, seq, _ = zx.shape
    xbc = cw.shape[1]
    cbk = SHIFT_COLS
    off = d_inner // cbk
    rows = min(SHIFT_ROWS, seq)

    def body(h_ref, w_ref, b_ref, o_ref, p_ref):
        w = w_ref[...]
        bias = b_ref[...]

        def chunk(c, carry):
            ext = _window(h_ref, c, rows, seq, HALO, 0, KEEP)
            acc = bias + w[3:4, :] * ext[KEEP:, :]
            for k in range(1, SSD_CONV):
                acc = acc + w[3 - k:4 - k, :] * _lag(ext, k)[KEEP:, :]
            here = pl.ds(pl.multiple_of(c * rows, rows), rows)
            o_ref[here, :] = acc * _sigmoid(acc)
            p_ref[here, :] = acc.astype(BF16)
            return carry

        lax.fori_loop(0, seq // rows, chunk, 0)

    blk = pl.BlockSpec((None, seq, cbk), lambda i, j: (i, 0, j))
    return pl.pallas_call(
        body, out_shape=(jax.ShapeDtypeStruct((b, seq, xbc), F32), jax.ShapeDtypeStruct((b, seq, xbc), BF16)),
        grid=(b, xbc // cbk),
        in_specs=[pl.BlockSpec((None, seq, cbk), lambda i, j: (i, 0, j + off)),
                  pl.BlockSpec((SSD_CONV, cbk), lambda i, j: (0, j)),
                  pl.BlockSpec((1, cbk), lambda i, j: (0, j))],
        out_specs=(blk, blk),
        compiler_params=_params(("parallel", "parallel")), name=name)(zx, cw, cb)


def _ssd_conv_bwd(zx, pre, dparts, ddt, dzx, cw, d_inner, name):
    b, seq, zw = zx.shape
    xbc = cw.shape[1]
    cbk = SHIFT_COLS
    off = d_inner // cbk
    rows = min(SHIFT_ROWS, seq)
    nblk = [p.shape[2] // cbk for p in dparts]
    first = [sum(nblk[:s]) for s in range(len(dparts))]
    nconv = xbc // cbk
    ncopy = ddt.shape[2] // cbk
    assert sum(nblk) == nconv and (off + nconv + ncopy) * cbk == zw and dzx.shape == (b, seq, zw)

    def body(h_ref, p_ref, gx_ref, gb_ref, gc_ref, t_ref, w_ref, z_ref, o_ref, dw_ref, db_ref):
        j = pl.program_id(0)

        @pl.when(j < nconv)
        def _():
            conv(h_ref, p_ref, gx_ref, gb_ref, gc_ref, w_ref, o_ref, dw_ref, db_ref)

        @pl.when(j >= nconv)
        def _():
            o_ref[...] = t_ref[...]

    def conv(h_ref, p_ref, gx_ref, gb_ref, gc_ref, w_ref, o_ref, dw_ref, db_ref):
        w = w_ref[...]
        j = pl.program_id(0)

        def chunk(c, carry):
            dws, dbias = carry
            here = pl.ds(pl.multiple_of(c * rows, rows), rows)
            pre = _window(p_ref, c, rows, seq, 0, HALO, KEEP)
            s = _sigmoid(pre)
            gsel = jnp.where(j < first[1], _window(gx_ref, c, rows, seq, 0, HALO, KEEP),
                             jnp.where(j < first[2], _window(gb_ref, c, rows, seq, 0, HALO, KEEP),
                                       _window(gc_ref, c, rows, seq, 0, HALO, KEEP)))
            dpre = gsel * (s * (1.0 + pre * (1.0 - s)))
            leads = [dpre] + [_lead(dpre, k) for k in range(1, SSD_CONV)]
            dx = w[3:4, :] * leads[0]
            for k in range(1, SSD_CONV):
                dx = dx + w[3 - k:4 - k, :] * leads[k]
            o_ref[here, :] = dx[:rows, :].astype(BF16)
            x0 = h_ref[here, :].astype(F32)
            dws = tuple(dws[k] + jnp.sum(leads[k][:rows, :] * x0, axis=0, keepdims=True) for k in range(SSD_CONV))
            dbias = dbias + jnp.sum(dpre[:rows, :], axis=0, keepdims=True)
            return dws, dbias

        z = jnp.zeros((1, cbk), F32)
        dws, dbias = lax.fori_loop(0, seq // rows, chunk, ((z,) * SSD_CONV, z))
        dwv = jnp.concatenate([dws[3 - i] for i in range(SSD_CONV)], axis=0)

        @pl.when(pl.program_id(1) == 0)
        def _():
            dw_ref[...] = dwv
            db_ref[...] = dbias

        @pl.when(pl.program_id(1) > 0)
        def _():
            dw_ref[...] += dwv
            db_ref[...] += dbias

    conv_j = lambda j: jnp.minimum(j, nconv - 1)
    return pl.pallas_call(
        body,
        out_shape=(jax.ShapeDtypeStruct((b, seq, zw), BF16), jax.ShapeDtypeStruct((SSD_CONV, xbc), F32),
                   jax.ShapeDtypeStruct((1, xbc), F32)),
        grid=(nconv + ncopy, b),
        in_specs=[pl.BlockSpec((None, seq, cbk), lambda j, i: (i, 0, conv_j(j) + off)),
                  pl.BlockSpec((None, seq, cbk), lambda j, i: (i, 0, conv_j(j)))] + [
                  pl.BlockSpec((None, seq, cbk), lambda j, i, s=s: (i, 0, jnp.clip(j - first[s], 0, nblk[s] - 1)))
                  for s in range(3)] + [
                  pl.BlockSpec((None, seq, cbk), lambda j, i: (i, 0, jnp.clip(j - nconv, 0, ncopy - 1))),
                  pl.BlockSpec((SSD_CONV, cbk), lambda j, i: (0, conv_j(j))),
                  ANY],
        out_specs=(pl.BlockSpec((None, seq, cbk), lambda j, i: (i, 0, j + off)),
                   pl.BlockSpec((SSD_CONV, cbk), lambda j, i: (0, conv_j(j))),
                   pl.BlockSpec((1, cbk), lambda j, i: (0, conv_j(j)))),
        input_output_aliases={7: 0},
        compiler_params=_params(("arbitrary", "arbitrary")), name=name)(zx, pre, *dparts, ddt, cw, dzx)


def _pool_sums(q, g, lead):
    sh = _lead if lead else _lag
    s2 = q + sh(q, 1)
    s4 = s2 + sh(s2, 2)
    s8 = s4 + sh(s4, 4)
    s16 = s8 + sh(s8, 8)
    return jnp.where(g == 0, s2, jnp.where(g == 1, s4, jnp.where(g == 2, s8, s16)))


def _pool_count(r0, n, g, shape):
    t = (r0 + lax.broadcasted_iota(jnp.int32, shape, 0) + 1).astype(F32)
    return jnp.minimum(t, (2 << g).astype(F32))


def _pool_fwd(h, pw, scale, name):
    b, seq, d = h.shape
    dg = d // 4
    rows = min(POOL_ROWS, seq)

    def body(h_ref, w_ref, s_ref, o_ref):
        g = pl.program_id(1)
        wmat = w_ref[...]
        sc = s_ref[...]

        def chunk(c, carry):
            r0 = c * rows
            ext = _window(h_ref, c, rows, seq, 16, 0)
            sums = _pool_sums(ext, g, False)[16:, :]
            mixed = sums / _pool_count(r0, rows, g, (rows, dg)) - ext[16:, :]
            o_ref[pl.ds(pl.multiple_of(r0, rows), rows), :] = _nn(mixed.astype(BF16), wmat) * sc
            return carry

        lax.fori_loop(0, seq // rows, chunk, 0)

    return pl.pallas_call(
        body, out_shape=jax.ShapeDtypeStruct((b, seq, d), F32), grid=(b, 4),
        in_specs=[pl.BlockSpec((None, seq, dg), lambda i, g: (i, 0, g)),
                  pl.BlockSpec((None, dg, dg), lambda i, g: (g, 0, 0)),
                  pl.BlockSpec((1, dg), lambda i, g: (0, g))],
        out_specs=pl.BlockSpec((None, seq, dg), lambda i, g: (i, 0, g)),
        compiler_params=_params(("parallel", "parallel")), name=name)(h, pw, scale)


def _pool_bwd(h, dout, pw, scale, name):
    b, seq, d = h.shape
    dg = d // 4
    rows = min(POOL_ROWS, seq)

    def body(h_ref, g_ref, w_ref, s_ref, o_ref, dw_ref, ds_ref, dw_acc):
        g = pl.program_id(0)
        wmat = w_ref[...]
        sc = s_ref[...]
        dw_acc[...] = jnp.zeros_like(dw_acc)

        def chunk(c, dsc):
            r0 = c * rows
            ext = _window(h_ref, c, rows, seq, 16, 0)
            sums = _pool_sums(ext, g, False)[16:, :]
            mixed = (sums / _pool_count(r0, rows, g, (rows, dg)) - ext[16:, :]).astype(BF16)
            gext = _window(g_ref, c, rows, seq, 0, 16)
            dsc = dsc + jnp.sum(gext[:rows, :] * _nn(mixed, wmat), axis=0, keepdims=True)
            dpre = (gext * sc).astype(BF16)
            dw_acc[...] += _tn(mixed, dpre[:rows, :])
            dmix = _nt(dpre, wmat)
            q = dmix / _pool_count(r0, rows + 16, g, (rows + 16, dg))
            back = _pool_sums(q, g, True)
            o_ref[pl.ds(pl.multiple_of(r0, rows), rows), :] = back[:rows, :] - dmix[:rows, :]
            return dsc

        dsc = lax.fori_loop(0, seq // rows, chunk, jnp.zeros((1, dg), F32))

        @pl.when(pl.program_id(1) == 0)
        def _():
            dw_ref[...] = dw_acc[...]
            ds_ref[...] = dsc

        @pl.when(pl.program_id(1) > 0)
        def _():
            dw_ref[...] += dw_acc[...]
            ds_ref[...] += dsc

    return pl.pallas_call(
        body,
        out_shape=(jax.ShapeDtypeStruct((b, seq, d), F32), jax.ShapeDtypeStruct((4, dg, dg), F32),
                   jax.ShapeDtypeStruct((1, d), F32)),
        grid=(4, b),
        in_specs=[pl.BlockSpec((None, seq, dg), lambda g, i: (i, 0, g)),
                  pl.BlockSpec((None, seq, dg), lambda g, i: (i, 0, g)),
                  pl.BlockSpec((None, dg, dg), lambda g, i: (g, 0, 0)),
                  pl.BlockSpec((1, dg), lambda g, i: (0, g))],
        out_specs=(pl.BlockSpec((None, seq, dg), lambda g, i: (i, 0, g)),
                   pl.BlockSpec((None, dg, dg), lambda g, i: (g, 0, 0)),
                   pl.BlockSpec((1, dg), lambda g, i: (0, g))),
        scratch_shapes=[pltpu.VMEM((dg, dg), F32)],
        compiler_params=_params(("parallel", "arbitrary")), name=name)(h, dout, pw, scale)


def _head_of(channel):
    return jnp.right_shift(channel, HEAD_DIM.bit_length() - 1)


def _ssd_consts(gw):
    q = CHUNK
    row = lax.broadcasted_iota(jnp.int32, (q, q), 0)
    col = lax.broadcasted_iota(jnp.int32, (q, q), 1)
    tril = (row >= col).astype(BF16)
    triu = (row <= col).astype(BF16)
    e = (_head_of(lax.broadcasted_iota(jnp.int32, (LANES, gw), 1))
         == lax.broadcasted_iota(jnp.int32, (LANES, gw), 0)).astype(BF16)
    et = (_head_of(lax.broadcasted_iota(jnp.int32, (gw, LANES), 0))
          == lax.broadcasted_iota(jnp.int32, (gw, LANES), 1)).astype(BF16)
    return row, col, tril, triu, e, et


def _ssd_common(dtr, dtb, alog, gw):
    q = CHUNK
    row, col, tril, triu, e, et = _ssd_consts(gw)
    dt = _softplus(dtr + dtb)
    a_row = -jnp.exp(alog)
    acum = _sel_left(tril, dt * a_row)
    ac_last = jnp.sum(jnp.where(row == q - 1, acum, 0.0), axis=0, keepdims=True)
    eac = jnp.exp(acum)
    de = jnp.exp(ac_last - acum)
    e2 = jnp.concatenate([e, e], axis=0)
    expand = _sel_right(jnp.concatenate([dt, eac, de], axis=0), e2, 2)
    dt_x, eac_x, de_x = expand[0:q], expand[q:2 * q], expand[2 * q:3 * q]
    acum_t = acum.T
    cd_col = jnp.exp(acum_t[:, q - 1:q])
    et3 = jnp.concatenate([et, et, et], axis=1)
    cdmat = _nn(et3, jnp.concatenate(_split(jnp.broadcast_to(cd_col, (LANES, D_STATE)), 3), axis=0))
    consts = dict(row=row, col=col, tril=tril, triu=triu, e=e, et=et)
    return dt, a_row, acum, acum_t, ac_last, eac, de, dt_x, eac_x, de_x, cdmat, consts


def _decay(acum, acum_t, j, row, col):
    diff = acum[:, j:j + 1] - acum_t[j:j + 1, :]
    return jnp.exp(jnp.where(row >= col, diff, -1e30))


def _ssd_fwd(xc, zx, dtr, dtb, alog, dskip, nw, d_inner, name):
    b, seq, xbc = xc.shape
    q = CHUNK
    nc = seq // q
    gw = d_inner // N_GROUPS
    nh = gw // HEAD_DIM
    xb0 = d_inner // D_STATE
    xc0 = xb0 + N_GROUPS

    nb = max(n for n in (4, 2, 1) if b % n == 0)

    def body(x_ref, b_ref, c_ref, z_ref, dtr_ref, dtb_ref, al_ref, dsk_ref, nw_ref, y_ref, yn_ref, st_ref, s_ref):
        @pl.when(pl.program_id(2) == 0)
        def _():
            s_ref[...] = jnp.zeros_like(s_ref)

        for s in range(nb):
            one(s, x_ref.at[s], b_ref.at[s], c_ref.at[s], z_ref.at[s], dtr_ref.at[s], dtb_ref, al_ref, dsk_ref, nw_ref,
                y_ref.at[s], yn_ref.at[s], st_ref.at[s], s_ref.at[s])

    def one(s, x_ref, b_ref, c_ref, z_ref, dtr_ref, dtb_ref, al_ref, dsk_ref, nw_ref, y_ref, yn_ref, st_ref, s_ref):
        prev = s_ref[...]
        st_ref[...] = prev
        x = x_ref[...]
        bm = b_ref[...].astype(BF16)
        cm = c_ref[...].astype(BF16)
        (dt, a_row, acum, acum_t, ac_last, eac, de, dt_x, eac_x, de_x, cdmat, k) = _ssd_common(
            dtr_ref[...], dtb_ref[0:1, :], al_ref[0:1, :], gw)
        xdt = x * dt_x
        xdt_b = xdt.astype(BF16)
        cb = _nt(cm, bm)
        half = _head_of(lax.broadcasted_iota(jnp.int32, (q, LANES), 1))
        pairs = []
        for p in range(nh // 2):
            xp = xdt_b[:, p * LANES:(p + 1) * LANES]
            ms = [(cb * _decay(acum, acum_t, 2 * p + e, k["row"], k["col"])).astype(BF16) for e in range(2)]
            xs = [jnp.where(half == e, xp, jnp.zeros_like(xp)) for e in range(2)]
            pairs.append(_nn(jnp.concatenate(ms, axis=1), jnp.concatenate(xs, axis=0)))
        prev_b = prev.astype(BF16)
        y = dsk_ref[0:1, :] * x + jnp.concatenate(pairs, axis=1) + eac_x * _nt(cm, prev_b)
        s_ref[...] = cdmat * prev + _tn((xdt * de_x).astype(BF16), bm)
        y_ref[...] = y
        z = z_ref[...].astype(F32)
        yg = y * (z * _sigmoid(z))
        r = lax.rsqrt(jnp.mean(yg * yg, axis=-1, keepdims=True) + EPS)
        yn_ref[...] = ((yg * r) * nw_ref[0:1, :]).astype(BF16)

    par = lambda w: pl.BlockSpec((None, 8, w), lambda i, g, c: (g, 0, 0))
    return pl.pallas_call(
        body,
        out_shape=(jax.ShapeDtypeStruct((b, seq, d_inner), F32), jax.ShapeDtypeStruct((b, seq, d_inner), BF16),
                   jax.ShapeDtypeStruct((b, nc, N_GROUPS, gw, D_STATE), F32)),
        grid=(b // nb, N_GROUPS, nc),
        in_specs=[pl.BlockSpec((nb, q, gw), lambda i, g, c: (i, c, g)),
                  pl.BlockSpec((nb, q, D_STATE), lambda i, g, c: (i, c, xb0 + g)),
                  pl.BlockSpec((nb, q, D_STATE), lambda i, g, c: (i, c, xc0 + g)),
                  pl.BlockSpec((nb, q, gw), lambda i, g, c: (i, c, g)),
                  pl.BlockSpec((nb, q, LANES), lambda i, g, c: (i, c, g)),
                  par(LANES), par(LANES), par(gw), par(gw)],
        out_specs=(pl.BlockSpec((nb, q, gw), lambda i, g, c: (i, c, g)),
                   pl.BlockSpec((nb, q, gw), lambda i, g, c: (i, c, g)),
                   pl.BlockSpec((nb, None, None, gw, D_STATE), lambda i, g, c: (i, c, g, 0, 0))),
        scratch_shapes=[pltpu.VMEM((nb, gw, D_STATE), F32)],
        compiler_params=_params(("parallel", "parallel", "arbitrary")), name=name,
    )(xc, xc, xc, zx, dtr, dtb, alog, dskip, nw)


def _ssd_bwd(xc, zx, dtr, y, dyn, st, dtb, alog, dskip, nw, d_inner, name):
    b, seq, xbc = xc.shape
    q = CHUNK
    nc = seq // q
    gw = d_inner // N_GROUPS
    nh = gw // HEAD_DIM
    xb0 = d_inner // D_STATE
    xc0 = xb0 + N_GROUPS

    nb = max(n for n in (4, 2, 1) if b % n == 0)

    def body(x_ref, b_ref, c_ref, z_ref, dtr_ref, y_ref, g_ref, st_ref, dtb_ref, al_ref, dsk_ref, nw_ref,
             dz_ref, dx_ref, db_ref, dc_ref, ddt_ref, dnw_ref, dd_ref, dal_ref, dbias_ref,
             ds_ref, colbuf, rowbuf):
        first = jnp.logical_and(pl.program_id(1) == 0, pl.program_id(2) == 0)

        @pl.when(pl.program_id(2) == 0)
        def _():
            ds_ref[...] = jnp.zeros_like(ds_ref)

        sums = [one(x_ref.at[s], b_ref.at[s], c_ref.at[s], z_ref.at[s], dtr_ref.at[s], y_ref.at[s], g_ref.at[s],
                    st_ref.at[s], dtb_ref, al_ref, dsk_ref, nw_ref, dz_ref.at[s], dx_ref.at[s], db_ref.at[s],
                    dc_ref.at[s], ddt_ref.at[s], ds_ref.at[s], colbuf.at[s], rowbuf.at[s]) for s in range(nb)]
        dnw, dd, dal, dbias = [functools.reduce(lambda p, r: p + r, [sm[i] for sm in sums]) for i in range(4)]

        @pl.when(first)
        def _():
            dnw_ref[...] = jnp.broadcast_to(dnw, (8, gw))
            dd_ref[...] = dd
            dal_ref[...] = jnp.broadcast_to(dal, (8, LANES))
            dbias_ref[...] = jnp.broadcast_to(dbias, (8, LANES))

        @pl.when(jnp.logical_not(first))
        def _():
            dnw_ref[...] += jnp.broadcast_to(dnw, (8, gw))
            dd_ref[...] += dd
            dal_ref[...] += jnp.broadcast_to(dal, (8, LANES))
            dbias_ref[...] += jnp.broadcast_to(dbias, (8, LANES))

    def one(x_ref, b_ref, c_ref, z_ref, dtr_ref, y_ref, g_ref, st_ref, dtb_ref, al_ref, dsk_ref, nw_ref,
            dz_ref, dx_ref, db_ref, dc_ref, ddt_ref, ds_ref, colbuf, rowbuf):
        x = x_ref[...]
        bm = b_ref[...].astype(BF16)
        cm = c_ref[...].astype(BF16)
        z = z_ref[...].astype(F32)
        y = y_ref[...]
        prev = st_ref[...]
        dtr = dtr_ref[...] + dtb_ref[0:1, :]
        (dt, a_row, acum, acum_t, ac_last, eac, de, dt_x, eac_x, de_x, cdmat, k) = _ssd_common(
            dtr_ref[...], dtb_ref[0:1, :], al_ref[0:1, :], gw)
        row, col = k["row"], k["col"]
        et2 = jnp.concatenate([k["et"], k["et"]], axis=0)

        sz = _sigmoid(z)
        silu_z = z * sz
        yg = y * silu_z
        r = lax.rsqrt(jnp.mean(yg * yg, axis=-1, keepdims=True) + EPS)
        xh = yg * r
        dyn = g_ref[...].astype(F32)
        gh = dyn * nw_ref[0:1, :]
        dyg = r * (gh - xh * jnp.mean(gh * xh, axis=-1, keepdims=True))
        dnw = jnp.sum(dyn * xh, axis=0, keepdims=True)
        g = dyg * silu_z
        dz_ref[...] = (dyg * y * (sz * (1.0 + z * (1.0 - sz)))).astype(BF16)
        dd = _sel_right(jnp.broadcast_to(jnp.sum(g * x, axis=0, keepdims=True), (8, gw)), et2, 2)

        xdt = x * dt_x
        xdt_b = xdt.astype(BF16)
        g_b = g.astype(BF16)
        prev_b = prev.astype(BF16)
        cb = _nt(cm, bm)

        cp = _nt(cm, prev_b)
        ge = g * eac_x
        dac = _sel_right(ge * cp, et2, 2)
        ge_b = ge.astype(BF16)
        dcm = _nn(ge_b, prev_b)
        dprev = _tn(ge_b, cm)

        colbuf[...] = jnp.zeros_like(colbuf)
        rowbuf[...] = jnp.zeros_like(rowbuf)
        dcb = jnp.zeros((q, q), F32)
        half = _head_of(lax.broadcasted_iota(jnp.int32, (q, LANES), 1))
        pairs, held = [], []
        for j in range(nh):
            pc = (j // 2) * LANES
            dec = _decay(acum, acum_t, j, row, col)
            m = cb * dec
            gj = jnp.where(half == j % 2, g[:, pc:pc + LANES], 0.0).astype(BF16)
            dm = _nt(gj, xdt_b[:, pc:pc + LANES])
            w = dm * m
            colbuf[:, j:j + 1] = jnp.sum(w, axis=1, keepdims=True)
            rowbuf[j:j + 1, :] = jnp.sum(w, axis=0, keepdims=True)
            dcb = dcb + dm * dec
            held.append((m.astype(BF16), gj))
            if j % 2 == 1:
                pairs.append(_tn(jnp.concatenate([held[0][0], held[1][0]], axis=0),
                                 jnp.concatenate([held[0][1], held[1][1]], axis=0)))
                held = []
        dxdt = jnp.concatenate(pairs, axis=1)
        dcb_b = dcb.astype(BF16)
        dcm = dcm + _nn(dcb_b, bm)
        dbm = _tn(dcb_b, cm)

        ds = ds_ref[...]
        ds_b = ds.astype(BF16)
        u = _nt(bm, ds_b)
        dxdt = dxdt + u * de_x
        dde = _sel_right(u * xdt, et2, 2)
        dbm = dbm + _nn((xdt * de_x).astype(BF16), ds_b)
        pm = jnp.concatenate(_split(ds * prev, 2), axis=1)
        t2 = _tn(pm, k["et"])
        dcd_row = jnp.sum(t2[0:D_STATE] + t2[D_STATE:2 * D_STATE], axis=0, keepdims=True)
        last = dcd_row * jnp.exp(ac_last) + jnp.sum(dde * de, axis=0, keepdims=True)
        dac = dac + colbuf[...] - rowbuf[...].T - dde * de + jnp.where(row == q - 1, last, 0.0)
        ds_ref[...] = cdmat * ds + dprev

        dadt = _sel_left(k["triu"], dac)
        ddt = _sel_right(dxdt * x, et2, 2) + dadt * a_row
        dal = jnp.sum(dadt * dt, axis=0, keepdims=True) * a_row
        lane = lax.broadcasted_iota(jnp.int32, (q, LANES), 1)
        ddtr = jnp.where(lane < nh, ddt * _sigmoid(dtr), 0.0)
        ddt_ref[...] = ddtr.astype(BF16)
        dbias = jnp.sum(ddtr, axis=0, keepdims=True)
        dx_ref[...] = dxdt * dt_x + dsk_ref[0:1, :] * g
        db_ref[...] = dbm
        dc_ref[...] = dcm
        return dnw, dd, dal, dbias

    rc = lambda c: nc - 1 - c
    par = lambda w: pl.BlockSpec((None, 8, w), lambda g, i, c: (g, 0, 0))
    blk = lambda w: pl.BlockSpec((nb, q, w), lambda g, i, c: (i, rc(c), g))
    return pl.pallas_call(
        body,
        out_shape=(jax.ShapeDtypeStruct((b, seq, zx.shape[2]), BF16),
                   jax.ShapeDtypeStruct((b, seq, d_inner), F32),
                   jax.ShapeDtypeStruct((b, seq, N_GROUPS * D_STATE), F32),
                   jax.ShapeDtypeStruct((b, seq, N_GROUPS * D_STATE), F32),
                   jax.ShapeDtypeStruct((b, seq, N_GROUPS * LANES), BF16),
                   jax.ShapeDtypeStruct((N_GROUPS, 8, gw), F32),
                   jax.ShapeDtypeStruct((N_GROUPS, 8, LANES), F32),
                   jax.ShapeDtypeStruct((N_GROUPS, 8, LANES), F32),
                   jax.ShapeDtypeStruct((N_GROUPS, 8, LANES), F32)),
        grid=(N_GROUPS, b // nb, nc),
        in_specs=[blk(gw),
                  pl.BlockSpec((nb, q, D_STATE), lambda g, i, c: (i, rc(c), xb0 + g)),
                  pl.BlockSpec((nb, q, D_STATE), lambda g, i, c: (i, rc(c), xc0 + g)),
                  blk(gw),
                  pl.BlockSpec((nb, q, LANES), lambda g, i, c: (i, rc(c), g)),
                  blk(gw), blk(gw),
                  pl.BlockSpec((nb, None, None, gw, D_STATE), lambda g, i, c: (i, rc(c), g, 0, 0)),
                  par(LANES), par(LANES), par(gw), par(gw)],
        out_specs=(blk(gw), blk(gw), blk(D_STATE), blk(D_STATE), blk(LANES),
                   par(gw), par(LANES), par(LANES), par(LANES)),
        scratch_shapes=[pltpu.VMEM((nb, gw, D_STATE), F32), pltpu.VMEM((nb, q, LANES), F32),
                        pltpu.VMEM((nb, LANES, q), F32)],
        compiler_params=_params(("parallel", "arbitrary", "arbitrary")), name=name,
    )(xc, xc, xc, zx, dtr, y, dyn, st, dtb, alog, dskip, nw)


def _adamw(w, g, m, v, name):
    rows, cols = w.shape
    tr = rows
    for cand in (512, 256, 128, 64, 32, 16, 8):
        if rows % cand == 0 and cand * cols * 4 <= 2 * 1024 * 1024:
            tr = cand
            break
    c1 = 1.0 - ADAM_B1 ** ADAM_STEP
    c2 = 1.0 - ADAM_B2 ** ADAM_STEP

    def body(w_ref, g_ref, m_ref, v_ref, d_ref, mo_ref, vo_ref, go_ref):
        gv = g_ref[...]
        go_ref[...] = gv
        mn = ADAM_B1 * m_ref[...] + (1.0 - ADAM_B1) * gv
        vn = ADAM_B2 * v_ref[...] + (1.0 - ADAM_B2) * (gv * gv)
        mo_ref[...] = mn
        vo_ref[...] = vn
        d_ref[...] = -ADAM_LR * ((mn / c1) / (jnp.sqrt(vn / c2) + ADAM_EPS) + ADAM_WD * w_ref[...])

    spec = pl.BlockSpec((tr, cols), lambda i: (i, 0))
    shp = jax.ShapeDtypeStruct((rows, cols), F32)
    return pl.pallas_call(body, out_shape=(shp, shp, shp, shp), grid=(rows // tr,), in_specs=[spec] * 4,
                          out_specs=(spec,) * 4, compiler_params=_params(("parallel",)), name=name)(w, g, m, v)


def _pick_rows(rows, row_bytes, limit=1 << 20):
    for cand in (2048, 1024, 512, 256, 128, 64, 32, 16):
        if rows % cand == 0 and cand * row_bytes <= limit:
            return cand
    return rows


def _as3d(a, lead):
    return a.reshape(a.shape[:lead] + (-1, a.shape[-1]))


def _pair_sum(g, got, core, name):
    h = got.shape[0]
    g3, got3 = _as3d(g, 1), _as3d(got, 1)
    _, rows, cols = got3.shape
    tr = _pick_rows(rows, cols * 4)

    def body(c_ref, g_ref, r_ref, o_ref):
        o_ref[...] = (g_ref[...] + r_ref[...]).astype(BF16)

    out = pl.pallas_call(
        body, out_shape=jax.ShapeDtypeStruct(got3.shape, BF16),
        grid_spec=pltpu.PrefetchScalarGridSpec(
            num_scalar_prefetch=1, grid=(h, rows // tr),
            in_specs=[pl.BlockSpec((None, tr, cols), lambda l, i, c_ref: (c_ref[0] * h + l, i, 0)),
                      pl.BlockSpec((None, tr, cols), lambda l, i, c_ref: (l, i, 0))],
            out_specs=pl.BlockSpec((None, tr, cols), lambda l, i, c_ref: (l, i, 0))),
        compiler_params=_params(("parallel", "parallel")), name=name)(core, g3, got3)
    return out.reshape(got.shape)


def _sum4(q, core, name):
    q4 = _as3d(q, 2)
    _, h, rows, cols = q4.shape
    tr = _pick_rows(rows, cols * 4)

    def body(c_ref, q0, q1, q2, q3, o_ref):
        o_ref[...] = ((q0[...].astype(F32) + q1[...].astype(F32)) + q2[...].astype(F32)) + q3[...].astype(F32)

    out = pl.pallas_call(
        body, out_shape=jax.ShapeDtypeStruct((2 * h, rows, cols), F32),
        grid_spec=pltpu.PrefetchScalarGridSpec(
            num_scalar_prefetch=1, grid=(h, rows // tr),
            in_specs=[pl.BlockSpec((None, None, tr, cols), lambda l, i, c_ref, k=k: (k, l, i, 0))
                      for k in range(N_CHIPS)],
            out_specs=pl.BlockSpec((None, tr, cols), lambda l, i, c_ref: (c_ref[0] * h + l, i, 0))),
        compiler_params=_params(("parallel", "parallel")), name=name)(core, q4, q4, q4, q4)
    return out.reshape((2 * h,) + q.shape[2:])


def _coords():
    return lax.axis_index("x"), lax.axis_index("y"), lax.axis_index("c")


def _other_chips(x, y):
    return [(1 - x, y), (x, 1 - y), (1 - x, 1 - y)]


def _allgather_halves(src, name):
    rows, cols = src.shape

    def body(x_ref, o_ref, send, recv, local):
        x, y, c = _coords()
        sib = (x, y, 1 - c)
        chips = _other_chips(x, y)

        def slot(h, cx, cy):
            return o_ref.at[h, 2 * cx + cy]

        def copy(kk, dst, to, src_ref):
            return pltpu.make_async_remote_copy(src_ref=src_ref, dst_ref=dst, send_sem=send.at[kk],
                                                recv_sem=recv.at[kk], device_id=to, device_id_type=MESH)

        mine = pltpu.make_async_copy(x_ref, slot(c, x, y), local)
        mine.start()
        first = [copy(0, slot(c, x, y), sib, x_ref)]
        first += [copy(1 + j, slot(c, x, y), (*chip, c), x_ref) for j, chip in enumerate(chips)]
        for cp in first:
            cp.start()
        passed = [copy(4 + j, slot(c, *chip), sib, slot(c, *chip)) for j, chip in enumerate(chips)]
        for j, chip in enumerate(chips):
            copy(1 + j, slot(c, *chip), (x, y, c), x_ref).wait_recv()
            passed[j].start()
        copy(0, slot(1 - c, x, y), (x, y, c), x_ref).wait_recv()
        for j, chip in enumerate(chips):
            copy(4 + j, slot(1 - c, *chip), (x, y, c), x_ref).wait_recv()
        for cp in first + passed:
            cp.wait_send()
        mine.wait()

    return pl.pallas_call(
        body, out_shape=jax.ShapeDtypeStruct((2, N_CHIPS, rows, cols), src.dtype),
        in_specs=[ANY], out_specs=ANY,
        scratch_shapes=[pltpu.SemaphoreType.DMA((7,)), pltpu.SemaphoreType.DMA((7,)), pltpu.SemaphoreType.DMA],
        name=name)(src)


MIXW = (("ssd_w_in", None), ("ssd_w_out", 0), ("pool_w", 1))
FFNW = (("ffn_w_up", 1), ("ffn_w_down", 0))


def _chip_window(axis, ref, layers, k):
    if axis is None:
        return ref.at[layers, k]
    n = ref.shape[1 + axis] // N_CHIPS
    sl = pl.ds(pl.multiple_of(k * n, LANES if 1 + axis == len(ref.shape) - 1 else 8), n)
    idx = [layers] + [slice(None)] * (len(ref.shape) - 1)
    idx[1 + axis] = sl
    return ref.at[tuple(idx)]


def _full_shape(axis, shard_shape):
    if axis is None:
        return (shard_shape[0], N_CHIPS) + tuple(shard_shape[1:])
    full = list(shard_shape)
    full[1 + axis] *= N_CHIPS
    return tuple(full)


HBM_SPEC = pl.BlockSpec(memory_space=pltpu.HBM)
SEM_SPEC = pl.BlockSpec(memory_space=pltpu.SEMAPHORE)


def _dma_sems(count):
    return pltpu.SemaphoreType.DMA((max(count, 1),))


def _wait_for(copy, kind):
    if kind == "recv":
        copy.wait_recv()
    elif kind == "send":
        copy.wait_send()
    else:
        copy.wait()


def _comm_fused(stages, counts, srcs, lands, name, inplace=False):
    ns, nl, k = len(srcs), len(lands), len(stages)

    def body(*refs):
        src_refs = refs[:ns]
        land_refs = refs[ns + (nl if inplace else 0):ns + (nl if inplace else 0) + nl]
        sem_refs = refs[len(refs) - 3 * k:]
        for s, stage_fn in enumerate(stages):
            starts, waits = stage_fn(src_refs, land_refs, tuple(sem_refs[3 * s:3 * s + 3]))
            for cp in starts:
                cp.start()
            for cp, kind in waits:
                _wait_for(cp, kind)

    scratch = []
    for cnt in counts:
        scratch += [_dma_sems(c) for c in cnt]
    outs = pl.pallas_call(
        body, out_shape=tuple(jax.ShapeDtypeStruct(a.shape, a.dtype) for a in lands),
        in_specs=[ANY] * (ns + (nl if inplace else 0)), out_specs=(ANY,) * nl,
        input_output_aliases={ns + i: i for i in range(nl)} if inplace else {},
        scratch_shapes=scratch, name=name)(*srcs, *(lands if inplace else ()))
    return list(outs)


class _SplitComm:
    def __init__(self, stages, counts, srcs, lands, name):
        self.stages, self.counts, self.name = stages, counts, name
        self.ns = len(srcs)
        self.data = [pltpu.with_memory_space_constraint(a, pltpu.HBM) for a in list(srcs) + list(lands)]
        self.sems = None
        self.step = 0

    def advance(self, after=None):
        i, k, nd, ns = self.step, len(self.stages), len(self.data), self.ns
        first, last = i == 0, i == k
        stages = self.stages
        after = list(after) if isinstance(after, (list, tuple)) else [after]

        def body(*refs):
            data = refs[:nd]
            pos = nd
            if not first:
                old = tuple(refs[pos:pos + 3])
                pos += 3 + len(after)
            if not last:
                new = tuple(refs[pos:pos + 3])
            if not first:
                for cp, kind in stages[i - 1](data[:ns], data[ns:], old)[1]:
                    _wait_for(cp, kind)
            if not last:
                for cp in stages[i](data[:ns], data[ns:], new)[0]:
                    cp.start()
                refs[len(refs) - 1][...] = jnp.zeros((8, LANES), F32)

        args = list(self.data)
        in_specs = [HBM_SPEC] * nd
        if not first:
            args += list(self.sems) + after
            in_specs += [SEM_SPEC] * 3 + [ANY] * len(after)
        out_shape, out_specs = [], []
        if not last:
            out_shape += [_dma_sems(c) for c in self.counts[i]]
            out_specs += [SEM_SPEC] * 3
        out_shape += [pltpu.HBM(a.shape, a.dtype) for a in self.data]
        out_specs += [HBM_SPEC] * nd
        if not last:
            out_shape.append(jax.ShapeDtypeStruct((8, LANES), F32))
            out_specs.append(pl.BlockSpec(memory_space=pltpu.VMEM))
        off = 0 if last else 3
        outs = pl.pallas_call(
            body, out_shape=tuple(out_shape), in_specs=in_specs, out_specs=tuple(out_specs),
            input_output_aliases={d: off + d for d in range(nd)},
            compiler_params=pltpu.CompilerParams(has_side_effects=pltpu.SideEffectType.DATAFLOW_SIDE_EFFECTING),
            name=f"{self.name}_{i}")(*args)
        self.sems = None if last else outs[:3]
        self.data = list(outs[off:off + nd])
        self.step += 1
        return None if last else outs[len(outs) - 1]

    def lands(self):
        return self.data[self.ns:]


def _gather_stages(spec):
    n = len(spec)

    def parts(srcs, lands):
        x, y, c = _coords()
        out = []
        for w, (_, axis) in enumerate(spec):
            h = srcs[w].shape[0] // 2
            mine, theirs = pl.ds(c * h, h), pl.ds((1 - c) * h, h)
            out.append((srcs[w].at[mine], lambda layers, k, w=w, axis=axis: _chip_window(axis, lands[w], layers, k),
                        mine, theirs))
        return x, y, c, 2 * x + y, (x, y, 1 - c), _other_chips(x, y), out

    def remote(src, dst, send, recv, idx, to):
        return pltpu.make_async_remote_copy(src_ref=src, dst_ref=dst, send_sem=send.at[idx], recv_sem=recv.at[idx],
                                            device_id=to, device_id_type=MESH)

    def stage0(srcs, lands, sems):
        send, recv, local = sems
        x, y, c, me, sib, chips, ps = parts(srcs, lands)
        starts, waits = [], []
        for w, (src, dst, mine, theirs) in enumerate(ps):
            lc = pltpu.make_async_copy(src, dst(mine, me), local.at[w])
            first = [remote(src, dst(mine, me), send, recv, 4 * w, sib)]
            first += [remote(src, dst(mine, me), send, recv, 4 * w + 1 + j, (cx, cy, c)) for j, (cx, cy) in enumerate(chips)]
            starts += [lc] + first
            waits.append((remote(src, dst(theirs, me), send, recv, 4 * w, (x, y, c)), "recv"))
            waits += [(remote(src, dst(mine, 2 * cx + cy), send, recv, 4 * w + 1 + j, (x, y, c)), "recv")
                      for j, (cx, cy) in enumerate(chips)]
            waits += [(cp, "send") for cp in first] + [(lc, "local")]
        return starts, waits

    def stage1(srcs, lands, sems):
        send, recv, _ = sems
        x, y, c, me, sib, chips, ps = parts(srcs, lands)
        starts, waits = [], []
        for w, (src, dst, mine, theirs) in enumerate(ps):
            for j, (cx, cy) in enumerate(chips):
                blk = dst(mine, 2 * cx + cy)
                fwd = remote(blk, blk, send, recv, 3 * w + j, sib)
                starts.append(fwd)
                waits.append((remote(src, dst(theirs, 2 * cx + cy), send, recv, 3 * w + j, (x, y, c)), "recv"))
                waits.append((fwd, "send"))
        return starts, waits

    return [stage0, stage1], [(4 * n, 4 * n, n), (3 * n, 3 * n, 0)]


def _swap_stages(spec):
    n = len(spec)

    def stage(srcs, lands, sems):
        send, recv, _ = sems
        x, y, c = _coords()
        starts, waits = [], []
        for w in range(n):
            h = srcs[w].shape[0] // 2
            cp = pltpu.make_async_remote_copy(src_ref=srcs[w].at[pl.ds((1 - c) * h, h)], dst_ref=lands[w],
                                              send_sem=send.at[w], recv_sem=recv.at[w],
                                              device_id=(x, y, 1 - c), device_id_type=MESH)
            starts.append(cp)
            waits += [(cp, "recv"), (cp, "send")]
        return starts, waits

    return [stage], [(n, n, 0)]


def _scatter_stages(spec):
    n = len(spec)

    def stage(srcs, lands, sems):
        send, recv, local = sems
        x, y, c = _coords()
        me = 2 * x + y
        starts, waits = [], []
        for w, (_, axis) in enumerate(spec):
            layers = pl.ds(0, srcs[w].shape[0])
            own = _chip_window(axis, srcs[w], layers, me)
            lc = pltpu.make_async_copy(own, lands[w].at[me], local.at[w])
            starts.append(lc)
            for j, (cx, cy) in enumerate(_other_chips(x, y)):
                cp = pltpu.make_async_remote_copy(src_ref=_chip_window(axis, srcs[w], layers, 2 * cx + cy),
                                                  dst_ref=lands[w].at[me], send_sem=send.at[3 * w + j],
                                                  recv_sem=recv.at[3 * w + j], device_id=(cx, cy, c), device_id_type=MESH)
                starts.append(cp)
                waits.append((pltpu.make_async_remote_copy(
                    src_ref=own, dst_ref=lands[w].at[2 * cx + cy], send_sem=send.at[3 * w + j], recv_sem=recv.at[3 * w + j],
                    device_id=(x, y, c), device_id_type=MESH), "recv"))
                waits.append((cp, "send"))
            waits.append((lc, "local"))
        return starts, waits

    return [stage], [(3 * n, 3 * n, n)]


def _share_stages(spec):
    n = len(spec)

    def stage(srcs, lands, sems):
        send, recv, _ = sems
        x, y, c = _coords()
        starts, waits = [], []
        for w in range(n):
            h = lands[w].shape[0] // 2
            mine, theirs = lands[w].at[pl.ds(c * h, h)], lands[w].at[pl.ds((1 - c) * h, h)]
            cp = pltpu.make_async_remote_copy(src_ref=mine, dst_ref=mine, send_sem=send.at[w], recv_sem=recv.at[w],
                                              device_id=(x, y, 1 - c), device_id_type=MESH)
            starts.append(cp)
            waits.append((pltpu.make_async_remote_copy(src_ref=theirs, dst_ref=theirs, send_sem=send.at[w],
                                                       recv_sem=recv.at[w], device_id=(x, y, c), device_id_type=MESH),
                          "recv"))
            waits.append((cp, "send"))
        return starts, waits

    return [stage], [(n, n, 0)]


def _shard_of(p, axis):
    if axis is None:
        return (p.shape[0],) + tuple(p.shape[2:])
    s = list(p.shape)
    s[1 + axis] //= N_CHIPS
    return tuple(s)


def _gather8_stages():
    def stage(srcs, lands, sems):
        send, recv, local = sems
        x, y, c = _coords()
        me = 4 * x + 2 * y + c
        lc = pltpu.make_async_copy(srcs[0], lands[0].at[me], local.at[0])
        starts, waits = [lc], []
        for kk in range(1, 8):
            to = (1 - x if kk & 4 else x, 1 - y if kk & 2 else y, 1 - c if kk & 1 else c)
            cp = pltpu.make_async_remote_copy(src_ref=srcs[0], dst_ref=lands[0].at[me], send_sem=send.at[kk - 1],
                                              recv_sem=recv.at[kk - 1], device_id=to, device_id_type=MESH)
            starts.append(cp)
            waits.append((pltpu.make_async_remote_copy(
                src_ref=srcs[0], dst_ref=lands[0].at[4 * to[0] + 2 * to[1] + to[2]], send_sem=send.at[kk - 1],
                recv_sem=recv.at[kk - 1], device_id=(x, y, c), device_id_type=MESH), "recv"))
            waits.append((cp, "send"))
        waits.append((lc, "local"))
        return starts, waits

    return [stage], [(7, 7, 1)]


def _sum8(buf, name):
    _, rows, cols = buf.shape
    tr = _pick_rows(rows, cols * 4)

    def body(*refs):
        acc = refs[0][...]
        for r in refs[1:8]:
            acc = acc + r[...]
        refs[8][...] = acc

    return pl.pallas_call(
        body, out_shape=jax.ShapeDtypeStruct((rows, cols), F32), grid=(rows // tr,),
        in_specs=[pl.BlockSpec((None, tr, cols), lambda i, k=k: (k, i, 0)) for k in range(8)],
        out_specs=pl.BlockSpec((tr, cols), lambda i: (i, 0)),
        compiler_params=_params(("parallel",)), name=name)(*([buf] * 8))


def _reduce_begin(spec, gs, core, tag, riders=(), during=None):
    stages, counts = _swap_stages(spec)
    got_shapes = [jax.ShapeDtypeStruct((g.shape[0] // 2,) + g.shape[1:], g.dtype) for g in gs]
    if during is None:
        got = _comm_fused(stages, counts, list(gs) + list(riders), got_shapes, "swap_" + tag)
    else:
        swap = _SplitComm(stages, counts, list(gs) + list(riders), [lax.empty(s.shape, s.dtype) for s in got_shapes],
                          "swap_" + tag)
        swap.advance(after=during(swap.advance()))
        gs, got = swap.data[:len(gs)], swap.lands()
    pair = [_pair_sum(a, r, core, "pair_sum_" + n) for a, r, (n, _) in zip(gs, got, spec)]
    stages, counts = _scatter_stages(spec)
    lands = [lax.empty((N_CHIPS,) + _shard_of(p, axis), p.dtype) for p, (_, axis) in zip(pair, spec)]
    comm = _SplitComm(stages, counts, pair, lands, "scatter_" + tag)
    return comm, comm.advance()


def _reduce_finish(spec, comm, core, tag, after):
    comm.advance(after=after)
    halves = [_sum4(q, core, "sum4_" + n) for q, (n, _) in zip(comm.lands(), spec)]
    stages, counts = _share_stages(spec)
    return _comm_fused(stages, counts, [], halves, "share_" + tag, inplace=True)


SMALL = (("ssd_conv_w", 2), ("pool_scale", 1), ("ffn_conv_w", 2))
REPL = ("ssd_conv_b", "ssd_dt_bias", "ssd_a_log", "ssd_d", "ssd_norm_w", "ffn_conv_b",
        "norm_mix_pre", "norm_mix_post", "norm_ffn_pre", "norm_ffn_post")
WEIGHTS = ("ssd_w_in", "ssd_conv_w", "ssd_conv_b", "ssd_dt_bias", "ssd_a_log", "ssd_d", "ssd_norm_w", "ssd_w_out",
           "pool_w", "pool_scale", "ffn_w_up", "ffn_conv_w", "ffn_conv_b", "ffn_w_down", "norm_mix_pre",
           "norm_mix_post", "norm_ffn_pre", "norm_ffn_post")


def _flat_rows(n):
    unit = 2 * 16 * FLAT_COLS
    return 2 * 16 * ((n + unit - 1) // unit)


def _flatten_shards(arrs, dtype):
    flat = jnp.concatenate([a.astype(dtype).reshape(-1) for a in arrs])
    rows = _flat_rows(flat.shape[0])
    flat = jnp.pad(flat, (0, rows * FLAT_COLS - flat.shape[0]))
    return flat.reshape(2, rows // 2, FLAT_COLS)


def _unflatten_full(gathered, shard_shapes, axes):
    per_chip = jnp.swapaxes(gathered, 0, 1).reshape(N_CHIPS, -1)
    out, off = [], 0
    for shp, ax in zip(shard_shapes, axes):
        n = math.prod(shp)
        pieces = [per_chip[k, off:off + n].reshape(shp) for k in range(N_CHIPS)]
        out.append(jnp.concatenate(pieces, axis=ax))
        off += n
    return out


def kernel(x, ssd_w_in, ssd_conv_w, ssd_conv_b, ssd_dt_bias, ssd_a_log, ssd_d, ssd_norm_w, ssd_w_out, pool_w, pool_scale, ffn_w_up, ffn_conv_w, ffn_conv_b, ffn_w_down, norm_mix_pre, norm_mix_post, norm_ffn_pre, norm_ffn_post, loss_target, m_ssd_w_in, m_ssd_conv_w, m_ssd_conv_b, m_ssd_dt_bias, m_ssd_a_log, m_ssd_d, m_ssd_norm_w, m_ssd_w_out, m_pool_w, m_pool_scale, m_ffn_w_up, m_ffn_conv_w, m_ffn_conv_b, m_ffn_w_down, m_norm_mix_pre, m_norm_mix_post, m_norm_ffn_pre, m_norm_ffn_post, v_ssd_w_in, v_ssd_conv_w, v_ssd_conv_b, v_ssd_dt_bias, v_ssd_a_log, v_ssd_d, v_ssd_norm_w, v_ssd_w_out, v_pool_w, v_pool_scale, v_ffn_w_up, v_ffn_conv_w, v_ffn_conv_b, v_ffn_w_down, v_norm_mix_pre, v_norm_mix_post, v_norm_ffn_pre, v_norm_ffn_post):
    wts = dict(ssd_w_in=ssd_w_in, ssd_conv_w=ssd_conv_w, ssd_conv_b=ssd_conv_b, ssd_dt_bias=ssd_dt_bias,
               ssd_a_log=ssd_a_log, ssd_d=ssd_d, ssd_norm_w=ssd_norm_w, ssd_w_out=ssd_w_out, pool_w=pool_w,
               pool_scale=pool_scale, ffn_w_up=ffn_w_up, ffn_conv_w=ffn_conv_w, ffn_conv_b=ffn_conv_b,
               ffn_w_down=ffn_w_down, norm_mix_pre=norm_mix_pre, norm_mix_post=norm_mix_post,
               norm_ffn_pre=norm_ffn_pre, norm_ffn_post=norm_ffn_post)
    mom = dict(ssd_w_in=m_ssd_w_in, ssd_conv_w=m_ssd_conv_w, ssd_conv_b=m_ssd_conv_b, ssd_dt_bias=m_ssd_dt_bias,
               ssd_a_log=m_ssd_a_log, ssd_d=m_ssd_d, ssd_norm_w=m_ssd_norm_w, ssd_w_out=m_ssd_w_out, pool_w=m_pool_w,
               pool_scale=m_pool_scale, ffn_w_up=m_ffn_w_up, ffn_conv_w=m_ffn_conv_w, ffn_conv_b=m_ffn_conv_b,
               ffn_w_down=m_ffn_w_down, norm_mix_pre=m_norm_mix_pre, norm_mix_post=m_norm_mix_post,
               norm_ffn_pre=m_norm_ffn_pre, norm_ffn_post=m_norm_ffn_post)
    var = dict(ssd_w_in=v_ssd_w_in, ssd_conv_w=v_ssd_conv_w, ssd_conv_b=v_ssd_conv_b, ssd_dt_bias=v_ssd_dt_bias,
               ssd_a_log=v_ssd_a_log, ssd_d=v_ssd_d, ssd_norm_w=v_ssd_norm_w, ssd_w_out=v_ssd_w_out, pool_w=v_pool_w,
               pool_scale=v_pool_scale, ffn_w_up=v_ffn_w_up, ffn_conv_w=v_ffn_conv_w, ffn_conv_b=v_ffn_conv_b,
               ffn_w_down=v_ffn_w_down, norm_mix_pre=v_norm_mix_pre, norm_mix_post=v_norm_mix_post,
               norm_ffn_pre=v_norm_ffn_pre, norm_ffn_post=v_norm_ffn_post)

    bl, seq, d = x.shape
    t = bl * seq
    depth = norm_mix_pre.shape[0]
    n_ssd = ssd_w_out.shape[0]
    d_inner = ssd_w_out.shape[1] * N_CHIPS
    nheads = d_inner // HEAD_DIM
    hpg = nheads // N_GROUPS
    gw = d_inner // N_GROUPS
    xbc = ssd_conv_w.shape[2] * N_CHIPS
    f2 = ffn_w_up.shape[2] * N_CHIPS
    ff = f2 // 2
    dg = d // 4
    cy = lax.axis_index("c")
    chip = 2 * lax.axis_index("x") + lax.axis_index("y")

    small_shapes = [wts[n].shape for n, _ in SMALL]
    small_axes = [a for _, a in SMALL]
    small_flat = _flatten_shards([wts[n] for n, _ in SMALL], F32)
    small_half = lax.dynamic_index_in_dim(small_flat, cy, 0, keepdims=False)
    small_all = _allgather_halves(small_half, "gather_small")
    conv_w, p_scale, f_conv_w = _unflatten_full(small_all, small_shapes, small_axes)
    def full_shapes(spec, shards):
        return [jax.ShapeDtypeStruct(_full_shape(axis, s.shape), s.dtype) for s, (_, axis) in zip(shards, spec)]

    def row_halves(a):
        return a.reshape((2, a.shape[0] // 2) + a.shape[1:])

    def join_w_in(g):
        return jnp.concatenate([g[:, k] for k in range(N_CHIPS)], axis=-1).reshape(d, -1)

    def join_w_out(g):
        r2 = g.shape[1] // N_CHIPS
        return jnp.concatenate([g[hf, k * r2:(k + 1) * r2] for k in range(N_CHIPS) for hf in range(2)], axis=0)

    ssd_spec = (("ssd_w_in", None), ("ssd_w_out", 0))
    first_shards = [row_halves(wts[n][0].astype(BF16)) for n, _ in ssd_spec]
    stages, counts = _gather_stages(ssd_spec)
    g_in0, g_out0 = _comm_fused(stages, counts, first_shards, full_shapes(ssd_spec, first_shards), "gather_first")
    w_in, w_out = [join_w_in(g_in0)], [join_w_out(g_out0)]
    rest_spec = ssd_spec * (n_ssd - 1) + (("pool_w", 1),) + FFNW
    rest_shards = [row_halves(wts[n][jj].astype(BF16)) for jj in range(1, n_ssd) for n, _ in ssd_spec]
    rest_shards += [wts["pool_w"].astype(BF16)] + [wts[n].astype(BF16) for n, _ in FFNW]
    stages, counts = _gather_stages(rest_spec)
    ffn_gather = _SplitComm(stages, counts, rest_shards + [g_out0],
                            [lax.empty(s.shape, s.dtype) for s in full_shapes(rest_spec, rest_shards)], "gather_rest")
    gather_token = ffn_gather.advance()

    def pad_heads(a):
        lead = a.shape[:-1]
        a = a.reshape(lead + (N_GROUPS, hpg))
        a = jnp.pad(a, [(0, 0)] * len(lead) + [(0, 0), (0, LANES - hpg)])
        return a.reshape(lead + (N_GROUPS * LANES,))

    def unpad_heads(a):
        lead = a.shape[:-1]
        return a.reshape(lead + (N_GROUPS, LANES))[..., :hpg].reshape(lead + (nheads,))

    def group_rows(a, width):
        return jnp.broadcast_to(a.reshape(N_GROUPS, 1, width), (N_GROUPS, 8, width))

    def pad_w_in(w):
        return jnp.concatenate([w[..., :d_inner + xbc], pad_heads(w[..., d_inner + xbc:])], axis=-1)

    w_in_p = [pad_w_in(w_in[0])]
    zw = w_in_p[0].shape[-1]
    w_pool = None

    x2 = x.reshape(t, d)
    tgt2 = loss_target.reshape(t, d)
    w_up = w_down = None

    saved = []
    cur = x2
    tokens = []
    h = _norm_fwd(cur, norm_mix_pre[0:1], BF16, "norm_pre_b", after=[gather_token])
    for i in range(depth):
        j = i // 2
        sv = dict(x_in=cur)
        if i % 2 == 0:
            zx = _mm(h, w_in_p[j], "nn", BF16, "mm_ssd_in", 2048, 512, d).reshape(bl, seq, zw)
            dtr = _mm(h, w_in_p[j][:, d_inner + xbc:], "nn", F32, "mm_ssd_dt", 2048, 512, d).reshape(bl, seq, -1)
            xc, xpre = _ssd_conv_fwd(zx, conv_w[j], ssd_conv_b[j:j + 1], d_inner, "ssd_conv_fwd")
            dtb = group_rows(pad_heads(ssd_dt_bias[j]), LANES)
            alog = group_rows(pad_heads(ssd_a_log[j]), LANES)
            dskip = group_rows(jnp.repeat(ssd_d[j], HEAD_DIM), gw)
            nw = group_rows(ssd_norm_w[j], gw)
            y, yn, st = _ssd_fwd(xc, zx, dtr, dtb, alog, dskip, nw, d_inner, "ssd_fwd")
            if i == 0:
                tokens.append(ffn_gather.advance(after=yn))
            mix = _mm(yn.reshape(t, d_inner), w_out[j], "nn", F32, "mm_ssd_out", 2048, 512, d_inner)
            sv.update(h=h, zx=zx, dtr=dtr, xc=xc, xpre=xpre, y=y, yn=yn, st=st, dtb=dtb, alog=alog, dskip=dskip, nw=nw)
        else:
            mix = _pool_fwd(h.reshape(bl, seq, d), w_pool[j], p_scale[j:j + 1], "pool_fwd").reshape(t, d)
            sv.update(h=h)
        sv.update(mix=mix)
        mid, u = _norm_post_pre(mix, norm_mix_post[i:i + 1], cur, norm_ffn_pre[i:i + 1], BF16, "norm_post_pre_b",
                                after=tokens)
        tokens = []
        if i == 0:
            ffn_gather.advance(after=u)
            rest = ffn_gather.lands()
            for jj in range(1, n_ssd):
                w_in_p.append(pad_w_in(join_w_in(rest[2 * (jj - 1)])))
                w_out.append(join_w_out(rest[2 * (jj - 1) + 1]))
            w_pool, w_up, w_down = rest[2 * (n_ssd - 1):]
        hpre = _mm(u, w_up, "nn", BF16, "mm_up", 2048, 512, d, b_layer=i).reshape(bl, seq, f2)
        act, pre_g, pre_v = _ffn_act_fwd(hpre, f_conv_w[i], ffn_conv_b[i:i + 1], "ffn_act_fwd")
        act = act.reshape(t, ff)
        fo = _mm(act, w_down, "nn", F32, "mm_down", 2048, 512, ff, b_layer=i)
        if i + 1 == depth:
            cur = _norm_fwd(fo, norm_ffn_post[i:i + 1], F32, "norm_post", resid=mid)
        elif i % 2 == 0:
            cur, h = _norm_post_pre(fo, norm_ffn_post[i:i + 1], mid, norm_mix_pre[i + 1:i + 2], F32, "norm_post_pre_f")
        else:
            cur, h = _norm_post_pre(fo, norm_ffn_post[i:i + 1], mid, norm_mix_pre[i + 1:i + 2], BF16, "norm_post_pre_b")
        sv.update(mid=mid, u=u, hpre=hpre, pre_g=pre_g, pre_v=pre_v, act=act, fo=fo)
        saved.append(sv)

    dcur, loss_part = _loss_head(cur, tgt2, "loss_head")

    g = {n: [None] * wts[n].shape[0] for n in WEIGHTS}
    gbuf = dict(up=lax.empty((depth, d, f2), F32), down=lax.empty((depth, ff, d), F32),
                out=lax.empty((n_ssd, d_inner, d), F32), win=lax.empty((n_ssd, d, zw), F32))
    core = cy.reshape(1).astype(jnp.int32)

    def mixer_bwd(i, dmid, dmix, behind=()):
        j = i // 2
        sv = saved[i]
        done = []
        if i % 2 == 0:
            dyn = _mm(dmix, w_out[j], "nt", BF16, "mm_ssd_out_dx", 1024, 1024, d, after=behind)
            gbuf["out"], tok = _mm(sv["yn"].reshape(t, d_inner), dmix, "tn", F32, "mm_ssd_out_dw", 1024, 1024, 2048,
                                   out_buf=(gbuf["out"], j))
            done.append(tok)
            dz, dxs, dbm, dcm, ddt, dnw, dd, dal, dbias = _ssd_bwd(
                sv["xc"], sv["zx"], sv["dtr"], sv["y"], dyn.reshape(bl, seq, d_inner), sv["st"], sv["dtb"], sv["alog"],
                sv["dskip"], sv["nw"], d_inner, "ssd_bwd")
            g["ssd_norm_w"][j] = dnw[:, 0, :].reshape(d_inner)
            g["ssd_d"][j] = dd[:, 0, :hpg].reshape(nheads)
            g["ssd_a_log"][j] = dal[:, 0, :hpg].reshape(nheads)
            g["ssd_dt_bias"][j] = dbias[:, 0, :hpg].reshape(nheads)
            dzx, dcw, dcb = _ssd_conv_bwd(sv["zx"], sv["xpre"], (dxs, dbm, dcm), ddt, dz, conv_w[j], d_inner,
                                          "ssd_conv_bwd")
            g["ssd_conv_w"][j] = dcw
            g["ssd_conv_b"][j] = dcb[0]
            dzx = dzx.reshape(t, zw)
            dh = _mm(dzx, w_in_p[j], "nt", BF16, "mm_ssd_in_dx", 1024, d, zw // 2)
            gbuf["win"], tok = _mm(sv["h"], dzx, "tn", F32, "mm_ssd_in_dw", 1024, zw // 4, 2048, out_buf=(gbuf["win"], j))
            done.append(tok)
        else:
            dh3, g["pool_w"][j], dps = _pool_bwd(sv["h"].reshape(bl, seq, d), dmix.reshape(bl, seq, d), w_pool[j],
                                                 p_scale[j:j + 1], "pool_bwd")
            g["pool_scale"][j] = dps[0]
            dh = dh3.reshape(t, d)
        if i == 0:
            dx_in, g["norm_mix_pre"][i] = _norm_bwd(sv["x_in"], norm_mix_pre[i:i + 1], dh, F32, "norm_bwd_r", resid=dmid,
                                                    after=done)
            return dx_in, None
        dx_in, dfo_prev, g["norm_mix_pre"][i], g["norm_ffn_post"][i - 1] = _norm_bwd2(
            sv["x_in"], norm_mix_pre[i:i + 1], dh, dmid, saved[i - 1]["fo"], norm_ffn_post[i - 1:i], BF16,
            "norm_bwd_in_post", after=done)
        return dx_in, dfo_prev

    ffn_comm = None
    dfo, g["norm_ffn_post"][depth - 1] = _norm_bwd(saved[depth - 1]["fo"], norm_ffn_post[depth - 1:depth], dcur, BF16,
                                                   "norm_bwd_b")
    for i in reversed(range(depth)):
        sv = saved[i]
        dact = _mm(dfo, w_down, "nt", BF16, "mm_down_dx", 1024, ff // 2, d, b_layer=i)
        gbuf["down"], tok_down = _mm(sv["act"], dfo, "tn", F32, "mm_down_dw", ff // 2, 1024, 2048,
                                     out_buf=(gbuf["down"], i))
        dhg, dhv, dcw, dcb = _ffn_act_bwd(sv["hpre"], sv["pre_g"], sv["pre_v"], dact.reshape(bl, seq, ff), f_conv_w[i],
                                          "ffn_act_bwd")
        g["ffn_conv_w"][i] = dcw
        g["ffn_conv_b"][i] = dcb[0]
        dhs = [dhg.reshape(t, ff), dhv.reshape(t, ff)]
        du = _mm(dhs, w_up, "nt", BF16, "mm_up_dx", 1024, d, ff, b_layer=i)
        gbuf["up"], tok_up = _mm(sv["u"], dhs, "tn", F32, "mm_up_dw", 1024, ff // 2, 2048, out_buf=(gbuf["up"], i))
        def pre_post(behind, i=i, sv=sv, du=du, dcur=dcur):
            return _norm_bwd2(sv["mid"], norm_ffn_pre[i:i + 1], du, dcur, sv["mix"], norm_mix_post[i:i + 1],
                              BF16 if i % 2 == 0 else F32,
                              "norm_bwd_pre_post_b" if i % 2 == 0 else "norm_bwd_pre_post_f", after=behind)

        if i > 0:
            dmid, dmix, g["norm_ffn_pre"][i], g["norm_mix_post"][i] = pre_post([tok_down, tok_up])
            dcur, dfo = mixer_bwd(i, dmid, dmix)
        else:
            held = []

            def during(token):
                held.extend(pre_post([tok_down, tok_up, token]))
                return held[0]

            ffn_comm, ffn_token = _reduce_begin(FFNW, [gbuf["up"], gbuf["down"]], core, "ffn", during=during)
            dmid, dmix, g["norm_ffn_pre"][i], g["norm_mix_post"][i] = held
            dcur, dfo = mixer_bwd(i, dmid, dmix, behind=[ffn_token])

    grad_x = dcur.reshape(bl, seq, d)
    for n in ("norm_mix_pre", "norm_mix_post", "norm_ffn_pre", "norm_ffn_post"):
        g[n] = [a[0] for a in g[n]]
    small_names = [n for n, _ in SMALL] + list(REPL)
    full = {n: jnp.stack(g[n], axis=0) for n in small_names}

    g_in = jnp.concatenate([gbuf["win"][..., :d_inner + xbc], unpad_heads(gbuf["win"][..., d_inner + xbc:])], axis=-1)
    g_in_cm = jnp.swapaxes(g_in.reshape(n_ssd, d, N_CHIPS, -1), 1, 2)
    vec = jnp.concatenate([full[n].reshape(-1) for n in small_names] + [loss_part[0, :1]])
    nvec = vec.shape[0]
    vrows = 16 * ((nvec + 16 * FLAT_COLS - 1) // (16 * FLAT_COLS))
    vec = jnp.pad(vec, (0, vrows * FLAT_COLS - nvec)).reshape(vrows, FLAT_COLS)
    stages, counts = _gather8_stages()
    small_comm = _SplitComm(stages, counts, [vec], [lax.empty((8, vrows, FLAT_COLS), F32)], "gather_small_grads")
    small_token = small_comm.advance()
    mix_comm, mix_token = _reduce_begin(MIXW, [g_in_cm, gbuf["out"], jnp.stack(g["pool_w"], axis=0)], core, "mixers",
                                        riders=[small_token])

    grads, deltas, new_m, new_v = {}, {}, {}, {}

    def adamw(n, gr):
        shp = wts[n].shape
        two = (math.prod(shp[:-1]), shp[-1])
        dl, mn, vn, go = _adamw(wts[n].reshape(two), gr.reshape(two), mom[n].reshape(two), var[n].reshape(two),
                                "adamw_" + n)
        grads[n], deltas[n], new_m[n], new_v[n] = go.reshape(shp), dl.reshape(shp), mn.reshape(shp), vn.reshape(shp)
        return dl

    small_comm.advance(after=mix_token)
    tot = _sum8(small_comm.lands()[0], "sum_small").reshape(-1)
    small_grads, off = {}, 0
    for n in small_names:
        cnt = math.prod(full[n].shape)
        small_grads[n] = tot[off:off + cnt].reshape(full[n].shape)
        off += cnt
    loss = tot[off]
    for n, ax in SMALL:
        w = wts[n].shape[ax]
        small_grads[n] = lax.dynamic_slice_in_dim(small_grads[n], chip * w, w, axis=ax)

    behind = [adamw(n, small_grads[n]) for n in small_names][-1:]
    ffn_grads = _reduce_finish(FFNW, ffn_comm, core, "ffn", after=mix_token)
    behind += [adamw(n, gr) for gr, (n, _) in zip(ffn_grads, FFNW)]
    mix_grads = _reduce_finish(MIXW, mix_comm, core, "mixers", after=behind)
    for gr, (n, _) in zip(mix_grads, MIXW):
        adamw(n, gr)

    return (loss, grad_x, *[grads[n] for n in WEIGHTS], *[deltas[n] for n in WEIGHTS],
            *[new_m[n] for n in WEIGHTS], *[new_v[n] for n in WEIGHTS])
```
